```python
import math
import jax, jax.numpy as jnp
from jax import lax
import numpy as np

D_MODEL = 1024
BATCH = 8
SEQ = 4096
DEPTH = 2

D_FF = 2816
HEAD_DIM = 64
W_A = 256
A_BLOCKS = 4
A_BLOCK_W = W_A // A_BLOCKS
LRU_C = 8.0
LRU_CONV = 4
N_Q_HEADS = 8
N_KV_HEADS = 2
W_B = N_Q_HEADS * HEAD_DIM
WINDOW = 128
BLK = 128
W_C = 256
C_GROUPS = 4
C_CONV = 31
D_MIX = W_A + W_B + W_C
OFF_LRU_X = 0
OFF_LRU_GATE = OFF_LRU_X + W_A
OFF_Q = OFF_LRU_GATE + W_A
OFF_K = OFF_Q + W_B
OFF_V = OFF_K + N_KV_HEADS * HEAD_DIM
OFF_GLU = OFF_V + N_KV_HEADS * HEAD_DIM
D_IN_PROJ = OFF_GLU + 2 * W_C
NORM_EPS = 1e-6
LN_EPS = 1e-5
NEG_BIG = -1e30

kernel_name = "hymba_style_lru_swa_conformer_macaron"


def rms_norm(x, g):
    xf = x.astype(jnp.float32)
    y = xf * lax.rsqrt(jnp.mean(xf * xf, axis=-1, keepdims=True) + NORM_EPS)
    return (y * g.astype(jnp.float32)).astype(x.dtype)


def layer_norm(x, g, b):
    xf = x.astype(jnp.float32)
    mu = jnp.mean(xf, axis=-1, keepdims=True)
    xc = xf - mu
    var = jnp.mean(xc * xc, axis=-1, keepdims=True)
    y = xc * lax.rsqrt(var + LN_EPS) * g.astype(jnp.float32) + b.astype(jnp.float32)
    return y.astype(x.dtype)


def swiglu(x, w_gu, w_down):
    g, u = jnp.split(x @ w_gu, 2, axis=-1)
    return (jax.nn.silu(g) * u) @ w_down


def causal_depthwise_conv(x, w, b):
    k = w.shape[0]
    y = lax.conv_general_dilated(
        x, w[:, None, :], window_strides=(1,), padding=[(k - 1, 0)],
        dimension_numbers=("NWC", "WIO", "NWC"), feature_group_count=x.shape[-1])
    return y + b


def rg_lru(x, w_a, b_a, w_x, b_x, lam):
    bsz, s, w = x.shape
    xb = x.reshape(bsz, s, A_BLOCKS, A_BLOCK_W)
    r = jax.nn.sigmoid(jnp.einsum("bshi,hij->bshj", xb, w_a).reshape(bsz, s, w) + b_a)
    i = jax.nn.sigmoid(jnp.einsum("bshi,hij->bshj", xb, w_x).reshape(bsz, s, w) + b_x)
    log_a = -LRU_C * r.astype(jnp.float32) * jax.nn.softplus(-lam.astype(jnp.float32))
    a = jnp.exp(log_a)
    u = jnp.sqrt(-jnp.expm1(2.0 * log_a)) * (i * x).astype(jnp.float32)

    def combine(left, right):
        a1, b1 = left
        a2, b2 = right
        return a1 * a2, a2 * b1 + b2

    _, h = lax.associative_scan(combine, (a, u), axis=1)
    return h.astype(x.dtype)


def sliding_window_attention_sinks(q, k, v, sinks):
    bsz, s, h, d = q.shape
    kvh = k.shape[2]
    grp = h // kvh
    nblk = s // BLK
    qb = q.reshape(bsz, nblk, BLK, kvh, grp, d)

    def banded(t):
        cur = t.reshape(bsz, nblk, BLK, kvh, d)
        prev = jnp.pad(t, ((0, 0), (BLK, 0), (0, 0), (0, 0)))[:, :s].reshape(bsz, nblk, BLK, kvh, d)
        return jnp.concatenate([prev, cur], axis=2)

    kw, vw = banded(k), banded(v)
    scores = jnp.einsum("bnqkgd,bnjkd->bnkgqj", qb, kw).astype(jnp.float32) * (1.0 / math.sqrt(d))
    qi = jnp.arange(BLK)[:, None]
    kj = jnp.arange(2 * BLK)[None, :]
    rel = BLK + qi - kj
    k_pos = (jnp.arange(nblk)[:, None, None] - 1) * BLK + kj[None]
    mask = (rel >= 0)[None] & (rel < WINDOW)[None] & (k_pos >= 0)
    scores = jnp.where(mask[None, :, None, None], scores, NEG_BIG)
    sink = sinks.astype(jnp.float32).reshape(1, 1, kvh, grp, 1, 1)
    m = jnp.maximum(jnp.max(scores, axis=-1, keepdims=True), sink)
    p = jnp.exp(scores - m)
    p = p / (jnp.sum(p, axis=-1, keepdims=True) + jnp.exp(sink - m))
    o = jnp.einsum("bnkgqj,bnjkd->bnqkgd", p.astype(v.dtype), vw)
    return o.reshape(bsz, s, h * d)


def conformer_conv(glu_in, w, b, ln_g, ln_b):
    a, g = jnp.split(glu_in, 2, axis=-1)
    y = a * jax.nn.sigmoid(g)
    y = causal_depthwise_conv(y, w, b)
    y = layer_norm(y, ln_g, ln_b)
    return jax.nn.silu(y)


def _fwd_setup_inputs(seed: int = 0) -> dict:
    key = jax.random.key(seed)
    ks = iter(jax.random.split(key, 40))
    L = DEPTH

    def nrm(shape, scale):
        return scale * jax.random.normal(next(ks), shape, jnp.float32)

    def gain(shape):
        return 1.0 + 0.05 * jax.random.normal(next(ks), shape, jnp.float32)

    a0 = jax.random.uniform(next(ks), (L, W_A), jnp.float32, minval=0.9, maxval=0.999)
    return {
        "x": jax.random.normal(next(ks), (BATCH, SEQ, D_MODEL), jnp.float32),
        "ffn1_pre_g": gain((L, D_MODEL)),
        "ffn1_w_gu": nrm((L, D_MODEL, 2 * D_FF), D_MODEL ** -0.5),
        "ffn1_w_down": nrm((L, D_FF, D_MODEL), D_FF ** -0.5),
        "ffn1_post_g": gain((L, D_MODEL)),
        "mix_pre_g": gain((L, D_MODEL)),
        "w_in": nrm((L, D_MODEL, D_IN_PROJ), D_MODEL ** -0.5),
        "lru_conv_w": nrm((L, LRU_CONV, W_A), LRU_CONV ** -0.5),
        "lru_conv_b": nrm((L, W_A), 0.02),
        "lru_w_a": nrm((L, A_BLOCKS, A_BLOCK_W, A_BLOCK_W), A_BLOCK_W ** -0.5),
        "lru_b_a": nrm((L, W_A), 0.02),
        "lru_w_x": nrm((L, A_BLOCKS, A_BLOCK_W, A_BLOCK_W), A_BLOCK_W ** -0.5),
        "lru_b_x": nrm((L, W_A), 0.02),
        "lru_lambda": jnp.log(a0) - jnp.log1p(-a0),
        "attn_sinks": nrm((L, N_Q_HEADS), 0.5),
        "conv_w": nrm((L, C_CONV, W_C), C_CONV ** -0.5),
        "conv_b": nrm((L, W_C), 0.02),
        "conv_ln_g": gain((L, W_C)),
        "conv_ln_b": nrm((L, W_C), 0.02),
        "group_g": gain((L, D_MIX)),
        "w_out": nrm((L, D_MIX, D_MODEL), D_MIX ** -0.5),
        "mix_post_g": gain((L, D_MODEL)),
        "ffn2_pre_g": gain((L, D_MODEL)),
        "ffn2_w_gu": nrm((L, D_MODEL, 2 * D_FF), D_MODEL ** -0.5),
        "ffn2_w_down": nrm((L, D_FF, D_MODEL), D_FF ** -0.5),
        "ffn2_post_g": gain((L, D_MODEL)),
    }


def _fwd_reference(x, ffn1_pre_g, ffn1_w_gu, ffn1_w_down, ffn1_post_g, mix_pre_g, w_in,
              lru_conv_w, lru_conv_b, lru_w_a, lru_b_a, lru_w_x, lru_b_x, lru_lambda,
              attn_sinks, conv_w, conv_b, conv_ln_g, conv_ln_b, group_g, w_out,
              mix_post_g, ffn2_pre_g, ffn2_w_gu, ffn2_w_down, ffn2_post_g):
    bsz, s, _ = x.shape
    for l in range(DEPTH):
        x = x + 0.5 * rms_norm(swiglu(rms_norm(x, ffn1_pre_g[l]), ffn1_w_gu[l], ffn1_w_down[l]), ffn1_post_g[l])

        hn = rms_norm(x, mix_pre_g[l])
        proj = hn @ w_in[l]
        lru_x = proj[..., OFF_LRU_X:OFF_LRU_GATE]
        lru_gate = proj[..., OFF_LRU_GATE:OFF_Q]
        q = proj[..., OFF_Q:OFF_K].reshape(bsz, s, N_Q_HEADS, HEAD_DIM)
        k = proj[..., OFF_K:OFF_V].reshape(bsz, s, N_KV_HEADS, HEAD_DIM)
        v = proj[..., OFF_V:OFF_GLU].reshape(bsz, s, N_KV_HEADS, HEAD_DIM)
        glu_in = proj[..., OFF_GLU:]

        y_a = jax.nn.gelu(lru_gate) * rg_lru(
            causal_depthwise_conv(lru_x, lru_conv_w[l], lru_conv_b[l]),
            lru_w_a[l], lru_b_a[l], lru_w_x[l], lru_b_x[l], lru_lambda[l])
        y_b = sliding_window_attention_sinks(q, k, v, attn_sinks[l])
        y_c = conformer_conv(glu_in, conv_w[l], conv_b[l], conv_ln_g[l], conv_ln_b[l])

        gg = group_g[l]
        y = jnp.concatenate([
            rms_norm(y_a, gg[:W_A]),
            rms_norm(y_b, gg[W_A:W_A + W_B]),
            rms_norm(y_c, gg[W_A + W_B:]),
        ], axis=-1)
        x = x + rms_norm(y @ w_out[l], mix_post_g[l])

        x = x + 0.5 * rms_norm(swiglu(rms_norm(x, ffn2_pre_g[l]), ffn2_w_gu[l], ffn2_w_down[l]), ffn2_post_g[l])
    return x


import jax as _jax
import jax.numpy as _jnp

TWIN_FORMAT = 'train_step'
FWD_PARAMS = ['x', 'ffn1_pre_g', 'ffn1_w_gu', 'ffn1_w_down', 'ffn1_post_g', 'mix_pre_g', 'w_in', 'lru_conv_w', 'lru_conv_b', 'lru_w_a', 'lru_b_a', 'lru_w_x', 'lru_b_x', 'lru_lambda', 'attn_sinks', 'conv_w', 'conv_b', 'conv_ln_g', 'conv_ln_b', 'group_g', 'w_out', 'mix_post_g', 'ffn2_pre_g', 'ffn2_w_gu', 'ffn2_w_down', 'ffn2_post_g']
TWIN_WEIGHTS = ['ffn1_pre_g', 'ffn1_w_gu', 'ffn1_w_down', 'ffn1_post_g', 'mix_pre_g', 'w_in', 'lru_conv_w', 'lru_conv_b', 'lru_w_a', 'lru_b_a', 'lru_w_x', 'lru_b_x', 'lru_lambda', 'attn_sinks', 'conv_w', 'conv_b', 'conv_ln_g', 'conv_ln_b', 'group_g', 'w_out', 'mix_post_g', 'ffn2_pre_g', 'ffn2_w_gu', 'ffn2_w_down', 'ffn2_post_g']
TWIN_DIFF_INPUT = 'x'
TWIN_INPUTS = ['x', 'ffn1_pre_g', 'ffn1_w_gu', 'ffn1_w_down', 'ffn1_post_g', 'mix_pre_g', 'w_in', 'lru_conv_w', 'lru_conv_b', 'lru_w_a', 'lru_b_a', 'lru_w_x', 'lru_b_x', 'lru_lambda', 'attn_sinks', 'conv_w', 'conv_b', 'conv_ln_g', 'conv_ln_b', 'group_g', 'w_out', 'mix_post_g', 'ffn2_pre_g', 'ffn2_w_gu', 'ffn2_w_down', 'ffn2_post_g', 'loss_target', 'm_ffn1_pre_g', 'm_ffn1_w_gu', 'm_ffn1_w_down', 'm_ffn1_post_g', 'm_mix_pre_g', 'm_w_in', 'm_lru_conv_w', 'm_lru_conv_b', 'm_lru_w_a', 'm_lru_b_a', 'm_lru_w_x', 'm_lru_b_x', 'm_lru_lambda', 'm_attn_sinks', 'm_conv_w', 'm_conv_b', 'm_conv_ln_g', 'm_conv_ln_b', 'm_group_g', 'm_w_out', 'm_mix_post_g', 'm_ffn2_pre_g', 'm_ffn2_w_gu', 'm_ffn2_w_down', 'm_ffn2_post_g', 'v_ffn1_pre_g', 'v_ffn1_w_gu', 'v_ffn1_w_down', 'v_ffn1_post_g', 'v_mix_pre_g', 'v_w_in', 'v_lru_conv_w', 'v_lru_conv_b', 'v_lru_w_a', 'v_lru_b_a', 'v_lru_w_x', 'v_lru_b_x', 'v_lru_lambda', 'v_attn_sinks', 'v_conv_w', 'v_conv_b', 'v_conv_ln_g', 'v_conv_ln_b', 'v_group_g', 'v_w_out', 'v_mix_post_g', 'v_ffn2_pre_g', 'v_ffn2_w_gu', 'v_ffn2_w_down', 'v_ffn2_post_g']
TWIN_OUTPUTS = ['loss', 'grad_x', 'grad_ffn1_pre_g', 'grad_ffn1_w_gu', 'grad_ffn1_w_down', 'grad_ffn1_post_g', 'grad_mix_pre_g', 'grad_w_in', 'grad_lru_conv_w', 'grad_lru_conv_b', 'grad_lru_w_a', 'grad_lru_b_a', 'grad_lru_w_x', 'grad_lru_b_x', 'grad_lru_lambda', 'grad_attn_sinks', 'grad_conv_w', 'grad_conv_b', 'grad_conv_ln_g', 'grad_conv_ln_b', 'grad_group_g', 'grad_w_out', 'grad_mix_post_g', 'grad_ffn2_pre_g', 'grad_ffn2_w_gu', 'grad_ffn2_w_down', 'grad_ffn2_post_g', 'delta_ffn1_pre_g', 'delta_ffn1_w_gu', 'delta_ffn1_w_down', 'delta_ffn1_post_g', 'delta_mix_pre_g', 'delta_w_in', 'delta_lru_conv_w', 'delta_lru_conv_b', 'delta_lru_w_a', 'delta_lru_b_a', 'delta_lru_w_x', 'delta_lru_b_x', 'delta_lru_lambda', 'delta_attn_sinks', 'delta_conv_w', 'delta_conv_b', 'delta_conv_ln_g', 'delta_conv_ln_b', 'delta_group_g', 'delta_w_out', 'delta_mix_post_g', 'delta_ffn2_pre_g', 'delta_ffn2_w_gu', 'delta_ffn2_w_down', 'delta_ffn2_post_g', 'new_m_ffn1_pre_g', 'new_m_ffn1_w_gu', 'new_m_ffn1_w_down', 'new_m_ffn1_post_g', 'new_m_mix_pre_g', 'new_m_w_in', 'new_m_lru_conv_w', 'new_m_lru_conv_b', 'new_m_lru_w_a', 'new_m_lru_b_a', 'new_m_lru_w_x', 'new_m_lru_b_x', 'new_m_lru_lambda', 'new_m_attn_sinks', 'new_m_conv_w', 'new_m_conv_b', 'new_m_conv_ln_g', 'new_m_conv_ln_b', 'new_m_group_g', 'new_m_w_out', 'new_m_mix_post_g', 'new_m_ffn2_pre_g', 'new_m_ffn2_w_gu', 'new_m_ffn2_w_down', 'new_m_ffn2_post_g', 'new_v_ffn1_pre_g', 'new_v_ffn1_w_gu', 'new_v_ffn1_w_down', 'new_v_ffn1_post_g', 'new_v_mix_pre_g', 'new_v_w_in', 'new_v_lru_conv_w', 'new_v_lru_conv_b', 'new_v_lru_w_a', 'new_v_lru_b_a', 'new_v_lru_w_x', 'new_v_lru_b_x', 'new_v_lru_lambda', 'new_v_attn_sinks', 'new_v_conv_w', 'new_v_conv_b', 'new_v_conv_ln_g', 'new_v_conv_ln_b', 'new_v_group_g', 'new_v_w_out', 'new_v_mix_post_g', 'new_v_ffn2_pre_g', 'new_v_ffn2_w_gu', 'new_v_ffn2_w_down', 'new_v_ffn2_post_g']
TWIN_LEAF_KINDS = {'loss': 'loss', 'grad_x': 'grad_x', 'grad_ffn1_pre_g': 'grad_w', 'grad_ffn1_w_gu': 'grad_w', 'grad_ffn1_w_down': 'grad_w', 'grad_ffn1_post_g': 'grad_w', 'grad_mix_pre_g': 'grad_w', 'grad_w_in': 'grad_w', 'grad_lru_conv_w': 'grad_w', 'grad_lru_conv_b': 'grad_w', 'grad_lru_w_a': 'grad_w', 'grad_lru_b_a': 'grad_w', 'grad_lru_w_x': 'grad_w', 'grad_lru_b_x': 'grad_w', 'grad_lru_lambda': 'grad_w', 'grad_attn_sinks': 'grad_w', 'grad_conv_w': 'grad_w', 'grad_conv_b': 'grad_w', 'grad_conv_ln_g': 'grad_w', 'grad_conv_ln_b': 'grad_w', 'grad_group_g': 'grad_w', 'grad_w_out': 'grad_w', 'grad_mix_post_g': 'grad_w', 'grad_ffn2_pre_g': 'grad_w', 'grad_ffn2_w_gu': 'grad_w', 'grad_ffn2_w_down': 'grad_w', 'grad_ffn2_post_g': 'grad_w', 'delta_ffn1_pre_g': 'delta_w', 'delta_ffn1_w_gu': 'delta_w', 'delta_ffn1_w_down': 'delta_w', 'delta_ffn1_post_g': 'delta_w', 'delta_mix_pre_g': 'delta_w', 'delta_w_in': 'delta_w', 'delta_lru_conv_w': 'delta_w', 'delta_lru_conv_b': 'delta_w', 'delta_lru_w_a': 'delta_w', 'delta_lru_b_a': 'delta_w', 'delta_lru_w_x': 'delta_w', 'delta_lru_b_x': 'delta_w', 'delta_lru_lambda': 'delta_w', 'delta_attn_sinks': 'delta_w', 'delta_conv_w': 'delta_w', 'delta_conv_b': 'delta_w', 'delta_conv_ln_g': 'delta_w', 'delta_conv_ln_b': 'delta_w', 'delta_group_g': 'delta_w', 'delta_w_out': 'delta_w', 'delta_mix_post_g': 'delta_w', 'delta_ffn2_pre_g': 'delta_w', 'delta_ffn2_w_gu': 'delta_w', 'delta_ffn2_w_down': 'delta_w', 'delta_ffn2_post_g': 'delta_w', 'new_m_ffn1_pre_g': 'new_m', 'new_m_ffn1_w_gu': 'new_m', 'new_m_ffn1_w_down': 'new_m', 'new_m_ffn1_post_g': 'new_m', 'new_m_mix_pre_g': 'new_m', 'new_m_w_in': 'new_m', 'new_m_lru_conv_w': 'new_m', 'new_m_lru_conv_b': 'new_m', 'new_m_lru_w_a': 'new_m', 'new_m_lru_b_a': 'new_m', 'new_m_lru_w_x': 'new_m', 'new_m_lru_b_x': 'new_m', 'new_m_lru_lambda': 'new_m', 'new_m_attn_sinks': 'new_m', 'new_m_conv_w': 'new_m', 'new_m_conv_b': 'new_m', 'new_m_conv_ln_g': 'new_m', 'new_m_conv_ln_b': 'new_m', 'new_m_group_g': 'new_m', 'new_m_w_out': 'new_m', 'new_m_mix_post_g': 'new_m', 'new_m_ffn2_pre_g': 'new_m', 'new_m_ffn2_w_gu': 'new_m', 'new_m_ffn2_w_down': 'new_m', 'new_m_ffn2_post_g': 'new_m', 'new_v_ffn1_pre_g': 'new_v', 'new_v_ffn1_w_gu': 'new_v', 'new_v_ffn1_w_down': 'new_v', 'new_v_ffn1_post_g': 'new_v', 'new_v_mix_pre_g': 'new_v', 'new_v_w_in': 'new_v', 'new_v_lru_conv_w': 'new_v', 'new_v_lru_conv_b': 'new_v', 'new_v_lru_w_a': 'new_v', 'new_v_lru_b_a': 'new_v', 'new_v_lru_w_x': 'new_v', 'new_v_lru_b_x': 'new_v', 'new_v_lru_lambda': 'new_v', 'new_v_attn_sinks': 'new_v', 'new_v_conv_w': 'new_v', 'new_v_conv_b': 'new_v', 'new_v_conv_ln_g': 'new_v', 'new_v_conv_ln_b': 'new_v', 'new_v_group_g': 'new_v', 'new_v_w_out': 'new_v', 'new_v_mix_post_g': 'new_v', 'new_v_ffn2_pre_g': 'new_v', 'new_v_ffn2_w_gu': 'new_v', 'new_v_ffn2_w_down': 'new_v', 'new_v_ffn2_post_g': 'new_v'}


def _forward(args):
    return _fwd_reference(*[args[k] for k in FWD_PARAMS])


def _output_shape():
    out = _jax.eval_shape(lambda: _forward(_fwd_setup_inputs(0)))
    return out.shape, out.dtype

N_MICROBATCH = 1
ADAM_LR = 0.001
ADAM_B1 = 0.9
ADAM_B2 = 0.999
ADAM_EPS = 1e-08
ADAM_WD = 0.01
ADAM_STEP = 10
PER_EXAMPLE_BATCH_AXIS = {'x': 0, 'loss_target': 0}
SHARED_INPUTS = []
_WEIGHT_DTYPES = {'ffn1_pre_g': _jnp.float32, 'ffn1_w_gu': _jnp.float32, 'ffn1_w_down': _jnp.float32, 'ffn1_post_g': _jnp.float32, 'mix_pre_g': _jnp.float32, 'w_in': _jnp.float32, 'lru_conv_w': _jnp.float32, 'lru_conv_b': _jnp.float32, 'lru_w_a': _jnp.float32, 'lru_b_a': _jnp.float32, 'lru_w_x': _jnp.float32, 'lru_b_x': _jnp.float32, 'lru_lambda': _jnp.float32, 'attn_sinks': _jnp.float32, 'conv_w': _jnp.float32, 'conv_b': _jnp.float32, 'conv_ln_g': _jnp.float32, 'conv_ln_b': _jnp.float32, 'group_g': _jnp.float32, 'w_out': _jnp.float32, 'mix_post_g': _jnp.float32, 'ffn2_pre_g': _jnp.float32, 'ffn2_w_gu': _jnp.float32, 'ffn2_w_down': _jnp.float32, 'ffn2_post_g': _jnp.float32}
MOMENT_SCALE = {'ffn1_pre_g': 1.096795e+00, 'ffn1_w_gu': 4.078287e-01, 'ffn1_w_down': 7.020059e-01, 'ffn1_post_g': 7.801161e+00, 'mix_pre_g': 1.444779e+00, 'w_in': 1.110448e+00, 'lru_conv_w': 1.363035e+00, 'lru_conv_b': 2.345027e+01, 'lru_w_a': 6.295094e-01, 'lru_b_a': 3.482262e-01, 'lru_w_x': 1.287853e+00, 'lru_b_x': 4.234821e-01, 'lru_lambda': 6.113155e-01, 'attn_sinks': 2.815872e-01, 'conv_w': 9.339333e-01, 'conv_b': 1.410046e+01, 'conv_ln_g': 5.630834e+00, 'conv_ln_b': 8.628237e+00, 'group_g': 2.073874e+00, 'w_out': 2.046213e+00, 'mix_post_g': 3.254144e+01, 'ffn2_pre_g': 5.961779e-01, 'ffn2_w_gu': 2.484720e-01, 'ffn2_w_down': 4.809340e-01, 'ffn2_post_g': 7.995985e+00}


def _to_microbatches(a, axis):
    t = _jnp.moveaxis(a, axis, 0)
    t = t.reshape((N_MICROBATCH, t.shape[0] // N_MICROBATCH) + t.shape[1:])
    return _jnp.moveaxis(t, 1, axis + 1)


def setup_inputs(seed: int = 0) -> dict:
    inp = _fwd_setup_inputs(seed)
    key = _jax.random.fold_in(_jax.random.key(seed), 7919)
    shape, _ = _output_shape()
    out = dict(inp)
    out["loss_target"] = _jax.random.normal(_jax.random.fold_in(key, 0), shape, _jnp.float32)
    for i, name in enumerate(TWIN_WEIGHTS):
        w = inp[name].astype(_jnp.float32)
        if MOMENT_SCALE is None:
            s = _jnp.sqrt(_jnp.mean(_jnp.square(w)) + 1e-30)
        else:
            s = MOMENT_SCALE[name]
        km, kv = _jax.random.split(_jax.random.fold_in(key, i + 1))
        out[name] = w
        out["m_" + name] = s * _jax.random.normal(km, w.shape, _jnp.float32)
        out["v_" + name] = (s * s) * _jax.random.uniform(kv, w.shape, _jnp.float32, 0.5, 1.5)
    if N_MICROBATCH > 1:
        for name, axis in PER_EXAMPLE_BATCH_AXIS.items():
            out[name] = _to_microbatches(out[name], axis)
    return {'x': out['x'], 'ffn1_pre_g': out['ffn1_pre_g'], 'ffn1_w_gu': out['ffn1_w_gu'], 'ffn1_w_down': out['ffn1_w_down'], 'ffn1_post_g': out['ffn1_post_g'], 'mix_pre_g': out['mix_pre_g'], 'w_in': out['w_in'], 'lru_conv_w': out['lru_conv_w'], 'lru_conv_b': out['lru_conv_b'], 'lru_w_a': out['lru_w_a'], 'lru_b_a': out['lru_b_a'], 'lru_w_x': out['lru_w_x'], 'lru_b_x': out['lru_b_x'], 'lru_lambda': out['lru_lambda'], 'attn_sinks': out['attn_sinks'], 'conv_w': out['conv_w'], 'conv_b': out['conv_b'], 'conv_ln_g': out['conv_ln_g'], 'conv_ln_b': out['conv_ln_b'], 'group_g': out['group_g'], 'w_out': out['w_out'], 'mix_post_g': out['mix_post_g'], 'ffn2_pre_g': out['ffn2_pre_g'], 'ffn2_w_gu': out['ffn2_w_gu'], 'ffn2_w_down': out['ffn2_w_down'], 'ffn2_post_g': out['ffn2_post_g'], 'loss_target': out['loss_target'], 'm_ffn1_pre_g': out['m_ffn1_pre_g'], 'm_ffn1_w_gu': out['m_ffn1_w_gu'], 'm_ffn1_w_down': out['m_ffn1_w_down'], 'm_ffn1_post_g': out['m_ffn1_post_g'], 'm_mix_pre_g': out['m_mix_pre_g'], 'm_w_in': out['m_w_in'], 'm_lru_conv_w': out['m_lru_conv_w'], 'm_lru_conv_b': out['m_lru_conv_b'], 'm_lru_w_a': out['m_lru_w_a'], 'm_lru_b_a': out['m_lru_b_a'], 'm_lru_w_x': out['m_lru_w_x'], 'm_lru_b_x': out['m_lru_b_x'], 'm_lru_lambda': out['m_lru_lambda'], 'm_attn_sinks': out['m_attn_sinks'], 'm_conv_w': out['m_conv_w'], 'm_conv_b': out['m_conv_b'], 'm_conv_ln_g': out['m_conv_ln_g'], 'm_conv_ln_b': out['m_conv_ln_b'], 'm_group_g': out['m_group_g'], 'm_w_out': out['m_w_out'], 'm_mix_post_g': out['m_mix_post_g'], 'm_ffn2_pre_g': out['m_ffn2_pre_g'], 'm_ffn2_w_gu': out['m_ffn2_w_gu'], 'm_ffn2_w_down': out['m_ffn2_w_down'], 'm_ffn2_post_g': out['m_ffn2_post_g'], 'v_ffn1_pre_g': out['v_ffn1_pre_g'], 'v_ffn1_w_gu': out['v_ffn1_w_gu'], 'v_ffn1_w_down': out['v_ffn1_w_down'], 'v_ffn1_post_g': out['v_ffn1_post_g'], 'v_mix_pre_g': out['v_mix_pre_g'], 'v_w_in': out['v_w_in'], 'v_lru_conv_w': out['v_lru_conv_w'], 'v_lru_conv_b': out['v_lru_conv_b'], 'v_lru_w_a': out['v_lru_w_a'], 'v_lru_b_a': out['v_lru_b_a'], 'v_lru_w_x': out['v_lru_w_x'], 'v_lru_b_x': out['v_lru_b_x'], 'v_lru_lambda': out['v_lru_lambda'], 'v_attn_sinks': out['v_attn_sinks'], 'v_conv_w': out['v_conv_w'], 'v_conv_b': out['v_conv_b'], 'v_conv_ln_g': out['v_conv_ln_g'], 'v_conv_ln_b': out['v_conv_ln_b'], 'v_group_g': out['v_group_g'], 'v_w_out': out['v_w_out'], 'v_mix_post_g': out['v_mix_post_g'], 'v_ffn2_pre_g': out['v_ffn2_pre_g'], 'v_ffn2_w_gu': out['v_ffn2_w_gu'], 'v_ffn2_w_down': out['v_ffn2_w_down'], 'v_ffn2_post_g': out['v_ffn2_post_g']}


def _loss(weights, diff, rest, loss_target):
    with _jax.named_scope("forward"):
        args = {**rest, TWIN_DIFF_INPUT: diff, **{k: w.astype(_WEIGHT_DTYPES[k]) for k, w in weights.items()}}
        y = _forward(args)
    with _jax.named_scope("loss_head"):
        err = _jnp.square(y.astype(_jnp.float32) - loss_target)
        return 0.5 * _jnp.sum(_jnp.mean(err, axis=-1)) if err.ndim else 0.5 * err


def _adamw(w, g, m, v):
    m = ADAM_B1 * m + (1.0 - ADAM_B1) * g
    v = ADAM_B2 * v + (1.0 - ADAM_B2) * _jnp.square(g)
    m_hat = m / (1.0 - ADAM_B1 ** ADAM_STEP)
    v_hat = v / (1.0 - ADAM_B2 ** ADAM_STEP)
    delta = -ADAM_LR * (m_hat / (_jnp.sqrt(v_hat) + ADAM_EPS) + ADAM_WD * w)
    return delta, m, v


def reference(x, ffn1_pre_g, ffn1_w_gu, ffn1_w_down, ffn1_post_g, mix_pre_g, w_in, lru_conv_w, lru_conv_b, lru_w_a, lru_b_a, lru_w_x, lru_b_x, lru_lambda, attn_sinks, conv_w, conv_b, conv_ln_g, conv_ln_b, group_g, w_out, mix_post_g, ffn2_pre_g, ffn2_w_gu, ffn2_w_down, ffn2_post_g, loss_target, m_ffn1_pre_g, m_ffn1_w_gu, m_ffn1_w_down, m_ffn1_post_g, m_mix_pre_g, m_w_in, m_lru_conv_w, m_lru_conv_b, m_lru_w_a, m_lru_b_a, m_lru_w_x, m_lru_b_x, m_lru_lambda, m_attn_sinks, m_conv_w, m_conv_b, m_conv_ln_g, m_conv_ln_b, m_group_g, m_w_out, m_mix_post_g, m_ffn2_pre_g, m_ffn2_w_gu, m_ffn2_w_down, m_ffn2_post_g, v_ffn1_pre_g, v_ffn1_w_gu, v_ffn1_w_down, v_ffn1_post_g, v_mix_pre_g, v_w_in, v_lru_conv_w, v_lru_conv_b, v_lru_w_a, v_lru_b_a, v_lru_w_x, v_lru_b_x, v_lru_lambda, v_attn_sinks, v_conv_w, v_conv_b, v_conv_ln_g, v_conv_ln_b, v_group_g, v_w_out, v_mix_post_g, v_ffn2_pre_g, v_ffn2_w_gu, v_ffn2_w_down, v_ffn2_post_g):
    given = dict(x=x, ffn1_pre_g=ffn1_pre_g, ffn1_w_gu=ffn1_w_gu, ffn1_w_down=ffn1_w_down, ffn1_post_g=ffn1_post_g, mix_pre_g=mix_pre_g, w_in=w_in, lru_conv_w=lru_conv_w, lru_conv_b=lru_conv_b, lru_w_a=lru_w_a, lru_b_a=lru_b_a, lru_w_x=lru_w_x, lru_b_x=lru_b_x, lru_lambda=lru_lambda, attn_sinks=attn_sinks, conv_w=conv_w, conv_b=conv_b, conv_ln_g=conv_ln_g, conv_ln_b=conv_ln_b, group_g=group_g, w_out=w_out, mix_post_g=mix_post_g, ffn2_pre_g=ffn2_pre_g, ffn2_w_gu=ffn2_w_gu, ffn2_w_down=ffn2_w_down, ffn2_post_g=ffn2_post_g, loss_target=loss_target, m_ffn1_pre_g=m_ffn1_pre_g, m_ffn1_w_gu=m_ffn1_w_gu, m_ffn1_w_down=m_ffn1_w_down, m_ffn1_post_g=m_ffn1_post_g, m_mix_pre_g=m_mix_pre_g, m_w_in=m_w_in, m_lru_conv_w=m_lru_conv_w, m_lru_conv_b=m_lru_conv_b, m_lru_w_a=m_lru_w_a, m_lru_b_a=m_lru_b_a, m_lru_w_x=m_lru_w_x, m_lru_b_x=m_lru_b_x, m_lru_lambda=m_lru_lambda, m_attn_sinks=m_attn_sinks, m_conv_w=m_conv_w, m_conv_b=m_conv_b, m_conv_ln_g=m_conv_ln_g, m_conv_ln_b=m_conv_ln_b, m_group_g=m_group_g, m_w_out=m_w_out, m_mix_post_g=m_mix_post_g, m_ffn2_pre_g=m_ffn2_pre_g, m_ffn2_w_gu=m_ffn2_w_gu, m_ffn2_w_down=m_ffn2_w_down, m_ffn2_post_g=m_ffn2_post_g, v_ffn1_pre_g=v_ffn1_pre_g, v_ffn1_w_gu=v_ffn1_w_gu, v_ffn1_w_down=v_ffn1_w_down, v_ffn1_post_g=v_ffn1_post_g, v_mix_pre_g=v_mix_pre_g, v_w_in=v_w_in, v_lru_conv_w=v_lru_conv_w, v_lru_conv_b=v_lru_conv_b, v_lru_w_a=v_lru_w_a, v_lru_b_a=v_lru_b_a, v_lru_w_x=v_lru_w_x, v_lru_b_x=v_lru_b_x, v_lru_lambda=v_lru_lambda, v_attn_sinks=v_attn_sinks, v_conv_w=v_conv_w, v_conv_b=v_conv_b, v_conv_ln_g=v_conv_ln_g, v_conv_ln_b=v_conv_ln_b, v_group_g=v_group_g, v_w_out=v_w_out, v_mix_post_g=v_mix_post_g, v_ffn2_pre_g=v_ffn2_pre_g, v_ffn2_w_gu=v_ffn2_w_gu, v_ffn2_w_down=v_ffn2_w_down, v_ffn2_post_g=v_ffn2_post_g)
    weights = {n: given[n] for n in TWIN_WEIGHTS}
    shared = {n: given[n] for n in SHARED_INPUTS}
    per_example = {n: given[n] for n in ['x']}
    grad_fn = _jax.value_and_grad(_loss, argnums=(0, 1))

    def one_microbatch(ex, loss_target):
        ex = dict(ex)
        diff = ex.pop(TWIN_DIFF_INPUT)
        return grad_fn(weights, diff, {**shared, **ex}, loss_target)

    if N_MICROBATCH == 1:
        loss, (grad_w, grad_x) = one_microbatch(per_example, given["loss_target"])
    else:
        def body(carry, xs):
            loss_sum, grad_sum = carry
            l_k, (gw_k, gx_k) = one_microbatch(xs[0], xs[1])
            with _jax.named_scope("update"):
                return (loss_sum + l_k, _jax.tree.map(_jnp.add, grad_sum, gw_k)), gx_k

        init = (_jnp.zeros((), _jnp.float32), _jax.tree.map(_jnp.zeros_like, weights))
        (loss, grad_w), grad_x = _jax.lax.scan(body, init, (per_example, given["loss_target"]))
    with _jax.named_scope("update"):
        delta_w, new_m, new_v = {}, {}, {}
        for n in TWIN_WEIGHTS:
            delta_w[n], new_m[n], new_v[n] = _adamw(weights[n], grad_w[n], given["m_" + n], given["v_" + n])
    return (loss, grad_x, *[grad_w[n] for n in TWIN_WEIGHTS], *[delta_w[n] for n in TWIN_WEIGHTS],
            *[new_m[n] for n in TWIN_WEIGHTS], *[new_v[n] for n in TWIN_WEIGHTS])
```

```python
import functools
import math

import jax
import jax.numpy as jnp
from jax import lax
from jax.experimental import pallas as pl
from jax.experimental.pallas import tpu as pltpu

F32 = jnp.float32
BF16 = jnp.bfloat16
SDS = jax.ShapeDtypeStruct

D = 1024
DFF = 2816
FH = DFF // 2
DEPTH = 2
W_A = 256
W_B = 512
W_C = 256
NQ = 8
HD = 64
BLK = 128
P_IN = 1792
LRU_K = 4
CONV_K = 31
LRU_C = 8.0
NORM_EPS = 1e-6
LN_EPS = 1e-5
NEG_BIG = -1e30
SCALE = 1.0 / math.sqrt(HD)

ADAM_LR = 0.001
ADAM_B1 = 0.9
ADAM_B2 = 0.999
ADAM_EPS = 1e-08
ADAM_WD = 0.01
ADAM_STEP = 10

VMEM_LIMIT = 56 * 1024 * 1024
NSHARD = 4
NDEV = 8

TN = (((0,), (0,)), ((), ()))
NT = (((1,), (1,)), ((), ()))


def _cp(*sem):
    return pltpu.CompilerParams(dimension_semantics=sem if sem else None, vmem_limit_bytes=VMEM_LIMIT)


def _rsq(x, eps):
    return lax.rsqrt(jnp.mean(x * x, axis=-1, keepdims=True) + eps)


def _rms_bwd_rows(x, g, dy):
    r = _rsq(x, NORM_EPS)
    xh = x * r
    dyg = dy * g
    dx = r * (dyg - xh * jnp.mean(dyg * xh, axis=-1, keepdims=True))
    return dx, dy * xh


def _sig(x):
    return jax.nn.sigmoid(x)


def _ffn_up(x, pre_g, wgu, l, tm):
    s = x.shape[0]

    def body(x_ref, g_ref, wg_ref, wu_ref, h_ref, go_ref, uo_ref, a_ref):
        @pl.when(pl.program_id(1) == 0)
        def _():
            xf = x_ref[...]
            h_ref[...] = (xf * _rsq(xf, NORM_EPS) * g_ref[...]).astype(BF16)

        h = h_ref[...]
        gg = jnp.dot(h, wg_ref[...], preferred_element_type=F32)
        uu = jnp.dot(h, wu_ref[...], preferred_element_type=F32)
        go_ref[...] = gg.astype(BF16)
        uo_ref[...] = uu.astype(BF16)
        a_ref[...] = (gg * _sig(gg) * uu).astype(BF16)

    wide = pl.BlockSpec((tm, FH), lambda i, j: (i, j))
    return pl.pallas_call(
        body, name="ffn_up", grid=(s // tm, 2),
        in_specs=[pl.BlockSpec((tm, D), lambda i, j: (i, 0)), pl.BlockSpec((1, D), lambda i, j: (0, 0)),
                  pl.BlockSpec((None, None, D, FH), lambda i, j: (l, j, 0, 0)),
                  pl.BlockSpec((None, None, D, FH), lambda i, j: (l, j + 2, 0, 0))],
        out_specs=[pl.BlockSpec((tm, D), lambda i, j: (i, 0)), wide, wide, wide],
        out_shape=[SDS((s, D), BF16), SDS((s, DFF), BF16), SDS((s, DFF), BF16), SDS((s, DFF), BF16)],
        compiler_params=_cp("parallel", "arbitrary"),
    )(x, pre_g, wgu, wgu)


def _mm_rms_res(a, w, l, x, g, c, tm, tk, name):
    s, k_dim = a.shape
    nk = k_dim // tk

    def body(a_ref, w_ref, x_ref, g_ref, z_ref, x1_ref):
        k = pl.program_id(1)
        p = jnp.dot(a_ref[...], w_ref[...], preferred_element_type=F32)

        @pl.when(k == 0)
        def _():
            z_ref[...] = p

        @pl.when(k > 0)
        def _():
            z_ref[...] += p

        @pl.when(k == nk - 1)
        def _():
            z = z_ref[...]
            x1_ref[...] = x_ref[...] + c * (z * _rsq(z, NORM_EPS) * g_ref[...])

    row = pl.BlockSpec((tm, D), lambda i, k: (i, 0))
    return pl.pallas_call(
        body, name=name, grid=(s // tm, nk),
        in_specs=[pl.BlockSpec((tm, tk), lambda i, k: (i, k)), pl.BlockSpec((None, tk, D), lambda i, k: (l, k, 0)),
                  row, pl.BlockSpec((1, D), lambda i, k: (0, 0))],
        out_specs=[row, row],
        out_shape=[SDS((s, D), F32), SDS((s, D), F32)],
        compiler_params=_cp("parallel", "arbitrary"),
    )(a, w, x, g)


def _rms_bwd(dy, z, g, c, tm, name):
    s = z.shape[0]

    def body(dy_ref, z_ref, g_ref, dz_ref, dg_ref):
        dz, dgr = _rms_bwd_rows(z_ref[...], g_ref[...], c * dy_ref[...])
        dz_ref[...] = dz.astype(BF16)
        part = jnp.sum(dgr, axis=0, keepdims=True)

        @pl.when(pl.program_id(0) == 0)
        def _():
            dg_ref[...] = part

        @pl.when(pl.program_id(0) > 0)
        def _():
            dg_ref[...] += part

    row = pl.BlockSpec((tm, D), lambda i: (i, 0))
    vec = pl.BlockSpec((1, D), lambda i: (0, 0))
    return pl.pallas_call(
        body, name=name, grid=(s // tm,), in_specs=[row, row, vec], out_specs=[row, vec],
        out_shape=[SDS((s, D), BF16), SDS((1, D), F32)], compiler_params=_cp("arbitrary"),
    )(dy, z, g)


def _ffn_bwd_mid(dz, wd, l, g, u, tm):
    s = dz.shape[0]

    def body(dz_ref, wd_ref, g_ref, u_ref, dg_ref, du_ref):
        da = lax.dot_general(dz_ref[...], wd_ref[...], NT, preferred_element_type=F32)
        gg = g_ref[...].astype(F32)
        uu = u_ref[...].astype(F32)
        sg = _sig(gg)
        dg_ref[...] = (da * uu * sg * (1.0 + gg * (1.0 - sg))).astype(BF16)
        du_ref[...] = (da * gg * sg).astype(BF16)

    wide = pl.BlockSpec((tm, FH), lambda i, j: (i, j))
    return pl.pallas_call(
        body, name="ffn_bwd_mid", grid=(s // tm, 2),
        in_specs=[pl.BlockSpec((tm, D), lambda i, j: (i, 0)), pl.BlockSpec((None, FH, D), lambda i, j: (l, j, 0)), wide, wide],
        out_specs=[wide, wide],
        out_shape=[SDS((s, DFF), BF16), SDS((s, DFF), BF16)],
        compiler_params=_cp("parallel", "arbitrary"),
    )(dz, wd, g, u)


def _ffn_bwd_dh(dg, du, wgu, l, x, pre_g, dx1, tm):
    s = x.shape[0]

    def body(dg_ref, du_ref, wg_ref, wu_ref, x_ref, g_ref, dx1_ref, dx_ref, dgp_ref):
        i, k = pl.program_id(0), pl.program_id(1)
        p = (lax.dot_general(dg_ref[...], wg_ref[...], NT, preferred_element_type=F32)
             + lax.dot_general(du_ref[...], wu_ref[...], NT, preferred_element_type=F32))

        @pl.when(k == 0)
        def _():
            dx_ref[...] = p

        @pl.when(k == 1)
        def _():
            dx, dgr = _rms_bwd_rows(x_ref[...], g_ref[...], dx_ref[...] + p)
            dx_ref[...] = dx1_ref[...] + dx
            part = jnp.sum(dgr, axis=0, keepdims=True)

            @pl.when(i == 0)
            def _():
                dgp_ref[...] = part

            @pl.when(i > 0)
            def _():
                dgp_ref[...] += part

    wide = pl.BlockSpec((tm, FH), lambda i, k: (i, k))
    row = pl.BlockSpec((tm, D), lambda i, k: (i, 0))
    vec = pl.BlockSpec((1, D), lambda i, k: (0, 0))
    return pl.pallas_call(
        body, name="ffn_bwd_dh", grid=(s // tm, 2),
        in_specs=[wide, wide, pl.BlockSpec((None, None, D, FH), lambda i, k: (l, k, 0, 0)),
                  pl.BlockSpec((None, None, D, FH), lambda i, k: (l, k + 2, 0, 0)), row, vec, row],
        out_specs=[row, vec],
        out_shape=[SDS((s, D), F32), SDS((1, D), F32)],
        compiler_params=_cp("arbitrary", "arbitrary"),
    )(dg, du, wgu, wgu, x, pre_g, dx1)


def _mm_tn_into(buf, a, b, lead, tka, tn, ts, name):
    s, ka = a.shape
    n = b.shape[1]
    nl = len(lead)

    def body(buf_ref, a_ref, b_ref, o_ref):
        p = lax.dot_general(a_ref[...], b_ref[...], TN, preferred_element_type=F32)

        @pl.when(pl.program_id(2) == 0)
        def _():
            o_ref[...] = p

        @pl.when(pl.program_id(2) > 0)
        def _():
            o_ref[...] += p

    return pl.pallas_call(
        body, name=name, grid=(ka // tka, n // tn, s // ts),
        in_specs=[pl.BlockSpec(memory_space=pl.ANY),
                  pl.BlockSpec((ts, tka), lambda ia, j, t: (t, ia)), pl.BlockSpec((ts, tn), lambda ia, j, t: (t, j))],
        out_specs=pl.BlockSpec((None,) * (nl + 1) + (tka, tn), lambda ia, j, t: lead + (j, ia, 0)),
        out_shape=SDS(buf.shape, F32), input_output_aliases={0: 0},
        compiler_params=_cp("parallel", "parallel", "arbitrary"),
    )(buf, a, b)


def _proj(x, g, w_in, l, tm):
    s = x.shape[0]

    def body(x_ref, g_ref, w_ref, h_ref, p_ref):
        xf = x_ref[...]
        h = (xf * _rsq(xf, NORM_EPS) * g_ref[...]).astype(BF16)
        h_ref[...] = h
        p_ref[...] = jnp.dot(h, w_ref[...], preferred_element_type=F32)

    return pl.pallas_call(
        body, name="proj", grid=(s // tm,),
        in_specs=[pl.BlockSpec((tm, D), lambda i: (i, 0)), pl.BlockSpec((1, D), lambda i: (0, 0)),
                  pl.BlockSpec((None, D, P_IN), lambda i: (l, 0, 0))],
        out_specs=[pl.BlockSpec((tm, D), lambda i: (i, 0)), pl.BlockSpec((tm, P_IN), lambda i: (i, 0))],
        out_shape=[SDS((s, D), BF16), SDS((s, P_IN), F32)],
        compiler_params=_cp("parallel"),
    )(x, g, w_in)


def _mm_nt(a, w, l, tm, name):
    s, k_dim = a.shape
    n = w.shape[1]

    def body(a_ref, w_ref, o_ref):
        o_ref[...] = lax.dot_general(a_ref[...], w_ref[...], NT, preferred_element_type=F32)

    return pl.pallas_call(
        body, name=name, grid=(s // tm,),
        in_specs=[pl.BlockSpec((tm, k_dim), lambda i: (i, 0)), pl.BlockSpec((None, n, k_dim), lambda i: (l, 0, 0))],
        out_specs=pl.BlockSpec((tm, n), lambda i: (i, 0)),
        out_shape=SDS((s, n), F32), compiler_params=_cp("parallel"),
    )(a, w)


def _mm_nt_rmsbwd(dp, w_in, l, x, g, dx1, tm):
    s = x.shape[0]

    def body(dp_ref, w_ref, x_ref, g_ref, dx1_ref, dx_ref, dg_ref):
        dh = lax.dot_general(dp_ref[...], w_ref[...], NT, preferred_element_type=F32)
        dx, dgr = _rms_bwd_rows(x_ref[...], g_ref[...], dh)
        dx_ref[...] = dx1_ref[...] + dx
        part = jnp.sum(dgr, axis=0, keepdims=True)

        @pl.when(pl.program_id(0) == 0)
        def _():
            dg_ref[...] = part

        @pl.when(pl.program_id(0) > 0)
        def _():
            dg_ref[...] += part

    row = pl.BlockSpec((tm, D), lambda i: (i, 0))
    vec = pl.BlockSpec((1, D), lambda i: (0, 0))
    return pl.pallas_call(
        body, name="mix_bwd_dx", grid=(s // tm,),
        in_specs=[pl.BlockSpec((tm, P_IN), lambda i: (i, 0)), pl.BlockSpec((None, D, P_IN), lambda i: (l, 0, 0)), row, vec, row],
        out_specs=[row, vec], out_shape=[SDS((s, D), F32), SDS((1, D), F32)],
        compiler_params=_cp("arbitrary"),
    )(dp, w_in, x, g, dx1)


def _row_iota(shape):
    return lax.broadcasted_iota(jnp.int32, shape, 0)


def _lru_gates(xc, wa_ref, ba_ref, wx_ref, bx_ref, lam_ref):
    xb = xc.astype(BF16)
    r = _sig(jnp.dot(xb, wa_ref[...], preferred_element_type=F32) + ba_ref[...])
    ig = _sig(jnp.dot(xb, wx_ref[...], preferred_element_type=F32) + bx_ref[...])
    nl = -lam_ref[...]
    sp = jnp.maximum(nl, 0.0) + jnp.log(1.0 + jnp.exp(-jnp.abs(nl)))
    log_a = -LRU_C * r * sp
    a = jnp.exp(log_a)
    x2 = 2.0 * log_a
    series = x2 * (1.0 + x2 * (0.5 + x2 * (1.0 / 6.0 + x2 * (1.0 / 24.0 + x2 * (1.0 / 120.0)))))
    em1 = jnp.where(x2 > -0.05, series, jnp.exp(x2) - 1.0)
    mlt = jnp.sqrt(-em1)
    return r, ig, a, mlt, sp


def _conv_taps(src_ref, w_ref, k_taps, pad, tc):
    acc = None
    for j in range(k_taps):
        term = w_ref[j:j + 1, :] * src_ref[pl.ds(pad - (k_taps - 1) + j, tc), :]
        acc = term if acc is None else acc + term
    return acc


def _gelu_parts(x):
    c0 = math.sqrt(2.0 / math.pi)
    inner = c0 * (x + 0.044715 * x * x * x)
    t = jnp.tanh(inner)
    gl = 0.5 * x * (1.0 + t)
    dgl = 0.5 * (1.0 + t) + 0.5 * x * (1.0 - t * t) * c0 * (1.0 + 3.0 * 0.044715 * x * x)
    return gl, dgl


def _lru_fwd(proj, cw, cb, wa, ba, wx, bx, lam, gg, tc):
    s = proj.shape[0]
    pad = 8

    def body(xcur_ref, xprev_ref, gate_ref, cw_ref, cb_ref, wa_ref, ba_ref, wx_ref, bx_ref, lam_ref, gg_ref,
             yn_ref, h_ref, xs_ref, hc_ref):
        i = pl.program_id(0)

        @pl.when(i == 0)
        def _():
            hc_ref[...] = jnp.zeros_like(hc_ref)

        xs_ref[0:pad, :] = jnp.where(i > 0, xprev_ref[tc - pad:tc, :], 0.0)
        xs_ref[pad:pad + tc, :] = xcur_ref[...]
        xc = _conv_taps(xs_ref, cw_ref, LRU_K, pad, tc) + cb_ref[...]
        _, ig, a, mlt, _ = _lru_gates(xc, wa_ref, ba_ref, wx_ref, bx_ref, lam_ref)
        u = mlt * (ig * xc)
        row = _row_iota((tc, W_A))
        d = 1
        while d < tc:
            ok = row >= d
            a_sh = jnp.where(ok, pltpu.roll(a, d, axis=0), 1.0)
            u_sh = jnp.where(ok, pltpu.roll(u, d, axis=0), 0.0)
            u = a * u_sh + u
            a = a * a_sh
            d *= 2
        h = u + a * hc_ref[...]
        hc_ref[...] = jnp.sum(jnp.where(row == tc - 1, h, 0.0), axis=0, keepdims=True)
        h_ref[...] = h
        gl, _ = _gelu_parts(gate_ref[...])
        ya = gl * h
        yn_ref[...] = (ya * _rsq(ya, NORM_EPS) * gg_ref[...]).astype(BF16)

    blk = lambda c: pl.BlockSpec((tc, W_A), lambda i, c=c: (i, c))
    full = lambda a: pl.BlockSpec(a.shape, lambda i: (0,) * a.ndim)
    params = [cw, cb, wa, ba, wx, bx, lam, gg]
    return pl.pallas_call(
        body, name="lru_fwd", grid=(s // tc,),
        in_specs=[blk(0), pl.BlockSpec((tc, W_A), lambda i: (jnp.maximum(i - 1, 0), 0)), blk(1)] + [full(a) for a in params],
        out_specs=[pl.BlockSpec((tc, W_A), lambda i: (i, 0))] * 2,
        out_shape=[SDS((s, W_A), BF16), SDS((s, W_A), F32)],
        scratch_shapes=[pltpu.VMEM((tc + pad, W_A), F32), pltpu.VMEM((1, W_A), F32)],
        compiler_params=_cp("arbitrary"),
    )(proj, proj, proj, *params)


def _acc(ref, first, val):
    @pl.when(first)
    def _():
        ref[...] = val

    @pl.when(jnp.logical_not(first))
    def _():
        ref[...] += val


def _lru_bwd(dy, proj, h, cw, cb, wa, ba, wx, bx, lam, gg, tc):
    s = proj.shape[0]
    nc = s // tc
    pad = 8

    def body(dy_ref, xcur_ref, xprev_ref, gate_ref, h_ref, hprev_ref, cw_ref, cb_ref, wa_ref, ba_ref, wx_ref, bx_ref,
             lam_ref, gg_ref,
             dp_ref, dcw_ref, dcb_ref, dwa_ref, dba_ref, dwx_ref, dbx_ref, dlam_ref, dgg_ref,
             xs_ref, ds_ref, mu_ref, nx_ref):
        step = pl.program_id(0)
        i = nc - 1 - step
        first = step == 0

        @pl.when(first)
        def _():
            mu_ref[...] = jnp.zeros_like(mu_ref)
            nx_ref[...] = jnp.zeros_like(nx_ref)

        xs_ref[0:pad, :] = jnp.where(i > 0, xprev_ref[tc - pad:tc, :], 0.0)
        xs_ref[pad:pad + tc, :] = xcur_ref[...]
        xc = _conv_taps(xs_ref, cw_ref, LRU_K, pad, tc) + cb_ref[...]
        r, ig, a, mlt, sp = _lru_gates(xc, wa_ref, ba_ref, wx_ref, bx_ref, lam_ref)
        hh = h_ref[...]
        gate = gate_ref[...]
        gl, dgl = _gelu_parts(gate)
        ya = gl * hh
        dya, dggr = _rms_bwd_rows(ya, gg_ref[...], dy_ref[...])
        _acc(dgg_ref, first, jnp.sum(dggr, axis=0, keepdims=True))
        dp_ref[:, W_A:2 * W_A] = dya * hh * dgl
        dh = dya * gl

        row = _row_iota((tc, W_A))
        aa = a
        uu = a * dh
        d = 1
        while d < tc:
            ok = row < tc - d
            a_sh = jnp.where(ok, pltpu.roll(aa, tc - d, axis=0), 1.0)
            u_sh = jnp.where(ok, pltpu.roll(uu, tc - d, axis=0), 0.0)
            uu = uu + aa * u_sh
            aa = aa * a_sh
            d *= 2
        cin = mu_ref[...]
        mu = uu + aa * cin
        lam_t = dh + jnp.where(row == tc - 1, cin, pltpu.roll(mu, tc - 1, axis=0))
        mu_ref[...] = jnp.sum(jnp.where(row == 0, mu, 0.0), axis=0, keepdims=True)
        hm1 = jnp.where(row == 0, jnp.where(i > 0, pltpu.roll(hprev_ref[...], 1, axis=0), 0.0),
                        pltpu.roll(hh, 1, axis=0))
        da = lam_t * hm1
        du = lam_t
        dmlt = du * ig * xc
        dig = du * mlt * xc
        dxc = du * mlt * ig
        dlog_a = da * a - dmlt * (a * a / mlt)
        dr = dlog_a * (-LRU_C * sp)
        dsp = jnp.sum(dlog_a * (-LRU_C * r), axis=0, keepdims=True)
        _acc(dlam_ref, first, dsp * (-_sig(-lam_ref[...])))
        dga = dr * r * (1.0 - r)
        dgx = dig * ig * (1.0 - ig)
        _acc(dba_ref, first, jnp.sum(dga, axis=0, keepdims=True))
        _acc(dbx_ref, first, jnp.sum(dgx, axis=0, keepdims=True))
        xb = xc.astype(BF16)
        dgab = dga.astype(BF16)
        dgxb = dgx.astype(BF16)
        _acc(dwa_ref, first, lax.dot_general(xb, dgab, TN, preferred_element_type=F32))
        _acc(dwx_ref, first, lax.dot_general(xb, dgxb, TN, preferred_element_type=F32))
        dxc = (dxc + lax.dot_general(dgab, wa_ref[...], NT, preferred_element_type=F32)
               + lax.dot_general(dgxb, wx_ref[...], NT, preferred_element_type=F32))

        _acc(dcb_ref, first, jnp.sum(dxc, axis=0, keepdims=True))
        r8 = _row_iota((8, W_A))
        dcw = jnp.zeros((8, W_A), F32)
        for j in range(LRU_K):
            tap = jnp.sum(dxc * xs_ref[pl.ds(pad - (LRU_K - 1) + j, tc), :], axis=0, keepdims=True)
            dcw = dcw + jnp.where(r8 == j, tap, 0.0)
        _acc(dcw_ref, first, dcw)
        ds_ref[0:tc, :] = dxc
        ds_ref[tc:tc + pad, :] = nx_ref[...]
        dlx = None
        for j in range(LRU_K):
            term = cw_ref[j:j + 1, :] * ds_ref[pl.ds(LRU_K - 1 - j, tc), :]
            dlx = term if dlx is None else dlx + term
        dp_ref[:, 0:W_A] = dlx
        nx_ref[...] = dxc[0:pad, :]

    rev = lambda c: pl.BlockSpec((tc, W_A), lambda t, c=c: (nc - 1 - t, c))
    prev = lambda c: pl.BlockSpec((tc, W_A), lambda t, c=c: (jnp.maximum(nc - 2 - t, 0), c))
    full = lambda a: pl.BlockSpec(a.shape, lambda t: (0,) * a.ndim)
    params = [cw, cb, wa, ba, wx, bx, lam, gg]
    vec = SDS((1, W_A), F32)
    sq = SDS((W_A, W_A), F32)
    outs = [SDS((s, 2 * W_A), F32), SDS((8, W_A), F32), vec, sq, vec, sq, vec, vec, vec]
    return pl.pallas_call(
        body, name="lru_bwd", grid=(nc,),
        in_specs=[rev(0), rev(0), prev(0), rev(1), rev(0), prev(0)] + [full(a) for a in params],
        out_specs=[pl.BlockSpec((tc, 2 * W_A), lambda t: (nc - 1 - t, 0))]
        + [pl.BlockSpec(o.shape, lambda t: (0, 0)) for o in outs[1:]],
        out_shape=outs,
        scratch_shapes=[pltpu.VMEM((tc + pad, W_A), F32), pltpu.VMEM((tc + pad, W_A), F32),
                        pltpu.VMEM((1, W_A), F32), pltpu.VMEM((pad, W_A), F32)],
        compiler_params=_cp("arbitrary"),
    )(dy, proj, proj, proj, h, h, *params)


def _attn_stack(qa, qb, kvh):
    lane = lax.broadcasted_iota(jnp.int32, qa.shape, 1)
    keep = (lane >= HD) if kvh == 1 else (lane < HD)
    parts = []
    for tile in (qa, qb):
        for half in (0, 1):
            y = tile if half == kvh else pltpu.roll(tile, HD, axis=1)
            parts.append(jnp.where(keep, y, 0.0))
    return jnp.concatenate(parts, axis=0)


def _attn_unstack(o, kvh):
    lane = lax.broadcasted_iota(jnp.int32, (BLK, 2 * HD), 1)
    tiles = []
    for t in range(2):
        halves = []
        for half in (0, 1):
            blk = o[(2 * t + half) * BLK:(2 * t + half + 1) * BLK, :]
            halves.append(blk if half == kvh else pltpu.roll(blk, HD, axis=1))
        tiles.append(jnp.where(lane < HD, halves[0], halves[1]))
    return tiles


def _attn_mask(n):
    qi = lax.broadcasted_iota(jnp.int32, (BLK, 2 * BLK), 0)
    kj = lax.broadcasted_iota(jnp.int32, (BLK, 2 * BLK), 1)
    rel = BLK + qi - kj
    return (rel >= 0) & (rel < BLK) & ((n - 1) * BLK + kj >= 0)


def _attn_probs(qs, kw, mask, sink_ref, kvh):
    sc = lax.dot_general(qs.astype(BF16), kw, NT, preferred_element_type=F32) * SCALE
    ps, psinks = [], []
    for rr in range(4):
        sk = sink_ref[4 * kvh + rr:4 * kvh + rr + 1, 0:1]
        sh = jnp.where(mask, sc[rr * BLK:(rr + 1) * BLK, :], NEG_BIG)
        m = jnp.maximum(jnp.max(sh, axis=-1, keepdims=True), sk)
        e = jnp.exp(sh - m)
        es = jnp.exp(sk - m)
        z = jnp.sum(e, axis=-1, keepdims=True) + es
        ps.append(e / z)
        psinks.append(es / z)
    return ps, psinks


def _attn_fwd(proj, sinks8, gg):
    s = proj.shape[0]

    def body(q_ref, kc_ref, kp_ref, vc_ref, vp_ref, sink_ref, gg_ref, yn_ref, ob_ref):
        n = pl.program_id(0)
        mask = _attn_mask(n)
        kw = jnp.concatenate([kp_ref[...], kc_ref[...]], axis=0).astype(BF16)
        vw = jnp.concatenate([vp_ref[...], vc_ref[...]], axis=0).astype(BF16)
        for kvh in range(2):
            qa = q_ref[:, 256 * kvh:256 * kvh + 128]
            qb = q_ref[:, 256 * kvh + 128:256 * kvh + 256]
            ps, _ = _attn_probs(_attn_stack(qa, qb, kvh), kw, mask, sink_ref, kvh)
            o = jnp.dot(jnp.concatenate(ps, axis=0).astype(BF16), vw, preferred_element_type=F32)
            ta, tb = _attn_unstack(o, kvh)
            ob_ref[:, 256 * kvh:256 * kvh + 128] = ta
            ob_ref[:, 256 * kvh + 128:256 * kvh + 256] = tb
        ob = ob_ref[...]
        yn_ref[...] = (ob * _rsq(ob, NORM_EPS) * gg_ref[...]).astype(BF16)

    kv = lambda c, back: pl.BlockSpec((BLK, 128), lambda n, c=c, back=back: (jnp.maximum(n - back, 0), c))
    out = pl.BlockSpec((BLK, W_B), lambda n: (n, 0))
    return pl.pallas_call(
        body, name="attn_fwd", grid=(s // BLK,),
        in_specs=[pl.BlockSpec((BLK, W_B), lambda n: (n, 1)), kv(8, 0), kv(8, 1), kv(9, 0), kv(9, 1),
                  pl.BlockSpec((8, 128), lambda n: (0, 0)), pl.BlockSpec((1, W_B), lambda n: (0, 0))],
        out_specs=[out, out], out_shape=[SDS((s, W_B), BF16), SDS((s, W_B), F32)],
        compiler_params=_cp("parallel"),
    )(proj, proj, proj, proj, proj, sinks8, gg)


def _attn_bwd(dy, proj, ob, sinks8, gg):
    s = proj.shape[0]

    def body(dya_ref, dyb_ref, q_ref, kc_ref, kp_ref, vc_ref, vp_ref, ob_ref, sink_ref, gg_ref,
             dq_ref, dcur_ref, dprev_ref, dsink_ref, dgg_ref):
        n = pl.program_id(0)
        first = n == 0
        mask = _attn_mask(n)
        kw = jnp.concatenate([kp_ref[...], kc_ref[...]], axis=0).astype(BF16)
        vw = jnp.concatenate([vp_ref[...], vc_ref[...]], axis=0).astype(BF16)
        dyn = jnp.concatenate([dya_ref[...], dyb_ref[...]], axis=1)
        dob, dggr = _rms_bwd_rows(ob_ref[...], gg_ref[...], dyn)
        _acc(dgg_ref, first, jnp.sum(dggr, axis=0, keepdims=True))
        r8 = _row_iota((8, 128))
        dsk = jnp.zeros((8, 128), F32)
        dkw = jnp.zeros((2 * BLK, 128), F32)
        dvw = jnp.zeros((2 * BLK, 128), F32)
        for kvh in range(2):
            qs = _attn_stack(q_ref[:, 256 * kvh:256 * kvh + 128], q_ref[:, 256 * kvh + 128:256 * kvh + 256], kvh)
            ps, psinks = _attn_probs(qs, kw, mask, sink_ref, kvh)
            dos = _attn_stack(dob[:, 256 * kvh:256 * kvh + 128], dob[:, 256 * kvh + 128:256 * kvh + 256], kvh)
            dosb = dos.astype(BF16)
            dp = lax.dot_general(dosb, vw, NT, preferred_element_type=F32)
            dss = []
            for rr in range(4):
                dpr = dp[rr * BLK:(rr + 1) * BLK, :]
                dd = jnp.sum(ps[rr] * dpr, axis=-1, keepdims=True)
                dss.append(ps[rr] * (dpr - dd) * SCALE)
                tot = jnp.sum(-psinks[rr] * dd, axis=0, keepdims=True)
                dsk = dsk + jnp.where(r8 == 4 * kvh + rr, tot, 0.0)
            dsb = jnp.concatenate(dss, axis=0).astype(BF16)
            pb = jnp.concatenate(ps, axis=0).astype(BF16)
            dqs = jnp.dot(dsb, kw, preferred_element_type=F32)
            ta, tb = _attn_unstack(dqs, kvh)
            dq_ref[:, 256 * kvh:256 * kvh + 128] = ta
            dq_ref[:, 256 * kvh + 128:256 * kvh + 256] = tb
            dkw = dkw + lax.dot_general(dsb, qs.astype(BF16), TN, preferred_element_type=F32)
            dvw = dvw + lax.dot_general(pb, dosb, TN, preferred_element_type=F32)
        _acc(dsink_ref, first, dsk)
        dprev_ref[:, 0:128] = dkw[0:BLK, :]
        dprev_ref[:, 128:256] = dvw[0:BLK, :]
        dcur_ref[:, 0:128] = dkw[BLK:2 * BLK, :]
        dcur_ref[:, 128:256] = dvw[BLK:2 * BLK, :]

    kv = lambda c, back: pl.BlockSpec((BLK, 128), lambda n, c=c, back=back: (jnp.maximum(n - back, 0), c))
    wide = pl.BlockSpec((BLK, W_B), lambda n: (n, 0))
    half = pl.BlockSpec((BLK, 256), lambda n: (n, 0))
    return pl.pallas_call(
        body, name="attn_bwd", grid=(s // BLK,),
        in_specs=[pl.BlockSpec((BLK, 256), lambda n: (n, 1)), pl.BlockSpec((BLK, 256), lambda n: (n, 2)),
                  pl.BlockSpec((BLK, W_B), lambda n: (n, 1)), kv(8, 0), kv(8, 1), kv(9, 0), kv(9, 1), wide,
                  pl.BlockSpec((8, 128), lambda n: (0, 0)), pl.BlockSpec((1, W_B), lambda n: (0, 0))],
        out_specs=[wide, half, half, pl.BlockSpec((8, 128), lambda n: (0, 0)), pl.BlockSpec((1, W_B), lambda n: (0, 0))],
        out_shape=[SDS((s, W_B), F32), SDS((s, 256), F32), SDS((s, 256), F32), SDS((8, 128), F32), SDS((1, W_B), F32)],
        compiler_params=_cp("arbitrary"),
    )(dy, dy, proj, proj, proj, proj, proj, ob, sinks8, gg)


def _ln_parts(y1, eps=LN_EPS):
    mu = jnp.mean(y1, axis=-1, keepdims=True)
    xc = y1 - mu
    rstd = lax.rsqrt(jnp.mean(xc * xc, axis=-1, keepdims=True) + eps)
    return xc * rstd, rstd


def _conf_fwd(proj, cw, cb, lg, lb, gg, tc):
    s = proj.shape[0]
    pad = 32

    def body(ac_ref, gc_ref, ap_ref, gp_ref, cw_ref, cb_ref, lg_ref, lb_ref, gg_ref, yn_ref, y1_ref, ys_ref):
        i = pl.program_id(0)
        tail = ap_ref[tc - pad:tc, :] * _sig(gp_ref[tc - pad:tc, :])
        ys_ref[0:pad, :] = jnp.where(i > 0, tail, 0.0)
        ys_ref[pad:pad + tc, :] = ac_ref[...] * _sig(gc_ref[...])
        y1 = _conv_taps(ys_ref, cw_ref, CONV_K, pad, tc) + cb_ref[...]
        y1_ref[...] = y1
        xh, _ = _ln_parts(y1)
        yl = xh * lg_ref[...] + lb_ref[...]
        yc = yl * _sig(yl)
        yn_ref[...] = (yc * _rsq(yc, NORM_EPS) * gg_ref[...]).astype(BF16)

    cur = lambda c: pl.BlockSpec((tc, W_C), lambda i, c=c: (i, c))
    prev = lambda c: pl.BlockSpec((tc, W_C), lambda i, c=c: (jnp.maximum(i - 1, 0), c))
    full = lambda a: pl.BlockSpec(a.shape, lambda i: (0,) * a.ndim)
    params = [cw, cb, lg, lb, gg]
    out = pl.BlockSpec((tc, W_C), lambda i: (i, 0))
    return pl.pallas_call(
        body, name="conf_fwd", grid=(s // tc,),
        in_specs=[cur(5), cur(6), prev(5), prev(6)] + [full(a) for a in params],
        out_specs=[out, out], out_shape=[SDS((s, W_C), BF16), SDS((s, W_C), F32)],
        scratch_shapes=[pltpu.VMEM((tc + pad, W_C), F32)],
        compiler_params=_cp("parallel"),
    )(proj, proj, proj, proj, *params)


def _conf_bwd(dy, proj, y1, cw, cb, lg, lb, gg, tc):
    s = proj.shape[0]
    nc = s // tc
    pad = 32

    def body(dy_ref, ac_ref, gc_ref, ap_ref, gp_ref, y1_ref, cw_ref, cb_ref, lg_ref, lb_ref, gg_ref,
             dp_ref, dcw_ref, dcb_ref, dlg_ref, dlb_ref, dgg_ref, ys_ref, ds_ref, nx_ref):
        step = pl.program_id(0)
        i = nc - 1 - step
        first = step == 0

        @pl.when(first)
        def _():
            nx_ref[...] = jnp.zeros_like(nx_ref)

        a = ac_ref[...]
        sg = _sig(gc_ref[...])
        tail = ap_ref[tc - pad:tc, :] * _sig(gp_ref[tc - pad:tc, :])
        ys_ref[0:pad, :] = jnp.where(i > 0, tail, 0.0)
        ys_ref[pad:pad + tc, :] = a * sg
        xh, rstd = _ln_parts(y1_ref[...])
        yl = xh * lg_ref[...] + lb_ref[...]
        sl = _sig(yl)
        yc = yl * sl
        dyc, dggr = _rms_bwd_rows(yc, gg_ref[...], dy_ref[...])
        _acc(dgg_ref, first, jnp.sum(dggr, axis=0, keepdims=True))
        dyl = dyc * sl * (1.0 + yl * (1.0 - sl))
        _acc(dlg_ref, first, jnp.sum(dyl * xh, axis=0, keepdims=True))
        _acc(dlb_ref, first, jnp.sum(dyl, axis=0, keepdims=True))
        dxh = dyl * lg_ref[...]
        dy1 = rstd * (dxh - jnp.mean(dxh, axis=-1, keepdims=True) - xh * jnp.mean(dxh * xh, axis=-1, keepdims=True))
        _acc(dcb_ref, first, jnp.sum(dy1, axis=0, keepdims=True))
        r32 = _row_iota((32, W_C))
        dcw = jnp.zeros((32, W_C), F32)
        for j in range(CONV_K):
            tap = jnp.sum(dy1 * ys_ref[pl.ds(pad - (CONV_K - 1) + j, tc), :], axis=0, keepdims=True)
            dcw = dcw + jnp.where(r32 == j, tap, 0.0)
        _acc(dcw_ref, first, dcw)
        ds_ref[0:tc, :] = dy1
        ds_ref[tc:tc + pad, :] = nx_ref[...]
        dy0 = None
        for j in range(CONV_K):
            term = cw_ref[j:j + 1, :] * ds_ref[pl.ds(CONV_K - 1 - j, tc), :]
            dy0 = term if dy0 is None else dy0 + term
        dp_ref[:, 0:W_C] = dy0 * sg
        dp_ref[:, W_C:2 * W_C] = dy0 * a * sg * (1.0 - sg)
        nx_ref[...] = dy1[0:pad, :]

    rev = lambda c: pl.BlockSpec((tc, W_C), lambda t, c=c: (nc - 1 - t, c))
    prev = lambda c: pl.BlockSpec((tc, W_C), lambda t, c=c: (jnp.maximum(nc - 2 - t, 0), c))
    full = lambda a: pl.BlockSpec(a.shape, lambda t: (0,) * a.ndim)
    params = [cw, cb, lg, lb, gg]
    vec = SDS((1, W_C), F32)
    outs = [SDS((s, 2 * W_C), F32), SDS((32, W_C), F32), vec, vec, vec, vec]
    return pl.pallas_call(
        body, name="conf_bwd", grid=(nc,),
        in_specs=[rev(3), rev(5), rev(6), prev(5), prev(6), rev(0)] + [full(a) for a in params],
        out_specs=[pl.BlockSpec((tc, 2 * W_C), lambda t: (nc - 1 - t, 0))]
        + [pl.BlockSpec(o.shape, lambda t: (0, 0)) for o in outs[1:]],
        out_shape=outs,
        scratch_shapes=[pltpu.VMEM((tc + pad, W_C), F32), pltpu.VMEM((tc + pad, W_C), F32), pltpu.VMEM((pad, W_C), F32)],
        compiler_params=_cp("arbitrary"),
    )(dy, proj, proj, proj, proj, y1, *params)


def _assemble_dproj(dlru, dq, dcur, dprev, dconf):
    s = dq.shape[0]
    nb = s // BLK

    def body(dl_ref, dq_ref, dc_ref, dn_ref, df_ref, o_ref):
        n = pl.program_id(0)
        o_ref[:, 0:512] = dl_ref[...].astype(BF16)
        o_ref[:, 512:1024] = dq_ref[...].astype(BF16)
        o_ref[:, 1024:1280] = (dc_ref[...] + jnp.where(n < nb - 1, dn_ref[...], 0.0)).astype(BF16)
        o_ref[:, 1280:1792] = df_ref[...].astype(BF16)

    wide = pl.BlockSpec((BLK, 512), lambda n: (n, 0))
    return pl.pallas_call(
        body, name="assemble_dproj", grid=(nb,),
        in_specs=[wide, wide, pl.BlockSpec((BLK, 256), lambda n: (n, 0)),
                  pl.BlockSpec((BLK, 256), lambda n: (jnp.minimum(n + 1, nb - 1), 0)), wide],
        out_specs=pl.BlockSpec((BLK, P_IN), lambda n: (n, 0)), out_shape=SDS((s, P_IN), BF16),
        compiler_params=_cp("parallel"),
    )(dlru, dq, dcur, dprev, dconf)


def _loss_grad(y, t, tm):
    s = y.shape[0]

    def body(y_ref, t_ref, dy_ref, l_ref):
        err = y_ref[...] - t_ref[...]
        dy_ref[...] = err * (1.0 / D)
        _acc(l_ref, pl.program_id(0) == 0, jnp.sum(err * err, axis=0, keepdims=True))

    row = pl.BlockSpec((tm, D), lambda i: (i, 0))
    return pl.pallas_call(
        body, name="loss_grad", grid=(s // tm,), in_specs=[row, row],
        out_specs=[row, pl.BlockSpec((1, D), lambda i: (0, 0))],
        out_shape=[SDS((s, D), F32), SDS((1, D), F32)], compiler_params=_cp("arbitrary"),
    )(y, t)


def _block_diag(w):
    out = jnp.zeros((W_A, W_A), w.dtype)
    for h in range(4):
        out = lax.dynamic_update_slice(out, w[h], (64 * h, 64 * h))
    return out


def _diag_blocks(m):
    return jnp.stack([m[64 * h:64 * (h + 1), 64 * h:64 * (h + 1)] for h in range(4)])


def _layer_params(small, l):
    v = lambda name: small[name][l].reshape(1, -1)
    gg = small["group_g"][l]
    return dict(
        ffn1_pre=v("ffn1_pre_g"), ffn1_post=v("ffn1_post_g"), mix_pre=v("mix_pre_g"), mix_post=v("mix_post_g"),
        ffn2_pre=v("ffn2_pre_g"), ffn2_post=v("ffn2_post_g"),
        lru_cw=small["lru_conv_w"][l], lru_cb=v("lru_conv_b"),
        wa=_block_diag(small["lru_w_a"][l]).astype(BF16), ba=v("lru_b_a"),
        wx=_block_diag(small["lru_w_x"][l]).astype(BF16), bx=v("lru_b_x"), lam=v("lru_lambda"),
        sinks8=jnp.broadcast_to(small["attn_sinks"][l][:, None], (NQ, 128)),
        conv_w=small["conv_w"][l], conv_b=v("conv_b"), ln_g=v("conv_ln_g"), ln_b=v("conv_ln_b"),
        gg_a=gg[0:W_A].reshape(1, -1), gg_b=gg[W_A:W_A + W_B].reshape(1, -1), gg_c=gg[W_A + W_B:].reshape(1, -1),
    )


def _local_step(x, target, big, small, bufs):
    s = x.shape[0]
    tm = min(512, s)
    tc = min(512, s // 2)
    saved = []
    for l in range(DEPTH):
        p = _layer_params(small, l)
        sv = dict(p=p, x0=x)
        h1, g1, u1, a1 = _ffn_up(x, p["ffn1_pre"], big["ffn1_w_gu"], l, tm)
        z1, x = _mm_rms_res(a1, big["ffn1_w_down"], l, x, p["ffn1_post"], 0.5, tm, FH, "ffn_down")
        sv.update(h1=h1, g1=g1, u1=u1, a1=a1, z1=z1, x1=x)
        hn, proj = _proj(x, p["mix_pre"], big["w_in"], l, tm)
        yn_a, hl = _lru_fwd(proj, p["lru_cw"], p["lru_cb"], p["wa"], p["ba"], p["wx"], p["bx"], p["lam"], p["gg_a"], tc)
        yn_b, ob = _attn_fwd(proj, p["sinks8"], p["gg_b"])
        yn_c, y1 = _conf_fwd(proj, p["conv_w"], p["conv_b"], p["ln_g"], p["ln_b"], p["gg_c"], tc)
        ycat = jnp.concatenate([yn_a, yn_b, yn_c], axis=1)
        zo, x = _mm_rms_res(ycat, big["w_out"], l, x, p["mix_post"], 1.0, tm, D, "mix_out")
        sv.update(hn=hn, proj=proj, hl=hl, ob=ob, y1=y1, ycat=ycat, zo=zo, x2=x)
        h2, g2, u2, a2 = _ffn_up(x, p["ffn2_pre"], big["ffn2_w_gu"], l, tm)
        z2, x = _mm_rms_res(a2, big["ffn2_w_down"], l, x, p["ffn2_post"], 0.5, tm, FH, "ffn_down")
        sv.update(h2=h2, g2=g2, u2=u2, a2=a2, z2=z2)
        saved.append(sv)

    dx, lcols = _loss_grad(x, target, tm)
    sgrads = [None] * DEPTH

    def ffn_bwd(dx, l, which, xin, h, g, u, a, z, pre, post):
        dz, dpost = _rms_bwd(dx, z, post, 0.5, tm, "ffn_post_bwd")
        dg, du = _ffn_bwd_mid(dz, big[which + "_w_down"], l, g, u, tm)
        bufs[which + "_w_down"] = _mm_tn_into(bufs[which + "_w_down"].reshape(DEPTH, 1, DFF, D), a, dz, (l,), FH, D, tm,
                                              "dw_down").reshape(DEPTH, NSHARD, DFF // NSHARD, D)
        bufs[which + "_w_gu"] = _mm_tn_into(bufs[which + "_w_gu"], h, dg, (l,), D, FH, tm, "dw_gate")
        bufs[which + "_w_gu"] = _mm_tn_into(bufs[which + "_w_gu"].reshape(DEPTH, 2, 2, D, FH), h, du, (l, 1), D, FH, tm,
                                            "dw_up").reshape(DEPTH, NSHARD, D, FH)
        dxn, dpre = _ffn_bwd_dh(dg, du, big[which + "_w_gu"], l, xin, pre, dx, tm)
        return dxn, dpre, dpost

    for l in reversed(range(DEPTH)):
        sv = saved[l]
        p = sv["p"]
        gr = {}
        dx, gr["ffn2_pre_g"], gr["ffn2_post_g"] = ffn_bwd(dx, l, "ffn2", sv["x2"], sv["h2"], sv["g2"], sv["u2"], sv["a2"],
                                                          sv["z2"], p["ffn2_pre"], p["ffn2_post"])
        do, gr["mix_post_g"] = _rms_bwd(dx, sv["zo"], p["mix_post"], 1.0, tm, "mix_post_bwd")
        bufs["w_out"] = _mm_tn_into(bufs["w_out"].reshape(DEPTH, 1, D, D), sv["ycat"], do, (l,), D, D, tm,
                                    "dw_out").reshape(DEPTH, NSHARD, D // NSHARD, D)
        dy = _mm_nt(do, big["w_out"], l, tm, "mix_dy")
        proj = sv["proj"]
        (dlru, dcw, gr["lru_conv_b"], dwa, gr["lru_b_a"], dwx, gr["lru_b_x"], gr["lru_lambda"], dgg_a) = _lru_bwd(
            dy, proj, sv["hl"], p["lru_cw"], p["lru_cb"], p["wa"], p["ba"], p["wx"], p["bx"], p["lam"], p["gg_a"], tc)
        dq, dcur, dprev, dsk, dgg_b = _attn_bwd(dy, proj, sv["ob"], p["sinks8"], p["gg_b"])
        dconf, dconvw, gr["conv_b"], gr["conv_ln_g"], gr["conv_ln_b"], dgg_c = _conf_bwd(
            dy, proj, sv["y1"], p["conv_w"], p["conv_b"], p["ln_g"], p["ln_b"], p["gg_c"], tc)
        dproj = _assemble_dproj(dlru, dq, dcur, dprev, dconf)
        bufs["w_in_flat"] = _mm_tn_into(bufs["w_in_flat"], sv["hn"], dproj, (l,), D, P_IN, tm, "dw_in")
        dx, gr["mix_pre_g"] = _mm_nt_rmsbwd(dproj, big["w_in"], l, sv["x1"], p["mix_pre"], dx, tm)
        gr["lru_conv_w"] = dcw[0:LRU_K]
        gr["lru_w_a"] = _diag_blocks(dwa)
        gr["lru_w_x"] = _diag_blocks(dwx)
        gr["attn_sinks"] = dsk[:, 0]
        gr["conv_w"] = dconvw[0:CONV_K]
        gr["group_g"] = jnp.concatenate([dgg_a, dgg_b, dgg_c], axis=1)
        dx, gr["ffn1_pre_g"], gr["ffn1_post_g"] = ffn_bwd(dx, l, "ffn1", sv["x0"], sv["h1"], sv["g1"], sv["u1"], sv["a1"],
                                                          sv["z1"], p["ffn1_pre"], p["ffn1_post"])
        sgrads[l] = gr
    return lcols, dx, bufs, sgrads


MESH = pl.DeviceIdType.MESH
ANY = pl.BlockSpec(memory_space=pl.ANY)


def _place():
    x, y, c = lax.axis_index("x"), lax.axis_index("y"), lax.axis_index("c")
    return x, y, c, [(1 - x, y), (x, 1 - y), (1 - x, 1 - y)]


def _allgather_weights(shards):
    n = len(shards)

    def body(*refs):
        ins, outs = refs[:n], refs[n:2 * n]
        send_sems, recv_sems, local_sems = refs[2 * n:]
        x, y, c, chips = _place()
        p = 2 * x + y
        me, sibling = (x, y, c), (x, y, 1 - c)

        def rcopy(a, k, layer, q, to, src=None):
            dst = outs[a].at[layer, q]
            return pltpu.make_async_remote_copy(src_ref=dst if src is None else src, dst_ref=dst,
                                                send_sem=send_sems.at[a, k], recv_sem=recv_sems.at[a, k],
                                                device_id=to, device_id_type=MESH)

        local = [pltpu.make_async_copy(ins[a], outs[a].at[:, p], local_sems.at[a]) for a in range(n)]
        for cp in local:
            cp.start()
        first = [rcopy(a, j, c, p, (*chip, c), src=ins[a].at[c]) for a in range(n) for j, chip in enumerate(chips)]
        for cp in first:
            cp.start()
        passed = []
        for a in range(n):
            for j, chip in enumerate(chips):
                q = 2 * chip[0] + chip[1]
                rcopy(a, j, c, q, me).wait_recv()
                passed.append(rcopy(a, 3 + j, c, q, sibling))
                passed[-1].start()
        for a in range(n):
            for j, chip in enumerate(chips):
                rcopy(a, 3 + j, 1 - c, 2 * chip[0] + chip[1], me).wait_recv()
        for cp in first + passed:
            cp.wait_send()
        for cp in local:
            cp.wait()

    return pl.pallas_call(
        body, name="allgather_weights", in_specs=[ANY] * n, out_specs=[ANY] * n,
        out_shape=[SDS((DEPTH, NSHARD) + s.shape[1:], s.dtype) for s in shards],
        scratch_shapes=[pltpu.SemaphoreType.DMA((n, 6)), pltpu.SemaphoreType.DMA((n, 6)), pltpu.SemaphoreType.DMA((n,))],
    )(*shards)


def _pair_exchange(gs):
    n = len(gs)

    def body(*refs):
        ins, outs = refs[:n], refs[n:2 * n]
        send_sems, recv_sems = refs[2 * n:]
        x, y, c, _ = _place()
        cps = [pltpu.make_async_remote_copy(src_ref=ins[a].at[1 - c], dst_ref=outs[a], send_sem=send_sems.at[a],
                                            recv_sem=recv_sems.at[a], device_id=(x, y, 1 - c), device_id_type=MESH)
               for a in range(n)]
        for cp in cps:
            cp.start()
        for cp in cps:
            cp.wait()

    return pl.pallas_call(
        body, name="grad_pair_exchange", in_specs=[ANY] * n, out_specs=[ANY] * n,
        out_shape=[SDS(g.shape[1:], g.dtype) for g in gs],
        scratch_shapes=[pltpu.SemaphoreType.DMA((n,)), pltpu.SemaphoreType.DMA((n,))],
    )(*gs)


def _chip_exchange(ts):
    n = len(ts)

    def body(*refs):
        ins, outs = refs[:n], refs[n:2 * n]
        send_sems, recv_sems = refs[2 * n:]
        x, y, c, chips = _place()
        cps = [pltpu.make_async_remote_copy(src_ref=ins[a].at[2 * chip[0] + chip[1]], dst_ref=outs[a].at[j],
                                            send_sem=send_sems.at[a, j], recv_sem=recv_sems.at[a, j],
                                            device_id=(*chip, c), device_id_type=MESH)
               for a in range(n) for j, chip in enumerate(chips)]
        for cp in cps:
            cp.start()
        for cp in cps:
            cp.wait()

    return pl.pallas_call(
        body, name="grad_chip_exchange", in_specs=[ANY] * n, out_specs=[ANY] * n,
        out_shape=[SDS((3,) + t.shape[1:], t.dtype) for t in ts],
        scratch_shapes=[pltpu.SemaphoreType.DMA((n, 3)), pltpu.SemaphoreType.DMA((n, 3))],
    )(*ts)


def _pair_share(reds):
    n = len(reds)

    def body(*refs):
        ins, outs = refs[:n], refs[n:2 * n]
        send_sems, recv_sems, local_sems = refs[2 * n:]
        x, y, c, _ = _place()
        local = [pltpu.make_async_copy(ins[a], outs[a].at[c], local_sems.at[a]) for a in range(n)]
        cps = [pltpu.make_async_remote_copy(src_ref=ins[a], dst_ref=outs[a].at[c], send_sem=send_sems.at[a],
                                            recv_sem=recv_sems.at[a], device_id=(x, y, 1 - c), device_id_type=MESH)
               for a in range(n)]
        for cp in local + cps:
            cp.start()
        for a in range(n):
            cps[a].wait_send()
            pltpu.make_async_remote_copy(src_ref=ins[a], dst_ref=outs[a].at[1 - c], send_sem=send_sems.at[a],
                                         recv_sem=recv_sems.at[a], device_id=(x, y, c), device_id_type=MESH).wait_recv()
        for cp in local:
            cp.wait()

    return pl.pallas_call(
        body, name="grad_pair_share", in_specs=[ANY] * n, out_specs=[ANY] * n,
        out_shape=[SDS((DEPTH,) + r.shape, r.dtype) for r in reds],
        scratch_shapes=[pltpu.SemaphoreType.DMA((n,)), pltpu.SemaphoreType.DMA((n,)), pltpu.SemaphoreType.DMA((n,))],
    )(*reds)


def _allreduce_small(buf):
    rows = buf.shape[0]

    def body(in_ref, out_ref, gather_ref, send_sems, recv_sems):
        x, y, c, _ = _place()
        me = 4 * x + 2 * y + c
        gather_ref[me] = in_ref[...]
        cps, slots = [], []
        for m in range(1, NDEV):
            px = 1 - x if m & 4 else x
            py = 1 - y if m & 2 else y
            pc = 1 - c if m & 1 else c
            cps.append(pltpu.make_async_remote_copy(src_ref=in_ref, dst_ref=gather_ref.at[me], send_sem=send_sems.at[m - 1],
                                                    recv_sem=recv_sems.at[m - 1], device_id=(px, py, pc), device_id_type=MESH))
            slots.append(4 * px + 2 * py + pc)
        for cp in cps:
            cp.start()
        for m in range(1, NDEV):
            pltpu.make_async_remote_copy(src_ref=in_ref, dst_ref=gather_ref.at[slots[m - 1]], send_sem=send_sems.at[m - 1],
                                         recv_sem=recv_sems.at[m - 1], device_id=(x, y, c), device_id_type=MESH).wait_recv()
        for cp in cps:
            cp.wait_send()
        total = gather_ref[0]
        for dev in range(1, NDEV):
            total = total + gather_ref[dev]
        out_ref[...] = total

    vm = pl.BlockSpec(memory_space=pltpu.VMEM)
    return pl.pallas_call(
        body, name="allreduce_small", in_specs=[vm], out_specs=vm, out_shape=SDS(buf.shape, F32),
        scratch_shapes=[pltpu.VMEM((NDEV, rows, 128), F32), pltpu.SemaphoreType.DMA((NDEV - 1,)),
                        pltpu.SemaphoreType.DMA((NDEV - 1,))],
        compiler_params=pltpu.CompilerParams(vmem_limit_bytes=VMEM_LIMIT),
    )(buf)


BLOCK_ELEMS = 256 * 1024


def _rows_per_block(rows, cols, mult):
    best = None
    for tr in range(mult, rows + 1, mult):
        if rows % tr == 0 and tr * cols <= BLOCK_ELEMS:
            best = tr
    assert best is not None, (rows, cols)
    return best


def _pair_sum(g, r, c_idx):
    _, nq, rows, cols = g.shape
    tr = _rows_per_block(rows, cols, 16)

    def body(c_ref, g_ref, r_ref, t_ref):
        t_ref[...] = (g_ref[...] + r_ref[...]).astype(BF16)

    blk = pl.BlockSpec((None, tr, cols), lambda q, i, cr: (q, i, 0))
    spec = pltpu.PrefetchScalarGridSpec(
        num_scalar_prefetch=1, grid=(nq, rows // tr),
        in_specs=[pl.BlockSpec((None, None, tr, cols), lambda q, i, cr: (cr[0], q, i, 0)), blk], out_specs=blk)
    return pl.pallas_call(body, name="grad_pair_sum", grid_spec=spec, out_shape=SDS((nq, rows, cols), BF16),
                          compiler_params=_cp("parallel", "parallel"))(c_idx, g, r)


def _chip_sum(g, r, rr, cp_idx):
    _, _, rows, cols = g.shape
    tr = _rows_per_block(rows, cols, 16)

    def body(cp_ref, g_ref, r_ref, rr_ref, o_ref):
        o_ref[...] = ((g_ref[...] + r_ref[...]) + rr_ref[0].astype(F32) + rr_ref[1].astype(F32) + rr_ref[2].astype(F32))

    spec = pltpu.PrefetchScalarGridSpec(
        num_scalar_prefetch=1, grid=(rows // tr,),
        in_specs=[pl.BlockSpec((None, None, tr, cols), lambda i, cp: (cp[0], cp[1], i, 0)),
                  pl.BlockSpec((None, tr, cols), lambda i, cp: (cp[1], i, 0)),
                  pl.BlockSpec((3, tr, cols), lambda i, cp: (0, i, 0))],
        out_specs=pl.BlockSpec((tr, cols), lambda i, cp: (i, 0)))
    return pl.pallas_call(body, name="grad_chip_sum", grid_spec=spec, out_shape=SDS((rows, cols), F32),
                          compiler_params=_cp("parallel"))(cp_idx, g, r, rr)


def _adamw(w, g, m, v):
    nb, rows, cols = w.shape
    tr = _rows_per_block(rows, cols, 8)

    def body(w_ref, g_ref, m_ref, v_ref, d_ref, mo_ref, vo_ref):
        gg = g_ref[...]
        mn = ADAM_B1 * m_ref[...] + (1.0 - ADAM_B1) * gg
        vn = ADAM_B2 * v_ref[...] + (1.0 - ADAM_B2) * (gg * gg)
        m_hat = mn / (1.0 - ADAM_B1 ** ADAM_STEP)
        v_hat = vn / (1.0 - ADAM_B2 ** ADAM_STEP)
        d_ref[...] = -ADAM_LR * (m_hat / (jnp.sqrt(v_hat) + ADAM_EPS) + ADAM_WD * w_ref[...])
        mo_ref[...] = mn
        vo_ref[...] = vn

    blk = pl.BlockSpec((None, tr, cols), lambda b, i: (b, i, 0))
    return pl.pallas_call(body, name="adamw", grid=(nb, rows // tr), in_specs=[blk] * 4, out_specs=[blk] * 3,
                          out_shape=[SDS(w.shape, F32)] * 3, compiler_params=_cp("parallel", "parallel"))(w, g, m, v)


_WEIGHTS = ["ffn1_pre_g", "ffn1_w_gu", "ffn1_w_down", "ffn1_post_g", "mix_pre_g", "w_in", "lru_conv_w", "lru_conv_b",
            "lru_w_a", "lru_b_a", "lru_w_x", "lru_b_x", "lru_lambda", "attn_sinks", "conv_w", "conv_b", "conv_ln_g",
            "conv_ln_b", "group_g", "w_out", "mix_post_g", "ffn2_pre_g", "ffn2_w_gu", "ffn2_w_down", "ffn2_post_g"]
_INPUTS = ["x"] + _WEIGHTS + ["loss_target"] + ["m_" + n for n in _WEIGHTS] + ["v_" + n for n in _WEIGHTS]
_BIG = ["ffn1_w_gu", "ffn1_w_down", "w_in", "w_out", "ffn2_w_gu", "ffn2_w_down"]
_SMALL_SHARDED = ["lru_conv_w", "conv_w"]
_SMALL_REPL = [n for n in _WEIGHTS if n not in _BIG and n not in _SMALL_SHARDED]


def _pack(arrs):
    rows = []
    for a in arrs:
        flat = a.reshape(-1)
        rows.append(jnp.pad(flat, (0, -flat.shape[0] % 128)).reshape(-1, 128))
    buf = jnp.concatenate(rows, axis=0)
    return jnp.pad(buf, ((0, -buf.shape[0] % 8), (0, 0)))


def _unpack(buf, shapes):
    out, row = [], 0
    for shp in shapes:
        size = math.prod(shp)
        nrow = -(-size // 128)
        out.append(buf[row:row + nrow].reshape(-1)[:size].reshape(shp))
        row += nrow
    return out


def kernel(*args):
    d = dict(zip(_INPUTS, args, strict=True))
    xi, yi, ci = lax.axis_index("x"), lax.axis_index("y"), lax.axis_index("c")
    p = 2 * xi + yi
    x, target = d["x"][0], d["loss_target"][0]

    gath = _allgather_weights([d[n].astype(BF16) for n in _BIG] + [d[n] for n in _SMALL_SHARDED])
    unshard_cols = lambda a: a.transpose(0, 2, 1, 3).reshape(a.shape[0], a.shape[2], NSHARD * a.shape[3])
    big = dict(ffn1_w_gu=gath[0], ffn1_w_down=gath[1].reshape(DEPTH, DFF, D), w_in=unshard_cols(gath[2]),
               w_out=gath[3].reshape(DEPTH, D, D), ffn2_w_gu=gath[4], ffn2_w_down=gath[5].reshape(DEPTH, DFF, D))
    small = {n: d[n] for n in _SMALL_REPL}
    small["lru_conv_w"] = unshard_cols(gath[6])
    small["conv_w"] = unshard_cols(gath[7])

    empty = lambda *shape: lax.empty(shape, F32)
    bufs = dict(ffn1_w_gu=empty(DEPTH, NSHARD, D, FH), ffn2_w_gu=empty(DEPTH, NSHARD, D, FH),
                ffn1_w_down=empty(DEPTH, NSHARD, DFF // NSHARD, D), ffn2_w_down=empty(DEPTH, NSHARD, DFF // NSHARD, D),
                w_in_flat=empty(DEPTH, 1, D, P_IN), w_out=empty(DEPTH, NSHARD, D // NSHARD, D))
    lcols, grad_x, bufs, sgrads = _local_step(x, target, big, small, bufs)

    dw_in = bufs["w_in_flat"].reshape(DEPTH, D, NSHARD, P_IN // NSHARD).transpose(0, 2, 1, 3)
    gs = [bufs["ffn1_w_gu"], bufs["ffn1_w_down"], dw_in, bufs["w_out"], bufs["ffn2_w_gu"], bufs["ffn2_w_down"]]
    rs = _pair_exchange(gs)
    c_idx = jnp.reshape(ci, (1,)).astype(jnp.int32)
    cp_idx = jnp.stack([ci, p]).astype(jnp.int32)
    ts = [_pair_sum(g, r, c_idx) for g, r in zip(gs, rs)]
    rrs = _chip_exchange(ts)
    reds = [_chip_sum(g, r, rr, cp_idx) for g, r, rr in zip(gs, rs, rrs)]
    grads = dict(zip(_BIG, _pair_share(reds)))

    stacked = {n: jnp.stack([sgrads[l][n].reshape(d[n].shape[1:]) for l in range(DEPTH)]) for n in _SMALL_REPL}
    for n in _SMALL_SHARDED:
        stacked[n] = jnp.stack([sgrads[l][n] for l in range(DEPTH)])
    loss_part = jnp.pad((0.5 / D) * jnp.sum(lcols).reshape(1), (0, 127))
    order = _SMALL_REPL + _SMALL_SHARDED
    summed = _unpack(_allreduce_small(_pack([loss_part] + [stacked[n] for n in order])),
                     [(128,)] + [stacked[n].shape for n in order])
    loss = summed[0][0]
    for n, g in zip(order, summed[1:]):
        if n in _SMALL_SHARDED:
            g = lax.dynamic_slice_in_dim(g, p * (g.shape[2] // NSHARD), g.shape[2] // NSHARD, axis=2)
        grads[n] = g

    delta, new_m, new_v = {}, {}, {}
    for n in _BIG:
        delta[n], new_m[n], new_v[n] = _adamw(d[n], grads[n], d["m_" + n], d["v_" + n])
    shapes = [d[n].shape for n in order]
    packed = [_pack([src(n) for n in order])[None] for src in
              (lambda n: d[n], lambda n: grads[n], lambda n: d["m_" + n], lambda n: d["v_" + n])]
    for out, res in zip((delta, new_m, new_v), _adamw(*packed)):
        out.update(zip(order, _unpack(res[0], shapes)))

    return (loss, grad_x[None], *[grads[n] for n in _WEIGHTS], *[delta[n] for n in _WEIGHTS],
            *[new_m[n] for n in _WEIGHTS], *[new_v[n] for n in _WEIGHTS])
```

```python
import functools
import math

import jax
import jax.numpy as jnp
from jax import lax
from jax.experimental import pallas as pl
from jax.experimental.pallas import tpu as pltpu

F32 = jnp.float32
BF16 = jnp.bfloat16
SDS = jax.ShapeDtypeStruct

D = 1024
DFF = 2816
FH = DFF // 2
DEPTH = 2
W_A = 256
W_B = 512
W_C = 256
NQ = 8
HD = 64
BLK = 128
P_IN = 1792
LRU_K = 4
CONV_K = 31
LRU_C = 8.0
NORM_EPS = 1e-6
LN_EPS = 1e-5
NEG_BIG = -1e30
SCALE = 1.0 / math.sqrt(HD)

ADAM_LR = 0.001
ADAM_B1 = 0.9
ADAM_B2 = 0.999
ADAM_EPS = 1e-08
ADAM_WD = 0.01
ADAM_STEP = 10

VMEM_LIMIT = 56 * 1024 * 1024
NSHARD = 4
NDEV = 8

TN = (((0,), (0,)), ((), ()))
NT = (((1,), (1,)), ((), ()))


def _cp(*sem):
    return pltpu.CompilerParams(dimension_semantics=sem if sem else None, vmem_limit_bytes=VMEM_LIMIT)


def _rsq(x, eps):
    return lax.rsqrt(jnp.mean(x * x, axis=-1, keepdims=True) + eps)


def _rms_bwd_rows(x, g, dy):
    r = _rsq(x, NORM_EPS)
    xh = x * r
    dyg = dy * g
    dx = r * (dyg - xh * jnp.mean(dyg * xh, axis=-1, keepdims=True))
    return dx, dy * xh


def _sig(x):
    return jax.nn.sigmoid(x)


def _ffn_up(x, pre_g, wgu, l, tm):
    s = x.shape[0]

    def body(x_ref, g_ref, wg_ref, wu_ref, h_ref, go_ref, uo_ref, a_ref):
        @pl.when(pl.program_id(1) == 0)
        def _():
            xf = x_ref[...]
            h_ref[...] = (xf * _rsq(xf, NORM_EPS) * g_ref[...]).astype(BF16)

        h = h_ref[...]
        gg = jnp.dot(h, wg_ref[...], preferred_element_type=F32)
        uu = jnp.dot(h, wu_ref[...], preferred_element_type=F32)
        go_ref[...] = gg.astype(BF16)
        uo_ref[...] = uu.astype(BF16)
        a_ref[...] = (gg * _sig(gg) * uu).astype(BF16)

    wide = pl.BlockSpec((tm, FH), lambda i, j: (i, j))
    return pl.pallas_call(
        body, name="ffn_up", grid=(s // tm, 2),
        in_specs=[pl.BlockSpec((tm, D), lambda i, j: (i, 0)), pl.BlockSpec((1, D), lambda i, j: (0, 0)),
                  pl.BlockSpec((None, None, D, FH), lambda i, j: (l, j, 0, 0)),
                  pl.BlockSpec((None, None, D, FH), lambda i, j: (l, j + 2, 0, 0))],
        out_specs=[pl.BlockSpec((tm, D), lambda i, j: (i, 0)), wide, wide, wide],
        out_shape=[SDS((s, D), BF16), SDS((s, DFF), BF16), SDS((s, DFF), BF16), SDS((s, DFF), BF16)],
        compiler_params=_cp("parallel", "arbitrary"),
    )(x, pre_g, wgu, wgu)


def _mm_rms_res(a, w, l, x, g, c, tm, tk, name):
    s, k_dim = a.shape
    nk = k_dim // tk

    def body(a_ref, w_ref, x_ref, g_ref, z_ref, x1_ref):
        k = pl.program_id(1)
        p = jnp.dot(a_ref[...], w_ref[...], preferred_element_type=F32)

        @pl.when(k == 0)
        def _():
            z_ref[...] = p

        @pl.when(k > 0)
        def _():
            z_ref[...] += p

        @pl.when(k == nk - 1)
        def _():
            z = z_ref[...]
            x1_ref[...] = x_ref[...] + c * (z * _rsq(z, NORM_EPS) * g_ref[...])

    row = pl.BlockSpec((tm, D), lambda i, k: (i, 0))
    return pl.pallas_call(
        body, name=name, grid=(s // tm, nk),
        in_specs=[pl.BlockSpec((tm, tk), lambda i, k: (i, k)), pl.BlockSpec((None, tk, D), lambda i, k: (l, k, 0)),
                  row, pl.BlockSpec((1, D), lambda i, k: (0, 0))],
        out_specs=[row, row],
        out_shape=[SDS((s, D), F32), SDS((s, D), F32)],
        compiler_params=_cp("parallel", "arbitrary"),
    )(a, w, x, g)


def _rms_bwd(dy, z, g, c, tm, name):
    s = z.shape[0]

    def body(dy_ref, z_ref, g_ref, dz_ref, dg_ref):
        dz, dgr = _rms_bwd_rows(z_ref[...], g_ref[...], c * dy_ref[...])
        dz_ref[...] = dz.astype(BF16)
        part = jnp.sum(dgr, axis=0, keepdims=True)

        @pl.when(pl.program_id(0) == 0)
        def _():
            dg_ref[...] = part

        @pl.when(pl.program_id(0) > 0)
        def _():
            dg_ref[...] += part

    row = pl.BlockSpec((tm, D), lambda i: (i, 0))
    vec = pl.BlockSpec((1, D), lambda i: (0, 0))
    return pl.pallas_call(
        body, name=name, grid=(s // tm,), in_specs=[row, row, vec], out_specs=[row, vec],
        out_shape=[SDS((s, D), BF16), SDS((1, D), F32)], compiler_params=_cp("arbitrary"),
    )(dy, z, g)


def _ffn_bwd_mid(dz, wd, l, g, u, tm):
    s = dz.shape[0]

    def body(dz_ref, wd_ref, g_ref, u_ref, dg_ref, du_ref):
        da = lax.dot_general(dz_ref[...], wd_ref[...], NT, preferred_element_type=F32)
        gg = g_ref[...].astype(F32)
        uu = u_ref[...].astype(F32)
        sg = _sig(gg)
        dg_ref[...] = (da * uu * sg * (1.0 + gg * (1.0 - sg))).astype(BF16)
        du_ref[...] = (da * gg * sg).astype(BF16)

    wide = pl.BlockSpec((tm, FH), lambda i, j: (i, j))
    return pl.pallas_call(
        body, name="ffn_bwd_mid", grid=(s // tm, 2),
        in_specs=[pl.BlockSpec((tm, D), lambda i, j: (i, 0)), pl.BlockSpec((None, FH, D), lambda i, j: (l, j, 0)), wide, wide],
        out_specs=[wide, wide],
        out_shape=[SDS((s, DFF), BF16), SDS((s, DFF), BF16)],
        compiler_params=_cp("parallel", "arbitrary"),
    )(dz, wd, g, u)


def _ffn_bwd_dh(dg, du, wgu, l, x, pre_g, dx1, tm):
    s = x.shape[0]

    def body(dg_ref, du_ref, wg_ref, wu_ref, x_ref, g_ref, dx1_ref, dx_ref, dgp_ref):
        i, k = pl.program_id(0), pl.program_id(1)
        p = (lax.dot_general(dg_ref[...], wg_ref[...], NT, preferred_element_type=F32)
             + lax.dot_general(du_ref[...], wu_ref[...], NT, preferred_element_type=F32))

        @pl.when(k == 0)
        def _():
            dx_ref[...] = p

        @pl.when(k == 1)
        def _():
            dx, dgr = _rms_bwd_rows(x_ref[...], g_ref[...], dx_ref[...] + p)
            dx_ref[...] = dx1_ref[...] + dx
            part = jnp.sum(dgr, axis=0, keepdims=True)

            @pl.when(i == 0)
            def _():
                dgp_ref[...] = part

            @pl.when(i > 0)
            def _():
                dgp_ref[...] += part

    wide = pl.BlockSpec((tm, FH), lambda i, k: (i, k))
    row = pl.BlockSpec((tm, D), lambda i, k: (i, 0))
    vec = pl.BlockSpec((1, D), lambda i, k: (0, 0))
    return pl.pallas_call(
        body, name="ffn_bwd_dh", grid=(s // tm, 2),
        in_specs=[wide, wide, pl.BlockSpec((None, None, D, FH), lambda i, k: (l, k, 0, 0)),
                  pl.BlockSpec((None, None, D, FH), lambda i, k: (l, k + 2, 0, 0)), row, vec, row],
        out_specs=[row, vec],
        out_shape=[SDS((s, D), F32), SDS((1, D), F32)],
        compiler_params=_cp("arbitrary", "arbitrary"),
    )(dg, du, wgu, wgu, x, pre_g, dx1)


def _mm_tn_into(buf, a, b, l, joff, tka, tn, ts, name):
    s, ka = a.shape
    n = b.shape[1]

    def body(buf_ref, a_ref, b_ref, o_ref):
        p = lax.dot_general(a_ref[...], b_ref[...], TN, preferred_element_type=F32)

        @pl.when(pl.program_id(2) == 0)
        def _():
            o_ref[...] = p

        @pl.when(pl.program_id(2) > 0)
        def _():
            o_ref[...] += p

    return pl.pallas_call(
        body, name=name, grid=(ka // tka, n // tn, s // ts),
        in_specs=[pl.BlockSpec(memory_space=pl.ANY),
                  pl.BlockSpec((ts, tka), lambda ia, j, t: (t, ia)), pl.BlockSpec((ts, tn), lambda ia, j, t: (t, j))],
        out_specs=pl.BlockSpec((None, None, tka, tn), lambda ia, j, t: (l, joff + j, ia, 0)),
        out_shape=SDS(buf.shape, F32), input_output_aliases={0: 0},
        compiler_params=_cp("parallel", "parallel", "arbitrary"),
    )(buf, a, b)


def _proj(x, g, w_in, l, tm):
    s = x.shape[0]

    def body(x_ref, g_ref, w_ref, h_ref, p_ref):
        xf = x_ref[...]
        h = (xf * _rsq(xf, NORM_EPS) * g_ref[...]).astype(BF16)
        h_ref[...] = h
        p_ref[...] = jnp.dot(h, w_ref[...], preferred_element_type=F32)

    return pl.pallas_call(
        body, name="proj", grid=(s // tm,),
        in_specs=[pl.BlockSpec((tm, D), lambda i: (i, 0)), pl.BlockSpec((1, D), lambda i: (0, 0)),
                  pl.BlockSpec((None, D, P_IN), lambda i: (l, 0, 0))],
        out_specs=[pl.BlockSpec((tm, D), lambda i: (i, 0)), pl.BlockSpec((tm, P_IN), lambda i: (i, 0))],
        out_shape=[SDS((s, D), BF16), SDS((s, P_IN), F32)],
        compiler_params=_cp("parallel"),
    )(x, g, w_in)


def _mm_nt(a, w, l, tm, name):
    s, k_dim = a.shape
    n = w.shape[1]

    def body(a_ref, w_ref, o_ref):
        o_ref[...] = lax.dot_general(a_ref[...], w_ref[...], NT, preferred_element_type=F32)

    return pl.pallas_call(
        body, name=name, grid=(s // tm,),
        in_specs=[pl.BlockSpec((tm, k_dim), lambda i: (i, 0)), pl.BlockSpec((None, n, k_dim), lambda i: (l, 0, 0))],
        out_specs=pl.BlockSpec((tm, n), lambda i: (i, 0)),
        out_shape=SDS((s, n), F32), compiler_params=_cp("parallel"),
    )(a, w)


def _mm_nt_rmsbwd(dp, w_in, l, x, g, dx1, tm):
    s = x.shape[0]

    def body(dp_ref, w_ref, x_ref, g_ref, dx1_ref, dx_ref, dg_ref):
        dh = lax.dot_general(dp_ref[...], w_ref[...], NT, preferred_element_type=F32)
        dx, dgr = _rms_bwd_rows(x_ref[...], g_ref[...], dh)
        dx_ref[...] = dx1_ref[...] + dx
        part = jnp.sum(dgr, axis=0, keepdims=True)

        @pl.when(pl.program_id(0) == 0)
        def _():
            dg_ref[...] = part

        @pl.when(pl.program_id(0) > 0)
        def _():
            dg_ref[...] += part

    row = pl.BlockSpec((tm, D), lambda i: (i, 0))
    vec = pl.BlockSpec((1, D), lambda i: (0, 0))
    return pl.pallas_call(
        body, name="mix_bwd_dx", grid=(s // tm,),
        in_specs=[pl.BlockSpec((tm, P_IN), lambda i: (i, 0)), pl.BlockSpec((None, D, P_IN), lambda i: (l, 0, 0)), row, vec, row],
        out_specs=[row, vec], out_shape=[SDS((s, D), F32), SDS((1, D), F32)],
        compiler_params=_cp("arbitrary"),
    )(dp, w_in, x, g, dx1)


def _row_iota(shape):
    return lax.broadcasted_iota(jnp.int32, shape, 0)


def _lru_gates(xc, wa_ref, ba_ref, wx_ref, bx_ref, lam_ref):
    xb = xc.astype(BF16)
    r = _sig(jnp.dot(xb, wa_ref[...], preferred_element_type=F32) + ba_ref[...])
    ig = _sig(jnp.dot(xb, wx_ref[...], preferred_element_type=F32) + bx_ref[...])
    nl = -lam_ref[...]
    sp = jnp.maximum(nl, 0.0) + jnp.log(1.0 + jnp.exp(-jnp.abs(nl)))
    log_a = -LRU_C * r * sp
    a = jnp.exp(log_a)
    x2 = 2.0 * log_a
    series = x2 * (1.0 + x2 * (0.5 + x2 * (1.0 / 6.0 + x2 * (1.0 / 24.0 + x2 * (1.0 / 120.0)))))
    em1 = jnp.where(x2 > -0.05, series, jnp.exp(x2) - 1.0)
    mlt = jnp.sqrt(-em1)
    return r, ig, a, mlt, sp


def _conv_taps(src_ref, w_ref, k_taps, pad, tc):
    acc = None
    for j in range(k_taps):
        term = w_ref[j:j + 1, :] * src_ref[pl.ds(pad - (k_taps - 1) + j, tc), :]
        acc = term if acc is None else acc + term
    return acc


def _gelu_parts(x):
    c0 = math.sqrt(2.0 / math.pi)
    inner = c0 * (x + 0.044715 * x * x * x)
    t = jnp.tanh(inner)
    gl = 0.5 * x * (1.0 + t)
    dgl = 0.5 * (1.0 + t) + 0.5 * x * (1.0 - t * t) * c0 * (1.0 + 3.0 * 0.044715 * x * x)
    return gl, dgl


def _lru_fwd(proj, cw, cb, wa, ba, wx, bx, lam, gg, tc):
    s = proj.shape[0]
    pad = 8

    def body(xcur_ref, xprev_ref, gate_ref, cw_ref, cb_ref, wa_ref, ba_ref, wx_ref, bx_ref, lam_ref, gg_ref,
             yn_ref, h_ref, xs_ref, hc_ref):
        i = pl.program_id(0)

        @pl.when(i == 0)
        def _():
            hc_ref[...] = jnp.zeros_like(hc_ref)

        xs_ref[0:pad, :] = jnp.where(i > 0, xprev_ref[tc - pad:tc, :], 0.0)
        xs_ref[pad:pad + tc, :] = xcur_ref[...]
        xc = _conv_taps(xs_ref, cw_ref, LRU_K, pad, tc) + cb_ref[...]
        _, ig, a, mlt, _ = _lru_gates(xc, wa_ref, ba_ref, wx_ref, bx_ref, lam_ref)
        u = mlt * (ig * xc)
        row = _row_iota((tc, W_A))
        d = 1
        while d < tc:
            ok = row >= d
            a_sh = jnp.where(ok, pltpu.roll(a, d, axis=0), 1.0)
            u_sh = jnp.where(ok, pltpu.roll(u, d, axis=0), 0.0)
            u = a * u_sh + u
            a = a * a_sh
            d *= 2
        h = u + a * hc_ref[...]
        hc_ref[...] = jnp.sum(jnp.where(row == tc - 1, h, 0.0), axis=0, keepdims=True)
        h_ref[...] = h
        gl, _ = _gelu_parts(gate_ref[...])
        ya = gl * h
        yn_ref[...] = (ya * _rsq(ya, NORM_EPS) * gg_ref[...]).astype(BF16)

    blk = lambda c: pl.BlockSpec((tc, W_A), lambda i, c=c: (i, c))
    full = lambda a: pl.BlockSpec(a.shape, lambda i: (0,) * a.ndim)
    params = [cw, cb, wa, ba, wx, bx, lam, gg]
    return pl.pallas_call(
        body, name="lru_fwd", grid=(s // tc,),
        in_specs=[blk(0), pl.BlockSpec((tc, W_A), lambda i: (jnp.maximum(i - 1, 0), 0)), blk(1)] + [full(a) for a in params],
        out_specs=[pl.BlockSpec((tc, W_A), lambda i: (i, 0))] * 2,
        out_shape=[SDS((s, W_A), BF16), SDS((s, W_A), F32)],
        scratch_shapes=[pltpu.VMEM((tc + pad, W_A), F32), pltpu.VMEM((1, W_A), F32)],
        compiler_params=_cp("arbitrary"),
    )(proj, proj, proj, *params)


def _acc(ref, first, val):
    @pl.when(first)
    def _():
        ref[...] = val

    @pl.when(jnp.logical_not(first))
    def _():
        ref[...] += val


def _lru_bwd(dy, proj, h, cw, cb, wa, ba, wx, bx, lam, gg, tc):
    s = proj.shape[0]
    nc = s // tc
    pad = 8

    def body(dy_ref, xcur_ref, xprev_ref, gate_ref, h_ref, hprev_ref, cw_ref, cb_ref, wa_ref, ba_ref, wx_ref, bx_ref,
             lam_ref, gg_ref,
             dp_ref, dcw_ref, dcb_ref, dwa_ref, dba_ref, dwx_ref, dbx_ref, dlam_ref, dgg_ref,
             xs_ref, ds_ref, mu_ref, nx_ref):
        step = pl.program_id(0)
        i = nc - 1 - step
        first = step == 0

        @pl.when(first)
        def _():
            mu_ref[...] = jnp.zeros_like(mu_ref)
            nx_ref[...] = jnp.zeros_like(nx_ref)

        xs_ref[0:pad, :] = jnp.where(i > 0, xprev_ref[tc - pad:tc, :], 0.0)
        xs_ref[pad:pad + tc, :] = xcur_ref[...]
        xc = _conv_taps(xs_ref, cw_ref, LRU_K, pad, tc) + cb_ref[...]
        r, ig, a, mlt, sp = _lru_gates(xc, wa_ref, ba_ref, wx_ref, bx_ref, lam_ref)
        hh = h_ref[...]
        gate = gate_ref[...]
        gl, dgl = _gelu_parts(gate)
        ya = gl * hh
        dya, dggr = _rms_bwd_rows(ya, gg_ref[...], dy_ref[...])
        _acc(dgg_ref, first, jnp.sum(dggr, axis=0, keepdims=True))
        dp_ref[:, W_A:2 * W_A] = dya * hh * dgl
        dh = dya * gl

        row = _row_iota((tc, W_A))
        aa = a
        uu = a * dh
        d = 1
        while d < tc:
            ok = row < tc - d
            a_sh = jnp.where(ok, pltpu.roll(aa, tc - d, axis=0), 1.0)
            u_sh = jnp.where(ok, pltpu.roll(uu, tc - d, axis=0), 0.0)
            uu = uu + aa * u_sh
            aa = aa * a_sh
            d *= 2
        cin = mu_ref[...]
        mu = uu + aa * cin
        lam_t = dh + jnp.where(row == tc - 1, cin, pltpu.roll(mu, tc - 1, axis=0))
        mu_ref[...] = jnp.sum(jnp.where(row == 0, mu, 0.0), axis=0, keepdims=True)
        hm1 = jnp.where(row == 0, jnp.where(i > 0, pltpu.roll(hprev_ref[...], 1, axis=0), 0.0),
                        pltpu.roll(hh, 1, axis=0))
        da = lam_t * hm1
        du = lam_t
        dmlt = du * ig * xc
        dig = du * mlt * xc
        dxc = du * mlt * ig
        dlog_a = da * a - dmlt * (a * a / mlt)
        dr = dlog_a * (-LRU_C * sp)
        dsp = jnp.sum(dlog_a * (-LRU_C * r), axis=0, keepdims=True)
        _acc(dlam_ref, first, dsp * (-_sig(-lam_ref[...])))
        dga = dr * r * (1.0 - r)
        dgx = dig * ig * (1.0 - ig)
        _acc(dba_ref, first, jnp.sum(dga, axis=0, keepdims=True))
        _acc(dbx_ref, first, jnp.sum(dgx, axis=0, keepdims=True))
        xb = xc.astype(BF16)
        dgab = dga.astype(BF16)
        dgxb = dgx.astype(BF16)
        _acc(dwa_ref, first, lax.dot_general(xb, dgab, TN, preferred_element_type=F32))
        _acc(dwx_ref, first, lax.dot_general(xb, dgxb, TN, preferred_element_type=F32))
        dxc = (dxc + lax.dot_general(dgab, wa_ref[...], NT, preferred_element_type=F32)
               + lax.dot_general(dgxb, wx_ref[...], NT, preferred_element_type=F32))

        _acc(dcb_ref, first, jnp.sum(dxc, axis=0, keepdims=True))
        r8 = _row_iota((8, W_A))
        dcw = jnp.zeros((8, W_A), F32)
        for j in range(LRU_K):
            tap = jnp.sum(dxc * xs_ref[pl.ds(pad - (LRU_K - 1) + j, tc), :], axis=0, keepdims=True)
            dcw = dcw + jnp.where(r8 == j, tap, 0.0)
        _acc(dcw_ref, first, dcw)
        ds_ref[0:tc, :] = dxc
        ds_ref[tc:tc + pad, :] = nx_ref[...]
        dlx = None
        for j in range(LRU_K):
            term = cw_ref[j:j + 1, :] * ds_ref[pl.ds(LRU_K - 1 - j, tc), :]
            dlx = term if dlx is None else dlx + term
        dp_ref[:, 0:W_A] = dlx
        nx_ref[...] = dxc[0:pad, :]

    rev = lambda c: pl.BlockSpec((tc, W_A), lambda t, c=c: (nc - 1 - t, c))
    prev = lambda c: pl.BlockSpec((tc, W_A), lambda t, c=c: (jnp.maximum(nc - 2 - t, 0), c))
    full = lambda a: pl.BlockSpec(a.shape, lambda t: (0,) * a.ndim)
    params = [cw, cb, wa, ba, wx, bx, lam, gg]
    vec = SDS((1, W_A), F32)
    sq = SDS((W_A, W_A), F32)
    outs = [SDS((s, 2 * W_A), F32), SDS((8, W_A), F32), vec, sq, vec, sq, vec, vec, vec]
    return pl.pallas_call(
        body, name="lru_bwd", grid=(nc,),
        in_specs=[rev(0), rev(0), prev(0), rev(1), rev(0), prev(0)] + [full(a) for a in params],
        out_specs=[pl.BlockSpec((tc, 2 * W_A), lambda t: (nc - 1 - t, 0))]
        + [pl.BlockSpec(o.shape, lambda t: (0, 0)) for o in outs[1:]],
        out_shape=outs,
        scratch_shapes=[pltpu.VMEM((tc + pad, W_A), F32), pltpu.VMEM((tc + pad, W_A), F32),
                        pltpu.VMEM((1, W_A), F32), pltpu.VMEM((pad, W_A), F32)],
        compiler_params=_cp("arbitrary"),
    )(dy, proj, proj, proj, h, h, *params)


def _attn_stack(qa, qb, kvh):
    lane = lax.broadcasted_iota(jnp.int32, qa.shape, 1)
    keep = (lane >= HD) if kvh == 1 else (lane < HD)
    parts = []
    for tile in (qa, qb):
        for half in (0, 1):
            y = tile if half == kvh else pltpu.roll(tile, HD, axis=1)
            parts.append(jnp.where(keep, y, 0.0))
    return jnp.concatenate(parts, axis=0)


def _attn_unstack(o, kvh):
    lane = lax.broadcasted_iota(jnp.int32, (BLK, 2 * HD), 1)
    tiles = []
    for t in range(2):
        halves = []
        for half in (0, 1):
            blk = o[(2 * t + half) * BLK:(2 * t + half + 1) * BLK, :]
            halves.append(blk if half == kvh else pltpu.roll(blk, HD, axis=1))
        tiles.append(jnp.where(lane < HD, halves[0], halves[1]))
    return tiles


def _attn_mask(n):
    qi = lax.broadcasted_iota(jnp.int32, (BLK, 2 * BLK), 0)
    kj = lax.broadcasted_iota(jnp.int32, (BLK, 2 * BLK), 1)
    rel = BLK + qi - kj
    return (rel >= 0) & (rel < BLK) & ((n - 1) * BLK + kj >= 0)


def _attn_probs(qs, kw, mask, sink_ref, kvh):
    sc = lax.dot_general(qs.astype(BF16), kw, NT, preferred_element_type=F32) * SCALE
    ps, psinks = [], []
    for rr in range(4):
        sk = sink_ref[4 * kvh + rr:4 * kvh + rr + 1, 0:1]
        sh = jnp.where(mask, sc[rr * BLK:(rr + 1) * BLK, :], NEG_BIG)
        m = jnp.maximum(jnp.max(sh, axis=-1, keepdims=True), sk)
        e = jnp.exp(sh - m)
        es = jnp.exp(sk - m)
        z = jnp.sum(e, axis=-1, keepdims=True) + es
        ps.append(e / z)
        psinks.append(es / z)
    return ps, psinks


def _attn_fwd(proj, sinks8, gg):
    s = proj.shape[0]

    def body(q_ref, kc_ref, kp_ref, vc_ref, vp_ref, sink_ref, gg_ref, yn_ref, ob_ref):
        n = pl.program_id(0)
        mask = _attn_mask(n)
        kw = jnp.concatenate([kp_ref[...], kc_ref[...]], axis=0).astype(BF16)
        vw = jnp.concatenate([vp_ref[...], vc_ref[...]], axis=0).astype(BF16)
        for kvh in range(2):
            qa = q_ref[:, 256 * kvh:256 * kvh + 128]
            qb = q_ref[:, 256 * kvh + 128:256 * kvh + 256]
            ps, _ = _attn_probs(_attn_stack(qa, qb, kvh), kw, mask, sink_ref, kvh)
            o = jnp.dot(jnp.concatenate(ps, axis=0).astype(BF16), vw, preferred_element_type=F32)
            ta, tb = _attn_unstack(o, kvh)
            ob_ref[:, 256 * kvh:256 * kvh + 128] = ta
            ob_ref[:, 256 * kvh + 128:256 * kvh + 256] = tb
        ob = ob_ref[...]
        yn_ref[...] = (ob * _rsq(ob, NORM_EPS) * gg_ref[...]).astype(BF16)

    kv = lambda c, back: pl.BlockSpec((BLK, 128), lambda n, c=c, back=back: (jnp.maximum(n - back, 0), c))
    out = pl.BlockSpec((BLK, W_B), lambda n: (n, 0))
    return pl.pallas_call(
        body, name="attn_fwd", grid=(s // BLK,),
        in_specs=[pl.BlockSpec((BLK, W_B), lambda n: (n, 1)), kv(8, 0), kv(8, 1), kv(9, 0), kv(9, 1),
                  pl.BlockSpec((8, 128), lambda n: (0, 0)), pl.BlockSpec((1, W_B), lambda n: (0, 0))],
        out_specs=[out, out], out_shape=[SDS((s, W_B), BF16), SDS((s, W_B), F32)],
        compiler_params=_cp("parallel"),
    )(proj, proj, proj, proj, proj, sinks8, gg)


def _attn_bwd(dy, proj, ob, sinks8, gg):
    s = proj.shape[0]

    def body(dya_ref, dyb_ref, q_ref, kc_ref, kp_ref, vc_ref, vp_ref, ob_ref, sink_ref, gg_ref,
             dq_ref, dcur_ref, dprev_ref, dsink_ref, dgg_ref):
        n = pl.program_id(0)
        first = n == 0
        mask = _attn_mask(n)
        kw = jnp.concatenate([kp_ref[...], kc_ref[...]], axis=0).astype(BF16)
        vw = jnp.concatenate([vp_ref[...], vc_ref[...]], axis=0).astype(BF16)
        dyn = jnp.concatenate([dya_ref[...], dyb_ref[...]], axis=1)
        dob, dggr = _rms_bwd_rows(ob_ref[...], gg_ref[...], dyn)
        _acc(dgg_ref, first, jnp.sum(dggr, axis=0, keepdims=True))
        r8 = _row_iota((8, 128))
        dsk = jnp.zeros((8, 128), F32)
        dkw = jnp.zeros((2 * BLK, 128), F32)
        dvw = jnp.zeros((2 * BLK, 128), F32)
        for kvh in range(2):
            qs = _attn_stack(q_ref[:, 256 * kvh:256 * kvh + 128], q_ref[:, 256 * kvh + 128:256 * kvh + 256], kvh)
            ps, psinks = _attn_probs(qs, kw, mask, sink_ref, kvh)
            dos = _attn_stack(dob[:, 256 * kvh:256 * kvh + 128], dob[:, 256 * kvh + 128:256 * kvh + 256], kvh)
            dosb = dos.astype(BF16)
            dp = lax.dot_general(dosb, vw, NT, preferred_element_type=F32)
            dss = []
            for rr in range(4):
                dpr = dp[rr * BLK:(rr + 1) * BLK, :]
                dd = jnp.sum(ps[rr] * dpr, axis=-1, keepdims=True)
                dss.append(ps[rr] * (dpr - dd) * SCALE)
                tot = jnp.sum(-psinks[rr] * dd, axis=0, keepdims=True)
                dsk = dsk + jnp.where(r8 == 4 * kvh + rr, tot, 0.0)
            dsb = jnp.concatenate(dss, axis=0).astype(BF16)
            pb = jnp.concatenate(ps, axis=0).astype(BF16)
            dqs = jnp.dot(dsb, kw, preferred_element_type=F32)
            ta, tb = _attn_unstack(dqs, kvh)
            dq_ref[:, 256 * kvh:256 * kvh + 128] = ta
            dq_ref[:, 256 * kvh + 128:256 * kvh + 256] = tb
            dkw = dkw + lax.dot_general(dsb, qs.astype(BF16), TN, preferred_element_type=F32)
            dvw = dvw + lax.dot_general(pb, dosb, TN, preferred_element_type=F32)
        _acc(dsink_ref, first, dsk)
        dprev_ref[:, 0:128] = dkw[0:BLK, :]
        dprev_ref[:, 128:256] = dvw[0:BLK, :]
        dcur_ref[:, 0:128] = dkw[BLK:2 * BLK, :]
        dcur_ref[:, 128:256] = dvw[BLK:2 * BLK, :]

    kv = lambda c, back: pl.BlockSpec((BLK, 128), lambda n, c=c, back=back: (jnp.maximum(n - back, 0), c))
    wide = pl.BlockSpec((BLK, W_B), lambda n: (n, 0))
    half = pl.BlockSpec((BLK, 256), lambda n: (n, 0))
    return pl.pallas_call(
        body, name="attn_bwd", grid=(s // BLK,),
        in_specs=[pl.BlockSpec((BLK, 256), lambda n: (n, 1)), pl.BlockSpec((BLK, 256), lambda n: (n, 2)),
                  pl.BlockSpec((BLK, W_B), lambda n: (n, 1)), kv(8, 0), kv(8, 1), kv(9, 0), kv(9, 1), wide,
                  pl.BlockSpec((8, 128), lambda n: (0, 0)), pl.BlockSpec((1, W_B), lambda n: (0, 0))],
        out_specs=[wide, half, half, pl.BlockSpec((8, 128), lambda n: (0, 0)), pl.BlockSpec((1, W_B), lambda n: (0, 0))],
        out_shape=[SDS((s, W_B), F32), SDS((s, 256), F32), SDS((s, 256), F32), SDS((8, 128), F32), SDS((1, W_B), F32)],
        compiler_params=_cp("arbitrary"),
    )(dy, dy, proj, proj, proj, proj, proj, ob, sinks8, gg)


def _ln_parts(y1, eps=LN_EPS):
    mu = jnp.mean(y1, axis=-1, keepdims=True)
    xc = y1 - mu
    rstd = lax.rsqrt(jnp.mean(xc * xc, axis=-1, keepdims=True) + eps)
    return xc * rstd, rstd


def _conf_fwd(proj, cw, cb, lg, lb, gg, tc):
    s = proj.shape[0]
    pad = 32

    def body(ac_ref, gc_ref, ap_ref, gp_ref, cw_ref, cb_ref, lg_ref, lb_ref, gg_ref, yn_ref, y1_ref, ys_ref):
        i = pl.program_id(0)
        tail = ap_ref[tc - pad:tc, :] * _sig(gp_ref[tc - pad:tc, :])
        ys_ref[0:pad, :] = jnp.where(i > 0, tail, 0.0)
        ys_ref[pad:pad + tc, :] = ac_ref[...] * _sig(gc_ref[...])
        y1 = _conv_taps(ys_ref, cw_ref, CONV_K, pad, tc) + cb_ref[...]
        y1_ref[...] = y1
        xh, _ = _ln_parts(y1)
        yl = xh * lg_ref[...] + lb_ref[...]
        yc = yl * _sig(yl)
        yn_ref[...] = (yc * _rsq(yc, NORM_EPS) * gg_ref[...]).astype(BF16)

    cur = lambda c: pl.BlockSpec((tc, W_C), lambda i, c=c: (i, c))
    prev = lambda c: pl.BlockSpec((tc, W_C), lambda i, c=c: (jnp.maximum(i - 1, 0), c))
    full = lambda a: pl.BlockSpec(a.shape, lambda i: (0,) * a.ndim)
    params = [cw, cb, lg, lb, gg]
    out = pl.BlockSpec((tc, W_C), lambda i: (i, 0))
    return pl.pallas_call(
        body, name="conf_fwd", grid=(s // tc,),
        in_specs=[cur(5), cur(6), prev(5), prev(6)] + [full(a) for a in params],
        out_specs=[out, out], out_shape=[SDS((s, W_C), BF16), SDS((s, W_C), F32)],
        scratch_shapes=[pltpu.VMEM((tc + pad, W_C), F32)],
        compiler_params=_cp("parallel"),
    )(proj, proj, proj, proj, *params)


def _conf_bwd(dy, proj, y1, cw, cb, lg, lb, gg, tc):
    s = proj.shape[0]
    nc = s // tc
    pad = 32

    def body(dy_ref, ac_ref, gc_ref, ap_ref, gp_ref, y1_ref, cw_ref, cb_ref, lg_ref, lb_ref, gg_ref,
             dp_ref, dcw_ref, dcb_ref, dlg_ref, dlb_ref, dgg_ref, ys_ref, ds_ref, nx_ref):
        step = pl.program_id(0)
        i = nc - 1 - step
        first = step == 0

        @pl.when(first)
        def _():
            nx_ref[...] = jnp.zeros_like(nx_ref)

        a = ac_ref[...]
        sg = _sig(gc_ref[...])
        tail = ap_ref[tc - pad:tc, :] * _sig(gp_ref[tc - pad:tc, :])
        ys_ref[0:pad, :] = jnp.where(i > 0, tail, 0.0)
        ys_ref[pad:pad + tc, :] = a * sg
        xh, rstd = _ln_parts(y1_ref[...])
        yl = xh * lg_ref[...] + lb_ref[...]
        sl = _sig(yl)
        yc = yl * sl
        dyc, dggr = _rms_bwd_rows(yc, gg_ref[...], dy_ref[...])
        _acc(dgg_ref, first, jnp.sum(dggr, axis=0, keepdims=True))
        dyl = dyc * sl * (1.0 + yl * (1.0 - sl))
        _acc(dlg_ref, first, jnp.sum(dyl * xh, axis=0, keepdims=True))
        _acc(dlb_ref, first, jnp.sum(dyl, axis=0, keepdims=True))
        dxh = dyl * lg_ref[...]
        dy1 = rstd * (dxh - jnp.mean(dxh, axis=-1, keepdims=True) - xh * jnp.mean(dxh * xh, axis=-1, keepdims=True))
        _acc(dcb_ref, first, jnp.sum(dy1, axis=0, keepdims=True))
        r32 = _row_iota((32, W_C))
        dcw = jnp.zeros((32, W_C), F32)
        for j in range(CONV_K):
            tap = jnp.sum(dy1 * ys_ref[pl.ds(pad - (CONV_K - 1) + j, tc), :], axis=0, keepdims=True)
            dcw = dcw + jnp.where(r32 == j, tap, 0.0)
        _acc(dcw_ref, first, dcw)
        ds_ref[0:tc, :] = dy1
        ds_ref[tc:tc + pad, :] = nx_ref[...]
        dy0 = None
        for j in range(CONV_K):
            term = cw_ref[j:j + 1, :] * ds_ref[pl.ds(CONV_K - 1 - j, tc), :]
            dy0 = term if dy0 is None else dy0 + term
        dp_ref[:, 0:W_C] = dy0 * sg
        dp_ref[:, W_C:2 * W_C] = dy0 * a * sg * (1.0 - sg)
        nx_ref[...] = dy1[0:pad, :]

    rev = lambda c: pl.BlockSpec((tc, W_C), lambda t, c=c: (nc - 1 - t, c))
    prev = lambda c: pl.BlockSpec((tc, W_C), lambda t, c=c: (jnp.maximum(nc - 2 - t, 0), c))
    full = lambda a: pl.BlockSpec(a.shape, lambda t: (0,) * a.ndim)
    params = [cw, cb, lg, lb, gg]
    vec = SDS((1, W_C), F32)
    outs = [SDS((s, 2 * W_C), F32), SDS((32, W_C), F32), vec, vec, vec, vec]
    return pl.pallas_call(
        body, name="conf_bwd", grid=(nc,),
        in_specs=[rev(3), rev(5), rev(6), prev(5), prev(6), rev(0)] + [full(a) for a in params],
        out_specs=[pl.BlockSpec((tc, 2 * W_C), lambda t: (nc - 1 - t, 0))]
        + [pl.BlockSpec(o.shape, lambda t: (0, 0)) for o in outs[1:]],
        out_shape=outs,
        scratch_shapes=[pltpu.VMEM((tc + pad, W_C), F32), pltpu.VMEM((tc + pad, W_C), F32), pltpu.VMEM((pad, W_C), F32)],
        compiler_params=_cp("arbitrary"),
    )(dy, proj, proj, proj, proj, y1, *params)


def _assemble_dproj(dlru, dq, dcur, dprev, dconf):
    s = dq.shape[0]
    nb = s // BLK

    def body(dl_ref, dq_ref, dc_ref, dn_ref, df_ref, o_ref):
        n = pl.program_id(0)
        o_ref[:, 0:512] = dl_ref[...].astype(BF16)
        o_ref[:, 512:1024] = dq_ref[...].astype(BF16)
        o_ref[:, 1024:1280] = (dc_ref[...] + jnp.where(n < nb - 1, dn_ref[...], 0.0)).astype(BF16)
        o_ref[:, 1280:1792] = df_ref[...].astype(BF16)

    wide = pl.BlockSpec((BLK, 512), lambda n: (n, 0))
    return pl.pallas_call(
        body, name="assemble_dproj", grid=(nb,),
        in_specs=[wide, wide, pl.BlockSpec((BLK, 256), lambda n: (n, 0)),
                  pl.BlockSpec((BLK, 256), lambda n: (jnp.minimum(n + 1, nb - 1), 0)), wide],
        out_specs=pl.BlockSpec((BLK, P_IN), lambda n: (n, 0)), out_shape=SDS((s, P_IN), BF16),
        compiler_params=_cp("parallel"),
    )(dlru, dq, dcur, dprev, dconf)


def _loss_grad(y, t, tm):
    s = y.shape[0]

    def body(y_ref, t_ref, dy_ref, l_ref):
        err = y_ref[...] - t_ref[...]
        dy_ref[...] = err * (1.0 / D)
        _acc(l_ref, pl.program_id(0) == 0, jnp.sum(err * err, axis=0, keepdims=True))

    row = pl.BlockSpec((tm, D), lambda i: (i, 0))
    return pl.pallas_call(
        body, name="loss_grad", grid=(s // tm,), in_specs=[row, row],
        out_specs=[row, pl.BlockSpec((1, D), lambda i: (0, 0))],
        out_shape=[SDS((s, D), F32), SDS((1, D), F32)], compiler_params=_cp("arbitrary"),
    )(y, t)


def _block_diag(w):
    out = jnp.zeros((W_A, W_A), w.dtype)
    for h in range(4):
        out = lax.dynamic_update_slice(out, w[h], (64 * h, 64 * h))
    return out


def _diag_blocks(m):
    return jnp.stack([m[64 * h:64 * (h + 1), 64 * h:64 * (h + 1)] for h in range(4)])


def _layer_params(small, l):
    v = lambda name: small[name][l].reshape(1, -1)
    gg = small["group_g"][l]
    return dict(
        ffn1_pre=v("ffn1_pre_g"), ffn1_post=v("ffn1_post_g"), mix_pre=v("mix_pre_g"), mix_post=v("mix_post_g"),
        ffn2_pre=v("ffn2_pre_g"), ffn2_post=v("ffn2_post_g"),
        lru_cw=small["lru_conv_w"][l], lru_cb=v("lru_conv_b"),
        wa=_block_diag(small["lru_w_a"][l]).astype(BF16), ba=v("lru_b_a"),
        wx=_block_diag(small["lru_w_x"][l]).astype(BF16), bx=v("lru_b_x"), lam=v("lru_lambda"),
        sinks8=jnp.broadcast_to(small["attn_sinks"][l][:, None], (NQ, 128)),
        conv_w=small["conv_w"][l], conv_b=v("conv_b"), ln_g=v("conv_ln_g"), ln_b=v("conv_ln_b"),
        gg_a=gg[0:W_A].reshape(1, -1), gg_b=gg[W_A:W_A + W_B].reshape(1, -1), gg_c=gg[W_A + W_B:].reshape(1, -1),
    )


def _local_step(x, target, big, small, bufs):
    s = x.shape[0]
    tm = min(512, s)
    tc = min(512, s // 2)
    saved = []
    for l in range(DEPTH):
        p = _layer_params(small, l)
        sv = dict(p=p, x0=x)
        h1, g1, u1, a1 = _ffn_up(x, p["ffn1_pre"], big["ffn1_w_gu"], l, tm)
        z1, x = _mm_rms_res(a1, big["ffn1_w_down"], l, x, p["ffn1_post"], 0.5, tm, FH, "ffn_down")
        sv.update(h1=h1, g1=g1, u1=u1, a1=a1, z1=z1, x1=x)
        hn, proj = _proj(x, p["mix_pre"], big["w_in"], l, tm)
        yn_a, hl = _lru_fwd(proj, p["lru_cw"], p["lru_cb"], p["wa"], p["ba"], p["wx"], p["bx"], p["lam"], p["gg_a"], tc)
        yn_b, ob = _attn_fwd(proj, p["sinks8"], p["gg_b"])
        yn_c, y1 = _conf_fwd(proj, p["conv_w"], p["conv_b"], p["ln_g"], p["ln_b"], p["gg_c"], tc)
        ycat = jnp.concatenate([yn_a, yn_b, yn_c], axis=1)
        zo, x = _mm_rms_res(ycat, big["w_out"], l, x, p["mix_post"], 1.0, tm, D, "mix_out")
        sv.update(hn=hn, proj=proj, hl=hl, ob=ob, y1=y1, ycat=ycat, zo=zo, x2=x)
        h2, g2, u2, a2 = _ffn_up(x, p["ffn2_pre"], big["ffn2_w_gu"], l, tm)
        z2, x = _mm_rms_res(a2, big["ffn2_w_down"], l, x, p["ffn2_post"], 0.5, tm, FH, "ffn_down")
        sv.update(h2=h2, g2=g2, u2=u2, a2=a2, z2=z2)
        saved.append(sv)

    dx, lcols = _loss_grad(x, target, tm)
    sgrads = [None] * DEPTH

    def ffn_bwd(dx, l, which, xin, h, g, u, a, z, pre, post):
        dz, dpost = _rms_bwd(dx, z, post, 0.5, tm, "ffn_post_bwd")
        dg, du = _ffn_bwd_mid(dz, big[which + "_w_down"], l, g, u, tm)
        bufs[which + "_w_down"] = _mm_tn_into(bufs[which + "_w_down"], a, dz, l, 0, FH, D, tm, "dw_down")
        bufs[which + "_w_gu"] = _mm_tn_into(bufs[which + "_w_gu"], h, dg, l, 0, D, FH, tm, "dw_gate")
        bufs[which + "_w_gu"] = _mm_tn_into(bufs[which + "_w_gu"], h, du, l, 2, D, FH, tm, "dw_up")
        dxn, dpre = _ffn_bwd_dh(dg, du, big[which + "_w_gu"], l, xin, pre, dx, tm)
        return dxn, dpre, dpost

    for l in reversed(range(DEPTH)):
        sv = saved[l]
        p = sv["p"]
        gr = {}
        dx, gr["ffn2_pre_g"], gr["ffn2_post_g"] = ffn_bwd(dx, l, "ffn2", sv["x2"], sv["h2"], sv["g2"], sv["u2"], sv["a2"],
                                                          sv["z2"], p["ffn2_pre"], p["ffn2_post"])
        do, gr["mix_post_g"] = _rms_bwd(dx, sv["zo"], p["mix_post"], 1.0, tm, "mix_post_bwd")
        bufs["w_out"] = _mm_tn_into(bufs["w_out"], sv["ycat"], do, l, 0, D, D, tm, "dw_out")
        dy = _mm_nt(do, big["w_out"], l, tm, "mix_dy")
        proj = sv["proj"]
        (dlru, dcw, gr["lru_conv_b"], dwa, gr["lru_b_a"], dwx, gr["lru_b_x"], gr["lru_lambda"], dgg_a) = _lru_bwd(
            dy, proj, sv["hl"], p["lru_cw"], p["lru_cb"], p["wa"], p["ba"], p["wx"], p["bx"], p["lam"], p["gg_a"], tc)
        dq, dcur, dprev, dsk, dgg_b = _attn_bwd(dy, proj, sv["ob"], p["sinks8"], p["gg_b"])
        dconf, dconvw, gr["conv_b"], gr["conv_ln_g"], gr["conv_ln_b"], dgg_c = _conf_bwd(
            dy, proj, sv["y1"], p["conv_w"], p["conv_b"], p["ln_g"], p["ln_b"], p["gg_c"], tc)
        dproj = _assemble_dproj(dlru, dq, dcur, dprev, dconf)
        bufs["w_in"] = _mm_tn_into(bufs["w_in"], sv["hn"], dproj, l, 0, D, P_IN, tm, "dw_in")
        dx, gr["mix_pre_g"] = _mm_nt_rmsbwd(dproj, big["w_in"], l, sv["x1"], p["mix_pre"], dx, tm)
        gr["lru_conv_w"] = dcw[0:LRU_K]
        gr["lru_w_a"] = _diag_blocks(dwa)
        gr["lru_w_x"] = _diag_blocks(dwx)
        gr["attn_sinks"] = dsk[:, 0]
        gr["conv_w"] = dconvw[0:CONV_K]
        gr["group_g"] = jnp.concatenate([dgg_a, dgg_b, dgg_c], axis=1)
        dx, gr["ffn1_pre_g"], gr["ffn1_post_g"] = ffn_bwd(dx, l, "ffn1", sv["x0"], sv["h1"], sv["g1"], sv["u1"], sv["a1"],
                                                          sv["z1"], p["ffn1_pre"], p["ffn1_post"])
        sgrads[l] = gr
    return lcols, dx, bufs, sgrads


MESH = pl.DeviceIdType.MESH
ANY = pl.BlockSpec(memory_space=pl.ANY)


def _place():
    x, y, c = lax.axis_index("x"), lax.axis_index("y"), lax.axis_index("c")
    return x, y, c, [(1 - x, y), (x, 1 - y), (1 - x, 1 - y)]


def _place_shard(w, p_idx, dtype):
    depth, rows, cols = w.shape
    tr = _rows_per_block(rows, cols, 16) if rows % 16 == 0 else rows

    def body(p_ref, buf_ref, w_ref, o_ref):
        o_ref[...] = w_ref[...].astype(dtype)

    spec = pltpu.PrefetchScalarGridSpec(
        num_scalar_prefetch=1, grid=(depth, rows // tr),
        in_specs=[ANY, pl.BlockSpec((None, tr, cols), lambda l, i, pr: (l, i, 0))],
        out_specs=pl.BlockSpec((None, None, tr, cols), lambda l, i, pr: (l, pr[0], i, 0)))
    shape = (depth, NSHARD, rows, cols)
    return pl.pallas_call(body, name="place_shard", grid_spec=spec, out_shape=SDS(shape, dtype),
                          input_output_aliases={1: 0}, compiler_params=_cp("parallel", "parallel"),
                          )(p_idx, lax.empty(shape, dtype), w)


def _allgather_weights(bufs):
    n = len(bufs)

    def body(*refs):
        outs = refs[n:2 * n]
        send_sems, recv_sems = refs[2 * n:]
        x, y, c, chips = _place()
        p = 2 * x + y
        me, sibling = (x, y, c), (x, y, 1 - c)

        def rcopy(a, k, layer, q, to):
            blk = outs[a].at[layer, q]
            return pltpu.make_async_remote_copy(src_ref=blk, dst_ref=blk, send_sem=send_sems.at[a, k],
                                                recv_sem=recv_sems.at[a, k], device_id=to, device_id_type=MESH)

        first = [rcopy(a, j, c, p, (*chip, c)) for a in range(n) for j, chip in enumerate(chips)]
        for cp in first:
            cp.start()
        passed = []
        for a in range(n):
            for j, chip in enumerate(chips):
                q = 2 * chip[0] + chip[1]
                rcopy(a, j, c, q, me).wait_recv()
                passed.append(rcopy(a, 3 + j, c, q, sibling))
                passed[-1].start()
        for a in range(n):
            for j, chip in enumerate(chips):
                rcopy(a, 3 + j, 1 - c, 2 * chip[0] + chip[1], me).wait_recv()
        for cp in first + passed:
            cp.wait_send()

    return pl.pallas_call(
        body, name="allgather_weights", in_specs=[ANY] * n, out_specs=[ANY] * n,
        out_shape=[SDS(b.shape, b.dtype) for b in bufs], input_output_aliases={a: a for a in range(n)},
        scratch_shapes=[pltpu.SemaphoreType.DMA((n, 6)), pltpu.SemaphoreType.DMA((n, 6))],
    )(*bufs)


def _pair_exchange(gs):
    n = len(gs)

    def body(*refs):
        ins, outs = refs[:n], refs[n:2 * n]
        send_sems, recv_sems = refs[2 * n:]
        x, y, c, _ = _place()
        cps = [pltpu.make_async_remote_copy(src_ref=ins[a].at[1 - c], dst_ref=outs[a], send_sem=send_sems.at[a],
                                            recv_sem=recv_sems.at[a], device_id=(x, y, 1 - c), device_id_type=MESH)
               for a in range(n)]
        for cp in cps:
            cp.start()
        for cp in cps:
            cp.wait()

    return pl.pallas_call(
        body, name="grad_pair_exchange", in_specs=[ANY] * n, out_specs=[ANY] * n,
        out_shape=[SDS(g.shape[1:], g.dtype) for g in gs],
        scratch_shapes=[pltpu.SemaphoreType.DMA((n,)), pltpu.SemaphoreType.DMA((n,))],
    )(*gs)


def _chip_exchange(ts):
    n = len(ts)

    def body(*refs):
        ins, outs = refs[:n], refs[n:2 * n]
        send_sems, recv_sems = refs[2 * n:]
        x, y, c, chips = _place()
        cps = [pltpu.make_async_remote_copy(src_ref=ins[a].at[2 * chip[0] + chip[1]], dst_ref=outs[a].at[j],
                                            send_sem=send_sems.at[a, j], recv_sem=recv_sems.at[a, j],
                                            device_id=(*chip, c), device_id_type=MESH)
               for a in range(n) for j, chip in enumerate(chips)]
        for cp in cps:
            cp.start()
        for cp in cps:
            cp.wait()

    return pl.pallas_call(
        body, name="grad_chip_exchange", in_specs=[ANY] * n, out_specs=[ANY] * n,
        out_shape=[SDS((3,) + t.shape[1:], t.dtype) for t in ts],
        scratch_shapes=[pltpu.SemaphoreType.DMA((n, 3)), pltpu.SemaphoreType.DMA((n, 3))],
    )(*ts)


def _pair_share(fulls):
    n = len(fulls)

    def body(*refs):
        outs = refs[n:2 * n]
        send_sems, recv_sems = refs[2 * n:]
        x, y, c, _ = _place()
        cps = [pltpu.make_async_remote_copy(src_ref=outs[a].at[c], dst_ref=outs[a].at[c], send_sem=send_sems.at[a],
                                            recv_sem=recv_sems.at[a], device_id=(x, y, 1 - c), device_id_type=MESH)
               for a in range(n)]
        for cp in cps:
            cp.start()
        for a in range(n):
            cps[a].wait_send()
            pltpu.make_async_remote_copy(src_ref=outs[a].at[1 - c], dst_ref=outs[a].at[1 - c], send_sem=send_sems.at[a],
                                         recv_sem=recv_sems.at[a], device_id=(x, y, c), device_id_type=MESH).wait_recv()

    return pl.pallas_call(
        body, name="grad_pair_share", in_specs=[ANY] * n, out_specs=[ANY] * n,
        out_shape=[SDS(f.shape, f.dtype) for f in fulls], input_output_aliases={a: a for a in range(n)},
        scratch_shapes=[pltpu.SemaphoreType.DMA((n,)), pltpu.SemaphoreType.DMA((n,))],
    )(*fulls)


def _allreduce_small(buf):
    rows = buf.shape[0]

    def body(in_ref, out_ref, gather_ref, send_sems, recv_sems):
        x, y, c, _ = _place()
        me = 4 * x + 2 * y + c
        gather_ref[me] = in_ref[...]
        cps, slots = [], []
        for m in range(1, NDEV):
            px = 1 - x if m & 4 else x
            py = 1 - y if m & 2 else y
            pc = 1 - c if m & 1 else c
            cps.append(pltpu.make_async_remote_copy(src_ref=in_ref, dst_ref=gather_ref.at[me], send_sem=send_sems.at[m - 1],
                                                    recv_sem=recv_sems.at[m - 1], device_id=(px, py, pc), device_id_type=MESH))
            slots.append(4 * px + 2 * py + pc)
        for cp in cps:
            cp.start()
        for m in range(1, NDEV):
            pltpu.make_async_remote_copy(src_ref=in_ref, dst_ref=gather_ref.at[slots[m - 1]], send_sem=send_sems.at[m - 1],
                                         recv_sem=recv_sems.at[m - 1], device_id=(x, y, c), device_id_type=MESH).wait_recv()
        for cp in cps:
            cp.wait_send()
        total = gather_ref[0]
        for dev in range(1, NDEV):
            total = total + gather_ref[dev]
        out_ref[...] = total

    vm = pl.BlockSpec(memory_space=pltpu.VMEM)
    return pl.pallas_call(
        body, name="allreduce_small", in_specs=[vm], out_specs=vm, out_shape=SDS(buf.shape, F32),
        scratch_shapes=[pltpu.VMEM((NDEV, rows, 128), F32), pltpu.SemaphoreType.DMA((NDEV - 1,)),
                        pltpu.SemaphoreType.DMA((NDEV - 1,))],
        compiler_params=pltpu.CompilerParams(vmem_limit_bytes=VMEM_LIMIT),
    )(buf)


BLOCK_ELEMS = 256 * 1024


def _rows_per_block(rows, cols, mult):
    best = None
    for tr in range(mult, rows + 1, mult):
        if rows % tr == 0 and tr * cols <= BLOCK_ELEMS:
            best = tr
    assert best is not None, (rows, cols)
    return best


def _pair_sum(g, r, c_idx):
    _, nq, rows, cols = g.shape
    tr = _rows_per_block(rows, cols, 16)

    def body(c_ref, g_ref, r_ref, t_ref):
        t_ref[...] = (g_ref[...] + r_ref[...]).astype(BF16)

    blk = pl.BlockSpec((None, tr, cols), lambda q, i, cr: (q, i, 0))
    spec = pltpu.PrefetchScalarGridSpec(
        num_scalar_prefetch=1, grid=(nq, rows // tr),
        in_specs=[pl.BlockSpec((None, None, tr, cols), lambda q, i, cr: (cr[0], q, i, 0)), blk], out_specs=blk)
    return pl.pallas_call(body, name="grad_pair_sum", grid_spec=spec, out_shape=SDS((nq, rows, cols), BF16),
                          compiler_params=_cp("parallel", "parallel"))(c_idx, g, r)


def _chip_sum(g, r, rr, cp_idx):
    depth, _, rows, cols = g.shape
    tr = _rows_per_block(rows, cols, 16)

    def body(cp_ref, buf_ref, g_ref, r_ref, rr_ref, o_ref):
        o_ref[...] = ((g_ref[...] + r_ref[...]) + rr_ref[0].astype(F32) + rr_ref[1].astype(F32) + rr_ref[2].astype(F32))

    spec = pltpu.PrefetchScalarGridSpec(
        num_scalar_prefetch=1, grid=(rows // tr,),
        in_specs=[ANY, pl.BlockSpec((None, None, tr, cols), lambda i, cp: (cp[0], cp[1], i, 0)),
                  pl.BlockSpec((None, tr, cols), lambda i, cp: (cp[1], i, 0)),
                  pl.BlockSpec((3, tr, cols), lambda i, cp: (0, i, 0))],
        out_specs=pl.BlockSpec((None, tr, cols), lambda i, cp: (cp[0], i, 0)))
    shape = (depth, rows, cols)
    return pl.pallas_call(body, name="grad_chip_sum", grid_spec=spec, out_shape=SDS(shape, F32),
                          input_output_aliases={1: 0}, compiler_params=_cp("parallel"),
                          )(cp_idx, lax.empty(shape, F32), g, r, rr)


def _adamw(w, g, m, v):
    nb, rows, cols = w.shape
    tr = _rows_per_block(rows, cols, 8)

    def body(w_ref, g_ref, m_ref, v_ref, d_ref, mo_ref, vo_ref):
        gg = g_ref[...]
        mn = ADAM_B1 * m_ref[...] + (1.0 - ADAM_B1) * gg
        vn = ADAM_B2 * v_ref[...] + (1.0 - ADAM_B2) * (gg * gg)
        m_hat = mn / (1.0 - ADAM_B1 ** ADAM_STEP)
        v_hat = vn / (1.0 - ADAM_B2 ** ADAM_STEP)
        d_ref[...] = -ADAM_LR * (m_hat / (jnp.sqrt(v_hat) + ADAM_EPS) + ADAM_WD * w_ref[...])
        mo_ref[...] = mn
        vo_ref[...] = vn

    blk = pl.BlockSpec((None, tr, cols), lambda b, i: (b, i, 0))
    return pl.pallas_call(body, name="adamw", grid=(nb, rows // tr), in_specs=[blk] * 4, out_specs=[blk] * 3,
                          out_shape=[SDS(w.shape, F32)] * 3, compiler_params=_cp("parallel", "parallel"))(w, g, m, v)


_WEIGHTS = ["ffn1_pre_g", "ffn1_w_gu", "ffn1_w_down", "ffn1_post_g", "mix_pre_g", "w_in", "lru_conv_w", "lru_conv_b",
            "lru_w_a", "lru_b_a", "lru_w_x", "lru_b_x", "lru_lambda", "attn_sinks", "conv_w", "conv_b", "conv_ln_g",
            "conv_ln_b", "group_g", "w_out", "mix_post_g", "ffn2_pre_g", "ffn2_w_gu", "ffn2_w_down", "ffn2_post_g"]
_INPUTS = ["x"] + _WEIGHTS + ["loss_target"] + ["m_" + n for n in _WEIGHTS] + ["v_" + n for n in _WEIGHTS]
_BIG = ["ffn1_w_gu", "ffn1_w_down", "w_in", "w_out", "ffn2_w_gu", "ffn2_w_down"]
_SMALL_SHARDED = ["lru_conv_w", "conv_w"]
_SMALL_REPL = [n for n in _WEIGHTS if n not in _BIG and n not in _SMALL_SHARDED]


PACK_TILE = 8 * 128


def _pack(arrs):
    parts = []
    for a in arrs:
        flat = a.reshape(-1)
        parts.append(jnp.pad(flat, (0, -flat.shape[0] % PACK_TILE)).reshape(-1, 128))
    return jnp.concatenate(parts, axis=0)


def _unpack(buf, shapes):
    out, row = [], 0
    for shp in shapes:
        size = math.prod(shp)
        nrow = -(-size // PACK_TILE) * 8
        out.append(buf[row:row + nrow].reshape(-1)[:size].reshape(shp))
        row += nrow
    return out


def kernel(*args):
    d = dict(zip(_INPUTS, args, strict=True))
    xi, yi, ci = lax.axis_index("x"), lax.axis_index("y"), lax.axis_index("c")
    p = 2 * xi + yi
    x, target = d["x"][0], d["loss_target"][0]

    c_idx = jnp.reshape(ci, (1,)).astype(jnp.int32)
    p_idx = jnp.reshape(p, (1,)).astype(jnp.int32)
    cp_idx = jnp.stack([ci, p]).astype(jnp.int32)

    gath = _allgather_weights([_place_shard(d[n], p_idx, BF16) for n in _BIG]
                              + [_place_shard(d[n], p_idx, F32) for n in _SMALL_SHARDED])
    unshard_cols = lambda a: a.transpose(0, 2, 1, 3).reshape(a.shape[0], a.shape[2], NSHARD * a.shape[3])
    big = dict(ffn1_w_gu=gath[0], ffn1_w_down=gath[1].reshape(DEPTH, DFF, D), w_in=unshard_cols(gath[2]),
               w_out=gath[3].reshape(DEPTH, D, D), ffn2_w_gu=gath[4], ffn2_w_down=gath[5].reshape(DEPTH, DFF, D))
    small = {n: d[n] for n in _SMALL_REPL}
    small["lru_conv_w"] = unshard_cols(gath[6])
    small["conv_w"] = unshard_cols(gath[7])

    empty = lambda *shape: lax.empty(shape, F32)
    bufs = dict(ffn1_w_gu=empty(DEPTH, NSHARD, D, FH), ffn2_w_gu=empty(DEPTH, NSHARD, D, FH),
                ffn1_w_down=empty(DEPTH, 1, DFF, D), ffn2_w_down=empty(DEPTH, 1, DFF, D),
                w_in=empty(DEPTH, 1, D, P_IN), w_out=empty(DEPTH, 1, D, D))
    lcols, grad_x, bufs, sgrads = _local_step(x, target, big, small, bufs)

    by_rows = lambda a: a.reshape(DEPTH, NSHARD, a.shape[2] // NSHARD, a.shape[3])
    dw_in = bufs["w_in"].reshape(DEPTH, D, NSHARD, P_IN // NSHARD).transpose(0, 2, 1, 3)
    gs = [bufs["ffn1_w_gu"], by_rows(bufs["ffn1_w_down"]), dw_in, by_rows(bufs["w_out"]), bufs["ffn2_w_gu"],
          by_rows(bufs["ffn2_w_down"])]
    rs = _pair_exchange(gs)
    ts = [_pair_sum(g, r, c_idx) for g, r in zip(gs, rs)]
    rrs = _chip_exchange(ts)
    reds = [_chip_sum(g, r, rr, cp_idx) for g, r, rr in zip(gs, rs, rrs)]
    grads = dict(zip(_BIG, _pair_share(reds)))

    stacked = {n: jnp.stack([sgrads[l][n].reshape(d[n].shape[1:]) for l in range(DEPTH)]) for n in _SMALL_REPL}
    for n in _SMALL_SHARDED:
        stacked[n] = jnp.stack([sgrads[l][n] for l in range(DEPTH)])
    loss_part = jnp.pad((0.5 / D) * jnp.sum(lcols).reshape(1), (0, 127))
    order = _SMALL_REPL + _SMALL_SHARDED
    summed = _unpack(_allreduce_small(_pack([loss_part] + [stacked[n] for n in order])),
                     [(128,)] + [stacked[n].shape for n in order])
    loss = summed[0][0]
    for n, g in zip(order, summed[1:]):
        if n in _SMALL_SHARDED:
            g = lax.dynamic_slice_in_dim(g, p * (g.shape[2] // NSHARD), g.shape[2] // NSHARD, axis=2)
        grads[n] = g

    delta, new_m, new_v = {}, {}, {}
    for n in _BIG:
        delta[n], new_m[n], new_v[n] = _adamw(d[n], grads[n], d["m_" + n], d["v_" + n])
    shapes = [d[n].shape for n in order]
    packed = [_pack([src(n) for n in order])[None] for src in
              (lambda n: d[n], lambda n: grads[n], lambda n: d["m_" + n], lambda n: d["v_" + n])]
    for out, res in zip((delta, new_m, new_v), _adamw(*packed)):
        out.update(zip(order, _unpack(res[0], shapes)))

    return (loss, grad_x[None], *[grads[n] for n in _WEIGHTS], *[delta[n] for n in _WEIGHTS],
            *[new_m[n] for n in _WEIGHTS], *[new_v[n] for n in _WEIGHTS])
```

```python
import functools
import math

import jax
import jax.numpy as jnp
from jax import lax
from jax.experimental import pallas as pl
from jax.experimental.pallas import tpu as pltpu

F32 = jnp.float32
BF16 = jnp.bfloat16
SDS = jax.ShapeDtypeStruct

D = 1024
DFF = 2816
FH = DFF // 2
DEPTH = 2
W_A = 256
W_B = 512
W_C = 256
NQ = 8
HD = 64
BLK = 128
P_IN = 1792
LRU_K = 4
CONV_K = 31
LRU_C = 8.0
NORM_EPS = 1e-6
LN_EPS = 1e-5
NEG_BIG = -1e30
SCALE = 1.0 / math.sqrt(HD)

ADAM_LR = 0.001
ADAM_B1 = 0.9
ADAM_B2 = 0.999
ADAM_EPS = 1e-08
ADAM_WD = 0.01
ADAM_STEP = 10

VMEM_LIMIT = 56 * 1024 * 1024
NSHARD = 4
NDEV = 8

TN = (((0,), (0,)), ((), ()))
NT = (((1,), (1,)), ((), ()))

MESH = pl.DeviceIdType.MESH
ANY = pl.BlockSpec(memory_space=pl.ANY)


def _cp(*sem):
    return pltpu.CompilerParams(dimension_semantics=sem if sem else None, vmem_limit_bytes=VMEM_LIMIT)


def _rsq(x, eps):
    return lax.rsqrt(jnp.mean(x * x, axis=-1, keepdims=True) + eps)


def _rms_bwd_rows(x, g, dy):
    r = _rsq(x, NORM_EPS)
    xh = x * r
    dyg = dy * g
    dx = r * (dyg - xh * jnp.mean(dyg * xh, axis=-1, keepdims=True))
    return dx, dy * xh


def _sig(x):
    return jax.nn.sigmoid(x)


def _ffn_up(x, pre_g, wgu, l, tm, dep=None):
    s = x.shape[0]
    deps = [] if dep is None else [dep]

    def body(x_ref, g_ref, wg_ref, wu_ref, *rest):
        h_ref, go_ref, uo_ref, a_ref = rest[len(deps):]

        @pl.when(pl.program_id(1) == 0)
        def _():
            xf = x_ref[...]
            h_ref[...] = (xf * _rsq(xf, NORM_EPS) * g_ref[...]).astype(BF16)

        h = h_ref[...]
        gg = jnp.dot(h, wg_ref[...], preferred_element_type=F32)
        uu = jnp.dot(h, wu_ref[...], preferred_element_type=F32)
        go_ref[...] = gg.astype(BF16)
        uo_ref[...] = uu.astype(BF16)
        a_ref[...] = (gg * _sig(gg) * uu).astype(BF16)

    wide = pl.BlockSpec((tm, FH), lambda i, j: (i, j))
    return pl.pallas_call(
        body, name="ffn_up", grid=(s // tm, 2),
        in_specs=[pl.BlockSpec((tm, D), lambda i, j: (i, 0)), pl.BlockSpec((1, D), lambda i, j: (0, 0)),
                  pl.BlockSpec((None, None, D, FH), lambda i, j: (l, j, 0, 0)),
                  pl.BlockSpec((None, None, D, FH), lambda i, j: (l, j + 2, 0, 0))] + [ANY] * len(deps),
        out_specs=[pl.BlockSpec((tm, D), lambda i, j: (i, 0)), wide, wide, wide],
        out_shape=[SDS((s, D), BF16), SDS((s, DFF), BF16), SDS((s, DFF), BF16), SDS((s, DFF), BF16)],
        compiler_params=_cp("parallel", "arbitrary"),
    )(x, pre_g, wgu, wgu, *deps)


def _mm_rms_res(a, w, l, x, g, c, tm, tk, name):
    s, k_dim = a.shape
    nk = k_dim // tk

    def body(a_ref, w_ref, x_ref, g_ref, z_ref, x1_ref):
        k = pl.program_id(1)
        p = jnp.dot(a_ref[...], w_ref[...], preferred_element_type=F32)

        @pl.when(k == 0)
        def _():
            z_ref[...] = p

        @pl.when(k > 0)
        def _():
            z_ref[...] += p

        @pl.when(k == nk - 1)
        def _():
            z = z_ref[...]
            x1_ref[...] = x_ref[...] + c * (z * _rsq(z, NORM_EPS) * g_ref[...])

    row = pl.BlockSpec((tm, D), lambda i, k: (i, 0))
    return pl.pallas_call(
        body, name=name, grid=(s // tm, nk),
        in_specs=[pl.BlockSpec((tm, tk), lambda i, k: (i, k)), pl.BlockSpec((None, tk, D), lambda i, k: (l, k, 0)),
                  row, pl.BlockSpec((1, D), lambda i, k: (0, 0))],
        out_specs=[row, row],
        out_shape=[SDS((s, D), F32), SDS((s, D), F32)],
        compiler_params=_cp("parallel", "arbitrary"),
    )(a, w, x, g)


def _rms_bwd(dy, z, g, c, tm, name, dep=None):
    s = z.shape[0]
    deps = [] if dep is None else [dep]

    def body(dy_ref, z_ref, g_ref, *rest):
        dz_ref, dg_ref = rest[len(deps):]
        dz, dgr = _rms_bwd_rows(z_ref[...], g_ref[...], c * dy_ref[...])
        dz_ref[...] = dz.astype(BF16)
        part = jnp.sum(dgr, axis=0, keepdims=True)

        @pl.when(pl.program_id(0) == 0)
        def _():
            dg_ref[...] = part

        @pl.when(pl.program_id(0) > 0)
        def _():
            dg_ref[...] += part

    row = pl.BlockSpec((tm, D), lambda i: (i, 0))
    vec = pl.BlockSpec((1, D), lambda i: (0, 0))
    return pl.pallas_call(
        body, name=name, grid=(s // tm,), in_specs=[row, row, vec] + [ANY] * len(deps), out_specs=[row, vec],
        out_shape=[SDS((s, D), BF16), SDS((1, D), F32)], compiler_params=_cp("arbitrary"),
    )(dy, z, g, *deps)


def _ffn_bwd_mid(dz, wd, l, g, u, tm):
    s = dz.shape[0]

    def body(dz_ref, wd_ref, g_ref, u_ref, dg_ref, du_ref):
        da = lax.dot_general(dz_ref[...], wd_ref[...], NT, preferred_element_type=F32)
        gg = g_ref[...].astype(F32)
        uu = u_ref[...].astype(F32)
        sg = _sig(gg)
        dg_ref[...] = (da * uu * sg * (1.0 + gg * (1.0 - sg))).astype(BF16)
        du_ref[...] = (da * gg * sg).astype(BF16)

    wide = pl.BlockSpec((tm, FH), lambda i, j: (i, j))
    return pl.pallas_call(
        body, name="ffn_bwd_mid", grid=(s // tm, 2),
        in_specs=[pl.BlockSpec((tm, D), lambda i, j: (i, 0)), pl.BlockSpec((None, FH, D), lambda i, j: (l, j, 0)), wide, wide],
        out_specs=[wide, wide],
        out_shape=[SDS((s, DFF), BF16), SDS((s, DFF), BF16)],
        compiler_params=_cp("parallel", "arbitrary"),
    )(dz, wd, g, u)


def _ffn_bwd_dh(dg, du, wgu, l, x, pre_g, dx1, tm):
    s = x.shape[0]

    def body(dg_ref, du_ref, wg_ref, wu_ref, x_ref, g_ref, dx1_ref, dx_ref, dgp_ref):
        i, k = pl.program_id(0), pl.program_id(1)
        p = (lax.dot_general(dg_ref[...], wg_ref[...], NT, preferred_element_type=F32)
             + lax.dot_general(du_ref[...], wu_ref[...], NT, preferred_element_type=F32))

        @pl.when(k == 0)
        def _():
            dx_ref[...] = p

        @pl.when(k == 1)
        def _():
            dx, dgr = _rms_bwd_rows(x_ref[...], g_ref[...], dx_ref[...] + p)
            dx_ref[...] = dx1_ref[...] + dx
            part = jnp.sum(dgr, axis=0, keepdims=True)

            @pl.when(i == 0)
            def _():
                dgp_ref[...] = part

            @pl.when(i > 0)
            def _():
                dgp_ref[...] += part

    wide = pl.BlockSpec((tm, FH), lambda i, k: (i, k))
    row = pl.BlockSpec((tm, D), lambda i, k: (i, 0))
    vec = pl.BlockSpec((1, D), lambda i, k: (0, 0))
    return pl.pallas_call(
        body, name="ffn_bwd_dh", grid=(s // tm, 2),
        in_specs=[wide, wide, pl.BlockSpec((None, None, D, FH), lambda i, k: (l, k, 0, 0)),
                  pl.BlockSpec((None, None, D, FH), lambda i, k: (l, k + 2, 0, 0)), row, vec, row],
        out_specs=[row, vec],
        out_shape=[SDS((s, D), F32), SDS((1, D), F32)],
        compiler_params=_cp("arbitrary", "arbitrary"),
    )(dg, du, wgu, wgu, x, pre_g, dx1)


def _mm_tn_into(buf, a, b, l, joff, tka, tn, ts, name):
    s, ka = a.shape
    n = b.shape[1]

    def body(buf_ref, a_ref, b_ref, o_ref):
        p = lax.dot_general(a_ref[...], b_ref[...], TN, preferred_element_type=F32)

        @pl.when(pl.program_id(2) == 0)
        def _():
            o_ref[...] = p

        @pl.when(pl.program_id(2) > 0)
        def _():
            o_ref[...] += p

    return pl.pallas_call(
        body, name=name, grid=(ka // tka, n // tn, s // ts),
        in_specs=[pl.BlockSpec(memory_space=pl.ANY),
                  pl.BlockSpec((ts, tka), lambda ia, j, t: (t, ia)), pl.BlockSpec((ts, tn), lambda ia, j, t: (t, j))],
        out_specs=pl.BlockSpec((None, None, tka, tn), lambda ia, j, t: (l, joff + j, ia, 0)),
        out_shape=SDS(buf.shape, F32), input_output_aliases={0: 0},
        compiler_params=_cp("parallel", "parallel", "arbitrary"),
    )(buf, a, b)


def _proj(x, g, w_in, l, tm):
    s = x.shape[0]

    def body(x_ref, g_ref, w_ref, h_ref, p_ref):
        xf = x_ref[...]
        h = (xf * _rsq(xf, NORM_EPS) * g_ref[...]).astype(BF16)
        h_ref[...] = h
        p_ref[...] = jnp.dot(h, w_ref[...], preferred_element_type=F32)

    return pl.pallas_call(
        body, name="proj", grid=(s // tm,),
        in_specs=[pl.BlockSpec((tm, D), lambda i: (i, 0)), pl.BlockSpec((1, D), lambda i: (0, 0)),
                  pl.BlockSpec((None, D, P_IN), lambda i: (l, 0, 0))],
        out_specs=[pl.BlockSpec((tm, D), lambda i: (i, 0)), pl.BlockSpec((tm, P_IN), lambda i: (i, 0))],
        out_shape=[SDS((s, D), BF16), SDS((s, P_IN), F32)],
        compiler_params=_cp("parallel"),
    )(x, g, w_in)


def _mm_nt(a, w, l, tm, name):
    s, k_dim = a.shape
    n = w.shape[1]

    def body(a_ref, w_ref, o_ref):
        o_ref[...] = lax.dot_general(a_ref[...], w_ref[...], NT, preferred_element_type=F32)

    return pl.pallas_call(
        body, name=name, grid=(s // tm,),
        in_specs=[pl.BlockSpec((tm, k_dim), lambda i: (i, 0)), pl.BlockSpec((None, n, k_dim), lambda i: (l, 0, 0))],
        out_specs=pl.BlockSpec((tm, n), lambda i: (i, 0)),
        out_shape=SDS((s, n), F32), compiler_params=_cp("parallel"),
    )(a, w)


def _mm_nt_rmsbwd(dp, w_in, l, x, g, dx1, tm):
    s = x.shape[0]

    def body(dp_ref, w_ref, x_ref, g_ref, dx1_ref, dx_ref, dg_ref):
        dh = lax.dot_general(dp_ref[...], w_ref[...], NT, preferred_element_type=F32)
        dx, dgr = _rms_bwd_rows(x_ref[...], g_ref[...], dh)
        dx_ref[...] = dx1_ref[...] + dx
        part = jnp.sum(dgr, axis=0, keepdims=True)

        @pl.when(pl.program_id(0) == 0)
        def _():
            dg_ref[...] = part

        @pl.when(pl.program_id(0) > 0)
        def _():
            dg_ref[...] += part

    row = pl.BlockSpec((tm, D), lambda i: (i, 0))
    vec = pl.BlockSpec((1, D), lambda i: (0, 0))
    return pl.pallas_call(
        body, name="mix_bwd_dx", grid=(s // tm,),
        in_specs=[pl.BlockSpec((tm, P_IN), lambda i: (i, 0)), pl.BlockSpec((None, D, P_IN), lambda i: (l, 0, 0)), row, vec, row],
        out_specs=[row, vec], out_shape=[SDS((s, D), F32), SDS((1, D), F32)],
        compiler_params=_cp("arbitrary"),
    )(dp, w_in, x, g, dx1)


def _row_iota(shape):
    return lax.broadcasted_iota(jnp.int32, shape, 0)


def _lru_gates(xc, wa_ref, ba_ref, wx_ref, bx_ref, lam_ref):
    xb = xc.astype(BF16)
    r = _sig(jnp.dot(xb, wa_ref[...], preferred_element_type=F32) + ba_ref[...])
    ig = _sig(jnp.dot(xb, wx_ref[...], preferred_element_type=F32) + bx_ref[...])
    nl = -lam_ref[...]
    sp = jnp.maximum(nl, 0.0) + jnp.log(1.0 + jnp.exp(-jnp.abs(nl)))
    log_a = -LRU_C * r * sp
    a = jnp.exp(log_a)
    x2 = 2.0 * log_a
    series = x2 * (1.0 + x2 * (0.5 + x2 * (1.0 / 6.0 + x2 * (1.0 / 24.0 + x2 * (1.0 / 120.0)))))
    em1 = jnp.where(x2 > -0.05, series, jnp.exp(x2) - 1.0)
    mlt = jnp.sqrt(-em1)
    return r, ig, a, mlt, sp


def _conv_taps(src_ref, w_ref, k_taps, pad, tc):
    acc = None
    for j in range(k_taps):
        term = w_ref[j:j + 1, :] * src_ref[pl.ds(pad - (k_taps - 1) + j, tc), :]
        acc = term if acc is None else acc + term
    return acc


def _gelu_parts(x):
    c0 = math.sqrt(2.0 / math.pi)
    inner = c0 * (x + 0.044715 * x * x * x)
    t = jnp.tanh(inner)
    gl = 0.5 * x * (1.0 + t)
    dgl = 0.5 * (1.0 + t) + 0.5 * x * (1.0 - t * t) * c0 * (1.0 + 3.0 * 0.044715 * x * x)
    return gl, dgl


def _lru_fwd(proj, cw, cb, wa, ba, wx, bx, lam, gg, tc):
    s = proj.shape[0]
    pad = 8

    def body(xcur_ref, xprev_ref, gate_ref, cw_ref, cb_ref, wa_ref, ba_ref, wx_ref, bx_ref, lam_ref, gg_ref,
             yn_ref, h_ref, xs_ref, hc_ref):
        i = pl.program_id(0)

        @pl.when(i == 0)
        def _():
            hc_ref[...] = jnp.zeros_like(hc_ref)

        xs_ref[0:pad, :] = jnp.where(i > 0, xprev_ref[tc - pad:tc, :], 0.0)
        xs_ref[pad:pad + tc, :] = xcur_ref[...]
        xc = _conv_taps(xs_ref, cw_ref, LRU_K, pad, tc) + cb_ref[...]
        _, ig, a, mlt, _ = _lru_gates(xc, wa_ref, ba_ref, wx_ref, bx_ref, lam_ref)
        u = mlt * (ig * xc)
        row = _row_iota((tc, W_A))
        d = 1
        while d < tc:
            ok = row >= d
            a_sh = jnp.where(ok, pltpu.roll(a, d, axis=0), 1.0)
            u_sh = jnp.where(ok, pltpu.roll(u, d, axis=0), 0.0)
            u = a * u_sh + u
            a = a * a_sh
            d *= 2
        h = u + a * hc_ref[...]
        hc_ref[...] = jnp.sum(jnp.where(row == tc - 1, h, 0.0), axis=0, keepdims=True)
        h_ref[...] = h
        gl, _ = _gelu_parts(gate_ref[...])
        ya = gl * h
        yn_ref[...] = (ya * _rsq(ya, NORM_EPS) * gg_ref[...]).astype(BF16)

    blk = lambda c: pl.BlockSpec((tc, W_A), lambda i, c=c: (i, c))
    full = lambda a: pl.BlockSpec(a.shape, lambda i: (0,) * a.ndim)
    params = [cw, cb, wa, ba, wx, bx, lam, gg]
    return pl.pallas_call(
        body, name="lru_fwd", grid=(s // tc,),
        in_specs=[blk(0), pl.BlockSpec((tc, W_A), lambda i: (jnp.maximum(i - 1, 0), 0)), blk(1)] + [full(a) for a in params],
        out_specs=[pl.BlockSpec((tc, W_A), lambda i: (i, 0))] * 2,
        out_shape=[SDS((s, W_A), BF16), SDS((s, W_A), F32)],
        scratch_shapes=[pltpu.VMEM((tc + pad, W_A), F32), pltpu.VMEM((1, W_A), F32)],
        compiler_params=_cp("arbitrary"),
    )(proj, proj, proj, *params)


def _acc(ref, first, val):
    @pl.when(first)
    def _():
        ref[...] = val

    @pl.when(jnp.logical_not(first))
    def _():
        ref[...] += val


def _lru_bwd(dy, proj, h, cw, cb, wa, ba, wx, bx, lam, gg, tc):
    s = proj.shape[0]
    nc = s // tc
    pad = 8

    def body(dy_ref, xcur_ref, xprev_ref, gate_ref, h_ref, hprev_ref, cw_ref, cb_ref, wa_ref, ba_ref, wx_ref, bx_ref,
             lam_ref, gg_ref,
             dp_ref, dcw_ref, dcb_ref, dwa_ref, dba_ref, dwx_ref, dbx_ref, dlam_ref, dgg_ref,
             xs_ref, ds_ref, mu_ref, nx_ref):
        step = pl.program_id(0)
        i = nc - 1 - step
        first = step == 0

        @pl.when(first)
        def _():
            mu_ref[...] = jnp.zeros_like(mu_ref)
            nx_ref[...] = jnp.zeros_like(nx_ref)

        xs_ref[0:pad, :] = jnp.where(i > 0, xprev_ref[tc - pad:tc, :], 0.0)
        xs_ref[pad:pad + tc, :] = xcur_ref[...]
        xc = _conv_taps(xs_ref, cw_ref, LRU_K, pad, tc) + cb_ref[...]
        r, ig, a, mlt, sp = _lru_gates(xc, wa_ref, ba_ref, wx_ref, bx_ref, lam_ref)
        hh = h_ref[...]
        gate = gate_ref[...]
        gl, dgl = _gelu_parts(gate)
        ya = gl * hh
        dya, dggr = _rms_bwd_rows(ya, gg_ref[...], dy_ref[...])
        _acc(dgg_ref, first, jnp.sum(dggr, axis=0, keepdims=True))
        dp_ref[:, W_A:2 * W_A] = dya * hh * dgl
        dh = dya * gl

        row = _row_iota((tc, W_A))
        aa = a
        uu = a * dh
        d = 1
        while d < tc:
            ok = row < tc - d
            a_sh = jnp.where(ok, pltpu.roll(aa, tc - d, axis=0), 1.0)
            u_sh = jnp.where(ok, pltpu.roll(uu, tc - d, axis=0), 0.0)
            uu = uu + aa * u_sh
            aa = aa * a_sh
            d *= 2
        cin = mu_ref[...]
        mu = uu + aa * cin
        lam_t = dh + jnp.where(row == tc - 1, cin, pltpu.roll(mu, tc - 1, axis=0))
        mu_ref[...] = jnp.sum(jnp.where(row == 0, mu, 0.0), axis=0, keepdims=True)
        hm1 = jnp.where(row == 0, jnp.where(i > 0, pltpu.roll(hprev_ref[...], 1, axis=0), 0.0),
                        pltpu.roll(hh, 1, axis=0))
        da = lam_t * hm1
        du = lam_t
        dmlt = du * ig * xc
        dig = du * mlt * xc
        dxc = du * mlt * ig
        dlog_a = da * a - dmlt * (a * a / mlt)
        dr = dlog_a * (-LRU_C * sp)
        dsp = jnp.sum(dlog_a * (-LRU_C * r), axis=0, keepdims=True)
        _acc(dlam_ref, first, dsp * (-_sig(-lam_ref[...])))
        dga = dr * r * (1.0 - r)
        dgx = dig * ig * (1.0 - ig)
        _acc(dba_ref, first, jnp.sum(dga, axis=0, keepdims=True))
        _acc(dbx_ref, first, jnp.sum(dgx, axis=0, keepdims=True))
        xb = xc.astype(BF16)
        dgab = dga.astype(BF16)
        dgxb = dgx.astype(BF16)
        _acc(dwa_ref, first, lax.dot_general(xb, dgab, TN, preferred_element_type=F32))
        _acc(dwx_ref, first, lax.dot_general(xb, dgxb, TN, preferred_element_type=F32))
        dxc = (dxc + lax.dot_general(dgab, wa_ref[...], NT, preferred_element_type=F32)
               + lax.dot_general(dgxb, wx_ref[...], NT, preferred_element_type=F32))

        _acc(dcb_ref, first, jnp.sum(dxc, axis=0, keepdims=True))
        r8 = _row_iota((8, W_A))
        dcw = jnp.zeros((8, W_A), F32)
        for j in range(LRU_K):
            tap = jnp.sum(dxc * xs_ref[pl.ds(pad - (LRU_K - 1) + j, tc), :], axis=0, keepdims=True)
            dcw = dcw + jnp.where(r8 == j, tap, 0.0)
        _acc(dcw_ref, first, dcw)
        ds_ref[0:tc, :] = dxc
        ds_ref[tc:tc + pad, :] = nx_ref[...]
        dlx = None
        for j in range(LRU_K):
            term = cw_ref[j:j + 1, :] * ds_ref[pl.ds(LRU_K - 1 - j, tc), :]
            dlx = term if dlx is None else dlx + term
        dp_ref[:, 0:W_A] = dlx
        nx_ref[...] = dxc[0:pad, :]

    rev = lambda c: pl.BlockSpec((tc, W_A), lambda t, c=c: (nc - 1 - t, c))
    prev = lambda c: pl.BlockSpec((tc, W_A), lambda t, c=c: (jnp.maximum(nc - 2 - t, 0), c))
    full = lambda a: pl.BlockSpec(a.shape, lambda t: (0,) * a.ndim)
    params = [cw, cb, wa, ba, wx, bx, lam, gg]
    vec = SDS((1, W_A), F32)
    sq = SDS((W_A, W_A), F32)
    outs = [SDS((s, 2 * W_A), F32), SDS((8, W_A), F32), vec, sq, vec, sq, vec, vec, vec]
    return pl.pallas_call(
        body, name="lru_bwd", grid=(nc,),
        in_specs=[rev(0), rev(0), prev(0), rev(1), rev(0), prev(0)] + [full(a) for a in params],
        out_specs=[pl.BlockSpec((tc, 2 * W_A), lambda t: (nc - 1 - t, 0))]
        + [pl.BlockSpec(o.shape, lambda t: (0, 0)) for o in outs[1:]],
        out_shape=outs,
        scratch_shapes=[pltpu.VMEM((tc + pad, W_A), F32), pltpu.VMEM((tc + pad, W_A), F32),
                        pltpu.VMEM((1, W_A), F32), pltpu.VMEM((pad, W_A), F32)],
        compiler_params=_cp("arbitrary"),
    )(dy, proj, proj, proj, h, h, *params)


def _attn_stack(qa, qb, kvh):
    lane = lax.broadcasted_iota(jnp.int32, qa.shape, 1)
    keep = (lane >= HD) if kvh == 1 else (lane < HD)
    parts = []
    for tile in (qa, qb):
        for half in (0, 1):
            y = tile if half == kvh else pltpu.roll(tile, HD, axis=1)
            parts.append(jnp.where(keep, y, 0.0))
    return jnp.concatenate(parts, axis=0)


def _attn_unstack(o, kvh):
    lane = lax.broadcasted_iota(jnp.int32, (BLK, 2 * HD), 1)
    tiles = []
    for t in range(2):
        halves = []
        for half in (0, 1):
            blk = o[(2 * t + half) * BLK:(2 * t + half + 1) * BLK, :]
            halves.append(blk if half == kvh else pltpu.roll(blk, HD, axis=1))
        tiles.append(jnp.where(lane < HD, halves[0], halves[1]))
    return tiles


def _attn_mask(n):
    qi = lax.broadcasted_iota(jnp.int32, (BLK, 2 * BLK), 0)
    kj = lax.broadcasted_iota(jnp.int32, (BLK, 2 * BLK), 1)
    rel = BLK + qi - kj
    return (rel >= 0) & (rel < BLK) & ((n - 1) * BLK + kj >= 0)


def _attn_probs(qs, kw, mask, sink_ref, kvh):
    sc = lax.dot_general(qs.astype(BF16), kw, NT, preferred_element_type=F32) * SCALE
    ps, psinks = [], []
    for rr in range(4):
        sk = sink_ref[4 * kvh + rr:4 * kvh + rr + 1, 0:1]
        sh = jnp.where(mask, sc[rr * BLK:(rr + 1) * BLK, :], NEG_BIG)
        m = jnp.maximum(jnp.max(sh, axis=-1, keepdims=True), sk)
        e = jnp.exp(sh - m)
        es = jnp.exp(sk - m)
        z = jnp.sum(e, axis=-1, keepdims=True) + es
        ps.append(e / z)
        psinks.append(es / z)
    return ps, psinks


def _attn_fwd(proj, sinks8, gg):
    s = proj.shape[0]

    def body(q_ref, kc_ref, kp_ref, vc_ref, vp_ref, sink_ref, gg_ref, yn_ref, ob_ref):
        n = pl.program_id(0)
        mask = _attn_mask(n)
        kw = jnp.concatenate([kp_ref[...], kc_ref[...]], axis=0).astype(BF16)
        vw = jnp.concatenate([vp_ref[...], vc_ref[...]], axis=0).astype(BF16)
        for kvh in range(2):
            qa = q_ref[:, 256 * kvh:256 * kvh + 128]
            qb = q_ref[:, 256 * kvh + 128:256 * kvh + 256]
            ps, _ = _attn_probs(_attn_stack(qa, qb, kvh), kw, mask, sink_ref, kvh)
            o = jnp.dot(jnp.concatenate(ps, axis=0).astype(BF16), vw, preferred_element_type=F32)
            ta, tb = _attn_unstack(o, kvh)
            ob_ref[:, 256 * kvh:256 * kvh + 128] = ta
            ob_ref[:, 256 * kvh + 128:256 * kvh + 256] = tb
        ob = ob_ref[...]
        yn_ref[...] = (ob * _rsq(ob, NORM_EPS) * gg_ref[...]).astype(BF16)

    kv = lambda c, back: pl.BlockSpec((BLK, 128), lambda n, c=c, back=back: (jnp.maximum(n - back, 0), c))
    out = pl.BlockSpec((BLK, W_B), lambda n: (n, 0))
    return pl.pallas_call(
        body, name="attn_fwd", grid=(s // BLK,),
        in_specs=[pl.BlockSpec((BLK, W_B), lambda n: (n, 1)), kv(8, 0), kv(8, 1), kv(9, 0), kv(9, 1),
                  pl.BlockSpec((8, 128), lambda n: (0, 0)), pl.BlockSpec((1, W_B), lambda n: (0, 0))],
        out_specs=[out, out], out_shape=[SDS((s, W_B), BF16), SDS((s, W_B), F32)],
        compiler_params=_cp("parallel"),
    )(proj, proj, proj, proj, proj, sinks8, gg)


def _attn_bwd(dy, proj, ob, sinks8, gg):
    s = proj.shape[0]

    def body(dya_ref, dyb_ref, q_ref, kc_ref, kp_ref, vc_ref, vp_ref, ob_ref, sink_ref, gg_ref,
             dq_ref, dcur_ref, dprev_ref, dsink_ref, dgg_ref):
        n = pl.program_id(0)
        first = n == 0
        mask = _attn_mask(n)
        kw = jnp.concatenate([kp_ref[...], kc_ref[...]], axis=0).astype(BF16)
        vw = jnp.concatenate([vp_ref[...], vc_ref[...]], axis=0).astype(BF16)
        dyn = jnp.concatenate([dya_ref[...], dyb_ref[...]], axis=1)
        dob, dggr = _rms_bwd_rows(ob_ref[...], gg_ref[...], dyn)
        _acc(dgg_ref, first, jnp.sum(dggr, axis=0, keepdims=True))
        r8 = _row_iota((8, 128))
        dsk = jnp.zeros((8, 128), F32)
        dkw = jnp.zeros((2 * BLK, 128), F32)
        dvw = jnp.zeros((2 * BLK, 128), F32)
        for kvh in range(2):
            qs = _attn_stack(q_ref[:, 256 * kvh:256 * kvh + 128], q_ref[:, 256 * kvh + 128:256 * kvh + 256], kvh)
            ps, psinks = _attn_probs(qs, kw, mask, sink_ref, kvh)
            dos = _attn_stack(dob[:, 256 * kvh:256 * kvh + 128], dob[:, 256 * kvh + 128:256 * kvh + 256], kvh)
            dosb = dos.astype(BF16)
            dp = lax.dot_general(dosb, vw, NT, preferred_element_type=F32)
            dss = []
            for rr in range(4):
                dpr = dp[rr * BLK:(rr + 1) * BLK, :]
                dd = jnp.sum(ps[rr] * dpr, axis=-1, keepdims=True)
                dss.append(ps[rr] * (dpr - dd) * SCALE)
                tot = jnp.sum(-psinks[rr] * dd, axis=0, keepdims=True)
                dsk = dsk + jnp.where(r8 == 4 * kvh + rr, tot, 0.0)
            dsb = jnp.concatenate(dss, axis=0).astype(BF16)
            pb = jnp.concatenate(ps, axis=0).astype(BF16)
            dqs = jnp.dot(dsb, kw, preferred_element_type=F32)
            ta, tb = _attn_unstack(dqs, kvh)
            dq_ref[:, 256 * kvh:256 * kvh + 128] = ta
            dq_ref[:, 256 * kvh + 128:256 * kvh + 256] = tb
            dkw = dkw + lax.dot_general(dsb, qs.astype(BF16), TN, preferred_element_type=F32)
            dvw = dvw + lax.dot_general(pb, dosb, TN, preferred_element_type=F32)
        _acc(dsink_ref, first, dsk)
        dprev_ref[:, 0:128] = dkw[0:BLK, :]
        dprev_ref[:, 128:256] = dvw[0:BLK, :]
        dcur_ref[:, 0:128] = dkw[BLK:2 * BLK, :]
        dcur_ref[:, 128:256] = dvw[BLK:2 * BLK, :]

    kv = lambda c, back: pl.BlockSpec((BLK, 128), lambda n, c=c, back=back: (jnp.maximum(n - back, 0), c))
    wide = pl.BlockSpec((BLK, W_B), lambda n: (n, 0))
    half = pl.BlockSpec((BLK, 256), lambda n: (n, 0))
    return pl.pallas_call(
        body, name="attn_bwd", grid=(s // BLK,),
        in_specs=[pl.BlockSpec((BLK, 256), lambda n: (n, 1)), pl.BlockSpec((BLK, 256), lambda n: (n, 2)),
                  pl.BlockSpec((BLK, W_B), lambda n: (n, 1)), kv(8, 0), kv(8, 1), kv(9, 0), kv(9, 1), wide,
                  pl.BlockSpec((8, 128), lambda n: (0, 0)), pl.BlockSpec((1, W_B), lambda n: (0, 0))],
        out_specs=[wide, half, half, pl.BlockSpec((8, 128), lambda n: (0, 0)), pl.BlockSpec((1, W_B), lambda n: (0, 0))],
        out_shape=[SDS((s, W_B), F32), SDS((s, 256), F32), SDS((s, 256), F32), SDS((8, 128), F32), SDS((1, W_B), F32)],
        compiler_params=_cp("arbitrary"),
    )(dy, dy, proj, proj, proj, proj, proj, ob, sinks8, gg)


def _ln_parts(y1, eps=LN_EPS):
    mu = jnp.mean(y1, axis=-1, keepdims=True)
    xc = y1 - mu
    rstd = lax.rsqrt(jnp.mean(xc * xc, axis=-1, keepdims=True) + eps)
    return xc * rstd, rstd


def _conf_fwd(proj, cw, cb, lg, lb, gg, tc):
    s = proj.shape[0]
    pad = 32

    def body(ac_ref, gc_ref, ap_ref, gp_ref, cw_ref, cb_ref, lg_ref, lb_ref, gg_ref, yn_ref, y1_ref, ys_ref):
        i = pl.program_id(0)
        tail = ap_ref[tc - pad:tc, :] * _sig(gp_ref[tc - pad:tc, :])
        ys_ref[0:pad, :] = jnp.where(i > 0, tail, 0.0)
        ys_ref[pad:pad + tc, :] = ac_ref[...] * _sig(gc_ref[...])
        y1 = _conv_taps(ys_ref, cw_ref, CONV_K, pad, tc) + cb_ref[...]
        y1_ref[...] = y1
        xh, _ = _ln_parts(y1)
        yl = xh * lg_ref[...] + lb_ref[...]
        yc = yl * _sig(yl)
        yn_ref[...] = (yc * _rsq(yc, NORM_EPS) * gg_ref[...]).astype(BF16)

    cur = lambda c: pl.BlockSpec((tc, W_C), lambda i, c=c: (i, c))
    prev = lambda c: pl.BlockSpec((tc, W_C), lambda i, c=c: (jnp.maximum(i - 1, 0), c))
    full = lambda a: pl.BlockSpec(a.shape, lambda i: (0,) * a.ndim)
    params = [cw, cb, lg, lb, gg]
    out = pl.BlockSpec((tc, W_C), lambda i: (i, 0))
    return pl.pallas_call(
        body, name="conf_fwd", grid=(s // tc,),
        in_specs=[cur(5), cur(6), prev(5), prev(6)] + [full(a) for a in params],
        out_specs=[out, out], out_shape=[SDS((s, W_C), BF16), SDS((s, W_C), F32)],
        scratch_shapes=[pltpu.VMEM((tc + pad, W_C), F32)],
        compiler_params=_cp("parallel"),
    )(proj, proj, proj, proj, *params)


def _conf_bwd(dy, proj, y1, cw, cb, lg, lb, gg, tc):
    s = proj.shape[0]
    nc = s // tc
    pad = 32

    def body(dy_ref, ac_ref, gc_ref, ap_ref, gp_ref, y1_ref, cw_ref, cb_ref, lg_ref, lb_ref, gg_ref,
             dp_ref, dcw_ref, dcb_ref, dlg_ref, dlb_ref, dgg_ref, ys_ref, ds_ref, nx_ref):
        step = pl.program_id(0)
        i = nc - 1 - step
        first = step == 0

        @pl.when(first)
        def _():
            nx_ref[...] = jnp.zeros_like(nx_ref)

        a = ac_ref[...]
        sg = _sig(gc_ref[...])
        tail = ap_ref[tc - pad:tc, :] * _sig(gp_ref[tc - pad:tc, :])
        ys_ref[0:pad, :] = jnp.where(i > 0, tail, 0.0)
        ys_ref[pad:pad + tc, :] = a * sg
        xh, rstd = _ln_parts(y1_ref[...])
        yl = xh * lg_ref[...] + lb_ref[...]
        sl = _sig(yl)
        yc = yl * sl
        dyc, dggr = _rms_bwd_rows(yc, gg_ref[...], dy_ref[...])
        _acc(dgg_ref, first, jnp.sum(dggr, axis=0, keepdims=True))
        dyl = dyc * sl * (1.0 + yl * (1.0 - sl))
        _acc(dlg_ref, first, jnp.sum(dyl * xh, axis=0, keepdims=True))
        _acc(dlb_ref, first, jnp.sum(dyl, axis=0, keepdims=True))
        dxh = dyl * lg_ref[...]
        dy1 = rstd * (dxh - jnp.mean(dxh, axis=-1, keepdims=True) - xh * jnp.mean(dxh * xh, axis=-1, keepdims=True))
        _acc(dcb_ref, first, jnp.sum(dy1, axis=0, keepdims=True))
        r32 = _row_iota((32, W_C))
        dcw = jnp.zeros((32, W_C), F32)
        for j in range(CONV_K):
            tap = jnp.sum(dy1 * ys_ref[pl.ds(pad - (CONV_K - 1) + j, tc), :], axis=0, keepdims=True)
            dcw = dcw + jnp.where(r32 == j, tap, 0.0)
        _acc(dcw_ref, first, dcw)
        ds_ref[0:tc, :] = dy1
        ds_ref[tc:tc + pad, :] = nx_ref[...]
        dy0 = None
        for j in range(CONV_K):
            term = cw_ref[j:j + 1, :] * ds_ref[pl.ds(CONV_K - 1 - j, tc), :]
            dy0 = term if dy0 is None else dy0 + term
        dp_ref[:, 0:W_C] = dy0 * sg
        dp_ref[:, W_C:2 * W_C] = dy0 * a * sg * (1.0 - sg)
        nx_ref[...] = dy1[0:pad, :]

    rev = lambda c: pl.BlockSpec((tc, W_C), lambda t, c=c: (nc - 1 - t, c))
    prev = lambda c: pl.BlockSpec((tc, W_C), lambda t, c=c: (jnp.maximum(nc - 2 - t, 0), c))
    full = lambda a: pl.BlockSpec(a.shape, lambda t: (0,) * a.ndim)
    params = [cw, cb, lg, lb, gg]
    vec = SDS((1, W_C), F32)
    outs = [SDS((s, 2 * W_C), F32), SDS((32, W_C), F32), vec, vec, vec, vec]
    return pl.pallas_call(
        body, name="conf_bwd", grid=(nc,),
        in_specs=[rev(3), rev(5), rev(6), prev(5), prev(6), rev(0)] + [full(a) for a in params],
        out_specs=[pl.BlockSpec((tc, 2 * W_C), lambda t: (nc - 1 - t, 0))]
        + [pl.BlockSpec(o.shape, lambda t: (0, 0)) for o in outs[1:]],
        out_shape=outs,
        scratch_shapes=[pltpu.VMEM((tc + pad, W_C), F32), pltpu.VMEM((tc + pad, W_C), F32), pltpu.VMEM((pad, W_C), F32)],
        compiler_params=_cp("arbitrary"),
    )(dy, proj, proj, proj, proj, y1, *params)


def _assemble_dproj(dlru, dq, dcur, dprev, dconf):
    s = dq.shape[0]
    nb = s // BLK

    def body(dl_ref, dq_ref, dc_ref, dn_ref, df_ref, o_ref):
        n = pl.program_id(0)
        o_ref[:, 0:512] = dl_ref[...].astype(BF16)
        o_ref[:, 512:1024] = dq_ref[...].astype(BF16)
        o_ref[:, 1024:1280] = (dc_ref[...] + jnp.where(n < nb - 1, dn_ref[...], 0.0)).astype(BF16)
        o_ref[:, 1280:1792] = df_ref[...].astype(BF16)

    wide = pl.BlockSpec((BLK, 512), lambda n: (n, 0))
    return pl.pallas_call(
        body, name="assemble_dproj", grid=(nb,),
        in_specs=[wide, wide, pl.BlockSpec((BLK, 256), lambda n: (n, 0)),
                  pl.BlockSpec((BLK, 256), lambda n: (jnp.minimum(n + 1, nb - 1), 0)), wide],
        out_specs=pl.BlockSpec((BLK, P_IN), lambda n: (n, 0)), out_shape=SDS((s, P_IN), BF16),
        compiler_params=_cp("parallel"),
    )(dlru, dq, dcur, dprev, dconf)


def _loss_grad(y, t, tm):
    s = y.shape[0]

    def body(y_ref, t_ref, dy_ref, l_ref):
        err = y_ref[...] - t_ref[...]
        dy_ref[...] = err * (1.0 / D)
        _acc(l_ref, pl.program_id(0) == 0, jnp.sum(err * err, axis=0, keepdims=True))

    row = pl.BlockSpec((tm, D), lambda i: (i, 0))
    return pl.pallas_call(
        body, name="loss_grad", grid=(s // tm,), in_specs=[row, row],
        out_specs=[row, pl.BlockSpec((1, D), lambda i: (0, 0))],
        out_shape=[SDS((s, D), F32), SDS((1, D), F32)], compiler_params=_cp("arbitrary"),
    )(y, t)


def _block_diag(w):
    rows = [jnp.concatenate([w[h] if k == h else jnp.zeros((64, 64), w.dtype) for k in range(4)], axis=1) for h in range(4)]
    return jnp.concatenate(rows, axis=0)


def _diag_blocks(m):
    return jnp.stack([m[64 * h:64 * (h + 1), 64 * h:64 * (h + 1)] for h in range(4)])


def _layer_params(small, l, lru_cw, conv_w):
    v = lambda name: small[name][l].reshape(1, -1)
    gg = small["group_g"][l]
    return dict(
        ffn1_pre=v("ffn1_pre_g"), ffn1_post=v("ffn1_post_g"), mix_pre=v("mix_pre_g"), mix_post=v("mix_post_g"),
        ffn2_pre=v("ffn2_pre_g"), ffn2_post=v("ffn2_post_g"),
        lru_cw=lru_cw, lru_cb=v("lru_conv_b"),
        wa=_block_diag(small["lru_w_a"][l]).astype(BF16), ba=v("lru_b_a"),
        wx=_block_diag(small["lru_w_x"][l]).astype(BF16), bx=v("lru_b_x"), lam=v("lru_lambda"),
        sinks8=jnp.broadcast_to(small["attn_sinks"][l][:, None], (NQ, 128)),
        conv_w=conv_w, conv_b=v("conv_b"), ln_g=v("conv_ln_g"), ln_b=v("conv_ln_b"),
        gg_a=gg[0:W_A].reshape(1, -1), gg_b=gg[W_A:W_A + W_B].reshape(1, -1), gg_c=gg[W_A + W_B:].reshape(1, -1),
    )


def _forward_layer(x, big, p, tm, tc, dep=None):
    sv = dict(p=p, x0=x)
    h1, g1, u1, a1 = _ffn_up(x, p["ffn1_pre"], big["ffn1_w_gu"], 0, tm, dep)
    z1, x = _mm_rms_res(a1, big["ffn1_w_down"], 0, x, p["ffn1_post"], 0.5, tm, FH, "ffn_down")
    sv.update(h1=h1, g1=g1, u1=u1, a1=a1, z1=z1, x1=x)
    hn, proj = _proj(x, p["mix_pre"], big["w_in"], 0, tm)
    yn_a, hl = _lru_fwd(proj, p["lru_cw"], p["lru_cb"], p["wa"], p["ba"], p["wx"], p["bx"], p["lam"], p["gg_a"], tc)
    yn_b, ob = _attn_fwd(proj, p["sinks8"], p["gg_b"])
    yn_c, y1 = _conf_fwd(proj, p["conv_w"], p["conv_b"], p["ln_g"], p["ln_b"], p["gg_c"], tc)
    ycat = jnp.concatenate([yn_a, yn_b, yn_c], axis=1)
    zo, x = _mm_rms_res(ycat, big["w_out"], 0, x, p["mix_post"], 1.0, tm, D, "mix_out")
    sv.update(hn=hn, proj=proj, hl=hl, ob=ob, y1=y1, ycat=ycat, zo=zo, x2=x)
    h2, g2, u2, a2 = _ffn_up(x, p["ffn2_pre"], big["ffn2_w_gu"], 0, tm)
    z2, x = _mm_rms_res(a2, big["ffn2_w_down"], 0, x, p["ffn2_post"], 0.5, tm, FH, "ffn_down")
    sv.update(h2=h2, g2=g2, u2=u2, a2=a2, z2=z2)
    return x, sv


def _grad_buffers():
    empty = lambda *shape: lax.empty(shape, F32)
    return dict(ffn1_w_gu=empty(1, NSHARD, D, FH), ffn2_w_gu=empty(1, NSHARD, D, FH), ffn1_w_down=empty(1, 1, DFF, D),
                ffn2_w_down=empty(1, 1, DFF, D), w_in=empty(1, 1, D, P_IN), w_out=empty(1, 1, D, D))


def _backward_layer(dx, big, sv, bufs, tm, tc, stage=None):
    p = sv["p"]
    gr = {}
    stage = stage or (lambda k, dx: None)

    def ffn_bwd(dx, which, xin, h, g, u, a, z, pre, post, dep):
        dz, dpost = _rms_bwd(dx, z, post, 0.5, tm, "ffn_post_bwd", dep)
        dg, du = _ffn_bwd_mid(dz, big[which + "_w_down"], 0, g, u, tm)
        bufs[which + "_w_down"] = _mm_tn_into(bufs[which + "_w_down"], a, dz, 0, 0, FH, D, tm, "dw_down")
        bufs[which + "_w_gu"] = _mm_tn_into(bufs[which + "_w_gu"], h, dg, 0, 0, D, FH, tm, "dw_gate")
        bufs[which + "_w_gu"] = _mm_tn_into(bufs[which + "_w_gu"], h, du, 0, 2, D, FH, tm, "dw_up")
        dxn, dpre = _ffn_bwd_dh(dg, du, big[which + "_w_gu"], 0, xin, pre, dx, tm)
        return dxn, dpre, dpost

    dx, gr["ffn2_pre_g"], gr["ffn2_post_g"] = ffn_bwd(dx, "ffn2", sv["x2"], sv["h2"], sv["g2"], sv["u2"], sv["a2"],
                                                      sv["z2"], p["ffn2_pre"], p["ffn2_post"], stage(0, dx))
    do, gr["mix_post_g"] = _rms_bwd(dx, sv["zo"], p["mix_post"], 1.0, tm, "mix_post_bwd", stage(1, dx))
    bufs["w_out"] = _mm_tn_into(bufs["w_out"], sv["ycat"], do, 0, 0, D, D, tm, "dw_out")
    dy = _mm_nt(do, big["w_out"], 0, tm, "mix_dy")
    proj = sv["proj"]
    (dlru, dcw, gr["lru_conv_b"], dwa, gr["lru_b_a"], dwx, gr["lru_b_x"], gr["lru_lambda"], dgg_a) = _lru_bwd(
        dy, proj, sv["hl"], p["lru_cw"], p["lru_cb"], p["wa"], p["ba"], p["wx"], p["bx"], p["lam"], p["gg_a"], tc)
    dq, dcur, dprev, dsk, dgg_b = _attn_bwd(dy, proj, sv["ob"], p["sinks8"], p["gg_b"])
    dconf, dconvw, gr["conv_b"], gr["conv_ln_g"], gr["conv_ln_b"], dgg_c = _conf_bwd(
        dy, proj, sv["y1"], p["conv_w"], p["conv_b"], p["ln_g"], p["ln_b"], p["gg_c"], tc)
    dproj = _assemble_dproj(dlru, dq, dcur, dprev, dconf)
    bufs["w_in"] = _mm_tn_into(bufs["w_in"], sv["hn"], dproj, 0, 0, D, P_IN, tm, "dw_in")
    dx, gr["mix_pre_g"] = _mm_nt_rmsbwd(dproj, big["w_in"], 0, sv["x1"], p["mix_pre"], dx, tm)
    gr["lru_conv_w"] = dcw[0:LRU_K]
    gr["lru_w_a"] = _diag_blocks(dwa)
    gr["lru_w_x"] = _diag_blocks(dwx)
    gr["attn_sinks"] = dsk[:, 0]
    gr["conv_w"] = dconvw[0:CONV_K]
    gr["group_g"] = jnp.concatenate([dgg_a, dgg_b, dgg_c], axis=1)
    dx, gr["ffn1_pre_g"], gr["ffn1_post_g"] = ffn_bwd(dx, "ffn1", sv["x0"], sv["h1"], sv["g1"], sv["u1"], sv["a1"],
                                                      sv["z1"], p["ffn1_pre"], p["ffn1_post"], stage(2, dx))
    stage(3, dx)
    return dx, bufs, gr


def _tiles(s):
    return min(512, s), min(512, s // 2)


HBM_SPEC = pl.BlockSpec(memory_space=pltpu.HBM)
SEM_SPEC = pl.BlockSpec(memory_space=pltpu.SEMAPHORE)
EFFECT = pltpu.SideEffectType.DATAFLOW_SIDE_EFFECTING


def _place():
    x, y, c = lax.axis_index("x"), lax.axis_index("y"), lax.axis_index("c")
    return x, y, c, [(1 - x, y), (x, 1 - y), (1 - x, 1 - y)]


def _rcopy(src, dst, send_sems, recv_sems, k, to):
    return pltpu.make_async_remote_copy(src_ref=src, dst_ref=dst, send_sem=send_sems.at[k], recv_sem=recv_sems.at[k],
                                        device_id=to, device_id_type=MESH)


def _half(rows, which):
    return pl.ds(which * (rows // 2), rows // 2)


def _place_shard(w, l, p_idx, dtype):
    _, rows, cols = w.shape
    tr = _rows_per_block(rows, cols, 16) if rows % 16 == 0 else rows

    def body(p_ref, buf_ref, w_ref, o_ref):
        o_ref[...] = w_ref[...].astype(dtype)

    spec = pltpu.PrefetchScalarGridSpec(
        num_scalar_prefetch=1, grid=(rows // tr,),
        in_specs=[ANY, pl.BlockSpec((None, tr, cols), lambda i, pr: (l, i, 0))],
        out_specs=pl.BlockSpec((None, None, tr, cols), lambda i, pr: (0, pr[0], i, 0)))
    shape = (1, NSHARD, rows, cols)
    return pl.pallas_call(body, name="place_shard", grid_spec=spec, out_shape=SDS(shape, dtype),
                          input_output_aliases={1: 0}, compiler_params=_cp("parallel"),
                          )(p_idx, lax.empty(shape, dtype), w)


def _gather_two_level(bufs, n_halved):
    n = len(bufs)

    def body(*refs):
        outs = refs[n:2 * n]
        send_sems, recv_sems = refs[2 * n:]
        x, y, c, chips = _place()
        p = 2 * x + y
        me, sibling = (x, y, c), (x, y, 1 - c)

        def blk(a, q, half):
            return outs[a].at[0, q, _half(outs[a].shape[2], half)] if a < n_halved else outs[a].at[0, q]

        def cp(a, k, q, half, to):
            return _rcopy(blk(a, q, half), blk(a, q, half), send_sems, recv_sems, 6 * a + k, to)

        first = [cp(a, j, p, c, (*chip, c)) for a in range(n) for j, chip in enumerate(chips)]
        for d in first:
            d.start()
        passed = []
        for a in range(n):
            for j, chip in enumerate(chips):
                q = 2 * chip[0] + chip[1]
                cp(a, j, q, c, me).wait_recv()
                if a < n_halved:
                    passed.append(cp(a, 3 + j, q, c, sibling))
                    passed[-1].start()
        for a in range(n_halved):
            for j, chip in enumerate(chips):
                cp(a, 3 + j, 2 * chip[0] + chip[1], 1 - c, me).wait_recv()
        for d in first + passed:
            d.wait_send()

    return pl.pallas_call(
        body, name="gather_layer0", in_specs=[ANY] * n, out_specs=[ANY] * n,
        out_shape=[SDS(b.shape, b.dtype) for b in bufs], input_output_aliases={a: a for a in range(n)},
        scratch_shapes=[pltpu.SemaphoreType.DMA((6 * n,)), pltpu.SemaphoreType.DMA((6 * n,))],
    )(*bufs)


def _exchange(name, bufs, plan, nsem):
    n = len(bufs)

    def body(*refs):
        cps = plan(refs[n:2 * n], refs[2 * n], refs[2 * n + 1])
        for cp in cps:
            cp.start()
        for cp in cps:
            cp.wait()

    return pl.pallas_call(
        body, name=name, in_specs=[ANY] * n, out_specs=[ANY] * n, out_shape=[SDS(b.shape, b.dtype) for b in bufs],
        input_output_aliases={a: a for a in range(n)},
        scratch_shapes=[pltpu.SemaphoreType.DMA((nsem,)), pltpu.SemaphoreType.DMA((nsem,))],
    )(*bufs)


def _exchange_start(name, bufs, plan, nsem):
    n = len(bufs)

    def body(*refs):
        for cp in plan(refs[:n], refs[n], refs[n + 1]):
            cp.start()
        token = refs[2 * n + 2]
        token[...] = jnp.zeros_like(token)

    outs = pl.pallas_call(
        body, name=name,
        out_shape=(pltpu.SemaphoreType.DMA((nsem,)), pltpu.SemaphoreType.DMA((nsem,)),
                   *[pltpu.HBM(b.shape, b.dtype) for b in bufs], SDS((8, 128), F32)),
        in_specs=[HBM_SPEC] * n,
        out_specs=(SEM_SPEC, SEM_SPEC, *[HBM_SPEC] * n, pl.BlockSpec(memory_space=pltpu.VMEM)),
        input_output_aliases={a: 2 + a for a in range(n)},
        compiler_params=pltpu.CompilerParams(has_side_effects=EFFECT),
    )(*[pltpu.with_memory_space_constraint(b, pltpu.HBM) for b in bufs])
    return outs[0], outs[1], list(outs[2:2 + n]), outs[2 + n]


def _exchange_wait(name, send_sems, recv_sems, bufs, plan, after):
    n = len(bufs)

    def body(*refs):
        for cp in plan(refs[:n], refs[n], refs[n + 1]):
            cp.wait_send()
            cp.wait_recv()

    return pl.pallas_call(
        body, name=name, out_shape=[pltpu.HBM(b.shape, b.dtype) for b in bufs],
        in_specs=[HBM_SPEC] * n + [SEM_SPEC, SEM_SPEC, ANY], out_specs=[HBM_SPEC] * n,
        input_output_aliases={a: a for a in range(n)},
        compiler_params=pltpu.CompilerParams(has_side_effects=EFFECT),
    )(*bufs, send_sems, recv_sems, after)


def _plan_gather(refs, send_sems, recv_sems):
    x, y, c, chips = _place()
    p = 2 * x + y
    return [_rcopy(r.at[0, p], r.at[0, p], send_sems, recv_sems, 3 * a + j, (*chip, c))
            for a, r in enumerate(refs) for j, chip in enumerate(chips)]


def _plan_pair_exchange(refs, send_sems, recv_sems):
    x, y, c, _ = _place()
    n = len(refs) // 2
    return [_rcopy(refs[a].at[:, _half(refs[a].shape[1], 1 - c)], refs[n + a], send_sems, recv_sems, a, (x, y, 1 - c))
            for a in range(n)]


def _plan_chip_exchange(refs, send_sems, recv_sems):
    x, y, c, chips = _place()
    n = len(refs) // 2
    return [_rcopy(refs[a].at[2 * chip[0] + chip[1]], refs[n + a].at[j], send_sems, recv_sems, 3 * a + j, (*chip, c))
            for a in range(n) for j, chip in enumerate(chips)]


def _plan_pair_share(refs, send_sems, recv_sems):
    x, y, c, _ = _place()
    return [_rcopy(r.at[_half(r.shape[0], c)], r.at[_half(r.shape[0], c)], send_sems, recv_sems, a, (x, y, 1 - c))
            for a, r in enumerate(refs)]


def _allreduce_small(buf):
    rows = buf.shape[0]

    def body(in_ref, out_ref, gather_ref, send_sems, recv_sems):
        x, y, c, _ = _place()
        me = 4 * x + 2 * y + c
        gather_ref[me] = in_ref[...]
        cps, slots = [], []
        for m in range(1, NDEV):
            px = 1 - x if m & 4 else x
            py = 1 - y if m & 2 else y
            pc = 1 - c if m & 1 else c
            cps.append(_rcopy(in_ref, gather_ref.at[me], send_sems, recv_sems, m - 1, (px, py, pc)))
            slots.append(4 * px + 2 * py + pc)
        for cp in cps:
            cp.start()
        for m in range(1, NDEV):
            _rcopy(in_ref, gather_ref.at[slots[m - 1]], send_sems, recv_sems, m - 1, (x, y, c)).wait_recv()
        for cp in cps:
            cp.wait_send()
        total = gather_ref[0]
        for dev in range(1, NDEV):
            total = total + gather_ref[dev]
        out_ref[...] = total

    vm = pl.BlockSpec(memory_space=pltpu.VMEM)
    return pl.pallas_call(
        body, name="allreduce_small", in_specs=[vm], out_specs=vm, out_shape=SDS(buf.shape, F32),
        scratch_shapes=[pltpu.VMEM((NDEV, rows, 128), F32), pltpu.SemaphoreType.DMA((NDEV - 1,)),
                        pltpu.SemaphoreType.DMA((NDEV - 1,))],
        compiler_params=pltpu.CompilerParams(vmem_limit_bytes=VMEM_LIMIT),
    )(buf)


BLOCK_ELEMS = 256 * 1024


def _rows_per_block(rows, cols, mult):
    best = None
    for tr in range(mult, rows + 1, mult):
        if rows % tr == 0 and tr * cols <= BLOCK_ELEMS:
            best = tr
    assert best is not None, (rows, cols)
    return best


def _pair_sum(g, r, c_idx):
    nq, rows, cols = g.shape
    half = rows // 2
    tr = _rows_per_block(half, cols, 16)
    nb = half // tr

    def body(c_ref, g_ref, r_ref, t_ref):
        t_ref[...] = (g_ref[...] + r_ref[...]).astype(BF16)

    blk = pl.BlockSpec((None, tr, cols), lambda q, i, cr: (q, i, 0))
    spec = pltpu.PrefetchScalarGridSpec(
        num_scalar_prefetch=1, grid=(nq, nb),
        in_specs=[pl.BlockSpec((None, tr, cols), lambda q, i, cr: (q, cr[0] * nb + i, 0)), blk], out_specs=blk)
    return pl.pallas_call(body, name="grad_pair_sum", grid_spec=spec, out_shape=SDS((nq, half, cols), BF16),
                          compiler_params=_cp("parallel", "parallel"))(c_idx, g, r)


def _chip_sum(g, r, rr, cp_idx):
    _, rows, cols = g.shape
    half = rows // 2
    tr = _rows_per_block(half, cols, 16)
    nb = half // tr

    def body(cp_ref, buf_ref, g_ref, r_ref, rr_ref, o_ref):
        o_ref[...] = ((g_ref[...] + r_ref[...]) + rr_ref[0].astype(F32) + rr_ref[1].astype(F32) + rr_ref[2].astype(F32))

    spec = pltpu.PrefetchScalarGridSpec(
        num_scalar_prefetch=1, grid=(nb,),
        in_specs=[ANY, pl.BlockSpec((None, tr, cols), lambda i, cp: (cp[1], cp[0] * nb + i, 0)),
                  pl.BlockSpec((None, tr, cols), lambda i, cp: (cp[1], i, 0)),
                  pl.BlockSpec((3, tr, cols), lambda i, cp: (0, i, 0))],
        out_specs=pl.BlockSpec((tr, cols), lambda i, cp: (cp[0] * nb + i, 0)))
    return pl.pallas_call(body, name="grad_chip_sum", grid_spec=spec, out_shape=SDS((rows, cols), F32),
                          input_output_aliases={1: 0}, compiler_params=_cp("parallel"),
                          )(cp_idx, lax.empty((rows, cols), F32), g, r, rr)


def _adamw_math(w, g, m, v):
    mn = ADAM_B1 * m + (1.0 - ADAM_B1) * g
    vn = ADAM_B2 * v + (1.0 - ADAM_B2) * (g * g)
    m_hat = mn / (1.0 - ADAM_B1 ** ADAM_STEP)
    v_hat = vn / (1.0 - ADAM_B2 ** ADAM_STEP)
    return -ADAM_LR * (m_hat / (jnp.sqrt(v_hat) + ADAM_EPS) + ADAM_WD * w), mn, vn


def _adamw_layers(w, gs, m, v):
    depth, rows, cols = w.shape
    tr = _rows_per_block(rows, cols, 8)

    def body(w_ref, g0_ref, g1_ref, m_ref, v_ref, go_ref, d_ref, mo_ref, vo_ref):
        gg = jnp.where(pl.program_id(0) == 0, g0_ref[...], g1_ref[...])
        go_ref[...] = gg
        d_ref[...], mo_ref[...], vo_ref[...] = _adamw_math(w_ref[...], gg, m_ref[...], v_ref[...])

    blk = pl.BlockSpec((None, tr, cols), lambda l, i: (l, i, 0))
    return pl.pallas_call(
        body, name="adamw_layers", grid=(depth, rows // tr),
        in_specs=[blk, pl.BlockSpec((tr, cols), lambda l, i: (i * (1 - l), 0)),
                  pl.BlockSpec((tr, cols), lambda l, i: (i * l, 0)), blk, blk],
        out_specs=[blk] * 4, out_shape=[SDS(w.shape, F32)] * 4,
        compiler_params=_cp("arbitrary", "arbitrary"))(w, gs[0], gs[1], m, v)


def _adamw_packed(w, g, m, v):
    def body(w_ref, g_ref, m_ref, v_ref, d_ref, mo_ref, vo_ref):
        d_ref[...], mo_ref[...], vo_ref[...] = _adamw_math(w_ref[...], g_ref[...], m_ref[...], v_ref[...])

    vm = pl.BlockSpec(memory_space=pltpu.VMEM)
    return pl.pallas_call(body, name="adamw_packed", in_specs=[vm] * 4, out_specs=[vm] * 3,
                          out_shape=[SDS(w.shape, F32)] * 3,
                          compiler_params=pltpu.CompilerParams(vmem_limit_bytes=VMEM_LIMIT))(w, g, m, v)


_WEIGHTS = ["ffn1_pre_g", "ffn1_w_gu", "ffn1_w_down", "ffn1_post_g", "mix_pre_g", "w_in", "lru_conv_w", "lru_conv_b",
            "lru_w_a", "lru_b_a", "lru_w_x", "lru_b_x", "lru_lambda", "attn_sinks", "conv_w", "conv_b", "conv_ln_g",
            "conv_ln_b", "group_g", "w_out", "mix_post_g", "ffn2_pre_g", "ffn2_w_gu", "ffn2_w_down", "ffn2_post_g"]
_INPUTS = ["x"] + _WEIGHTS + ["loss_target"] + ["m_" + n for n in _WEIGHTS] + ["v_" + n for n in _WEIGHTS]
_BIG = ["ffn1_w_gu", "ffn1_w_down", "w_in", "w_out", "ffn2_w_gu", "ffn2_w_down"]
_SMALL_SHARDED = ["lru_conv_w", "conv_w"]
_SMALL_REPL = [n for n in _WEIGHTS if n not in _BIG and n not in _SMALL_SHARDED]
_GATHERED = _BIG + _SMALL_SHARDED

PACK_TILE = 8 * 128


def _pack(arrs):
    parts = []
    for a in arrs:
        flat = a.reshape(-1)
        parts.append(jnp.pad(flat, (0, -flat.shape[0] % PACK_TILE)).reshape(-1, 128))
    return jnp.concatenate(parts, axis=0)


def _unpack(buf, shapes):
    out, row = [], 0
    for shp in shapes:
        size = math.prod(shp)
        nrow = -(-size // PACK_TILE) * 8
        out.append(buf[row:row + nrow].reshape(-1)[:size].reshape(shp))
        row += nrow
    return out


def _unshard_cols(a):
    return a.transpose(0, 2, 1, 3).reshape(1, a.shape[2], NSHARD * a.shape[3])


def _full_weights(gathered):
    g = dict(zip(_GATHERED, gathered))
    big = dict(ffn1_w_gu=g["ffn1_w_gu"], ffn1_w_down=g["ffn1_w_down"].reshape(1, DFF, D), w_in=_unshard_cols(g["w_in"]),
               w_out=g["w_out"].reshape(1, D, D), ffn2_w_gu=g["ffn2_w_gu"], ffn2_w_down=g["ffn2_w_down"].reshape(1, DFF, D))
    return big, _unshard_cols(g["lru_conv_w"])[0], _unshard_cols(g["conv_w"])[0]


def _by_shard(bufs):
    rows = lambda a: a.reshape(NSHARD, a.shape[2] // NSHARD, a.shape[3])
    dw_in = bufs["w_in"].reshape(D, NSHARD, P_IN // NSHARD).transpose(1, 0, 2)
    return [bufs["ffn1_w_gu"][0], rows(bufs["ffn1_w_down"]), dw_in, rows(bufs["w_out"]), bufs["ffn2_w_gu"][0],
            rows(bufs["ffn2_w_down"])]


class _GradReduce:
    def __init__(self, gs, c_idx, cp_idx, tag):
        self.gs, self.c_idx, self.cp_idx, self.tag = gs, c_idx, cp_idx, tag
        self.n = len(gs)
        self.pending = None
        self.result = None

    def _lands1(self):
        return [lax.empty((NSHARD, g.shape[1] // 2, g.shape[2]), F32) for g in self.gs]

    def _sum1(self, rs):
        return [_pair_sum(g, r, self.c_idx) for g, r in zip(self.gs, rs)]

    def _lands2(self, ts):
        return [lax.empty((3,) + t.shape[1:], BF16) for t in ts]

    def _sum2(self, rs, rrs):
        return [_chip_sum(g, r, rr, self.cp_idx) for g, r, rr in zip(self.gs, rs, rrs)]

    def run(self):
        n = self.n
        out = _exchange("grad_pair_exchange" + self.tag, self.gs + self._lands1(), _plan_pair_exchange, n)
        rs = out[n:]
        ts = self._sum1(rs)
        rrs = _exchange("grad_chip_exchange" + self.tag, ts + self._lands2(ts), _plan_chip_exchange, 3 * n)[n:]
        self.result = _exchange("grad_pair_share" + self.tag, self._sum2(rs, rrs), _plan_pair_share, n)
        return self.result

    def stage(self, k, dx):
        n = self.n
        names = ["grad_pair_exchange", "grad_chip_exchange", "grad_pair_share"]
        plans = [_plan_pair_exchange, _plan_chip_exchange, _plan_pair_share]
        nsems = [n, 3 * n, n]
        if k > 0:
            ss, rs_, bufs = self.pending
            done = _exchange_wait(names[k - 1] + self.tag + "_wait", ss, rs_, bufs, plans[k - 1], dx)
            if k == 1:
                self.gs, self.rs = done[:n], done[n:]
                ts = self._sum1(self.rs)
                nxt = ts + self._lands2(ts)
            elif k == 2:
                nxt = self._sum2(self.rs, done[n:])
            else:
                self.result = done
                return None
        else:
            nxt = self.gs + self._lands1()
        ss, rs_, bufs, token = _exchange_start(names[k] + self.tag + "_start", nxt, plans[k], nsems[k])
        self.pending = (ss, rs_, bufs)
        return token


def kernel(*args):
    d = dict(zip(_INPUTS, args, strict=True))
    xi, yi, ci = lax.axis_index("x"), lax.axis_index("y"), lax.axis_index("c")
    p = 2 * xi + yi
    c_idx = jnp.reshape(ci, (1,)).astype(jnp.int32)
    p_idx = jnp.reshape(p, (1,)).astype(jnp.int32)
    cp_idx = jnp.stack([ci, p]).astype(jnp.int32)
    x, target = d["x"][0], d["loss_target"][0]
    tm, tc = _tiles(x.shape[0])

    placed = [[_place_shard(d[n], l, p_idx, BF16 if n in _BIG else F32) for n in _GATHERED] for l in range(DEPTH)]
    big0, lru_cw0, conv_w0 = _full_weights(_gather_two_level(placed[0], len(_BIG)))
    nsem = 3 * len(_GATHERED)
    send_sems, recv_sems, flying, token = _exchange_start("gather_layer1_start", placed[1], _plan_gather, nsem)
    small = {n: d[n] for n in _SMALL_REPL}
    x1, sv0 = _forward_layer(x, big0, _layer_params(small, 0, lru_cw0, conv_w0), tm, tc, token)
    big1, lru_cw1, conv_w1 = _full_weights(
        _exchange_wait("gather_layer1_wait", send_sems, recv_sems, flying, _plan_gather, x1))
    x2, sv1 = _forward_layer(x1, big1, _layer_params(small, 1, lru_cw1, conv_w1), tm, tc)
    dx, lcols = _loss_grad(x2, target, tm)

    dx, bufs1, sg1 = _backward_layer(dx, big1, sv1, _grad_buffers(), tm, tc)
    red1 = _GradReduce(_by_shard(bufs1), c_idx, cp_idx, "_l1")
    grad_x, bufs0, sg0 = _backward_layer(dx, big0, sv0, _grad_buffers(), tm, tc, red1.stage)
    red0 = _GradReduce(_by_shard(bufs0), c_idx, cp_idx, "_l0")
    big_grads = list(zip(red0.run(), red1.result))
    sgrads = [sg0, sg1]

    stacked = {n: jnp.stack([sgrads[l][n].reshape(d[n].shape[1:]) for l in range(DEPTH)]) for n in _SMALL_REPL}
    for n in _SMALL_SHARDED:
        stacked[n] = jnp.stack([sgrads[l][n] for l in range(DEPTH)])
    loss_part = jnp.pad((0.5 / D) * jnp.sum(lcols).reshape(1), (0, 127))
    order = _SMALL_REPL + _SMALL_SHARDED
    summed = _unpack(_allreduce_small(_pack([loss_part] + [stacked[n] for n in order])),
                     [(128,)] + [stacked[n].shape for n in order])
    loss = summed[0][0]
    grads = {}
    for n, g in zip(order, summed[1:]):
        if n in _SMALL_SHARDED:
            g = lax.dynamic_slice_in_dim(g, p * (g.shape[2] // NSHARD), g.shape[2] // NSHARD, axis=2)
        grads[n] = g

    delta, new_m, new_v = {}, {}, {}
    for n, gs in zip(_BIG, big_grads):
        grads[n], delta[n], new_m[n], new_v[n] = _adamw_layers(d[n], gs, d["m_" + n], d["v_" + n])
    shapes = [d[n].shape for n in order]
    packed = [_pack([src(n) for n in order]) for src in
              (lambda n: d[n], lambda n: grads[n], lambda n: d["m_" + n], lambda n: d["v_" + n])]
    for out, res in zip((delta, new_m, new_v), _adamw_packed(*packed)):
        out.update(zip(order, _unpack(res, shapes)))

    return (loss, grad_x[None], *[grads[n] for n in _WEIGHTS], *[delta[n] for n in _WEIGHTS],
            *[new_m[n] for n in _WEIGHTS], *[new_v[n] for n in _WEIGHTS])
```

```python
import functools
import math

import jax
import jax.numpy as jnp
from jax import lax
from jax.experimental import pallas as pl
from jax.experimental.pallas import tpu as pltpu

F32 = jnp.float32
BF16 = jnp.bfloat16
SDS = jax.ShapeDtypeStruct

D = 1024
DFF = 2816
FH = DFF // 2
DEPTH = 2
W_A = 256
W_B = 512
W_C = 256
NQ = 8
HD = 64
BLK = 128
P_IN = 1792
LRU_K = 4
CONV_K = 31
LRU_C = 8.0
NORM_EPS = 1e-6
LN_EPS = 1e-5
NEG_BIG = -1e30
SCALE = 1.0 / math.sqrt(HD)

ADAM_LR = 0.001
ADAM_B1 = 0.9
ADAM_B2 = 0.999
ADAM_EPS = 1e-08
ADAM_WD = 0.01
ADAM_STEP = 10

VMEM_LIMIT = 56 * 1024 * 1024
NSHARD = 4
NDEV = 8

TN = (((0,), (0,)), ((), ()))
NT = (((1,), (1,)), ((), ()))

MESH = pl.DeviceIdType.MESH
ANY = pl.BlockSpec(memory_space=pl.ANY)


def _cp(*sem):
    return pltpu.CompilerParams(dimension_semantics=sem if sem else None, vmem_limit_bytes=VMEM_LIMIT)


def _rsq(x, eps):
    return lax.rsqrt(jnp.mean(x * x, axis=-1, keepdims=True) + eps)


def _rms_bwd_rows(x, g, dy):
    r = _rsq(x, NORM_EPS)
    xh = x * r
    dyg = dy * g
    dx = r * (dyg - xh * jnp.mean(dyg * xh, axis=-1, keepdims=True))
    return dx, dy * xh


def _sig(x):
    return jax.nn.sigmoid(x)


def _ffn_up(x, pre_g, wgu, l, tm, deps=()):
    s = x.shape[0]
    deps = list(deps)

    def body(x_ref, g_ref, wg_ref, wu_ref, *rest):
        h_ref, go_ref, uo_ref, a_ref = rest[len(deps):]

        @pl.when(pl.program_id(1) == 0)
        def _():
            xf = x_ref[...]
            h_ref[...] = (xf * _rsq(xf, NORM_EPS) * g_ref[...]).astype(BF16)

        h = h_ref[...]
        gg = jnp.dot(h, wg_ref[...], preferred_element_type=F32)
        uu = jnp.dot(h, wu_ref[...], preferred_element_type=F32)
        go_ref[...] = gg.astype(BF16)
        uo_ref[...] = uu.astype(BF16)
        a_ref[...] = (gg * _sig(gg) * uu).astype(BF16)

    wide = pl.BlockSpec((tm, FH), lambda i, j: (i, j))
    return pl.pallas_call(
        body, name="ffn_up", grid=(s // tm, 2),
        in_specs=[pl.BlockSpec((tm, D), lambda i, j: (i, 0)), pl.BlockSpec((1, D), lambda i, j: (0, 0)),
                  pl.BlockSpec((None, None, D, FH), lambda i, j: (l, j, 0, 0)),
                  pl.BlockSpec((None, None, D, FH), lambda i, j: (l, j + 2, 0, 0))] + [ANY] * len(deps),
        out_specs=[pl.BlockSpec((tm, D), lambda i, j: (i, 0)), wide, wide, wide],
        out_shape=[SDS((s, D), BF16), SDS((s, DFF), BF16), SDS((s, DFF), BF16), SDS((s, DFF), BF16)],
        compiler_params=_cp("parallel", "arbitrary"),
    )(x, pre_g, wgu, wgu, *deps)


def _mm_rms_res(a, w, l, x, g, c, tm, tk, name):
    s, k_dim = a.shape
    nk = k_dim // tk

    def body(a_ref, w_ref, x_ref, g_ref, z_ref, x1_ref):
        k = pl.program_id(1)
        p = jnp.dot(a_ref[...], w_ref[...], preferred_element_type=F32)

        @pl.when(k == 0)
        def _():
            z_ref[...] = p

        @pl.when(k > 0)
        def _():
            z_ref[...] += p

        @pl.when(k == nk - 1)
        def _():
            z = z_ref[...]
            x1_ref[...] = x_ref[...] + c * (z * _rsq(z, NORM_EPS) * g_ref[...])

    row = pl.BlockSpec((tm, D), lambda i, k: (i, 0))
    return pl.pallas_call(
        body, name=name, grid=(s // tm, nk),
        in_specs=[pl.BlockSpec((tm, tk), lambda i, k: (i, k)), pl.BlockSpec((None, tk, D), lambda i, k: (l, k, 0)),
                  row, pl.BlockSpec((1, D), lambda i, k: (0, 0))],
        out_specs=[row, row],
        out_shape=[SDS((s, D), F32), SDS((s, D), F32)],
        compiler_params=_cp("parallel", "arbitrary"),
    )(a, w, x, g)


def _rms_bwd(dy, z, g, c, tm, name, deps=()):
    s = z.shape[0]
    deps = list(deps)

    def body(dy_ref, z_ref, g_ref, *rest):
        dz_ref, dg_ref = rest[len(deps):]
        dz, dgr = _rms_bwd_rows(z_ref[...], g_ref[...], c * dy_ref[...])
        dz_ref[...] = dz.astype(BF16)
        part = jnp.sum(dgr, axis=0, keepdims=True)

        @pl.when(pl.program_id(0) == 0)
        def _():
            dg_ref[...] = part

        @pl.when(pl.program_id(0) > 0)
        def _():
            dg_ref[...] += part

    row = pl.BlockSpec((tm, D), lambda i: (i, 0))
    vec = pl.BlockSpec((1, D), lambda i: (0, 0))
    return pl.pallas_call(
        body, name=name, grid=(s // tm,), in_specs=[row, row, vec] + [ANY] * len(deps), out_specs=[row, vec],
        out_shape=[SDS((s, D), BF16), SDS((1, D), F32)], compiler_params=_cp("arbitrary"),
    )(dy, z, g, *deps)


def _ffn_bwd_mid(dz, wd, l, g, u, tm):
    s = dz.shape[0]

    def body(dz_ref, wd_ref, g_ref, u_ref, dg_ref, du_ref):
        da = lax.dot_general(dz_ref[...], wd_ref[...], NT, preferred_element_type=F32)
        gg = g_ref[...].astype(F32)
        uu = u_ref[...].astype(F32)
        sg = _sig(gg)
        dg_ref[...] = (da * uu * sg * (1.0 + gg * (1.0 - sg))).astype(BF16)
        du_ref[...] = (da * gg * sg).astype(BF16)

    wide = pl.BlockSpec((tm, FH), lambda i, j: (i, j))
    return pl.pallas_call(
        body, name="ffn_bwd_mid", grid=(s // tm, 2),
        in_specs=[pl.BlockSpec((tm, D), lambda i, j: (i, 0)), pl.BlockSpec((None, FH, D), lambda i, j: (l, j, 0)), wide, wide],
        out_specs=[wide, wide],
        out_shape=[SDS((s, DFF), BF16), SDS((s, DFF), BF16)],
        compiler_params=_cp("parallel", "arbitrary"),
    )(dz, wd, g, u)


def _ffn_bwd_dh(dg, du, wgu, l, x, pre_g, dx1, tm):
    s = x.shape[0]

    def body(dg_ref, du_ref, wg_ref, wu_ref, x_ref, g_ref, dx1_ref, dx_ref, dgp_ref):
        i, k = pl.program_id(0), pl.program_id(1)
        p = (lax.dot_general(dg_ref[...], wg_ref[...], NT, preferred_element_type=F32)
             + lax.dot_general(du_ref[...], wu_ref[...], NT, preferred_element_type=F32))

        @pl.when(k == 0)
        def _():
            dx_ref[...] = p

        @pl.when(k == 1)
        def _():
            dx, dgr = _rms_bwd_rows(x_ref[...], g_ref[...], dx_ref[...] + p)
            dx_ref[...] = dx1_ref[...] + dx
            part = jnp.sum(dgr, axis=0, keepdims=True)

            @pl.when(i == 0)
            def _():
                dgp_ref[...] = part

            @pl.when(i > 0)
            def _():
                dgp_ref[...] += part

    wide = pl.BlockSpec((tm, FH), lambda i, k: (i, k))
    row = pl.BlockSpec((tm, D), lambda i, k: (i, 0))
    vec = pl.BlockSpec((1, D), lambda i, k: (0, 0))
    return pl.pallas_call(
        body, name="ffn_bwd_dh", grid=(s // tm, 2),
        in_specs=[wide, wide, pl.BlockSpec((None, None, D, FH), lambda i, k: (l, k, 0, 0)),
                  pl.BlockSpec((None, None, D, FH), lambda i, k: (l, k + 2, 0, 0)), row, vec, row],
        out_specs=[row, vec],
        out_shape=[SDS((s, D), F32), SDS((1, D), F32)],
        compiler_params=_cp("arbitrary", "arbitrary"),
    )(dg, du, wgu, wgu, x, pre_g, dx1)


def _mm_tn_into(buf, a, b, l, joff, tka, tn, ts, name):
    s, ka = a.shape
    n = b.shape[1]

    def body(buf_ref, a_ref, b_ref, o_ref):
        p = lax.dot_general(a_ref[...], b_ref[...], TN, preferred_element_type=F32)

        @pl.when(pl.program_id(2) == 0)
        def _():
            o_ref[...] = p

        @pl.when(pl.program_id(2) > 0)
        def _():
            o_ref[...] += p

    return pl.pallas_call(
        body, name=name, grid=(ka // tka, n // tn, s // ts),
        in_specs=[pl.BlockSpec(memory_space=pl.ANY),
                  pl.BlockSpec((ts, tka), lambda ia, j, t: (t, ia)), pl.BlockSpec((ts, tn), lambda ia, j, t: (t, j))],
        out_specs=pl.BlockSpec((None, None, tka, tn), lambda ia, j, t: (l, joff + j, ia, 0)),
        out_shape=SDS(buf.shape, F32), input_output_aliases={0: 0},
        compiler_params=_cp("parallel", "parallel", "arbitrary"),
    )(buf, a, b)


def _proj(x, g, w_in, l, tm):
    s = x.shape[0]

    def body(x_ref, g_ref, w_ref, h_ref, p_ref):
        xf = x_ref[...]
        h = (xf * _rsq(xf, NORM_EPS) * g_ref[...]).astype(BF16)
        h_ref[...] = h
        p_ref[...] = jnp.dot(h, w_ref[...], preferred_element_type=F32)

    return pl.pallas_call(
        body, name="proj", grid=(s // tm,),
        in_specs=[pl.BlockSpec((tm, D), lambda i: (i, 0)), pl.BlockSpec((1, D), lambda i: (0, 0)),
                  pl.BlockSpec((None, D, P_IN), lambda i: (l, 0, 0))],
        out_specs=[pl.BlockSpec((tm, D), lambda i: (i, 0)), pl.BlockSpec((tm, P_IN), lambda i: (i, 0))],
        out_shape=[SDS((s, D), BF16), SDS((s, P_IN), F32)],
        compiler_params=_cp("parallel"),
    )(x, g, w_in)


def _mm_nt(a, w, l, tm, name):
    s, k_dim = a.shape
    n = w.shape[1]

    def body(a_ref, w_ref, o_ref):
        o_ref[...] = lax.dot_general(a_ref[...], w_ref[...], NT, preferred_element_type=F32)

    return pl.pallas_call(
        body, name=name, grid=(s // tm,),
        in_specs=[pl.BlockSpec((tm, k_dim), lambda i: (i, 0)), pl.BlockSpec((None, n, k_dim), lambda i: (l, 0, 0))],
        out_specs=pl.BlockSpec((tm, n), lambda i: (i, 0)),
        out_shape=SDS((s, n), F32), compiler_params=_cp("parallel"),
    )(a, w)


def _mm_nt_rmsbwd(dp, w_in, l, x, g, dx1, tm):
    s = x.shape[0]

    def body(dp_ref, w_ref, x_ref, g_ref, dx1_ref, dx_ref, dg_ref):
        dh = lax.dot_general(dp_ref[...], w_ref[...], NT, preferred_element_type=F32)
        dx, dgr = _rms_bwd_rows(x_ref[...], g_ref[...], dh)
        dx_ref[...] = dx1_ref[...] + dx
        part = jnp.sum(dgr, axis=0, keepdims=True)

        @pl.when(pl.program_id(0) == 0)
        def _():
            dg_ref[...] = part

        @pl.when(pl.program_id(0) > 0)
        def _():
            dg_ref[...] += part

    row = pl.BlockSpec((tm, D), lambda i: (i, 0))
    vec = pl.BlockSpec((1, D), lambda i: (0, 0))
    return pl.pallas_call(
        body, name="mix_bwd_dx", grid=(s // tm,),
        in_specs=[pl.BlockSpec((tm, P_IN), lambda i: (i, 0)), pl.BlockSpec((None, D, P_IN), lambda i: (l, 0, 0)), row, vec, row],
        out_specs=[row, vec], out_shape=[SDS((s, D), F32), SDS((1, D), F32)],
        compiler_params=_cp("arbitrary"),
    )(dp, w_in, x, g, dx1)


def _row_iota(shape):
    return lax.broadcasted_iota(jnp.int32, shape, 0)


def _lru_gates(xc, wa_ref, ba_ref, wx_ref, bx_ref, lam_ref):
    xb = xc.astype(BF16)
    r = _sig(jnp.dot(xb, wa_ref[...], preferred_element_type=F32) + ba_ref[...])
    ig = _sig(jnp.dot(xb, wx_ref[...], preferred_element_type=F32) + bx_ref[...])
    nl = -lam_ref[...]
    sp = jnp.maximum(nl, 0.0) + jnp.log(1.0 + jnp.exp(-jnp.abs(nl)))
    log_a = -LRU_C * r * sp
    a = jnp.exp(log_a)
    x2 = 2.0 * log_a
    series = x2 * (1.0 + x2 * (0.5 + x2 * (1.0 / 6.0 + x2 * (1.0 / 24.0 + x2 * (1.0 / 120.0)))))
    em1 = jnp.where(x2 > -0.05, series, jnp.exp(x2) - 1.0)
    mlt = jnp.sqrt(-em1)
    return r, ig, a, mlt, sp


def _conv_taps(src_ref, w_ref, k_taps, pad, tc):
    acc = None
    for j in range(k_taps):
        term = w_ref[j:j + 1, :] * src_ref[pl.ds(pad - (k_taps - 1) + j, tc), :]
        acc = term if acc is None else acc + term
    return acc


def _gelu_parts(x):
    c0 = math.sqrt(2.0 / math.pi)
    inner = c0 * (x + 0.044715 * x * x * x)
    t = jnp.tanh(inner)
    gl = 0.5 * x * (1.0 + t)
    dgl = 0.5 * (1.0 + t) + 0.5 * x * (1.0 - t * t) * c0 * (1.0 + 3.0 * 0.044715 * x * x)
    return gl, dgl


def _lru_fwd(proj, cw, cb, wa, ba, wx, bx, lam, gg, tc):
    s = proj.shape[0]
    pad = 8

    def body(xcur_ref, xprev_ref, gate_ref, cw_ref, cb_ref, wa_ref, ba_ref, wx_ref, bx_ref, lam_ref, gg_ref,
             yn_ref, h_ref, xs_ref, hc_ref):
        i = pl.program_id(0)

        @pl.when(i == 0)
        def _():
            hc_ref[...] = jnp.zeros_like(hc_ref)

        xs_ref[0:pad, :] = jnp.where(i > 0, xprev_ref[tc - pad:tc, :], 0.0)
        xs_ref[pad:pad + tc, :] = xcur_ref[...]
        xc = _conv_taps(xs_ref, cw_ref, LRU_K, pad, tc) + cb_ref[...]
        _, ig, a, mlt, _ = _lru_gates(xc, wa_ref, ba_ref, wx_ref, bx_ref, lam_ref)
        u = mlt * (ig * xc)
        row = _row_iota((tc, W_A))
        d = 1
        while d < tc:
            ok = row >= d
            a_sh = jnp.where(ok, pltpu.roll(a, d, axis=0), 1.0)
            u_sh = jnp.where(ok, pltpu.roll(u, d, axis=0), 0.0)
            u = a * u_sh + u
            a = a * a_sh
            d *= 2
        h = u + a * hc_ref[...]
        hc_ref[...] = jnp.sum(jnp.where(row == tc - 1, h, 0.0), axis=0, keepdims=True)
        h_ref[...] = h
        gl, _ = _gelu_parts(gate_ref[...])
        ya = gl * h
        yn_ref[...] = (ya * _rsq(ya, NORM_EPS) * gg_ref[...]).astype(BF16)

    blk = lambda c: pl.BlockSpec((tc, W_A), lambda i, c=c: (i, c))
    full = lambda a: pl.BlockSpec(a.shape, lambda i: (0,) * a.ndim)
    params = [cw, cb, wa, ba, wx, bx, lam, gg]
    return pl.pallas_call(
        body, name="lru_fwd", grid=(s // tc,),
        in_specs=[blk(0), pl.BlockSpec((tc, W_A), lambda i: (jnp.maximum(i - 1, 0), 0)), blk(1)] + [full(a) for a in params],
        out_specs=[pl.BlockSpec((tc, W_A), lambda i: (i, 0))] * 2,
        out_shape=[SDS((s, W_A), BF16), SDS((s, W_A), F32)],
        scratch_shapes=[pltpu.VMEM((tc + pad, W_A), F32), pltpu.VMEM((1, W_A), F32)],
        compiler_params=_cp("arbitrary"),
    )(proj, proj, proj, *params)


def _acc(ref, first, val):
    @pl.when(first)
    def _():
        ref[...] = val

    @pl.when(jnp.logical_not(first))
    def _():
        ref[...] += val


def _lru_bwd(dy, proj, h, cw, cb, wa, ba, wx, bx, lam, gg, tc):
    s = proj.shape[0]
    nc = s // tc
    pad = 8

    def body(dy_ref, xcur_ref, xprev_ref, gate_ref, h_ref, hprev_ref, cw_ref, cb_ref, wa_ref, ba_ref, wx_ref, bx_ref,
             lam_ref, gg_ref,
             dp_ref, dcw_ref, dcb_ref, dwa_ref, dba_ref, dwx_ref, dbx_ref, dlam_ref, dgg_ref,
             xs_ref, ds_ref, mu_ref, nx_ref):
        step = pl.program_id(0)
        i = nc - 1 - step
        first = step == 0

        @pl.when(first)
        def _():
            mu_ref[...] = jnp.zeros_like(mu_ref)
            nx_ref[...] = jnp.zeros_like(nx_ref)

        xs_ref[0:pad, :] = jnp.where(i > 0, xprev_ref[tc - pad:tc, :], 0.0)
        xs_ref[pad:pad + tc, :] = xcur_ref[...]
        xc = _conv_taps(xs_ref, cw_ref, LRU_K, pad, tc) + cb_ref[...]
        r, ig, a, mlt, sp = _lru_gates(xc, wa_ref, ba_ref, wx_ref, bx_ref, lam_ref)
        hh = h_ref[...]
        gate = gate_ref[...]
        gl, dgl = _gelu_parts(gate)
        ya = gl * hh
        dya, dggr = _rms_bwd_rows(ya, gg_ref[...], dy_ref[...])
        _acc(dgg_ref, first, jnp.sum(dggr, axis=0, keepdims=True))
        dp_ref[:, W_A:2 * W_A] = dya * hh * dgl
        dh = dya * gl

        row = _row_iota((tc, W_A))
        aa = a
        uu = a * dh
        d = 1
        while d < tc:
            ok = row < tc - d
            a_sh = jnp.where(ok, pltpu.roll(aa, tc - d, axis=0), 1.0)
            u_sh = jnp.where(ok, pltpu.roll(uu, tc - d, axis=0), 0.0)
            uu = uu + aa * u_sh
            aa = aa * a_sh
            d *= 2
        cin = mu_ref[...]
        mu = uu + aa * cin
        lam_t = dh + jnp.where(row == tc - 1, cin, pltpu.roll(mu, tc - 1, axis=0))
        mu_ref[...] = jnp.sum(jnp.where(row == 0, mu, 0.0), axis=0, keepdims=True)
        hm1 = jnp.where(row == 0, jnp.where(i > 0, pltpu.roll(hprev_ref[...], 1, axis=0), 0.0),
                        pltpu.roll(hh, 1, axis=0))
        da = lam_t * hm1
        du = lam_t
        dmlt = du * ig * xc
        dig = du * mlt * xc
        dxc = du * mlt * ig
        dlog_a = da * a - dmlt * (a * a / mlt)
        dr = dlog_a * (-LRU_C * sp)
        dsp = jnp.sum(dlog_a * (-LRU_C * r), axis=0, keepdims=True)
        _acc(dlam_ref, first, dsp * (-_sig(-lam_ref[...])))
        dga = dr * r * (1.0 - r)
        dgx = dig * ig * (1.0 - ig)
        _acc(dba_ref, first, jnp.sum(dga, axis=0, keepdims=True))
        _acc(dbx_ref, first, jnp.sum(dgx, axis=0, keepdims=True))
        xb = xc.astype(BF16)
        dgab = dga.astype(BF16)
        dgxb = dgx.astype(BF16)
        _acc(dwa_ref, first, lax.dot_general(xb, dgab, TN, preferred_element_type=F32))
        _acc(dwx_ref, first, lax.dot_general(xb, dgxb, TN, preferred_element_type=F32))
        dxc = (dxc + lax.dot_general(dgab, wa_ref[...], NT, preferred_element_type=F32)
               + lax.dot_general(dgxb, wx_ref[...], NT, preferred_element_type=F32))

        _acc(dcb_ref, first, jnp.sum(dxc, axis=0, keepdims=True))
        r8 = _row_iota((8, W_A))
        dcw = jnp.zeros((8, W_A), F32)
        for j in range(LRU_K):
            tap = jnp.sum(dxc * xs_ref[pl.ds(pad - (LRU_K - 1) + j, tc), :], axis=0, keepdims=True)
            dcw = dcw + jnp.where(r8 == j, tap, 0.0)
        _acc(dcw_ref, first, dcw)
        ds_ref[0:tc, :] = dxc
        ds_ref[tc:tc + pad, :] = nx_ref[...]
        dlx = None
        for j in range(LRU_K):
            term = cw_ref[j:j + 1, :] * ds_ref[pl.ds(LRU_K - 1 - j, tc), :]
            dlx = term if dlx is None else dlx + term
        dp_ref[:, 0:W_A] = dlx
        nx_ref[...] = dxc[0:pad, :]

    rev = lambda c: pl.BlockSpec((tc, W_A), lambda t, c=c: (nc - 1 - t, c))
    prev = lambda c: pl.BlockSpec((tc, W_A), lambda t, c=c: (jnp.maximum(nc - 2 - t, 0), c))
    full = lambda a: pl.BlockSpec(a.shape, lambda t: (0,) * a.ndim)
    params = [cw, cb, wa, ba, wx, bx, lam, gg]
    vec = SDS((1, W_A), F32)
    sq = SDS((W_A, W_A), F32)
    outs = [SDS((s, 2 * W_A), F32), SDS((8, W_A), F32), vec, sq, vec, sq, vec, vec, vec]
    return pl.pallas_call(
        body, name="lru_bwd", grid=(nc,),
        in_specs=[rev(0), rev(0), prev(0), rev(1), rev(0), prev(0)] + [full(a) for a in params],
        out_specs=[pl.BlockSpec((tc, 2 * W_A), lambda t: (nc - 1 - t, 0))]
        + [pl.BlockSpec(o.shape, lambda t: (0, 0)) for o in outs[1:]],
        out_shape=outs,
        scratch_shapes=[pltpu.VMEM((tc + pad, W_A), F32), pltpu.VMEM((tc + pad, W_A), F32),
                        pltpu.VMEM((1, W_A), F32), pltpu.VMEM((pad, W_A), F32)],
        compiler_params=_cp("arbitrary"),
    )(dy, proj, proj, proj, h, h, *params)


def _attn_stack(qa, qb, kvh):
    lane = lax.broadcasted_iota(jnp.int32, qa.shape, 1)
    keep = (lane >= HD) if kvh == 1 else (lane < HD)
    parts = []
    for tile in (qa, qb):
        for half in (0, 1):
            y = tile if half == kvh else pltpu.roll(tile, HD, axis=1)
            parts.append(jnp.where(keep, y, 0.0))
    return jnp.concatenate(parts, axis=0)


def _attn_unstack(o, kvh):
    lane = lax.broadcasted_iota(jnp.int32, (BLK, 2 * HD), 1)
    tiles = []
    for t in range(2):
        halves = []
        for half in (0, 1):
            blk = o[(2 * t + half) * BLK:(2 * t + half + 1) * BLK, :]
            halves.append(blk if half == kvh else pltpu.roll(blk, HD, axis=1))
        tiles.append(jnp.where(lane < HD, halves[0], halves[1]))
    return tiles


def _attn_mask(n):
    qi = lax.broadcasted_iota(jnp.int32, (BLK, 2 * BLK), 0)
    kj = lax.broadcasted_iota(jnp.int32, (BLK, 2 * BLK), 1)
    rel = BLK + qi - kj
    return (rel >= 0) & (rel < BLK) & ((n - 1) * BLK + kj >= 0)


def _attn_probs(qs, kw, mask, sink_ref, kvh):
    sc = lax.dot_general(qs.astype(BF16), kw, NT, preferred_element_type=F32) * SCALE
    ps, psinks = [], []
    for rr in range(4):
        sk = sink_ref[4 * kvh + rr:4 * kvh + rr + 1, 0:1]
        sh = jnp.where(mask, sc[rr * BLK:(rr + 1) * BLK, :], NEG_BIG)
        m = jnp.maximum(jnp.max(sh, axis=-1, keepdims=True), sk)
        e = jnp.exp(sh - m)
        es = jnp.exp(sk - m)
        z = jnp.sum(e, axis=-1, keepdims=True) + es
        ps.append(e / z)
        psinks.append(es / z)
    return ps, psinks


def _attn_fwd(proj, sinks8, gg):
    s = proj.shape[0]

    def body(q_ref, kc_ref, kp_ref, vc_ref, vp_ref, sink_ref, gg_ref, yn_ref, ob_ref):
        n = pl.program_id(0)
        mask = _attn_mask(n)
        kw = jnp.concatenate([kp_ref[...], kc_ref[...]], axis=0).astype(BF16)
        vw = jnp.concatenate([vp_ref[...], vc_ref[...]], axis=0).astype(BF16)
        for kvh in range(2):
            qa = q_ref[:, 256 * kvh:256 * kvh + 128]
            qb = q_ref[:, 256 * kvh + 128:256 * kvh + 256]
            ps, _ = _attn_probs(_attn_stack(qa, qb, kvh), kw, mask, sink_ref, kvh)
            o = jnp.dot(jnp.concatenate(ps, axis=0).astype(BF16), vw, preferred_element_type=F32)
            ta, tb = _attn_unstack(o, kvh)
            ob_ref[:, 256 * kvh:256 * kvh + 128] = ta
            ob_ref[:, 256 * kvh + 128:256 * kvh + 256] = tb
        ob = ob_ref[...]
        yn_ref[...] = (ob * _rsq(ob, NORM_EPS) * gg_ref[...]).astype(BF16)

    kv = lambda c, back: pl.BlockSpec((BLK, 128), lambda n, c=c, back=back: (jnp.maximum(n - back, 0), c))
    out = pl.BlockSpec((BLK, W_B), lambda n: (n, 0))
    return pl.pallas_call(
        body, name="attn_fwd", grid=(s // BLK,),
        in_specs=[pl.BlockSpec((BLK, W_B), lambda n: (n, 1)), kv(8, 0), kv(8, 1), kv(9, 0), kv(9, 1),
                  pl.BlockSpec((8, 128), lambda n: (0, 0)), pl.BlockSpec((1, W_B), lambda n: (0, 0))],
        out_specs=[out, out], out_shape=[SDS((s, W_B), BF16), SDS((s, W_B), F32)],
        compiler_params=_cp("parallel"),
    )(proj, proj, proj, proj, proj, sinks8, gg)


def _attn_bwd(dy, proj, ob, sinks8, gg):
    s = proj.shape[0]

    def body(dya_ref, dyb_ref, q_ref, kc_ref, kp_ref, vc_ref, vp_ref, ob_ref, sink_ref, gg_ref,
             dq_ref, dcur_ref, dprev_ref, dsink_ref, dgg_ref):
        n = pl.program_id(0)
        first = n == 0
        mask = _attn_mask(n)
        kw = jnp.concatenate([kp_ref[...], kc_ref[...]], axis=0).astype(BF16)
        vw = jnp.concatenate([vp_ref[...], vc_ref[...]], axis=0).astype(BF16)
        dyn = jnp.concatenate([dya_ref[...], dyb_ref[...]], axis=1)
        dob, dggr = _rms_bwd_rows(ob_ref[...], gg_ref[...], dyn)
        _acc(dgg_ref, first, jnp.sum(dggr, axis=0, keepdims=True))
        r8 = _row_iota((8, 128))
        dsk = jnp.zeros((8, 128), F32)
        dkw = jnp.zeros((2 * BLK, 128), F32)
        dvw = jnp.zeros((2 * BLK, 128), F32)
        for kvh in range(2):
            qs = _attn_stack(q_ref[:, 256 * kvh:256 * kvh + 128], q_ref[:, 256 * kvh + 128:256 * kvh + 256], kvh)
            ps, psinks = _attn_probs(qs, kw, mask, sink_ref, kvh)
            dos = _attn_stack(dob[:, 256 * kvh:256 * kvh + 128], dob[:, 256 * kvh + 128:256 * kvh + 256], kvh)
            dosb = dos.astype(BF16)
            dp = lax.dot_general(dosb, vw, NT, preferred_element_type=F32)
            dss = []
            for rr in range(4):
                dpr = dp[rr * BLK:(rr + 1) * BLK, :]
                dd = jnp.sum(ps[rr] * dpr, axis=-1, keepdims=True)
                dss.append(ps[rr] * (dpr - dd) * SCALE)
                tot = jnp.sum(-psinks[rr] * dd, axis=0, keepdims=True)
                dsk = dsk + jnp.where(r8 == 4 * kvh + rr, tot, 0.0)
            dsb = jnp.concatenate(dss, axis=0).astype(BF16)
            pb = jnp.concatenate(ps, axis=0).astype(BF16)
            dqs = jnp.dot(dsb, kw, preferred_element_type=F32)
            ta, tb = _attn_unstack(dqs, kvh)
            dq_ref[:, 256 * kvh:256 * kvh + 128] = ta
            dq_ref[:, 256 * kvh + 128:256 * kvh + 256] = tb
            dkw = dkw + lax.dot_general(dsb, qs.astype(BF16), TN, preferred_element_type=F32)
            dvw = dvw + lax.dot_general(pb, dosb, TN, preferred_element_type=F32)
        _acc(dsink_ref, first, dsk)
        dprev_ref[:, 0:128] = dkw[0:BLK, :]
        dprev_ref[:, 128:256] = dvw[0:BLK, :]
        dcur_ref[:, 0:128] = dkw[BLK:2 * BLK, :]
        dcur_ref[:, 128:256] = dvw[BLK:2 * BLK, :]

    kv = lambda c, back: pl.BlockSpec((BLK, 128), lambda n, c=c, back=back: (jnp.maximum(n - back, 0), c))
    wide = pl.BlockSpec((BLK, W_B), lambda n: (n, 0))
    half = pl.BlockSpec((BLK, 256), lambda n: (n, 0))
    return pl.pallas_call(
        body, name="attn_bwd", grid=(s // BLK,),
        in_specs=[pl.BlockSpec((BLK, 256), lambda n: (n, 1)), pl.BlockSpec((BLK, 256), lambda n: (n, 2)),
                  pl.BlockSpec((BLK, W_B), lambda n: (n, 1)), kv(8, 0), kv(8, 1), kv(9, 0), kv(9, 1), wide,
                  pl.BlockSpec((8, 128), lambda n: (0, 0)), pl.BlockSpec((1, W_B), lambda n: (0, 0))],
        out_specs=[wide, half, half, pl.BlockSpec((8, 128), lambda n: (0, 0)), pl.BlockSpec((1, W_B), lambda n: (0, 0))],
        out_shape=[SDS((s, W_B), F32), SDS((s, 256), F32), SDS((s, 256), F32), SDS((8, 128), F32), SDS((1, W_B), F32)],
        compiler_params=_cp("arbitrary"),
    )(dy, dy, proj, proj, proj, proj, proj, ob, sinks8, gg)


def _ln_parts(y1, eps=LN_EPS):
    mu = jnp.mean(y1, axis=-1, keepdims=True)
    xc = y1 - mu
    rstd = lax.rsqrt(jnp.mean(xc * xc, axis=-1, keepdims=True) + eps)
    return xc * rstd, rstd


def _conf_fwd(proj, cw, cb, lg, lb, gg, tc):
    s = proj.shape[0]
    pad = 32

    def body(ac_ref, gc_ref, ap_ref, gp_ref, cw_ref, cb_ref, lg_ref, lb_ref, gg_ref, yn_ref, y1_ref, ys_ref):
        i = pl.program_id(0)
        tail = ap_ref[tc - pad:tc, :] * _sig(gp_ref[tc - pad:tc, :])
        ys_ref[0:pad, :] = jnp.where(i > 0, tail, 0.0)
        ys_ref[pad:pad + tc, :] = ac_ref[...] * _sig(gc_ref[...])
        y1 = _conv_taps(ys_ref, cw_ref, CONV_K, pad, tc) + cb_ref[...]
        y1_ref[...] = y1
        xh, _ = _ln_parts(y1)
        yl = xh * lg_ref[...] + lb_ref[...]
        yc = yl * _sig(yl)
        yn_ref[...] = (yc * _rsq(yc, NORM_EPS) * gg_ref[...]).astype(BF16)

    cur = lambda c: pl.BlockSpec((tc, W_C), lambda i, c=c: (i, c))
    prev = lambda c: pl.BlockSpec((tc, W_C), lambda i, c=c: (jnp.maximum(i - 1, 0), c))
    full = lambda a: pl.BlockSpec(a.shape, lambda i: (0,) * a.ndim)
    params = [cw, cb, lg, lb, gg]
    out = pl.BlockSpec((tc, W_C), lambda i: (i, 0))
    return pl.pallas_call(
        body, name="conf_fwd", grid=(s // tc,),
        in_specs=[cur(5), cur(6), prev(5), prev(6)] + [full(a) for a in params],
        out_specs=[out, out], out_shape=[SDS((s, W_C), BF16), SDS((s, W_C), F32)],
        scratch_shapes=[pltpu.VMEM((tc + pad, W_C), F32)],
        compiler_params=_cp("parallel"),
    )(proj, proj, proj, proj, *params)


def _conf_bwd(dy, proj, y1, cw, cb, lg, lb, gg, tc):
    s = proj.shape[0]
    nc = s // tc
    pad = 32

    def body(dy_ref, ac_ref, gc_ref, ap_ref, gp_ref, y1_ref, cw_ref, cb_ref, lg_ref, lb_ref, gg_ref,
             dp_ref, dcw_ref, dcb_ref, dlg_ref, dlb_ref, dgg_ref, ys_ref, ds_ref, nx_ref):
        step = pl.program_id(0)
        i = nc - 1 - step
        first = step == 0

        @pl.when(first)
        def _():
            nx_ref[...] = jnp.zeros_like(nx_ref)

        a = ac_ref[...]
        sg = _sig(gc_ref[...])
        tail = ap_ref[tc - pad:tc, :] * _sig(gp_ref[tc - pad:tc, :])
        ys_ref[0:pad, :] = jnp.where(i > 0, tail, 0.0)
        ys_ref[pad:pad + tc, :] = a * sg
        xh, rstd = _ln_parts(y1_ref[...])
        yl = xh * lg_ref[...] + lb_ref[...]
        sl = _sig(yl)
        yc = yl * sl
        dyc, dggr = _rms_bwd_rows(yc, gg_ref[...], dy_ref[...])
        _acc(dgg_ref, first, jnp.sum(dggr, axis=0, keepdims=True))
        dyl = dyc * sl * (1.0 + yl * (1.0 - sl))
        _acc(dlg_ref, first, jnp.sum(dyl * xh, axis=0, keepdims=True))
        _acc(dlb_ref, first, jnp.sum(dyl, axis=0, keepdims=True))
        dxh = dyl * lg_ref[...]
        dy1 = rstd * (dxh - jnp.mean(dxh, axis=-1, keepdims=True) - xh * jnp.mean(dxh * xh, axis=-1, keepdims=True))
        _acc(dcb_ref, first, jnp.sum(dy1, axis=0, keepdims=True))
        r32 = _row_iota((32, W_C))
        dcw = jnp.zeros((32, W_C), F32)
        for j in range(CONV_K):
            tap = jnp.sum(dy1 * ys_ref[pl.ds(pad - (CONV_K - 1) + j, tc), :], axis=0, keepdims=True)
            dcw = dcw + jnp.where(r32 == j, tap, 0.0)
        _acc(dcw_ref, first, dcw)
        ds_ref[0:tc, :] = dy1
        ds_ref[tc:tc + pad, :] = nx_ref[...]
        dy0 = None
        for j in range(CONV_K):
            term = cw_ref[j:j + 1, :] * ds_ref[pl.ds(CONV_K - 1 - j, tc), :]
            dy0 = term if dy0 is None else dy0 + term
        dp_ref[:, 0:W_C] = dy0 * sg
        dp_ref[:, W_C:2 * W_C] = dy0 * a * sg * (1.0 - sg)
        nx_ref[...] = dy1[0:pad, :]

    rev = lambda c: pl.BlockSpec((tc, W_C), lambda t, c=c: (nc - 1 - t, c))
    prev = lambda c: pl.BlockSpec((tc, W_C), lambda t, c=c: (jnp.maximum(nc - 2 - t, 0), c))
    full = lambda a: pl.BlockSpec(a.shape, lambda t: (0,) * a.ndim)
    params = [cw, cb, lg, lb, gg]
    vec = SDS((1, W_C), F32)
    outs = [SDS((s, 2 * W_C), F32), SDS((32, W_C), F32), vec, vec, vec, vec]
    return pl.pallas_call(
        body, name="conf_bwd", grid=(nc,),
        in_specs=[rev(3), rev(5), rev(6), prev(5), prev(6), rev(0)] + [full(a) for a in params],
        out_specs=[pl.BlockSpec((tc, 2 * W_C), lambda t: (nc - 1 - t, 0))]
        + [pl.BlockSpec(o.shape, lambda t: (0, 0)) for o in outs[1:]],
        out_shape=outs,
        scratch_shapes=[pltpu.VMEM((tc + pad, W_C), F32), pltpu.VMEM((tc + pad, W_C), F32), pltpu.VMEM((pad, W_C), F32)],
        compiler_params=_cp("arbitrary"),
    )(dy, proj, proj, proj, proj, y1, *params)


def _assemble_dproj(dlru, dq, dcur, dprev, dconf):
    s = dq.shape[0]
    nb = s // BLK

    def body(dl_ref, dq_ref, dc_ref, dn_ref, df_ref, o_ref):
        n = pl.program_id(0)
        o_ref[:, 0:512] = dl_ref[...].astype(BF16)
        o_ref[:, 512:1024] = dq_ref[...].astype(BF16)
        o_ref[:, 1024:1280] = (dc_ref[...] + jnp.where(n < nb - 1, dn_ref[...], 0.0)).astype(BF16)
        o_ref[:, 1280:1792] = df_ref[...].astype(BF16)

    wide = pl.BlockSpec((BLK, 512), lambda n: (n, 0))
    return pl.pallas_call(
        body, name="assemble_dproj", grid=(nb,),
        in_specs=[wide, wide, pl.BlockSpec((BLK, 256), lambda n: (n, 0)),
                  pl.BlockSpec((BLK, 256), lambda n: (jnp.minimum(n + 1, nb - 1), 0)), wide],
        out_specs=pl.BlockSpec((BLK, P_IN), lambda n: (n, 0)), out_shape=SDS((s, P_IN), BF16),
        compiler_params=_cp("parallel"),
    )(dlru, dq, dcur, dprev, dconf)


def _loss_grad(y, t, tm):
    s = y.shape[0]

    def body(y_ref, t_ref, dy_ref, l_ref):
        err = y_ref[...] - t_ref[...]
        dy_ref[...] = err * (1.0 / D)
        _acc(l_ref, pl.program_id(0) == 0, jnp.sum(err * err, axis=0, keepdims=True))

    row = pl.BlockSpec((tm, D), lambda i: (i, 0))
    return pl.pallas_call(
        body, name="loss_grad", grid=(s // tm,), in_specs=[row, row],
        out_specs=[row, pl.BlockSpec((1, D), lambda i: (0, 0))],
        out_shape=[SDS((s, D), F32), SDS((1, D), F32)], compiler_params=_cp("arbitrary"),
    )(y, t)


def _block_diag(w):
    rows = [jnp.concatenate([w[h] if k == h else jnp.zeros((64, 64), w.dtype) for k in range(4)], axis=1) for h in range(4)]
    return jnp.concatenate(rows, axis=0)


def _diag_blocks(m):
    return jnp.stack([m[64 * h:64 * (h + 1), 64 * h:64 * (h + 1)] for h in range(4)])


def _layer_params(small, l):
    v = lambda name: small[name][l].reshape(1, -1)
    gg = small["group_g"][l]
    return dict(
        ffn1_pre=v("ffn1_pre_g"), ffn1_post=v("ffn1_post_g"), mix_pre=v("mix_pre_g"), mix_post=v("mix_post_g"),
        ffn2_pre=v("ffn2_pre_g"), ffn2_post=v("ffn2_post_g"), lru_cb=v("lru_conv_b"),
        wa=_block_diag(small["lru_w_a"][l]).astype(BF16), ba=v("lru_b_a"),
        wx=_block_diag(small["lru_w_x"][l]).astype(BF16), bx=v("lru_b_x"), lam=v("lru_lambda"),
        sinks8=jnp.broadcast_to(small["attn_sinks"][l][:, None], (NQ, 128)),
        conv_b=v("conv_b"), ln_g=v("conv_ln_g"), ln_b=v("conv_ln_b"),
        gg_a=gg[0:W_A].reshape(1, -1), gg_b=gg[W_A:W_A + W_B].reshape(1, -1), gg_c=gg[W_A + W_B:].reshape(1, -1),
    )


def _forward_layer(x, weights, p, tm, tc, deps=()):
    big = dict(weights("ffn1", x))
    p = dict(p)
    sv = dict(x0=x)
    h1, g1, u1, a1 = _ffn_up(x, p["ffn1_pre"], big["ffn1_w_gu"], 0, tm, deps)
    z1, x = _mm_rms_res(a1, big["ffn1_w_down"], 0, x, p["ffn1_post"], 0.5, tm, FH, "ffn_down")
    sv.update(h1=h1, g1=g1, u1=u1, a1=a1, z1=z1, x1=x)
    big.update(weights("mix", x))
    p.update(lru_cw=big.pop("lru_conv_w"), conv_w=big.pop("conv_w"))
    hn, proj = _proj(x, p["mix_pre"], big["w_in"], 0, tm)
    yn_a, hl = _lru_fwd(proj, p["lru_cw"], p["lru_cb"], p["wa"], p["ba"], p["wx"], p["bx"], p["lam"], p["gg_a"], tc)
    yn_b, ob = _attn_fwd(proj, p["sinks8"], p["gg_b"])
    yn_c, y1 = _conf_fwd(proj, p["conv_w"], p["conv_b"], p["ln_g"], p["ln_b"], p["gg_c"], tc)
    ycat = jnp.concatenate([yn_a, yn_b, yn_c], axis=1)
    zo, x = _mm_rms_res(ycat, big["w_out"], 0, x, p["mix_post"], 1.0, tm, D, "mix_out")
    sv.update(hn=hn, proj=proj, hl=hl, ob=ob, y1=y1, ycat=ycat, zo=zo, x2=x)
    big.update(weights("ffn2", x))
    h2, g2, u2, a2 = _ffn_up(x, p["ffn2_pre"], big["ffn2_w_gu"], 0, tm)
    z2, x = _mm_rms_res(a2, big["ffn2_w_down"], 0, x, p["ffn2_post"], 0.5, tm, FH, "ffn_down")
    sv.update(h2=h2, g2=g2, u2=u2, a2=a2, z2=z2, p=p, big=big)
    return x, sv


def _grad_buffers():
    empty = lambda *shape: lax.empty(shape, F32)
    return dict(ffn1_w_gu=empty(1, NSHARD, D, FH), ffn2_w_gu=empty(1, NSHARD, D, FH), ffn1_w_down=empty(1, 1, DFF, D),
                ffn2_w_down=empty(1, 1, DFF, D), w_in=empty(1, 1, D, P_IN), w_out=empty(1, 1, D, D))


def _backward_layer(dx, sv, bufs, tm, tc, stage):
    p, big = sv["p"], sv["big"]
    gr = {}

    def ffn_bwd(dx, which, xin, h, g, u, a, z, pre, post, deps):
        dz, dpost = _rms_bwd(dx, z, post, 0.5, tm, "ffn_post_bwd", deps)
        dg, du = _ffn_bwd_mid(dz, big[which + "_w_down"], 0, g, u, tm)
        bufs[which + "_w_down"] = _mm_tn_into(bufs[which + "_w_down"], a, dz, 0, 0, FH, D, tm, "dw_down")
        bufs[which + "_w_gu"] = _mm_tn_into(bufs[which + "_w_gu"], h, dg, 0, 0, D, FH, tm, "dw_gate")
        bufs[which + "_w_gu"] = _mm_tn_into(bufs[which + "_w_gu"], h, du, 0, 2, D, FH, tm, "dw_up")
        dxn, dpre = _ffn_bwd_dh(dg, du, big[which + "_w_gu"], 0, xin, pre, dx, tm)
        return dxn, dpre, dpost

    dx, gr["ffn2_pre_g"], gr["ffn2_post_g"] = ffn_bwd(dx, "ffn2", sv["x2"], sv["h2"], sv["g2"], sv["u2"], sv["a2"],
                                                      sv["z2"], p["ffn2_pre"], p["ffn2_post"], stage({}, dx))
    done = {n: bufs[n] for n in ("ffn2_w_gu", "ffn2_w_down")}
    do, gr["mix_post_g"] = _rms_bwd(dx, sv["zo"], p["mix_post"], 1.0, tm, "mix_post_bwd", stage(done, dx))
    bufs["w_out"] = _mm_tn_into(bufs["w_out"], sv["ycat"], do, 0, 0, D, D, tm, "dw_out")
    dy = _mm_nt(do, big["w_out"], 0, tm, "mix_dy")
    proj = sv["proj"]
    (dlru, dcw, gr["lru_conv_b"], dwa, gr["lru_b_a"], dwx, gr["lru_b_x"], gr["lru_lambda"], dgg_a) = _lru_bwd(
        dy, proj, sv["hl"], p["lru_cw"], p["lru_cb"], p["wa"], p["ba"], p["wx"], p["bx"], p["lam"], p["gg_a"], tc)
    dq, dcur, dprev, dsk, dgg_b = _attn_bwd(dy, proj, sv["ob"], p["sinks8"], p["gg_b"])
    dconf, dconvw, gr["conv_b"], gr["conv_ln_g"], gr["conv_ln_b"], dgg_c = _conf_bwd(
        dy, proj, sv["y1"], p["conv_w"], p["conv_b"], p["ln_g"], p["ln_b"], p["gg_c"], tc)
    dproj = _assemble_dproj(dlru, dq, dcur, dprev, dconf)
    bufs["w_in"] = _mm_tn_into(bufs["w_in"], sv["hn"], dproj, 0, 0, D, P_IN, tm, "dw_in")
    dx, gr["mix_pre_g"] = _mm_nt_rmsbwd(dproj, big["w_in"], 0, sv["x1"], p["mix_pre"], dx, tm)
    gr["lru_conv_w"] = dcw[0:LRU_K]
    gr["lru_w_a"] = _diag_blocks(dwa)
    gr["lru_w_x"] = _diag_blocks(dwx)
    gr["attn_sinks"] = dsk[:, 0]
    gr["conv_w"] = dconvw[0:CONV_K]
    gr["group_g"] = jnp.concatenate([dgg_a, dgg_b, dgg_c], axis=1)
    dx, gr["ffn1_pre_g"], gr["ffn1_post_g"] = ffn_bwd(dx, "ffn1", sv["x0"], sv["h1"], sv["g1"], sv["u1"], sv["a1"],
                                                      sv["z1"], p["ffn1_pre"], p["ffn1_post"],
                                                      stage({n: bufs[n] for n in ("w_in", "w_out")}, dx))
    return dx, gr


def _tiles(s):
    return min(512, s), min(512, s // 2)


HBM_SPEC = pl.BlockSpec(memory_space=pltpu.HBM)
SEM_SPEC = pl.BlockSpec(memory_space=pltpu.SEMAPHORE)
EFFECT = pltpu.SideEffectType.DATAFLOW_SIDE_EFFECTING


def _place():
    x, y, c = lax.axis_index("x"), lax.axis_index("y"), lax.axis_index("c")
    return x, y, c, [(1 - x, y), (x, 1 - y), (1 - x, 1 - y)]


def _rcopy(src, dst, send_sems, recv_sems, k, to):
    return pltpu.make_async_remote_copy(src_ref=src, dst_ref=dst, send_sem=send_sems.at[k], recv_sem=recv_sems.at[k],
                                        device_id=to, device_id_type=MESH)


def _half(rows, which):
    return pl.ds(which * (rows // 2), rows // 2)


def _place_shard(w, l, p_idx, dtype):
    _, rows, cols = w.shape
    tr = _rows_per_block(rows, cols, 16) if rows % 16 == 0 else rows

    def body(p_ref, buf_ref, w_ref, o_ref):
        o_ref[...] = w_ref[...].astype(dtype)

    spec = pltpu.PrefetchScalarGridSpec(
        num_scalar_prefetch=1, grid=(rows // tr,),
        in_specs=[ANY, pl.BlockSpec((None, tr, cols), lambda i, pr: (l, i, 0))],
        out_specs=pl.BlockSpec((None, None, tr, cols), lambda i, pr: (0, pr[0], i, 0)))
    shape = (1, NSHARD, rows, cols)
    return pl.pallas_call(body, name="place_shard", grid_spec=spec, out_shape=SDS(shape, dtype),
                          input_output_aliases={1: 0}, compiler_params=_cp("parallel"),
                          )(p_idx, lax.empty(shape, dtype), w)


def _gather_two_level(bufs, n_halved):
    n = len(bufs)

    def body(*refs):
        outs = refs[n:2 * n]
        send_sems, recv_sems = refs[2 * n:]
        x, y, c, chips = _place()
        p = 2 * x + y
        me, sibling = (x, y, c), (x, y, 1 - c)

        def blk(a, q, half):
            return outs[a].at[0, q, _half(outs[a].shape[2], half)] if a < n_halved else outs[a].at[0, q]

        def cp(a, k, q, half, to):
            return _rcopy(blk(a, q, half), blk(a, q, half), send_sems, recv_sems, 6 * a + k, to)

        first = [cp(a, j, p, c, (*chip, c)) for a in range(n) for j, chip in enumerate(chips)]
        for d in first:
            d.start()
        passed = []
        for a in range(n):
            for j, chip in enumerate(chips):
                q = 2 * chip[0] + chip[1]
                cp(a, j, q, c, me).wait_recv()
                if a < n_halved:
                    passed.append(cp(a, 3 + j, q, c, sibling))
                    passed[-1].start()
        for a in range(n_halved):
            for j, chip in enumerate(chips):
                cp(a, 3 + j, 2 * chip[0] + chip[1], 1 - c, me).wait_recv()
        for d in first + passed:
            d.wait_send()

    return pl.pallas_call(
        body, name="gather_layer0", in_specs=[ANY] * n, out_specs=[ANY] * n,
        out_shape=[SDS(b.shape, b.dtype) for b in bufs], input_output_aliases={a: a for a in range(n)},
        scratch_shapes=[pltpu.SemaphoreType.DMA((6 * n,)), pltpu.SemaphoreType.DMA((6 * n,))],
    )(*bufs)


def _run_plans(plans, refs, send_sems, recv_sems):
    cps, b0, s0 = [], 0, 0
    for plan, nb, ns in plans:
        cps += plan(refs[b0:b0 + nb], send_sems, recv_sems, s0)
        b0, s0 = b0 + nb, s0 + ns
    return cps


def _exchange(name, bufs, plans):
    n = len(bufs)
    nsem = sum(ns for _, _, ns in plans)

    def body(*refs):
        cps = _run_plans(plans, refs[n:2 * n], refs[2 * n], refs[2 * n + 1])
        for cp in cps:
            cp.start()
        for cp in cps:
            cp.wait()

    return pl.pallas_call(
        body, name=name, in_specs=[ANY] * n, out_specs=[ANY] * n, out_shape=[SDS(b.shape, b.dtype) for b in bufs],
        input_output_aliases={a: a for a in range(n)},
        scratch_shapes=[pltpu.SemaphoreType.DMA((nsem,)), pltpu.SemaphoreType.DMA((nsem,))],
    )(*bufs)


def _exchange_start(name, bufs, plans, deps=()):
    n = len(bufs)
    nsem = sum(ns for _, _, ns in plans)
    deps = list(deps)
    first_out = n + len(deps)

    def body(*refs):
        for cp in _run_plans(plans, refs[:n], refs[first_out], refs[first_out + 1]):
            cp.start()
        token = refs[first_out + 2 + n]
        token[...] = jnp.zeros_like(token)

    outs = pl.pallas_call(
        body, name=name,
        out_shape=(pltpu.SemaphoreType.DMA((nsem,)), pltpu.SemaphoreType.DMA((nsem,)),
                   *[pltpu.HBM(b.shape, b.dtype) for b in bufs], SDS((8, 128), F32)),
        in_specs=[HBM_SPEC] * n + [ANY] * len(deps),
        out_specs=(SEM_SPEC, SEM_SPEC, *[HBM_SPEC] * n, pl.BlockSpec(memory_space=pltpu.VMEM)),
        input_output_aliases={a: 2 + a for a in range(n)},
        compiler_params=pltpu.CompilerParams(has_side_effects=EFFECT),
    )(*[pltpu.with_memory_space_constraint(b, pltpu.HBM) for b in bufs], *deps)
    return outs[0], outs[1], list(outs[2:2 + n]), outs[2 + n]


def _exchange_wait(name, send_sems, recv_sems, bufs, plans, after):
    n = len(bufs)

    def body(*refs):
        for cp in _run_plans(plans, refs[:n], refs[n], refs[n + 1]):
            cp.wait_send()
            cp.wait_recv()

    return pl.pallas_call(
        body, name=name, out_shape=[pltpu.HBM(b.shape, b.dtype) for b in bufs],
        in_specs=[HBM_SPEC] * n + [SEM_SPEC, SEM_SPEC, ANY], out_specs=[HBM_SPEC] * n,
        input_output_aliases={a: a for a in range(n)},
        compiler_params=pltpu.CompilerParams(has_side_effects=EFFECT),
    )(*bufs, send_sems, recv_sems, after)


def _plan_gather(refs, send_sems, recv_sems, base):
    x, y, c, chips = _place()
    p = 2 * x + y
    return [_rcopy(r.at[0, p], r.at[0, p], send_sems, recv_sems, base + 3 * a + j, (*chip, c))
            for a, r in enumerate(refs) for j, chip in enumerate(chips)]


def _plan_pair_exchange(refs, send_sems, recv_sems, base):
    x, y, c, _ = _place()
    n = len(refs) // 2
    return [_rcopy(refs[a].at[:, _half(refs[a].shape[1], 1 - c)], refs[n + a], send_sems, recv_sems, base + a,
                   (x, y, 1 - c)) for a in range(n)]


def _plan_chip_exchange(refs, send_sems, recv_sems, base):
    x, y, c, chips = _place()
    n = len(refs) // 2
    return [_rcopy(refs[a].at[2 * chip[0] + chip[1]], refs[n + a].at[j], send_sems, recv_sems, base + 3 * a + j,
                   (*chip, c)) for a in range(n) for j, chip in enumerate(chips)]


def _plan_pair_share(refs, send_sems, recv_sems, base):
    x, y, c, _ = _place()
    return [_rcopy(r.at[_half(r.shape[0], c)], r.at[_half(r.shape[0], c)], send_sems, recv_sems, base + a,
                   (x, y, 1 - c)) for a, r in enumerate(refs)]


def _allreduce_small(buf):
    rows = buf.shape[0]

    def body(in_ref, out_ref, gather_ref, send_sems, recv_sems):
        x, y, c, _ = _place()
        me = 4 * x + 2 * y + c
        gather_ref[me] = in_ref[...]
        cps, slots = [], []
        for m in range(1, NDEV):
            px = 1 - x if m & 4 else x
            py = 1 - y if m & 2 else y
            pc = 1 - c if m & 1 else c
            cps.append(_rcopy(in_ref, gather_ref.at[me], send_sems, recv_sems, m - 1, (px, py, pc)))
            slots.append(4 * px + 2 * py + pc)
        for cp in cps:
            cp.start()
        for m in range(1, NDEV):
            _rcopy(in_ref, gather_ref.at[slots[m - 1]], send_sems, recv_sems, m - 1, (x, y, c)).wait_recv()
        for cp in cps:
            cp.wait_send()
        total = gather_ref[0]
        for dev in range(1, NDEV):
            total = total + gather_ref[dev]
        out_ref[...] = total

    vm = pl.BlockSpec(memory_space=pltpu.VMEM)
    return pl.pallas_call(
        body, name="allreduce_small", in_specs=[vm], out_specs=vm, out_shape=SDS(buf.shape, F32),
        scratch_shapes=[pltpu.VMEM((NDEV, rows, 128), F32), pltpu.SemaphoreType.DMA((NDEV - 1,)),
                        pltpu.SemaphoreType.DMA((NDEV - 1,))],
        compiler_params=pltpu.CompilerParams(vmem_limit_bytes=VMEM_LIMIT),
    )(buf)


BLOCK_ELEMS = 256 * 1024


def _rows_per_block(rows, cols, mult):
    best = None
    for tr in range(mult, rows + 1, mult):
        if rows % tr == 0 and tr * cols <= BLOCK_ELEMS:
            best = tr
    assert best is not None, (rows, cols)
    return best


def _pair_sum(g, r, c_idx):
    nq, rows, cols = g.shape
    half = rows // 2
    tr = _rows_per_block(half, cols, 16)
    nb = half // tr

    def body(c_ref, g_ref, r_ref, t_ref):
        t_ref[...] = (g_ref[...] + r_ref[...]).astype(BF16)

    blk = pl.BlockSpec((None, tr, cols), lambda q, i, cr: (q, i, 0))
    spec = pltpu.PrefetchScalarGridSpec(
        num_scalar_prefetch=1, grid=(nq, nb),
        in_specs=[pl.BlockSpec((None, tr, cols), lambda q, i, cr: (q, cr[0] * nb + i, 0)), blk], out_specs=blk)
    return pl.pallas_call(body, name="grad_pair_sum", grid_spec=spec, out_shape=SDS((nq, half, cols), BF16),
                          compiler_params=_cp("parallel", "parallel"))(c_idx, g, r)


def _chip_sum(g, r, rr, cp_idx):
    _, rows, cols = g.shape
    half = rows // 2
    tr = _rows_per_block(half, cols, 16)
    nb = half // tr

    def body(cp_ref, buf_ref, g_ref, r_ref, rr_ref, o_ref):
        o_ref[...] = ((g_ref[...] + r_ref[...]) + rr_ref[0].astype(F32) + rr_ref[1].astype(F32) + rr_ref[2].astype(F32))

    spec = pltpu.PrefetchScalarGridSpec(
        num_scalar_prefetch=1, grid=(nb,),
        in_specs=[ANY, pl.BlockSpec((None, tr, cols), lambda i, cp: (cp[1], cp[0] * nb + i, 0)),
                  pl.BlockSpec((None, tr, cols), lambda i, cp: (cp[1], i, 0)),
                  pl.BlockSpec((3, tr, cols), lambda i, cp: (0, i, 0))],
        out_specs=pl.BlockSpec((tr, cols), lambda i, cp: (cp[0] * nb + i, 0)))
    return pl.pallas_call(body, name="grad_chip_sum", grid_spec=spec, out_shape=SDS((rows, cols), F32),
                          input_output_aliases={1: 0}, compiler_params=_cp("parallel"),
                          )(cp_idx, lax.empty((rows, cols), F32), g, r, rr)


def _adamw_math(w, g, m, v):
    mn = ADAM_B1 * m + (1.0 - ADAM_B1) * g
    vn = ADAM_B2 * v + (1.0 - ADAM_B2) * (g * g)
    m_hat = mn / (1.0 - ADAM_B1 ** ADAM_STEP)
    v_hat = vn / (1.0 - ADAM_B2 ** ADAM_STEP)
    return -ADAM_LR * (m_hat / (jnp.sqrt(v_hat) + ADAM_EPS) + ADAM_WD * w), mn, vn


def _adamw_layers(w, gs, m, v):
    depth, rows, cols = w.shape
    tr = _rows_per_block(rows, cols, 8)

    def body(w_ref, g0_ref, g1_ref, m_ref, v_ref, go_ref, d_ref, mo_ref, vo_ref):
        gg = jnp.where(pl.program_id(0) == 0, g0_ref[...], g1_ref[...])
        go_ref[...] = gg
        d_ref[...], mo_ref[...], vo_ref[...] = _adamw_math(w_ref[...], gg, m_ref[...], v_ref[...])

    blk = pl.BlockSpec((None, tr, cols), lambda l, i: (l, i, 0))
    return pl.pallas_call(
        body, name="adamw_layers", grid=(depth, rows // tr),
        in_specs=[blk, pl.BlockSpec((tr, cols), lambda l, i: (i * (1 - l), 0)),
                  pl.BlockSpec((tr, cols), lambda l, i: (i * l, 0)), blk, blk],
        out_specs=[blk] * 4, out_shape=[SDS(w.shape, F32)] * 4,
        compiler_params=_cp("arbitrary", "arbitrary"))(w, gs[0], gs[1], m, v)


def _adamw_packed(w, g, m, v):
    def body(w_ref, g_ref, m_ref, v_ref, d_ref, mo_ref, vo_ref):
        d_ref[...], mo_ref[...], vo_ref[...] = _adamw_math(w_ref[...], g_ref[...], m_ref[...], v_ref[...])

    vm = pl.BlockSpec(memory_space=pltpu.VMEM)
    return pl.pallas_call(body, name="adamw_packed", in_specs=[vm] * 4, out_specs=[vm] * 3,
                          out_shape=[SDS(w.shape, F32)] * 3,
                          compiler_params=pltpu.CompilerParams(vmem_limit_bytes=VMEM_LIMIT))(w, g, m, v)


_WEIGHTS = ["ffn1_pre_g", "ffn1_w_gu", "ffn1_w_down", "ffn1_post_g", "mix_pre_g", "w_in", "lru_conv_w", "lru_conv_b",
            "lru_w_a", "lru_b_a", "lru_w_x", "lru_b_x", "lru_lambda", "attn_sinks", "conv_w", "conv_b", "conv_ln_g",
            "conv_ln_b", "group_g", "w_out", "mix_post_g", "ffn2_pre_g", "ffn2_w_gu", "ffn2_w_down", "ffn2_post_g"]
_INPUTS = ["x"] + _WEIGHTS + ["loss_target"] + ["m_" + n for n in _WEIGHTS] + ["v_" + n for n in _WEIGHTS]
_BIG = ["ffn1_w_gu", "ffn1_w_down", "w_in", "w_out", "ffn2_w_gu", "ffn2_w_down"]
_SMALL_SHARDED = ["lru_conv_w", "conv_w"]
_SMALL_REPL = [n for n in _WEIGHTS if n not in _BIG and n not in _SMALL_SHARDED]

PACK_TILE = 8 * 128


def _pack(arrs):
    parts = []
    for a in arrs:
        flat = a.reshape(-1)
        parts.append(jnp.pad(flat, (0, -flat.shape[0] % PACK_TILE)).reshape(-1, 128))
    return jnp.concatenate(parts, axis=0)


def _unpack(buf, shapes):
    out, row = [], 0
    for shp in shapes:
        size = math.prod(shp)
        nrow = -(-size // PACK_TILE) * 8
        out.append(buf[row:row + nrow].reshape(-1)[:size].reshape(shp))
        row += nrow
    return out


def _unshard_cols(a):
    return a.transpose(0, 2, 1, 3).reshape(1, a.shape[2], NSHARD * a.shape[3])


_GROUPS = dict(ffn1=["ffn1_w_gu", "ffn1_w_down"], mix=["w_in", "w_out", "lru_conv_w", "conv_w"],
               ffn2=["ffn2_w_gu", "ffn2_w_down"])


def _full_weights(group, gathered):
    g = dict(zip(_GROUPS[group], gathered))
    if group == "mix":
        return dict(w_in=_unshard_cols(g["w_in"]), w_out=g["w_out"].reshape(1, D, D),
                    lru_conv_w=_unshard_cols(g["lru_conv_w"])[0], conv_w=_unshard_cols(g["conv_w"])[0])
    return {group + "_w_gu": g[group + "_w_gu"], group + "_w_down": g[group + "_w_down"].reshape(1, DFF, D)}


def _by_shard(name, buf):
    if name.endswith("w_gu"):
        return buf[0]
    if name == "w_in":
        return buf.reshape(D, NSHARD, P_IN // NSHARD).transpose(1, 0, 2)
    return buf.reshape(NSHARD, buf.shape[2] // NSHARD, buf.shape[3])


class _Reducer:
    PLANS = (_plan_pair_exchange, _plan_chip_exchange, _plan_pair_share)

    def __init__(self, keys, gs, c_idx, cp_idx):
        self.keys, self.gs, self.c_idx, self.cp_idx = keys, gs, c_idx, cp_idx
        self.n = len(gs)
        self.step = 0
        self.result = None

    def inputs(self):
        n = self.n
        if self.step == 0:
            bufs = self.gs + [lax.empty((NSHARD, g.shape[1] // 2, g.shape[2]), F32) for g in self.gs]
        elif self.step == 1:
            ts = [_pair_sum(g, r, self.c_idx) for g, r in zip(self.gs, self.rs)]
            bufs = ts + [lax.empty((3,) + t.shape[1:], BF16) for t in ts]
        else:
            bufs = [_chip_sum(g, r, rr, self.cp_idx) for g, r, rr in zip(self.gs, self.rs, self.rrs)]
        return bufs, (self.PLANS[self.step], len(bufs), (n, 3 * n, n)[self.step])

    def absorb(self, done):
        n = self.n
        if self.step == 0:
            self.gs, self.rs = done[:n], done[n:]
        elif self.step == 1:
            self.rrs = done[n:]
        else:
            self.result = dict(zip(self.keys, done))
        self.step += 1


class _ReducePipeline:
    def __init__(self, c_idx, cp_idx):
        self.c_idx, self.cp_idx = c_idx, cp_idx
        self.reducers, self.flying, self.calls = [], None, 0

    def add(self, layer, done):
        if done:
            keys = [(layer, n) for n in done]
            self.reducers.append(_Reducer(keys, [_by_shard(n, b) for n, b in done.items()], self.c_idx, self.cp_idx))

    def _next(self):
        active = [r for r in self.reducers if r.step < 3]
        bufs, plans = [], []
        for r in active:
            b, triple = r.inputs()
            bufs += b
            plans.append(triple)
        self.calls += 1
        return active, bufs, plans, "grad_exchange%d" % self.calls

    def _absorb(self, active, plans, done):
        at = 0
        for r, (_, nb, _) in zip(active, plans):
            r.absorb(done[at:at + nb])
            at += nb

    def _land(self, after):
        if self.flying is not None:
            active, plans, name, send_sems, recv_sems, bufs = self.flying
            self._absorb(active, plans, _exchange_wait(name + "_wait", send_sems, recv_sems, bufs, plans, after))
            self.flying = None

    def hook(self, after):
        self._land(after)
        active, bufs, plans, name = self._next()
        if not active:
            return []
        send_sems, recv_sems, bufs, token = _exchange_start(name + "_start", bufs, plans)
        self.flying = (active, plans, name, send_sems, recv_sems, bufs)
        return [token]

    def finish(self, after):
        self._land(after)
        while True:
            active, bufs, plans, name = self._next()
            if not active:
                break
            self._absorb(active, plans, _exchange(name, bufs, plans))
        out = {}
        for r in self.reducers:
            out.update(r.result)
        return out


def kernel(*args):
    d = dict(zip(_INPUTS, args, strict=True))
    xi, yi, ci = lax.axis_index("x"), lax.axis_index("y"), lax.axis_index("c")
    p = 2 * xi + yi
    c_idx = jnp.reshape(ci, (1,)).astype(jnp.int32)
    p_idx = jnp.reshape(p, (1,)).astype(jnp.int32)
    cp_idx = jnp.stack([ci, p]).astype(jnp.int32)
    x, target = d["x"][0], d["loss_target"][0]
    tm, tc = _tiles(x.shape[0])

    groups = [(l, grp) for l in range(DEPTH) for grp in _GROUPS]
    placed = {(l, grp): [_place_shard(d[n], l, p_idx, BF16 if n in _BIG else F32) for n in _GROUPS[grp]]
              for l, grp in groups}
    ready = {groups[0]: _gather_two_level(placed[groups[0]], len(placed[groups[0]]))}
    flying, tokens = {}, [ready[groups[0]][0]]
    for l, grp in groups[1:]:
        plans = [(_plan_gather, len(placed[l, grp]), 3 * len(placed[l, grp]))]
        send_sems, recv_sems, bufs, token = _exchange_start("gather_l%d_%s_start" % (l, grp), placed[l, grp], plans,
                                                             tokens[-1:])
        flying[l, grp] = (send_sems, recv_sems, bufs, plans)
        tokens.append(token)

    def weights_of(l):
        def weights(grp, after):
            if (l, grp) not in ready:
                send_sems, recv_sems, bufs, plans = flying[l, grp]
                ready[l, grp] = _exchange_wait("gather_l%d_%s_wait" % (l, grp), send_sems, recv_sems, bufs, plans, after)
            return _full_weights(grp, ready[l, grp])
        return weights

    small = {n: d[n] for n in _SMALL_REPL}
    x1, sv0 = _forward_layer(x, weights_of(0), _layer_params(small, 0), tm, tc, tokens[1:])
    x2, sv1 = _forward_layer(x1, weights_of(1), _layer_params(small, 1), tm, tc)
    dx, lcols = _loss_grad(x2, target, tm)

    pipe = _ReducePipeline(c_idx, cp_idx)
    sgrads = [None] * DEPTH
    for l, sv in ((1, sv1), (0, sv0)):
        bufs = _grad_buffers()

        def stage(done, dx, l=l):
            pipe.add(l, done)
            return pipe.hook(dx)

        dx, sgrads[l] = _backward_layer(dx, sv, bufs, tm, tc, stage)
        pipe.add(l, {n: bufs[n] for n in ("ffn1_w_gu", "ffn1_w_down")})
    grad_x = dx
    reduced = pipe.finish(grad_x)

    stacked = {n: jnp.stack([sgrads[l][n].reshape(d[n].shape[1:]) for l in range(DEPTH)]) for n in _SMALL_REPL}
    for n in _SMALL_SHARDED:
        stacked[n] = jnp.stack([sgrads[l][n] for l in range(DEPTH)])
    loss_part = jnp.pad((0.5 / D) * jnp.sum(lcols).reshape(1), (0, 127))
    order = _SMALL_REPL + _SMALL_SHARDED
    summed = _unpack(_allreduce_small(_pack([loss_part] + [stacked[n] for n in order])),
                     [(128,)] + [stacked[n].shape for n in order])
    loss = summed[0][0]
    grads = {}
    for n, g in zip(order, summed[1:]):
        if n in _SMALL_SHARDED:
            g = lax.dynamic_slice_in_dim(g, p * (g.shape[2] // NSHARD), g.shape[2] // NSHARD, axis=2)
        grads[n] = g

    delta, new_m, new_v = {}, {}, {}
    for n in _BIG:
        grads[n], delta[n], new_m[n], new_v[n] = _adamw_layers(d[n], [reduced[l, n] for l in range(DEPTH)],
                                                                d["m_" + n], d["v_" + n])
    shapes = [d[n].shape for n in order]
    packed = [_pack([src(n) for n in order]) for src in
              (lambda n: d[n], lambda n: grads[n], lambda n: d["m_" + n], lambda n: d["v_" + n])]
    for out, res in zip((delta, new_m, new_v), _adamw_packed(*packed)):
        out.update(zip(order, _unpack(res, shapes)))

    return (loss, grad_x[None], *[grads[n] for n in _WEIGHTS], *[delta[n] for n in _WEIGHTS],
            *[new_m[n] for n in _WEIGHTS], *[new_v[n] for n in _WEIGHTS])
```

```python
import functools
import math

import jax
import jax.numpy as jnp
from jax import lax
from jax.experimental import pallas as pl
from jax.experimental.pallas import tpu as pltpu

F32 = jnp.float32
BF16 = jnp.bfloat16
SDS = jax.ShapeDtypeStruct

D = 1024
DFF = 2816
FH = DFF // 2
DEPTH = 2
W_A = 256
W_B = 512
W_C = 256
NQ = 8
HD = 64
BLK = 128
P_IN = 1792
LRU_K = 4
CONV_K = 31
LRU_C = 8.0
NORM_EPS = 1e-6
LN_EPS = 1e-5
NEG_BIG = -1e30
SCALE = 1.0 / math.sqrt(HD)

ADAM_LR = 0.001
ADAM_B1 = 0.9
ADAM_B2 = 0.999
ADAM_EPS = 1e-08
ADAM_WD = 0.01
ADAM_STEP = 10

VMEM_LIMIT = 56 * 1024 * 1024
NSHARD = 4
NDEV = 8

TN = (((0,), (0,)), ((), ()))
NT = (((1,), (1,)), ((), ()))

MESH = pl.DeviceIdType.MESH
ANY = pl.BlockSpec(memory_space=pl.ANY)


def _cp(*sem):
    return pltpu.CompilerParams(dimension_semantics=sem if sem else None, vmem_limit_bytes=VMEM_LIMIT)


def _rsq(x, eps):
    return lax.rsqrt(jnp.mean(x * x, axis=-1, keepdims=True) + eps)


def _rms_bwd_rows(x, g, dy):
    r = _rsq(x, NORM_EPS)
    xh = x * r
    dyg = dy * g
    dx = r * (dyg - xh * jnp.mean(dyg * xh, axis=-1, keepdims=True))
    return dx, dy * xh


def _sig(x):
    return jax.nn.sigmoid(x)


def _ffn_up(x, pre_g, wgu, l, tm, deps=()):
    s = x.shape[0]
    deps = list(deps)

    def body(x_ref, g_ref, wg_ref, wu_ref, *rest):
        h_ref, go_ref, uo_ref, a_ref = rest[len(deps):]

        @pl.when(pl.program_id(1) == 0)
        def _():
            xf = x_ref[...]
            h_ref[...] = (xf * _rsq(xf, NORM_EPS) * g_ref[...]).astype(BF16)

        h = h_ref[...]
        gg = jnp.dot(h, wg_ref[...], preferred_element_type=F32)
        uu = jnp.dot(h, wu_ref[...], preferred_element_type=F32)
        go_ref[...] = gg.astype(BF16)
        uo_ref[...] = uu.astype(BF16)
        a_ref[...] = (gg * _sig(gg) * uu).astype(BF16)

    wide = pl.BlockSpec((tm, FH), lambda i, j: (i, j))
    return pl.pallas_call(
        body, name="ffn_up", grid=(s // tm, 2),
        in_specs=[pl.BlockSpec((tm, D), lambda i, j: (i, 0)), pl.BlockSpec((1, D), lambda i, j: (0, 0)),
                  pl.BlockSpec((None, None, D, FH), lambda i, j: (l, j, 0, 0)),
                  pl.BlockSpec((None, None, D, FH), lambda i, j: (l, j + 2, 0, 0))] + [ANY] * len(deps),
        out_specs=[pl.BlockSpec((tm, D), lambda i, j: (i, 0)), wide, wide, wide],
        out_shape=[SDS((s, D), BF16), SDS((s, DFF), BF16), SDS((s, DFF), BF16), SDS((s, DFF), BF16)],
        compiler_params=_cp("parallel", "arbitrary"),
    )(x, pre_g, wgu, wgu, *deps)


def _mm_rms_res(a, w, l, x, g, c, tm, tk, name):
    s, k_dim = a.shape
    nk = k_dim // tk

    def body(a_ref, w_ref, x_ref, g_ref, z_ref, x1_ref):
        k = pl.program_id(1)
        p = jnp.dot(a_ref[...], w_ref[...], preferred_element_type=F32)

        @pl.when(k == 0)
        def _():
            z_ref[...] = p

        @pl.when(k > 0)
        def _():
            z_ref[...] += p

        @pl.when(k == nk - 1)
        def _():
            z = z_ref[...]
            x1_ref[...] = x_ref[...] + c * (z * _rsq(z, NORM_EPS) * g_ref[...])

    row = pl.BlockSpec((tm, D), lambda i, k: (i, 0))
    return pl.pallas_call(
        body, name=name, grid=(s // tm, nk),
        in_specs=[pl.BlockSpec((tm, tk), lambda i, k: (i, k)), pl.BlockSpec((None, tk, D), lambda i, k: (l, k, 0)),
                  row, pl.BlockSpec((1, D), lambda i, k: (0, 0))],
        out_specs=[row, row],
        out_shape=[SDS((s, D), F32), SDS((s, D), F32)],
        compiler_params=_cp("parallel", "arbitrary"),
    )(a, w, x, g)


def _rms_bwd(dy, z, g, c, tm, name, deps=()):
    s = z.shape[0]
    deps = list(deps)

    def body(dy_ref, z_ref, g_ref, *rest):
        dz_ref, dg_ref = rest[len(deps):]
        dz, dgr = _rms_bwd_rows(z_ref[...], g_ref[...], c * dy_ref[...])
        dz_ref[...] = dz.astype(BF16)
        part = jnp.sum(dgr, axis=0, keepdims=True)

        @pl.when(pl.program_id(0) == 0)
        def _():
            dg_ref[...] = part

        @pl.when(pl.program_id(0) > 0)
        def _():
            dg_ref[...] += part

    row = pl.BlockSpec((tm, D), lambda i: (i, 0))
    vec = pl.BlockSpec((1, D), lambda i: (0, 0))
    return pl.pallas_call(
        body, name=name, grid=(s // tm,), in_specs=[row, row, vec] + [ANY] * len(deps), out_specs=[row, vec],
        out_shape=[SDS((s, D), BF16), SDS((1, D), F32)], compiler_params=_cp("arbitrary"),
    )(dy, z, g, *deps)


def _ffn_bwd_mid(dz, wd, l, g, u, tm):
    s = dz.shape[0]

    def body(dz_ref, wd_ref, g_ref, u_ref, dg_ref, du_ref):
        da = lax.dot_general(dz_ref[...], wd_ref[...], NT, preferred_element_type=F32)
        gg = g_ref[...].astype(F32)
        uu = u_ref[...].astype(F32)
        sg = _sig(gg)
        dg_ref[...] = (da * uu * sg * (1.0 + gg * (1.0 - sg))).astype(BF16)
        du_ref[...] = (da * gg * sg).astype(BF16)

    wide = pl.BlockSpec((tm, FH), lambda i, j: (i, j))
    return pl.pallas_call(
        body, name="ffn_bwd_mid", grid=(s // tm, 2),
        in_specs=[pl.BlockSpec((tm, D), lambda i, j: (i, 0)), pl.BlockSpec((None, FH, D), lambda i, j: (l, j, 0)), wide, wide],
        out_specs=[wide, wide],
        out_shape=[SDS((s, DFF), BF16), SDS((s, DFF), BF16)],
        compiler_params=_cp("parallel", "arbitrary"),
    )(dz, wd, g, u)


def _ffn_bwd_dh(dg, du, wgu, l, x, pre_g, dx1, tm):
    s = x.shape[0]

    def body(dg_ref, du_ref, wg_ref, wu_ref, x_ref, g_ref, dx1_ref, dx_ref, dgp_ref):
        i, k = pl.program_id(0), pl.program_id(1)
        p = (lax.dot_general(dg_ref[...], wg_ref[...], NT, preferred_element_type=F32)
             + lax.dot_general(du_ref[...], wu_ref[...], NT, preferred_element_type=F32))

        @pl.when(k == 0)
        def _():
            dx_ref[...] = p

        @pl.when(k == 1)
        def _():
            dx, dgr = _rms_bwd_rows(x_ref[...], g_ref[...], dx_ref[...] + p)
            dx_ref[...] = dx1_ref[...] + dx
            part = jnp.sum(dgr, axis=0, keepdims=True)

            @pl.when(i == 0)
            def _():
                dgp_ref[...] = part

            @pl.when(i > 0)
            def _():
                dgp_ref[...] += part

    wide = pl.BlockSpec((tm, FH), lambda i, k: (i, k))
    row = pl.BlockSpec((tm, D), lambda i, k: (i, 0))
    vec = pl.BlockSpec((1, D), lambda i, k: (0, 0))
    return pl.pallas_call(
        body, name="ffn_bwd_dh", grid=(s // tm, 2),
        in_specs=[wide, wide, pl.BlockSpec((None, None, D, FH), lambda i, k: (l, k, 0, 0)),
                  pl.BlockSpec((None, None, D, FH), lambda i, k: (l, k + 2, 0, 0)), row, vec, row],
        out_specs=[row, vec],
        out_shape=[SDS((s, D), F32), SDS((1, D), F32)],
        compiler_params=_cp("arbitrary", "arbitrary"),
    )(dg, du, wgu, wgu, x, pre_g, dx1)


def _mm_tn_into(buf, a, b, l, joff, tka, tn, ts, name):
    s, ka = a.shape
    n = b.shape[1]

    def body(buf_ref, a_ref, b_ref, o_ref):
        p = lax.dot_general(a_ref[...], b_ref[...], TN, preferred_element_type=F32)

        @pl.when(pl.program_id(2) == 0)
        def _():
            o_ref[...] = p

        @pl.when(pl.program_id(2) > 0)
        def _():
            o_ref[...] += p

    return pl.pallas_call(
        body, name=name, grid=(ka // tka, n // tn, s // ts),
        in_specs=[pl.BlockSpec(memory_space=pl.ANY),
                  pl.BlockSpec((ts, tka), lambda ia, j, t: (t, ia)), pl.BlockSpec((ts, tn), lambda ia, j, t: (t, j))],
        out_specs=pl.BlockSpec((None, None, tka, tn), lambda ia, j, t: (l, joff + j, ia, 0)),
        out_shape=SDS(buf.shape, F32), input_output_aliases={0: 0},
        compiler_params=_cp("parallel", "parallel", "arbitrary"),
    )(buf, a, b)


def _proj(x, g, w_in, l, tm):
    s = x.shape[0]

    def body(x_ref, g_ref, w_ref, h_ref, p_ref):
        xf = x_ref[...]
        h = (xf * _rsq(xf, NORM_EPS) * g_ref[...]).astype(BF16)
        h_ref[...] = h
        p_ref[...] = jnp.dot(h, w_ref[...], preferred_element_type=F32)

    return pl.pallas_call(
        body, name="proj", grid=(s // tm,),
        in_specs=[pl.BlockSpec((tm, D), lambda i: (i, 0)), pl.BlockSpec((1, D), lambda i: (0, 0)),
                  pl.BlockSpec((None, D, P_IN), lambda i: (l, 0, 0))],
        out_specs=[pl.BlockSpec((tm, D), lambda i: (i, 0)), pl.BlockSpec((tm, P_IN), lambda i: (i, 0))],
        out_shape=[SDS((s, D), BF16), SDS((s, P_IN), F32)],
        compiler_params=_cp("parallel"),
    )(x, g, w_in)


def _mm_nt(a, w, l, tm, name):
    s, k_dim = a.shape
    n = w.shape[1]

    def body(a_ref, w_ref, o_ref):
        o_ref[...] = lax.dot_general(a_ref[...], w_ref[...], NT, preferred_element_type=F32)

    return pl.pallas_call(
        body, name=name, grid=(s // tm,),
        in_specs=[pl.BlockSpec((tm, k_dim), lambda i: (i, 0)), pl.BlockSpec((None, n, k_dim), lambda i: (l, 0, 0))],
        out_specs=pl.BlockSpec((tm, n), lambda i: (i, 0)),
        out_shape=SDS((s, n), F32), compiler_params=_cp("parallel"),
    )(a, w)


def _mm_nt_rmsbwd(dp, w_in, l, x, g, dx1, tm):
    s = x.shape[0]

    def body(dp_ref, w_ref, x_ref, g_ref, dx1_ref, dx_ref, dg_ref):
        dh = lax.dot_general(dp_ref[...], w_ref[...], NT, preferred_element_type=F32)
        dx, dgr = _rms_bwd_rows(x_ref[...], g_ref[...], dh)
        dx_ref[...] = dx1_ref[...] + dx
        part = jnp.sum(dgr, axis=0, keepdims=True)

        @pl.when(pl.program_id(0) == 0)
        def _():
            dg_ref[...] = part

        @pl.when(pl.program_id(0) > 0)
        def _():
            dg_ref[...] += part

    row = pl.BlockSpec((tm, D), lambda i: (i, 0))
    vec = pl.BlockSpec((1, D), lambda i: (0, 0))
    return pl.pallas_call(
        body, name="mix_bwd_dx", grid=(s // tm,),
        in_specs=[pl.BlockSpec((tm, P_IN), lambda i: (i, 0)), pl.BlockSpec((None, D, P_IN), lambda i: (l, 0, 0)), row, vec, row],
        out_specs=[row, vec], out_shape=[SDS((s, D), F32), SDS((1, D), F32)],
        compiler_params=_cp("arbitrary"),
    )(dp, w_in, x, g, dx1)


def _row_iota(shape):
    return lax.broadcasted_iota(jnp.int32, shape, 0)


def _lru_gates(xc, wa_ref, ba_ref, wx_ref, bx_ref, lam_ref):
    xb = xc.astype(BF16)
    r = _sig(jnp.dot(xb, wa_ref[...], preferred_element_type=F32) + ba_ref[...])
    ig = _sig(jnp.dot(xb, wx_ref[...], preferred_element_type=F32) + bx_ref[...])
    nl = -lam_ref[...]
    sp = jnp.maximum(nl, 0.0) + jnp.log(1.0 + jnp.exp(-jnp.abs(nl)))
    log_a = -LRU_C * r * sp
    a = jnp.exp(log_a)
    x2 = 2.0 * log_a
    series = x2 * (1.0 + x2 * (0.5 + x2 * (1.0 / 6.0 + x2 * (1.0 / 24.0 + x2 * (1.0 / 120.0)))))
    em1 = jnp.where(x2 > -0.05, series, jnp.exp(x2) - 1.0)
    mlt = jnp.sqrt(-em1)
    return r, ig, a, mlt, sp


def _conv_taps(src_ref, w_ref, k_taps, pad, tc):
    acc = None
    for j in range(k_taps):
        term = w_ref[j:j + 1, :] * src_ref[pl.ds(pad - (k_taps - 1) + j, tc), :]
        acc = term if acc is None else acc + term
    return acc


def _gelu_parts(x):
    c0 = math.sqrt(2.0 / math.pi)
    inner = c0 * (x + 0.044715 * x * x * x)
    t = jnp.tanh(inner)
    gl = 0.5 * x * (1.0 + t)
    dgl = 0.5 * (1.0 + t) + 0.5 * x * (1.0 - t * t) * c0 * (1.0 + 3.0 * 0.044715 * x * x)
    return gl, dgl


def _lru_fwd(proj, cw, cb, wa, ba, wx, bx, lam, gg, tc):
    s = proj.shape[0]
    pad = 8

    def body(xcur_ref, xprev_ref, gate_ref, cw_ref, cb_ref, wa_ref, ba_ref, wx_ref, bx_ref, lam_ref, gg_ref,
             yn_ref, h_ref, xs_ref, hc_ref):
        i = pl.program_id(0)

        @pl.when(i == 0)
        def _():
            hc_ref[...] = jnp.zeros_like(hc_ref)

        xs_ref[0:pad, :] = jnp.where(i > 0, xprev_ref[tc - pad:tc, :], 0.0)
        xs_ref[pad:pad + tc, :] = xcur_ref[...]
        xc = _conv_taps(xs_ref, cw_ref, LRU_K, pad, tc) + cb_ref[...]
        _, ig, a, mlt, _ = _lru_gates(xc, wa_ref, ba_ref, wx_ref, bx_ref, lam_ref)
        u = mlt * (ig * xc)
        row = _row_iota((tc, W_A))
        d = 1
        while d < tc:
            ok = row >= d
            a_sh = jnp.where(ok, pltpu.roll(a, d, axis=0), 1.0)
            u_sh = jnp.where(ok, pltpu.roll(u, d, axis=0), 0.0)
            u = a * u_sh + u
            a = a * a_sh
            d *= 2
        h = u + a * hc_ref[...]
        hc_ref[...] = jnp.sum(jnp.where(row == tc - 1, h, 0.0), axis=0, keepdims=True)
        h_ref[...] = h
        gl, _ = _gelu_parts(gate_ref[...])
        ya = gl * h
        yn_ref[...] = (ya * _rsq(ya, NORM_EPS) * gg_ref[...]).astype(BF16)

    blk = lambda c: pl.BlockSpec((tc, W_A), lambda i, c=c: (i, c))
    full = lambda a: pl.BlockSpec(a.shape, lambda i: (0,) * a.ndim)
    params = [cw, cb, wa, ba, wx, bx, lam, gg]
    return pl.pallas_call(
        body, name="lru_fwd", grid=(s // tc,),
        in_specs=[blk(0), pl.BlockSpec((tc, W_A), lambda i: (jnp.maximum(i - 1, 0), 0)), blk(1)] + [full(a) for a in params],
        out_specs=[pl.BlockSpec((tc, W_A), lambda i: (i, 0))] * 2,
        out_shape=[SDS((s, W_A), BF16), SDS((s, W_A), F32)],
        scratch_shapes=[pltpu.VMEM((tc + pad, W_A), F32), pltpu.VMEM((1, W_A), F32)],
        compiler_params=_cp("arbitrary"),
    )(proj, proj, proj, *params)


def _acc(ref, first, val):
    @pl.when(first)
    def _():
        ref[...] = val

    @pl.when(jnp.logical_not(first))
    def _():
        ref[...] += val


def _lru_bwd(dy, proj, h, cw, cb, wa, ba, wx, bx, lam, gg, tc):
    s = proj.shape[0]
    nc = s // tc
    pad = 8

    def body(dy_ref, xcur_ref, xprev_ref, gate_ref, h_ref, hprev_ref, cw_ref, cb_ref, wa_ref, ba_ref, wx_ref, bx_ref,
             lam_ref, gg_ref,
             dp_ref, dcw_ref, dcb_ref, dwa_ref, dba_ref, dwx_ref, dbx_ref, dlam_ref, dgg_ref,
             xs_ref, ds_ref, mu_ref, nx_ref):
        step = pl.program_id(0)
        i = nc - 1 - step
        first = step == 0

        @pl.when(first)
        def _():
            mu_ref[...] = jnp.zeros_like(mu_ref)
            nx_ref[...] = jnp.zeros_like(nx_ref)

        xs_ref[0:pad, :] = jnp.where(i > 0, xprev_ref[tc - pad:tc, :], 0.0)
        xs_ref[pad:pad + tc, :] = xcur_ref[...]
        xc = _conv_taps(xs_ref, cw_ref, LRU_K, pad, tc) + cb_ref[...]
        r, ig, a, mlt, sp = _lru_gates(xc, wa_ref, ba_ref, wx_ref, bx_ref, lam_ref)
        hh = h_ref[...]
        gate = gate_ref[...]
        gl, dgl = _gelu_parts(gate)
        ya = gl * hh
        dya, dggr = _rms_bwd_rows(ya, gg_ref[...], dy_ref[...])
        _acc(dgg_ref, first, jnp.sum(dggr, axis=0, keepdims=True))
        dp_ref[:, W_A:2 * W_A] = dya * hh * dgl
        dh = dya * gl

        row = _row_iota((tc, W_A))
        aa = a
        uu = a * dh
        d = 1
        while d < tc:
            ok = row < tc - d
            a_sh = jnp.where(ok, pltpu.roll(aa, tc - d, axis=0), 1.0)
            u_sh = jnp.where(ok, pltpu.roll(uu, tc - d, axis=0), 0.0)
            uu = uu + aa * u_sh
            aa = aa * a_sh
            d *= 2
        cin = mu_ref[...]
        mu = uu + aa * cin
        lam_t = dh + jnp.where(row == tc - 1, cin, pltpu.roll(mu, tc - 1, axis=0))
        mu_ref[...] = jnp.sum(jnp.where(row == 0, mu, 0.0), axis=0, keepdims=True)
        hm1 = jnp.where(row == 0, jnp.where(i > 0, pltpu.roll(hprev_ref[...], 1, axis=0), 0.0),
                        pltpu.roll(hh, 1, axis=0))
        da = lam_t * hm1
        du = lam_t
        dmlt = du * ig * xc
        dig = du * mlt * xc
        dxc = du * mlt * ig
        dlog_a = da * a - dmlt * (a * a / mlt)
        dr = dlog_a * (-LRU_C * sp)
        dsp = jnp.sum(dlog_a * (-LRU_C * r), axis=0, keepdims=True)
        _acc(dlam_ref, first, dsp * (-_sig(-lam_ref[...])))
        dga = dr * r * (1.0 - r)
        dgx = dig * ig * (1.0 - ig)
        _acc(dba_ref, first, jnp.sum(dga, axis=0, keepdims=True))
        _acc(dbx_ref, first, jnp.sum(dgx, axis=0, keepdims=True))
        xb = xc.astype(BF16)
        dgab = dga.astype(BF16)
        dgxb = dgx.astype(BF16)
        _acc(dwa_ref, first, lax.dot_general(xb, dgab, TN, preferred_element_type=F32))
        _acc(dwx_ref, first, lax.dot_general(xb, dgxb, TN, preferred_element_type=F32))
        dxc = (dxc + lax.dot_general(dgab, wa_ref[...], NT, preferred_element_type=F32)
               + lax.dot_general(dgxb, wx_ref[...], NT, preferred_element_type=F32))

        _acc(dcb_ref, first, jnp.sum(dxc, axis=0, keepdims=True))
        r8 = _row_iota((8, W_A))
        dcw = jnp.zeros((8, W_A), F32)
        for j in range(LRU_K):
            tap = jnp.sum(dxc * xs_ref[pl.ds(pad - (LRU_K - 1) + j, tc), :], axis=0, keepdims=True)
            dcw = dcw + jnp.where(r8 == j, tap, 0.0)
        _acc(dcw_ref, first, dcw)
        ds_ref[0:tc, :] = dxc
        ds_ref[tc:tc + pad, :] = nx_ref[...]
        dlx = None
        for j in range(LRU_K):
            term = cw_ref[j:j + 1, :] * ds_ref[pl.ds(LRU_K - 1 - j, tc), :]
            dlx = term if dlx is None else dlx + term
        dp_ref[:, 0:W_A] = dlx
        nx_ref[...] = dxc[0:pad, :]

    rev = lambda c: pl.BlockSpec((tc, W_A), lambda t, c=c: (nc - 1 - t, c))
    prev = lambda c: pl.BlockSpec((tc, W_A), lambda t, c=c: (jnp.maximum(nc - 2 - t, 0), c))
    full = lambda a: pl.BlockSpec(a.shape, lambda t: (0,) * a.ndim)
    params = [cw, cb, wa, ba, wx, bx, lam, gg]
    vec = SDS((1, W_A), F32)
    sq = SDS((W_A, W_A), F32)
    outs = [SDS((s, 2 * W_A), F32), SDS((8, W_A), F32), vec, sq, vec, sq, vec, vec, vec]
    return pl.pallas_call(
        body, name="lru_bwd", grid=(nc,),
        in_specs=[rev(0), rev(0), prev(0), rev(1), rev(0), prev(0)] + [full(a) for a in params],
        out_specs=[pl.BlockSpec((tc, 2 * W_A), lambda t: (nc - 1 - t, 0))]
        + [pl.BlockSpec(o.shape, lambda t: (0, 0)) for o in outs[1:]],
        out_shape=outs,
        scratch_shapes=[pltpu.VMEM((tc + pad, W_A), F32), pltpu.VMEM((tc + pad, W_A), F32),
                        pltpu.VMEM((1, W_A), F32), pltpu.VMEM((pad, W_A), F32)],
        compiler_params=_cp("arbitrary"),
    )(dy, proj, proj, proj, h, h, *params)


def _attn_stack(qa, qb, kvh):
    lane = lax.broadcasted_iota(jnp.int32, qa.shape, 1)
    keep = (lane >= HD) if kvh == 1 else (lane < HD)
    parts = []
    for tile in (qa, qb):
        for half in (0, 1):
            y = tile if half == kvh else pltpu.roll(tile, HD, axis=1)
            parts.append(jnp.where(keep, y, 0.0))
    return jnp.concatenate(parts, axis=0)


def _attn_unstack(o, kvh):
    lane = lax.broadcasted_iota(jnp.int32, (BLK, 2 * HD), 1)
    tiles = []
    for t in range(2):
        halves = []
        for half in (0, 1):
            blk = o[(2 * t + half) * BLK:(2 * t + half + 1) * BLK, :]
            halves.append(blk if half == kvh else pltpu.roll(blk, HD, axis=1))
        tiles.append(jnp.where(lane < HD, halves[0], halves[1]))
    return tiles


def _attn_mask(n):
    qi = lax.broadcasted_iota(jnp.int32, (BLK, 2 * BLK), 0)
    kj = lax.broadcasted_iota(jnp.int32, (BLK, 2 * BLK), 1)
    rel = BLK + qi - kj
    return (rel >= 0) & (rel < BLK) & ((n - 1) * BLK + kj >= 0)


def _attn_probs(qs, kw, mask, sink_ref, kvh):
    sc = lax.dot_general(qs.astype(BF16), kw, NT, preferred_element_type=F32) * SCALE
    ps, psinks = [], []
    for rr in range(4):
        sk = sink_ref[4 * kvh + rr:4 * kvh + rr + 1, 0:1]
        sh = jnp.where(mask, sc[rr * BLK:(rr + 1) * BLK, :], NEG_BIG)
        m = jnp.maximum(jnp.max(sh, axis=-1, keepdims=True), sk)
        e = jnp.exp(sh - m)
        es = jnp.exp(sk - m)
        z = jnp.sum(e, axis=-1, keepdims=True) + es
        ps.append(e / z)
        psinks.append(es / z)
    return ps, psinks


def _attn_fwd(proj, sinks8, gg):
    s = proj.shape[0]

    def body(q_ref, kc_ref, kp_ref, vc_ref, vp_ref, sink_ref, gg_ref, yn_ref, ob_ref):
        n = pl.program_id(0)
        mask = _attn_mask(n)
        kw = jnp.concatenate([kp_ref[...], kc_ref[...]], axis=0).astype(BF16)
        vw = jnp.concatenate([vp_ref[...], vc_ref[...]], axis=0).astype(BF16)
        for kvh in range(2):
            qa = q_ref[:, 256 * kvh:256 * kvh + 128]
            qb = q_ref[:, 256 * kvh + 128:256 * kvh + 256]
            ps, _ = _attn_probs(_attn_stack(qa, qb, kvh), kw, mask, sink_ref, kvh)
            o = jnp.dot(jnp.concatenate(ps, axis=0).astype(BF16), vw, preferred_element_type=F32)
            ta, tb = _attn_unstack(o, kvh)
            ob_ref[:, 256 * kvh:256 * kvh + 128] = ta
            ob_ref[:, 256 * kvh + 128:256 * kvh + 256] = tb
        ob = ob_ref[...]
        yn_ref[...] = (ob * _rsq(ob, NORM_EPS) * gg_ref[...]).astype(BF16)

    kv = lambda c, back: pl.BlockSpec((BLK, 128), lambda n, c=c, back=back: (jnp.maximum(n - back, 0), c))
    out = pl.BlockSpec((BLK, W_B), lambda n: (n, 0))
    return pl.pallas_call(
        body, name="attn_fwd", grid=(s // BLK,),
        in_specs=[pl.BlockSpec((BLK, W_B), lambda n: (n, 1)), kv(8, 0), kv(8, 1), kv(9, 0), kv(9, 1),
                  pl.BlockSpec((8, 128), lambda n: (0, 0)), pl.BlockSpec((1, W_B), lambda n: (0, 0))],
        out_specs=[out, out], out_shape=[SDS((s, W_B), BF16), SDS((s, W_B), F32)],
        compiler_params=_cp("parallel"),
    )(proj, proj, proj, proj, proj, sinks8, gg)


def _attn_bwd(dy, proj, ob, sinks8, gg):
    s = proj.shape[0]

    def body(dya_ref, dyb_ref, q_ref, kc_ref, kp_ref, vc_ref, vp_ref, ob_ref, sink_ref, gg_ref,
             dq_ref, dcur_ref, dprev_ref, dsink_ref, dgg_ref):
        n = pl.program_id(0)
        first = n == 0
        mask = _attn_mask(n)
        kw = jnp.concatenate([kp_ref[...], kc_ref[...]], axis=0).astype(BF16)
        vw = jnp.concatenate([vp_ref[...], vc_ref[...]], axis=0).astype(BF16)
        dyn = jnp.concatenate([dya_ref[...], dyb_ref[...]], axis=1)
        dob, dggr = _rms_bwd_rows(ob_ref[...], gg_ref[...], dyn)
        _acc(dgg_ref, first, jnp.sum(dggr, axis=0, keepdims=True))
        r8 = _row_iota((8, 128))
        dsk = jnp.zeros((8, 128), F32)
        dkw = jnp.zeros((2 * BLK, 128), F32)
        dvw = jnp.zeros((2 * BLK, 128), F32)
        for kvh in range(2):
            qs = _attn_stack(q_ref[:, 256 * kvh:256 * kvh + 128], q_ref[:, 256 * kvh + 128:256 * kvh + 256], kvh)
            ps, psinks = _attn_probs(qs, kw, mask, sink_ref, kvh)
            dos = _attn_stack(dob[:, 256 * kvh:256 * kvh + 128], dob[:, 256 * kvh + 128:256 * kvh + 256], kvh)
            dosb = dos.astype(BF16)
            dp = lax.dot_general(dosb, vw, NT, preferred_element_type=F32)
            dss = []
            for rr in range(4):
                dpr = dp[rr * BLK:(rr + 1) * BLK, :]
                dd = jnp.sum(ps[rr] * dpr, axis=-1, keepdims=True)
                dss.append(ps[rr] * (dpr - dd) * SCALE)
                tot = jnp.sum(-psinks[rr] * dd, axis=0, keepdims=True)
                dsk = dsk + jnp.where(r8 == 4 * kvh + rr, tot, 0.0)
            dsb = jnp.concatenate(dss, axis=0).astype(BF16)
            pb = jnp.concatenate(ps, axis=0).astype(BF16)
            dqs = jnp.dot(dsb, kw, preferred_element_type=F32)
            ta, tb = _attn_unstack(dqs, kvh)
            dq_ref[:, 256 * kvh:256 * kvh + 128] = ta
            dq_ref[:, 256 * kvh + 128:256 * kvh + 256] = tb
            dkw = dkw + lax.dot_general(dsb, qs.astype(BF16), TN, preferred_element_type=F32)
            dvw = dvw + lax.dot_general(pb, dosb, TN, preferred_element_type=F32)
        _acc(dsink_ref, first, dsk)
        dprev_ref[:, 0:128] = dkw[0:BLK, :]
        dprev_ref[:, 128:256] = dvw[0:BLK, :]
        dcur_ref[:, 0:128] = dkw[BLK:2 * BLK, :]
        dcur_ref[:, 128:256] = dvw[BLK:2 * BLK, :]

    kv = lambda c, back: pl.BlockSpec((BLK, 128), lambda n, c=c, back=back: (jnp.maximum(n - back, 0), c))
    wide = pl.BlockSpec((BLK, W_B), lambda n: (n, 0))
    half = pl.BlockSpec((BLK, 256), lambda n: (n, 0))
    return pl.pallas_call(
        body, name="attn_bwd", grid=(s // BLK,),
        in_specs=[pl.BlockSpec((BLK, 256), lambda n: (n, 1)), pl.BlockSpec((BLK, 256), lambda n: (n, 2)),
                  pl.BlockSpec((BLK, W_B), lambda n: (n, 1)), kv(8, 0), kv(8, 1), kv(9, 0), kv(9, 1), wide,
                  pl.BlockSpec((8, 128), lambda n: (0, 0)), pl.BlockSpec((1, W_B), lambda n: (0, 0))],
        out_specs=[wide, half, half, pl.BlockSpec((8, 128), lambda n: (0, 0)), pl.BlockSpec((1, W_B), lambda n: (0, 0))],
        out_shape=[SDS((s, W_B), F32), SDS((s, 256), F32), SDS((s, 256), F32), SDS((8, 128), F32), SDS((1, W_B), F32)],
        compiler_params=_cp("arbitrary"),
    )(dy, dy, proj, proj, proj, proj, proj, ob, sinks8, gg)


def _ln_parts(y1, eps=LN_EPS):
    mu = jnp.mean(y1, axis=-1, keepdims=True)
    xc = y1 - mu
    rstd = lax.rsqrt(jnp.mean(xc * xc, axis=-1, keepdims=True) + eps)
    return xc * rstd, rstd


def _conf_fwd(proj, cw, cb, lg, lb, gg, tc):
    s = proj.shape[0]
    pad = 32

    def body(ac_ref, gc_ref, ap_ref, gp_ref, cw_ref, cb_ref, lg_ref, lb_ref, gg_ref, yn_ref, y1_ref, ys_ref):
        i = pl.program_id(0)
        tail = ap_ref[tc - pad:tc, :] * _sig(gp_ref[tc - pad:tc, :])
        ys_ref[0:pad, :] = jnp.where(i > 0, tail, 0.0)
        ys_ref[pad:pad + tc, :] = ac_ref[...] * _sig(gc_ref[...])
        y1 = _conv_taps(ys_ref, cw_ref, CONV_K, pad, tc) + cb_ref[...]
        y1_ref[...] = y1
        xh, _ = _ln_parts(y1)
        yl = xh * lg_ref[...] + lb_ref[...]
        yc = yl * _sig(yl)
        yn_ref[...] = (yc * _rsq(yc, NORM_EPS) * gg_ref[...]).astype(BF16)

    cur = lambda c: pl.BlockSpec((tc, W_C), lambda i, c=c: (i, c))
    prev = lambda c: pl.BlockSpec((tc, W_C), lambda i, c=c: (jnp.maximum(i - 1, 0), c))
    full = lambda a: pl.BlockSpec(a.shape, lambda i: (0,) * a.ndim)
    params = [cw, cb, lg, lb, gg]
    out = pl.BlockSpec((tc, W_C), lambda i: (i, 0))
    return pl.pallas_call(
        body, name="conf_fwd", grid=(s // tc,),
        in_specs=[cur(5), cur(6), prev(5), prev(6)] + [full(a) for a in params],
        out_specs=[out, out], out_shape=[SDS((s, W_C), BF16), SDS((s, W_C), F32)],
        scratch_shapes=[pltpu.VMEM((tc + pad, W_C), F32)],
        compiler_params=_cp("parallel"),
    )(proj, proj, proj, proj, *params)


def _conf_bwd(dy, proj, y1, cw, cb, lg, lb, gg, tc):
    s = proj.shape[0]
    nc = s // tc
    pad = 32

    def body(dy_ref, ac_ref, gc_ref, ap_ref, gp_ref, y1_ref, cw_ref, cb_ref, lg_ref, lb_ref, gg_ref,
             dp_ref, dcw_ref, dcb_ref, dlg_ref, dlb_ref, dgg_ref, ys_ref, ds_ref, nx_ref):
        step = pl.program_id(0)
        i = nc - 1 - step
        first = step == 0

        @pl.when(first)
        def _():
            nx_ref[...] = jnp.zeros_like(nx_ref)

        a = ac_ref[...]
        sg = _sig(gc_ref[...])
        tail = ap_ref[tc - pad:tc, :] * _sig(gp_ref[tc - pad:tc, :])
        ys_ref[0:pad, :] = jnp.where(i > 0, tail, 0.0)
        ys_ref[pad:pad + tc, :] = a * sg
        xh, rstd = _ln_parts(y1_ref[...])
        yl = xh * lg_ref[...] + lb_ref[...]
        sl = _sig(yl)
        yc = yl * sl
        dyc, dggr = _rms_bwd_rows(yc, gg_ref[...], dy_ref[...])
        _acc(dgg_ref, first, jnp.sum(dggr, axis=0, keepdims=True))
        dyl = dyc * sl * (1.0 + yl * (1.0 - sl))
        _acc(dlg_ref, first, jnp.sum(dyl * xh, axis=0, keepdims=True))
        _acc(dlb_ref, first, jnp.sum(dyl, axis=0, keepdims=True))
        dxh = dyl * lg_ref[...]
        dy1 = rstd * (dxh - jnp.mean(dxh, axis=-1, keepdims=True) - xh * jnp.mean(dxh * xh, axis=-1, keepdims=True))
        _acc(dcb_ref, first, jnp.sum(dy1, axis=0, keepdims=True))
        r32 = _row_iota((32, W_C))
        dcw = jnp.zeros((32, W_C), F32)
        for j in range(CONV_K):
            tap = jnp.sum(dy1 * ys_ref[pl.ds(pad - (CONV_K - 1) + j, tc), :], axis=0, keepdims=True)
            dcw = dcw + jnp.where(r32 == j, tap, 0.0)
        _acc(dcw_ref, first, dcw)
        ds_ref[0:tc, :] = dy1
        ds_ref[tc:tc + pad, :] = nx_ref[...]
        dy0 = None
        for j in range(CONV_K):
            term = cw_ref[j:j + 1, :] * ds_ref[pl.ds(CONV_K - 1 - j, tc), :]
            dy0 = term if dy0 is None else dy0 + term
        dp_ref[:, 0:W_C] = dy0 * sg
        dp_ref[:, W_C:2 * W_C] = dy0 * a * sg * (1.0 - sg)
        nx_ref[...] = dy1[0:pad, :]

    rev = lambda c: pl.BlockSpec((tc, W_C), lambda t, c=c: (nc - 1 - t, c))
    prev = lambda c: pl.BlockSpec((tc, W_C), lambda t, c=c: (jnp.maximum(nc - 2 - t, 0), c))
    full = lambda a: pl.BlockSpec(a.shape, lambda t: (0,) * a.ndim)
    params = [cw, cb, lg, lb, gg]
    vec = SDS((1, W_C), F32)
    outs = [SDS((s, 2 * W_C), F32), SDS((32, W_C), F32), vec, vec, vec, vec]
    return pl.pallas_call(
        body, name="conf_bwd", grid=(nc,),
        in_specs=[rev(3), rev(5), rev(6), prev(5), prev(6), rev(0)] + [full(a) for a in params],
        out_specs=[pl.BlockSpec((tc, 2 * W_C), lambda t: (nc - 1 - t, 0))]
        + [pl.BlockSpec(o.shape, lambda t: (0, 0)) for o in outs[1:]],
        out_shape=outs,
        scratch_shapes=[pltpu.VMEM((tc + pad, W_C), F32), pltpu.VMEM((tc + pad, W_C), F32), pltpu.VMEM((pad, W_C), F32)],
        compiler_params=_cp("arbitrary"),
    )(dy, proj, proj, proj, proj, y1, *params)


def _assemble_dproj(dlru, dq, dcur, dprev, dconf):
    s = dq.shape[0]
    nb = s // BLK

    def body(dl_ref, dq_ref, dc_ref, dn_ref, df_ref, o_ref):
        n = pl.program_id(0)
        o_ref[:, 0:512] = dl_ref[...].astype(BF16)
        o_ref[:, 512:1024] = dq_ref[...].astype(BF16)
        o_ref[:, 1024:1280] = (dc_ref[...] + jnp.where(n < nb - 1, dn_ref[...], 0.0)).astype(BF16)
        o_ref[:, 1280:1792] = df_ref[...].astype(BF16)

    wide = pl.BlockSpec((BLK, 512), lambda n: (n, 0))
    return pl.pallas_call(
        body, name="assemble_dproj", grid=(nb,),
        in_specs=[wide, wide, pl.BlockSpec((BLK, 256), lambda n: (n, 0)),
                  pl.BlockSpec((BLK, 256), lambda n: (jnp.minimum(n + 1, nb - 1), 0)), wide],
        out_specs=pl.BlockSpec((BLK, P_IN), lambda n: (n, 0)), out_shape=SDS((s, P_IN), BF16),
        compiler_params=_cp("parallel"),
    )(dlru, dq, dcur, dprev, dconf)


def _loss_grad(y, t, tm):
    s = y.shape[0]

    def body(y_ref, t_ref, dy_ref, l_ref):
        err = y_ref[...] - t_ref[...]
        dy_ref[...] = err * (1.0 / D)
        _acc(l_ref, pl.program_id(0) == 0, jnp.sum(err * err, axis=0, keepdims=True))

    row = pl.BlockSpec((tm, D), lambda i: (i, 0))
    return pl.pallas_call(
        body, name="loss_grad", grid=(s // tm,), in_specs=[row, row],
        out_specs=[row, pl.BlockSpec((1, D), lambda i: (0, 0))],
        out_shape=[SDS((s, D), F32), SDS((1, D), F32)], compiler_params=_cp("arbitrary"),
    )(y, t)


def _block_diag(w):
    rows = [jnp.concatenate([w[h] if k == h else jnp.zeros((64, 64), w.dtype) for k in range(4)], axis=1) for h in range(4)]
    return jnp.concatenate(rows, axis=0)


def _diag_blocks(m):
    return jnp.stack([m[64 * h:64 * (h + 1), 64 * h:64 * (h + 1)] for h in range(4)])


def _layer_params(small, l):
    v = lambda name: small[name][l].reshape(1, -1)
    gg = small["group_g"][l]
    return dict(
        ffn1_pre=v("ffn1_pre_g"), ffn1_post=v("ffn1_post_g"), mix_pre=v("mix_pre_g"), mix_post=v("mix_post_g"),
        ffn2_pre=v("ffn2_pre_g"), ffn2_post=v("ffn2_post_g"), lru_cb=v("lru_conv_b"),
        wa=_block_diag(small["lru_w_a"][l]).astype(BF16), ba=v("lru_b_a"),
        wx=_block_diag(small["lru_w_x"][l]).astype(BF16), bx=v("lru_b_x"), lam=v("lru_lambda"),
        sinks8=jnp.broadcast_to(small["attn_sinks"][l][:, None], (NQ, 128)),
        conv_b=v("conv_b"), ln_g=v("conv_ln_g"), ln_b=v("conv_ln_b"),
        gg_a=gg[0:W_A].reshape(1, -1), gg_b=gg[W_A:W_A + W_B].reshape(1, -1), gg_c=gg[W_A + W_B:].reshape(1, -1),
    )


def _forward_layer(x, weights, p, tiles, deps=()):
    _, mm, _, tc = tiles
    big = dict(weights("ffn1", x))
    p = dict(p)
    sv = dict(x0=x)
    h1, g1, u1, a1 = _ffn_up(x, p["ffn1_pre"], big["ffn1_w_gu"], 0, mm, deps)
    z1, x = _mm_rms_res(a1, big["ffn1_w_down"], 0, x, p["ffn1_post"], 0.5, mm, FH, "ffn_down")
    sv.update(h1=h1, g1=g1, u1=u1, a1=a1, z1=z1, x1=x)
    big.update(weights("mix", x))
    p.update(lru_cw=big.pop("lru_conv_w"), conv_w=big.pop("conv_w"))
    hn, proj = _proj(x, p["mix_pre"], big["w_in"], 0, mm)
    yn_a, hl = _lru_fwd(proj, p["lru_cw"], p["lru_cb"], p["wa"], p["ba"], p["wx"], p["bx"], p["lam"], p["gg_a"], tc)
    yn_b, ob = _attn_fwd(proj, p["sinks8"], p["gg_b"])
    yn_c, y1 = _conf_fwd(proj, p["conv_w"], p["conv_b"], p["ln_g"], p["ln_b"], p["gg_c"], tc)
    ycat = jnp.concatenate([yn_a, yn_b, yn_c], axis=1)
    zo, x = _mm_rms_res(ycat, big["w_out"], 0, x, p["mix_post"], 1.0, mm, D, "mix_out")
    sv.update(hn=hn, proj=proj, hl=hl, ob=ob, y1=y1, ycat=ycat, zo=zo, x2=x)
    big.update(weights("ffn2", x))
    h2, g2, u2, a2 = _ffn_up(x, p["ffn2_pre"], big["ffn2_w_gu"], 0, mm)
    z2, x = _mm_rms_res(a2, big["ffn2_w_down"], 0, x, p["ffn2_post"], 0.5, mm, FH, "ffn_down")
    sv.update(h2=h2, g2=g2, u2=u2, a2=a2, z2=z2, p=p, big=big)
    return x, sv


def _grad_buffers():
    empty = lambda *shape: lax.empty(shape, F32)
    return dict(ffn1_w_gu=empty(1, NSHARD, D, FH), ffn2_w_gu=empty(1, NSHARD, D, FH), ffn1_w_down=empty(1, 1, DFF, D),
                ffn2_w_down=empty(1, 1, DFF, D), w_in=empty(1, 1, D, P_IN), w_out=empty(1, 1, D, D))


def _backward_layer(dx, sv, bufs, tiles, stage):
    p, big = sv["p"], sv["big"]
    tm, mm, dw, tc = tiles
    gr = {}

    def ffn_bwd(dx, which, xin, h, g, u, a, z, pre, post, deps):
        dz, dpost = _rms_bwd(dx, z, post, 0.5, tm, "ffn_post_bwd", deps)
        dg, du = _ffn_bwd_mid(dz, big[which + "_w_down"], 0, g, u, mm)
        bufs[which + "_w_down"] = _mm_tn_into(bufs[which + "_w_down"], a, dz, 0, 0, FH, D, dw, "dw_down")
        bufs[which + "_w_gu"] = _mm_tn_into(bufs[which + "_w_gu"], h, dg, 0, 0, D, FH, dw, "dw_gate")
        bufs[which + "_w_gu"] = _mm_tn_into(bufs[which + "_w_gu"], h, du, 0, 2, D, FH, dw, "dw_up")
        dxn, dpre = _ffn_bwd_dh(dg, du, big[which + "_w_gu"], 0, xin, pre, dx, mm)
        return dxn, dpre, dpost

    dx, gr["ffn2_pre_g"], gr["ffn2_post_g"] = ffn_bwd(dx, "ffn2", sv["x2"], sv["h2"], sv["g2"], sv["u2"], sv["a2"],
                                                      sv["z2"], p["ffn2_pre"], p["ffn2_post"], stage({}, dx))
    done = {n: bufs[n] for n in ("ffn2_w_gu", "ffn2_w_down")}
    do, gr["mix_post_g"] = _rms_bwd(dx, sv["zo"], p["mix_post"], 1.0, tm, "mix_post_bwd", stage(done, dx))
    bufs["w_out"] = _mm_tn_into(bufs["w_out"], sv["ycat"], do, 0, 0, D, D, dw, "dw_out")
    dy = _mm_nt(do, big["w_out"], 0, mm, "mix_dy")
    proj = sv["proj"]
    (dlru, dcw, gr["lru_conv_b"], dwa, gr["lru_b_a"], dwx, gr["lru_b_x"], gr["lru_lambda"], dgg_a) = _lru_bwd(
        dy, proj, sv["hl"], p["lru_cw"], p["lru_cb"], p["wa"], p["ba"], p["wx"], p["bx"], p["lam"], p["gg_a"], tc)
    dq, dcur, dprev, dsk, dgg_b = _attn_bwd(dy, proj, sv["ob"], p["sinks8"], p["gg_b"])
    dconf, dconvw, gr["conv_b"], gr["conv_ln_g"], gr["conv_ln_b"], dgg_c = _conf_bwd(
        dy, proj, sv["y1"], p["conv_w"], p["conv_b"], p["ln_g"], p["ln_b"], p["gg_c"], tc)
    dproj = _assemble_dproj(dlru, dq, dcur, dprev, dconf)
    bufs["w_in"] = _mm_tn_into(bufs["w_in"], sv["hn"], dproj, 0, 0, D, P_IN, dw, "dw_in")
    dx, gr["mix_pre_g"] = _mm_nt_rmsbwd(dproj, big["w_in"], 0, sv["x1"], p["mix_pre"], dx, mm)
    gr["lru_conv_w"] = dcw[0:LRU_K]
    gr["lru_w_a"] = _diag_blocks(dwa)
    gr["lru_w_x"] = _diag_blocks(dwx)
    gr["attn_sinks"] = dsk[:, 0]
    gr["conv_w"] = dconvw[0:CONV_K]
    gr["group_g"] = jnp.concatenate([dgg_a, dgg_b, dgg_c], axis=1)
    dx, gr["ffn1_pre_g"], gr["ffn1_post_g"] = ffn_bwd(dx, "ffn1", sv["x0"], sv["h1"], sv["g1"], sv["u1"], sv["a1"],
                                                      sv["z1"], p["ffn1_pre"], p["ffn1_post"],
                                                      stage({n: bufs[n] for n in ("w_in", "w_out")}, dx))
    return dx, gr


def _tiles(s):
    return min(512, s), min(1024, s), min(2048, s), min(512, s // 2)


HBM_SPEC = pl.BlockSpec(memory_space=pltpu.HBM)
SEM_SPEC = pl.BlockSpec(memory_space=pltpu.SEMAPHORE)
EFFECT = pltpu.SideEffectType.DATAFLOW_SIDE_EFFECTING


def _place():
    x, y, c = lax.axis_index("x"), lax.axis_index("y"), lax.axis_index("c")
    return x, y, c, [(1 - x, y), (x, 1 - y), (1 - x, 1 - y)]


def _rcopy(src, dst, send_sems, recv_sems, k, to):
    return pltpu.make_async_remote_copy(src_ref=src, dst_ref=dst, send_sem=send_sems.at[k], recv_sem=recv_sems.at[k],
                                        device_id=to, device_id_type=MESH)


def _half(rows, which):
    return pl.ds(which * (rows // 2), rows // 2)


def _place_shard(w, l, p_idx, dtype):
    _, rows, cols = w.shape
    tr = _rows_per_block(rows, cols, 16) if rows % 16 == 0 else rows

    def body(p_ref, buf_ref, w_ref, o_ref):
        o_ref[...] = w_ref[...].astype(dtype)

    spec = pltpu.PrefetchScalarGridSpec(
        num_scalar_prefetch=1, grid=(rows // tr,),
        in_specs=[ANY, pl.BlockSpec((None, tr, cols), lambda i, pr: (l, i, 0))],
        out_specs=pl.BlockSpec((None, None, tr, cols), lambda i, pr: (0, pr[0], i, 0)))
    shape = (1, NSHARD, rows, cols)
    return pl.pallas_call(body, name="place_shard", grid_spec=spec, out_shape=SDS(shape, dtype),
                          input_output_aliases={1: 0}, compiler_params=_cp("parallel"),
                          )(p_idx, lax.empty(shape, dtype), w)


def _gather_two_level(bufs, n_halved):
    n = len(bufs)

    def body(*refs):
        outs = refs[n:2 * n]
        send_sems, recv_sems = refs[2 * n:]
        x, y, c, chips = _place()
        p = 2 * x + y
        me, sibling = (x, y, c), (x, y, 1 - c)

        def blk(a, q, half):
            return outs[a].at[0, q, _half(outs[a].shape[2], half)] if a < n_halved else outs[a].at[0, q]

        def cp(a, k, q, half, to):
            return _rcopy(blk(a, q, half), blk(a, q, half), send_sems, recv_sems, 6 * a + k, to)

        first = [cp(a, j, p, c, (*chip, c)) for a in range(n) for j, chip in enumerate(chips)]
        for d in first:
            d.start()
        passed = []
        for a in range(n):
            for j, chip in enumerate(chips):
                q = 2 * chip[0] + chip[1]
                cp(a, j, q, c, me).wait_recv()
                if a < n_halved:
                    passed.append(cp(a, 3 + j, q, c, sibling))
                    passed[-1].start()
        for a in range(n_halved):
            for j, chip in enumerate(chips):
                cp(a, 3 + j, 2 * chip[0] + chip[1], 1 - c, me).wait_recv()
        for d in first + passed:
            d.wait_send()

    return pl.pallas_call(
        body, name="gather_layer0", in_specs=[ANY] * n, out_specs=[ANY] * n,
        out_shape=[SDS(b.shape, b.dtype) for b in bufs], input_output_aliases={a: a for a in range(n)},
        scratch_shapes=[pltpu.SemaphoreType.DMA((6 * n,)), pltpu.SemaphoreType.DMA((6 * n,))],
    )(*bufs)


def _run_plans(plans, refs, send_sems, recv_sems):
    cps, b0, s0 = [], 0, 0
    for plan, nb, ns in plans:
        cps += plan(refs[b0:b0 + nb], send_sems, recv_sems, s0)
        b0, s0 = b0 + nb, s0 + ns
    return cps


def _exchange(name, bufs, plans):
    n = len(bufs)
    nsem = sum(ns for _, _, ns in plans)

    def body(*refs):
        cps = _run_plans(plans, refs[n:2 * n], refs[2 * n], refs[2 * n + 1])
        for cp in cps:
            cp.start()
        for cp in cps:
            cp.wait()

    return pl.pallas_call(
        body, name=name, in_specs=[ANY] * n, out_specs=[ANY] * n, out_shape=[SDS(b.shape, b.dtype) for b in bufs],
        input_output_aliases={a: a for a in range(n)},
        scratch_shapes=[pltpu.SemaphoreType.DMA((nsem,)), pltpu.SemaphoreType.DMA((nsem,))],
    )(*bufs)


def _exchange_start(name, bufs, plans, deps=()):
    n = len(bufs)
    nsem = sum(ns for _, _, ns in plans)
    deps = list(deps)
    first_out = n + len(deps)

    def body(*refs):
        for cp in _run_plans(plans, refs[:n], refs[first_out], refs[first_out + 1]):
            cp.start()
        token = refs[first_out + 2 + n]
        token[...] = jnp.zeros_like(token)

    outs = pl.pallas_call(
        body, name=name,
        out_shape=(pltpu.SemaphoreType.DMA((nsem,)), pltpu.SemaphoreType.DMA((nsem,)),
                   *[pltpu.HBM(b.shape, b.dtype) for b in bufs], SDS((8, 128), F32)),
        in_specs=[HBM_SPEC] * n + [ANY] * len(deps),
        out_specs=(SEM_SPEC, SEM_SPEC, *[HBM_SPEC] * n, pl.BlockSpec(memory_space=pltpu.VMEM)),
        input_output_aliases={a: 2 + a for a in range(n)},
        compiler_params=pltpu.CompilerParams(has_side_effects=EFFECT),
    )(*[pltpu.with_memory_space_constraint(b, pltpu.HBM) for b in bufs], *deps)
    return outs[0], outs[1], list(outs[2:2 + n]), outs[2 + n]


def _exchange_wait(name, send_sems, recv_sems, bufs, plans, after):
    n = len(bufs)

    def body(*refs):
        for cp in _run_plans(plans, refs[:n], refs[n], refs[n + 1]):
            cp.wait_send()
            cp.wait_recv()

    return pl.pallas_call(
        body, name=name, out_shape=[pltpu.HBM(b.shape, b.dtype) for b in bufs],
        in_specs=[HBM_SPEC] * n + [SEM_SPEC, SEM_SPEC, ANY], out_specs=[HBM_SPEC] * n,
        input_output_aliases={a: a for a in range(n)},
        compiler_params=pltpu.CompilerParams(has_side_effects=EFFECT),
    )(*bufs, send_sems, recv_sems, after)


def _plan_gather(refs, send_sems, recv_sems, base):
    x, y, c, chips = _place()
    p = 2 * x + y
    return [_rcopy(r.at[0, p], r.at[0, p], send_sems, recv_sems, base + 3 * a + j, (*chip, c))
            for a, r in enumerate(refs) for j, chip in enumerate(chips)]


def _plan_pair_exchange(refs, send_sems, recv_sems, base):
    x, y, c, _ = _place()
    n = len(refs) // 2
    return [_rcopy(refs[a].at[:, _half(refs[a].shape[1], 1 - c)], refs[n + a], send_sems, recv_sems, base + a,
                   (x, y, 1 - c)) for a in range(n)]


def _plan_chip_exchange(refs, send_sems, recv_sems, base):
    x, y, c, chips = _place()
    n = len(refs) // 2
    return [_rcopy(refs[a].at[2 * chip[0] + chip[1]], refs[n + a].at[j], send_sems, recv_sems, base + 3 * a + j,
                   (*chip, c)) for a in range(n) for j, chip in enumerate(chips)]


def _plan_pair_share(refs, send_sems, recv_sems, base):
    x, y, c, _ = _place()
    return [_rcopy(r.at[_half(r.shape[0], c)], r.at[_half(r.shape[0], c)], send_sems, recv_sems, base + a,
                   (x, y, 1 - c)) for a, r in enumerate(refs)]


def _allreduce_small(buf):
    rows = buf.shape[0]

    def body(in_ref, out_ref, gather_ref, send_sems, recv_sems):
        x, y, c, _ = _place()
        me = 4 * x + 2 * y + c
        gather_ref[me] = in_ref[...]
        cps, slots = [], []
        for m in range(1, NDEV):
            px = 1 - x if m & 4 else x
            py = 1 - y if m & 2 else y
            pc = 1 - c if m & 1 else c
            cps.append(_rcopy(in_ref, gather_ref.at[me], send_sems, recv_sems, m - 1, (px, py, pc)))
            slots.append(4 * px + 2 * py + pc)
        for cp in cps:
            cp.start()
        for m in range(1, NDEV):
            _rcopy(in_ref, gather_ref.at[slots[m - 1]], send_sems, recv_sems, m - 1, (x, y, c)).wait_recv()
        for cp in cps:
            cp.wait_send()
        total = gather_ref[0]
        for dev in range(1, NDEV):
            total = total + gather_ref[dev]
        out_ref[...] = total

    vm = pl.BlockSpec(memory_space=pltpu.VMEM)
    return pl.pallas_call(
        body, name="allreduce_small", in_specs=[vm], out_specs=vm, out_shape=SDS(buf.shape, F32),
        scratch_shapes=[pltpu.VMEM((NDEV, rows, 128), F32), pltpu.SemaphoreType.DMA((NDEV - 1,)),
                        pltpu.SemaphoreType.DMA((NDEV - 1,))],
        compiler_params=pltpu.CompilerParams(vmem_limit_bytes=VMEM_LIMIT),
    )(buf)


BLOCK_ELEMS = 256 * 1024


def _rows_per_block(rows, cols, mult):
    best = None
    for tr in range(mult, rows + 1, mult):
        if rows % tr == 0 and tr * cols <= BLOCK_ELEMS:
            best = tr
    assert best is not None, (rows, cols)
    return best


def _pair_sum(g, r, c_idx):
    nq, rows, cols = g.shape
    half = rows // 2
    tr = _rows_per_block(half, cols, 16)
    nb = half // tr

    def body(c_ref, g_ref, r_ref, t_ref):
        t_ref[...] = (g_ref[...] + r_ref[...]).astype(BF16)

    blk = pl.BlockSpec((None, tr, cols), lambda q, i, cr: (q, i, 0))
    spec = pltpu.PrefetchScalarGridSpec(
        num_scalar_prefetch=1, grid=(nq, nb),
        in_specs=[pl.BlockSpec((None, tr, cols), lambda q, i, cr: (q, cr[0] * nb + i, 0)), blk], out_specs=blk)
    return pl.pallas_call(body, name="grad_pair_sum", grid_spec=spec, out_shape=SDS((nq, half, cols), BF16),
                          compiler_params=_cp("parallel", "parallel"))(c_idx, g, r)


def _chip_sum(g, r, rr, cp_idx):
    _, rows, cols = g.shape
    half = rows // 2
    tr = _rows_per_block(half, cols, 16)
    nb = half // tr

    def body(cp_ref, buf_ref, g_ref, r_ref, rr_ref, o_ref):
        o_ref[...] = ((g_ref[...] + r_ref[...]) + rr_ref[0].astype(F32) + rr_ref[1].astype(F32) + rr_ref[2].astype(F32))

    spec = pltpu.PrefetchScalarGridSpec(
        num_scalar_prefetch=1, grid=(nb,),
        in_specs=[ANY, pl.BlockSpec((None, tr, cols), lambda i, cp: (cp[1], cp[0] * nb + i, 0)),
                  pl.BlockSpec((None, tr, cols), lambda i, cp: (cp[1], i, 0)),
                  pl.BlockSpec((3, tr, cols), lambda i, cp: (0, i, 0))],
        out_specs=pl.BlockSpec((tr, cols), lambda i, cp: (cp[0] * nb + i, 0)))
    return pl.pallas_call(body, name="grad_chip_sum", grid_spec=spec, out_shape=SDS((rows, cols), F32),
                          input_output_aliases={1: 0}, compiler_params=_cp("parallel"),
                          )(cp_idx, lax.empty((rows, cols), F32), g, r, rr)


def _adamw_math(w, g, m, v):
    mn = ADAM_B1 * m + (1.0 - ADAM_B1) * g
    vn = ADAM_B2 * v + (1.0 - ADAM_B2) * (g * g)
    m_hat = mn / (1.0 - ADAM_B1 ** ADAM_STEP)
    v_hat = vn / (1.0 - ADAM_B2 ** ADAM_STEP)
    return -ADAM_LR * (m_hat / (jnp.sqrt(v_hat) + ADAM_EPS) + ADAM_WD * w), mn, vn


def _adamw_layers(w, gs, m, v):
    depth, rows, cols = w.shape
    tr = _rows_per_block(rows, cols, 8)

    def body(w_ref, g0_ref, g1_ref, m_ref, v_ref, go_ref, d_ref, mo_ref, vo_ref):
        gg = jnp.where(pl.program_id(0) == 0, g0_ref[...], g1_ref[...])
        go_ref[...] = gg
        d_ref[...], mo_ref[...], vo_ref[...] = _adamw_math(w_ref[...], gg, m_ref[...], v_ref[...])

    blk = pl.BlockSpec((None, tr, cols), lambda l, i: (l, i, 0))
    return pl.pallas_call(
        body, name="adamw_layers", grid=(depth, rows // tr),
        in_specs=[blk, pl.BlockSpec((tr, cols), lambda l, i: (i * (1 - l), 0)),
                  pl.BlockSpec((tr, cols), lambda l, i: (i * l, 0)), blk, blk],
        out_specs=[blk] * 4, out_shape=[SDS(w.shape, F32)] * 4,
        compiler_params=_cp("arbitrary", "arbitrary"))(w, gs[0], gs[1], m, v)


def _adamw_packed(w, g, m, v):
    def body(w_ref, g_ref, m_ref, v_ref, d_ref, mo_ref, vo_ref):
        d_ref[...], mo_ref[...], vo_ref[...] = _adamw_math(w_ref[...], g_ref[...], m_ref[...], v_ref[...])

    vm = pl.BlockSpec(memory_space=pltpu.VMEM)
    return pl.pallas_call(body, name="adamw_packed", in_specs=[vm] * 4, out_specs=[vm] * 3,
                          out_shape=[SDS(w.shape, F32)] * 3,
                          compiler_params=pltpu.CompilerParams(vmem_limit_bytes=VMEM_LIMIT))(w, g, m, v)


_WEIGHTS = ["ffn1_pre_g", "ffn1_w_gu", "ffn1_w_down", "ffn1_post_g", "mix_pre_g", "w_in", "lru_conv_w", "lru_conv_b",
            "lru_w_a", "lru_b_a", "lru_w_x", "lru_b_x", "lru_lambda", "attn_sinks", "conv_w", "conv_b", "conv_ln_g",
            "conv_ln_b", "group_g", "w_out", "mix_post_g", "ffn2_pre_g", "ffn2_w_gu", "ffn2_w_down", "ffn2_post_g"]
_INPUTS = ["x"] + _WEIGHTS + ["loss_target"] + ["m_" + n for n in _WEIGHTS] + ["v_" + n for n in _WEIGHTS]
_BIG = ["ffn1_w_gu", "ffn1_w_down", "w_in", "w_out", "ffn2_w_gu", "ffn2_w_down"]
_SMALL_SHARDED = ["lru_conv_w", "conv_w"]
_SMALL_REPL = [n for n in _WEIGHTS if n not in _BIG and n not in _SMALL_SHARDED]

PACK_TILE = 8 * 128


def _pack(arrs):
    parts = []
    for a in arrs:
        flat = a.reshape(-1)
        parts.append(jnp.pad(flat, (0, -flat.shape[0] % PACK_TILE)).reshape(-1, 128))
    return jnp.concatenate(parts, axis=0)


def _unpack(buf, shapes):
    out, row = [], 0
    for shp in shapes:
        size = math.prod(shp)
        nrow = -(-size // PACK_TILE) * 8
        out.append(buf[row:row + nrow].reshape(-1)[:size].reshape(shp))
        row += nrow
    return out


def _unshard_cols(a):
    return a.transpose(0, 2, 1, 3).reshape(1, a.shape[2], NSHARD * a.shape[3])


_GROUPS = dict(ffn1=["ffn1_w_gu", "ffn1_w_down"], mix=["w_in", "w_out", "lru_conv_w", "conv_w"],
               ffn2=["ffn2_w_gu", "ffn2_w_down"])


def _full_weights(group, gathered):
    g = dict(zip(_GROUPS[group], gathered))
    if group == "mix":
        return dict(w_in=_unshard_cols(g["w_in"]), w_out=g["w_out"].reshape(1, D, D),
                    lru_conv_w=_unshard_cols(g["lru_conv_w"])[0], conv_w=_unshard_cols(g["conv_w"])[0])
    return {group + "_w_gu": g[group + "_w_gu"], group + "_w_down": g[group + "_w_down"].reshape(1, DFF, D)}


def _by_shard(name, buf):
    if name.endswith("w_gu"):
        return buf[0]
    if name == "w_in":
        return buf.reshape(D, NSHARD, P_IN // NSHARD).transpose(1, 0, 2)
    return buf.reshape(NSHARD, buf.shape[2] // NSHARD, buf.shape[3])


class _Reducer:
    PLANS = (_plan_pair_exchange, _plan_chip_exchange, _plan_pair_share)

    def __init__(self, keys, gs, c_idx, cp_idx):
        self.keys, self.gs, self.c_idx, self.cp_idx = keys, gs, c_idx, cp_idx
        self.n = len(gs)
        self.step = 0
        self.result = None

    def inputs(self):
        n = self.n
        if self.step == 0:
            bufs = self.gs + [lax.empty((NSHARD, g.shape[1] // 2, g.shape[2]), F32) for g in self.gs]
        elif self.step == 1:
            ts = [_pair_sum(g, r, self.c_idx) for g, r in zip(self.gs, self.rs)]
            bufs = ts + [lax.empty((3,) + t.shape[1:], BF16) for t in ts]
        else:
            bufs = [_chip_sum(g, r, rr, self.cp_idx) for g, r, rr in zip(self.gs, self.rs, self.rrs)]
        return bufs, (self.PLANS[self.step], len(bufs), (n, 3 * n, n)[self.step])

    def absorb(self, done):
        n = self.n
        if self.step == 0:
            self.gs, self.rs = done[:n], done[n:]
        elif self.step == 1:
            self.rrs = done[n:]
        else:
            self.result = dict(zip(self.keys, done))
        self.step += 1


class _ReducePipeline:
    def __init__(self, c_idx, cp_idx):
        self.c_idx, self.cp_idx = c_idx, cp_idx
        self.reducers, self.flying, self.calls = [], None, 0

    def add(self, layer, done):
        if done:
            keys = [(layer, n) for n in done]
            self.reducers.append(_Reducer(keys, [_by_shard(n, b) for n, b in done.items()], self.c_idx, self.cp_idx))

    def _next(self):
        active = [r for r in self.reducers if r.step < 3]
        bufs, plans = [], []
        for r in active:
            b, triple = r.inputs()
            bufs += b
            plans.append(triple)
        self.calls += 1
        return active, bufs, plans, "grad_exchange%d" % self.calls

    def _absorb(self, active, plans, done):
        at = 0
        for r, (_, nb, _) in zip(active, plans):
            r.absorb(done[at:at + nb])
            at += nb

    def _land(self, after):
        if self.flying is not None:
            active, plans, name, send_sems, recv_sems, bufs = self.flying
            self._absorb(active, plans, _exchange_wait(name + "_wait", send_sems, recv_sems, bufs, plans, after))
            self.flying = None

    def hook(self, after):
        self._land(after)
        active, bufs, plans, name = self._next()
        if not active:
            return []
        send_sems, recv_sems, bufs, token = _exchange_start(name + "_start", bufs, plans)
        self.flying = (active, plans, name, send_sems, recv_sems, bufs)
        return [token]

    def finish(self, after):
        self._land(after)
        while True:
            active, bufs, plans, name = self._next()
            if not active:
                break
            self._absorb(active, plans, _exchange(name, bufs, plans))
        out = {}
        for r in self.reducers:
            out.update(r.result)
        return out


def kernel(*args):
    d = dict(zip(_INPUTS, args, strict=True))
    xi, yi, ci = lax.axis_index("x"), lax.axis_index("y"), lax.axis_index("c")
    p = 2 * xi + yi
    c_idx = jnp.reshape(ci, (1,)).astype(jnp.int32)
    p_idx = jnp.reshape(p, (1,)).astype(jnp.int32)
    cp_idx = jnp.stack([ci, p]).astype(jnp.int32)
    x, target = d["x"][0], d["loss_target"][0]
    tiles = _tiles(x.shape[0])

    groups = [(l, grp) for l in range(DEPTH) for grp in _GROUPS]
    placed = {(l, grp): [_place_shard(d[n], l, p_idx, BF16 if n in _BIG else F32) for n in _GROUPS[grp]]
              for l, grp in groups}
    ready = {groups[0]: _gather_two_level(placed[groups[0]], len(placed[groups[0]]))}
    flying, tokens = {}, [ready[groups[0]][0]]
    for l, grp in groups[1:]:
        plans = [(_plan_gather, len(placed[l, grp]), 3 * len(placed[l, grp]))]
        send_sems, recv_sems, bufs, token = _exchange_start("gather_l%d_%s_start" % (l, grp), placed[l, grp], plans,
                                                             tokens[-1:])
        flying[l, grp] = (send_sems, recv_sems, bufs, plans)
        tokens.append(token)

    def weights_of(l):
        def weights(grp, after):
            if (l, grp) not in ready:
                send_sems, recv_sems, bufs, plans = flying[l, grp]
                ready[l, grp] = _exchange_wait("gather_l%d_%s_wait" % (l, grp), send_sems, recv_sems, bufs, plans, after)
            return _full_weights(grp, ready[l, grp])
        return weights

    small = {n: d[n] for n in _SMALL_REPL}
    x1, sv0 = _forward_layer(x, weights_of(0), _layer_params(small, 0), tiles, tokens[1:])
    x2, sv1 = _forward_layer(x1, weights_of(1), _layer_params(small, 1), tiles)
    dx, lcols = _loss_grad(x2, target, tiles[0])

    pipe = _ReducePipeline(c_idx, cp_idx)
    sgrads = [None] * DEPTH
    for l, sv in ((1, sv1), (0, sv0)):
        bufs = _grad_buffers()

        def stage(done, dx, l=l):
            pipe.add(l, done)
            return pipe.hook(dx)

        dx, sgrads[l] = _backward_layer(dx, sv, bufs, tiles, stage)
        pipe.add(l, {n: bufs[n] for n in ("ffn1_w_gu", "ffn1_w_down")})
    grad_x = dx
    reduced = pipe.finish(grad_x)

    stacked = {n: jnp.stack([sgrads[l][n].reshape(d[n].shape[1:]) for l in range(DEPTH)]) for n in _SMALL_REPL}
    for n in _SMALL_SHARDED:
        stacked[n] = jnp.stack([sgrads[l][n] for l in range(DEPTH)])
    loss_part = jnp.pad((0.5 / D) * jnp.sum(lcols).reshape(1), (0, 127))
    order = _SMALL_REPL + _SMALL_SHARDED
    summed = _unpack(_allreduce_small(_pack([loss_part] + [stacked[n] for n in order])),
                     [(128,)] + [stacked[n].shape for n in order])
    loss = summed[0][0]
    grads = {}
    for n, g in zip(order, summed[1:]):
        if n in _SMALL_SHARDED:
            g = lax.dynamic_slice_in_dim(g, p * (g.shape[2] // NSHARD), g.shape[2] // NSHARD, axis=2)
        grads[n] = g

    delta, new_m, new_v = {}, {}, {}
    for n in _BIG:
        grads[n], delta[n], new_m[n], new_v[n] = _adamw_layers(d[n], [reduced[l, n] for l in range(DEPTH)],
                                                                d["m_" + n], d["v_" + n])
    shapes = [d[n].shape for n in order]
    packed = [_pack([src(n) for n in order]) for src in
              (lambda n: d[n], lambda n: grads[n], lambda n: d["m_" + n], lambda n: d["v_" + n])]
    for out, res in zip((delta, new_m, new_v), _adamw_packed(*packed)):
        out.update(zip(order, _unpack(res, shapes)))

    return (loss, grad_x[None], *[grads[n] for n in _WEIGHTS], *[delta[n] for n in _WEIGHTS],
            *[new_m[n] for n in _WEIGHTS], *[new_v[n] for n in _WEIGHTS])
```

```python
import functools
import math

import jax
import jax.numpy as jnp
from jax import lax
from jax.experimental import pallas as pl
from jax.experimental.pallas import tpu as pltpu

F32 = jnp.float32
BF16 = jnp.bfloat16
SDS = jax.ShapeDtypeStruct

D = 1024
DFF = 2816
FH = DFF // 2
DEPTH = 2
W_A = 256
W_B = 512
W_C = 256
NQ = 8
HD = 64
BLK = 128
P_IN = 1792
LRU_K = 4
CONV_K = 31
LRU_C = 8.0
NORM_EPS = 1e-6
LN_EPS = 1e-5
NEG_BIG = -1e30
SCALE = 1.0 / math.sqrt(HD)

ADAM_LR = 0.001
ADAM_B1 = 0.9
ADAM_B2 = 0.999
ADAM_EPS = 1e-08
ADAM_WD = 0.01
ADAM_STEP = 10

VMEM_LIMIT = 56 * 1024 * 1024
NSHARD = 4
NDEV = 8

TN = (((0,), (0,)), ((), ()))
NT = (((1,), (1,)), ((), ()))

MESH = pl.DeviceIdType.MESH
ANY = pl.BlockSpec(memory_space=pl.ANY)


def _cp(*sem):
    return pltpu.CompilerParams(dimension_semantics=sem if sem else None, vmem_limit_bytes=VMEM_LIMIT)


def _rsq(x, eps):
    return lax.rsqrt(jnp.mean(x * x, axis=-1, keepdims=True) + eps)


def _rms_bwd_rows(x, g, dy):
    r = _rsq(x, NORM_EPS)
    xh = x * r
    dyg = dy * g
    dx = r * (dyg - xh * jnp.mean(dyg * xh, axis=-1, keepdims=True))
    return dx, dy * xh


def _sig(x):
    return jax.nn.sigmoid(x)


def _ffn_up(x, pre_g, wgu, l, tm, deps=()):
    s = x.shape[0]
    deps = list(deps)

    def body(x_ref, g_ref, wg_ref, wu_ref, *rest):
        h_ref, go_ref, uo_ref, a_ref = rest[len(deps):]

        @pl.when(pl.program_id(1) == 0)
        def _():
            xf = x_ref[...]
            h_ref[...] = (xf * _rsq(xf, NORM_EPS) * g_ref[...]).astype(BF16)

        h = h_ref[...]
        gg = jnp.dot(h, wg_ref[...], preferred_element_type=F32)
        uu = jnp.dot(h, wu_ref[...], preferred_element_type=F32)
        go_ref[...] = gg.astype(BF16)
        uo_ref[...] = uu.astype(BF16)
        a_ref[...] = (gg * _sig(gg) * uu).astype(BF16)

    wide = pl.BlockSpec((tm, FH), lambda i, j: (i, j))
    return pl.pallas_call(
        body, name="ffn_up", grid=(s // tm, 2),
        in_specs=[pl.BlockSpec((tm, D), lambda i, j: (i, 0)), pl.BlockSpec((1, D), lambda i, j: (0, 0)),
                  pl.BlockSpec((None, None, D, FH), lambda i, j: (l, j, 0, 0)),
                  pl.BlockSpec((None, None, D, FH), lambda i, j: (l, j + 2, 0, 0))] + [ANY] * len(deps),
        out_specs=[pl.BlockSpec((tm, D), lambda i, j: (i, 0)), wide, wide, wide],
        out_shape=[SDS((s, D), BF16), SDS((s, DFF), BF16), SDS((s, DFF), BF16), SDS((s, DFF), BF16)],
        compiler_params=_cp("parallel", "arbitrary"),
    )(x, pre_g, wgu, wgu, *deps)


def _mm_rms_res(a, w, l, x, g, c, tm, tk, name):
    s, k_dim = a.shape
    nk = k_dim // tk

    def body(a_ref, w_ref, x_ref, g_ref, z_ref, x1_ref):
        k = pl.program_id(1)
        p = jnp.dot(a_ref[...], w_ref[...], preferred_element_type=F32)

        @pl.when(k == 0)
        def _():
            z_ref[...] = p

        @pl.when(k > 0)
        def _():
            z_ref[...] += p

        @pl.when(k == nk - 1)
        def _():
            z = z_ref[...]
            x1_ref[...] = x_ref[...] + c * (z * _rsq(z, NORM_EPS) * g_ref[...])

    row = pl.BlockSpec((tm, D), lambda i, k: (i, 0))
    return pl.pallas_call(
        body, name=name, grid=(s // tm, nk),
        in_specs=[pl.BlockSpec((tm, tk), lambda i, k: (i, k)), pl.BlockSpec((None, tk, D), lambda i, k: (l, k, 0)),
                  row, pl.BlockSpec((1, D), lambda i, k: (0, 0))],
        out_specs=[row, row],
        out_shape=[SDS((s, D), F32), SDS((s, D), F32)],
        compiler_params=_cp("parallel", "arbitrary"),
    )(a, w, x, g)


def _rms_bwd(dy, z, g, c, tm, name, deps=()):
    s = z.shape[0]
    deps = list(deps)

    def body(dy_ref, z_ref, g_ref, *rest):
        dz_ref, dg_ref = rest[len(deps):]
        dz, dgr = _rms_bwd_rows(z_ref[...], g_ref[...], c * dy_ref[...])
        dz_ref[...] = dz.astype(BF16)
        part = jnp.sum(dgr, axis=0, keepdims=True)

        @pl.when(pl.program_id(0) == 0)
        def _():
            dg_ref[...] = part

        @pl.when(pl.program_id(0) > 0)
        def _():
            dg_ref[...] += part

    row = pl.BlockSpec((tm, D), lambda i: (i, 0))
    vec = pl.BlockSpec((1, D), lambda i: (0, 0))
    return pl.pallas_call(
        body, name=name, grid=(s // tm,), in_specs=[row, row, vec] + [ANY] * len(deps), out_specs=[row, vec],
        out_shape=[SDS((s, D), BF16), SDS((1, D), F32)], compiler_params=_cp("arbitrary"),
    )(dy, z, g, *deps)


def _ffn_bwd_mid(dz, wd, l, g, u, tm):
    s = dz.shape[0]

    def body(dz_ref, wd_ref, g_ref, u_ref, dg_ref, du_ref):
        da = lax.dot_general(dz_ref[...], wd_ref[...], NT, preferred_element_type=F32)
        gg = g_ref[...].astype(F32)
        uu = u_ref[...].astype(F32)
        sg = _sig(gg)
        dg_ref[...] = (da * uu * sg * (1.0 + gg * (1.0 - sg))).astype(BF16)
        du_ref[...] = (da * gg * sg).astype(BF16)

    wide = pl.BlockSpec((tm, FH), lambda i, j: (i, j))
    return pl.pallas_call(
        body, name="ffn_bwd_mid", grid=(s // tm, 2),
        in_specs=[pl.BlockSpec((tm, D), lambda i, j: (i, 0)), pl.BlockSpec((None, FH, D), lambda i, j: (l, j, 0)), wide, wide],
        out_specs=[wide, wide],
        out_shape=[SDS((s, DFF), BF16), SDS((s, DFF), BF16)],
        compiler_params=_cp("parallel", "arbitrary"),
    )(dz, wd, g, u)


def _ffn_bwd_dh(dg, du, wgu, l, x, pre_g, dx1, tm):
    s = x.shape[0]

    def body(dg_ref, du_ref, wg_ref, wu_ref, x_ref, g_ref, dx1_ref, dx_ref, dgp_ref):
        i, k = pl.program_id(0), pl.program_id(1)
        p = (lax.dot_general(dg_ref[...], wg_ref[...], NT, preferred_element_type=F32)
             + lax.dot_general(du_ref[...], wu_ref[...], NT, preferred_element_type=F32))

        @pl.when(k == 0)
        def _():
            dx_ref[...] = p

        @pl.when(k == 1)
        def _():
            dx, dgr = _rms_bwd_rows(x_ref[...], g_ref[...], dx_ref[...] + p)
            dx_ref[...] = dx1_ref[...] + dx
            part = jnp.sum(dgr, axis=0, keepdims=True)

            @pl.when(i == 0)
            def _():
                dgp_ref[...] = part

            @pl.when(i > 0)
            def _():
                dgp_ref[...] += part

    wide = pl.BlockSpec((tm, FH), lambda i, k: (i, k))
    row = pl.BlockSpec((tm, D), lambda i, k: (i, 0))
    vec = pl.BlockSpec((1, D), lambda i, k: (0, 0))
    return pl.pallas_call(
        body, name="ffn_bwd_dh", grid=(s // tm, 2),
        in_specs=[wide, wide, pl.BlockSpec((None, None, D, FH), lambda i, k: (l, k, 0, 0)),
                  pl.BlockSpec((None, None, D, FH), lambda i, k: (l, k + 2, 0, 0)), row, vec, row],
        out_specs=[row, vec],
        out_shape=[SDS((s, D), F32), SDS((1, D), F32)],
        compiler_params=_cp("arbitrary", "arbitrary"),
    )(dg, du, wgu, wgu, x, pre_g, dx1)


def _mm_tn_into(buf, a, b, l, joff, tka, tn, ts, name):
    s, ka = a.shape
    n = b.shape[1]

    def body(buf_ref, a_ref, b_ref, o_ref):
        p = lax.dot_general(a_ref[...], b_ref[...], TN, preferred_element_type=F32)

        @pl.when(pl.program_id(2) == 0)
        def _():
            o_ref[...] = p

        @pl.when(pl.program_id(2) > 0)
        def _():
            o_ref[...] += p

    return pl.pallas_call(
        body, name=name, grid=(ka // tka, n // tn, s // ts),
        in_specs=[pl.BlockSpec(memory_space=pl.ANY),
                  pl.BlockSpec((ts, tka), lambda ia, j, t: (t, ia)), pl.BlockSpec((ts, tn), lambda ia, j, t: (t, j))],
        out_specs=pl.BlockSpec((None, None, tka, tn), lambda ia, j, t: (l, joff + j, ia, 0)),
        out_shape=SDS(buf.shape, F32), input_output_aliases={0: 0},
        compiler_params=_cp("parallel", "parallel", "arbitrary"),
    )(buf, a, b)


def _proj(x, g, w_in, l, tm):
    s = x.shape[0]

    def body(x_ref, g_ref, w_ref, h_ref, p_ref):
        xf = x_ref[...]
        h = (xf * _rsq(xf, NORM_EPS) * g_ref[...]).astype(BF16)
        h_ref[...] = h
        p_ref[...] = jnp.dot(h, w_ref[...], preferred_element_type=F32)

    return pl.pallas_call(
        body, name="proj", grid=(s // tm,),
        in_specs=[pl.BlockSpec((tm, D), lambda i: (i, 0)), pl.BlockSpec((1, D), lambda i: (0, 0)),
                  pl.BlockSpec((None, D, P_IN), lambda i: (l, 0, 0))],
        out_specs=[pl.BlockSpec((tm, D), lambda i: (i, 0)), pl.BlockSpec((tm, P_IN), lambda i: (i, 0))],
        out_shape=[SDS((s, D), BF16), SDS((s, P_IN), F32)],
        compiler_params=_cp("parallel"),
    )(x, g, w_in)


def _mm_nt(a, w, l, tm, name):
    s, k_dim = a.shape
    n = w.shape[1]

    def body(a_ref, w_ref, o_ref):
        o_ref[...] = lax.dot_general(a_ref[...], w_ref[...], NT, preferred_element_type=F32)

    return pl.pallas_call(
        body, name=name, grid=(s // tm,),
        in_specs=[pl.BlockSpec((tm, k_dim), lambda i: (i, 0)), pl.BlockSpec((None, n, k_dim), lambda i: (l, 0, 0))],
        out_specs=pl.BlockSpec((tm, n), lambda i: (i, 0)),
        out_shape=SDS((s, n), F32), compiler_params=_cp("parallel"),
    )(a, w)


def _mm_nt_rmsbwd(dp, w_in, l, x, g, dx1, tm):
    s = x.shape[0]

    def body(dp_ref, w_ref, x_ref, g_ref, dx1_ref, dx_ref, dg_ref):
        dh = lax.dot_general(dp_ref[...], w_ref[...], NT, preferred_element_type=F32)
        dx, dgr = _rms_bwd_rows(x_ref[...], g_ref[...], dh)
        dx_ref[...] = dx1_ref[...] + dx
        part = jnp.sum(dgr, axis=0, keepdims=True)

        @pl.when(pl.program_id(0) == 0)
        def _():
            dg_ref[...] = part

        @pl.when(pl.program_id(0) > 0)
        def _():
            dg_ref[...] += part

    row = pl.BlockSpec((tm, D), lambda i: (i, 0))
    vec = pl.BlockSpec((1, D), lambda i: (0, 0))
    return pl.pallas_call(
        body, name="mix_bwd_dx", grid=(s // tm,),
        in_specs=[pl.BlockSpec((tm, P_IN), lambda i: (i, 0)), pl.BlockSpec((None, D, P_IN), lambda i: (l, 0, 0)), row, vec, row],
        out_specs=[row, vec], out_shape=[SDS((s, D), F32), SDS((1, D), F32)],
        compiler_params=_cp("arbitrary"),
    )(dp, w_in, x, g, dx1)


def _row_iota(shape):
    return lax.broadcasted_iota(jnp.int32, shape, 0)


def _lru_gates(xc, wa_ref, ba_ref, wx_ref, bx_ref, lam_ref):
    xb = xc.astype(BF16)
    r = _sig(jnp.dot(xb, wa_ref[...], preferred_element_type=F32) + ba_ref[...])
    ig = _sig(jnp.dot(xb, wx_ref[...], preferred_element_type=F32) + bx_ref[...])
    nl = -lam_ref[...]
    sp = jnp.maximum(nl, 0.0) + jnp.log(1.0 + jnp.exp(-jnp.abs(nl)))
    log_a = -LRU_C * r * sp
    a = jnp.exp(log_a)
    x2 = 2.0 * log_a
    series = x2 * (1.0 + x2 * (0.5 + x2 * (1.0 / 6.0 + x2 * (1.0 / 24.0 + x2 * (1.0 / 120.0)))))
    em1 = jnp.where(x2 > -0.05, series, jnp.exp(x2) - 1.0)
    mlt = jnp.sqrt(-em1)
    return r, ig, a, mlt, sp


def _conv_taps(src_ref, w_ref, k_taps, pad, tc):
    acc = None
    for j in range(k_taps):
        term = w_ref[j:j + 1, :] * src_ref[pl.ds(pad - (k_taps - 1) + j, tc), :]
        acc = term if acc is None else acc + term
    return acc


def _gelu_parts(x):
    c0 = math.sqrt(2.0 / math.pi)
    inner = c0 * (x + 0.044715 * x * x * x)
    t = jnp.tanh(inner)
    gl = 0.5 * x * (1.0 + t)
    dgl = 0.5 * (1.0 + t) + 0.5 * x * (1.0 - t * t) * c0 * (1.0 + 3.0 * 0.044715 * x * x)
    return gl, dgl


def _lru_fwd(proj, cw, cb, wa, ba, wx, bx, lam, gg, tc):
    s = proj.shape[0]
    pad = 8

    def body(xcur_ref, xprev_ref, gate_ref, cw_ref, cb_ref, wa_ref, ba_ref, wx_ref, bx_ref, lam_ref, gg_ref,
             yn_ref, h_ref, xs_ref, hc_ref):
        i = pl.program_id(0)

        @pl.when(i == 0)
        def _():
            hc_ref[...] = jnp.zeros_like(hc_ref)

        xs_ref[0:pad, :] = jnp.where(i > 0, xprev_ref[tc - pad:tc, :], 0.0)
        xs_ref[pad:pad + tc, :] = xcur_ref[...]
        xc = _conv_taps(xs_ref, cw_ref, LRU_K, pad, tc) + cb_ref[...]
        _, ig, a, mlt, _ = _lru_gates(xc, wa_ref, ba_ref, wx_ref, bx_ref, lam_ref)
        u = mlt * (ig * xc)
        row = _row_iota((tc, W_A))
        d = 1
        while d < tc:
            ok = row >= d
            a_sh = jnp.where(ok, pltpu.roll(a, d, axis=0), 1.0)
            u_sh = jnp.where(ok, pltpu.roll(u, d, axis=0), 0.0)
            u = a * u_sh + u
            a = a * a_sh
            d *= 2
        h = u + a * hc_ref[...]
        hc_ref[...] = jnp.sum(jnp.where(row == tc - 1, h, 0.0), axis=0, keepdims=True)
        h_ref[...] = h
        gl, _ = _gelu_parts(gate_ref[...])
        ya = gl * h
        yn_ref[...] = (ya * _rsq(ya, NORM_EPS) * gg_ref[...]).astype(BF16)

    blk = lambda c: pl.BlockSpec((tc, W_A), lambda i, c=c: (i, c))
    full = lambda a: pl.BlockSpec(a.shape, lambda i: (0,) * a.ndim)
    params = [cw, cb, wa, ba, wx, bx, lam, gg]
    return pl.pallas_call(
        body, name="lru_fwd", grid=(s // tc,),
        in_specs=[blk(0), pl.BlockSpec((tc, W_A), lambda i: (jnp.maximum(i - 1, 0), 0)), blk(1)] + [full(a) for a in params],
        out_specs=[pl.BlockSpec((tc, W_A), lambda i: (i, 0))] * 2,
        out_shape=[SDS((s, W_A), BF16), SDS((s, W_A), F32)],
        scratch_shapes=[pltpu.VMEM((tc + pad, W_A), F32), pltpu.VMEM((1, W_A), F32)],
        compiler_params=_cp("arbitrary"),
    )(proj, proj, proj, *params)


def _acc(ref, first, val):
    @pl.when(first)
    def _():
        ref[...] = val

    @pl.when(jnp.logical_not(first))
    def _():
        ref[...] += val


def _lru_bwd(dy, proj, h, cw, cb, wa, ba, wx, bx, lam, gg, tc):
    s = proj.shape[0]
    nc = s // tc
    pad = 8

    def body(dy_ref, xcur_ref, xprev_ref, gate_ref, h_ref, hprev_ref, cw_ref, cb_ref, wa_ref, ba_ref, wx_ref, bx_ref,
             lam_ref, gg_ref,
             dp_ref, dcw_ref, dcb_ref, dwa_ref, dba_ref, dwx_ref, dbx_ref, dlam_ref, dgg_ref,
             xs_ref, ds_ref, mu_ref, nx_ref):
        step = pl.program_id(0)
        i = nc - 1 - step
        first = step == 0

        @pl.when(first)
        def _():
            mu_ref[...] = jnp.zeros_like(mu_ref)
            nx_ref[...] = jnp.zeros_like(nx_ref)

        xs_ref[0:pad, :] = jnp.where(i > 0, xprev_ref[tc - pad:tc, :], 0.0)
        xs_ref[pad:pad + tc, :] = xcur_ref[...]
        xc = _conv_taps(xs_ref, cw_ref, LRU_K, pad, tc) + cb_ref[...]
        r, ig, a, mlt, sp = _lru_gates(xc, wa_ref, ba_ref, wx_ref, bx_ref, lam_ref)
        hh = h_ref[...]
        gate = gate_ref[...]
        gl, dgl = _gelu_parts(gate)
        ya = gl * hh
        dya, dggr = _rms_bwd_rows(ya, gg_ref[...], dy_ref[...])
        _acc(dgg_ref, first, jnp.sum(dggr, axis=0, keepdims=True))
        dp_ref[:, W_A:2 * W_A] = dya * hh * dgl
        dh = dya * gl

        row = _row_iota((tc, W_A))
        aa = a
        uu = a * dh
        d = 1
        while d < tc:
            ok = row < tc - d
            a_sh = jnp.where(ok, pltpu.roll(aa, tc - d, axis=0), 1.0)
            u_sh = jnp.where(ok, pltpu.roll(uu, tc - d, axis=0), 0.0)
            uu = uu + aa * u_sh
            aa = aa * a_sh
            d *= 2
        cin = mu_ref[...]
        mu = uu + aa * cin
        lam_t = dh + jnp.where(row == tc - 1, cin, pltpu.roll(mu, tc - 1, axis=0))
        mu_ref[...] = jnp.sum(jnp.where(row == 0, mu, 0.0), axis=0, keepdims=True)
        hm1 = jnp.where(row == 0, jnp.where(i > 0, pltpu.roll(hprev_ref[...], 1, axis=0), 0.0),
                        pltpu.roll(hh, 1, axis=0))
        da = lam_t * hm1
        du = lam_t
        dmlt = du * ig * xc
        dig = du * mlt * xc
        dxc = du * mlt * ig
        dlog_a = da * a - dmlt * (a * a / mlt)
        dr = dlog_a * (-LRU_C * sp)
        dsp = jnp.sum(dlog_a * (-LRU_C * r), axis=0, keepdims=True)
        _acc(dlam_ref, first, dsp * (-_sig(-lam_ref[...])))
        dga = dr * r * (1.0 - r)
        dgx = dig * ig * (1.0 - ig)
        _acc(dba_ref, first, jnp.sum(dga, axis=0, keepdims=True))
        _acc(dbx_ref, first, jnp.sum(dgx, axis=0, keepdims=True))
        xb = xc.astype(BF16)
        dgab = dga.astype(BF16)
        dgxb = dgx.astype(BF16)
        _acc(dwa_ref, first, lax.dot_general(xb, dgab, TN, preferred_element_type=F32))
        _acc(dwx_ref, first, lax.dot_general(xb, dgxb, TN, preferred_element_type=F32))
        dxc = (dxc + lax.dot_general(dgab, wa_ref[...], NT, preferred_element_type=F32)
               + lax.dot_general(dgxb, wx_ref[...], NT, preferred_element_type=F32))

        _acc(dcb_ref, first, jnp.sum(dxc, axis=0, keepdims=True))
        r8 = _row_iota((8, W_A))
        dcw = jnp.zeros((8, W_A), F32)
        for j in range(LRU_K):
            tap = jnp.sum(dxc * xs_ref[pl.ds(pad - (LRU_K - 1) + j, tc), :], axis=0, keepdims=True)
            dcw = dcw + jnp.where(r8 == j, tap, 0.0)
        _acc(dcw_ref, first, dcw)
        ds_ref[0:tc, :] = dxc
        ds_ref[tc:tc + pad, :] = nx_ref[...]
        dlx = None
        for j in range(LRU_K):
            term = cw_ref[j:j + 1, :] * ds_ref[pl.ds(LRU_K - 1 - j, tc), :]
            dlx = term if dlx is None else dlx + term
        dp_ref[:, 0:W_A] = dlx
        nx_ref[...] = dxc[0:pad, :]

    rev = lambda c: pl.BlockSpec((tc, W_A), lambda t, c=c: (nc - 1 - t, c))
    prev = lambda c: pl.BlockSpec((tc, W_A), lambda t, c=c: (jnp.maximum(nc - 2 - t, 0), c))
    full = lambda a: pl.BlockSpec(a.shape, lambda t: (0,) * a.ndim)
    params = [cw, cb, wa, ba, wx, bx, lam, gg]
    vec = SDS((1, W_A), F32)
    sq = SDS((W_A, W_A), F32)
    outs = [SDS((s, 2 * W_A), F32), SDS((8, W_A), F32), vec, sq, vec, sq, vec, vec, vec]
    return pl.pallas_call(
        body, name="lru_bwd", grid=(nc,),
        in_specs=[rev(0), rev(0), prev(0), rev(1), rev(0), prev(0)] + [full(a) for a in params],
        out_specs=[pl.BlockSpec((tc, 2 * W_A), lambda t: (nc - 1 - t, 0))]
        + [pl.BlockSpec(o.shape, lambda t: (0, 0)) for o in outs[1:]],
        out_shape=outs,
        scratch_shapes=[pltpu.VMEM((tc + pad, W_A), F32), pltpu.VMEM((tc + pad, W_A), F32),
                        pltpu.VMEM((1, W_A), F32), pltpu.VMEM((pad, W_A), F32)],
        compiler_params=_cp("arbitrary"),
    )(dy, proj, proj, proj, h, h, *params)


def _attn_stack(qa, qb, kvh):
    lane = lax.broadcasted_iota(jnp.int32, qa.shape, 1)
    keep = (lane >= HD) if kvh == 1 else (lane < HD)
    parts = []
    for tile in (qa, qb):
        for half in (0, 1):
            y = tile if half == kvh else pltpu.roll(tile, HD, axis=1)
            parts.append(jnp.where(keep, y, 0.0))
    return jnp.concatenate(parts, axis=0)


def _attn_unstack(o, kvh):
    lane = lax.broadcasted_iota(jnp.int32, (BLK, 2 * HD), 1)
    tiles = []
    for t in range(2):
        halves = []
        for half in (0, 1):
            blk = o[(2 * t + half) * BLK:(2 * t + half + 1) * BLK, :]
            halves.append(blk if half == kvh else pltpu.roll(blk, HD, axis=1))
        tiles.append(jnp.where(lane < HD, halves[0], halves[1]))
    return tiles


def _attn_stack_all(x_ref_or_val):
    return jnp.concatenate([_attn_stack(x_ref_or_val[:, 256 * kvh:256 * kvh + 128],
                                        x_ref_or_val[:, 256 * kvh + 128:256 * kvh + 256], kvh) for kvh in range(2)], axis=0)


def _attn_unstack_all(o, dst_ref):
    for kvh in range(2):
        ta, tb = _attn_unstack(o[4 * BLK * kvh:4 * BLK * (kvh + 1), :], kvh)
        dst_ref[:, 256 * kvh:256 * kvh + 128] = ta
        dst_ref[:, 256 * kvh + 128:256 * kvh + 256] = tb


def _attn_probs(qs, kw, n, sink_ref):
    rows = NQ * BLK
    sc = lax.dot_general(qs.astype(BF16), kw, NT, preferred_element_type=F32) * SCALE
    qi = lax.broadcasted_iota(jnp.int32, (rows, 2 * BLK), 0) & (BLK - 1)
    kj = lax.broadcasted_iota(jnp.int32, (rows, 2 * BLK), 1)
    rel = BLK + qi - kj
    mask = (rel >= 0) & (rel < BLK) & ((n - 1) * BLK + kj >= 0)
    head = lax.broadcasted_iota(jnp.int32, (rows, 1), 0) // BLK
    sk = jnp.zeros((rows, 1), F32)
    for h in range(NQ):
        sk = jnp.where(head == h, sink_ref[h:h + 1, 0:1], sk)
    sh = jnp.where(mask, sc, NEG_BIG)
    m = jnp.maximum(jnp.max(sh, axis=-1, keepdims=True), sk)
    e = jnp.exp(sh - m)
    es = jnp.exp(sk - m)
    rz = 1.0 / (jnp.sum(e, axis=-1, keepdims=True) + es)
    return e * rz, es * rz


def _attn_fwd(proj, sinks8, gg):
    s = proj.shape[0]

    def body(q_ref, kc_ref, kp_ref, vc_ref, vp_ref, sink_ref, gg_ref, yn_ref, ob_ref):
        n = pl.program_id(0)
        kw = jnp.concatenate([kp_ref[...], kc_ref[...]], axis=0).astype(BF16)
        vw = jnp.concatenate([vp_ref[...], vc_ref[...]], axis=0).astype(BF16)
        p, _ = _attn_probs(_attn_stack_all(q_ref), kw, n, sink_ref)
        _attn_unstack_all(jnp.dot(p.astype(BF16), vw, preferred_element_type=F32), ob_ref)
        ob = ob_ref[...]
        yn_ref[...] = (ob * _rsq(ob, NORM_EPS) * gg_ref[...]).astype(BF16)

    kv = lambda c, back: pl.BlockSpec((BLK, 128), lambda n, c=c, back=back: (jnp.maximum(n - back, 0), c))
    out = pl.BlockSpec((BLK, W_B), lambda n: (n, 0))
    return pl.pallas_call(
        body, name="attn_fwd", grid=(s // BLK,),
        in_specs=[pl.BlockSpec((BLK, W_B), lambda n: (n, 1)), kv(8, 0), kv(8, 1), kv(9, 0), kv(9, 1),
                  pl.BlockSpec((8, 128), lambda n: (0, 0)), pl.BlockSpec((1, W_B), lambda n: (0, 0))],
        out_specs=[out, out], out_shape=[SDS((s, W_B), BF16), SDS((s, W_B), F32)],
        compiler_params=_cp("parallel"),
    )(proj, proj, proj, proj, proj, sinks8, gg)


def _attn_bwd(dy, proj, ob, sinks8, gg):
    s = proj.shape[0]

    def body(dya_ref, dyb_ref, q_ref, kc_ref, kp_ref, vc_ref, vp_ref, ob_ref, sink_ref, gg_ref,
             dq_ref, dcur_ref, dprev_ref, dsink_ref, dgg_ref):
        n = pl.program_id(0)
        first = n == 0
        kw = jnp.concatenate([kp_ref[...], kc_ref[...]], axis=0).astype(BF16)
        vw = jnp.concatenate([vp_ref[...], vc_ref[...]], axis=0).astype(BF16)
        dyn = jnp.concatenate([dya_ref[...], dyb_ref[...]], axis=1)
        dob, dggr = _rms_bwd_rows(ob_ref[...], gg_ref[...], dyn)
        _acc(dgg_ref, first, jnp.sum(dggr, axis=0, keepdims=True))
        qs = _attn_stack_all(q_ref)
        p, psink = _attn_probs(qs, kw, n, sink_ref)
        dosb = _attn_stack_all(dob).astype(BF16)
        dp = lax.dot_general(dosb, vw, NT, preferred_element_type=F32)
        dd = jnp.sum(p * dp, axis=-1, keepdims=True)
        dsb = (p * (dp - dd) * SCALE).astype(BF16)
        dsink_rows = -psink * dd
        r8 = _row_iota((8, 128))
        dsk = jnp.zeros((8, 128), F32)
        for h in range(NQ):
            dsk = dsk + jnp.where(r8 == h, jnp.sum(dsink_rows[h * BLK:(h + 1) * BLK, :], axis=0, keepdims=True), 0.0)
        _acc(dsink_ref, first, dsk)
        _attn_unstack_all(jnp.dot(dsb, kw, preferred_element_type=F32), dq_ref)
        dkw = lax.dot_general(dsb, qs.astype(BF16), TN, preferred_element_type=F32)
        dvw = lax.dot_general(p.astype(BF16), dosb, TN, preferred_element_type=F32)
        dprev_ref[:, 0:128] = dkw[0:BLK, :]
        dprev_ref[:, 128:256] = dvw[0:BLK, :]
        dcur_ref[:, 0:128] = dkw[BLK:2 * BLK, :]
        dcur_ref[:, 128:256] = dvw[BLK:2 * BLK, :]

    kv = lambda c, back: pl.BlockSpec((BLK, 128), lambda n, c=c, back=back: (jnp.maximum(n - back, 0), c))
    wide = pl.BlockSpec((BLK, W_B), lambda n: (n, 0))
    half = pl.BlockSpec((BLK, 256), lambda n: (n, 0))
    return pl.pallas_call(
        body, name="attn_bwd", grid=(s // BLK,),
        in_specs=[pl.BlockSpec((BLK, 256), lambda n: (n, 1)), pl.BlockSpec((BLK, 256), lambda n: (n, 2)),
                  pl.BlockSpec((BLK, W_B), lambda n: (n, 1)), kv(8, 0), kv(8, 1), kv(9, 0), kv(9, 1), wide,
                  pl.BlockSpec((8, 128), lambda n: (0, 0)), pl.BlockSpec((1, W_B), lambda n: (0, 0))],
        out_specs=[wide, half, half, pl.BlockSpec((8, 128), lambda n: (0, 0)), pl.BlockSpec((1, W_B), lambda n: (0, 0))],
        out_shape=[SDS((s, W_B), F32), SDS((s, 256), F32), SDS((s, 256), F32), SDS((8, 128), F32), SDS((1, W_B), F32)],
        compiler_params=_cp("arbitrary"),
    )(dy, dy, proj, proj, proj, proj, proj, ob, sinks8, gg)


def _ln_parts(y1, eps=LN_EPS):
    mu = jnp.mean(y1, axis=-1, keepdims=True)
    xc = y1 - mu
    rstd = lax.rsqrt(jnp.mean(xc * xc, axis=-1, keepdims=True) + eps)
    return xc * rstd, rstd


def _conf_fwd(proj, cw, cb, lg, lb, gg, tc):
    s = proj.shape[0]
    pad = 32

    def body(ac_ref, gc_ref, ap_ref, gp_ref, cw_ref, cb_ref, lg_ref, lb_ref, gg_ref, yn_ref, y1_ref, ys_ref):
        i = pl.program_id(0)
        tail = ap_ref[tc - pad:tc, :] * _sig(gp_ref[tc - pad:tc, :])
        ys_ref[0:pad, :] = jnp.where(i > 0, tail, 0.0)
        ys_ref[pad:pad + tc, :] = ac_ref[...] * _sig(gc_ref[...])
        y1 = _conv_taps(ys_ref, cw_ref, CONV_K, pad, tc) + cb_ref[...]
        y1_ref[...] = y1
        xh, _ = _ln_parts(y1)
        yl = xh * lg_ref[...] + lb_ref[...]
        yc = yl * _sig(yl)
        yn_ref[...] = (yc * _rsq(yc, NORM_EPS) * gg_ref[...]).astype(BF16)

    cur = lambda c: pl.BlockSpec((tc, W_C), lambda i, c=c: (i, c))
    prev = lambda c: pl.BlockSpec((tc, W_C), lambda i, c=c: (jnp.maximum(i - 1, 0), c))
    full = lambda a: pl.BlockSpec(a.shape, lambda i: (0,) * a.ndim)
    params = [cw, cb, lg, lb, gg]
    out = pl.BlockSpec((tc, W_C), lambda i: (i, 0))
    return pl.pallas_call(
        body, name="conf_fwd", grid=(s // tc,),
        in_specs=[cur(5), cur(6), prev(5), prev(6)] + [full(a) for a in params],
        out_specs=[out, out], out_shape=[SDS((s, W_C), BF16), SDS((s, W_C), F32)],
        scratch_shapes=[pltpu.VMEM((tc + pad, W_C), F32)],
        compiler_params=_cp("parallel"),
    )(proj, proj, proj, proj, *params)


def _conf_bwd(dy, proj, y1, cw, cb, lg, lb, gg, tc):
    s = proj.shape[0]
    nc = s // tc
    pad = 32

    def body(dy_ref, ac_ref, gc_ref, ap_ref, gp_ref, y1_ref, cw_ref, cb_ref, lg_ref, lb_ref, gg_ref,
             dp_ref, dcw_ref, dcb_ref, dlg_ref, dlb_ref, dgg_ref, ys_ref, ds_ref, nx_ref):
        step = pl.program_id(0)
        i = nc - 1 - step
        first = step == 0

        @pl.when(first)
        def _():
            nx_ref[...] = jnp.zeros_like(nx_ref)

        a = ac_ref[...]
        sg = _sig(gc_ref[...])
        tail = ap_ref[tc - pad:tc, :] * _sig(gp_ref[tc - pad:tc, :])
        ys_ref[0:pad, :] = jnp.where(i > 0, tail, 0.0)
        ys_ref[pad:pad + tc, :] = a * sg
        xh, rstd = _ln_parts(y1_ref[...])
        yl = xh * lg_ref[...] + lb_ref[...]
        sl = _sig(yl)
        yc = yl * sl
        dyc, dggr = _rms_bwd_rows(yc, gg_ref[...], dy_ref[...])
        _acc(dgg_ref, first, jnp.sum(dggr, axis=0, keepdims=True))
        dyl = dyc * sl * (1.0 + yl * (1.0 - sl))
        _acc(dlg_ref, first, jnp.sum(dyl * xh, axis=0, keepdims=True))
        _acc(dlb_ref, first, jnp.sum(dyl, axis=0, keepdims=True))
        dxh = dyl * lg_ref[...]
        dy1 = rstd * (dxh - jnp.mean(dxh, axis=-1, keepdims=True) - xh * jnp.mean(dxh * xh, axis=-1, keepdims=True))
        _acc(dcb_ref, first, jnp.sum(dy1, axis=0, keepdims=True))
        r32 = _row_iota((32, W_C))
        dcw = jnp.zeros((32, W_C), F32)
        for j in range(CONV_K):
            tap = jnp.sum(dy1 * ys_ref[pl.ds(pad - (CONV_K - 1) + j, tc), :], axis=0, keepdims=True)
            dcw = dcw + jnp.where(r32 == j, tap, 0.0)
        _acc(dcw_ref, first, dcw)
        ds_ref[0:tc, :] = dy1
        ds_ref[tc:tc + pad, :] = nx_ref[...]
        dy0 = None
        for j in range(CONV_K):
            term = cw_ref[j:j + 1, :] * ds_ref[pl.ds(CONV_K - 1 - j, tc), :]
            dy0 = term if dy0 is None else dy0 + term
        dp_ref[:, 0:W_C] = dy0 * sg
        dp_ref[:, W_C:2 * W_C] = dy0 * a * sg * (1.0 - sg)
        nx_ref[...] = dy1[0:pad, :]

    rev = lambda c: pl.BlockSpec((tc, W_C), lambda t, c=c: (nc - 1 - t, c))
    prev = lambda c: pl.BlockSpec((tc, W_C), lambda t, c=c: (jnp.maximum(nc - 2 - t, 0), c))
    full = lambda a: pl.BlockSpec(a.shape, lambda t: (0,) * a.ndim)
    params = [cw, cb, lg, lb, gg]
    vec = SDS((1, W_C), F32)
    outs = [SDS((s, 2 * W_C), F32), SDS((32, W_C), F32), vec, vec, vec, vec]
    return pl.pallas_call(
        body, name="conf_bwd", grid=(nc,),
        in_specs=[rev(3), rev(5), rev(6), prev(5), prev(6), rev(0)] + [full(a) for a in params],
        out_specs=[pl.BlockSpec((tc, 2 * W_C), lambda t: (nc - 1 - t, 0))]
        + [pl.BlockSpec(o.shape, lambda t: (0, 0)) for o in outs[1:]],
        out_shape=outs,
        scratch_shapes=[pltpu.VMEM((tc + pad, W_C), F32), pltpu.VMEM((tc + pad, W_C), F32), pltpu.VMEM((pad, W_C), F32)],
        compiler_params=_cp("arbitrary"),
    )(dy, proj, proj, proj, proj, y1, *params)


def _assemble_dproj(dlru, dq, dcur, dprev, dconf):
    s = dq.shape[0]
    nb = s // BLK

    def body(dl_ref, dq_ref, dc_ref, dn_ref, df_ref, o_ref):
        n = pl.program_id(0)
        o_ref[:, 0:512] = dl_ref[...].astype(BF16)
        o_ref[:, 512:1024] = dq_ref[...].astype(BF16)
        o_ref[:, 1024:1280] = (dc_ref[...] + jnp.where(n < nb - 1, dn_ref[...], 0.0)).astype(BF16)
        o_ref[:, 1280:1792] = df_ref[...].astype(BF16)

    wide = pl.BlockSpec((BLK, 512), lambda n: (n, 0))
    return pl.pallas_call(
        body, name="assemble_dproj", grid=(nb,),
        in_specs=[wide, wide, pl.BlockSpec((BLK, 256), lambda n: (n, 0)),
                  pl.BlockSpec((BLK, 256), lambda n: (jnp.minimum(n + 1, nb - 1), 0)), wide],
        out_specs=pl.BlockSpec((BLK, P_IN), lambda n: (n, 0)), out_shape=SDS((s, P_IN), BF16),
        compiler_params=_cp("parallel"),
    )(dlru, dq, dcur, dprev, dconf)


def _loss_grad(y, t, tm):
    s = y.shape[0]

    def body(y_ref, t_ref, dy_ref, l_ref):
        err = y_ref[...] - t_ref[...]
        dy_ref[...] = err * (1.0 / D)
        _acc(l_ref, pl.program_id(0) == 0, jnp.sum(err * err, axis=0, keepdims=True))

    row = pl.BlockSpec((tm, D), lambda i: (i, 0))
    return pl.pallas_call(
        body, name="loss_grad", grid=(s // tm,), in_specs=[row, row],
        out_specs=[row, pl.BlockSpec((1, D), lambda i: (0, 0))],
        out_shape=[SDS((s, D), F32), SDS((1, D), F32)], compiler_params=_cp("arbitrary"),
    )(y, t)


def _block_diag(w):
    rows = [jnp.concatenate([w[h] if k == h else jnp.zeros((64, 64), w.dtype) for k in range(4)], axis=1) for h in range(4)]
    return jnp.concatenate(rows, axis=0)


def _diag_blocks(m):
    return jnp.stack([m[64 * h:64 * (h + 1), 64 * h:64 * (h + 1)] for h in range(4)])


def _layer_params(small, l):
    v = lambda name: small[name][l].reshape(1, -1)
    gg = small["group_g"][l]
    return dict(
        ffn1_pre=v("ffn1_pre_g"), ffn1_post=v("ffn1_post_g"), mix_pre=v("mix_pre_g"), mix_post=v("mix_post_g"),
        ffn2_pre=v("ffn2_pre_g"), ffn2_post=v("ffn2_post_g"), lru_cb=v("lru_conv_b"),
        wa=_block_diag(small["lru_w_a"][l]).astype(BF16), ba=v("lru_b_a"),
        wx=_block_diag(small["lru_w_x"][l]).astype(BF16), bx=v("lru_b_x"), lam=v("lru_lambda"),
        sinks8=jnp.broadcast_to(small["attn_sinks"][l][:, None], (NQ, 128)),
        conv_b=v("conv_b"), ln_g=v("conv_ln_g"), ln_b=v("conv_ln_b"),
        gg_a=gg[0:W_A].reshape(1, -1), gg_b=gg[W_A:W_A + W_B].reshape(1, -1), gg_c=gg[W_A + W_B:].reshape(1, -1),
    )


def _forward_layer(x, weights, p, tiles, deps=()):
    _, mm, _, tc = tiles
    big = dict(weights("ffn1", x))
    p = dict(p)
    sv = dict(x0=x)
    h1, g1, u1, a1 = _ffn_up(x, p["ffn1_pre"], big["ffn1_w_gu"], 0, mm, deps)
    z1, x = _mm_rms_res(a1, big["ffn1_w_down"], 0, x, p["ffn1_post"], 0.5, mm, FH, "ffn_down")
    sv.update(h1=h1, g1=g1, u1=u1, a1=a1, z1=z1, x1=x)
    big.update(weights("mix", x))
    p.update(lru_cw=big.pop("lru_conv_w"), conv_w=big.pop("conv_w"))
    hn, proj = _proj(x, p["mix_pre"], big["w_in"], 0, mm)
    yn_a, hl = _lru_fwd(proj, p["lru_cw"], p["lru_cb"], p["wa"], p["ba"], p["wx"], p["bx"], p["lam"], p["gg_a"], tc)
    yn_b, ob = _attn_fwd(proj, p["sinks8"], p["gg_b"])
    yn_c, y1 = _conf_fwd(proj, p["conv_w"], p["conv_b"], p["ln_g"], p["ln_b"], p["gg_c"], tc)
    ycat = jnp.concatenate([yn_a, yn_b, yn_c], axis=1)
    zo, x = _mm_rms_res(ycat, big["w_out"], 0, x, p["mix_post"], 1.0, mm, D, "mix_out")
    sv.update(hn=hn, proj=proj, hl=hl, ob=ob, y1=y1, ycat=ycat, zo=zo, x2=x)
    big.update(weights("ffn2", x))
    h2, g2, u2, a2 = _ffn_up(x, p["ffn2_pre"], big["ffn2_w_gu"], 0, mm)
    z2, x = _mm_rms_res(a2, big["ffn2_w_down"], 0, x, p["ffn2_post"], 0.5, mm, FH, "ffn_down")
    sv.update(h2=h2, g2=g2, u2=u2, a2=a2, z2=z2, p=p, big=big)
    return x, sv


def _grad_buffers():
    empty = lambda *shape: lax.empty(shape, F32)
    return dict(ffn1_w_gu=empty(1, NSHARD, D, FH), ffn2_w_gu=empty(1, NSHARD, D, FH), ffn1_w_down=empty(1, 1, DFF, D),
                ffn2_w_down=empty(1, 1, DFF, D), w_in=empty(1, 1, D, P_IN), w_out=empty(1, 1, D, D))


def _backward_layer(dx, sv, bufs, tiles, stage):
    p, big = sv["p"], sv["big"]
    tm, mm, dw, tc = tiles
    gr = {}

    def ffn_bwd(dx, which, xin, h, g, u, a, z, pre, post, deps):
        dz, dpost = _rms_bwd(dx, z, post, 0.5, tm, "ffn_post_bwd", deps)
        dg, du = _ffn_bwd_mid(dz, big[which + "_w_down"], 0, g, u, mm)
        bufs[which + "_w_down"] = _mm_tn_into(bufs[which + "_w_down"], a, dz, 0, 0, FH, D, dw, "dw_down")
        bufs[which + "_w_gu"] = _mm_tn_into(bufs[which + "_w_gu"], h, dg, 0, 0, D, FH, dw, "dw_gate")
        bufs[which + "_w_gu"] = _mm_tn_into(bufs[which + "_w_gu"], h, du, 0, 2, D, FH, dw, "dw_up")
        dxn, dpre = _ffn_bwd_dh(dg, du, big[which + "_w_gu"], 0, xin, pre, dx, mm)
        return dxn, dpre, dpost

    dx, gr["ffn2_pre_g"], gr["ffn2_post_g"] = ffn_bwd(dx, "ffn2", sv["x2"], sv["h2"], sv["g2"], sv["u2"], sv["a2"],
                                                      sv["z2"], p["ffn2_pre"], p["ffn2_post"], stage({}, dx))
    done = {n: bufs[n] for n in ("ffn2_w_gu", "ffn2_w_down")}
    do, gr["mix_post_g"] = _rms_bwd(dx, sv["zo"], p["mix_post"], 1.0, tm, "mix_post_bwd", stage(done, dx))
    bufs["w_out"] = _mm_tn_into(bufs["w_out"], sv["ycat"], do, 0, 0, D, D, dw, "dw_out")
    dy = _mm_nt(do, big["w_out"], 0, mm, "mix_dy")
    proj = sv["proj"]
    (dlru, dcw, gr["lru_conv_b"], dwa, gr["lru_b_a"], dwx, gr["lru_b_x"], gr["lru_lambda"], dgg_a) = _lru_bwd(
        dy, proj, sv["hl"], p["lru_cw"], p["lru_cb"], p["wa"], p["ba"], p["wx"], p["bx"], p["lam"], p["gg_a"], tc)
    dq, dcur, dprev, dsk, dgg_b = _attn_bwd(dy, proj, sv["ob"], p["sinks8"], p["gg_b"])
    dconf, dconvw, gr["conv_b"], gr["conv_ln_g"], gr["conv_ln_b"], dgg_c = _conf_bwd(
        dy, proj, sv["y1"], p["conv_w"], p["conv_b"], p["ln_g"], p["ln_b"], p["gg_c"], tc)
    dproj = _assemble_dproj(dlru, dq, dcur, dprev, dconf)
    bufs["w_in"] = _mm_tn_into(bufs["w_in"], sv["hn"], dproj, 0, 0, D, P_IN, dw, "dw_in")
    dx, gr["mix_pre_g"] = _mm_nt_rmsbwd(dproj, big["w_in"], 0, sv["x1"], p["mix_pre"], dx, mm)
    gr["lru_conv_w"] = dcw[0:LRU_K]
    gr["lru_w_a"] = _diag_blocks(dwa)
    gr["lru_w_x"] = _diag_blocks(dwx)
    gr["attn_sinks"] = dsk[:, 0]
    gr["conv_w"] = dconvw[0:CONV_K]
    gr["group_g"] = jnp.concatenate([dgg_a, dgg_b, dgg_c], axis=1)
    dx, gr["ffn1_pre_g"], gr["ffn1_post_g"] = ffn_bwd(dx, "ffn1", sv["x0"], sv["h1"], sv["g1"], sv["u1"], sv["a1"],
                                                      sv["z1"], p["ffn1_pre"], p["ffn1_post"],
                                                      stage({n: bufs[n] for n in ("w_in", "w_out")}, dx))
    return dx, gr


def _tiles(s):
    return min(512, s), min(1024, s), min(2048, s), min(512, s // 2)


HBM_SPEC = pl.BlockSpec(memory_space=pltpu.HBM)
SEM_SPEC = pl.BlockSpec(memory_space=pltpu.SEMAPHORE)
EFFECT = pltpu.SideEffectType.DATAFLOW_SIDE_EFFECTING


def _place():
    x, y, c = lax.axis_index("x"), lax.axis_index("y"), lax.axis_index("c")
    return x, y, c, [(1 - x, y), (x, 1 - y), (1 - x, 1 - y)]


def _rcopy(src, dst, send_sems, recv_sems, k, to):
    return pltpu.make_async_remote_copy(src_ref=src, dst_ref=dst, send_sem=send_sems.at[k], recv_sem=recv_sems.at[k],
                                        device_id=to, device_id_type=MESH)


def _half(rows, which):
    return pl.ds(which * (rows // 2), rows // 2)


def _place_shard(w, l, p_idx, dtype):
    _, rows, cols = w.shape
    tr = _rows_per_block(rows, cols, 16) if rows % 16 == 0 else rows

    def body(p_ref, buf_ref, w_ref, o_ref):
        o_ref[...] = w_ref[...].astype(dtype)

    spec = pltpu.PrefetchScalarGridSpec(
        num_scalar_prefetch=1, grid=(rows // tr,),
        in_specs=[ANY, pl.BlockSpec((None, tr, cols), lambda i, pr: (l, i, 0))],
        out_specs=pl.BlockSpec((None, None, tr, cols), lambda i, pr: (0, pr[0], i, 0)))
    shape = (1, NSHARD, rows, cols)
    return pl.pallas_call(body, name="place_shard", grid_spec=spec, out_shape=SDS(shape, dtype),
                          input_output_aliases={1: 0}, compiler_params=_cp("parallel"),
                          )(p_idx, lax.empty(shape, dtype), w)


def _gather_two_level(bufs, n_halved):
    n = len(bufs)

    def body(*refs):
        outs = refs[n:2 * n]
        send_sems, recv_sems = refs[2 * n:]
        x, y, c, chips = _place()
        p = 2 * x + y
        me, sibling = (x, y, c), (x, y, 1 - c)

        def blk(a, q, half):
            return outs[a].at[0, q, _half(outs[a].shape[2], half)] if a < n_halved else outs[a].at[0, q]

        def cp(a, k, q, half, to):
            return _rcopy(blk(a, q, half), blk(a, q, half), send_sems, recv_sems, 6 * a + k, to)

        first = [cp(a, j, p, c, (*chip, c)) for a in range(n) for j, chip in enumerate(chips)]
        for d in first:
            d.start()
        passed = []
        for a in range(n):
            for j, chip in enumerate(chips):
                q = 2 * chip[0] + chip[1]
                cp(a, j, q, c, me).wait_recv()
                if a < n_halved:
                    passed.append(cp(a, 3 + j, q, c, sibling))
                    passed[-1].start()
        for a in range(n_halved):
            for j, chip in enumerate(chips):
                cp(a, 3 + j, 2 * chip[0] + chip[1], 1 - c, me).wait_recv()
        for d in first + passed:
            d.wait_send()

    return pl.pallas_call(
        body, name="gather_layer0", in_specs=[ANY] * n, out_specs=[ANY] * n,
        out_shape=[SDS(b.shape, b.dtype) for b in bufs], input_output_aliases={a: a for a in range(n)},
        scratch_shapes=[pltpu.SemaphoreType.DMA((6 * n,)), pltpu.SemaphoreType.DMA((6 * n,))],
    )(*bufs)


def _run_plans(plans, refs, send_sems, recv_sems):
    cps, b0, s0 = [], 0, 0
    for plan, nb, ns in plans:
        cps += plan(refs[b0:b0 + nb], send_sems, recv_sems, s0)
        b0, s0 = b0 + nb, s0 + ns
    return cps


def _exchange(name, bufs, plans):
    n = len(bufs)
    nsem = sum(ns for _, _, ns in plans)

    def body(*refs):
        cps = _run_plans(plans, refs[n:2 * n], refs[2 * n], refs[2 * n + 1])
        for cp in cps:
            cp.start()
        for cp in cps:
            cp.wait()

    return pl.pallas_call(
        body, name=name, in_specs=[ANY] * n, out_specs=[ANY] * n, out_shape=[SDS(b.shape, b.dtype) for b in bufs],
        input_output_aliases={a: a for a in range(n)},
        scratch_shapes=[pltpu.SemaphoreType.DMA((nsem,)), pltpu.SemaphoreType.DMA((nsem,))],
    )(*bufs)


def _exchange_start(name, bufs, plans, deps=()):
    n = len(bufs)
    nsem = sum(ns for _, _, ns in plans)
    deps = list(deps)
    first_out = n + len(deps)

    def body(*refs):
        for cp in _run_plans(plans, refs[:n], refs[first_out], refs[first_out + 1]):
            cp.start()
        token = refs[first_out + 2 + n]
        token[...] = jnp.zeros_like(token)

    outs = pl.pallas_call(
        body, name=name,
        out_shape=(pltpu.SemaphoreType.DMA((nsem,)), pltpu.SemaphoreType.DMA((nsem,)),
                   *[pltpu.HBM(b.shape, b.dtype) for b in bufs], SDS((8, 128), F32)),
        in_specs=[HBM_SPEC] * n + [ANY] * len(deps),
        out_specs=(SEM_SPEC, SEM_SPEC, *[HBM_SPEC] * n, pl.BlockSpec(memory_space=pltpu.VMEM)),
        input_output_aliases={a: 2 + a for a in range(n)},
        compiler_params=pltpu.CompilerParams(has_side_effects=EFFECT),
    )(*[pltpu.with_memory_space_constraint(b, pltpu.HBM) for b in bufs], *deps)
    return outs[0], outs[1], list(outs[2:2 + n]), outs[2 + n]


def _exchange_wait(name, send_sems, recv_sems, bufs, plans, after):
    n = len(bufs)

    def body(*refs):
        for cp in _run_plans(plans, refs[:n], refs[n], refs[n + 1]):
            cp.wait_send()
            cp.wait_recv()

    return pl.pallas_call(
        body, name=name, out_shape=[pltpu.HBM(b.shape, b.dtype) for b in bufs],
        in_specs=[HBM_SPEC] * n + [SEM_SPEC, SEM_SPEC, ANY], out_specs=[HBM_SPEC] * n,
        input_output_aliases={a: a for a in range(n)},
        compiler_params=pltpu.CompilerParams(has_side_effects=EFFECT),
    )(*bufs, send_sems, recv_sems, after)


def _plan_gather(refs, send_sems, recv_sems, base):
    x, y, c, chips = _place()
    p = 2 * x + y
    return [_rcopy(r.at[0, p], r.at[0, p], send_sems, recv_sems, base + 3 * a + j, (*chip, c))
            for a, r in enumerate(refs) for j, chip in enumerate(chips)]


def _plan_pair_exchange(refs, send_sems, recv_sems, base):
    x, y, c, _ = _place()
    n = len(refs) // 2
    return [_rcopy(refs[a].at[:, _half(refs[a].shape[1], 1 - c)], refs[n + a], send_sems, recv_sems, base + a,
                   (x, y, 1 - c)) for a in range(n)]


def _plan_chip_exchange(refs, send_sems, recv_sems, base):
    x, y, c, chips = _place()
    n = len(refs) // 2
    return [_rcopy(refs[a].at[2 * chip[0] + chip[1]], refs[n + a].at[j], send_sems, recv_sems, base + 3 * a + j,
                   (*chip, c)) for a in range(n) for j, chip in enumerate(chips)]


def _plan_pair_share(refs, send_sems, recv_sems, base):
    x, y, c, _ = _place()
    return [_rcopy(r.at[_half(r.shape[0], c)], r.at[_half(r.shape[0], c)], send_sems, recv_sems, base + a,
                   (x, y, 1 - c)) for a, r in enumerate(refs)]


def _allreduce_small(buf):
    rows = buf.shape[0]

    def body(in_ref, out_ref, gather_ref, send_sems, recv_sems):
        x, y, c, _ = _place()
        me = 4 * x + 2 * y + c
        gather_ref[me] = in_ref[...]
        cps, slots = [], []
        for m in range(1, NDEV):
            px = 1 - x if m & 4 else x
            py = 1 - y if m & 2 else y
            pc = 1 - c if m & 1 else c
            cps.append(_rcopy(in_ref, gather_ref.at[me], send_sems, recv_sems, m - 1, (px, py, pc)))
            slots.append(4 * px + 2 * py + pc)
        for cp in cps:
            cp.start()
        for m in range(1, NDEV):
            _rcopy(in_ref, gather_ref.at[slots[m - 1]], send_sems, recv_sems, m - 1, (x, y, c)).wait_recv()
        for cp in cps:
            cp.wait_send()
        total = gather_ref[0]
        for dev in range(1, NDEV):
            total = total + gather_ref[dev]
        out_ref[...] = total

    vm = pl.BlockSpec(memory_space=pltpu.VMEM)
    return pl.pallas_call(
        body, name="allreduce_small", in_specs=[vm], out_specs=vm, out_shape=SDS(buf.shape, F32),
        scratch_shapes=[pltpu.VMEM((NDEV, rows, 128), F32), pltpu.SemaphoreType.DMA((NDEV - 1,)),
                        pltpu.SemaphoreType.DMA((NDEV - 1,))],
        compiler_params=pltpu.CompilerParams(vmem_limit_bytes=VMEM_LIMIT),
    )(buf)


BLOCK_ELEMS = 256 * 1024


def _rows_per_block(rows, cols, mult):
    best = None
    for tr in range(mult, rows + 1, mult):
        if rows % tr == 0 and tr * cols <= BLOCK_ELEMS:
            best = tr
    assert best is not None, (rows, cols)
    return best


def _pair_sum(g, r, c_idx):
    nq, rows, cols = g.shape
    half = rows // 2
    tr = _rows_per_block(half, cols, 16)
    nb = half // tr

    def body(c_ref, g_ref, r_ref, t_ref):
        t_ref[...] = (g_ref[...] + r_ref[...]).astype(BF16)

    blk = pl.BlockSpec((None, tr, cols), lambda q, i, cr: (q, i, 0))
    spec = pltpu.PrefetchScalarGridSpec(
        num_scalar_prefetch=1, grid=(nq, nb),
        in_specs=[pl.BlockSpec((None, tr, cols), lambda q, i, cr: (q, cr[0] * nb + i, 0)), blk], out_specs=blk)
    return pl.pallas_call(body, name="grad_pair_sum", grid_spec=spec, out_shape=SDS((nq, half, cols), BF16),
                          compiler_params=_cp("parallel", "parallel"))(c_idx, g, r)


def _chip_sum(g, r, rr, cp_idx):
    _, rows, cols = g.shape
    half = rows // 2
    tr = _rows_per_block(half, cols, 16)
    nb = half // tr

    def body(cp_ref, buf_ref, g_ref, r_ref, rr_ref, o_ref):
        o_ref[...] = ((g_ref[...] + r_ref[...]) + rr_ref[0].astype(F32) + rr_ref[1].astype(F32) + rr_ref[2].astype(F32))

    spec = pltpu.PrefetchScalarGridSpec(
        num_scalar_prefetch=1, grid=(nb,),
        in_specs=[ANY, pl.BlockSpec((None, tr, cols), lambda i, cp: (cp[1], cp[0] * nb + i, 0)),
                  pl.BlockSpec((None, tr, cols), lambda i, cp: (cp[1], i, 0)),
                  pl.BlockSpec((3, tr, cols), lambda i, cp: (0, i, 0))],
        out_specs=pl.BlockSpec((tr, cols), lambda i, cp: (cp[0] * nb + i, 0)))
    return pl.pallas_call(body, name="grad_chip_sum", grid_spec=spec, out_shape=SDS((rows, cols), F32),
                          input_output_aliases={1: 0}, compiler_params=_cp("parallel"),
                          )(cp_idx, lax.empty((rows, cols), F32), g, r, rr)


def _adamw_math(w, g, m, v):
    mn = ADAM_B1 * m + (1.0 - ADAM_B1) * g
    vn = ADAM_B2 * v + (1.0 - ADAM_B2) * (g * g)
    m_hat = mn / (1.0 - ADAM_B1 ** ADAM_STEP)
    v_hat = vn / (1.0 - ADAM_B2 ** ADAM_STEP)
    return -ADAM_LR * (m_hat / (jnp.sqrt(v_hat) + ADAM_EPS) + ADAM_WD * w), mn, vn


def _adamw_layers(w, gs, m, v):
    depth, rows, cols = w.shape
    tr = _rows_per_block(rows, cols, 8)

    def body(w_ref, g0_ref, g1_ref, m_ref, v_ref, go_ref, d_ref, mo_ref, vo_ref):
        gg = jnp.where(pl.program_id(0) == 0, g0_ref[...], g1_ref[...])
        go_ref[...] = gg
        d_ref[...], mo_ref[...], vo_ref[...] = _adamw_math(w_ref[...], gg, m_ref[...], v_ref[...])

    blk = pl.BlockSpec((None, tr, cols), lambda l, i: (l, i, 0))
    return pl.pallas_call(
        body, name="adamw_layers", grid=(depth, rows // tr),
        in_specs=[blk, pl.BlockSpec((tr, cols), lambda l, i: (i * (1 - l), 0)),
                  pl.BlockSpec((tr, cols), lambda l, i: (i * l, 0)), blk, blk],
        out_specs=[blk] * 4, out_shape=[SDS(w.shape, F32)] * 4,
        compiler_params=_cp("arbitrary", "arbitrary"))(w, gs[0], gs[1], m, v)


def _adamw_packed(w, g, m, v):
    def body(w_ref, g_ref, m_ref, v_ref, d_ref, mo_ref, vo_ref):
        d_ref[...], mo_ref[...], vo_ref[...] = _adamw_math(w_ref[...], g_ref[...], m_ref[...], v_ref[...])

    vm = pl.BlockSpec(memory_space=pltpu.VMEM)
    return pl.pallas_call(body, name="adamw_packed", in_specs=[vm] * 4, out_specs=[vm] * 3,
                          out_shape=[SDS(w.shape, F32)] * 3,
                          compiler_params=pltpu.CompilerParams(vmem_limit_bytes=VMEM_LIMIT))(w, g, m, v)


_WEIGHTS = ["ffn1_pre_g", "ffn1_w_gu", "ffn1_w_down", "ffn1_post_g", "mix_pre_g", "w_in", "lru_conv_w", "lru_conv_b",
            "lru_w_a", "lru_b_a", "lru_w_x", "lru_b_x", "lru_lambda", "attn_sinks", "conv_w", "conv_b", "conv_ln_g",
            "conv_ln_b", "group_g", "w_out", "mix_post_g", "ffn2_pre_g", "ffn2_w_gu", "ffn2_w_down", "ffn2_post_g"]
_INPUTS = ["x"] + _WEIGHTS + ["loss_target"] + ["m_" + n for n in _WEIGHTS] + ["v_" + n for n in _WEIGHTS]
_BIG = ["ffn1_w_gu", "ffn1_w_down", "w_in", "w_out", "ffn2_w_gu", "ffn2_w_down"]
_SMALL_SHARDED = ["lru_conv_w", "conv_w"]
_SMALL_REPL = [n for n in _WEIGHTS if n not in _BIG and n not in _SMALL_SHARDED]

PACK_TILE = 8 * 128


def _pack(arrs):
    parts = []
    for a in arrs:
        flat = a.reshape(-1)
        parts.append(jnp.pad(flat, (0, -flat.shape[0] % PACK_TILE)).reshape(-1, 128))
    return jnp.concatenate(parts, axis=0)


def _unpack(buf, shapes):
    out, row = [], 0
    for shp in shapes:
        size = math.prod(shp)
        nrow = -(-size // PACK_TILE) * 8
        out.append(buf[row:row + nrow].reshape(-1)[:size].reshape(shp))
        row += nrow
    return out


def _unshard_cols(a):
    return a.transpose(0, 2, 1, 3).reshape(1, a.shape[2], NSHARD * a.shape[3])


_GROUPS = dict(ffn1=["ffn1_w_gu", "ffn1_w_down"], mix=["w_in", "w_out", "lru_conv_w", "conv_w"],
               ffn2=["ffn2_w_gu", "ffn2_w_down"])


def _full_weights(group, gathered):
    g = dict(zip(_GROUPS[group], gathered))
    if group == "mix":
        return dict(w_in=_unshard_cols(g["w_in"]), w_out=g["w_out"].reshape(1, D, D),
                    lru_conv_w=_unshard_cols(g["lru_conv_w"])[0], conv_w=_unshard_cols(g["conv_w"])[0])
    return {group + "_w_gu": g[group + "_w_gu"], group + "_w_down": g[group + "_w_down"].reshape(1, DFF, D)}


def _by_shard(name, buf):
    if name.endswith("w_gu"):
        return buf[0]
    if name == "w_in":
        return buf.reshape(D, NSHARD, P_IN // NSHARD).transpose(1, 0, 2)
    return buf.reshape(NSHARD, buf.shape[2] // NSHARD, buf.shape[3])


class _Reducer:
    PLANS = (_plan_pair_exchange, _plan_chip_exchange, _plan_pair_share)

    def __init__(self, keys, gs, c_idx, cp_idx):
        self.keys, self.gs, self.c_idx, self.cp_idx = keys, gs, c_idx, cp_idx
        self.n = len(gs)
        self.step = 0
        self.result = None

    def inputs(self):
        n = self.n
        if self.step == 0:
            bufs = self.gs + [lax.empty((NSHARD, g.shape[1] // 2, g.shape[2]), F32) for g in self.gs]
        elif self.step == 1:
            ts = [_pair_sum(g, r, self.c_idx) for g, r in zip(self.gs, self.rs)]
            bufs = ts + [lax.empty((3,) + t.shape[1:], BF16) for t in ts]
        else:
            bufs = [_chip_sum(g, r, rr, self.cp_idx) for g, r, rr in zip(self.gs, self.rs, self.rrs)]
        return bufs, (self.PLANS[self.step], len(bufs), (n, 3 * n, n)[self.step])

    def absorb(self, done):
        n = self.n
        if self.step == 0:
            self.gs, self.rs = done[:n], done[n:]
        elif self.step == 1:
            self.rrs = done[n:]
        else:
            self.result = dict(zip(self.keys, done))
        self.step += 1


class _ReducePipeline:
    def __init__(self, c_idx, cp_idx):
        self.c_idx, self.cp_idx = c_idx, cp_idx
        self.reducers, self.flying, self.calls = [], None, 0

    def add(self, layer, done):
        if done:
            keys = [(layer, n) for n in done]
            self.reducers.append(_Reducer(keys, [_by_shard(n, b) for n, b in done.items()], self.c_idx, self.cp_idx))

    def _next(self):
        active = [r for r in self.reducers if r.step < 3]
        bufs, plans = [], []
        for r in active:
            b, triple = r.inputs()
            bufs += b
            plans.append(triple)
        self.calls += 1
        return active, bufs, plans, "grad_exchange%d" % self.calls

    def _absorb(self, active, plans, done):
        at = 0
        for r, (_, nb, _) in zip(active, plans):
            r.absorb(done[at:at + nb])
            at += nb

    def _land(self, after):
        if self.flying is not None:
            active, plans, name, send_sems, recv_sems, bufs = self.flying
            self._absorb(active, plans, _exchange_wait(name + "_wait", send_sems, recv_sems, bufs, plans, after))
            self.flying = None

    def hook(self, after):
        self._land(after)
        active, bufs, plans, name = self._next()
        if not active:
            return []
        send_sems, recv_sems, bufs, token = _exchange_start(name + "_start", bufs, plans)
        self.flying = (active, plans, name, send_sems, recv_sems, bufs)
        return [token]

    def finish(self, after):
        self._land(after)
        while True:
            active, bufs, plans, name = self._next()
            if not active:
                break
            self._absorb(active, plans, _exchange(name, bufs, plans))
        out = {}
        for r in self.reducers:
            out.update(r.result)
        return out


def kernel(*args):
    d = dict(zip(_INPUTS, args, strict=True))
    xi, yi, ci = lax.axis_index("x"), lax.axis_index("y"), lax.axis_index("c")
    p = 2 * xi + yi
    c_idx = jnp.reshape(ci, (1,)).astype(jnp.int32)
    p_idx = jnp.reshape(p, (1,)).astype(jnp.int32)
    cp_idx = jnp.stack([ci, p]).astype(jnp.int32)
    x, target = d["x"][0], d["loss_target"][0]
    tiles = _tiles(x.shape[0])

    groups = [(l, grp) for l in range(DEPTH) for grp in _GROUPS]
    placed = {(l, grp): [_place_shard(d[n], l, p_idx, BF16 if n in _BIG else F32) for n in _GROUPS[grp]]
              for l, grp in groups}
    ready = {groups[0]: _gather_two_level(placed[groups[0]], len(placed[groups[0]]))}
    flying, tokens = {}, [ready[groups[0]][0]]
    for l, grp in groups[1:]:
        plans = [(_plan_gather, len(placed[l, grp]), 3 * len(placed[l, grp]))]
        send_sems, recv_sems, bufs, token = _exchange_start("gather_l%d_%s_start" % (l, grp), placed[l, grp], plans,
                                                             tokens[-1:])
        flying[l, grp] = (send_sems, recv_sems, bufs, plans)
        tokens.append(token)

    def weights_of(l):
        def weights(grp, after):
            if (l, grp) not in ready:
                send_sems, recv_sems, bufs, plans = flying[l, grp]
                ready[l, grp] = _exchange_wait("gather_l%d_%s_wait" % (l, grp), send_sems, recv_sems, bufs, plans, after)
            return _full_weights(grp, ready[l, grp])
        return weights

    small = {n: d[n] for n in _SMALL_REPL}
    x1, sv0 = _forward_layer(x, weights_of(0), _layer_params(small, 0), tiles, tokens[1:])
    x2, sv1 = _forward_layer(x1, weights_of(1), _layer_params(small, 1), tiles)
    dx, lcols = _loss_grad(x2, target, tiles[0])

    pipe = _ReducePipeline(c_idx, cp_idx)
    sgrads = [None] * DEPTH
    for l, sv in ((1, sv1), (0, sv0)):
        bufs = _grad_buffers()

        def stage(done, dx, l=l):
            pipe.add(l, done)
            return pipe.hook(dx)

        dx, sgrads[l] = _backward_layer(dx, sv, bufs, tiles, stage)
        pipe.add(l, {n: bufs[n] for n in ("ffn1_w_gu", "ffn1_w_down")})
    grad_x = dx
    reduced = pipe.finish(grad_x)

    stacked = {n: jnp.stack([sgrads[l][n].reshape(d[n].shape[1:]) for l in range(DEPTH)]) for n in _SMALL_REPL}
    for n in _SMALL_SHARDED:
        stacked[n] = jnp.stack([sgrads[l][n] for l in range(DEPTH)])
    loss_part = jnp.pad((0.5 / D) * jnp.sum(lcols).reshape(1), (0, 127))
    order = _SMALL_REPL + _SMALL_SHARDED
    summed = _unpack(_allreduce_small(_pack([loss_part] + [stacked[n] for n in order])),
                     [(128,)] + [stacked[n].shape for n in order])
    loss = summed[0][0]
    grads = {}
    for n, g in zip(order, summed[1:]):
        if n in _SMALL_SHARDED:
            g = lax.dynamic_slice_in_dim(g, p * (g.shape[2] // NSHARD), g.shape[2] // NSHARD, axis=2)
        grads[n] = g

    delta, new_m, new_v = {}, {}, {}
    for n in _BIG:
        grads[n], delta[n], new_m[n], new_v[n] = _adamw_layers(d[n], [reduced[l, n] for l in range(DEPTH)],
                                                                d["m_" + n], d["v_" + n])
    shapes = [d[n].shape for n in order]
    packed = [_pack([src(n) for n in order]) for src in
              (lambda n: d[n], lambda n: grads[n], lambda n: d["m_" + n], lambda n: d["v_" + n])]
    for out, res in zip((delta, new_m, new_v), _adamw_packed(*packed)):
        out.update(zip(order, _unpack(res, shapes)))

    return (loss, grad_x[None], *[grads[n] for n in _WEIGHTS], *[delta[n] for n in _WEIGHTS],
            *[new_m[n] for n in _WEIGHTS], *[new_v[n] for n in _WEIGHTS])
```

```python
import functools
import math

import jax
import jax.numpy as jnp
from jax import lax
from jax.experimental import pallas as pl
from jax.experimental.pallas import tpu as pltpu

F32 = jnp.float32
BF16 = jnp.bfloat16
SDS = jax.ShapeDtypeStruct

D = 1024
DFF = 2816
FH = DFF // 2
DEPTH = 2
W_A = 256
W_B = 512
W_C = 256
NQ = 8
HD = 64
BLK = 128
ATT_NB_FWD = 1
ATT_NB_BWD = 2
P_IN = 1792
LRU_K = 4
CONV_K = 31
LRU_C = 8.0
NORM_EPS = 1e-6
LN_EPS = 1e-5
NEG_BIG = -1e30
SCALE = 1.0 / math.sqrt(HD)

ADAM_LR = 0.001
ADAM_B1 = 0.9
ADAM_B2 = 0.999
ADAM_EPS = 1e-08
ADAM_WD = 0.01
ADAM_STEP = 10

VMEM_LIMIT = 56 * 1024 * 1024
NSHARD = 4
NDEV = 8

TN = (((0,), (0,)), ((), ()))
NT = (((1,), (1,)), ((), ()))

MESH = pl.DeviceIdType.MESH
ANY = pl.BlockSpec(memory_space=pl.ANY)


def _cp(*sem):
    return pltpu.CompilerParams(dimension_semantics=sem if sem else None, vmem_limit_bytes=VMEM_LIMIT)


def _rsq(x, eps):
    return lax.rsqrt(jnp.mean(x * x, axis=-1, keepdims=True) + eps)


def _rms_bwd_rows(x, g, dy):
    r = _rsq(x, NORM_EPS)
    xh = x * r
    dyg = dy * g
    dx = r * (dyg - xh * jnp.mean(dyg * xh, axis=-1, keepdims=True))
    return dx, dy * xh


def _sig(x):
    return jax.nn.sigmoid(x)


def _ffn_up(x, pre_g, wgu, l, tm, deps=()):
    s = x.shape[0]
    deps = list(deps)

    def body(x_ref, g_ref, wg_ref, wu_ref, *rest):
        h_ref, go_ref, uo_ref, a_ref = rest[len(deps):]

        @pl.when(pl.program_id(1) == 0)
        def _():
            xf = x_ref[...]
            h_ref[...] = (xf * _rsq(xf, NORM_EPS) * g_ref[...]).astype(BF16)

        h = h_ref[...]
        gg = jnp.dot(h, wg_ref[...], preferred_element_type=F32)
        uu = jnp.dot(h, wu_ref[...], preferred_element_type=F32)
        go_ref[...] = gg.astype(BF16)
        uo_ref[...] = uu.astype(BF16)
        a_ref[...] = (gg * _sig(gg) * uu).astype(BF16)

    wide = pl.BlockSpec((tm, FH), lambda i, j: (i, j))
    return pl.pallas_call(
        body, name="ffn_up", grid=(s // tm, 2),
        in_specs=[pl.BlockSpec((tm, D), lambda i, j: (i, 0)), pl.BlockSpec((1, D), lambda i, j: (0, 0)),
                  pl.BlockSpec((None, None, D, FH), lambda i, j: (l, j, 0, 0)),
                  pl.BlockSpec((None, None, D, FH), lambda i, j: (l, j + 2, 0, 0))] + [ANY] * len(deps),
        out_specs=[pl.BlockSpec((tm, D), lambda i, j: (i, 0)), wide, wide, wide],
        out_shape=[SDS((s, D), BF16), SDS((s, DFF), BF16), SDS((s, DFF), BF16), SDS((s, DFF), BF16)],
        compiler_params=_cp("parallel", "arbitrary"),
    )(x, pre_g, wgu, wgu, *deps)


def _mm_rms_res(a, w, l, x, g, c, tm, tk, name):
    s, k_dim = a.shape
    nk = k_dim // tk

    def body(a_ref, w_ref, x_ref, g_ref, z_ref, x1_ref):
        k = pl.program_id(1)
        p = jnp.dot(a_ref[...], w_ref[...], preferred_element_type=F32)

        @pl.when(k == 0)
        def _():
            z_ref[...] = p

        @pl.when(k > 0)
        def _():
            z_ref[...] += p

        @pl.when(k == nk - 1)
        def _():
            z = z_ref[...]
            x1_ref[...] = x_ref[...] + c * (z * _rsq(z, NORM_EPS) * g_ref[...])

    row = pl.BlockSpec((tm, D), lambda i, k: (i, 0))
    return pl.pallas_call(
        body, name=name, grid=(s // tm, nk),
        in_specs=[pl.BlockSpec((tm, tk), lambda i, k: (i, k)), pl.BlockSpec((None, tk, D), lambda i, k: (l, k, 0)),
                  row, pl.BlockSpec((1, D), lambda i, k: (0, 0))],
        out_specs=[row, row],
        out_shape=[SDS((s, D), F32), SDS((s, D), F32)],
        compiler_params=_cp("parallel", "arbitrary"),
    )(a, w, x, g)


def _rms_bwd(dy, z, g, c, tm, name, deps=()):
    s = z.shape[0]
    deps = list(deps)

    def body(dy_ref, z_ref, g_ref, *rest):
        dz_ref, dg_ref = rest[len(deps):]
        dz, dgr = _rms_bwd_rows(z_ref[...], g_ref[...], c * dy_ref[...])
        dz_ref[...] = dz.astype(BF16)
        part = jnp.sum(dgr, axis=0, keepdims=True)

        @pl.when(pl.program_id(0) == 0)
        def _():
            dg_ref[...] = part

        @pl.when(pl.program_id(0) > 0)
        def _():
            dg_ref[...] += part

    row = pl.BlockSpec((tm, D), lambda i: (i, 0))
    vec = pl.BlockSpec((1, D), lambda i: (0, 0))
    return pl.pallas_call(
        body, name=name, grid=(s // tm,), in_specs=[row, row, vec] + [ANY] * len(deps), out_specs=[row, vec],
        out_shape=[SDS((s, D), BF16), SDS((1, D), F32)], compiler_params=_cp("arbitrary"),
    )(dy, z, g, *deps)


def _ffn_bwd_mid(dz, wd, l, g, u, tm):
    s = dz.shape[0]

    def body(dz_ref, wd_ref, g_ref, u_ref, dg_ref, du_ref):
        da = lax.dot_general(dz_ref[...], wd_ref[...], NT, preferred_element_type=F32)
        gg = g_ref[...].astype(F32)
        uu = u_ref[...].astype(F32)
        sg = _sig(gg)
        dg_ref[...] = (da * uu * sg * (1.0 + gg * (1.0 - sg))).astype(BF16)
        du_ref[...] = (da * gg * sg).astype(BF16)

    wide = pl.BlockSpec((tm, FH), lambda i, j: (i, j))
    return pl.pallas_call(
        body, name="ffn_bwd_mid", grid=(s // tm, 2),
        in_specs=[pl.BlockSpec((tm, D), lambda i, j: (i, 0)), pl.BlockSpec((None, FH, D), lambda i, j: (l, j, 0)), wide, wide],
        out_specs=[wide, wide],
        out_shape=[SDS((s, DFF), BF16), SDS((s, DFF), BF16)],
        compiler_params=_cp("parallel", "arbitrary"),
    )(dz, wd, g, u)


def _ffn_bwd_dh(dg, du, wgu, l, x, pre_g, dx1, tm):
    s = x.shape[0]

    def body(dg_ref, du_ref, wg_ref, wu_ref, x_ref, g_ref, dx1_ref, dx_ref, dgp_ref):
        i, k = pl.program_id(0), pl.program_id(1)
        p = (lax.dot_general(dg_ref[...], wg_ref[...], NT, preferred_element_type=F32)
             + lax.dot_general(du_ref[...], wu_ref[...], NT, preferred_element_type=F32))

        @pl.when(k == 0)
        def _():
            dx_ref[...] = p

        @pl.when(k == 1)
        def _():
            dx, dgr = _rms_bwd_rows(x_ref[...], g_ref[...], dx_ref[...] + p)
            dx_ref[...] = dx1_ref[...] + dx
            part = jnp.sum(dgr, axis=0, keepdims=True)

            @pl.when(i == 0)
            def _():
                dgp_ref[...] = part

            @pl.when(i > 0)
            def _():
                dgp_ref[...] += part

    wide = pl.BlockSpec((tm, FH), lambda i, k: (i, k))
    row = pl.BlockSpec((tm, D), lambda i, k: (i, 0))
    vec = pl.BlockSpec((1, D), lambda i, k: (0, 0))
    return pl.pallas_call(
        body, name="ffn_bwd_dh", grid=(s // tm, 2),
        in_specs=[wide, wide, pl.BlockSpec((None, None, D, FH), lambda i, k: (l, k, 0, 0)),
                  pl.BlockSpec((None, None, D, FH), lambda i, k: (l, k + 2, 0, 0)), row, vec, row],
        out_specs=[row, vec],
        out_shape=[SDS((s, D), F32), SDS((1, D), F32)],
        compiler_params=_cp("arbitrary", "arbitrary"),
    )(dg, du, wgu, wgu, x, pre_g, dx1)


def _mm_tn_into(buf, a, b, l, joff, tka, tn, ts, name):
    s, ka = a.shape
    n = b.shape[1]

    def body(buf_ref, a_ref, b_ref, o_ref):
        p = lax.dot_general(a_ref[...], b_ref[...], TN, preferred_element_type=F32)

        @pl.when(pl.program_id(2) == 0)
        def _():
            o_ref[...] = p

        @pl.when(pl.program_id(2) > 0)
        def _():
            o_ref[...] += p

    return pl.pallas_call(
        body, name=name, grid=(ka // tka, n // tn, s // ts),
        in_specs=[pl.BlockSpec(memory_space=pl.ANY),
                  pl.BlockSpec((ts, tka), lambda ia, j, t: (t, ia)), pl.BlockSpec((ts, tn), lambda ia, j, t: (t, j))],
        out_specs=pl.BlockSpec((None, None, tka, tn), lambda ia, j, t: (l, joff + j, ia, 0)),
        out_shape=SDS(buf.shape, F32), input_output_aliases={0: 0},
        compiler_params=_cp("parallel", "parallel", "arbitrary"),
    )(buf, a, b)


def _proj(x, g, w_in, l, tm):
    s = x.shape[0]

    def body(x_ref, g_ref, w_ref, h_ref, p_ref):
        xf = x_ref[...]
        h = (xf * _rsq(xf, NORM_EPS) * g_ref[...]).astype(BF16)
        h_ref[...] = h
        p_ref[...] = jnp.dot(h, w_ref[...], preferred_element_type=F32)

    return pl.pallas_call(
        body, name="proj", grid=(s // tm,),
        in_specs=[pl.BlockSpec((tm, D), lambda i: (i, 0)), pl.BlockSpec((1, D), lambda i: (0, 0)),
                  pl.BlockSpec((None, D, P_IN), lambda i: (l, 0, 0))],
        out_specs=[pl.BlockSpec((tm, D), lambda i: (i, 0)), pl.BlockSpec((tm, P_IN), lambda i: (i, 0))],
        out_shape=[SDS((s, D), BF16), SDS((s, P_IN), F32)],
        compiler_params=_cp("parallel"),
    )(x, g, w_in)


def _mm_nt(a, w, l, tm, name):
    s, k_dim = a.shape
    n = w.shape[1]

    def body(a_ref, w_ref, o_ref):
        o_ref[...] = lax.dot_general(a_ref[...], w_ref[...], NT, preferred_element_type=F32)

    return pl.pallas_call(
        body, name=name, grid=(s // tm,),
        in_specs=[pl.BlockSpec((tm, k_dim), lambda i: (i, 0)), pl.BlockSpec((None, n, k_dim), lambda i: (l, 0, 0))],
        out_specs=pl.BlockSpec((tm, n), lambda i: (i, 0)),
        out_shape=SDS((s, n), F32), compiler_params=_cp("parallel"),
    )(a, w)


def _mm_nt_rmsbwd(dp, w_in, l, x, g, dx1, tm):
    s = x.shape[0]

    def body(dp_ref, w_ref, x_ref, g_ref, dx1_ref, dx_ref, dg_ref):
        dh = lax.dot_general(dp_ref[...], w_ref[...], NT, preferred_element_type=F32)
        dx, dgr = _rms_bwd_rows(x_ref[...], g_ref[...], dh)
        dx_ref[...] = dx1_ref[...] + dx
        part = jnp.sum(dgr, axis=0, keepdims=True)

        @pl.when(pl.program_id(0) == 0)
        def _():
            dg_ref[...] = part

        @pl.when(pl.program_id(0) > 0)
        def _():
            dg_ref[...] += part

    row = pl.BlockSpec((tm, D), lambda i: (i, 0))
    vec = pl.BlockSpec((1, D), lambda i: (0, 0))
    return pl.pallas_call(
        body, name="mix_bwd_dx", grid=(s // tm,),
        in_specs=[pl.BlockSpec((tm, P_IN), lambda i: (i, 0)), pl.BlockSpec((None, D, P_IN), lambda i: (l, 0, 0)), row, vec, row],
        out_specs=[row, vec], out_shape=[SDS((s, D), F32), SDS((1, D), F32)],
        compiler_params=_cp("arbitrary"),
    )(dp, w_in, x, g, dx1)


def _row_iota(shape):
    return lax.broadcasted_iota(jnp.int32, shape, 0)


def _lru_gates(xc, wa_ref, ba_ref, wx_ref, bx_ref, lam_ref):
    xb = xc.astype(BF16)
    r = _sig(jnp.dot(xb, wa_ref[...], preferred_element_type=F32) + ba_ref[...])
    ig = _sig(jnp.dot(xb, wx_ref[...], preferred_element_type=F32) + bx_ref[...])
    nl = -lam_ref[...]
    sp = jnp.maximum(nl, 0.0) + jnp.log(1.0 + jnp.exp(-jnp.abs(nl)))
    log_a = -LRU_C * r * sp
    a = jnp.exp(log_a)
    x2 = 2.0 * log_a
    series = x2 * (1.0 + x2 * (0.5 + x2 * (1.0 / 6.0 + x2 * (1.0 / 24.0 + x2 * (1.0 / 120.0)))))
    em1 = jnp.where(x2 > -0.05, series, jnp.exp(x2) - 1.0)
    mlt = jnp.sqrt(-em1)
    return r, ig, a, mlt, sp


def _conv_taps(src_ref, w_ref, k_taps, pad, tc):
    acc = None
    for j in range(k_taps):
        term = w_ref[j:j + 1, :] * src_ref[pl.ds(pad - (k_taps - 1) + j, tc), :]
        acc = term if acc is None else acc + term
    return acc


def _gelu_parts(x):
    c0 = math.sqrt(2.0 / math.pi)
    inner = c0 * (x + 0.044715 * x * x * x)
    t = jnp.tanh(inner)
    gl = 0.5 * x * (1.0 + t)
    dgl = 0.5 * (1.0 + t) + 0.5 * x * (1.0 - t * t) * c0 * (1.0 + 3.0 * 0.044715 * x * x)
    return gl, dgl


def _lru_fwd(proj, cw, cb, wa, ba, wx, bx, lam, gg, tc):
    s = proj.shape[0]
    pad = 8

    def body(xcur_ref, xprev_ref, gate_ref, cw_ref, cb_ref, wa_ref, ba_ref, wx_ref, bx_ref, lam_ref, gg_ref,
             yn_ref, h_ref, xs_ref, hc_ref):
        i = pl.program_id(0)

        @pl.when(i == 0)
        def _():
            hc_ref[...] = jnp.zeros_like(hc_ref)

        xs_ref[0:pad, :] = jnp.where(i > 0, xprev_ref[tc - pad:tc, :], 0.0)
        xs_ref[pad:pad + tc, :] = xcur_ref[...]
        xc = _conv_taps(xs_ref, cw_ref, LRU_K, pad, tc) + cb_ref[...]
        _, ig, a, mlt, _ = _lru_gates(xc, wa_ref, ba_ref, wx_ref, bx_ref, lam_ref)
        u = mlt * (ig * xc)
        row = _row_iota((tc, W_A))
        d = 1
        while d < tc:
            ok = row >= d
            a_sh = jnp.where(ok, pltpu.roll(a, d, axis=0), 1.0)
            u_sh = jnp.where(ok, pltpu.roll(u, d, axis=0), 0.0)
            u = a * u_sh + u
            a = a * a_sh
            d *= 2
        h = u + a * hc_ref[...]
        hc_ref[...] = jnp.sum(jnp.where(row == tc - 1, h, 0.0), axis=0, keepdims=True)
        h_ref[...] = h
        gl, _ = _gelu_parts(gate_ref[...])
        ya = gl * h
        yn_ref[...] = (ya * _rsq(ya, NORM_EPS) * gg_ref[...]).astype(BF16)

    blk = lambda c: pl.BlockSpec((tc, W_A), lambda i, c=c: (i, c))
    full = lambda a: pl.BlockSpec(a.shape, lambda i: (0,) * a.ndim)
    params = [cw, cb, wa, ba, wx, bx, lam, gg]
    return pl.pallas_call(
        body, name="lru_fwd", grid=(s // tc,),
        in_specs=[blk(0), pl.BlockSpec((tc, W_A), lambda i: (jnp.maximum(i - 1, 0), 0)), blk(1)] + [full(a) for a in params],
        out_specs=[pl.BlockSpec((tc, W_A), lambda i: (i, 0))] * 2,
        out_shape=[SDS((s, W_A), BF16), SDS((s, W_A), F32)],
        scratch_shapes=[pltpu.VMEM((tc + pad, W_A), F32), pltpu.VMEM((1, W_A), F32)],
        compiler_params=_cp("arbitrary"),
    )(proj, proj, proj, *params)


def _acc(ref, first, val):
    @pl.when(first)
    def _():
        ref[...] = val

    @pl.when(jnp.logical_not(first))
    def _():
        ref[...] += val


def _lru_bwd(dy, proj, h, cw, cb, wa, ba, wx, bx, lam, gg, tc):
    s = proj.shape[0]
    nc = s // tc
    pad = 8

    def body(dy_ref, xcur_ref, xprev_ref, gate_ref, h_ref, hprev_ref, cw_ref, cb_ref, wa_ref, ba_ref, wx_ref, bx_ref,
             lam_ref, gg_ref,
             dp_ref, dcw_ref, dcb_ref, dwa_ref, dba_ref, dwx_ref, dbx_ref, dlam_ref, dgg_ref,
             xs_ref, ds_ref, mu_ref, nx_ref):
        step = pl.program_id(0)
        i = nc - 1 - step
        first = step == 0

        @pl.when(first)
        def _():
            mu_ref[...] = jnp.zeros_like(mu_ref)
            nx_ref[...] = jnp.zeros_like(nx_ref)

        xs_ref[0:pad, :] = jnp.where(i > 0, xprev_ref[tc - pad:tc, :], 0.0)
        xs_ref[pad:pad + tc, :] = xcur_ref[...]
        xc = _conv_taps(xs_ref, cw_ref, LRU_K, pad, tc) + cb_ref[...]
        r, ig, a, mlt, sp = _lru_gates(xc, wa_ref, ba_ref, wx_ref, bx_ref, lam_ref)
        hh = h_ref[...]
        gate = gate_ref[...]
        gl, dgl = _gelu_parts(gate)
        ya = gl * hh
        dya, dggr = _rms_bwd_rows(ya, gg_ref[...], dy_ref[...])
        _acc(dgg_ref, first, jnp.sum(dggr, axis=0, keepdims=True))
        dp_ref[:, W_A:2 * W_A] = dya * hh * dgl
        dh = dya * gl

        row = _row_iota((tc, W_A))
        aa = a
        uu = a * dh
        d = 1
        while d < tc:
            ok = row < tc - d
            a_sh = jnp.where(ok, pltpu.roll(aa, tc - d, axis=0), 1.0)
            u_sh = jnp.where(ok, pltpu.roll(uu, tc - d, axis=0), 0.0)
            uu = uu + aa * u_sh
            aa = aa * a_sh
            d *= 2
        cin = mu_ref[...]
        mu = uu + aa * cin
        lam_t = dh + jnp.where(row == tc - 1, cin, pltpu.roll(mu, tc - 1, axis=0))
        mu_ref[...] = jnp.sum(jnp.where(row == 0, mu, 0.0), axis=0, keepdims=True)
        hm1 = jnp.where(row == 0, jnp.where(i > 0, pltpu.roll(hprev_ref[...], 1, axis=0), 0.0),
                        pltpu.roll(hh, 1, axis=0))
        da = lam_t * hm1
        du = lam_t
        dmlt = du * ig * xc
        dig = du * mlt * xc
        dxc = du * mlt * ig
        dlog_a = da * a - dmlt * (a * a / mlt)
        dr = dlog_a * (-LRU_C * sp)
        dsp = jnp.sum(dlog_a * (-LRU_C * r), axis=0, keepdims=True)
        _acc(dlam_ref, first, dsp * (-_sig(-lam_ref[...])))
        dga = dr * r * (1.0 - r)
        dgx = dig * ig * (1.0 - ig)
        _acc(dba_ref, first, jnp.sum(dga, axis=0, keepdims=True))
        _acc(dbx_ref, first, jnp.sum(dgx, axis=0, keepdims=True))
        xb = xc.astype(BF16)
        dgab = dga.astype(BF16)
        dgxb = dgx.astype(BF16)
        _acc(dwa_ref, first, lax.dot_general(xb, dgab, TN, preferred_element_type=F32))
        _acc(dwx_ref, first, lax.dot_general(xb, dgxb, TN, preferred_element_type=F32))
        dxc = (dxc + lax.dot_general(dgab, wa_ref[...], NT, preferred_element_type=F32)
               + lax.dot_general(dgxb, wx_ref[...], NT, preferred_element_type=F32))

        _acc(dcb_ref, first, jnp.sum(dxc, axis=0, keepdims=True))
        r8 = _row_iota((8, W_A))
        dcw = jnp.zeros((8, W_A), F32)
        for j in range(LRU_K):
            tap = jnp.sum(dxc * xs_ref[pl.ds(pad - (LRU_K - 1) + j, tc), :], axis=0, keepdims=True)
            dcw = dcw + jnp.where(r8 == j, tap, 0.0)
        _acc(dcw_ref, first, dcw)
        ds_ref[0:tc, :] = dxc
        ds_ref[tc:tc + pad, :] = nx_ref[...]
        dlx = None
        for j in range(LRU_K):
            term = cw_ref[j:j + 1, :] * ds_ref[pl.ds(LRU_K - 1 - j, tc), :]
            dlx = term if dlx is None else dlx + term
        dp_ref[:, 0:W_A] = dlx
        nx_ref[...] = dxc[0:pad, :]

    rev = lambda c: pl.BlockSpec((tc, W_A), lambda t, c=c: (nc - 1 - t, c))
    prev = lambda c: pl.BlockSpec((tc, W_A), lambda t, c=c: (jnp.maximum(nc - 2 - t, 0), c))
    full = lambda a: pl.BlockSpec(a.shape, lambda t: (0,) * a.ndim)
    params = [cw, cb, wa, ba, wx, bx, lam, gg]
    vec = SDS((1, W_A), F32)
    sq = SDS((W_A, W_A), F32)
    outs = [SDS((s, 2 * W_A), F32), SDS((8, W_A), F32), vec, sq, vec, sq, vec, vec, vec]
    return pl.pallas_call(
        body, name="lru_bwd", grid=(nc,),
        in_specs=[rev(0), rev(0), prev(0), rev(1), rev(0), prev(0)] + [full(a) for a in params],
        out_specs=[pl.BlockSpec((tc, 2 * W_A), lambda t: (nc - 1 - t, 0))]
        + [pl.BlockSpec(o.shape, lambda t: (0, 0)) for o in outs[1:]],
        out_shape=outs,
        scratch_shapes=[pltpu.VMEM((tc + pad, W_A), F32), pltpu.VMEM((tc + pad, W_A), F32),
                        pltpu.VMEM((1, W_A), F32), pltpu.VMEM((pad, W_A), F32)],
        compiler_params=_cp("arbitrary"),
    )(dy, proj, proj, proj, h, h, *params)


def _attn_stack(qa, qb, kvh):
    lane = lax.broadcasted_iota(jnp.int32, qa.shape, 1)
    keep = (lane >= HD) if kvh == 1 else (lane < HD)
    parts = []
    for tile in (qa, qb):
        for half in (0, 1):
            y = tile if half == kvh else pltpu.roll(tile, HD, axis=1)
            parts.append(jnp.where(keep, y, 0.0))
    return jnp.concatenate(parts, axis=0)


def _attn_unstack(o, kvh):
    lane = lax.broadcasted_iota(jnp.int32, (BLK, 2 * HD), 1)
    tiles = []
    for t in range(2):
        halves = []
        for half in (0, 1):
            blk = o[(2 * t + half) * BLK:(2 * t + half + 1) * BLK, :]
            halves.append(blk if half == kvh else pltpu.roll(blk, HD, axis=1))
        tiles.append(jnp.where(lane < HD, halves[0], halves[1]))
    return tiles


def _attn_stack_all(x_ref_or_val):
    return jnp.concatenate([_attn_stack(x_ref_or_val[:, 256 * kvh:256 * kvh + 128],
                                        x_ref_or_val[:, 256 * kvh + 128:256 * kvh + 256], kvh) for kvh in range(2)], axis=0)


def _attn_unstack_all(o, dst_ref):
    for kvh in range(2):
        ta, tb = _attn_unstack(o[4 * BLK * kvh:4 * BLK * (kvh + 1), :], kvh)
        dst_ref[:, 256 * kvh:256 * kvh + 128] = ta
        dst_ref[:, 256 * kvh + 128:256 * kvh + 256] = tb


def _attn_windows(cur_ref, prev_ref, nb):
    blocks = [prev_ref[...]] + [cur_ref[b * BLK:(b + 1) * BLK, :] for b in range(nb)]
    return [jnp.concatenate(blocks[b:b + 2], axis=0).astype(BF16) for b in range(nb)]


def _attn_probs(qs, kw, n, sink_ref):
    rows = NQ * BLK
    sc = lax.dot_general(qs.astype(BF16), kw, NT, preferred_element_type=F32) * SCALE
    qi = lax.broadcasted_iota(jnp.int32, (rows, 2 * BLK), 0) & (BLK - 1)
    kj = lax.broadcasted_iota(jnp.int32, (rows, 2 * BLK), 1)
    rel = BLK + qi - kj
    mask = (rel >= 0) & (rel < BLK) & ((n - 1) * BLK + kj >= 0)
    head = lax.broadcasted_iota(jnp.int32, (rows, 1), 0) // BLK
    sk = jnp.zeros((rows, 1), F32)
    for h in range(NQ):
        sk = jnp.where(head == h, sink_ref[h:h + 1, 0:1], sk)
    sh = jnp.where(mask, sc, NEG_BIG)
    m = jnp.maximum(jnp.max(sh, axis=-1, keepdims=True), sk)
    e = jnp.exp(sh - m)
    es = jnp.exp(sk - m)
    rz = 1.0 / (jnp.sum(e, axis=-1, keepdims=True) + es)
    return e * rz, es * rz


def _attn_fwd(proj, sinks8, gg):
    s = proj.shape[0]
    nb = ATT_NB_FWD

    def body(q_ref, kc_ref, kp_ref, vc_ref, vp_ref, sink_ref, gg_ref, yn_ref, ob_ref):
        kws, vws = _attn_windows(kc_ref, kp_ref, nb), _attn_windows(vc_ref, vp_ref, nb)
        for b in range(nb):
            rows = pl.ds(b * BLK, BLK)
            p, _ = _attn_probs(_attn_stack_all(q_ref.at[rows, :]), kws[b], nb * pl.program_id(0) + b, sink_ref)
            _attn_unstack_all(jnp.dot(p.astype(BF16), vws[b], preferred_element_type=F32), ob_ref.at[rows, :])
        ob = ob_ref[...]
        yn_ref[...] = (ob * _rsq(ob, NORM_EPS) * gg_ref[...]).astype(BF16)

    tb = nb * BLK
    cur = lambda c: pl.BlockSpec((tb, 128), lambda m, c=c: (m, c))
    prev = lambda c: pl.BlockSpec((BLK, 128), lambda m, c=c: (jnp.maximum(nb * m - 1, 0), c))
    out = pl.BlockSpec((tb, W_B), lambda m: (m, 0))
    return pl.pallas_call(
        body, name="attn_fwd", grid=(s // tb,),
        in_specs=[pl.BlockSpec((tb, W_B), lambda m: (m, 1)), cur(8), prev(8), cur(9), prev(9),
                  pl.BlockSpec((8, 128), lambda n: (0, 0)), pl.BlockSpec((1, W_B), lambda n: (0, 0))],
        out_specs=[out, out], out_shape=[SDS((s, W_B), BF16), SDS((s, W_B), F32)],
        compiler_params=_cp("parallel"),
    )(proj, proj, proj, proj, proj, sinks8, gg)


def _attn_bwd(dy, proj, ob, sinks8, gg):
    s = proj.shape[0]
    nb = ATT_NB_BWD

    def body(dya_ref, dyb_ref, q_ref, kc_ref, kp_ref, vc_ref, vp_ref, ob_ref, sink_ref, gg_ref,
             dq_ref, dcur_ref, dprev_ref, dsink_ref, dgg_ref):
        first = pl.program_id(0) == 0
        kws, vws = _attn_windows(kc_ref, kp_ref, nb), _attn_windows(vc_ref, vp_ref, nb)
        dyn = jnp.concatenate([dya_ref[...], dyb_ref[...]], axis=1)
        dob, dggr = _rms_bwd_rows(ob_ref[...], gg_ref[...], dyn)
        _acc(dgg_ref, first, jnp.sum(dggr, axis=0, keepdims=True))
        r8 = _row_iota((8, 128))
        dsk = jnp.zeros((8, 128), F32)
        for b in range(nb):
            rows = pl.ds(b * BLK, BLK)
            qs = _attn_stack_all(q_ref.at[rows, :])
            p, psink = _attn_probs(qs, kws[b], nb * pl.program_id(0) + b, sink_ref)
            dosb = _attn_stack_all(dob[b * BLK:(b + 1) * BLK, :]).astype(BF16)
            dp = lax.dot_general(dosb, vws[b], NT, preferred_element_type=F32)
            dd = jnp.sum(p * dp, axis=-1, keepdims=True)
            dsb = (p * (dp - dd) * SCALE).astype(BF16)
            dsink_rows = -psink * dd
            for h in range(NQ):
                dsk = dsk + jnp.where(r8 == h, jnp.sum(dsink_rows[h * BLK:(h + 1) * BLK, :], axis=0, keepdims=True), 0.0)
            _attn_unstack_all(jnp.dot(dsb, kws[b], preferred_element_type=F32), dq_ref.at[rows, :])
            dkw = lax.dot_general(dsb, qs.astype(BF16), TN, preferred_element_type=F32)
            dvw = lax.dot_general(p.astype(BF16), dosb, TN, preferred_element_type=F32)
            dprev_ref[rows, 0:128] = dkw[0:BLK, :]
            dprev_ref[rows, 128:256] = dvw[0:BLK, :]
            dcur_ref[rows, 0:128] = dkw[BLK:2 * BLK, :]
            dcur_ref[rows, 128:256] = dvw[BLK:2 * BLK, :]
        _acc(dsink_ref, first, dsk)

    tb = nb * BLK
    cur = lambda c: pl.BlockSpec((tb, 128), lambda m, c=c: (m, c))
    prev = lambda c: pl.BlockSpec((BLK, 128), lambda m, c=c: (jnp.maximum(nb * m - 1, 0), c))
    wide = pl.BlockSpec((tb, W_B), lambda m: (m, 0))
    half = pl.BlockSpec((tb, 256), lambda m: (m, 0))
    return pl.pallas_call(
        body, name="attn_bwd", grid=(s // tb,),
        in_specs=[pl.BlockSpec((tb, 256), lambda m: (m, 1)), pl.BlockSpec((tb, 256), lambda m: (m, 2)),
                  pl.BlockSpec((tb, W_B), lambda m: (m, 1)), cur(8), prev(8), cur(9), prev(9), wide,
                  pl.BlockSpec((8, 128), lambda n: (0, 0)), pl.BlockSpec((1, W_B), lambda n: (0, 0))],
        out_specs=[wide, half, half, pl.BlockSpec((8, 128), lambda n: (0, 0)), pl.BlockSpec((1, W_B), lambda n: (0, 0))],
        out_shape=[SDS((s, W_B), F32), SDS((s, 256), F32), SDS((s, 256), F32), SDS((8, 128), F32), SDS((1, W_B), F32)],
        compiler_params=_cp("arbitrary"),
    )(dy, dy, proj, proj, proj, proj, proj, ob, sinks8, gg)


def _ln_parts(y1, eps=LN_EPS):
    mu = jnp.mean(y1, axis=-1, keepdims=True)
    xc = y1 - mu
    rstd = lax.rsqrt(jnp.mean(xc * xc, axis=-1, keepdims=True) + eps)
    return xc * rstd, rstd


def _conf_fwd(proj, cw, cb, lg, lb, gg, tc):
    s = proj.shape[0]
    pad = 32

    def body(ac_ref, gc_ref, ap_ref, gp_ref, cw_ref, cb_ref, lg_ref, lb_ref, gg_ref, yn_ref, y1_ref, ys_ref):
        i = pl.program_id(0)
        tail = ap_ref[tc - pad:tc, :] * _sig(gp_ref[tc - pad:tc, :])
        ys_ref[0:pad, :] = jnp.where(i > 0, tail, 0.0)
        ys_ref[pad:pad + tc, :] = ac_ref[...] * _sig(gc_ref[...])
        y1 = _conv_taps(ys_ref, cw_ref, CONV_K, pad, tc) + cb_ref[...]
        y1_ref[...] = y1
        xh, _ = _ln_parts(y1)
        yl = xh * lg_ref[...] + lb_ref[...]
        yc = yl * _sig(yl)
        yn_ref[...] = (yc * _rsq(yc, NORM_EPS) * gg_ref[...]).astype(BF16)

    cur = lambda c: pl.BlockSpec((tc, W_C), lambda i, c=c: (i, c))
    prev = lambda c: pl.BlockSpec((tc, W_C), lambda i, c=c: (jnp.maximum(i - 1, 0), c))
    full = lambda a: pl.BlockSpec(a.shape, lambda i: (0,) * a.ndim)
    params = [cw, cb, lg, lb, gg]
    out = pl.BlockSpec((tc, W_C), lambda i: (i, 0))
    return pl.pallas_call(
        body, name="conf_fwd", grid=(s // tc,),
        in_specs=[cur(5), cur(6), prev(5), prev(6)] + [full(a) for a in params],
        out_specs=[out, out], out_shape=[SDS((s, W_C), BF16), SDS((s, W_C), F32)],
        scratch_shapes=[pltpu.VMEM((tc + pad, W_C), F32)],
        compiler_params=_cp("parallel"),
    )(proj, proj, proj, proj, *params)


def _conf_bwd(dy, proj, y1, cw, cb, lg, lb, gg, tc):
    s = proj.shape[0]
    nc = s // tc
    pad = 32

    def body(dy_ref, ac_ref, gc_ref, ap_ref, gp_ref, y1_ref, cw_ref, cb_ref, lg_ref, lb_ref, gg_ref,
             dp_ref, dcw_ref, dcb_ref, dlg_ref, dlb_ref, dgg_ref, ys_ref, ds_ref, nx_ref):
        step = pl.program_id(0)
        i = nc - 1 - step
        first = step == 0

        @pl.when(first)
        def _():
            nx_ref[...] = jnp.zeros_like(nx_ref)

        a = ac_ref[...]
        sg = _sig(gc_ref[...])
        tail = ap_ref[tc - pad:tc, :] * _sig(gp_ref[tc - pad:tc, :])
        ys_ref[0:pad, :] = jnp.where(i > 0, tail, 0.0)
        ys_ref[pad:pad + tc, :] = a * sg
        xh, rstd = _ln_parts(y1_ref[...])
        yl = xh * lg_ref[...] + lb_ref[...]
        sl = _sig(yl)
        yc = yl * sl
        dyc, dggr = _rms_bwd_rows(yc, gg_ref[...], dy_ref[...])
        _acc(dgg_ref, first, jnp.sum(dggr, axis=0, keepdims=True))
        dyl = dyc * sl * (1.0 + yl * (1.0 - sl))
        _acc(dlg_ref, first, jnp.sum(dyl * xh, axis=0, keepdims=True))
        _acc(dlb_ref, first, jnp.sum(dyl, axis=0, keepdims=True))
        dxh = dyl * lg_ref[...]
        dy1 = rstd * (dxh - jnp.mean(dxh, axis=-1, keepdims=True) - xh * jnp.mean(dxh * xh, axis=-1, keepdims=True))
        _acc(dcb_ref, first, jnp.sum(dy1, axis=0, keepdims=True))
        r32 = _row_iota((32, W_C))
        dcw = jnp.zeros((32, W_C), F32)
        for j in range(CONV_K):
            tap = jnp.sum(dy1 * ys_ref[pl.ds(pad - (CONV_K - 1) + j, tc), :], axis=0, keepdims=True)
            dcw = dcw + jnp.where(r32 == j, tap, 0.0)
        _acc(dcw_ref, first, dcw)
        ds_ref[0:tc, :] = dy1
        ds_ref[tc:tc + pad, :] = nx_ref[...]
        dy0 = None
        for j in range(CONV_K):
            term = cw_ref[j:j + 1, :] * ds_ref[pl.ds(CONV_K - 1 - j, tc), :]
            dy0 = term if dy0 is None else dy0 + term
        dp_ref[:, 0:W_C] = dy0 * sg
        dp_ref[:, W_C:2 * W_C] = dy0 * a * sg * (1.0 - sg)
        nx_ref[...] = dy1[0:pad, :]

    rev = lambda c: pl.BlockSpec((tc, W_C), lambda t, c=c: (nc - 1 - t, c))
    prev = lambda c: pl.BlockSpec((tc, W_C), lambda t, c=c: (jnp.maximum(nc - 2 - t, 0), c))
    full = lambda a: pl.BlockSpec(a.shape, lambda t: (0,) * a.ndim)
    params = [cw, cb, lg, lb, gg]
    vec = SDS((1, W_C), F32)
    outs = [SDS((s, 2 * W_C), F32), SDS((32, W_C), F32), vec, vec, vec, vec]
    return pl.pallas_call(
        body, name="conf_bwd", grid=(nc,),
        in_specs=[rev(3), rev(5), rev(6), prev(5), prev(6), rev(0)] + [full(a) for a in params],
        out_specs=[pl.BlockSpec((tc, 2 * W_C), lambda t: (nc - 1 - t, 0))]
        + [pl.BlockSpec(o.shape, lambda t: (0, 0)) for o in outs[1:]],
        out_shape=outs,
        scratch_shapes=[pltpu.VMEM((tc + pad, W_C), F32), pltpu.VMEM((tc + pad, W_C), F32), pltpu.VMEM((pad, W_C), F32)],
        compiler_params=_cp("arbitrary"),
    )(dy, proj, proj, proj, proj, y1, *params)


def _assemble_dproj(dlru, dq, dcur, dprev, dconf):
    s = dq.shape[0]
    nb = s // BLK

    def body(dl_ref, dq_ref, dc_ref, dn_ref, df_ref, o_ref):
        n = pl.program_id(0)
        o_ref[:, 0:512] = dl_ref[...].astype(BF16)
        o_ref[:, 512:1024] = dq_ref[...].astype(BF16)
        o_ref[:, 1024:1280] = (dc_ref[...] + jnp.where(n < nb - 1, dn_ref[...], 0.0)).astype(BF16)
        o_ref[:, 1280:1792] = df_ref[...].astype(BF16)

    wide = pl.BlockSpec((BLK, 512), lambda n: (n, 0))
    return pl.pallas_call(
        body, name="assemble_dproj", grid=(nb,),
        in_specs=[wide, wide, pl.BlockSpec((BLK, 256), lambda n: (n, 0)),
                  pl.BlockSpec((BLK, 256), lambda n: (jnp.minimum(n + 1, nb - 1), 0)), wide],
        out_specs=pl.BlockSpec((BLK, P_IN), lambda n: (n, 0)), out_shape=SDS((s, P_IN), BF16),
        compiler_params=_cp("parallel"),
    )(dlru, dq, dcur, dprev, dconf)


def _loss_grad(y, t, tm):
    s = y.shape[0]

    def body(y_ref, t_ref, dy_ref, l_ref):
        err = y_ref[...] - t_ref[...]
        dy_ref[...] = err * (1.0 / D)
        _acc(l_ref, pl.program_id(0) == 0, jnp.sum(err * err, axis=0, keepdims=True))

    row = pl.BlockSpec((tm, D), lambda i: (i, 0))
    return pl.pallas_call(
        body, name="loss_grad", grid=(s // tm,), in_specs=[row, row],
        out_specs=[row, pl.BlockSpec((1, D), lambda i: (0, 0))],
        out_shape=[SDS((s, D), F32), SDS((1, D), F32)], compiler_params=_cp("arbitrary"),
    )(y, t)


def _block_diag(w):
    rows = [jnp.concatenate([w[h] if k == h else jnp.zeros((64, 64), w.dtype) for k in range(4)], axis=1) for h in range(4)]
    return jnp.concatenate(rows, axis=0)


def _diag_blocks(m):
    return jnp.stack([m[64 * h:64 * (h + 1), 64 * h:64 * (h + 1)] for h in range(4)])


def _layer_params(small, l):
    v = lambda name: small[name][l].reshape(1, -1)
    gg = small["group_g"][l]
    return dict(
        ffn1_pre=v("ffn1_pre_g"), ffn1_post=v("ffn1_post_g"), mix_pre=v("mix_pre_g"), mix_post=v("mix_post_g"),
        ffn2_pre=v("ffn2_pre_g"), ffn2_post=v("ffn2_post_g"), lru_cb=v("lru_conv_b"),
        wa=_block_diag(small["lru_w_a"][l]).astype(BF16), ba=v("lru_b_a"),
        wx=_block_diag(small["lru_w_x"][l]).astype(BF16), bx=v("lru_b_x"), lam=v("lru_lambda"),
        sinks8=jnp.broadcast_to(small["attn_sinks"][l][:, None], (NQ, 128)),
        conv_b=v("conv_b"), ln_g=v("conv_ln_g"), ln_b=v("conv_ln_b"),
        gg_a=gg[0:W_A].reshape(1, -1), gg_b=gg[W_A:W_A + W_B].reshape(1, -1), gg_c=gg[W_A + W_B:].reshape(1, -1),
    )


def _forward_layer(x, weights, p, tiles, deps=()):
    _, mm, _, tc = tiles
    big = dict(weights("ffn1", x))
    p = dict(p)
    sv = dict(x0=x)
    h1, g1, u1, a1 = _ffn_up(x, p["ffn1_pre"], big["ffn1_w_gu"], 0, mm, deps)
    z1, x = _mm_rms_res(a1, big["ffn1_w_down"], 0, x, p["ffn1_post"], 0.5, mm, FH, "ffn_down")
    sv.update(h1=h1, g1=g1, u1=u1, a1=a1, z1=z1, x1=x)
    big.update(weights("mix", x))
    p.update(lru_cw=big.pop("lru_conv_w"), conv_w=big.pop("conv_w"))
    hn, proj = _proj(x, p["mix_pre"], big["w_in"], 0, mm)
    yn_a, hl = _lru_fwd(proj, p["lru_cw"], p["lru_cb"], p["wa"], p["ba"], p["wx"], p["bx"], p["lam"], p["gg_a"], tc)
    yn_b, ob = _attn_fwd(proj, p["sinks8"], p["gg_b"])
    yn_c, y1 = _conf_fwd(proj, p["conv_w"], p["conv_b"], p["ln_g"], p["ln_b"], p["gg_c"], tc)
    ycat = jnp.concatenate([yn_a, yn_b, yn_c], axis=1)
    zo, x = _mm_rms_res(ycat, big["w_out"], 0, x, p["mix_post"], 1.0, mm, D, "mix_out")
    sv.update(hn=hn, proj=proj, hl=hl, ob=ob, y1=y1, ycat=ycat, zo=zo, x2=x)
    big.update(weights("ffn2", x))
    h2, g2, u2, a2 = _ffn_up(x, p["ffn2_pre"], big["ffn2_w_gu"], 0, mm)
    z2, x = _mm_rms_res(a2, big["ffn2_w_down"], 0, x, p["ffn2_post"], 0.5, mm, FH, "ffn_down")
    sv.update(h2=h2, g2=g2, u2=u2, a2=a2, z2=z2, p=p, big=big)
    return x, sv


def _grad_buffers():
    empty = lambda *shape: lax.empty(shape, F32)
    return dict(ffn1_w_gu=empty(1, NSHARD, D, FH), ffn2_w_gu=empty(1, NSHARD, D, FH), ffn1_w_down=empty(1, 1, DFF, D),
                ffn2_w_down=empty(1, 1, DFF, D), w_in=empty(1, 1, D, P_IN), w_out=empty(1, 1, D, D))


def _backward_layer(dx, sv, bufs, tiles, stage):
    p, big = sv["p"], sv["big"]
    tm, mm, dw, tc = tiles
    gr = {}

    def ffn_bwd(dx, which, xin, h, g, u, a, z, pre, post, deps):
        dz, dpost = _rms_bwd(dx, z, post, 0.5, tm, "ffn_post_bwd", deps)
        dg, du = _ffn_bwd_mid(dz, big[which + "_w_down"], 0, g, u, mm)
        bufs[which + "_w_down"] = _mm_tn_into(bufs[which + "_w_down"], a, dz, 0, 0, FH, D, dw, "dw_down")
        bufs[which + "_w_gu"] = _mm_tn_into(bufs[which + "_w_gu"], h, dg, 0, 0, D, FH, dw, "dw_gate")
        bufs[which + "_w_gu"] = _mm_tn_into(bufs[which + "_w_gu"], h, du, 0, 2, D, FH, dw, "dw_up")
        dxn, dpre = _ffn_bwd_dh(dg, du, big[which + "_w_gu"], 0, xin, pre, dx, mm)
        return dxn, dpre, dpost

    dx, gr["ffn2_pre_g"], gr["ffn2_post_g"] = ffn_bwd(dx, "ffn2", sv["x2"], sv["h2"], sv["g2"], sv["u2"], sv["a2"],
                                                      sv["z2"], p["ffn2_pre"], p["ffn2_post"], stage({}, dx))
    done = {n: bufs[n] for n in ("ffn2_w_gu", "ffn2_w_down")}
    do, gr["mix_post_g"] = _rms_bwd(dx, sv["zo"], p["mix_post"], 1.0, tm, "mix_post_bwd", stage(done, dx))
    bufs["w_out"] = _mm_tn_into(bufs["w_out"], sv["ycat"], do, 0, 0, D, D, dw, "dw_out")
    dy = _mm_nt(do, big["w_out"], 0, mm, "mix_dy")
    proj = sv["proj"]
    (dlru, dcw, gr["lru_conv_b"], dwa, gr["lru_b_a"], dwx, gr["lru_b_x"], gr["lru_lambda"], dgg_a) = _lru_bwd(
        dy, proj, sv["hl"], p["lru_cw"], p["lru_cb"], p["wa"], p["ba"], p["wx"], p["bx"], p["lam"], p["gg_a"], tc)
    dq, dcur, dprev, dsk, dgg_b = _attn_bwd(dy, proj, sv["ob"], p["sinks8"], p["gg_b"])
    dconf, dconvw, gr["conv_b"], gr["conv_ln_g"], gr["conv_ln_b"], dgg_c = _conf_bwd(
        dy, proj, sv["y1"], p["conv_w"], p["conv_b"], p["ln_g"], p["ln_b"], p["gg_c"], tc)
    dproj = _assemble_dproj(dlru, dq, dcur, dprev, dconf)
    bufs["w_in"] = _mm_tn_into(bufs["w_in"], sv["hn"], dproj, 0, 0, D, P_IN, dw, "dw_in")
    dx, gr["mix_pre_g"] = _mm_nt_rmsbwd(dproj, big["w_in"], 0, sv["x1"], p["mix_pre"], dx, mm)
    gr["lru_conv_w"] = dcw[0:LRU_K]
    gr["lru_w_a"] = _diag_blocks(dwa)
    gr["lru_w_x"] = _diag_blocks(dwx)
    gr["attn_sinks"] = dsk[:, 0]
    gr["conv_w"] = dconvw[0:CONV_K]
    gr["group_g"] = jnp.concatenate([dgg_a, dgg_b, dgg_c], axis=1)
    dx, gr["ffn1_pre_g"], gr["ffn1_post_g"] = ffn_bwd(dx, "ffn1", sv["x0"], sv["h1"], sv["g1"], sv["u1"], sv["a1"],
                                                      sv["z1"], p["ffn1_pre"], p["ffn1_post"],
                                                      stage({n: bufs[n] for n in ("w_in", "w_out")}, dx))
    return dx, gr


def _tiles(s):
    return min(512, s), min(1024, s), min(2048, s), min(512, s // 2)


HBM_SPEC = pl.BlockSpec(memory_space=pltpu.HBM)
SEM_SPEC = pl.BlockSpec(memory_space=pltpu.SEMAPHORE)
EFFECT = pltpu.SideEffectType.DATAFLOW_SIDE_EFFECTING


def _place():
    x, y, c = lax.axis_index("x"), lax.axis_index("y"), lax.axis_index("c")
    return x, y, c, [(1 - x, y), (x, 1 - y), (1 - x, 1 - y)]


def _rcopy(src, dst, send_sems, recv_sems, k, to):
    return pltpu.make_async_remote_copy(src_ref=src, dst_ref=dst, send_sem=send_sems.at[k], recv_sem=recv_sems.at[k],
                                        device_id=to, device_id_type=MESH)


def _half(rows, which):
    return pl.ds(which * (rows // 2), rows // 2)


def _place_shard(w, l, p_idx, dtype):
    _, rows, cols = w.shape
    tr = _rows_per_block(rows, cols, 16) if rows % 16 == 0 else rows

    def body(p_ref, buf_ref, w_ref, o_ref):
        o_ref[...] = w_ref[...].astype(dtype)

    spec = pltpu.PrefetchScalarGridSpec(
        num_scalar_prefetch=1, grid=(rows // tr,),
        in_specs=[ANY, pl.BlockSpec((None, tr, cols), lambda i, pr: (l, i, 0))],
        out_specs=pl.BlockSpec((None, None, tr, cols), lambda i, pr: (0, pr[0], i, 0)))
    shape = (1, NSHARD, rows, cols)
    return pl.pallas_call(body, name="place_shard", grid_spec=spec, out_shape=SDS(shape, dtype),
                          input_output_aliases={1: 0}, compiler_params=_cp("parallel"),
                          )(p_idx, lax.empty(shape, dtype), w)


def _gather_two_level(bufs, n_halved):
    n = len(bufs)

    def body(*refs):
        outs = refs[n:2 * n]
        send_sems, recv_sems = refs[2 * n:]
        x, y, c, chips = _place()
        p = 2 * x + y
        me, sibling = (x, y, c), (x, y, 1 - c)

        def blk(a, q, half):
            return outs[a].at[0, q, _half(outs[a].shape[2], half)] if a < n_halved else outs[a].at[0, q]

        def cp(a, k, q, half, to):
            return _rcopy(blk(a, q, half), blk(a, q, half), send_sems, recv_sems, 6 * a + k, to)

        first = [cp(a, j, p, c, (*chip, c)) for a in range(n) for j, chip in enumerate(chips)]
        for d in first:
            d.start()
        passed = []
        for a in range(n):
            for j, chip in enumerate(chips):
                q = 2 * chip[0] + chip[1]
                cp(a, j, q, c, me).wait_recv()
                if a < n_halved:
                    passed.append(cp(a, 3 + j, q, c, sibling))
                    passed[-1].start()
        for a in range(n_halved):
            for j, chip in enumerate(chips):
                cp(a, 3 + j, 2 * chip[0] + chip[1], 1 - c, me).wait_recv()
        for d in first + passed:
            d.wait_send()

    return pl.pallas_call(
        body, name="gather_layer0", in_specs=[ANY] * n, out_specs=[ANY] * n,
        out_shape=[SDS(b.shape, b.dtype) for b in bufs], input_output_aliases={a: a for a in range(n)},
        scratch_shapes=[pltpu.SemaphoreType.DMA((6 * n,)), pltpu.SemaphoreType.DMA((6 * n,))],
    )(*bufs)


def _run_plans(plans, refs, send_sems, recv_sems):
    cps, b0, s0 = [], 0, 0
    for plan, nb, ns in plans:
        cps += plan(refs[b0:b0 + nb], send_sems, recv_sems, s0)
        b0, s0 = b0 + nb, s0 + ns
    return cps


def _exchange(name, bufs, plans):
    n = len(bufs)
    nsem = sum(ns for _, _, ns in plans)

    def body(*refs):
        cps = _run_plans(plans, refs[n:2 * n], refs[2 * n], refs[2 * n + 1])
        for cp in cps:
            cp.start()
        for cp in cps:
            cp.wait()

    return pl.pallas_call(
        body, name=name, in_specs=[ANY] * n, out_specs=[ANY] * n, out_shape=[SDS(b.shape, b.dtype) for b in bufs],
        input_output_aliases={a: a for a in range(n)},
        scratch_shapes=[pltpu.SemaphoreType.DMA((nsem,)), pltpu.SemaphoreType.DMA((nsem,))],
    )(*bufs)


def _exchange_start(name, bufs, plans, deps=()):
    n = len(bufs)
    nsem = sum(ns for _, _, ns in plans)
    deps = list(deps)
    first_out = n + len(deps)

    def body(*refs):
        for cp in _run_plans(plans, refs[:n], refs[first_out], refs[first_out + 1]):
            cp.start()
        token = refs[first_out + 2 + n]
        token[...] = jnp.zeros_like(token)

    outs = pl.pallas_call(
        body, name=name,
        out_shape=(pltpu.SemaphoreType.DMA((nsem,)), pltpu.SemaphoreType.DMA((nsem,)),
                   *[pltpu.HBM(b.shape, b.dtype) for b in bufs], SDS((8, 128), F32)),
        in_specs=[HBM_SPEC] * n + [ANY] * len(deps),
        out_specs=(SEM_SPEC, SEM_SPEC, *[HBM_SPEC] * n, pl.BlockSpec(memory_space=pltpu.VMEM)),
        input_output_aliases={a: 2 + a for a in range(n)},
        compiler_params=pltpu.CompilerParams(has_side_effects=EFFECT),
    )(*[pltpu.with_memory_space_constraint(b, pltpu.HBM) for b in bufs], *deps)
    return outs[0], outs[1], list(outs[2:2 + n]), outs[2 + n]


def _exchange_wait(name, send_sems, recv_sems, bufs, plans, after):
    n = len(bufs)

    def body(*refs):
        for cp in _run_plans(plans, refs[:n], refs[n], refs[n + 1]):
            cp.wait_send()
            cp.wait_recv()

    return pl.pallas_call(
        body, name=name, out_shape=[pltpu.HBM(b.shape, b.dtype) for b in bufs],
        in_specs=[HBM_SPEC] * n + [SEM_SPEC, SEM_SPEC, ANY], out_specs=[HBM_SPEC] * n,
        input_output_aliases={a: a for a in range(n)},
        compiler_params=pltpu.CompilerParams(has_side_effects=EFFECT),
    )(*bufs, send_sems, recv_sems, after)


def _plan_gather(refs, send_sems, recv_sems, base):
    x, y, c, chips = _place()
    p = 2 * x + y
    return [_rcopy(r.at[0, p], r.at[0, p], send_sems, recv_sems, base + 3 * a + j, (*chip, c))
            for a, r in enumerate(refs) for j, chip in enumerate(chips)]


def _plan_pair_exchange(refs, send_sems, recv_sems, base):
    x, y, c, _ = _place()
    n = len(refs) // 2
    return [_rcopy(refs[a].at[:, _half(refs[a].shape[1], 1 - c)], refs[n + a], send_sems, recv_sems, base + a,
                   (x, y, 1 - c)) for a in range(n)]


def _plan_chip_exchange(refs, send_sems, recv_sems, base):
    x, y, c, chips = _place()
    n = len(refs) // 2
    return [_rcopy(refs[a].at[2 * chip[0] + chip[1]], refs[n + a].at[j], send_sems, recv_sems, base + 3 * a + j,
                   (*chip, c)) for a in range(n) for j, chip in enumerate(chips)]


def _plan_pair_share(refs, send_sems, recv_sems, base):
    x, y, c, _ = _place()
    return [_rcopy(r.at[_half(r.shape[0], c)], r.at[_half(r.shape[0], c)], send_sems, recv_sems, base + a,
                   (x, y, 1 - c)) for a, r in enumerate(refs)]


def _allreduce_small(buf):
    rows = buf.shape[0]

    def body(in_ref, out_ref, gather_ref, send_sems, recv_sems):
        x, y, c, _ = _place()
        me = 4 * x + 2 * y + c
        gather_ref[me] = in_ref[...]
        cps, slots = [], []
        for m in range(1, NDEV):
            px = 1 - x if m & 4 else x
            py = 1 - y if m & 2 else y
            pc = 1 - c if m & 1 else c
            cps.append(_rcopy(in_ref, gather_ref.at[me], send_sems, recv_sems, m - 1, (px, py, pc)))
            slots.append(4 * px + 2 * py + pc)
        for cp in cps:
            cp.start()
        for m in range(1, NDEV):
            _rcopy(in_ref, gather_ref.at[slots[m - 1]], send_sems, recv_sems, m - 1, (x, y, c)).wait_recv()
        for cp in cps:
            cp.wait_send()
        total = gather_ref[0]
        for dev in range(1, NDEV):
            total = total + gather_ref[dev]
        out_ref[...] = total

    vm = pl.BlockSpec(memory_space=pltpu.VMEM)
    return pl.pallas_call(
        body, name="allreduce_small", in_specs=[vm], out_specs=vm, out_shape=SDS(buf.shape, F32),
        scratch_shapes=[pltpu.VMEM((NDEV, rows, 128), F32), pltpu.SemaphoreType.DMA((NDEV - 1,)),
                        pltpu.SemaphoreType.DMA((NDEV - 1,))],
        compiler_params=pltpu.CompilerParams(vmem_limit_bytes=VMEM_LIMIT),
    )(buf)


BLOCK_ELEMS = 256 * 1024


def _rows_per_block(rows, cols, mult):
    best = None
    for tr in range(mult, rows + 1, mult):
        if rows % tr == 0 and tr * cols <= BLOCK_ELEMS:
            best = tr
    assert best is not None, (rows, cols)
    return best


def _pair_sum(g, r, c_idx):
    nq, rows, cols = g.shape
    half = rows // 2
    tr = _rows_per_block(half, cols, 16)
    nb = half // tr

    def body(c_ref, g_ref, r_ref, t_ref):
        t_ref[...] = (g_ref[...] + r_ref[...]).astype(BF16)

    blk = pl.BlockSpec((None, tr, cols), lambda q, i, cr: (q, i, 0))
    spec = pltpu.PrefetchScalarGridSpec(
        num_scalar_prefetch=1, grid=(nq, nb),
        in_specs=[pl.BlockSpec((None, tr, cols), lambda q, i, cr: (q, cr[0] * nb + i, 0)), blk], out_specs=blk)
    return pl.pallas_call(body, name="grad_pair_sum", grid_spec=spec, out_shape=SDS((nq, half, cols), BF16),
                          compiler_params=_cp("parallel", "parallel"))(c_idx, g, r)


def _chip_sum(g, r, rr, cp_idx):
    _, rows, cols = g.shape
    half = rows // 2
    tr = _rows_per_block(half, cols, 16)
    nb = half // tr

    def body(cp_ref, buf_ref, g_ref, r_ref, rr_ref, o_ref):
        o_ref[...] = ((g_ref[...] + r_ref[...]) + rr_ref[0].astype(F32) + rr_ref[1].astype(F32) + rr_ref[2].astype(F32))

    spec = pltpu.PrefetchScalarGridSpec(
        num_scalar_prefetch=1, grid=(nb,),
        in_specs=[ANY, pl.BlockSpec((None, tr, cols), lambda i, cp: (cp[1], cp[0] * nb + i, 0)),
                  pl.BlockSpec((None, tr, cols), lambda i, cp: (cp[1], i, 0)),
                  pl.BlockSpec((3, tr, cols), lambda i, cp: (0, i, 0))],
        out_specs=pl.BlockSpec((tr, cols), lambda i, cp: (cp[0] * nb + i, 0)))
    return pl.pallas_call(body, name="grad_chip_sum", grid_spec=spec, out_shape=SDS((rows, cols), F32),
                          input_output_aliases={1: 0}, compiler_params=_cp("parallel"),
                          )(cp_idx, lax.empty((rows, cols), F32), g, r, rr)


def _adamw_math(w, g, m, v):
    mn = ADAM_B1 * m + (1.0 - ADAM_B1) * g
    vn = ADAM_B2 * v + (1.0 - ADAM_B2) * (g * g)
    m_hat = mn / (1.0 - ADAM_B1 ** ADAM_STEP)
    v_hat = vn / (1.0 - ADAM_B2 ** ADAM_STEP)
    return -ADAM_LR * (m_hat / (jnp.sqrt(v_hat) + ADAM_EPS) + ADAM_WD * w), mn, vn


def _adamw_layers(w, gs, m, v):
    depth, rows, cols = w.shape
    tr = _rows_per_block(rows, cols, 8)

    def body(w_ref, g0_ref, g1_ref, m_ref, v_ref, go_ref, d_ref, mo_ref, vo_ref):
        gg = jnp.where(pl.program_id(0) == 0, g0_ref[...], g1_ref[...])
        go_ref[...] = gg
        d_ref[...], mo_ref[...], vo_ref[...] = _adamw_math(w_ref[...], gg, m_ref[...], v_ref[...])

    blk = pl.BlockSpec((None, tr, cols), lambda l, i: (l, i, 0))
    return pl.pallas_call(
        body, name="adamw_layers", grid=(depth, rows // tr),
        in_specs=[blk, pl.BlockSpec((tr, cols), lambda l, i: (i * (1 - l), 0)),
                  pl.BlockSpec((tr, cols), lambda l, i: (i * l, 0)), blk, blk],
        out_specs=[blk] * 4, out_shape=[SDS(w.shape, F32)] * 4,
        compiler_params=_cp("arbitrary", "arbitrary"))(w, gs[0], gs[1], m, v)


def _adamw_small(ws, gs, ms, vs):
    n = len(ws)

    def body(*refs):
        w, g, m, v, d_out, m_out, v_out = (refs[k * n:(k + 1) * n] for k in range(7))
        for k in range(n):
            d_out[k][...], m_out[k][...], v_out[k][...] = _adamw_math(w[k][...], g[k][...], m[k][...], v[k][...])

    vm = pl.BlockSpec(memory_space=pltpu.VMEM)
    outs = pl.pallas_call(body, name="adamw_small", in_specs=[vm] * (4 * n), out_specs=[vm] * (3 * n),
                          out_shape=[SDS(w.shape, F32) for w in ws] * 3,
                          compiler_params=pltpu.CompilerParams(vmem_limit_bytes=VMEM_LIMIT))(*ws, *gs, *ms, *vs)
    return outs[:n], outs[n:2 * n], outs[2 * n:]


_WEIGHTS = ["ffn1_pre_g", "ffn1_w_gu", "ffn1_w_down", "ffn1_post_g", "mix_pre_g", "w_in", "lru_conv_w", "lru_conv_b",
            "lru_w_a", "lru_b_a", "lru_w_x", "lru_b_x", "lru_lambda", "attn_sinks", "conv_w", "conv_b", "conv_ln_g",
            "conv_ln_b", "group_g", "w_out", "mix_post_g", "ffn2_pre_g", "ffn2_w_gu", "ffn2_w_down", "ffn2_post_g"]
_INPUTS = ["x"] + _WEIGHTS + ["loss_target"] + ["m_" + n for n in _WEIGHTS] + ["v_" + n for n in _WEIGHTS]
_BIG = ["ffn1_w_gu", "ffn1_w_down", "w_in", "w_out", "ffn2_w_gu", "ffn2_w_down"]
_SMALL_SHARDED = ["lru_conv_w", "conv_w"]
_SMALL_REPL = [n for n in _WEIGHTS if n not in _BIG and n not in _SMALL_SHARDED]

PACK_TILE = 8 * 128


def _pack(arrs):
    parts = []
    for a in arrs:
        flat = a.reshape(-1)
        parts.append(jnp.pad(flat, (0, -flat.shape[0] % PACK_TILE)).reshape(-1, 128))
    return jnp.concatenate(parts, axis=0)


def _unpack(buf, shapes):
    out, row = [], 0
    for shp in shapes:
        size = math.prod(shp)
        nrow = -(-size // PACK_TILE) * 8
        out.append(buf[row:row + nrow].reshape(-1)[:size].reshape(shp))
        row += nrow
    return out


def _unshard_cols(a):
    return a.transpose(0, 2, 1, 3).reshape(1, a.shape[2], NSHARD * a.shape[3])


_GROUPS = dict(ffn1=["ffn1_w_gu", "ffn1_w_down"], mix=["w_in", "w_out", "lru_conv_w", "conv_w"],
               ffn2=["ffn2_w_gu", "ffn2_w_down"])


def _full_weights(group, gathered):
    g = dict(zip(_GROUPS[group], gathered))
    if group == "mix":
        return dict(w_in=_unshard_cols(g["w_in"]), w_out=g["w_out"].reshape(1, D, D),
                    lru_conv_w=_unshard_cols(g["lru_conv_w"])[0], conv_w=_unshard_cols(g["conv_w"])[0])
    return {group + "_w_gu": g[group + "_w_gu"], group + "_w_down": g[group + "_w_down"].reshape(1, DFF, D)}


def _by_shard(name, buf):
    if name.endswith("w_gu"):
        return buf[0]
    if name == "w_in":
        return buf.reshape(D, NSHARD, P_IN // NSHARD).transpose(1, 0, 2)
    return buf.reshape(NSHARD, buf.shape[2] // NSHARD, buf.shape[3])


class _Reducer:
    PLANS = (_plan_pair_exchange, _plan_chip_exchange, _plan_pair_share)

    def __init__(self, keys, gs, c_idx, cp_idx):
        self.keys, self.gs, self.c_idx, self.cp_idx = keys, gs, c_idx, cp_idx
        self.n = len(gs)
        self.step = 0
        self.result = None

    def inputs(self):
        n = self.n
        if self.step == 0:
            bufs = self.gs + [lax.empty((NSHARD, g.shape[1] // 2, g.shape[2]), F32) for g in self.gs]
        elif self.step == 1:
            ts = [_pair_sum(g, r, self.c_idx) for g, r in zip(self.gs, self.rs)]
            bufs = ts + [lax.empty((3,) + t.shape[1:], BF16) for t in ts]
        else:
            bufs = [_chip_sum(g, r, rr, self.cp_idx) for g, r, rr in zip(self.gs, self.rs, self.rrs)]
        return bufs, (self.PLANS[self.step], len(bufs), (n, 3 * n, n)[self.step])

    def absorb(self, done):
        n = self.n
        if self.step == 0:
            self.gs, self.rs = done[:n], done[n:]
        elif self.step == 1:
            self.rrs = done[n:]
        else:
            self.result = dict(zip(self.keys, done))
        self.step += 1


class _ReducePipeline:
    def __init__(self, c_idx, cp_idx):
        self.c_idx, self.cp_idx = c_idx, cp_idx
        self.reducers, self.flying, self.calls = [], None, 0

    def add(self, layer, done):
        if done:
            keys = [(layer, n) for n in done]
            self.reducers.append(_Reducer(keys, [_by_shard(n, b) for n, b in done.items()], self.c_idx, self.cp_idx))

    def _next(self):
        active = [r for r in self.reducers if r.step < 3]
        bufs, plans = [], []
        for r in active:
            b, triple = r.inputs()
            bufs += b
            plans.append(triple)
        self.calls += 1
        return active, bufs, plans, "grad_exchange%d" % self.calls

    def _absorb(self, active, plans, done):
        at = 0
        for r, (_, nb, _) in zip(active, plans):
            r.absorb(done[at:at + nb])
            at += nb

    def _land(self, after):
        if self.flying is not None:
            active, plans, name, send_sems, recv_sems, bufs = self.flying
            self._absorb(active, plans, _exchange_wait(name + "_wait", send_sems, recv_sems, bufs, plans, after))
            self.flying = None

    def hook(self, after):
        self._land(after)
        active, bufs, plans, name = self._next()
        if not active:
            return []
        send_sems, recv_sems, bufs, token = _exchange_start(name + "_start", bufs, plans)
        self.flying = (active, plans, name, send_sems, recv_sems, bufs)
        return [token]

    def finish(self, after):
        self._land(after)
        while True:
            active, bufs, plans, name = self._next()
            if not active:
                break
            self._absorb(active, plans, _exchange(name, bufs, plans))
        out = {}
        for r in self.reducers:
            out.update(r.result)
        return out


def kernel(*args):
    d = dict(zip(_INPUTS, args, strict=True))
    xi, yi, ci = lax.axis_index("x"), lax.axis_index("y"), lax.axis_index("c")
    p = 2 * xi + yi
    c_idx = jnp.reshape(ci, (1,)).astype(jnp.int32)
    p_idx = jnp.reshape(p, (1,)).astype(jnp.int32)
    cp_idx = jnp.stack([ci, p]).astype(jnp.int32)
    x, target = d["x"][0], d["loss_target"][0]
    tiles = _tiles(x.shape[0])

    groups = [(l, grp) for l in range(DEPTH) for grp in _GROUPS]
    placed = {(l, grp): [_place_shard(d[n], l, p_idx, BF16 if n in _BIG else F32) for n in _GROUPS[grp]]
              for l, grp in groups}
    ready = {groups[0]: _gather_two_level(placed[groups[0]], len(placed[groups[0]]))}
    flying, tokens = {}, [ready[groups[0]][0]]
    for l, grp in groups[1:]:
        plans = [(_plan_gather, len(placed[l, grp]), 3 * len(placed[l, grp]))]
        send_sems, recv_sems, bufs, token = _exchange_start("gather_l%d_%s_start" % (l, grp), placed[l, grp], plans,
                                                             tokens[-1:])
        flying[l, grp] = (send_sems, recv_sems, bufs, plans)
        tokens.append(token)

    def weights_of(l):
        def weights(grp, after):
            if (l, grp) not in ready:
                send_sems, recv_sems, bufs, plans = flying[l, grp]
                ready[l, grp] = _exchange_wait("gather_l%d_%s_wait" % (l, grp), send_sems, recv_sems, bufs, plans, after)
            return _full_weights(grp, ready[l, grp])
        return weights

    small = {n: d[n] for n in _SMALL_REPL}
    x1, sv0 = _forward_layer(x, weights_of(0), _layer_params(small, 0), tiles, tokens[1:])
    x2, sv1 = _forward_layer(x1, weights_of(1), _layer_params(small, 1), tiles)
    dx, lcols = _loss_grad(x2, target, tiles[0])

    pipe = _ReducePipeline(c_idx, cp_idx)
    sgrads = [None] * DEPTH
    for l, sv in ((1, sv1), (0, sv0)):
        bufs = _grad_buffers()

        def stage(done, dx, l=l):
            pipe.add(l, done)
            return pipe.hook(dx)

        dx, sgrads[l] = _backward_layer(dx, sv, bufs, tiles, stage)
        pipe.add(l, {n: bufs[n] for n in ("ffn1_w_gu", "ffn1_w_down")})
    grad_x = dx
    reduced = pipe.finish(grad_x)

    stacked = {n: jnp.stack([sgrads[l][n].reshape(d[n].shape[1:]) for l in range(DEPTH)]) for n in _SMALL_REPL}
    for n in _SMALL_SHARDED:
        stacked[n] = jnp.stack([sgrads[l][n] for l in range(DEPTH)])
    loss_part = jnp.pad((0.5 / D) * jnp.sum(lcols).reshape(1), (0, 127))
    order = _SMALL_REPL + _SMALL_SHARDED
    summed = _unpack(_allreduce_small(_pack([loss_part] + [stacked[n] for n in order])),
                     [(128,)] + [stacked[n].shape for n in order])
    loss = summed[0][0]
    grads = {}
    for n, g in zip(order, summed[1:]):
        if n in _SMALL_SHARDED:
            g = lax.dynamic_slice_in_dim(g, p * (g.shape[2] // NSHARD), g.shape[2] // NSHARD, axis=2)
        grads[n] = g

    delta, new_m, new_v = {}, {}, {}
    for n in _BIG:
        grads[n], delta[n], new_m[n], new_v[n] = _adamw_layers(d[n], [reduced[l, n] for l in range(DEPTH)],
                                                                d["m_" + n], d["v_" + n])
    small_out = _adamw_small([d[n] for n in order], [grads[n] for n in order], [d["m_" + n] for n in order],
                             [d["v_" + n] for n in order])
    for out, res in zip((delta, new_m, new_v), small_out):
        out.update(zip(order, res))

    return (loss, grad_x[None], *[grads[n] for n in _WEIGHTS], *[delta[n] for n in _WEIGHTS],
            *[new_m[n] for n in _WEIGHTS], *[new_v[n] for n in _WEIGHTS])
```

```python
import functools
import math

import jax
import jax.numpy as jnp
from jax import lax
from jax.experimental import pallas as pl
from jax.experimental.pallas import tpu as pltpu

F32 = jnp.float32
BF16 = jnp.bfloat16
SDS = jax.ShapeDtypeStruct

D = 1024
DFF = 2816
FH = DFF // 2
DEPTH = 2
W_A = 256
W_B = 512
W_C = 256
NQ = 8
HD = 64
BLK = 128
ATT_NB_FWD = 1
ATT_NB_BWD = 4
P_IN = 1792
LRU_K = 4
CONV_K = 31
LRU_C = 8.0
NORM_EPS = 1e-6
LN_EPS = 1e-5
NEG_BIG = -1e30
SCALE = 1.0 / math.sqrt(HD)

ADAM_LR = 0.001
ADAM_B1 = 0.9
ADAM_B2 = 0.999
ADAM_EPS = 1e-08
ADAM_WD = 0.01
ADAM_STEP = 10

VMEM_LIMIT = 60 * 1024 * 1024
NSHARD = 4
NDEV = 8

TN = (((0,), (0,)), ((), ()))
NT = (((1,), (1,)), ((), ()))

MESH = pl.DeviceIdType.MESH
ANY = pl.BlockSpec(memory_space=pl.ANY)


def _cp(*sem):
    return pltpu.CompilerParams(dimension_semantics=sem if sem else None, vmem_limit_bytes=VMEM_LIMIT)


def _rsq(x, eps):
    return lax.rsqrt(jnp.mean(x * x, axis=-1, keepdims=True) + eps)


def _rms_bwd_rows(x, g, dy):
    r = _rsq(x, NORM_EPS)
    xh = x * r
    dyg = dy * g
    dx = r * (dyg - xh * jnp.mean(dyg * xh, axis=-1, keepdims=True))
    return dx, dy * xh


def _sig(x):
    return jax.nn.sigmoid(x)


def _ffn_up(x, pre_g, wgu, l, tm, deps=()):
    s = x.shape[0]
    deps = list(deps)

    def body(x_ref, g_ref, wg_ref, wu_ref, *rest):
        h_ref, go_ref, uo_ref, a_ref = rest[len(deps):]

        @pl.when(pl.program_id(1) == 0)
        def _():
            xf = x_ref[...]
            h_ref[...] = (xf * _rsq(xf, NORM_EPS) * g_ref[...]).astype(BF16)

        h = h_ref[...]
        gg = jnp.dot(h, wg_ref[...], preferred_element_type=F32)
        uu = jnp.dot(h, wu_ref[...], preferred_element_type=F32)
        sg = _sig(gg)
        silu = gg * sg
        go_ref[...] = (uu * (sg * (1.0 + gg * (1.0 - sg)))).astype(BF16)
        uo_ref[...] = silu.astype(BF16)
        a_ref[...] = (silu * uu).astype(BF16)

    wide = pl.BlockSpec((tm, FH), lambda i, j: (i, j))
    return pl.pallas_call(
        body, name="ffn_up", grid=(s // tm, 2),
        in_specs=[pl.BlockSpec((tm, D), lambda i, j: (i, 0)), pl.BlockSpec((1, D), lambda i, j: (0, 0)),
                  pl.BlockSpec((None, None, D, FH), lambda i, j: (l, j, 0, 0)),
                  pl.BlockSpec((None, None, D, FH), lambda i, j: (l, j + 2, 0, 0))] + [ANY] * len(deps),
        out_specs=[pl.BlockSpec((tm, D), lambda i, j: (i, 0)), wide, wide, wide],
        out_shape=[SDS((s, D), BF16), SDS((s, DFF), BF16), SDS((s, DFF), BF16), SDS((s, DFF), BF16)],
        compiler_params=_cp("parallel", "arbitrary"),
    )(x, pre_g, wgu, wgu, *deps)


def _mm_rms_res(a, w, l, x, g, c, tm, tk, name):
    s, k_dim = a.shape
    nk = k_dim // tk

    def body(a_ref, w_ref, x_ref, g_ref, z_ref, x1_ref):
        k = pl.program_id(1)
        p = jnp.dot(a_ref[...], w_ref[...], preferred_element_type=F32)

        @pl.when(k == 0)
        def _():
            z_ref[...] = p

        @pl.when(k > 0)
        def _():
            z_ref[...] += p

        @pl.when(k == nk - 1)
        def _():
            z = z_ref[...]
            x1_ref[...] = x_ref[...] + c * (z * _rsq(z, NORM_EPS) * g_ref[...])

    row = pl.BlockSpec((tm, D), lambda i, k: (i, 0))
    return pl.pallas_call(
        body, name=name, grid=(s // tm, nk),
        in_specs=[pl.BlockSpec((tm, tk), lambda i, k: (i, k)), pl.BlockSpec((None, tk, D), lambda i, k: (l, k, 0)),
                  row, pl.BlockSpec((1, D), lambda i, k: (0, 0))],
        out_specs=[row, row],
        out_shape=[SDS((s, D), F32), SDS((s, D), F32)],
        compiler_params=_cp("parallel", "arbitrary"),
    )(a, w, x, g)


def _rms_bwd(dy, z, g, c, tm, name, deps=()):
    s = z.shape[0]
    deps = list(deps)

    def body(dy_ref, z_ref, g_ref, *rest):
        dz_ref, dg_ref = rest[len(deps):]
        dz, dgr = _rms_bwd_rows(z_ref[...], g_ref[...], c * dy_ref[...])
        dz_ref[...] = dz.astype(BF16)
        part = jnp.sum(dgr, axis=0, keepdims=True)

        @pl.when(pl.program_id(0) == 0)
        def _():
            dg_ref[...] = part

        @pl.when(pl.program_id(0) > 0)
        def _():
            dg_ref[...] += part

    row = pl.BlockSpec((tm, D), lambda i: (i, 0))
    vec = pl.BlockSpec((1, D), lambda i: (0, 0))
    return pl.pallas_call(
        body, name=name, grid=(s // tm,), in_specs=[row, row, vec] + [ANY] * len(deps), out_specs=[row, vec],
        out_shape=[SDS((s, D), BF16), SDS((1, D), F32)], compiler_params=_cp("arbitrary"),
    )(dy, z, g, *deps)


def _ffn_bwd_mid(dz, wd, l, dadg, dadu, tm):
    s = dz.shape[0]

    def body(dz_ref, wd_ref, g_ref, u_ref, dg_ref, du_ref):
        da = lax.dot_general(dz_ref[...], wd_ref[...], NT, preferred_element_type=F32)
        dg_ref[...] = (da * g_ref[...].astype(F32)).astype(BF16)
        du_ref[...] = (da * u_ref[...].astype(F32)).astype(BF16)

    wide = pl.BlockSpec((tm, FH), lambda i, j: (i, j))
    return pl.pallas_call(
        body, name="ffn_bwd_mid", grid=(s // tm, 2),
        in_specs=[pl.BlockSpec((tm, D), lambda i, j: (i, 0)), pl.BlockSpec((None, FH, D), lambda i, j: (l, j, 0)), wide, wide],
        out_specs=[wide, wide],
        out_shape=[SDS((s, DFF), BF16), SDS((s, DFF), BF16)],
        compiler_params=_cp("parallel", "arbitrary"),
    )(dz, wd, dadg, dadu)


def _ffn_bwd_dh(dg, du, wgu, l, x, pre_g, dx1, tm):
    s = x.shape[0]

    def body(dg_ref, du_ref, wg_ref, wu_ref, x_ref, g_ref, dx1_ref, dx_ref, dgp_ref):
        i, k = pl.program_id(0), pl.program_id(1)
        p = (lax.dot_general(dg_ref[...], wg_ref[...], NT, preferred_element_type=F32)
             + lax.dot_general(du_ref[...], wu_ref[...], NT, preferred_element_type=F32))

        @pl.when(k == 0)
        def _():
            dx_ref[...] = p

        @pl.when(k == 1)
        def _():
            dx, dgr = _rms_bwd_rows(x_ref[...], g_ref[...], dx_ref[...] + p)
            dx_ref[...] = dx1_ref[...] + dx
            part = jnp.sum(dgr, axis=0, keepdims=True)

            @pl.when(i == 0)
            def _():
                dgp_ref[...] = part

            @pl.when(i > 0)
            def _():
                dgp_ref[...] += part

    wide = pl.BlockSpec((tm, FH), lambda i, k: (i, k))
    row = pl.BlockSpec((tm, D), lambda i, k: (i, 0))
    vec = pl.BlockSpec((1, D), lambda i, k: (0, 0))
    return pl.pallas_call(
        body, name="ffn_bwd_dh", grid=(s // tm, 2),
        in_specs=[wide, wide, pl.BlockSpec((None, None, D, FH), lambda i, k: (l, k, 0, 0)),
                  pl.BlockSpec((None, None, D, FH), lambda i, k: (l, k + 2, 0, 0)), row, vec, row],
        out_specs=[row, vec],
        out_shape=[SDS((s, D), F32), SDS((1, D), F32)],
        compiler_params=_cp("arbitrary", "arbitrary"),
    )(dg, du, wgu, wgu, x, pre_g, dx1)


def _mm_tn_into(buf, a, b, l, joff, tka, tn, ts, name):
    s, ka = a.shape
    n = b.shape[1]

    def body(buf_ref, a_ref, b_ref, o_ref):
        p = lax.dot_general(a_ref[...], b_ref[...], TN, preferred_element_type=F32)

        @pl.when(pl.program_id(2) == 0)
        def _():
            o_ref[...] = p

        @pl.when(pl.program_id(2) > 0)
        def _():
            o_ref[...] += p

    return pl.pallas_call(
        body, name=name, grid=(ka // tka, n // tn, s // ts),
        in_specs=[pl.BlockSpec(memory_space=pl.ANY),
                  pl.BlockSpec((ts, tka), lambda ia, j, t: (t, ia)), pl.BlockSpec((ts, tn), lambda ia, j, t: (t, j))],
        out_specs=pl.BlockSpec((None, None, tka, tn), lambda ia, j, t: (l, joff + j, ia, 0)),
        out_shape=SDS(buf.shape, F32), input_output_aliases={0: 0},
        compiler_params=_cp("parallel", "parallel", "arbitrary"),
    )(buf, a, b)


def _proj(x, g, w_in, l, tm):
    s = x.shape[0]

    def body(x_ref, g_ref, w_ref, h_ref, p_ref):
        xf = x_ref[...]
        h = (xf * _rsq(xf, NORM_EPS) * g_ref[...]).astype(BF16)
        h_ref[...] = h
        p_ref[...] = jnp.dot(h, w_ref[...], preferred_element_type=F32)

    return pl.pallas_call(
        body, name="proj", grid=(s // tm,),
        in_specs=[pl.BlockSpec((tm, D), lambda i: (i, 0)), pl.BlockSpec((1, D), lambda i: (0, 0)),
                  pl.BlockSpec((None, D, P_IN), lambda i: (l, 0, 0))],
        out_specs=[pl.BlockSpec((tm, D), lambda i: (i, 0)), pl.BlockSpec((tm, P_IN), lambda i: (i, 0))],
        out_shape=[SDS((s, D), BF16), SDS((s, P_IN), F32)],
        compiler_params=_cp("parallel"),
    )(x, g, w_in)


def _mm_nt(a, w, l, tm, name):
    s, k_dim = a.shape
    n = w.shape[1]

    def body(a_ref, w_ref, o_ref):
        o_ref[...] = lax.dot_general(a_ref[...], w_ref[...], NT, preferred_element_type=F32)

    return pl.pallas_call(
        body, name=name, grid=(s // tm,),
        in_specs=[pl.BlockSpec((tm, k_dim), lambda i: (i, 0)), pl.BlockSpec((None, n, k_dim), lambda i: (l, 0, 0))],
        out_specs=pl.BlockSpec((tm, n), lambda i: (i, 0)),
        out_shape=SDS((s, n), F32), compiler_params=_cp("parallel"),
    )(a, w)


def _mm_nt_rmsbwd(dp, w_in, l, x, g, dx1, tm):
    s = x.shape[0]

    def body(dp_ref, w_ref, x_ref, g_ref, dx1_ref, dx_ref, dg_ref):
        dh = lax.dot_general(dp_ref[...], w_ref[...], NT, preferred_element_type=F32)
        dx, dgr = _rms_bwd_rows(x_ref[...], g_ref[...], dh)
        dx_ref[...] = dx1_ref[...] + dx
        part = jnp.sum(dgr, axis=0, keepdims=True)

        @pl.when(pl.program_id(0) == 0)
        def _():
            dg_ref[...] = part

        @pl.when(pl.program_id(0) > 0)
        def _():
            dg_ref[...] += part

    row = pl.BlockSpec((tm, D), lambda i: (i, 0))
    vec = pl.BlockSpec((1, D), lambda i: (0, 0))
    return pl.pallas_call(
        body, name="mix_bwd_dx", grid=(s // tm,),
        in_specs=[pl.BlockSpec((tm, P_IN), lambda i: (i, 0)), pl.BlockSpec((None, D, P_IN), lambda i: (l, 0, 0)), row, vec, row],
        out_specs=[row, vec], out_shape=[SDS((s, D), F32), SDS((1, D), F32)],
        compiler_params=_cp("arbitrary"),
    )(dp, w_in, x, g, dx1)


def _row_iota(shape):
    return lax.broadcasted_iota(jnp.int32, shape, 0)


def _lru_gates(xc, wa_ref, ba_ref, wx_ref, bx_ref, lam_ref):
    xb = xc.astype(BF16)
    r = _sig(jnp.dot(xb, wa_ref[...], preferred_element_type=F32) + ba_ref[...])
    ig = _sig(jnp.dot(xb, wx_ref[...], preferred_element_type=F32) + bx_ref[...])
    nl = -lam_ref[...]
    sp = jnp.maximum(nl, 0.0) + jnp.log(1.0 + jnp.exp(-jnp.abs(nl)))
    log_a = -LRU_C * r * sp
    a = jnp.exp(log_a)
    x2 = 2.0 * log_a
    series = x2 * (1.0 + x2 * (0.5 + x2 * (1.0 / 6.0 + x2 * (1.0 / 24.0 + x2 * (1.0 / 120.0)))))
    em1 = jnp.where(x2 > -0.05, series, jnp.exp(x2) - 1.0)
    mlt = jnp.sqrt(-em1)
    return r, ig, a, mlt, sp


def _conv_taps(src_ref, w_ref, k_taps, pad, tc):
    acc = None
    for j in range(k_taps):
        term = w_ref[j:j + 1, :] * src_ref[pl.ds(pad - (k_taps - 1) + j, tc), :]
        acc = term if acc is None else acc + term
    return acc


def _gelu_parts(x):
    c0 = math.sqrt(2.0 / math.pi)
    inner = c0 * (x + 0.044715 * x * x * x)
    t = jnp.tanh(inner)
    gl = 0.5 * x * (1.0 + t)
    dgl = 0.5 * (1.0 + t) + 0.5 * x * (1.0 - t * t) * c0 * (1.0 + 3.0 * 0.044715 * x * x)
    return gl, dgl


def _lru_fwd(proj, cw, cb, wa, ba, wx, bx, lam, gg, tc):
    s = proj.shape[0]
    pad = 8

    def body(xcur_ref, xprev_ref, gate_ref, cw_ref, cb_ref, wa_ref, ba_ref, wx_ref, bx_ref, lam_ref, gg_ref,
             yn_ref, h_ref, xs_ref, hc_ref):
        i = pl.program_id(0)

        @pl.when(i == 0)
        def _():
            hc_ref[...] = jnp.zeros_like(hc_ref)

        xs_ref[0:pad, :] = jnp.where(i > 0, xprev_ref[tc - pad:tc, :], 0.0)
        xs_ref[pad:pad + tc, :] = xcur_ref[...]
        xc = _conv_taps(xs_ref, cw_ref, LRU_K, pad, tc) + cb_ref[...]
        _, ig, a, mlt, _ = _lru_gates(xc, wa_ref, ba_ref, wx_ref, bx_ref, lam_ref)
        u = mlt * (ig * xc)
        row = _row_iota((tc, W_A))
        d = 1
        while d < tc:
            ok = row >= d
            a_sh = jnp.where(ok, pltpu.roll(a, d, axis=0), 1.0)
            u_sh = jnp.where(ok, pltpu.roll(u, d, axis=0), 0.0)
            u = a * u_sh + u
            a = a * a_sh
            d *= 2
        h = u + a * hc_ref[...]
        hc_ref[...] = jnp.sum(jnp.where(row == tc - 1, h, 0.0), axis=0, keepdims=True)
        h_ref[...] = h
        gl, _ = _gelu_parts(gate_ref[...])
        ya = gl * h
        yn_ref[...] = (ya * _rsq(ya, NORM_EPS) * gg_ref[...]).astype(BF16)

    blk = lambda c: pl.BlockSpec((tc, W_A), lambda i, c=c: (i, c))
    full = lambda a: pl.BlockSpec(a.shape, lambda i: (0,) * a.ndim)
    params = [cw, cb, wa, ba, wx, bx, lam, gg]
    return pl.pallas_call(
        body, name="lru_fwd", grid=(s // tc,),
        in_specs=[blk(0), pl.BlockSpec((tc, W_A), lambda i: (jnp.maximum(i - 1, 0), 0)), blk(1)] + [full(a) for a in params],
        out_specs=[pl.BlockSpec((tc, W_A), lambda i: (i, 0))] * 2,
        out_shape=[SDS((s, W_A), BF16), SDS((s, W_A), F32)],
        scratch_shapes=[pltpu.VMEM((tc + pad, W_A), F32), pltpu.VMEM((1, W_A), F32)],
        compiler_params=_cp("arbitrary"),
    )(proj, proj, proj, *params)


def _acc(ref, first, val):
    @pl.when(first)
    def _():
        ref[...] = val

    @pl.when(jnp.logical_not(first))
    def _():
        ref[...] += val


def _lru_bwd(dy, proj, h, cw, cb, wa, ba, wx, bx, lam, gg, tc):
    s = proj.shape[0]
    nc = s // tc
    pad = 8

    def body(dy_ref, xcur_ref, xprev_ref, gate_ref, h_ref, hprev_ref, cw_ref, cb_ref, wa_ref, ba_ref, wx_ref, bx_ref,
             lam_ref, gg_ref,
             dp_ref, dcw_ref, dcb_ref, dwa_ref, dba_ref, dwx_ref, dbx_ref, dlam_ref, dgg_ref,
             xs_ref, ds_ref, mu_ref, nx_ref):
        step = pl.program_id(0)
        i = nc - 1 - step
        first = step == 0

        @pl.when(first)
        def _():
            mu_ref[...] = jnp.zeros_like(mu_ref)
            nx_ref[...] = jnp.zeros_like(nx_ref)

        xs_ref[0:pad, :] = jnp.where(i > 0, xprev_ref[tc - pad:tc, :], 0.0)
        xs_ref[pad:pad + tc, :] = xcur_ref[...]
        xc = _conv_taps(xs_ref, cw_ref, LRU_K, pad, tc) + cb_ref[...]
        r, ig, a, mlt, sp = _lru_gates(xc, wa_ref, ba_ref, wx_ref, bx_ref, lam_ref)
        hh = h_ref[...]
        gate = gate_ref[...]
        gl, dgl = _gelu_parts(gate)
        ya = gl * hh
        dya, dggr = _rms_bwd_rows(ya, gg_ref[...], dy_ref[...])
        _acc(dgg_ref, first, jnp.sum(dggr, axis=0, keepdims=True))
        dp_ref[:, W_A:2 * W_A] = dya * hh * dgl
        dh = dya * gl

        row = _row_iota((tc, W_A))
        aa = a
        uu = a * dh
        d = 1
        while d < tc:
            ok = row < tc - d
            a_sh = jnp.where(ok, pltpu.roll(aa, tc - d, axis=0), 1.0)
            u_sh = jnp.where(ok, pltpu.roll(uu, tc - d, axis=0), 0.0)
            uu = uu + aa * u_sh
            aa = aa * a_sh
            d *= 2
        cin = mu_ref[...]
        mu = uu + aa * cin
        lam_t = dh + jnp.where(row == tc - 1, cin, pltpu.roll(mu, tc - 1, axis=0))
        mu_ref[...] = jnp.sum(jnp.where(row == 0, mu, 0.0), axis=0, keepdims=True)
        hm1 = jnp.where(row == 0, jnp.where(i > 0, pltpu.roll(hprev_ref[...], 1, axis=0), 0.0),
                        pltpu.roll(hh, 1, axis=0))
        da = lam_t * hm1
        du = lam_t
        dmlt = du * ig * xc
        dig = du * mlt * xc
        dxc = du * mlt * ig
        dlog_a = da * a - dmlt * (a * a / mlt)
        dr = dlog_a * (-LRU_C * sp)
        dsp = jnp.sum(dlog_a * (-LRU_C * r), axis=0, keepdims=True)
        _acc(dlam_ref, first, dsp * (-_sig(-lam_ref[...])))
        dga = dr * r * (1.0 - r)
        dgx = dig * ig * (1.0 - ig)
        _acc(dba_ref, first, jnp.sum(dga, axis=0, keepdims=True))
        _acc(dbx_ref, first, jnp.sum(dgx, axis=0, keepdims=True))
        xb = xc.astype(BF16)
        dgab = dga.astype(BF16)
        dgxb = dgx.astype(BF16)
        _acc(dwa_ref, first, lax.dot_general(xb, dgab, TN, preferred_element_type=F32))
        _acc(dwx_ref, first, lax.dot_general(xb, dgxb, TN, preferred_element_type=F32))
        dxc = (dxc + lax.dot_general(dgab, wa_ref[...], NT, preferred_element_type=F32)
               + lax.dot_general(dgxb, wx_ref[...], NT, preferred_element_type=F32))

        _acc(dcb_ref, first, jnp.sum(dxc, axis=0, keepdims=True))
        r8 = _row_iota((8, W_A))
        dcw = jnp.zeros((8, W_A), F32)
        for j in range(LRU_K):
            tap = jnp.sum(dxc * xs_ref[pl.ds(pad - (LRU_K - 1) + j, tc), :], axis=0, keepdims=True)
            dcw = dcw + jnp.where(r8 == j, tap, 0.0)
        _acc(dcw_ref, first, dcw)
        ds_ref[0:tc, :] = dxc
        ds_ref[tc:tc + pad, :] = nx_ref[...]
        dlx = None
        for j in range(LRU_K):
            term = cw_ref[j:j + 1, :] * ds_ref[pl.ds(LRU_K - 1 - j, tc), :]
            dlx = term if dlx is None else dlx + term
        dp_ref[:, 0:W_A] = dlx
        nx_ref[...] = dxc[0:pad, :]

    rev = lambda c: pl.BlockSpec((tc, W_A), lambda t, c=c: (nc - 1 - t, c))
    prev = lambda c: pl.BlockSpec((tc, W_A), lambda t, c=c: (jnp.maximum(nc - 2 - t, 0), c))
    full = lambda a: pl.BlockSpec(a.shape, lambda t: (0,) * a.ndim)
    params = [cw, cb, wa, ba, wx, bx, lam, gg]
    vec = SDS((1, W_A), F32)
    sq = SDS((W_A, W_A), F32)
    outs = [SDS((s, 2 * W_A), F32), SDS((8, W_A), F32), vec, sq, vec, sq, vec, vec, vec]
    return pl.pallas_call(
        body, name="lru_bwd", grid=(nc,),
        in_specs=[rev(0), rev(0), prev(0), rev(1), rev(0), prev(0)] + [full(a) for a in params],
        out_specs=[pl.BlockSpec((tc, 2 * W_A), lambda t: (nc - 1 - t, 0))]
        + [pl.BlockSpec(o.shape, lambda t: (0, 0)) for o in outs[1:]],
        out_shape=outs,
        scratch_shapes=[pltpu.VMEM((tc + pad, W_A), F32), pltpu.VMEM((tc + pad, W_A), F32),
                        pltpu.VMEM((1, W_A), F32), pltpu.VMEM((pad, W_A), F32)],
        compiler_params=_cp("arbitrary"),
    )(dy, proj, proj, proj, h, h, *params)


def _attn_stack(qa, qb, kvh):
    lane = lax.broadcasted_iota(jnp.int32, qa.shape, 1)
    keep = (lane >= HD) if kvh == 1 else (lane < HD)
    parts = []
    for tile in (qa, qb):
        for half in (0, 1):
            y = tile if half == kvh else pltpu.roll(tile, HD, axis=1)
            parts.append(jnp.where(keep, y, 0.0))
    return jnp.concatenate(parts, axis=0)


def _attn_unstack(o, kvh):
    lane = lax.broadcasted_iota(jnp.int32, (BLK, 2 * HD), 1)
    tiles = []
    for t in range(2):
        halves = []
        for half in (0, 1):
            blk = o[(2 * t + half) * BLK:(2 * t + half + 1) * BLK, :]
            halves.append(blk if half == kvh else pltpu.roll(blk, HD, axis=1))
        tiles.append(jnp.where(lane < HD, halves[0], halves[1]))
    return tiles


def _attn_stack_all(x_ref_or_val):
    return jnp.concatenate([_attn_stack(x_ref_or_val[:, 256 * kvh:256 * kvh + 128],
                                        x_ref_or_val[:, 256 * kvh + 128:256 * kvh + 256], kvh) for kvh in range(2)], axis=0)


def _attn_unstack_all(o, dst_ref):
    for kvh in range(2):
        ta, tb = _attn_unstack(o[4 * BLK * kvh:4 * BLK * (kvh + 1), :], kvh)
        dst_ref[:, 256 * kvh:256 * kvh + 128] = ta
        dst_ref[:, 256 * kvh + 128:256 * kvh + 256] = tb


def _attn_windows(cur_ref, prev_ref, nb):
    blocks = [prev_ref[...]] + [cur_ref[b * BLK:(b + 1) * BLK, :] for b in range(nb)]
    return [jnp.concatenate(blocks[b:b + 2], axis=0).astype(BF16) for b in range(nb)]


def _attn_probs(qs, kw, n, sink_ref):
    rows = NQ * BLK
    sc = lax.dot_general(qs.astype(BF16), kw, NT, preferred_element_type=F32) * SCALE
    qi = lax.broadcasted_iota(jnp.int32, (rows, 2 * BLK), 0) & (BLK - 1)
    kj = lax.broadcasted_iota(jnp.int32, (rows, 2 * BLK), 1)
    rel = BLK + qi - kj
    mask = (rel >= 0) & (rel < BLK) & ((n - 1) * BLK + kj >= 0)
    head = lax.broadcasted_iota(jnp.int32, (rows, 1), 0) // BLK
    sk = jnp.zeros((rows, 1), F32)
    for h in range(NQ):
        sk = jnp.where(head == h, sink_ref[h:h + 1, 0:1], sk)
    sh = jnp.where(mask, sc, NEG_BIG)
    m = jnp.maximum(jnp.max(sh, axis=-1, keepdims=True), sk)
    e = jnp.exp(sh - m)
    es = jnp.exp(sk - m)
    rz = 1.0 / (jnp.sum(e, axis=-1, keepdims=True) + es)
    return e * rz, es * rz


def _attn_fwd(proj, sinks8, gg):
    s = proj.shape[0]
    nb = ATT_NB_FWD

    def body(q_ref, kc_ref, kp_ref, vc_ref, vp_ref, sink_ref, gg_ref, yn_ref, ob_ref):
        kws, vws = _attn_windows(kc_ref, kp_ref, nb), _attn_windows(vc_ref, vp_ref, nb)
        for b in range(nb):
            rows = pl.ds(b * BLK, BLK)
            p, _ = _attn_probs(_attn_stack_all(q_ref.at[rows, :]), kws[b], nb * pl.program_id(0) + b, sink_ref)
            _attn_unstack_all(jnp.dot(p.astype(BF16), vws[b], preferred_element_type=F32), ob_ref.at[rows, :])
        ob = ob_ref[...]
        yn_ref[...] = (ob * _rsq(ob, NORM_EPS) * gg_ref[...]).astype(BF16)

    tb = nb * BLK
    cur = lambda c: pl.BlockSpec((tb, 128), lambda m, c=c: (m, c))
    prev = lambda c: pl.BlockSpec((BLK, 128), lambda m, c=c: (jnp.maximum(nb * m - 1, 0), c))
    out = pl.BlockSpec((tb, W_B), lambda m: (m, 0))
    return pl.pallas_call(
        body, name="attn_fwd", grid=(s // tb,),
        in_specs=[pl.BlockSpec((tb, W_B), lambda m: (m, 1)), cur(8), prev(8), cur(9), prev(9),
                  pl.BlockSpec((8, 128), lambda n: (0, 0)), pl.BlockSpec((1, W_B), lambda n: (0, 0))],
        out_specs=[out, out], out_shape=[SDS((s, W_B), BF16), SDS((s, W_B), F32)],
        compiler_params=_cp("parallel"),
    )(proj, proj, proj, proj, proj, sinks8, gg)


def _attn_bwd(dy, proj, ob, sinks8, gg):
    s = proj.shape[0]
    nb = ATT_NB_BWD

    def body(dya_ref, dyb_ref, q_ref, kc_ref, kp_ref, vc_ref, vp_ref, ob_ref, sink_ref, gg_ref,
             dq_ref, dcur_ref, dprev_ref, dsink_ref, dgg_ref):
        first = pl.program_id(0) == 0
        kws, vws = _attn_windows(kc_ref, kp_ref, nb), _attn_windows(vc_ref, vp_ref, nb)
        dyn = jnp.concatenate([dya_ref[...], dyb_ref[...]], axis=1)
        dob, dggr = _rms_bwd_rows(ob_ref[...], gg_ref[...], dyn)
        _acc(dgg_ref, first, jnp.sum(dggr, axis=0, keepdims=True))
        r8 = _row_iota((8, 128))
        dsk = jnp.zeros((8, 128), F32)
        for b in range(nb):
            rows = pl.ds(b * BLK, BLK)
            qs = _attn_stack_all(q_ref.at[rows, :])
            p, psink = _attn_probs(qs, kws[b], nb * pl.program_id(0) + b, sink_ref)
            dosb = _attn_stack_all(dob[b * BLK:(b + 1) * BLK, :]).astype(BF16)
            dp = lax.dot_general(dosb, vws[b], NT, preferred_element_type=F32)
            dd = jnp.sum(p * dp, axis=-1, keepdims=True)
            dsb = (p * (dp - dd) * SCALE).astype(BF16)
            dsink_rows = -psink * dd
            for h in range(NQ):
                dsk = dsk + jnp.where(r8 == h, jnp.sum(dsink_rows[h * BLK:(h + 1) * BLK, :], axis=0, keepdims=True), 0.0)
            _attn_unstack_all(jnp.dot(dsb, kws[b], preferred_element_type=F32), dq_ref.at[rows, :])
            dkw = lax.dot_general(dsb, qs.astype(BF16), TN, preferred_element_type=F32)
            dvw = lax.dot_general(p.astype(BF16), dosb, TN, preferred_element_type=F32)
            dprev_ref[rows, 0:128] = dkw[0:BLK, :]
            dprev_ref[rows, 128:256] = dvw[0:BLK, :]
            dcur_ref[rows, 0:128] = dkw[BLK:2 * BLK, :]
            dcur_ref[rows, 128:256] = dvw[BLK:2 * BLK, :]
        _acc(dsink_ref, first, dsk)

    tb = nb * BLK
    cur = lambda c: pl.BlockSpec((tb, 128), lambda m, c=c: (m, c))
    prev = lambda c: pl.BlockSpec((BLK, 128), lambda m, c=c: (jnp.maximum(nb * m - 1, 0), c))
    wide = pl.BlockSpec((tb, W_B), lambda m: (m, 0))
    half = pl.BlockSpec((tb, 256), lambda m: (m, 0))
    return pl.pallas_call(
        body, name="attn_bwd", grid=(s // tb,),
        in_specs=[pl.BlockSpec((tb, 256), lambda m: (m, 1)), pl.BlockSpec((tb, 256), lambda m: (m, 2)),
                  pl.BlockSpec((tb, W_B), lambda m: (m, 1)), cur(8), prev(8), cur(9), prev(9), wide,
                  pl.BlockSpec((8, 128), lambda n: (0, 0)), pl.BlockSpec((1, W_B), lambda n: (0, 0))],
        out_specs=[wide, half, half, pl.BlockSpec((8, 128), lambda n: (0, 0)), pl.BlockSpec((1, W_B), lambda n: (0, 0))],
        out_shape=[SDS((s, W_B), F32), SDS((s, 256), F32), SDS((s, 256), F32), SDS((8, 128), F32), SDS((1, W_B), F32)],
        compiler_params=_cp("arbitrary"),
    )(dy, dy, proj, proj, proj, proj, proj, ob, sinks8, gg)


def _ln_parts(y1, eps=LN_EPS):
    mu = jnp.mean(y1, axis=-1, keepdims=True)
    xc = y1 - mu
    rstd = lax.rsqrt(jnp.mean(xc * xc, axis=-1, keepdims=True) + eps)
    return xc * rstd, rstd


def _conf_fwd(proj, cw, cb, lg, lb, gg, tc):
    s = proj.shape[0]
    pad = 32

    def body(ac_ref, gc_ref, ap_ref, gp_ref, cw_ref, cb_ref, lg_ref, lb_ref, gg_ref, yn_ref, y1_ref, ys_ref):
        i = pl.program_id(0)
        tail = ap_ref[tc - pad:tc, :] * _sig(gp_ref[tc - pad:tc, :])
        ys_ref[0:pad, :] = jnp.where(i > 0, tail, 0.0)
        ys_ref[pad:pad + tc, :] = ac_ref[...] * _sig(gc_ref[...])
        y1 = _conv_taps(ys_ref, cw_ref, CONV_K, pad, tc) + cb_ref[...]
        y1_ref[...] = y1
        xh, _ = _ln_parts(y1)
        yl = xh * lg_ref[...] + lb_ref[...]
        yc = yl * _sig(yl)
        yn_ref[...] = (yc * _rsq(yc, NORM_EPS) * gg_ref[...]).astype(BF16)

    cur = lambda c: pl.BlockSpec((tc, W_C), lambda i, c=c: (i, c))
    prev = lambda c: pl.BlockSpec((tc, W_C), lambda i, c=c: (jnp.maximum(i - 1, 0), c))
    full = lambda a: pl.BlockSpec(a.shape, lambda i: (0,) * a.ndim)
    params = [cw, cb, lg, lb, gg]
    out = pl.BlockSpec((tc, W_C), lambda i: (i, 0))
    return pl.pallas_call(
        body, name="conf_fwd", grid=(s // tc,),
        in_specs=[cur(5), cur(6), prev(5), prev(6)] + [full(a) for a in params],
        out_specs=[out, out], out_shape=[SDS((s, W_C), BF16), SDS((s, W_C), F32)],
        scratch_shapes=[pltpu.VMEM((tc + pad, W_C), F32)],
        compiler_params=_cp("parallel"),
    )(proj, proj, proj, proj, *params)


def _conf_bwd(dy, proj, y1, cw, cb, lg, lb, gg, tc):
    s = proj.shape[0]
    nc = s // tc
    pad = 32

    def body(dy_ref, ac_ref, gc_ref, ap_ref, gp_ref, y1_ref, cw_ref, cb_ref, lg_ref, lb_ref, gg_ref,
             dp_ref, dcw_ref, dcb_ref, dlg_ref, dlb_ref, dgg_ref, ys_ref, ds_ref, nx_ref):
        step = pl.program_id(0)
        i = nc - 1 - step
        first = step == 0

        @pl.when(first)
        def _():
            nx_ref[...] = jnp.zeros_like(nx_ref)

        a = ac_ref[...]
        sg = _sig(gc_ref[...])
        tail = ap_ref[tc - pad:tc, :] * _sig(gp_ref[tc - pad:tc, :])
        ys_ref[0:pad, :] = jnp.where(i > 0, tail, 0.0)
        ys_ref[pad:pad + tc, :] = a * sg
        xh, rstd = _ln_parts(y1_ref[...])
        yl = xh * lg_ref[...] + lb_ref[...]
        sl = _sig(yl)
        yc = yl * sl
        dyc, dggr = _rms_bwd_rows(yc, gg_ref[...], dy_ref[...])
        _acc(dgg_ref, first, jnp.sum(dggr, axis=0, keepdims=True))
        dyl = dyc * sl * (1.0 + yl * (1.0 - sl))
        _acc(dlg_ref, first, jnp.sum(dyl * xh, axis=0, keepdims=True))
        _acc(dlb_ref, first, jnp.sum(dyl, axis=0, keepdims=True))
        dxh = dyl * lg_ref[...]
        dy1 = rstd * (dxh - jnp.mean(dxh, axis=-1, keepdims=True) - xh * jnp.mean(dxh * xh, axis=-1, keepdims=True))
        _acc(dcb_ref, first, jnp.sum(dy1, axis=0, keepdims=True))
        r32 = _row_iota((32, W_C))
        dcw = jnp.zeros((32, W_C), F32)
        for j in range(CONV_K):
            tap = jnp.sum(dy1 * ys_ref[pl.ds(pad - (CONV_K - 1) + j, tc), :], axis=0, keepdims=True)
            dcw = dcw + jnp.where(r32 == j, tap, 0.0)
        _acc(dcw_ref, first, dcw)
        ds_ref[0:tc, :] = dy1
        ds_ref[tc:tc + pad, :] = nx_ref[...]
        dy0 = None
        for j in range(CONV_K):
            term = cw_ref[j:j + 1, :] * ds_ref[pl.ds(CONV_K - 1 - j, tc), :]
            dy0 = term if dy0 is None else dy0 + term
        dp_ref[:, 0:W_C] = dy0 * sg
        dp_ref[:, W_C:2 * W_C] = dy0 * a * sg * (1.0 - sg)
        nx_ref[...] = dy1[0:pad, :]

    rev = lambda c: pl.BlockSpec((tc, W_C), lambda t, c=c: (nc - 1 - t, c))
    prev = lambda c: pl.BlockSpec((tc, W_C), lambda t, c=c: (jnp.maximum(nc - 2 - t, 0), c))
    full = lambda a: pl.BlockSpec(a.shape, lambda t: (0,) * a.ndim)
    params = [cw, cb, lg, lb, gg]
    vec = SDS((1, W_C), F32)
    outs = [SDS((s, 2 * W_C), F32), SDS((32, W_C), F32), vec, vec, vec, vec]
    return pl.pallas_call(
        body, name="conf_bwd", grid=(nc,),
        in_specs=[rev(3), rev(5), rev(6), prev(5), prev(6), rev(0)] + [full(a) for a in params],
        out_specs=[pl.BlockSpec((tc, 2 * W_C), lambda t: (nc - 1 - t, 0))]
        + [pl.BlockSpec(o.shape, lambda t: (0, 0)) for o in outs[1:]],
        out_shape=outs,
        scratch_shapes=[pltpu.VMEM((tc + pad, W_C), F32), pltpu.VMEM((tc + pad, W_C), F32), pltpu.VMEM((pad, W_C), F32)],
        compiler_params=_cp("arbitrary"),
    )(dy, proj, proj, proj, proj, y1, *params)


def _assemble_dproj(dlru, dq, dcur, dprev, dconf):
    s = dq.shape[0]
    nb = s // BLK

    def body(dl_ref, dq_ref, dc_ref, dn_ref, df_ref, o_ref):
        n = pl.program_id(0)
        o_ref[:, 0:512] = dl_ref[...].astype(BF16)
        o_ref[:, 512:1024] = dq_ref[...].astype(BF16)
        o_ref[:, 1024:1280] = (dc_ref[...] + jnp.where(n < nb - 1, dn_ref[...], 0.0)).astype(BF16)
        o_ref[:, 1280:1792] = df_ref[...].astype(BF16)

    wide = pl.BlockSpec((BLK, 512), lambda n: (n, 0))
    return pl.pallas_call(
        body, name="assemble_dproj", grid=(nb,),
        in_specs=[wide, wide, pl.BlockSpec((BLK, 256), lambda n: (n, 0)),
                  pl.BlockSpec((BLK, 256), lambda n: (jnp.minimum(n + 1, nb - 1), 0)), wide],
        out_specs=pl.BlockSpec((BLK, P_IN), lambda n: (n, 0)), out_shape=SDS((s, P_IN), BF16),
        compiler_params=_cp("parallel"),
    )(dlru, dq, dcur, dprev, dconf)


def _loss_grad(y, t, tm):
    s = y.shape[0]

    def body(y_ref, t_ref, dy_ref, l_ref):
        err = y_ref[...] - t_ref[...]
        dy_ref[...] = err * (1.0 / D)
        _acc(l_ref, pl.program_id(0) == 0, jnp.sum(err * err, axis=0, keepdims=True))

    row = pl.BlockSpec((tm, D), lambda i: (i, 0))
    return pl.pallas_call(
        body, name="loss_grad", grid=(s // tm,), in_specs=[row, row],
        out_specs=[row, pl.BlockSpec((1, D), lambda i: (0, 0))],
        out_shape=[SDS((s, D), F32), SDS((1, D), F32)], compiler_params=_cp("arbitrary"),
    )(y, t)


def _block_diag(w):
    rows = [jnp.concatenate([w[h] if k == h else jnp.zeros((64, 64), w.dtype) for k in range(4)], axis=1) for h in range(4)]
    return jnp.concatenate(rows, axis=0)


def _diag_blocks(m):
    return jnp.stack([m[64 * h:64 * (h + 1), 64 * h:64 * (h + 1)] for h in range(4)])


def _layer_params(small, l):
    v = lambda name: small[name][l].reshape(1, -1)
    gg = small["group_g"][l]
    return dict(
        ffn1_pre=v("ffn1_pre_g"), ffn1_post=v("ffn1_post_g"), mix_pre=v("mix_pre_g"), mix_post=v("mix_post_g"),
        ffn2_pre=v("ffn2_pre_g"), ffn2_post=v("ffn2_post_g"), lru_cb=v("lru_conv_b"),
        wa=_block_diag(small["lru_w_a"][l]).astype(BF16), ba=v("lru_b_a"),
        wx=_block_diag(small["lru_w_x"][l]).astype(BF16), bx=v("lru_b_x"), lam=v("lru_lambda"),
        sinks8=jnp.broadcast_to(small["attn_sinks"][l][:, None], (NQ, 128)),
        conv_b=v("conv_b"), ln_g=v("conv_ln_g"), ln_b=v("conv_ln_b"),
        gg_a=gg[0:W_A].reshape(1, -1), gg_b=gg[W_A:W_A + W_B].reshape(1, -1), gg_c=gg[W_A + W_B:].reshape(1, -1),
    )


def _forward_layer(x, weights, p, tiles, deps=()):
    _, mm, _, tc = tiles
    big = dict(weights("ffn1", x))
    p = dict(p)
    sv = dict(x0=x)
    h1, g1, u1, a1 = _ffn_up(x, p["ffn1_pre"], big["ffn1_w_gu"], 0, mm, deps)
    z1, x = _mm_rms_res(a1, big["ffn1_w_down"], 0, x, p["ffn1_post"], 0.5, mm, FH, "ffn_down")
    sv.update(h1=h1, g1=g1, u1=u1, a1=a1, z1=z1, x1=x)
    big.update(weights("mix", x))
    p.update(lru_cw=big.pop("lru_conv_w"), conv_w=big.pop("conv_w"))
    hn, proj = _proj(x, p["mix_pre"], big["w_in"], 0, mm)
    yn_a, hl = _lru_fwd(proj, p["lru_cw"], p["lru_cb"], p["wa"], p["ba"], p["wx"], p["bx"], p["lam"], p["gg_a"], tc)
    yn_b, ob = _attn_fwd(proj, p["sinks8"], p["gg_b"])
    yn_c, y1 = _conf_fwd(proj, p["conv_w"], p["conv_b"], p["ln_g"], p["ln_b"], p["gg_c"], tc)
    ycat = jnp.concatenate([yn_a, yn_b, yn_c], axis=1)
    zo, x = _mm_rms_res(ycat, big["w_out"], 0, x, p["mix_post"], 1.0, mm, D, "mix_out")
    sv.update(hn=hn, proj=proj, hl=hl, ob=ob, y1=y1, ycat=ycat, zo=zo, x2=x)
    big.update(weights("ffn2", x))
    h2, g2, u2, a2 = _ffn_up(x, p["ffn2_pre"], big["ffn2_w_gu"], 0, mm)
    z2, x = _mm_rms_res(a2, big["ffn2_w_down"], 0, x, p["ffn2_post"], 0.5, mm, FH, "ffn_down")
    sv.update(h2=h2, g2=g2, u2=u2, a2=a2, z2=z2, p=p, big=big)
    return x, sv


def _grad_buffers():
    empty = lambda *shape: lax.empty(shape, F32)
    return dict(ffn1_w_gu=empty(1, NSHARD, D, FH), ffn2_w_gu=empty(1, NSHARD, D, FH), ffn1_w_down=empty(1, 1, DFF, D),
                ffn2_w_down=empty(1, 1, DFF, D), w_in=empty(1, 1, D, P_IN), w_out=empty(1, 1, D, D))


def _backward_layer(dx, sv, bufs, tiles, stage):
    p, big = sv["p"], sv["big"]
    tm, mm, dw, tc = tiles
    gr = {}

    def ffn_bwd(dx, which, xin, h, g, u, a, z, pre, post, deps):
        dz, dpost = _rms_bwd(dx, z, post, 0.5, tm, "ffn_post_bwd", deps)
        dg, du = _ffn_bwd_mid(dz, big[which + "_w_down"], 0, g, u, mm)
        bufs[which + "_w_down"] = _mm_tn_into(bufs[which + "_w_down"], a, dz, 0, 0, FH, D, dw, "dw_down")
        bufs[which + "_w_gu"] = _mm_tn_into(bufs[which + "_w_gu"], h, dg, 0, 0, D, FH, dw, "dw_gate")
        bufs[which + "_w_gu"] = _mm_tn_into(bufs[which + "_w_gu"], h, du, 0, 2, D, FH, dw, "dw_up")
        dxn, dpre = _ffn_bwd_dh(dg, du, big[which + "_w_gu"], 0, xin, pre, dx, mm)
        return dxn, dpre, dpost

    dx, gr["ffn2_pre_g"], gr["ffn2_post_g"] = ffn_bwd(dx, "ffn2", sv["x2"], sv["h2"], sv["g2"], sv["u2"], sv["a2"],
                                                      sv["z2"], p["ffn2_pre"], p["ffn2_post"], stage({}, dx))
    done = {n: bufs[n] for n in ("ffn2_w_gu", "ffn2_w_down")}
    do, gr["mix_post_g"] = _rms_bwd(dx, sv["zo"], p["mix_post"], 1.0, tm, "mix_post_bwd", stage(done, dx))
    bufs["w_out"] = _mm_tn_into(bufs["w_out"], sv["ycat"], do, 0, 0, D, D, dw, "dw_out")
    dy = _mm_nt(do, big["w_out"], 0, mm, "mix_dy")
    proj = sv["proj"]
    (dlru, dcw, gr["lru_conv_b"], dwa, gr["lru_b_a"], dwx, gr["lru_b_x"], gr["lru_lambda"], dgg_a) = _lru_bwd(
        dy, proj, sv["hl"], p["lru_cw"], p["lru_cb"], p["wa"], p["ba"], p["wx"], p["bx"], p["lam"], p["gg_a"], tc)
    dq, dcur, dprev, dsk, dgg_b = _attn_bwd(dy, proj, sv["ob"], p["sinks8"], p["gg_b"])
    dconf, dconvw, gr["conv_b"], gr["conv_ln_g"], gr["conv_ln_b"], dgg_c = _conf_bwd(
        dy, proj, sv["y1"], p["conv_w"], p["conv_b"], p["ln_g"], p["ln_b"], p["gg_c"], tc)
    dproj = _assemble_dproj(dlru, dq, dcur, dprev, dconf)
    bufs["w_in"] = _mm_tn_into(bufs["w_in"], sv["hn"], dproj, 0, 0, D, P_IN, dw, "dw_in")
    dx, gr["mix_pre_g"] = _mm_nt_rmsbwd(dproj, big["w_in"], 0, sv["x1"], p["mix_pre"], dx, mm)
    gr["lru_conv_w"] = dcw[0:LRU_K]
    gr["lru_w_a"] = _diag_blocks(dwa)
    gr["lru_w_x"] = _diag_blocks(dwx)
    gr["attn_sinks"] = dsk[:, 0]
    gr["conv_w"] = dconvw[0:CONV_K]
    gr["group_g"] = jnp.concatenate([dgg_a, dgg_b, dgg_c], axis=1)
    dx, gr["ffn1_pre_g"], gr["ffn1_post_g"] = ffn_bwd(dx, "ffn1", sv["x0"], sv["h1"], sv["g1"], sv["u1"], sv["a1"],
                                                      sv["z1"], p["ffn1_pre"], p["ffn1_post"],
                                                      stage({n: bufs[n] for n in ("w_in", "w_out")}, dx))
    return dx, gr


def _tiles(s):
    return min(512, s), min(1024, s), min(2048, s), min(512, s // 2)


HBM_SPEC = pl.BlockSpec(memory_space=pltpu.HBM)
SEM_SPEC = pl.BlockSpec(memory_space=pltpu.SEMAPHORE)
EFFECT = pltpu.SideEffectType.DATAFLOW_SIDE_EFFECTING


def _place():
    x, y, c = lax.axis_index("x"), lax.axis_index("y"), lax.axis_index("c")
    return x, y, c, [(1 - x, y), (x, 1 - y), (1 - x, 1 - y)]


def _rcopy(src, dst, send_sems, recv_sems, k, to):
    return pltpu.make_async_remote_copy(src_ref=src, dst_ref=dst, send_sem=send_sems.at[k], recv_sem=recv_sems.at[k],
                                        device_id=to, device_id_type=MESH)


def _half(rows, which):
    return pl.ds(which * (rows // 2), rows // 2)


def _place_shard(w, l, p_idx, dtype):
    _, rows, cols = w.shape
    tr = _rows_per_block(rows, cols, 16) if rows % 16 == 0 else rows

    def body(p_ref, buf_ref, w_ref, o_ref):
        o_ref[...] = w_ref[...].astype(dtype)

    spec = pltpu.PrefetchScalarGridSpec(
        num_scalar_prefetch=1, grid=(rows // tr,),
        in_specs=[ANY, pl.BlockSpec((None, tr, cols), lambda i, pr: (l, i, 0))],
        out_specs=pl.BlockSpec((None, None, tr, cols), lambda i, pr: (0, pr[0], i, 0)))
    shape = (1, NSHARD, rows, cols)
    return pl.pallas_call(body, name="place_shard", grid_spec=spec, out_shape=SDS(shape, dtype),
                          input_output_aliases={1: 0}, compiler_params=_cp("parallel"),
                          )(p_idx, lax.empty(shape, dtype), w)


def _gather_two_level(bufs, n_halved):
    n = len(bufs)

    def body(*refs):
        outs = refs[n:2 * n]
        send_sems, recv_sems = refs[2 * n:]
        x, y, c, chips = _place()
        p = 2 * x + y
        me, sibling = (x, y, c), (x, y, 1 - c)

        def blk(a, q, half):
            return outs[a].at[0, q, _half(outs[a].shape[2], half)] if a < n_halved else outs[a].at[0, q]

        def cp(a, k, q, half, to):
            return _rcopy(blk(a, q, half), blk(a, q, half), send_sems, recv_sems, 6 * a + k, to)

        first = [cp(a, j, p, c, (*chip, c)) for a in range(n) for j, chip in enumerate(chips)]
        for d in first:
            d.start()
        passed = []
        for a in range(n):
            for j, chip in enumerate(chips):
                q = 2 * chip[0] + chip[1]
                cp(a, j, q, c, me).wait_recv()
                if a < n_halved:
                    passed.append(cp(a, 3 + j, q, c, sibling))
                    passed[-1].start()
        for a in range(n_halved):
            for j, chip in enumerate(chips):
                cp(a, 3 + j, 2 * chip[0] + chip[1], 1 - c, me).wait_recv()
        for d in first + passed:
            d.wait_send()

    return pl.pallas_call(
        body, name="gather_layer0", in_specs=[ANY] * n, out_specs=[ANY] * n,
        out_shape=[SDS(b.shape, b.dtype) for b in bufs], input_output_aliases={a: a for a in range(n)},
        scratch_shapes=[pltpu.SemaphoreType.DMA((6 * n,)), pltpu.SemaphoreType.DMA((6 * n,))],
    )(*bufs)


def _run_plans(plans, refs, send_sems, recv_sems):
    cps, b0, s0 = [], 0, 0
    for plan, nb, ns in plans:
        cps += plan(refs[b0:b0 + nb], send_sems, recv_sems, s0)
        b0, s0 = b0 + nb, s0 + ns
    return cps


def _exchange(name, bufs, plans):
    n = len(bufs)
    nsem = sum(ns for _, _, ns in plans)

    def body(*refs):
        cps = _run_plans(plans, refs[n:2 * n], refs[2 * n], refs[2 * n + 1])
        for cp in cps:
            cp.start()
        for cp in cps:
            cp.wait()

    return pl.pallas_call(
        body, name=name, in_specs=[ANY] * n, out_specs=[ANY] * n, out_shape=[SDS(b.shape, b.dtype) for b in bufs],
        input_output_aliases={a: a for a in range(n)},
        scratch_shapes=[pltpu.SemaphoreType.DMA((nsem,)), pltpu.SemaphoreType.DMA((nsem,))],
    )(*bufs)


def _exchange_start(name, bufs, plans, deps=()):
    n = len(bufs)
    nsem = sum(ns for _, _, ns in plans)
    deps = list(deps)
    first_out = n + len(deps)

    def body(*refs):
        for cp in _run_plans(plans, refs[:n], refs[first_out], refs[first_out + 1]):
            cp.start()
        token = refs[first_out + 2 + n]
        token[...] = jnp.zeros_like(token)

    outs = pl.pallas_call(
        body, name=name,
        out_shape=(pltpu.SemaphoreType.DMA((nsem,)), pltpu.SemaphoreType.DMA((nsem,)),
                   *[pltpu.HBM(b.shape, b.dtype) for b in bufs], SDS((8, 128), F32)),
        in_specs=[HBM_SPEC] * n + [ANY] * len(deps),
        out_specs=(SEM_SPEC, SEM_SPEC, *[HBM_SPEC] * n, pl.BlockSpec(memory_space=pltpu.VMEM)),
        input_output_aliases={a: 2 + a for a in range(n)},
        compiler_params=pltpu.CompilerParams(has_side_effects=EFFECT),
    )(*[pltpu.with_memory_space_constraint(b, pltpu.HBM) for b in bufs], *deps)
    return outs[0], outs[1], list(outs[2:2 + n]), outs[2 + n]


def _exchange_wait(name, send_sems, recv_sems, bufs, plans, after):
    n = len(bufs)

    def body(*refs):
        for cp in _run_plans(plans, refs[:n], refs[n], refs[n + 1]):
            cp.wait_send()
            cp.wait_recv()

    return pl.pallas_call(
        body, name=name, out_shape=[pltpu.HBM(b.shape, b.dtype) for b in bufs],
        in_specs=[HBM_SPEC] * n + [SEM_SPEC, SEM_SPEC, ANY], out_specs=[HBM_SPEC] * n,
        input_output_aliases={a: a for a in range(n)},
        compiler_params=pltpu.CompilerParams(has_side_effects=EFFECT),
    )(*bufs, send_sems, recv_sems, after)


def _plan_gather(refs, send_sems, recv_sems, base):
    x, y, c, chips = _place()
    p = 2 * x + y
    return [_rcopy(r.at[0, p], r.at[0, p], send_sems, recv_sems, base + 3 * a + j, (*chip, c))
            for a, r in enumerate(refs) for j, chip in enumerate(chips)]


def _plan_pair_exchange(refs, send_sems, recv_sems, base):
    x, y, c, _ = _place()
    n = len(refs) // 2
    return [_rcopy(refs[a].at[:, _half(refs[a].shape[1], 1 - c)], refs[n + a], send_sems, recv_sems, base + a,
                   (x, y, 1 - c)) for a in range(n)]


def _plan_chip_exchange(refs, send_sems, recv_sems, base):
    x, y, c, chips = _place()
    n = len(refs) // 2
    return [_rcopy(refs[a].at[2 * chip[0] + chip[1]], refs[n + a].at[j], send_sems, recv_sems, base + 3 * a + j,
                   (*chip, c)) for a in range(n) for j, chip in enumerate(chips)]


def _plan_pair_share(refs, send_sems, recv_sems, base):
    x, y, c, _ = _place()
    return [_rcopy(r.at[_half(r.shape[0], c)], r.at[_half(r.shape[0], c)], send_sems, recv_sems, base + a,
                   (x, y, 1 - c)) for a, r in enumerate(refs)]


def _plan_small_gather(refs, send_sems, recv_sems, base):
    x, y, c, _ = _place()
    me = 4 * x + 2 * y + c
    cps = []
    for m in range(1, NDEV):
        peer = (1 - x if m & 4 else x, 1 - y if m & 2 else y, 1 - c if m & 1 else c)
        cps.append(_rcopy(refs[0], refs[1].at[me], send_sems, recv_sems, base + m - 1, peer))
    return cps


def _sum_small(buf, gathered):
    def body(buf_ref, g_ref, o_ref):
        x, y, c, _ = _place()
        me = 4 * x + 2 * y + c
        total = jnp.where(me == 0, buf_ref[...], g_ref[0])
        for dev in range(1, NDEV):
            total = total + jnp.where(me == dev, buf_ref[...], g_ref[dev])
        o_ref[...] = total

    vm = pl.BlockSpec(memory_space=pltpu.VMEM)
    return pl.pallas_call(body, name="sum_small", in_specs=[vm, vm], out_specs=vm, out_shape=SDS(buf.shape, F32),
                          compiler_params=pltpu.CompilerParams(vmem_limit_bytes=VMEM_LIMIT))(buf, gathered)


BLOCK_ELEMS = 256 * 1024


def _rows_per_block(rows, cols, mult):
    best = None
    for tr in range(mult, rows + 1, mult):
        if rows % tr == 0 and tr * cols <= BLOCK_ELEMS:
            best = tr
    assert best is not None, (rows, cols)
    return best


def _pair_sum(g, r, c_idx):
    nq, rows, cols = g.shape
    half = rows // 2
    tr = _rows_per_block(half, cols, 16)
    nb = half // tr

    def body(c_ref, g_ref, r_ref, t_ref):
        t_ref[...] = (g_ref[...] + r_ref[...]).astype(BF16)

    blk = pl.BlockSpec((None, tr, cols), lambda q, i, cr: (q, i, 0))
    spec = pltpu.PrefetchScalarGridSpec(
        num_scalar_prefetch=1, grid=(nq, nb),
        in_specs=[pl.BlockSpec((None, tr, cols), lambda q, i, cr: (q, cr[0] * nb + i, 0)), blk], out_specs=blk)
    return pl.pallas_call(body, name="grad_pair_sum", grid_spec=spec, out_shape=SDS((nq, half, cols), BF16),
                          compiler_params=_cp("parallel", "parallel"))(c_idx, g, r)


def _chip_sum(g, r, rr, cp_idx):
    _, rows, cols = g.shape
    half = rows // 2
    tr = _rows_per_block(half, cols, 16)
    nb = half // tr

    def body(cp_ref, buf_ref, g_ref, r_ref, rr_ref, o_ref):
        o_ref[...] = ((g_ref[...] + r_ref[...]) + rr_ref[0].astype(F32) + rr_ref[1].astype(F32) + rr_ref[2].astype(F32))

    spec = pltpu.PrefetchScalarGridSpec(
        num_scalar_prefetch=1, grid=(nb,),
        in_specs=[ANY, pl.BlockSpec((None, tr, cols), lambda i, cp: (cp[1], cp[0] * nb + i, 0)),
                  pl.BlockSpec((None, tr, cols), lambda i, cp: (cp[1], i, 0)),
                  pl.BlockSpec((3, tr, cols), lambda i, cp: (0, i, 0))],
        out_specs=pl.BlockSpec((tr, cols), lambda i, cp: (cp[0] * nb + i, 0)))
    return pl.pallas_call(body, name="grad_chip_sum", grid_spec=spec, out_shape=SDS((rows, cols), F32),
                          input_output_aliases={1: 0}, compiler_params=_cp("parallel"),
                          )(cp_idx, lax.empty((rows, cols), F32), g, r, rr)


def _adamw_math(w, g, m, v):
    mn = ADAM_B1 * m + (1.0 - ADAM_B1) * g
    vn = ADAM_B2 * v + (1.0 - ADAM_B2) * (g * g)
    m_hat = mn / (1.0 - ADAM_B1 ** ADAM_STEP)
    v_hat = vn / (1.0 - ADAM_B2 ** ADAM_STEP)
    return -ADAM_LR * (m_hat / (jnp.sqrt(v_hat) + ADAM_EPS) + ADAM_WD * w), mn, vn


def _adamw_layers(w, gs, m, v):
    depth, rows, cols = w.shape
    tr = _rows_per_block(rows, cols, 8)

    def body(w_ref, g0_ref, g1_ref, m_ref, v_ref, go_ref, d_ref, mo_ref, vo_ref):
        gg = jnp.where(pl.program_id(0) == 0, g0_ref[...], g1_ref[...])
        go_ref[...] = gg
        d_ref[...], mo_ref[...], vo_ref[...] = _adamw_math(w_ref[...], gg, m_ref[...], v_ref[...])

    blk = pl.BlockSpec((None, tr, cols), lambda l, i: (l, i, 0))
    return pl.pallas_call(
        body, name="adamw_layers", grid=(depth, rows // tr),
        in_specs=[blk, pl.BlockSpec((tr, cols), lambda l, i: (i * (1 - l), 0)),
                  pl.BlockSpec((tr, cols), lambda l, i: (i * l, 0)), blk, blk],
        out_specs=[blk] * 4, out_shape=[SDS(w.shape, F32)] * 4,
        compiler_params=_cp("arbitrary", "arbitrary"))(w, gs[0], gs[1], m, v)


def _adamw_small(ws, gs, ms, vs):
    n = len(ws)

    def body(*refs):
        w, g, m, v, d_out, m_out, v_out = (refs[k * n:(k + 1) * n] for k in range(7))
        for k in range(n):
            d_out[k][...], m_out[k][...], v_out[k][...] = _adamw_math(w[k][...], g[k][...], m[k][...], v[k][...])

    vm = pl.BlockSpec(memory_space=pltpu.VMEM)
    outs = pl.pallas_call(body, name="adamw_small", in_specs=[vm] * (4 * n), out_specs=[vm] * (3 * n),
                          out_shape=[SDS(w.shape, F32) for w in ws] * 3,
                          compiler_params=pltpu.CompilerParams(vmem_limit_bytes=VMEM_LIMIT))(*ws, *gs, *ms, *vs)
    return outs[:n], outs[n:2 * n], outs[2 * n:]


_WEIGHTS = ["ffn1_pre_g", "ffn1_w_gu", "ffn1_w_down", "ffn1_post_g", "mix_pre_g", "w_in", "lru_conv_w", "lru_conv_b",
            "lru_w_a", "lru_b_a", "lru_w_x", "lru_b_x", "lru_lambda", "attn_sinks", "conv_w", "conv_b", "conv_ln_g",
            "conv_ln_b", "group_g", "w_out", "mix_post_g", "ffn2_pre_g", "ffn2_w_gu", "ffn2_w_down", "ffn2_post_g"]
_INPUTS = ["x"] + _WEIGHTS + ["loss_target"] + ["m_" + n for n in _WEIGHTS] + ["v_" + n for n in _WEIGHTS]
_BIG = ["ffn1_w_gu", "ffn1_w_down", "w_in", "w_out", "ffn2_w_gu", "ffn2_w_down"]
_SMALL_SHARDED = ["lru_conv_w", "conv_w"]
_SMALL_REPL = [n for n in _WEIGHTS if n not in _BIG and n not in _SMALL_SHARDED]

PACK_TILE = 8 * 128


def _pack(arrs):
    parts = []
    for a in arrs:
        flat = a.reshape(-1)
        parts.append(jnp.pad(flat, (0, -flat.shape[0] % PACK_TILE)).reshape(-1, 128))
    return jnp.concatenate(parts, axis=0)


def _unpack(buf, shapes):
    out, row = [], 0
    for shp in shapes:
        size = math.prod(shp)
        nrow = -(-size // PACK_TILE) * 8
        out.append(buf[row:row + nrow].reshape(-1)[:size].reshape(shp))
        row += nrow
    return out


def _unshard_cols(a):
    return a.transpose(0, 2, 1, 3).reshape(1, a.shape[2], NSHARD * a.shape[3])


_GROUPS = dict(ffn1=["ffn1_w_gu", "ffn1_w_down"], mix=["w_in", "w_out", "lru_conv_w", "conv_w"],
               ffn2=["ffn2_w_gu", "ffn2_w_down"])


def _full_weights(group, gathered):
    g = dict(zip(_GROUPS[group], gathered))
    if group == "mix":
        return dict(w_in=_unshard_cols(g["w_in"]), w_out=g["w_out"].reshape(1, D, D),
                    lru_conv_w=_unshard_cols(g["lru_conv_w"])[0], conv_w=_unshard_cols(g["conv_w"])[0])
    return {group + "_w_gu": g[group + "_w_gu"], group + "_w_down": g[group + "_w_down"].reshape(1, DFF, D)}


def _by_shard(name, buf):
    if name.endswith("w_gu"):
        return buf[0]
    if name == "w_in":
        return buf.reshape(D, NSHARD, P_IN // NSHARD).transpose(1, 0, 2)
    return buf.reshape(NSHARD, buf.shape[2] // NSHARD, buf.shape[3])


class _Reducer:
    PLANS = (_plan_pair_exchange, _plan_chip_exchange, _plan_pair_share)

    def __init__(self, keys, gs, c_idx, cp_idx):
        self.keys, self.gs, self.c_idx, self.cp_idx = keys, gs, c_idx, cp_idx
        self.n = len(gs)
        self.step = 0
        self.result = None

    def inputs(self):
        n = self.n
        if self.step == 0:
            bufs = self.gs + [lax.empty((NSHARD, g.shape[1] // 2, g.shape[2]), F32) for g in self.gs]
        elif self.step == 1:
            ts = [_pair_sum(g, r, self.c_idx) for g, r in zip(self.gs, self.rs)]
            bufs = ts + [lax.empty((3,) + t.shape[1:], BF16) for t in ts]
        else:
            bufs = [_chip_sum(g, r, rr, self.cp_idx) for g, r, rr in zip(self.gs, self.rs, self.rrs)]
        return bufs, (self.PLANS[self.step], len(bufs), (n, 3 * n, n)[self.step])

    def absorb(self, done):
        n = self.n
        if self.step == 0:
            self.gs, self.rs = done[:n], done[n:]
        elif self.step == 1:
            self.rrs = done[n:]
        else:
            self.result = dict(zip(self.keys, done))
        self.step += 1


class _SmallGather:
    def __init__(self, buf):
        self.buf, self.step, self.result, self.gathered = buf, 0, {}, None

    def inputs(self):
        return [self.buf, jnp.zeros((NDEV,) + self.buf.shape, F32)], (_plan_small_gather, 2, NDEV - 1)

    def absorb(self, done):
        self.buf, self.gathered = done
        self.step = 3


class _ReducePipeline:
    def __init__(self, c_idx, cp_idx):
        self.c_idx, self.cp_idx = c_idx, cp_idx
        self.reducers, self.flying, self.calls = [], None, 0

    def add(self, layer, done):
        if done:
            keys = [(layer, n) for n in done]
            self.reducers.append(_Reducer(keys, [_by_shard(n, b) for n, b in done.items()], self.c_idx, self.cp_idx))

    def _next(self):
        active = [r for r in self.reducers if r.step < 3]
        bufs, plans = [], []
        for r in active:
            b, triple = r.inputs()
            bufs += b
            plans.append(triple)
        self.calls += 1
        return active, bufs, plans, "grad_exchange%d" % self.calls

    def _absorb(self, active, plans, done):
        at = 0
        for r, (_, nb, _) in zip(active, plans):
            r.absorb(done[at:at + nb])
            at += nb

    def _land(self, after):
        if self.flying is not None:
            active, plans, name, send_sems, recv_sems, bufs = self.flying
            self._absorb(active, plans, _exchange_wait(name + "_wait", send_sems, recv_sems, bufs, plans, after))
            self.flying = None

    def hook(self, after):
        self._land(after)
        active, bufs, plans, name = self._next()
        if not active:
            return []
        send_sems, recv_sems, bufs, token = _exchange_start(name + "_start", bufs, plans)
        self.flying = (active, plans, name, send_sems, recv_sems, bufs)
        return [token]

    def finish(self, after):
        self._land(after)
        while True:
            active, bufs, plans, name = self._next()
            if not active:
                break
            self._absorb(active, plans, _exchange(name, bufs, plans))
        out = {}
        for r in self.reducers:
            out.update(r.result)
        return out


def kernel(*args):
    d = dict(zip(_INPUTS, args, strict=True))
    xi, yi, ci = lax.axis_index("x"), lax.axis_index("y"), lax.axis_index("c")
    p = 2 * xi + yi
    c_idx = jnp.reshape(ci, (1,)).astype(jnp.int32)
    p_idx = jnp.reshape(p, (1,)).astype(jnp.int32)
    cp_idx = jnp.stack([ci, p]).astype(jnp.int32)
    x, target = d["x"][0], d["loss_target"][0]
    tiles = _tiles(x.shape[0])

    groups = [(l, grp) for l in range(DEPTH) for grp in _GROUPS]
    placed = {(l, grp): [_place_shard(d[n], l, p_idx, BF16 if n in _BIG else F32) for n in _GROUPS[grp]]
              for l, grp in groups}
    ready = {groups[0]: _gather_two_level(placed[groups[0]], len(placed[groups[0]]))}
    flying, tokens = {}, [ready[groups[0]][0]]
    for l, grp in groups[1:]:
        plans = [(_plan_gather, len(placed[l, grp]), 3 * len(placed[l, grp]))]
        send_sems, recv_sems, bufs, token = _exchange_start("gather_l%d_%s_start" % (l, grp), placed[l, grp], plans,
                                                             tokens[-1:])
        flying[l, grp] = (send_sems, recv_sems, bufs, plans)
        tokens.append(token)

    def weights_of(l):
        def weights(grp, after):
            if (l, grp) not in ready:
                send_sems, recv_sems, bufs, plans = flying[l, grp]
                ready[l, grp] = _exchange_wait("gather_l%d_%s_wait" % (l, grp), send_sems, recv_sems, bufs, plans, after)
            return _full_weights(grp, ready[l, grp])
        return weights

    small = {n: d[n] for n in _SMALL_REPL}
    x1, sv0 = _forward_layer(x, weights_of(0), _layer_params(small, 0), tiles, tokens[1:])
    x2, sv1 = _forward_layer(x1, weights_of(1), _layer_params(small, 1), tiles)
    dx, lcols = _loss_grad(x2, target, tiles[0])

    pipe = _ReducePipeline(c_idx, cp_idx)
    sgrads = [None] * DEPTH
    for l, sv in ((1, sv1), (0, sv0)):
        bufs = _grad_buffers()

        def stage(done, dx, l=l):
            pipe.add(l, done)
            return pipe.hook(dx)

        dx, sgrads[l] = _backward_layer(dx, sv, bufs, tiles, stage)
        pipe.add(l, {n: bufs[n] for n in ("ffn1_w_gu", "ffn1_w_down")})
    grad_x = dx

    stacked = {n: jnp.stack([sgrads[l][n].reshape(d[n].shape[1:]) for l in range(DEPTH)]) for n in _SMALL_REPL}
    for n in _SMALL_SHARDED:
        stacked[n] = jnp.stack([sgrads[l][n] for l in range(DEPTH)])
    loss_part = jnp.pad((0.5 / D) * jnp.sum(lcols).reshape(1), (0, 127))
    order = _SMALL_REPL + _SMALL_SHARDED
    small_gather = _SmallGather(_pack([loss_part] + [stacked[n] for n in order]))
    pipe.reducers.append(small_gather)
    reduced = pipe.finish(grad_x)
    summed = _unpack(_sum_small(small_gather.buf, small_gather.gathered), [(128,)] + [stacked[n].shape for n in order])
    loss = summed[0][0]
    grads = {}
    for n, g in zip(order, summed[1:]):
        if n in _SMALL_SHARDED:
            g = lax.dynamic_slice_in_dim(g, p * (g.shape[2] // NSHARD), g.shape[2] // NSHARD, axis=2)
        grads[n] = g

    delta, new_m, new_v = {}, {}, {}
    for n in _BIG:
        grads[n], delta[n], new_m[n], new_v[n] = _adamw_layers(d[n], [reduced[l, n] for l in range(DEPTH)],
                                                                d["m_" + n], d["v_" + n])
    small_out = _adamw_small([d[n] for n in order], [grads[n] for n in order], [d["m_" + n] for n in order],
                             [d["v_" + n] for n in order])
    for out, res in zip((delta, new_m, new_v), small_out):
        out.update(zip(order, res))

    return (loss, grad_x[None], *[grads[n] for n in _WEIGHTS], *[delta[n] for n in _WEIGHTS],
            *[new_m[n] for n in _WEIGHTS], *[new_v[n] for n in _WEIGHTS])
```

```python
import functools
import math

import jax
import jax.numpy as jnp
from jax import lax
from jax.experimental import pallas as pl
from jax.experimental.pallas import tpu as pltpu

F32 = jnp.float32
BF16 = jnp.bfloat16
SDS = jax.ShapeDtypeStruct

D = 1024
DFF = 2816
FH = DFF // 2
DEPTH = 2
W_A = 256
W_B = 512
W_C = 256
NQ = 8
HD = 64
BLK = 128
ATT_NB_FWD = 1
ATT_NB_BWD = 4
P_IN = 1792
LRU_K = 4
CONV_K = 31
LRU_C = 8.0
NORM_EPS = 1e-6
LN_EPS = 1e-5
NEG_BIG = -1e30
SCALE = 1.0 / math.sqrt(HD)

ADAM_LR = 0.001
ADAM_B1 = 0.9
ADAM_B2 = 0.999
ADAM_EPS = 1e-08
ADAM_WD = 0.01
ADAM_STEP = 10

VMEM_LIMIT = 60 * 1024 * 1024
NSHARD = 4
NDEV = 8

TN = (((0,), (0,)), ((), ()))
NT = (((1,), (1,)), ((), ()))

MESH = pl.DeviceIdType.MESH
ANY = pl.BlockSpec(memory_space=pl.ANY)


def _cp(*sem):
    return pltpu.CompilerParams(dimension_semantics=sem if sem else None, vmem_limit_bytes=VMEM_LIMIT)


def _rsq(x, eps):
    return lax.rsqrt(jnp.mean(x * x, axis=-1, keepdims=True) + eps)


def _rms_bwd_rows(x, g, dy):
    r = _rsq(x, NORM_EPS)
    xh = x * r
    dyg = dy * g
    dx = r * (dyg - xh * jnp.mean(dyg * xh, axis=-1, keepdims=True))
    return dx, dy * xh


def _sig(x):
    return jax.nn.sigmoid(x)


def _ffn_up(x, pre_g, wgu, l, tm, deps=()):
    s = x.shape[0]
    deps = list(deps)

    def body(x_ref, g_ref, wg_ref, wu_ref, *rest):
        h_ref, go_ref, uo_ref, a_ref = rest[len(deps):]

        @pl.when(pl.program_id(1) == 0)
        def _():
            xf = x_ref[...]
            h_ref[...] = (xf * _rsq(xf, NORM_EPS) * g_ref[...]).astype(BF16)

        h = h_ref[...]
        gg = jnp.dot(h, wg_ref[...], preferred_element_type=F32)
        uu = jnp.dot(h, wu_ref[...], preferred_element_type=F32)
        sg = _sig(gg)
        silu = gg * sg
        go_ref[...] = (uu * (sg * (1.0 + gg * (1.0 - sg)))).astype(BF16)
        uo_ref[...] = silu.astype(BF16)
        a_ref[...] = (silu * uu).astype(BF16)

    wide = pl.BlockSpec((tm, FH), lambda i, j: (i, j))
    return pl.pallas_call(
        body, name="ffn_up", grid=(s // tm, 2),
        in_specs=[pl.BlockSpec((tm, D), lambda i, j: (i, 0)), pl.BlockSpec((1, D), lambda i, j: (0, 0)),
                  pl.BlockSpec((None, None, D, FH), lambda i, j: (l, j, 0, 0)),
                  pl.BlockSpec((None, None, D, FH), lambda i, j: (l, j + 2, 0, 0))] + [ANY] * len(deps),
        out_specs=[pl.BlockSpec((tm, D), lambda i, j: (i, 0)), wide, wide, wide],
        out_shape=[SDS((s, D), BF16), SDS((s, DFF), BF16), SDS((s, DFF), BF16), SDS((s, DFF), BF16)],
        compiler_params=_cp("parallel", "arbitrary"),
    )(x, pre_g, wgu, wgu, *deps)


def _mm_rms_res(a, w, l, x, g, c, tm, tk, name):
    s, k_dim = a.shape
    nk = k_dim // tk

    def body(a_ref, w_ref, x_ref, g_ref, z_ref, x1_ref):
        k = pl.program_id(1)
        p = jnp.dot(a_ref[...], w_ref[...], preferred_element_type=F32)

        @pl.when(k == 0)
        def _():
            z_ref[...] = p

        @pl.when(k > 0)
        def _():
            z_ref[...] += p

        @pl.when(k == nk - 1)
        def _():
            z = z_ref[...]
            x1_ref[...] = x_ref[...] + c * (z * _rsq(z, NORM_EPS) * g_ref[...])

    row = pl.BlockSpec((tm, D), lambda i, k: (i, 0))
    return pl.pallas_call(
        body, name=name, grid=(s // tm, nk),
        in_specs=[pl.BlockSpec((tm, tk), lambda i, k: (i, k)), pl.BlockSpec((None, tk, D), lambda i, k: (l, k, 0)),
                  row, pl.BlockSpec((1, D), lambda i, k: (0, 0))],
        out_specs=[row, row],
        out_shape=[SDS((s, D), F32), SDS((s, D), F32)],
        compiler_params=_cp("parallel", "arbitrary"),
    )(a, w, x, g)


def _rms_bwd(dy, z, g, c, tm, name, deps=()):
    s = z.shape[0]
    deps = list(deps)

    def body(dy_ref, z_ref, g_ref, *rest):
        dz_ref, dg_ref = rest[len(deps):]
        dz, dgr = _rms_bwd_rows(z_ref[...], g_ref[...], c * dy_ref[...])
        dz_ref[...] = dz.astype(BF16)
        part = jnp.sum(dgr, axis=0, keepdims=True)

        @pl.when(pl.program_id(0) == 0)
        def _():
            dg_ref[...] = part

        @pl.when(pl.program_id(0) > 0)
        def _():
            dg_ref[...] += part

    row = pl.BlockSpec((tm, D), lambda i: (i, 0))
    vec = pl.BlockSpec((1, D), lambda i: (0, 0))
    return pl.pallas_call(
        body, name=name, grid=(s // tm,), in_specs=[row, row, vec] + [ANY] * len(deps), out_specs=[row, vec],
        out_shape=[SDS((s, D), BF16), SDS((1, D), F32)], compiler_params=_cp("arbitrary"),
    )(dy, z, g, *deps)


def _ffn_bwd_mid(dz, wd, l, dadg, dadu, tm):
    s = dz.shape[0]

    def body(dz_ref, wd_ref, g_ref, u_ref, dg_ref, du_ref):
        da = lax.dot_general(dz_ref[...], wd_ref[...], NT, preferred_element_type=F32)
        dg_ref[...] = (da * g_ref[...].astype(F32)).astype(BF16)
        du_ref[...] = (da * u_ref[...].astype(F32)).astype(BF16)

    wide = pl.BlockSpec((tm, FH), lambda i, j: (i, j))
    return pl.pallas_call(
        body, name="ffn_bwd_mid", grid=(s // tm, 2),
        in_specs=[pl.BlockSpec((tm, D), lambda i, j: (i, 0)), pl.BlockSpec((None, FH, D), lambda i, j: (l, j, 0)), wide, wide],
        out_specs=[wide, wide],
        out_shape=[SDS((s, DFF), BF16), SDS((s, DFF), BF16)],
        compiler_params=_cp("parallel", "arbitrary"),
    )(dz, wd, dadg, dadu)


def _ffn_bwd_dh(dg, du, wgu, l, x, pre_g, dx1, tm, deps=()):
    s = x.shape[0]
    deps = list(deps)

    def body(dg_ref, du_ref, wg_ref, wu_ref, x_ref, g_ref, dx1_ref, *rest):
        dx_ref, dgp_ref = rest[len(deps):]
        i, k = pl.program_id(0), pl.program_id(1)
        p = (lax.dot_general(dg_ref[...], wg_ref[...], NT, preferred_element_type=F32)
             + lax.dot_general(du_ref[...], wu_ref[...], NT, preferred_element_type=F32))

        @pl.when(k == 0)
        def _():
            dx_ref[...] = p

        @pl.when(k == 1)
        def _():
            dx, dgr = _rms_bwd_rows(x_ref[...], g_ref[...], dx_ref[...] + p)
            dx_ref[...] = dx1_ref[...] + dx
            part = jnp.sum(dgr, axis=0, keepdims=True)

            @pl.when(i == 0)
            def _():
                dgp_ref[...] = part

            @pl.when(i > 0)
            def _():
                dgp_ref[...] += part

    wide = pl.BlockSpec((tm, FH), lambda i, k: (i, k))
    row = pl.BlockSpec((tm, D), lambda i, k: (i, 0))
    vec = pl.BlockSpec((1, D), lambda i, k: (0, 0))
    return pl.pallas_call(
        body, name="ffn_bwd_dh", grid=(s // tm, 2),
        in_specs=[wide, wide, pl.BlockSpec((None, None, D, FH), lambda i, k: (l, k, 0, 0)),
                  pl.BlockSpec((None, None, D, FH), lambda i, k: (l, k + 2, 0, 0)), row, vec, row] + [ANY] * len(deps),
        out_specs=[row, vec],
        out_shape=[SDS((s, D), F32), SDS((1, D), F32)],
        compiler_params=_cp("arbitrary", "arbitrary"),
    )(dg, du, wgu, wgu, x, pre_g, dx1, *deps)


def _mm_tn_into(buf, a, b, l, joff, tka, tn, ts, name):
    s, ka = a.shape
    n = b.shape[1]

    def body(buf_ref, a_ref, b_ref, o_ref):
        p = lax.dot_general(a_ref[...], b_ref[...], TN, preferred_element_type=F32)

        @pl.when(pl.program_id(2) == 0)
        def _():
            o_ref[...] = p

        @pl.when(pl.program_id(2) > 0)
        def _():
            o_ref[...] += p

    return pl.pallas_call(
        body, name=name, grid=(ka // tka, n // tn, s // ts),
        in_specs=[pl.BlockSpec(memory_space=pl.ANY),
                  pl.BlockSpec((ts, tka), lambda ia, j, t: (t, ia)), pl.BlockSpec((ts, tn), lambda ia, j, t: (t, j))],
        out_specs=pl.BlockSpec((None, None, tka, tn), lambda ia, j, t: (l, joff + j, ia, 0)),
        out_shape=SDS(buf.shape, F32), input_output_aliases={0: 0},
        compiler_params=_cp("parallel", "parallel", "arbitrary"),
    )(buf, a, b)


def _proj(x, g, w_in, l, tm):
    s = x.shape[0]

    def body(x_ref, g_ref, w_ref, h_ref, p_ref):
        xf = x_ref[...]
        h = (xf * _rsq(xf, NORM_EPS) * g_ref[...]).astype(BF16)
        h_ref[...] = h
        p_ref[...] = jnp.dot(h, w_ref[...], preferred_element_type=F32)

    return pl.pallas_call(
        body, name="proj", grid=(s // tm,),
        in_specs=[pl.BlockSpec((tm, D), lambda i: (i, 0)), pl.BlockSpec((1, D), lambda i: (0, 0)),
                  pl.BlockSpec((None, D, P_IN), lambda i: (l, 0, 0))],
        out_specs=[pl.BlockSpec((tm, D), lambda i: (i, 0)), pl.BlockSpec((tm, P_IN), lambda i: (i, 0))],
        out_shape=[SDS((s, D), BF16), SDS((s, P_IN), F32)],
        compiler_params=_cp("parallel"),
    )(x, g, w_in)


def _mm_nt(a, w, l, tm, name):
    s, k_dim = a.shape
    n = w.shape[1]

    def body(a_ref, w_ref, o_ref):
        o_ref[...] = lax.dot_general(a_ref[...], w_ref[...], NT, preferred_element_type=F32)

    return pl.pallas_call(
        body, name=name, grid=(s // tm,),
        in_specs=[pl.BlockSpec((tm, k_dim), lambda i: (i, 0)), pl.BlockSpec((None, n, k_dim), lambda i: (l, 0, 0))],
        out_specs=pl.BlockSpec((tm, n), lambda i: (i, 0)),
        out_shape=SDS((s, n), F32), compiler_params=_cp("parallel"),
    )(a, w)


def _mm_nt_rmsbwd(dp, w_in, l, x, g, dx1, tm):
    s = x.shape[0]

    def body(dp_ref, w_ref, x_ref, g_ref, dx1_ref, dx_ref, dg_ref):
        dh = lax.dot_general(dp_ref[...], w_ref[...], NT, preferred_element_type=F32)
        dx, dgr = _rms_bwd_rows(x_ref[...], g_ref[...], dh)
        dx_ref[...] = dx1_ref[...] + dx
        part = jnp.sum(dgr, axis=0, keepdims=True)

        @pl.when(pl.program_id(0) == 0)
        def _():
            dg_ref[...] = part

        @pl.when(pl.program_id(0) > 0)
        def _():
            dg_ref[...] += part

    row = pl.BlockSpec((tm, D), lambda i: (i, 0))
    vec = pl.BlockSpec((1, D), lambda i: (0, 0))
    return pl.pallas_call(
        body, name="mix_bwd_dx", grid=(s // tm,),
        in_specs=[pl.BlockSpec((tm, P_IN), lambda i: (i, 0)), pl.BlockSpec((None, D, P_IN), lambda i: (l, 0, 0)), row, vec, row],
        out_specs=[row, vec], out_shape=[SDS((s, D), F32), SDS((1, D), F32)],
        compiler_params=_cp("arbitrary"),
    )(dp, w_in, x, g, dx1)


def _row_iota(shape):
    return lax.broadcasted_iota(jnp.int32, shape, 0)


def _lru_gates(xc, wa_ref, ba_ref, wx_ref, bx_ref, lam_ref):
    xb = xc.astype(BF16)
    r = _sig(jnp.dot(xb, wa_ref[...], preferred_element_type=F32) + ba_ref[...])
    ig = _sig(jnp.dot(xb, wx_ref[...], preferred_element_type=F32) + bx_ref[...])
    nl = -lam_ref[...]
    sp = jnp.maximum(nl, 0.0) + jnp.log(1.0 + jnp.exp(-jnp.abs(nl)))
    log_a = -LRU_C * r * sp
    a = jnp.exp(log_a)
    x2 = 2.0 * log_a
    series = x2 * (1.0 + x2 * (0.5 + x2 * (1.0 / 6.0 + x2 * (1.0 / 24.0 + x2 * (1.0 / 120.0)))))
    em1 = jnp.where(x2 > -0.05, series, jnp.exp(x2) - 1.0)
    mlt = jnp.sqrt(-em1)
    return r, ig, a, mlt, sp


def _conv_taps(src_ref, w_ref, k_taps, pad, tc):
    acc = None
    for j in range(k_taps):
        term = w_ref[j:j + 1, :] * src_ref[pl.ds(pad - (k_taps - 1) + j, tc), :]
        acc = term if acc is None else acc + term
    return acc


def _gelu_parts(x):
    c0 = math.sqrt(2.0 / math.pi)
    inner = c0 * (x + 0.044715 * x * x * x)
    t = jnp.tanh(inner)
    gl = 0.5 * x * (1.0 + t)
    dgl = 0.5 * (1.0 + t) + 0.5 * x * (1.0 - t * t) * c0 * (1.0 + 3.0 * 0.044715 * x * x)
    return gl, dgl


def _lru_fwd(proj, cw, cb, wa, ba, wx, bx, lam, gg, tc):
    s = proj.shape[0]
    pad = 8

    def body(xcur_ref, xprev_ref, gate_ref, cw_ref, cb_ref, wa_ref, ba_ref, wx_ref, bx_ref, lam_ref, gg_ref,
             yn_ref, h_ref, xs_ref, hc_ref):
        i = pl.program_id(0)

        @pl.when(i == 0)
        def _():
            hc_ref[...] = jnp.zeros_like(hc_ref)

        xs_ref[0:pad, :] = jnp.where(i > 0, xprev_ref[tc - pad:tc, :], 0.0)
        xs_ref[pad:pad + tc, :] = xcur_ref[...]
        xc = _conv_taps(xs_ref, cw_ref, LRU_K, pad, tc) + cb_ref[...]
        _, ig, a, mlt, _ = _lru_gates(xc, wa_ref, ba_ref, wx_ref, bx_ref, lam_ref)
        u = mlt * (ig * xc)
        row = _row_iota((tc, W_A))
        d = 1
        while d < tc:
            ok = row >= d
            a_sh = jnp.where(ok, pltpu.roll(a, d, axis=0), 1.0)
            u_sh = jnp.where(ok, pltpu.roll(u, d, axis=0), 0.0)
            u = a * u_sh + u
            a = a * a_sh
            d *= 2
        h = u + a * hc_ref[...]
        hc_ref[...] = jnp.sum(jnp.where(row == tc - 1, h, 0.0), axis=0, keepdims=True)
        h_ref[...] = h
        gl, _ = _gelu_parts(gate_ref[...])
        ya = gl * h
        yn_ref[...] = (ya * _rsq(ya, NORM_EPS) * gg_ref[...]).astype(BF16)

    blk = lambda c: pl.BlockSpec((tc, W_A), lambda i, c=c: (i, c))
    full = lambda a: pl.BlockSpec(a.shape, lambda i: (0,) * a.ndim)
    params = [cw, cb, wa, ba, wx, bx, lam, gg]
    return pl.pallas_call(
        body, name="lru_fwd", grid=(s // tc,),
        in_specs=[blk(0), pl.BlockSpec((tc, W_A), lambda i: (jnp.maximum(i - 1, 0), 0)), blk(1)] + [full(a) for a in params],
        out_specs=[pl.BlockSpec((tc, W_A), lambda i: (i, 0))] * 2,
        out_shape=[SDS((s, W_A), BF16), SDS((s, W_A), F32)],
        scratch_shapes=[pltpu.VMEM((tc + pad, W_A), F32), pltpu.VMEM((1, W_A), F32)],
        compiler_params=_cp("arbitrary"),
    )(proj, proj, proj, *params)


def _acc(ref, first, val):
    @pl.when(first)
    def _():
        ref[...] = val

    @pl.when(jnp.logical_not(first))
    def _():
        ref[...] += val


def _lru_bwd(dy, proj, h, cw, cb, wa, ba, wx, bx, lam, gg, tc):
    s = proj.shape[0]
    nc = s // tc
    pad = 8

    def body(dy_ref, xcur_ref, xprev_ref, gate_ref, h_ref, hprev_ref, cw_ref, cb_ref, wa_ref, ba_ref, wx_ref, bx_ref,
             lam_ref, gg_ref,
             dp_ref, dcw_ref, dcb_ref, dwa_ref, dba_ref, dwx_ref, dbx_ref, dlam_ref, dgg_ref,
             xs_ref, ds_ref, mu_ref, nx_ref):
        step = pl.program_id(0)
        i = nc - 1 - step
        first = step == 0

        @pl.when(first)
        def _():
            mu_ref[...] = jnp.zeros_like(mu_ref)
            nx_ref[...] = jnp.zeros_like(nx_ref)

        xs_ref[0:pad, :] = jnp.where(i > 0, xprev_ref[tc - pad:tc, :], 0.0)
        xs_ref[pad:pad + tc, :] = xcur_ref[...]
        xc = _conv_taps(xs_ref, cw_ref, LRU_K, pad, tc) + cb_ref[...]
        r, ig, a, mlt, sp = _lru_gates(xc, wa_ref, ba_ref, wx_ref, bx_ref, lam_ref)
        hh = h_ref[...]
        gate = gate_ref[...]
        gl, dgl = _gelu_parts(gate)
        ya = gl * hh
        dya, dggr = _rms_bwd_rows(ya, gg_ref[...], dy_ref[...])
        _acc(dgg_ref, first, jnp.sum(dggr, axis=0, keepdims=True))
        dp_ref[:, W_A:2 * W_A] = dya * hh * dgl
        dh = dya * gl

        row = _row_iota((tc, W_A))
        aa = a
        uu = a * dh
        d = 1
        while d < tc:
            ok = row < tc - d
            a_sh = jnp.where(ok, pltpu.roll(aa, tc - d, axis=0), 1.0)
            u_sh = jnp.where(ok, pltpu.roll(uu, tc - d, axis=0), 0.0)
            uu = uu + aa * u_sh
            aa = aa * a_sh
            d *= 2
        cin = mu_ref[...]
        mu = uu + aa * cin
        lam_t = dh + jnp.where(row == tc - 1, cin, pltpu.roll(mu, tc - 1, axis=0))
        mu_ref[...] = jnp.sum(jnp.where(row == 0, mu, 0.0), axis=0, keepdims=True)
        hm1 = jnp.where(row == 0, jnp.where(i > 0, pltpu.roll(hprev_ref[...], 1, axis=0), 0.0),
                        pltpu.roll(hh, 1, axis=0))
        da = lam_t * hm1
        du = lam_t
        dmlt = du * ig * xc
        dig = du * mlt * xc
        dxc = du * mlt * ig
        dlog_a = da * a - dmlt * (a * a / mlt)
        dr = dlog_a * (-LRU_C * sp)
        dsp = jnp.sum(dlog_a * (-LRU_C * r), axis=0, keepdims=True)
        _acc(dlam_ref, first, dsp * (-_sig(-lam_ref[...])))
        dga = dr * r * (1.0 - r)
        dgx = dig * ig * (1.0 - ig)
        _acc(dba_ref, first, jnp.sum(dga, axis=0, keepdims=True))
        _acc(dbx_ref, first, jnp.sum(dgx, axis=0, keepdims=True))
        xb = xc.astype(BF16)
        dgab = dga.astype(BF16)
        dgxb = dgx.astype(BF16)
        _acc(dwa_ref, first, lax.dot_general(xb, dgab, TN, preferred_element_type=F32))
        _acc(dwx_ref, first, lax.dot_general(xb, dgxb, TN, preferred_element_type=F32))
        dxc = (dxc + lax.dot_general(dgab, wa_ref[...], NT, preferred_element_type=F32)
               + lax.dot_general(dgxb, wx_ref[...], NT, preferred_element_type=F32))

        _acc(dcb_ref, first, jnp.sum(dxc, axis=0, keepdims=True))
        r8 = _row_iota((8, W_A))
        dcw = jnp.zeros((8, W_A), F32)
        for j in range(LRU_K):
            tap = jnp.sum(dxc * xs_ref[pl.ds(pad - (LRU_K - 1) + j, tc), :], axis=0, keepdims=True)
            dcw = dcw + jnp.where(r8 == j, tap, 0.0)
        _acc(dcw_ref, first, dcw)
        ds_ref[0:tc, :] = dxc
        ds_ref[tc:tc + pad, :] = nx_ref[...]
        dlx = None
        for j in range(LRU_K):
            term = cw_ref[j:j + 1, :] * ds_ref[pl.ds(LRU_K - 1 - j, tc), :]
            dlx = term if dlx is None else dlx + term
        dp_ref[:, 0:W_A] = dlx
        nx_ref[...] = dxc[0:pad, :]

    rev = lambda c: pl.BlockSpec((tc, W_A), lambda t, c=c: (nc - 1 - t, c))
    prev = lambda c: pl.BlockSpec((tc, W_A), lambda t, c=c: (jnp.maximum(nc - 2 - t, 0), c))
    full = lambda a: pl.BlockSpec(a.shape, lambda t: (0,) * a.ndim)
    params = [cw, cb, wa, ba, wx, bx, lam, gg]
    vec = SDS((1, W_A), F32)
    sq = SDS((W_A, W_A), F32)
    outs = [SDS((s, 2 * W_A), F32), SDS((8, W_A), F32), vec, sq, vec, sq, vec, vec, vec]
    return pl.pallas_call(
        body, name="lru_bwd", grid=(nc,),
        in_specs=[rev(0), rev(0), prev(0), rev(1), rev(0), prev(0)] + [full(a) for a in params],
        out_specs=[pl.BlockSpec((tc, 2 * W_A), lambda t: (nc - 1 - t, 0))]
        + [pl.BlockSpec(o.shape, lambda t: (0, 0)) for o in outs[1:]],
        out_shape=outs,
        scratch_shapes=[pltpu.VMEM((tc + pad, W_A), F32), pltpu.VMEM((tc + pad, W_A), F32),
                        pltpu.VMEM((1, W_A), F32), pltpu.VMEM((pad, W_A), F32)],
        compiler_params=_cp("arbitrary"),
    )(dy, proj, proj, proj, h, h, *params)


def _attn_stack(qa, qb, kvh):
    lane = lax.broadcasted_iota(jnp.int32, qa.shape, 1)
    keep = (lane >= HD) if kvh == 1 else (lane < HD)
    parts = []
    for tile in (qa, qb):
        for half in (0, 1):
            y = tile if half == kvh else pltpu.roll(tile, HD, axis=1)
            parts.append(jnp.where(keep, y, 0.0))
    return jnp.concatenate(parts, axis=0)


def _attn_unstack(o, kvh):
    lane = lax.broadcasted_iota(jnp.int32, (BLK, 2 * HD), 1)
    tiles = []
    for t in range(2):
        halves = []
        for half in (0, 1):
            blk = o[(2 * t + half) * BLK:(2 * t + half + 1) * BLK, :]
            halves.append(blk if half == kvh else pltpu.roll(blk, HD, axis=1))
        tiles.append(jnp.where(lane < HD, halves[0], halves[1]))
    return tiles


def _attn_stack_all(x_ref_or_val):
    return jnp.concatenate([_attn_stack(x_ref_or_val[:, 256 * kvh:256 * kvh + 128],
                                        x_ref_or_val[:, 256 * kvh + 128:256 * kvh + 256], kvh) for kvh in range(2)], axis=0)


def _attn_unstack_all(o, dst_ref):
    for kvh in range(2):
        ta, tb = _attn_unstack(o[4 * BLK * kvh:4 * BLK * (kvh + 1), :], kvh)
        dst_ref[:, 256 * kvh:256 * kvh + 128] = ta
        dst_ref[:, 256 * kvh + 128:256 * kvh + 256] = tb


def _attn_windows(cur_ref, prev_ref, nb):
    blocks = [prev_ref[...]] + [cur_ref[b * BLK:(b + 1) * BLK, :] for b in range(nb)]
    return [jnp.concatenate(blocks[b:b + 2], axis=0).astype(BF16) for b in range(nb)]


def _attn_probs(qs, kw, n, sink_ref):
    rows = NQ * BLK
    sc = lax.dot_general(qs.astype(BF16), kw, NT, preferred_element_type=F32) * SCALE
    qi = lax.broadcasted_iota(jnp.int32, (rows, 2 * BLK), 0) & (BLK - 1)
    kj = lax.broadcasted_iota(jnp.int32, (rows, 2 * BLK), 1)
    rel = BLK + qi - kj
    mask = (rel >= 0) & (rel < BLK) & ((n - 1) * BLK + kj >= 0)
    head = lax.broadcasted_iota(jnp.int32, (rows, 1), 0) // BLK
    sk = jnp.zeros((rows, 1), F32)
    for h in range(NQ):
        sk = jnp.where(head == h, sink_ref[h:h + 1, 0:1], sk)
    sh = jnp.where(mask, sc, NEG_BIG)
    m = jnp.maximum(jnp.max(sh, axis=-1, keepdims=True), sk)
    e = jnp.exp(sh - m)
    es = jnp.exp(sk - m)
    rz = 1.0 / (jnp.sum(e, axis=-1, keepdims=True) + es)
    return e * rz, es * rz


def _attn_fwd(proj, sinks8, gg):
    s = proj.shape[0]
    nb = ATT_NB_FWD

    def body(q_ref, kc_ref, kp_ref, vc_ref, vp_ref, sink_ref, gg_ref, yn_ref, ob_ref):
        kws, vws = _attn_windows(kc_ref, kp_ref, nb), _attn_windows(vc_ref, vp_ref, nb)
        for b in range(nb):
            rows = pl.ds(b * BLK, BLK)
            p, _ = _attn_probs(_attn_stack_all(q_ref.at[rows, :]), kws[b], nb * pl.program_id(0) + b, sink_ref)
            _attn_unstack_all(jnp.dot(p.astype(BF16), vws[b], preferred_element_type=F32), ob_ref.at[rows, :])
        ob = ob_ref[...]
        yn_ref[...] = (ob * _rsq(ob, NORM_EPS) * gg_ref[...]).astype(BF16)

    tb = nb * BLK
    cur = lambda c: pl.BlockSpec((tb, 128), lambda m, c=c: (m, c))
    prev = lambda c: pl.BlockSpec((BLK, 128), lambda m, c=c: (jnp.maximum(nb * m - 1, 0), c))
    out = pl.BlockSpec((tb, W_B), lambda m: (m, 0))
    return pl.pallas_call(
        body, name="attn_fwd", grid=(s // tb,),
        in_specs=[pl.BlockSpec((tb, W_B), lambda m: (m, 1)), cur(8), prev(8), cur(9), prev(9),
                  pl.BlockSpec((8, 128), lambda n: (0, 0)), pl.BlockSpec((1, W_B), lambda n: (0, 0))],
        out_specs=[out, out], out_shape=[SDS((s, W_B), BF16), SDS((s, W_B), F32)],
        compiler_params=_cp("parallel"),
    )(proj, proj, proj, proj, proj, sinks8, gg)


def _attn_bwd(dy, proj, ob, sinks8, gg):
    s = proj.shape[0]
    nb = ATT_NB_BWD

    def body(dya_ref, dyb_ref, q_ref, kc_ref, kp_ref, vc_ref, vp_ref, ob_ref, sink_ref, gg_ref,
             dq_ref, dcur_ref, dprev_ref, dsink_ref, dgg_ref):
        first = pl.program_id(0) == 0
        kws, vws = _attn_windows(kc_ref, kp_ref, nb), _attn_windows(vc_ref, vp_ref, nb)
        dyn = jnp.concatenate([dya_ref[...], dyb_ref[...]], axis=1)
        dob, dggr = _rms_bwd_rows(ob_ref[...], gg_ref[...], dyn)
        _acc(dgg_ref, first, jnp.sum(dggr, axis=0, keepdims=True))
        r8 = _row_iota((8, 128))
        dsk = jnp.zeros((8, 128), F32)
        for b in range(nb):
            rows = pl.ds(b * BLK, BLK)
            qs = _attn_stack_all(q_ref.at[rows, :])
            p, psink = _attn_probs(qs, kws[b], nb * pl.program_id(0) + b, sink_ref)
            dosb = _attn_stack_all(dob[b * BLK:(b + 1) * BLK, :]).astype(BF16)
            dp = lax.dot_general(dosb, vws[b], NT, preferred_element_type=F32)
            dd = jnp.sum(p * dp, axis=-1, keepdims=True)
            dsb = (p * (dp - dd) * SCALE).astype(BF16)
            dsink_rows = -psink * dd
            for h in range(NQ):
                dsk = dsk + jnp.where(r8 == h, jnp.sum(dsink_rows[h * BLK:(h + 1) * BLK, :], axis=0, keepdims=True), 0.0)
            _attn_unstack_all(jnp.dot(dsb, kws[b], preferred_element_type=F32), dq_ref.at[rows, :])
            dkw = lax.dot_general(dsb, qs.astype(BF16), TN, preferred_element_type=F32)
            dvw = lax.dot_general(p.astype(BF16), dosb, TN, preferred_element_type=F32)
            dprev_ref[rows, 0:128] = dkw[0:BLK, :]
            dprev_ref[rows, 128:256] = dvw[0:BLK, :]
            dcur_ref[rows, 0:128] = dkw[BLK:2 * BLK, :]
            dcur_ref[rows, 128:256] = dvw[BLK:2 * BLK, :]
        _acc(dsink_ref, first, dsk)

    tb = nb * BLK
    cur = lambda c: pl.BlockSpec((tb, 128), lambda m, c=c: (m, c))
    prev = lambda c: pl.BlockSpec((BLK, 128), lambda m, c=c: (jnp.maximum(nb * m - 1, 0), c))
    wide = pl.BlockSpec((tb, W_B), lambda m: (m, 0))
    half = pl.BlockSpec((tb, 256), lambda m: (m, 0))
    return pl.pallas_call(
        body, name="attn_bwd", grid=(s // tb,),
        in_specs=[pl.BlockSpec((tb, 256), lambda m: (m, 1)), pl.BlockSpec((tb, 256), lambda m: (m, 2)),
                  pl.BlockSpec((tb, W_B), lambda m: (m, 1)), cur(8), prev(8), cur(9), prev(9), wide,
                  pl.BlockSpec((8, 128), lambda n: (0, 0)), pl.BlockSpec((1, W_B), lambda n: (0, 0))],
        out_specs=[wide, half, half, pl.BlockSpec((8, 128), lambda n: (0, 0)), pl.BlockSpec((1, W_B), lambda n: (0, 0))],
        out_shape=[SDS((s, W_B), F32), SDS((s, 256), F32), SDS((s, 256), F32), SDS((8, 128), F32), SDS((1, W_B), F32)],
        compiler_params=_cp("arbitrary"),
    )(dy, dy, proj, proj, proj, proj, proj, ob, sinks8, gg)


def _ln_parts(y1, eps=LN_EPS):
    mu = jnp.mean(y1, axis=-1, keepdims=True)
    xc = y1 - mu
    rstd = lax.rsqrt(jnp.mean(xc * xc, axis=-1, keepdims=True) + eps)
    return xc * rstd, rstd


def _conf_fwd(proj, cw, cb, lg, lb, gg, tc):
    s = proj.shape[0]
    pad = 32

    def body(ac_ref, gc_ref, ap_ref, gp_ref, cw_ref, cb_ref, lg_ref, lb_ref, gg_ref, yn_ref, y1_ref, ys_ref):
        i = pl.program_id(0)
        tail = ap_ref[tc - pad:tc, :] * _sig(gp_ref[tc - pad:tc, :])
        ys_ref[0:pad, :] = jnp.where(i > 0, tail, 0.0)
        ys_ref[pad:pad + tc, :] = ac_ref[...] * _sig(gc_ref[...])
        y1 = _conv_taps(ys_ref, cw_ref, CONV_K, pad, tc) + cb_ref[...]
        y1_ref[...] = y1
        xh, _ = _ln_parts(y1)
        yl = xh * lg_ref[...] + lb_ref[...]
        yc = yl * _sig(yl)
        yn_ref[...] = (yc * _rsq(yc, NORM_EPS) * gg_ref[...]).astype(BF16)

    cur = lambda c: pl.BlockSpec((tc, W_C), lambda i, c=c: (i, c))
    prev = lambda c: pl.BlockSpec((tc, W_C), lambda i, c=c: (jnp.maximum(i - 1, 0), c))
    full = lambda a: pl.BlockSpec(a.shape, lambda i: (0,) * a.ndim)
    params = [cw, cb, lg, lb, gg]
    out = pl.BlockSpec((tc, W_C), lambda i: (i, 0))
    return pl.pallas_call(
        body, name="conf_fwd", grid=(s // tc,),
        in_specs=[cur(5), cur(6), prev(5), prev(6)] + [full(a) for a in params],
        out_specs=[out, out], out_shape=[SDS((s, W_C), BF16), SDS((s, W_C), F32)],
        scratch_shapes=[pltpu.VMEM((tc + pad, W_C), F32)],
        compiler_params=_cp("parallel"),
    )(proj, proj, proj, proj, *params)


def _conf_bwd(dy, proj, y1, cw, cb, lg, lb, gg, tc):
    s = proj.shape[0]
    nc = s // tc
    pad = 32

    def body(dy_ref, ac_ref, gc_ref, ap_ref, gp_ref, y1_ref, cw_ref, cb_ref, lg_ref, lb_ref, gg_ref,
             dp_ref, dcw_ref, dcb_ref, dlg_ref, dlb_ref, dgg_ref, ys_ref, ds_ref, nx_ref):
        step = pl.program_id(0)
        i = nc - 1 - step
        first = step == 0

        @pl.when(first)
        def _():
            nx_ref[...] = jnp.zeros_like(nx_ref)

        a = ac_ref[...]
        sg = _sig(gc_ref[...])
        tail = ap_ref[tc - pad:tc, :] * _sig(gp_ref[tc - pad:tc, :])
        ys_ref[0:pad, :] = jnp.where(i > 0, tail, 0.0)
        ys_ref[pad:pad + tc, :] = a * sg
        xh, rstd = _ln_parts(y1_ref[...])
        yl = xh * lg_ref[...] + lb_ref[...]
        sl = _sig(yl)
        yc = yl * sl
        dyc, dggr = _rms_bwd_rows(yc, gg_ref[...], dy_ref[...])
        _acc(dgg_ref, first, jnp.sum(dggr, axis=0, keepdims=True))
        dyl = dyc * sl * (1.0 + yl * (1.0 - sl))
        _acc(dlg_ref, first, jnp.sum(dyl * xh, axis=0, keepdims=True))
        _acc(dlb_ref, first, jnp.sum(dyl, axis=0, keepdims=True))
        dxh = dyl * lg_ref[...]
        dy1 = rstd * (dxh - jnp.mean(dxh, axis=-1, keepdims=True) - xh * jnp.mean(dxh * xh, axis=-1, keepdims=True))
        _acc(dcb_ref, first, jnp.sum(dy1, axis=0, keepdims=True))
        r32 = _row_iota((32, W_C))
        dcw = jnp.zeros((32, W_C), F32)
        for j in range(CONV_K):
            tap = jnp.sum(dy1 * ys_ref[pl.ds(pad - (CONV_K - 1) + j, tc), :], axis=0, keepdims=True)
            dcw = dcw + jnp.where(r32 == j, tap, 0.0)
        _acc(dcw_ref, first, dcw)
        ds_ref[0:tc, :] = dy1
        ds_ref[tc:tc + pad, :] = nx_ref[...]
        dy0 = None
        for j in range(CONV_K):
            term = cw_ref[j:j + 1, :] * ds_ref[pl.ds(CONV_K - 1 - j, tc), :]
            dy0 = term if dy0 is None else dy0 + term
        dp_ref[:, 0:W_C] = dy0 * sg
        dp_ref[:, W_C:2 * W_C] = dy0 * a * sg * (1.0 - sg)
        nx_ref[...] = dy1[0:pad, :]

    rev = lambda c: pl.BlockSpec((tc, W_C), lambda t, c=c: (nc - 1 - t, c))
    prev = lambda c: pl.BlockSpec((tc, W_C), lambda t, c=c: (jnp.maximum(nc - 2 - t, 0), c))
    full = lambda a: pl.BlockSpec(a.shape, lambda t: (0,) * a.ndim)
    params = [cw, cb, lg, lb, gg]
    vec = SDS((1, W_C), F32)
    outs = [SDS((s, 2 * W_C), F32), SDS((32, W_C), F32), vec, vec, vec, vec]
    return pl.pallas_call(
        body, name="conf_bwd", grid=(nc,),
        in_specs=[rev(3), rev(5), rev(6), prev(5), prev(6), rev(0)] + [full(a) for a in params],
        out_specs=[pl.BlockSpec((tc, 2 * W_C), lambda t: (nc - 1 - t, 0))]
        + [pl.BlockSpec(o.shape, lambda t: (0, 0)) for o in outs[1:]],
        out_shape=outs,
        scratch_shapes=[pltpu.VMEM((tc + pad, W_C), F32), pltpu.VMEM((tc + pad, W_C), F32), pltpu.VMEM((pad, W_C), F32)],
        compiler_params=_cp("arbitrary"),
    )(dy, proj, proj, proj, proj, y1, *params)


def _assemble_dproj(dlru, dq, dcur, dprev, dconf):
    s = dq.shape[0]
    nb = s // BLK

    def body(dl_ref, dq_ref, dc_ref, dn_ref, df_ref, o_ref):
        n = pl.program_id(0)
        o_ref[:, 0:512] = dl_ref[...].astype(BF16)
        o_ref[:, 512:1024] = dq_ref[...].astype(BF16)
        o_ref[:, 1024:1280] = (dc_ref[...] + jnp.where(n < nb - 1, dn_ref[...], 0.0)).astype(BF16)
        o_ref[:, 1280:1792] = df_ref[...].astype(BF16)

    wide = pl.BlockSpec((BLK, 512), lambda n: (n, 0))
    return pl.pallas_call(
        body, name="assemble_dproj", grid=(nb,),
        in_specs=[wide, wide, pl.BlockSpec((BLK, 256), lambda n: (n, 0)),
                  pl.BlockSpec((BLK, 256), lambda n: (jnp.minimum(n + 1, nb - 1), 0)), wide],
        out_specs=pl.BlockSpec((BLK, P_IN), lambda n: (n, 0)), out_shape=SDS((s, P_IN), BF16),
        compiler_params=_cp("parallel"),
    )(dlru, dq, dcur, dprev, dconf)


def _loss_grad(y, t, tm):
    s = y.shape[0]

    def body(y_ref, t_ref, dy_ref, l_ref):
        err = y_ref[...] - t_ref[...]
        dy_ref[...] = err * (1.0 / D)
        _acc(l_ref, pl.program_id(0) == 0, jnp.sum(err * err, axis=0, keepdims=True))

    row = pl.BlockSpec((tm, D), lambda i: (i, 0))
    return pl.pallas_call(
        body, name="loss_grad", grid=(s // tm,), in_specs=[row, row],
        out_specs=[row, pl.BlockSpec((1, D), lambda i: (0, 0))],
        out_shape=[SDS((s, D), F32), SDS((1, D), F32)], compiler_params=_cp("arbitrary"),
    )(y, t)


def _block_diag(w):
    rows = [jnp.concatenate([w[h] if k == h else jnp.zeros((64, 64), w.dtype) for k in range(4)], axis=1) for h in range(4)]
    return jnp.concatenate(rows, axis=0)


def _diag_blocks(m):
    return jnp.stack([m[64 * h:64 * (h + 1), 64 * h:64 * (h + 1)] for h in range(4)])


def _layer_params(small, l):
    v = lambda name: small[name][l].reshape(1, -1)
    gg = small["group_g"][l]
    return dict(
        ffn1_pre=v("ffn1_pre_g"), ffn1_post=v("ffn1_post_g"), mix_pre=v("mix_pre_g"), mix_post=v("mix_post_g"),
        ffn2_pre=v("ffn2_pre_g"), ffn2_post=v("ffn2_post_g"), lru_cb=v("lru_conv_b"),
        wa=_block_diag(small["lru_w_a"][l]).astype(BF16), ba=v("lru_b_a"),
        wx=_block_diag(small["lru_w_x"][l]).astype(BF16), bx=v("lru_b_x"), lam=v("lru_lambda"),
        sinks8=jnp.broadcast_to(small["attn_sinks"][l][:, None], (NQ, 128)),
        conv_b=v("conv_b"), ln_g=v("conv_ln_g"), ln_b=v("conv_ln_b"),
        gg_a=gg[0:W_A].reshape(1, -1), gg_b=gg[W_A:W_A + W_B].reshape(1, -1), gg_c=gg[W_A + W_B:].reshape(1, -1),
    )


def _forward_layer(x, weights, p, tiles, deps=()):
    _, mm, _, tc = tiles
    big = dict(weights("ffn1_gu", x))
    p = dict(p)
    sv = dict(x0=x)
    h1, g1, u1, a1 = _ffn_up(x, p["ffn1_pre"], big["ffn1_w_gu"], 0, mm, deps)
    big.update(weights("ffn1_down", a1))
    z1, x = _mm_rms_res(a1, big["ffn1_w_down"], 0, x, p["ffn1_post"], 0.5, mm, FH, "ffn_down")
    sv.update(h1=h1, g1=g1, u1=u1, a1=a1, z1=z1, x1=x)
    big.update(weights("mix", x))
    p.update(lru_cw=big.pop("lru_conv_w"), conv_w=big.pop("conv_w"))
    hn, proj = _proj(x, p["mix_pre"], big["w_in"], 0, mm)
    yn_a, hl = _lru_fwd(proj, p["lru_cw"], p["lru_cb"], p["wa"], p["ba"], p["wx"], p["bx"], p["lam"], p["gg_a"], tc)
    yn_b, ob = _attn_fwd(proj, p["sinks8"], p["gg_b"])
    yn_c, y1 = _conf_fwd(proj, p["conv_w"], p["conv_b"], p["ln_g"], p["ln_b"], p["gg_c"], tc)
    ycat = jnp.concatenate([yn_a, yn_b, yn_c], axis=1)
    zo, x = _mm_rms_res(ycat, big["w_out"], 0, x, p["mix_post"], 1.0, mm, D, "mix_out")
    sv.update(hn=hn, proj=proj, hl=hl, ob=ob, y1=y1, ycat=ycat, zo=zo, x2=x)
    big.update(weights("ffn2", x))
    h2, g2, u2, a2 = _ffn_up(x, p["ffn2_pre"], big["ffn2_w_gu"], 0, mm)
    z2, x = _mm_rms_res(a2, big["ffn2_w_down"], 0, x, p["ffn2_post"], 0.5, mm, FH, "ffn_down")
    sv.update(h2=h2, g2=g2, u2=u2, a2=a2, z2=z2, p=p, big=big)
    return x, sv


def _grad_buffers():
    empty = lambda *shape: lax.empty(shape, F32)
    return dict(ffn1_w_gu=empty(1, NSHARD, D, FH), ffn2_w_gu=empty(1, NSHARD, D, FH), ffn1_w_down=empty(1, 1, DFF, D),
                ffn2_w_down=empty(1, 1, DFF, D), w_in=empty(1, 1, D, P_IN), w_out=empty(1, 1, D, D))


def _backward_layer(dx, sv, bufs, tiles, stage):
    p, big = sv["p"], sv["big"]
    tm, mm, dw, tc = tiles
    gr = {}

    def ffn_bwd(dx, which, xin, h, g, u, a, z, pre, post, deps):
        dz, dpost = _rms_bwd(dx, z, post, 0.5, tm, "ffn_post_bwd", deps)
        dg, du = _ffn_bwd_mid(dz, big[which + "_w_down"], 0, g, u, mm)
        bufs[which + "_w_down"] = _mm_tn_into(bufs[which + "_w_down"], a, dz, 0, 0, FH, D, dw, "dw_down")
        bufs[which + "_w_gu"] = _mm_tn_into(bufs[which + "_w_gu"], h, dg, 0, 0, D, FH, dw, "dw_gate")
        bufs[which + "_w_gu"] = _mm_tn_into(bufs[which + "_w_gu"], h, du, 0, 2, D, FH, dw, "dw_up")
        deps = stage({n: bufs[n] for n in (which + "_w_gu", which + "_w_down")}, bufs[which + "_w_gu"])
        dxn, dpre = _ffn_bwd_dh(dg, du, big[which + "_w_gu"], 0, xin, pre, dx, mm, deps)
        return dxn, dpre, dpost

    dx, gr["ffn2_pre_g"], gr["ffn2_post_g"] = ffn_bwd(dx, "ffn2", sv["x2"], sv["h2"], sv["g2"], sv["u2"], sv["a2"],
                                                      sv["z2"], p["ffn2_pre"], p["ffn2_post"], ())
    do, gr["mix_post_g"] = _rms_bwd(dx, sv["zo"], p["mix_post"], 1.0, tm, "mix_post_bwd")
    bufs["w_out"] = _mm_tn_into(bufs["w_out"], sv["ycat"], do, 0, 0, D, D, dw, "dw_out")
    dy = _mm_nt(do, big["w_out"], 0, mm, "mix_dy")
    proj = sv["proj"]
    (dlru, dcw, gr["lru_conv_b"], dwa, gr["lru_b_a"], dwx, gr["lru_b_x"], gr["lru_lambda"], dgg_a) = _lru_bwd(
        dy, proj, sv["hl"], p["lru_cw"], p["lru_cb"], p["wa"], p["ba"], p["wx"], p["bx"], p["lam"], p["gg_a"], tc)
    dq, dcur, dprev, dsk, dgg_b = _attn_bwd(dy, proj, sv["ob"], p["sinks8"], p["gg_b"])
    dconf, dconvw, gr["conv_b"], gr["conv_ln_g"], gr["conv_ln_b"], dgg_c = _conf_bwd(
        dy, proj, sv["y1"], p["conv_w"], p["conv_b"], p["ln_g"], p["ln_b"], p["gg_c"], tc)
    dproj = _assemble_dproj(dlru, dq, dcur, dprev, dconf)
    bufs["w_in"] = _mm_tn_into(bufs["w_in"], sv["hn"], dproj, 0, 0, D, P_IN, dw, "dw_in")
    dx, gr["mix_pre_g"] = _mm_nt_rmsbwd(dproj, big["w_in"], 0, sv["x1"], p["mix_pre"], dx, mm)
    gr["lru_conv_w"] = dcw[0:LRU_K]
    gr["lru_w_a"] = _diag_blocks(dwa)
    gr["lru_w_x"] = _diag_blocks(dwx)
    gr["attn_sinks"] = dsk[:, 0]
    gr["conv_w"] = dconvw[0:CONV_K]
    gr["group_g"] = jnp.concatenate([dgg_a, dgg_b, dgg_c], axis=1)
    dx, gr["ffn1_pre_g"], gr["ffn1_post_g"] = ffn_bwd(dx, "ffn1", sv["x0"], sv["h1"], sv["g1"], sv["u1"], sv["a1"],
                                                      sv["z1"], p["ffn1_pre"], p["ffn1_post"],
                                                      stage({n: bufs[n] for n in ("w_in", "w_out")}, dx))
    return dx, gr


def _tiles(s):
    return min(512, s), min(1024, s), min(2048, s), min(512, s // 2)


HBM_SPEC = pl.BlockSpec(memory_space=pltpu.HBM)
SEM_SPEC = pl.BlockSpec(memory_space=pltpu.SEMAPHORE)
EFFECT = pltpu.SideEffectType.DATAFLOW_SIDE_EFFECTING


def _place():
    x, y, c = lax.axis_index("x"), lax.axis_index("y"), lax.axis_index("c")
    return x, y, c, [(1 - x, y), (x, 1 - y), (1 - x, 1 - y)]


def _rcopy(src, dst, send_sems, recv_sems, k, to):
    return pltpu.make_async_remote_copy(src_ref=src, dst_ref=dst, send_sem=send_sems.at[k], recv_sem=recv_sems.at[k],
                                        device_id=to, device_id_type=MESH)


def _half(rows, which):
    return pl.ds(which * (rows // 2), rows // 2)


def _place_shard(w, l, p_idx, dtype):
    _, rows, cols = w.shape
    tr = _rows_per_block(rows, cols, 16) if rows % 16 == 0 else rows

    def body(p_ref, buf_ref, w_ref, o_ref):
        o_ref[...] = w_ref[...].astype(dtype)

    spec = pltpu.PrefetchScalarGridSpec(
        num_scalar_prefetch=1, grid=(rows // tr,),
        in_specs=[ANY, pl.BlockSpec((None, tr, cols), lambda i, pr: (l, i, 0))],
        out_specs=pl.BlockSpec((None, None, tr, cols), lambda i, pr: (0, pr[0], i, 0)))
    shape = (1, NSHARD, rows, cols)
    return pl.pallas_call(body, name="place_shard", grid_spec=spec, out_shape=SDS(shape, dtype),
                          input_output_aliases={1: 0}, compiler_params=_cp("parallel"),
                          )(p_idx, lax.empty(shape, dtype), w)


def _gather_two_level(bufs, n_halved):
    n = len(bufs)

    def body(*refs):
        outs = refs[n:2 * n]
        send_sems, recv_sems = refs[2 * n:]
        x, y, c, chips = _place()
        p = 2 * x + y
        me, sibling = (x, y, c), (x, y, 1 - c)

        def blk(a, q, half):
            return outs[a].at[0, q, _half(outs[a].shape[2], half)] if a < n_halved else outs[a].at[0, q]

        def cp(a, k, q, half, to):
            return _rcopy(blk(a, q, half), blk(a, q, half), send_sems, recv_sems, 6 * a + k, to)

        first = [cp(a, j, p, c, (*chip, c)) for a in range(n) for j, chip in enumerate(chips)]
        for d in first:
            d.start()
        passed = []
        for a in range(n):
            for j, chip in enumerate(chips):
                q = 2 * chip[0] + chip[1]
                cp(a, j, q, c, me).wait_recv()
                if a < n_halved:
                    passed.append(cp(a, 3 + j, q, c, sibling))
                    passed[-1].start()
        for a in range(n_halved):
            for j, chip in enumerate(chips):
                cp(a, 3 + j, 2 * chip[0] + chip[1], 1 - c, me).wait_recv()
        for d in first + passed:
            d.wait_send()

    return pl.pallas_call(
        body, name="gather_layer0", in_specs=[ANY] * n, out_specs=[ANY] * n,
        out_shape=[SDS(b.shape, b.dtype) for b in bufs], input_output_aliases={a: a for a in range(n)},
        scratch_shapes=[pltpu.SemaphoreType.DMA((6 * n,)), pltpu.SemaphoreType.DMA((6 * n,))],
    )(*bufs)


def _run_plans(plans, refs, send_sems, recv_sems):
    cps, b0, s0 = [], 0, 0
    for plan, nb, ns in plans:
        cps += plan(refs[b0:b0 + nb], send_sems, recv_sems, s0)
        b0, s0 = b0 + nb, s0 + ns
    return cps


def _exchange(name, bufs, plans):
    n = len(bufs)
    nsem = sum(ns for _, _, ns in plans)

    def body(*refs):
        cps = _run_plans(plans, refs[n:2 * n], refs[2 * n], refs[2 * n + 1])
        for cp in cps:
            cp.start()
        for cp in cps:
            cp.wait()

    return pl.pallas_call(
        body, name=name, in_specs=[ANY] * n, out_specs=[ANY] * n, out_shape=[SDS(b.shape, b.dtype) for b in bufs],
        input_output_aliases={a: a for a in range(n)},
        scratch_shapes=[pltpu.SemaphoreType.DMA((nsem,)), pltpu.SemaphoreType.DMA((nsem,))],
    )(*bufs)


def _exchange_start(name, bufs, plans, deps=()):
    n = len(bufs)
    nsem = sum(ns for _, _, ns in plans)
    deps = list(deps)
    first_out = n + len(deps)

    def body(*refs):
        for cp in _run_plans(plans, refs[:n], refs[first_out], refs[first_out + 1]):
            cp.start()
        token = refs[first_out + 2 + n]
        token[...] = jnp.zeros_like(token)

    outs = pl.pallas_call(
        body, name=name,
        out_shape=(pltpu.SemaphoreType.DMA((nsem,)), pltpu.SemaphoreType.DMA((nsem,)),
                   *[pltpu.HBM(b.shape, b.dtype) for b in bufs], SDS((8, 128), F32)),
        in_specs=[HBM_SPEC] * n + [ANY] * len(deps),
        out_specs=(SEM_SPEC, SEM_SPEC, *[HBM_SPEC] * n, pl.BlockSpec(memory_space=pltpu.VMEM)),
        input_output_aliases={a: 2 + a for a in range(n)},
        compiler_params=pltpu.CompilerParams(has_side_effects=EFFECT),
    )(*[pltpu.with_memory_space_constraint(b, pltpu.HBM) for b in bufs], *deps)
    return outs[0], outs[1], list(outs[2:2 + n]), outs[2 + n]


def _exchange_wait(name, send_sems, recv_sems, bufs, plans, after):
    n = len(bufs)

    def body(*refs):
        for cp in _run_plans(plans, refs[:n], refs[n], refs[n + 1]):
            cp.wait_send()
            cp.wait_recv()

    return pl.pallas_call(
        body, name=name, out_shape=[pltpu.HBM(b.shape, b.dtype) for b in bufs],
        in_specs=[HBM_SPEC] * n + [SEM_SPEC, SEM_SPEC, ANY], out_specs=[HBM_SPEC] * n,
        input_output_aliases={a: a for a in range(n)},
        compiler_params=pltpu.CompilerParams(has_side_effects=EFFECT),
    )(*bufs, send_sems, recv_sems, after)


def _plan_gather(refs, send_sems, recv_sems, base):
    x, y, c, chips = _place()
    p = 2 * x + y
    return [_rcopy(r.at[0, p], r.at[0, p], send_sems, recv_sems, base + 3 * a + j, (*chip, c))
            for a, r in enumerate(refs) for j, chip in enumerate(chips)]


def _plan_pair_exchange(refs, send_sems, recv_sems, base):
    x, y, c, _ = _place()
    n = len(refs) // 2
    return [_rcopy(refs[a].at[:, _half(refs[a].shape[1], 1 - c)], refs[n + a], send_sems, recv_sems, base + a,
                   (x, y, 1 - c)) for a in range(n)]


def _plan_chip_exchange(refs, send_sems, recv_sems, base):
    x, y, c, chips = _place()
    n = len(refs) // 2
    return [_rcopy(refs[a].at[2 * chip[0] + chip[1]], refs[n + a].at[j], send_sems, recv_sems, base + 3 * a + j,
                   (*chip, c)) for a in range(n) for j, chip in enumerate(chips)]


def _plan_pair_share(refs, send_sems, recv_sems, base):
    x, y, c, _ = _place()
    return [_rcopy(r.at[_half(r.shape[0], c)], r.at[_half(r.shape[0], c)], send_sems, recv_sems, base + a,
                   (x, y, 1 - c)) for a, r in enumerate(refs)]


def _plan_small_gather(refs, send_sems, recv_sems, base):
    x, y, c, _ = _place()
    me = 4 * x + 2 * y + c
    cps = []
    for m in range(1, NDEV):
        peer = (1 - x if m & 4 else x, 1 - y if m & 2 else y, 1 - c if m & 1 else c)
        cps.append(_rcopy(refs[0], refs[1].at[me], send_sems, recv_sems, base + m - 1, peer))
    return cps


def _sum_small(buf, gathered):
    def body(buf_ref, g_ref, o_ref):
        x, y, c, _ = _place()
        me = 4 * x + 2 * y + c
        total = jnp.where(me == 0, buf_ref[...], g_ref[0])
        for dev in range(1, NDEV):
            total = total + jnp.where(me == dev, buf_ref[...], g_ref[dev])
        o_ref[...] = total

    vm = pl.BlockSpec(memory_space=pltpu.VMEM)
    return pl.pallas_call(body, name="sum_small", in_specs=[vm, vm], out_specs=vm, out_shape=SDS(buf.shape, F32),
                          compiler_params=pltpu.CompilerParams(vmem_limit_bytes=VMEM_LIMIT))(buf, gathered)


BLOCK_ELEMS = 256 * 1024


def _rows_per_block(rows, cols, mult):
    best = None
    for tr in range(mult, rows + 1, mult):
        if rows % tr == 0 and tr * cols <= BLOCK_ELEMS:
            best = tr
    assert best is not None, (rows, cols)
    return best


def _pair_sum(g, r, c_idx):
    nq, rows, cols = g.shape
    half = rows // 2
    tr = _rows_per_block(half, cols, 16)
    nb = half // tr

    def body(c_ref, g_ref, r_ref, t_ref):
        t_ref[...] = (g_ref[...] + r_ref[...]).astype(BF16)

    blk = pl.BlockSpec((None, tr, cols), lambda q, i, cr: (q, i, 0))
    spec = pltpu.PrefetchScalarGridSpec(
        num_scalar_prefetch=1, grid=(nq, nb),
        in_specs=[pl.BlockSpec((None, tr, cols), lambda q, i, cr: (q, cr[0] * nb + i, 0)), blk], out_specs=blk)
    return pl.pallas_call(body, name="grad_pair_sum", grid_spec=spec, out_shape=SDS((nq, half, cols), BF16),
                          compiler_params=_cp("parallel", "parallel"))(c_idx, g, r)


def _chip_sum(g, r, rr, cp_idx):
    _, rows, cols = g.shape
    half = rows // 2
    tr = _rows_per_block(half, cols, 16)
    nb = half // tr

    def body(cp_ref, buf_ref, g_ref, r_ref, rr_ref, o_ref):
        o_ref[...] = ((g_ref[...] + r_ref[...]) + rr_ref[0].astype(F32) + rr_ref[1].astype(F32) + rr_ref[2].astype(F32))

    spec = pltpu.PrefetchScalarGridSpec(
        num_scalar_prefetch=1, grid=(nb,),
        in_specs=[ANY, pl.BlockSpec((None, tr, cols), lambda i, cp: (cp[1], cp[0] * nb + i, 0)),
                  pl.BlockSpec((None, tr, cols), lambda i, cp: (cp[1], i, 0)),
                  pl.BlockSpec((3, tr, cols), lambda i, cp: (0, i, 0))],
        out_specs=pl.BlockSpec((tr, cols), lambda i, cp: (cp[0] * nb + i, 0)))
    return pl.pallas_call(body, name="grad_chip_sum", grid_spec=spec, out_shape=SDS((rows, cols), F32),
                          input_output_aliases={1: 0}, compiler_params=_cp("parallel"),
                          )(cp_idx, lax.empty((rows, cols), F32), g, r, rr)


def _adamw_math(w, g, m, v):
    mn = ADAM_B1 * m + (1.0 - ADAM_B1) * g
    vn = ADAM_B2 * v + (1.0 - ADAM_B2) * (g * g)
    m_hat = mn / (1.0 - ADAM_B1 ** ADAM_STEP)
    v_hat = vn / (1.0 - ADAM_B2 ** ADAM_STEP)
    return -ADAM_LR * (m_hat / (jnp.sqrt(v_hat) + ADAM_EPS) + ADAM_WD * w), mn, vn


def _adamw_layers(w, gs, m, v):
    depth, rows, cols = w.shape
    tr = _rows_per_block(rows, cols, 8)

    def body(w_ref, g0_ref, g1_ref, m_ref, v_ref, go_ref, d_ref, mo_ref, vo_ref):
        gg = jnp.where(pl.program_id(0) == 0, g0_ref[...], g1_ref[...])
        go_ref[...] = gg
        d_ref[...], mo_ref[...], vo_ref[...] = _adamw_math(w_ref[...], gg, m_ref[...], v_ref[...])

    blk = pl.BlockSpec((None, tr, cols), lambda l, i: (l, i, 0))
    return pl.pallas_call(
        body, name="adamw_layers", grid=(depth, rows // tr),
        in_specs=[blk, pl.BlockSpec((tr, cols), lambda l, i: (i * (1 - l), 0)),
                  pl.BlockSpec((tr, cols), lambda l, i: (i * l, 0)), blk, blk],
        out_specs=[blk] * 4, out_shape=[SDS(w.shape, F32)] * 4,
        compiler_params=_cp("arbitrary", "arbitrary"))(w, gs[0], gs[1], m, v)


def _adamw_small(ws, gs, ms, vs):
    n = len(ws)

    def body(*refs):
        w, g, m, v, d_out, m_out, v_out = (refs[k * n:(k + 1) * n] for k in range(7))
        for k in range(n):
            d_out[k][...], m_out[k][...], v_out[k][...] = _adamw_math(w[k][...], g[k][...], m[k][...], v[k][...])

    vm = pl.BlockSpec(memory_space=pltpu.VMEM)
    outs = pl.pallas_call(body, name="adamw_small", in_specs=[vm] * (4 * n), out_specs=[vm] * (3 * n),
                          out_shape=[SDS(w.shape, F32) for w in ws] * 3,
                          compiler_params=pltpu.CompilerParams(vmem_limit_bytes=VMEM_LIMIT))(*ws, *gs, *ms, *vs)
    return outs[:n], outs[n:2 * n], outs[2 * n:]


_WEIGHTS = ["ffn1_pre_g", "ffn1_w_gu", "ffn1_w_down", "ffn1_post_g", "mix_pre_g", "w_in", "lru_conv_w", "lru_conv_b",
            "lru_w_a", "lru_b_a", "lru_w_x", "lru_b_x", "lru_lambda", "attn_sinks", "conv_w", "conv_b", "conv_ln_g",
            "conv_ln_b", "group_g", "w_out", "mix_post_g", "ffn2_pre_g", "ffn2_w_gu", "ffn2_w_down", "ffn2_post_g"]
_INPUTS = ["x"] + _WEIGHTS + ["loss_target"] + ["m_" + n for n in _WEIGHTS] + ["v_" + n for n in _WEIGHTS]
_BIG = ["ffn1_w_gu", "ffn1_w_down", "w_in", "w_out", "ffn2_w_gu", "ffn2_w_down"]
_SMALL_SHARDED = ["lru_conv_w", "conv_w"]
_SMALL_REPL = [n for n in _WEIGHTS if n not in _BIG and n not in _SMALL_SHARDED]

PACK_TILE = 8 * 128


def _pack(arrs):
    parts = []
    for a in arrs:
        flat = a.reshape(-1)
        parts.append(jnp.pad(flat, (0, -flat.shape[0] % PACK_TILE)).reshape(-1, 128))
    return jnp.concatenate(parts, axis=0)


def _unpack(buf, shapes):
    out, row = [], 0
    for shp in shapes:
        size = math.prod(shp)
        nrow = -(-size // PACK_TILE) * 8
        out.append(buf[row:row + nrow].reshape(-1)[:size].reshape(shp))
        row += nrow
    return out


def _unshard_cols(a):
    return a.transpose(0, 2, 1, 3).reshape(1, a.shape[2], NSHARD * a.shape[3])


_GROUPS = dict(ffn1_gu=["ffn1_w_gu"], ffn1_down=["ffn1_w_down"], mix=["w_in", "w_out", "lru_conv_w", "conv_w"],
               ffn2=["ffn2_w_gu", "ffn2_w_down"])


def _full_weights(group, gathered):
    g = dict(zip(_GROUPS[group], gathered))
    if group == "mix":
        return dict(w_in=_unshard_cols(g["w_in"]), w_out=g["w_out"].reshape(1, D, D),
                    lru_conv_w=_unshard_cols(g["lru_conv_w"])[0], conv_w=_unshard_cols(g["conv_w"])[0])
    return {n: (a.reshape(1, DFF, D) if n.endswith("w_down") else a) for n, a in g.items()}


def _by_shard(name, buf):
    if name.endswith("w_gu"):
        return buf[0]
    if name == "w_in":
        return buf.reshape(D, NSHARD, P_IN // NSHARD).transpose(1, 0, 2)
    return buf.reshape(NSHARD, buf.shape[2] // NSHARD, buf.shape[3])


class _Reducer:
    PLANS = (_plan_pair_exchange, _plan_chip_exchange, _plan_pair_share)

    def __init__(self, keys, gs, c_idx, cp_idx):
        self.keys, self.gs, self.c_idx, self.cp_idx = keys, gs, c_idx, cp_idx
        self.n = len(gs)
        self.step = 0
        self.result = None

    def inputs(self):
        n = self.n
        if self.step == 0:
            bufs = self.gs + [lax.empty((NSHARD, g.shape[1] // 2, g.shape[2]), F32) for g in self.gs]
        elif self.step == 1:
            ts = [_pair_sum(g, r, self.c_idx) for g, r in zip(self.gs, self.rs)]
            bufs = ts + [lax.empty((3,) + t.shape[1:], BF16) for t in ts]
        else:
            bufs = [_chip_sum(g, r, rr, self.cp_idx) for g, r, rr in zip(self.gs, self.rs, self.rrs)]
        return bufs, (self.PLANS[self.step], len(bufs), (n, 3 * n, n)[self.step])

    def absorb(self, done):
        n = self.n
        if self.step == 0:
            self.gs, self.rs = done[:n], done[n:]
        elif self.step == 1:
            self.rrs = done[n:]
        else:
            self.result = dict(zip(self.keys, done))
        self.step += 1


class _SmallGather:
    def __init__(self, buf):
        self.buf, self.step, self.result, self.gathered = buf, 0, {}, None

    def inputs(self):
        return [self.buf, jnp.zeros((NDEV,) + self.buf.shape, F32)], (_plan_small_gather, 2, NDEV - 1)

    def absorb(self, done):
        self.buf, self.gathered = done
        self.step = 3


class _ReducePipeline:
    def __init__(self, c_idx, cp_idx):
        self.c_idx, self.cp_idx = c_idx, cp_idx
        self.reducers, self.flying, self.calls = [], None, 0

    def add(self, layer, done):
        if done:
            keys = [(layer, n) for n in done]
            self.reducers.append(_Reducer(keys, [_by_shard(n, b) for n, b in done.items()], self.c_idx, self.cp_idx))

    def _next(self):
        active = [r for r in self.reducers if r.step < 3]
        bufs, plans = [], []
        for r in active:
            b, triple = r.inputs()
            bufs += b
            plans.append(triple)
        self.calls += 1
        return active, bufs, plans, "grad_exchange%d" % self.calls

    def _absorb(self, active, plans, done):
        at = 0
        for r, (_, nb, _) in zip(active, plans):
            r.absorb(done[at:at + nb])
            at += nb

    def _land(self, after):
        if self.flying is not None:
            active, plans, name, send_sems, recv_sems, bufs = self.flying
            self._absorb(active, plans, _exchange_wait(name + "_wait", send_sems, recv_sems, bufs, plans, after))
            self.flying = None

    def hook(self, after):
        self._land(after)
        active, bufs, plans, name = self._next()
        if not active:
            return []
        send_sems, recv_sems, bufs, token = _exchange_start(name + "_start", bufs, plans)
        self.flying = (active, plans, name, send_sems, recv_sems, bufs)
        return [token]

    def finish(self, after):
        self._land(after)
        while True:
            active, bufs, plans, name = self._next()
            if not active:
                break
            self._absorb(active, plans, _exchange(name, bufs, plans))
        out = {}
        for r in self.reducers:
            out.update(r.result)
        return out


def kernel(*args):
    d = dict(zip(_INPUTS, args, strict=True))
    xi, yi, ci = lax.axis_index("x"), lax.axis_index("y"), lax.axis_index("c")
    p = 2 * xi + yi
    c_idx = jnp.reshape(ci, (1,)).astype(jnp.int32)
    p_idx = jnp.reshape(p, (1,)).astype(jnp.int32)
    cp_idx = jnp.stack([ci, p]).astype(jnp.int32)
    x, target = d["x"][0], d["loss_target"][0]
    tiles = _tiles(x.shape[0])

    groups = [(l, grp) for l in range(DEPTH) for grp in _GROUPS]
    placed = {(l, grp): [_place_shard(d[n], l, p_idx, BF16 if n in _BIG else F32) for n in _GROUPS[grp]]
              for l, grp in groups}
    ready = {groups[0]: _gather_two_level(placed[groups[0]], len(placed[groups[0]]))}
    flying, tokens = {}, [ready[groups[0]][0]]
    for l, grp in groups[1:]:
        plans = [(_plan_gather, len(placed[l, grp]), 3 * len(placed[l, grp]))]
        send_sems, recv_sems, bufs, token = _exchange_start("gather_l%d_%s_start" % (l, grp), placed[l, grp], plans,
                                                             tokens[-1:])
        flying[l, grp] = (send_sems, recv_sems, bufs, plans)
        tokens.append(token)

    def weights_of(l):
        def weights(grp, after):
            if (l, grp) not in ready:
                send_sems, recv_sems, bufs, plans = flying[l, grp]
                ready[l, grp] = _exchange_wait("gather_l%d_%s_wait" % (l, grp), send_sems, recv_sems, bufs, plans, after)
            return _full_weights(grp, ready[l, grp])
        return weights

    small = {n: d[n] for n in _SMALL_REPL}
    x1, sv0 = _forward_layer(x, weights_of(0), _layer_params(small, 0), tiles, tokens[1:])
    x2, sv1 = _forward_layer(x1, weights_of(1), _layer_params(small, 1), tiles)
    dx, lcols = _loss_grad(x2, target, tiles[0])

    pipe = _ReducePipeline(c_idx, cp_idx)
    sgrads = [None] * DEPTH
    for l, sv in ((1, sv1), (0, sv0)):
        bufs = _grad_buffers()

        def stage(done, dx, l=l):
            pipe.add(l, done)
            return pipe.hook(dx)

        dx, sgrads[l] = _backward_layer(dx, sv, bufs, tiles, stage)
    grad_x = dx

    stacked = {n: jnp.stack([sgrads[l][n].reshape(d[n].shape[1:]) for l in range(DEPTH)]) for n in _SMALL_REPL}
    for n in _SMALL_SHARDED:
        stacked[n] = jnp.stack([sgrads[l][n] for l in range(DEPTH)])
    loss_part = jnp.pad((0.5 / D) * jnp.sum(lcols).reshape(1), (0, 127))
    order = _SMALL_REPL + _SMALL_SHARDED
    small_gather = _SmallGather(_pack([loss_part] + [stacked[n] for n in order]))
    pipe.reducers.append(small_gather)
    reduced = pipe.finish(grad_x)
    summed = _unpack(_sum_small(small_gather.buf, small_gather.gathered), [(128,)] + [stacked[n].shape for n in order])
    loss = summed[0][0]
    grads = {}
    for n, g in zip(order, summed[1:]):
        if n in _SMALL_SHARDED:
            g = lax.dynamic_slice_in_dim(g, p * (g.shape[2] // NSHARD), g.shape[2] // NSHARD, axis=2)
        grads[n] = g

    delta, new_m, new_v = {}, {}, {}
    for n in _BIG:
        grads[n], delta[n], new_m[n], new_v[n] = _adamw_layers(d[n], [reduced[l, n] for l in range(DEPTH)],
                                                                d["m_" + n], d["v_" + n])
    small_out = _adamw_small([d[n] for n in order], [grads[n] for n in order], [d["m_" + n] for n in order],
                             [d["v_" + n] for n in order])
    for out, res in zip((delta, new_m, new_v), small_out):
        out.update(zip(order, res))

    return (loss, grad_x[None], *[grads[n] for n in _WEIGHTS], *[delta[n] for n in _WEIGHTS],
            *[new_m[n] for n in _WEIGHTS], *[new_v[n] for n in _WEIGHTS])
```

```python
import functools
import math

import jax
import jax.numpy as jnp
from jax import lax
from jax.experimental import pallas as pl
from jax.experimental.pallas import tpu as pltpu

F32 = jnp.float32
BF16 = jnp.bfloat16
SDS = jax.ShapeDtypeStruct

D = 1024
DFF = 2816
FH = DFF // 2
DEPTH = 2
W_A = 256
W_B = 512
W_C = 256
NQ = 8
HD = 64
BLK = 128
ATT_NB_FWD = 1
ATT_NB_BWD = 4
P_IN = 1792
LRU_K = 4
CONV_K = 31
LRU_C = 8.0
NORM_EPS = 1e-6
LN_EPS = 1e-5
NEG_BIG = -1e30
SCALE = 1.0 / math.sqrt(HD)

ADAM_LR = 0.001
ADAM_B1 = 0.9
ADAM_B2 = 0.999
ADAM_EPS = 1e-08
ADAM_WD = 0.01
ADAM_STEP = 10

VMEM_LIMIT = 60 * 1024 * 1024
NSHARD = 4
NDEV = 8

TN = (((0,), (0,)), ((), ()))
NT = (((1,), (1,)), ((), ()))

MESH = pl.DeviceIdType.MESH
ANY = pl.BlockSpec(memory_space=pl.ANY)


def _cp(*sem):
    return pltpu.CompilerParams(dimension_semantics=sem if sem else None, vmem_limit_bytes=VMEM_LIMIT)


def _rsq(x, eps):
    return lax.rsqrt(jnp.mean(x * x, axis=-1, keepdims=True) + eps)


def _rms_bwd_rows(x, g, dy):
    r = _rsq(x, NORM_EPS)
    xh = x * r
    dyg = dy * g
    dx = r * (dyg - xh * jnp.mean(dyg * xh, axis=-1, keepdims=True))
    return dx, dy * xh


def _sig(x):
    return jax.nn.sigmoid(x)


def _ffn_up(x, pre_g, wgu, l, tm, deps=()):
    s = x.shape[0]
    deps = list(deps)

    def body(x_ref, g_ref, wg_ref, wu_ref, *rest):
        h_ref, go_ref, uo_ref, a_ref = rest[len(deps):]

        @pl.when(pl.program_id(1) == 0)
        def _():
            xf = x_ref[...]
            h_ref[...] = (xf * _rsq(xf, NORM_EPS) * g_ref[...]).astype(BF16)

        h = h_ref[...]
        gg = jnp.dot(h, wg_ref[...], preferred_element_type=F32)
        uu = jnp.dot(h, wu_ref[...], preferred_element_type=F32)
        sg = _sig(gg)
        silu = gg * sg
        go_ref[...] = (uu * (sg * (1.0 + gg * (1.0 - sg)))).astype(BF16)
        uo_ref[...] = silu.astype(BF16)
        a_ref[...] = (silu * uu).astype(BF16)

    wide = pl.BlockSpec((tm, FH), lambda i, j: (i, j))
    return pl.pallas_call(
        body, name="ffn_up", grid=(s // tm, 2),
        in_specs=[pl.BlockSpec((tm, D), lambda i, j: (i, 0)), pl.BlockSpec((1, D), lambda i, j: (0, 0)),
                  pl.BlockSpec((None, None, D, FH), lambda i, j: (l, j, 0, 0)),
                  pl.BlockSpec((None, None, D, FH), lambda i, j: (l, j + 2, 0, 0))] + [ANY] * len(deps),
        out_specs=[pl.BlockSpec((tm, D), lambda i, j: (i, 0)), wide, wide, wide],
        out_shape=[SDS((s, D), BF16), SDS((s, DFF), BF16), SDS((s, DFF), BF16), SDS((s, DFF), BF16)],
        compiler_params=_cp("parallel", "arbitrary"),
    )(x, pre_g, wgu, wgu, *deps)


def _mm_rms_res(a, w, l, x, g, c, tm, tk, name):
    s, k_dim = a.shape
    nk = k_dim // tk

    def body(a_ref, w_ref, x_ref, g_ref, z_ref, x1_ref):
        k = pl.program_id(1)
        p = jnp.dot(a_ref[...], w_ref[...], preferred_element_type=F32)

        @pl.when(k == 0)
        def _():
            z_ref[...] = p

        @pl.when(k > 0)
        def _():
            z_ref[...] += p

        @pl.when(k == nk - 1)
        def _():
            z = z_ref[...]
            x1_ref[...] = x_ref[...] + c * (z * _rsq(z, NORM_EPS) * g_ref[...])

    row = pl.BlockSpec((tm, D), lambda i, k: (i, 0))
    return pl.pallas_call(
        body, name=name, grid=(s // tm, nk),
        in_specs=[pl.BlockSpec((tm, tk), lambda i, k: (i, k)), pl.BlockSpec((None, tk, D), lambda i, k: (l, k, 0)),
                  row, pl.BlockSpec((1, D), lambda i, k: (0, 0))],
        out_specs=[row, row],
        out_shape=[SDS((s, D), F32), SDS((s, D), F32)],
        compiler_params=_cp("parallel", "arbitrary"),
    )(a, w, x, g)


def _rms_bwd(dy, z, g, c, tm, name, deps=()):
    s = z.shape[0]
    deps = list(deps)

    def body(dy_ref, z_ref, g_ref, *rest):
        dz_ref, dg_ref = rest[len(deps):]
        dz, dgr = _rms_bwd_rows(z_ref[...], g_ref[...], c * dy_ref[...])
        dz_ref[...] = dz.astype(BF16)
        part = jnp.sum(dgr, axis=0, keepdims=True)

        @pl.when(pl.program_id(0) == 0)
        def _():
            dg_ref[...] = part

        @pl.when(pl.program_id(0) > 0)
        def _():
            dg_ref[...] += part

    row = pl.BlockSpec((tm, D), lambda i: (i, 0))
    vec = pl.BlockSpec((1, D), lambda i: (0, 0))
    return pl.pallas_call(
        body, name=name, grid=(s // tm,), in_specs=[row, row, vec] + [ANY] * len(deps), out_specs=[row, vec],
        out_shape=[SDS((s, D), BF16), SDS((1, D), F32)], compiler_params=_cp("arbitrary"),
    )(dy, z, g, *deps)


def _ffn_bwd_mid(dz, wd, l, dadg, dadu, tm):
    s = dz.shape[0]

    def body(dz_ref, wd_ref, g_ref, u_ref, dg_ref, du_ref):
        da = lax.dot_general(dz_ref[...], wd_ref[...], NT, preferred_element_type=F32)
        dg_ref[...] = (da * g_ref[...].astype(F32)).astype(BF16)
        du_ref[...] = (da * u_ref[...].astype(F32)).astype(BF16)

    wide = pl.BlockSpec((tm, FH), lambda i, j: (i, j))
    return pl.pallas_call(
        body, name="ffn_bwd_mid", grid=(s // tm, 2),
        in_specs=[pl.BlockSpec((tm, D), lambda i, j: (i, 0)), pl.BlockSpec((None, FH, D), lambda i, j: (l, j, 0)), wide, wide],
        out_specs=[wide, wide],
        out_shape=[SDS((s, DFF), BF16), SDS((s, DFF), BF16)],
        compiler_params=_cp("parallel", "arbitrary"),
    )(dz, wd, dadg, dadu)


def _ffn_bwd_dh(dg, du, wgu, l, x, pre_g, dx1, tm, deps=()):
    s = x.shape[0]
    deps = list(deps)

    def body(dg_ref, du_ref, wg_ref, wu_ref, x_ref, g_ref, dx1_ref, *rest):
        dx_ref, dgp_ref = rest[len(deps):]
        i, k = pl.program_id(0), pl.program_id(1)
        p = (lax.dot_general(dg_ref[...], wg_ref[...], NT, preferred_element_type=F32)
             + lax.dot_general(du_ref[...], wu_ref[...], NT, preferred_element_type=F32))

        @pl.when(k == 0)
        def _():
            dx_ref[...] = p

        @pl.when(k == 1)
        def _():
            dx, dgr = _rms_bwd_rows(x_ref[...], g_ref[...], dx_ref[...] + p)
            dx_ref[...] = dx1_ref[...] + dx
            part = jnp.sum(dgr, axis=0, keepdims=True)

            @pl.when(i == 0)
            def _():
                dgp_ref[...] = part

            @pl.when(i > 0)
            def _():
                dgp_ref[...] += part

    wide = pl.BlockSpec((tm, FH), lambda i, k: (i, k))
    row = pl.BlockSpec((tm, D), lambda i, k: (i, 0))
    vec = pl.BlockSpec((1, D), lambda i, k: (0, 0))
    return pl.pallas_call(
        body, name="ffn_bwd_dh", grid=(s // tm, 2),
        in_specs=[wide, wide, pl.BlockSpec((None, None, D, FH), lambda i, k: (l, k, 0, 0)),
                  pl.BlockSpec((None, None, D, FH), lambda i, k: (l, k + 2, 0, 0)), row, vec, row] + [ANY] * len(deps),
        out_specs=[row, vec],
        out_shape=[SDS((s, D), F32), SDS((1, D), F32)],
        compiler_params=_cp("arbitrary", "arbitrary"),
    )(dg, du, wgu, wgu, x, pre_g, dx1, *deps)


def _mm_tn_into(buf, a, b, l, joff, tka, tn, ts, name):
    s, ka = a.shape
    n = b.shape[1]

    def body(buf_ref, a_ref, b_ref, o_ref):
        p = lax.dot_general(a_ref[...], b_ref[...], TN, preferred_element_type=F32)

        @pl.when(pl.program_id(2) == 0)
        def _():
            o_ref[...] = p

        @pl.when(pl.program_id(2) > 0)
        def _():
            o_ref[...] += p

    return pl.pallas_call(
        body, name=name, grid=(ka // tka, n // tn, s // ts),
        in_specs=[pl.BlockSpec(memory_space=pl.ANY),
                  pl.BlockSpec((ts, tka), lambda ia, j, t: (t, ia)), pl.BlockSpec((ts, tn), lambda ia, j, t: (t, j))],
        out_specs=pl.BlockSpec((None, None, tka, tn), lambda ia, j, t: (l, joff + j, ia, 0)),
        out_shape=SDS(buf.shape, F32), input_output_aliases={0: 0},
        compiler_params=_cp("parallel", "parallel", "arbitrary"),
    )(buf, a, b)


def _proj(x, g, w_in, l, tm):
    s = x.shape[0]

    def body(x_ref, g_ref, w_ref, h_ref, p_ref):
        xf = x_ref[...]
        h = (xf * _rsq(xf, NORM_EPS) * g_ref[...]).astype(BF16)
        h_ref[...] = h
        p_ref[...] = jnp.dot(h, w_ref[...], preferred_element_type=F32)

    return pl.pallas_call(
        body, name="proj", grid=(s // tm,),
        in_specs=[pl.BlockSpec((tm, D), lambda i: (i, 0)), pl.BlockSpec((1, D), lambda i: (0, 0)),
                  pl.BlockSpec((None, D, P_IN), lambda i: (l, 0, 0))],
        out_specs=[pl.BlockSpec((tm, D), lambda i: (i, 0)), pl.BlockSpec((tm, P_IN), lambda i: (i, 0))],
        out_shape=[SDS((s, D), BF16), SDS((s, P_IN), F32)],
        compiler_params=_cp("parallel"),
    )(x, g, w_in)


def _mm_nt(a, w, l, tm, name):
    s, k_dim = a.shape
    n = w.shape[1]

    def body(a_ref, w_ref, o_ref):
        o_ref[...] = lax.dot_general(a_ref[...], w_ref[...], NT, preferred_element_type=F32)

    return pl.pallas_call(
        body, name=name, grid=(s // tm,),
        in_specs=[pl.BlockSpec((tm, k_dim), lambda i: (i, 0)), pl.BlockSpec((None, n, k_dim), lambda i: (l, 0, 0))],
        out_specs=pl.BlockSpec((tm, n), lambda i: (i, 0)),
        out_shape=SDS((s, n), F32), compiler_params=_cp("parallel"),
    )(a, w)


def _mm_nt_rmsbwd(dp, w_in, l, x, g, dx1, tm):
    s = x.shape[0]

    def body(dp_ref, w_ref, x_ref, g_ref, dx1_ref, dx_ref, dg_ref):
        dh = lax.dot_general(dp_ref[...], w_ref[...], NT, preferred_element_type=F32)
        dx, dgr = _rms_bwd_rows(x_ref[...], g_ref[...], dh)
        dx_ref[...] = dx1_ref[...] + dx
        part = jnp.sum(dgr, axis=0, keepdims=True)

        @pl.when(pl.program_id(0) == 0)
        def _():
            dg_ref[...] = part

        @pl.when(pl.program_id(0) > 0)
        def _():
            dg_ref[...] += part

    row = pl.BlockSpec((tm, D), lambda i: (i, 0))
    vec = pl.BlockSpec((1, D), lambda i: (0, 0))
    return pl.pallas_call(
        body, name="mix_bwd_dx", grid=(s // tm,),
        in_specs=[pl.BlockSpec((tm, P_IN), lambda i: (i, 0)), pl.BlockSpec((None, D, P_IN), lambda i: (l, 0, 0)), row, vec, row],
        out_specs=[row, vec], out_shape=[SDS((s, D), F32), SDS((1, D), F32)],
        compiler_params=_cp("arbitrary"),
    )(dp, w_in, x, g, dx1)


def _row_iota(shape):
    return lax.broadcasted_iota(jnp.int32, shape, 0)


def _lru_gates(xc, wa_ref, ba_ref, wx_ref, bx_ref, lam_ref):
    xb = xc.astype(BF16)
    r = _sig(jnp.dot(xb, wa_ref[...], preferred_element_type=F32) + ba_ref[...])
    ig = _sig(jnp.dot(xb, wx_ref[...], preferred_element_type=F32) + bx_ref[...])
    nl = -lam_ref[...]
    sp = jnp.maximum(nl, 0.0) + jnp.log(1.0 + jnp.exp(-jnp.abs(nl)))
    log_a = -LRU_C * r * sp
    a = jnp.exp(log_a)
    x2 = 2.0 * log_a
    series = x2 * (1.0 + x2 * (0.5 + x2 * (1.0 / 6.0 + x2 * (1.0 / 24.0 + x2 * (1.0 / 120.0)))))
    em1 = jnp.where(x2 > -0.05, series, jnp.exp(x2) - 1.0)
    mlt = jnp.sqrt(-em1)
    return r, ig, a, mlt, sp


def _conv_taps(src_ref, w_ref, k_taps, pad, tc):
    acc = None
    for j in range(k_taps):
        term = w_ref[j:j + 1, :] * src_ref[pl.ds(pad - (k_taps - 1) + j, tc), :]
        acc = term if acc is None else acc + term
    return acc


def _gelu_parts(x):
    c0 = math.sqrt(2.0 / math.pi)
    inner = c0 * (x + 0.044715 * x * x * x)
    t = jnp.tanh(inner)
    gl = 0.5 * x * (1.0 + t)
    dgl = 0.5 * (1.0 + t) + 0.5 * x * (1.0 - t * t) * c0 * (1.0 + 3.0 * 0.044715 * x * x)
    return gl, dgl


def _lru_fwd(proj, cw, cb, wa, ba, wx, bx, lam, gg, tc):
    s = proj.shape[0]
    pad = 8

    def body(xcur_ref, xprev_ref, gate_ref, cw_ref, cb_ref, wa_ref, ba_ref, wx_ref, bx_ref, lam_ref, gg_ref,
             yn_ref, h_ref, xs_ref, hc_ref):
        i = pl.program_id(0)

        @pl.when(i == 0)
        def _():
            hc_ref[...] = jnp.zeros_like(hc_ref)

        xs_ref[0:pad, :] = jnp.where(i > 0, xprev_ref[tc - pad:tc, :], 0.0)
        xs_ref[pad:pad + tc, :] = xcur_ref[...]
        xc = _conv_taps(xs_ref, cw_ref, LRU_K, pad, tc) + cb_ref[...]
        _, ig, a, mlt, _ = _lru_gates(xc, wa_ref, ba_ref, wx_ref, bx_ref, lam_ref)
        u = mlt * (ig * xc)
        row = _row_iota((tc, W_A))
        d = 1
        while d < tc:
            ok = row >= d
            a_sh = jnp.where(ok, pltpu.roll(a, d, axis=0), 1.0)
            u_sh = jnp.where(ok, pltpu.roll(u, d, axis=0), 0.0)
            u = a * u_sh + u
            a = a * a_sh
            d *= 2
        h = u + a * hc_ref[...]
        hc_ref[...] = jnp.sum(jnp.where(row == tc - 1, h, 0.0), axis=0, keepdims=True)
        h_ref[...] = h
        gl, _ = _gelu_parts(gate_ref[...])
        ya = gl * h
        yn_ref[...] = (ya * _rsq(ya, NORM_EPS) * gg_ref[...]).astype(BF16)

    blk = lambda c: pl.BlockSpec((tc, W_A), lambda i, c=c: (i, c))
    full = lambda a: pl.BlockSpec(a.shape, lambda i: (0,) * a.ndim)
    params = [cw, cb, wa, ba, wx, bx, lam, gg]
    return pl.pallas_call(
        body, name="lru_fwd", grid=(s // tc,),
        in_specs=[blk(0), pl.BlockSpec((tc, W_A), lambda i: (jnp.maximum(i - 1, 0), 0)), blk(1)] + [full(a) for a in params],
        out_specs=[pl.BlockSpec((tc, W_A), lambda i: (i, 0))] * 2,
        out_shape=[SDS((s, W_A), BF16), SDS((s, W_A), F32)],
        scratch_shapes=[pltpu.VMEM((tc + pad, W_A), F32), pltpu.VMEM((1, W_A), F32)],
        compiler_params=_cp("arbitrary"),
    )(proj, proj, proj, *params)


def _acc(ref, first, val):
    @pl.when(first)
    def _():
        ref[...] = val

    @pl.when(jnp.logical_not(first))
    def _():
        ref[...] += val


def _lru_bwd(dy, proj, h, cw, cb, wa, ba, wx, bx, lam, gg, tc):
    s = proj.shape[0]
    nc = s // tc
    pad = 8

    def body(dy_ref, xcur_ref, xprev_ref, gate_ref, h_ref, hprev_ref, cw_ref, cb_ref, wa_ref, ba_ref, wx_ref, bx_ref,
             lam_ref, gg_ref,
             dp_ref, dcw_ref, dcb_ref, dwa_ref, dba_ref, dwx_ref, dbx_ref, dlam_ref, dgg_ref,
             xs_ref, ds_ref, mu_ref, nx_ref):
        step = pl.program_id(0)
        i = nc - 1 - step
        first = step == 0

        @pl.when(first)
        def _():
            mu_ref[...] = jnp.zeros_like(mu_ref)
            nx_ref[...] = jnp.zeros_like(nx_ref)

        xs_ref[0:pad, :] = jnp.where(i > 0, xprev_ref[tc - pad:tc, :], 0.0)
        xs_ref[pad:pad + tc, :] = xcur_ref[...]
        xc = _conv_taps(xs_ref, cw_ref, LRU_K, pad, tc) + cb_ref[...]
        r, ig, a, mlt, sp = _lru_gates(xc, wa_ref, ba_ref, wx_ref, bx_ref, lam_ref)
        hh = h_ref[...]
        gate = gate_ref[...]
        gl, dgl = _gelu_parts(gate)
        ya = gl * hh
        dya, dggr = _rms_bwd_rows(ya, gg_ref[...], dy_ref[...])
        _acc(dgg_ref, first, jnp.sum(dggr, axis=0, keepdims=True))
        dp_ref[:, W_A:2 * W_A] = dya * hh * dgl
        dh = dya * gl

        row = _row_iota((tc, W_A))
        aa = a
        uu = a * dh
        d = 1
        while d < tc:
            ok = row < tc - d
            a_sh = jnp.where(ok, pltpu.roll(aa, tc - d, axis=0), 1.0)
            u_sh = jnp.where(ok, pltpu.roll(uu, tc - d, axis=0), 0.0)
            uu = uu + aa * u_sh
            aa = aa * a_sh
            d *= 2
        cin = mu_ref[...]
        mu = uu + aa * cin
        lam_t = dh + jnp.where(row == tc - 1, cin, pltpu.roll(mu, tc - 1, axis=0))
        mu_ref[...] = jnp.sum(jnp.where(row == 0, mu, 0.0), axis=0, keepdims=True)
        hm1 = jnp.where(row == 0, jnp.where(i > 0, pltpu.roll(hprev_ref[...], 1, axis=0), 0.0),
                        pltpu.roll(hh, 1, axis=0))
        da = lam_t * hm1
        du = lam_t
        dmlt = du * ig * xc
        dig = du * mlt * xc
        dxc = du * mlt * ig
        dlog_a = da * a - dmlt * (a * a / mlt)
        dr = dlog_a * (-LRU_C * sp)
        dsp = jnp.sum(dlog_a * (-LRU_C * r), axis=0, keepdims=True)
        _acc(dlam_ref, first, dsp * (-_sig(-lam_ref[...])))
        dga = dr * r * (1.0 - r)
        dgx = dig * ig * (1.0 - ig)
        _acc(dba_ref, first, jnp.sum(dga, axis=0, keepdims=True))
        _acc(dbx_ref, first, jnp.sum(dgx, axis=0, keepdims=True))
        xb = xc.astype(BF16)
        dgab = dga.astype(BF16)
        dgxb = dgx.astype(BF16)
        _acc(dwa_ref, first, lax.dot_general(xb, dgab, TN, preferred_element_type=F32))
        _acc(dwx_ref, first, lax.dot_general(xb, dgxb, TN, preferred_element_type=F32))
        dxc = (dxc + lax.dot_general(dgab, wa_ref[...], NT, preferred_element_type=F32)
               + lax.dot_general(dgxb, wx_ref[...], NT, preferred_element_type=F32))

        _acc(dcb_ref, first, jnp.sum(dxc, axis=0, keepdims=True))
        r8 = _row_iota((8, W_A))
        dcw = jnp.zeros((8, W_A), F32)
        for j in range(LRU_K):
            tap = jnp.sum(dxc * xs_ref[pl.ds(pad - (LRU_K - 1) + j, tc), :], axis=0, keepdims=True)
            dcw = dcw + jnp.where(r8 == j, tap, 0.0)
        _acc(dcw_ref, first, dcw)
        ds_ref[0:tc, :] = dxc
        ds_ref[tc:tc + pad, :] = nx_ref[...]
        dlx = None
        for j in range(LRU_K):
            term = cw_ref[j:j + 1, :] * ds_ref[pl.ds(LRU_K - 1 - j, tc), :]
            dlx = term if dlx is None else dlx + term
        dp_ref[:, 0:W_A] = dlx
        nx_ref[...] = dxc[0:pad, :]

    rev = lambda c: pl.BlockSpec((tc, W_A), lambda t, c=c: (nc - 1 - t, c))
    prev = lambda c: pl.BlockSpec((tc, W_A), lambda t, c=c: (jnp.maximum(nc - 2 - t, 0), c))
    full = lambda a: pl.BlockSpec(a.shape, lambda t: (0,) * a.ndim)
    params = [cw, cb, wa, ba, wx, bx, lam, gg]
    vec = SDS((1, W_A), F32)
    sq = SDS((W_A, W_A), F32)
    outs = [SDS((s, 2 * W_A), F32), SDS((8, W_A), F32), vec, sq, vec, sq, vec, vec, vec]
    return pl.pallas_call(
        body, name="lru_bwd", grid=(nc,),
        in_specs=[rev(0), rev(0), prev(0), rev(1), rev(0), prev(0)] + [full(a) for a in params],
        out_specs=[pl.BlockSpec((tc, 2 * W_A), lambda t: (nc - 1 - t, 0))]
        + [pl.BlockSpec(o.shape, lambda t: (0, 0)) for o in outs[1:]],
        out_shape=outs,
        scratch_shapes=[pltpu.VMEM((tc + pad, W_A), F32), pltpu.VMEM((tc + pad, W_A), F32),
                        pltpu.VMEM((1, W_A), F32), pltpu.VMEM((pad, W_A), F32)],
        compiler_params=_cp("arbitrary"),
    )(dy, proj, proj, proj, h, h, *params)


def _attn_stack(qa, qb, kvh):
    lane = lax.broadcasted_iota(jnp.int32, qa.shape, 1)
    keep = (lane >= HD) if kvh == 1 else (lane < HD)
    parts = []
    for tile in (qa, qb):
        for half in (0, 1):
            y = tile if half == kvh else pltpu.roll(tile, HD, axis=1)
            parts.append(jnp.where(keep, y, 0.0))
    return jnp.concatenate(parts, axis=0)


def _attn_unstack(o, kvh):
    lane = lax.broadcasted_iota(jnp.int32, (BLK, 2 * HD), 1)
    tiles = []
    for t in range(2):
        halves = []
        for half in (0, 1):
            blk = o[(2 * t + half) * BLK:(2 * t + half + 1) * BLK, :]
            halves.append(blk if half == kvh else pltpu.roll(blk, HD, axis=1))
        tiles.append(jnp.where(lane < HD, halves[0], halves[1]))
    return tiles


def _attn_stack_all(x_ref_or_val):
    return jnp.concatenate([_attn_stack(x_ref_or_val[:, 256 * kvh:256 * kvh + 128],
                                        x_ref_or_val[:, 256 * kvh + 128:256 * kvh + 256], kvh) for kvh in range(2)], axis=0)


def _attn_unstack_all(o, dst_ref):
    for kvh in range(2):
        ta, tb = _attn_unstack(o[4 * BLK * kvh:4 * BLK * (kvh + 1), :], kvh)
        dst_ref[:, 256 * kvh:256 * kvh + 128] = ta
        dst_ref[:, 256 * kvh + 128:256 * kvh + 256] = tb


def _attn_windows(cur_ref, prev_ref, nb):
    blocks = [prev_ref[...]] + [cur_ref[b * BLK:(b + 1) * BLK, :] for b in range(nb)]
    return [jnp.concatenate(blocks[b:b + 2], axis=0).astype(BF16) for b in range(nb)]


def _attn_probs(qs, kw, n, sink_ref):
    rows = NQ * BLK
    sc = lax.dot_general(qs.astype(BF16), kw, NT, preferred_element_type=F32) * SCALE
    qi = lax.broadcasted_iota(jnp.int32, (rows, 2 * BLK), 0) & (BLK - 1)
    kj = lax.broadcasted_iota(jnp.int32, (rows, 2 * BLK), 1)
    rel = BLK + qi - kj
    mask = (rel >= 0) & (rel < BLK) & ((n - 1) * BLK + kj >= 0)
    head = lax.broadcasted_iota(jnp.int32, (rows, 1), 0) // BLK
    sk = jnp.zeros((rows, 1), F32)
    for h in range(NQ):
        sk = jnp.where(head == h, sink_ref[h:h + 1, 0:1], sk)
    sh = jnp.where(mask, sc, NEG_BIG)
    m = jnp.maximum(jnp.max(sh, axis=-1, keepdims=True), sk)
    e = jnp.exp(sh - m)
    es = jnp.exp(sk - m)
    rz = 1.0 / (jnp.sum(e, axis=-1, keepdims=True) + es)
    return e * rz, es * rz


def _attn_fwd(proj, sinks8, gg):
    s = proj.shape[0]
    nb = ATT_NB_FWD

    def body(q_ref, kc_ref, kp_ref, vc_ref, vp_ref, sink_ref, gg_ref, yn_ref, ob_ref):
        kws, vws = _attn_windows(kc_ref, kp_ref, nb), _attn_windows(vc_ref, vp_ref, nb)
        for b in range(nb):
            rows = pl.ds(b * BLK, BLK)
            p, _ = _attn_probs(_attn_stack_all(q_ref.at[rows, :]), kws[b], nb * pl.program_id(0) + b, sink_ref)
            _attn_unstack_all(jnp.dot(p.astype(BF16), vws[b], preferred_element_type=F32), ob_ref.at[rows, :])
        ob = ob_ref[...]
        yn_ref[...] = (ob * _rsq(ob, NORM_EPS) * gg_ref[...]).astype(BF16)

    tb = nb * BLK
    cur = lambda c: pl.BlockSpec((tb, 128), lambda m, c=c: (m, c))
    prev = lambda c: pl.BlockSpec((BLK, 128), lambda m, c=c: (jnp.maximum(nb * m - 1, 0), c))
    out = pl.BlockSpec((tb, W_B), lambda m: (m, 0))
    return pl.pallas_call(
        body, name="attn_fwd", grid=(s // tb,),
        in_specs=[pl.BlockSpec((tb, W_B), lambda m: (m, 1)), cur(8), prev(8), cur(9), prev(9),
                  pl.BlockSpec((8, 128), lambda n: (0, 0)), pl.BlockSpec((1, W_B), lambda n: (0, 0))],
        out_specs=[out, out], out_shape=[SDS((s, W_B), BF16), SDS((s, W_B), F32)],
        compiler_params=_cp("parallel"),
    )(proj, proj, proj, proj, proj, sinks8, gg)


def _attn_bwd(dy, proj, ob, sinks8, gg):
    s = proj.shape[0]
    nb = ATT_NB_BWD

    def body(dya_ref, dyb_ref, q_ref, kc_ref, kp_ref, vc_ref, vp_ref, ob_ref, sink_ref, gg_ref,
             dq_ref, dcur_ref, dprev_ref, dsink_ref, dgg_ref):
        first = pl.program_id(0) == 0
        kws, vws = _attn_windows(kc_ref, kp_ref, nb), _attn_windows(vc_ref, vp_ref, nb)
        dyn = jnp.concatenate([dya_ref[...], dyb_ref[...]], axis=1)
        dob, dggr = _rms_bwd_rows(ob_ref[...], gg_ref[...], dyn)
        _acc(dgg_ref, first, jnp.sum(dggr, axis=0, keepdims=True))
        r8 = _row_iota((8, 128))
        dsk = jnp.zeros((8, 128), F32)
        for b in range(nb):
            rows = pl.ds(b * BLK, BLK)
            qs = _attn_stack_all(q_ref.at[rows, :])
            p, psink = _attn_probs(qs, kws[b], nb * pl.program_id(0) + b, sink_ref)
            dosb = _attn_stack_all(dob[b * BLK:(b + 1) * BLK, :]).astype(BF16)
            dp = lax.dot_general(dosb, vws[b], NT, preferred_element_type=F32)
            dd = jnp.sum(p * dp, axis=-1, keepdims=True)
            dsb = (p * (dp - dd) * SCALE).astype(BF16)
            dsink_rows = -psink * dd
            for h in range(NQ):
                dsk = dsk + jnp.where(r8 == h, jnp.sum(dsink_rows[h * BLK:(h + 1) * BLK, :], axis=0, keepdims=True), 0.0)
            _attn_unstack_all(jnp.dot(dsb, kws[b], preferred_element_type=F32), dq_ref.at[rows, :])
            dkw = lax.dot_general(dsb, qs.astype(BF16), TN, preferred_element_type=F32)
            dvw = lax.dot_general(p.astype(BF16), dosb, TN, preferred_element_type=F32)
            dprev_ref[rows, 0:128] = dkw[0:BLK, :]
            dprev_ref[rows, 128:256] = dvw[0:BLK, :]
            dcur_ref[rows, 0:128] = dkw[BLK:2 * BLK, :]
            dcur_ref[rows, 128:256] = dvw[BLK:2 * BLK, :]
        _acc(dsink_ref, first, dsk)

    tb = nb * BLK
    cur = lambda c: pl.BlockSpec((tb, 128), lambda m, c=c: (m, c))
    prev = lambda c: pl.BlockSpec((BLK, 128), lambda m, c=c: (jnp.maximum(nb * m - 1, 0), c))
    wide = pl.BlockSpec((tb, W_B), lambda m: (m, 0))
    half = pl.BlockSpec((tb, 256), lambda m: (m, 0))
    return pl.pallas_call(
        body, name="attn_bwd", grid=(s // tb,),
        in_specs=[pl.BlockSpec((tb, 256), lambda m: (m, 1)), pl.BlockSpec((tb, 256), lambda m: (m, 2)),
                  pl.BlockSpec((tb, W_B), lambda m: (m, 1)), cur(8), prev(8), cur(9), prev(9), wide,
                  pl.BlockSpec((8, 128), lambda n: (0, 0)), pl.BlockSpec((1, W_B), lambda n: (0, 0))],
        out_specs=[wide, half, half, pl.BlockSpec((8, 128), lambda n: (0, 0)), pl.BlockSpec((1, W_B), lambda n: (0, 0))],
        out_shape=[SDS((s, W_B), F32), SDS((s, 256), F32), SDS((s, 256), F32), SDS((8, 128), F32), SDS((1, W_B), F32)],
        compiler_params=_cp("arbitrary"),
    )(dy, dy, proj, proj, proj, proj, proj, ob, sinks8, gg)


def _ln_parts(y1, eps=LN_EPS):
    mu = jnp.mean(y1, axis=-1, keepdims=True)
    xc = y1 - mu
    rstd = lax.rsqrt(jnp.mean(xc * xc, axis=-1, keepdims=True) + eps)
    return xc * rstd, rstd


def _conf_fwd(proj, cw, cb, lg, lb, gg, tc):
    s = proj.shape[0]
    pad = 32

    def body(ac_ref, gc_ref, ap_ref, gp_ref, cw_ref, cb_ref, lg_ref, lb_ref, gg_ref, yn_ref, y1_ref, ys_ref):
        i = pl.program_id(0)
        tail = ap_ref[tc - pad:tc, :] * _sig(gp_ref[tc - pad:tc, :])
        ys_ref[0:pad, :] = jnp.where(i > 0, tail, 0.0)
        ys_ref[pad:pad + tc, :] = ac_ref[...] * _sig(gc_ref[...])
        y1 = _conv_taps(ys_ref, cw_ref, CONV_K, pad, tc) + cb_ref[...]
        y1_ref[...] = y1
        xh, _ = _ln_parts(y1)
        yl = xh * lg_ref[...] + lb_ref[...]
        yc = yl * _sig(yl)
        yn_ref[...] = (yc * _rsq(yc, NORM_EPS) * gg_ref[...]).astype(BF16)

    cur = lambda c: pl.BlockSpec((tc, W_C), lambda i, c=c: (i, c))
    prev = lambda c: pl.BlockSpec((tc, W_C), lambda i, c=c: (jnp.maximum(i - 1, 0), c))
    full = lambda a: pl.BlockSpec(a.shape, lambda i: (0,) * a.ndim)
    params = [cw, cb, lg, lb, gg]
    out = pl.BlockSpec((tc, W_C), lambda i: (i, 0))
    return pl.pallas_call(
        body, name="conf_fwd", grid=(s // tc,),
        in_specs=[cur(5), cur(6), prev(5), prev(6)] + [full(a) for a in params],
        out_specs=[out, out], out_shape=[SDS((s, W_C), BF16), SDS((s, W_C), F32)],
        scratch_shapes=[pltpu.VMEM((tc + pad, W_C), F32)],
        compiler_params=_cp("parallel"),
    )(proj, proj, proj, proj, *params)


def _conf_bwd(dy, proj, y1, cw, cb, lg, lb, gg, tc):
    s = proj.shape[0]
    nc = s // tc
    pad = 32

    def body(dy_ref, ac_ref, gc_ref, ap_ref, gp_ref, y1_ref, cw_ref, cb_ref, lg_ref, lb_ref, gg_ref,
             dp_ref, dcw_ref, dcb_ref, dlg_ref, dlb_ref, dgg_ref, ys_ref, ds_ref, nx_ref):
        step = pl.program_id(0)
        i = nc - 1 - step
        first = step == 0

        @pl.when(first)
        def _():
            nx_ref[...] = jnp.zeros_like(nx_ref)

        a = ac_ref[...]
        sg = _sig(gc_ref[...])
        tail = ap_ref[tc - pad:tc, :] * _sig(gp_ref[tc - pad:tc, :])
        ys_ref[0:pad, :] = jnp.where(i > 0, tail, 0.0)
        ys_ref[pad:pad + tc, :] = a * sg
        xh, rstd = _ln_parts(y1_ref[...])
        yl = xh * lg_ref[...] + lb_ref[...]
        sl = _sig(yl)
        yc = yl * sl
        dyc, dggr = _rms_bwd_rows(yc, gg_ref[...], dy_ref[...])
        _acc(dgg_ref, first, jnp.sum(dggr, axis=0, keepdims=True))
        dyl = dyc * sl * (1.0 + yl * (1.0 - sl))
        _acc(dlg_ref, first, jnp.sum(dyl * xh, axis=0, keepdims=True))
        _acc(dlb_ref, first, jnp.sum(dyl, axis=0, keepdims=True))
        dxh = dyl * lg_ref[...]
        dy1 = rstd * (dxh - jnp.mean(dxh, axis=-1, keepdims=True) - xh * jnp.mean(dxh * xh, axis=-1, keepdims=True))
        _acc(dcb_ref, first, jnp.sum(dy1, axis=0, keepdims=True))
        r32 = _row_iota((32, W_C))
        dcw = jnp.zeros((32, W_C), F32)
        for j in range(CONV_K):
            tap = jnp.sum(dy1 * ys_ref[pl.ds(pad - (CONV_K - 1) + j, tc), :], axis=0, keepdims=True)
            dcw = dcw + jnp.where(r32 == j, tap, 0.0)
        _acc(dcw_ref, first, dcw)
        ds_ref[0:tc, :] = dy1
        ds_ref[tc:tc + pad, :] = nx_ref[...]
        dy0 = None
        for j in range(CONV_K):
            term = cw_ref[j:j + 1, :] * ds_ref[pl.ds(CONV_K - 1 - j, tc), :]
            dy0 = term if dy0 is None else dy0 + term
        dp_ref[:, 0:W_C] = dy0 * sg
        dp_ref[:, W_C:2 * W_C] = dy0 * a * sg * (1.0 - sg)
        nx_ref[...] = dy1[0:pad, :]

    rev = lambda c: pl.BlockSpec((tc, W_C), lambda t, c=c: (nc - 1 - t, c))
    prev = lambda c: pl.BlockSpec((tc, W_C), lambda t, c=c: (jnp.maximum(nc - 2 - t, 0), c))
    full = lambda a: pl.BlockSpec(a.shape, lambda t: (0,) * a.ndim)
    params = [cw, cb, lg, lb, gg]
    vec = SDS((1, W_C), F32)
    outs = [SDS((s, 2 * W_C), F32), SDS((32, W_C), F32), vec, vec, vec, vec]
    return pl.pallas_call(
        body, name="conf_bwd", grid=(nc,),
        in_specs=[rev(3), rev(5), rev(6), prev(5), prev(6), rev(0)] + [full(a) for a in params],
        out_specs=[pl.BlockSpec((tc, 2 * W_C), lambda t: (nc - 1 - t, 0))]
        + [pl.BlockSpec(o.shape, lambda t: (0, 0)) for o in outs[1:]],
        out_shape=outs,
        scratch_shapes=[pltpu.VMEM((tc + pad, W_C), F32), pltpu.VMEM((tc + pad, W_C), F32), pltpu.VMEM((pad, W_C), F32)],
        compiler_params=_cp("arbitrary"),
    )(dy, proj, proj, proj, proj, y1, *params)


def _assemble_dproj(dlru, dq, dcur, dprev, dconf):
    s = dq.shape[0]
    nb = s // BLK

    def body(dl_ref, dq_ref, dc_ref, dn_ref, df_ref, o_ref):
        n = pl.program_id(0)
        o_ref[:, 0:512] = dl_ref[...].astype(BF16)
        o_ref[:, 512:1024] = dq_ref[...].astype(BF16)
        o_ref[:, 1024:1280] = (dc_ref[...] + jnp.where(n < nb - 1, dn_ref[...], 0.0)).astype(BF16)
        o_ref[:, 1280:1792] = df_ref[...].astype(BF16)

    wide = pl.BlockSpec((BLK, 512), lambda n: (n, 0))
    return pl.pallas_call(
        body, name="assemble_dproj", grid=(nb,),
        in_specs=[wide, wide, pl.BlockSpec((BLK, 256), lambda n: (n, 0)),
                  pl.BlockSpec((BLK, 256), lambda n: (jnp.minimum(n + 1, nb - 1), 0)), wide],
        out_specs=pl.BlockSpec((BLK, P_IN), lambda n: (n, 0)), out_shape=SDS((s, P_IN), BF16),
        compiler_params=_cp("parallel"),
    )(dlru, dq, dcur, dprev, dconf)


def _loss_grad(y, t, tm):
    s = y.shape[0]

    def body(y_ref, t_ref, dy_ref, l_ref):
        err = y_ref[...] - t_ref[...]
        dy_ref[...] = err * (1.0 / D)
        _acc(l_ref, pl.program_id(0) == 0, jnp.sum(err * err, axis=0, keepdims=True))

    row = pl.BlockSpec((tm, D), lambda i: (i, 0))
    return pl.pallas_call(
        body, name="loss_grad", grid=(s // tm,), in_specs=[row, row],
        out_specs=[row, pl.BlockSpec((1, D), lambda i: (0, 0))],
        out_shape=[SDS((s, D), F32), SDS((1, D), F32)], compiler_params=_cp("arbitrary"),
    )(y, t)


def _block_diag(w):
    rows = [jnp.concatenate([w[h] if k == h else jnp.zeros((64, 64), w.dtype) for k in range(4)], axis=1) for h in range(4)]
    return jnp.concatenate(rows, axis=0)


def _diag_blocks(m):
    return jnp.stack([m[64 * h:64 * (h + 1), 64 * h:64 * (h + 1)] for h in range(4)])


def _layer_params(small, l):
    v = lambda name: small[name][l].reshape(1, -1)
    gg = small["group_g"][l]
    return dict(
        ffn1_pre=v("ffn1_pre_g"), ffn1_post=v("ffn1_post_g"), mix_pre=v("mix_pre_g"), mix_post=v("mix_post_g"),
        ffn2_pre=v("ffn2_pre_g"), ffn2_post=v("ffn2_post_g"), lru_cb=v("lru_conv_b"),
        wa=_block_diag(small["lru_w_a"][l]).astype(BF16), ba=v("lru_b_a"),
        wx=_block_diag(small["lru_w_x"][l]).astype(BF16), bx=v("lru_b_x"), lam=v("lru_lambda"),
        sinks8=jnp.broadcast_to(small["attn_sinks"][l][:, None], (NQ, 128)),
        conv_b=v("conv_b"), ln_g=v("conv_ln_g"), ln_b=v("conv_ln_b"),
        gg_a=gg[0:W_A].reshape(1, -1), gg_b=gg[W_A:W_A + W_B].reshape(1, -1), gg_c=gg[W_A + W_B:].reshape(1, -1),
    )


def _forward_layer(x, weights, p, tiles, deps=()):
    _, mm, _, tc = tiles
    big = dict(weights("ffn1_gu", x))
    p = dict(p)
    sv = dict(x0=x)
    h1, g1, u1, a1 = _ffn_up(x, p["ffn1_pre"], big["ffn1_w_gu"], 0, mm, deps)
    big.update(weights("ffn1_down", a1))
    z1, x = _mm_rms_res(a1, big["ffn1_w_down"], 0, x, p["ffn1_post"], 0.5, mm, FH, "ffn_down")
    sv.update(h1=h1, g1=g1, u1=u1, a1=a1, z1=z1, x1=x)
    big.update(weights("mix", x))
    p.update(lru_cw=big.pop("lru_conv_w"), conv_w=big.pop("conv_w"))
    hn, proj = _proj(x, p["mix_pre"], big["w_in"], 0, mm)
    yn_a, hl = _lru_fwd(proj, p["lru_cw"], p["lru_cb"], p["wa"], p["ba"], p["wx"], p["bx"], p["lam"], p["gg_a"], tc)
    yn_b, ob = _attn_fwd(proj, p["sinks8"], p["gg_b"])
    yn_c, y1 = _conf_fwd(proj, p["conv_w"], p["conv_b"], p["ln_g"], p["ln_b"], p["gg_c"], tc)
    ycat = jnp.concatenate([yn_a, yn_b, yn_c], axis=1)
    zo, x = _mm_rms_res(ycat, big["w_out"], 0, x, p["mix_post"], 1.0, mm, D, "mix_out")
    sv.update(hn=hn, proj=proj, hl=hl, ob=ob, y1=y1, ycat=ycat, zo=zo, x2=x)
    big.update(weights("ffn2", x))
    h2, g2, u2, a2 = _ffn_up(x, p["ffn2_pre"], big["ffn2_w_gu"], 0, mm)
    z2, x = _mm_rms_res(a2, big["ffn2_w_down"], 0, x, p["ffn2_post"], 0.5, mm, FH, "ffn_down")
    sv.update(h2=h2, g2=g2, u2=u2, a2=a2, z2=z2, p=p, big=big)
    return x, sv


def _grad_buffers():
    empty = lambda *shape: lax.empty(shape, F32)
    return dict(ffn1_w_gu=empty(1, NSHARD, D, FH), ffn2_w_gu=empty(1, NSHARD, D, FH), ffn1_w_down=empty(1, 1, DFF, D),
                ffn2_w_down=empty(1, 1, DFF, D), w_in=empty(1, 1, D, P_IN), w_out=empty(1, 1, D, D))


def _backward_layer(dx, sv, bufs, tiles, stage):
    p, big = sv["p"], sv["big"]
    tm, mm, dw, tc = tiles
    gr = {}

    def ffn_bwd(dx, which, xin, h, g, u, a, z, pre, post, deps):
        dz, dpost = _rms_bwd(dx, z, post, 0.5, tm, "ffn_post_bwd", deps)
        dg, du = _ffn_bwd_mid(dz, big[which + "_w_down"], 0, g, u, mm)
        bufs[which + "_w_down"] = _mm_tn_into(bufs[which + "_w_down"], a, dz, 0, 0, FH, D, dw, "dw_down")
        bufs[which + "_w_gu"] = _mm_tn_into(bufs[which + "_w_gu"], h, dg, 0, 0, D, FH, dw, "dw_gate")
        bufs[which + "_w_gu"] = _mm_tn_into(bufs[which + "_w_gu"], h, du, 0, 2, D, FH, dw, "dw_up")
        deps = stage({n: bufs[n] for n in (which + "_w_gu", which + "_w_down")}, bufs[which + "_w_gu"])
        dxn, dpre = _ffn_bwd_dh(dg, du, big[which + "_w_gu"], 0, xin, pre, dx, mm, deps)
        return dxn, dpre, dpost

    dx, gr["ffn2_pre_g"], gr["ffn2_post_g"] = ffn_bwd(dx, "ffn2", sv["x2"], sv["h2"], sv["g2"], sv["u2"], sv["a2"],
                                                      sv["z2"], p["ffn2_pre"], p["ffn2_post"], ())
    do, gr["mix_post_g"] = _rms_bwd(dx, sv["zo"], p["mix_post"], 1.0, tm, "mix_post_bwd")
    bufs["w_out"] = _mm_tn_into(bufs["w_out"], sv["ycat"], do, 0, 0, D, D, dw, "dw_out")
    dy = _mm_nt(do, big["w_out"], 0, mm, "mix_dy")
    proj = sv["proj"]
    (dlru, dcw, gr["lru_conv_b"], dwa, gr["lru_b_a"], dwx, gr["lru_b_x"], gr["lru_lambda"], dgg_a) = _lru_bwd(
        dy, proj, sv["hl"], p["lru_cw"], p["lru_cb"], p["wa"], p["ba"], p["wx"], p["bx"], p["lam"], p["gg_a"], tc)
    dq, dcur, dprev, dsk, dgg_b = _attn_bwd(dy, proj, sv["ob"], p["sinks8"], p["gg_b"])
    dconf, dconvw, gr["conv_b"], gr["conv_ln_g"], gr["conv_ln_b"], dgg_c = _conf_bwd(
        dy, proj, sv["y1"], p["conv_w"], p["conv_b"], p["ln_g"], p["ln_b"], p["gg_c"], tc)
    dproj = _assemble_dproj(dlru, dq, dcur, dprev, dconf)
    bufs["w_in"] = _mm_tn_into(bufs["w_in"], sv["hn"], dproj, 0, 0, D, P_IN, dw, "dw_in")
    dx, gr["mix_pre_g"] = _mm_nt_rmsbwd(dproj, big["w_in"], 0, sv["x1"], p["mix_pre"], dx, mm)
    gr["lru_conv_w"] = dcw[0:LRU_K]
    gr["lru_w_a"] = _diag_blocks(dwa)
    gr["lru_w_x"] = _diag_blocks(dwx)
    gr["attn_sinks"] = dsk[:, 0]
    gr["conv_w"] = dconvw[0:CONV_K]
    gr["group_g"] = jnp.concatenate([dgg_a, dgg_b, dgg_c], axis=1)
    dx, gr["ffn1_pre_g"], gr["ffn1_post_g"] = ffn_bwd(dx, "ffn1", sv["x0"], sv["h1"], sv["g1"], sv["u1"], sv["a1"],
                                                      sv["z1"], p["ffn1_pre"], p["ffn1_post"],
                                                      stage({n: bufs[n] for n in ("w_in", "w_out")}, dx))
    return dx, gr


def _tiles(s):
    return min(512, s), min(1024, s), min(2048, s), min(512, s // 2)


HBM_SPEC = pl.BlockSpec(memory_space=pltpu.HBM)
SEM_SPEC = pl.BlockSpec(memory_space=pltpu.SEMAPHORE)
EFFECT = pltpu.SideEffectType.DATAFLOW_SIDE_EFFECTING


def _place():
    x, y, c = lax.axis_index("x"), lax.axis_index("y"), lax.axis_index("c")
    return x, y, c, [(1 - x, y), (x, 1 - y), (1 - x, 1 - y)]


def _rcopy(src, dst, send_sems, recv_sems, k, to):
    return pltpu.make_async_remote_copy(src_ref=src, dst_ref=dst, send_sem=send_sems.at[k], recv_sem=recv_sems.at[k],
                                        device_id=to, device_id_type=MESH)


def _half(rows, which):
    return pl.ds(which * (rows // 2), rows // 2)


def _place_shard(w, l, p_idx, dtype):
    _, rows, cols = w.shape
    tr = _rows_per_block(rows, cols, 16) if rows % 16 == 0 else rows

    def body(p_ref, buf_ref, w_ref, o_ref):
        o_ref[...] = w_ref[...].astype(dtype)

    spec = pltpu.PrefetchScalarGridSpec(
        num_scalar_prefetch=1, grid=(rows // tr,),
        in_specs=[ANY, pl.BlockSpec((None, tr, cols), lambda i, pr: (l, i, 0))],
        out_specs=pl.BlockSpec((None, None, tr, cols), lambda i, pr: (0, pr[0], i, 0)))
    shape = (1, NSHARD, rows, cols)
    return pl.pallas_call(body, name="place_shard", grid_spec=spec, out_shape=SDS(shape, dtype),
                          input_output_aliases={1: 0}, compiler_params=_cp("parallel"),
                          )(p_idx, lax.empty(shape, dtype), w)


def _gather_two_level(bufs, n_halved):
    n = len(bufs)

    def body(*refs):
        outs = refs[n:2 * n]
        send_sems, recv_sems = refs[2 * n:]
        x, y, c, chips = _place()
        p = 2 * x + y
        me, sibling = (x, y, c), (x, y, 1 - c)

        def blk(a, q, half):
            return outs[a].at[0, q, _half(outs[a].shape[2], half)] if a < n_halved else outs[a].at[0, q]

        def cp(a, k, q, half, to):
            return _rcopy(blk(a, q, half), blk(a, q, half), send_sems, recv_sems, 6 * a + k, to)

        first = [cp(a, j, p, c, (*chip, c)) for a in range(n) for j, chip in enumerate(chips)]
        for d in first:
            d.start()
        passed = []
        for a in range(n):
            for j, chip in enumerate(chips):
                q = 2 * chip[0] + chip[1]
                cp(a, j, q, c, me).wait_recv()
                if a < n_halved:
                    passed.append(cp(a, 3 + j, q, c, sibling))
                    passed[-1].start()
        for a in range(n_halved):
            for j, chip in enumerate(chips):
                cp(a, 3 + j, 2 * chip[0] + chip[1], 1 - c, me).wait_recv()
        for d in first + passed:
            d.wait_send()

    return pl.pallas_call(
        body, name="gather_layer0", in_specs=[ANY] * n, out_specs=[ANY] * n,
        out_shape=[SDS(b.shape, b.dtype) for b in bufs], input_output_aliases={a: a for a in range(n)},
        scratch_shapes=[pltpu.SemaphoreType.DMA((6 * n,)), pltpu.SemaphoreType.DMA((6 * n,))],
    )(*bufs)


def _run_plans(plans, refs, send_sems, recv_sems):
    cps, b0, s0 = [], 0, 0
    for plan, nb, ns in plans:
        cps += plan(refs[b0:b0 + nb], send_sems, recv_sems, s0)
        b0, s0 = b0 + nb, s0 + ns
    return cps


def _exchange(name, bufs, plans):
    n = len(bufs)
    nsem = sum(ns for _, _, ns in plans)

    def body(*refs):
        cps = _run_plans(plans, refs[n:2 * n], refs[2 * n], refs[2 * n + 1])
        for cp in cps:
            cp.start()
        for cp in cps:
            cp.wait()

    return pl.pallas_call(
        body, name=name, in_specs=[ANY] * n, out_specs=[ANY] * n, out_shape=[SDS(b.shape, b.dtype) for b in bufs],
        input_output_aliases={a: a for a in range(n)},
        scratch_shapes=[pltpu.SemaphoreType.DMA((nsem,)), pltpu.SemaphoreType.DMA((nsem,))],
    )(*bufs)


def _exchange_start(name, bufs, plans, deps=()):
    n = len(bufs)
    nsem = sum(ns for _, _, ns in plans)
    deps = list(deps)
    first_out = n + len(deps)

    def body(*refs):
        for cp in _run_plans(plans, refs[:n], refs[first_out], refs[first_out + 1]):
            cp.start()
        token = refs[first_out + 2 + n]
        token[...] = jnp.zeros_like(token)

    outs = pl.pallas_call(
        body, name=name,
        out_shape=(pltpu.SemaphoreType.DMA((nsem,)), pltpu.SemaphoreType.DMA((nsem,)),
                   *[pltpu.HBM(b.shape, b.dtype) for b in bufs], SDS((8, 128), F32)),
        in_specs=[HBM_SPEC] * n + [ANY] * len(deps),
        out_specs=(SEM_SPEC, SEM_SPEC, *[HBM_SPEC] * n, pl.BlockSpec(memory_space=pltpu.VMEM)),
        input_output_aliases={a: 2 + a for a in range(n)},
        compiler_params=pltpu.CompilerParams(has_side_effects=EFFECT),
    )(*[pltpu.with_memory_space_constraint(b, pltpu.HBM) for b in bufs], *deps)
    return outs[0], outs[1], list(outs[2:2 + n]), outs[2 + n]


def _exchange_wait(name, send_sems, recv_sems, bufs, plans, after):
    n = len(bufs)

    def body(*refs):
        for cp in _run_plans(plans, refs[:n], refs[n], refs[n + 1]):
            cp.wait_send()
            cp.wait_recv()

    return pl.pallas_call(
        body, name=name, out_shape=[pltpu.HBM(b.shape, b.dtype) for b in bufs],
        in_specs=[HBM_SPEC] * n + [SEM_SPEC, SEM_SPEC, ANY], out_specs=[HBM_SPEC] * n,
        input_output_aliases={a: a for a in range(n)},
        compiler_params=pltpu.CompilerParams(has_side_effects=EFFECT),
    )(*bufs, send_sems, recv_sems, after)


def _plan_gather(refs, send_sems, recv_sems, base):
    x, y, c, chips = _place()
    p = 2 * x + y
    return [_rcopy(r.at[0, p], r.at[0, p], send_sems, recv_sems, base + 3 * a + j, (*chip, c))
            for a, r in enumerate(refs) for j, chip in enumerate(chips)]


def _plan_pair_exchange(refs, send_sems, recv_sems, base):
    x, y, c, _ = _place()
    n = len(refs) // 2
    return [_rcopy(refs[a].at[:, _half(refs[a].shape[1], 1 - c)], refs[n + a], send_sems, recv_sems, base + a,
                   (x, y, 1 - c)) for a in range(n)]


def _plan_chip_exchange(refs, send_sems, recv_sems, base):
    x, y, c, chips = _place()
    n = len(refs) // 2
    return [_rcopy(refs[a].at[2 * chip[0] + chip[1]], refs[n + a].at[j], send_sems, recv_sems, base + 3 * a + j,
                   (*chip, c)) for a in range(n) for j, chip in enumerate(chips)]


def _plan_pair_share(refs, send_sems, recv_sems, base):
    x, y, c, _ = _place()
    return [_rcopy(r.at[_half(r.shape[0], c)], r.at[_half(r.shape[0], c)], send_sems, recv_sems, base + a,
                   (x, y, 1 - c)) for a, r in enumerate(refs)]


def _plan_small_gather(refs, send_sems, recv_sems, base):
    x, y, c, _ = _place()
    me = 4 * x + 2 * y + c
    cps = []
    for m in range(1, NDEV):
        peer = (1 - x if m & 4 else x, 1 - y if m & 2 else y, 1 - c if m & 1 else c)
        cps.append(_rcopy(refs[0], refs[1].at[me], send_sems, recv_sems, base + m - 1, peer))
    return cps


def _sum_small(buf, gathered):
    def body(buf_ref, g_ref, o_ref):
        x, y, c, _ = _place()
        me = 4 * x + 2 * y + c
        total = jnp.where(me == 0, buf_ref[...], g_ref[0])
        for dev in range(1, NDEV):
            total = total + jnp.where(me == dev, buf_ref[...], g_ref[dev])
        o_ref[...] = total

    vm = pl.BlockSpec(memory_space=pltpu.VMEM)
    return pl.pallas_call(body, name="sum_small", in_specs=[vm, vm], out_specs=vm, out_shape=SDS(buf.shape, F32),
                          compiler_params=pltpu.CompilerParams(vmem_limit_bytes=VMEM_LIMIT))(buf, gathered)


BLOCK_ELEMS = 256 * 1024


def _rows_per_block(rows, cols, mult):
    best = None
    for tr in range(mult, rows + 1, mult):
        if rows % tr == 0 and tr * cols <= BLOCK_ELEMS:
            best = tr
    assert best is not None, (rows, cols)
    return best


def _pair_sum(g, r, c_idx):
    nq, rows, cols = g.shape
    half = rows // 2
    tr = _rows_per_block(half, cols, 16)
    nb = half // tr

    def body(c_ref, g_ref, r_ref, t_ref):
        t_ref[...] = (g_ref[...] + r_ref[...]).astype(BF16)

    blk = pl.BlockSpec((None, tr, cols), lambda q, i, cr: (q, i, 0))
    spec = pltpu.PrefetchScalarGridSpec(
        num_scalar_prefetch=1, grid=(nq, nb),
        in_specs=[pl.BlockSpec((None, tr, cols), lambda q, i, cr: (q, cr[0] * nb + i, 0)), blk], out_specs=blk)
    return pl.pallas_call(body, name="grad_pair_sum", grid_spec=spec, out_shape=SDS((nq, half, cols), BF16),
                          compiler_params=_cp("parallel", "parallel"))(c_idx, g, r)


def _chip_sum(g, r, rr, cp_idx):
    _, rows, cols = g.shape
    half = rows // 2
    tr = _rows_per_block(half, cols, 16)
    nb = half // tr

    def body(cp_ref, buf_ref, g_ref, r_ref, rr_ref, o_ref):
        o_ref[...] = ((g_ref[...] + r_ref[...]) + rr_ref[0].astype(F32) + rr_ref[1].astype(F32) + rr_ref[2].astype(F32))

    spec = pltpu.PrefetchScalarGridSpec(
        num_scalar_prefetch=1, grid=(nb,),
        in_specs=[ANY, pl.BlockSpec((None, tr, cols), lambda i, cp: (cp[1], cp[0] * nb + i, 0)),
                  pl.BlockSpec((None, tr, cols), lambda i, cp: (cp[1], i, 0)),
                  pl.BlockSpec((3, tr, cols), lambda i, cp: (0, i, 0))],
        out_specs=pl.BlockSpec((tr, cols), lambda i, cp: (cp[0] * nb + i, 0)))
    return pl.pallas_call(body, name="grad_chip_sum", grid_spec=spec, out_shape=SDS((rows, cols), F32),
                          input_output_aliases={1: 0}, compiler_params=_cp("parallel"),
                          )(cp_idx, lax.empty((rows, cols), F32), g, r, rr)


def _adamw_math(w, g, m, v):
    mn = ADAM_B1 * m + (1.0 - ADAM_B1) * g
    vn = ADAM_B2 * v + (1.0 - ADAM_B2) * (g * g)
    m_hat = mn / (1.0 - ADAM_B1 ** ADAM_STEP)
    v_hat = vn / (1.0 - ADAM_B2 ** ADAM_STEP)
    return -ADAM_LR * (m_hat / (jnp.sqrt(v_hat) + ADAM_EPS) + ADAM_WD * w), mn, vn


def _adamw_layer(w, g, m, v, l, outs, deps=()):
    _, rows, cols = w.shape
    tr = _rows_per_block(rows, cols, 8)
    deps = list(deps)

    def body(*refs):
        w_ref, g_ref, m_ref, v_ref = refs[4:8]
        go_ref, d_ref, mo_ref, vo_ref = refs[8 + len(deps):]
        gg = g_ref[...]
        go_ref[...] = gg
        d_ref[...], mo_ref[...], vo_ref[...] = _adamw_math(w_ref[...], gg, m_ref[...], v_ref[...])

    blk = pl.BlockSpec((None, tr, cols), lambda i: (l, i, 0))
    return pl.pallas_call(
        body, name="adamw_layer", grid=(rows // tr,),
        in_specs=[ANY] * 4 + [blk, pl.BlockSpec((tr, cols), lambda i: (i, 0)), blk, blk] + [ANY] * len(deps),
        out_specs=[blk] * 4, out_shape=[SDS(w.shape, F32)] * 4, input_output_aliases={k: k for k in range(4)},
        compiler_params=_cp("parallel"))(*outs, w, g, m, v, *deps)


def _adamw_small(ws, gs, ms, vs, deps=()):
    n = len(ws)
    deps = list(deps)

    def body(*refs):
        refs = refs[:4 * n] + refs[4 * n + len(deps):]
        w, g, m, v, d_out, m_out, v_out = (refs[k * n:(k + 1) * n] for k in range(7))
        for k in range(n):
            d_out[k][...], m_out[k][...], v_out[k][...] = _adamw_math(w[k][...], g[k][...], m[k][...], v[k][...])

    vm = pl.BlockSpec(memory_space=pltpu.VMEM)
    outs = pl.pallas_call(body, name="adamw_small", in_specs=[vm] * (4 * n) + [ANY] * len(deps), out_specs=[vm] * (3 * n),
                          out_shape=[SDS(w.shape, F32) for w in ws] * 3,
                          compiler_params=pltpu.CompilerParams(vmem_limit_bytes=VMEM_LIMIT))(*ws, *gs, *ms, *vs, *deps)
    return outs[:n], outs[n:2 * n], outs[2 * n:]


_WEIGHTS = ["ffn1_pre_g", "ffn1_w_gu", "ffn1_w_down", "ffn1_post_g", "mix_pre_g", "w_in", "lru_conv_w", "lru_conv_b",
            "lru_w_a", "lru_b_a", "lru_w_x", "lru_b_x", "lru_lambda", "attn_sinks", "conv_w", "conv_b", "conv_ln_g",
            "conv_ln_b", "group_g", "w_out", "mix_post_g", "ffn2_pre_g", "ffn2_w_gu", "ffn2_w_down", "ffn2_post_g"]
_INPUTS = ["x"] + _WEIGHTS + ["loss_target"] + ["m_" + n for n in _WEIGHTS] + ["v_" + n for n in _WEIGHTS]
_BIG = ["ffn1_w_gu", "ffn1_w_down", "w_in", "w_out", "ffn2_w_gu", "ffn2_w_down"]
_SMALL_SHARDED = ["lru_conv_w", "conv_w"]
_SMALL_REPL = [n for n in _WEIGHTS if n not in _BIG and n not in _SMALL_SHARDED]

PACK_TILE = 8 * 128


def _pack(arrs):
    parts = []
    for a in arrs:
        flat = a.reshape(-1)
        parts.append(jnp.pad(flat, (0, -flat.shape[0] % PACK_TILE)).reshape(-1, 128))
    return jnp.concatenate(parts, axis=0)


def _unpack(buf, shapes):
    out, row = [], 0
    for shp in shapes:
        size = math.prod(shp)
        nrow = -(-size // PACK_TILE) * 8
        out.append(buf[row:row + nrow].reshape(-1)[:size].reshape(shp))
        row += nrow
    return out


def _unshard_cols(a):
    return a.transpose(0, 2, 1, 3).reshape(1, a.shape[2], NSHARD * a.shape[3])


_GROUPS = dict(ffn1_gu=["ffn1_w_gu"], ffn1_down=["ffn1_w_down"], mix=["w_in", "w_out", "lru_conv_w", "conv_w"],
               ffn2=["ffn2_w_gu", "ffn2_w_down"])


def _full_weights(group, gathered):
    g = dict(zip(_GROUPS[group], gathered))
    if group == "mix":
        return dict(w_in=_unshard_cols(g["w_in"]), w_out=g["w_out"].reshape(1, D, D),
                    lru_conv_w=_unshard_cols(g["lru_conv_w"])[0], conv_w=_unshard_cols(g["conv_w"])[0])
    return {n: (a.reshape(1, DFF, D) if n.endswith("w_down") else a) for n, a in g.items()}


def _by_shard(name, buf):
    if name.endswith("w_gu"):
        return buf[0]
    if name == "w_in":
        return buf.reshape(D, NSHARD, P_IN // NSHARD).transpose(1, 0, 2)
    return buf.reshape(NSHARD, buf.shape[2] // NSHARD, buf.shape[3])


class _Reducer:
    PLANS = (_plan_pair_exchange, _plan_chip_exchange, _plan_pair_share)

    def __init__(self, keys, gs, c_idx, cp_idx):
        self.keys, self.gs, self.c_idx, self.cp_idx = keys, gs, c_idx, cp_idx
        self.n = len(gs)
        self.step = 0
        self.result = None

    def inputs(self):
        n = self.n
        if self.step == 0:
            bufs = self.gs + [lax.empty((NSHARD, g.shape[1] // 2, g.shape[2]), F32) for g in self.gs]
        elif self.step == 1:
            ts = [_pair_sum(g, r, self.c_idx) for g, r in zip(self.gs, self.rs)]
            bufs = ts + [lax.empty((3,) + t.shape[1:], BF16) for t in ts]
        else:
            bufs = [_chip_sum(g, r, rr, self.cp_idx) for g, r, rr in zip(self.gs, self.rs, self.rrs)]
        return bufs, (self.PLANS[self.step], len(bufs), (n, 3 * n, n)[self.step])

    def absorb(self, done):
        n = self.n
        if self.step == 0:
            self.gs, self.rs = done[:n], done[n:]
        elif self.step == 1:
            self.rrs = done[n:]
        else:
            self.result = dict(zip(self.keys, done))
        self.step += 1


class _SmallGather:
    def __init__(self, buf):
        self.buf, self.step, self.result, self.gathered = buf, 0, {}, None

    def inputs(self):
        return [self.buf, jnp.zeros((NDEV,) + self.buf.shape, F32)], (_plan_small_gather, 2, NDEV - 1)

    def absorb(self, done):
        self.buf, self.gathered = done
        self.step = 3


class _ReducePipeline:
    def __init__(self, c_idx, cp_idx):
        self.c_idx, self.cp_idx = c_idx, cp_idx
        self.reducers, self.flying, self.calls = [], None, 0

    def add(self, layer, done):
        if done:
            keys = [(layer, n) for n in done]
            self.reducers.append(_Reducer(keys, [_by_shard(n, b) for n, b in done.items()], self.c_idx, self.cp_idx))

    def _next(self):
        active = [r for r in self.reducers if r.step < 3]
        bufs, plans = [], []
        for r in active:
            b, triple = r.inputs()
            bufs += b
            plans.append(triple)
        self.calls += 1
        return active, bufs, plans, "grad_exchange%d" % self.calls

    def _absorb(self, active, plans, done):
        at = 0
        for r, (_, nb, _) in zip(active, plans):
            r.absorb(done[at:at + nb])
            at += nb

    def _land(self, after):
        if self.flying is not None:
            active, plans, name, send_sems, recv_sems, bufs = self.flying
            self._absorb(active, plans, _exchange_wait(name + "_wait", send_sems, recv_sems, bufs, plans, after))
            self.flying = None

    def hook(self, after):
        self._land(after)
        active, bufs, plans, name = self._next()
        if not active:
            return []
        send_sems, recv_sems, bufs, token = _exchange_start(name + "_start", bufs, plans)
        self.flying = (active, plans, name, send_sems, recv_sems, bufs)
        return [token]

    def available(self):
        out = {}
        for r in self.reducers:
            if r.step == 3:
                out.update(r.result)
        return out

    def finish(self, after):
        self._land(after)
        while True:
            active, bufs, plans, name = self._next()
            if not active:
                break
            self._absorb(active, plans, _exchange(name, bufs, plans))
        out = {}
        for r in self.reducers:
            out.update(r.result)
        return out


def kernel(*args):
    d = dict(zip(_INPUTS, args, strict=True))
    xi, yi, ci = lax.axis_index("x"), lax.axis_index("y"), lax.axis_index("c")
    p = 2 * xi + yi
    c_idx = jnp.reshape(ci, (1,)).astype(jnp.int32)
    p_idx = jnp.reshape(p, (1,)).astype(jnp.int32)
    cp_idx = jnp.stack([ci, p]).astype(jnp.int32)
    x, target = d["x"][0], d["loss_target"][0]
    tiles = _tiles(x.shape[0])

    groups = [(l, grp) for l in range(DEPTH) for grp in _GROUPS]
    placed = {(l, grp): [_place_shard(d[n], l, p_idx, BF16 if n in _BIG else F32) for n in _GROUPS[grp]]
              for l, grp in groups}
    ready = {groups[0]: _gather_two_level(placed[groups[0]], len(placed[groups[0]]))}
    flying, tokens = {}, [ready[groups[0]][0]]
    for l, grp in groups[1:]:
        plans = [(_plan_gather, len(placed[l, grp]), 3 * len(placed[l, grp]))]
        send_sems, recv_sems, bufs, token = _exchange_start("gather_l%d_%s_start" % (l, grp), placed[l, grp], plans,
                                                             tokens[-1:])
        flying[l, grp] = (send_sems, recv_sems, bufs, plans)
        tokens.append(token)

    def weights_of(l):
        def weights(grp, after):
            if (l, grp) not in ready:
                send_sems, recv_sems, bufs, plans = flying[l, grp]
                ready[l, grp] = _exchange_wait("gather_l%d_%s_wait" % (l, grp), send_sems, recv_sems, bufs, plans, after)
            return _full_weights(grp, ready[l, grp])
        return weights

    small = {n: d[n] for n in _SMALL_REPL}
    x1, sv0 = _forward_layer(x, weights_of(0), _layer_params(small, 0), tiles, tokens[1:])
    x2, sv1 = _forward_layer(x1, weights_of(1), _layer_params(small, 1), tiles)
    dx, lcols = _loss_grad(x2, target, tiles[0])

    pipe = _ReducePipeline(c_idx, cp_idx)
    sgrads = [None] * DEPTH
    for l, sv in ((1, sv1), (0, sv0)):
        bufs = _grad_buffers()

        def stage(done, dx, l=l):
            pipe.add(l, done)
            return pipe.hook(dx)

        dx, sgrads[l] = _backward_layer(dx, sv, bufs, tiles, stage)
    grad_x = dx

    stacked = {n: jnp.stack([sgrads[l][n].reshape(d[n].shape[1:]) for l in range(DEPTH)]) for n in _SMALL_REPL}
    for n in _SMALL_SHARDED:
        stacked[n] = jnp.stack([sgrads[l][n] for l in range(DEPTH)])
    loss_part = jnp.pad((0.5 / D) * jnp.sum(lcols).reshape(1), (0, 127))
    order = _SMALL_REPL + _SMALL_SHARDED
    small_gather = _SmallGather(_pack([loss_part] + [stacked[n] for n in order]))
    pipe.reducers.append(small_gather)

    results = {n: tuple(lax.empty(d[n].shape, F32) for _ in range(4)) for n in _BIG}
    applied = set()

    def apply_ready(deps, last):
        for (l, n), g in pipe.available().items():
            if (l, n) not in applied:
                results[n] = _adamw_layer(d[n], g, d["m_" + n], d["v_" + n], l, results[n], deps)
                applied.add((l, n))
                deps, last = (), results[n][1]
        return last

    last = apply_ready(pipe.hook(grad_x), grad_x)
    token = pipe.hook(last)
    summed = _unpack(_sum_small(small_gather.buf, small_gather.gathered), [(128,)] + [stacked[n].shape for n in order])
    loss = summed[0][0]
    grads = {}
    for n, g in zip(order, summed[1:]):
        if n in _SMALL_SHARDED:
            g = lax.dynamic_slice_in_dim(g, p * (g.shape[2] // NSHARD), g.shape[2] // NSHARD, axis=2)
        grads[n] = g
    delta, new_m, new_v = {}, {}, {}
    small_out = _adamw_small([d[n] for n in order], [grads[n] for n in order], [d["m_" + n] for n in order],
                             [d["v_" + n] for n in order], token)
    for out, res in zip((delta, new_m, new_v), small_out):
        out.update(zip(order, res))
    last = apply_ready((), small_out[0][0])
    pipe.finish(last)
    apply_ready((), last)
    for n in _BIG:
        grads[n], delta[n], new_m[n], new_v[n] = results[n]

    return (loss, grad_x[None], *[grads[n] for n in _WEIGHTS], *[delta[n] for n in _WEIGHTS],
            *[new_m[n] for n in _WEIGHTS], *[new_v[n] for n in _WEIGHTS])
```

```python
import functools
import math

import jax
import jax.numpy as jnp
from jax import lax
from jax.experimental import pallas as pl
from jax.experimental.pallas import tpu as pltpu

F32 = jnp.float32
BF16 = jnp.bfloat16
SDS = jax.ShapeDtypeStruct

D = 1024
DFF = 2816
FH = DFF // 2
DEPTH = 2
W_A = 256
W_B = 512
W_C = 256
NQ = 8
HD = 64
BLK = 128
ATT_NB_FWD = 1
ATT_NB_BWD = 4
P_IN = 1792
LRU_K = 4
CONV_K = 31
LRU_C = 8.0
NORM_EPS = 1e-6
LN_EPS = 1e-5
NEG_BIG = -1e30
SCALE = 1.0 / math.sqrt(HD)

ADAM_LR = 0.001
ADAM_B1 = 0.9
ADAM_B2 = 0.999
ADAM_EPS = 1e-08
ADAM_WD = 0.01
ADAM_STEP = 10

VMEM_LIMIT = 60 * 1024 * 1024
NSHARD = 4
NDEV = 8

TN = (((0,), (0,)), ((), ()))
NT = (((1,), (1,)), ((), ()))

MESH = pl.DeviceIdType.MESH
ANY = pl.BlockSpec(memory_space=pl.ANY)


def _cp(*sem):
    return pltpu.CompilerParams(dimension_semantics=sem if sem else None, vmem_limit_bytes=VMEM_LIMIT)


def _rsq(x, eps):
    return lax.rsqrt(jnp.mean(x * x, axis=-1, keepdims=True) + eps)


def _rms_bwd_rows(x, g, dy):
    r = _rsq(x, NORM_EPS)
    xh = x * r
    dyg = dy * g
    dx = r * (dyg - xh * jnp.mean(dyg * xh, axis=-1, keepdims=True))
    return dx, dy * xh


def _sig(x):
    return jax.nn.sigmoid(x)


def _ffn_up(x, pre_g, wgu, l, tm, deps=()):
    s = x.shape[0]
    deps = list(deps)

    def body(x_ref, g_ref, wg_ref, wu_ref, *rest):
        h_ref, go_ref, uo_ref, a_ref = rest[len(deps):]

        @pl.when(pl.program_id(1) == 0)
        def _():
            xf = x_ref[...]
            h_ref[...] = (xf * _rsq(xf, NORM_EPS) * g_ref[...]).astype(BF16)

        h = h_ref[...]
        gg = jnp.dot(h, wg_ref[...], preferred_element_type=F32)
        uu = jnp.dot(h, wu_ref[...], preferred_element_type=F32)
        sg = _sig(gg)
        silu = gg * sg
        go_ref[...] = (uu * (sg * (1.0 + gg * (1.0 - sg)))).astype(BF16)
        uo_ref[...] = silu.astype(BF16)
        a_ref[...] = (silu * uu).astype(BF16)

    wide = pl.BlockSpec((tm, FH), lambda i, j: (i, j))
    return pl.pallas_call(
        body, name="ffn_up", grid=(s // tm, 2),
        in_specs=[pl.BlockSpec((tm, D), lambda i, j: (i, 0)), pl.BlockSpec((1, D), lambda i, j: (0, 0)),
                  pl.BlockSpec((None, None, D, FH), lambda i, j: (l, j, 0, 0)),
                  pl.BlockSpec((None, None, D, FH), lambda i, j: (l, j + 2, 0, 0))] + [ANY] * len(deps),
        out_specs=[pl.BlockSpec((tm, D), lambda i, j: (i, 0)), wide, wide, wide],
        out_shape=[SDS((s, D), BF16), SDS((s, DFF), BF16), SDS((s, DFF), BF16), SDS((s, DFF), BF16)],
        compiler_params=_cp("parallel", "arbitrary"),
    )(x, pre_g, wgu, wgu, *deps)


def _mm_rms_res(a, w, l, x, g, c, tm, tk, name):
    s, k_dim = a.shape
    nk = k_dim // tk

    def body(a_ref, w_ref, x_ref, g_ref, z_ref, x1_ref):
        k = pl.program_id(1)
        p = jnp.dot(a_ref[...], w_ref[...], preferred_element_type=F32)

        @pl.when(k == 0)
        def _():
            z_ref[...] = p

        @pl.when(k > 0)
        def _():
            z_ref[...] += p

        @pl.when(k == nk - 1)
        def _():
            z = z_ref[...]
            x1_ref[...] = x_ref[...] + c * (z * _rsq(z, NORM_EPS) * g_ref[...])

    row = pl.BlockSpec((tm, D), lambda i, k: (i, 0))
    return pl.pallas_call(
        body, name=name, grid=(s // tm, nk),
        in_specs=[pl.BlockSpec((tm, tk), lambda i, k: (i, k)), pl.BlockSpec((None, tk, D), lambda i, k: (l, k, 0)),
                  row, pl.BlockSpec((1, D), lambda i, k: (0, 0))],
        out_specs=[row, row],
        out_shape=[SDS((s, D), F32), SDS((s, D), F32)],
        compiler_params=_cp("parallel", "arbitrary"),
    )(a, w, x, g)


def _rms_bwd(dy, z, g, c, tm, name, deps=()):
    s = z.shape[0]
    deps = list(deps)

    def body(dy_ref, z_ref, g_ref, *rest):
        dz_ref, dg_ref = rest[len(deps):]
        dz, dgr = _rms_bwd_rows(z_ref[...], g_ref[...], c * dy_ref[...])
        dz_ref[...] = dz.astype(BF16)
        part = jnp.sum(dgr, axis=0, keepdims=True)

        @pl.when(pl.program_id(0) == 0)
        def _():
            dg_ref[...] = part

        @pl.when(pl.program_id(0) > 0)
        def _():
            dg_ref[...] += part

    row = pl.BlockSpec((tm, D), lambda i: (i, 0))
    vec = pl.BlockSpec((1, D), lambda i: (0, 0))
    return pl.pallas_call(
        body, name=name, grid=(s // tm,), in_specs=[row, row, vec] + [ANY] * len(deps), out_specs=[row, vec],
        out_shape=[SDS((s, D), BF16), SDS((1, D), F32)], compiler_params=_cp("arbitrary"),
    )(dy, z, g, *deps)


def _ffn_bwd_mid(dz, wd, l, dadg, dadu, tm):
    s = dz.shape[0]

    def body(dz_ref, wd_ref, g_ref, u_ref, dg_ref, du_ref):
        da = lax.dot_general(dz_ref[...], wd_ref[...], NT, preferred_element_type=F32)
        dg_ref[...] = (da * g_ref[...].astype(F32)).astype(BF16)
        du_ref[...] = (da * u_ref[...].astype(F32)).astype(BF16)

    wide = pl.BlockSpec((tm, FH), lambda i, j: (i, j))
    return pl.pallas_call(
        body, name="ffn_bwd_mid", grid=(s // tm, 2),
        in_specs=[pl.BlockSpec((tm, D), lambda i, j: (i, 0)), pl.BlockSpec((None, FH, D), lambda i, j: (l, j, 0)), wide, wide],
        out_specs=[wide, wide],
        out_shape=[SDS((s, DFF), BF16), SDS((s, DFF), BF16)],
        compiler_params=_cp("parallel", "arbitrary"),
    )(dz, wd, dadg, dadu)


def _ffn_bwd_dh(dg, du, wgu, l, x, pre_g, dx1, tm, deps=()):
    s = x.shape[0]
    deps = list(deps)

    def body(dg_ref, du_ref, wg_ref, wu_ref, x_ref, g_ref, dx1_ref, *rest):
        dx_ref, dgp_ref = rest[len(deps):]
        i, k = pl.program_id(0), pl.program_id(1)
        p = (lax.dot_general(dg_ref[...], wg_ref[...], NT, preferred_element_type=F32)
             + lax.dot_general(du_ref[...], wu_ref[...], NT, preferred_element_type=F32))

        @pl.when(k == 0)
        def _():
            dx_ref[...] = p

        @pl.when(k == 1)
        def _():
            dx, dgr = _rms_bwd_rows(x_ref[...], g_ref[...], dx_ref[...] + p)
            dx_ref[...] = dx1_ref[...] + dx
            part = jnp.sum(dgr, axis=0, keepdims=True)

            @pl.when(i == 0)
            def _():
                dgp_ref[...] = part

            @pl.when(i > 0)
            def _():
                dgp_ref[...] += part

    wide = pl.BlockSpec((tm, FH), lambda i, k: (i, k))
    row = pl.BlockSpec((tm, D), lambda i, k: (i, 0))
    vec = pl.BlockSpec((1, D), lambda i, k: (0, 0))
    return pl.pallas_call(
        body, name="ffn_bwd_dh", grid=(s // tm, 2),
        in_specs=[wide, wide, pl.BlockSpec((None, None, D, FH), lambda i, k: (l, k, 0, 0)),
                  pl.BlockSpec((None, None, D, FH), lambda i, k: (l, k + 2, 0, 0)), row, vec, row] + [ANY] * len(deps),
        out_specs=[row, vec],
        out_shape=[SDS((s, D), F32), SDS((1, D), F32)],
        compiler_params=_cp("arbitrary", "arbitrary"),
    )(dg, du, wgu, wgu, x, pre_g, dx1, *deps)


def _mm_tn_into(buf, a, b, l, joff, tka, tn, ts, name):
    s, ka = a.shape
    n = b.shape[1]

    def body(buf_ref, a_ref, b_ref, o_ref):
        p = lax.dot_general(a_ref[...], b_ref[...], TN, preferred_element_type=F32)

        @pl.when(pl.program_id(2) == 0)
        def _():
            o_ref[...] = p

        @pl.when(pl.program_id(2) > 0)
        def _():
            o_ref[...] += p

    return pl.pallas_call(
        body, name=name, grid=(ka // tka, n // tn, s // ts),
        in_specs=[pl.BlockSpec(memory_space=pl.ANY),
                  pl.BlockSpec((ts, tka), lambda ia, j, t: (t, ia)), pl.BlockSpec((ts, tn), lambda ia, j, t: (t, j))],
        out_specs=pl.BlockSpec((None, None, tka, tn), lambda ia, j, t: (l, joff + j, ia, 0)),
        out_shape=SDS(buf.shape, F32), input_output_aliases={0: 0},
        compiler_params=_cp("parallel", "parallel", "arbitrary"),
    )(buf, a, b)


def _proj(x, g, w_in, l, tm):
    s = x.shape[0]

    def body(x_ref, g_ref, w_ref, h_ref, p_ref):
        xf = x_ref[...]
        h = (xf * _rsq(xf, NORM_EPS) * g_ref[...]).astype(BF16)
        h_ref[...] = h
        p_ref[...] = jnp.dot(h, w_ref[...], preferred_element_type=F32)

    return pl.pallas_call(
        body, name="proj", grid=(s // tm,),
        in_specs=[pl.BlockSpec((tm, D), lambda i: (i, 0)), pl.BlockSpec((1, D), lambda i: (0, 0)),
                  pl.BlockSpec((None, D, P_IN), lambda i: (l, 0, 0))],
        out_specs=[pl.BlockSpec((tm, D), lambda i: (i, 0)), pl.BlockSpec((tm, P_IN), lambda i: (i, 0))],
        out_shape=[SDS((s, D), BF16), SDS((s, P_IN), F32)],
        compiler_params=_cp("parallel"),
    )(x, g, w_in)


def _mm_nt(a, w, l, tm, name):
    s, k_dim = a.shape
    n = w.shape[1]

    def body(a_ref, w_ref, o_ref):
        o_ref[...] = lax.dot_general(a_ref[...], w_ref[...], NT, preferred_element_type=F32)

    return pl.pallas_call(
        body, name=name, grid=(s // tm,),
        in_specs=[pl.BlockSpec((tm, k_dim), lambda i: (i, 0)), pl.BlockSpec((None, n, k_dim), lambda i: (l, 0, 0))],
        out_specs=pl.BlockSpec((tm, n), lambda i: (i, 0)),
        out_shape=SDS((s, n), F32), compiler_params=_cp("parallel"),
    )(a, w)


def _mm_nt_rmsbwd(dp, w_in, l, x, g, dx1, tm):
    s = x.shape[0]

    def body(dp_ref, w_ref, x_ref, g_ref, dx1_ref, dx_ref, dg_ref):
        dh = lax.dot_general(dp_ref[...], w_ref[...], NT, preferred_element_type=F32)
        dx, dgr = _rms_bwd_rows(x_ref[...], g_ref[...], dh)
        dx_ref[...] = dx1_ref[...] + dx
        part = jnp.sum(dgr, axis=0, keepdims=True)

        @pl.when(pl.program_id(0) == 0)
        def _():
            dg_ref[...] = part

        @pl.when(pl.program_id(0) > 0)
        def _():
            dg_ref[...] += part

    row = pl.BlockSpec((tm, D), lambda i: (i, 0))
    vec = pl.BlockSpec((1, D), lambda i: (0, 0))
    return pl.pallas_call(
        body, name="mix_bwd_dx", grid=(s // tm,),
        in_specs=[pl.BlockSpec((tm, P_IN), lambda i: (i, 0)), pl.BlockSpec((None, D, P_IN), lambda i: (l, 0, 0)), row, vec, row],
        out_specs=[row, vec], out_shape=[SDS((s, D), F32), SDS((1, D), F32)],
        compiler_params=_cp("arbitrary"),
    )(dp, w_in, x, g, dx1)


def _row_iota(shape):
    return lax.broadcasted_iota(jnp.int32, shape, 0)


def _lru_gates(xc, wa_ref, ba_ref, wx_ref, bx_ref, lam_ref):
    xb = xc.astype(BF16)
    r = _sig(jnp.dot(xb, wa_ref[...], preferred_element_type=F32) + ba_ref[...])
    ig = _sig(jnp.dot(xb, wx_ref[...], preferred_element_type=F32) + bx_ref[...])
    nl = -lam_ref[...]
    sp = jnp.maximum(nl, 0.0) + jnp.log(1.0 + jnp.exp(-jnp.abs(nl)))
    log_a = -LRU_C * r * sp
    a = jnp.exp(log_a)
    x2 = 2.0 * log_a
    series = x2 * (1.0 + x2 * (0.5 + x2 * (1.0 / 6.0 + x2 * (1.0 / 24.0 + x2 * (1.0 / 120.0)))))
    em1 = jnp.where(x2 > -0.05, series, jnp.exp(x2) - 1.0)
    mlt = jnp.sqrt(-em1)
    return r, ig, a, mlt, sp


def _conv_taps(src_ref, w_ref, k_taps, pad, tc):
    acc = None
    for j in range(k_taps):
        term = w_ref[j:j + 1, :] * src_ref[pl.ds(pad - (k_taps - 1) + j, tc), :]
        acc = term if acc is None else acc + term
    return acc


def _gelu_parts(x):
    c0 = math.sqrt(2.0 / math.pi)
    inner = c0 * (x + 0.044715 * x * x * x)
    t = jnp.tanh(inner)
    gl = 0.5 * x * (1.0 + t)
    dgl = 0.5 * (1.0 + t) + 0.5 * x * (1.0 - t * t) * c0 * (1.0 + 3.0 * 0.044715 * x * x)
    return gl, dgl


def _lru_fwd(proj, cw, cb, wa, ba, wx, bx, lam, gg, tc):
    s = proj.shape[0]
    pad = 8

    def body(xcur_ref, xprev_ref, gate_ref, cw_ref, cb_ref, wa_ref, ba_ref, wx_ref, bx_ref, lam_ref, gg_ref,
             yn_ref, h_ref, xs_ref, hc_ref):
        i = pl.program_id(0)

        @pl.when(i == 0)
        def _():
            hc_ref[...] = jnp.zeros_like(hc_ref)

        xs_ref[0:pad, :] = jnp.where(i > 0, xprev_ref[tc - pad:tc, :], 0.0)
        xs_ref[pad:pad + tc, :] = xcur_ref[...]
        xc = _conv_taps(xs_ref, cw_ref, LRU_K, pad, tc) + cb_ref[...]
        _, ig, a, mlt, _ = _lru_gates(xc, wa_ref, ba_ref, wx_ref, bx_ref, lam_ref)
        u = mlt * (ig * xc)
        row = _row_iota((tc, W_A))
        d = 1
        while d < tc:
            ok = row >= d
            a_sh = jnp.where(ok, pltpu.roll(a, d, axis=0), 1.0)
            u_sh = jnp.where(ok, pltpu.roll(u, d, axis=0), 0.0)
            u = a * u_sh + u
            a = a * a_sh
            d *= 2
        h = u + a * hc_ref[...]
        hc_ref[...] = jnp.sum(jnp.where(row == tc - 1, h, 0.0), axis=0, keepdims=True)
        h_ref[...] = h
        gl, _ = _gelu_parts(gate_ref[...])
        ya = gl * h
        yn_ref[...] = (ya * _rsq(ya, NORM_EPS) * gg_ref[...]).astype(BF16)

    blk = lambda c: pl.BlockSpec((tc, W_A), lambda i, c=c: (i, c))
    full = lambda a: pl.BlockSpec(a.shape, lambda i: (0,) * a.ndim)
    params = [cw, cb, wa, ba, wx, bx, lam, gg]
    return pl.pallas_call(
        body, name="lru_fwd", grid=(s // tc,),
        in_specs=[blk(0), pl.BlockSpec((tc, W_A), lambda i: (jnp.maximum(i - 1, 0), 0)), blk(1)] + [full(a) for a in params],
        out_specs=[pl.BlockSpec((tc, W_A), lambda i: (i, 0))] * 2,
        out_shape=[SDS((s, W_A), BF16), SDS((s, W_A), F32)],
        scratch_shapes=[pltpu.VMEM((tc + pad, W_A), F32), pltpu.VMEM((1, W_A), F32)],
        compiler_params=_cp("arbitrary"),
    )(proj, proj, proj, *params)


def _acc(ref, first, val):
    @pl.when(first)
    def _():
        ref[...] = val

    @pl.when(jnp.logical_not(first))
    def _():
        ref[...] += val


def _lru_bwd(dy, proj, h, cw, cb, wa, ba, wx, bx, lam, gg, tc):
    s = proj.shape[0]
    nc = s // tc
    pad = 8

    def body(dy_ref, xcur_ref, xprev_ref, gate_ref, h_ref, hprev_ref, cw_ref, cb_ref, wa_ref, ba_ref, wx_ref, bx_ref,
             lam_ref, gg_ref,
             dp_ref, dcw_ref, dcb_ref, dwa_ref, dba_ref, dwx_ref, dbx_ref, dlam_ref, dgg_ref,
             xs_ref, ds_ref, mu_ref, nx_ref):
        step = pl.program_id(0)
        i = nc - 1 - step
        first = step == 0

        @pl.when(first)
        def _():
            mu_ref[...] = jnp.zeros_like(mu_ref)
            nx_ref[...] = jnp.zeros_like(nx_ref)

        xs_ref[0:pad, :] = jnp.where(i > 0, xprev_ref[tc - pad:tc, :], 0.0)
        xs_ref[pad:pad + tc, :] = xcur_ref[...]
        xc = _conv_taps(xs_ref, cw_ref, LRU_K, pad, tc) + cb_ref[...]
        r, ig, a, mlt, sp = _lru_gates(xc, wa_ref, ba_ref, wx_ref, bx_ref, lam_ref)
        hh = h_ref[...]
        gate = gate_ref[...]
        gl, dgl = _gelu_parts(gate)
        ya = gl * hh
        dya, dggr = _rms_bwd_rows(ya, gg_ref[...], dy_ref[...])
        _acc(dgg_ref, first, jnp.sum(dggr, axis=0, keepdims=True))
        dp_ref[:, W_A:2 * W_A] = dya * hh * dgl
        dh = dya * gl

        row = _row_iota((tc, W_A))
        aa = a
        uu = a * dh
        d = 1
        while d < tc:
            ok = row < tc - d
            a_sh = jnp.where(ok, pltpu.roll(aa, tc - d, axis=0), 1.0)
            u_sh = jnp.where(ok, pltpu.roll(uu, tc - d, axis=0), 0.0)
            uu = uu + aa * u_sh
            aa = aa * a_sh
            d *= 2
        cin = mu_ref[...]
        mu = uu + aa * cin
        lam_t = dh + jnp.where(row == tc - 1, cin, pltpu.roll(mu, tc - 1, axis=0))
        mu_ref[...] = jnp.sum(jnp.where(row == 0, mu, 0.0), axis=0, keepdims=True)
        hm1 = jnp.where(row == 0, jnp.where(i > 0, pltpu.roll(hprev_ref[...], 1, axis=0), 0.0),
                        pltpu.roll(hh, 1, axis=0))
        da = lam_t * hm1
        du = lam_t
        dmlt = du * ig * xc
        dig = du * mlt * xc
        dxc = du * mlt * ig
        dlog_a = da * a - dmlt * (a * a / mlt)
        dr = dlog_a * (-LRU_C * sp)
        dsp = jnp.sum(dlog_a * (-LRU_C * r), axis=0, keepdims=True)
        _acc(dlam_ref, first, dsp * (-_sig(-lam_ref[...])))
        dga = dr * r * (1.0 - r)
        dgx = dig * ig * (1.0 - ig)
        _acc(dba_ref, first, jnp.sum(dga, axis=0, keepdims=True))
        _acc(dbx_ref, first, jnp.sum(dgx, axis=0, keepdims=True))
        xb = xc.astype(BF16)
        dgab = dga.astype(BF16)
        dgxb = dgx.astype(BF16)
        _acc(dwa_ref, first, lax.dot_general(xb, dgab, TN, preferred_element_type=F32))
        _acc(dwx_ref, first, lax.dot_general(xb, dgxb, TN, preferred_element_type=F32))
        dxc = (dxc + lax.dot_general(dgab, wa_ref[...], NT, preferred_element_type=F32)
               + lax.dot_general(dgxb, wx_ref[...], NT, preferred_element_type=F32))

        _acc(dcb_ref, first, jnp.sum(dxc, axis=0, keepdims=True))
        r8 = _row_iota((8, W_A))
        dcw = jnp.zeros((8, W_A), F32)
        for j in range(LRU_K):
            tap = jnp.sum(dxc * xs_ref[pl.ds(pad - (LRU_K - 1) + j, tc), :], axis=0, keepdims=True)
            dcw = dcw + jnp.where(r8 == j, tap, 0.0)
        _acc(dcw_ref, first, dcw)
        ds_ref[0:tc, :] = dxc
        ds_ref[tc:tc + pad, :] = nx_ref[...]
        dlx = None
        for j in range(LRU_K):
            term = cw_ref[j:j + 1, :] * ds_ref[pl.ds(LRU_K - 1 - j, tc), :]
            dlx = term if dlx is None else dlx + term
        dp_ref[:, 0:W_A] = dlx
        nx_ref[...] = dxc[0:pad, :]

    rev = lambda c: pl.BlockSpec((tc, W_A), lambda t, c=c: (nc - 1 - t, c))
    prev = lambda c: pl.BlockSpec((tc, W_A), lambda t, c=c: (jnp.maximum(nc - 2 - t, 0), c))
    full = lambda a: pl.BlockSpec(a.shape, lambda t: (0,) * a.ndim)
    params = [cw, cb, wa, ba, wx, bx, lam, gg]
    vec = SDS((1, W_A), F32)
    sq = SDS((W_A, W_A), F32)
    outs = [SDS((s, 2 * W_A), F32), SDS((8, W_A), F32), vec, sq, vec, sq, vec, vec, vec]
    return pl.pallas_call(
        body, name="lru_bwd", grid=(nc,),
        in_specs=[rev(0), rev(0), prev(0), rev(1), rev(0), prev(0)] + [full(a) for a in params],
        out_specs=[pl.BlockSpec((tc, 2 * W_A), lambda t: (nc - 1 - t, 0))]
        + [pl.BlockSpec(o.shape, lambda t: (0, 0)) for o in outs[1:]],
        out_shape=outs,
        scratch_shapes=[pltpu.VMEM((tc + pad, W_A), F32), pltpu.VMEM((tc + pad, W_A), F32),
                        pltpu.VMEM((1, W_A), F32), pltpu.VMEM((pad, W_A), F32)],
        compiler_params=_cp("arbitrary"),
    )(dy, proj, proj, proj, h, h, *params)


def _attn_stack(qa, qb, kvh):
    lane = lax.broadcasted_iota(jnp.int32, qa.shape, 1)
    keep = (lane >= HD) if kvh == 1 else (lane < HD)
    parts = []
    for tile in (qa, qb):
        for half in (0, 1):
            y = tile if half == kvh else pltpu.roll(tile, HD, axis=1)
            parts.append(jnp.where(keep, y, 0.0))
    return jnp.concatenate(parts, axis=0)


def _attn_unstack(o, kvh):
    lane = lax.broadcasted_iota(jnp.int32, (BLK, 2 * HD), 1)
    tiles = []
    for t in range(2):
        halves = []
        for half in (0, 1):
            blk = o[(2 * t + half) * BLK:(2 * t + half + 1) * BLK, :]
            halves.append(blk if half == kvh else pltpu.roll(blk, HD, axis=1))
        tiles.append(jnp.where(lane < HD, halves[0], halves[1]))
    return tiles


def _attn_stack_all(x_ref_or_val):
    return jnp.concatenate([_attn_stack(x_ref_or_val[:, 256 * kvh:256 * kvh + 128],
                                        x_ref_or_val[:, 256 * kvh + 128:256 * kvh + 256], kvh) for kvh in range(2)], axis=0)


def _attn_unstack_all(o, dst_ref):
    for kvh in range(2):
        ta, tb = _attn_unstack(o[4 * BLK * kvh:4 * BLK * (kvh + 1), :], kvh)
        dst_ref[:, 256 * kvh:256 * kvh + 128] = ta
        dst_ref[:, 256 * kvh + 128:256 * kvh + 256] = tb


def _attn_windows(cur_ref, prev_ref, nb):
    blocks = [prev_ref[...]] + [cur_ref[b * BLK:(b + 1) * BLK, :] for b in range(nb)]
    return [jnp.concatenate(blocks[b:b + 2], axis=0).astype(BF16) for b in range(nb)]


def _attn_probs(qs, kw, n, sink_ref):
    rows = NQ * BLK
    sc = lax.dot_general(qs.astype(BF16), kw, NT, preferred_element_type=F32) * SCALE
    qi = lax.broadcasted_iota(jnp.int32, (rows, 2 * BLK), 0) & (BLK - 1)
    kj = lax.broadcasted_iota(jnp.int32, (rows, 2 * BLK), 1)
    rel = BLK + qi - kj
    mask = (rel >= 0) & (rel < BLK) & ((n - 1) * BLK + kj >= 0)
    head = lax.broadcasted_iota(jnp.int32, (rows, 1), 0) // BLK
    sk = jnp.zeros((rows, 1), F32)
    for h in range(NQ):
        sk = jnp.where(head == h, sink_ref[h:h + 1, 0:1], sk)
    sh = jnp.where(mask, sc, NEG_BIG)
    m = jnp.maximum(jnp.max(sh, axis=-1, keepdims=True), sk)
    e = jnp.exp(sh - m)
    es = jnp.exp(sk - m)
    rz = 1.0 / (jnp.sum(e, axis=-1, keepdims=True) + es)
    return e * rz, es * rz


def _attn_fwd(proj, sinks8, gg):
    s = proj.shape[0]
    nb = ATT_NB_FWD

    def body(q_ref, kc_ref, kp_ref, vc_ref, vp_ref, sink_ref, gg_ref, yn_ref, ob_ref):
        kws, vws = _attn_windows(kc_ref, kp_ref, nb), _attn_windows(vc_ref, vp_ref, nb)
        for b in range(nb):
            rows = pl.ds(b * BLK, BLK)
            p, _ = _attn_probs(_attn_stack_all(q_ref.at[rows, :]), kws[b], nb * pl.program_id(0) + b, sink_ref)
            _attn_unstack_all(jnp.dot(p.astype(BF16), vws[b], preferred_element_type=F32), ob_ref.at[rows, :])
        ob = ob_ref[...]
        yn_ref[...] = (ob * _rsq(ob, NORM_EPS) * gg_ref[...]).astype(BF16)

    tb = nb * BLK
    cur = lambda c: pl.BlockSpec((tb, 128), lambda m, c=c: (m, c))
    prev = lambda c: pl.BlockSpec((BLK, 128), lambda m, c=c: (jnp.maximum(nb * m - 1, 0), c))
    out = pl.BlockSpec((tb, W_B), lambda m: (m, 0))
    return pl.pallas_call(
        body, name="attn_fwd", grid=(s // tb,),
        in_specs=[pl.BlockSpec((tb, W_B), lambda m: (m, 1)), cur(8), prev(8), cur(9), prev(9),
                  pl.BlockSpec((8, 128), lambda n: (0, 0)), pl.BlockSpec((1, W_B), lambda n: (0, 0))],
        out_specs=[out, out], out_shape=[SDS((s, W_B), BF16), SDS((s, W_B), F32)],
        compiler_params=_cp("parallel"),
    )(proj, proj, proj, proj, proj, sinks8, gg)


def _attn_bwd(dy, proj, ob, sinks8, gg):
    s = proj.shape[0]
    nb = ATT_NB_BWD

    def body(dya_ref, dyb_ref, q_ref, kc_ref, kp_ref, vc_ref, vp_ref, ob_ref, sink_ref, gg_ref,
             dq_ref, dcur_ref, dprev_ref, dsink_ref, dgg_ref):
        first = pl.program_id(0) == 0
        kws, vws = _attn_windows(kc_ref, kp_ref, nb), _attn_windows(vc_ref, vp_ref, nb)
        dyn = jnp.concatenate([dya_ref[...], dyb_ref[...]], axis=1)
        dob, dggr = _rms_bwd_rows(ob_ref[...], gg_ref[...], dyn)
        _acc(dgg_ref, first, jnp.sum(dggr, axis=0, keepdims=True))
        r8 = _row_iota((8, 128))
        dsk = jnp.zeros((8, 128), F32)
        for b in range(nb):
            rows = pl.ds(b * BLK, BLK)
            qs = _attn_stack_all(q_ref.at[rows, :])
            p, psink = _attn_probs(qs, kws[b], nb * pl.program_id(0) + b, sink_ref)
            dosb = _attn_stack_all(dob[b * BLK:(b + 1) * BLK, :]).astype(BF16)
            dp = lax.dot_general(dosb, vws[b], NT, preferred_element_type=F32)
            dd = jnp.sum(p * dp, axis=-1, keepdims=True)
            dsb = (p * (dp - dd) * SCALE).astype(BF16)
            dsink_rows = -psink * dd
            for h in range(NQ):
                dsk = dsk + jnp.where(r8 == h, jnp.sum(dsink_rows[h * BLK:(h + 1) * BLK, :], axis=0, keepdims=True), 0.0)
            _attn_unstack_all(jnp.dot(dsb, kws[b], preferred_element_type=F32), dq_ref.at[rows, :])
            dkw = lax.dot_general(dsb, qs.astype(BF16), TN, preferred_element_type=F32)
            dvw = lax.dot_general(p.astype(BF16), dosb, TN, preferred_element_type=F32)
            dprev_ref[rows, 0:128] = dkw[0:BLK, :]
            dprev_ref[rows, 128:256] = dvw[0:BLK, :]
            dcur_ref[rows, 0:128] = dkw[BLK:2 * BLK, :]
            dcur_ref[rows, 128:256] = dvw[BLK:2 * BLK, :]
        _acc(dsink_ref, first, dsk)

    tb = nb * BLK
    cur = lambda c: pl.BlockSpec((tb, 128), lambda m, c=c: (m, c))
    prev = lambda c: pl.BlockSpec((BLK, 128), lambda m, c=c: (jnp.maximum(nb * m - 1, 0), c))
    wide = pl.BlockSpec((tb, W_B), lambda m: (m, 0))
    half = pl.BlockSpec((tb, 256), lambda m: (m, 0))
    return pl.pallas_call(
        body, name="attn_bwd", grid=(s // tb,),
        in_specs=[pl.BlockSpec((tb, 256), lambda m: (m, 1)), pl.BlockSpec((tb, 256), lambda m: (m, 2)),
                  pl.BlockSpec((tb, W_B), lambda m: (m, 1)), cur(8), prev(8), cur(9), prev(9), wide,
                  pl.BlockSpec((8, 128), lambda n: (0, 0)), pl.BlockSpec((1, W_B), lambda n: (0, 0))],
        out_specs=[wide, half, half, pl.BlockSpec((8, 128), lambda n: (0, 0)), pl.BlockSpec((1, W_B), lambda n: (0, 0))],
        out_shape=[SDS((s, W_B), F32), SDS((s, 256), F32), SDS((s, 256), F32), SDS((8, 128), F32), SDS((1, W_B), F32)],
        compiler_params=_cp("arbitrary"),
    )(dy, dy, proj, proj, proj, proj, proj, ob, sinks8, gg)


def _ln_parts(y1, eps=LN_EPS):
    mu = jnp.mean(y1, axis=-1, keepdims=True)
    xc = y1 - mu
    rstd = lax.rsqrt(jnp.mean(xc * xc, axis=-1, keepdims=True) + eps)
    return xc * rstd, rstd


def _conf_fwd(proj, cw, cb, lg, lb, gg, tc):
    s = proj.shape[0]
    pad = 32

    def body(ac_ref, gc_ref, ap_ref, gp_ref, cw_ref, cb_ref, lg_ref, lb_ref, gg_ref, yn_ref, y1_ref, ys_ref):
        i = pl.program_id(0)
        tail = ap_ref[tc - pad:tc, :] * _sig(gp_ref[tc - pad:tc, :])
        ys_ref[0:pad, :] = jnp.where(i > 0, tail, 0.0)
        ys_ref[pad:pad + tc, :] = ac_ref[...] * _sig(gc_ref[...])
        y1 = _conv_taps(ys_ref, cw_ref, CONV_K, pad, tc) + cb_ref[...]
        y1_ref[...] = y1
        xh, _ = _ln_parts(y1)
        yl = xh * lg_ref[...] + lb_ref[...]
        yc = yl * _sig(yl)
        yn_ref[...] = (yc * _rsq(yc, NORM_EPS) * gg_ref[...]).astype(BF16)

    cur = lambda c: pl.BlockSpec((tc, W_C), lambda i, c=c: (i, c))
    prev = lambda c: pl.BlockSpec((tc, W_C), lambda i, c=c: (jnp.maximum(i - 1, 0), c))
    full = lambda a: pl.BlockSpec(a.shape, lambda i: (0,) * a.ndim)
    params = [cw, cb, lg, lb, gg]
    out = pl.BlockSpec((tc, W_C), lambda i: (i, 0))
    return pl.pallas_call(
        body, name="conf_fwd", grid=(s // tc,),
        in_specs=[cur(5), cur(6), prev(5), prev(6)] + [full(a) for a in params],
        out_specs=[out, out], out_shape=[SDS((s, W_C), BF16), SDS((s, W_C), F32)],
        scratch_shapes=[pltpu.VMEM((tc + pad, W_C), F32)],
        compiler_params=_cp("parallel"),
    )(proj, proj, proj, proj, *params)


def _conf_bwd(dy, proj, y1, cw, cb, lg, lb, gg, tc):
    s = proj.shape[0]
    nc = s // tc
    pad = 32

    def body(dy_ref, ac_ref, gc_ref, ap_ref, gp_ref, y1_ref, cw_ref, cb_ref, lg_ref, lb_ref, gg_ref,
             dp_ref, dcw_ref, dcb_ref, dlg_ref, dlb_ref, dgg_ref, ys_ref, ds_ref, nx_ref):
        step = pl.program_id(0)
        i = nc - 1 - step
        first = step == 0

        @pl.when(first)
        def _():
            nx_ref[...] = jnp.zeros_like(nx_ref)

        a = ac_ref[...]
        sg = _sig(gc_ref[...])
        tail = ap_ref[tc - pad:tc, :] * _sig(gp_ref[tc - pad:tc, :])
        ys_ref[0:pad, :] = jnp.where(i > 0, tail, 0.0)
        ys_ref[pad:pad + tc, :] = a * sg
        xh, rstd = _ln_parts(y1_ref[...])
        yl = xh * lg_ref[...] + lb_ref[...]
        sl = _sig(yl)
        yc = yl * sl
        dyc, dggr = _rms_bwd_rows(yc, gg_ref[...], dy_ref[...])
        _acc(dgg_ref, first, jnp.sum(dggr, axis=0, keepdims=True))
        dyl = dyc * sl * (1.0 + yl * (1.0 - sl))
        _acc(dlg_ref, first, jnp.sum(dyl * xh, axis=0, keepdims=True))
        _acc(dlb_ref, first, jnp.sum(dyl, axis=0, keepdims=True))
        dxh = dyl * lg_ref[...]
        dy1 = rstd * (dxh - jnp.mean(dxh, axis=-1, keepdims=True) - xh * jnp.mean(dxh * xh, axis=-1, keepdims=True))
        _acc(dcb_ref, first, jnp.sum(dy1, axis=0, keepdims=True))
        r32 = _row_iota((32, W_C))
        dcw = jnp.zeros((32, W_C), F32)
        for j in range(CONV_K):
            tap = jnp.sum(dy1 * ys_ref[pl.ds(pad - (CONV_K - 1) + j, tc), :], axis=0, keepdims=True)
            dcw = dcw + jnp.where(r32 == j, tap, 0.0)
        _acc(dcw_ref, first, dcw)
        ds_ref[0:tc, :] = dy1
        ds_ref[tc:tc + pad, :] = nx_ref[...]
        dy0 = None
        for j in range(CONV_K):
            term = cw_ref[j:j + 1, :] * ds_ref[pl.ds(CONV_K - 1 - j, tc), :]
            dy0 = term if dy0 is None else dy0 + term
        dp_ref[:, 0:W_C] = dy0 * sg
        dp_ref[:, W_C:2 * W_C] = dy0 * a * sg * (1.0 - sg)
        nx_ref[...] = dy1[0:pad, :]

    rev = lambda c: pl.BlockSpec((tc, W_C), lambda t, c=c: (nc - 1 - t, c))
    prev = lambda c: pl.BlockSpec((tc, W_C), lambda t, c=c: (jnp.maximum(nc - 2 - t, 0), c))
    full = lambda a: pl.BlockSpec(a.shape, lambda t: (0,) * a.ndim)
    params = [cw, cb, lg, lb, gg]
    vec = SDS((1, W_C), F32)
    outs = [SDS((s, 2 * W_C), F32), SDS((32, W_C), F32), vec, vec, vec, vec]
    return pl.pallas_call(
        body, name="conf_bwd", grid=(nc,),
        in_specs=[rev(3), rev(5), rev(6), prev(5), prev(6), rev(0)] + [full(a) for a in params],
        out_specs=[pl.BlockSpec((tc, 2 * W_C), lambda t: (nc - 1 - t, 0))]
        + [pl.BlockSpec(o.shape, lambda t: (0, 0)) for o in outs[1:]],
        out_shape=outs,
        scratch_shapes=[pltpu.VMEM((tc + pad, W_C), F32), pltpu.VMEM((tc + pad, W_C), F32), pltpu.VMEM((pad, W_C), F32)],
        compiler_params=_cp("arbitrary"),
    )(dy, proj, proj, proj, proj, y1, *params)


def _assemble_dproj(dlru, dq, dcur, dprev, dconf):
    s = dq.shape[0]
    nb = s // BLK

    def body(dl_ref, dq_ref, dc_ref, dn_ref, df_ref, o_ref):
        n = pl.program_id(0)
        o_ref[:, 0:512] = dl_ref[...].astype(BF16)
        o_ref[:, 512:1024] = dq_ref[...].astype(BF16)
        o_ref[:, 1024:1280] = (dc_ref[...] + jnp.where(n < nb - 1, dn_ref[...], 0.0)).astype(BF16)
        o_ref[:, 1280:1792] = df_ref[...].astype(BF16)

    wide = pl.BlockSpec((BLK, 512), lambda n: (n, 0))
    return pl.pallas_call(
        body, name="assemble_dproj", grid=(nb,),
        in_specs=[wide, wide, pl.BlockSpec((BLK, 256), lambda n: (n, 0)),
                  pl.BlockSpec((BLK, 256), lambda n: (jnp.minimum(n + 1, nb - 1), 0)), wide],
        out_specs=pl.BlockSpec((BLK, P_IN), lambda n: (n, 0)), out_shape=SDS((s, P_IN), BF16),
        compiler_params=_cp("parallel"),
    )(dlru, dq, dcur, dprev, dconf)


def _loss_grad(y, t, tm):
    s = y.shape[0]

    def body(y_ref, t_ref, dy_ref, l_ref):
        err = y_ref[...] - t_ref[...]
        dy_ref[...] = err * (1.0 / D)
        _acc(l_ref, pl.program_id(0) == 0, jnp.sum(err * err, axis=0, keepdims=True))

    row = pl.BlockSpec((tm, D), lambda i: (i, 0))
    return pl.pallas_call(
        body, name="loss_grad", grid=(s // tm,), in_specs=[row, row],
        out_specs=[row, pl.BlockSpec((1, D), lambda i: (0, 0))],
        out_shape=[SDS((s, D), F32), SDS((1, D), F32)], compiler_params=_cp("arbitrary"),
    )(y, t)


def _block_diag(w):
    rows = [jnp.concatenate([w[h] if k == h else jnp.zeros((64, 64), w.dtype) for k in range(4)], axis=1) for h in range(4)]
    return jnp.concatenate(rows, axis=0)


def _diag_blocks(m):
    return jnp.stack([m[64 * h:64 * (h + 1), 64 * h:64 * (h + 1)] for h in range(4)])


def _layer_params(small, l):
    v = lambda name: small[name][l].reshape(1, -1)
    gg = small["group_g"][l]
    return dict(
        ffn1_pre=v("ffn1_pre_g"), ffn1_post=v("ffn1_post_g"), mix_pre=v("mix_pre_g"), mix_post=v("mix_post_g"),
        ffn2_pre=v("ffn2_pre_g"), ffn2_post=v("ffn2_post_g"), lru_cb=v("lru_conv_b"),
        wa=_block_diag(small["lru_w_a"][l]).astype(BF16), ba=v("lru_b_a"),
        wx=_block_diag(small["lru_w_x"][l]).astype(BF16), bx=v("lru_b_x"), lam=v("lru_lambda"),
        sinks8=jnp.broadcast_to(small["attn_sinks"][l][:, None], (NQ, 128)),
        conv_b=v("conv_b"), ln_g=v("conv_ln_g"), ln_b=v("conv_ln_b"),
        gg_a=gg[0:W_A].reshape(1, -1), gg_b=gg[W_A:W_A + W_B].reshape(1, -1), gg_c=gg[W_A + W_B:].reshape(1, -1),
    )


def _forward_layer(x, weights, p, tiles, deps=()):
    _, mm, _, tc = tiles
    big = dict(weights("ffn1_gu", x))
    p = dict(p)
    sv = dict(x0=x)
    h1, g1, u1, a1 = _ffn_up(x, p["ffn1_pre"], big["ffn1_w_gu"], 0, mm, deps)
    big.update(weights("ffn1_down", a1))
    z1, x = _mm_rms_res(a1, big["ffn1_w_down"], 0, x, p["ffn1_post"], 0.5, mm, FH, "ffn_down")
    sv.update(h1=h1, g1=g1, u1=u1, a1=a1, z1=z1, x1=x)
    big.update(weights("mix", x))
    p.update(lru_cw=big.pop("lru_conv_w"), conv_w=big.pop("conv_w"))
    hn, proj = _proj(x, p["mix_pre"], big["w_in"], 0, mm)
    yn_a, hl = _lru_fwd(proj, p["lru_cw"], p["lru_cb"], p["wa"], p["ba"], p["wx"], p["bx"], p["lam"], p["gg_a"], tc)
    yn_b, ob = _attn_fwd(proj, p["sinks8"], p["gg_b"])
    yn_c, y1 = _conf_fwd(proj, p["conv_w"], p["conv_b"], p["ln_g"], p["ln_b"], p["gg_c"], tc)
    ycat = jnp.concatenate([yn_a, yn_b, yn_c], axis=1)
    zo, x = _mm_rms_res(ycat, big["w_out"], 0, x, p["mix_post"], 1.0, mm, D, "mix_out")
    sv.update(hn=hn, proj=proj, hl=hl, ob=ob, y1=y1, ycat=ycat, zo=zo, x2=x)
    big.update(weights("ffn2", x))
    h2, g2, u2, a2 = _ffn_up(x, p["ffn2_pre"], big["ffn2_w_gu"], 0, mm)
    z2, x = _mm_rms_res(a2, big["ffn2_w_down"], 0, x, p["ffn2_post"], 0.5, mm, FH, "ffn_down")
    sv.update(h2=h2, g2=g2, u2=u2, a2=a2, z2=z2, p=p, big=big)
    return x, sv


def _grad_buffers():
    empty = lambda *shape: lax.empty(shape, F32)
    return dict(ffn1_w_gu=empty(1, NSHARD, D, FH), ffn2_w_gu=empty(1, NSHARD, D, FH), ffn1_w_down=empty(1, 1, DFF, D),
                ffn2_w_down=empty(1, 1, DFF, D), w_in=empty(1, 1, D, P_IN), w_out=empty(1, 1, D, D))


def _backward_layer(dx, sv, bufs, tiles, stage):
    p, big = sv["p"], sv["big"]
    tm, mm, dw, tc = tiles
    gr = {}

    def ffn_bwd(dx, which, xin, h, g, u, a, z, pre, post, deps):
        dz, dpost = _rms_bwd(dx, z, post, 0.5, tm, "ffn_post_bwd", deps)
        dg, du = _ffn_bwd_mid(dz, big[which + "_w_down"], 0, g, u, mm)
        bufs[which + "_w_down"] = _mm_tn_into(bufs[which + "_w_down"], a, dz, 0, 0, FH, D, dw, "dw_down")
        bufs[which + "_w_gu"] = _mm_tn_into(bufs[which + "_w_gu"], h, dg, 0, 0, D, FH, dw, "dw_gate")
        bufs[which + "_w_gu"] = _mm_tn_into(bufs[which + "_w_gu"], h, du, 0, 2, D, FH, dw, "dw_up")
        deps = stage({n: bufs[n] for n in (which + "_w_gu", which + "_w_down")}, bufs[which + "_w_gu"])
        dxn, dpre = _ffn_bwd_dh(dg, du, big[which + "_w_gu"], 0, xin, pre, dx, mm, deps)
        return dxn, dpre, dpost

    dx, gr["ffn2_pre_g"], gr["ffn2_post_g"] = ffn_bwd(dx, "ffn2", sv["x2"], sv["h2"], sv["g2"], sv["u2"], sv["a2"],
                                                      sv["z2"], p["ffn2_pre"], p["ffn2_post"], ())
    do, gr["mix_post_g"] = _rms_bwd(dx, sv["zo"], p["mix_post"], 1.0, tm, "mix_post_bwd")
    bufs["w_out"] = _mm_tn_into(bufs["w_out"], sv["ycat"], do, 0, 0, D, D, dw, "dw_out")
    dy = _mm_nt(do, big["w_out"], 0, mm, "mix_dy")
    proj = sv["proj"]
    (dlru, dcw, gr["lru_conv_b"], dwa, gr["lru_b_a"], dwx, gr["lru_b_x"], gr["lru_lambda"], dgg_a) = _lru_bwd(
        dy, proj, sv["hl"], p["lru_cw"], p["lru_cb"], p["wa"], p["ba"], p["wx"], p["bx"], p["lam"], p["gg_a"], tc)
    dq, dcur, dprev, dsk, dgg_b = _attn_bwd(dy, proj, sv["ob"], p["sinks8"], p["gg_b"])
    dconf, dconvw, gr["conv_b"], gr["conv_ln_g"], gr["conv_ln_b"], dgg_c = _conf_bwd(
        dy, proj, sv["y1"], p["conv_w"], p["conv_b"], p["ln_g"], p["ln_b"], p["gg_c"], tc)
    dproj = _assemble_dproj(dlru, dq, dcur, dprev, dconf)
    bufs["w_in"] = _mm_tn_into(bufs["w_in"], sv["hn"], dproj, 0, 0, D, P_IN, dw, "dw_in")
    dx, gr["mix_pre_g"] = _mm_nt_rmsbwd(dproj, big["w_in"], 0, sv["x1"], p["mix_pre"], dx, mm)
    gr["lru_conv_w"] = dcw[0:LRU_K]
    gr["lru_w_a"] = _diag_blocks(dwa)
    gr["lru_w_x"] = _diag_blocks(dwx)
    gr["attn_sinks"] = dsk[:, 0]
    gr["conv_w"] = dconvw[0:CONV_K]
    gr["group_g"] = jnp.concatenate([dgg_a, dgg_b, dgg_c], axis=1)
    dx, gr["ffn1_pre_g"], gr["ffn1_post_g"] = ffn_bwd(dx, "ffn1", sv["x0"], sv["h1"], sv["g1"], sv["u1"], sv["a1"],
                                                      sv["z1"], p["ffn1_pre"], p["ffn1_post"],
                                                      stage({n: bufs[n] for n in ("w_in", "w_out")}, dx))
    return dx, gr


def _tiles(s):
    return min(512, s), min(1024, s), min(2048, s), min(512, s // 2)


HBM_SPEC = pl.BlockSpec(memory_space=pltpu.HBM)
SEM_SPEC = pl.BlockSpec(memory_space=pltpu.SEMAPHORE)
EFFECT = pltpu.SideEffectType.DATAFLOW_SIDE_EFFECTING


def _place():
    x, y, c = lax.axis_index("x"), lax.axis_index("y"), lax.axis_index("c")
    return x, y, c, [(1 - x, y), (x, 1 - y), (1 - x, 1 - y)]


def _rcopy(src, dst, send_sems, recv_sems, k, to):
    return pltpu.make_async_remote_copy(src_ref=src, dst_ref=dst, send_sem=send_sems.at[k], recv_sem=recv_sems.at[k],
                                        device_id=to, device_id_type=MESH)


def _half(rows, which):
    return pl.ds(which * (rows // 2), rows // 2)


def _place_shard(w, l, p_idx, dtype):
    _, rows, cols = w.shape
    tr = _rows_per_block(rows, cols, 16) if rows % 16 == 0 else rows

    def body(p_ref, buf_ref, w_ref, o_ref):
        o_ref[...] = w_ref[...].astype(dtype)

    spec = pltpu.PrefetchScalarGridSpec(
        num_scalar_prefetch=1, grid=(rows // tr,),
        in_specs=[ANY, pl.BlockSpec((None, tr, cols), lambda i, pr: (l, i, 0))],
        out_specs=pl.BlockSpec((None, None, tr, cols), lambda i, pr: (0, pr[0], i, 0)))
    shape = (1, NSHARD, rows, cols)
    return pl.pallas_call(body, name="place_shard", grid_spec=spec, out_shape=SDS(shape, dtype),
                          input_output_aliases={1: 0}, compiler_params=_cp("parallel"),
                          )(p_idx, lax.empty(shape, dtype), w)


def _gather_two_level(bufs, n_halved):
    n = len(bufs)

    def body(*refs):
        outs = refs[n:2 * n]
        send_sems, recv_sems = refs[2 * n:]
        x, y, c, chips = _place()
        p = 2 * x + y
        me, sibling = (x, y, c), (x, y, 1 - c)

        def blk(a, q, half):
            return outs[a].at[0, q, _half(outs[a].shape[2], half)] if a < n_halved else outs[a].at[0, q]

        def cp(a, k, q, half, to):
            return _rcopy(blk(a, q, half), blk(a, q, half), send_sems, recv_sems, 6 * a + k, to)

        first = [cp(a, j, p, c, (*chip, c)) for a in range(n) for j, chip in enumerate(chips)]
        for d in first:
            d.start()
        passed = []
        for a in range(n):
            for j, chip in enumerate(chips):
                q = 2 * chip[0] + chip[1]
                cp(a, j, q, c, me).wait_recv()
                if a < n_halved:
                    passed.append(cp(a, 3 + j, q, c, sibling))
                    passed[-1].start()
        for a in range(n_halved):
            for j, chip in enumerate(chips):
                cp(a, 3 + j, 2 * chip[0] + chip[1], 1 - c, me).wait_recv()
        for d in first + passed:
            d.wait_send()

    return pl.pallas_call(
        body, name="gather_layer0", in_specs=[ANY] * n, out_specs=[ANY] * n,
        out_shape=[SDS(b.shape, b.dtype) for b in bufs], input_output_aliases={a: a for a in range(n)},
        scratch_shapes=[pltpu.SemaphoreType.DMA((6 * n,)), pltpu.SemaphoreType.DMA((6 * n,))],
    )(*bufs)


def _run_plans(plans, refs, send_sems, recv_sems):
    cps, b0, s0 = [], 0, 0
    for plan, nb, ns in plans:
        cps += plan(refs[b0:b0 + nb], send_sems, recv_sems, s0)
        b0, s0 = b0 + nb, s0 + ns
    return cps


def _exchange(name, bufs, plans):
    n = len(bufs)
    nsem = sum(ns for _, _, ns in plans)

    def body(*refs):
        cps = _run_plans(plans, refs[n:2 * n], refs[2 * n], refs[2 * n + 1])
        for cp in cps:
            cp.start()
        for cp in cps:
            cp.wait()

    return pl.pallas_call(
        body, name=name, in_specs=[ANY] * n, out_specs=[ANY] * n, out_shape=[SDS(b.shape, b.dtype) for b in bufs],
        input_output_aliases={a: a for a in range(n)},
        scratch_shapes=[pltpu.SemaphoreType.DMA((nsem,)), pltpu.SemaphoreType.DMA((nsem,))],
    )(*bufs)


def _exchange_start(name, bufs, plans, deps=()):
    n = len(bufs)
    nsem = sum(ns for _, _, ns in plans)
    deps = list(deps)
    first_out = n + len(deps)

    def body(*refs):
        for cp in _run_plans(plans, refs[:n], refs[first_out], refs[first_out + 1]):
            cp.start()
        token = refs[first_out + 2 + n]
        token[...] = jnp.zeros_like(token)

    outs = pl.pallas_call(
        body, name=name,
        out_shape=(pltpu.SemaphoreType.DMA((nsem,)), pltpu.SemaphoreType.DMA((nsem,)),
                   *[pltpu.HBM(b.shape, b.dtype) for b in bufs], SDS((8, 128), F32)),
        in_specs=[HBM_SPEC] * n + [ANY] * len(deps),
        out_specs=(SEM_SPEC, SEM_SPEC, *[HBM_SPEC] * n, pl.BlockSpec(memory_space=pltpu.VMEM)),
        input_output_aliases={a: 2 + a for a in range(n)},
        compiler_params=pltpu.CompilerParams(has_side_effects=EFFECT),
    )(*[pltpu.with_memory_space_constraint(b, pltpu.HBM) for b in bufs], *deps)
    return outs[0], outs[1], list(outs[2:2 + n]), outs[2 + n]


def _exchange_wait(name, send_sems, recv_sems, bufs, plans, after):
    n = len(bufs)

    def body(*refs):
        for cp in _run_plans(plans, refs[:n], refs[n], refs[n + 1]):
            cp.wait_send()
            cp.wait_recv()

    return pl.pallas_call(
        body, name=name, out_shape=[pltpu.HBM(b.shape, b.dtype) for b in bufs],
        in_specs=[HBM_SPEC] * n + [SEM_SPEC, SEM_SPEC, ANY], out_specs=[HBM_SPEC] * n,
        input_output_aliases={a: a for a in range(n)},
        compiler_params=pltpu.CompilerParams(has_side_effects=EFFECT),
    )(*bufs, send_sems, recv_sems, after)


def _plan_gather(refs, send_sems, recv_sems, base):
    x, y, c, chips = _place()
    p = 2 * x + y
    return [_rcopy(r.at[0, p], r.at[0, p], send_sems, recv_sems, base + 3 * a + j, (*chip, c))
            for a, r in enumerate(refs) for j, chip in enumerate(chips)]


def _plan_pair_exchange(refs, send_sems, recv_sems, base):
    x, y, c, _ = _place()
    n = len(refs) // 2
    return [_rcopy(refs[a].at[:, _half(refs[a].shape[1], 1 - c)], refs[n + a], send_sems, recv_sems, base + a,
                   (x, y, 1 - c)) for a in range(n)]


def _plan_chip_exchange(refs, send_sems, recv_sems, base):
    x, y, c, chips = _place()
    n = len(refs) // 2
    return [_rcopy(refs[a].at[2 * chip[0] + chip[1]], refs[n + a].at[j], send_sems, recv_sems, base + 3 * a + j,
                   (*chip, c)) for a in range(n) for j, chip in enumerate(chips)]


def _plan_pair_share(refs, send_sems, recv_sems, base):
    x, y, c, _ = _place()
    return [_rcopy(r.at[_half(r.shape[0], c)], r.at[_half(r.shape[0], c)], send_sems, recv_sems, base + a,
                   (x, y, 1 - c)) for a, r in enumerate(refs)]


def _plan_small_gather(refs, send_sems, recv_sems, base):
    x, y, c, _ = _place()
    me = 4 * x + 2 * y + c
    cps = []
    for m in range(1, NDEV):
        peer = (1 - x if m & 4 else x, 1 - y if m & 2 else y, 1 - c if m & 1 else c)
        cps.append(_rcopy(refs[0], refs[1].at[me], send_sems, recv_sems, base + m - 1, peer))
    return cps


def _sum_small(buf, gathered):
    def body(buf_ref, g_ref, o_ref):
        x, y, c, _ = _place()
        me = 4 * x + 2 * y + c
        total = jnp.where(me == 0, buf_ref[...], g_ref[0])
        for dev in range(1, NDEV):
            total = total + jnp.where(me == dev, buf_ref[...], g_ref[dev])
        o_ref[...] = total

    vm = pl.BlockSpec(memory_space=pltpu.VMEM)
    return pl.pallas_call(body, name="sum_small", in_specs=[vm, vm], out_specs=vm, out_shape=SDS(buf.shape, F32),
                          compiler_params=pltpu.CompilerParams(vmem_limit_bytes=VMEM_LIMIT))(buf, gathered)


BLOCK_ELEMS = 256 * 1024


def _rows_per_block(rows, cols, mult):
    best = None
    for tr in range(mult, rows + 1, mult):
        if rows % tr == 0 and tr * cols <= BLOCK_ELEMS:
            best = tr
    assert best is not None, (rows, cols)
    return best


def _pair_sum(g, r, c_idx):
    nq, rows, cols = g.shape
    half = rows // 2
    tr = _rows_per_block(half, cols, 16)
    nb = half // tr

    def body(c_ref, g_ref, r_ref, t_ref):
        t_ref[...] = (g_ref[...] + r_ref[...]).astype(BF16)

    blk = pl.BlockSpec((None, tr, cols), lambda q, i, cr: (q, i, 0))
    spec = pltpu.PrefetchScalarGridSpec(
        num_scalar_prefetch=1, grid=(nq, nb),
        in_specs=[pl.BlockSpec((None, tr, cols), lambda q, i, cr: (q, cr[0] * nb + i, 0)), blk], out_specs=blk)
    return pl.pallas_call(body, name="grad_pair_sum", grid_spec=spec, out_shape=SDS((nq, half, cols), BF16),
                          compiler_params=_cp("parallel", "parallel"))(c_idx, g, r)


def _chip_sum(g, r, rr, cp_idx):
    _, rows, cols = g.shape
    half = rows // 2
    tr = _rows_per_block(half, cols, 16)
    nb = half // tr

    def body(cp_ref, buf_ref, g_ref, r_ref, rr_ref, o_ref):
        o_ref[...] = ((g_ref[...] + r_ref[...]) + rr_ref[0].astype(F32) + rr_ref[1].astype(F32) + rr_ref[2].astype(F32))

    spec = pltpu.PrefetchScalarGridSpec(
        num_scalar_prefetch=1, grid=(nb,),
        in_specs=[ANY, pl.BlockSpec((None, tr, cols), lambda i, cp: (cp[1], cp[0] * nb + i, 0)),
                  pl.BlockSpec((None, tr, cols), lambda i, cp: (cp[1], i, 0)),
                  pl.BlockSpec((3, tr, cols), lambda i, cp: (0, i, 0))],
        out_specs=pl.BlockSpec((tr, cols), lambda i, cp: (cp[0] * nb + i, 0)))
    return pl.pallas_call(body, name="grad_chip_sum", grid_spec=spec, out_shape=SDS((rows, cols), F32),
                          input_output_aliases={1: 0}, compiler_params=_cp("parallel"),
                          )(cp_idx, lax.empty((rows, cols), F32), g, r, rr)


def _adamw_math(w, g, m, v):
    mn = ADAM_B1 * m + (1.0 - ADAM_B1) * g
    vn = ADAM_B2 * v + (1.0 - ADAM_B2) * (g * g)
    m_hat = mn / (1.0 - ADAM_B1 ** ADAM_STEP)
    v_hat = vn / (1.0 - ADAM_B2 ** ADAM_STEP)
    return -ADAM_LR * (m_hat / (jnp.sqrt(v_hat) + ADAM_EPS) + ADAM_WD * w), mn, vn


def _adamw_layer(w, g, m, v, l, outs, deps=()):
    _, rows, cols = w.shape
    tr = _rows_per_block(rows, cols, 8)
    deps = list(deps)

    def body(*refs):
        w_ref, g_ref, m_ref, v_ref = refs[4:8]
        go_ref, d_ref, mo_ref, vo_ref = refs[8 + len(deps):]
        gg = g_ref[...]
        go_ref[...] = gg
        d_ref[...], mo_ref[...], vo_ref[...] = _adamw_math(w_ref[...], gg, m_ref[...], v_ref[...])

    blk = pl.BlockSpec((None, tr, cols), lambda i: (l, i, 0))
    return pl.pallas_call(
        body, name="adamw_layer", grid=(rows // tr,),
        in_specs=[ANY] * 4 + [blk, pl.BlockSpec((tr, cols), lambda i: (i, 0)), blk, blk] + [ANY] * len(deps),
        out_specs=[blk] * 4, out_shape=[SDS(w.shape, F32)] * 4, input_output_aliases={k: k for k in range(4)},
        compiler_params=_cp("parallel"))(*outs, w, g, m, v, *deps)


def _adamw_small(ws, gs, ms, vs, deps=()):
    n = len(ws)
    deps = list(deps)

    def body(*refs):
        refs = refs[:4 * n] + refs[4 * n + len(deps):]
        w, g, m, v, d_out, m_out, v_out = (refs[k * n:(k + 1) * n] for k in range(7))
        for k in range(n):
            d_out[k][...], m_out[k][...], v_out[k][...] = _adamw_math(w[k][...], g[k][...], m[k][...], v[k][...])

    vm = pl.BlockSpec(memory_space=pltpu.VMEM)
    outs = pl.pallas_call(body, name="adamw_small", in_specs=[vm] * (4 * n) + [ANY] * len(deps), out_specs=[vm] * (3 * n),
                          out_shape=[SDS(w.shape, F32) for w in ws] * 3,
                          compiler_params=pltpu.CompilerParams(vmem_limit_bytes=VMEM_LIMIT))(*ws, *gs, *ms, *vs, *deps)
    return outs[:n], outs[n:2 * n], outs[2 * n:]


_WEIGHTS = ["ffn1_pre_g", "ffn1_w_gu", "ffn1_w_down", "ffn1_post_g", "mix_pre_g", "w_in", "lru_conv_w", "lru_conv_b",
            "lru_w_a", "lru_b_a", "lru_w_x", "lru_b_x", "lru_lambda", "attn_sinks", "conv_w", "conv_b", "conv_ln_g",
            "conv_ln_b", "group_g", "w_out", "mix_post_g", "ffn2_pre_g", "ffn2_w_gu", "ffn2_w_down", "ffn2_post_g"]
_INPUTS = ["x"] + _WEIGHTS + ["loss_target"] + ["m_" + n for n in _WEIGHTS] + ["v_" + n for n in _WEIGHTS]
_BIG = ["ffn1_w_gu", "ffn1_w_down", "w_in", "w_out", "ffn2_w_gu", "ffn2_w_down"]
_SMALL_SHARDED = ["lru_conv_w", "conv_w"]
_SMALL_REPL = [n for n in _WEIGHTS if n not in _BIG and n not in _SMALL_SHARDED]

PACK_TILE = 8 * 128


def _pack(arrs):
    parts = []
    for a in arrs:
        flat = a.reshape(-1)
        parts.append(jnp.pad(flat, (0, -flat.shape[0] % PACK_TILE)).reshape(-1, 128))
    return jnp.concatenate(parts, axis=0)


def _unpack(buf, shapes):
    out, row = [], 0
    for shp in shapes:
        size = math.prod(shp)
        nrow = -(-size // PACK_TILE) * 8
        out.append(buf[row:row + nrow].reshape(-1)[:size].reshape(shp))
        row += nrow
    return out


def _unshard_cols(a):
    return a.transpose(0, 2, 1, 3).reshape(1, a.shape[2], NSHARD * a.shape[3])


_GROUPS = dict(ffn1_gu=["ffn1_w_gu"], ffn1_down=["ffn1_w_down"], mix=["w_in", "w_out", "lru_conv_w", "conv_w"],
               ffn2=["ffn2_w_gu", "ffn2_w_down"])


def _full_weights(group, gathered):
    g = dict(zip(_GROUPS[group], gathered))
    if group == "mix":
        return dict(w_in=_unshard_cols(g["w_in"]), w_out=g["w_out"].reshape(1, D, D),
                    lru_conv_w=_unshard_cols(g["lru_conv_w"])[0], conv_w=_unshard_cols(g["conv_w"])[0])
    return {n: (a.reshape(1, DFF, D) if n.endswith("w_down") else a) for n, a in g.items()}


def _by_shard(name, buf):
    if name.endswith("w_gu"):
        return buf[0]
    if name == "w_in":
        return buf.reshape(D, NSHARD, P_IN // NSHARD).transpose(1, 0, 2)
    return buf.reshape(NSHARD, buf.shape[2] // NSHARD, buf.shape[3])


class _Reducer:
    PLANS = (_plan_pair_exchange, _plan_chip_exchange, _plan_pair_share)

    def __init__(self, keys, gs, c_idx, cp_idx):
        self.keys, self.gs, self.c_idx, self.cp_idx = keys, gs, c_idx, cp_idx
        self.n = len(gs)
        self.step = 0
        self.result = None

    def inputs(self):
        n = self.n
        if self.step == 0:
            bufs = self.gs + [lax.empty((NSHARD, g.shape[1] // 2, g.shape[2]), F32) for g in self.gs]
        elif self.step == 1:
            ts = [_pair_sum(g, r, self.c_idx) for g, r in zip(self.gs, self.rs)]
            bufs = ts + [lax.empty((3,) + t.shape[1:], BF16) for t in ts]
        else:
            bufs = [_chip_sum(g, r, rr, self.cp_idx) for g, r, rr in zip(self.gs, self.rs, self.rrs)]
        return bufs, (self.PLANS[self.step], len(bufs), (n, 3 * n, n)[self.step])

    def absorb(self, done):
        n = self.n
        if self.step == 0:
            self.gs, self.rs = done[:n], done[n:]
        elif self.step == 1:
            self.rrs = done[n:]
        else:
            self.result = dict(zip(self.keys, done))
        self.step += 1


class _SmallGather:
    def __init__(self, buf):
        self.buf, self.step, self.result, self.gathered = buf, 0, {}, None

    def inputs(self):
        return [self.buf, jnp.zeros((NDEV,) + self.buf.shape, F32)], (_plan_small_gather, 2, NDEV - 1)

    def absorb(self, done):
        self.buf, self.gathered = done
        self.step = 3


class _ReducePipeline:
    def __init__(self, c_idx, cp_idx):
        self.c_idx, self.cp_idx = c_idx, cp_idx
        self.reducers, self.flying, self.calls = [], None, 0

    def add(self, layer, done):
        if done:
            keys = [(layer, n) for n in done]
            self.reducers.append(_Reducer(keys, [_by_shard(n, b) for n, b in done.items()], self.c_idx, self.cp_idx))

    def _next(self):
        active = [r for r in self.reducers if r.step < 3]
        bufs, plans = [], []
        for r in active:
            b, triple = r.inputs()
            bufs += b
            plans.append(triple)
        self.calls += 1
        return active, bufs, plans, "grad_exchange%d" % self.calls

    def _absorb(self, active, plans, done):
        at = 0
        for r, (_, nb, _) in zip(active, plans):
            r.absorb(done[at:at + nb])
            at += nb

    def _land(self, after):
        if self.flying is not None:
            active, plans, name, send_sems, recv_sems, bufs = self.flying
            self._absorb(active, plans, _exchange_wait(name + "_wait", send_sems, recv_sems, bufs, plans, after))
            self.flying = None

    def hook(self, after):
        self._land(after)
        active, bufs, plans, name = self._next()
        if not active:
            return []
        send_sems, recv_sems, bufs, token = _exchange_start(name + "_start", bufs, plans)
        self.flying = (active, plans, name, send_sems, recv_sems, bufs)
        return [token]

    def available(self):
        out = {}
        for r in self.reducers:
            if r.step == 3:
                out.update(r.result)
        return out

    def finish(self, after):
        self._land(after)
        while True:
            active, bufs, plans, name = self._next()
            if not active:
                break
            self._absorb(active, plans, _exchange(name, bufs, plans))
        out = {}
        for r in self.reducers:
            out.update(r.result)
        return out


def kernel(*args):
    d = dict(zip(_INPUTS, args, strict=True))
    xi, yi, ci = lax.axis_index("x"), lax.axis_index("y"), lax.axis_index("c")
    p = 2 * xi + yi
    c_idx = jnp.reshape(ci, (1,)).astype(jnp.int32)
    p_idx = jnp.reshape(p, (1,)).astype(jnp.int32)
    cp_idx = jnp.stack([ci, p]).astype(jnp.int32)
    x, target = d["x"][0], d["loss_target"][0]
    tiles = _tiles(x.shape[0])

    groups = [(l, grp) for l in range(DEPTH) for grp in _GROUPS]
    placed = {(l, grp): [_place_shard(d[n], l, p_idx, BF16 if n in _BIG else F32) for n in _GROUPS[grp]]
              for l, grp in groups}
    ready = {groups[0]: _gather_two_level(placed[groups[0]], len(placed[groups[0]]))}
    flying, tokens = {}, [ready[groups[0]][0]]
    for l, grp in groups[1:]:
        plans = [(_plan_gather, len(placed[l, grp]), 3 * len(placed[l, grp]))]
        send_sems, recv_sems, bufs, token = _exchange_start("gather_l%d_%s_start" % (l, grp), placed[l, grp], plans,
                                                             tokens[-1:])
        flying[l, grp] = (send_sems, recv_sems, bufs, plans)
        tokens.append(token)

    def weights_of(l):
        def weights(grp, after):
            if (l, grp) not in ready:
                send_sems, recv_sems, bufs, plans = flying[l, grp]
                ready[l, grp] = _exchange_wait("gather_l%d_%s_wait" % (l, grp), send_sems, recv_sems, bufs, plans, after)
            return _full_weights(grp, ready[l, grp])
        return weights

    small = {n: d[n] for n in _SMALL_REPL}
    x1, sv0 = _forward_layer(x, weights_of(0), _layer_params(small, 0), tiles, tokens[1:])
    x2, sv1 = _forward_layer(x1, weights_of(1), _layer_params(small, 1), tiles)
    dx, lcols = _loss_grad(x2, target, tiles[0])

    pipe = _ReducePipeline(c_idx, cp_idx)
    sgrads = [None] * DEPTH
    for l, sv in ((1, sv1), (0, sv0)):
        bufs = _grad_buffers()

        def stage(done, dx, l=l):
            pipe.add(l, done)
            return pipe.hook(dx)

        dx, sgrads[l] = _backward_layer(dx, sv, bufs, tiles, stage)
    grad_x = dx

    stacked = {n: jnp.stack([sgrads[l][n].reshape(d[n].shape[1:]) for l in range(DEPTH)]) for n in _SMALL_REPL}
    for n in _SMALL_SHARDED:
        stacked[n] = jnp.stack([sgrads[l][n] for l in range(DEPTH)])
    loss_part = jnp.pad((0.5 / D) * jnp.sum(lcols).reshape(1), (0, 127))
    order = _SMALL_REPL + _SMALL_SHARDED
    small_gather = _SmallGather(_pack([loss_part] + [stacked[n] for n in order]))
    pipe.reducers.append(small_gather)

    results = {n: tuple(lax.empty(d[n].shape, F32) for _ in range(4)) for n in _BIG}
    applied = set()

    def apply_ready(deps, last):
        for (l, n), g in pipe.available().items():
            if (l, n) not in applied:
                results[n] = _adamw_layer(d[n], g, d["m_" + n], d["v_" + n], l, results[n], deps)
                applied.add((l, n))
                last = results[n][1]
                deps = [last]
        return last

    last = apply_ready(pipe.hook(grad_x), grad_x)
    token = pipe.hook(last)
    summed = _unpack(_sum_small(small_gather.buf, small_gather.gathered), [(128,)] + [stacked[n].shape for n in order])
    loss = summed[0][0]
    grads = {}
    for n, g in zip(order, summed[1:]):
        if n in _SMALL_SHARDED:
            g = lax.dynamic_slice_in_dim(g, p * (g.shape[2] // NSHARD), g.shape[2] // NSHARD, axis=2)
        grads[n] = g
    delta, new_m, new_v = {}, {}, {}
    small_out = _adamw_small([d[n] for n in order], [grads[n] for n in order], [d["m_" + n] for n in order],
                             [d["v_" + n] for n in order], token)
    for out, res in zip((delta, new_m, new_v), small_out):
        out.update(zip(order, res))
    last = apply_ready([small_out[0][0]], small_out[0][0])
    pipe.finish(last)
    apply_ready((), last)
    for n in _BIG:
        grads[n], delta[n], new_m[n], new_v[n] = results[n]

    return (loss, grad_x[None], *[grads[n] for n in _WEIGHTS], *[delta[n] for n in _WEIGHTS],
            *[new_m[n] for n in _WEIGHTS], *[new_v[n] for n in _WEIGHTS])
```

```python
import functools
import math

import jax
import jax.numpy as jnp
from jax import lax
from jax.experimental import pallas as pl
from jax.experimental.pallas import tpu as pltpu

F32 = jnp.float32
BF16 = jnp.bfloat16
SDS = jax.ShapeDtypeStruct

D = 1024
DFF = 2816
FH = DFF // 2
DEPTH = 2
W_A = 256
W_B = 512
W_C = 256
NQ = 8
HD = 64
BLK = 128
ATT_NB_FWD = 1
ATT_NB_BWD = 4
P_IN = 1792
LRU_K = 4
CONV_K = 31
LRU_C = 8.0
NORM_EPS = 1e-6
LN_EPS = 1e-5
NEG_BIG = -1e30
SCALE = 1.0 / math.sqrt(HD)

ADAM_LR = 0.001
ADAM_B1 = 0.9
ADAM_B2 = 0.999
ADAM_EPS = 1e-08
ADAM_WD = 0.01
ADAM_STEP = 10

VMEM_LIMIT = 60 * 1024 * 1024
NSHARD = 4
NDEV = 8

TN = (((0,), (0,)), ((), ()))
NT = (((1,), (1,)), ((), ()))

MESH = pl.DeviceIdType.MESH
ANY = pl.BlockSpec(memory_space=pl.ANY)


def _cp(*sem):
    return pltpu.CompilerParams(dimension_semantics=sem if sem else None, vmem_limit_bytes=VMEM_LIMIT)


def _rsq(x, eps):
    return lax.rsqrt(jnp.mean(x * x, axis=-1, keepdims=True) + eps)


def _rms_bwd_rows(x, g, dy):
    r = _rsq(x, NORM_EPS)
    xh = x * r
    dyg = dy * g
    dx = r * (dyg - xh * jnp.mean(dyg * xh, axis=-1, keepdims=True))
    return dx, dy * xh


def _sig(x):
    return jax.nn.sigmoid(x)


def _ffn_up(x, pre_g, wgu, l, tm, deps=()):
    s = x.shape[0]
    deps = list(deps)

    def body(x_ref, g_ref, wg_ref, wu_ref, *rest):
        h_ref, go_ref, uo_ref, a_ref = rest[len(deps):]

        @pl.when(pl.program_id(1) == 0)
        def _():
            xf = x_ref[...]
            h_ref[...] = (xf * _rsq(xf, NORM_EPS) * g_ref[...]).astype(BF16)

        h = h_ref[...]
        gg = jnp.dot(h, wg_ref[...], preferred_element_type=F32)
        uu = jnp.dot(h, wu_ref[...], preferred_element_type=F32)
        sg = _sig(gg)
        silu = gg * sg
        go_ref[...] = (uu * (sg * (1.0 + gg * (1.0 - sg)))).astype(BF16)
        uo_ref[...] = silu.astype(BF16)
        a_ref[...] = (silu * uu).astype(BF16)

    wide = pl.BlockSpec((tm, FH), lambda i, j: (i, j))
    return pl.pallas_call(
        body, name="ffn_up", grid=(s // tm, 2),
        in_specs=[pl.BlockSpec((tm, D), lambda i, j: (i, 0)), pl.BlockSpec((1, D), lambda i, j: (0, 0)),
                  pl.BlockSpec((None, None, D, FH), lambda i, j: (l, j, 0, 0)),
                  pl.BlockSpec((None, None, D, FH), lambda i, j: (l, j + 2, 0, 0))] + [ANY] * len(deps),
        out_specs=[pl.BlockSpec((tm, D), lambda i, j: (i, 0)), wide, wide, wide],
        out_shape=[SDS((s, D), BF16), SDS((s, DFF), BF16), SDS((s, DFF), BF16), SDS((s, DFF), BF16)],
        compiler_params=_cp("parallel", "arbitrary"),
    )(x, pre_g, wgu, wgu, *deps)


def _mm_rms_res(a, w, l, x, g, c, tm, tk, name):
    s, k_dim = a.shape
    nk = k_dim // tk

    def body(a_ref, w_ref, x_ref, g_ref, z_ref, x1_ref):
        k = pl.program_id(1)
        p = jnp.dot(a_ref[...], w_ref[...], preferred_element_type=F32)

        @pl.when(k == 0)
        def _():
            z_ref[...] = p

        @pl.when(k > 0)
        def _():
            z_ref[...] += p

        @pl.when(k == nk - 1)
        def _():
            z = z_ref[...]
            x1_ref[...] = x_ref[...] + c * (z * _rsq(z, NORM_EPS) * g_ref[...])

    row = pl.BlockSpec((tm, D), lambda i, k: (i, 0))
    return pl.pallas_call(
        body, name=name, grid=(s // tm, nk),
        in_specs=[pl.BlockSpec((tm, tk), lambda i, k: (i, k)), pl.BlockSpec((None, tk, D), lambda i, k: (l, k, 0)),
                  row, pl.BlockSpec((1, D), lambda i, k: (0, 0))],
        out_specs=[row, row],
        out_shape=[SDS((s, D), F32), SDS((s, D), F32)],
        compiler_params=_cp("parallel", "arbitrary"),
    )(a, w, x, g)


def _rms_bwd(dy, z, g, c, tm, name, deps=()):
    s = z.shape[0]
    deps = list(deps)

    def body(dy_ref, z_ref, g_ref, *rest):
        dz_ref, dg_ref = rest[len(deps):]
        dz, dgr = _rms_bwd_rows(z_ref[...], g_ref[...], c * dy_ref[...])
        dz_ref[...] = dz.astype(BF16)
        part = jnp.sum(dgr, axis=0, keepdims=True)

        @pl.when(pl.program_id(0) == 0)
        def _():
            dg_ref[...] = part

        @pl.when(pl.program_id(0) > 0)
        def _():
            dg_ref[...] += part

    row = pl.BlockSpec((tm, D), lambda i: (i, 0))
    vec = pl.BlockSpec((1, D), lambda i: (0, 0))
    return pl.pallas_call(
        body, name=name, grid=(s // tm,), in_specs=[row, row, vec] + [ANY] * len(deps), out_specs=[row, vec],
        out_shape=[SDS((s, D), BF16), SDS((1, D), F32)], compiler_params=_cp("arbitrary"),
    )(dy, z, g, *deps)


def _ffn_bwd_mid(dz, wd, l, dadg, dadu, tm):
    s = dz.shape[0]

    def body(dz_ref, wd_ref, g_ref, u_ref, dg_ref, du_ref):
        da = lax.dot_general(dz_ref[...], wd_ref[...], NT, preferred_element_type=F32)
        dg_ref[...] = (da * g_ref[...].astype(F32)).astype(BF16)
        du_ref[...] = (da * u_ref[...].astype(F32)).astype(BF16)

    wide = pl.BlockSpec((tm, FH), lambda i, j: (i, j))
    return pl.pallas_call(
        body, name="ffn_bwd_mid", grid=(s // tm, 2),
        in_specs=[pl.BlockSpec((tm, D), lambda i, j: (i, 0)), pl.BlockSpec((None, FH, D), lambda i, j: (l, j, 0)), wide, wide],
        out_specs=[wide, wide],
        out_shape=[SDS((s, DFF), BF16), SDS((s, DFF), BF16)],
        compiler_params=_cp("parallel", "arbitrary"),
    )(dz, wd, dadg, dadu)


def _ffn_bwd_dh(dg, du, wgu, l, x, pre_g, dx1, tm, deps=()):
    s = x.shape[0]
    deps = list(deps)

    def body(dg_ref, du_ref, wg_ref, wu_ref, x_ref, g_ref, dx1_ref, *rest):
        dx_ref, dgp_ref = rest[len(deps):]
        i, k = pl.program_id(0), pl.program_id(1)
        p = (lax.dot_general(dg_ref[...], wg_ref[...], NT, preferred_element_type=F32)
             + lax.dot_general(du_ref[...], wu_ref[...], NT, preferred_element_type=F32))

        @pl.when(k == 0)
        def _():
            dx_ref[...] = p

        @pl.when(k == 1)
        def _():
            dx, dgr = _rms_bwd_rows(x_ref[...], g_ref[...], dx_ref[...] + p)
            dx_ref[...] = dx1_ref[...] + dx
            part = jnp.sum(dgr, axis=0, keepdims=True)

            @pl.when(i == 0)
            def _():
                dgp_ref[...] = part

            @pl.when(i > 0)
            def _():
                dgp_ref[...] += part

    wide = pl.BlockSpec((tm, FH), lambda i, k: (i, k))
    row = pl.BlockSpec((tm, D), lambda i, k: (i, 0))
    vec = pl.BlockSpec((1, D), lambda i, k: (0, 0))
    return pl.pallas_call(
        body, name="ffn_bwd_dh", grid=(s // tm, 2),
        in_specs=[wide, wide, pl.BlockSpec((None, None, D, FH), lambda i, k: (l, k, 0, 0)),
                  pl.BlockSpec((None, None, D, FH), lambda i, k: (l, k + 2, 0, 0)), row, vec, row] + [ANY] * len(deps),
        out_specs=[row, vec],
        out_shape=[SDS((s, D), F32), SDS((1, D), F32)],
        compiler_params=_cp("arbitrary", "arbitrary"),
    )(dg, du, wgu, wgu, x, pre_g, dx1, *deps)


def _mm_tn_into(buf, a, b, l, joff, tka, tn, ts, name):
    s, ka = a.shape
    n = b.shape[1]

    def body(buf_ref, a_ref, b_ref, o_ref):
        p = lax.dot_general(a_ref[...], b_ref[...], TN, preferred_element_type=F32)

        @pl.when(pl.program_id(2) == 0)
        def _():
            o_ref[...] = p

        @pl.when(pl.program_id(2) > 0)
        def _():
            o_ref[...] += p

    return pl.pallas_call(
        body, name=name, grid=(ka // tka, n // tn, s // ts),
        in_specs=[pl.BlockSpec(memory_space=pl.ANY),
                  pl.BlockSpec((ts, tka), lambda ia, j, t: (t, ia)), pl.BlockSpec((ts, tn), lambda ia, j, t: (t, j))],
        out_specs=pl.BlockSpec((None, None, tka, tn), lambda ia, j, t: (l, joff + j, ia, 0)),
        out_shape=SDS(buf.shape, F32), input_output_aliases={0: 0},
        compiler_params=_cp("parallel", "parallel", "arbitrary"),
    )(buf, a, b)


def _proj(x, g, w_in, l, tm):
    s = x.shape[0]

    def body(x_ref, g_ref, w_ref, h_ref, p_ref):
        xf = x_ref[...]
        h = (xf * _rsq(xf, NORM_EPS) * g_ref[...]).astype(BF16)
        h_ref[...] = h
        p_ref[...] = jnp.dot(h, w_ref[...], preferred_element_type=F32)

    return pl.pallas_call(
        body, name="proj", grid=(s // tm,),
        in_specs=[pl.BlockSpec((tm, D), lambda i: (i, 0)), pl.BlockSpec((1, D), lambda i: (0, 0)),
                  pl.BlockSpec((None, D, P_IN), lambda i: (l, 0, 0))],
        out_specs=[pl.BlockSpec((tm, D), lambda i: (i, 0)), pl.BlockSpec((tm, P_IN), lambda i: (i, 0))],
        out_shape=[SDS((s, D), BF16), SDS((s, P_IN), F32)],
        compiler_params=_cp("parallel"),
    )(x, g, w_in)


def _mm_nt(a, w, l, tm, name):
    s, k_dim = a.shape
    n = w.shape[1]

    def body(a_ref, w_ref, o_ref):
        o_ref[...] = lax.dot_general(a_ref[...], w_ref[...], NT, preferred_element_type=F32)

    return pl.pallas_call(
        body, name=name, grid=(s // tm,),
        in_specs=[pl.BlockSpec((tm, k_dim), lambda i: (i, 0)), pl.BlockSpec((None, n, k_dim), lambda i: (l, 0, 0))],
        out_specs=pl.BlockSpec((tm, n), lambda i: (i, 0)),
        out_shape=SDS((s, n), F32), compiler_params=_cp("parallel"),
    )(a, w)


def _mm_nt_rmsbwd(dp, w_in, l, x, g, dx1, tm):
    s = x.shape[0]

    def body(dp_ref, w_ref, x_ref, g_ref, dx1_ref, dx_ref, dg_ref):
        dh = lax.dot_general(dp_ref[...], w_ref[...], NT, preferred_element_type=F32)
        dx, dgr = _rms_bwd_rows(x_ref[...], g_ref[...], dh)
        dx_ref[...] = dx1_ref[...] + dx
        part = jnp.sum(dgr, axis=0, keepdims=True)

        @pl.when(pl.program_id(0) == 0)
        def _():
            dg_ref[...] = part

        @pl.when(pl.program_id(0) > 0)
        def _():
            dg_ref[...] += part

    row = pl.BlockSpec((tm, D), lambda i: (i, 0))
    vec = pl.BlockSpec((1, D), lambda i: (0, 0))
    return pl.pallas_call(
        body, name="mix_bwd_dx", grid=(s // tm,),
        in_specs=[pl.BlockSpec((tm, P_IN), lambda i: (i, 0)), pl.BlockSpec((None, D, P_IN), lambda i: (l, 0, 0)), row, vec, row],
        out_specs=[row, vec], out_shape=[SDS((s, D), F32), SDS((1, D), F32)],
        compiler_params=_cp("arbitrary"),
    )(dp, w_in, x, g, dx1)


def _row_iota(shape):
    return lax.broadcasted_iota(jnp.int32, shape, 0)


def _lru_gates(xc, wa_ref, ba_ref, wx_ref, bx_ref, lam_ref):
    xb = xc.astype(BF16)
    r = _sig(jnp.dot(xb, wa_ref[...], preferred_element_type=F32) + ba_ref[...])
    ig = _sig(jnp.dot(xb, wx_ref[...], preferred_element_type=F32) + bx_ref[...])
    nl = -lam_ref[...]
    sp = jnp.maximum(nl, 0.0) + jnp.log(1.0 + jnp.exp(-jnp.abs(nl)))
    log_a = -LRU_C * r * sp
    a = jnp.exp(log_a)
    x2 = 2.0 * log_a
    series = x2 * (1.0 + x2 * (0.5 + x2 * (1.0 / 6.0 + x2 * (1.0 / 24.0 + x2 * (1.0 / 120.0)))))
    em1 = jnp.where(x2 > -0.05, series, jnp.exp(x2) - 1.0)
    mlt = jnp.sqrt(-em1)
    return r, ig, a, mlt, sp


def _conv_taps(src_ref, w_ref, k_taps, pad, tc):
    acc = None
    for j in range(k_taps):
        term = w_ref[j:j + 1, :] * src_ref[pl.ds(pad - (k_taps - 1) + j, tc), :]
        acc = term if acc is None else acc + term
    return acc


def _gelu_parts(x):
    c0 = math.sqrt(2.0 / math.pi)
    inner = c0 * (x + 0.044715 * x * x * x)
    t = jnp.tanh(inner)
    gl = 0.5 * x * (1.0 + t)
    dgl = 0.5 * (1.0 + t) + 0.5 * x * (1.0 - t * t) * c0 * (1.0 + 3.0 * 0.044715 * x * x)
    return gl, dgl


def _lru_fwd(proj, cw, cb, wa, ba, wx, bx, lam, gg, tc):
    s = proj.shape[0]
    pad = 8

    def body(xcur_ref, xprev_ref, gate_ref, cw_ref, cb_ref, wa_ref, ba_ref, wx_ref, bx_ref, lam_ref, gg_ref,
             yn_ref, h_ref, xs_ref, hc_ref):
        i = pl.program_id(0)

        @pl.when(i == 0)
        def _():
            hc_ref[...] = jnp.zeros_like(hc_ref)

        xs_ref[0:pad, :] = jnp.where(i > 0, xprev_ref[tc - pad:tc, :], 0.0)
        xs_ref[pad:pad + tc, :] = xcur_ref[...]
        xc = _conv_taps(xs_ref, cw_ref, LRU_K, pad, tc) + cb_ref[...]
        _, ig, a, mlt, _ = _lru_gates(xc, wa_ref, ba_ref, wx_ref, bx_ref, lam_ref)
        u = mlt * (ig * xc)
        row = _row_iota((tc, W_A))
        d = 1
        while d < tc:
            ok = row >= d
            a_sh = jnp.where(ok, pltpu.roll(a, d, axis=0), 1.0)
            u_sh = jnp.where(ok, pltpu.roll(u, d, axis=0), 0.0)
            u = a * u_sh + u
            a = a * a_sh
            d *= 2
        h = u + a * hc_ref[...]
        hc_ref[...] = jnp.sum(jnp.where(row == tc - 1, h, 0.0), axis=0, keepdims=True)
        h_ref[...] = h
        gl, _ = _gelu_parts(gate_ref[...])
        ya = gl * h
        yn_ref[...] = (ya * _rsq(ya, NORM_EPS) * gg_ref[...]).astype(BF16)

    blk = lambda c: pl.BlockSpec((tc, W_A), lambda i, c=c: (i, c))
    full = lambda a: pl.BlockSpec(a.shape, lambda i: (0,) * a.ndim)
    params = [cw, cb, wa, ba, wx, bx, lam, gg]
    return pl.pallas_call(
        body, name="lru_fwd", grid=(s // tc,),
        in_specs=[blk(0), pl.BlockSpec((tc, W_A), lambda i: (jnp.maximum(i - 1, 0), 0)), blk(1)] + [full(a) for a in params],
        out_specs=[pl.BlockSpec((tc, W_A), lambda i: (i, 0))] * 2,
        out_shape=[SDS((s, W_A), BF16), SDS((s, W_A), F32)],
        scratch_shapes=[pltpu.VMEM((tc + pad, W_A), F32), pltpu.VMEM((1, W_A), F32)],
        compiler_params=_cp("arbitrary"),
    )(proj, proj, proj, *params)


def _acc(ref, first, val):
    @pl.when(first)
    def _():
        ref[...] = val

    @pl.when(jnp.logical_not(first))
    def _():
        ref[...] += val


def _lru_bwd(dy, proj, h, cw, cb, wa, ba, wx, bx, lam, gg, tc):
    s = proj.shape[0]
    nc = s // tc
    pad = 8

    def body(dy_ref, xcur_ref, xprev_ref, gate_ref, h_ref, hprev_ref, cw_ref, cb_ref, wa_ref, ba_ref, wx_ref, bx_ref,
             lam_ref, gg_ref,
             dp_ref, dcw_ref, dcb_ref, dwa_ref, dba_ref, dwx_ref, dbx_ref, dlam_ref, dgg_ref,
             xs_ref, ds_ref, mu_ref, nx_ref):
        step = pl.program_id(0)
        i = nc - 1 - step
        first = step == 0

        @pl.when(first)
        def _():
            mu_ref[...] = jnp.zeros_like(mu_ref)
            nx_ref[...] = jnp.zeros_like(nx_ref)

        xs_ref[0:pad, :] = jnp.where(i > 0, xprev_ref[tc - pad:tc, :], 0.0)
        xs_ref[pad:pad + tc, :] = xcur_ref[...]
        xc = _conv_taps(xs_ref, cw_ref, LRU_K, pad, tc) + cb_ref[...]
        r, ig, a, mlt, sp = _lru_gates(xc, wa_ref, ba_ref, wx_ref, bx_ref, lam_ref)
        hh = h_ref[...]
        gate = gate_ref[...]
        gl, dgl = _gelu_parts(gate)
        ya = gl * hh
        dya, dggr = _rms_bwd_rows(ya, gg_ref[...], dy_ref[...])
        _acc(dgg_ref, first, jnp.sum(dggr, axis=0, keepdims=True))
        dp_ref[:, W_A:2 * W_A] = dya * hh * dgl
        dh = dya * gl

        row = _row_iota((tc, W_A))
        aa = a
        uu = a * dh
        d = 1
        while d < tc:
            ok = row < tc - d
            a_sh = jnp.where(ok, pltpu.roll(aa, tc - d, axis=0), 1.0)
            u_sh = jnp.where(ok, pltpu.roll(uu, tc - d, axis=0), 0.0)
            uu = uu + aa * u_sh
            aa = aa * a_sh
            d *= 2
        cin = mu_ref[...]
        mu = uu + aa * cin
        lam_t = dh + jnp.where(row == tc - 1, cin, pltpu.roll(mu, tc - 1, axis=0))
        mu_ref[...] = jnp.sum(jnp.where(row == 0, mu, 0.0), axis=0, keepdims=True)
        hm1 = jnp.where(row == 0, jnp.where(i > 0, pltpu.roll(hprev_ref[...], 1, axis=0), 0.0),
                        pltpu.roll(hh, 1, axis=0))
        da = lam_t * hm1
        du = lam_t
        dmlt = du * ig * xc
        dig = du * mlt * xc
        dxc = du * mlt * ig
        dlog_a = da * a - dmlt * (a * a / mlt)
        dr = dlog_a * (-LRU_C * sp)
        dsp = jnp.sum(dlog_a * (-LRU_C * r), axis=0, keepdims=True)
        _acc(dlam_ref, first, dsp * (-_sig(-lam_ref[...])))
        dga = dr * r * (1.0 - r)
        dgx = dig * ig * (1.0 - ig)
        _acc(dba_ref, first, jnp.sum(dga, axis=0, keepdims=True))
        _acc(dbx_ref, first, jnp.sum(dgx, axis=0, keepdims=True))
        xb = xc.astype(BF16)
        dgab = dga.astype(BF16)
        dgxb = dgx.astype(BF16)
        _acc(dwa_ref, first, lax.dot_general(xb, dgab, TN, preferred_element_type=F32))
        _acc(dwx_ref, first, lax.dot_general(xb, dgxb, TN, preferred_element_type=F32))
        dxc = (dxc + lax.dot_general(dgab, wa_ref[...], NT, preferred_element_type=F32)
               + lax.dot_general(dgxb, wx_ref[...], NT, preferred_element_type=F32))

        _acc(dcb_ref, first, jnp.sum(dxc, axis=0, keepdims=True))
        r8 = _row_iota((8, W_A))
        dcw = jnp.zeros((8, W_A), F32)
        for j in range(LRU_K):
            tap = jnp.sum(dxc * xs_ref[pl.ds(pad - (LRU_K - 1) + j, tc), :], axis=0, keepdims=True)
            dcw = dcw + jnp.where(r8 == j, tap, 0.0)
        _acc(dcw_ref, first, dcw)
        ds_ref[0:tc, :] = dxc
        ds_ref[tc:tc + pad, :] = nx_ref[...]
        dlx = None
        for j in range(LRU_K):
            term = cw_ref[j:j + 1, :] * ds_ref[pl.ds(LRU_K - 1 - j, tc), :]
            dlx = term if dlx is None else dlx + term
        dp_ref[:, 0:W_A] = dlx
        nx_ref[...] = dxc[0:pad, :]

    rev = lambda c: pl.BlockSpec((tc, W_A), lambda t, c=c: (nc - 1 - t, c))
    prev = lambda c: pl.BlockSpec((tc, W_A), lambda t, c=c: (jnp.maximum(nc - 2 - t, 0), c))
    full = lambda a: pl.BlockSpec(a.shape, lambda t: (0,) * a.ndim)
    params = [cw, cb, wa, ba, wx, bx, lam, gg]
    vec = SDS((1, W_A), F32)
    sq = SDS((W_A, W_A), F32)
    outs = [SDS((s, 2 * W_A), F32), SDS((8, W_A), F32), vec, sq, vec, sq, vec, vec, vec]
    return pl.pallas_call(
        body, name="lru_bwd", grid=(nc,),
        in_specs=[rev(0), rev(0), prev(0), rev(1), rev(0), prev(0)] + [full(a) for a in params],
        out_specs=[pl.BlockSpec((tc, 2 * W_A), lambda t: (nc - 1 - t, 0))]
        + [pl.BlockSpec(o.shape, lambda t: (0, 0)) for o in outs[1:]],
        out_shape=outs,
        scratch_shapes=[pltpu.VMEM((tc + pad, W_A), F32), pltpu.VMEM((tc + pad, W_A), F32),
                        pltpu.VMEM((1, W_A), F32), pltpu.VMEM((pad, W_A), F32)],
        compiler_params=_cp("arbitrary"),
    )(dy, proj, proj, proj, h, h, *params)


def _attn_stack(qa, qb, kvh):
    lane = lax.broadcasted_iota(jnp.int32, qa.shape, 1)
    keep = (lane >= HD) if kvh == 1 else (lane < HD)
    parts = []
    for tile in (qa, qb):
        for half in (0, 1):
            y = tile if half == kvh else pltpu.roll(tile, HD, axis=1)
            parts.append(jnp.where(keep, y, 0.0))
    return jnp.concatenate(parts, axis=0)


def _attn_unstack(o, kvh):
    lane = lax.broadcasted_iota(jnp.int32, (BLK, 2 * HD), 1)
    tiles = []
    for t in range(2):
        halves = []
        for half in (0, 1):
            blk = o[(2 * t + half) * BLK:(2 * t + half + 1) * BLK, :]
            halves.append(blk if half == kvh else pltpu.roll(blk, HD, axis=1))
        tiles.append(jnp.where(lane < HD, halves[0], halves[1]))
    return tiles


def _attn_stack_all(x_ref_or_val):
    return jnp.concatenate([_attn_stack(x_ref_or_val[:, 256 * kvh:256 * kvh + 128],
                                        x_ref_or_val[:, 256 * kvh + 128:256 * kvh + 256], kvh) for kvh in range(2)], axis=0)


def _attn_unstack_all(o, dst_ref):
    for kvh in range(2):
        ta, tb = _attn_unstack(o[4 * BLK * kvh:4 * BLK * (kvh + 1), :], kvh)
        dst_ref[:, 256 * kvh:256 * kvh + 128] = ta
        dst_ref[:, 256 * kvh + 128:256 * kvh + 256] = tb


def _attn_windows(cur_ref, prev_ref, nb):
    blocks = [prev_ref[...]] + [cur_ref[b * BLK:(b + 1) * BLK, :] for b in range(nb)]
    return [jnp.concatenate(blocks[b:b + 2], axis=0).astype(BF16) for b in range(nb)]


def _attn_probs(qs, kw, n, sink_ref):
    rows = NQ * BLK
    sc = lax.dot_general(qs.astype(BF16), kw, NT, preferred_element_type=F32) * SCALE
    qi = lax.broadcasted_iota(jnp.int32, (rows, 2 * BLK), 0) & (BLK - 1)
    kj = lax.broadcasted_iota(jnp.int32, (rows, 2 * BLK), 1)
    rel = BLK + qi - kj
    mask = (rel >= 0) & (rel < BLK) & ((n - 1) * BLK + kj >= 0)
    head = lax.broadcasted_iota(jnp.int32, (rows, 1), 0) // BLK
    sk = jnp.zeros((rows, 1), F32)
    for h in range(NQ):
        sk = jnp.where(head == h, sink_ref[h:h + 1, 0:1], sk)
    sh = jnp.where(mask, sc, NEG_BIG)
    m = jnp.maximum(jnp.max(sh, axis=-1, keepdims=True), sk)
    e = jnp.exp(sh - m)
    es = jnp.exp(sk - m)
    rz = 1.0 / (jnp.sum(e, axis=-1, keepdims=True) + es)
    return e * rz, es * rz


def _attn_fwd(proj, sinks8, gg):
    s = proj.shape[0]
    nb = ATT_NB_FWD

    def body(q_ref, kc_ref, kp_ref, vc_ref, vp_ref, sink_ref, gg_ref, yn_ref, ob_ref):
        kws, vws = _attn_windows(kc_ref, kp_ref, nb), _attn_windows(vc_ref, vp_ref, nb)
        for b in range(nb):
            rows = pl.ds(b * BLK, BLK)
            p, _ = _attn_probs(_attn_stack_all(q_ref.at[rows, :]), kws[b], nb * pl.program_id(0) + b, sink_ref)
            _attn_unstack_all(jnp.dot(p.astype(BF16), vws[b], preferred_element_type=F32), ob_ref.at[rows, :])
        ob = ob_ref[...]
        yn_ref[...] = (ob * _rsq(ob, NORM_EPS) * gg_ref[...]).astype(BF16)

    tb = nb * BLK
    cur = lambda c: pl.BlockSpec((tb, 128), lambda m, c=c: (m, c))
    prev = lambda c: pl.BlockSpec((BLK, 128), lambda m, c=c: (jnp.maximum(nb * m - 1, 0), c))
    out = pl.BlockSpec((tb, W_B), lambda m: (m, 0))
    return pl.pallas_call(
        body, name="attn_fwd", grid=(s // tb,),
        in_specs=[pl.BlockSpec((tb, W_B), lambda m: (m, 1)), cur(8), prev(8), cur(9), prev(9),
                  pl.BlockSpec((8, 128), lambda n: (0, 0)), pl.BlockSpec((1, W_B), lambda n: (0, 0))],
        out_specs=[out, out], out_shape=[SDS((s, W_B), BF16), SDS((s, W_B), F32)],
        compiler_params=_cp("parallel"),
    )(proj, proj, proj, proj, proj, sinks8, gg)


def _attn_bwd(dy, proj, ob, sinks8, gg):
    s = proj.shape[0]
    nb = ATT_NB_BWD

    def body(dya_ref, dyb_ref, q_ref, kc_ref, kp_ref, vc_ref, vp_ref, ob_ref, sink_ref, gg_ref,
             dq_ref, dcur_ref, dprev_ref, dsink_ref, dgg_ref):
        first = pl.program_id(0) == 0
        kws, vws = _attn_windows(kc_ref, kp_ref, nb), _attn_windows(vc_ref, vp_ref, nb)
        dyn = jnp.concatenate([dya_ref[...], dyb_ref[...]], axis=1)
        dob, dggr = _rms_bwd_rows(ob_ref[...], gg_ref[...], dyn)
        _acc(dgg_ref, first, jnp.sum(dggr, axis=0, keepdims=True))
        r8 = _row_iota((8, 128))
        dsk = jnp.zeros((8, 128), F32)
        for b in range(nb):
            rows = pl.ds(b * BLK, BLK)
            qs = _attn_stack_all(q_ref.at[rows, :])
            p, psink = _attn_probs(qs, kws[b], nb * pl.program_id(0) + b, sink_ref)
            dosb = _attn_stack_all(dob[b * BLK:(b + 1) * BLK, :]).astype(BF16)
            dp = lax.dot_general(dosb, vws[b], NT, preferred_element_type=F32)
            dd = jnp.sum(p * dp, axis=-1, keepdims=True)
            dsb = (p * (dp - dd) * SCALE).astype(BF16)
            dsink_rows = -psink * dd
            for h in range(NQ):
                dsk = dsk + jnp.where(r8 == h, jnp.sum(dsink_rows[h * BLK:(h + 1) * BLK, :], axis=0, keepdims=True), 0.0)
            _attn_unstack_all(jnp.dot(dsb, kws[b], preferred_element_type=F32), dq_ref.at[rows, :])
            dkw = lax.dot_general(dsb, qs.astype(BF16), TN, preferred_element_type=F32)
            dvw = lax.dot_general(p.astype(BF16), dosb, TN, preferred_element_type=F32)
            dprev_ref[rows, 0:128] = dkw[0:BLK, :]
            dprev_ref[rows, 128:256] = dvw[0:BLK, :]
            dcur_ref[rows, 0:128] = dkw[BLK:2 * BLK, :]
            dcur_ref[rows, 128:256] = dvw[BLK:2 * BLK, :]
        _acc(dsink_ref, first, dsk)

    tb = nb * BLK
    cur = lambda c: pl.BlockSpec((tb, 128), lambda m, c=c: (m, c))
    prev = lambda c: pl.BlockSpec((BLK, 128), lambda m, c=c: (jnp.maximum(nb * m - 1, 0), c))
    wide = pl.BlockSpec((tb, W_B), lambda m: (m, 0))
    half = pl.BlockSpec((tb, 256), lambda m: (m, 0))
    return pl.pallas_call(
        body, name="attn_bwd", grid=(s // tb,),
        in_specs=[pl.BlockSpec((tb, 256), lambda m: (m, 1)), pl.BlockSpec((tb, 256), lambda m: (m, 2)),
                  pl.BlockSpec((tb, W_B), lambda m: (m, 1)), cur(8), prev(8), cur(9), prev(9), wide,
                  pl.BlockSpec((8, 128), lambda n: (0, 0)), pl.BlockSpec((1, W_B), lambda n: (0, 0))],
        out_specs=[wide, half, half, pl.BlockSpec((8, 128), lambda n: (0, 0)), pl.BlockSpec((1, W_B), lambda n: (0, 0))],
        out_shape=[SDS((s, W_B), F32), SDS((s, 256), F32), SDS((s, 256), F32), SDS((8, 128), F32), SDS((1, W_B), F32)],
        compiler_params=_cp("arbitrary"),
    )(dy, dy, proj, proj, proj, proj, proj, ob, sinks8, gg)


def _ln_parts(y1, eps=LN_EPS):
    mu = jnp.mean(y1, axis=-1, keepdims=True)
    xc = y1 - mu
    rstd = lax.rsqrt(jnp.mean(xc * xc, axis=-1, keepdims=True) + eps)
    return xc * rstd, rstd


def _conf_fwd(proj, cw, cb, lg, lb, gg, tc):
    s = proj.shape[0]
    pad = 32

    def body(ac_ref, gc_ref, ap_ref, gp_ref, cw_ref, cb_ref, lg_ref, lb_ref, gg_ref, yn_ref, y1_ref, ys_ref):
        i = pl.program_id(0)
        tail = ap_ref[tc - pad:tc, :] * _sig(gp_ref[tc - pad:tc, :])
        ys_ref[0:pad, :] = jnp.where(i > 0, tail, 0.0)
        ys_ref[pad:pad + tc, :] = ac_ref[...] * _sig(gc_ref[...])
        y1 = _conv_taps(ys_ref, cw_ref, CONV_K, pad, tc) + cb_ref[...]
        y1_ref[...] = y1
        xh, _ = _ln_parts(y1)
        yl = xh * lg_ref[...] + lb_ref[...]
        yc = yl * _sig(yl)
        yn_ref[...] = (yc * _rsq(yc, NORM_EPS) * gg_ref[...]).astype(BF16)

    cur = lambda c: pl.BlockSpec((tc, W_C), lambda i, c=c: (i, c))
    prev = lambda c: pl.BlockSpec((tc, W_C), lambda i, c=c: (jnp.maximum(i - 1, 0), c))
    full = lambda a: pl.BlockSpec(a.shape, lambda i: (0,) * a.ndim)
    params = [cw, cb, lg, lb, gg]
    out = pl.BlockSpec((tc, W_C), lambda i: (i, 0))
    return pl.pallas_call(
        body, name="conf_fwd", grid=(s // tc,),
        in_specs=[cur(5), cur(6), prev(5), prev(6)] + [full(a) for a in params],
        out_specs=[out, out], out_shape=[SDS((s, W_C), BF16), SDS((s, W_C), F32)],
        scratch_shapes=[pltpu.VMEM((tc + pad, W_C), F32)],
        compiler_params=_cp("parallel"),
    )(proj, proj, proj, proj, *params)


def _conf_bwd(dy, proj, y1, cw, cb, lg, lb, gg, tc):
    s = proj.shape[0]
    nc = s // tc
    pad = 32

    def body(dy_ref, ac_ref, gc_ref, ap_ref, gp_ref, y1_ref, cw_ref, cb_ref, lg_ref, lb_ref, gg_ref,
             dp_ref, dcw_ref, dcb_ref, dlg_ref, dlb_ref, dgg_ref, ys_ref, ds_ref, nx_ref):
        step = pl.program_id(0)
        i = nc - 1 - step
        first = step == 0

        @pl.when(first)
        def _():
            nx_ref[...] = jnp.zeros_like(nx_ref)

        a = ac_ref[...]
        sg = _sig(gc_ref[...])
        tail = ap_ref[tc - pad:tc, :] * _sig(gp_ref[tc - pad:tc, :])
        ys_ref[0:pad, :] = jnp.where(i > 0, tail, 0.0)
        ys_ref[pad:pad + tc, :] = a * sg
        xh, rstd = _ln_parts(y1_ref[...])
        yl = xh * lg_ref[...] + lb_ref[...]
        sl = _sig(yl)
        yc = yl * sl
        dyc, dggr = _rms_bwd_rows(yc, gg_ref[...], dy_ref[...])
        _acc(dgg_ref, first, jnp.sum(dggr, axis=0, keepdims=True))
        dyl = dyc * sl * (1.0 + yl * (1.0 - sl))
        _acc(dlg_ref, first, jnp.sum(dyl * xh, axis=0, keepdims=True))
        _acc(dlb_ref, first, jnp.sum(dyl, axis=0, keepdims=True))
        dxh = dyl * lg_ref[...]
        dy1 = rstd * (dxh - jnp.mean(dxh, axis=-1, keepdims=True) - xh * jnp.mean(dxh * xh, axis=-1, keepdims=True))
        _acc(dcb_ref, first, jnp.sum(dy1, axis=0, keepdims=True))
        r32 = _row_iota((32, W_C))
        dcw = jnp.zeros((32, W_C), F32)
        for j in range(CONV_K):
            tap = jnp.sum(dy1 * ys_ref[pl.ds(pad - (CONV_K - 1) + j, tc), :], axis=0, keepdims=True)
            dcw = dcw + jnp.where(r32 == j, tap, 0.0)
        _acc(dcw_ref, first, dcw)
        ds_ref[0:tc, :] = dy1
        ds_ref[tc:tc + pad, :] = nx_ref[...]
        dy0 = None
        for j in range(CONV_K):
            term = cw_ref[j:j + 1, :] * ds_ref[pl.ds(CONV_K - 1 - j, tc), :]
            dy0 = term if dy0 is None else dy0 + term
        dp_ref[:, 0:W_C] = dy0 * sg
        dp_ref[:, W_C:2 * W_C] = dy0 * a * sg * (1.0 - sg)
        nx_ref[...] = dy1[0:pad, :]

    rev = lambda c: pl.BlockSpec((tc, W_C), lambda t, c=c: (nc - 1 - t, c))
    prev = lambda c: pl.BlockSpec((tc, W_C), lambda t, c=c: (jnp.maximum(nc - 2 - t, 0), c))
    full = lambda a: pl.BlockSpec(a.shape, lambda t: (0,) * a.ndim)
    params = [cw, cb, lg, lb, gg]
    vec = SDS((1, W_C), F32)
    outs = [SDS((s, 2 * W_C), F32), SDS((32, W_C), F32), vec, vec, vec, vec]
    return pl.pallas_call(
        body, name="conf_bwd", grid=(nc,),
        in_specs=[rev(3), rev(5), rev(6), prev(5), prev(6), rev(0)] + [full(a) for a in params],
        out_specs=[pl.BlockSpec((tc, 2 * W_C), lambda t: (nc - 1 - t, 0))]
        + [pl.BlockSpec(o.shape, lambda t: (0, 0)) for o in outs[1:]],
        out_shape=outs,
        scratch_shapes=[pltpu.VMEM((tc + pad, W_C), F32), pltpu.VMEM((tc + pad, W_C), F32), pltpu.VMEM((pad, W_C), F32)],
        compiler_params=_cp("arbitrary"),
    )(dy, proj, proj, proj, proj, y1, *params)


def _assemble_dproj(dlru, dq, dcur, dprev, dconf):
    s = dq.shape[0]
    nb = s // BLK

    def body(dl_ref, dq_ref, dc_ref, dn_ref, df_ref, o_ref):
        n = pl.program_id(0)
        o_ref[:, 0:512] = dl_ref[...].astype(BF16)
        o_ref[:, 512:1024] = dq_ref[...].astype(BF16)
        o_ref[:, 1024:1280] = (dc_ref[...] + jnp.where(n < nb - 1, dn_ref[...], 0.0)).astype(BF16)
        o_ref[:, 1280:1792] = df_ref[...].astype(BF16)

    wide = pl.BlockSpec((BLK, 512), lambda n: (n, 0))
    return pl.pallas_call(
        body, name="assemble_dproj", grid=(nb,),
        in_specs=[wide, wide, pl.BlockSpec((BLK, 256), lambda n: (n, 0)),
                  pl.BlockSpec((BLK, 256), lambda n: (jnp.minimum(n + 1, nb - 1), 0)), wide],
        out_specs=pl.BlockSpec((BLK, P_IN), lambda n: (n, 0)), out_shape=SDS((s, P_IN), BF16),
        compiler_params=_cp("parallel"),
    )(dlru, dq, dcur, dprev, dconf)


def _loss_grad(y, t, tm):
    s = y.shape[0]

    def body(y_ref, t_ref, dy_ref, l_ref):
        err = y_ref[...] - t_ref[...]
        dy_ref[...] = err * (1.0 / D)
        _acc(l_ref, pl.program_id(0) == 0, jnp.sum(err * err, axis=0, keepdims=True))

    row = pl.BlockSpec((tm, D), lambda i: (i, 0))
    return pl.pallas_call(
        body, name="loss_grad", grid=(s // tm,), in_specs=[row, row],
        out_specs=[row, pl.BlockSpec((1, D), lambda i: (0, 0))],
        out_shape=[SDS((s, D), F32), SDS((1, D), F32)], compiler_params=_cp("arbitrary"),
    )(y, t)


def _block_diag(w):
    rows = [jnp.concatenate([w[h] if k == h else jnp.zeros((64, 64), w.dtype) for k in range(4)], axis=1) for h in range(4)]
    return jnp.concatenate(rows, axis=0)


def _diag_blocks(m):
    return jnp.stack([m[64 * h:64 * (h + 1), 64 * h:64 * (h + 1)] for h in range(4)])


def _layer_params(small, l):
    v = lambda name: small[name][l].reshape(1, -1)
    gg = small["group_g"][l]
    return dict(
        ffn1_pre=v("ffn1_pre_g"), ffn1_post=v("ffn1_post_g"), mix_pre=v("mix_pre_g"), mix_post=v("mix_post_g"),
        ffn2_pre=v("ffn2_pre_g"), ffn2_post=v("ffn2_post_g"), lru_cb=v("lru_conv_b"),
        wa=_block_diag(small["lru_w_a"][l]).astype(BF16), ba=v("lru_b_a"),
        wx=_block_diag(small["lru_w_x"][l]).astype(BF16), bx=v("lru_b_x"), lam=v("lru_lambda"),
        sinks8=jnp.broadcast_to(small["attn_sinks"][l][:, None], (NQ, 128)),
        conv_b=v("conv_b"), ln_g=v("conv_ln_g"), ln_b=v("conv_ln_b"),
        gg_a=gg[0:W_A].reshape(1, -1), gg_b=gg[W_A:W_A + W_B].reshape(1, -1), gg_c=gg[W_A + W_B:].reshape(1, -1),
    )


def _forward_layer(x, weights, p, tiles, deps=()):
    _, mm, _, tc = tiles
    big = dict(weights("ffn1_gu", x))
    p = dict(p)
    sv = dict(x0=x)
    h1, g1, u1, a1 = _ffn_up(x, p["ffn1_pre"], big["ffn1_w_gu"], 0, mm, deps)
    big.update(weights("ffn1_down", a1))
    z1, x = _mm_rms_res(a1, big["ffn1_w_down"], 0, x, p["ffn1_post"], 0.5, mm, FH, "ffn_down")
    sv.update(h1=h1, g1=g1, u1=u1, a1=a1, z1=z1, x1=x)
    big.update(weights("mix", x))
    p.update(lru_cw=big.pop("lru_conv_w"), conv_w=big.pop("conv_w"))
    hn, proj = _proj(x, p["mix_pre"], big["w_in"], 0, mm)
    yn_a, hl = _lru_fwd(proj, p["lru_cw"], p["lru_cb"], p["wa"], p["ba"], p["wx"], p["bx"], p["lam"], p["gg_a"], tc)
    yn_b, ob = _attn_fwd(proj, p["sinks8"], p["gg_b"])
    yn_c, y1 = _conf_fwd(proj, p["conv_w"], p["conv_b"], p["ln_g"], p["ln_b"], p["gg_c"], tc)
    ycat = jnp.concatenate([yn_a, yn_b, yn_c], axis=1)
    zo, x = _mm_rms_res(ycat, big["w_out"], 0, x, p["mix_post"], 1.0, mm, D, "mix_out")
    sv.update(hn=hn, proj=proj, hl=hl, ob=ob, y1=y1, ycat=ycat, zo=zo, x2=x)
    big.update(weights("ffn2", x))
    h2, g2, u2, a2 = _ffn_up(x, p["ffn2_pre"], big["ffn2_w_gu"], 0, mm)
    z2, x = _mm_rms_res(a2, big["ffn2_w_down"], 0, x, p["ffn2_post"], 0.5, mm, FH, "ffn_down")
    sv.update(h2=h2, g2=g2, u2=u2, a2=a2, z2=z2, p=p, big=big)
    return x, sv


def _grad_buffers():
    empty = lambda *shape: lax.empty(shape, F32)
    return dict(ffn1_w_gu=empty(1, NSHARD, D, FH), ffn2_w_gu=empty(1, NSHARD, D, FH), ffn1_w_down=empty(1, 1, DFF, D),
                ffn2_w_down=empty(1, 1, DFF, D), w_in=empty(1, 1, D, P_IN), w_out=empty(1, 1, D, D))


def _backward_layer(dx, sv, bufs, tiles, stage):
    p, big = sv["p"], sv["big"]
    tm, mm, dw, tc = tiles
    gr = {}

    def ffn_bwd(dx, which, xin, h, g, u, a, z, pre, post, deps):
        dz, dpost = _rms_bwd(dx, z, post, 0.5, tm, "ffn_post_bwd", deps)
        dg, du = _ffn_bwd_mid(dz, big[which + "_w_down"], 0, g, u, mm)
        bufs[which + "_w_down"] = _mm_tn_into(bufs[which + "_w_down"], a, dz, 0, 0, FH, D, dw, "dw_down")
        bufs[which + "_w_gu"] = _mm_tn_into(bufs[which + "_w_gu"], h, dg, 0, 0, D, FH, dw, "dw_gate")
        bufs[which + "_w_gu"] = _mm_tn_into(bufs[which + "_w_gu"], h, du, 0, 2, D, FH, dw, "dw_up")
        deps = stage({n: bufs[n] for n in (which + "_w_gu", which + "_w_down")}, bufs[which + "_w_gu"])
        dxn, dpre = _ffn_bwd_dh(dg, du, big[which + "_w_gu"], 0, xin, pre, dx, mm, deps)
        return dxn, dpre, dpost

    dx, gr["ffn2_pre_g"], gr["ffn2_post_g"] = ffn_bwd(dx, "ffn2", sv["x2"], sv["h2"], sv["g2"], sv["u2"], sv["a2"],
                                                      sv["z2"], p["ffn2_pre"], p["ffn2_post"], ())
    do, gr["mix_post_g"] = _rms_bwd(dx, sv["zo"], p["mix_post"], 1.0, tm, "mix_post_bwd")
    bufs["w_out"] = _mm_tn_into(bufs["w_out"], sv["ycat"], do, 0, 0, D, D, dw, "dw_out")
    dy = _mm_nt(do, big["w_out"], 0, mm, "mix_dy")
    proj = sv["proj"]
    (dlru, dcw, gr["lru_conv_b"], dwa, gr["lru_b_a"], dwx, gr["lru_b_x"], gr["lru_lambda"], dgg_a) = _lru_bwd(
        dy, proj, sv["hl"], p["lru_cw"], p["lru_cb"], p["wa"], p["ba"], p["wx"], p["bx"], p["lam"], p["gg_a"], tc)
    dq, dcur, dprev, dsk, dgg_b = _attn_bwd(dy, proj, sv["ob"], p["sinks8"], p["gg_b"])
    dconf, dconvw, gr["conv_b"], gr["conv_ln_g"], gr["conv_ln_b"], dgg_c = _conf_bwd(
        dy, proj, sv["y1"], p["conv_w"], p["conv_b"], p["ln_g"], p["ln_b"], p["gg_c"], tc)
    dproj = _assemble_dproj(dlru, dq, dcur, dprev, dconf)
    bufs["w_in"] = _mm_tn_into(bufs["w_in"], sv["hn"], dproj, 0, 0, D, P_IN, dw, "dw_in")
    dx, gr["mix_pre_g"] = _mm_nt_rmsbwd(dproj, big["w_in"], 0, sv["x1"], p["mix_pre"], dx, mm)
    gr["lru_conv_w"] = dcw[0:LRU_K]
    gr["lru_w_a"] = _diag_blocks(dwa)
    gr["lru_w_x"] = _diag_blocks(dwx)
    gr["attn_sinks"] = dsk[:, 0]
    gr["conv_w"] = dconvw[0:CONV_K]
    gr["group_g"] = jnp.concatenate([dgg_a, dgg_b, dgg_c], axis=1)
    dx, gr["ffn1_pre_g"], gr["ffn1_post_g"] = ffn_bwd(dx, "ffn1", sv["x0"], sv["h1"], sv["g1"], sv["u1"], sv["a1"],
                                                      sv["z1"], p["ffn1_pre"], p["ffn1_post"],
                                                      stage({n: bufs[n] for n in ("w_in", "w_out")}, dx))
    return dx, gr


def _tiles(s):
    return min(1024, s), min(1024, s), min(2048, s), min(512, s // 2)


HBM_SPEC = pl.BlockSpec(memory_space=pltpu.HBM)
SEM_SPEC = pl.BlockSpec(memory_space=pltpu.SEMAPHORE)
EFFECT = pltpu.SideEffectType.DATAFLOW_SIDE_EFFECTING


def _place():
    x, y, c = lax.axis_index("x"), lax.axis_index("y"), lax.axis_index("c")
    return x, y, c, [(1 - x, y), (x, 1 - y), (1 - x, 1 - y)]


def _rcopy(src, dst, send_sems, recv_sems, k, to):
    return pltpu.make_async_remote_copy(src_ref=src, dst_ref=dst, send_sem=send_sems.at[k], recv_sem=recv_sems.at[k],
                                        device_id=to, device_id_type=MESH)


def _half(rows, which):
    return pl.ds(which * (rows // 2), rows // 2)


def _place_shard(w, l, p_idx, dtype):
    _, rows, cols = w.shape
    tr = _rows_per_block(rows, cols, 16, SUM_BLOCK_ELEMS) if rows % 16 == 0 else rows

    def body(p_ref, buf_ref, w_ref, o_ref):
        o_ref[...] = w_ref[...].astype(dtype)

    spec = pltpu.PrefetchScalarGridSpec(
        num_scalar_prefetch=1, grid=(rows // tr,),
        in_specs=[ANY, pl.BlockSpec((None, tr, cols), lambda i, pr: (l, i, 0))],
        out_specs=pl.BlockSpec((None, None, tr, cols), lambda i, pr: (0, pr[0], i, 0)))
    shape = (1, NSHARD, rows, cols)
    return pl.pallas_call(body, name="place_shard", grid_spec=spec, out_shape=SDS(shape, dtype),
                          input_output_aliases={1: 0}, compiler_params=_cp("parallel"),
                          )(p_idx, lax.empty(shape, dtype), w)


def _gather_two_level(bufs, n_halved):
    n = len(bufs)

    def body(*refs):
        outs = refs[n:2 * n]
        send_sems, recv_sems = refs[2 * n:]
        x, y, c, chips = _place()
        p = 2 * x + y
        me, sibling = (x, y, c), (x, y, 1 - c)

        def blk(a, q, half):
            return outs[a].at[0, q, _half(outs[a].shape[2], half)] if a < n_halved else outs[a].at[0, q]

        def cp(a, k, q, half, to):
            return _rcopy(blk(a, q, half), blk(a, q, half), send_sems, recv_sems, 6 * a + k, to)

        first = [cp(a, j, p, c, (*chip, c)) for a in range(n) for j, chip in enumerate(chips)]
        for d in first:
            d.start()
        passed = []
        for a in range(n):
            for j, chip in enumerate(chips):
                q = 2 * chip[0] + chip[1]
                cp(a, j, q, c, me).wait_recv()
                if a < n_halved:
                    passed.append(cp(a, 3 + j, q, c, sibling))
                    passed[-1].start()
        for a in range(n_halved):
            for j, chip in enumerate(chips):
                cp(a, 3 + j, 2 * chip[0] + chip[1], 1 - c, me).wait_recv()
        for d in first + passed:
            d.wait_send()

    return pl.pallas_call(
        body, name="gather_layer0", in_specs=[ANY] * n, out_specs=[ANY] * n,
        out_shape=[SDS(b.shape, b.dtype) for b in bufs], input_output_aliases={a: a for a in range(n)},
        scratch_shapes=[pltpu.SemaphoreType.DMA((6 * n,)), pltpu.SemaphoreType.DMA((6 * n,))],
    )(*bufs)


def _run_plans(plans, refs, send_sems, recv_sems):
    cps, b0, s0 = [], 0, 0
    for plan, nb, ns in plans:
        cps += plan(refs[b0:b0 + nb], send_sems, recv_sems, s0)
        b0, s0 = b0 + nb, s0 + ns
    return cps


def _exchange(name, bufs, plans):
    n = len(bufs)
    nsem = sum(ns for _, _, ns in plans)

    def body(*refs):
        cps = _run_plans(plans, refs[n:2 * n], refs[2 * n], refs[2 * n + 1])
        for cp in cps:
            cp.start()
        for cp in cps:
            cp.wait()

    return pl.pallas_call(
        body, name=name, in_specs=[ANY] * n, out_specs=[ANY] * n, out_shape=[SDS(b.shape, b.dtype) for b in bufs],
        input_output_aliases={a: a for a in range(n)},
        scratch_shapes=[pltpu.SemaphoreType.DMA((nsem,)), pltpu.SemaphoreType.DMA((nsem,))],
    )(*bufs)


def _exchange_start(name, bufs, plans, deps=()):
    n = len(bufs)
    nsem = sum(ns for _, _, ns in plans)
    deps = list(deps)
    first_out = n + len(deps)

    def body(*refs):
        for cp in _run_plans(plans, refs[:n], refs[first_out], refs[first_out + 1]):
            cp.start()
        token = refs[first_out + 2 + n]
        token[...] = jnp.zeros_like(token)

    outs = pl.pallas_call(
        body, name=name,
        out_shape=(pltpu.SemaphoreType.DMA((nsem,)), pltpu.SemaphoreType.DMA((nsem,)),
                   *[pltpu.HBM(b.shape, b.dtype) for b in bufs], SDS((8, 128), F32)),
        in_specs=[HBM_SPEC] * n + [ANY] * len(deps),
        out_specs=(SEM_SPEC, SEM_SPEC, *[HBM_SPEC] * n, pl.BlockSpec(memory_space=pltpu.VMEM)),
        input_output_aliases={a: 2 + a for a in range(n)},
        compiler_params=pltpu.CompilerParams(has_side_effects=EFFECT),
    )(*[pltpu.with_memory_space_constraint(b, pltpu.HBM) for b in bufs], *deps)
    return outs[0], outs[1], list(outs[2:2 + n]), outs[2 + n]


def _exchange_wait(name, send_sems, recv_sems, bufs, plans, after):
    n = len(bufs)

    def body(*refs):
        for cp in _run_plans(plans, refs[:n], refs[n], refs[n + 1]):
            cp.wait_send()
            cp.wait_recv()

    return pl.pallas_call(
        body, name=name, out_shape=[pltpu.HBM(b.shape, b.dtype) for b in bufs],
        in_specs=[HBM_SPEC] * n + [SEM_SPEC, SEM_SPEC, ANY], out_specs=[HBM_SPEC] * n,
        input_output_aliases={a: a for a in range(n)},
        compiler_params=pltpu.CompilerParams(has_side_effects=EFFECT),
    )(*bufs, send_sems, recv_sems, after)


def _plan_gather(refs, send_sems, recv_sems, base):
    x, y, c, chips = _place()
    p = 2 * x + y
    return [_rcopy(r.at[0, p], r.at[0, p], send_sems, recv_sems, base + 3 * a + j, (*chip, c))
            for a, r in enumerate(refs) for j, chip in enumerate(chips)]


def _plan_pair_exchange(refs, send_sems, recv_sems, base):
    x, y, c, _ = _place()
    n = len(refs) // 2
    return [_rcopy(refs[a].at[:, _half(refs[a].shape[1], 1 - c)], refs[n + a], send_sems, recv_sems, base + a,
                   (x, y, 1 - c)) for a in range(n)]


def _plan_chip_exchange(refs, send_sems, recv_sems, base):
    x, y, c, chips = _place()
    n = len(refs) // 2
    return [_rcopy(refs[a].at[2 * chip[0] + chip[1]], refs[n + a].at[j], send_sems, recv_sems, base + 3 * a + j,
                   (*chip, c)) for a in range(n) for j, chip in enumerate(chips)]


def _plan_pair_share(refs, send_sems, recv_sems, base):
    x, y, c, _ = _place()
    return [_rcopy(r.at[_half(r.shape[0], c)], r.at[_half(r.shape[0], c)], send_sems, recv_sems, base + a,
                   (x, y, 1 - c)) for a, r in enumerate(refs)]


def _plan_small_gather(refs, send_sems, recv_sems, base):
    x, y, c, _ = _place()
    me = 4 * x + 2 * y + c
    cps = []
    for m in range(1, NDEV):
        peer = (1 - x if m & 4 else x, 1 - y if m & 2 else y, 1 - c if m & 1 else c)
        cps.append(_rcopy(refs[0], refs[1].at[me], send_sems, recv_sems, base + m - 1, peer))
    return cps


def _sum_small(buf, gathered):
    def body(buf_ref, g_ref, o_ref):
        x, y, c, _ = _place()
        me = 4 * x + 2 * y + c
        total = jnp.where(me == 0, buf_ref[...], g_ref[0])
        for dev in range(1, NDEV):
            total = total + jnp.where(me == dev, buf_ref[...], g_ref[dev])
        o_ref[...] = total

    vm = pl.BlockSpec(memory_space=pltpu.VMEM)
    return pl.pallas_call(body, name="sum_small", in_specs=[vm, vm], out_specs=vm, out_shape=SDS(buf.shape, F32),
                          compiler_params=pltpu.CompilerParams(vmem_limit_bytes=VMEM_LIMIT))(buf, gathered)


BLOCK_ELEMS = 256 * 1024
SUM_BLOCK_ELEMS = 1024 * 1024


def _rows_per_block(rows, cols, mult, limit=BLOCK_ELEMS):
    best = None
    for tr in range(mult, rows + 1, mult):
        if rows % tr == 0 and tr * cols <= limit:
            best = tr
    assert best is not None, (rows, cols)
    return best


def _pair_sum(g, r, c_idx):
    nq, rows, cols = g.shape
    half = rows // 2
    tr = _rows_per_block(half, cols, 16, SUM_BLOCK_ELEMS)
    nb = half // tr

    def body(c_ref, g_ref, r_ref, t_ref):
        t_ref[...] = (g_ref[...] + r_ref[...]).astype(BF16)

    blk = pl.BlockSpec((None, tr, cols), lambda q, i, cr: (q, i, 0))
    spec = pltpu.PrefetchScalarGridSpec(
        num_scalar_prefetch=1, grid=(nq, nb),
        in_specs=[pl.BlockSpec((None, tr, cols), lambda q, i, cr: (q, cr[0] * nb + i, 0)), blk], out_specs=blk)
    return pl.pallas_call(body, name="grad_pair_sum", grid_spec=spec, out_shape=SDS((nq, half, cols), BF16),
                          compiler_params=_cp("parallel", "parallel"))(c_idx, g, r)


def _chip_sum(g, r, rr, cp_idx):
    _, rows, cols = g.shape
    half = rows // 2
    tr = _rows_per_block(half, cols, 16, SUM_BLOCK_ELEMS)
    nb = half // tr

    def body(cp_ref, buf_ref, g_ref, r_ref, rr_ref, o_ref):
        o_ref[...] = ((g_ref[...] + r_ref[...]) + rr_ref[0].astype(F32) + rr_ref[1].astype(F32) + rr_ref[2].astype(F32))

    spec = pltpu.PrefetchScalarGridSpec(
        num_scalar_prefetch=1, grid=(nb,),
        in_specs=[ANY, pl.BlockSpec((None, tr, cols), lambda i, cp: (cp[1], cp[0] * nb + i, 0)),
                  pl.BlockSpec((None, tr, cols), lambda i, cp: (cp[1], i, 0)),
                  pl.BlockSpec((3, tr, cols), lambda i, cp: (0, i, 0))],
        out_specs=pl.BlockSpec((tr, cols), lambda i, cp: (cp[0] * nb + i, 0)))
    return pl.pallas_call(body, name="grad_chip_sum", grid_spec=spec, out_shape=SDS((rows, cols), F32),
                          input_output_aliases={1: 0}, compiler_params=_cp("parallel"),
                          )(cp_idx, lax.empty((rows, cols), F32), g, r, rr)


def _adamw_math(w, g, m, v):
    mn = ADAM_B1 * m + (1.0 - ADAM_B1) * g
    vn = ADAM_B2 * v + (1.0 - ADAM_B2) * (g * g)
    m_hat = mn / (1.0 - ADAM_B1 ** ADAM_STEP)
    v_hat = vn / (1.0 - ADAM_B2 ** ADAM_STEP)
    return -ADAM_LR * (m_hat / (jnp.sqrt(v_hat) + ADAM_EPS) + ADAM_WD * w), mn, vn


def _adamw_layer(w, g, m, v, l, outs, deps=()):
    _, rows, cols = w.shape
    tr = _rows_per_block(rows, cols, 8)
    deps = list(deps)

    def body(*refs):
        w_ref, g_ref, m_ref, v_ref = refs[4:8]
        go_ref, d_ref, mo_ref, vo_ref = refs[8 + len(deps):]
        gg = g_ref[...]
        go_ref[...] = gg
        d_ref[...], mo_ref[...], vo_ref[...] = _adamw_math(w_ref[...], gg, m_ref[...], v_ref[...])

    blk = pl.BlockSpec((None, tr, cols), lambda i: (l, i, 0))
    return pl.pallas_call(
        body, name="adamw_layer", grid=(rows // tr,),
        in_specs=[ANY] * 4 + [blk, pl.BlockSpec((tr, cols), lambda i: (i, 0)), blk, blk] + [ANY] * len(deps),
        out_specs=[blk] * 4, out_shape=[SDS(w.shape, F32)] * 4, input_output_aliases={k: k for k in range(4)},
        compiler_params=_cp("parallel"))(*outs, w, g, m, v, *deps)


def _adamw_small(ws, gs, ms, vs, deps=()):
    n = len(ws)
    deps = list(deps)

    def body(*refs):
        refs = refs[:4 * n] + refs[4 * n + len(deps):]
        w, g, m, v, d_out, m_out, v_out = (refs[k * n:(k + 1) * n] for k in range(7))
        for k in range(n):
            d_out[k][...], m_out[k][...], v_out[k][...] = _adamw_math(w[k][...], g[k][...], m[k][...], v[k][...])

    vm = pl.BlockSpec(memory_space=pltpu.VMEM)
    outs = pl.pallas_call(body, name="adamw_small", in_specs=[vm] * (4 * n) + [ANY] * len(deps), out_specs=[vm] * (3 * n),
                          out_shape=[SDS(w.shape, F32) for w in ws] * 3,
                          compiler_params=pltpu.CompilerParams(vmem_limit_bytes=VMEM_LIMIT))(*ws, *gs, *ms, *vs, *deps)
    return outs[:n], outs[n:2 * n], outs[2 * n:]


_WEIGHTS = ["ffn1_pre_g", "ffn1_w_gu", "ffn1_w_down", "ffn1_post_g", "mix_pre_g", "w_in", "lru_conv_w", "lru_conv_b",
            "lru_w_a", "lru_b_a", "lru_w_x", "lru_b_x", "lru_lambda", "attn_sinks", "conv_w", "conv_b", "conv_ln_g",
            "conv_ln_b", "group_g", "w_out", "mix_post_g", "ffn2_pre_g", "ffn2_w_gu", "ffn2_w_down", "ffn2_post_g"]
_INPUTS = ["x"] + _WEIGHTS + ["loss_target"] + ["m_" + n for n in _WEIGHTS] + ["v_" + n for n in _WEIGHTS]
_BIG = ["ffn1_w_gu", "ffn1_w_down", "w_in", "w_out", "ffn2_w_gu", "ffn2_w_down"]
_SMALL_SHARDED = ["lru_conv_w", "conv_w"]
_SMALL_REPL = [n for n in _WEIGHTS if n not in _BIG and n not in _SMALL_SHARDED]

PACK_TILE = 8 * 128


def _pack(arrs):
    parts = []
    for a in arrs:
        flat = a.reshape(-1)
        parts.append(jnp.pad(flat, (0, -flat.shape[0] % PACK_TILE)).reshape(-1, 128))
    return jnp.concatenate(parts, axis=0)


def _unpack(buf, shapes):
    out, row = [], 0
    for shp in shapes:
        size = math.prod(shp)
        nrow = -(-size // PACK_TILE) * 8
        out.append(buf[row:row + nrow].reshape(-1)[:size].reshape(shp))
        row += nrow
    return out


def _unshard_cols(a):
    return a.transpose(0, 2, 1, 3).reshape(1, a.shape[2], NSHARD * a.shape[3])


_GROUPS = dict(ffn1_gu=["ffn1_w_gu"], ffn1_down=["ffn1_w_down"], mix=["w_in", "w_out", "lru_conv_w", "conv_w"],
               ffn2=["ffn2_w_gu", "ffn2_w_down"])


def _full_weights(group, gathered):
    g = dict(zip(_GROUPS[group], gathered))
    if group == "mix":
        return dict(w_in=_unshard_cols(g["w_in"]), w_out=g["w_out"].reshape(1, D, D),
                    lru_conv_w=_unshard_cols(g["lru_conv_w"])[0], conv_w=_unshard_cols(g["conv_w"])[0])
    return {n: (a.reshape(1, DFF, D) if n.endswith("w_down") else a) for n, a in g.items()}


def _by_shard(name, buf):
    if name.endswith("w_gu"):
        return buf[0]
    if name == "w_in":
        return buf.reshape(D, NSHARD, P_IN // NSHARD).transpose(1, 0, 2)
    return buf.reshape(NSHARD, buf.shape[2] // NSHARD, buf.shape[3])


class _Reducer:
    PLANS = (_plan_pair_exchange, _plan_chip_exchange, _plan_pair_share)

    def __init__(self, keys, gs, c_idx, cp_idx):
        self.keys, self.gs, self.c_idx, self.cp_idx = keys, gs, c_idx, cp_idx
        self.n = len(gs)
        self.step = 0
        self.result = None

    def inputs(self):
        n = self.n
        if self.step == 0:
            bufs = self.gs + [lax.empty((NSHARD, g.shape[1] // 2, g.shape[2]), F32) for g in self.gs]
        elif self.step == 1:
            ts = [_pair_sum(g, r, self.c_idx) for g, r in zip(self.gs, self.rs)]
            bufs = ts + [lax.empty((3,) + t.shape[1:], BF16) for t in ts]
        else:
            bufs = [_chip_sum(g, r, rr, self.cp_idx) for g, r, rr in zip(self.gs, self.rs, self.rrs)]
        return bufs, (self.PLANS[self.step], len(bufs), (n, 3 * n, n)[self.step])

    def absorb(self, done):
        n = self.n
        if self.step == 0:
            self.gs, self.rs = done[:n], done[n:]
        elif self.step == 1:
            self.rrs = done[n:]
        else:
            self.result = dict(zip(self.keys, done))
        self.step += 1


class _SmallGather:
    def __init__(self, buf):
        self.buf, self.step, self.result, self.gathered = buf, 0, {}, None

    def inputs(self):
        return [self.buf, jnp.zeros((NDEV,) + self.buf.shape, F32)], (_plan_small_gather, 2, NDEV - 1)

    def absorb(self, done):
        self.buf, self.gathered = done
        self.step = 3


class _ReducePipeline:
    def __init__(self, c_idx, cp_idx):
        self.c_idx, self.cp_idx = c_idx, cp_idx
        self.reducers, self.flying, self.calls = [], None, 0

    def add(self, layer, done):
        if done:
            keys = [(layer, n) for n in done]
            self.reducers.append(_Reducer(keys, [_by_shard(n, b) for n, b in done.items()], self.c_idx, self.cp_idx))

    def _next(self):
        active = [r for r in self.reducers if r.step < 3]
        bufs, plans = [], []
        for r in active:
            b, triple = r.inputs()
            bufs += b
            plans.append(triple)
        self.calls += 1
        return active, bufs, plans, "grad_exchange%d" % self.calls

    def _absorb(self, active, plans, done):
        at = 0
        for r, (_, nb, _) in zip(active, plans):
            r.absorb(done[at:at + nb])
            at += nb

    def _land(self, after):
        if self.flying is not None:
            active, plans, name, send_sems, recv_sems, bufs = self.flying
            self._absorb(active, plans, _exchange_wait(name + "_wait", send_sems, recv_sems, bufs, plans, after))
            self.flying = None

    def hook(self, after):
        self._land(after)
        active, bufs, plans, name = self._next()
        if not active:
            return []
        send_sems, recv_sems, bufs, token = _exchange_start(name + "_start", bufs, plans)
        self.flying = (active, plans, name, send_sems, recv_sems, bufs)
        return [token]

    def available(self):
        out = {}
        for r in self.reducers:
            if r.step == 3:
                out.update(r.result)
        return out

    def finish(self, after):
        self._land(after)
        while True:
            active, bufs, plans, name = self._next()
            if not active:
                break
            self._absorb(active, plans, _exchange(name, bufs, plans))
        out = {}
        for r in self.reducers:
            out.update(r.result)
        return out


def kernel(*args):
    d = dict(zip(_INPUTS, args, strict=True))
    xi, yi, ci = lax.axis_index("x"), lax.axis_index("y"), lax.axis_index("c")
    p = 2 * xi + yi
    c_idx = jnp.reshape(ci, (1,)).astype(jnp.int32)
    p_idx = jnp.reshape(p, (1,)).astype(jnp.int32)
    cp_idx = jnp.stack([ci, p]).astype(jnp.int32)
    x, target = d["x"][0], d["loss_target"][0]
    tiles = _tiles(x.shape[0])

    groups = [(l, grp) for l in range(DEPTH) for grp in _GROUPS]
    placed = {(l, grp): [_place_shard(d[n], l, p_idx, BF16 if n in _BIG else F32) for n in _GROUPS[grp]]
              for l, grp in groups}
    ready = {groups[0]: _gather_two_level(placed[groups[0]], len(placed[groups[0]]))}
    flying, tokens = {}, [ready[groups[0]][0]]
    for l, grp in groups[1:]:
        plans = [(_plan_gather, len(placed[l, grp]), 3 * len(placed[l, grp]))]
        send_sems, recv_sems, bufs, token = _exchange_start("gather_l%d_%s_start" % (l, grp), placed[l, grp], plans,
                                                             tokens[-1:])
        flying[l, grp] = (send_sems, recv_sems, bufs, plans)
        tokens.append(token)

    def weights_of(l):
        def weights(grp, after):
            if (l, grp) not in ready:
                send_sems, recv_sems, bufs, plans = flying[l, grp]
                ready[l, grp] = _exchange_wait("gather_l%d_%s_wait" % (l, grp), send_sems, recv_sems, bufs, plans, after)
            return _full_weights(grp, ready[l, grp])
        return weights

    small = {n: d[n] for n in _SMALL_REPL}
    x1, sv0 = _forward_layer(x, weights_of(0), _layer_params(small, 0), tiles, tokens[1:])
    x2, sv1 = _forward_layer(x1, weights_of(1), _layer_params(small, 1), tiles)
    dx, lcols = _loss_grad(x2, target, tiles[0])

    pipe = _ReducePipeline(c_idx, cp_idx)
    sgrads = [None] * DEPTH
    for l, sv in ((1, sv1), (0, sv0)):
        bufs = _grad_buffers()

        def stage(done, dx, l=l):
            pipe.add(l, done)
            return pipe.hook(dx)

        dx, sgrads[l] = _backward_layer(dx, sv, bufs, tiles, stage)
    grad_x = dx

    stacked = {n: jnp.stack([sgrads[l][n].reshape(d[n].shape[1:]) for l in range(DEPTH)]) for n in _SMALL_REPL}
    for n in _SMALL_SHARDED:
        stacked[n] = jnp.stack([sgrads[l][n] for l in range(DEPTH)])
    loss_part = jnp.pad((0.5 / D) * jnp.sum(lcols).reshape(1), (0, 127))
    order = _SMALL_REPL + _SMALL_SHARDED
    small_gather = _SmallGather(_pack([loss_part] + [stacked[n] for n in order]))
    pipe.reducers.append(small_gather)

    results = {n: tuple(lax.empty(d[n].shape, F32) for _ in range(4)) for n in _BIG}
    applied = set()

    def apply_ready(deps, last):
        for (l, n), g in pipe.available().items():
            if (l, n) not in applied:
                results[n] = _adamw_layer(d[n], g, d["m_" + n], d["v_" + n], l, results[n], deps)
                applied.add((l, n))
                last = results[n][1]
                deps = [last]
        return last

    last = apply_ready(pipe.hook(grad_x), grad_x)
    token = pipe.hook(last)
    summed = _unpack(_sum_small(small_gather.buf, small_gather.gathered), [(128,)] + [stacked[n].shape for n in order])
    loss = summed[0][0]
    grads = {}
    for n, g in zip(order, summed[1:]):
        if n in _SMALL_SHARDED:
            g = lax.dynamic_slice_in_dim(g, p * (g.shape[2] // NSHARD), g.shape[2] // NSHARD, axis=2)
        grads[n] = g
    delta, new_m, new_v = {}, {}, {}
    small_out = _adamw_small([d[n] for n in order], [grads[n] for n in order], [d["m_" + n] for n in order],
                             [d["v_" + n] for n in order], token)
    for out, res in zip((delta, new_m, new_v), small_out):
        out.update(zip(order, res))
    last = apply_ready([small_out[0][0]], small_out[0][0])
    pipe.finish(last)
    apply_ready((), last)
    for n in _BIG:
        grads[n], delta[n], new_m[n], new_v[n] = results[n]

    return (loss, grad_x[None], *[grads[n] for n in _WEIGHTS], *[delta[n] for n in _WEIGHTS],
            *[new_m[n] for n in _WEIGHTS], *[new_v[n] for n in _WEIGHTS])
```

```python
import functools
import math

import jax
import jax.numpy as jnp
from jax import lax
from jax.experimental import pallas as pl
from jax.experimental.pallas import tpu as pltpu

F32 = jnp.float32
BF16 = jnp.bfloat16
SDS = jax.ShapeDtypeStruct

D = 1024
DFF = 2816
FH = DFF // 2
DEPTH = 2
W_A = 256
W_B = 512
W_C = 256
NQ = 8
HD = 64
BLK = 128
ATT_NB_FWD = 1
ATT_NB_BWD = 4
P_IN = 1792
LRU_K = 4
CONV_K = 31
LRU_C = 8.0
NORM_EPS = 1e-6
LN_EPS = 1e-5
NEG_BIG = -1e30
SCALE = 1.0 / math.sqrt(HD)

ADAM_LR = 0.001
ADAM_B1 = 0.9
ADAM_B2 = 0.999
ADAM_EPS = 1e-08
ADAM_WD = 0.01
ADAM_STEP = 10

VMEM_LIMIT = 60 * 1024 * 1024
NSHARD = 4
NDEV = 8

TN = (((0,), (0,)), ((), ()))
NT = (((1,), (1,)), ((), ()))

MESH = pl.DeviceIdType.MESH
ANY = pl.BlockSpec(memory_space=pl.ANY)


def _cp(*sem):
    return pltpu.CompilerParams(dimension_semantics=sem if sem else None, vmem_limit_bytes=VMEM_LIMIT)


def _rsq(x, eps):
    return lax.rsqrt(jnp.mean(x * x, axis=-1, keepdims=True) + eps)


def _rms_bwd_rows(x, g, dy):
    r = _rsq(x, NORM_EPS)
    xh = x * r
    dyg = dy * g
    dx = r * (dyg - xh * jnp.mean(dyg * xh, axis=-1, keepdims=True))
    return dx, dy * xh


def _sig(x):
    return jax.nn.sigmoid(x)


def _ffn_up(x, pre_g, wgu, l, tm, deps=()):
    s = x.shape[0]
    deps = list(deps)

    def body(x_ref, g_ref, wg_ref, wu_ref, *rest):
        h_ref, go_ref, uo_ref, a_ref = rest[len(deps):]

        @pl.when(pl.program_id(1) == 0)
        def _():
            xf = x_ref[...]
            h_ref[...] = (xf * _rsq(xf, NORM_EPS) * g_ref[...]).astype(BF16)

        h = h_ref[...]
        gg = jnp.dot(h, wg_ref[...], preferred_element_type=F32)
        uu = jnp.dot(h, wu_ref[...], preferred_element_type=F32)
        sg = _sig(gg)
        silu = gg * sg
        go_ref[...] = (uu * (sg * (1.0 + gg * (1.0 - sg)))).astype(BF16)
        uo_ref[...] = silu.astype(BF16)
        a_ref[...] = (silu * uu).astype(BF16)

    wide = pl.BlockSpec((tm, FH), lambda i, j: (i, j))
    return pl.pallas_call(
        body, name="ffn_up", grid=(s // tm, 2),
        in_specs=[pl.BlockSpec((tm, D), lambda i, j: (i, 0)), pl.BlockSpec((1, D), lambda i, j: (0, 0)),
                  pl.BlockSpec((None, None, D, FH), lambda i, j: (l, j, 0, 0)),
                  pl.BlockSpec((None, None, D, FH), lambda i, j: (l, j + 2, 0, 0))] + [ANY] * len(deps),
        out_specs=[pl.BlockSpec((tm, D), lambda i, j: (i, 0)), wide, wide, wide],
        out_shape=[SDS((s, D), BF16), SDS((s, DFF), BF16), SDS((s, DFF), BF16), SDS((s, DFF), BF16)],
        compiler_params=_cp("parallel", "arbitrary"),
    )(x, pre_g, wgu, wgu, *deps)


def _mm_rms_res(a, w, l, x, g, c, tm, tk, name):
    s, k_dim = a.shape
    nk = k_dim // tk

    def body(a_ref, w_ref, x_ref, g_ref, z_ref, x1_ref):
        k = pl.program_id(1)
        p = jnp.dot(a_ref[...], w_ref[...], preferred_element_type=F32)

        @pl.when(k == 0)
        def _():
            z_ref[...] = p

        @pl.when(k > 0)
        def _():
            z_ref[...] += p

        @pl.when(k == nk - 1)
        def _():
            z = z_ref[...]
            x1_ref[...] = x_ref[...] + c * (z * _rsq(z, NORM_EPS) * g_ref[...])

    row = pl.BlockSpec((tm, D), lambda i, k: (i, 0))
    return pl.pallas_call(
        body, name=name, grid=(s // tm, nk),
        in_specs=[pl.BlockSpec((tm, tk), lambda i, k: (i, k)), pl.BlockSpec((None, tk, D), lambda i, k: (l, k, 0)),
                  row, pl.BlockSpec((1, D), lambda i, k: (0, 0))],
        out_specs=[row, row],
        out_shape=[SDS((s, D), F32), SDS((s, D), F32)],
        compiler_params=_cp("parallel", "arbitrary"),
    )(a, w, x, g)


def _rms_bwd(dy, z, g, c, tm, name, deps=()):
    s = z.shape[0]
    deps = list(deps)

    def body(dy_ref, z_ref, g_ref, *rest):
        dz_ref, dg_ref = rest[len(deps):]
        dz, dgr = _rms_bwd_rows(z_ref[...], g_ref[...], c * dy_ref[...])
        dz_ref[...] = dz.astype(BF16)
        part = jnp.sum(dgr, axis=0, keepdims=True)

        @pl.when(pl.program_id(0) == 0)
        def _():
            dg_ref[...] = part

        @pl.when(pl.program_id(0) > 0)
        def _():
            dg_ref[...] += part

    row = pl.BlockSpec((tm, D), lambda i: (i, 0))
    vec = pl.BlockSpec((1, D), lambda i: (0, 0))
    return pl.pallas_call(
        body, name=name, grid=(s // tm,), in_specs=[row, row, vec] + [ANY] * len(deps), out_specs=[row, vec],
        out_shape=[SDS((s, D), BF16), SDS((1, D), F32)], compiler_params=_cp("arbitrary"),
    )(dy, z, g, *deps)


def _ffn_bwd_mid(dz, wd, l, dadg, dadu, tm):
    s = dz.shape[0]

    def body(dz_ref, wd_ref, g_ref, u_ref, dg_ref, du_ref):
        da = lax.dot_general(dz_ref[...], wd_ref[...], NT, preferred_element_type=F32)
        dg_ref[...] = (da * g_ref[...].astype(F32)).astype(BF16)
        du_ref[...] = (da * u_ref[...].astype(F32)).astype(BF16)

    wide = pl.BlockSpec((tm, FH), lambda i, j: (i, j))
    return pl.pallas_call(
        body, name="ffn_bwd_mid", grid=(s // tm, 2),
        in_specs=[pl.BlockSpec((tm, D), lambda i, j: (i, 0)), pl.BlockSpec((None, FH, D), lambda i, j: (l, j, 0)), wide, wide],
        out_specs=[wide, wide],
        out_shape=[SDS((s, DFF), BF16), SDS((s, DFF), BF16)],
        compiler_params=_cp("parallel", "arbitrary"),
    )(dz, wd, dadg, dadu)


def _ffn_bwd_dh(dg, du, wgu, l, x, pre_g, dx1, tm, deps=()):
    s = x.shape[0]
    deps = list(deps)

    def body(dg_ref, du_ref, wg_ref, wu_ref, x_ref, g_ref, dx1_ref, *rest):
        dx_ref, dgp_ref = rest[len(deps):]
        i, k = pl.program_id(0), pl.program_id(1)
        p = (lax.dot_general(dg_ref[...], wg_ref[...], NT, preferred_element_type=F32)
             + lax.dot_general(du_ref[...], wu_ref[...], NT, preferred_element_type=F32))

        @pl.when(k == 0)
        def _():
            dx_ref[...] = p

        @pl.when(k == 1)
        def _():
            dx, dgr = _rms_bwd_rows(x_ref[...], g_ref[...], dx_ref[...] + p)
            dx_ref[...] = dx1_ref[...] + dx
            part = jnp.sum(dgr, axis=0, keepdims=True)

            @pl.when(i == 0)
            def _():
                dgp_ref[...] = part

            @pl.when(i > 0)
            def _():
                dgp_ref[...] += part

    wide = pl.BlockSpec((tm, FH), lambda i, k: (i, k))
    row = pl.BlockSpec((tm, D), lambda i, k: (i, 0))
    vec = pl.BlockSpec((1, D), lambda i, k: (0, 0))
    return pl.pallas_call(
        body, name="ffn_bwd_dh", grid=(s // tm, 2),
        in_specs=[wide, wide, pl.BlockSpec((None, None, D, FH), lambda i, k: (l, k, 0, 0)),
                  pl.BlockSpec((None, None, D, FH), lambda i, k: (l, k + 2, 0, 0)), row, vec, row] + [ANY] * len(deps),
        out_specs=[row, vec],
        out_shape=[SDS((s, D), F32), SDS((1, D), F32)],
        compiler_params=_cp("arbitrary", "arbitrary"),
    )(dg, du, wgu, wgu, x, pre_g, dx1, *deps)


def _mm_tn_into(buf, a, b, l, joff, tka, tn, ts, name):
    s, ka = a.shape
    n = b.shape[1]

    def body(buf_ref, a_ref, b_ref, o_ref):
        p = lax.dot_general(a_ref[...], b_ref[...], TN, preferred_element_type=F32)

        @pl.when(pl.program_id(2) == 0)
        def _():
            o_ref[...] = p

        @pl.when(pl.program_id(2) > 0)
        def _():
            o_ref[...] += p

    return pl.pallas_call(
        body, name=name, grid=(ka // tka, n // tn, s // ts),
        in_specs=[pl.BlockSpec(memory_space=pl.ANY),
                  pl.BlockSpec((ts, tka), lambda ia, j, t: (t, ia)), pl.BlockSpec((ts, tn), lambda ia, j, t: (t, j))],
        out_specs=pl.BlockSpec((None, None, tka, tn), lambda ia, j, t: (l, joff + j, ia, 0)),
        out_shape=SDS(buf.shape, F32), input_output_aliases={0: 0},
        compiler_params=_cp("parallel", "parallel", "arbitrary"),
    )(buf, a, b)


def _proj(x, g, w_in, l, tm):
    s = x.shape[0]

    def body(x_ref, g_ref, w_ref, h_ref, p_ref):
        xf = x_ref[...]
        h = (xf * _rsq(xf, NORM_EPS) * g_ref[...]).astype(BF16)
        h_ref[...] = h
        p_ref[...] = jnp.dot(h, w_ref[...], preferred_element_type=F32)

    return pl.pallas_call(
        body, name="proj", grid=(s // tm,),
        in_specs=[pl.BlockSpec((tm, D), lambda i: (i, 0)), pl.BlockSpec((1, D), lambda i: (0, 0)),
                  pl.BlockSpec((None, D, P_IN), lambda i: (l, 0, 0))],
        out_specs=[pl.BlockSpec((tm, D), lambda i: (i, 0)), pl.BlockSpec((tm, P_IN), lambda i: (i, 0))],
        out_shape=[SDS((s, D), BF16), SDS((s, P_IN), F32)],
        compiler_params=_cp("parallel"),
    )(x, g, w_in)


def _mm_nt(a, w, l, tm, name):
    s, k_dim = a.shape
    n = w.shape[1]

    def body(a_ref, w_ref, o_ref):
        o_ref[...] = lax.dot_general(a_ref[...], w_ref[...], NT, preferred_element_type=F32)

    return pl.pallas_call(
        body, name=name, grid=(s // tm,),
        in_specs=[pl.BlockSpec((tm, k_dim), lambda i: (i, 0)), pl.BlockSpec((None, n, k_dim), lambda i: (l, 0, 0))],
        out_specs=pl.BlockSpec((tm, n), lambda i: (i, 0)),
        out_shape=SDS((s, n), F32), compiler_params=_cp("parallel"),
    )(a, w)


def _mm_nt_rmsbwd(dp, w_in, l, x, g, dx1, tm):
    s = x.shape[0]

    def body(dp_ref, w_ref, x_ref, g_ref, dx1_ref, dx_ref, dg_ref):
        dh = lax.dot_general(dp_ref[...], w_ref[...], NT, preferred_element_type=F32)
        dx, dgr = _rms_bwd_rows(x_ref[...], g_ref[...], dh)
        dx_ref[...] = dx1_ref[...] + dx
        part = jnp.sum(dgr, axis=0, keepdims=True)

        @pl.when(pl.program_id(0) == 0)
        def _():
            dg_ref[...] = part

        @pl.when(pl.program_id(0) > 0)
        def _():
            dg_ref[...] += part

    row = pl.BlockSpec((tm, D), lambda i: (i, 0))
    vec = pl.BlockSpec((1, D), lambda i: (0, 0))
    return pl.pallas_call(
        body, name="mix_bwd_dx", grid=(s // tm,),
        in_specs=[pl.BlockSpec((tm, P_IN), lambda i: (i, 0)), pl.BlockSpec((None, D, P_IN), lambda i: (l, 0, 0)), row, vec, row],
        out_specs=[row, vec], out_shape=[SDS((s, D), F32), SDS((1, D), F32)],
        compiler_params=_cp("arbitrary"),
    )(dp, w_in, x, g, dx1)


def _row_iota(shape):
    return lax.broadcasted_iota(jnp.int32, shape, 0)


def _lru_gates(xc, wa_ref, ba_ref, wx_ref, bx_ref, lam_ref):
    xb = xc.astype(BF16)
    r = _sig(jnp.dot(xb, wa_ref[...], preferred_element_type=F32) + ba_ref[...])
    ig = _sig(jnp.dot(xb, wx_ref[...], preferred_element_type=F32) + bx_ref[...])
    nl = -lam_ref[...]
    sp = jnp.maximum(nl, 0.0) + jnp.log(1.0 + jnp.exp(-jnp.abs(nl)))
    log_a = -LRU_C * r * sp
    a = jnp.exp(log_a)
    x2 = 2.0 * log_a
    series = x2 * (1.0 + x2 * (0.5 + x2 * (1.0 / 6.0 + x2 * (1.0 / 24.0 + x2 * (1.0 / 120.0)))))
    em1 = jnp.where(x2 > -0.05, series, jnp.exp(x2) - 1.0)
    mlt = jnp.sqrt(-em1)
    return r, ig, a, mlt, sp


def _conv_taps(src_ref, w_ref, k_taps, pad, tc):
    acc = None
    for j in range(k_taps):
        term = w_ref[j:j + 1, :] * src_ref[pl.ds(pad - (k_taps - 1) + j, tc), :]
        acc = term if acc is None else acc + term
    return acc


def _fill_shifted(src_ref, sh_ref):
    n = src_ref.shape[0] - 8
    for s in range(1, 8):
        sh_ref[s, 0:n, :] = src_ref[pl.ds(s, n), :]


def _shifted_rows(src_ref, sh_ref, offset, tc):
    if offset % 8 == 0:
        return src_ref[pl.ds(offset, tc), :]
    return sh_ref[offset % 8, pl.ds(offset - offset % 8, tc), :]


def _gelu_parts(x):
    c0 = math.sqrt(2.0 / math.pi)
    inner = c0 * (x + 0.044715 * x * x * x)
    t = jnp.tanh(inner)
    gl = 0.5 * x * (1.0 + t)
    dgl = 0.5 * (1.0 + t) + 0.5 * x * (1.0 - t * t) * c0 * (1.0 + 3.0 * 0.044715 * x * x)
    return gl, dgl


def _lru_fwd(proj, cw, cb, wa, ba, wx, bx, lam, gg, tc):
    s = proj.shape[0]
    pad = 8

    def body(xcur_ref, xprev_ref, gate_ref, cw_ref, cb_ref, wa_ref, ba_ref, wx_ref, bx_ref, lam_ref, gg_ref,
             yn_ref, h_ref, xs_ref, hc_ref):
        i = pl.program_id(0)

        @pl.when(i == 0)
        def _():
            hc_ref[...] = jnp.zeros_like(hc_ref)

        xs_ref[0:pad, :] = jnp.where(i > 0, xprev_ref[tc - pad:tc, :], 0.0)
        xs_ref[pad:pad + tc, :] = xcur_ref[...]
        xc = _conv_taps(xs_ref, cw_ref, LRU_K, pad, tc) + cb_ref[...]
        _, ig, a, mlt, _ = _lru_gates(xc, wa_ref, ba_ref, wx_ref, bx_ref, lam_ref)
        u = mlt * (ig * xc)
        row = _row_iota((tc, W_A))
        d = 1
        while d < tc:
            ok = row >= d
            a_sh = jnp.where(ok, pltpu.roll(a, d, axis=0), 1.0)
            u_sh = jnp.where(ok, pltpu.roll(u, d, axis=0), 0.0)
            u = a * u_sh + u
            a = a * a_sh
            d *= 2
        h = u + a * hc_ref[...]
        hc_ref[...] = jnp.sum(jnp.where(row == tc - 1, h, 0.0), axis=0, keepdims=True)
        h_ref[...] = h
        gl, _ = _gelu_parts(gate_ref[...])
        ya = gl * h
        yn_ref[...] = (ya * _rsq(ya, NORM_EPS) * gg_ref[...]).astype(BF16)

    blk = lambda c: pl.BlockSpec((tc, W_A), lambda i, c=c: (i, c))
    full = lambda a: pl.BlockSpec(a.shape, lambda i: (0,) * a.ndim)
    params = [cw, cb, wa, ba, wx, bx, lam, gg]
    return pl.pallas_call(
        body, name="lru_fwd", grid=(s // tc,),
        in_specs=[blk(0), pl.BlockSpec((tc, W_A), lambda i: (jnp.maximum(i - 1, 0), 0)), blk(1)] + [full(a) for a in params],
        out_specs=[pl.BlockSpec((tc, W_A), lambda i: (i, 0))] * 2,
        out_shape=[SDS((s, W_A), BF16), SDS((s, W_A), F32)],
        scratch_shapes=[pltpu.VMEM((tc + pad, W_A), F32), pltpu.VMEM((1, W_A), F32)],
        compiler_params=_cp("arbitrary"),
    )(proj, proj, proj, *params)


def _acc(ref, first, val):
    @pl.when(first)
    def _():
        ref[...] = val

    @pl.when(jnp.logical_not(first))
    def _():
        ref[...] += val


def _lru_bwd(dy, proj, h, cw, cb, wa, ba, wx, bx, lam, gg, tc):
    s = proj.shape[0]
    nc = s // tc
    pad = 8

    def body(dy_ref, xcur_ref, xprev_ref, gate_ref, h_ref, hprev_ref, cw_ref, cb_ref, wa_ref, ba_ref, wx_ref, bx_ref,
             lam_ref, gg_ref,
             dp_ref, dcw_ref, dcb_ref, dwa_ref, dba_ref, dwx_ref, dbx_ref, dlam_ref, dgg_ref,
             xs_ref, ds_ref, mu_ref, nx_ref):
        step = pl.program_id(0)
        i = nc - 1 - step
        first = step == 0

        @pl.when(first)
        def _():
            mu_ref[...] = jnp.zeros_like(mu_ref)
            nx_ref[...] = jnp.zeros_like(nx_ref)

        xs_ref[0:pad, :] = jnp.where(i > 0, xprev_ref[tc - pad:tc, :], 0.0)
        xs_ref[pad:pad + tc, :] = xcur_ref[...]
        xc = _conv_taps(xs_ref, cw_ref, LRU_K, pad, tc) + cb_ref[...]
        r, ig, a, mlt, sp = _lru_gates(xc, wa_ref, ba_ref, wx_ref, bx_ref, lam_ref)
        hh = h_ref[...]
        gate = gate_ref[...]
        gl, dgl = _gelu_parts(gate)
        ya = gl * hh
        dya, dggr = _rms_bwd_rows(ya, gg_ref[...], dy_ref[...])
        _acc(dgg_ref, first, jnp.sum(dggr, axis=0, keepdims=True))
        dp_ref[:, W_A:2 * W_A] = dya * hh * dgl
        dh = dya * gl

        row = _row_iota((tc, W_A))
        aa = a
        uu = a * dh
        d = 1
        while d < tc:
            ok = row < tc - d
            a_sh = jnp.where(ok, pltpu.roll(aa, tc - d, axis=0), 1.0)
            u_sh = jnp.where(ok, pltpu.roll(uu, tc - d, axis=0), 0.0)
            uu = uu + aa * u_sh
            aa = aa * a_sh
            d *= 2
        cin = mu_ref[...]
        mu = uu + aa * cin
        lam_t = dh + jnp.where(row == tc - 1, cin, pltpu.roll(mu, tc - 1, axis=0))
        mu_ref[...] = jnp.sum(jnp.where(row == 0, mu, 0.0), axis=0, keepdims=True)
        hm1 = jnp.where(row == 0, jnp.where(i > 0, pltpu.roll(hprev_ref[...], 1, axis=0), 0.0),
                        pltpu.roll(hh, 1, axis=0))
        da = lam_t * hm1
        du = lam_t
        dmlt = du * ig * xc
        dig = du * mlt * xc
        dxc = du * mlt * ig
        dlog_a = da * a - dmlt * (a * a / mlt)
        dr = dlog_a * (-LRU_C * sp)
        dsp = jnp.sum(dlog_a * (-LRU_C * r), axis=0, keepdims=True)
        _acc(dlam_ref, first, dsp * (-_sig(-lam_ref[...])))
        dga = dr * r * (1.0 - r)
        dgx = dig * ig * (1.0 - ig)
        _acc(dba_ref, first, jnp.sum(dga, axis=0, keepdims=True))
        _acc(dbx_ref, first, jnp.sum(dgx, axis=0, keepdims=True))
        xb = xc.astype(BF16)
        dgab = dga.astype(BF16)
        dgxb = dgx.astype(BF16)
        _acc(dwa_ref, first, lax.dot_general(xb, dgab, TN, preferred_element_type=F32))
        _acc(dwx_ref, first, lax.dot_general(xb, dgxb, TN, preferred_element_type=F32))
        dxc = (dxc + lax.dot_general(dgab, wa_ref[...], NT, preferred_element_type=F32)
               + lax.dot_general(dgxb, wx_ref[...], NT, preferred_element_type=F32))

        _acc(dcb_ref, first, jnp.sum(dxc, axis=0, keepdims=True))
        r8 = _row_iota((8, W_A))
        dcw = jnp.zeros((8, W_A), F32)
        for j in range(LRU_K):
            tap = jnp.sum(dxc * xs_ref[pl.ds(pad - (LRU_K - 1) + j, tc), :], axis=0, keepdims=True)
            dcw = dcw + jnp.where(r8 == j, tap, 0.0)
        _acc(dcw_ref, first, dcw)
        ds_ref[0:tc, :] = dxc
        ds_ref[tc:tc + pad, :] = nx_ref[...]
        dlx = None
        for j in range(LRU_K):
            term = cw_ref[j:j + 1, :] * ds_ref[pl.ds(LRU_K - 1 - j, tc), :]
            dlx = term if dlx is None else dlx + term
        dp_ref[:, 0:W_A] = dlx
        nx_ref[...] = dxc[0:pad, :]

    rev = lambda c: pl.BlockSpec((tc, W_A), lambda t, c=c: (nc - 1 - t, c))
    prev = lambda c: pl.BlockSpec((tc, W_A), lambda t, c=c: (jnp.maximum(nc - 2 - t, 0), c))
    full = lambda a: pl.BlockSpec(a.shape, lambda t: (0,) * a.ndim)
    params = [cw, cb, wa, ba, wx, bx, lam, gg]
    vec = SDS((1, W_A), F32)
    sq = SDS((W_A, W_A), F32)
    outs = [SDS((s, 2 * W_A), F32), SDS((8, W_A), F32), vec, sq, vec, sq, vec, vec, vec]
    return pl.pallas_call(
        body, name="lru_bwd", grid=(nc,),
        in_specs=[rev(0), rev(0), prev(0), rev(1), rev(0), prev(0)] + [full(a) for a in params],
        out_specs=[pl.BlockSpec((tc, 2 * W_A), lambda t: (nc - 1 - t, 0))]
        + [pl.BlockSpec(o.shape, lambda t: (0, 0)) for o in outs[1:]],
        out_shape=outs,
        scratch_shapes=[pltpu.VMEM((tc + pad, W_A), F32), pltpu.VMEM((tc + pad, W_A), F32),
                        pltpu.VMEM((1, W_A), F32), pltpu.VMEM((pad, W_A), F32)],
        compiler_params=_cp("arbitrary"),
    )(dy, proj, proj, proj, h, h, *params)


def _attn_stack(qa, qb, kvh):
    lane = lax.broadcasted_iota(jnp.int32, qa.shape, 1)
    keep = (lane >= HD) if kvh == 1 else (lane < HD)
    parts = []
    for tile in (qa, qb):
        for half in (0, 1):
            y = tile if half == kvh else pltpu.roll(tile, HD, axis=1)
            parts.append(jnp.where(keep, y, 0.0))
    return jnp.concatenate(parts, axis=0)


def _attn_unstack(o, kvh):
    lane = lax.broadcasted_iota(jnp.int32, (BLK, 2 * HD), 1)
    tiles = []
    for t in range(2):
        halves = []
        for half in (0, 1):
            blk = o[(2 * t + half) * BLK:(2 * t + half + 1) * BLK, :]
            halves.append(blk if half == kvh else pltpu.roll(blk, HD, axis=1))
        tiles.append(jnp.where(lane < HD, halves[0], halves[1]))
    return tiles


def _attn_stack_all(x_ref_or_val):
    return jnp.concatenate([_attn_stack(x_ref_or_val[:, 256 * kvh:256 * kvh + 128],
                                        x_ref_or_val[:, 256 * kvh + 128:256 * kvh + 256], kvh) for kvh in range(2)], axis=0)


def _attn_unstack_all(o, dst_ref):
    for kvh in range(2):
        ta, tb = _attn_unstack(o[4 * BLK * kvh:4 * BLK * (kvh + 1), :], kvh)
        dst_ref[:, 256 * kvh:256 * kvh + 128] = ta
        dst_ref[:, 256 * kvh + 128:256 * kvh + 256] = tb


def _attn_windows(cur_ref, prev_ref, nb):
    blocks = [prev_ref[...]] + [cur_ref[b * BLK:(b + 1) * BLK, :] for b in range(nb)]
    return [jnp.concatenate(blocks[b:b + 2], axis=0).astype(BF16) for b in range(nb)]


def _attn_probs(qs, kw, n, sink_ref):
    rows = NQ * BLK
    sc = lax.dot_general(qs.astype(BF16), kw, NT, preferred_element_type=F32) * SCALE
    qi = lax.broadcasted_iota(jnp.int32, (rows, 2 * BLK), 0) & (BLK - 1)
    kj = lax.broadcasted_iota(jnp.int32, (rows, 2 * BLK), 1)
    rel = BLK + qi - kj
    mask = (rel >= 0) & (rel < BLK) & ((n - 1) * BLK + kj >= 0)
    head = lax.broadcasted_iota(jnp.int32, (rows, 1), 0) // BLK
    sk = jnp.zeros((rows, 1), F32)
    for h in range(NQ):
        sk = jnp.where(head == h, sink_ref[h:h + 1, 0:1], sk)
    sh = jnp.where(mask, sc, NEG_BIG)
    m = jnp.maximum(jnp.max(sh, axis=-1, keepdims=True), sk)
    e = jnp.exp(sh - m)
    es = jnp.exp(sk - m)
    rz = 1.0 / (jnp.sum(e, axis=-1, keepdims=True) + es)
    return e * rz, es * rz


def _attn_fwd(proj, sinks8, gg):
    s = proj.shape[0]
    nb = ATT_NB_FWD

    def body(q_ref, kc_ref, kp_ref, vc_ref, vp_ref, sink_ref, gg_ref, yn_ref, ob_ref):
        kws, vws = _attn_windows(kc_ref, kp_ref, nb), _attn_windows(vc_ref, vp_ref, nb)
        for b in range(nb):
            rows = pl.ds(b * BLK, BLK)
            p, _ = _attn_probs(_attn_stack_all(q_ref.at[rows, :]), kws[b], nb * pl.program_id(0) + b, sink_ref)
            _attn_unstack_all(jnp.dot(p.astype(BF16), vws[b], preferred_element_type=F32), ob_ref.at[rows, :])
        ob = ob_ref[...]
        yn_ref[...] = (ob * _rsq(ob, NORM_EPS) * gg_ref[...]).astype(BF16)

    tb = nb * BLK
    cur = lambda c: pl.BlockSpec((tb, 128), lambda m, c=c: (m, c))
    prev = lambda c: pl.BlockSpec((BLK, 128), lambda m, c=c: (jnp.maximum(nb * m - 1, 0), c))
    out = pl.BlockSpec((tb, W_B), lambda m: (m, 0))
    return pl.pallas_call(
        body, name="attn_fwd", grid=(s // tb,),
        in_specs=[pl.BlockSpec((tb, W_B), lambda m: (m, 1)), cur(8), prev(8), cur(9), prev(9),
                  pl.BlockSpec((8, 128), lambda n: (0, 0)), pl.BlockSpec((1, W_B), lambda n: (0, 0))],
        out_specs=[out, out], out_shape=[SDS((s, W_B), BF16), SDS((s, W_B), F32)],
        compiler_params=_cp("parallel"),
    )(proj, proj, proj, proj, proj, sinks8, gg)


def _attn_bwd(dy, proj, ob, sinks8, gg):
    s = proj.shape[0]
    nb = ATT_NB_BWD

    def body(dya_ref, dyb_ref, q_ref, kc_ref, kp_ref, vc_ref, vp_ref, ob_ref, sink_ref, gg_ref,
             dq_ref, dcur_ref, dprev_ref, dsink_ref, dgg_ref):
        first = pl.program_id(0) == 0
        kws, vws = _attn_windows(kc_ref, kp_ref, nb), _attn_windows(vc_ref, vp_ref, nb)
        dyn = jnp.concatenate([dya_ref[...], dyb_ref[...]], axis=1)
        dob, dggr = _rms_bwd_rows(ob_ref[...], gg_ref[...], dyn)
        _acc(dgg_ref, first, jnp.sum(dggr, axis=0, keepdims=True))
        r8 = _row_iota((8, 128))
        dsk = jnp.zeros((8, 128), F32)
        for b in range(nb):
            rows = pl.ds(b * BLK, BLK)
            qs = _attn_stack_all(q_ref.at[rows, :])
            p, psink = _attn_probs(qs, kws[b], nb * pl.program_id(0) + b, sink_ref)
            dosb = _attn_stack_all(dob[b * BLK:(b + 1) * BLK, :]).astype(BF16)
            dp = lax.dot_general(dosb, vws[b], NT, preferred_element_type=F32)
            dd = jnp.sum(p * dp, axis=-1, keepdims=True)
            dsb = (p * (dp - dd) * SCALE).astype(BF16)
            dsink_rows = -psink * dd
            for h in range(NQ):
                dsk = dsk + jnp.where(r8 == h, jnp.sum(dsink_rows[h * BLK:(h + 1) * BLK, :], axis=0, keepdims=True), 0.0)
            _attn_unstack_all(jnp.dot(dsb, kws[b], preferred_element_type=F32), dq_ref.at[rows, :])
            dkw = lax.dot_general(dsb, qs.astype(BF16), TN, preferred_element_type=F32)
            dvw = lax.dot_general(p.astype(BF16), dosb, TN, preferred_element_type=F32)
            dprev_ref[rows, 0:128] = dkw[0:BLK, :]
            dprev_ref[rows, 128:256] = dvw[0:BLK, :]
            dcur_ref[rows, 0:128] = dkw[BLK:2 * BLK, :]
            dcur_ref[rows, 128:256] = dvw[BLK:2 * BLK, :]
        _acc(dsink_ref, first, dsk)

    tb = nb * BLK
    cur = lambda c: pl.BlockSpec((tb, 128), lambda m, c=c: (m, c))
    prev = lambda c: pl.BlockSpec((BLK, 128), lambda m, c=c: (jnp.maximum(nb * m - 1, 0), c))
    wide = pl.BlockSpec((tb, W_B), lambda m: (m, 0))
    half = pl.BlockSpec((tb, 256), lambda m: (m, 0))
    return pl.pallas_call(
        body, name="attn_bwd", grid=(s // tb,),
        in_specs=[pl.BlockSpec((tb, 256), lambda m: (m, 1)), pl.BlockSpec((tb, 256), lambda m: (m, 2)),
                  pl.BlockSpec((tb, W_B), lambda m: (m, 1)), cur(8), prev(8), cur(9), prev(9), wide,
                  pl.BlockSpec((8, 128), lambda n: (0, 0)), pl.BlockSpec((1, W_B), lambda n: (0, 0))],
        out_specs=[wide, half, half, pl.BlockSpec((8, 128), lambda n: (0, 0)), pl.BlockSpec((1, W_B), lambda n: (0, 0))],
        out_shape=[SDS((s, W_B), F32), SDS((s, 256), F32), SDS((s, 256), F32), SDS((8, 128), F32), SDS((1, W_B), F32)],
        compiler_params=_cp("arbitrary"),
    )(dy, dy, proj, proj, proj, proj, proj, ob, sinks8, gg)


def _ln_parts(y1, eps=LN_EPS):
    mu = jnp.mean(y1, axis=-1, keepdims=True)
    xc = y1 - mu
    rstd = lax.rsqrt(jnp.mean(xc * xc, axis=-1, keepdims=True) + eps)
    return xc * rstd, rstd


def _conf_fwd(proj, cw, cb, lg, lb, gg, tc):
    s = proj.shape[0]
    pad = 32

    def body(ac_ref, gc_ref, ap_ref, gp_ref, cw_ref, cb_ref, lg_ref, lb_ref, gg_ref, yn_ref, y1_ref, ys_ref, sh_ref):
        i = pl.program_id(0)
        tail = ap_ref[tc - pad:tc, :] * _sig(gp_ref[tc - pad:tc, :])
        ys_ref[0:pad, :] = jnp.where(i > 0, tail, 0.0)
        ys_ref[pad:pad + tc, :] = ac_ref[...] * _sig(gc_ref[...])
        _fill_shifted(ys_ref, sh_ref)
        y1 = cb_ref[...]
        for j in range(CONV_K):
            y1 = y1 + cw_ref[j:j + 1, :] * _shifted_rows(ys_ref, sh_ref, pad - (CONV_K - 1) + j, tc)
        y1_ref[...] = y1
        xh, _ = _ln_parts(y1)
        yl = xh * lg_ref[...] + lb_ref[...]
        yc = yl * _sig(yl)
        yn_ref[...] = (yc * _rsq(yc, NORM_EPS) * gg_ref[...]).astype(BF16)

    cur = lambda c: pl.BlockSpec((tc, W_C), lambda i, c=c: (i, c))
    prev = lambda c: pl.BlockSpec((tc, W_C), lambda i, c=c: (jnp.maximum(i - 1, 0), c))
    full = lambda a: pl.BlockSpec(a.shape, lambda i: (0,) * a.ndim)
    params = [cw, cb, lg, lb, gg]
    out = pl.BlockSpec((tc, W_C), lambda i: (i, 0))
    return pl.pallas_call(
        body, name="conf_fwd", grid=(s // tc,),
        in_specs=[cur(5), cur(6), prev(5), prev(6)] + [full(a) for a in params],
        out_specs=[out, out], out_shape=[SDS((s, W_C), BF16), SDS((s, W_C), F32)],
        scratch_shapes=[pltpu.VMEM((tc + pad, W_C), F32), pltpu.VMEM((8, tc + pad, W_C), F32)],
        compiler_params=_cp("parallel"),
    )(proj, proj, proj, proj, *params)


def _conf_bwd(dy, proj, y1, cw, cb, lg, lb, gg, tc):
    s = proj.shape[0]
    nc = s // tc
    pad = 32

    def body(dy_ref, ac_ref, gc_ref, ap_ref, gp_ref, y1_ref, cw_ref, cb_ref, lg_ref, lb_ref, gg_ref,
             dp_ref, dcw_ref, dcb_ref, dlg_ref, dlb_ref, dgg_ref, ys_ref, ds_ref, nx_ref, ysh_ref, dsh_ref):
        step = pl.program_id(0)
        i = nc - 1 - step
        first = step == 0

        @pl.when(first)
        def _():
            nx_ref[...] = jnp.zeros_like(nx_ref)

        a = ac_ref[...]
        sg = _sig(gc_ref[...])
        tail = ap_ref[tc - pad:tc, :] * _sig(gp_ref[tc - pad:tc, :])
        ys_ref[0:pad, :] = jnp.where(i > 0, tail, 0.0)
        ys_ref[pad:pad + tc, :] = a * sg
        xh, rstd = _ln_parts(y1_ref[...])
        yl = xh * lg_ref[...] + lb_ref[...]
        sl = _sig(yl)
        yc = yl * sl
        dyc, dggr = _rms_bwd_rows(yc, gg_ref[...], dy_ref[...])
        _acc(dgg_ref, first, jnp.sum(dggr, axis=0, keepdims=True))
        dyl = dyc * sl * (1.0 + yl * (1.0 - sl))
        _acc(dlg_ref, first, jnp.sum(dyl * xh, axis=0, keepdims=True))
        _acc(dlb_ref, first, jnp.sum(dyl, axis=0, keepdims=True))
        dxh = dyl * lg_ref[...]
        dy1 = rstd * (dxh - jnp.mean(dxh, axis=-1, keepdims=True) - xh * jnp.mean(dxh * xh, axis=-1, keepdims=True))
        _acc(dcb_ref, first, jnp.sum(dy1, axis=0, keepdims=True))
        r32 = _row_iota((32, W_C))
        dcw = jnp.zeros((32, W_C), F32)
        _fill_shifted(ys_ref, ysh_ref)
        for j in range(CONV_K):
            tap = jnp.sum(dy1 * _shifted_rows(ys_ref, ysh_ref, pad - (CONV_K - 1) + j, tc), axis=0, keepdims=True)
            dcw = dcw + jnp.where(r32 == j, tap, 0.0)
        _acc(dcw_ref, first, dcw)
        ds_ref[0:tc, :] = dy1
        ds_ref[tc:tc + pad, :] = nx_ref[...]
        _fill_shifted(ds_ref, dsh_ref)
        dy0 = None
        for j in range(CONV_K):
            term = cw_ref[j:j + 1, :] * _shifted_rows(ds_ref, dsh_ref, CONV_K - 1 - j, tc)
            dy0 = term if dy0 is None else dy0 + term
        dp_ref[:, 0:W_C] = dy0 * sg
        dp_ref[:, W_C:2 * W_C] = dy0 * a * sg * (1.0 - sg)
        nx_ref[...] = dy1[0:pad, :]

    rev = lambda c: pl.BlockSpec((tc, W_C), lambda t, c=c: (nc - 1 - t, c))
    prev = lambda c: pl.BlockSpec((tc, W_C), lambda t, c=c: (jnp.maximum(nc - 2 - t, 0), c))
    full = lambda a: pl.BlockSpec(a.shape, lambda t: (0,) * a.ndim)
    params = [cw, cb, lg, lb, gg]
    vec = SDS((1, W_C), F32)
    outs = [SDS((s, 2 * W_C), F32), SDS((32, W_C), F32), vec, vec, vec, vec]
    return pl.pallas_call(
        body, name="conf_bwd", grid=(nc,),
        in_specs=[rev(3), rev(5), rev(6), prev(5), prev(6), rev(0)] + [full(a) for a in params],
        out_specs=[pl.BlockSpec((tc, 2 * W_C), lambda t: (nc - 1 - t, 0))]
        + [pl.BlockSpec(o.shape, lambda t: (0, 0)) for o in outs[1:]],
        out_shape=outs,
        scratch_shapes=[pltpu.VMEM((tc + pad, W_C), F32), pltpu.VMEM((tc + pad, W_C), F32), pltpu.VMEM((pad, W_C), F32),
                        pltpu.VMEM((8, tc + pad, W_C), F32), pltpu.VMEM((8, tc + pad, W_C), F32)],
        compiler_params=_cp("arbitrary"),
    )(dy, proj, proj, proj, proj, y1, *params)


def _assemble_dproj(dlru, dq, dcur, dprev, dconf):
    s = dq.shape[0]
    nb = s // BLK

    def body(dl_ref, dq_ref, dc_ref, dn_ref, df_ref, o_ref):
        n = pl.program_id(0)
        o_ref[:, 0:512] = dl_ref[...].astype(BF16)
        o_ref[:, 512:1024] = dq_ref[...].astype(BF16)
        o_ref[:, 1024:1280] = (dc_ref[...] + jnp.where(n < nb - 1, dn_ref[...], 0.0)).astype(BF16)
        o_ref[:, 1280:1792] = df_ref[...].astype(BF16)

    wide = pl.BlockSpec((BLK, 512), lambda n: (n, 0))
    return pl.pallas_call(
        body, name="assemble_dproj", grid=(nb,),
        in_specs=[wide, wide, pl.BlockSpec((BLK, 256), lambda n: (n, 0)),
                  pl.BlockSpec((BLK, 256), lambda n: (jnp.minimum(n + 1, nb - 1), 0)), wide],
        out_specs=pl.BlockSpec((BLK, P_IN), lambda n: (n, 0)), out_shape=SDS((s, P_IN), BF16),
        compiler_params=_cp("parallel"),
    )(dlru, dq, dcur, dprev, dconf)


def _loss_grad(y, t, tm):
    s = y.shape[0]

    def body(y_ref, t_ref, dy_ref, l_ref):
        err = y_ref[...] - t_ref[...]
        dy_ref[...] = err * (1.0 / D)
        _acc(l_ref, pl.program_id(0) == 0, jnp.sum(err * err, axis=0, keepdims=True))

    row = pl.BlockSpec((tm, D), lambda i: (i, 0))
    return pl.pallas_call(
        body, name="loss_grad", grid=(s // tm,), in_specs=[row, row],
        out_specs=[row, pl.BlockSpec((1, D), lambda i: (0, 0))],
        out_shape=[SDS((s, D), F32), SDS((1, D), F32)], compiler_params=_cp("arbitrary"),
    )(y, t)


def _block_diag(w):
    rows = [jnp.concatenate([w[h] if k == h else jnp.zeros((64, 64), w.dtype) for k in range(4)], axis=1) for h in range(4)]
    return jnp.concatenate(rows, axis=0)


def _diag_blocks(m):
    return jnp.stack([m[64 * h:64 * (h + 1), 64 * h:64 * (h + 1)] for h in range(4)])


def _layer_params(small, l):
    v = lambda name: small[name][l].reshape(1, -1)
    gg = small["group_g"][l]
    return dict(
        ffn1_pre=v("ffn1_pre_g"), ffn1_post=v("ffn1_post_g"), mix_pre=v("mix_pre_g"), mix_post=v("mix_post_g"),
        ffn2_pre=v("ffn2_pre_g"), ffn2_post=v("ffn2_post_g"), lru_cb=v("lru_conv_b"),
        wa=_block_diag(small["lru_w_a"][l]).astype(BF16), ba=v("lru_b_a"),
        wx=_block_diag(small["lru_w_x"][l]).astype(BF16), bx=v("lru_b_x"), lam=v("lru_lambda"),
        sinks8=jnp.broadcast_to(small["attn_sinks"][l][:, None], (NQ, 128)),
        conv_b=v("conv_b"), ln_g=v("conv_ln_g"), ln_b=v("conv_ln_b"),
        gg_a=gg[0:W_A].reshape(1, -1), gg_b=gg[W_A:W_A + W_B].reshape(1, -1), gg_c=gg[W_A + W_B:].reshape(1, -1),
    )


def _forward_layer(x, weights, p, tiles, deps=()):
    _, mm, _, tc = tiles
    big = dict(weights("ffn1_gu", x))
    p = dict(p)
    sv = dict(x0=x)
    h1, g1, u1, a1 = _ffn_up(x, p["ffn1_pre"], big["ffn1_w_gu"], 0, mm, deps)
    big.update(weights("ffn1_down", a1))
    z1, x = _mm_rms_res(a1, big["ffn1_w_down"], 0, x, p["ffn1_post"], 0.5, mm, FH, "ffn_down")
    sv.update(h1=h1, g1=g1, u1=u1, a1=a1, z1=z1, x1=x)
    big.update(weights("mix", x))
    p.update(lru_cw=big.pop("lru_conv_w"), conv_w=big.pop("conv_w"))
    hn, proj = _proj(x, p["mix_pre"], big["w_in"], 0, mm)
    yn_a, hl = _lru_fwd(proj, p["lru_cw"], p["lru_cb"], p["wa"], p["ba"], p["wx"], p["bx"], p["lam"], p["gg_a"], tc)
    yn_b, ob = _attn_fwd(proj, p["sinks8"], p["gg_b"])
    yn_c, y1 = _conf_fwd(proj, p["conv_w"], p["conv_b"], p["ln_g"], p["ln_b"], p["gg_c"], tc)
    ycat = jnp.concatenate([yn_a, yn_b, yn_c], axis=1)
    zo, x = _mm_rms_res(ycat, big["w_out"], 0, x, p["mix_post"], 1.0, mm, D, "mix_out")
    sv.update(hn=hn, proj=proj, hl=hl, ob=ob, y1=y1, ycat=ycat, zo=zo, x2=x)
    big.update(weights("ffn2", x))
    h2, g2, u2, a2 = _ffn_up(x, p["ffn2_pre"], big["ffn2_w_gu"], 0, mm)
    z2, x = _mm_rms_res(a2, big["ffn2_w_down"], 0, x, p["ffn2_post"], 0.5, mm, FH, "ffn_down")
    sv.update(h2=h2, g2=g2, u2=u2, a2=a2, z2=z2, p=p, big=big)
    return x, sv


def _grad_buffers():
    empty = lambda *shape: lax.empty(shape, F32)
    return dict(ffn1_w_gu=empty(1, NSHARD, D, FH), ffn2_w_gu=empty(1, NSHARD, D, FH), ffn1_w_down=empty(1, 1, DFF, D),
                ffn2_w_down=empty(1, 1, DFF, D), w_in=empty(1, 1, D, P_IN), w_out=empty(1, 1, D, D))


def _backward_layer(dx, sv, bufs, tiles, stage):
    p, big = sv["p"], sv["big"]
    tm, mm, dw, tc = tiles
    gr = {}

    def ffn_bwd(dx, which, xin, h, g, u, a, z, pre, post, deps):
        dz, dpost = _rms_bwd(dx, z, post, 0.5, tm, "ffn_post_bwd", deps)
        dg, du = _ffn_bwd_mid(dz, big[which + "_w_down"], 0, g, u, mm)
        bufs[which + "_w_down"] = _mm_tn_into(bufs[which + "_w_down"], a, dz, 0, 0, FH, D, dw, "dw_down")
        bufs[which + "_w_gu"] = _mm_tn_into(bufs[which + "_w_gu"], h, dg, 0, 0, D, FH, dw, "dw_gate")
        bufs[which + "_w_gu"] = _mm_tn_into(bufs[which + "_w_gu"], h, du, 0, 2, D, FH, dw, "dw_up")
        deps = stage({n: bufs[n] for n in (which + "_w_gu", which + "_w_down")}, bufs[which + "_w_gu"])
        dxn, dpre = _ffn_bwd_dh(dg, du, big[which + "_w_gu"], 0, xin, pre, dx, mm, deps)
        return dxn, dpre, dpost

    dx, gr["ffn2_pre_g"], gr["ffn2_post_g"] = ffn_bwd(dx, "ffn2", sv["x2"], sv["h2"], sv["g2"], sv["u2"], sv["a2"],
                                                      sv["z2"], p["ffn2_pre"], p["ffn2_post"], ())
    do, gr["mix_post_g"] = _rms_bwd(dx, sv["zo"], p["mix_post"], 1.0, tm, "mix_post_bwd")
    bufs["w_out"] = _mm_tn_into(bufs["w_out"], sv["ycat"], do, 0, 0, D, D, dw, "dw_out")
    dy = _mm_nt(do, big["w_out"], 0, mm, "mix_dy")
    proj = sv["proj"]
    (dlru, dcw, gr["lru_conv_b"], dwa, gr["lru_b_a"], dwx, gr["lru_b_x"], gr["lru_lambda"], dgg_a) = _lru_bwd(
        dy, proj, sv["hl"], p["lru_cw"], p["lru_cb"], p["wa"], p["ba"], p["wx"], p["bx"], p["lam"], p["gg_a"], tc)
    dq, dcur, dprev, dsk, dgg_b = _attn_bwd(dy, proj, sv["ob"], p["sinks8"], p["gg_b"])
    dconf, dconvw, gr["conv_b"], gr["conv_ln_g"], gr["conv_ln_b"], dgg_c = _conf_bwd(
        dy, proj, sv["y1"], p["conv_w"], p["conv_b"], p["ln_g"], p["ln_b"], p["gg_c"], tc)
    dproj = _assemble_dproj(dlru, dq, dcur, dprev, dconf)
    bufs["w_in"] = _mm_tn_into(bufs["w_in"], sv["hn"], dproj, 0, 0, D, P_IN, dw, "dw_in")
    dx, gr["mix_pre_g"] = _mm_nt_rmsbwd(dproj, big["w_in"], 0, sv["x1"], p["mix_pre"], dx, mm)
    gr["lru_conv_w"] = dcw[0:LRU_K]
    gr["lru_w_a"] = _diag_blocks(dwa)
    gr["lru_w_x"] = _diag_blocks(dwx)
    gr["attn_sinks"] = dsk[:, 0]
    gr["conv_w"] = dconvw[0:CONV_K]
    gr["group_g"] = jnp.concatenate([dgg_a, dgg_b, dgg_c], axis=1)
    dx, gr["ffn1_pre_g"], gr["ffn1_post_g"] = ffn_bwd(dx, "ffn1", sv["x0"], sv["h1"], sv["g1"], sv["u1"], sv["a1"],
                                                      sv["z1"], p["ffn1_pre"], p["ffn1_post"],
                                                      stage({n: bufs[n] for n in ("w_in", "w_out")}, dx))
    return dx, gr


def _tiles(s):
    return min(1024, s), min(1024, s), min(2048, s), min(512, s // 2)


HBM_SPEC = pl.BlockSpec(memory_space=pltpu.HBM)
SEM_SPEC = pl.BlockSpec(memory_space=pltpu.SEMAPHORE)
EFFECT = pltpu.SideEffectType.DATAFLOW_SIDE_EFFECTING


def _place():
    x, y, c = lax.axis_index("x"), lax.axis_index("y"), lax.axis_index("c")
    return x, y, c, [(1 - x, y), (x, 1 - y), (1 - x, 1 - y)]


def _rcopy(src, dst, send_sems, recv_sems, k, to):
    return pltpu.make_async_remote_copy(src_ref=src, dst_ref=dst, send_sem=send_sems.at[k], recv_sem=recv_sems.at[k],
                                        device_id=to, device_id_type=MESH)


def _half(rows, which):
    return pl.ds(which * (rows // 2), rows // 2)


def _place_shard(w, l, p_idx, dtype):
    _, rows, cols = w.shape
    tr = _rows_per_block(rows, cols, 16, SUM_BLOCK_ELEMS) if rows % 16 == 0 else rows

    def body(p_ref, buf_ref, w_ref, o_ref):
        o_ref[...] = w_ref[...].astype(dtype)

    spec = pltpu.PrefetchScalarGridSpec(
        num_scalar_prefetch=1, grid=(rows // tr,),
        in_specs=[ANY, pl.BlockSpec((None, tr, cols), lambda i, pr: (l, i, 0))],
        out_specs=pl.BlockSpec((None, None, tr, cols), lambda i, pr: (0, pr[0], i, 0)))
    shape = (1, NSHARD, rows, cols)
    return pl.pallas_call(body, name="place_shard", grid_spec=spec, out_shape=SDS(shape, dtype),
                          input_output_aliases={1: 0}, compiler_params=_cp("parallel"),
                          )(p_idx, lax.empty(shape, dtype), w)


def _gather_two_level(bufs, n_halved):
    n = len(bufs)

    def body(*refs):
        outs = refs[n:2 * n]
        send_sems, recv_sems = refs[2 * n:]
        x, y, c, chips = _place()
        p = 2 * x + y
        me, sibling = (x, y, c), (x, y, 1 - c)

        def blk(a, q, half):
            return outs[a].at[0, q, _half(outs[a].shape[2], half)] if a < n_halved else outs[a].at[0, q]

        def cp(a, k, q, half, to):
            return _rcopy(blk(a, q, half), blk(a, q, half), send_sems, recv_sems, 6 * a + k, to)

        first = [cp(a, j, p, c, (*chip, c)) for a in range(n) for j, chip in enumerate(chips)]
        for d in first:
            d.start()
        passed = []
        for a in range(n):
            for j, chip in enumerate(chips):
                q = 2 * chip[0] + chip[1]
                cp(a, j, q, c, me).wait_recv()
                if a < n_halved:
                    passed.append(cp(a, 3 + j, q, c, sibling))
                    passed[-1].start()
        for a in range(n_halved):
            for j, chip in enumerate(chips):
                cp(a, 3 + j, 2 * chip[0] + chip[1], 1 - c, me).wait_recv()
        for d in first + passed:
            d.wait_send()

    return pl.pallas_call(
        body, name="gather_layer0", in_specs=[ANY] * n, out_specs=[ANY] * n,
        out_shape=[SDS(b.shape, b.dtype) for b in bufs], input_output_aliases={a: a for a in range(n)},
        scratch_shapes=[pltpu.SemaphoreType.DMA((6 * n,)), pltpu.SemaphoreType.DMA((6 * n,))],
    )(*bufs)


def _run_plans(plans, refs, send_sems, recv_sems):
    cps, b0, s0 = [], 0, 0
    for plan, nb, ns in plans:
        cps += plan(refs[b0:b0 + nb], send_sems, recv_sems, s0)
        b0, s0 = b0 + nb, s0 + ns
    return cps


def _exchange(name, bufs, plans):
    n = len(bufs)
    nsem = sum(ns for _, _, ns in plans)

    def body(*refs):
        cps = _run_plans(plans, refs[n:2 * n], refs[2 * n], refs[2 * n + 1])
        for cp in cps:
            cp.start()
        for cp in cps:
            cp.wait()

    return pl.pallas_call(
        body, name=name, in_specs=[ANY] * n, out_specs=[ANY] * n, out_shape=[SDS(b.shape, b.dtype) for b in bufs],
        input_output_aliases={a: a for a in range(n)},
        scratch_shapes=[pltpu.SemaphoreType.DMA((nsem,)), pltpu.SemaphoreType.DMA((nsem,))],
    )(*bufs)


def _exchange_start(name, bufs, plans, deps=()):
    n = len(bufs)
    nsem = sum(ns for _, _, ns in plans)
    deps = list(deps)
    first_out = n + len(deps)

    def body(*refs):
        for cp in _run_plans(plans, refs[:n], refs[first_out], refs[first_out + 1]):
            cp.start()
        token = refs[first_out + 2 + n]
        token[...] = jnp.zeros_like(token)

    outs = pl.pallas_call(
        body, name=name,
        out_shape=(pltpu.SemaphoreType.DMA((nsem,)), pltpu.SemaphoreType.DMA((nsem,)),
                   *[pltpu.HBM(b.shape, b.dtype) for b in bufs], SDS((8, 128), F32)),
        in_specs=[HBM_SPEC] * n + [ANY] * len(deps),
        out_specs=(SEM_SPEC, SEM_SPEC, *[HBM_SPEC] * n, pl.BlockSpec(memory_space=pltpu.VMEM)),
        input_output_aliases={a: 2 + a for a in range(n)},
        compiler_params=pltpu.CompilerParams(has_side_effects=EFFECT),
    )(*[pltpu.with_memory_space_constraint(b, pltpu.HBM) for b in bufs], *deps)
    return outs[0], outs[1], list(outs[2:2 + n]), outs[2 + n]


def _exchange_wait(name, send_sems, recv_sems, bufs, plans, after):
    n = len(bufs)

    def body(*refs):
        for cp in _run_plans(plans, refs[:n], refs[n], refs[n + 1]):
            cp.wait_send()
            cp.wait_recv()

    return pl.pallas_call(
        body, name=name, out_shape=[pltpu.HBM(b.shape, b.dtype) for b in bufs],
        in_specs=[HBM_SPEC] * n + [SEM_SPEC, SEM_SPEC, ANY], out_specs=[HBM_SPEC] * n,
        input_output_aliases={a: a for a in range(n)},
        compiler_params=pltpu.CompilerParams(has_side_effects=EFFECT),
    )(*bufs, send_sems, recv_sems, after)


def _plan_gather(refs, send_sems, recv_sems, base):
    x, y, c, chips = _place()
    p = 2 * x + y
    return [_rcopy(r.at[0, p], r.at[0, p], send_sems, recv_sems, base + 3 * a + j, (*chip, c))
            for a, r in enumerate(refs) for j, chip in enumerate(chips)]


def _plan_pair_exchange(refs, send_sems, recv_sems, base):
    x, y, c, _ = _place()
    n = len(refs) // 2
    return [_rcopy(refs[a].at[:, _half(refs[a].shape[1], 1 - c)], refs[n + a], send_sems, recv_sems, base + a,
                   (x, y, 1 - c)) for a in range(n)]


def _plan_chip_exchange(refs, send_sems, recv_sems, base):
    x, y, c, chips = _place()
    n = len(refs) // 2
    return [_rcopy(refs[a].at[2 * chip[0] + chip[1]], refs[n + a].at[j], send_sems, recv_sems, base + 3 * a + j,
                   (*chip, c)) for a in range(n) for j, chip in enumerate(chips)]


def _plan_pair_share(refs, send_sems, recv_sems, base):
    x, y, c, _ = _place()
    return [_rcopy(r.at[_half(r.shape[0], c)], r.at[_half(r.shape[0], c)], send_sems, recv_sems, base + a,
                   (x, y, 1 - c)) for a, r in enumerate(refs)]


def _plan_small_gather(refs, send_sems, recv_sems, base):
    x, y, c, _ = _place()
    me = 4 * x + 2 * y + c
    cps = []
    for m in range(1, NDEV):
        peer = (1 - x if m & 4 else x, 1 - y if m & 2 else y, 1 - c if m & 1 else c)
        cps.append(_rcopy(refs[0], refs[1].at[me], send_sems, recv_sems, base + m - 1, peer))
    return cps


def _sum_small(buf, gathered):
    def body(buf_ref, g_ref, o_ref):
        x, y, c, _ = _place()
        me = 4 * x + 2 * y + c
        total = jnp.where(me == 0, buf_ref[...], g_ref[0])
        for dev in range(1, NDEV):
            total = total + jnp.where(me == dev, buf_ref[...], g_ref[dev])
        o_ref[...] = total

    vm = pl.BlockSpec(memory_space=pltpu.VMEM)
    return pl.pallas_call(body, name="sum_small", in_specs=[vm, vm], out_specs=vm, out_shape=SDS(buf.shape, F32),
                          compiler_params=pltpu.CompilerParams(vmem_limit_bytes=VMEM_LIMIT))(buf, gathered)


BLOCK_ELEMS = 512 * 1024
SUM_BLOCK_ELEMS = 1024 * 1024


def _rows_per_block(rows, cols, mult, limit=BLOCK_ELEMS):
    best = None
    for tr in range(mult, rows + 1, mult):
        if rows % tr == 0 and tr * cols <= limit:
            best = tr
    assert best is not None, (rows, cols)
    return best


def _pair_sum(g, r, c_idx):
    nq, rows, cols = g.shape
    half = rows // 2
    tr = _rows_per_block(half, cols, 16, SUM_BLOCK_ELEMS)
    nb = half // tr

    def body(c_ref, g_ref, r_ref, t_ref):
        t_ref[...] = (g_ref[...] + r_ref[...]).astype(BF16)

    blk = pl.BlockSpec((None, tr, cols), lambda q, i, cr: (q, i, 0))
    spec = pltpu.PrefetchScalarGridSpec(
        num_scalar_prefetch=1, grid=(nq, nb),
        in_specs=[pl.BlockSpec((None, tr, cols), lambda q, i, cr: (q, cr[0] * nb + i, 0)), blk], out_specs=blk)
    return pl.pallas_call(body, name="grad_pair_sum", grid_spec=spec, out_shape=SDS((nq, half, cols), BF16),
                          compiler_params=_cp("parallel", "parallel"))(c_idx, g, r)


def _chip_sum(g, r, rr, cp_idx):
    _, rows, cols = g.shape
    half = rows // 2
    tr = _rows_per_block(half, cols, 16, SUM_BLOCK_ELEMS)
    nb = half // tr

    def body(cp_ref, buf_ref, g_ref, r_ref, rr_ref, o_ref):
        o_ref[...] = ((g_ref[...] + r_ref[...]) + rr_ref[0].astype(F32) + rr_ref[1].astype(F32) + rr_ref[2].astype(F32))

    spec = pltpu.PrefetchScalarGridSpec(
        num_scalar_prefetch=1, grid=(nb,),
        in_specs=[ANY, pl.BlockSpec((None, tr, cols), lambda i, cp: (cp[1], cp[0] * nb + i, 0)),
                  pl.BlockSpec((None, tr, cols), lambda i, cp: (cp[1], i, 0)),
                  pl.BlockSpec((3, tr, cols), lambda i, cp: (0, i, 0))],
        out_specs=pl.BlockSpec((tr, cols), lambda i, cp: (cp[0] * nb + i, 0)))
    return pl.pallas_call(body, name="grad_chip_sum", grid_spec=spec, out_shape=SDS((rows, cols), F32),
                          input_output_aliases={1: 0}, compiler_params=_cp("parallel"),
                          )(cp_idx, lax.empty((rows, cols), F32), g, r, rr)


def _adamw_math(w, g, m, v):
    mn = ADAM_B1 * m + (1.0 - ADAM_B1) * g
    vn = ADAM_B2 * v + (1.0 - ADAM_B2) * (g * g)
    m_hat = mn / (1.0 - ADAM_B1 ** ADAM_STEP)
    v_hat = vn / (1.0 - ADAM_B2 ** ADAM_STEP)
    return -ADAM_LR * (m_hat / (jnp.sqrt(v_hat) + ADAM_EPS) + ADAM_WD * w), mn, vn


def _adamw_layer(w, g, m, v, l, outs, deps=()):
    _, rows, cols = w.shape
    tr = _rows_per_block(rows, cols, 8)
    deps = list(deps)

    def body(*refs):
        w_ref, g_ref, m_ref, v_ref = refs[4:8]
        go_ref, d_ref, mo_ref, vo_ref = refs[8 + len(deps):]
        gg = g_ref[...]
        go_ref[...] = gg
        d_ref[...], mo_ref[...], vo_ref[...] = _adamw_math(w_ref[...], gg, m_ref[...], v_ref[...])

    blk = pl.BlockSpec((None, tr, cols), lambda i: (l, i, 0))
    return pl.pallas_call(
        body, name="adamw_layer", grid=(rows // tr,),
        in_specs=[ANY] * 4 + [blk, pl.BlockSpec((tr, cols), lambda i: (i, 0)), blk, blk] + [ANY] * len(deps),
        out_specs=[blk] * 4, out_shape=[SDS(w.shape, F32)] * 4, input_output_aliases={k: k for k in range(4)},
        compiler_params=_cp("parallel"))(*outs, w, g, m, v, *deps)


def _adamw_small(ws, gs, ms, vs, deps=()):
    n = len(ws)
    deps = list(deps)

    def body(*refs):
        refs = refs[:4 * n] + refs[4 * n + len(deps):]
        w, g, m, v, d_out, m_out, v_out = (refs[k * n:(k + 1) * n] for k in range(7))
        for k in range(n):
            d_out[k][...], m_out[k][...], v_out[k][...] = _adamw_math(w[k][...], g[k][...], m[k][...], v[k][...])

    vm = pl.BlockSpec(memory_space=pltpu.VMEM)
    outs = pl.pallas_call(body, name="adamw_small", in_specs=[vm] * (4 * n) + [ANY] * len(deps), out_specs=[vm] * (3 * n),
                          out_shape=[SDS(w.shape, F32) for w in ws] * 3,
                          compiler_params=pltpu.CompilerParams(vmem_limit_bytes=VMEM_LIMIT))(*ws, *gs, *ms, *vs, *deps)
    return outs[:n], outs[n:2 * n], outs[2 * n:]


_WEIGHTS = ["ffn1_pre_g", "ffn1_w_gu", "ffn1_w_down", "ffn1_post_g", "mix_pre_g", "w_in", "lru_conv_w", "lru_conv_b",
            "lru_w_a", "lru_b_a", "lru_w_x", "lru_b_x", "lru_lambda", "attn_sinks", "conv_w", "conv_b", "conv_ln_g",
            "conv_ln_b", "group_g", "w_out", "mix_post_g", "ffn2_pre_g", "ffn2_w_gu", "ffn2_w_down", "ffn2_post_g"]
_INPUTS = ["x"] + _WEIGHTS + ["loss_target"] + ["m_" + n for n in _WEIGHTS] + ["v_" + n for n in _WEIGHTS]
_BIG = ["ffn1_w_gu", "ffn1_w_down", "w_in", "w_out", "ffn2_w_gu", "ffn2_w_down"]
_SMALL_SHARDED = ["lru_conv_w", "conv_w"]
_SMALL_REPL = [n for n in _WEIGHTS if n not in _BIG and n not in _SMALL_SHARDED]

PACK_TILE = 8 * 128


def _pack(arrs):
    parts = []
    for a in arrs:
        flat = a.reshape(-1)
        parts.append(jnp.pad(flat, (0, -flat.shape[0] % PACK_TILE)).reshape(-1, 128))
    return jnp.concatenate(parts, axis=0)


def _unpack(buf, shapes):
    out, row = [], 0
    for shp in shapes:
        size = math.prod(shp)
        nrow = -(-size // PACK_TILE) * 8
        out.append(buf[row:row + nrow].reshape(-1)[:size].reshape(shp))
        row += nrow
    return out


def _unshard_cols(a):
    return a.transpose(0, 2, 1, 3).reshape(1, a.shape[2], NSHARD * a.shape[3])


_GROUPS = dict(ffn1_gu=["ffn1_w_gu"], ffn1_down=["ffn1_w_down"], mix=["w_in", "w_out", "lru_conv_w", "conv_w"],
               ffn2=["ffn2_w_gu", "ffn2_w_down"])


def _full_weights(group, gathered):
    g = dict(zip(_GROUPS[group], gathered))
    if group == "mix":
        return dict(w_in=_unshard_cols(g["w_in"]), w_out=g["w_out"].reshape(1, D, D),
                    lru_conv_w=_unshard_cols(g["lru_conv_w"])[0], conv_w=_unshard_cols(g["conv_w"])[0])
    return {n: (a.reshape(1, DFF, D) if n.endswith("w_down") else a) for n, a in g.items()}


def _by_shard(name, buf):
    if name.endswith("w_gu"):
        return buf[0]
    if name == "w_in":
        return buf.reshape(D, NSHARD, P_IN // NSHARD).transpose(1, 0, 2)
    return buf.reshape(NSHARD, buf.shape[2] // NSHARD, buf.shape[3])


class _Reducer:
    PLANS = (_plan_pair_exchange, _plan_chip_exchange, _plan_pair_share)

    def __init__(self, keys, gs, c_idx, cp_idx):
        self.keys, self.gs, self.c_idx, self.cp_idx = keys, gs, c_idx, cp_idx
        self.n = len(gs)
        self.step = 0
        self.result = None

    def inputs(self):
        n = self.n
        if self.step == 0:
            bufs = self.gs + [lax.empty((NSHARD, g.shape[1] // 2, g.shape[2]), F32) for g in self.gs]
        elif self.step == 1:
            ts = [_pair_sum(g, r, self.c_idx) for g, r in zip(self.gs, self.rs)]
            bufs = ts + [lax.empty((3,) + t.shape[1:], BF16) for t in ts]
        else:
            bufs = [_chip_sum(g, r, rr, self.cp_idx) for g, r, rr in zip(self.gs, self.rs, self.rrs)]
        return bufs, (self.PLANS[self.step], len(bufs), (n, 3 * n, n)[self.step])

    def absorb(self, done):
        n = self.n
        if self.step == 0:
            self.gs, self.rs = done[:n], done[n:]
        elif self.step == 1:
            self.rrs = done[n:]
        else:
            self.result = dict(zip(self.keys, done))
        self.step += 1


class _SmallGather:
    def __init__(self, buf):
        self.buf, self.step, self.result, self.gathered = buf, 0, {}, None

    def inputs(self):
        return [self.buf, jnp.zeros((NDEV,) + self.buf.shape, F32)], (_plan_small_gather, 2, NDEV - 1)

    def absorb(self, done):
        self.buf, self.gathered = done
        self.step = 3


class _ReducePipeline:
    def __init__(self, c_idx, cp_idx):
        self.c_idx, self.cp_idx = c_idx, cp_idx
        self.reducers, self.flying, self.calls = [], None, 0

    def add(self, layer, done):
        if done:
            keys = [(layer, n) for n in done]
            self.reducers.append(_Reducer(keys, [_by_shard(n, b) for n, b in done.items()], self.c_idx, self.cp_idx))

    def _next(self):
        active = [r for r in self.reducers if r.step < 3]
        bufs, plans = [], []
        for r in active:
            b, triple = r.inputs()
            bufs += b
            plans.append(triple)
        self.calls += 1
        return active, bufs, plans, "grad_exchange%d" % self.calls

    def _absorb(self, active, plans, done):
        at = 0
        for r, (_, nb, _) in zip(active, plans):
            r.absorb(done[at:at + nb])
            at += nb

    def _land(self, after):
        if self.flying is not None:
            active, plans, name, send_sems, recv_sems, bufs = self.flying
            self._absorb(active, plans, _exchange_wait(name + "_wait", send_sems, recv_sems, bufs, plans, after))
            self.flying = None

    def hook(self, after):
        self._land(after)
        active, bufs, plans, name = self._next()
        if not active:
            return []
        send_sems, recv_sems, bufs, token = _exchange_start(name + "_start", bufs, plans)
        self.flying = (active, plans, name, send_sems, recv_sems, bufs)
        return [token]

    def available(self):
        out = {}
        for r in self.reducers:
            if r.step == 3:
                out.update(r.result)
        return out

    def finish(self, after):
        self._land(after)
        while True:
            active, bufs, plans, name = self._next()
            if not active:
                break
            self._absorb(active, plans, _exchange(name, bufs, plans))
        out = {}
        for r in self.reducers:
            out.update(r.result)
        return out


def kernel(*args):
    d = dict(zip(_INPUTS, args, strict=True))
    xi, yi, ci = lax.axis_index("x"), lax.axis_index("y"), lax.axis_index("c")
    p = 2 * xi + yi
    c_idx = jnp.reshape(ci, (1,)).astype(jnp.int32)
    p_idx = jnp.reshape(p, (1,)).astype(jnp.int32)
    cp_idx = jnp.stack([ci, p]).astype(jnp.int32)
    x, target = d["x"][0], d["loss_target"][0]
    tiles = _tiles(x.shape[0])

    groups = [(l, grp) for l in range(DEPTH) for grp in _GROUPS]
    placed = {(l, grp): [_place_shard(d[n], l, p_idx, BF16 if n in _BIG else F32) for n in _GROUPS[grp]]
              for l, grp in groups}
    ready = {groups[0]: _gather_two_level(placed[groups[0]], len(placed[groups[0]]))}
    flying, tokens = {}, [ready[groups[0]][0]]
    for l, grp in groups[1:]:
        plans = [(_plan_gather, len(placed[l, grp]), 3 * len(placed[l, grp]))]
        send_sems, recv_sems, bufs, token = _exchange_start("gather_l%d_%s_start" % (l, grp), placed[l, grp], plans,
                                                             tokens[-1:])
        flying[l, grp] = (send_sems, recv_sems, bufs, plans)
        tokens.append(token)

    def weights_of(l):
        def weights(grp, after):
            if (l, grp) not in ready:
                send_sems, recv_sems, bufs, plans = flying[l, grp]
                ready[l, grp] = _exchange_wait("gather_l%d_%s_wait" % (l, grp), send_sems, recv_sems, bufs, plans, after)
            return _full_weights(grp, ready[l, grp])
        return weights

    small = {n: d[n] for n in _SMALL_REPL}
    x1, sv0 = _forward_layer(x, weights_of(0), _layer_params(small, 0), tiles, tokens[1:])
    x2, sv1 = _forward_layer(x1, weights_of(1), _layer_params(small, 1), tiles)
    dx, lcols = _loss_grad(x2, target, tiles[0])

    pipe = _ReducePipeline(c_idx, cp_idx)
    sgrads = [None] * DEPTH
    for l, sv in ((1, sv1), (0, sv0)):
        bufs = _grad_buffers()

        def stage(done, dx, l=l):
            pipe.add(l, done)
            return pipe.hook(dx)

        dx, sgrads[l] = _backward_layer(dx, sv, bufs, tiles, stage)
    grad_x = dx

    stacked = {n: jnp.stack([sgrads[l][n].reshape(d[n].shape[1:]) for l in range(DEPTH)]) for n in _SMALL_REPL}
    for n in _SMALL_SHARDED:
        stacked[n] = jnp.stack([sgrads[l][n] for l in range(DEPTH)])
    loss_part = jnp.pad((0.5 / D) * jnp.sum(lcols).reshape(1), (0, 127))
    order = _SMALL_REPL + _SMALL_SHARDED
    small_gather = _SmallGather(_pack([loss_part] + [stacked[n] for n in order]))
    pipe.reducers.append(small_gather)

    results = {n: tuple(lax.empty(d[n].shape, F32) for _ in range(4)) for n in _BIG}
    applied = set()

    def apply_ready(deps, last):
        for (l, n), g in pipe.available().items():
            if (l, n) not in applied:
                results[n] = _adamw_layer(d[n], g, d["m_" + n], d["v_" + n], l, results[n], deps)
                applied.add((l, n))
                last = results[n][1]
                deps = [last]
        return last

    last = apply_ready(pipe.hook(grad_x), grad_x)
    token = pipe.hook(last)
    summed = _unpack(_sum_small(small_gather.buf, small_gather.gathered), [(128,)] + [stacked[n].shape for n in order])
    loss = summed[0][0]
    grads = {}
    for n, g in zip(order, summed[1:]):
        if n in _SMALL_SHARDED:
            g = lax.dynamic_slice_in_dim(g, p * (g.shape[2] // NSHARD), g.shape[2] // NSHARD, axis=2)
        grads[n] = g
    delta, new_m, new_v = {}, {}, {}
    small_out = _adamw_small([d[n] for n in order], [grads[n] for n in order], [d["m_" + n] for n in order],
                             [d["v_" + n] for n in order], token)
    for out, res in zip((delta, new_m, new_v), small_out):
        out.update(zip(order, res))
    last = apply_ready([small_out[0][0]], small_out[0][0])
    pipe.finish(last)
    apply_ready((), last)
    for n in _BIG:
        grads[n], delta[n], new_m[n], new_v[n] = results[n]

    return (loss, grad_x[None], *[grads[n] for n in _WEIGHTS], *[delta[n] for n in _WEIGHTS],
            *[new_m[n] for n in _WEIGHTS], *[new_v[n] for n in _WEIGHTS])
```

```python
import functools
import math

import jax
import jax.numpy as jnp
from jax import lax
from jax.experimental import pallas as pl
from jax.experimental.pallas import tpu as pltpu

F32 = jnp.float32
BF16 = jnp.bfloat16
SDS = jax.ShapeDtypeStruct

D = 1024
DFF = 2816
FH = DFF // 2
DEPTH = 2
W_A = 256
W_B = 512
W_C = 256
NQ = 8
HD = 64
BLK = 128
ATT_NB_FWD = 1
ATT_NB_BWD = 4
P_IN = 1792
LRU_K = 4
CONV_K = 31
LRU_C = 8.0
NORM_EPS = 1e-6
LN_EPS = 1e-5
NEG_BIG = -1e30
SCALE = 1.0 / math.sqrt(HD)

ADAM_LR = 0.001
ADAM_B1 = 0.9
ADAM_B2 = 0.999
ADAM_EPS = 1e-08
ADAM_WD = 0.01
ADAM_STEP = 10

VMEM_LIMIT = 60 * 1024 * 1024
NSHARD = 4
NDEV = 8

TN = (((0,), (0,)), ((), ()))
NT = (((1,), (1,)), ((), ()))

MESH = pl.DeviceIdType.MESH
ANY = pl.BlockSpec(memory_space=pl.ANY)


def _cp(*sem):
    return pltpu.CompilerParams(dimension_semantics=sem if sem else None, vmem_limit_bytes=VMEM_LIMIT)


def _rsq(x, eps):
    return lax.rsqrt(jnp.mean(x * x, axis=-1, keepdims=True) + eps)


def _rms_bwd_rows(x, g, dy):
    r = _rsq(x, NORM_EPS)
    xh = x * r
    dyg = dy * g
    dx = r * (dyg - xh * jnp.mean(dyg * xh, axis=-1, keepdims=True))
    return dx, dy * xh


def _sig(x):
    return jax.nn.sigmoid(x)


def _ffn_up(x, pre_g, wgu, l, tm, deps=()):
    s = x.shape[0]
    deps = list(deps)

    def body(x_ref, g_ref, wg_ref, wu_ref, *rest):
        h_ref, go_ref, uo_ref, a_ref = rest[len(deps):]

        @pl.when(pl.program_id(1) == 0)
        def _():
            xf = x_ref[...]
            h_ref[...] = (xf * _rsq(xf, NORM_EPS) * g_ref[...]).astype(BF16)

        h = h_ref[...]
        gg = jnp.dot(h, wg_ref[...], preferred_element_type=F32)
        uu = jnp.dot(h, wu_ref[...], preferred_element_type=F32)
        sg = _sig(gg)
        silu = gg * sg
        go_ref[...] = (uu * (sg * (1.0 + gg * (1.0 - sg)))).astype(BF16)
        uo_ref[...] = silu.astype(BF16)
        a_ref[...] = (silu * uu).astype(BF16)

    wide = pl.BlockSpec((tm, FH), lambda i, j: (i, j))
    return pl.pallas_call(
        body, name="ffn_up", grid=(s // tm, 2),
        in_specs=[pl.BlockSpec((tm, D), lambda i, j: (i, 0)), pl.BlockSpec((1, D), lambda i, j: (0, 0)),
                  pl.BlockSpec((None, None, D, FH), lambda i, j: (l, j, 0, 0)),
                  pl.BlockSpec((None, None, D, FH), lambda i, j: (l, j + 2, 0, 0))] + [ANY] * len(deps),
        out_specs=[pl.BlockSpec((tm, D), lambda i, j: (i, 0)), wide, wide, wide],
        out_shape=[SDS((s, D), BF16), SDS((s, DFF), BF16), SDS((s, DFF), BF16), SDS((s, DFF), BF16)],
        compiler_params=_cp("parallel", "arbitrary"),
    )(x, pre_g, wgu, wgu, *deps)


def _mm_rms_res(a, w, l, x, g, c, tm, tk, name):
    s, k_dim = a.shape
    nk = k_dim // tk

    def body(a_ref, w_ref, x_ref, g_ref, z_ref, x1_ref):
        k = pl.program_id(1)
        p = jnp.dot(a_ref[...], w_ref[...], preferred_element_type=F32)

        @pl.when(k == 0)
        def _():
            z_ref[...] = p

        @pl.when(k > 0)
        def _():
            z_ref[...] += p

        @pl.when(k == nk - 1)
        def _():
            z = z_ref[...]
            x1_ref[...] = x_ref[...] + c * (z * _rsq(z, NORM_EPS) * g_ref[...])

    row = pl.BlockSpec((tm, D), lambda i, k: (i, 0))
    return pl.pallas_call(
        body, name=name, grid=(s // tm, nk),
        in_specs=[pl.BlockSpec((tm, tk), lambda i, k: (i, k)), pl.BlockSpec((None, tk, D), lambda i, k: (l, k, 0)),
                  row, pl.BlockSpec((1, D), lambda i, k: (0, 0))],
        out_specs=[row, row],
        out_shape=[SDS((s, D), F32), SDS((s, D), F32)],
        compiler_params=_cp("parallel", "arbitrary"),
    )(a, w, x, g)


def _rms_bwd(dy, z, g, c, tm, name, deps=()):
    s = z.shape[0]
    deps = list(deps)

    def body(dy_ref, z_ref, g_ref, *rest):
        dz_ref, dg_ref = rest[len(deps):]
        dz, dgr = _rms_bwd_rows(z_ref[...], g_ref[...], c * dy_ref[...])
        dz_ref[...] = dz.astype(BF16)
        part = jnp.sum(dgr, axis=0, keepdims=True)

        @pl.when(pl.program_id(0) == 0)
        def _():
            dg_ref[...] = part

        @pl.when(pl.program_id(0) > 0)
        def _():
            dg_ref[...] += part

    row = pl.BlockSpec((tm, D), lambda i: (i, 0))
    vec = pl.BlockSpec((1, D), lambda i: (0, 0))
    return pl.pallas_call(
        body, name=name, grid=(s // tm,), in_specs=[row, row, vec] + [ANY] * len(deps), out_specs=[row, vec],
        out_shape=[SDS((s, D), BF16), SDS((1, D), F32)], compiler_params=_cp("arbitrary"),
    )(dy, z, g, *deps)


def _ffn_bwd_mid(dz, wd, l, dadg, dadu, tm):
    s = dz.shape[0]

    def body(dz_ref, wd_ref, g_ref, u_ref, dg_ref, du_ref):
        da = lax.dot_general(dz_ref[...], wd_ref[...], NT, preferred_element_type=F32)
        dg_ref[...] = (da * g_ref[...].astype(F32)).astype(BF16)
        du_ref[...] = (da * u_ref[...].astype(F32)).astype(BF16)

    wide = pl.BlockSpec((tm, FH), lambda i, j: (i, j))
    return pl.pallas_call(
        body, name="ffn_bwd_mid", grid=(s // tm, 2),
        in_specs=[pl.BlockSpec((tm, D), lambda i, j: (i, 0)), pl.BlockSpec((None, FH, D), lambda i, j: (l, j, 0)), wide, wide],
        out_specs=[wide, wide],
        out_shape=[SDS((s, DFF), BF16), SDS((s, DFF), BF16)],
        compiler_params=_cp("parallel", "arbitrary"),
    )(dz, wd, dadg, dadu)


def _ffn_bwd_dh(dg, du, wgu, l, x, pre_g, dx1, tm, deps=()):
    s = x.shape[0]
    deps = list(deps)

    def body(dg_ref, du_ref, wg_ref, wu_ref, x_ref, g_ref, dx1_ref, *rest):
        dx_ref, dgp_ref = rest[len(deps):]
        i, k = pl.program_id(0), pl.program_id(1)
        p = (lax.dot_general(dg_ref[...], wg_ref[...], NT, preferred_element_type=F32)
             + lax.dot_general(du_ref[...], wu_ref[...], NT, preferred_element_type=F32))

        @pl.when(k == 0)
        def _():
            dx_ref[...] = p

        @pl.when(k == 1)
        def _():
            dx, dgr = _rms_bwd_rows(x_ref[...], g_ref[...], dx_ref[...] + p)
            dx_ref[...] = dx1_ref[...] + dx
            part = jnp.sum(dgr, axis=0, keepdims=True)

            @pl.when(i == 0)
            def _():
                dgp_ref[...] = part

            @pl.when(i > 0)
            def _():
                dgp_ref[...] += part

    wide = pl.BlockSpec((tm, FH), lambda i, k: (i, k))
    row = pl.BlockSpec((tm, D), lambda i, k: (i, 0))
    vec = pl.BlockSpec((1, D), lambda i, k: (0, 0))
    return pl.pallas_call(
        body, name="ffn_bwd_dh", grid=(s // tm, 2),
        in_specs=[wide, wide, pl.BlockSpec((None, None, D, FH), lambda i, k: (l, k, 0, 0)),
                  pl.BlockSpec((None, None, D, FH), lambda i, k: (l, k + 2, 0, 0)), row, vec, row] + [ANY] * len(deps),
        out_specs=[row, vec],
        out_shape=[SDS((s, D), F32), SDS((1, D), F32)],
        compiler_params=_cp("arbitrary", "arbitrary"),
    )(dg, du, wgu, wgu, x, pre_g, dx1, *deps)


def _mm_tn_into(buf, a, b, l, joff, tka, tn, ts, name):
    s, ka = a.shape
    n = b.shape[1]

    def body(buf_ref, a_ref, b_ref, o_ref):
        p = lax.dot_general(a_ref[...], b_ref[...], TN, preferred_element_type=F32)

        @pl.when(pl.program_id(2) == 0)
        def _():
            o_ref[...] = p

        @pl.when(pl.program_id(2) > 0)
        def _():
            o_ref[...] += p

    return pl.pallas_call(
        body, name=name, grid=(ka // tka, n // tn, s // ts),
        in_specs=[pl.BlockSpec(memory_space=pl.ANY),
                  pl.BlockSpec((ts, tka), lambda ia, j, t: (t, ia)), pl.BlockSpec((ts, tn), lambda ia, j, t: (t, j))],
        out_specs=pl.BlockSpec((None, None, tka, tn), lambda ia, j, t: (l, joff + j, ia, 0)),
        out_shape=SDS(buf.shape, F32), input_output_aliases={0: 0},
        compiler_params=_cp("parallel", "parallel", "arbitrary"),
    )(buf, a, b)


def _proj(x, g, w_in, l, tm):
    s = x.shape[0]

    def body(x_ref, g_ref, w_ref, h_ref, p_ref):
        xf = x_ref[...]
        h = (xf * _rsq(xf, NORM_EPS) * g_ref[...]).astype(BF16)
        h_ref[...] = h
        p_ref[...] = jnp.dot(h, w_ref[...], preferred_element_type=F32)

    return pl.pallas_call(
        body, name="proj", grid=(s // tm,),
        in_specs=[pl.BlockSpec((tm, D), lambda i: (i, 0)), pl.BlockSpec((1, D), lambda i: (0, 0)),
                  pl.BlockSpec((None, D, P_IN), lambda i: (l, 0, 0))],
        out_specs=[pl.BlockSpec((tm, D), lambda i: (i, 0)), pl.BlockSpec((tm, P_IN), lambda i: (i, 0))],
        out_shape=[SDS((s, D), BF16), SDS((s, P_IN), F32)],
        compiler_params=_cp("parallel"),
    )(x, g, w_in)


def _mm_nt(a, w, l, tm, name):
    s, k_dim = a.shape
    n = w.shape[1]

    def body(a_ref, w_ref, o_ref):
        o_ref[...] = lax.dot_general(a_ref[...], w_ref[...], NT, preferred_element_type=F32)

    return pl.pallas_call(
        body, name=name, grid=(s // tm,),
        in_specs=[pl.BlockSpec((tm, k_dim), lambda i: (i, 0)), pl.BlockSpec((None, n, k_dim), lambda i: (l, 0, 0))],
        out_specs=pl.BlockSpec((tm, n), lambda i: (i, 0)),
        out_shape=SDS((s, n), F32), compiler_params=_cp("parallel"),
    )(a, w)


def _mm_nt_rmsbwd(dp, w_in, l, x, g, dx1, tm):
    s = x.shape[0]

    def body(dp_ref, w_ref, x_ref, g_ref, dx1_ref, dx_ref, dg_ref):
        dh = lax.dot_general(dp_ref[...], w_ref[...], NT, preferred_element_type=F32)
        dx, dgr = _rms_bwd_rows(x_ref[...], g_ref[...], dh)
        dx_ref[...] = dx1_ref[...] + dx
        part = jnp.sum(dgr, axis=0, keepdims=True)

        @pl.when(pl.program_id(0) == 0)
        def _():
            dg_ref[...] = part

        @pl.when(pl.program_id(0) > 0)
        def _():
            dg_ref[...] += part

    row = pl.BlockSpec((tm, D), lambda i: (i, 0))
    vec = pl.BlockSpec((1, D), lambda i: (0, 0))
    return pl.pallas_call(
        body, name="mix_bwd_dx", grid=(s // tm,),
        in_specs=[pl.BlockSpec((tm, P_IN), lambda i: (i, 0)), pl.BlockSpec((None, D, P_IN), lambda i: (l, 0, 0)), row, vec, row],
        out_specs=[row, vec], out_shape=[SDS((s, D), F32), SDS((1, D), F32)],
        compiler_params=_cp("arbitrary"),
    )(dp, w_in, x, g, dx1)


def _row_iota(shape):
    return lax.broadcasted_iota(jnp.int32, shape, 0)


def _lru_gates(xc, wa_ref, ba_ref, wx_ref, bx_ref, lam_ref):
    xb = xc.astype(BF16)
    r = _sig(jnp.dot(xb, wa_ref[...], preferred_element_type=F32) + ba_ref[...])
    ig = _sig(jnp.dot(xb, wx_ref[...], preferred_element_type=F32) + bx_ref[...])
    nl = -lam_ref[...]
    sp = jnp.maximum(nl, 0.0) + jnp.log(1.0 + jnp.exp(-jnp.abs(nl)))
    log_a = -LRU_C * r * sp
    a = jnp.exp(log_a)
    x2 = 2.0 * log_a
    series = x2 * (1.0 + x2 * (0.5 + x2 * (1.0 / 6.0 + x2 * (1.0 / 24.0 + x2 * (1.0 / 120.0)))))
    em1 = jnp.where(x2 > -0.05, series, jnp.exp(x2) - 1.0)
    mlt = jnp.sqrt(-em1)
    return r, ig, a, mlt, sp


def _conv_taps(src_ref, w_ref, k_taps, pad, tc):
    acc = None
    for j in range(k_taps):
        term = w_ref[j:j + 1, :] * src_ref[pl.ds(pad - (k_taps - 1) + j, tc), :]
        acc = term if acc is None else acc + term
    return acc


def _fill_shifted(src_ref, sh_ref):
    n = src_ref.shape[0] - 8
    for s in range(1, 8):
        sh_ref[s, 0:n, :] = src_ref[pl.ds(s, n), :]


def _shifted_rows(src_ref, sh_ref, offset, tc):
    if offset % 8 == 0:
        return src_ref[pl.ds(offset, tc), :]
    return sh_ref[offset % 8, pl.ds(offset - offset % 8, tc), :]


def _gelu_parts(x):
    c0 = math.sqrt(2.0 / math.pi)
    inner = c0 * (x + 0.044715 * x * x * x)
    t = jnp.tanh(inner)
    gl = 0.5 * x * (1.0 + t)
    dgl = 0.5 * (1.0 + t) + 0.5 * x * (1.0 - t * t) * c0 * (1.0 + 3.0 * 0.044715 * x * x)
    return gl, dgl


def _lru_fwd(proj, cw, cb, wa, ba, wx, bx, lam, gg, tc):
    s = proj.shape[0]
    pad = 8

    def body(xcur_ref, xprev_ref, gate_ref, cw_ref, cb_ref, wa_ref, ba_ref, wx_ref, bx_ref, lam_ref, gg_ref,
             yn_ref, h_ref, xs_ref, hc_ref):
        i = pl.program_id(0)

        @pl.when(i == 0)
        def _():
            hc_ref[...] = jnp.zeros_like(hc_ref)

        xs_ref[0:pad, :] = jnp.where(i > 0, xprev_ref[tc - pad:tc, :], 0.0)
        xs_ref[pad:pad + tc, :] = xcur_ref[...]
        xc = _conv_taps(xs_ref, cw_ref, LRU_K, pad, tc) + cb_ref[...]
        _, ig, a, mlt, _ = _lru_gates(xc, wa_ref, ba_ref, wx_ref, bx_ref, lam_ref)
        u = mlt * (ig * xc)
        row = _row_iota((tc, W_A))
        d = 1
        while d < tc:
            ok = row >= d
            a_sh = jnp.where(ok, pltpu.roll(a, d, axis=0), 1.0)
            u_sh = jnp.where(ok, pltpu.roll(u, d, axis=0), 0.0)
            u = a * u_sh + u
            a = a * a_sh
            d *= 2
        h = u + a * hc_ref[...]
        hc_ref[...] = jnp.sum(jnp.where(row == tc - 1, h, 0.0), axis=0, keepdims=True)
        h_ref[...] = h
        gl, _ = _gelu_parts(gate_ref[...])
        ya = gl * h
        yn_ref[...] = (ya * _rsq(ya, NORM_EPS) * gg_ref[...]).astype(BF16)

    blk = lambda c: pl.BlockSpec((tc, W_A), lambda i, c=c: (i, c))
    full = lambda a: pl.BlockSpec(a.shape, lambda i: (0,) * a.ndim)
    params = [cw, cb, wa, ba, wx, bx, lam, gg]
    return pl.pallas_call(
        body, name="lru_fwd", grid=(s // tc,),
        in_specs=[blk(0), pl.BlockSpec((tc, W_A), lambda i: (jnp.maximum(i - 1, 0), 0)), blk(1)] + [full(a) for a in params],
        out_specs=[pl.BlockSpec((tc, W_A), lambda i: (i, 0))] * 2,
        out_shape=[SDS((s, W_A), BF16), SDS((s, W_A), F32)],
        scratch_shapes=[pltpu.VMEM((tc + pad, W_A), F32), pltpu.VMEM((1, W_A), F32)],
        compiler_params=_cp("arbitrary"),
    )(proj, proj, proj, *params)


def _acc(ref, first, val):
    @pl.when(first)
    def _():
        ref[...] = val

    @pl.when(jnp.logical_not(first))
    def _():
        ref[...] += val


def _lru_bwd(dy, proj, h, cw, cb, wa, ba, wx, bx, lam, gg, tc):
    s = proj.shape[0]
    nc = s // tc
    pad = 8

    def body(dy_ref, xcur_ref, xprev_ref, gate_ref, h_ref, hprev_ref, cw_ref, cb_ref, wa_ref, ba_ref, wx_ref, bx_ref,
             lam_ref, gg_ref,
             dp_ref, dcw_ref, dcb_ref, dwa_ref, dba_ref, dwx_ref, dbx_ref, dlam_ref, dgg_ref,
             xs_ref, ds_ref, mu_ref, nx_ref):
        step = pl.program_id(0)
        i = nc - 1 - step
        first = step == 0

        @pl.when(first)
        def _():
            mu_ref[...] = jnp.zeros_like(mu_ref)
            nx_ref[...] = jnp.zeros_like(nx_ref)

        xs_ref[0:pad, :] = jnp.where(i > 0, xprev_ref[tc - pad:tc, :], 0.0)
        xs_ref[pad:pad + tc, :] = xcur_ref[...]
        xc = _conv_taps(xs_ref, cw_ref, LRU_K, pad, tc) + cb_ref[...]
        r, ig, a, mlt, sp = _lru_gates(xc, wa_ref, ba_ref, wx_ref, bx_ref, lam_ref)
        hh = h_ref[...]
        gate = gate_ref[...]
        gl, dgl = _gelu_parts(gate)
        ya = gl * hh
        dya, dggr = _rms_bwd_rows(ya, gg_ref[...], dy_ref[...])
        _acc(dgg_ref, first, jnp.sum(dggr, axis=0, keepdims=True))
        dp_ref[:, W_A:2 * W_A] = dya * hh * dgl
        dh = dya * gl

        row = _row_iota((tc, W_A))
        aa = a
        uu = a * dh
        d = 1
        while d < tc:
            ok = row < tc - d
            a_sh = jnp.where(ok, pltpu.roll(aa, tc - d, axis=0), 1.0)
            u_sh = jnp.where(ok, pltpu.roll(uu, tc - d, axis=0), 0.0)
            uu = uu + aa * u_sh
            aa = aa * a_sh
            d *= 2
        cin = mu_ref[...]
        mu = uu + aa * cin
        lam_t = dh + jnp.where(row == tc - 1, cin, pltpu.roll(mu, tc - 1, axis=0))
        mu_ref[...] = jnp.sum(jnp.where(row == 0, mu, 0.0), axis=0, keepdims=True)
        hm1 = jnp.where(row == 0, jnp.where(i > 0, pltpu.roll(hprev_ref[...], 1, axis=0), 0.0),
                        pltpu.roll(hh, 1, axis=0))
        da = lam_t * hm1
        du = lam_t
        dmlt = du * ig * xc
        dig = du * mlt * xc
        dxc = du * mlt * ig
        dlog_a = da * a - dmlt * (a * a / mlt)
        dr = dlog_a * (-LRU_C * sp)
        dsp = jnp.sum(dlog_a * (-LRU_C * r), axis=0, keepdims=True)
        _acc(dlam_ref, first, dsp * (-_sig(-lam_ref[...])))
        dga = dr * r * (1.0 - r)
        dgx = dig * ig * (1.0 - ig)
        _acc(dba_ref, first, jnp.sum(dga, axis=0, keepdims=True))
        _acc(dbx_ref, first, jnp.sum(dgx, axis=0, keepdims=True))
        xb = xc.astype(BF16)
        dgab = dga.astype(BF16)
        dgxb = dgx.astype(BF16)
        _acc(dwa_ref, first, lax.dot_general(xb, dgab, TN, preferred_element_type=F32))
        _acc(dwx_ref, first, lax.dot_general(xb, dgxb, TN, preferred_element_type=F32))
        dxc = (dxc + lax.dot_general(dgab, wa_ref[...], NT, preferred_element_type=F32)
               + lax.dot_general(dgxb, wx_ref[...], NT, preferred_element_type=F32))

        _acc(dcb_ref, first, jnp.sum(dxc, axis=0, keepdims=True))
        r8 = _row_iota((8, W_A))
        dcw = jnp.zeros((8, W_A), F32)
        for j in range(LRU_K):
            tap = jnp.sum(dxc * xs_ref[pl.ds(pad - (LRU_K - 1) + j, tc), :], axis=0, keepdims=True)
            dcw = dcw + jnp.where(r8 == j, tap, 0.0)
        _acc(dcw_ref, first, dcw)
        ds_ref[0:tc, :] = dxc
        ds_ref[tc:tc + pad, :] = nx_ref[...]
        dlx = None
        for j in range(LRU_K):
            term = cw_ref[j:j + 1, :] * ds_ref[pl.ds(LRU_K - 1 - j, tc), :]
            dlx = term if dlx is None else dlx + term
        dp_ref[:, 0:W_A] = dlx
        nx_ref[...] = dxc[0:pad, :]

    rev = lambda c: pl.BlockSpec((tc, W_A), lambda t, c=c: (nc - 1 - t, c))
    prev = lambda c: pl.BlockSpec((tc, W_A), lambda t, c=c: (jnp.maximum(nc - 2 - t, 0), c))
    full = lambda a: pl.BlockSpec(a.shape, lambda t: (0,) * a.ndim)
    params = [cw, cb, wa, ba, wx, bx, lam, gg]
    vec = SDS((1, W_A), F32)
    sq = SDS((W_A, W_A), F32)
    outs = [SDS((s, 2 * W_A), F32), SDS((8, W_A), F32), vec, sq, vec, sq, vec, vec, vec]
    return pl.pallas_call(
        body, name="lru_bwd", grid=(nc,),
        in_specs=[rev(0), rev(0), prev(0), rev(1), rev(0), prev(0)] + [full(a) for a in params],
        out_specs=[pl.BlockSpec((tc, 2 * W_A), lambda t: (nc - 1 - t, 0))]
        + [pl.BlockSpec(o.shape, lambda t: (0, 0)) for o in outs[1:]],
        out_shape=outs,
        scratch_shapes=[pltpu.VMEM((tc + pad, W_A), F32), pltpu.VMEM((tc + pad, W_A), F32),
                        pltpu.VMEM((1, W_A), F32), pltpu.VMEM((pad, W_A), F32)],
        compiler_params=_cp("arbitrary"),
    )(dy, proj, proj, proj, h, h, *params)


def _attn_stack(qa, qb, kvh):
    lane = lax.broadcasted_iota(jnp.int32, qa.shape, 1)
    keep = (lane >= HD) if kvh == 1 else (lane < HD)
    parts = []
    for tile in (qa, qb):
        for half in (0, 1):
            y = tile if half == kvh else pltpu.roll(tile, HD, axis=1)
            parts.append(jnp.where(keep, y, 0.0))
    return jnp.concatenate(parts, axis=0)


def _attn_unstack(o, kvh):
    lane = lax.broadcasted_iota(jnp.int32, (BLK, 2 * HD), 1)
    tiles = []
    for t in range(2):
        halves = []
        for half in (0, 1):
            blk = o[(2 * t + half) * BLK:(2 * t + half + 1) * BLK, :]
            halves.append(blk if half == kvh else pltpu.roll(blk, HD, axis=1))
        tiles.append(jnp.where(lane < HD, halves[0], halves[1]))
    return tiles


def _attn_stack_all(x_ref_or_val):
    return jnp.concatenate([_attn_stack(x_ref_or_val[:, 256 * kvh:256 * kvh + 128],
                                        x_ref_or_val[:, 256 * kvh + 128:256 * kvh + 256], kvh) for kvh in range(2)], axis=0)


def _attn_unstack_all(o, dst_ref):
    for kvh in range(2):
        ta, tb = _attn_unstack(o[4 * BLK * kvh:4 * BLK * (kvh + 1), :], kvh)
        dst_ref[:, 256 * kvh:256 * kvh + 128] = ta
        dst_ref[:, 256 * kvh + 128:256 * kvh + 256] = tb


def _attn_windows(cur_ref, prev_ref, nb):
    blocks = [prev_ref[...]] + [cur_ref[b * BLK:(b + 1) * BLK, :] for b in range(nb)]
    return [jnp.concatenate(blocks[b:b + 2], axis=0).astype(BF16) for b in range(nb)]


def _attn_probs(qs, kw, n, sink_ref):
    rows = NQ * BLK
    sc = lax.dot_general(qs.astype(BF16), kw, NT, preferred_element_type=F32) * SCALE
    qi = lax.broadcasted_iota(jnp.int32, (rows, 2 * BLK), 0) & (BLK - 1)
    kj = lax.broadcasted_iota(jnp.int32, (rows, 2 * BLK), 1)
    rel = BLK + qi - kj
    mask = (rel >= 0) & (rel < BLK) & ((n - 1) * BLK + kj >= 0)
    head = lax.broadcasted_iota(jnp.int32, (rows, 1), 0) // BLK
    sk = jnp.zeros((rows, 1), F32)
    for h in range(NQ):
        sk = jnp.where(head == h, sink_ref[h:h + 1, 0:1], sk)
    sh = jnp.where(mask, sc, NEG_BIG)
    m = jnp.maximum(jnp.max(sh, axis=-1, keepdims=True), sk)
    e = jnp.exp(sh - m)
    es = jnp.exp(sk - m)
    rz = 1.0 / (jnp.sum(e, axis=-1, keepdims=True) + es)
    return e * rz, es * rz


def _attn_fwd(proj, sinks8, gg):
    s = proj.shape[0]
    nb = ATT_NB_FWD

    def body(q_ref, kc_ref, kp_ref, vc_ref, vp_ref, sink_ref, gg_ref, yn_ref, ob_ref):
        kws, vws = _attn_windows(kc_ref, kp_ref, nb), _attn_windows(vc_ref, vp_ref, nb)
        for b in range(nb):
            rows = pl.ds(b * BLK, BLK)
            p, _ = _attn_probs(_attn_stack_all(q_ref.at[rows, :]), kws[b], nb * pl.program_id(0) + b, sink_ref)
            _attn_unstack_all(jnp.dot(p.astype(BF16), vws[b], preferred_element_type=F32), ob_ref.at[rows, :])
        ob = ob_ref[...]
        yn_ref[...] = (ob * _rsq(ob, NORM_EPS) * gg_ref[...]).astype(BF16)

    tb = nb * BLK
    cur = lambda c: pl.BlockSpec((tb, 128), lambda m, c=c: (m, c))
    prev = lambda c: pl.BlockSpec((BLK, 128), lambda m, c=c: (jnp.maximum(nb * m - 1, 0), c))
    out = pl.BlockSpec((tb, W_B), lambda m: (m, 0))
    return pl.pallas_call(
        body, name="attn_fwd", grid=(s // tb,),
        in_specs=[pl.BlockSpec((tb, W_B), lambda m: (m, 1)), cur(8), prev(8), cur(9), prev(9),
                  pl.BlockSpec((8, 128), lambda n: (0, 0)), pl.BlockSpec((1, W_B), lambda n: (0, 0))],
        out_specs=[out, out], out_shape=[SDS((s, W_B), BF16), SDS((s, W_B), F32)],
        compiler_params=_cp("parallel"),
    )(proj, proj, proj, proj, proj, sinks8, gg)


def _attn_bwd(dy, proj, ob, sinks8, gg):
    s = proj.shape[0]
    nb = ATT_NB_BWD

    def body(dya_ref, dyb_ref, q_ref, kc_ref, kp_ref, vc_ref, vp_ref, ob_ref, sink_ref, gg_ref,
             dq_ref, dcur_ref, dprev_ref, dsink_ref, dgg_ref):
        first = pl.program_id(0) == 0
        kws, vws = _attn_windows(kc_ref, kp_ref, nb), _attn_windows(vc_ref, vp_ref, nb)
        dyn = jnp.concatenate([dya_ref[...], dyb_ref[...]], axis=1)
        dob, dggr = _rms_bwd_rows(ob_ref[...], gg_ref[...], dyn)
        _acc(dgg_ref, first, jnp.sum(dggr, axis=0, keepdims=True))
        r8 = _row_iota((8, 128))
        dsk = jnp.zeros((8, 128), F32)
        for b in range(nb):
            rows = pl.ds(b * BLK, BLK)
            qs = _attn_stack_all(q_ref.at[rows, :])
            p, psink = _attn_probs(qs, kws[b], nb * pl.program_id(0) + b, sink_ref)
            dosb = _attn_stack_all(dob[b * BLK:(b + 1) * BLK, :]).astype(BF16)
            dp = lax.dot_general(dosb, vws[b], NT, preferred_element_type=F32)
            dd = jnp.sum(p * dp, axis=-1, keepdims=True)
            dsb = (p * (dp - dd) * SCALE).astype(BF16)
            dsink_rows = -psink * dd
            for h in range(NQ):
                dsk = dsk + jnp.where(r8 == h, jnp.sum(dsink_rows[h * BLK:(h + 1) * BLK, :], axis=0, keepdims=True), 0.0)
            _attn_unstack_all(jnp.dot(dsb, kws[b], preferred_element_type=F32), dq_ref.at[rows, :])
            dkw = lax.dot_general(dsb, qs.astype(BF16), TN, preferred_element_type=F32)
            dvw = lax.dot_general(p.astype(BF16), dosb, TN, preferred_element_type=F32)
            dprev_ref[rows, 0:128] = dkw[0:BLK, :]
            dprev_ref[rows, 128:256] = dvw[0:BLK, :]
            dcur_ref[rows, 0:128] = dkw[BLK:2 * BLK, :]
            dcur_ref[rows, 128:256] = dvw[BLK:2 * BLK, :]
        _acc(dsink_ref, first, dsk)

    tb = nb * BLK
    cur = lambda c: pl.BlockSpec((tb, 128), lambda m, c=c: (m, c))
    prev = lambda c: pl.BlockSpec((BLK, 128), lambda m, c=c: (jnp.maximum(nb * m - 1, 0), c))
    wide = pl.BlockSpec((tb, W_B), lambda m: (m, 0))
    half = pl.BlockSpec((tb, 256), lambda m: (m, 0))
    return pl.pallas_call(
        body, name="attn_bwd", grid=(s // tb,),
        in_specs=[pl.BlockSpec((tb, 256), lambda m: (m, 1)), pl.BlockSpec((tb, 256), lambda m: (m, 2)),
                  pl.BlockSpec((tb, W_B), lambda m: (m, 1)), cur(8), prev(8), cur(9), prev(9), wide,
                  pl.BlockSpec((8, 128), lambda n: (0, 0)), pl.BlockSpec((1, W_B), lambda n: (0, 0))],
        out_specs=[wide, half, half, pl.BlockSpec((8, 128), lambda n: (0, 0)), pl.BlockSpec((1, W_B), lambda n: (0, 0))],
        out_shape=[SDS((s, W_B), F32), SDS((s, 256), F32), SDS((s, 256), F32), SDS((8, 128), F32), SDS((1, W_B), F32)],
        compiler_params=_cp("arbitrary"),
    )(dy, dy, proj, proj, proj, proj, proj, ob, sinks8, gg)


def _ln_parts(y1, eps=LN_EPS):
    mu = jnp.mean(y1, axis=-1, keepdims=True)
    xc = y1 - mu
    rstd = lax.rsqrt(jnp.mean(xc * xc, axis=-1, keepdims=True) + eps)
    return xc * rstd, rstd


def _conf_fwd(proj, cw, cb, lg, lb, gg, tc):
    s = proj.shape[0]
    pad = 32

    def body(ac_ref, gc_ref, ap_ref, gp_ref, cw_ref, cb_ref, lg_ref, lb_ref, gg_ref, yn_ref, y1_ref, ys_ref, sh_ref):
        i = pl.program_id(0)
        tail = ap_ref[tc - pad:tc, :] * _sig(gp_ref[tc - pad:tc, :])
        ys_ref[0:pad, :] = jnp.where(i > 0, tail, 0.0)
        ys_ref[pad:pad + tc, :] = ac_ref[...] * _sig(gc_ref[...])
        _fill_shifted(ys_ref, sh_ref)
        y1 = cb_ref[...]
        for j in range(CONV_K):
            y1 = y1 + cw_ref[j:j + 1, :] * _shifted_rows(ys_ref, sh_ref, pad - (CONV_K - 1) + j, tc)
        y1_ref[...] = y1
        xh, _ = _ln_parts(y1)
        yl = xh * lg_ref[...] + lb_ref[...]
        yc = yl * _sig(yl)
        yn_ref[...] = (yc * _rsq(yc, NORM_EPS) * gg_ref[...]).astype(BF16)

    cur = lambda c: pl.BlockSpec((tc, W_C), lambda i, c=c: (i, c))
    prev = lambda c: pl.BlockSpec((tc, W_C), lambda i, c=c: (jnp.maximum(i - 1, 0), c))
    full = lambda a: pl.BlockSpec(a.shape, lambda i: (0,) * a.ndim)
    params = [cw, cb, lg, lb, gg]
    out = pl.BlockSpec((tc, W_C), lambda i: (i, 0))
    return pl.pallas_call(
        body, name="conf_fwd", grid=(s // tc,),
        in_specs=[cur(5), cur(6), prev(5), prev(6)] + [full(a) for a in params],
        out_specs=[out, out], out_shape=[SDS((s, W_C), BF16), SDS((s, W_C), F32)],
        scratch_shapes=[pltpu.VMEM((tc + pad, W_C), F32), pltpu.VMEM((8, tc + pad, W_C), F32)],
        compiler_params=_cp("parallel"),
    )(proj, proj, proj, proj, *params)


def _conf_bwd(dy, proj, y1, cw, cb, lg, lb, gg, tc):
    s = proj.shape[0]
    nc = s // tc
    pad = 32

    def body(dy_ref, ac_ref, gc_ref, ap_ref, gp_ref, y1_ref, cw_ref, cb_ref, lg_ref, lb_ref, gg_ref,
             dp_ref, dcw_ref, dcb_ref, dlg_ref, dlb_ref, dgg_ref, ys_ref, ds_ref, nx_ref, ysh_ref, dsh_ref):
        step = pl.program_id(0)
        i = nc - 1 - step
        first = step == 0

        @pl.when(first)
        def _():
            nx_ref[...] = jnp.zeros_like(nx_ref)

        a = ac_ref[...]
        sg = _sig(gc_ref[...])
        tail = ap_ref[tc - pad:tc, :] * _sig(gp_ref[tc - pad:tc, :])
        ys_ref[0:pad, :] = jnp.where(i > 0, tail, 0.0)
        ys_ref[pad:pad + tc, :] = a * sg
        xh, rstd = _ln_parts(y1_ref[...])
        yl = xh * lg_ref[...] + lb_ref[...]
        sl = _sig(yl)
        yc = yl * sl
        dyc, dggr = _rms_bwd_rows(yc, gg_ref[...], dy_ref[...])
        _acc(dgg_ref, first, jnp.sum(dggr, axis=0, keepdims=True))
        dyl = dyc * sl * (1.0 + yl * (1.0 - sl))
        _acc(dlg_ref, first, jnp.sum(dyl * xh, axis=0, keepdims=True))
        _acc(dlb_ref, first, jnp.sum(dyl, axis=0, keepdims=True))
        dxh = dyl * lg_ref[...]
        dy1 = rstd * (dxh - jnp.mean(dxh, axis=-1, keepdims=True) - xh * jnp.mean(dxh * xh, axis=-1, keepdims=True))
        _acc(dcb_ref, first, jnp.sum(dy1, axis=0, keepdims=True))
        r32 = _row_iota((32, W_C))
        dcw = jnp.zeros((32, W_C), F32)
        _fill_shifted(ys_ref, ysh_ref)
        for j in range(CONV_K):
            tap = jnp.sum(dy1 * _shifted_rows(ys_ref, ysh_ref, pad - (CONV_K - 1) + j, tc), axis=0, keepdims=True)
            dcw = dcw + jnp.where(r32 == j, tap, 0.0)
        _acc(dcw_ref, first, dcw)
        ds_ref[0:tc, :] = dy1
        ds_ref[tc:tc + pad, :] = nx_ref[...]
        _fill_shifted(ds_ref, dsh_ref)
        dy0 = None
        for j in range(CONV_K):
            term = cw_ref[j:j + 1, :] * _shifted_rows(ds_ref, dsh_ref, CONV_K - 1 - j, tc)
            dy0 = term if dy0 is None else dy0 + term
        dp_ref[:, 0:W_C] = dy0 * sg
        dp_ref[:, W_C:2 * W_C] = dy0 * a * sg * (1.0 - sg)
        nx_ref[...] = dy1[0:pad, :]

    rev = lambda c: pl.BlockSpec((tc, W_C), lambda t, c=c: (nc - 1 - t, c))
    prev = lambda c: pl.BlockSpec((tc, W_C), lambda t, c=c: (jnp.maximum(nc - 2 - t, 0), c))
    full = lambda a: pl.BlockSpec(a.shape, lambda t: (0,) * a.ndim)
    params = [cw, cb, lg, lb, gg]
    vec = SDS((1, W_C), F32)
    outs = [SDS((s, 2 * W_C), F32), SDS((32, W_C), F32), vec, vec, vec, vec]
    return pl.pallas_call(
        body, name="conf_bwd", grid=(nc,),
        in_specs=[rev(3), rev(5), rev(6), prev(5), prev(6), rev(0)] + [full(a) for a in params],
        out_specs=[pl.BlockSpec((tc, 2 * W_C), lambda t: (nc - 1 - t, 0))]
        + [pl.BlockSpec(o.shape, lambda t: (0, 0)) for o in outs[1:]],
        out_shape=outs,
        scratch_shapes=[pltpu.VMEM((tc + pad, W_C), F32), pltpu.VMEM((tc + pad, W_C), F32), pltpu.VMEM((pad, W_C), F32),
                        pltpu.VMEM((8, tc + pad, W_C), F32), pltpu.VMEM((8, tc + pad, W_C), F32)],
        compiler_params=_cp("arbitrary"),
    )(dy, proj, proj, proj, proj, y1, *params)


def _assemble_dproj(dlru, dq, dcur, dprev, dconf):
    s = dq.shape[0]
    nb = s // BLK

    def body(dl_ref, dq_ref, dc_ref, dn_ref, df_ref, o_ref):
        n = pl.program_id(0)
        o_ref[:, 0:512] = dl_ref[...].astype(BF16)
        o_ref[:, 512:1024] = dq_ref[...].astype(BF16)
        o_ref[:, 1024:1280] = (dc_ref[...] + jnp.where(n < nb - 1, dn_ref[...], 0.0)).astype(BF16)
        o_ref[:, 1280:1792] = df_ref[...].astype(BF16)

    wide = pl.BlockSpec((BLK, 512), lambda n: (n, 0))
    return pl.pallas_call(
        body, name="assemble_dproj", grid=(nb,),
        in_specs=[wide, wide, pl.BlockSpec((BLK, 256), lambda n: (n, 0)),
                  pl.BlockSpec((BLK, 256), lambda n: (jnp.minimum(n + 1, nb - 1), 0)), wide],
        out_specs=pl.BlockSpec((BLK, P_IN), lambda n: (n, 0)), out_shape=SDS((s, P_IN), BF16),
        compiler_params=_cp("parallel"),
    )(dlru, dq, dcur, dprev, dconf)


def _loss_grad(y, t, tm):
    s = y.shape[0]

    def body(y_ref, t_ref, dy_ref, l_ref):
        err = y_ref[...] - t_ref[...]
        dy_ref[...] = err * (1.0 / D)
        _acc(l_ref, pl.program_id(0) == 0, jnp.sum(err * err, axis=0, keepdims=True))

    row = pl.BlockSpec((tm, D), lambda i: (i, 0))
    return pl.pallas_call(
        body, name="loss_grad", grid=(s // tm,), in_specs=[row, row],
        out_specs=[row, pl.BlockSpec((1, D), lambda i: (0, 0))],
        out_shape=[SDS((s, D), F32), SDS((1, D), F32)], compiler_params=_cp("arbitrary"),
    )(y, t)


def _block_diag(w):
    rows = [jnp.concatenate([w[h] if k == h else jnp.zeros((64, 64), w.dtype) for k in range(4)], axis=1) for h in range(4)]
    return jnp.concatenate(rows, axis=0)


def _diag_blocks(m):
    return jnp.stack([m[64 * h:64 * (h + 1), 64 * h:64 * (h + 1)] for h in range(4)])


def _layer_params(small, l):
    v = lambda name: small[name][l].reshape(1, -1)
    gg = small["group_g"][l]
    return dict(
        ffn1_pre=v("ffn1_pre_g"), ffn1_post=v("ffn1_post_g"), mix_pre=v("mix_pre_g"), mix_post=v("mix_post_g"),
        ffn2_pre=v("ffn2_pre_g"), ffn2_post=v("ffn2_post_g"), lru_cb=v("lru_conv_b"),
        wa=_block_diag(small["lru_w_a"][l]).astype(BF16), ba=v("lru_b_a"),
        wx=_block_diag(small["lru_w_x"][l]).astype(BF16), bx=v("lru_b_x"), lam=v("lru_lambda"),
        sinks8=jnp.broadcast_to(small["attn_sinks"][l][:, None], (NQ, 128)),
        conv_b=v("conv_b"), ln_g=v("conv_ln_g"), ln_b=v("conv_ln_b"),
        gg_a=gg[0:W_A].reshape(1, -1), gg_b=gg[W_A:W_A + W_B].reshape(1, -1), gg_c=gg[W_A + W_B:].reshape(1, -1),
    )


def _forward_layer(x, weights, p, tiles, deps=()):
    _, mm, _, tc = tiles
    big = dict(weights("ffn1_gu", x))
    p = dict(p)
    sv = dict(x0=x)
    h1, g1, u1, a1 = _ffn_up(x, p["ffn1_pre"], big["ffn1_w_gu"], 0, mm, deps)
    big.update(weights("ffn1_down", a1))
    z1, x = _mm_rms_res(a1, big["ffn1_w_down"], 0, x, p["ffn1_post"], 0.5, mm, DFF, "ffn_down")
    sv.update(h1=h1, g1=g1, u1=u1, a1=a1, z1=z1, x1=x)
    big.update(weights("mix", x))
    p.update(lru_cw=big.pop("lru_conv_w"), conv_w=big.pop("conv_w"))
    hn, proj = _proj(x, p["mix_pre"], big["w_in"], 0, mm)
    yn_a, hl = _lru_fwd(proj, p["lru_cw"], p["lru_cb"], p["wa"], p["ba"], p["wx"], p["bx"], p["lam"], p["gg_a"], tc)
    yn_b, ob = _attn_fwd(proj, p["sinks8"], p["gg_b"])
    yn_c, y1 = _conf_fwd(proj, p["conv_w"], p["conv_b"], p["ln_g"], p["ln_b"], p["gg_c"], tc)
    ycat = jnp.concatenate([yn_a, yn_b, yn_c], axis=1)
    zo, x = _mm_rms_res(ycat, big["w_out"], 0, x, p["mix_post"], 1.0, mm, D, "mix_out")
    sv.update(hn=hn, proj=proj, hl=hl, ob=ob, y1=y1, ycat=ycat, zo=zo, x2=x)
    big.update(weights("ffn2", x))
    h2, g2, u2, a2 = _ffn_up(x, p["ffn2_pre"], big["ffn2_w_gu"], 0, mm)
    z2, x = _mm_rms_res(a2, big["ffn2_w_down"], 0, x, p["ffn2_post"], 0.5, mm, DFF, "ffn_down")
    sv.update(h2=h2, g2=g2, u2=u2, a2=a2, z2=z2, p=p, big=big)
    return x, sv


def _grad_buffers():
    empty = lambda *shape: lax.empty(shape, F32)
    return dict(ffn1_w_gu=empty(1, NSHARD, D, FH), ffn2_w_gu=empty(1, NSHARD, D, FH), ffn1_w_down=empty(1, 1, DFF, D),
                ffn2_w_down=empty(1, 1, DFF, D), w_in=empty(1, 1, D, P_IN), w_out=empty(1, 1, D, D))


def _backward_layer(dx, sv, bufs, tiles, stage):
    p, big = sv["p"], sv["big"]
    tm, mm, dw, tc = tiles
    gr = {}

    def ffn_bwd(dx, which, xin, h, g, u, a, z, pre, post, deps):
        dz, dpost = _rms_bwd(dx, z, post, 0.5, tm, "ffn_post_bwd", deps)
        dg, du = _ffn_bwd_mid(dz, big[which + "_w_down"], 0, g, u, mm)
        bufs[which + "_w_down"] = _mm_tn_into(bufs[which + "_w_down"], a, dz, 0, 0, FH, D, dw, "dw_down")
        bufs[which + "_w_gu"] = _mm_tn_into(bufs[which + "_w_gu"], h, dg, 0, 0, D, FH, dw, "dw_gate")
        bufs[which + "_w_gu"] = _mm_tn_into(bufs[which + "_w_gu"], h, du, 0, 2, D, FH, dw, "dw_up")
        deps = stage({n: bufs[n] for n in (which + "_w_gu", which + "_w_down")}, bufs[which + "_w_gu"])
        dxn, dpre = _ffn_bwd_dh(dg, du, big[which + "_w_gu"], 0, xin, pre, dx, mm, deps)
        return dxn, dpre, dpost

    dx, gr["ffn2_pre_g"], gr["ffn2_post_g"] = ffn_bwd(dx, "ffn2", sv["x2"], sv["h2"], sv["g2"], sv["u2"], sv["a2"],
                                                      sv["z2"], p["ffn2_pre"], p["ffn2_post"], ())
    do, gr["mix_post_g"] = _rms_bwd(dx, sv["zo"], p["mix_post"], 1.0, tm, "mix_post_bwd")
    bufs["w_out"] = _mm_tn_into(bufs["w_out"], sv["ycat"], do, 0, 0, D, D, dw, "dw_out")
    dy = _mm_nt(do, big["w_out"], 0, mm, "mix_dy")
    proj = sv["proj"]
    (dlru, dcw, gr["lru_conv_b"], dwa, gr["lru_b_a"], dwx, gr["lru_b_x"], gr["lru_lambda"], dgg_a) = _lru_bwd(
        dy, proj, sv["hl"], p["lru_cw"], p["lru_cb"], p["wa"], p["ba"], p["wx"], p["bx"], p["lam"], p["gg_a"], tc)
    dq, dcur, dprev, dsk, dgg_b = _attn_bwd(dy, proj, sv["ob"], p["sinks8"], p["gg_b"])
    dconf, dconvw, gr["conv_b"], gr["conv_ln_g"], gr["conv_ln_b"], dgg_c = _conf_bwd(
        dy, proj, sv["y1"], p["conv_w"], p["conv_b"], p["ln_g"], p["ln_b"], p["gg_c"], tc)
    dproj = _assemble_dproj(dlru, dq, dcur, dprev, dconf)
    bufs["w_in"] = _mm_tn_into(bufs["w_in"], sv["hn"], dproj, 0, 0, D, P_IN, dw, "dw_in")
    dx, gr["mix_pre_g"] = _mm_nt_rmsbwd(dproj, big["w_in"], 0, sv["x1"], p["mix_pre"], dx, mm)
    gr["lru_conv_w"] = dcw[0:LRU_K]
    gr["lru_w_a"] = _diag_blocks(dwa)
    gr["lru_w_x"] = _diag_blocks(dwx)
    gr["attn_sinks"] = dsk[:, 0]
    gr["conv_w"] = dconvw[0:CONV_K]
    gr["group_g"] = jnp.concatenate([dgg_a, dgg_b, dgg_c], axis=1)
    dx, gr["ffn1_pre_g"], gr["ffn1_post_g"] = ffn_bwd(dx, "ffn1", sv["x0"], sv["h1"], sv["g1"], sv["u1"], sv["a1"],
                                                      sv["z1"], p["ffn1_pre"], p["ffn1_post"],
                                                      stage({n: bufs[n] for n in ("w_in", "w_out")}, dx))
    return dx, gr


def _tiles(s):
    return min(1024, s), min(1024, s), min(2048, s), min(512, s // 2)


HBM_SPEC = pl.BlockSpec(memory_space=pltpu.HBM)
SEM_SPEC = pl.BlockSpec(memory_space=pltpu.SEMAPHORE)
EFFECT = pltpu.SideEffectType.DATAFLOW_SIDE_EFFECTING


def _place():
    x, y, c = lax.axis_index("x"), lax.axis_index("y"), lax.axis_index("c")
    return x, y, c, [(1 - x, y), (x, 1 - y), (1 - x, 1 - y)]


def _rcopy(src, dst, send_sems, recv_sems, k, to):
    return pltpu.make_async_remote_copy(src_ref=src, dst_ref=dst, send_sem=send_sems.at[k], recv_sem=recv_sems.at[k],
                                        device_id=to, device_id_type=MESH)


def _half(rows, which):
    return pl.ds(which * (rows // 2), rows // 2)


def _place_shard(w, l, p_idx, dtype):
    _, rows, cols = w.shape
    tr = _rows_per_block(rows, cols, 16, SUM_BLOCK_ELEMS) if rows % 16 == 0 else rows

    def body(p_ref, buf_ref, w_ref, o_ref):
        o_ref[...] = w_ref[...].astype(dtype)

    spec = pltpu.PrefetchScalarGridSpec(
        num_scalar_prefetch=1, grid=(rows // tr,),
        in_specs=[ANY, pl.BlockSpec((None, tr, cols), lambda i, pr: (l, i, 0))],
        out_specs=pl.BlockSpec((None, None, tr, cols), lambda i, pr: (0, pr[0], i, 0)))
    shape = (1, NSHARD, rows, cols)
    return pl.pallas_call(body, name="place_shard", grid_spec=spec, out_shape=SDS(shape, dtype),
                          input_output_aliases={1: 0}, compiler_params=_cp("parallel"),
                          )(p_idx, lax.empty(shape, dtype), w)


def _gather_two_level(bufs, n_halved):
    n = len(bufs)

    def body(*refs):
        outs = refs[n:2 * n]
        send_sems, recv_sems = refs[2 * n:]
        x, y, c, chips = _place()
        p = 2 * x + y
        me, sibling = (x, y, c), (x, y, 1 - c)

        def blk(a, q, half):
            return outs[a].at[0, q, _half(outs[a].shape[2], half)] if a < n_halved else outs[a].at[0, q]

        def cp(a, k, q, half, to):
            return _rcopy(blk(a, q, half), blk(a, q, half), send_sems, recv_sems, 6 * a + k, to)

        first = [cp(a, j, p, c, (*chip, c)) for a in range(n) for j, chip in enumerate(chips)]
        for d in first:
            d.start()
        passed = []
        for a in range(n):
            for j, chip in enumerate(chips):
                q = 2 * chip[0] + chip[1]
                cp(a, j, q, c, me).wait_recv()
                if a < n_halved:
                    passed.append(cp(a, 3 + j, q, c, sibling))
                    passed[-1].start()
        for a in range(n_halved):
            for j, chip in enumerate(chips):
                cp(a, 3 + j, 2 * chip[0] + chip[1], 1 - c, me).wait_recv()
        for d in first + passed:
            d.wait_send()

    return pl.pallas_call(
        body, name="gather_layer0", in_specs=[ANY] * n, out_specs=[ANY] * n,
        out_shape=[SDS(b.shape, b.dtype) for b in bufs], input_output_aliases={a: a for a in range(n)},
        scratch_shapes=[pltpu.SemaphoreType.DMA((6 * n,)), pltpu.SemaphoreType.DMA((6 * n,))],
    )(*bufs)


def _run_plans(plans, refs, send_sems, recv_sems):
    cps, b0, s0 = [], 0, 0
    for plan, nb, ns in plans:
        cps += plan(refs[b0:b0 + nb], send_sems, recv_sems, s0)
        b0, s0 = b0 + nb, s0 + ns
    return cps


def _exchange(name, bufs, plans):
    n = len(bufs)
    nsem = sum(ns for _, _, ns in plans)

    def body(*refs):
        cps = _run_plans(plans, refs[n:2 * n], refs[2 * n], refs[2 * n + 1])
        for cp in cps:
            cp.start()
        for cp in cps:
            cp.wait()

    return pl.pallas_call(
        body, name=name, in_specs=[ANY] * n, out_specs=[ANY] * n, out_shape=[SDS(b.shape, b.dtype) for b in bufs],
        input_output_aliases={a: a for a in range(n)},
        scratch_shapes=[pltpu.SemaphoreType.DMA((nsem,)), pltpu.SemaphoreType.DMA((nsem,))],
    )(*bufs)


def _exchange_start(name, bufs, plans, deps=()):
    n = len(bufs)
    nsem = sum(ns for _, _, ns in plans)
    deps = list(deps)
    first_out = n + len(deps)

    def body(*refs):
        for cp in _run_plans(plans, refs[:n], refs[first_out], refs[first_out + 1]):
            cp.start()
        token = refs[first_out + 2 + n]
        token[...] = jnp.zeros_like(token)

    outs = pl.pallas_call(
        body, name=name,
        out_shape=(pltpu.SemaphoreType.DMA((nsem,)), pltpu.SemaphoreType.DMA((nsem,)),
                   *[pltpu.HBM(b.shape, b.dtype) for b in bufs], SDS((8, 128), F32)),
        in_specs=[HBM_SPEC] * n + [ANY] * len(deps),
        out_specs=(SEM_SPEC, SEM_SPEC, *[HBM_SPEC] * n, pl.BlockSpec(memory_space=pltpu.VMEM)),
        input_output_aliases={a: 2 + a for a in range(n)},
        compiler_params=pltpu.CompilerParams(has_side_effects=EFFECT),
    )(*[pltpu.with_memory_space_constraint(b, pltpu.HBM) for b in bufs], *deps)
    return outs[0], outs[1], list(outs[2:2 + n]), outs[2 + n]


def _exchange_wait(name, send_sems, recv_sems, bufs, plans, after):
    n = len(bufs)

    def body(*refs):
        for cp in _run_plans(plans, refs[:n], refs[n], refs[n + 1]):
            cp.wait_send()
            cp.wait_recv()

    return pl.pallas_call(
        body, name=name, out_shape=[pltpu.HBM(b.shape, b.dtype) for b in bufs],
        in_specs=[HBM_SPEC] * n + [SEM_SPEC, SEM_SPEC, ANY], out_specs=[HBM_SPEC] * n,
        input_output_aliases={a: a for a in range(n)},
        compiler_params=pltpu.CompilerParams(has_side_effects=EFFECT),
    )(*bufs, send_sems, recv_sems, after)


def _plan_gather(refs, send_sems, recv_sems, base):
    x, y, c, chips = _place()
    p = 2 * x + y
    return [_rcopy(r.at[0, p], r.at[0, p], send_sems, recv_sems, base + 3 * a + j, (*chip, c))
            for a, r in enumerate(refs) for j, chip in enumerate(chips)]


def _plan_pair_exchange(refs, send_sems, recv_sems, base):
    x, y, c, _ = _place()
    n = len(refs) // 2
    return [_rcopy(refs[a].at[:, _half(refs[a].shape[1], 1 - c)], refs[n + a], send_sems, recv_sems, base + a,
                   (x, y, 1 - c)) for a in range(n)]


def _plan_chip_exchange(refs, send_sems, recv_sems, base):
    x, y, c, chips = _place()
    n = len(refs) // 2
    return [_rcopy(refs[a].at[2 * chip[0] + chip[1]], refs[n + a].at[j], send_sems, recv_sems, base + 3 * a + j,
                   (*chip, c)) for a in range(n) for j, chip in enumerate(chips)]


def _plan_pair_share(refs, send_sems, recv_sems, base):
    x, y, c, _ = _place()
    return [_rcopy(r.at[_half(r.shape[0], c)], r.at[_half(r.shape[0], c)], send_sems, recv_sems, base + a,
                   (x, y, 1 - c)) for a, r in enumerate(refs)]


def _plan_small_gather(refs, send_sems, recv_sems, base):
    x, y, c, _ = _place()
    me = 4 * x + 2 * y + c
    cps = []
    for m in range(1, NDEV):
        peer = (1 - x if m & 4 else x, 1 - y if m & 2 else y, 1 - c if m & 1 else c)
        cps.append(_rcopy(refs[0], refs[1].at[me], send_sems, recv_sems, base + m - 1, peer))
    return cps


def _sum_small(buf, gathered):
    def body(buf_ref, g_ref, o_ref):
        x, y, c, _ = _place()
        me = 4 * x + 2 * y + c
        total = jnp.where(me == 0, buf_ref[...], g_ref[0])
        for dev in range(1, NDEV):
            total = total + jnp.where(me == dev, buf_ref[...], g_ref[dev])
        o_ref[...] = total

    vm = pl.BlockSpec(memory_space=pltpu.VMEM)
    return pl.pallas_call(body, name="sum_small", in_specs=[vm, vm], out_specs=vm, out_shape=SDS(buf.shape, F32),
                          compiler_params=pltpu.CompilerParams(vmem_limit_bytes=VMEM_LIMIT))(buf, gathered)


BLOCK_ELEMS = 512 * 1024
SUM_BLOCK_ELEMS = 1024 * 1024


def _rows_per_block(rows, cols, mult, limit=BLOCK_ELEMS):
    best = None
    for tr in range(mult, rows + 1, mult):
        if rows % tr == 0 and tr * cols <= limit:
            best = tr
    assert best is not None, (rows, cols)
    return best


def _pair_sum(g, r, c_idx):
    nq, rows, cols = g.shape
    half = rows // 2
    tr = _rows_per_block(half, cols, 16, SUM_BLOCK_ELEMS)
    nb = half // tr

    def body(c_ref, g_ref, r_ref, t_ref):
        t_ref[...] = (g_ref[...] + r_ref[...]).astype(BF16)

    blk = pl.BlockSpec((None, tr, cols), lambda q, i, cr: (q, i, 0))
    spec = pltpu.PrefetchScalarGridSpec(
        num_scalar_prefetch=1, grid=(nq, nb),
        in_specs=[pl.BlockSpec((None, tr, cols), lambda q, i, cr: (q, cr[0] * nb + i, 0)), blk], out_specs=blk)
    return pl.pallas_call(body, name="grad_pair_sum", grid_spec=spec, out_shape=SDS((nq, half, cols), BF16),
                          compiler_params=_cp("parallel", "parallel"))(c_idx, g, r)


def _chip_sum(g, r, rr, cp_idx):
    _, rows, cols = g.shape
    half = rows // 2
    tr = _rows_per_block(half, cols, 16, SUM_BLOCK_ELEMS)
    nb = half // tr

    def body(cp_ref, buf_ref, g_ref, r_ref, rr_ref, o_ref):
        o_ref[...] = ((g_ref[...] + r_ref[...]) + rr_ref[0].astype(F32) + rr_ref[1].astype(F32) + rr_ref[2].astype(F32))

    spec = pltpu.PrefetchScalarGridSpec(
        num_scalar_prefetch=1, grid=(nb,),
        in_specs=[ANY, pl.BlockSpec((None, tr, cols), lambda i, cp: (cp[1], cp[0] * nb + i, 0)),
                  pl.BlockSpec((None, tr, cols), lambda i, cp: (cp[1], i, 0)),
                  pl.BlockSpec((3, tr, cols), lambda i, cp: (0, i, 0))],
        out_specs=pl.BlockSpec((tr, cols), lambda i, cp: (cp[0] * nb + i, 0)))
    return pl.pallas_call(body, name="grad_chip_sum", grid_spec=spec, out_shape=SDS((rows, cols), F32),
                          input_output_aliases={1: 0}, compiler_params=_cp("parallel"),
                          )(cp_idx, lax.empty((rows, cols), F32), g, r, rr)


def _adamw_math(w, g, m, v):
    mn = ADAM_B1 * m + (1.0 - ADAM_B1) * g
    vn = ADAM_B2 * v + (1.0 - ADAM_B2) * (g * g)
    m_hat = mn / (1.0 - ADAM_B1 ** ADAM_STEP)
    v_hat = vn / (1.0 - ADAM_B2 ** ADAM_STEP)
    return -ADAM_LR * (m_hat / (jnp.sqrt(v_hat) + ADAM_EPS) + ADAM_WD * w), mn, vn


def _adamw_layer(w, g, m, v, l, outs, deps=()):
    _, rows, cols = w.shape
    tr = _rows_per_block(rows, cols, 8)
    deps = list(deps)

    def body(*refs):
        w_ref, g_ref, m_ref, v_ref = refs[4:8]
        go_ref, d_ref, mo_ref, vo_ref = refs[8 + len(deps):]
        gg = g_ref[...]
        go_ref[...] = gg
        d_ref[...], mo_ref[...], vo_ref[...] = _adamw_math(w_ref[...], gg, m_ref[...], v_ref[...])

    blk = pl.BlockSpec((None, tr, cols), lambda i: (l, i, 0))
    return pl.pallas_call(
        body, name="adamw_layer", grid=(rows // tr,),
        in_specs=[ANY] * 4 + [blk, pl.BlockSpec((tr, cols), lambda i: (i, 0)), blk, blk] + [ANY] * len(deps),
        out_specs=[blk] * 4, out_shape=[SDS(w.shape, F32)] * 4, input_output_aliases={k: k for k in range(4)},
        compiler_params=_cp("parallel"))(*outs, w, g, m, v, *deps)


def _adamw_small(ws, gs, ms, vs, deps=()):
    n = len(ws)
    deps = list(deps)

    def body(*refs):
        refs = refs[:4 * n] + refs[4 * n + len(deps):]
        w, g, m, v, d_out, m_out, v_out = (refs[k * n:(k + 1) * n] for k in range(7))
        for k in range(n):
            d_out[k][...], m_out[k][...], v_out[k][...] = _adamw_math(w[k][...], g[k][...], m[k][...], v[k][...])

    vm = pl.BlockSpec(memory_space=pltpu.VMEM)
    outs = pl.pallas_call(body, name="adamw_small", in_specs=[vm] * (4 * n) + [ANY] * len(deps), out_specs=[vm] * (3 * n),
                          out_shape=[SDS(w.shape, F32) for w in ws] * 3,
                          compiler_params=pltpu.CompilerParams(vmem_limit_bytes=VMEM_LIMIT))(*ws, *gs, *ms, *vs, *deps)
    return outs[:n], outs[n:2 * n], outs[2 * n:]


_WEIGHTS = ["ffn1_pre_g", "ffn1_w_gu", "ffn1_w_down", "ffn1_post_g", "mix_pre_g", "w_in", "lru_conv_w", "lru_conv_b",
            "lru_w_a", "lru_b_a", "lru_w_x", "lru_b_x", "lru_lambda", "attn_sinks", "conv_w", "conv_b", "conv_ln_g",
            "conv_ln_b", "group_g", "w_out", "mix_post_g", "ffn2_pre_g", "ffn2_w_gu", "ffn2_w_down", "ffn2_post_g"]
_INPUTS = ["x"] + _WEIGHTS + ["loss_target"] + ["m_" + n for n in _WEIGHTS] + ["v_" + n for n in _WEIGHTS]
_BIG = ["ffn1_w_gu", "ffn1_w_down", "w_in", "w_out", "ffn2_w_gu", "ffn2_w_down"]
_SMALL_SHARDED = ["lru_conv_w", "conv_w"]
_SMALL_REPL = [n for n in _WEIGHTS if n not in _BIG and n not in _SMALL_SHARDED]

PACK_TILE = 8 * 128


def _pack(arrs):
    parts = []
    for a in arrs:
        flat = a.reshape(-1)
        parts.append(jnp.pad(flat, (0, -flat.shape[0] % PACK_TILE)).reshape(-1, 128))
    return jnp.concatenate(parts, axis=0)


def _unpack(buf, shapes):
    out, row = [], 0
    for shp in shapes:
        size = math.prod(shp)
        nrow = -(-size // PACK_TILE) * 8
        out.append(buf[row:row + nrow].reshape(-1)[:size].reshape(shp))
        row += nrow
    return out


def _unshard_cols(a):
    return a.transpose(0, 2, 1, 3).reshape(1, a.shape[2], NSHARD * a.shape[3])


_GROUPS = dict(ffn1_gu=["ffn1_w_gu"], ffn1_down=["ffn1_w_down"], mix=["w_in", "w_out", "lru_conv_w", "conv_w"],
               ffn2=["ffn2_w_gu", "ffn2_w_down"])


def _full_weights(group, gathered):
    g = dict(zip(_GROUPS[group], gathered))
    if group == "mix":
        return dict(w_in=_unshard_cols(g["w_in"]), w_out=g["w_out"].reshape(1, D, D),
                    lru_conv_w=_unshard_cols(g["lru_conv_w"])[0], conv_w=_unshard_cols(g["conv_w"])[0])
    return {n: (a.reshape(1, DFF, D) if n.endswith("w_down") else a) for n, a in g.items()}


def _by_shard(name, buf):
    if name.endswith("w_gu"):
        return buf[0]
    if name == "w_in":
        return buf.reshape(D, NSHARD, P_IN // NSHARD).transpose(1, 0, 2)
    return buf.reshape(NSHARD, buf.shape[2] // NSHARD, buf.shape[3])


class _Reducer:
    PLANS = (_plan_pair_exchange, _plan_chip_exchange, _plan_pair_share)

    def __init__(self, keys, gs, c_idx, cp_idx):
        self.keys, self.gs, self.c_idx, self.cp_idx = keys, gs, c_idx, cp_idx
        self.n = len(gs)
        self.step = 0
        self.result = None

    def inputs(self):
        n = self.n
        if self.step == 0:
            bufs = self.gs + [lax.empty((NSHARD, g.shape[1] // 2, g.shape[2]), F32) for g in self.gs]
        elif self.step == 1:
            ts = [_pair_sum(g, r, self.c_idx) for g, r in zip(self.gs, self.rs)]
            bufs = ts + [lax.empty((3,) + t.shape[1:], BF16) for t in ts]
        else:
            bufs = [_chip_sum(g, r, rr, self.cp_idx) for g, r, rr in zip(self.gs, self.rs, self.rrs)]
        return bufs, (self.PLANS[self.step], len(bufs), (n, 3 * n, n)[self.step])

    def absorb(self, done):
        n = self.n
        if self.step == 0:
            self.gs, self.rs = done[:n], done[n:]
        elif self.step == 1:
            self.rrs = done[n:]
        else:
            self.result = dict(zip(self.keys, done))
        self.step += 1


class _SmallGather:
    def __init__(self, buf):
        self.buf, self.step, self.result, self.gathered = buf, 0, {}, None

    def inputs(self):
        return [self.buf, jnp.zeros((NDEV,) + self.buf.shape, F32)], (_plan_small_gather, 2, NDEV - 1)

    def absorb(self, done):
        self.buf, self.gathered = done
        self.step = 3


class _ReducePipeline:
    def __init__(self, c_idx, cp_idx):
        self.c_idx, self.cp_idx = c_idx, cp_idx
        self.reducers, self.flying, self.calls = [], None, 0

    def add(self, layer, done):
        if done:
            keys = [(layer, n) for n in done]
            self.reducers.append(_Reducer(keys, [_by_shard(n, b) for n, b in done.items()], self.c_idx, self.cp_idx))

    def _next(self):
        active = [r for r in self.reducers if r.step < 3]
        bufs, plans = [], []
        for r in active:
            b, triple = r.inputs()
            bufs += b
            plans.append(triple)
        self.calls += 1
        return active, bufs, plans, "grad_exchange%d" % self.calls

    def _absorb(self, active, plans, done):
        at = 0
        for r, (_, nb, _) in zip(active, plans):
            r.absorb(done[at:at + nb])
            at += nb

    def _land(self, after):
        if self.flying is not None:
            active, plans, name, send_sems, recv_sems, bufs = self.flying
            self._absorb(active, plans, _exchange_wait(name + "_wait", send_sems, recv_sems, bufs, plans, after))
            self.flying = None

    def hook(self, after):
        self._land(after)
        active, bufs, plans, name = self._next()
        if not active:
            return []
        send_sems, recv_sems, bufs, token = _exchange_start(name + "_start", bufs, plans)
        self.flying = (active, plans, name, send_sems, recv_sems, bufs)
        return [token]

    def available(self):
        out = {}
        for r in self.reducers:
            if r.step == 3:
                out.update(r.result)
        return out

    def finish(self, after):
        self._land(after)
        while True:
            active, bufs, plans, name = self._next()
            if not active:
                break
            self._absorb(active, plans, _exchange(name, bufs, plans))
        out = {}
        for r in self.reducers:
            out.update(r.result)
        return out


def kernel(*args):
    d = dict(zip(_INPUTS, args, strict=True))
    xi, yi, ci = lax.axis_index("x"), lax.axis_index("y"), lax.axis_index("c")
    p = 2 * xi + yi
    c_idx = jnp.reshape(ci, (1,)).astype(jnp.int32)
    p_idx = jnp.reshape(p, (1,)).astype(jnp.int32)
    cp_idx = jnp.stack([ci, p]).astype(jnp.int32)
    x, target = d["x"][0], d["loss_target"][0]
    tiles = _tiles(x.shape[0])

    groups = [(l, grp) for l in range(DEPTH) for grp in _GROUPS]
    placed = {(l, grp): [_place_shard(d[n], l, p_idx, BF16 if n in _BIG else F32) for n in _GROUPS[grp]]
              for l, grp in groups}
    ready = {groups[0]: _gather_two_level(placed[groups[0]], len(placed[groups[0]]))}
    flying, tokens = {}, [ready[groups[0]][0]]
    for l, grp in groups[1:]:
        plans = [(_plan_gather, len(placed[l, grp]), 3 * len(placed[l, grp]))]
        send_sems, recv_sems, bufs, token = _exchange_start("gather_l%d_%s_start" % (l, grp), placed[l, grp], plans,
                                                             tokens[-1:])
        flying[l, grp] = (send_sems, recv_sems, bufs, plans)
        tokens.append(token)

    def weights_of(l):
        def weights(grp, after):
            if (l, grp) not in ready:
                send_sems, recv_sems, bufs, plans = flying[l, grp]
                ready[l, grp] = _exchange_wait("gather_l%d_%s_wait" % (l, grp), send_sems, recv_sems, bufs, plans, after)
            return _full_weights(grp, ready[l, grp])
        return weights

    small = {n: d[n] for n in _SMALL_REPL}
    x1, sv0 = _forward_layer(x, weights_of(0), _layer_params(small, 0), tiles, tokens[1:])
    x2, sv1 = _forward_layer(x1, weights_of(1), _layer_params(small, 1), tiles)
    dx, lcols = _loss_grad(x2, target, tiles[0])

    pipe = _ReducePipeline(c_idx, cp_idx)
    sgrads = [None] * DEPTH
    for l, sv in ((1, sv1), (0, sv0)):
        bufs = _grad_buffers()

        def stage(done, dx, l=l):
            pipe.add(l, done)
            return pipe.hook(dx)

        dx, sgrads[l] = _backward_layer(dx, sv, bufs, tiles, stage)
    grad_x = dx

    stacked = {n: jnp.stack([sgrads[l][n].reshape(d[n].shape[1:]) for l in range(DEPTH)]) for n in _SMALL_REPL}
    for n in _SMALL_SHARDED:
        stacked[n] = jnp.stack([sgrads[l][n] for l in range(DEPTH)])
    loss_part = jnp.pad((0.5 / D) * jnp.sum(lcols).reshape(1), (0, 127))
    order = _SMALL_REPL + _SMALL_SHARDED
    small_gather = _SmallGather(_pack([loss_part] + [stacked[n] for n in order]))
    pipe.reducers.append(small_gather)

    results = {n: tuple(lax.empty(d[n].shape, F32) for _ in range(4)) for n in _BIG}
    applied = set()

    def apply_ready(deps, last):
        for (l, n), g in pipe.available().items():
            if (l, n) not in applied:
                results[n] = _adamw_layer(d[n], g, d["m_" + n], d["v_" + n], l, results[n], deps)
                applied.add((l, n))
                last = results[n][1]
                deps = [last]
        return last

    last = apply_ready(pipe.hook(grad_x), grad_x)
    token = pipe.hook(last)
    summed = _unpack(_sum_small(small_gather.buf, small_gather.gathered), [(128,)] + [stacked[n].shape for n in order])
    loss = summed[0][0]
    grads = {}
    for n, g in zip(order, summed[1:]):
        if n in _SMALL_SHARDED:
            g = lax.dynamic_slice_in_dim(g, p * (g.shape[2] // NSHARD), g.shape[2] // NSHARD, axis=2)
        grads[n] = g
    delta, new_m, new_v = {}, {}, {}
    small_out = _adamw_small([d[n] for n in order], [grads[n] for n in order], [d["m_" + n] for n in order],
                             [d["v_" + n] for n in order], token)
    for out, res in zip((delta, new_m, new_v), small_out):
        out.update(zip(order, res))
    last = apply_ready([small_out[0][0]], small_out[0][0])
    pipe.finish(last)
    apply_ready((), last)
    for n in _BIG:
        grads[n], delta[n], new_m[n], new_v[n] = results[n]

    return (loss, grad_x[None], *[grads[n] for n in _WEIGHTS], *[delta[n] for n in _WEIGHTS],
            *[new_m[n] for n in _WEIGHTS], *[new_v[n] for n in _WEIGHTS])
```

```python
import functools
import math

import jax
import jax.numpy as jnp
from jax import lax
from jax.experimental import pallas as pl
from jax.experimental.pallas import tpu as pltpu

F32 = jnp.float32
BF16 = jnp.bfloat16
SDS = jax.ShapeDtypeStruct

D = 1024
DFF = 2816
FH = DFF // 2
DEPTH = 2
W_A = 256
W_B = 512
W_C = 256
NQ = 8
HD = 64
BLK = 128
ATT_NB_FWD = 1
ATT_NB_BWD = 4
P_IN = 1792
LRU_K = 4
CONV_K = 31
LRU_C = 8.0
NORM_EPS = 1e-6
LN_EPS = 1e-5
NEG_BIG = -1e30
SCALE = 1.0 / math.sqrt(HD)

ADAM_LR = 0.001
ADAM_B1 = 0.9
ADAM_B2 = 0.999
ADAM_EPS = 1e-08
ADAM_WD = 0.01
ADAM_STEP = 10

VMEM_LIMIT = 60 * 1024 * 1024
NSHARD = 4
NDEV = 8

TN = (((0,), (0,)), ((), ()))
NT = (((1,), (1,)), ((), ()))

MESH = pl.DeviceIdType.MESH
ANY = pl.BlockSpec(memory_space=pl.ANY)


def _cp(*sem):
    return pltpu.CompilerParams(dimension_semantics=sem if sem else None, vmem_limit_bytes=VMEM_LIMIT)


def _rsq(x, eps):
    return lax.rsqrt(jnp.mean(x * x, axis=-1, keepdims=True) + eps)


def _rms_bwd_rows(x, g, dy):
    r = _rsq(x, NORM_EPS)
    xh = x * r
    dyg = dy * g
    dx = r * (dyg - xh * jnp.mean(dyg * xh, axis=-1, keepdims=True))
    return dx, dy * xh


def _sig(x):
    return jax.nn.sigmoid(x)


def _ffn_up(x, pre_g, wgu, l, tm, deps=()):
    s = x.shape[0]
    deps = list(deps)

    def body(x_ref, g_ref, wg_ref, wu_ref, *rest):
        h_ref, go_ref, uo_ref, a_ref = rest[len(deps):]

        @pl.when(pl.program_id(1) == 0)
        def _():
            xf = x_ref[...]
            h_ref[...] = (xf * _rsq(xf, NORM_EPS) * g_ref[...]).astype(BF16)

        h = h_ref[...]
        gg = jnp.dot(h, wg_ref[...], preferred_element_type=F32)
        uu = jnp.dot(h, wu_ref[...], preferred_element_type=F32)
        sg = _sig(gg)
        silu = gg * sg
        go_ref[...] = (uu * (sg * (1.0 + gg * (1.0 - sg)))).astype(BF16)
        uo_ref[...] = silu.astype(BF16)
        a_ref[...] = (silu * uu).astype(BF16)

    wide = pl.BlockSpec((tm, FH), lambda i, j: (i, j))
    return pl.pallas_call(
        body, name="ffn_up", grid=(s // tm, 2),
        in_specs=[pl.BlockSpec((tm, D), lambda i, j: (i, 0)), pl.BlockSpec((1, D), lambda i, j: (0, 0)),
                  pl.BlockSpec((None, None, D, FH), lambda i, j: (l, j, 0, 0)),
                  pl.BlockSpec((None, None, D, FH), lambda i, j: (l, j + 2, 0, 0))] + [ANY] * len(deps),
        out_specs=[pl.BlockSpec((tm, D), lambda i, j: (i, 0)), wide, wide, wide],
        out_shape=[SDS((s, D), BF16), SDS((s, DFF), BF16), SDS((s, DFF), BF16), SDS((s, DFF), BF16)],
        compiler_params=_cp("parallel", "arbitrary"),
    )(x, pre_g, wgu, wgu, *deps)


def _mm_rms_res(a, w, l, x, g, c, tm, tk, name):
    s, k_dim = a.shape
    nk = k_dim // tk

    def body(a_ref, w_ref, x_ref, g_ref, z_ref, x1_ref):
        k = pl.program_id(1)
        p = jnp.dot(a_ref[...], w_ref[...], preferred_element_type=F32)

        @pl.when(k == 0)
        def _():
            z_ref[...] = p

        @pl.when(k > 0)
        def _():
            z_ref[...] += p

        @pl.when(k == nk - 1)
        def _():
            z = z_ref[...]
            x1_ref[...] = x_ref[...] + c * (z * _rsq(z, NORM_EPS) * g_ref[...])

    row = pl.BlockSpec((tm, D), lambda i, k: (i, 0))
    return pl.pallas_call(
        body, name=name, grid=(s // tm, nk),
        in_specs=[pl.BlockSpec((tm, tk), lambda i, k: (i, k)), pl.BlockSpec((None, tk, D), lambda i, k: (l, k, 0)),
                  row, pl.BlockSpec((1, D), lambda i, k: (0, 0))],
        out_specs=[row, row],
        out_shape=[SDS((s, D), F32), SDS((s, D), F32)],
        compiler_params=_cp("parallel", "arbitrary"),
    )(a, w, x, g)


def _rms_bwd(dy, z, g, c, tm, name, deps=()):
    s = z.shape[0]
    deps = list(deps)

    def body(dy_ref, z_ref, g_ref, *rest):
        dz_ref, dg_ref = rest[len(deps):]
        dz, dgr = _rms_bwd_rows(z_ref[...], g_ref[...], c * dy_ref[...])
        dz_ref[...] = dz.astype(BF16)
        part = jnp.sum(dgr, axis=0, keepdims=True)

        @pl.when(pl.program_id(0) == 0)
        def _():
            dg_ref[...] = part

        @pl.when(pl.program_id(0) > 0)
        def _():
            dg_ref[...] += part

    row = pl.BlockSpec((tm, D), lambda i: (i, 0))
    vec = pl.BlockSpec((1, D), lambda i: (0, 0))
    return pl.pallas_call(
        body, name=name, grid=(s // tm,), in_specs=[row, row, vec] + [ANY] * len(deps), out_specs=[row, vec],
        out_shape=[SDS((s, D), BF16), SDS((1, D), F32)], compiler_params=_cp("arbitrary"),
    )(dy, z, g, *deps)


def _ffn_bwd_mid(dz, wd, l, dadg, dadu, tm):
    s = dz.shape[0]

    def body(dz_ref, wd_ref, g_ref, u_ref, dgu_ref):
        da = lax.dot_general(dz_ref[...], wd_ref[...], NT, preferred_element_type=F32)
        dgu_ref[:, 0:FH] = (da * g_ref[...].astype(F32)).astype(BF16)
        dgu_ref[:, FH:2 * FH] = (da * u_ref[...].astype(F32)).astype(BF16)

    wide = pl.BlockSpec((tm, FH), lambda i, j: (i, j))
    return pl.pallas_call(
        body, name="ffn_bwd_mid", grid=(s // tm, 2),
        in_specs=[pl.BlockSpec((tm, D), lambda i, j: (i, 0)), pl.BlockSpec((None, FH, D), lambda i, j: (l, j, 0)), wide, wide],
        out_specs=pl.BlockSpec((tm, 2 * FH), lambda i, j: (i, j)),
        out_shape=SDS((s, 2 * DFF), BF16),
        compiler_params=_cp("parallel", "arbitrary"),
    )(dz, wd, dadg, dadu)


def _ffn_bwd_dh(dgu, wgu, l, x, pre_g, dx1, tm, deps=()):
    s = x.shape[0]
    deps = list(deps)

    def body(dgu_ref, w_hbm, x_ref, g_ref, dx1_ref, *rest):
        dx_ref, dgp_ref, wcat_ref, sems = rest[len(deps):]
        i = pl.program_id(0)

        @pl.when(i == 0)
        def _():
            cps = [pltpu.make_async_copy(w_hbm.at[l, q], wcat_ref.at[:, pl.ds((2 * (q % 2) + q // 2) * FH, FH)], sems.at[q])
                   for q in range(NSHARD)]
            for cp in cps:
                cp.start()
            for cp in cps:
                cp.wait()

        dh = lax.dot_general(dgu_ref[...], wcat_ref[...], NT, preferred_element_type=F32)
        dx, dgr = _rms_bwd_rows(x_ref[...], g_ref[...], dh)
        dx_ref[...] = dx1_ref[...] + dx
        _acc(dgp_ref, i == 0, jnp.sum(dgr, axis=0, keepdims=True))

    row = pl.BlockSpec((tm, D), lambda i: (i, 0))
    vec = pl.BlockSpec((1, D), lambda i: (0, 0))
    return pl.pallas_call(
        body, name="ffn_bwd_dh", grid=(s // tm,),
        in_specs=[pl.BlockSpec((tm, 2 * DFF), lambda i: (i, 0)), ANY, row, vec, row] + [ANY] * len(deps),
        out_specs=[row, vec],
        out_shape=[SDS((s, D), F32), SDS((1, D), F32)],
        scratch_shapes=[pltpu.VMEM((D, 2 * DFF), BF16), pltpu.SemaphoreType.DMA((NSHARD,))],
        compiler_params=_cp("arbitrary"),
    )(dgu, wgu, x, pre_g, dx1, *deps)


def _mm_tn_into(buf, a, b, l, joff, tka, tn, ts, name, bstride=1, boff=0):
    s, ka = a.shape
    n = b.shape[1] // bstride

    def body(buf_ref, a_ref, b_ref, o_ref):
        p = lax.dot_general(a_ref[...], b_ref[...], TN, preferred_element_type=F32)

        @pl.when(pl.program_id(2) == 0)
        def _():
            o_ref[...] = p

        @pl.when(pl.program_id(2) > 0)
        def _():
            o_ref[...] += p

    return pl.pallas_call(
        body, name=name, grid=(ka // tka, n // tn, s // ts),
        in_specs=[pl.BlockSpec(memory_space=pl.ANY),
                  pl.BlockSpec((ts, tka), lambda ia, j, t: (t, ia)),
                  pl.BlockSpec((ts, tn), lambda ia, j, t: (t, bstride * j + boff))],
        out_specs=pl.BlockSpec((None, None, tka, tn), lambda ia, j, t: (l, joff + j, ia, 0)),
        out_shape=SDS(buf.shape, F32), input_output_aliases={0: 0},
        compiler_params=_cp("parallel", "parallel", "arbitrary"),
    )(buf, a, b)


def _proj(x, g, w_in, l, tm):
    s = x.shape[0]

    def body(x_ref, g_ref, w_ref, h_ref, p_ref):
        xf = x_ref[...]
        h = (xf * _rsq(xf, NORM_EPS) * g_ref[...]).astype(BF16)
        h_ref[...] = h
        p_ref[...] = jnp.dot(h, w_ref[...], preferred_element_type=F32)

    return pl.pallas_call(
        body, name="proj", grid=(s // tm,),
        in_specs=[pl.BlockSpec((tm, D), lambda i: (i, 0)), pl.BlockSpec((1, D), lambda i: (0, 0)),
                  pl.BlockSpec((None, D, P_IN), lambda i: (l, 0, 0))],
        out_specs=[pl.BlockSpec((tm, D), lambda i: (i, 0)), pl.BlockSpec((tm, P_IN), lambda i: (i, 0))],
        out_shape=[SDS((s, D), BF16), SDS((s, P_IN), F32)],
        compiler_params=_cp("parallel"),
    )(x, g, w_in)


def _mm_nt(a, w, l, tm, name):
    s, k_dim = a.shape
    n = w.shape[1]

    def body(a_ref, w_ref, o_ref):
        o_ref[...] = lax.dot_general(a_ref[...], w_ref[...], NT, preferred_element_type=F32)

    return pl.pallas_call(
        body, name=name, grid=(s // tm,),
        in_specs=[pl.BlockSpec((tm, k_dim), lambda i: (i, 0)), pl.BlockSpec((None, n, k_dim), lambda i: (l, 0, 0))],
        out_specs=pl.BlockSpec((tm, n), lambda i: (i, 0)),
        out_shape=SDS((s, n), F32), compiler_params=_cp("parallel"),
    )(a, w)


def _mm_nt_rmsbwd(dp, w_in, l, x, g, dx1, tm):
    s = x.shape[0]

    def body(dp_ref, w_ref, x_ref, g_ref, dx1_ref, dx_ref, dg_ref):
        dh = lax.dot_general(dp_ref[...], w_ref[...], NT, preferred_element_type=F32)
        dx, dgr = _rms_bwd_rows(x_ref[...], g_ref[...], dh)
        dx_ref[...] = dx1_ref[...] + dx
        part = jnp.sum(dgr, axis=0, keepdims=True)

        @pl.when(pl.program_id(0) == 0)
        def _():
            dg_ref[...] = part

        @pl.when(pl.program_id(0) > 0)
        def _():
            dg_ref[...] += part

    row = pl.BlockSpec((tm, D), lambda i: (i, 0))
    vec = pl.BlockSpec((1, D), lambda i: (0, 0))
    return pl.pallas_call(
        body, name="mix_bwd_dx", grid=(s // tm,),
        in_specs=[pl.BlockSpec((tm, P_IN), lambda i: (i, 0)), pl.BlockSpec((None, D, P_IN), lambda i: (l, 0, 0)), row, vec, row],
        out_specs=[row, vec], out_shape=[SDS((s, D), F32), SDS((1, D), F32)],
        compiler_params=_cp("arbitrary"),
    )(dp, w_in, x, g, dx1)


def _row_iota(shape):
    return lax.broadcasted_iota(jnp.int32, shape, 0)


def _lru_gates(xc, wa_ref, ba_ref, wx_ref, bx_ref, lam_ref):
    xb = xc.astype(BF16)
    r = _sig(jnp.dot(xb, wa_ref[...], preferred_element_type=F32) + ba_ref[...])
    ig = _sig(jnp.dot(xb, wx_ref[...], preferred_element_type=F32) + bx_ref[...])
    nl = -lam_ref[...]
    sp = jnp.maximum(nl, 0.0) + jnp.log(1.0 + jnp.exp(-jnp.abs(nl)))
    log_a = -LRU_C * r * sp
    a = jnp.exp(log_a)
    x2 = 2.0 * log_a
    series = x2 * (1.0 + x2 * (0.5 + x2 * (1.0 / 6.0 + x2 * (1.0 / 24.0 + x2 * (1.0 / 120.0)))))
    em1 = jnp.where(x2 > -0.05, series, jnp.exp(x2) - 1.0)
    mlt = jnp.sqrt(-em1)
    return r, ig, a, mlt, sp


def _conv_taps(src_ref, w_ref, k_taps, pad, tc):
    acc = None
    for j in range(k_taps):
        term = w_ref[j:j + 1, :] * src_ref[pl.ds(pad - (k_taps - 1) + j, tc), :]
        acc = term if acc is None else acc + term
    return acc


def _fill_shifted(src_ref, sh_ref):
    n = src_ref.shape[0] - 8
    for s in range(1, 8):
        sh_ref[s, 0:n, :] = src_ref[pl.ds(s, n), :]


def _shifted_rows(src_ref, sh_ref, offset, tc):
    if offset % 8 == 0:
        return src_ref[pl.ds(offset, tc), :]
    return sh_ref[offset % 8, pl.ds(offset - offset % 8, tc), :]


def _gelu_parts(x):
    c0 = math.sqrt(2.0 / math.pi)
    inner = c0 * (x + 0.044715 * x * x * x)
    t = jnp.tanh(inner)
    gl = 0.5 * x * (1.0 + t)
    dgl = 0.5 * (1.0 + t) + 0.5 * x * (1.0 - t * t) * c0 * (1.0 + 3.0 * 0.044715 * x * x)
    return gl, dgl


def _lru_fwd(proj, cw, cb, wa, ba, wx, bx, lam, gg, tc):
    s = proj.shape[0]
    pad = 8

    def body(xcur_ref, xprev_ref, gate_ref, cw_ref, cb_ref, wa_ref, ba_ref, wx_ref, bx_ref, lam_ref, gg_ref,
             yn_ref, h_ref, xs_ref, hc_ref):
        i = pl.program_id(0)

        @pl.when(i == 0)
        def _():
            hc_ref[...] = jnp.zeros_like(hc_ref)

        xs_ref[0:pad, :] = jnp.where(i > 0, xprev_ref[tc - pad:tc, :], 0.0)
        xs_ref[pad:pad + tc, :] = xcur_ref[...]
        xc = _conv_taps(xs_ref, cw_ref, LRU_K, pad, tc) + cb_ref[...]
        _, ig, a, mlt, _ = _lru_gates(xc, wa_ref, ba_ref, wx_ref, bx_ref, lam_ref)
        u = mlt * (ig * xc)
        row = _row_iota((tc, W_A))
        d = 1
        while d < tc:
            ok = row >= d
            a_sh = jnp.where(ok, pltpu.roll(a, d, axis=0), 1.0)
            u_sh = jnp.where(ok, pltpu.roll(u, d, axis=0), 0.0)
            u = a * u_sh + u
            a = a * a_sh
            d *= 2
        h = u + a * hc_ref[...]
        hc_ref[...] = jnp.sum(jnp.where(row == tc - 1, h, 0.0), axis=0, keepdims=True)
        h_ref[...] = h
        gl, _ = _gelu_parts(gate_ref[...])
        ya = gl * h
        yn_ref[...] = (ya * _rsq(ya, NORM_EPS) * gg_ref[...]).astype(BF16)

    blk = lambda c: pl.BlockSpec((tc, W_A), lambda i, c=c: (i, c))
    full = lambda a: pl.BlockSpec(a.shape, lambda i: (0,) * a.ndim)
    params = [cw, cb, wa, ba, wx, bx, lam, gg]
    return pl.pallas_call(
        body, name="lru_fwd", grid=(s // tc,),
        in_specs=[blk(0), pl.BlockSpec((tc, W_A), lambda i: (jnp.maximum(i - 1, 0), 0)), blk(1)] + [full(a) for a in params],
        out_specs=[pl.BlockSpec((tc, W_A), lambda i: (i, 0))] * 2,
        out_shape=[SDS((s, W_A), BF16), SDS((s, W_A), F32)],
        scratch_shapes=[pltpu.VMEM((tc + pad, W_A), F32), pltpu.VMEM((1, W_A), F32)],
        compiler_params=_cp("arbitrary"),
    )(proj, proj, proj, *params)


def _acc(ref, first, val):
    @pl.when(first)
    def _():
        ref[...] = val

    @pl.when(jnp.logical_not(first))
    def _():
        ref[...] += val


def _lru_bwd(dy, proj, h, cw, cb, wa, ba, wx, bx, lam, gg, tc):
    s = proj.shape[0]
    nc = s // tc
    pad = 8

    def body(dy_ref, xcur_ref, xprev_ref, gate_ref, h_ref, hprev_ref, cw_ref, cb_ref, wa_ref, ba_ref, wx_ref, bx_ref,
             lam_ref, gg_ref,
             dp_ref, dcw_ref, dcb_ref, dwa_ref, dba_ref, dwx_ref, dbx_ref, dlam_ref, dgg_ref,
             xs_ref, ds_ref, mu_ref, nx_ref):
        step = pl.program_id(0)
        i = nc - 1 - step
        first = step == 0

        @pl.when(first)
        def _():
            mu_ref[...] = jnp.zeros_like(mu_ref)
            nx_ref[...] = jnp.zeros_like(nx_ref)

        xs_ref[0:pad, :] = jnp.where(i > 0, xprev_ref[tc - pad:tc, :], 0.0)
        xs_ref[pad:pad + tc, :] = xcur_ref[...]
        xc = _conv_taps(xs_ref, cw_ref, LRU_K, pad, tc) + cb_ref[...]
        r, ig, a, mlt, sp = _lru_gates(xc, wa_ref, ba_ref, wx_ref, bx_ref, lam_ref)
        hh = h_ref[...]
        gate = gate_ref[...]
        gl, dgl = _gelu_parts(gate)
        ya = gl * hh
        dya, dggr = _rms_bwd_rows(ya, gg_ref[...], dy_ref[...])
        _acc(dgg_ref, first, jnp.sum(dggr, axis=0, keepdims=True))
        dp_ref[:, W_A:2 * W_A] = dya * hh * dgl
        dh = dya * gl

        row = _row_iota((tc, W_A))
        aa = a
        uu = a * dh
        d = 1
        while d < tc:
            ok = row < tc - d
            a_sh = jnp.where(ok, pltpu.roll(aa, tc - d, axis=0), 1.0)
            u_sh = jnp.where(ok, pltpu.roll(uu, tc - d, axis=0), 0.0)
            uu = uu + aa * u_sh
            aa = aa * a_sh
            d *= 2
        cin = mu_ref[...]
        mu = uu + aa * cin
        lam_t = dh + jnp.where(row == tc - 1, cin, pltpu.roll(mu, tc - 1, axis=0))
        mu_ref[...] = jnp.sum(jnp.where(row == 0, mu, 0.0), axis=0, keepdims=True)
        hm1 = jnp.where(row == 0, jnp.where(i > 0, pltpu.roll(hprev_ref[...], 1, axis=0), 0.0),
                        pltpu.roll(hh, 1, axis=0))
        da = lam_t * hm1
        du = lam_t
        dmlt = du * ig * xc
        dig = du * mlt * xc
        dxc = du * mlt * ig
        dlog_a = da * a - dmlt * (a * a / mlt)
        dr = dlog_a * (-LRU_C * sp)
        dsp = jnp.sum(dlog_a * (-LRU_C * r), axis=0, keepdims=True)
        _acc(dlam_ref, first, dsp * (-_sig(-lam_ref[...])))
        dga = dr * r * (1.0 - r)
        dgx = dig * ig * (1.0 - ig)
        _acc(dba_ref, first, jnp.sum(dga, axis=0, keepdims=True))
        _acc(dbx_ref, first, jnp.sum(dgx, axis=0, keepdims=True))
        xb = xc.astype(BF16)
        dgab = dga.astype(BF16)
        dgxb = dgx.astype(BF16)
        _acc(dwa_ref, first, lax.dot_general(xb, dgab, TN, preferred_element_type=F32))
        _acc(dwx_ref, first, lax.dot_general(xb, dgxb, TN, preferred_element_type=F32))
        dxc = (dxc + lax.dot_general(dgab, wa_ref[...], NT, preferred_element_type=F32)
               + lax.dot_general(dgxb, wx_ref[...], NT, preferred_element_type=F32))

        _acc(dcb_ref, first, jnp.sum(dxc, axis=0, keepdims=True))
        r8 = _row_iota((8, W_A))
        dcw = jnp.zeros((8, W_A), F32)
        for j in range(LRU_K):
            tap = jnp.sum(dxc * xs_ref[pl.ds(pad - (LRU_K - 1) + j, tc), :], axis=0, keepdims=True)
            dcw = dcw + jnp.where(r8 == j, tap, 0.0)
        _acc(dcw_ref, first, dcw)
        ds_ref[0:tc, :] = dxc
        ds_ref[tc:tc + pad, :] = nx_ref[...]
        dlx = None
        for j in range(LRU_K):
            term = cw_ref[j:j + 1, :] * ds_ref[pl.ds(LRU_K - 1 - j, tc), :]
            dlx = term if dlx is None else dlx + term
        dp_ref[:, 0:W_A] = dlx
        nx_ref[...] = dxc[0:pad, :]

    rev = lambda c: pl.BlockSpec((tc, W_A), lambda t, c=c: (nc - 1 - t, c))
    prev = lambda c: pl.BlockSpec((tc, W_A), lambda t, c=c: (jnp.maximum(nc - 2 - t, 0), c))
    full = lambda a: pl.BlockSpec(a.shape, lambda t: (0,) * a.ndim)
    params = [cw, cb, wa, ba, wx, bx, lam, gg]
    vec = SDS((1, W_A), F32)
    sq = SDS((W_A, W_A), F32)
    outs = [SDS((s, 2 * W_A), F32), SDS((8, W_A), F32), vec, sq, vec, sq, vec, vec, vec]
    return pl.pallas_call(
        body, name="lru_bwd", grid=(nc,),
        in_specs=[rev(0), rev(0), prev(0), rev(1), rev(0), prev(0)] + [full(a) for a in params],
        out_specs=[pl.BlockSpec((tc, 2 * W_A), lambda t: (nc - 1 - t, 0))]
        + [pl.BlockSpec(o.shape, lambda t: (0, 0)) for o in outs[1:]],
        out_shape=outs,
        scratch_shapes=[pltpu.VMEM((tc + pad, W_A), F32), pltpu.VMEM((tc + pad, W_A), F32),
                        pltpu.VMEM((1, W_A), F32), pltpu.VMEM((pad, W_A), F32)],
        compiler_params=_cp("arbitrary"),
    )(dy, proj, proj, proj, h, h, *params)


def _attn_stack(qa, qb, kvh):
    lane = lax.broadcasted_iota(jnp.int32, qa.shape, 1)
    keep = (lane >= HD) if kvh == 1 else (lane < HD)
    parts = []
    for tile in (qa, qb):
        for half in (0, 1):
            y = tile if half == kvh else pltpu.roll(tile, HD, axis=1)
            parts.append(jnp.where(keep, y, 0.0))
    return jnp.concatenate(parts, axis=0)


def _attn_unstack(o, kvh):
    lane = lax.broadcasted_iota(jnp.int32, (BLK, 2 * HD), 1)
    tiles = []
    for t in range(2):
        halves = []
        for half in (0, 1):
            blk = o[(2 * t + half) * BLK:(2 * t + half + 1) * BLK, :]
            halves.append(blk if half == kvh else pltpu.roll(blk, HD, axis=1))
        tiles.append(jnp.where(lane < HD, halves[0], halves[1]))
    return tiles


def _attn_stack_all(x_ref_or_val):
    return jnp.concatenate([_attn_stack(x_ref_or_val[:, 256 * kvh:256 * kvh + 128],
                                        x_ref_or_val[:, 256 * kvh + 128:256 * kvh + 256], kvh) for kvh in range(2)], axis=0)


def _attn_unstack_all(o, dst_ref):
    for kvh in range(2):
        ta, tb = _attn_unstack(o[4 * BLK * kvh:4 * BLK * (kvh + 1), :], kvh)
        dst_ref[:, 256 * kvh:256 * kvh + 128] = ta
        dst_ref[:, 256 * kvh + 128:256 * kvh + 256] = tb


def _attn_windows(cur_ref, prev_ref, nb):
    blocks = [prev_ref[...]] + [cur_ref[b * BLK:(b + 1) * BLK, :] for b in range(nb)]
    return [jnp.concatenate(blocks[b:b + 2], axis=0).astype(BF16) for b in range(nb)]


def _attn_probs(qs, kw, n, sink_ref):
    rows = NQ * BLK
    sc = lax.dot_general(qs.astype(BF16), kw, NT, preferred_element_type=F32) * SCALE
    qi = lax.broadcasted_iota(jnp.int32, (rows, 2 * BLK), 0) & (BLK - 1)
    kj = lax.broadcasted_iota(jnp.int32, (rows, 2 * BLK), 1)
    rel = BLK + qi - kj
    mask = (rel >= 0) & (rel < BLK) & ((n - 1) * BLK + kj >= 0)
    head = lax.broadcasted_iota(jnp.int32, (rows, 1), 0) // BLK
    sk = jnp.zeros((rows, 1), F32)
    for h in range(NQ):
        sk = jnp.where(head == h, sink_ref[h:h + 1, 0:1], sk)
    sh = jnp.where(mask, sc, NEG_BIG)
    m = jnp.maximum(jnp.max(sh, axis=-1, keepdims=True), sk)
    e = jnp.exp(sh - m)
    es = jnp.exp(sk - m)
    rz = 1.0 / (jnp.sum(e, axis=-1, keepdims=True) + es)
    return e * rz, es * rz


def _attn_fwd(proj, sinks8, gg):
    s = proj.shape[0]
    nb = ATT_NB_FWD

    def body(q_ref, kc_ref, kp_ref, vc_ref, vp_ref, sink_ref, gg_ref, yn_ref, ob_ref):
        kws, vws = _attn_windows(kc_ref, kp_ref, nb), _attn_windows(vc_ref, vp_ref, nb)
        for b in range(nb):
            rows = pl.ds(b * BLK, BLK)
            p, _ = _attn_probs(_attn_stack_all(q_ref.at[rows, :]), kws[b], nb * pl.program_id(0) + b, sink_ref)
            _attn_unstack_all(jnp.dot(p.astype(BF16), vws[b], preferred_element_type=F32), ob_ref.at[rows, :])
        ob = ob_ref[...]
        yn_ref[...] = (ob * _rsq(ob, NORM_EPS) * gg_ref[...]).astype(BF16)

    tb = nb * BLK
    cur = lambda c: pl.BlockSpec((tb, 128), lambda m, c=c: (m, c))
    prev = lambda c: pl.BlockSpec((BLK, 128), lambda m, c=c: (jnp.maximum(nb * m - 1, 0), c))
    out = pl.BlockSpec((tb, W_B), lambda m: (m, 0))
    return pl.pallas_call(
        body, name="attn_fwd", grid=(s // tb,),
        in_specs=[pl.BlockSpec((tb, W_B), lambda m: (m, 1)), cur(8), prev(8), cur(9), prev(9),
                  pl.BlockSpec((8, 128), lambda n: (0, 0)), pl.BlockSpec((1, W_B), lambda n: (0, 0))],
        out_specs=[out, out], out_shape=[SDS((s, W_B), BF16), SDS((s, W_B), F32)],
        compiler_params=_cp("parallel"),
    )(proj, proj, proj, proj, proj, sinks8, gg)


def _attn_bwd(dy, proj, ob, sinks8, gg):
    s = proj.shape[0]
    nb = ATT_NB_BWD

    def body(dya_ref, dyb_ref, q_ref, kc_ref, kp_ref, vc_ref, vp_ref, ob_ref, sink_ref, gg_ref,
             dq_ref, dcur_ref, dprev_ref, dsink_ref, dgg_ref):
        first = pl.program_id(0) == 0
        kws, vws = _attn_windows(kc_ref, kp_ref, nb), _attn_windows(vc_ref, vp_ref, nb)
        dyn = jnp.concatenate([dya_ref[...], dyb_ref[...]], axis=1)
        dob, dggr = _rms_bwd_rows(ob_ref[...], gg_ref[...], dyn)
        _acc(dgg_ref, first, jnp.sum(dggr, axis=0, keepdims=True))
        r8 = _row_iota((8, 128))
        dsk = jnp.zeros((8, 128), F32)
        for b in range(nb):
            rows = pl.ds(b * BLK, BLK)
            qs = _attn_stack_all(q_ref.at[rows, :])
            p, psink = _attn_probs(qs, kws[b], nb * pl.program_id(0) + b, sink_ref)
            dosb = _attn_stack_all(dob[b * BLK:(b + 1) * BLK, :]).astype(BF16)
            dp = lax.dot_general(dosb, vws[b], NT, preferred_element_type=F32)
            dd = jnp.sum(p * dp, axis=-1, keepdims=True)
            dsb = (p * (dp - dd) * SCALE).astype(BF16)
            dsink_rows = -psink * dd
            for h in range(NQ):
                dsk = dsk + jnp.where(r8 == h, jnp.sum(dsink_rows[h * BLK:(h + 1) * BLK, :], axis=0, keepdims=True), 0.0)
            _attn_unstack_all(jnp.dot(dsb, kws[b], preferred_element_type=F32), dq_ref.at[rows, :])
            dkw = lax.dot_general(dsb, qs.astype(BF16), TN, preferred_element_type=F32)
            dvw = lax.dot_general(p.astype(BF16), dosb, TN, preferred_element_type=F32)
            dprev_ref[rows, 0:128] = dkw[0:BLK, :]
            dprev_ref[rows, 128:256] = dvw[0:BLK, :]
            dcur_ref[rows, 0:128] = dkw[BLK:2 * BLK, :]
            dcur_ref[rows, 128:256] = dvw[BLK:2 * BLK, :]
        _acc(dsink_ref, first, dsk)

    tb = nb * BLK
    cur = lambda c: pl.BlockSpec((tb, 128), lambda m, c=c: (m, c))
    prev = lambda c: pl.BlockSpec((BLK, 128), lambda m, c=c: (jnp.maximum(nb * m - 1, 0), c))
    wide = pl.BlockSpec((tb, W_B), lambda m: (m, 0))
    half = pl.BlockSpec((tb, 256), lambda m: (m, 0))
    return pl.pallas_call(
        body, name="attn_bwd", grid=(s // tb,),
        in_specs=[pl.BlockSpec((tb, 256), lambda m: (m, 1)), pl.BlockSpec((tb, 256), lambda m: (m, 2)),
                  pl.BlockSpec((tb, W_B), lambda m: (m, 1)), cur(8), prev(8), cur(9), prev(9), wide,
                  pl.BlockSpec((8, 128), lambda n: (0, 0)), pl.BlockSpec((1, W_B), lambda n: (0, 0))],
        out_specs=[wide, half, half, pl.BlockSpec((8, 128), lambda n: (0, 0)), pl.BlockSpec((1, W_B), lambda n: (0, 0))],
        out_shape=[SDS((s, W_B), F32), SDS((s, 256), F32), SDS((s, 256), F32), SDS((8, 128), F32), SDS((1, W_B), F32)],
        compiler_params=_cp("arbitrary"),
    )(dy, dy, proj, proj, proj, proj, proj, ob, sinks8, gg)


def _ln_parts(y1, eps=LN_EPS):
    mu = jnp.mean(y1, axis=-1, keepdims=True)
    xc = y1 - mu
    rstd = lax.rsqrt(jnp.mean(xc * xc, axis=-1, keepdims=True) + eps)
    return xc * rstd, rstd


def _conf_fwd(proj, cw, cb, lg, lb, gg, tc):
    s = proj.shape[0]
    pad = 32

    def body(ac_ref, gc_ref, ap_ref, gp_ref, cw_ref, cb_ref, lg_ref, lb_ref, gg_ref, yn_ref, y1_ref, ys_ref, sh_ref):
        i = pl.program_id(0)
        tail = ap_ref[tc - pad:tc, :] * _sig(gp_ref[tc - pad:tc, :])
        ys_ref[0:pad, :] = jnp.where(i > 0, tail, 0.0)
        ys_ref[pad:pad + tc, :] = ac_ref[...] * _sig(gc_ref[...])
        _fill_shifted(ys_ref, sh_ref)
        y1 = cb_ref[...]
        for j in range(CONV_K):
            y1 = y1 + cw_ref[j:j + 1, :] * _shifted_rows(ys_ref, sh_ref, pad - (CONV_K - 1) + j, tc)
        y1_ref[...] = y1
        xh, _ = _ln_parts(y1)
        yl = xh * lg_ref[...] + lb_ref[...]
        yc = yl * _sig(yl)
        yn_ref[...] = (yc * _rsq(yc, NORM_EPS) * gg_ref[...]).astype(BF16)

    cur = lambda c: pl.BlockSpec((tc, W_C), lambda i, c=c: (i, c))
    prev = lambda c: pl.BlockSpec((tc, W_C), lambda i, c=c: (jnp.maximum(i - 1, 0), c))
    full = lambda a: pl.BlockSpec(a.shape, lambda i: (0,) * a.ndim)
    params = [cw, cb, lg, lb, gg]
    out = pl.BlockSpec((tc, W_C), lambda i: (i, 0))
    return pl.pallas_call(
        body, name="conf_fwd", grid=(s // tc,),
        in_specs=[cur(5), cur(6), prev(5), prev(6)] + [full(a) for a in params],
        out_specs=[out, out], out_shape=[SDS((s, W_C), BF16), SDS((s, W_C), F32)],
        scratch_shapes=[pltpu.VMEM((tc + pad, W_C), F32), pltpu.VMEM((8, tc + pad, W_C), F32)],
        compiler_params=_cp("parallel"),
    )(proj, proj, proj, proj, *params)


def _conf_bwd(dy, proj, y1, cw, cb, lg, lb, gg, tc):
    s = proj.shape[0]
    nc = s // tc
    pad = 32

    def body(dy_ref, ac_ref, gc_ref, ap_ref, gp_ref, y1_ref, cw_ref, cb_ref, lg_ref, lb_ref, gg_ref,
             dp_ref, dcw_ref, dcb_ref, dlg_ref, dlb_ref, dgg_ref, ys_ref, ds_ref, nx_ref, ysh_ref, dsh_ref):
        step = pl.program_id(0)
        i = nc - 1 - step
        first = step == 0

        @pl.when(first)
        def _():
            nx_ref[...] = jnp.zeros_like(nx_ref)

        a = ac_ref[...]
        sg = _sig(gc_ref[...])
        tail = ap_ref[tc - pad:tc, :] * _sig(gp_ref[tc - pad:tc, :])
        ys_ref[0:pad, :] = jnp.where(i > 0, tail, 0.0)
        ys_ref[pad:pad + tc, :] = a * sg
        xh, rstd = _ln_parts(y1_ref[...])
        yl = xh * lg_ref[...] + lb_ref[...]
        sl = _sig(yl)
        yc = yl * sl
        dyc, dggr = _rms_bwd_rows(yc, gg_ref[...], dy_ref[...])
        _acc(dgg_ref, first, jnp.sum(dggr, axis=0, keepdims=True))
        dyl = dyc * sl * (1.0 + yl * (1.0 - sl))
        _acc(dlg_ref, first, jnp.sum(dyl * xh, axis=0, keepdims=True))
        _acc(dlb_ref, first, jnp.sum(dyl, axis=0, keepdims=True))
        dxh = dyl * lg_ref[...]
        dy1 = rstd * (dxh - jnp.mean(dxh, axis=-1, keepdims=True) - xh * jnp.mean(dxh * xh, axis=-1, keepdims=True))
        _acc(dcb_ref, first, jnp.sum(dy1, axis=0, keepdims=True))
        r32 = _row_iota((32, W_C))
        dcw = jnp.zeros((32, W_C), F32)
        _fill_shifted(ys_ref, ysh_ref)
        for j in range(CONV_K):
            tap = jnp.sum(dy1 * _shifted_rows(ys_ref, ysh_ref, pad - (CONV_K - 1) + j, tc), axis=0, keepdims=True)
            dcw = dcw + jnp.where(r32 == j, tap, 0.0)
        _acc(dcw_ref, first, dcw)
        ds_ref[0:tc, :] = dy1
        ds_ref[tc:tc + pad, :] = nx_ref[...]
        _fill_shifted(ds_ref, dsh_ref)
        dy0 = None
        for j in range(CONV_K):
            term = cw_ref[j:j + 1, :] * _shifted_rows(ds_ref, dsh_ref, CONV_K - 1 - j, tc)
            dy0 = term if dy0 is None else dy0 + term
        dp_ref[:, 0:W_C] = dy0 * sg
        dp_ref[:, W_C:2 * W_C] = dy0 * a * sg * (1.0 - sg)
        nx_ref[...] = dy1[0:pad, :]

    rev = lambda c: pl.BlockSpec((tc, W_C), lambda t, c=c: (nc - 1 - t, c))
    prev = lambda c: pl.BlockSpec((tc, W_C), lambda t, c=c: (jnp.maximum(nc - 2 - t, 0), c))
    full = lambda a: pl.BlockSpec(a.shape, lambda t: (0,) * a.ndim)
    params = [cw, cb, lg, lb, gg]
    vec = SDS((1, W_C), F32)
    outs = [SDS((s, 2 * W_C), F32), SDS((32, W_C), F32), vec, vec, vec, vec]
    return pl.pallas_call(
        body, name="conf_bwd", grid=(nc,),
        in_specs=[rev(3), rev(5), rev(6), prev(5), prev(6), rev(0)] + [full(a) for a in params],
        out_specs=[pl.BlockSpec((tc, 2 * W_C), lambda t: (nc - 1 - t, 0))]
        + [pl.BlockSpec(o.shape, lambda t: (0, 0)) for o in outs[1:]],
        out_shape=outs,
        scratch_shapes=[pltpu.VMEM((tc + pad, W_C), F32), pltpu.VMEM((tc + pad, W_C), F32), pltpu.VMEM((pad, W_C), F32),
                        pltpu.VMEM((8, tc + pad, W_C), F32), pltpu.VMEM((8, tc + pad, W_C), F32)],
        compiler_params=_cp("arbitrary"),
    )(dy, proj, proj, proj, proj, y1, *params)


def _assemble_dproj(dlru, dq, dcur, dprev, dconf):
    s = dq.shape[0]
    nb = s // BLK

    def body(dl_ref, dq_ref, dc_ref, dn_ref, df_ref, o_ref):
        n = pl.program_id(0)
        o_ref[:, 0:512] = dl_ref[...].astype(BF16)
        o_ref[:, 512:1024] = dq_ref[...].astype(BF16)
        o_ref[:, 1024:1280] = (dc_ref[...] + jnp.where(n < nb - 1, dn_ref[...], 0.0)).astype(BF16)
        o_ref[:, 1280:1792] = df_ref[...].astype(BF16)

    wide = pl.BlockSpec((BLK, 512), lambda n: (n, 0))
    return pl.pallas_call(
        body, name="assemble_dproj", grid=(nb,),
        in_specs=[wide, wide, pl.BlockSpec((BLK, 256), lambda n: (n, 0)),
                  pl.BlockSpec((BLK, 256), lambda n: (jnp.minimum(n + 1, nb - 1), 0)), wide],
        out_specs=pl.BlockSpec((BLK, P_IN), lambda n: (n, 0)), out_shape=SDS((s, P_IN), BF16),
        compiler_params=_cp("parallel"),
    )(dlru, dq, dcur, dprev, dconf)


def _loss_grad(y, t, tm):
    s = y.shape[0]

    def body(y_ref, t_ref, dy_ref, l_ref):
        err = y_ref[...] - t_ref[...]
        dy_ref[...] = err * (1.0 / D)
        _acc(l_ref, pl.program_id(0) == 0, jnp.sum(err * err, axis=0, keepdims=True))

    row = pl.BlockSpec((tm, D), lambda i: (i, 0))
    return pl.pallas_call(
        body, name="loss_grad", grid=(s // tm,), in_specs=[row, row],
        out_specs=[row, pl.BlockSpec((1, D), lambda i: (0, 0))],
        out_shape=[SDS((s, D), F32), SDS((1, D), F32)], compiler_params=_cp("arbitrary"),
    )(y, t)


def _block_diag(w):
    rows = [jnp.concatenate([w[h] if k == h else jnp.zeros((64, 64), w.dtype) for k in range(4)], axis=1) for h in range(4)]
    return jnp.concatenate(rows, axis=0)


def _diag_blocks(m):
    return jnp.stack([m[64 * h:64 * (h + 1), 64 * h:64 * (h + 1)] for h in range(4)])


def _layer_params(small, l):
    v = lambda name: small[name][l].reshape(1, -1)
    gg = small["group_g"][l]
    return dict(
        ffn1_pre=v("ffn1_pre_g"), ffn1_post=v("ffn1_post_g"), mix_pre=v("mix_pre_g"), mix_post=v("mix_post_g"),
        ffn2_pre=v("ffn2_pre_g"), ffn2_post=v("ffn2_post_g"), lru_cb=v("lru_conv_b"),
        wa=_block_diag(small["lru_w_a"][l]).astype(BF16), ba=v("lru_b_a"),
        wx=_block_diag(small["lru_w_x"][l]).astype(BF16), bx=v("lru_b_x"), lam=v("lru_lambda"),
        sinks8=jnp.broadcast_to(small["attn_sinks"][l][:, None], (NQ, 128)),
        conv_b=v("conv_b"), ln_g=v("conv_ln_g"), ln_b=v("conv_ln_b"),
        gg_a=gg[0:W_A].reshape(1, -1), gg_b=gg[W_A:W_A + W_B].reshape(1, -1), gg_c=gg[W_A + W_B:].reshape(1, -1),
    )


def _forward_layer(x, weights, p, tiles, deps=()):
    _, mm, _, tc, _ = tiles
    big = dict(weights("ffn1_gu", x))
    p = dict(p)
    sv = dict(x0=x)
    h1, g1, u1, a1 = _ffn_up(x, p["ffn1_pre"], big["ffn1_w_gu"], 0, mm, deps)
    big.update(weights("ffn1_down", a1))
    z1, x = _mm_rms_res(a1, big["ffn1_w_down"], 0, x, p["ffn1_post"], 0.5, mm, DFF, "ffn_down")
    sv.update(h1=h1, g1=g1, u1=u1, a1=a1, z1=z1, x1=x)
    big.update(weights("mix", x))
    p.update(lru_cw=big.pop("lru_conv_w"), conv_w=big.pop("conv_w"))
    hn, proj = _proj(x, p["mix_pre"], big["w_in"], 0, mm)
    yn_a, hl = _lru_fwd(proj, p["lru_cw"], p["lru_cb"], p["wa"], p["ba"], p["wx"], p["bx"], p["lam"], p["gg_a"], tc)
    yn_b, ob = _attn_fwd(proj, p["sinks8"], p["gg_b"])
    yn_c, y1 = _conf_fwd(proj, p["conv_w"], p["conv_b"], p["ln_g"], p["ln_b"], p["gg_c"], tc)
    ycat = jnp.concatenate([yn_a, yn_b, yn_c], axis=1)
    zo, x = _mm_rms_res(ycat, big["w_out"], 0, x, p["mix_post"], 1.0, mm, D, "mix_out")
    sv.update(hn=hn, proj=proj, hl=hl, ob=ob, y1=y1, ycat=ycat, zo=zo, x2=x)
    big.update(weights("ffn2", x))
    h2, g2, u2, a2 = _ffn_up(x, p["ffn2_pre"], big["ffn2_w_gu"], 0, mm)
    z2, x = _mm_rms_res(a2, big["ffn2_w_down"], 0, x, p["ffn2_post"], 0.5, mm, DFF, "ffn_down")
    sv.update(h2=h2, g2=g2, u2=u2, a2=a2, z2=z2, p=p, big=big)
    return x, sv


def _grad_buffers():
    empty = lambda *shape: lax.empty(shape, F32)
    return dict(ffn1_w_gu=empty(1, NSHARD, D, FH), ffn2_w_gu=empty(1, NSHARD, D, FH), ffn1_w_down=empty(1, 1, DFF, D),
                ffn2_w_down=empty(1, 1, DFF, D), w_in=empty(1, 1, D, P_IN), w_out=empty(1, 1, D, D))


def _backward_layer(dx, sv, bufs, tiles, stage):
    p, big = sv["p"], sv["big"]
    tm, mm, dw, tc, dh_rows = tiles
    gr = {}

    def ffn_bwd(dx, which, xin, h, g, u, a, z, pre, post, deps):
        dz, dpost = _rms_bwd(dx, z, post, 0.5, tm, "ffn_post_bwd", deps)
        dgu = _ffn_bwd_mid(dz, big[which + "_w_down"], 0, g, u, mm)
        bufs[which + "_w_down"] = _mm_tn_into(bufs[which + "_w_down"], a, dz, 0, 0, FH, D, dw, "dw_down")
        bufs[which + "_w_gu"] = _mm_tn_into(bufs[which + "_w_gu"], h, dgu, 0, 0, D, FH, dw, "dw_gate", 2, 0)
        bufs[which + "_w_gu"] = _mm_tn_into(bufs[which + "_w_gu"], h, dgu, 0, 2, D, FH, dw, "dw_up", 2, 1)
        deps = stage({n: bufs[n] for n in (which + "_w_gu", which + "_w_down")}, bufs[which + "_w_gu"])
        dxn, dpre = _ffn_bwd_dh(dgu, big[which + "_w_gu"], 0, xin, pre, dx, dh_rows, deps)
        return dxn, dpre, dpost

    dx, gr["ffn2_pre_g"], gr["ffn2_post_g"] = ffn_bwd(dx, "ffn2", sv["x2"], sv["h2"], sv["g2"], sv["u2"], sv["a2"],
                                                      sv["z2"], p["ffn2_pre"], p["ffn2_post"], ())
    do, gr["mix_post_g"] = _rms_bwd(dx, sv["zo"], p["mix_post"], 1.0, tm, "mix_post_bwd")
    bufs["w_out"] = _mm_tn_into(bufs["w_out"], sv["ycat"], do, 0, 0, D, D, dw, "dw_out")
    dy = _mm_nt(do, big["w_out"], 0, mm, "mix_dy")
    proj = sv["proj"]
    (dlru, dcw, gr["lru_conv_b"], dwa, gr["lru_b_a"], dwx, gr["lru_b_x"], gr["lru_lambda"], dgg_a) = _lru_bwd(
        dy, proj, sv["hl"], p["lru_cw"], p["lru_cb"], p["wa"], p["ba"], p["wx"], p["bx"], p["lam"], p["gg_a"], tc)
    dq, dcur, dprev, dsk, dgg_b = _attn_bwd(dy, proj, sv["ob"], p["sinks8"], p["gg_b"])
    dconf, dconvw, gr["conv_b"], gr["conv_ln_g"], gr["conv_ln_b"], dgg_c = _conf_bwd(
        dy, proj, sv["y1"], p["conv_w"], p["conv_b"], p["ln_g"], p["ln_b"], p["gg_c"], tc)
    dproj = _assemble_dproj(dlru, dq, dcur, dprev, dconf)
    bufs["w_in"] = _mm_tn_into(bufs["w_in"], sv["hn"], dproj, 0, 0, D, P_IN, dw, "dw_in")
    dx, gr["mix_pre_g"] = _mm_nt_rmsbwd(dproj, big["w_in"], 0, sv["x1"], p["mix_pre"], dx, mm)
    gr["lru_conv_w"] = dcw[0:LRU_K]
    gr["lru_w_a"] = _diag_blocks(dwa)
    gr["lru_w_x"] = _diag_blocks(dwx)
    gr["attn_sinks"] = dsk[:, 0]
    gr["conv_w"] = dconvw[0:CONV_K]
    gr["group_g"] = jnp.concatenate([dgg_a, dgg_b, dgg_c], axis=1)
    dx, gr["ffn1_pre_g"], gr["ffn1_post_g"] = ffn_bwd(dx, "ffn1", sv["x0"], sv["h1"], sv["g1"], sv["u1"], sv["a1"],
                                                      sv["z1"], p["ffn1_pre"], p["ffn1_post"],
                                                      stage({n: bufs[n] for n in ("w_in", "w_out")}, dx))
    return dx, gr


def _tiles(s):
    return min(1024, s), min(1024, s), min(2048, s), min(512, s // 2), min(512, s)


HBM_SPEC = pl.BlockSpec(memory_space=pltpu.HBM)
SEM_SPEC = pl.BlockSpec(memory_space=pltpu.SEMAPHORE)
EFFECT = pltpu.SideEffectType.DATAFLOW_SIDE_EFFECTING


def _place():
    x, y, c = lax.axis_index("x"), lax.axis_index("y"), lax.axis_index("c")
    return x, y, c, [(1 - x, y), (x, 1 - y), (1 - x, 1 - y)]


def _rcopy(src, dst, send_sems, recv_sems, k, to):
    return pltpu.make_async_remote_copy(src_ref=src, dst_ref=dst, send_sem=send_sems.at[k], recv_sem=recv_sems.at[k],
                                        device_id=to, device_id_type=MESH)


def _half(rows, which):
    return pl.ds(which * (rows // 2), rows // 2)


def _place_shard(w, l, p_idx, dtype):
    _, rows, cols = w.shape
    tr = _rows_per_block(rows, cols, 16, SUM_BLOCK_ELEMS) if rows % 16 == 0 else rows

    def body(p_ref, buf_ref, w_ref, o_ref):
        o_ref[...] = w_ref[...].astype(dtype)

    spec = pltpu.PrefetchScalarGridSpec(
        num_scalar_prefetch=1, grid=(rows // tr,),
        in_specs=[ANY, pl.BlockSpec((None, tr, cols), lambda i, pr: (l, i, 0))],
        out_specs=pl.BlockSpec((None, None, tr, cols), lambda i, pr: (0, pr[0], i, 0)))
    shape = (1, NSHARD, rows, cols)
    return pl.pallas_call(body, name="place_shard", grid_spec=spec, out_shape=SDS(shape, dtype),
                          input_output_aliases={1: 0}, compiler_params=_cp("parallel"),
                          )(p_idx, lax.empty(shape, dtype), w)


def _gather_two_level(bufs, n_halved):
    n = len(bufs)

    def body(*refs):
        outs = refs[n:2 * n]
        send_sems, recv_sems = refs[2 * n:]
        x, y, c, chips = _place()
        p = 2 * x + y
        me, sibling = (x, y, c), (x, y, 1 - c)

        def blk(a, q, half):
            return outs[a].at[0, q, _half(outs[a].shape[2], half)] if a < n_halved else outs[a].at[0, q]

        def cp(a, k, q, half, to):
            return _rcopy(blk(a, q, half), blk(a, q, half), send_sems, recv_sems, 6 * a + k, to)

        first = [cp(a, j, p, c, (*chip, c)) for a in range(n) for j, chip in enumerate(chips)]
        for d in first:
            d.start()
        passed = []
        for a in range(n):
            for j, chip in enumerate(chips):
                q = 2 * chip[0] + chip[1]
                cp(a, j, q, c, me).wait_recv()
                if a < n_halved:
                    passed.append(cp(a, 3 + j, q, c, sibling))
                    passed[-1].start()
        for a in range(n_halved):
            for j, chip in enumerate(chips):
                cp(a, 3 + j, 2 * chip[0] + chip[1], 1 - c, me).wait_recv()
        for d in first + passed:
            d.wait_send()

    return pl.pallas_call(
        body, name="gather_layer0", in_specs=[ANY] * n, out_specs=[ANY] * n,
        out_shape=[SDS(b.shape, b.dtype) for b in bufs], input_output_aliases={a: a for a in range(n)},
        scratch_shapes=[pltpu.SemaphoreType.DMA((6 * n,)), pltpu.SemaphoreType.DMA((6 * n,))],
    )(*bufs)


def _run_plans(plans, refs, send_sems, recv_sems):
    cps, b0, s0 = [], 0, 0
    for plan, nb, ns in plans:
        cps += plan(refs[b0:b0 + nb], send_sems, recv_sems, s0)
        b0, s0 = b0 + nb, s0 + ns
    return cps


def _exchange(name, bufs, plans):
    n = len(bufs)
    nsem = sum(ns for _, _, ns in plans)

    def body(*refs):
        cps = _run_plans(plans, refs[n:2 * n], refs[2 * n], refs[2 * n + 1])
        for cp in cps:
            cp.start()
        for cp in cps:
            cp.wait()

    return pl.pallas_call(
        body, name=name, in_specs=[ANY] * n, out_specs=[ANY] * n, out_shape=[SDS(b.shape, b.dtype) for b in bufs],
        input_output_aliases={a: a for a in range(n)},
        scratch_shapes=[pltpu.SemaphoreType.DMA((nsem,)), pltpu.SemaphoreType.DMA((nsem,))],
    )(*bufs)


def _exchange_start(name, bufs, plans, deps=()):
    n = len(bufs)
    nsem = sum(ns for _, _, ns in plans)
    deps = list(deps)
    first_out = n + len(deps)

    def body(*refs):
        for cp in _run_plans(plans, refs[:n], refs[first_out], refs[first_out + 1]):
            cp.start()
        token = refs[first_out + 2 + n]
        token[...] = jnp.zeros_like(token)

    outs = pl.pallas_call(
        body, name=name,
        out_shape=(pltpu.SemaphoreType.DMA((nsem,)), pltpu.SemaphoreType.DMA((nsem,)),
                   *[pltpu.HBM(b.shape, b.dtype) for b in bufs], SDS((8, 128), F32)),
        in_specs=[HBM_SPEC] * n + [ANY] * len(deps),
        out_specs=(SEM_SPEC, SEM_SPEC, *[HBM_SPEC] * n, pl.BlockSpec(memory_space=pltpu.VMEM)),
        input_output_aliases={a: 2 + a for a in range(n)},
        compiler_params=pltpu.CompilerParams(has_side_effects=EFFECT),
    )(*[pltpu.with_memory_space_constraint(b, pltpu.HBM) for b in bufs], *deps)
    return outs[0], outs[1], list(outs[2:2 + n]), outs[2 + n]


def _exchange_wait(name, send_sems, recv_sems, bufs, plans, after):
    n = len(bufs)

    def body(*refs):
        for cp in _run_plans(plans, refs[:n], refs[n], refs[n + 1]):
            cp.wait_send()
            cp.wait_recv()

    return pl.pallas_call(
        body, name=name, out_shape=[pltpu.HBM(b.shape, b.dtype) for b in bufs],
        in_specs=[HBM_SPEC] * n + [SEM_SPEC, SEM_SPEC, ANY], out_specs=[HBM_SPEC] * n,
        input_output_aliases={a: a for a in range(n)},
        compiler_params=pltpu.CompilerParams(has_side_effects=EFFECT),
    )(*bufs, send_sems, recv_sems, after)


def _plan_gather(refs, send_sems, recv_sems, base):
    x, y, c, chips = _place()
    p = 2 * x + y
    return [_rcopy(r.at[0, p], r.at[0, p], send_sems, recv_sems, base + 3 * a + j, (*chip, c))
            for a, r in enumerate(refs) for j, chip in enumerate(chips)]


def _plan_pair_exchange(refs, send_sems, recv_sems, base):
    x, y, c, _ = _place()
    n = len(refs) // 2
    return [_rcopy(refs[a].at[:, _half(refs[a].shape[1], 1 - c)], refs[n + a], send_sems, recv_sems, base + a,
                   (x, y, 1 - c)) for a in range(n)]


def _plan_chip_exchange(refs, send_sems, recv_sems, base):
    x, y, c, chips = _place()
    n = len(refs) // 2
    return [_rcopy(refs[a].at[2 * chip[0] + chip[1]], refs[n + a].at[j], send_sems, recv_sems, base + 3 * a + j,
                   (*chip, c)) for a in range(n) for j, chip in enumerate(chips)]


def _plan_pair_share(refs, send_sems, recv_sems, base):
    x, y, c, _ = _place()
    return [_rcopy(r.at[_half(r.shape[0], c)], r.at[_half(r.shape[0], c)], send_sems, recv_sems, base + a,
                   (x, y, 1 - c)) for a, r in enumerate(refs)]


def _plan_small_gather(refs, send_sems, recv_sems, base):
    x, y, c, _ = _place()
    me = 4 * x + 2 * y + c
    cps = []
    for m in range(1, NDEV):
        peer = (1 - x if m & 4 else x, 1 - y if m & 2 else y, 1 - c if m & 1 else c)
        cps.append(_rcopy(refs[0], refs[1].at[me], send_sems, recv_sems, base + m - 1, peer))
    return cps


def _sum_small(buf, gathered):
    def body(buf_ref, g_ref, o_ref):
        x, y, c, _ = _place()
        me = 4 * x + 2 * y + c
        total = jnp.where(me == 0, buf_ref[...], g_ref[0])
        for dev in range(1, NDEV):
            total = total + jnp.where(me == dev, buf_ref[...], g_ref[dev])
        o_ref[...] = total

    vm = pl.BlockSpec(memory_space=pltpu.VMEM)
    return pl.pallas_call(body, name="sum_small", in_specs=[vm, vm], out_specs=vm, out_shape=SDS(buf.shape, F32),
                          compiler_params=pltpu.CompilerParams(vmem_limit_bytes=VMEM_LIMIT))(buf, gathered)


BLOCK_ELEMS = 512 * 1024
SUM_BLOCK_ELEMS = 1024 * 1024


def _rows_per_block(rows, cols, mult, limit=BLOCK_ELEMS):
    best = None
    for tr in range(mult, rows + 1, mult):
        if rows % tr == 0 and tr * cols <= limit:
            best = tr
    assert best is not None, (rows, cols)
    return best


def _pair_sum(g, r, c_idx):
    nq, rows, cols = g.shape
    half = rows // 2
    tr = _rows_per_block(half, cols, 16, SUM_BLOCK_ELEMS)
    nb = half // tr

    def body(c_ref, g_ref, r_ref, t_ref):
        t_ref[...] = (g_ref[...] + r_ref[...]).astype(BF16)

    blk = pl.BlockSpec((None, tr, cols), lambda q, i, cr: (q, i, 0))
    spec = pltpu.PrefetchScalarGridSpec(
        num_scalar_prefetch=1, grid=(nq, nb),
        in_specs=[pl.BlockSpec((None, tr, cols), lambda q, i, cr: (q, cr[0] * nb + i, 0)), blk], out_specs=blk)
    return pl.pallas_call(body, name="grad_pair_sum", grid_spec=spec, out_shape=SDS((nq, half, cols), BF16),
                          compiler_params=_cp("parallel", "parallel"))(c_idx, g, r)


def _chip_sum(g, r, rr, cp_idx):
    _, rows, cols = g.shape
    half = rows // 2
    tr = _rows_per_block(half, cols, 16, SUM_BLOCK_ELEMS)
    nb = half // tr

    def body(cp_ref, buf_ref, g_ref, r_ref, rr_ref, o_ref):
        o_ref[...] = ((g_ref[...] + r_ref[...]) + rr_ref[0].astype(F32) + rr_ref[1].astype(F32) + rr_ref[2].astype(F32))

    spec = pltpu.PrefetchScalarGridSpec(
        num_scalar_prefetch=1, grid=(nb,),
        in_specs=[ANY, pl.BlockSpec((None, tr, cols), lambda i, cp: (cp[1], cp[0] * nb + i, 0)),
                  pl.BlockSpec((None, tr, cols), lambda i, cp: (cp[1], i, 0)),
                  pl.BlockSpec((3, tr, cols), lambda i, cp: (0, i, 0))],
        out_specs=pl.BlockSpec((tr, cols), lambda i, cp: (cp[0] * nb + i, 0)))
    return pl.pallas_call(body, name="grad_chip_sum", grid_spec=spec, out_shape=SDS((rows, cols), F32),
                          input_output_aliases={1: 0}, compiler_params=_cp("parallel"),
                          )(cp_idx, lax.empty((rows, cols), F32), g, r, rr)


def _adamw_math(w, g, m, v):
    mn = ADAM_B1 * m + (1.0 - ADAM_B1) * g
    vn = ADAM_B2 * v + (1.0 - ADAM_B2) * (g * g)
    m_hat = mn / (1.0 - ADAM_B1 ** ADAM_STEP)
    v_hat = vn / (1.0 - ADAM_B2 ** ADAM_STEP)
    return -ADAM_LR * (m_hat / (jnp.sqrt(v_hat) + ADAM_EPS) + ADAM_WD * w), mn, vn


def _adamw_layer(w, g, m, v, l, outs, deps=()):
    _, rows, cols = w.shape
    tr = _rows_per_block(rows, cols, 8)
    deps = list(deps)

    def body(*refs):
        w_ref, g_ref, m_ref, v_ref = refs[4:8]
        go_ref, d_ref, mo_ref, vo_ref = refs[8 + len(deps):]
        gg = g_ref[...]
        go_ref[...] = gg
        d_ref[...], mo_ref[...], vo_ref[...] = _adamw_math(w_ref[...], gg, m_ref[...], v_ref[...])

    blk = pl.BlockSpec((None, tr, cols), lambda i: (l, i, 0))
    return pl.pallas_call(
        body, name="adamw_layer", grid=(rows // tr,),
        in_specs=[ANY] * 4 + [blk, pl.BlockSpec((tr, cols), lambda i: (i, 0)), blk, blk] + [ANY] * len(deps),
        out_specs=[blk] * 4, out_shape=[SDS(w.shape, F32)] * 4, input_output_aliases={k: k for k in range(4)},
        compiler_params=_cp("parallel"))(*outs, w, g, m, v, *deps)


def _adamw_small(ws, gs, ms, vs, deps=()):
    n = len(ws)
    deps = list(deps)

    def body(*refs):
        refs = refs[:4 * n] + refs[4 * n + len(deps):]
        w, g, m, v, d_out, m_out, v_out = (refs[k * n:(k + 1) * n] for k in range(7))
        for k in range(n):
            d_out[k][...], m_out[k][...], v_out[k][...] = _adamw_math(w[k][...], g[k][...], m[k][...], v[k][...])

    vm = pl.BlockSpec(memory_space=pltpu.VMEM)
    outs = pl.pallas_call(body, name="adamw_small", in_specs=[vm] * (4 * n) + [ANY] * len(deps), out_specs=[vm] * (3 * n),
                          out_shape=[SDS(w.shape, F32) for w in ws] * 3,
                          compiler_params=pltpu.CompilerParams(vmem_limit_bytes=VMEM_LIMIT))(*ws, *gs, *ms, *vs, *deps)
    return outs[:n], outs[n:2 * n], outs[2 * n:]


_WEIGHTS = ["ffn1_pre_g", "ffn1_w_gu", "ffn1_w_down", "ffn1_post_g", "mix_pre_g", "w_in", "lru_conv_w", "lru_conv_b",
            "lru_w_a", "lru_b_a", "lru_w_x", "lru_b_x", "lru_lambda", "attn_sinks", "conv_w", "conv_b", "conv_ln_g",
            "conv_ln_b", "group_g", "w_out", "mix_post_g", "ffn2_pre_g", "ffn2_w_gu", "ffn2_w_down", "ffn2_post_g"]
_INPUTS = ["x"] + _WEIGHTS + ["loss_target"] + ["m_" + n for n in _WEIGHTS] + ["v_" + n for n in _WEIGHTS]
_BIG = ["ffn1_w_gu", "ffn1_w_down", "w_in", "w_out", "ffn2_w_gu", "ffn2_w_down"]
_SMALL_SHARDED = ["lru_conv_w", "conv_w"]
_SMALL_REPL = [n for n in _WEIGHTS if n not in _BIG and n not in _SMALL_SHARDED]

PACK_TILE = 8 * 128


def _pack(arrs):
    parts = []
    for a in arrs:
        flat = a.reshape(-1)
        parts.append(jnp.pad(flat, (0, -flat.shape[0] % PACK_TILE)).reshape(-1, 128))
    return jnp.concatenate(parts, axis=0)


def _unpack(buf, shapes):
    out, row = [], 0
    for shp in shapes:
        size = math.prod(shp)
        nrow = -(-size // PACK_TILE) * 8
        out.append(buf[row:row + nrow].reshape(-1)[:size].reshape(shp))
        row += nrow
    return out


def _unshard_cols(a):
    return a.transpose(0, 2, 1, 3).reshape(1, a.shape[2], NSHARD * a.shape[3])


_GROUPS = dict(ffn1_gu=["ffn1_w_gu"], ffn1_down=["ffn1_w_down"], mix=["w_in", "w_out", "lru_conv_w", "conv_w"],
               ffn2=["ffn2_w_gu", "ffn2_w_down"])


def _full_weights(group, gathered):
    g = dict(zip(_GROUPS[group], gathered))
    if group == "mix":
        return dict(w_in=_unshard_cols(g["w_in"]), w_out=g["w_out"].reshape(1, D, D),
                    lru_conv_w=_unshard_cols(g["lru_conv_w"])[0], conv_w=_unshard_cols(g["conv_w"])[0])
    return {n: (a.reshape(1, DFF, D) if n.endswith("w_down") else a) for n, a in g.items()}


def _by_shard(name, buf):
    if name.endswith("w_gu"):
        return buf[0]
    if name == "w_in":
        return buf.reshape(D, NSHARD, P_IN // NSHARD).transpose(1, 0, 2)
    return buf.reshape(NSHARD, buf.shape[2] // NSHARD, buf.shape[3])


class _Reducer:
    PLANS = (_plan_pair_exchange, _plan_chip_exchange, _plan_pair_share)

    def __init__(self, keys, gs, c_idx, cp_idx):
        self.keys, self.gs, self.c_idx, self.cp_idx = keys, gs, c_idx, cp_idx
        self.n = len(gs)
        self.step = 0
        self.result = None

    def inputs(self):
        n = self.n
        if self.step == 0:
            bufs = self.gs + [lax.empty((NSHARD, g.shape[1] // 2, g.shape[2]), F32) for g in self.gs]
        elif self.step == 1:
            ts = [_pair_sum(g, r, self.c_idx) for g, r in zip(self.gs, self.rs)]
            bufs = ts + [lax.empty((3,) + t.shape[1:], BF16) for t in ts]
        else:
            bufs = [_chip_sum(g, r, rr, self.cp_idx) for g, r, rr in zip(self.gs, self.rs, self.rrs)]
        return bufs, (self.PLANS[self.step], len(bufs), (n, 3 * n, n)[self.step])

    def absorb(self, done):
        n = self.n
        if self.step == 0:
            self.gs, self.rs = done[:n], done[n:]
        elif self.step == 1:
            self.rrs = done[n:]
        else:
            self.result = dict(zip(self.keys, done))
        self.step += 1


class _SmallGather:
    def __init__(self, buf):
        self.buf, self.step, self.result, self.gathered = buf, 0, {}, None

    def inputs(self):
        return [self.buf, jnp.zeros((NDEV,) + self.buf.shape, F32)], (_plan_small_gather, 2, NDEV - 1)

    def absorb(self, done):
        self.buf, self.gathered = done
        self.step = 3


class _ReducePipeline:
    def __init__(self, c_idx, cp_idx):
        self.c_idx, self.cp_idx = c_idx, cp_idx
        self.reducers, self.flying, self.calls = [], None, 0

    def add(self, layer, done):
        if done:
            keys = [(layer, n) for n in done]
            self.reducers.append(_Reducer(keys, [_by_shard(n, b) for n, b in done.items()], self.c_idx, self.cp_idx))

    def _next(self):
        active = [r for r in self.reducers if r.step < 3]
        bufs, plans = [], []
        for r in active:
            b, triple = r.inputs()
            bufs += b
            plans.append(triple)
        self.calls += 1
        return active, bufs, plans, "grad_exchange%d" % self.calls

    def _absorb(self, active, plans, done):
        at = 0
        for r, (_, nb, _) in zip(active, plans):
            r.absorb(done[at:at + nb])
            at += nb

    def _land(self, after):
        if self.flying is not None:
            active, plans, name, send_sems, recv_sems, bufs = self.flying
            self._absorb(active, plans, _exchange_wait(name + "_wait", send_sems, recv_sems, bufs, plans, after))
            self.flying = None

    def hook(self, after):
        self._land(after)
        active, bufs, plans, name = self._next()
        if not active:
            return []
        send_sems, recv_sems, bufs, token = _exchange_start(name + "_start", bufs, plans)
        self.flying = (active, plans, name, send_sems, recv_sems, bufs)
        return [token]

    def available(self):
        out = {}
        for r in self.reducers:
            if r.step == 3:
                out.update(r.result)
        return out

    def finish(self, after):
        self._land(after)
        while True:
            active, bufs, plans, name = self._next()
            if not active:
                break
            self._absorb(active, plans, _exchange(name, bufs, plans))
        out = {}
        for r in self.reducers:
            out.update(r.result)
        return out


def kernel(*args):
    d = dict(zip(_INPUTS, args, strict=True))
    xi, yi, ci = lax.axis_index("x"), lax.axis_index("y"), lax.axis_index("c")
    p = 2 * xi + yi
    c_idx = jnp.reshape(ci, (1,)).astype(jnp.int32)
    p_idx = jnp.reshape(p, (1,)).astype(jnp.int32)
    cp_idx = jnp.stack([ci, p]).astype(jnp.int32)
    x, target = d["x"][0], d["loss_target"][0]
    tiles = _tiles(x.shape[0])

    groups = [(l, grp) for l in range(DEPTH) for grp in _GROUPS]
    placed = {(l, grp): [_place_shard(d[n], l, p_idx, BF16 if n in _BIG else F32) for n in _GROUPS[grp]]
              for l, grp in groups}
    ready = {groups[0]: _gather_two_level(placed[groups[0]], len(placed[groups[0]]))}
    flying, tokens = {}, [ready[groups[0]][0]]
    for l, grp in groups[1:]:
        plans = [(_plan_gather, len(placed[l, grp]), 3 * len(placed[l, grp]))]
        send_sems, recv_sems, bufs, token = _exchange_start("gather_l%d_%s_start" % (l, grp), placed[l, grp], plans,
                                                             tokens[-1:])
        flying[l, grp] = (send_sems, recv_sems, bufs, plans)
        tokens.append(token)

    def weights_of(l):
        def weights(grp, after):
            if (l, grp) not in ready:
                send_sems, recv_sems, bufs, plans = flying[l, grp]
                ready[l, grp] = _exchange_wait("gather_l%d_%s_wait" % (l, grp), send_sems, recv_sems, bufs, plans, after)
            return _full_weights(grp, ready[l, grp])
        return weights

    small = {n: d[n] for n in _SMALL_REPL}
    x1, sv0 = _forward_layer(x, weights_of(0), _layer_params(small, 0), tiles, tokens[1:])
    x2, sv1 = _forward_layer(x1, weights_of(1), _layer_params(small, 1), tiles)
    dx, lcols = _loss_grad(x2, target, tiles[0])

    pipe = _ReducePipeline(c_idx, cp_idx)
    sgrads = [None] * DEPTH
    for l, sv in ((1, sv1), (0, sv0)):
        bufs = _grad_buffers()

        def stage(done, dx, l=l):
            pipe.add(l, done)
            return pipe.hook(dx)

        dx, sgrads[l] = _backward_layer(dx, sv, bufs, tiles, stage)
    grad_x = dx

    stacked = {n: jnp.stack([sgrads[l][n].reshape(d[n].shape[1:]) for l in range(DEPTH)]) for n in _SMALL_REPL}
    for n in _SMALL_SHARDED:
        stacked[n] = jnp.stack([sgrads[l][n] for l in range(DEPTH)])
    loss_part = jnp.pad((0.5 / D) * jnp.sum(lcols).reshape(1), (0, 127))
    order = _SMALL_REPL + _SMALL_SHARDED
    small_gather = _SmallGather(_pack([loss_part] + [stacked[n] for n in order]))
    pipe.reducers.append(small_gather)

    results = {n: tuple(lax.empty(d[n].shape, F32) for _ in range(4)) for n in _BIG}
    applied = set()

    def apply_ready(deps, last):
        for (l, n), g in pipe.available().items():
            if (l, n) not in applied:
                results[n] = _adamw_layer(d[n], g, d["m_" + n], d["v_" + n], l, results[n], deps)
                applied.add((l, n))
                last = results[n][1]
                deps = [last]
        return last

    last = apply_ready(pipe.hook(grad_x), grad_x)
    token = pipe.hook(last)
    summed = _unpack(_sum_small(small_gather.buf, small_gather.gathered), [(128,)] + [stacked[n].shape for n in order])
    loss = summed[0][0]
    grads = {}
    for n, g in zip(order, summed[1:]):
        if n in _SMALL_SHARDED:
            g = lax.dynamic_slice_in_dim(g, p * (g.shape[2] // NSHARD), g.shape[2] // NSHARD, axis=2)
        grads[n] = g
    delta, new_m, new_v = {}, {}, {}
    small_out = _adamw_small([d[n] for n in order], [grads[n] for n in order], [d["m_" + n] for n in order],
                             [d["v_" + n] for n in order], token)
    for out, res in zip((delta, new_m, new_v), small_out):
        out.update(zip(order, res))
    last = apply_ready([small_out[0][0]], small_out[0][0])
    pipe.finish(last)
    apply_ready((), last)
    for n in _BIG:
        grads[n], delta[n], new_m[n], new_v[n] = results[n]

    return (loss, grad_x[None], *[grads[n] for n in _WEIGHTS], *[delta[n] for n in _WEIGHTS],
            *[new_m[n] for n in _WEIGHTS], *[new_v[n] for n in _WEIGHTS])
```

```python
import math

import jax
import jax.numpy as jnp
import numpy as np
from jax import lax
from jax.experimental import pallas as pl
from jax.experimental.pallas import tpu as pltpu

F32 = jnp.float32
BF16 = jnp.bfloat16
SDS = jax.ShapeDtypeStruct

D = 1024
DFF = 2816
FH = DFF // 2
DEPTH = 2
W_A = 256
W_B = 512
W_C = 256
NQ = 8
HD = 64
BLK = 128
ATT_NB_FWD = 1
ATT_NB_BWD = 4
P_IN = 1792
LRU_K = 4
CONV_K = 31
LRU_C = 8.0
NORM_EPS = 1e-6
LN_EPS = 1e-5
NEG_BIG = -1e30
SCALE = 1.0 / math.sqrt(HD)

ADAM_LR = 0.001
ADAM_B1 = 0.9
ADAM_B2 = 0.999
ADAM_EPS = 1e-08
ADAM_WD = 0.01
ADAM_STEP = 10

VMEM_LIMIT = 60 * 1024 * 1024
NSHARD = 4
NDEV = 8

TN = (((0,), (0,)), ((), ()))
NT = (((1,), (1,)), ((), ()))

MESH = pl.DeviceIdType.MESH
ANY = pl.BlockSpec(memory_space=pl.ANY)


def _cp(*sem):
    return pltpu.CompilerParams(dimension_semantics=sem if sem else None, vmem_limit_bytes=VMEM_LIMIT)


def _rsq(x, eps):
    return lax.rsqrt(jnp.mean(x * x, axis=-1, keepdims=True) + eps)


def _rms_bwd_rows(x, g, dy):
    r = _rsq(x, NORM_EPS)
    xh = x * r
    dyg = dy * g
    dx = r * (dyg - xh * jnp.mean(dyg * xh, axis=-1, keepdims=True))
    return dx, dy * xh


def _sig(x):
    return jax.nn.sigmoid(x)


def _ffn_up(x, pre_g, wgu, l, tm, deps=()):
    s = x.shape[0]
    deps = list(deps)

    def body(x_ref, g_ref, wg_ref, wu_ref, *rest):
        h_ref, go_ref, uo_ref, a_ref = rest[len(deps):]

        @pl.when(pl.program_id(1) == 0)
        def _():
            xf = x_ref[...]
            h_ref[...] = (xf * _rsq(xf, NORM_EPS) * g_ref[...]).astype(BF16)

        h = h_ref[...]
        gg = jnp.dot(h, wg_ref[...], preferred_element_type=F32)
        uu = jnp.dot(h, wu_ref[...], preferred_element_type=F32)
        sg = _sig(gg)
        silu = gg * sg
        go_ref[...] = (uu * (sg * (1.0 + gg * (1.0 - sg)))).astype(BF16)
        uo_ref[...] = silu.astype(BF16)
        a_ref[...] = (silu * uu).astype(BF16)

    wide = pl.BlockSpec((tm, FH), lambda i, j: (i, j))
    return pl.pallas_call(
        body, name="ffn_up", grid=(s // tm, 2),
        in_specs=[pl.BlockSpec((tm, D), lambda i, j: (i, 0)), pl.BlockSpec((1, D), lambda i, j: (0, 0)),
                  pl.BlockSpec((None, None, D, FH), lambda i, j: (l, j, 0, 0)),
                  pl.BlockSpec((None, None, D, FH), lambda i, j: (l, j + 2, 0, 0))] + [ANY] * len(deps),
        out_specs=[pl.BlockSpec((tm, D), lambda i, j: (i, 0)), wide, wide, wide],
        out_shape=[SDS((s, D), BF16), SDS((s, DFF), BF16), SDS((s, DFF), BF16), SDS((s, DFF), BF16)],
        compiler_params=_cp("parallel", "arbitrary"),
    )(x, pre_g, wgu, wgu, *deps)


def _mm_rms_res(a, w, l, x, g, c, tm, tk, name):
    s, k_dim = a.shape
    nk = k_dim // tk

    def body(a_ref, w_ref, x_ref, g_ref, z_ref, x1_ref):
        k = pl.program_id(1)
        p = jnp.dot(a_ref[...], w_ref[...], preferred_element_type=F32)

        @pl.when(k == 0)
        def _():
            z_ref[...] = p

        @pl.when(k > 0)
        def _():
            z_ref[...] += p

        @pl.when(k == nk - 1)
        def _():
            z = z_ref[...]
            x1_ref[...] = x_ref[...] + c * (z * _rsq(z, NORM_EPS) * g_ref[...])

    row = pl.BlockSpec((tm, D), lambda i, k: (i, 0))
    return pl.pallas_call(
        body, name=name, grid=(s // tm, nk),
        in_specs=[pl.BlockSpec((tm, tk), lambda i, k: (i, k)), pl.BlockSpec((None, tk, D), lambda i, k: (l, k, 0)),
                  row, pl.BlockSpec((1, D), lambda i, k: (0, 0))],
        out_specs=[row, row],
        out_shape=[SDS((s, D), F32), SDS((s, D), F32)],
        compiler_params=_cp("parallel", "arbitrary"),
    )(a, w, x, g)


def _rms_bwd(dy, z, g, c, tm, name, deps=()):
    s = z.shape[0]
    deps = list(deps)

    def body(dy_ref, z_ref, g_ref, *rest):
        dz_ref, dg_ref = rest[len(deps):]
        dz, dgr = _rms_bwd_rows(z_ref[...], g_ref[...], c * dy_ref[...])
        dz_ref[...] = dz.astype(BF16)
        part = jnp.sum(dgr, axis=0, keepdims=True)

        @pl.when(pl.program_id(0) == 0)
        def _():
            dg_ref[...] = part

        @pl.when(pl.program_id(0) > 0)
        def _():
            dg_ref[...] += part

    row = pl.BlockSpec((tm, D), lambda i: (i, 0))
    vec = pl.BlockSpec((1, D), lambda i: (0, 0))
    return pl.pallas_call(
        body, name=name, grid=(s // tm,), in_specs=[row, row, vec] + [ANY] * len(deps), out_specs=[row, vec],
        out_shape=[SDS((s, D), BF16), SDS((1, D), F32)], compiler_params=_cp("arbitrary"),
    )(dy, z, g, *deps)


def _ffn_bwd_mid(dz, wd, l, dadg, dadu, tm):
    s = dz.shape[0]

    def body(dz_ref, wd_ref, g_ref, u_ref, dgu_ref):
        da = lax.dot_general(dz_ref[...], wd_ref[...], NT, preferred_element_type=F32)
        dgu_ref[:, 0:FH] = (da * g_ref[...].astype(F32)).astype(BF16)
        dgu_ref[:, FH:2 * FH] = (da * u_ref[...].astype(F32)).astype(BF16)

    wide = pl.BlockSpec((tm, FH), lambda i, j: (i, j))
    return pl.pallas_call(
        body, name="ffn_bwd_mid", grid=(s // tm, 2),
        in_specs=[pl.BlockSpec((tm, D), lambda i, j: (i, 0)), pl.BlockSpec((None, FH, D), lambda i, j: (l, j, 0)), wide, wide],
        out_specs=pl.BlockSpec((tm, 2 * FH), lambda i, j: (i, j)),
        out_shape=SDS((s, 2 * DFF), BF16),
        compiler_params=_cp("parallel", "arbitrary"),
    )(dz, wd, dadg, dadu)


def _ffn_bwd_dh(dgu, wgu, l, x, pre_g, dx1, tm, deps=()):
    s = x.shape[0]
    deps = list(deps)

    def body(dgu_ref, w_hbm, x_ref, g_ref, dx1_ref, *rest):
        dx_ref, dgp_ref, wcat_ref, sems = rest[len(deps):]
        i = pl.program_id(0)

        @pl.when(i == 0)
        def _():
            cps = [pltpu.make_async_copy(w_hbm.at[l, q], wcat_ref.at[:, pl.ds((2 * (q % 2) + q // 2) * FH, FH)], sems.at[q])
                   for q in range(NSHARD)]
            for cp in cps:
                cp.start()
            for cp in cps:
                cp.wait()

        dh = lax.dot_general(dgu_ref[...], wcat_ref[...], NT, preferred_element_type=F32)
        dx, dgr = _rms_bwd_rows(x_ref[...], g_ref[...], dh)
        dx_ref[...] = dx1_ref[...] + dx
        _acc(dgp_ref, i == 0, jnp.sum(dgr, axis=0, keepdims=True))

    row = pl.BlockSpec((tm, D), lambda i: (i, 0))
    vec = pl.BlockSpec((1, D), lambda i: (0, 0))
    return pl.pallas_call(
        body, name="ffn_bwd_dh", grid=(s // tm,),
        in_specs=[pl.BlockSpec((tm, 2 * DFF), lambda i: (i, 0)), ANY, row, vec, row] + [ANY] * len(deps),
        out_specs=[row, vec],
        out_shape=[SDS((s, D), F32), SDS((1, D), F32)],
        scratch_shapes=[pltpu.VMEM((D, 2 * DFF), BF16), pltpu.SemaphoreType.DMA((NSHARD,))],
        compiler_params=_cp("arbitrary"),
    )(dgu, wgu, x, pre_g, dx1, *deps)


def _mm_tn_into(buf, a, b, l, joff, tka, tn, ts, name, bstride=1, boff=0):
    s, ka = a.shape
    n = b.shape[1] // bstride

    def body(buf_ref, a_ref, b_ref, o_ref):
        p = lax.dot_general(a_ref[...], b_ref[...], TN, preferred_element_type=F32)

        @pl.when(pl.program_id(2) == 0)
        def _():
            o_ref[...] = p

        @pl.when(pl.program_id(2) > 0)
        def _():
            o_ref[...] += p

    return pl.pallas_call(
        body, name=name, grid=(ka // tka, n // tn, s // ts),
        in_specs=[pl.BlockSpec(memory_space=pl.ANY),
                  pl.BlockSpec((ts, tka), lambda ia, j, t: (t, ia)),
                  pl.BlockSpec((ts, tn), lambda ia, j, t: (t, bstride * j + boff))],
        out_specs=pl.BlockSpec((None, None, tka, tn), lambda ia, j, t: (l, joff + j, ia, 0)),
        out_shape=SDS(buf.shape, F32), input_output_aliases={0: 0},
        compiler_params=_cp("parallel", "parallel", "arbitrary"),
    )(buf, a, b)


def _proj(x, g, w_in, l, tm):
    s = x.shape[0]

    def body(x_ref, g_ref, w_ref, h_ref, p_ref):
        xf = x_ref[...]
        h = (xf * _rsq(xf, NORM_EPS) * g_ref[...]).astype(BF16)
        h_ref[...] = h
        p_ref[...] = jnp.dot(h, w_ref[...], preferred_element_type=F32)

    return pl.pallas_call(
        body, name="proj", grid=(s // tm,),
        in_specs=[pl.BlockSpec((tm, D), lambda i: (i, 0)), pl.BlockSpec((1, D), lambda i: (0, 0)),
                  pl.BlockSpec((None, D, P_IN), lambda i: (l, 0, 0))],
        out_specs=[pl.BlockSpec((tm, D), lambda i: (i, 0)), pl.BlockSpec((tm, P_IN), lambda i: (i, 0))],
        out_shape=[SDS((s, D), BF16), SDS((s, P_IN), F32)],
        compiler_params=_cp("parallel"),
    )(x, g, w_in)


def _mm_nt(a, w, l, tm, name):
    s, k_dim = a.shape
    n = w.shape[1]

    def body(a_ref, w_ref, o_ref):
        o_ref[...] = lax.dot_general(a_ref[...], w_ref[...], NT, preferred_element_type=F32)

    return pl.pallas_call(
        body, name=name, grid=(s // tm,),
        in_specs=[pl.BlockSpec((tm, k_dim), lambda i: (i, 0)), pl.BlockSpec((None, n, k_dim), lambda i: (l, 0, 0))],
        out_specs=pl.BlockSpec((tm, n), lambda i: (i, 0)),
        out_shape=SDS((s, n), F32), compiler_params=_cp("parallel"),
    )(a, w)


def _mm_nt_rmsbwd(dp, w_in, l, x, g, dx1, tm):
    s = x.shape[0]

    def body(dp_ref, w_ref, x_ref, g_ref, dx1_ref, dx_ref, dg_ref):
        dh = lax.dot_general(dp_ref[...], w_ref[...], NT, preferred_element_type=F32)
        dx, dgr = _rms_bwd_rows(x_ref[...], g_ref[...], dh)
        dx_ref[...] = dx1_ref[...] + dx
        part = jnp.sum(dgr, axis=0, keepdims=True)

        @pl.when(pl.program_id(0) == 0)
        def _():
            dg_ref[...] = part

        @pl.when(pl.program_id(0) > 0)
        def _():
            dg_ref[...] += part

    row = pl.BlockSpec((tm, D), lambda i: (i, 0))
    vec = pl.BlockSpec((1, D), lambda i: (0, 0))
    return pl.pallas_call(
        body, name="mix_bwd_dx", grid=(s // tm,),
        in_specs=[pl.BlockSpec((tm, P_IN), lambda i: (i, 0)), pl.BlockSpec((None, D, P_IN), lambda i: (l, 0, 0)), row, vec, row],
        out_specs=[row, vec], out_shape=[SDS((s, D), F32), SDS((1, D), F32)],
        compiler_params=_cp("arbitrary"),
    )(dp, w_in, x, g, dx1)


def _row_iota(shape):
    return lax.broadcasted_iota(jnp.int32, shape, 0)


def _lru_gates(xc, wa_ref, ba_ref, wx_ref, bx_ref, lam_ref):
    xb = xc.astype(BF16)
    r = _sig(jnp.dot(xb, wa_ref[...], preferred_element_type=F32) + ba_ref[...])
    ig = _sig(jnp.dot(xb, wx_ref[...], preferred_element_type=F32) + bx_ref[...])
    nl = -lam_ref[...]
    sp = jnp.maximum(nl, 0.0) + jnp.log(1.0 + jnp.exp(-jnp.abs(nl)))
    log_a = -LRU_C * r * sp
    a = jnp.exp(log_a)
    x2 = 2.0 * log_a
    series = x2 * (1.0 + x2 * (0.5 + x2 * (1.0 / 6.0 + x2 * (1.0 / 24.0 + x2 * (1.0 / 120.0)))))
    em1 = jnp.where(x2 > -0.05, series, jnp.exp(x2) - 1.0)
    mlt = jnp.sqrt(-em1)
    return r, ig, a, mlt, sp


def _conv_taps(src_ref, w_ref, k_taps, pad, tc):
    acc = None
    for j in range(k_taps):
        term = w_ref[j:j + 1, :] * src_ref[pl.ds(pad - (k_taps - 1) + j, tc), :]
        acc = term if acc is None else acc + term
    return acc


def _fill_shifted(src_ref, sh_ref):
    n = src_ref.shape[0] - 8
    for s in range(1, 8):
        sh_ref[s, 0:n, :] = src_ref[pl.ds(s, n), :]


def _shifted_rows(src_ref, sh_ref, offset, tc):
    if offset % 8 == 0:
        return src_ref[pl.ds(offset, tc), :]
    return sh_ref[offset % 8, pl.ds(offset - offset % 8, tc), :]


def _gelu_parts(x):
    c0 = math.sqrt(2.0 / math.pi)
    inner = c0 * (x + 0.044715 * x * x * x)
    t = jnp.tanh(inner)
    gl = 0.5 * x * (1.0 + t)
    dgl = 0.5 * (1.0 + t) + 0.5 * x * (1.0 - t * t) * c0 * (1.0 + 3.0 * 0.044715 * x * x)
    return gl, dgl


def _lru_fwd(proj, cw, cb, wa, ba, wx, bx, lam, gg, tc):
    s = proj.shape[0]
    pad = 8

    def body(xcur_ref, xprev_ref, gate_ref, cw_ref, cb_ref, wa_ref, ba_ref, wx_ref, bx_ref, lam_ref, gg_ref,
             yn_ref, h_ref, xs_ref, hc_ref):
        i = pl.program_id(0)

        @pl.when(i == 0)
        def _():
            hc_ref[...] = jnp.zeros_like(hc_ref)

        xs_ref[0:pad, :] = jnp.where(i > 0, xprev_ref[tc - pad:tc, :], 0.0)
        xs_ref[pad:pad + tc, :] = xcur_ref[...]
        xc = _conv_taps(xs_ref, cw_ref, LRU_K, pad, tc) + cb_ref[...]
        _, ig, a, mlt, _ = _lru_gates(xc, wa_ref, ba_ref, wx_ref, bx_ref, lam_ref)
        u = mlt * (ig * xc)
        row = _row_iota((tc, W_A))
        d = 1
        while d < tc:
            ok = row >= d
            a_sh = jnp.where(ok, pltpu.roll(a, d, axis=0), 1.0)
            u_sh = jnp.where(ok, pltpu.roll(u, d, axis=0), 0.0)
            u = a * u_sh + u
            a = a * a_sh
            d *= 2
        h = u + a * hc_ref[...]
        hc_ref[...] = jnp.sum(jnp.where(row == tc - 1, h, 0.0), axis=0, keepdims=True)
        h_ref[...] = h
        gl, _ = _gelu_parts(gate_ref[...])
        ya = gl * h
        yn_ref[...] = (ya * _rsq(ya, NORM_EPS) * gg_ref[...]).astype(BF16)

    blk = lambda c: pl.BlockSpec((tc, W_A), lambda i, c=c: (i, c))
    full = lambda a: pl.BlockSpec(a.shape, lambda i: (0,) * a.ndim)
    params = [cw, cb, wa, ba, wx, bx, lam, gg]
    return pl.pallas_call(
        body, name="lru_fwd", grid=(s // tc,),
        in_specs=[blk(0), pl.BlockSpec((tc, W_A), lambda i: (jnp.maximum(i - 1, 0), 0)), blk(1)] + [full(a) for a in params],
        out_specs=[pl.BlockSpec((tc, W_A), lambda i: (i, 0))] * 2,
        out_shape=[SDS((s, W_A), BF16), SDS((s, W_A), F32)],
        scratch_shapes=[pltpu.VMEM((tc + pad, W_A), F32), pltpu.VMEM((1, W_A), F32)],
        compiler_params=_cp("arbitrary"),
    )(proj, proj, proj, *params)


def _acc(ref, first, val):
    @pl.when(first)
    def _():
        ref[...] = val

    @pl.when(jnp.logical_not(first))
    def _():
        ref[...] += val


def _lru_bwd(dy, proj, h, cw, cb, wa, ba, wx, bx, lam, gg, tc):
    s = proj.shape[0]
    nc = s // tc
    pad = 8

    def body(dy_ref, xcur_ref, xprev_ref, gate_ref, h_ref, hprev_ref, cw_ref, cb_ref, wa_ref, ba_ref, wx_ref, bx_ref,
             lam_ref, gg_ref,
             dp_ref, dcw_ref, dcb_ref, dwa_ref, dba_ref, dwx_ref, dbx_ref, dlam_ref, dgg_ref,
             xs_ref, ds_ref, mu_ref, nx_ref):
        step = pl.program_id(0)
        i = nc - 1 - step
        first = step == 0

        @pl.when(first)
        def _():
            mu_ref[...] = jnp.zeros_like(mu_ref)
            nx_ref[...] = jnp.zeros_like(nx_ref)

        xs_ref[0:pad, :] = jnp.where(i > 0, xprev_ref[tc - pad:tc, :], 0.0)
        xs_ref[pad:pad + tc, :] = xcur_ref[...]
        xc = _conv_taps(xs_ref, cw_ref, LRU_K, pad, tc) + cb_ref[...]
        r, ig, a, mlt, sp = _lru_gates(xc, wa_ref, ba_ref, wx_ref, bx_ref, lam_ref)
        hh = h_ref[...]
        gate = gate_ref[...]
        gl, dgl = _gelu_parts(gate)
        ya = gl * hh
        dya, dggr = _rms_bwd_rows(ya, gg_ref[...], dy_ref[...])
        _acc(dgg_ref, first, jnp.sum(dggr, axis=0, keepdims=True))
        dp_ref[:, W_A:2 * W_A] = dya * hh * dgl
        dh = dya * gl

        row = _row_iota((tc, W_A))
        aa = a
        uu = a * dh
        d = 1
        while d < tc:
            ok = row < tc - d
            a_sh = jnp.where(ok, pltpu.roll(aa, tc - d, axis=0), 1.0)
            u_sh = jnp.where(ok, pltpu.roll(uu, tc - d, axis=0), 0.0)
            uu = uu + aa * u_sh
            aa = aa * a_sh
            d *= 2
        cin = mu_ref[...]
        mu = uu + aa * cin
        lam_t = dh + jnp.where(row == tc - 1, cin, pltpu.roll(mu, tc - 1, axis=0))
        mu_ref[...] = jnp.sum(jnp.where(row == 0, mu, 0.0), axis=0, keepdims=True)
        hm1 = jnp.where(row == 0, jnp.where(i > 0, pltpu.roll(hprev_ref[...], 1, axis=0), 0.0),
                        pltpu.roll(hh, 1, axis=0))
        da = lam_t * hm1
        du = lam_t
        dmlt = du * ig * xc
        dig = du * mlt * xc
        dxc = du * mlt * ig
        dlog_a = da * a - dmlt * (a * a / mlt)
        dr = dlog_a * (-LRU_C * sp)
        dsp = jnp.sum(dlog_a * (-LRU_C * r), axis=0, keepdims=True)
        _acc(dlam_ref, first, dsp * (-_sig(-lam_ref[...])))
        dga = dr * r * (1.0 - r)
        dgx = dig * ig * (1.0 - ig)
        _acc(dba_ref, first, jnp.sum(dga, axis=0, keepdims=True))
        _acc(dbx_ref, first, jnp.sum(dgx, axis=0, keepdims=True))
        xb = xc.astype(BF16)
        dgab = dga.astype(BF16)
        dgxb = dgx.astype(BF16)
        _acc(dwa_ref, first, lax.dot_general(xb, dgab, TN, preferred_element_type=F32))
        _acc(dwx_ref, first, lax.dot_general(xb, dgxb, TN, preferred_element_type=F32))
        dxc = (dxc + lax.dot_general(dgab, wa_ref[...], NT, preferred_element_type=F32)
               + lax.dot_general(dgxb, wx_ref[...], NT, preferred_element_type=F32))

        _acc(dcb_ref, first, jnp.sum(dxc, axis=0, keepdims=True))
        r8 = _row_iota((8, W_A))
        dcw = jnp.zeros((8, W_A), F32)
        for j in range(LRU_K):
            tap = jnp.sum(dxc * xs_ref[pl.ds(pad - (LRU_K - 1) + j, tc), :], axis=0, keepdims=True)
            dcw = dcw + jnp.where(r8 == j, tap, 0.0)
        _acc(dcw_ref, first, dcw)
        ds_ref[0:tc, :] = dxc
        ds_ref[tc:tc + pad, :] = nx_ref[...]
        dlx = None
        for j in range(LRU_K):
            term = cw_ref[j:j + 1, :] * ds_ref[pl.ds(LRU_K - 1 - j, tc), :]
            dlx = term if dlx is None else dlx + term
        dp_ref[:, 0:W_A] = dlx
        nx_ref[...] = dxc[0:pad, :]

    rev = lambda c: pl.BlockSpec((tc, W_A), lambda t, c=c: (nc - 1 - t, c))
    prev = lambda c: pl.BlockSpec((tc, W_A), lambda t, c=c: (jnp.maximum(nc - 2 - t, 0), c))
    full = lambda a: pl.BlockSpec(a.shape, lambda t: (0,) * a.ndim)
    params = [cw, cb, wa, ba, wx, bx, lam, gg]
    vec = SDS((1, W_A), F32)
    sq = SDS((W_A, W_A), F32)
    outs = [SDS((s, 2 * W_A), F32), SDS((8, W_A), F32), vec, sq, vec, sq, vec, vec, vec]
    return pl.pallas_call(
        body, name="lru_bwd", grid=(nc,),
        in_specs=[rev(0), rev(0), prev(0), rev(1), rev(0), prev(0)] + [full(a) for a in params],
        out_specs=[pl.BlockSpec((tc, 2 * W_A), lambda t: (nc - 1 - t, 0))]
        + [pl.BlockSpec(o.shape, lambda t: (0, 0)) for o in outs[1:]],
        out_shape=outs,
        scratch_shapes=[pltpu.VMEM((tc + pad, W_A), F32), pltpu.VMEM((tc + pad, W_A), F32),
                        pltpu.VMEM((1, W_A), F32), pltpu.VMEM((pad, W_A), F32)],
        compiler_params=_cp("arbitrary"),
    )(dy, proj, proj, proj, h, h, *params)


def _attn_stack(qa, qb, kvh):
    lane = lax.broadcasted_iota(jnp.int32, qa.shape, 1)
    keep = (lane >= HD) if kvh == 1 else (lane < HD)
    parts = []
    for tile in (qa, qb):
        for half in (0, 1):
            y = tile if half == kvh else pltpu.roll(tile, HD, axis=1)
            parts.append(jnp.where(keep, y, 0.0))
    return jnp.concatenate(parts, axis=0)


def _attn_unstack(o, kvh):
    lane = lax.broadcasted_iota(jnp.int32, (BLK, 2 * HD), 1)
    tiles = []
    for t in range(2):
        halves = []
        for half in (0, 1):
            blk = o[(2 * t + half) * BLK:(2 * t + half + 1) * BLK, :]
            halves.append(blk if half == kvh else pltpu.roll(blk, HD, axis=1))
        tiles.append(jnp.where(lane < HD, halves[0], halves[1]))
    return tiles


def _attn_stack_all(x_ref_or_val):
    return jnp.concatenate([_attn_stack(x_ref_or_val[:, 256 * kvh:256 * kvh + 128],
                                        x_ref_or_val[:, 256 * kvh + 128:256 * kvh + 256], kvh) for kvh in range(2)], axis=0)


def _attn_unstack_all(o, dst_ref):
    for kvh in range(2):
        ta, tb = _attn_unstack(o[4 * BLK * kvh:4 * BLK * (kvh + 1), :], kvh)
        dst_ref[:, 256 * kvh:256 * kvh + 128] = ta
        dst_ref[:, 256 * kvh + 128:256 * kvh + 256] = tb


def _attn_windows(cur_ref, prev_ref, nb):
    blocks = [prev_ref[...]] + [cur_ref[b * BLK:(b + 1) * BLK, :] for b in range(nb)]
    return [jnp.concatenate(blocks[b:b + 2], axis=0).astype(BF16) for b in range(nb)]


def _attn_bias():
    qi = np.arange(NQ * BLK)[:, None] % BLK
    kj = np.arange(2 * BLK)[None, :]
    rel = BLK + qi - kj
    ok = (rel >= 0) & (rel < BLK)
    return jnp.asarray(np.stack([np.where(ok & (kj >= BLK), 0.0, NEG_BIG), np.where(ok, 0.0, NEG_BIG)]), F32)


def _attn_probs(qs, kw, first, sink_ref, bias_ref):
    rows = NQ * BLK
    bias = bias_ref[1] if first is False else jnp.where(first, bias_ref[0], bias_ref[1])
    sh = lax.dot_general(qs.astype(BF16), kw, NT, preferred_element_type=F32) * SCALE + bias
    head = lax.broadcasted_iota(jnp.int32, (rows, 1), 0) // BLK
    sk = jnp.zeros((rows, 1), F32)
    for h in range(NQ):
        sk = jnp.where(head == h, sink_ref[h:h + 1, 0:1], sk)
    m = jnp.maximum(jnp.max(sh, axis=-1, keepdims=True), sk)
    e = jnp.exp(sh - m)
    es = jnp.exp(sk - m)
    rz = 1.0 / (jnp.sum(e, axis=-1, keepdims=True) + es)
    return e * rz, es * rz


def _attn_fwd(proj, sinks8, gg):
    s = proj.shape[0]
    nb = ATT_NB_FWD

    def body(q_ref, kc_ref, kp_ref, vc_ref, vp_ref, sink_ref, gg_ref, bias_ref, yn_ref, ob_ref):
        kws, vws = _attn_windows(kc_ref, kp_ref, nb), _attn_windows(vc_ref, vp_ref, nb)
        for b in range(nb):
            rows = pl.ds(b * BLK, BLK)
            first = (pl.program_id(0) == 0) if b == 0 else False
            p, _ = _attn_probs(_attn_stack_all(q_ref.at[rows, :]), kws[b], first, sink_ref, bias_ref)
            _attn_unstack_all(jnp.dot(p.astype(BF16), vws[b], preferred_element_type=F32), ob_ref.at[rows, :])
        ob = ob_ref[...]
        yn_ref[...] = (ob * _rsq(ob, NORM_EPS) * gg_ref[...]).astype(BF16)

    tb = nb * BLK
    cur = lambda c: pl.BlockSpec((tb, 128), lambda m, c=c: (m, c))
    prev = lambda c: pl.BlockSpec((BLK, 128), lambda m, c=c: (jnp.maximum(nb * m - 1, 0), c))
    out = pl.BlockSpec((tb, W_B), lambda m: (m, 0))
    return pl.pallas_call(
        body, name="attn_fwd", grid=(s // tb,),
        in_specs=[pl.BlockSpec((tb, W_B), lambda m: (m, 1)), cur(8), prev(8), cur(9), prev(9),
                  pl.BlockSpec((8, 128), lambda n: (0, 0)), pl.BlockSpec((1, W_B), lambda n: (0, 0)),
                  pl.BlockSpec((2, NQ * BLK, 2 * BLK), lambda n: (0, 0, 0))],
        out_specs=[out, out], out_shape=[SDS((s, W_B), BF16), SDS((s, W_B), F32)],
        compiler_params=_cp("parallel"),
    )(proj, proj, proj, proj, proj, sinks8, gg, _attn_bias())


def _attn_bwd(dy, proj, ob, sinks8, gg):
    s = proj.shape[0]
    nb = ATT_NB_BWD

    def body(dya_ref, dyb_ref, q_ref, kc_ref, kp_ref, vc_ref, vp_ref, ob_ref, sink_ref, gg_ref, bias_ref,
             dq_ref, dcur_ref, dprev_ref, dsink_ref, dgg_ref):
        first = pl.program_id(0) == 0
        kws, vws = _attn_windows(kc_ref, kp_ref, nb), _attn_windows(vc_ref, vp_ref, nb)
        dyn = jnp.concatenate([dya_ref[...], dyb_ref[...]], axis=1)
        dob, dggr = _rms_bwd_rows(ob_ref[...], gg_ref[...], dyn)
        _acc(dgg_ref, first, jnp.sum(dggr, axis=0, keepdims=True))
        r8 = _row_iota((8, 128))
        dsk = jnp.zeros((8, 128), F32)
        for b in range(nb):
            rows = pl.ds(b * BLK, BLK)
            qs = _attn_stack_all(q_ref.at[rows, :])
            p, psink = _attn_probs(qs, kws[b], first if b == 0 else False, sink_ref, bias_ref)
            dosb = _attn_stack_all(dob[b * BLK:(b + 1) * BLK, :]).astype(BF16)
            dp = lax.dot_general(dosb, vws[b], NT, preferred_element_type=F32)
            dd = jnp.sum(p * dp, axis=-1, keepdims=True)
            dsb = (p * (dp - dd) * SCALE).astype(BF16)
            dsink_rows = -psink * dd
            for h in range(NQ):
                dsk = dsk + jnp.where(r8 == h, jnp.sum(dsink_rows[h * BLK:(h + 1) * BLK, :], axis=0, keepdims=True), 0.0)
            _attn_unstack_all(jnp.dot(dsb, kws[b], preferred_element_type=F32), dq_ref.at[rows, :])
            dkw = lax.dot_general(dsb, qs.astype(BF16), TN, preferred_element_type=F32)
            dvw = lax.dot_general(p.astype(BF16), dosb, TN, preferred_element_type=F32)
            dprev_ref[rows, 0:128] = dkw[0:BLK, :]
            dprev_ref[rows, 128:256] = dvw[0:BLK, :]
            dcur_ref[rows, 0:128] = dkw[BLK:2 * BLK, :]
            dcur_ref[rows, 128:256] = dvw[BLK:2 * BLK, :]
        _acc(dsink_ref, first, dsk)

    tb = nb * BLK
    cur = lambda c: pl.BlockSpec((tb, 128), lambda m, c=c: (m, c))
    prev = lambda c: pl.BlockSpec((BLK, 128), lambda m, c=c: (jnp.maximum(nb * m - 1, 0), c))
    wide = pl.BlockSpec((tb, W_B), lambda m: (m, 0))
    half = pl.BlockSpec((tb, 256), lambda m: (m, 0))
    return pl.pallas_call(
        body, name="attn_bwd", grid=(s // tb,),
        in_specs=[pl.BlockSpec((tb, 256), lambda m: (m, 1)), pl.BlockSpec((tb, 256), lambda m: (m, 2)),
                  pl.BlockSpec((tb, W_B), lambda m: (m, 1)), cur(8), prev(8), cur(9), prev(9), wide,
                  pl.BlockSpec((8, 128), lambda n: (0, 0)), pl.BlockSpec((1, W_B), lambda n: (0, 0)),
                  pl.BlockSpec((2, NQ * BLK, 2 * BLK), lambda n: (0, 0, 0))],
        out_specs=[wide, half, half, pl.BlockSpec((8, 128), lambda n: (0, 0)), pl.BlockSpec((1, W_B), lambda n: (0, 0))],
        out_shape=[SDS((s, W_B), F32), SDS((s, 256), F32), SDS((s, 256), F32), SDS((8, 128), F32), SDS((1, W_B), F32)],
        compiler_params=_cp("arbitrary"),
    )(dy, dy, proj, proj, proj, proj, proj, ob, sinks8, gg, _attn_bias())


def _ln_parts(y1, eps=LN_EPS):
    mu = jnp.mean(y1, axis=-1, keepdims=True)
    xc = y1 - mu
    rstd = lax.rsqrt(jnp.mean(xc * xc, axis=-1, keepdims=True) + eps)
    return xc * rstd, rstd


def _conf_fwd(proj, cw, cb, lg, lb, gg, tc):
    s = proj.shape[0]
    pad = 32

    def body(ac_ref, gc_ref, ap_ref, gp_ref, cw_ref, cb_ref, lg_ref, lb_ref, gg_ref, yn_ref, y1_ref, ys_ref, sh_ref):
        i = pl.program_id(0)
        tail = ap_ref[tc - pad:tc, :] * _sig(gp_ref[tc - pad:tc, :])
        ys_ref[0:pad, :] = jnp.where(i > 0, tail, 0.0)
        ys_ref[pad:pad + tc, :] = ac_ref[...] * _sig(gc_ref[...])
        _fill_shifted(ys_ref, sh_ref)
        y1 = cb_ref[...]
        for j in range(CONV_K):
            y1 = y1 + cw_ref[j:j + 1, :] * _shifted_rows(ys_ref, sh_ref, pad - (CONV_K - 1) + j, tc)
        y1_ref[...] = y1
        xh, _ = _ln_parts(y1)
        yl = xh * lg_ref[...] + lb_ref[...]
        yc = yl * _sig(yl)
        yn_ref[...] = (yc * _rsq(yc, NORM_EPS) * gg_ref[...]).astype(BF16)

    cur = lambda c: pl.BlockSpec((tc, W_C), lambda i, c=c: (i, c))
    prev = lambda c: pl.BlockSpec((tc, W_C), lambda i, c=c: (jnp.maximum(i - 1, 0), c))
    full = lambda a: pl.BlockSpec(a.shape, lambda i: (0,) * a.ndim)
    params = [cw, cb, lg, lb, gg]
    out = pl.BlockSpec((tc, W_C), lambda i: (i, 0))
    return pl.pallas_call(
        body, name="conf_fwd", grid=(s // tc,),
        in_specs=[cur(5), cur(6), prev(5), prev(6)] + [full(a) for a in params],
        out_specs=[out, out], out_shape=[SDS((s, W_C), BF16), SDS((s, W_C), F32)],
        scratch_shapes=[pltpu.VMEM((tc + pad, W_C), F32), pltpu.VMEM((8, tc + pad, W_C), F32)],
        compiler_params=_cp("parallel"),
    )(proj, proj, proj, proj, *params)


def _conf_bwd(dy, proj, y1, cw, cb, lg, lb, gg, tc):
    s = proj.shape[0]
    nc = s // tc
    pad = 32

    def body(dy_ref, ac_ref, gc_ref, ap_ref, gp_ref, y1_ref, cw_ref, cb_ref, lg_ref, lb_ref, gg_ref,
             dp_ref, dcw_ref, dcb_ref, dlg_ref, dlb_ref, dgg_ref, ys_ref, ds_ref, nx_ref, ysh_ref, dsh_ref):
        step = pl.program_id(0)
        i = nc - 1 - step
        first = step == 0

        @pl.when(first)
        def _():
            nx_ref[...] = jnp.zeros_like(nx_ref)

        a = ac_ref[...]
        sg = _sig(gc_ref[...])
        tail = ap_ref[tc - pad:tc, :] * _sig(gp_ref[tc - pad:tc, :])
        ys_ref[0:pad, :] = jnp.where(i > 0, tail, 0.0)
        ys_ref[pad:pad + tc, :] = a * sg
        xh, rstd = _ln_parts(y1_ref[...])
        yl = xh * lg_ref[...] + lb_ref[...]
        sl = _sig(yl)
        yc = yl * sl
        dyc, dggr = _rms_bwd_rows(yc, gg_ref[...], dy_ref[...])
        _acc(dgg_ref, first, jnp.sum(dggr, axis=0, keepdims=True))
        dyl = dyc * sl * (1.0 + yl * (1.0 - sl))
        _acc(dlg_ref, first, jnp.sum(dyl * xh, axis=0, keepdims=True))
        _acc(dlb_ref, first, jnp.sum(dyl, axis=0, keepdims=True))
        dxh = dyl * lg_ref[...]
        dy1 = rstd * (dxh - jnp.mean(dxh, axis=-1, keepdims=True) - xh * jnp.mean(dxh * xh, axis=-1, keepdims=True))
        _acc(dcb_ref, first, jnp.sum(dy1, axis=0, keepdims=True))
        r32 = _row_iota((32, W_C))
        dcw = jnp.zeros((32, W_C), F32)
        _fill_shifted(ys_ref, ysh_ref)
        for j in range(CONV_K):
            tap = jnp.sum(dy1 * _shifted_rows(ys_ref, ysh_ref, pad - (CONV_K - 1) + j, tc), axis=0, keepdims=True)
            dcw = dcw + jnp.where(r32 == j, tap, 0.0)
        _acc(dcw_ref, first, dcw)
        ds_ref[0:tc, :] = dy1
        ds_ref[tc:tc + pad, :] = nx_ref[...]
        _fill_shifted(ds_ref, dsh_ref)
        dy0 = None
        for j in range(CONV_K):
            term = cw_ref[j:j + 1, :] * _shifted_rows(ds_ref, dsh_ref, CONV_K - 1 - j, tc)
            dy0 = term if dy0 is None else dy0 + term
        dp_ref[:, 0:W_C] = dy0 * sg
        dp_ref[:, W_C:2 * W_C] = dy0 * a * sg * (1.0 - sg)
        nx_ref[...] = dy1[0:pad, :]

    rev = lambda c: pl.BlockSpec((tc, W_C), lambda t, c=c: (nc - 1 - t, c))
    prev = lambda c: pl.BlockSpec((tc, W_C), lambda t, c=c: (jnp.maximum(nc - 2 - t, 0), c))
    full = lambda a: pl.BlockSpec(a.shape, lambda t: (0,) * a.ndim)
    params = [cw, cb, lg, lb, gg]
    vec = SDS((1, W_C), F32)
    outs = [SDS((s, 2 * W_C), F32), SDS((32, W_C), F32), vec, vec, vec, vec]
    return pl.pallas_call(
        body, name="conf_bwd", grid=(nc,),
        in_specs=[rev(3), rev(5), rev(6), prev(5), prev(6), rev(0)] + [full(a) for a in params],
        out_specs=[pl.BlockSpec((tc, 2 * W_C), lambda t: (nc - 1 - t, 0))]
        + [pl.BlockSpec(o.shape, lambda t: (0, 0)) for o in outs[1:]],
        out_shape=outs,
        scratch_shapes=[pltpu.VMEM((tc + pad, W_C), F32), pltpu.VMEM((tc + pad, W_C), F32), pltpu.VMEM((pad, W_C), F32),
                        pltpu.VMEM((8, tc + pad, W_C), F32), pltpu.VMEM((8, tc + pad, W_C), F32)],
        compiler_params=_cp("arbitrary"),
    )(dy, proj, proj, proj, proj, y1, *params)


def _assemble_dproj(dlru, dq, dcur, dprev, dconf):
    s = dq.shape[0]
    nb = s // BLK

    def body(dl_ref, dq_ref, dc_ref, dn_ref, df_ref, o_ref):
        n = pl.program_id(0)
        o_ref[:, 0:512] = dl_ref[...].astype(BF16)
        o_ref[:, 512:1024] = dq_ref[...].astype(BF16)
        o_ref[:, 1024:1280] = (dc_ref[...] + jnp.where(n < nb - 1, dn_ref[...], 0.0)).astype(BF16)
        o_ref[:, 1280:1792] = df_ref[...].astype(BF16)

    wide = pl.BlockSpec((BLK, 512), lambda n: (n, 0))
    return pl.pallas_call(
        body, name="assemble_dproj", grid=(nb,),
        in_specs=[wide, wide, pl.BlockSpec((BLK, 256), lambda n: (n, 0)),
                  pl.BlockSpec((BLK, 256), lambda n: (jnp.minimum(n + 1, nb - 1), 0)), wide],
        out_specs=pl.BlockSpec((BLK, P_IN), lambda n: (n, 0)), out_shape=SDS((s, P_IN), BF16),
        compiler_params=_cp("parallel"),
    )(dlru, dq, dcur, dprev, dconf)


def _loss_grad(y, t, tm):
    s = y.shape[0]

    def body(y_ref, t_ref, dy_ref, l_ref):
        err = y_ref[...] - t_ref[...]
        dy_ref[...] = err * (1.0 / D)
        _acc(l_ref, pl.program_id(0) == 0, jnp.sum(err * err, axis=0, keepdims=True))

    row = pl.BlockSpec((tm, D), lambda i: (i, 0))
    return pl.pallas_call(
        body, name="loss_grad", grid=(s // tm,), in_specs=[row, row],
        out_specs=[row, pl.BlockSpec((1, D), lambda i: (0, 0))],
        out_shape=[SDS((s, D), F32), SDS((1, D), F32)], compiler_params=_cp("arbitrary"),
    )(y, t)


def _block_diag(w):
    rows = [jnp.concatenate([w[h] if k == h else jnp.zeros((64, 64), w.dtype) for k in range(4)], axis=1) for h in range(4)]
    return jnp.concatenate(rows, axis=0)


def _diag_blocks(m):
    return jnp.stack([m[64 * h:64 * (h + 1), 64 * h:64 * (h + 1)] for h in range(4)])


def _layer_params(small, l):
    v = lambda name: small[name][l].reshape(1, -1)
    gg = small["group_g"][l]
    return dict(
        ffn1_pre=v("ffn1_pre_g"), ffn1_post=v("ffn1_post_g"), mix_pre=v("mix_pre_g"), mix_post=v("mix_post_g"),
        ffn2_pre=v("ffn2_pre_g"), ffn2_post=v("ffn2_post_g"), lru_cb=v("lru_conv_b"),
        wa=_block_diag(small["lru_w_a"][l]).astype(BF16), ba=v("lru_b_a"),
        wx=_block_diag(small["lru_w_x"][l]).astype(BF16), bx=v("lru_b_x"), lam=v("lru_lambda"),
        sinks8=jnp.broadcast_to(small["attn_sinks"][l][:, None], (NQ, 128)),
        conv_b=v("conv_b"), ln_g=v("conv_ln_g"), ln_b=v("conv_ln_b"),
        gg_a=gg[0:W_A].reshape(1, -1), gg_b=gg[W_A:W_A + W_B].reshape(1, -1), gg_c=gg[W_A + W_B:].reshape(1, -1),
    )


def _forward_layer(x, weights, p, tiles, deps=()):
    _, mm, _, tc, _ = tiles
    big = dict(weights("ffn1_gu", x))
    p = dict(p)
    sv = dict(x0=x)
    h1, g1, u1, a1 = _ffn_up(x, p["ffn1_pre"], big["ffn1_w_gu"], 0, mm, deps)
    big.update(weights("ffn1_down", a1))
    z1, x = _mm_rms_res(a1, big["ffn1_w_down"], 0, x, p["ffn1_post"], 0.5, mm, DFF, "ffn_down")
    sv.update(h1=h1, g1=g1, u1=u1, a1=a1, z1=z1, x1=x)
    big.update(weights("mix", x))
    p.update(lru_cw=big.pop("lru_conv_w"), conv_w=big.pop("conv_w"))
    hn, proj = _proj(x, p["mix_pre"], big["w_in"], 0, mm)
    yn_a, hl = _lru_fwd(proj, p["lru_cw"], p["lru_cb"], p["wa"], p["ba"], p["wx"], p["bx"], p["lam"], p["gg_a"], tc)
    yn_b, ob = _attn_fwd(proj, p["sinks8"], p["gg_b"])
    yn_c, y1 = _conf_fwd(proj, p["conv_w"], p["conv_b"], p["ln_g"], p["ln_b"], p["gg_c"], tc)
    ycat = jnp.concatenate([yn_a, yn_b, yn_c], axis=1)
    zo, x = _mm_rms_res(ycat, big["w_out"], 0, x, p["mix_post"], 1.0, mm, D, "mix_out")
    sv.update(hn=hn, proj=proj, hl=hl, ob=ob, y1=y1, ycat=ycat, zo=zo, x2=x)
    big.update(weights("ffn2", x))
    h2, g2, u2, a2 = _ffn_up(x, p["ffn2_pre"], big["ffn2_w_gu"], 0, mm)
    z2, x = _mm_rms_res(a2, big["ffn2_w_down"], 0, x, p["ffn2_post"], 0.5, mm, DFF, "ffn_down")
    sv.update(h2=h2, g2=g2, u2=u2, a2=a2, z2=z2, p=p, big=big)
    return x, sv


def _grad_buffers():
    empty = lambda *shape: lax.empty(shape, F32)
    return dict(ffn1_w_gu=empty(1, NSHARD, D, FH), ffn2_w_gu=empty(1, NSHARD, D, FH), ffn1_w_down=empty(1, 1, DFF, D),
                ffn2_w_down=empty(1, 1, DFF, D), w_in=empty(1, 1, D, P_IN), w_out=empty(1, 1, D, D))


def _backward_layer(dx, sv, bufs, tiles, stage):
    p, big = sv["p"], sv["big"]
    tm, mm, dw, tc, dh_rows = tiles
    gr = {}

    def ffn_bwd(dx, which, xin, h, g, u, a, z, pre, post, deps):
        dz, dpost = _rms_bwd(dx, z, post, 0.5, tm, "ffn_post_bwd", deps)
        dgu = _ffn_bwd_mid(dz, big[which + "_w_down"], 0, g, u, mm)
        bufs[which + "_w_down"] = _mm_tn_into(bufs[which + "_w_down"], a, dz, 0, 0, FH, D, dw, "dw_down")
        bufs[which + "_w_gu"] = _mm_tn_into(bufs[which + "_w_gu"], h, dgu, 0, 0, D, FH, dw, "dw_gate", 2, 0)
        bufs[which + "_w_gu"] = _mm_tn_into(bufs[which + "_w_gu"], h, dgu, 0, 2, D, FH, dw, "dw_up", 2, 1)
        deps = stage({n: bufs[n] for n in (which + "_w_gu", which + "_w_down")}, bufs[which + "_w_gu"])
        dxn, dpre = _ffn_bwd_dh(dgu, big[which + "_w_gu"], 0, xin, pre, dx, dh_rows, deps)
        return dxn, dpre, dpost

    dx, gr["ffn2_pre_g"], gr["ffn2_post_g"] = ffn_bwd(dx, "ffn2", sv["x2"], sv["h2"], sv["g2"], sv["u2"], sv["a2"],
                                                      sv["z2"], p["ffn2_pre"], p["ffn2_post"], ())
    do, gr["mix_post_g"] = _rms_bwd(dx, sv["zo"], p["mix_post"], 1.0, tm, "mix_post_bwd")
    bufs["w_out"] = _mm_tn_into(bufs["w_out"], sv["ycat"], do, 0, 0, D, D, dw, "dw_out")
    dy = _mm_nt(do, big["w_out"], 0, mm, "mix_dy")
    proj = sv["proj"]
    (dlru, dcw, gr["lru_conv_b"], dwa, gr["lru_b_a"], dwx, gr["lru_b_x"], gr["lru_lambda"], dgg_a) = _lru_bwd(
        dy, proj, sv["hl"], p["lru_cw"], p["lru_cb"], p["wa"], p["ba"], p["wx"], p["bx"], p["lam"], p["gg_a"], tc)
    dq, dcur, dprev, dsk, dgg_b = _attn_bwd(dy, proj, sv["ob"], p["sinks8"], p["gg_b"])
    dconf, dconvw, gr["conv_b"], gr["conv_ln_g"], gr["conv_ln_b"], dgg_c = _conf_bwd(
        dy, proj, sv["y1"], p["conv_w"], p["conv_b"], p["ln_g"], p["ln_b"], p["gg_c"], tc)
    dproj = _assemble_dproj(dlru, dq, dcur, dprev, dconf)
    bufs["w_in"] = _mm_tn_into(bufs["w_in"], sv["hn"], dproj, 0, 0, D, P_IN, dw, "dw_in")
    dx, gr["mix_pre_g"] = _mm_nt_rmsbwd(dproj, big["w_in"], 0, sv["x1"], p["mix_pre"], dx, mm)
    gr["lru_conv_w"] = dcw[0:LRU_K]
    gr["lru_w_a"] = _diag_blocks(dwa)
    gr["lru_w_x"] = _diag_blocks(dwx)
    gr["attn_sinks"] = dsk[:, 0]
    gr["conv_w"] = dconvw[0:CONV_K]
    gr["group_g"] = jnp.concatenate([dgg_a, dgg_b, dgg_c], axis=1)
    dx, gr["ffn1_pre_g"], gr["ffn1_post_g"] = ffn_bwd(dx, "ffn1", sv["x0"], sv["h1"], sv["g1"], sv["u1"], sv["a1"],
                                                      sv["z1"], p["ffn1_pre"], p["ffn1_post"],
                                                      stage({n: bufs[n] for n in ("w_in", "w_out")}, dx))
    return dx, gr


def _tiles(s):
    return min(1024, s), min(1024, s), min(2048, s), min(512, s // 2), min(512, s)


HBM_SPEC = pl.BlockSpec(memory_space=pltpu.HBM)
SEM_SPEC = pl.BlockSpec(memory_space=pltpu.SEMAPHORE)
EFFECT = pltpu.SideEffectType.DATAFLOW_SIDE_EFFECTING


def _place():
    x, y, c = lax.axis_index("x"), lax.axis_index("y"), lax.axis_index("c")
    return x, y, c, [(1 - x, y), (x, 1 - y), (1 - x, 1 - y)]


def _rcopy(src, dst, send_sems, recv_sems, k, to):
    return pltpu.make_async_remote_copy(src_ref=src, dst_ref=dst, send_sem=send_sems.at[k], recv_sem=recv_sems.at[k],
                                        device_id=to, device_id_type=MESH)


def _half(rows, which):
    return pl.ds(which * (rows // 2), rows // 2)


def _place_shard(w, l, p_idx, dtype):
    _, rows, cols = w.shape
    tr = _rows_per_block(rows, cols, 16, SUM_BLOCK_ELEMS) if rows % 16 == 0 else rows

    def body(p_ref, buf_ref, w_ref, o_ref):
        o_ref[...] = w_ref[...].astype(dtype)

    spec = pltpu.PrefetchScalarGridSpec(
        num_scalar_prefetch=1, grid=(rows // tr,),
        in_specs=[ANY, pl.BlockSpec((None, tr, cols), lambda i, pr: (l, i, 0))],
        out_specs=pl.BlockSpec((None, None, tr, cols), lambda i, pr: (0, pr[0], i, 0)))
    shape = (1, NSHARD, rows, cols)
    return pl.pallas_call(body, name="place_shard", grid_spec=spec, out_shape=SDS(shape, dtype),
                          input_output_aliases={1: 0}, compiler_params=_cp("parallel"),
                          )(p_idx, lax.empty(shape, dtype), w)


def _gather_two_level(bufs, n_halved):
    n = len(bufs)

    def body(*refs):
        outs = refs[n:2 * n]
        send_sems, recv_sems = refs[2 * n:]
        x, y, c, chips = _place()
        p = 2 * x + y
        me, sibling = (x, y, c), (x, y, 1 - c)

        def blk(a, q, half):
            return outs[a].at[0, q, _half(outs[a].shape[2], half)] if a < n_halved else outs[a].at[0, q]

        def cp(a, k, q, half, to):
            return _rcopy(blk(a, q, half), blk(a, q, half), send_sems, recv_sems, 6 * a + k, to)

        first = [cp(a, j, p, c, (*chip, c)) for a in range(n) for j, chip in enumerate(chips)]
        for d in first:
            d.start()
        passed = []
        for a in range(n):
            for j, chip in enumerate(chips):
                q = 2 * chip[0] + chip[1]
                cp(a, j, q, c, me).wait_recv()
                if a < n_halved:
                    passed.append(cp(a, 3 + j, q, c, sibling))
                    passed[-1].start()
        for a in range(n_halved):
            for j, chip in enumerate(chips):
                cp(a, 3 + j, 2 * chip[0] + chip[1], 1 - c, me).wait_recv()
        for d in first + passed:
            d.wait_send()

    return pl.pallas_call(
        body, name="gather_layer0", in_specs=[ANY] * n, out_specs=[ANY] * n,
        out_shape=[SDS(b.shape, b.dtype) for b in bufs], input_output_aliases={a: a for a in range(n)},
        scratch_shapes=[pltpu.SemaphoreType.DMA((6 * n,)), pltpu.SemaphoreType.DMA((6 * n,))],
    )(*bufs)


def _run_plans(plans, refs, send_sems, recv_sems):
    cps, b0, s0 = [], 0, 0
    for plan, nb, ns in plans:
        cps += plan(refs[b0:b0 + nb], send_sems, recv_sems, s0)
        b0, s0 = b0 + nb, s0 + ns
    return cps


def _exchange(name, bufs, plans):
    n = len(bufs)
    nsem = sum(ns for _, _, ns in plans)

    def body(*refs):
        cps = _run_plans(plans, refs[n:2 * n], refs[2 * n], refs[2 * n + 1])
        for cp in cps:
            cp.start()
        for cp in cps:
            cp.wait()

    return pl.pallas_call(
        body, name=name, in_specs=[ANY] * n, out_specs=[ANY] * n, out_shape=[SDS(b.shape, b.dtype) for b in bufs],
        input_output_aliases={a: a for a in range(n)},
        scratch_shapes=[pltpu.SemaphoreType.DMA((nsem,)), pltpu.SemaphoreType.DMA((nsem,))],
    )(*bufs)


def _exchange_start(name, bufs, plans, deps=()):
    n = len(bufs)
    nsem = sum(ns for _, _, ns in plans)
    deps = list(deps)
    first_out = n + len(deps)

    def body(*refs):
        for cp in _run_plans(plans, refs[:n], refs[first_out], refs[first_out + 1]):
            cp.start()
        token = refs[first_out + 2 + n]
        token[...] = jnp.zeros_like(token)

    outs = pl.pallas_call(
        body, name=name,
        out_shape=(pltpu.SemaphoreType.DMA((nsem,)), pltpu.SemaphoreType.DMA((nsem,)),
                   *[pltpu.HBM(b.shape, b.dtype) for b in bufs], SDS((8, 128), F32)),
        in_specs=[HBM_SPEC] * n + [ANY] * len(deps),
        out_specs=(SEM_SPEC, SEM_SPEC, *[HBM_SPEC] * n, pl.BlockSpec(memory_space=pltpu.VMEM)),
        input_output_aliases={a: 2 + a for a in range(n)},
        compiler_params=pltpu.CompilerParams(has_side_effects=EFFECT),
    )(*[pltpu.with_memory_space_constraint(b, pltpu.HBM) for b in bufs], *deps)
    return outs[0], outs[1], list(outs[2:2 + n]), outs[2 + n]


def _exchange_wait(name, send_sems, recv_sems, bufs, plans, after):
    n = len(bufs)

    def body(*refs):
        for cp in _run_plans(plans, refs[:n], refs[n], refs[n + 1]):
            cp.wait_send()
            cp.wait_recv()

    return pl.pallas_call(
        body, name=name, out_shape=[pltpu.HBM(b.shape, b.dtype) for b in bufs],
        in_specs=[HBM_SPEC] * n + [SEM_SPEC, SEM_SPEC, ANY], out_specs=[HBM_SPEC] * n,
        input_output_aliases={a: a for a in range(n)},
        compiler_params=pltpu.CompilerParams(has_side_effects=EFFECT),
    )(*bufs, send_sems, recv_sems, after)


def _plan_gather(refs, send_sems, recv_sems, base):
    x, y, c, chips = _place()
    p = 2 * x + y
    return [_rcopy(r.at[0, p], r.at[0, p], send_sems, recv_sems, base + 3 * a + j, (*chip, c))
            for a, r in enumerate(refs) for j, chip in enumerate(chips)]


def _plan_pair_exchange(refs, send_sems, recv_sems, base):
    x, y, c, _ = _place()
    n = len(refs) // 2
    return [_rcopy(refs[a].at[:, _half(refs[a].shape[1], 1 - c)], refs[n + a], send_sems, recv_sems, base + a,
                   (x, y, 1 - c)) for a in range(n)]


def _plan_chip_exchange(refs, send_sems, recv_sems, base):
    x, y, c, chips = _place()
    n = len(refs) // 2
    return [_rcopy(refs[a].at[2 * chip[0] + chip[1]], refs[n + a].at[j], send_sems, recv_sems, base + 3 * a + j,
                   (*chip, c)) for a in range(n) for j, chip in enumerate(chips)]


def _plan_pair_share(refs, send_sems, recv_sems, base):
    x, y, c, _ = _place()
    return [_rcopy(r.at[_half(r.shape[0], c)], r.at[_half(r.shape[0], c)], send_sems, recv_sems, base + a,
                   (x, y, 1 - c)) for a, r in enumerate(refs)]


def _plan_small_gather(refs, send_sems, recv_sems, base):
    x, y, c, _ = _place()
    me = 4 * x + 2 * y + c
    cps = []
    for m in range(1, NDEV):
        peer = (1 - x if m & 4 else x, 1 - y if m & 2 else y, 1 - c if m & 1 else c)
        cps.append(_rcopy(refs[0], refs[1].at[me], send_sems, recv_sems, base + m - 1, peer))
    return cps


def _sum_small(buf, gathered):
    def body(buf_ref, g_ref, o_ref):
        x, y, c, _ = _place()
        me = 4 * x + 2 * y + c
        total = jnp.where(me == 0, buf_ref[...], g_ref[0])
        for dev in range(1, NDEV):
            total = total + jnp.where(me == dev, buf_ref[...], g_ref[dev])
        o_ref[...] = total

    vm = pl.BlockSpec(memory_space=pltpu.VMEM)
    return pl.pallas_call(body, name="sum_small", in_specs=[vm, vm], out_specs=vm, out_shape=SDS(buf.shape, F32),
                          compiler_params=pltpu.CompilerParams(vmem_limit_bytes=VMEM_LIMIT))(buf, gathered)


BLOCK_ELEMS = 512 * 1024
SUM_BLOCK_ELEMS = 1024 * 1024


def _rows_per_block(rows, cols, mult, limit=BLOCK_ELEMS):
    best = None
    for tr in range(mult, rows + 1, mult):
        if rows % tr == 0 and tr * cols <= limit:
            best = tr
    assert best is not None, (rows, cols)
    return best


def _pair_sum(g, r, c_idx):
    nq, rows, cols = g.shape
    half = rows // 2
    tr = _rows_per_block(half, cols, 16, SUM_BLOCK_ELEMS)
    nb = half // tr

    def body(c_ref, g_ref, r_ref, t_ref):
        t_ref[...] = (g_ref[...] + r_ref[...]).astype(BF16)

    blk = pl.BlockSpec((None, tr, cols), lambda q, i, cr: (q, i, 0))
    spec = pltpu.PrefetchScalarGridSpec(
        num_scalar_prefetch=1, grid=(nq, nb),
        in_specs=[pl.BlockSpec((None, tr, cols), lambda q, i, cr: (q, cr[0] * nb + i, 0)), blk], out_specs=blk)
    return pl.pallas_call(body, name="grad_pair_sum", grid_spec=spec, out_shape=SDS((nq, half, cols), BF16),
                          compiler_params=_cp("parallel", "parallel"))(c_idx, g, r)


def _chip_sum(g, r, rr, cp_idx):
    _, rows, cols = g.shape
    half = rows // 2
    tr = _rows_per_block(half, cols, 16, SUM_BLOCK_ELEMS)
    nb = half // tr

    def body(cp_ref, buf_ref, g_ref, r_ref, rr_ref, o_ref):
        o_ref[...] = ((g_ref[...] + r_ref[...]) + rr_ref[0].astype(F32) + rr_ref[1].astype(F32) + rr_ref[2].astype(F32))

    spec = pltpu.PrefetchScalarGridSpec(
        num_scalar_prefetch=1, grid=(nb,),
        in_specs=[ANY, pl.BlockSpec((None, tr, cols), lambda i, cp: (cp[1], cp[0] * nb + i, 0)),
                  pl.BlockSpec((None, tr, cols), lambda i, cp: (cp[1], i, 0)),
                  pl.BlockSpec((3, tr, cols), lambda i, cp: (0, i, 0))],
        out_specs=pl.BlockSpec((tr, cols), lambda i, cp: (cp[0] * nb + i, 0)))
    return pl.pallas_call(body, name="grad_chip_sum", grid_spec=spec, out_shape=SDS((rows, cols), F32),
                          input_output_aliases={1: 0}, compiler_params=_cp("parallel"),
                          )(cp_idx, lax.empty((rows, cols), F32), g, r, rr)


def _adamw_math(w, g, m, v):
    mn = ADAM_B1 * m + (1.0 - ADAM_B1) * g
    vn = ADAM_B2 * v + (1.0 - ADAM_B2) * (g * g)
    m_hat = mn / (1.0 - ADAM_B1 ** ADAM_STEP)
    v_hat = vn / (1.0 - ADAM_B2 ** ADAM_STEP)
    return -ADAM_LR * (m_hat / (jnp.sqrt(v_hat) + ADAM_EPS) + ADAM_WD * w), mn, vn


def _adamw_layer(w, g, m, v, l, outs, deps=()):
    _, rows, cols = w.shape
    tr = _rows_per_block(rows, cols, 8)
    deps = list(deps)

    def body(*refs):
        w_ref, g_ref, m_ref, v_ref = refs[4:8]
        go_ref, d_ref, mo_ref, vo_ref = refs[8 + len(deps):]
        gg = g_ref[...]
        go_ref[...] = gg
        d_ref[...], mo_ref[...], vo_ref[...] = _adamw_math(w_ref[...], gg, m_ref[...], v_ref[...])

    blk = pl.BlockSpec((None, tr, cols), lambda i: (l, i, 0))
    return pl.pallas_call(
        body, name="adamw_layer", grid=(rows // tr,),
        in_specs=[ANY] * 4 + [blk, pl.BlockSpec((tr, cols), lambda i: (i, 0)), blk, blk] + [ANY] * len(deps),
        out_specs=[blk] * 4, out_shape=[SDS(w.shape, F32)] * 4, input_output_aliases={k: k for k in range(4)},
        compiler_params=_cp("parallel"))(*outs, w, g, m, v, *deps)


def _adamw_small(ws, gs, ms, vs, deps=()):
    n = len(ws)
    deps = list(deps)

    def body(*refs):
        refs = refs[:4 * n] + refs[4 * n + len(deps):]
        w, g, m, v, d_out, m_out, v_out = (refs[k * n:(k + 1) * n] for k in range(7))
        for k in range(n):
            d_out[k][...], m_out[k][...], v_out[k][...] = _adamw_math(w[k][...], g[k][...], m[k][...], v[k][...])

    vm = pl.BlockSpec(memory_space=pltpu.VMEM)
    outs = pl.pallas_call(body, name="adamw_small", in_specs=[vm] * (4 * n) + [ANY] * len(deps), out_specs=[vm] * (3 * n),
                          out_shape=[SDS(w.shape, F32) for w in ws] * 3,
                          compiler_params=pltpu.CompilerParams(vmem_limit_bytes=VMEM_LIMIT))(*ws, *gs, *ms, *vs, *deps)
    return outs[:n], outs[n:2 * n], outs[2 * n:]


_WEIGHTS = ["ffn1_pre_g", "ffn1_w_gu", "ffn1_w_down", "ffn1_post_g", "mix_pre_g", "w_in", "lru_conv_w", "lru_conv_b",
            "lru_w_a", "lru_b_a", "lru_w_x", "lru_b_x", "lru_lambda", "attn_sinks", "conv_w", "conv_b", "conv_ln_g",
            "conv_ln_b", "group_g", "w_out", "mix_post_g", "ffn2_pre_g", "ffn2_w_gu", "ffn2_w_down", "ffn2_post_g"]
_INPUTS = ["x"] + _WEIGHTS + ["loss_target"] + ["m_" + n for n in _WEIGHTS] + ["v_" + n for n in _WEIGHTS]
_BIG = ["ffn1_w_gu", "ffn1_w_down", "w_in", "w_out", "ffn2_w_gu", "ffn2_w_down"]
_SMALL_SHARDED = ["lru_conv_w", "conv_w"]
_SMALL_REPL = [n for n in _WEIGHTS if n not in _BIG and n not in _SMALL_SHARDED]

PACK_TILE = 8 * 128


def _pack(arrs):
    parts = []
    for a in arrs:
        flat = a.reshape(-1)
        parts.append(jnp.pad(flat, (0, -flat.shape[0] % PACK_TILE)).reshape(-1, 128))
    return jnp.concatenate(parts, axis=0)


def _unpack(buf, shapes):
    out, row = [], 0
    for shp in shapes:
        size = math.prod(shp)
        nrow = -(-size // PACK_TILE) * 8
        out.append(buf[row:row + nrow].reshape(-1)[:size].reshape(shp))
        row += nrow
    return out


def _unshard_cols(a):
    return a.transpose(0, 2, 1, 3).reshape(1, a.shape[2], NSHARD * a.shape[3])


_GROUPS = dict(ffn1_gu=["ffn1_w_gu"], ffn1_down=["ffn1_w_down"], mix=["w_in", "w_out", "lru_conv_w", "conv_w"],
               ffn2=["ffn2_w_gu", "ffn2_w_down"])


def _full_weights(group, gathered):
    g = dict(zip(_GROUPS[group], gathered))
    if group == "mix":
        return dict(w_in=_unshard_cols(g["w_in"]), w_out=g["w_out"].reshape(1, D, D),
                    lru_conv_w=_unshard_cols(g["lru_conv_w"])[0], conv_w=_unshard_cols(g["conv_w"])[0])
    return {n: (a.reshape(1, DFF, D) if n.endswith("w_down") else a) for n, a in g.items()}


def _by_shard(name, buf):
    if name.endswith("w_gu"):
        return buf[0]
    if name == "w_in":
        return buf.reshape(D, NSHARD, P_IN // NSHARD).transpose(1, 0, 2)
    return buf.reshape(NSHARD, buf.shape[2] // NSHARD, buf.shape[3])


class _Reducer:
    PLANS = (_plan_pair_exchange, _plan_chip_exchange, _plan_pair_share)

    def __init__(self, keys, gs, c_idx, cp_idx):
        self.keys, self.gs, self.c_idx, self.cp_idx = keys, gs, c_idx, cp_idx
        self.n = len(gs)
        self.step = 0
        self.result = None

    def inputs(self):
        n = self.n
        if self.step == 0:
            bufs = self.gs + [lax.empty((NSHARD, g.shape[1] // 2, g.shape[2]), F32) for g in self.gs]
        elif self.step == 1:
            ts = [_pair_sum(g, r, self.c_idx) for g, r in zip(self.gs, self.rs)]
            bufs = ts + [lax.empty((3,) + t.shape[1:], BF16) for t in ts]
        else:
            bufs = [_chip_sum(g, r, rr, self.cp_idx) for g, r, rr in zip(self.gs, self.rs, self.rrs)]
        return bufs, (self.PLANS[self.step], len(bufs), (n, 3 * n, n)[self.step])

    def absorb(self, done):
        n = self.n
        if self.step == 0:
            self.gs, self.rs = done[:n], done[n:]
        elif self.step == 1:
            self.rrs = done[n:]
        else:
            self.result = dict(zip(self.keys, done))
        self.step += 1


class _SmallGather:
    def __init__(self, buf):
        self.buf, self.step, self.result, self.gathered = buf, 0, {}, None

    def inputs(self):
        return [self.buf, jnp.zeros((NDEV,) + self.buf.shape, F32)], (_plan_small_gather, 2, NDEV - 1)

    def absorb(self, done):
        self.buf, self.gathered = done
        self.step = 3


class _ReducePipeline:
    def __init__(self, c_idx, cp_idx):
        self.c_idx, self.cp_idx = c_idx, cp_idx
        self.reducers, self.flying, self.calls = [], None, 0

    def add(self, layer, done):
        if done:
            keys = [(layer, n) for n in done]
            self.reducers.append(_Reducer(keys, [_by_shard(n, b) for n, b in done.items()], self.c_idx, self.cp_idx))

    def _next(self):
        active = [r for r in self.reducers if r.step < 3]
        bufs, plans = [], []
        for r in active:
            b, triple = r.inputs()
            bufs += b
            plans.append(triple)
        self.calls += 1
        return active, bufs, plans, "grad_exchange%d" % self.calls

    def _absorb(self, active, plans, done):
        at = 0
        for r, (_, nb, _) in zip(active, plans):
            r.absorb(done[at:at + nb])
            at += nb

    def _land(self, after):
        if self.flying is not None:
            active, plans, name, send_sems, recv_sems, bufs = self.flying
            self._absorb(active, plans, _exchange_wait(name + "_wait", send_sems, recv_sems, bufs, plans, after))
            self.flying = None

    def hook(self, after):
        self._land(after)
        active, bufs, plans, name = self._next()
        if not active:
            return []
        send_sems, recv_sems, bufs, token = _exchange_start(name + "_start", bufs, plans)
        self.flying = (active, plans, name, send_sems, recv_sems, bufs)
        return [token]

    def available(self):
        out = {}
        for r in self.reducers:
            if r.step == 3:
                out.update(r.result)
        return out

    def finish(self, after):
        self._land(after)
        while True:
            active, bufs, plans, name = self._next()
            if not active:
                break
            self._absorb(active, plans, _exchange(name, bufs, plans))
        out = {}
        for r in self.reducers:
            out.update(r.result)
        return out


def kernel(*args):
    d = dict(zip(_INPUTS, args, strict=True))
    xi, yi, ci = lax.axis_index("x"), lax.axis_index("y"), lax.axis_index("c")
    p = 2 * xi + yi
    c_idx = jnp.reshape(ci, (1,)).astype(jnp.int32)
    p_idx = jnp.reshape(p, (1,)).astype(jnp.int32)
    cp_idx = jnp.stack([ci, p]).astype(jnp.int32)
    x, target = d["x"][0], d["loss_target"][0]
    tiles = _tiles(x.shape[0])

    groups = [(l, grp) for l in range(DEPTH) for grp in _GROUPS]
    placed = {(l, grp): [_place_shard(d[n], l, p_idx, BF16 if n in _BIG else F32) for n in _GROUPS[grp]]
              for l, grp in groups}
    ready = {groups[0]: _gather_two_level(placed[groups[0]], len(placed[groups[0]]))}
    flying, tokens = {}, [ready[groups[0]][0]]
    for l, grp in groups[1:]:
        plans = [(_plan_gather, len(placed[l, grp]), 3 * len(placed[l, grp]))]
        send_sems, recv_sems, bufs, token = _exchange_start("gather_l%d_%s_start" % (l, grp), placed[l, grp], plans,
                                                             tokens[-1:])
        flying[l, grp] = (send_sems, recv_sems, bufs, plans)
        tokens.append(token)

    def weights_of(l):
        def weights(grp, after):
            if (l, grp) not in ready:
                send_sems, recv_sems, bufs, plans = flying[l, grp]
                ready[l, grp] = _exchange_wait("gather_l%d_%s_wait" % (l, grp), send_sems, recv_sems, bufs, plans, after)
            return _full_weights(grp, ready[l, grp])
        return weights

    small = {n: d[n] for n in _SMALL_REPL}
    x1, sv0 = _forward_layer(x, weights_of(0), _layer_params(small, 0), tiles, tokens[1:])
    x2, sv1 = _forward_layer(x1, weights_of(1), _layer_params(small, 1), tiles)
    dx, lcols = _loss_grad(x2, target, tiles[0])

    pipe = _ReducePipeline(c_idx, cp_idx)
    sgrads = [None] * DEPTH
    for l, sv in ((1, sv1), (0, sv0)):
        bufs = _grad_buffers()

        def stage(done, dx, l=l):
            pipe.add(l, done)
            return pipe.hook(dx)

        dx, sgrads[l] = _backward_layer(dx, sv, bufs, tiles, stage)
    grad_x = dx

    stacked = {n: jnp.stack([sgrads[l][n].reshape(d[n].shape[1:]) for l in range(DEPTH)]) for n in _SMALL_REPL}
    for n in _SMALL_SHARDED:
        stacked[n] = jnp.stack([sgrads[l][n] for l in range(DEPTH)])
    loss_part = jnp.pad((0.5 / D) * jnp.sum(lcols).reshape(1), (0, 127))
    order = _SMALL_REPL + _SMALL_SHARDED
    small_gather = _SmallGather(_pack([loss_part] + [stacked[n] for n in order]))
    pipe.reducers.append(small_gather)

    results = {n: tuple(lax.empty(d[n].shape, F32) for _ in range(4)) for n in _BIG}
    applied = set()

    def apply_ready(deps, last):
        for (l, n), g in pipe.available().items():
            if (l, n) not in applied:
                results[n] = _adamw_layer(d[n], g, d["m_" + n], d["v_" + n], l, results[n], deps)
                applied.add((l, n))
                last = results[n][1]
                deps = [last]
        return last

    last = apply_ready(pipe.hook(grad_x), grad_x)
    token = pipe.hook(last)
    summed = _unpack(_sum_small(small_gather.buf, small_gather.gathered), [(128,)] + [stacked[n].shape for n in order])
    loss = summed[0][0]
    grads = {}
    for n, g in zip(order, summed[1:]):
        if n in _SMALL_SHARDED:
            g = lax.dynamic_slice_in_dim(g, p * (g.shape[2] // NSHARD), g.shape[2] // NSHARD, axis=2)
        grads[n] = g
    delta, new_m, new_v = {}, {}, {}
    small_out = _adamw_small([d[n] for n in order], [grads[n] for n in order], [d["m_" + n] for n in order],
                             [d["v_" + n] for n in order], token)
    for out, res in zip((delta, new_m, new_v), small_out):
        out.update(zip(order, res))
    last = apply_ready([small_out[0][0]], small_out[0][0])
    pipe.finish(last)
    apply_ready((), last)
    for n in _BIG:
        grads[n], delta[n], new_m[n], new_v[n] = results[n]

    return (loss, grad_x[None], *[grads[n] for n in _WEIGHTS], *[delta[n] for n in _WEIGHTS],
            *[new_m[n] for n in _WEIGHTS], *[new_v[n] for n in _WEIGHTS])
```

```python
import math

import jax
import jax.numpy as jnp
import numpy as np
from jax import lax
from jax.experimental import pallas as pl
from jax.experimental.pallas import tpu as pltpu

F32 = jnp.float32
BF16 = jnp.bfloat16
SDS = jax.ShapeDtypeStruct

D = 1024
DFF = 2816
FH = DFF // 2
DEPTH = 2
W_A = 256
W_B = 512
W_C = 256
NQ = 8
HD = 64
BLK = 128
ATT_NB_FWD = 1
ATT_NB_BWD = 4
P_IN = 1792
LRU_K = 4
CONV_K = 31
LRU_C = 8.0
NORM_EPS = 1e-6
LN_EPS = 1e-5
NEG_BIG = -1e30
SCALE = 1.0 / math.sqrt(HD)

ADAM_LR = 0.001
ADAM_B1 = 0.9
ADAM_B2 = 0.999
ADAM_EPS = 1e-08
ADAM_WD = 0.01
ADAM_STEP = 10

VMEM_LIMIT = 60 * 1024 * 1024
NSHARD = 4
NDEV = 8

TN = (((0,), (0,)), ((), ()))
NT = (((1,), (1,)), ((), ()))

MESH = pl.DeviceIdType.MESH
ANY = pl.BlockSpec(memory_space=pl.ANY)


def _cp(*sem):
    return pltpu.CompilerParams(dimension_semantics=sem if sem else None, vmem_limit_bytes=VMEM_LIMIT)


def _rsq(x, eps):
    return lax.rsqrt(jnp.mean(x * x, axis=-1, keepdims=True) + eps)


def _rms_bwd_rows(x, g, dy):
    r = _rsq(x, NORM_EPS)
    xh = x * r
    dyg = dy * g
    dx = r * (dyg - xh * jnp.mean(dyg * xh, axis=-1, keepdims=True))
    return dx, dy * xh


def _sig(x):
    return jax.nn.sigmoid(x)


def _ffn_up(x, pre_g, wgu, l, tm, deps=()):
    s = x.shape[0]
    deps = list(deps)

    def body(x_ref, g_ref, wg_ref, wu_ref, *rest):
        h_ref, go_ref, uo_ref, a_ref = rest[len(deps):]

        @pl.when(pl.program_id(1) == 0)
        def _():
            xf = x_ref[...]
            h_ref[...] = (xf * _rsq(xf, NORM_EPS) * g_ref[...]).astype(BF16)

        h = h_ref[...]
        gg = jnp.dot(h, wg_ref[...], preferred_element_type=F32)
        uu = jnp.dot(h, wu_ref[...], preferred_element_type=F32)
        sg = _sig(gg)
        silu = gg * sg
        go_ref[...] = (uu * (sg * (1.0 + gg * (1.0 - sg)))).astype(BF16)
        uo_ref[...] = silu.astype(BF16)
        a_ref[...] = (silu * uu).astype(BF16)

    wide = pl.BlockSpec((tm, FH), lambda i, j: (i, j))
    return pl.pallas_call(
        body, name="ffn_up", grid=(s // tm, 2),
        in_specs=[pl.BlockSpec((tm, D), lambda i, j: (i, 0)), pl.BlockSpec((1, D), lambda i, j: (0, 0)),
                  pl.BlockSpec((None, None, D, FH), lambda i, j: (l, j, 0, 0)),
                  pl.BlockSpec((None, None, D, FH), lambda i, j: (l, j + 2, 0, 0))] + [ANY] * len(deps),
        out_specs=[pl.BlockSpec((tm, D), lambda i, j: (i, 0)), wide, wide, wide],
        out_shape=[SDS((s, D), BF16), SDS((s, DFF), BF16), SDS((s, DFF), BF16), SDS((s, DFF), BF16)],
        compiler_params=_cp("parallel", "arbitrary"),
    )(x, pre_g, wgu, wgu, *deps)


def _mm_rms_res(a, w, l, x, g, c, tm, tk, name):
    s, k_dim = a.shape
    nk = k_dim // tk

    def body(a_ref, w_ref, x_ref, g_ref, z_ref, x1_ref):
        k = pl.program_id(1)
        p = jnp.dot(a_ref[...], w_ref[...], preferred_element_type=F32)

        @pl.when(k == 0)
        def _():
            z_ref[...] = p

        @pl.when(k > 0)
        def _():
            z_ref[...] += p

        @pl.when(k == nk - 1)
        def _():
            z = z_ref[...]
            x1_ref[...] = x_ref[...] + c * (z * _rsq(z, NORM_EPS) * g_ref[...])

    row = pl.BlockSpec((tm, D), lambda i, k: (i, 0))
    return pl.pallas_call(
        body, name=name, grid=(s // tm, nk),
        in_specs=[pl.BlockSpec((tm, tk), lambda i, k: (i, k)), pl.BlockSpec((None, tk, D), lambda i, k: (l, k, 0)),
                  row, pl.BlockSpec((1, D), lambda i, k: (0, 0))],
        out_specs=[row, row],
        out_shape=[SDS((s, D), F32), SDS((s, D), F32)],
        compiler_params=_cp("parallel", "arbitrary"),
    )(a, w, x, g)


def _rms_bwd(dy, z, g, c, tm, name, deps=()):
    s = z.shape[0]
    deps = list(deps)

    def body(dy_ref, z_ref, g_ref, *rest):
        dz_ref, dg_ref = rest[len(deps):]
        dz, dgr = _rms_bwd_rows(z_ref[...], g_ref[...], c * dy_ref[...])
        dz_ref[...] = dz.astype(BF16)
        part = jnp.sum(dgr, axis=0, keepdims=True)

        @pl.when(pl.program_id(0) == 0)
        def _():
            dg_ref[...] = part

        @pl.when(pl.program_id(0) > 0)
        def _():
            dg_ref[...] += part

    row = pl.BlockSpec((tm, D), lambda i: (i, 0))
    vec = pl.BlockSpec((1, D), lambda i: (0, 0))
    return pl.pallas_call(
        body, name=name, grid=(s // tm,), in_specs=[row, row, vec] + [ANY] * len(deps), out_specs=[row, vec],
        out_shape=[SDS((s, D), BF16), SDS((1, D), F32)], compiler_params=_cp("arbitrary"),
    )(dy, z, g, *deps)


def _ffn_bwd_mid(dz, wd, l, dadg, dadu, tm):
    s = dz.shape[0]

    def body(dz_ref, wd_ref, g_ref, u_ref, dgu_ref):
        da = lax.dot_general(dz_ref[...], wd_ref[...], NT, preferred_element_type=F32)
        dgu_ref[:, 0:FH] = (da * g_ref[...].astype(F32)).astype(BF16)
        dgu_ref[:, FH:2 * FH] = (da * u_ref[...].astype(F32)).astype(BF16)

    wide = pl.BlockSpec((tm, FH), lambda i, j: (i, j))
    return pl.pallas_call(
        body, name="ffn_bwd_mid", grid=(s // tm, 2),
        in_specs=[pl.BlockSpec((tm, D), lambda i, j: (i, 0)), pl.BlockSpec((None, FH, D), lambda i, j: (l, j, 0)), wide, wide],
        out_specs=pl.BlockSpec((tm, 2 * FH), lambda i, j: (i, j)),
        out_shape=SDS((s, 2 * DFF), BF16),
        compiler_params=_cp("parallel", "arbitrary"),
    )(dz, wd, dadg, dadu)


def _ffn_bwd_dh(dgu, wgu, l, x, pre_g, dx1, tm, deps=()):
    s = x.shape[0]
    deps = list(deps)

    def body(dgu_ref, w_hbm, x_ref, g_ref, dx1_ref, *rest):
        dx_ref, dgp_ref, wcat_ref, sems = rest[len(deps):]
        i = pl.program_id(0)

        @pl.when(i == 0)
        def _():
            cps = [pltpu.make_async_copy(w_hbm.at[l, q], wcat_ref.at[:, pl.ds((2 * (q % 2) + q // 2) * FH, FH)], sems.at[q])
                   for q in range(NSHARD)]
            for cp in cps:
                cp.start()
            for cp in cps:
                cp.wait()

        dh = lax.dot_general(dgu_ref[...], wcat_ref[...], NT, preferred_element_type=F32)
        dx, dgr = _rms_bwd_rows(x_ref[...], g_ref[...], dh)
        dx_ref[...] = dx1_ref[...] + dx
        _acc(dgp_ref, i == 0, jnp.sum(dgr, axis=0, keepdims=True))

    row = pl.BlockSpec((tm, D), lambda i: (i, 0))
    vec = pl.BlockSpec((1, D), lambda i: (0, 0))
    return pl.pallas_call(
        body, name="ffn_bwd_dh", grid=(s // tm,),
        in_specs=[pl.BlockSpec((tm, 2 * DFF), lambda i: (i, 0)), ANY, row, vec, row] + [ANY] * len(deps),
        out_specs=[row, vec],
        out_shape=[SDS((s, D), F32), SDS((1, D), F32)],
        scratch_shapes=[pltpu.VMEM((D, 2 * DFF), BF16), pltpu.SemaphoreType.DMA((NSHARD,))],
        compiler_params=_cp("arbitrary"),
    )(dgu, wgu, x, pre_g, dx1, *deps)


def _mm_tn_into(buf, a, b, l, joff, tka, tn, ts, name, bstride=1, boff=0):
    s, ka = a.shape
    n = b.shape[1] // bstride

    def body(buf_ref, a_ref, b_ref, o_ref):
        p = lax.dot_general(a_ref[...], b_ref[...], TN, preferred_element_type=F32)

        @pl.when(pl.program_id(2) == 0)
        def _():
            o_ref[...] = p

        @pl.when(pl.program_id(2) > 0)
        def _():
            o_ref[...] += p

    return pl.pallas_call(
        body, name=name, grid=(ka // tka, n // tn, s // ts),
        in_specs=[pl.BlockSpec(memory_space=pl.ANY),
                  pl.BlockSpec((ts, tka), lambda ia, j, t: (t, ia)),
                  pl.BlockSpec((ts, tn), lambda ia, j, t: (t, bstride * j + boff))],
        out_specs=pl.BlockSpec((None, None, tka, tn), lambda ia, j, t: (l, joff + j, ia, 0)),
        out_shape=SDS(buf.shape, F32), input_output_aliases={0: 0},
        compiler_params=_cp("parallel", "parallel", "arbitrary"),
    )(buf, a, b)


def _proj(x, g, w_in, l, tm):
    s = x.shape[0]

    def body(x_ref, g_ref, w_ref, h_ref, p_ref):
        xf = x_ref[...]
        h = (xf * _rsq(xf, NORM_EPS) * g_ref[...]).astype(BF16)
        h_ref[...] = h
        p_ref[...] = jnp.dot(h, w_ref[...], preferred_element_type=F32)

    return pl.pallas_call(
        body, name="proj", grid=(s // tm,),
        in_specs=[pl.BlockSpec((tm, D), lambda i: (i, 0)), pl.BlockSpec((1, D), lambda i: (0, 0)),
                  pl.BlockSpec((None, D, P_IN), lambda i: (l, 0, 0))],
        out_specs=[pl.BlockSpec((tm, D), lambda i: (i, 0)), pl.BlockSpec((tm, P_IN), lambda i: (i, 0))],
        out_shape=[SDS((s, D), BF16), SDS((s, P_IN), F32)],
        compiler_params=_cp("parallel"),
    )(x, g, w_in)


def _mm_nt(a, w, l, tm, name):
    s, k_dim = a.shape
    n = w.shape[1]

    def body(a_ref, w_ref, o_ref):
        o_ref[...] = lax.dot_general(a_ref[...], w_ref[...], NT, preferred_element_type=F32)

    return pl.pallas_call(
        body, name=name, grid=(s // tm,),
        in_specs=[pl.BlockSpec((tm, k_dim), lambda i: (i, 0)), pl.BlockSpec((None, n, k_dim), lambda i: (l, 0, 0))],
        out_specs=pl.BlockSpec((tm, n), lambda i: (i, 0)),
        out_shape=SDS((s, n), F32), compiler_params=_cp("parallel"),
    )(a, w)


def _mm_nt_rmsbwd(dp, w_in, l, x, g, dx1, tm):
    s = x.shape[0]

    def body(dp_ref, w_ref, x_ref, g_ref, dx1_ref, dx_ref, dg_ref):
        dh = lax.dot_general(dp_ref[...], w_ref[...], NT, preferred_element_type=F32)
        dx, dgr = _rms_bwd_rows(x_ref[...], g_ref[...], dh)
        dx_ref[...] = dx1_ref[...] + dx
        part = jnp.sum(dgr, axis=0, keepdims=True)

        @pl.when(pl.program_id(0) == 0)
        def _():
            dg_ref[...] = part

        @pl.when(pl.program_id(0) > 0)
        def _():
            dg_ref[...] += part

    row = pl.BlockSpec((tm, D), lambda i: (i, 0))
    vec = pl.BlockSpec((1, D), lambda i: (0, 0))
    return pl.pallas_call(
        body, name="mix_bwd_dx", grid=(s // tm,),
        in_specs=[pl.BlockSpec((tm, P_IN), lambda i: (i, 0)), pl.BlockSpec((None, D, P_IN), lambda i: (l, 0, 0)), row, vec, row],
        out_specs=[row, vec], out_shape=[SDS((s, D), F32), SDS((1, D), F32)],
        compiler_params=_cp("arbitrary"),
    )(dp, w_in, x, g, dx1)


def _row_iota(shape):
    return lax.broadcasted_iota(jnp.int32, shape, 0)


def _lru_gates(xc, wa_ref, ba_ref, wx_ref, bx_ref, lam_ref):
    xb = xc.astype(BF16)
    r = _sig(jnp.dot(xb, wa_ref[...], preferred_element_type=F32) + ba_ref[...])
    ig = _sig(jnp.dot(xb, wx_ref[...], preferred_element_type=F32) + bx_ref[...])
    nl = -lam_ref[...]
    sp = jnp.maximum(nl, 0.0) + jnp.log(1.0 + jnp.exp(-jnp.abs(nl)))
    log_a = -LRU_C * r * sp
    a = jnp.exp(log_a)
    x2 = 2.0 * log_a
    series = x2 * (1.0 + x2 * (0.5 + x2 * (1.0 / 6.0 + x2 * (1.0 / 24.0 + x2 * (1.0 / 120.0)))))
    em1 = jnp.where(x2 > -0.05, series, jnp.exp(x2) - 1.0)
    mlt = jnp.sqrt(-em1)
    return r, ig, a, mlt, sp


def _conv_taps(src_ref, w_ref, k_taps, pad, tc):
    acc = None
    for j in range(k_taps):
        term = w_ref[j:j + 1, :] * src_ref[pl.ds(pad - (k_taps - 1) + j, tc), :]
        acc = term if acc is None else acc + term
    return acc


def _fill_shifted(src_ref, sh_ref):
    n = src_ref.shape[0] - 8
    for s in range(1, 8):
        sh_ref[s, 0:n, :] = src_ref[pl.ds(s, n), :]


def _shifted_rows(src_ref, sh_ref, offset, tc):
    if offset % 8 == 0:
        return src_ref[pl.ds(offset, tc), :]
    return sh_ref[offset % 8, pl.ds(offset - offset % 8, tc), :]


def _gelu_parts(x):
    c0 = math.sqrt(2.0 / math.pi)
    inner = c0 * (x + 0.044715 * x * x * x)
    t = jnp.tanh(inner)
    gl = 0.5 * x * (1.0 + t)
    dgl = 0.5 * (1.0 + t) + 0.5 * x * (1.0 - t * t) * c0 * (1.0 + 3.0 * 0.044715 * x * x)
    return gl, dgl


def _lru_fwd(proj, cw, cb, wa, ba, wx, bx, lam, gg, tc):
    s = proj.shape[0]
    pad = 8

    def body(xcur_ref, xprev_ref, gate_ref, cw_ref, cb_ref, wa_ref, ba_ref, wx_ref, bx_ref, lam_ref, gg_ref,
             yn_ref, h_ref, xs_ref, hc_ref):
        i = pl.program_id(0)

        @pl.when(i == 0)
        def _():
            hc_ref[...] = jnp.zeros_like(hc_ref)

        xs_ref[0:pad, :] = jnp.where(i > 0, xprev_ref[tc - pad:tc, :], 0.0)
        xs_ref[pad:pad + tc, :] = xcur_ref[...]
        xc = _conv_taps(xs_ref, cw_ref, LRU_K, pad, tc) + cb_ref[...]
        _, ig, a, mlt, _ = _lru_gates(xc, wa_ref, ba_ref, wx_ref, bx_ref, lam_ref)
        u = mlt * (ig * xc)
        row = _row_iota((tc, W_A))
        d = 1
        while d < tc:
            ok = row >= d
            a_sh = jnp.where(ok, pltpu.roll(a, d, axis=0), 1.0)
            u_sh = jnp.where(ok, pltpu.roll(u, d, axis=0), 0.0)
            u = a * u_sh + u
            a = a * a_sh
            d *= 2
        h = u + a * hc_ref[...]
        hc_ref[...] = jnp.sum(jnp.where(row == tc - 1, h, 0.0), axis=0, keepdims=True)
        h_ref[...] = h
        gl, _ = _gelu_parts(gate_ref[...])
        ya = gl * h
        yn_ref[...] = (ya * _rsq(ya, NORM_EPS) * gg_ref[...]).astype(BF16)

    blk = lambda c: pl.BlockSpec((tc, W_A), lambda i, c=c: (i, c))
    full = lambda a: pl.BlockSpec(a.shape, lambda i: (0,) * a.ndim)
    params = [cw, cb, wa, ba, wx, bx, lam, gg]
    return pl.pallas_call(
        body, name="lru_fwd", grid=(s // tc,),
        in_specs=[blk(0), pl.BlockSpec((tc, W_A), lambda i: (jnp.maximum(i - 1, 0), 0)), blk(1)] + [full(a) for a in params],
        out_specs=[pl.BlockSpec((tc, W_A), lambda i: (i, 0))] * 2,
        out_shape=[SDS((s, W_A), BF16), SDS((s, W_A), F32)],
        scratch_shapes=[pltpu.VMEM((tc + pad, W_A), F32), pltpu.VMEM((1, W_A), F32)],
        compiler_params=_cp("arbitrary"),
    )(proj, proj, proj, *params)


def _acc(ref, first, val):
    @pl.when(first)
    def _():
        ref[...] = val

    @pl.when(jnp.logical_not(first))
    def _():
        ref[...] += val


def _lru_bwd(dy, proj, h, cw, cb, wa, ba, wx, bx, lam, gg, tc):
    s = proj.shape[0]
    nc = s // tc
    pad = 8

    def body(dy_ref, xcur_ref, xprev_ref, gate_ref, h_ref, hprev_ref, cw_ref, cb_ref, wa_ref, ba_ref, wx_ref, bx_ref,
             lam_ref, gg_ref,
             dp_ref, dcw_ref, dcb_ref, dwa_ref, dba_ref, dwx_ref, dbx_ref, dlam_ref, dgg_ref,
             xs_ref, ds_ref, mu_ref, nx_ref):
        step = pl.program_id(0)
        i = nc - 1 - step
        first = step == 0

        @pl.when(first)
        def _():
            mu_ref[...] = jnp.zeros_like(mu_ref)
            nx_ref[...] = jnp.zeros_like(nx_ref)

        xs_ref[0:pad, :] = jnp.where(i > 0, xprev_ref[tc - pad:tc, :], 0.0)
        xs_ref[pad:pad + tc, :] = xcur_ref[...]
        xc = _conv_taps(xs_ref, cw_ref, LRU_K, pad, tc) + cb_ref[...]
        r, ig, a, mlt, sp = _lru_gates(xc, wa_ref, ba_ref, wx_ref, bx_ref, lam_ref)
        hh = h_ref[...]
        gate = gate_ref[...]
        gl, dgl = _gelu_parts(gate)
        ya = gl * hh
        dya, dggr = _rms_bwd_rows(ya, gg_ref[...], dy_ref[...])
        _acc(dgg_ref, first, jnp.sum(dggr, axis=0, keepdims=True))
        dp_ref[:, W_A:2 * W_A] = dya * hh * dgl
        dh = dya * gl

        row = _row_iota((tc, W_A))
        aa = a
        uu = a * dh
        d = 1
        while d < tc:
            ok = row < tc - d
            a_sh = jnp.where(ok, pltpu.roll(aa, tc - d, axis=0), 1.0)
            u_sh = jnp.where(ok, pltpu.roll(uu, tc - d, axis=0), 0.0)
            uu = uu + aa * u_sh
            aa = aa * a_sh
            d *= 2
        cin = mu_ref[...]
        mu = uu + aa * cin
        lam_t = dh + jnp.where(row == tc - 1, cin, pltpu.roll(mu, tc - 1, axis=0))
        mu_ref[...] = jnp.sum(jnp.where(row == 0, mu, 0.0), axis=0, keepdims=True)
        hm1 = jnp.where(row == 0, jnp.where(i > 0, pltpu.roll(hprev_ref[...], 1, axis=0), 0.0),
                        pltpu.roll(hh, 1, axis=0))
        da = lam_t * hm1
        du = lam_t
        dmlt = du * ig * xc
        dig = du * mlt * xc
        dxc = du * mlt * ig
        dlog_a = da * a - dmlt * (a * a / mlt)
        dr = dlog_a * (-LRU_C * sp)
        dsp = jnp.sum(dlog_a * (-LRU_C * r), axis=0, keepdims=True)
        _acc(dlam_ref, first, dsp * (-_sig(-lam_ref[...])))
        dga = dr * r * (1.0 - r)
        dgx = dig * ig * (1.0 - ig)
        _acc(dba_ref, first, jnp.sum(dga, axis=0, keepdims=True))
        _acc(dbx_ref, first, jnp.sum(dgx, axis=0, keepdims=True))
        xb = xc.astype(BF16)
        dgab = dga.astype(BF16)
        dgxb = dgx.astype(BF16)
        _acc(dwa_ref, first, lax.dot_general(xb, dgab, TN, preferred_element_type=F32))
        _acc(dwx_ref, first, lax.dot_general(xb, dgxb, TN, preferred_element_type=F32))
        dxc = (dxc + lax.dot_general(dgab, wa_ref[...], NT, preferred_element_type=F32)
               + lax.dot_general(dgxb, wx_ref[...], NT, preferred_element_type=F32))

        _acc(dcb_ref, first, jnp.sum(dxc, axis=0, keepdims=True))
        r8 = _row_iota((8, W_A))
        dcw = jnp.zeros((8, W_A), F32)
        for j in range(LRU_K):
            tap = jnp.sum(dxc * xs_ref[pl.ds(pad - (LRU_K - 1) + j, tc), :], axis=0, keepdims=True)
            dcw = dcw + jnp.where(r8 == j, tap, 0.0)
        _acc(dcw_ref, first, dcw)
        ds_ref[0:tc, :] = dxc
        ds_ref[tc:tc + pad, :] = nx_ref[...]
        dlx = None
        for j in range(LRU_K):
            term = cw_ref[j:j + 1, :] * ds_ref[pl.ds(LRU_K - 1 - j, tc), :]
            dlx = term if dlx is None else dlx + term
        dp_ref[:, 0:W_A] = dlx
        nx_ref[...] = dxc[0:pad, :]

    rev = lambda c: pl.BlockSpec((tc, W_A), lambda t, c=c: (nc - 1 - t, c))
    prev = lambda c: pl.BlockSpec((tc, W_A), lambda t, c=c: (jnp.maximum(nc - 2 - t, 0), c))
    full = lambda a: pl.BlockSpec(a.shape, lambda t: (0,) * a.ndim)
    params = [cw, cb, wa, ba, wx, bx, lam, gg]
    vec = SDS((1, W_A), F32)
    sq = SDS((W_A, W_A), F32)
    outs = [SDS((s, 2 * W_A), F32), SDS((8, W_A), F32), vec, sq, vec, sq, vec, vec, vec]
    return pl.pallas_call(
        body, name="lru_bwd", grid=(nc,),
        in_specs=[rev(0), rev(0), prev(0), rev(1), rev(0), prev(0)] + [full(a) for a in params],
        out_specs=[pl.BlockSpec((tc, 2 * W_A), lambda t: (nc - 1 - t, 0))]
        + [pl.BlockSpec(o.shape, lambda t: (0, 0)) for o in outs[1:]],
        out_shape=outs,
        scratch_shapes=[pltpu.VMEM((tc + pad, W_A), F32), pltpu.VMEM((tc + pad, W_A), F32),
                        pltpu.VMEM((1, W_A), F32), pltpu.VMEM((pad, W_A), F32)],
        compiler_params=_cp("arbitrary"),
    )(dy, proj, proj, proj, h, h, *params)


def _attn_stack(qa, qb, kvh):
    lane = lax.broadcasted_iota(jnp.int32, qa.shape, 1)
    keep = (lane >= HD) if kvh == 1 else (lane < HD)
    parts = []
    for tile in (qa, qb):
        for half in (0, 1):
            y = tile if half == kvh else pltpu.roll(tile, HD, axis=1)
            parts.append(jnp.where(keep, y, 0.0))
    return jnp.concatenate(parts, axis=0)


def _attn_unstack(o, kvh):
    lane = lax.broadcasted_iota(jnp.int32, (BLK, 2 * HD), 1)
    tiles = []
    for t in range(2):
        halves = []
        for half in (0, 1):
            blk = o[(2 * t + half) * BLK:(2 * t + half + 1) * BLK, :]
            halves.append(blk if half == kvh else pltpu.roll(blk, HD, axis=1))
        tiles.append(jnp.where(lane < HD, halves[0], halves[1]))
    return tiles


def _attn_stack_all(x_ref_or_val):
    return jnp.concatenate([_attn_stack(x_ref_or_val[:, 256 * kvh:256 * kvh + 128],
                                        x_ref_or_val[:, 256 * kvh + 128:256 * kvh + 256], kvh) for kvh in range(2)], axis=0)


def _attn_unstack_all(o, dst_ref):
    for kvh in range(2):
        ta, tb = _attn_unstack(o[4 * BLK * kvh:4 * BLK * (kvh + 1), :], kvh)
        dst_ref[:, 256 * kvh:256 * kvh + 128] = ta
        dst_ref[:, 256 * kvh + 128:256 * kvh + 256] = tb


def _attn_windows(cur_ref, prev_ref, nb):
    blocks = [prev_ref[...]] + [cur_ref[b * BLK:(b + 1) * BLK, :] for b in range(nb)]
    return [jnp.concatenate(blocks[b:b + 2], axis=0).astype(BF16) for b in range(nb)]


def _attn_bias():
    qi = np.arange(NQ * BLK)[:, None] % BLK
    kj = np.arange(2 * BLK)[None, :]
    rel = BLK + qi - kj
    ok = (rel >= 0) & (rel < BLK)
    return jnp.asarray(np.stack([np.where(ok & (kj >= BLK), 0.0, NEG_BIG), np.where(ok, 0.0, NEG_BIG)]), F32)


def _attn_probs(qs, kw, first, sink_ref, bias_ref):
    rows = NQ * BLK
    bias = bias_ref[1] if first is False else jnp.where(first, bias_ref[0], bias_ref[1])
    sh = lax.dot_general(qs.astype(BF16), kw, NT, preferred_element_type=F32) * SCALE + bias
    head = lax.broadcasted_iota(jnp.int32, (rows, 1), 0) // BLK
    sk = jnp.zeros((rows, 1), F32)
    for h in range(NQ):
        sk = jnp.where(head == h, sink_ref[h:h + 1, 0:1], sk)
    m = jnp.maximum(jnp.max(sh, axis=-1, keepdims=True), sk)
    e = jnp.exp(sh - m)
    es = jnp.exp(sk - m)
    rz = 1.0 / (jnp.sum(e, axis=-1, keepdims=True) + es)
    return e * rz, es * rz


def _attn_fwd(proj, sinks8, gg):
    s = proj.shape[0]
    nb = ATT_NB_FWD

    def body(q_ref, kc_ref, kp_ref, vc_ref, vp_ref, sink_ref, gg_ref, bias_ref, yn_ref, ob_ref):
        kws, vws = _attn_windows(kc_ref, kp_ref, nb), _attn_windows(vc_ref, vp_ref, nb)
        for b in range(nb):
            rows = pl.ds(b * BLK, BLK)
            first = (pl.program_id(0) == 0) if b == 0 else False
            p, _ = _attn_probs(_attn_stack_all(q_ref.at[rows, :]), kws[b], first, sink_ref, bias_ref)
            _attn_unstack_all(jnp.dot(p.astype(BF16), vws[b], preferred_element_type=F32), ob_ref.at[rows, :])
        ob = ob_ref[...]
        yn_ref[...] = (ob * _rsq(ob, NORM_EPS) * gg_ref[...]).astype(BF16)

    tb = nb * BLK
    cur = lambda c: pl.BlockSpec((tb, 128), lambda m, c=c: (m, c))
    prev = lambda c: pl.BlockSpec((BLK, 128), lambda m, c=c: (jnp.maximum(nb * m - 1, 0), c))
    out = pl.BlockSpec((tb, W_B), lambda m: (m, 0))
    return pl.pallas_call(
        body, name="attn_fwd", grid=(s // tb,),
        in_specs=[pl.BlockSpec((tb, W_B), lambda m: (m, 1)), cur(8), prev(8), cur(9), prev(9),
                  pl.BlockSpec((8, 128), lambda n: (0, 0)), pl.BlockSpec((1, W_B), lambda n: (0, 0)),
                  pl.BlockSpec((2, NQ * BLK, 2 * BLK), lambda n: (0, 0, 0))],
        out_specs=[out, out], out_shape=[SDS((s, W_B), BF16), SDS((s, W_B), F32)],
        compiler_params=_cp("parallel"),
    )(proj, proj, proj, proj, proj, sinks8, gg, _attn_bias())


def _attn_bwd(dy, proj, ob, sinks8, gg):
    s = proj.shape[0]
    nb = ATT_NB_BWD

    def body(dya_ref, dyb_ref, q_ref, kc_ref, kp_ref, vc_ref, vp_ref, ob_ref, sink_ref, gg_ref, bias_ref,
             dq_ref, dcur_ref, dprev_ref, dsink_ref, dgg_ref):
        first = pl.program_id(0) == 0
        kws, vws = _attn_windows(kc_ref, kp_ref, nb), _attn_windows(vc_ref, vp_ref, nb)
        dyn = jnp.concatenate([dya_ref[...], dyb_ref[...]], axis=1)
        dob, dggr = _rms_bwd_rows(ob_ref[...], gg_ref[...], dyn)
        _acc(dgg_ref, first, jnp.sum(dggr, axis=0, keepdims=True))
        r8 = _row_iota((8, 128))
        dsk = jnp.zeros((8, 128), F32)
        for b in range(nb):
            rows = pl.ds(b * BLK, BLK)
            qs = _attn_stack_all(q_ref.at[rows, :])
            p, psink = _attn_probs(qs, kws[b], first if b == 0 else False, sink_ref, bias_ref)
            dosb = _attn_stack_all(dob[b * BLK:(b + 1) * BLK, :]).astype(BF16)
            dp = lax.dot_general(dosb, vws[b], NT, preferred_element_type=F32)
            dd = jnp.sum(p * dp, axis=-1, keepdims=True)
            dsb = (p * (dp - dd) * SCALE).astype(BF16)
            dsink_rows = -psink * dd
            for h in range(NQ):
                dsk = dsk + jnp.where(r8 == h, jnp.sum(dsink_rows[h * BLK:(h + 1) * BLK, :], axis=0, keepdims=True), 0.0)
            _attn_unstack_all(jnp.dot(dsb, kws[b], preferred_element_type=F32), dq_ref.at[rows, :])
            dkw = lax.dot_general(dsb, qs.astype(BF16), TN, preferred_element_type=F32)
            dvw = lax.dot_general(p.astype(BF16), dosb, TN, preferred_element_type=F32)
            dprev_ref[rows, 0:128] = dkw[0:BLK, :]
            dprev_ref[rows, 128:256] = dvw[0:BLK, :]
            dcur_ref[rows, 0:128] = dkw[BLK:2 * BLK, :]
            dcur_ref[rows, 128:256] = dvw[BLK:2 * BLK, :]
        _acc(dsink_ref, first, dsk)

    tb = nb * BLK
    cur = lambda c: pl.BlockSpec((tb, 128), lambda m, c=c: (m, c))
    prev = lambda c: pl.BlockSpec((BLK, 128), lambda m, c=c: (jnp.maximum(nb * m - 1, 0), c))
    wide = pl.BlockSpec((tb, W_B), lambda m: (m, 0))
    half = pl.BlockSpec((tb, 256), lambda m: (m, 0))
    return pl.pallas_call(
        body, name="attn_bwd", grid=(s // tb,),
        in_specs=[pl.BlockSpec((tb, 256), lambda m: (m, 1)), pl.BlockSpec((tb, 256), lambda m: (m, 2)),
                  pl.BlockSpec((tb, W_B), lambda m: (m, 1)), cur(8), prev(8), cur(9), prev(9), wide,
                  pl.BlockSpec((8, 128), lambda n: (0, 0)), pl.BlockSpec((1, W_B), lambda n: (0, 0)),
                  pl.BlockSpec((2, NQ * BLK, 2 * BLK), lambda n: (0, 0, 0))],
        out_specs=[wide, half, half, pl.BlockSpec((8, 128), lambda n: (0, 0)), pl.BlockSpec((1, W_B), lambda n: (0, 0))],
        out_shape=[SDS((s, W_B), F32), SDS((s, 256), F32), SDS((s, 256), F32), SDS((8, 128), F32), SDS((1, W_B), F32)],
        compiler_params=_cp("arbitrary"),
    )(dy, dy, proj, proj, proj, proj, proj, ob, sinks8, gg, _attn_bias())


def _ln_parts(y1, eps=LN_EPS):
    mu = jnp.mean(y1, axis=-1, keepdims=True)
    xc = y1 - mu
    rstd = lax.rsqrt(jnp.mean(xc * xc, axis=-1, keepdims=True) + eps)
    return xc * rstd, rstd


def _conf_fwd(proj, cw, cb, lg, lb, gg, tc):
    s = proj.shape[0]
    pad = 32

    def body(ac_ref, gc_ref, ap_ref, gp_ref, cw_ref, cb_ref, lg_ref, lb_ref, gg_ref, yn_ref, y1_ref, ys_ref, sh_ref):
        i = pl.program_id(0)
        tail = ap_ref[tc - pad:tc, :] * _sig(gp_ref[tc - pad:tc, :])
        ys_ref[0:pad, :] = jnp.where(i > 0, tail, 0.0)
        ys_ref[pad:pad + tc, :] = ac_ref[...] * _sig(gc_ref[...])
        _fill_shifted(ys_ref, sh_ref)
        y1 = cb_ref[...]
        for j in range(CONV_K):
            y1 = y1 + cw_ref[j:j + 1, :] * _shifted_rows(ys_ref, sh_ref, pad - (CONV_K - 1) + j, tc)
        y1_ref[...] = y1
        xh, _ = _ln_parts(y1)
        yl = xh * lg_ref[...] + lb_ref[...]
        yc = yl * _sig(yl)
        yn_ref[...] = (yc * _rsq(yc, NORM_EPS) * gg_ref[...]).astype(BF16)

    cur = lambda c: pl.BlockSpec((tc, W_C), lambda i, c=c: (i, c))
    prev = lambda c: pl.BlockSpec((tc, W_C), lambda i, c=c: (jnp.maximum(i - 1, 0), c))
    full = lambda a: pl.BlockSpec(a.shape, lambda i: (0,) * a.ndim)
    params = [cw, cb, lg, lb, gg]
    out = pl.BlockSpec((tc, W_C), lambda i: (i, 0))
    return pl.pallas_call(
        body, name="conf_fwd", grid=(s // tc,),
        in_specs=[cur(5), cur(6), prev(5), prev(6)] + [full(a) for a in params],
        out_specs=[out, out], out_shape=[SDS((s, W_C), BF16), SDS((s, W_C), F32)],
        scratch_shapes=[pltpu.VMEM((tc + pad, W_C), F32), pltpu.VMEM((8, tc + pad, W_C), F32)],
        compiler_params=_cp("parallel"),
    )(proj, proj, proj, proj, *params)


def _conf_bwd(dy, proj, y1, cw, cb, lg, lb, gg, tc):
    s = proj.shape[0]
    nc = s // tc
    pad = 32

    def body(dy_ref, ac_ref, gc_ref, ap_ref, gp_ref, y1_ref, cw_ref, cb_ref, lg_ref, lb_ref, gg_ref,
             dp_ref, dcw_ref, dcb_ref, dlg_ref, dlb_ref, dgg_ref, ys_ref, ds_ref, nx_ref, ysh_ref, dsh_ref):
        step = pl.program_id(0)
        i = nc - 1 - step
        first = step == 0

        @pl.when(first)
        def _():
            nx_ref[...] = jnp.zeros_like(nx_ref)

        a = ac_ref[...]
        sg = _sig(gc_ref[...])
        tail = ap_ref[tc - pad:tc, :] * _sig(gp_ref[tc - pad:tc, :])
        ys_ref[0:pad, :] = jnp.where(i > 0, tail, 0.0)
        ys_ref[pad:pad + tc, :] = a * sg
        xh, rstd = _ln_parts(y1_ref[...])
        yl = xh * lg_ref[...] + lb_ref[...]
        sl = _sig(yl)
        yc = yl * sl
        dyc, dggr = _rms_bwd_rows(yc, gg_ref[...], dy_ref[...])
        _acc(dgg_ref, first, jnp.sum(dggr, axis=0, keepdims=True))
        dyl = dyc * sl * (1.0 + yl * (1.0 - sl))
        _acc(dlg_ref, first, jnp.sum(dyl * xh, axis=0, keepdims=True))
        _acc(dlb_ref, first, jnp.sum(dyl, axis=0, keepdims=True))
        dxh = dyl * lg_ref[...]
        dy1 = rstd * (dxh - jnp.mean(dxh, axis=-1, keepdims=True) - xh * jnp.mean(dxh * xh, axis=-1, keepdims=True))
        _acc(dcb_ref, first, jnp.sum(dy1, axis=0, keepdims=True))
        r32 = _row_iota((32, W_C))
        dcw = jnp.zeros((32, W_C), F32)
        _fill_shifted(ys_ref, ysh_ref)
        for j in range(CONV_K):
            tap = jnp.sum(dy1 * _shifted_rows(ys_ref, ysh_ref, pad - (CONV_K - 1) + j, tc), axis=0, keepdims=True)
            dcw = dcw + jnp.where(r32 == j, tap, 0.0)
        _acc(dcw_ref, first, dcw)
        ds_ref[0:tc, :] = dy1
        ds_ref[tc:tc + pad, :] = nx_ref[...]
        _fill_shifted(ds_ref, dsh_ref)
        dy0 = None
        for j in range(CONV_K):
            term = cw_ref[j:j + 1, :] * _shifted_rows(ds_ref, dsh_ref, CONV_K - 1 - j, tc)
            dy0 = term if dy0 is None else dy0 + term
        dp_ref[:, 0:W_C] = dy0 * sg
        dp_ref[:, W_C:2 * W_C] = dy0 * a * sg * (1.0 - sg)
        nx_ref[...] = dy1[0:pad, :]

    rev = lambda c: pl.BlockSpec((tc, W_C), lambda t, c=c: (nc - 1 - t, c))
    prev = lambda c: pl.BlockSpec((tc, W_C), lambda t, c=c: (jnp.maximum(nc - 2 - t, 0), c))
    full = lambda a: pl.BlockSpec(a.shape, lambda t: (0,) * a.ndim)
    params = [cw, cb, lg, lb, gg]
    vec = SDS((1, W_C), F32)
    outs = [SDS((s, 2 * W_C), F32), SDS((32, W_C), F32), vec, vec, vec, vec]
    return pl.pallas_call(
        body, name="conf_bwd", grid=(nc,),
        in_specs=[rev(3), rev(5), rev(6), prev(5), prev(6), rev(0)] + [full(a) for a in params],
        out_specs=[pl.BlockSpec((tc, 2 * W_C), lambda t: (nc - 1 - t, 0))]
        + [pl.BlockSpec(o.shape, lambda t: (0, 0)) for o in outs[1:]],
        out_shape=outs,
        scratch_shapes=[pltpu.VMEM((tc + pad, W_C), F32), pltpu.VMEM((tc + pad, W_C), F32), pltpu.VMEM((pad, W_C), F32),
                        pltpu.VMEM((8, tc + pad, W_C), F32), pltpu.VMEM((8, tc + pad, W_C), F32)],
        compiler_params=_cp("arbitrary"),
    )(dy, proj, proj, proj, proj, y1, *params)


def _assemble_dproj(dlru, dq, dcur, dprev, dconf):
    s = dq.shape[0]
    nb = s // BLK

    def body(dl_ref, dq_ref, dc_ref, dn_ref, df_ref, o_ref):
        n = pl.program_id(0)
        o_ref[:, 0:512] = dl_ref[...].astype(BF16)
        o_ref[:, 512:1024] = dq_ref[...].astype(BF16)
        o_ref[:, 1024:1280] = (dc_ref[...] + jnp.where(n < nb - 1, dn_ref[...], 0.0)).astype(BF16)
        o_ref[:, 1280:1792] = df_ref[...].astype(BF16)

    wide = pl.BlockSpec((BLK, 512), lambda n: (n, 0))
    return pl.pallas_call(
        body, name="assemble_dproj", grid=(nb,),
        in_specs=[wide, wide, pl.BlockSpec((BLK, 256), lambda n: (n, 0)),
                  pl.BlockSpec((BLK, 256), lambda n: (jnp.minimum(n + 1, nb - 1), 0)), wide],
        out_specs=pl.BlockSpec((BLK, P_IN), lambda n: (n, 0)), out_shape=SDS((s, P_IN), BF16),
        compiler_params=_cp("parallel"),
    )(dlru, dq, dcur, dprev, dconf)


def _loss_grad(y, t, tm):
    s = y.shape[0]

    def body(y_ref, t_ref, dy_ref, l_ref):
        err = y_ref[...] - t_ref[...]
        dy_ref[...] = err * (1.0 / D)
        _acc(l_ref, pl.program_id(0) == 0, jnp.sum(err * err, axis=0, keepdims=True))

    row = pl.BlockSpec((tm, D), lambda i: (i, 0))
    return pl.pallas_call(
        body, name="loss_grad", grid=(s // tm,), in_specs=[row, row],
        out_specs=[row, pl.BlockSpec((1, D), lambda i: (0, 0))],
        out_shape=[SDS((s, D), F32), SDS((1, D), F32)], compiler_params=_cp("arbitrary"),
    )(y, t)


def _block_diag(w):
    rows = [jnp.concatenate([w[h] if k == h else jnp.zeros((64, 64), w.dtype) for k in range(4)], axis=1) for h in range(4)]
    return jnp.concatenate(rows, axis=0)


def _diag_blocks(m):
    return jnp.stack([m[64 * h:64 * (h + 1), 64 * h:64 * (h + 1)] for h in range(4)])


def _layer_params(small, l):
    v = lambda name: small[name][l].reshape(1, -1)
    gg = small["group_g"][l]
    return dict(
        ffn1_pre=v("ffn1_pre_g"), ffn1_post=v("ffn1_post_g"), mix_pre=v("mix_pre_g"), mix_post=v("mix_post_g"),
        ffn2_pre=v("ffn2_pre_g"), ffn2_post=v("ffn2_post_g"), lru_cb=v("lru_conv_b"),
        wa=_block_diag(small["lru_w_a"][l]).astype(BF16), ba=v("lru_b_a"),
        wx=_block_diag(small["lru_w_x"][l]).astype(BF16), bx=v("lru_b_x"), lam=v("lru_lambda"),
        sinks8=jnp.broadcast_to(small["attn_sinks"][l][:, None], (NQ, 128)),
        conv_b=v("conv_b"), ln_g=v("conv_ln_g"), ln_b=v("conv_ln_b"),
        gg_a=gg[0:W_A].reshape(1, -1), gg_b=gg[W_A:W_A + W_B].reshape(1, -1), gg_c=gg[W_A + W_B:].reshape(1, -1),
    )


def _forward_layer(x, weights, p, tiles, deps=()):
    _, mm, _, tc, _ = tiles
    big = dict(weights("ffn1_gu", x))
    p = dict(p)
    sv = dict(x0=x)
    h1, g1, u1, a1 = _ffn_up(x, p["ffn1_pre"], big["ffn1_w_gu"], 0, mm, deps)
    big.update(weights("ffn1_down", a1))
    z1, x = _mm_rms_res(a1, big["ffn1_w_down"], 0, x, p["ffn1_post"], 0.5, mm, DFF, "ffn_down")
    sv.update(h1=h1, g1=g1, u1=u1, a1=a1, z1=z1, x1=x)
    big.update(weights("mix", x))
    p.update(lru_cw=big.pop("lru_conv_w"), conv_w=big.pop("conv_w"))
    hn, proj = _proj(x, p["mix_pre"], big["w_in"], 0, mm)
    yn_a, hl = _lru_fwd(proj, p["lru_cw"], p["lru_cb"], p["wa"], p["ba"], p["wx"], p["bx"], p["lam"], p["gg_a"], tc)
    yn_b, ob = _attn_fwd(proj, p["sinks8"], p["gg_b"])
    yn_c, y1 = _conf_fwd(proj, p["conv_w"], p["conv_b"], p["ln_g"], p["ln_b"], p["gg_c"], tc)
    ycat = jnp.concatenate([yn_a, yn_b, yn_c], axis=1)
    zo, x = _mm_rms_res(ycat, big["w_out"], 0, x, p["mix_post"], 1.0, mm, D, "mix_out")
    sv.update(hn=hn, proj=proj, hl=hl, ob=ob, y1=y1, ycat=ycat, zo=zo, x2=x)
    big.update(weights("ffn2", x))
    h2, g2, u2, a2 = _ffn_up(x, p["ffn2_pre"], big["ffn2_w_gu"], 0, mm)
    z2, x = _mm_rms_res(a2, big["ffn2_w_down"], 0, x, p["ffn2_post"], 0.5, mm, DFF, "ffn_down")
    sv.update(h2=h2, g2=g2, u2=u2, a2=a2, z2=z2, p=p, big=big)
    return x, sv


def _grad_buffers():
    empty = lambda *shape: lax.empty(shape, F32)
    return dict(ffn1_w_gu=empty(1, NSHARD, D, FH), ffn2_w_gu=empty(1, NSHARD, D, FH), ffn1_w_down=empty(1, 1, DFF, D),
                ffn2_w_down=empty(1, 1, DFF, D), w_in=empty(1, 1, D, P_IN), w_out=empty(1, 1, D, D))


def _backward_layer(dx, sv, bufs, tiles, stage):
    p, big = sv["p"], sv["big"]
    tm, mm, dw, tc, dh_rows = tiles
    gr = {}

    def ffn_bwd(dx, which, xin, h, g, u, a, z, pre, post, deps):
        dz, dpost = _rms_bwd(dx, z, post, 0.5, tm, "ffn_post_bwd", deps)
        dgu = _ffn_bwd_mid(dz, big[which + "_w_down"], 0, g, u, mm)
        bufs[which + "_w_down"] = _mm_tn_into(bufs[which + "_w_down"], a, dz, 0, 0, FH, D, dw, "dw_down")
        bufs[which + "_w_gu"] = _mm_tn_into(bufs[which + "_w_gu"], h, dgu, 0, 0, D, FH, dw, "dw_gate", 2, 0)
        bufs[which + "_w_gu"] = _mm_tn_into(bufs[which + "_w_gu"], h, dgu, 0, 2, D, FH, dw, "dw_up", 2, 1)
        deps = stage({n: bufs[n] for n in (which + "_w_gu", which + "_w_down")}, bufs[which + "_w_gu"])
        dxn, dpre = _ffn_bwd_dh(dgu, big[which + "_w_gu"], 0, xin, pre, dx, dh_rows, deps)
        return dxn, dpre, dpost

    dx, gr["ffn2_pre_g"], gr["ffn2_post_g"] = ffn_bwd(dx, "ffn2", sv["x2"], sv["h2"], sv["g2"], sv["u2"], sv["a2"],
                                                      sv["z2"], p["ffn2_pre"], p["ffn2_post"], ())
    do, gr["mix_post_g"] = _rms_bwd(dx, sv["zo"], p["mix_post"], 1.0, tm, "mix_post_bwd")
    bufs["w_out"] = _mm_tn_into(bufs["w_out"], sv["ycat"], do, 0, 0, D, D, dw, "dw_out")
    dy = _mm_nt(do, big["w_out"], 0, mm, "mix_dy")
    proj = sv["proj"]
    (dlru, dcw, gr["lru_conv_b"], dwa, gr["lru_b_a"], dwx, gr["lru_b_x"], gr["lru_lambda"], dgg_a) = _lru_bwd(
        dy, proj, sv["hl"], p["lru_cw"], p["lru_cb"], p["wa"], p["ba"], p["wx"], p["bx"], p["lam"], p["gg_a"], tc)
    dq, dcur, dprev, dsk, dgg_b = _attn_bwd(dy, proj, sv["ob"], p["sinks8"], p["gg_b"])
    dconf, dconvw, gr["conv_b"], gr["conv_ln_g"], gr["conv_ln_b"], dgg_c = _conf_bwd(
        dy, proj, sv["y1"], p["conv_w"], p["conv_b"], p["ln_g"], p["ln_b"], p["gg_c"], tc)
    dproj = _assemble_dproj(dlru, dq, dcur, dprev, dconf)
    bufs["w_in"] = _mm_tn_into(bufs["w_in"], sv["hn"], dproj, 0, 0, D, P_IN, dw, "dw_in")
    dx, gr["mix_pre_g"] = _mm_nt_rmsbwd(dproj, big["w_in"], 0, sv["x1"], p["mix_pre"], dx, mm)
    gr["lru_conv_w"] = dcw[0:LRU_K]
    gr["lru_w_a"] = _diag_blocks(dwa)
    gr["lru_w_x"] = _diag_blocks(dwx)
    gr["attn_sinks"] = dsk[:, 0]
    gr["conv_w"] = dconvw[0:CONV_K]
    gr["group_g"] = jnp.concatenate([dgg_a, dgg_b, dgg_c], axis=1)
    dx, gr["ffn1_pre_g"], gr["ffn1_post_g"] = ffn_bwd(dx, "ffn1", sv["x0"], sv["h1"], sv["g1"], sv["u1"], sv["a1"],
                                                      sv["z1"], p["ffn1_pre"], p["ffn1_post"],
                                                      stage({n: bufs[n] for n in ("w_in", "w_out")}, dx))
    return dx, gr


def _tiles(s):
    return min(1024, s), min(1024, s), min(2048, s), min(512, s // 2), min(512, s)


HBM_SPEC = pl.BlockSpec(memory_space=pltpu.HBM)
SEM_SPEC = pl.BlockSpec(memory_space=pltpu.SEMAPHORE)
EFFECT = pltpu.SideEffectType.DATAFLOW_SIDE_EFFECTING


def _place():
    x, y, c = lax.axis_index("x"), lax.axis_index("y"), lax.axis_index("c")
    return x, y, c, [(1 - x, y), (x, 1 - y), (1 - x, 1 - y)]


def _rcopy(src, dst, send_sems, recv_sems, k, to):
    return pltpu.make_async_remote_copy(src_ref=src, dst_ref=dst, send_sem=send_sems.at[k], recv_sem=recv_sems.at[k],
                                        device_id=to, device_id_type=MESH)


def _half(rows, which):
    return pl.ds(which * (rows // 2), rows // 2)


def _place_shard(w, l, p_idx, dtype, deps=()):
    _, rows, cols = w.shape
    tr = _rows_per_block(rows, cols, 16, SUM_BLOCK_ELEMS) if rows % 16 == 0 else rows
    deps = list(deps)

    def body(p_ref, buf_ref, w_ref, *rest):
        rest[len(deps)][...] = w_ref[...].astype(dtype)

    spec = pltpu.PrefetchScalarGridSpec(
        num_scalar_prefetch=1, grid=(rows // tr,),
        in_specs=[ANY, pl.BlockSpec((None, tr, cols), lambda i, pr: (l, i, 0))] + [ANY] * len(deps),
        out_specs=pl.BlockSpec((None, None, tr, cols), lambda i, pr: (0, pr[0], i, 0)))
    shape = (1, NSHARD, rows, cols)
    return pl.pallas_call(body, name="place_shard", grid_spec=spec, out_shape=SDS(shape, dtype),
                          input_output_aliases={1: 0}, compiler_params=_cp("parallel"),
                          )(p_idx, lax.empty(shape, dtype), w, *deps)


def _run_plans(plans, refs, send_sems, recv_sems):
    cps, b0, s0 = [], 0, 0
    for plan, nb, ns in plans:
        cps += plan(refs[b0:b0 + nb], send_sems, recv_sems, s0)
        b0, s0 = b0 + nb, s0 + ns
    return cps


def _exchange(name, bufs, plans):
    n = len(bufs)
    nsem = sum(ns for _, _, ns in plans)

    def body(*refs):
        cps = _run_plans(plans, refs[n:2 * n], refs[2 * n], refs[2 * n + 1])
        for cp in cps:
            cp.start()
        for cp in cps:
            cp.wait()

    return pl.pallas_call(
        body, name=name, in_specs=[ANY] * n, out_specs=[ANY] * n, out_shape=[SDS(b.shape, b.dtype) for b in bufs],
        input_output_aliases={a: a for a in range(n)},
        scratch_shapes=[pltpu.SemaphoreType.DMA((nsem,)), pltpu.SemaphoreType.DMA((nsem,))],
    )(*bufs)


def _exchange_start(name, bufs, plans, deps=()):
    n = len(bufs)
    nsem = sum(ns for _, _, ns in plans)
    deps = list(deps)
    first_out = n + len(deps)

    def body(*refs):
        for cp in _run_plans(plans, refs[:n], refs[first_out], refs[first_out + 1]):
            cp.start()
        token = refs[first_out + 2 + n]
        token[...] = jnp.zeros_like(token)

    outs = pl.pallas_call(
        body, name=name,
        out_shape=(pltpu.SemaphoreType.DMA((nsem,)), pltpu.SemaphoreType.DMA((nsem,)),
                   *[pltpu.HBM(b.shape, b.dtype) for b in bufs], SDS((8, 128), F32)),
        in_specs=[HBM_SPEC] * n + [ANY] * len(deps),
        out_specs=(SEM_SPEC, SEM_SPEC, *[HBM_SPEC] * n, pl.BlockSpec(memory_space=pltpu.VMEM)),
        input_output_aliases={a: 2 + a for a in range(n)},
        compiler_params=pltpu.CompilerParams(has_side_effects=EFFECT),
    )(*[pltpu.with_memory_space_constraint(b, pltpu.HBM) for b in bufs], *deps)
    return outs[0], outs[1], list(outs[2:2 + n]), outs[2 + n]


def _exchange_wait(name, send_sems, recv_sems, bufs, plans, after):
    n = len(bufs)

    def body(*refs):
        for cp in _run_plans(plans, refs[:n], refs[n], refs[n + 1]):
            cp.wait_send()
            cp.wait_recv()

    return pl.pallas_call(
        body, name=name, out_shape=[pltpu.HBM(b.shape, b.dtype) for b in bufs],
        in_specs=[HBM_SPEC] * n + [SEM_SPEC, SEM_SPEC, ANY], out_specs=[HBM_SPEC] * n,
        input_output_aliases={a: a for a in range(n)},
        compiler_params=pltpu.CompilerParams(has_side_effects=EFFECT),
    )(*bufs, send_sems, recv_sems, after)


def _plan_gather(refs, send_sems, recv_sems, base):
    x, y, c, chips = _place()
    p = 2 * x + y
    return [_rcopy(r.at[0, p], r.at[0, p], send_sems, recv_sems, base + 3 * a + j, (*chip, c))
            for a, r in enumerate(refs) for j, chip in enumerate(chips)]


def _plan_gather_half(refs, send_sems, recv_sems, base):
    x, y, c, chips = _place()
    p = 2 * x + y
    return [_rcopy(r.at[0, p, _half(r.shape[2], c)], r.at[0, p, _half(r.shape[2], c)], send_sems, recv_sems,
                   base + 3 * a + j, (*chip, c)) for a, r in enumerate(refs) for j, chip in enumerate(chips)]


def _plan_forward_half(refs, send_sems, recv_sems, base):
    x, y, c, chips = _place()
    cps = []
    for a, r in enumerate(refs):
        for j, chip in enumerate(chips):
            blk = r.at[0, 2 * chip[0] + chip[1], _half(r.shape[2], c)]
            cps.append(_rcopy(blk, blk, send_sems, recv_sems, base + 3 * a + j, (x, y, 1 - c)))
    return cps


def _plan_pair_exchange(refs, send_sems, recv_sems, base):
    x, y, c, _ = _place()
    n = len(refs) // 2
    return [_rcopy(refs[a].at[:, _half(refs[a].shape[1], 1 - c)], refs[n + a], send_sems, recv_sems, base + a,
                   (x, y, 1 - c)) for a in range(n)]


def _plan_chip_exchange(refs, send_sems, recv_sems, base):
    x, y, c, chips = _place()
    n = len(refs) // 2
    return [_rcopy(refs[a].at[2 * chip[0] + chip[1]], refs[n + a].at[j], send_sems, recv_sems, base + 3 * a + j,
                   (*chip, c)) for a in range(n) for j, chip in enumerate(chips)]


def _plan_pair_share(refs, send_sems, recv_sems, base):
    x, y, c, _ = _place()
    return [_rcopy(r.at[_half(r.shape[0], c)], r.at[_half(r.shape[0], c)], send_sems, recv_sems, base + a,
                   (x, y, 1 - c)) for a, r in enumerate(refs)]


def _plan_small_gather(refs, send_sems, recv_sems, base):
    x, y, c, _ = _place()
    me = 4 * x + 2 * y + c
    cps = []
    for m in range(1, NDEV):
        peer = (1 - x if m & 4 else x, 1 - y if m & 2 else y, 1 - c if m & 1 else c)
        cps.append(_rcopy(refs[0], refs[1].at[me], send_sems, recv_sems, base + m - 1, peer))
    return cps


def _sum_small(buf, gathered):
    def body(buf_ref, g_ref, o_ref):
        x, y, c, _ = _place()
        me = 4 * x + 2 * y + c
        total = jnp.where(me == 0, buf_ref[...], g_ref[0])
        for dev in range(1, NDEV):
            total = total + jnp.where(me == dev, buf_ref[...], g_ref[dev])
        o_ref[...] = total

    vm = pl.BlockSpec(memory_space=pltpu.VMEM)
    return pl.pallas_call(body, name="sum_small", in_specs=[vm, vm], out_specs=vm, out_shape=SDS(buf.shape, F32),
                          compiler_params=pltpu.CompilerParams(vmem_limit_bytes=VMEM_LIMIT))(buf, gathered)


BLOCK_ELEMS = 512 * 1024
SUM_BLOCK_ELEMS = 1024 * 1024


def _rows_per_block(rows, cols, mult, limit=BLOCK_ELEMS):
    best = None
    for tr in range(mult, rows + 1, mult):
        if rows % tr == 0 and tr * cols <= limit:
            best = tr
    assert best is not None, (rows, cols)
    return best


def _pair_sum(g, r, c_idx):
    nq, rows, cols = g.shape
    half = rows // 2
    tr = _rows_per_block(half, cols, 16, SUM_BLOCK_ELEMS)
    nb = half // tr

    def body(c_ref, g_ref, r_ref, t_ref):
        t_ref[...] = (g_ref[...] + r_ref[...]).astype(BF16)

    blk = pl.BlockSpec((None, tr, cols), lambda q, i, cr: (q, i, 0))
    spec = pltpu.PrefetchScalarGridSpec(
        num_scalar_prefetch=1, grid=(nq, nb),
        in_specs=[pl.BlockSpec((None, tr, cols), lambda q, i, cr: (q, cr[0] * nb + i, 0)), blk], out_specs=blk)
    return pl.pallas_call(body, name="grad_pair_sum", grid_spec=spec, out_shape=SDS((nq, half, cols), BF16),
                          compiler_params=_cp("parallel", "parallel"))(c_idx, g, r)


def _chip_sum(g, r, rr, cp_idx):
    _, rows, cols = g.shape
    half = rows // 2
    tr = _rows_per_block(half, cols, 16, SUM_BLOCK_ELEMS)
    nb = half // tr

    def body(cp_ref, buf_ref, g_ref, r_ref, rr_ref, o_ref):
        o_ref[...] = ((g_ref[...] + r_ref[...]) + rr_ref[0].astype(F32) + rr_ref[1].astype(F32) + rr_ref[2].astype(F32))

    spec = pltpu.PrefetchScalarGridSpec(
        num_scalar_prefetch=1, grid=(nb,),
        in_specs=[ANY, pl.BlockSpec((None, tr, cols), lambda i, cp: (cp[1], cp[0] * nb + i, 0)),
                  pl.BlockSpec((None, tr, cols), lambda i, cp: (cp[1], i, 0)),
                  pl.BlockSpec((3, tr, cols), lambda i, cp: (0, i, 0))],
        out_specs=pl.BlockSpec((tr, cols), lambda i, cp: (cp[0] * nb + i, 0)))
    return pl.pallas_call(body, name="grad_chip_sum", grid_spec=spec, out_shape=SDS((rows, cols), F32),
                          input_output_aliases={1: 0}, compiler_params=_cp("parallel"),
                          )(cp_idx, lax.empty((rows, cols), F32), g, r, rr)


def _adamw_math(w, g, m, v):
    mn = ADAM_B1 * m + (1.0 - ADAM_B1) * g
    vn = ADAM_B2 * v + (1.0 - ADAM_B2) * (g * g)
    m_hat = mn / (1.0 - ADAM_B1 ** ADAM_STEP)
    v_hat = vn / (1.0 - ADAM_B2 ** ADAM_STEP)
    return -ADAM_LR * (m_hat / (jnp.sqrt(v_hat) + ADAM_EPS) + ADAM_WD * w), mn, vn


def _adamw_layer(w, g, m, v, l, outs, deps=()):
    _, rows, cols = w.shape
    tr = _rows_per_block(rows, cols, 8)
    deps = list(deps)

    def body(*refs):
        w_ref, g_ref, m_ref, v_ref = refs[4:8]
        go_ref, d_ref, mo_ref, vo_ref = refs[8 + len(deps):]
        gg = g_ref[...]
        go_ref[...] = gg
        d_ref[...], mo_ref[...], vo_ref[...] = _adamw_math(w_ref[...], gg, m_ref[...], v_ref[...])

    blk = pl.BlockSpec((None, tr, cols), lambda i: (l, i, 0))
    return pl.pallas_call(
        body, name="adamw_layer", grid=(rows // tr,),
        in_specs=[ANY] * 4 + [blk, pl.BlockSpec((tr, cols), lambda i: (i, 0)), blk, blk] + [ANY] * len(deps),
        out_specs=[blk] * 4, out_shape=[SDS(w.shape, F32)] * 4, input_output_aliases={k: k for k in range(4)},
        compiler_params=_cp("parallel"))(*outs, w, g, m, v, *deps)


def _adamw_small(ws, gs, ms, vs, deps=()):
    n = len(ws)
    deps = list(deps)

    def body(*refs):
        refs = refs[:4 * n] + refs[4 * n + len(deps):]
        w, g, m, v, d_out, m_out, v_out = (refs[k * n:(k + 1) * n] for k in range(7))
        for k in range(n):
            d_out[k][...], m_out[k][...], v_out[k][...] = _adamw_math(w[k][...], g[k][...], m[k][...], v[k][...])

    vm = pl.BlockSpec(memory_space=pltpu.VMEM)
    outs = pl.pallas_call(body, name="adamw_small", in_specs=[vm] * (4 * n) + [ANY] * len(deps), out_specs=[vm] * (3 * n),
                          out_shape=[SDS(w.shape, F32) for w in ws] * 3,
                          compiler_params=pltpu.CompilerParams(vmem_limit_bytes=VMEM_LIMIT))(*ws, *gs, *ms, *vs, *deps)
    return outs[:n], outs[n:2 * n], outs[2 * n:]


_WEIGHTS = ["ffn1_pre_g", "ffn1_w_gu", "ffn1_w_down", "ffn1_post_g", "mix_pre_g", "w_in", "lru_conv_w", "lru_conv_b",
            "lru_w_a", "lru_b_a", "lru_w_x", "lru_b_x", "lru_lambda", "attn_sinks", "conv_w", "conv_b", "conv_ln_g",
            "conv_ln_b", "group_g", "w_out", "mix_post_g", "ffn2_pre_g", "ffn2_w_gu", "ffn2_w_down", "ffn2_post_g"]
_INPUTS = ["x"] + _WEIGHTS + ["loss_target"] + ["m_" + n for n in _WEIGHTS] + ["v_" + n for n in _WEIGHTS]
_BIG = ["ffn1_w_gu", "ffn1_w_down", "w_in", "w_out", "ffn2_w_gu", "ffn2_w_down"]
_SMALL_SHARDED = ["lru_conv_w", "conv_w"]
_SMALL_REPL = [n for n in _WEIGHTS if n not in _BIG and n not in _SMALL_SHARDED]

PACK_TILE = 8 * 128


def _pack(arrs):
    parts = []
    for a in arrs:
        flat = a.reshape(-1)
        parts.append(jnp.pad(flat, (0, -flat.shape[0] % PACK_TILE)).reshape(-1, 128))
    return jnp.concatenate(parts, axis=0)


def _unpack(buf, shapes):
    out, row = [], 0
    for shp in shapes:
        size = math.prod(shp)
        nrow = -(-size // PACK_TILE) * 8
        out.append(buf[row:row + nrow].reshape(-1)[:size].reshape(shp))
        row += nrow
    return out


def _unshard_cols(a):
    return a.transpose(0, 2, 1, 3).reshape(1, a.shape[2], NSHARD * a.shape[3])


_GROUPS = dict(ffn1_gu=["ffn1_w_gu"], ffn1_down=["ffn1_w_down"], mix=["w_in", "w_out", "lru_conv_w", "conv_w"],
               ffn2=["ffn2_w_gu", "ffn2_w_down"])


def _full_weights(group, gathered):
    g = dict(zip(_GROUPS[group], gathered))
    if group == "mix":
        return dict(w_in=_unshard_cols(g["w_in"]), w_out=g["w_out"].reshape(1, D, D),
                    lru_conv_w=_unshard_cols(g["lru_conv_w"])[0], conv_w=_unshard_cols(g["conv_w"])[0])
    return {n: (a.reshape(1, DFF, D) if n.endswith("w_down") else a) for n, a in g.items()}


def _by_shard(name, buf):
    if name.endswith("w_gu"):
        return buf[0]
    if name == "w_in":
        return buf.reshape(D, NSHARD, P_IN // NSHARD).transpose(1, 0, 2)
    return buf.reshape(NSHARD, buf.shape[2] // NSHARD, buf.shape[3])


class _Reducer:
    PLANS = (_plan_pair_exchange, _plan_chip_exchange, _plan_pair_share)

    def __init__(self, keys, gs, c_idx, cp_idx):
        self.keys, self.gs, self.c_idx, self.cp_idx = keys, gs, c_idx, cp_idx
        self.n = len(gs)
        self.step = 0
        self.result = None

    def inputs(self):
        n = self.n
        if self.step == 0:
            bufs = self.gs + [lax.empty((NSHARD, g.shape[1] // 2, g.shape[2]), F32) for g in self.gs]
        elif self.step == 1:
            ts = [_pair_sum(g, r, self.c_idx) for g, r in zip(self.gs, self.rs)]
            bufs = ts + [lax.empty((3,) + t.shape[1:], BF16) for t in ts]
        else:
            bufs = [_chip_sum(g, r, rr, self.cp_idx) for g, r, rr in zip(self.gs, self.rs, self.rrs)]
        return bufs, (self.PLANS[self.step], len(bufs), (n, 3 * n, n)[self.step])

    def absorb(self, done):
        n = self.n
        if self.step == 0:
            self.gs, self.rs = done[:n], done[n:]
        elif self.step == 1:
            self.rrs = done[n:]
        else:
            self.result = dict(zip(self.keys, done))
        self.step += 1


class _SmallGather:
    def __init__(self, buf):
        self.buf, self.step, self.result, self.gathered = buf, 0, {}, None

    def inputs(self):
        return [self.buf, jnp.zeros((NDEV,) + self.buf.shape, F32)], (_plan_small_gather, 2, NDEV - 1)

    def absorb(self, done):
        self.buf, self.gathered = done
        self.step = 3


class _ReducePipeline:
    def __init__(self, c_idx, cp_idx):
        self.c_idx, self.cp_idx = c_idx, cp_idx
        self.reducers, self.flying, self.calls = [], None, 0

    def add(self, layer, done):
        if done:
            keys = [(layer, n) for n in done]
            self.reducers.append(_Reducer(keys, [_by_shard(n, b) for n, b in done.items()], self.c_idx, self.cp_idx))

    def _next(self):
        active = [r for r in self.reducers if r.step < 3]
        bufs, plans = [], []
        for r in active:
            b, triple = r.inputs()
            bufs += b
            plans.append(triple)
        self.calls += 1
        return active, bufs, plans, "grad_exchange%d" % self.calls

    def _absorb(self, active, plans, done):
        at = 0
        for r, (_, nb, _) in zip(active, plans):
            r.absorb(done[at:at + nb])
            at += nb

    def _land(self, after):
        if self.flying is not None:
            active, plans, name, send_sems, recv_sems, bufs = self.flying
            self._absorb(active, plans, _exchange_wait(name + "_wait", send_sems, recv_sems, bufs, plans, after))
            self.flying = None

    def hook(self, after):
        self._land(after)
        active, bufs, plans, name = self._next()
        if not active:
            return []
        send_sems, recv_sems, bufs, token = _exchange_start(name + "_start", bufs, plans)
        self.flying = (active, plans, name, send_sems, recv_sems, bufs)
        return [token]

    def available(self):
        out = {}
        for r in self.reducers:
            if r.step == 3:
                out.update(r.result)
        return out

    def finish(self, after):
        self._land(after)
        while True:
            active, bufs, plans, name = self._next()
            if not active:
                break
            self._absorb(active, plans, _exchange(name, bufs, plans))
        out = {}
        for r in self.reducers:
            out.update(r.result)
        return out


def kernel(*args):
    d = dict(zip(_INPUTS, args, strict=True))
    xi, yi, ci = lax.axis_index("x"), lax.axis_index("y"), lax.axis_index("c")
    p = 2 * xi + yi
    c_idx = jnp.reshape(ci, (1,)).astype(jnp.int32)
    p_idx = jnp.reshape(p, (1,)).astype(jnp.int32)
    cp_idx = jnp.stack([ci, p]).astype(jnp.int32)
    x, target = d["x"][0], d["loss_target"][0]
    tiles = _tiles(x.shape[0])

    groups = [(l, grp) for l in range(DEPTH) for grp in _GROUPS]
    place = lambda l, grp, deps: [_place_shard(d[n], l, p_idx, BF16 if n in _BIG else F32, deps) for n in _GROUPS[grp]]
    first = place(*groups[0], ())
    half_plans = [(_plan_gather_half, len(first), 3 * len(first))]
    first_sems = _exchange_start("gather_first_start", first, half_plans)
    tokens = [first_sems[3]]
    flying = {}
    for l, grp in groups[1:]:
        placed = place(l, grp, tokens[:1])
        plans = [(_plan_gather, len(placed), 3 * len(placed))]
        send_sems, recv_sems, bufs, token = _exchange_start("gather_l%d_%s_start" % (l, grp), placed, plans, tokens[-1:])
        flying[l, grp] = (send_sems, recv_sems, bufs, plans)
        tokens.append(token)
    first = _exchange_wait("gather_first_wait", first_sems[0], first_sems[1], first_sems[2], half_plans, tokens[-1])
    ready = {groups[0]: _exchange("gather_first_forward", first, [(_plan_forward_half, len(first), 3 * len(first))])}

    def weights_of(l):
        def weights(grp, after):
            if (l, grp) not in ready:
                send_sems, recv_sems, bufs, plans = flying[l, grp]
                ready[l, grp] = _exchange_wait("gather_l%d_%s_wait" % (l, grp), send_sems, recv_sems, bufs, plans, after)
            return _full_weights(grp, ready[l, grp])
        return weights

    small = {n: d[n] for n in _SMALL_REPL}
    x1, sv0 = _forward_layer(x, weights_of(0), _layer_params(small, 0), tiles)
    x2, sv1 = _forward_layer(x1, weights_of(1), _layer_params(small, 1), tiles)
    dx, lcols = _loss_grad(x2, target, tiles[0])

    pipe = _ReducePipeline(c_idx, cp_idx)
    sgrads = [None] * DEPTH
    for l, sv in ((1, sv1), (0, sv0)):
        bufs = _grad_buffers()

        def stage(done, dx, l=l):
            pipe.add(l, done)
            return pipe.hook(dx)

        dx, sgrads[l] = _backward_layer(dx, sv, bufs, tiles, stage)
    grad_x = dx

    stacked = {n: jnp.stack([sgrads[l][n].reshape(d[n].shape[1:]) for l in range(DEPTH)]) for n in _SMALL_REPL}
    for n in _SMALL_SHARDED:
        stacked[n] = jnp.stack([sgrads[l][n] for l in range(DEPTH)])
    loss_part = jnp.pad((0.5 / D) * jnp.sum(lcols).reshape(1), (0, 127))
    order = _SMALL_REPL + _SMALL_SHARDED
    small_gather = _SmallGather(_pack([loss_part] + [stacked[n] for n in order]))
    pipe.reducers.append(small_gather)

    results = {n: tuple(lax.empty(d[n].shape, F32) for _ in range(4)) for n in _BIG}
    applied = set()

    def apply_ready(deps, last):
        for (l, n), g in pipe.available().items():
            if (l, n) not in applied:
                results[n] = _adamw_layer(d[n], g, d["m_" + n], d["v_" + n], l, results[n], deps)
                applied.add((l, n))
                last = results[n][1]
                deps = [last]
        return last

    last = apply_ready(pipe.hook(grad_x), grad_x)
    token = pipe.hook(last)
    summed = _unpack(_sum_small(small_gather.buf, small_gather.gathered), [(128,)] + [stacked[n].shape for n in order])
    loss = summed[0][0]
    grads = {}
    for n, g in zip(order, summed[1:]):
        if n in _SMALL_SHARDED:
            g = lax.dynamic_slice_in_dim(g, p * (g.shape[2] // NSHARD), g.shape[2] // NSHARD, axis=2)
        grads[n] = g
    delta, new_m, new_v = {}, {}, {}
    small_out = _adamw_small([d[n] for n in order], [grads[n] for n in order], [d["m_" + n] for n in order],
                             [d["v_" + n] for n in order], token)
    for out, res in zip((delta, new_m, new_v), small_out):
        out.update(zip(order, res))
    last = apply_ready([small_out[0][0]], small_out[0][0])
    pipe.finish(last)
    apply_ready((), last)
    for n in _BIG:
        grads[n], delta[n], new_m[n], new_v[n] = results[n]

    return (loss, grad_x[None], *[grads[n] for n in _WEIGHTS], *[delta[n] for n in _WEIGHTS],
            *[new_m[n] for n in _WEIGHTS], *[new_v[n] for n in _WEIGHTS])
```

```python
import math

import jax
import jax.numpy as jnp
import numpy as np
from jax import lax
from jax.experimental import pallas as pl
from jax.experimental.pallas import tpu as pltpu

F32 = jnp.float32
BF16 = jnp.bfloat16
SDS = jax.ShapeDtypeStruct

D = 1024
DFF = 2816
FH = DFF // 2
DEPTH = 2
W_A = 256
W_B = 512
W_C = 256
NQ = 8
HD = 64
BLK = 128
ATT_NB_FWD = 1
ATT_NB_BWD = 8
P_IN = 1792
LRU_K = 4
CONV_K = 31
LRU_C = 8.0
NORM_EPS = 1e-6
LN_EPS = 1e-5
NEG_BIG = -1e30
SCALE = 1.0 / math.sqrt(HD)

ADAM_LR = 0.001
ADAM_B1 = 0.9
ADAM_B2 = 0.999
ADAM_EPS = 1e-08
ADAM_WD = 0.01
ADAM_STEP = 10

VMEM_LIMIT = 60 * 1024 * 1024
NSHARD = 4
NDEV = 8

TN = (((0,), (0,)), ((), ()))
NT = (((1,), (1,)), ((), ()))

MESH = pl.DeviceIdType.MESH
ANY = pl.BlockSpec(memory_space=pl.ANY)


def _cp(*sem):
    return pltpu.CompilerParams(dimension_semantics=sem if sem else None, vmem_limit_bytes=VMEM_LIMIT)


def _rsq(x, eps):
    return lax.rsqrt(jnp.mean(x * x, axis=-1, keepdims=True) + eps)


def _rms_bwd_rows(x, g, dy):
    r = _rsq(x, NORM_EPS)
    xh = x * r
    dyg = dy * g
    dx = r * (dyg - xh * jnp.mean(dyg * xh, axis=-1, keepdims=True))
    return dx, dy * xh


def _sig(x):
    return jax.nn.sigmoid(x)


def _ffn_up(x, pre_g, wgu, l, tm, deps=()):
    s = x.shape[0]
    deps = list(deps)

    def body(x_ref, g_ref, wg_ref, wu_ref, *rest):
        h_ref, go_ref, uo_ref, a_ref = rest[len(deps):]

        @pl.when(pl.program_id(1) == 0)
        def _():
            xf = x_ref[...]
            h_ref[...] = (xf * _rsq(xf, NORM_EPS) * g_ref[...]).astype(BF16)

        h = h_ref[...]
        gg = jnp.dot(h, wg_ref[...], preferred_element_type=F32)
        uu = jnp.dot(h, wu_ref[...], preferred_element_type=F32)
        sg = _sig(gg)
        silu = gg * sg
        go_ref[...] = (uu * (sg * (1.0 + gg * (1.0 - sg)))).astype(BF16)
        uo_ref[...] = silu.astype(BF16)
        a_ref[...] = (silu * uu).astype(BF16)

    wide = pl.BlockSpec((tm, FH), lambda i, j: (i, j))
    return pl.pallas_call(
        body, name="ffn_up", grid=(s // tm, 2),
        in_specs=[pl.BlockSpec((tm, D), lambda i, j: (i, 0)), pl.BlockSpec((1, D), lambda i, j: (0, 0)),
                  pl.BlockSpec((None, None, D, FH), lambda i, j: (l, j, 0, 0)),
                  pl.BlockSpec((None, None, D, FH), lambda i, j: (l, j + 2, 0, 0))] + [ANY] * len(deps),
        out_specs=[pl.BlockSpec((tm, D), lambda i, j: (i, 0)), wide, wide, wide],
        out_shape=[SDS((s, D), BF16), SDS((s, DFF), BF16), SDS((s, DFF), BF16), SDS((s, DFF), BF16)],
        compiler_params=_cp("parallel", "arbitrary"),
    )(x, pre_g, wgu, wgu, *deps)


def _mm_rms_res(a, w, l, x, g, c, tm, tk, name):
    s, k_dim = a.shape
    nk = k_dim // tk

    def body(a_ref, w_ref, x_ref, g_ref, z_ref, x1_ref):
        k = pl.program_id(1)
        p = jnp.dot(a_ref[...], w_ref[...], preferred_element_type=F32)

        @pl.when(k == 0)
        def _():
            z_ref[...] = p

        @pl.when(k > 0)
        def _():
            z_ref[...] += p

        @pl.when(k == nk - 1)
        def _():
            z = z_ref[...]
            x1_ref[...] = x_ref[...] + c * (z * _rsq(z, NORM_EPS) * g_ref[...])

    row = pl.BlockSpec((tm, D), lambda i, k: (i, 0))
    return pl.pallas_call(
        body, name=name, grid=(s // tm, nk),
        in_specs=[pl.BlockSpec((tm, tk), lambda i, k: (i, k)), pl.BlockSpec((None, tk, D), lambda i, k: (l, k, 0)),
                  row, pl.BlockSpec((1, D), lambda i, k: (0, 0))],
        out_specs=[row, row],
        out_shape=[SDS((s, D), F32), SDS((s, D), F32)],
        compiler_params=_cp("parallel", "arbitrary"),
    )(a, w, x, g)


def _rms_bwd(dy, z, g, c, tm, name, deps=()):
    s = z.shape[0]
    deps = list(deps)

    def body(dy_ref, z_ref, g_ref, *rest):
        dz_ref, dg_ref = rest[len(deps):]
        dz, dgr = _rms_bwd_rows(z_ref[...], g_ref[...], c * dy_ref[...])
        dz_ref[...] = dz.astype(BF16)
        part = jnp.sum(dgr, axis=0, keepdims=True)

        @pl.when(pl.program_id(0) == 0)
        def _():
            dg_ref[...] = part

        @pl.when(pl.program_id(0) > 0)
        def _():
            dg_ref[...] += part

    row = pl.BlockSpec((tm, D), lambda i: (i, 0))
    vec = pl.BlockSpec((1, D), lambda i: (0, 0))
    return pl.pallas_call(
        body, name=name, grid=(s // tm,), in_specs=[row, row, vec] + [ANY] * len(deps), out_specs=[row, vec],
        out_shape=[SDS((s, D), BF16), SDS((1, D), F32)], compiler_params=_cp("arbitrary"),
    )(dy, z, g, *deps)


def _ffn_bwd_mid(dz, wd, l, dadg, dadu, tm):
    s = dz.shape[0]

    def body(dz_ref, wd_ref, g_ref, u_ref, dgu_ref):
        da = lax.dot_general(dz_ref[...], wd_ref[...], NT, preferred_element_type=F32)
        dgu_ref[:, 0:FH] = (da * g_ref[...].astype(F32)).astype(BF16)
        dgu_ref[:, FH:2 * FH] = (da * u_ref[...].astype(F32)).astype(BF16)

    wide = pl.BlockSpec((tm, FH), lambda i, j: (i, j))
    return pl.pallas_call(
        body, name="ffn_bwd_mid", grid=(s // tm, 2),
        in_specs=[pl.BlockSpec((tm, D), lambda i, j: (i, 0)), pl.BlockSpec((None, FH, D), lambda i, j: (l, j, 0)), wide, wide],
        out_specs=pl.BlockSpec((tm, 2 * FH), lambda i, j: (i, j)),
        out_shape=SDS((s, 2 * DFF), BF16),
        compiler_params=_cp("parallel", "arbitrary"),
    )(dz, wd, dadg, dadu)


def _ffn_bwd_dh(dgu, wgu, l, x, pre_g, dx1, tm, deps=()):
    s = x.shape[0]
    deps = list(deps)

    def body(dgu_ref, w_hbm, x_ref, g_ref, dx1_ref, *rest):
        dx_ref, dgp_ref, wcat_ref, sems = rest[len(deps):]
        i = pl.program_id(0)

        @pl.when(i == 0)
        def _():
            cps = [pltpu.make_async_copy(w_hbm.at[l, q], wcat_ref.at[:, pl.ds((2 * (q % 2) + q // 2) * FH, FH)], sems.at[q])
                   for q in range(NSHARD)]
            for cp in cps:
                cp.start()
            for cp in cps:
                cp.wait()

        dh = lax.dot_general(dgu_ref[...], wcat_ref[...], NT, preferred_element_type=F32)
        dx, dgr = _rms_bwd_rows(x_ref[...], g_ref[...], dh)
        dx_ref[...] = dx1_ref[...] + dx
        _acc(dgp_ref, i == 0, jnp.sum(dgr, axis=0, keepdims=True))

    row = pl.BlockSpec((tm, D), lambda i: (i, 0))
    vec = pl.BlockSpec((1, D), lambda i: (0, 0))
    return pl.pallas_call(
        body, name="ffn_bwd_dh", grid=(s // tm,),
        in_specs=[pl.BlockSpec((tm, 2 * DFF), lambda i: (i, 0)), ANY, row, vec, row] + [ANY] * len(deps),
        out_specs=[row, vec],
        out_shape=[SDS((s, D), F32), SDS((1, D), F32)],
        scratch_shapes=[pltpu.VMEM((D, 2 * DFF), BF16), pltpu.SemaphoreType.DMA((NSHARD,))],
        compiler_params=_cp("arbitrary"),
    )(dgu, wgu, x, pre_g, dx1, *deps)


def _mm_tn_into(buf, a, b, l, joff, tka, tn, ts, name, bstride=1, boff=0):
    s, ka = a.shape
    n = b.shape[1] // bstride

    def body(buf_ref, a_ref, b_ref, o_ref):
        p = lax.dot_general(a_ref[...], b_ref[...], TN, preferred_element_type=F32)

        @pl.when(pl.program_id(2) == 0)
        def _():
            o_ref[...] = p

        @pl.when(pl.program_id(2) > 0)
        def _():
            o_ref[...] += p

    return pl.pallas_call(
        body, name=name, grid=(ka // tka, n // tn, s // ts),
        in_specs=[pl.BlockSpec(memory_space=pl.ANY),
                  pl.BlockSpec((ts, tka), lambda ia, j, t: (t, ia)),
                  pl.BlockSpec((ts, tn), lambda ia, j, t: (t, bstride * j + boff))],
        out_specs=pl.BlockSpec((None, None, tka, tn), lambda ia, j, t: (l, joff + j, ia, 0)),
        out_shape=SDS(buf.shape, F32), input_output_aliases={0: 0},
        compiler_params=_cp("parallel", "parallel", "arbitrary"),
    )(buf, a, b)


def _proj(x, g, w_in, l, tm):
    s = x.shape[0]

    def body(x_ref, g_ref, w_ref, h_ref, p_ref):
        xf = x_ref[...]
        h = (xf * _rsq(xf, NORM_EPS) * g_ref[...]).astype(BF16)
        h_ref[...] = h
        p_ref[...] = jnp.dot(h, w_ref[...], preferred_element_type=F32)

    return pl.pallas_call(
        body, name="proj", grid=(s // tm,),
        in_specs=[pl.BlockSpec((tm, D), lambda i: (i, 0)), pl.BlockSpec((1, D), lambda i: (0, 0)),
                  pl.BlockSpec((None, D, P_IN), lambda i: (l, 0, 0))],
        out_specs=[pl.BlockSpec((tm, D), lambda i: (i, 0)), pl.BlockSpec((tm, P_IN), lambda i: (i, 0))],
        out_shape=[SDS((s, D), BF16), SDS((s, P_IN), F32)],
        compiler_params=_cp("parallel"),
    )(x, g, w_in)


def _mm_nt(a, w, l, tm, name):
    s, k_dim = a.shape
    n = w.shape[1]

    def body(a_ref, w_ref, o_ref):
        o_ref[...] = lax.dot_general(a_ref[...], w_ref[...], NT, preferred_element_type=F32)

    return pl.pallas_call(
        body, name=name, grid=(s // tm,),
        in_specs=[pl.BlockSpec((tm, k_dim), lambda i: (i, 0)), pl.BlockSpec((None, n, k_dim), lambda i: (l, 0, 0))],
        out_specs=pl.BlockSpec((tm, n), lambda i: (i, 0)),
        out_shape=SDS((s, n), F32), compiler_params=_cp("parallel"),
    )(a, w)


def _mm_nt_rmsbwd(dp, w_in, l, x, g, dx1, tm):
    s = x.shape[0]

    def body(dp_ref, w_ref, x_ref, g_ref, dx1_ref, dx_ref, dg_ref):
        dh = lax.dot_general(dp_ref[...], w_ref[...], NT, preferred_element_type=F32)
        dx, dgr = _rms_bwd_rows(x_ref[...], g_ref[...], dh)
        dx_ref[...] = dx1_ref[...] + dx
        part = jnp.sum(dgr, axis=0, keepdims=True)

        @pl.when(pl.program_id(0) == 0)
        def _():
            dg_ref[...] = part

        @pl.when(pl.program_id(0) > 0)
        def _():
            dg_ref[...] += part

    row = pl.BlockSpec((tm, D), lambda i: (i, 0))
    vec = pl.BlockSpec((1, D), lambda i: (0, 0))
    return pl.pallas_call(
        body, name="mix_bwd_dx", grid=(s // tm,),
        in_specs=[pl.BlockSpec((tm, P_IN), lambda i: (i, 0)), pl.BlockSpec((None, D, P_IN), lambda i: (l, 0, 0)), row, vec, row],
        out_specs=[row, vec], out_shape=[SDS((s, D), F32), SDS((1, D), F32)],
        compiler_params=_cp("arbitrary"),
    )(dp, w_in, x, g, dx1)


def _row_iota(shape):
    return lax.broadcasted_iota(jnp.int32, shape, 0)


def _lru_gates(xc, wa_ref, ba_ref, wx_ref, bx_ref, lam_ref):
    xb = xc.astype(BF16)
    r = _sig(jnp.dot(xb, wa_ref[...], preferred_element_type=F32) + ba_ref[...])
    ig = _sig(jnp.dot(xb, wx_ref[...], preferred_element_type=F32) + bx_ref[...])
    nl = -lam_ref[...]
    sp = jnp.maximum(nl, 0.0) + jnp.log(1.0 + jnp.exp(-jnp.abs(nl)))
    log_a = -LRU_C * r * sp
    a = jnp.exp(log_a)
    mlt = jnp.sqrt((1.0 + a * a) * jnp.tanh(-log_a))
    return r, ig, a, mlt, sp


def _conv_taps(src_ref, w_ref, k_taps, pad, tc):
    acc = None
    for j in range(k_taps):
        term = w_ref[j:j + 1, :] * src_ref[pl.ds(pad - (k_taps - 1) + j, tc), :]
        acc = term if acc is None else acc + term
    return acc


def _fill_shifted(src_ref, sh_ref):
    n = src_ref.shape[0] - 8
    for s in range(1, 8):
        sh_ref[s, 0:n, :] = src_ref[pl.ds(s, n), :]


def _shifted_rows(src_ref, sh_ref, offset, tc):
    if offset % 8 == 0:
        return src_ref[pl.ds(offset, tc), :]
    return sh_ref[offset % 8, pl.ds(offset - offset % 8, tc), :]


def _gelu_parts(x):
    c0 = math.sqrt(2.0 / math.pi)
    inner = c0 * (x + 0.044715 * x * x * x)
    t = jnp.tanh(inner)
    gl = 0.5 * x * (1.0 + t)
    dgl = 0.5 * (1.0 + t) + 0.5 * x * (1.0 - t * t) * c0 * (1.0 + 3.0 * 0.044715 * x * x)
    return gl, dgl


def _lru_fwd(proj, cw, cb, wa, ba, wx, bx, lam, gg, tc):
    s = proj.shape[0]
    pad = 8

    def body(xcur_ref, xprev_ref, gate_ref, cw_ref, cb_ref, wa_ref, ba_ref, wx_ref, bx_ref, lam_ref, gg_ref,
             yn_ref, h_ref, xs_ref, hc_ref):
        i = pl.program_id(0)

        @pl.when(i == 0)
        def _():
            hc_ref[...] = jnp.zeros_like(hc_ref)

        xs_ref[0:pad, :] = jnp.where(i > 0, xprev_ref[tc - pad:tc, :], 0.0)
        xs_ref[pad:pad + tc, :] = xcur_ref[...]
        xc = _conv_taps(xs_ref, cw_ref, LRU_K, pad, tc) + cb_ref[...]
        _, ig, a, mlt, _ = _lru_gates(xc, wa_ref, ba_ref, wx_ref, bx_ref, lam_ref)
        u = mlt * (ig * xc)
        row = _row_iota((tc, W_A))
        d = 1
        while d < tc:
            ok = row >= d
            a_sh = jnp.where(ok, pltpu.roll(a, d, axis=0), 1.0)
            u_sh = jnp.where(ok, pltpu.roll(u, d, axis=0), 0.0)
            u = a * u_sh + u
            a = a * a_sh
            d *= 2
        h = u + a * hc_ref[...]
        hc_ref[...] = jnp.sum(jnp.where(row == tc - 1, h, 0.0), axis=0, keepdims=True)
        h_ref[...] = h
        gl, _ = _gelu_parts(gate_ref[...])
        ya = gl * h
        yn_ref[...] = (ya * _rsq(ya, NORM_EPS) * gg_ref[...]).astype(BF16)

    blk = lambda c: pl.BlockSpec((tc, W_A), lambda i, c=c: (i, c))
    full = lambda a: pl.BlockSpec(a.shape, lambda i: (0,) * a.ndim)
    params = [cw, cb, wa, ba, wx, bx, lam, gg]
    return pl.pallas_call(
        body, name="lru_fwd", grid=(s // tc,),
        in_specs=[blk(0), pl.BlockSpec((tc, W_A), lambda i: (jnp.maximum(i - 1, 0), 0)), blk(1)] + [full(a) for a in params],
        out_specs=[pl.BlockSpec((tc, W_A), lambda i: (i, 0))] * 2,
        out_shape=[SDS((s, W_A), BF16), SDS((s, W_A), F32)],
        scratch_shapes=[pltpu.VMEM((tc + pad, W_A), F32), pltpu.VMEM((1, W_A), F32)],
        compiler_params=_cp("arbitrary"),
    )(proj, proj, proj, *params)


def _acc(ref, first, val):
    @pl.when(first)
    def _():
        ref[...] = val

    @pl.when(jnp.logical_not(first))
    def _():
        ref[...] += val


def _lru_bwd(dy, proj, h, cw, cb, wa, ba, wx, bx, lam, gg, tc):
    s = proj.shape[0]
    nc = s // tc
    pad = 8

    def body(dy_ref, xcur_ref, xprev_ref, gate_ref, h_ref, hprev_ref, cw_ref, cb_ref, wa_ref, ba_ref, wx_ref, bx_ref,
             lam_ref, gg_ref,
             dp_ref, dcw_ref, dcb_ref, dwa_ref, dba_ref, dwx_ref, dbx_ref, dlam_ref, dgg_ref,
             xs_ref, ds_ref, mu_ref, nx_ref):
        step = pl.program_id(0)
        i = nc - 1 - step
        first = step == 0

        @pl.when(first)
        def _():
            mu_ref[...] = jnp.zeros_like(mu_ref)
            nx_ref[...] = jnp.zeros_like(nx_ref)

        xs_ref[0:pad, :] = jnp.where(i > 0, xprev_ref[tc - pad:tc, :], 0.0)
        xs_ref[pad:pad + tc, :] = xcur_ref[...]
        xc = _conv_taps(xs_ref, cw_ref, LRU_K, pad, tc) + cb_ref[...]
        r, ig, a, mlt, sp = _lru_gates(xc, wa_ref, ba_ref, wx_ref, bx_ref, lam_ref)
        hh = h_ref[...]
        gate = gate_ref[...]
        gl, dgl = _gelu_parts(gate)
        ya = gl * hh
        dya, dggr = _rms_bwd_rows(ya, gg_ref[...], dy_ref[...])
        _acc(dgg_ref, first, jnp.sum(dggr, axis=0, keepdims=True))
        dp_ref[:, W_A:2 * W_A] = dya * hh * dgl
        dh = dya * gl

        row = _row_iota((tc, W_A))
        aa = a
        uu = a * dh
        d = 1
        while d < tc:
            ok = row < tc - d
            a_sh = jnp.where(ok, pltpu.roll(aa, tc - d, axis=0), 1.0)
            u_sh = jnp.where(ok, pltpu.roll(uu, tc - d, axis=0), 0.0)
            uu = uu + aa * u_sh
            aa = aa * a_sh
            d *= 2
        cin = mu_ref[...]
        mu = uu + aa * cin
        lam_t = dh + jnp.where(row == tc - 1, cin, pltpu.roll(mu, tc - 1, axis=0))
        mu_ref[...] = jnp.sum(jnp.where(row == 0, mu, 0.0), axis=0, keepdims=True)
        hm1 = jnp.where(row == 0, jnp.where(i > 0, pltpu.roll(hprev_ref[...], 1, axis=0), 0.0),
                        pltpu.roll(hh, 1, axis=0))
        da = lam_t * hm1
        du = lam_t
        dmlt = du * ig * xc
        dig = du * mlt * xc
        dxc = du * mlt * ig
        dlog_a = da * a - dmlt * (a * a / mlt)
        dr = dlog_a * (-LRU_C * sp)
        dsp = jnp.sum(dlog_a * (-LRU_C * r), axis=0, keepdims=True)
        _acc(dlam_ref, first, dsp * (-_sig(-lam_ref[...])))
        dga = dr * r * (1.0 - r)
        dgx = dig * ig * (1.0 - ig)
        _acc(dba_ref, first, jnp.sum(dga, axis=0, keepdims=True))
        _acc(dbx_ref, first, jnp.sum(dgx, axis=0, keepdims=True))
        xb = xc.astype(BF16)
        dgab = dga.astype(BF16)
        dgxb = dgx.astype(BF16)
        _acc(dwa_ref, first, lax.dot_general(xb, dgab, TN, preferred_element_type=F32))
        _acc(dwx_ref, first, lax.dot_general(xb, dgxb, TN, preferred_element_type=F32))
        dxc = (dxc + lax.dot_general(dgab, wa_ref[...], NT, preferred_element_type=F32)
               + lax.dot_general(dgxb, wx_ref[...], NT, preferred_element_type=F32))

        _acc(dcb_ref, first, jnp.sum(dxc, axis=0, keepdims=True))
        r8 = _row_iota((8, W_A))
        dcw = jnp.zeros((8, W_A), F32)
        for j in range(LRU_K):
            tap = jnp.sum(dxc * xs_ref[pl.ds(pad - (LRU_K - 1) + j, tc), :], axis=0, keepdims=True)
            dcw = dcw + jnp.where(r8 == j, tap, 0.0)
        _acc(dcw_ref, first, dcw)
        ds_ref[0:tc, :] = dxc
        ds_ref[tc:tc + pad, :] = nx_ref[...]
        dlx = None
        for j in range(LRU_K):
            term = cw_ref[j:j + 1, :] * ds_ref[pl.ds(LRU_K - 1 - j, tc), :]
            dlx = term if dlx is None else dlx + term
        dp_ref[:, 0:W_A] = dlx
        nx_ref[...] = dxc[0:pad, :]

    rev = lambda c: pl.BlockSpec((tc, W_A), lambda t, c=c: (nc - 1 - t, c))
    prev = lambda c: pl.BlockSpec((tc, W_A), lambda t, c=c: (jnp.maximum(nc - 2 - t, 0), c))
    full = lambda a: pl.BlockSpec(a.shape, lambda t: (0,) * a.ndim)
    params = [cw, cb, wa, ba, wx, bx, lam, gg]
    vec = SDS((1, W_A), F32)
    sq = SDS((W_A, W_A), F32)
    outs = [SDS((s, 2 * W_A), F32), SDS((8, W_A), F32), vec, sq, vec, sq, vec, vec, vec]
    return pl.pallas_call(
        body, name="lru_bwd", grid=(nc,),
        in_specs=[rev(0), rev(0), prev(0), rev(1), rev(0), prev(0)] + [full(a) for a in params],
        out_specs=[pl.BlockSpec((tc, 2 * W_A), lambda t: (nc - 1 - t, 0))]
        + [pl.BlockSpec(o.shape, lambda t: (0, 0)) for o in outs[1:]],
        out_shape=outs,
        scratch_shapes=[pltpu.VMEM((tc + pad, W_A), F32), pltpu.VMEM((tc + pad, W_A), F32),
                        pltpu.VMEM((1, W_A), F32), pltpu.VMEM((pad, W_A), F32)],
        compiler_params=_cp("arbitrary"),
    )(dy, proj, proj, proj, h, h, *params)


def _attn_stack(qa, qb, kvh):
    lane = lax.broadcasted_iota(jnp.int32, qa.shape, 1)
    keep = (lane >= HD) if kvh == 1 else (lane < HD)
    parts = []
    for tile in (qa, qb):
        for half in (0, 1):
            y = tile if half == kvh else pltpu.roll(tile, HD, axis=1)
            parts.append(jnp.where(keep, y, 0.0))
    return jnp.concatenate(parts, axis=0)


def _attn_unstack(o, kvh):
    lane = lax.broadcasted_iota(jnp.int32, (BLK, 2 * HD), 1)
    tiles = []
    for t in range(2):
        halves = []
        for half in (0, 1):
            blk = o[(2 * t + half) * BLK:(2 * t + half + 1) * BLK, :]
            halves.append(blk if half == kvh else pltpu.roll(blk, HD, axis=1))
        tiles.append(jnp.where(lane < HD, halves[0], halves[1]))
    return tiles


def _attn_stack_all(x_ref_or_val):
    return jnp.concatenate([_attn_stack(x_ref_or_val[:, 256 * kvh:256 * kvh + 128],
                                        x_ref_or_val[:, 256 * kvh + 128:256 * kvh + 256], kvh) for kvh in range(2)], axis=0)


def _attn_unstack_all(o, dst_ref):
    for kvh in range(2):
        ta, tb = _attn_unstack(o[4 * BLK * kvh:4 * BLK * (kvh + 1), :], kvh)
        dst_ref[:, 256 * kvh:256 * kvh + 128] = ta
        dst_ref[:, 256 * kvh + 128:256 * kvh + 256] = tb


def _attn_windows(cur_ref, prev_ref, nb):
    blocks = [prev_ref[...]] + [cur_ref[b * BLK:(b + 1) * BLK, :] for b in range(nb)]
    return [jnp.concatenate(blocks[b:b + 2], axis=0).astype(BF16) for b in range(nb)]


def _attn_bias():
    qi = np.arange(NQ * BLK)[:, None] % BLK
    kj = np.arange(2 * BLK)[None, :]
    rel = BLK + qi - kj
    ok = (rel >= 0) & (rel < BLK)
    return jnp.asarray(np.stack([np.where(ok & (kj >= BLK), 0.0, NEG_BIG), np.where(ok, 0.0, NEG_BIG)]), F32)


def _attn_probs(qs, kw, first, sink_ref, bias_ref):
    rows = NQ * BLK
    bias = bias_ref[1] if first is False else jnp.where(first, bias_ref[0], bias_ref[1])
    sh = lax.dot_general(qs.astype(BF16), kw, NT, preferred_element_type=F32) * SCALE + bias
    head = lax.broadcasted_iota(jnp.int32, (rows, 1), 0) // BLK
    sk = jnp.zeros((rows, 1), F32)
    for h in range(NQ):
        sk = jnp.where(head == h, sink_ref[h:h + 1, 0:1], sk)
    m = jnp.maximum(jnp.max(sh, axis=-1, keepdims=True), sk)
    e = jnp.exp(sh - m)
    es = jnp.exp(sk - m)
    rz = 1.0 / (jnp.sum(e, axis=-1, keepdims=True) + es)
    return e * rz, es * rz


def _attn_fwd(proj, sinks8, gg):
    s = proj.shape[0]
    nb = ATT_NB_FWD

    def body(q_ref, kc_ref, kp_ref, vc_ref, vp_ref, sink_ref, gg_ref, bias_ref, yn_ref, ob_ref):
        kws, vws = _attn_windows(kc_ref, kp_ref, nb), _attn_windows(vc_ref, vp_ref, nb)
        for b in range(nb):
            rows = pl.ds(b * BLK, BLK)
            first = (pl.program_id(0) == 0) if b == 0 else False
            p, _ = _attn_probs(_attn_stack_all(q_ref.at[rows, :]), kws[b], first, sink_ref, bias_ref)
            _attn_unstack_all(jnp.dot(p.astype(BF16), vws[b], preferred_element_type=F32), ob_ref.at[rows, :])
        ob = ob_ref[...]
        yn_ref[...] = (ob * _rsq(ob, NORM_EPS) * gg_ref[...]).astype(BF16)

    tb = nb * BLK
    cur = lambda c: pl.BlockSpec((tb, 128), lambda m, c=c: (m, c))
    prev = lambda c: pl.BlockSpec((BLK, 128), lambda m, c=c: (jnp.maximum(nb * m - 1, 0), c))
    out = pl.BlockSpec((tb, W_B), lambda m: (m, 0))
    return pl.pallas_call(
        body, name="attn_fwd", grid=(s // tb,),
        in_specs=[pl.BlockSpec((tb, W_B), lambda m: (m, 1)), cur(8), prev(8), cur(9), prev(9),
                  pl.BlockSpec((8, 128), lambda n: (0, 0)), pl.BlockSpec((1, W_B), lambda n: (0, 0)),
                  pl.BlockSpec((2, NQ * BLK, 2 * BLK), lambda n: (0, 0, 0))],
        out_specs=[out, out], out_shape=[SDS((s, W_B), BF16), SDS((s, W_B), F32)],
        compiler_params=_cp("parallel"),
    )(proj, proj, proj, proj, proj, sinks8, gg, _attn_bias())


def _attn_bwd(dy, proj, ob, sinks8, gg):
    s = proj.shape[0]
    nb = ATT_NB_BWD

    def body(dya_ref, dyb_ref, q_ref, kc_ref, kp_ref, vc_ref, vp_ref, ob_ref, sink_ref, gg_ref, bias_ref,
             dq_ref, dcur_ref, dprev_ref, dsink_ref, dgg_ref):
        first = pl.program_id(0) == 0
        kws, vws = _attn_windows(kc_ref, kp_ref, nb), _attn_windows(vc_ref, vp_ref, nb)
        dyn = jnp.concatenate([dya_ref[...], dyb_ref[...]], axis=1)
        dob, dggr = _rms_bwd_rows(ob_ref[...], gg_ref[...], dyn)
        _acc(dgg_ref, first, jnp.sum(dggr, axis=0, keepdims=True))
        r8 = _row_iota((8, 128))
        dsk = jnp.zeros((8, 128), F32)
        for b in range(nb):
            rows = pl.ds(b * BLK, BLK)
            qs = _attn_stack_all(q_ref.at[rows, :])
            p, psink = _attn_probs(qs, kws[b], first if b == 0 else False, sink_ref, bias_ref)
            dosb = _attn_stack_all(dob[b * BLK:(b + 1) * BLK, :]).astype(BF16)
            dp = lax.dot_general(dosb, vws[b], NT, preferred_element_type=F32)
            dd = jnp.sum(p * dp, axis=-1, keepdims=True)
            dsb = (p * (dp - dd) * SCALE).astype(BF16)
            dsink_rows = -psink * dd
            for h in range(NQ):
                dsk = dsk + jnp.where(r8 == h, jnp.sum(dsink_rows[h * BLK:(h + 1) * BLK, :], axis=0, keepdims=True), 0.0)
            _attn_unstack_all(jnp.dot(dsb, kws[b], preferred_element_type=F32), dq_ref.at[rows, :])
            dkw = lax.dot_general(dsb, qs.astype(BF16), TN, preferred_element_type=F32)
            dvw = lax.dot_general(p.astype(BF16), dosb, TN, preferred_element_type=F32)
            dprev_ref[rows, 0:128] = dkw[0:BLK, :]
            dprev_ref[rows, 128:256] = dvw[0:BLK, :]
            dcur_ref[rows, 0:128] = dkw[BLK:2 * BLK, :]
            dcur_ref[rows, 128:256] = dvw[BLK:2 * BLK, :]
        _acc(dsink_ref, first, dsk)

    tb = nb * BLK
    cur = lambda c: pl.BlockSpec((tb, 128), lambda m, c=c: (m, c))
    prev = lambda c: pl.BlockSpec((BLK, 128), lambda m, c=c: (jnp.maximum(nb * m - 1, 0), c))
    wide = pl.BlockSpec((tb, W_B), lambda m: (m, 0))
    half = pl.BlockSpec((tb, 256), lambda m: (m, 0))
    return pl.pallas_call(
        body, name="attn_bwd", grid=(s // tb,),
        in_specs=[pl.BlockSpec((tb, 256), lambda m: (m, 1)), pl.BlockSpec((tb, 256), lambda m: (m, 2)),
                  pl.BlockSpec((tb, W_B), lambda m: (m, 1)), cur(8), prev(8), cur(9), prev(9), wide,
                  pl.BlockSpec((8, 128), lambda n: (0, 0)), pl.BlockSpec((1, W_B), lambda n: (0, 0)),
                  pl.BlockSpec((2, NQ * BLK, 2 * BLK), lambda n: (0, 0, 0))],
        out_specs=[wide, half, half, pl.BlockSpec((8, 128), lambda n: (0, 0)), pl.BlockSpec((1, W_B), lambda n: (0, 0))],
        out_shape=[SDS((s, W_B), F32), SDS((s, 256), F32), SDS((s, 256), F32), SDS((8, 128), F32), SDS((1, W_B), F32)],
        compiler_params=_cp("arbitrary"),
    )(dy, dy, proj, proj, proj, proj, proj, ob, sinks8, gg, _attn_bias())


def _ln_parts(y1, eps=LN_EPS):
    mu = jnp.mean(y1, axis=-1, keepdims=True)
    xc = y1 - mu
    rstd = lax.rsqrt(jnp.mean(xc * xc, axis=-1, keepdims=True) + eps)
    return xc * rstd, rstd


def _conf_fwd(proj, cw, cb, lg, lb, gg, tc):
    s = proj.shape[0]
    pad = 32

    def body(ac_ref, gc_ref, ap_ref, gp_ref, cw_ref, cb_ref, lg_ref, lb_ref, gg_ref, yn_ref, y1_ref, ys_ref, sh_ref):
        i = pl.program_id(0)
        tail = ap_ref[tc - pad:tc, :] * _sig(gp_ref[tc - pad:tc, :])
        ys_ref[0:pad, :] = jnp.where(i > 0, tail, 0.0)
        ys_ref[pad:pad + tc, :] = ac_ref[...] * _sig(gc_ref[...])
        _fill_shifted(ys_ref, sh_ref)
        y1 = cb_ref[...]
        for j in range(CONV_K):
            y1 = y1 + cw_ref[j:j + 1, :] * _shifted_rows(ys_ref, sh_ref, pad - (CONV_K - 1) + j, tc)
        y1_ref[...] = y1
        xh, _ = _ln_parts(y1)
        yl = xh * lg_ref[...] + lb_ref[...]
        yc = yl * _sig(yl)
        yn_ref[...] = (yc * _rsq(yc, NORM_EPS) * gg_ref[...]).astype(BF16)

    cur = lambda c: pl.BlockSpec((tc, W_C), lambda i, c=c: (i, c))
    prev = lambda c: pl.BlockSpec((tc, W_C), lambda i, c=c: (jnp.maximum(i - 1, 0), c))
    full = lambda a: pl.BlockSpec(a.shape, lambda i: (0,) * a.ndim)
    params = [cw, cb, lg, lb, gg]
    out = pl.BlockSpec((tc, W_C), lambda i: (i, 0))
    return pl.pallas_call(
        body, name="conf_fwd", grid=(s // tc,),
        in_specs=[cur(5), cur(6), prev(5), prev(6)] + [full(a) for a in params],
        out_specs=[out, out], out_shape=[SDS((s, W_C), BF16), SDS((s, W_C), F32)],
        scratch_shapes=[pltpu.VMEM((tc + pad, W_C), F32), pltpu.VMEM((8, tc + pad, W_C), F32)],
        compiler_params=_cp("parallel"),
    )(proj, proj, proj, proj, *params)


def _conf_bwd(dy, proj, y1, cw, cb, lg, lb, gg, tc):
    s = proj.shape[0]
    nc = s // tc
    pad = 32

    def body(dy_ref, ac_ref, gc_ref, ap_ref, gp_ref, y1_ref, cw_ref, cb_ref, lg_ref, lb_ref, gg_ref,
             dp_ref, dcw_ref, dcb_ref, dlg_ref, dlb_ref, dgg_ref, ys_ref, ds_ref, nx_ref, ysh_ref, dsh_ref):
        step = pl.program_id(0)
        i = nc - 1 - step
        first = step == 0

        @pl.when(first)
        def _():
            nx_ref[...] = jnp.zeros_like(nx_ref)

        a = ac_ref[...]
        sg = _sig(gc_ref[...])
        tail = ap_ref[tc - pad:tc, :] * _sig(gp_ref[tc - pad:tc, :])
        ys_ref[0:pad, :] = jnp.where(i > 0, tail, 0.0)
        ys_ref[pad:pad + tc, :] = a * sg
        xh, rstd = _ln_parts(y1_ref[...])
        yl = xh * lg_ref[...] + lb_ref[...]
        sl = _sig(yl)
        yc = yl * sl
        dyc, dggr = _rms_bwd_rows(yc, gg_ref[...], dy_ref[...])
        _acc(dgg_ref, first, jnp.sum(dggr, axis=0, keepdims=True))
        dyl = dyc * sl * (1.0 + yl * (1.0 - sl))
        _acc(dlg_ref, first, jnp.sum(dyl * xh, axis=0, keepdims=True))
        _acc(dlb_ref, first, jnp.sum(dyl, axis=0, keepdims=True))
        dxh = dyl * lg_ref[...]
        dy1 = rstd * (dxh - jnp.mean(dxh, axis=-1, keepdims=True) - xh * jnp.mean(dxh * xh, axis=-1, keepdims=True))
        _acc(dcb_ref, first, jnp.sum(dy1, axis=0, keepdims=True))
        r32 = _row_iota((32, W_C))
        dcw = jnp.zeros((32, W_C), F32)
        _fill_shifted(ys_ref, ysh_ref)
        for j in range(CONV_K):
            tap = jnp.sum(dy1 * _shifted_rows(ys_ref, ysh_ref, pad - (CONV_K - 1) + j, tc), axis=0, keepdims=True)
            dcw = dcw + jnp.where(r32 == j, tap, 0.0)
        _acc(dcw_ref, first, dcw)
        ds_ref[0:tc, :] = dy1
        ds_ref[tc:tc + pad, :] = nx_ref[...]
        _fill_shifted(ds_ref, dsh_ref)
        dy0 = None
        for j in range(CONV_K):
            term = cw_ref[j:j + 1, :] * _shifted_rows(ds_ref, dsh_ref, CONV_K - 1 - j, tc)
            dy0 = term if dy0 is None else dy0 + term
        dp_ref[:, 0:W_C] = dy0 * sg
        dp_ref[:, W_C:2 * W_C] = dy0 * a * sg * (1.0 - sg)
        nx_ref[...] = dy1[0:pad, :]

    rev = lambda c: pl.BlockSpec((tc, W_C), lambda t, c=c: (nc - 1 - t, c))
    prev = lambda c: pl.BlockSpec((tc, W_C), lambda t, c=c: (jnp.maximum(nc - 2 - t, 0), c))
    full = lambda a: pl.BlockSpec(a.shape, lambda t: (0,) * a.ndim)
    params = [cw, cb, lg, lb, gg]
    vec = SDS((1, W_C), F32)
    outs = [SDS((s, 2 * W_C), F32), SDS((32, W_C), F32), vec, vec, vec, vec]
    return pl.pallas_call(
        body, name="conf_bwd", grid=(nc,),
        in_specs=[rev(3), rev(5), rev(6), prev(5), prev(6), rev(0)] + [full(a) for a in params],
        out_specs=[pl.BlockSpec((tc, 2 * W_C), lambda t: (nc - 1 - t, 0))]
        + [pl.BlockSpec(o.shape, lambda t: (0, 0)) for o in outs[1:]],
        out_shape=outs,
        scratch_shapes=[pltpu.VMEM((tc + pad, W_C), F32), pltpu.VMEM((tc + pad, W_C), F32), pltpu.VMEM((pad, W_C), F32),
                        pltpu.VMEM((8, tc + pad, W_C), F32), pltpu.VMEM((8, tc + pad, W_C), F32)],
        compiler_params=_cp("arbitrary"),
    )(dy, proj, proj, proj, proj, y1, *params)


def _assemble_dproj(dlru, dq, dcur, dprev, dconf):
    s = dq.shape[0]
    nb = s // BLK

    def body(dl_ref, dq_ref, dc_ref, dn_ref, df_ref, o_ref):
        n = pl.program_id(0)
        o_ref[:, 0:512] = dl_ref[...].astype(BF16)
        o_ref[:, 512:1024] = dq_ref[...].astype(BF16)
        o_ref[:, 1024:1280] = (dc_ref[...] + jnp.where(n < nb - 1, dn_ref[...], 0.0)).astype(BF16)
        o_ref[:, 1280:1792] = df_ref[...].astype(BF16)

    wide = pl.BlockSpec((BLK, 512), lambda n: (n, 0))
    return pl.pallas_call(
        body, name="assemble_dproj", grid=(nb,),
        in_specs=[wide, wide, pl.BlockSpec((BLK, 256), lambda n: (n, 0)),
                  pl.BlockSpec((BLK, 256), lambda n: (jnp.minimum(n + 1, nb - 1), 0)), wide],
        out_specs=pl.BlockSpec((BLK, P_IN), lambda n: (n, 0)), out_shape=SDS((s, P_IN), BF16),
        compiler_params=_cp("parallel"),
    )(dlru, dq, dcur, dprev, dconf)


def _loss_grad(y, t, tm):
    s = y.shape[0]

    def body(y_ref, t_ref, dy_ref, l_ref):
        err = y_ref[...] - t_ref[...]
        dy_ref[...] = err * (1.0 / D)
        _acc(l_ref, pl.program_id(0) == 0, jnp.sum(err * err, axis=0, keepdims=True))

    row = pl.BlockSpec((tm, D), lambda i: (i, 0))
    return pl.pallas_call(
        body, name="loss_grad", grid=(s // tm,), in_specs=[row, row],
        out_specs=[row, pl.BlockSpec((1, D), lambda i: (0, 0))],
        out_shape=[SDS((s, D), F32), SDS((1, D), F32)], compiler_params=_cp("arbitrary"),
    )(y, t)


def _block_diag(w):
    rows = [jnp.concatenate([w[h] if k == h else jnp.zeros((64, 64), w.dtype) for k in range(4)], axis=1) for h in range(4)]
    return jnp.concatenate(rows, axis=0)


def _diag_blocks(m):
    return jnp.stack([m[64 * h:64 * (h + 1), 64 * h:64 * (h + 1)] for h in range(4)])


def _layer_params(small, l):
    v = lambda name: small[name][l].reshape(1, -1)
    gg = small["group_g"][l]
    return dict(
        ffn1_pre=v("ffn1_pre_g"), ffn1_post=v("ffn1_post_g"), mix_pre=v("mix_pre_g"), mix_post=v("mix_post_g"),
        ffn2_pre=v("ffn2_pre_g"), ffn2_post=v("ffn2_post_g"), lru_cb=v("lru_conv_b"),
        wa=_block_diag(small["lru_w_a"][l]).astype(BF16), ba=v("lru_b_a"),
        wx=_block_diag(small["lru_w_x"][l]).astype(BF16), bx=v("lru_b_x"), lam=v("lru_lambda"),
        sinks8=jnp.broadcast_to(small["attn_sinks"][l][:, None], (NQ, 128)),
        conv_b=v("conv_b"), ln_g=v("conv_ln_g"), ln_b=v("conv_ln_b"),
        gg_a=gg[0:W_A].reshape(1, -1), gg_b=gg[W_A:W_A + W_B].reshape(1, -1), gg_c=gg[W_A + W_B:].reshape(1, -1),
    )


def _forward_layer(x, weights, p, tiles, deps=()):
    _, mm, _, tc, _ = tiles
    big = dict(weights("ffn1_gu", x))
    p = dict(p)
    sv = dict(x0=x)
    h1, g1, u1, a1 = _ffn_up(x, p["ffn1_pre"], big["ffn1_w_gu"], 0, mm, deps)
    big.update(weights("ffn1_down", a1))
    z1, x = _mm_rms_res(a1, big["ffn1_w_down"], 0, x, p["ffn1_post"], 0.5, mm, DFF, "ffn_down")
    sv.update(h1=h1, g1=g1, u1=u1, a1=a1, z1=z1, x1=x)
    big.update(weights("mix", x))
    p.update(lru_cw=big.pop("lru_conv_w"), conv_w=big.pop("conv_w"))
    hn, proj = _proj(x, p["mix_pre"], big["w_in"], 0, mm)
    yn_a, hl = _lru_fwd(proj, p["lru_cw"], p["lru_cb"], p["wa"], p["ba"], p["wx"], p["bx"], p["lam"], p["gg_a"], tc)
    yn_b, ob = _attn_fwd(proj, p["sinks8"], p["gg_b"])
    yn_c, y1 = _conf_fwd(proj, p["conv_w"], p["conv_b"], p["ln_g"], p["ln_b"], p["gg_c"], tc)
    ycat = jnp.concatenate([yn_a, yn_b, yn_c], axis=1)
    zo, x = _mm_rms_res(ycat, big["w_out"], 0, x, p["mix_post"], 1.0, mm, D, "mix_out")
    sv.update(hn=hn, proj=proj, hl=hl, ob=ob, y1=y1, ycat=ycat, zo=zo, x2=x)
    big.update(weights("ffn2", x))
    h2, g2, u2, a2 = _ffn_up(x, p["ffn2_pre"], big["ffn2_w_gu"], 0, mm)
    z2, x = _mm_rms_res(a2, big["ffn2_w_down"], 0, x, p["ffn2_post"], 0.5, mm, DFF, "ffn_down")
    sv.update(h2=h2, g2=g2, u2=u2, a2=a2, z2=z2, p=p, big=big)
    return x, sv


def _grad_buffers():
    empty = lambda *shape: lax.empty(shape, F32)
    return dict(ffn1_w_gu=empty(1, NSHARD, D, FH), ffn2_w_gu=empty(1, NSHARD, D, FH), ffn1_w_down=empty(1, 1, DFF, D),
                ffn2_w_down=empty(1, 1, DFF, D), w_in=empty(1, 1, D, P_IN), w_out=empty(1, 1, D, D))


def _backward_layer(dx, sv, bufs, tiles, stage):
    p, big = sv["p"], sv["big"]
    tm, mm, dw, tc, dh_rows = tiles
    gr = {}

    def ffn_bwd(dx, which, xin, h, g, u, a, z, pre, post, deps):
        dz, dpost = _rms_bwd(dx, z, post, 0.5, tm, "ffn_post_bwd", deps)
        dgu = _ffn_bwd_mid(dz, big[which + "_w_down"], 0, g, u, mm)
        bufs[which + "_w_down"] = _mm_tn_into(bufs[which + "_w_down"], a, dz, 0, 0, FH, D, dw, "dw_down")
        bufs[which + "_w_gu"] = _mm_tn_into(bufs[which + "_w_gu"], h, dgu, 0, 0, D, FH, dw, "dw_gate", 2, 0)
        bufs[which + "_w_gu"] = _mm_tn_into(bufs[which + "_w_gu"], h, dgu, 0, 2, D, FH, dw, "dw_up", 2, 1)
        deps = stage({n: bufs[n] for n in (which + "_w_gu", which + "_w_down")}, bufs[which + "_w_gu"])
        dxn, dpre = _ffn_bwd_dh(dgu, big[which + "_w_gu"], 0, xin, pre, dx, dh_rows, deps)
        return dxn, dpre, dpost

    dx, gr["ffn2_pre_g"], gr["ffn2_post_g"] = ffn_bwd(dx, "ffn2", sv["x2"], sv["h2"], sv["g2"], sv["u2"], sv["a2"],
                                                      sv["z2"], p["ffn2_pre"], p["ffn2_post"], ())
    do, gr["mix_post_g"] = _rms_bwd(dx, sv["zo"], p["mix_post"], 1.0, tm, "mix_post_bwd")
    bufs["w_out"] = _mm_tn_into(bufs["w_out"], sv["ycat"], do, 0, 0, D, D, dw, "dw_out")
    dy = _mm_nt(do, big["w_out"], 0, mm, "mix_dy")
    proj = sv["proj"]
    (dlru, dcw, gr["lru_conv_b"], dwa, gr["lru_b_a"], dwx, gr["lru_b_x"], gr["lru_lambda"], dgg_a) = _lru_bwd(
        dy, proj, sv["hl"], p["lru_cw"], p["lru_cb"], p["wa"], p["ba"], p["wx"], p["bx"], p["lam"], p["gg_a"], tc)
    dq, dcur, dprev, dsk, dgg_b = _attn_bwd(dy, proj, sv["ob"], p["sinks8"], p["gg_b"])
    dconf, dconvw, gr["conv_b"], gr["conv_ln_g"], gr["conv_ln_b"], dgg_c = _conf_bwd(
        dy, proj, sv["y1"], p["conv_w"], p["conv_b"], p["ln_g"], p["ln_b"], p["gg_c"], tc)
    dproj = _assemble_dproj(dlru, dq, dcur, dprev, dconf)
    bufs["w_in"] = _mm_tn_into(bufs["w_in"], sv["hn"], dproj, 0, 0, D, P_IN, dw, "dw_in")
    dx, gr["mix_pre_g"] = _mm_nt_rmsbwd(dproj, big["w_in"], 0, sv["x1"], p["mix_pre"], dx, mm)
    gr["lru_conv_w"] = dcw[0:LRU_K]
    gr["lru_w_a"] = _diag_blocks(dwa)
    gr["lru_w_x"] = _diag_blocks(dwx)
    gr["attn_sinks"] = dsk[:, 0]
    gr["conv_w"] = dconvw[0:CONV_K]
    gr["group_g"] = jnp.concatenate([dgg_a, dgg_b, dgg_c], axis=1)
    dx, gr["ffn1_pre_g"], gr["ffn1_post_g"] = ffn_bwd(dx, "ffn1", sv["x0"], sv["h1"], sv["g1"], sv["u1"], sv["a1"],
                                                      sv["z1"], p["ffn1_pre"], p["ffn1_post"],
                                                      stage({n: bufs[n] for n in ("w_in", "w_out")}, dx))
    return dx, gr


def _tiles(s):
    return min(1024, s), min(1024, s), min(2048, s), min(512, s // 2), min(512, s)


HBM_SPEC = pl.BlockSpec(memory_space=pltpu.HBM)
SEM_SPEC = pl.BlockSpec(memory_space=pltpu.SEMAPHORE)
EFFECT = pltpu.SideEffectType.DATAFLOW_SIDE_EFFECTING


def _place():
    x, y, c = lax.axis_index("x"), lax.axis_index("y"), lax.axis_index("c")
    return x, y, c, [(1 - x, y), (x, 1 - y), (1 - x, 1 - y)]


def _rcopy(src, dst, send_sems, recv_sems, k, to):
    return pltpu.make_async_remote_copy(src_ref=src, dst_ref=dst, send_sem=send_sems.at[k], recv_sem=recv_sems.at[k],
                                        device_id=to, device_id_type=MESH)


def _half(rows, which):
    return pl.ds(which * (rows // 2), rows // 2)


def _place_shard(w, l, p_idx, dtype, deps=()):
    _, rows, cols = w.shape
    tr = _rows_per_block(rows, cols, 16, SUM_BLOCK_ELEMS) if rows % 16 == 0 else rows
    deps = list(deps)

    def body(p_ref, buf_ref, w_ref, *rest):
        rest[len(deps)][...] = w_ref[...].astype(dtype)

    spec = pltpu.PrefetchScalarGridSpec(
        num_scalar_prefetch=1, grid=(rows // tr,),
        in_specs=[ANY, pl.BlockSpec((None, tr, cols), lambda i, pr: (l, i, 0))] + [ANY] * len(deps),
        out_specs=pl.BlockSpec((None, None, tr, cols), lambda i, pr: (0, pr[0], i, 0)))
    shape = (1, NSHARD, rows, cols)
    return pl.pallas_call(body, name="place_shard", grid_spec=spec, out_shape=SDS(shape, dtype),
                          input_output_aliases={1: 0}, compiler_params=_cp("parallel"),
                          )(p_idx, lax.empty(shape, dtype), w, *deps)


def _run_plans(plans, refs, send_sems, recv_sems):
    cps, b0, s0 = [], 0, 0
    for plan, nb, ns in plans:
        cps += plan(refs[b0:b0 + nb], send_sems, recv_sems, s0)
        b0, s0 = b0 + nb, s0 + ns
    return cps


def _exchange(name, bufs, plans):
    n = len(bufs)
    nsem = sum(ns for _, _, ns in plans)

    def body(*refs):
        cps = _run_plans(plans, refs[n:2 * n], refs[2 * n], refs[2 * n + 1])
        for cp in cps:
            cp.start()
        for cp in cps:
            cp.wait()

    return pl.pallas_call(
        body, name=name, in_specs=[ANY] * n, out_specs=[ANY] * n, out_shape=[SDS(b.shape, b.dtype) for b in bufs],
        input_output_aliases={a: a for a in range(n)},
        scratch_shapes=[pltpu.SemaphoreType.DMA((nsem,)), pltpu.SemaphoreType.DMA((nsem,))],
    )(*bufs)


def _exchange_start(name, bufs, plans, deps=()):
    n = len(bufs)
    nsem = sum(ns for _, _, ns in plans)
    deps = list(deps)
    first_out = n + len(deps)

    def body(*refs):
        for cp in _run_plans(plans, refs[:n], refs[first_out], refs[first_out + 1]):
            cp.start()
        token = refs[first_out + 2 + n]
        token[...] = jnp.zeros_like(token)

    outs = pl.pallas_call(
        body, name=name,
        out_shape=(pltpu.SemaphoreType.DMA((nsem,)), pltpu.SemaphoreType.DMA((nsem,)),
                   *[pltpu.HBM(b.shape, b.dtype) for b in bufs], SDS((8, 128), F32)),
        in_specs=[HBM_SPEC] * n + [ANY] * len(deps),
        out_specs=(SEM_SPEC, SEM_SPEC, *[HBM_SPEC] * n, pl.BlockSpec(memory_space=pltpu.VMEM)),
        input_output_aliases={a: 2 + a for a in range(n)},
        compiler_params=pltpu.CompilerParams(has_side_effects=EFFECT),
    )(*[pltpu.with_memory_space_constraint(b, pltpu.HBM) for b in bufs], *deps)
    return outs[0], outs[1], list(outs[2:2 + n]), outs[2 + n]


def _exchange_wait(name, send_sems, recv_sems, bufs, plans, after):
    n = len(bufs)

    def body(*refs):
        for cp in _run_plans(plans, refs[:n], refs[n], refs[n + 1]):
            cp.wait_send()
            cp.wait_recv()

    return pl.pallas_call(
        body, name=name, out_shape=[pltpu.HBM(b.shape, b.dtype) for b in bufs],
        in_specs=[HBM_SPEC] * n + [SEM_SPEC, SEM_SPEC, ANY], out_specs=[HBM_SPEC] * n,
        input_output_aliases={a: a for a in range(n)},
        compiler_params=pltpu.CompilerParams(has_side_effects=EFFECT),
    )(*bufs, send_sems, recv_sems, after)


def _plan_gather(refs, send_sems, recv_sems, base):
    x, y, c, chips = _place()
    p = 2 * x + y
    return [_rcopy(r.at[0, p], r.at[0, p], send_sems, recv_sems, base + 3 * a + j, (*chip, c))
            for a, r in enumerate(refs) for j, chip in enumerate(chips)]


def _plan_gather_half(refs, send_sems, recv_sems, base):
    x, y, c, chips = _place()
    p = 2 * x + y
    return [_rcopy(r.at[0, p, _half(r.shape[2], c)], r.at[0, p, _half(r.shape[2], c)], send_sems, recv_sems,
                   base + 3 * a + j, (*chip, c)) for a, r in enumerate(refs) for j, chip in enumerate(chips)]


def _plan_forward_half(refs, send_sems, recv_sems, base):
    x, y, c, chips = _place()
    cps = []
    for a, r in enumerate(refs):
        for j, chip in enumerate(chips):
            blk = r.at[0, 2 * chip[0] + chip[1], _half(r.shape[2], c)]
            cps.append(_rcopy(blk, blk, send_sems, recv_sems, base + 3 * a + j, (x, y, 1 - c)))
    return cps


def _plan_pair_exchange(refs, send_sems, recv_sems, base):
    x, y, c, _ = _place()
    n = len(refs) // 2
    return [_rcopy(refs[a].at[:, _half(refs[a].shape[1], 1 - c)], refs[n + a], send_sems, recv_sems, base + a,
                   (x, y, 1 - c)) for a in range(n)]


def _plan_chip_exchange(refs, send_sems, recv_sems, base):
    x, y, c, chips = _place()
    n = len(refs) // 2
    return [_rcopy(refs[a].at[2 * chip[0] + chip[1]], refs[n + a].at[j], send_sems, recv_sems, base + 3 * a + j,
                   (*chip, c)) for a in range(n) for j, chip in enumerate(chips)]


def _plan_pair_share(refs, send_sems, recv_sems, base):
    x, y, c, _ = _place()
    return [_rcopy(r.at[_half(r.shape[0], c)], r.at[_half(r.shape[0], c)], send_sems, recv_sems, base + a,
                   (x, y, 1 - c)) for a, r in enumerate(refs)]


def _plan_small_gather(refs, send_sems, recv_sems, base):
    x, y, c, _ = _place()
    me = 4 * x + 2 * y + c
    cps = []
    for m in range(1, NDEV):
        peer = (1 - x if m & 4 else x, 1 - y if m & 2 else y, 1 - c if m & 1 else c)
        cps.append(_rcopy(refs[0], refs[1].at[me], send_sems, recv_sems, base + m - 1, peer))
    return cps


def _sum_small(buf, gathered):
    def body(buf_ref, g_ref, o_ref):
        x, y, c, _ = _place()
        me = 4 * x + 2 * y + c
        total = jnp.where(me == 0, buf_ref[...], g_ref[0])
        for dev in range(1, NDEV):
            total = total + jnp.where(me == dev, buf_ref[...], g_ref[dev])
        o_ref[...] = total

    vm = pl.BlockSpec(memory_space=pltpu.VMEM)
    return pl.pallas_call(body, name="sum_small", in_specs=[vm, vm], out_specs=vm, out_shape=SDS(buf.shape, F32),
                          compiler_params=pltpu.CompilerParams(vmem_limit_bytes=VMEM_LIMIT))(buf, gathered)


BLOCK_ELEMS = 512 * 1024
SUM_BLOCK_ELEMS = 1024 * 1024


def _rows_per_block(rows, cols, mult, limit=BLOCK_ELEMS):
    best = None
    for tr in range(mult, rows + 1, mult):
        if rows % tr == 0 and tr * cols <= limit:
            best = tr
    assert best is not None, (rows, cols)
    return best


def _pair_sum(g, r, c_idx):
    nq, rows, cols = g.shape
    half = rows // 2
    tr = _rows_per_block(half, cols, 16, SUM_BLOCK_ELEMS)
    nb = half // tr

    def body(c_ref, g_ref, r_ref, t_ref):
        t_ref[...] = (g_ref[...] + r_ref[...]).astype(BF16)

    blk = pl.BlockSpec((None, tr, cols), lambda q, i, cr: (q, i, 0))
    spec = pltpu.PrefetchScalarGridSpec(
        num_scalar_prefetch=1, grid=(nq, nb),
        in_specs=[pl.BlockSpec((None, tr, cols), lambda q, i, cr: (q, cr[0] * nb + i, 0)), blk], out_specs=blk)
    return pl.pallas_call(body, name="grad_pair_sum", grid_spec=spec, out_shape=SDS((nq, half, cols), BF16),
                          compiler_params=_cp("parallel", "parallel"))(c_idx, g, r)


def _chip_sum(g, r, rr, cp_idx):
    _, rows, cols = g.shape
    half = rows // 2
    tr = _rows_per_block(half, cols, 16, SUM_BLOCK_ELEMS)
    nb = half // tr

    def body(cp_ref, buf_ref, g_ref, r_ref, rr_ref, o_ref):
        o_ref[...] = ((g_ref[...] + r_ref[...]) + rr_ref[0].astype(F32) + rr_ref[1].astype(F32) + rr_ref[2].astype(F32))

    spec = pltpu.PrefetchScalarGridSpec(
        num_scalar_prefetch=1, grid=(nb,),
        in_specs=[ANY, pl.BlockSpec((None, tr, cols), lambda i, cp: (cp[1], cp[0] * nb + i, 0)),
                  pl.BlockSpec((None, tr, cols), lambda i, cp: (cp[1], i, 0)),
                  pl.BlockSpec((3, tr, cols), lambda i, cp: (0, i, 0))],
        out_specs=pl.BlockSpec((tr, cols), lambda i, cp: (cp[0] * nb + i, 0)))
    return pl.pallas_call(body, name="grad_chip_sum", grid_spec=spec, out_shape=SDS((rows, cols), F32),
                          input_output_aliases={1: 0}, compiler_params=_cp("parallel"),
                          )(cp_idx, lax.empty((rows, cols), F32), g, r, rr)


def _adamw_math(w, g, m, v):
    mn = ADAM_B1 * m + (1.0 - ADAM_B1) * g
    vn = ADAM_B2 * v + (1.0 - ADAM_B2) * (g * g)
    m_hat = mn / (1.0 - ADAM_B1 ** ADAM_STEP)
    v_hat = vn / (1.0 - ADAM_B2 ** ADAM_STEP)
    return -ADAM_LR * (m_hat / (jnp.sqrt(v_hat) + ADAM_EPS) + ADAM_WD * w), mn, vn


def _adamw_layer(w, g, m, v, l, outs, deps=()):
    _, rows, cols = w.shape
    tr = _rows_per_block(rows, cols, 8)
    deps = list(deps)

    def body(*refs):
        w_ref, g_ref, m_ref, v_ref = refs[4:8]
        go_ref, d_ref, mo_ref, vo_ref = refs[8 + len(deps):]
        gg = g_ref[...]
        go_ref[...] = gg
        d_ref[...], mo_ref[...], vo_ref[...] = _adamw_math(w_ref[...], gg, m_ref[...], v_ref[...])

    blk = pl.BlockSpec((None, tr, cols), lambda i: (l, i, 0))
    return pl.pallas_call(
        body, name="adamw_layer", grid=(rows // tr,),
        in_specs=[ANY] * 4 + [blk, pl.BlockSpec((tr, cols), lambda i: (i, 0)), blk, blk] + [ANY] * len(deps),
        out_specs=[blk] * 4, out_shape=[SDS(w.shape, F32)] * 4, input_output_aliases={k: k for k in range(4)},
        compiler_params=_cp("parallel"))(*outs, w, g, m, v, *deps)


def _adamw_small(ws, gs, ms, vs, deps=()):
    n = len(ws)
    deps = list(deps)

    def body(*refs):
        refs = refs[:4 * n] + refs[4 * n + len(deps):]
        w, g, m, v, d_out, m_out, v_out = (refs[k * n:(k + 1) * n] for k in range(7))
        for k in range(n):
            d_out[k][...], m_out[k][...], v_out[k][...] = _adamw_math(w[k][...], g[k][...], m[k][...], v[k][...])

    vm = pl.BlockSpec(memory_space=pltpu.VMEM)
    outs = pl.pallas_call(body, name="adamw_small", in_specs=[vm] * (4 * n) + [ANY] * len(deps), out_specs=[vm] * (3 * n),
                          out_shape=[SDS(w.shape, F32) for w in ws] * 3,
                          compiler_params=pltpu.CompilerParams(vmem_limit_bytes=VMEM_LIMIT))(*ws, *gs, *ms, *vs, *deps)
    return outs[:n], outs[n:2 * n], outs[2 * n:]


_WEIGHTS = ["ffn1_pre_g", "ffn1_w_gu", "ffn1_w_down", "ffn1_post_g", "mix_pre_g", "w_in", "lru_conv_w", "lru_conv_b",
            "lru_w_a", "lru_b_a", "lru_w_x", "lru_b_x", "lru_lambda", "attn_sinks", "conv_w", "conv_b", "conv_ln_g",
            "conv_ln_b", "group_g", "w_out", "mix_post_g", "ffn2_pre_g", "ffn2_w_gu", "ffn2_w_down", "ffn2_post_g"]
_INPUTS = ["x"] + _WEIGHTS + ["loss_target"] + ["m_" + n for n in _WEIGHTS] + ["v_" + n for n in _WEIGHTS]
_BIG = ["ffn1_w_gu", "ffn1_w_down", "w_in", "w_out", "ffn2_w_gu", "ffn2_w_down"]
_SMALL_SHARDED = ["lru_conv_w", "conv_w"]
_SMALL_REPL = [n for n in _WEIGHTS if n not in _BIG and n not in _SMALL_SHARDED]

PACK_TILE = 8 * 128


def _pack(arrs):
    parts = []
    for a in arrs:
        flat = a.reshape(-1)
        parts.append(jnp.pad(flat, (0, -flat.shape[0] % PACK_TILE)).reshape(-1, 128))
    return jnp.concatenate(parts, axis=0)


def _unpack(buf, shapes):
    out, row = [], 0
    for shp in shapes:
        size = math.prod(shp)
        nrow = -(-size // PACK_TILE) * 8
        out.append(buf[row:row + nrow].reshape(-1)[:size].reshape(shp))
        row += nrow
    return out


def _unshard_cols(a):
    return a.transpose(0, 2, 1, 3).reshape(1, a.shape[2], NSHARD * a.shape[3])


_GROUPS = dict(ffn1_gu=["ffn1_w_gu"], ffn1_down=["ffn1_w_down"], mix=["w_in", "w_out", "lru_conv_w", "conv_w"],
               ffn2=["ffn2_w_gu", "ffn2_w_down"])


def _full_weights(group, gathered):
    g = dict(zip(_GROUPS[group], gathered))
    if group == "mix":
        return dict(w_in=_unshard_cols(g["w_in"]), w_out=g["w_out"].reshape(1, D, D),
                    lru_conv_w=_unshard_cols(g["lru_conv_w"])[0], conv_w=_unshard_cols(g["conv_w"])[0])
    return {n: (a.reshape(1, DFF, D) if n.endswith("w_down") else a) for n, a in g.items()}


def _by_shard(name, buf):
    if name.endswith("w_gu"):
        return buf[0]
    if name == "w_in":
        return buf.reshape(D, NSHARD, P_IN // NSHARD).transpose(1, 0, 2)
    return buf.reshape(NSHARD, buf.shape[2] // NSHARD, buf.shape[3])


class _Reducer:
    PLANS = (_plan_pair_exchange, _plan_chip_exchange, _plan_pair_share)

    def __init__(self, keys, gs, c_idx, cp_idx):
        self.keys, self.gs, self.c_idx, self.cp_idx = keys, gs, c_idx, cp_idx
        self.n = len(gs)
        self.step = 0
        self.result = None

    def inputs(self):
        n = self.n
        if self.step == 0:
            bufs = self.gs + [lax.empty((NSHARD, g.shape[1] // 2, g.shape[2]), F32) for g in self.gs]
        elif self.step == 1:
            ts = [_pair_sum(g, r, self.c_idx) for g, r in zip(self.gs, self.rs)]
            bufs = ts + [lax.empty((3,) + t.shape[1:], BF16) for t in ts]
        else:
            bufs = [_chip_sum(g, r, rr, self.cp_idx) for g, r, rr in zip(self.gs, self.rs, self.rrs)]
        return bufs, (self.PLANS[self.step], len(bufs), (n, 3 * n, n)[self.step])

    def absorb(self, done):
        n = self.n
        if self.step == 0:
            self.gs, self.rs = done[:n], done[n:]
        elif self.step == 1:
            self.rrs = done[n:]
        else:
            self.result = dict(zip(self.keys, done))
        self.step += 1


class _SmallGather:
    def __init__(self, buf):
        self.buf, self.step, self.result, self.gathered = buf, 0, {}, None

    def inputs(self):
        return [self.buf, jnp.zeros((NDEV,) + self.buf.shape, F32)], (_plan_small_gather, 2, NDEV - 1)

    def absorb(self, done):
        self.buf, self.gathered = done
        self.step = 3


class _ReducePipeline:
    def __init__(self, c_idx, cp_idx):
        self.c_idx, self.cp_idx = c_idx, cp_idx
        self.reducers, self.flying, self.calls = [], None, 0

    def add(self, layer, done):
        if done:
            keys = [(layer, n) for n in done]
            self.reducers.append(_Reducer(keys, [_by_shard(n, b) for n, b in done.items()], self.c_idx, self.cp_idx))

    def _next(self):
        active = [r for r in self.reducers if r.step < 3]
        bufs, plans = [], []
        for r in active:
            b, triple = r.inputs()
            bufs += b
            plans.append(triple)
        self.calls += 1
        return active, bufs, plans, "grad_exchange%d" % self.calls

    def _absorb(self, active, plans, done):
        at = 0
        for r, (_, nb, _) in zip(active, plans):
            r.absorb(done[at:at + nb])
            at += nb

    def _land(self, after):
        if self.flying is not None:
            active, plans, name, send_sems, recv_sems, bufs = self.flying
            self._absorb(active, plans, _exchange_wait(name + "_wait", send_sems, recv_sems, bufs, plans, after))
            self.flying = None

    def hook(self, after):
        self._land(after)
        active, bufs, plans, name = self._next()
        if not active:
            return []
        send_sems, recv_sems, bufs, token = _exchange_start(name + "_start", bufs, plans)
        self.flying = (active, plans, name, send_sems, recv_sems, bufs)
        return [token]

    def available(self):
        out = {}
        for r in self.reducers:
            if r.step == 3:
                out.update(r.result)
        return out

    def finish(self, after):
        self._land(after)
        while True:
            active, bufs, plans, name = self._next()
            if not active:
                break
            self._absorb(active, plans, _exchange(name, bufs, plans))
        out = {}
        for r in self.reducers:
            out.update(r.result)
        return out


def kernel(*args):
    d = dict(zip(_INPUTS, args, strict=True))
    xi, yi, ci = lax.axis_index("x"), lax.axis_index("y"), lax.axis_index("c")
    p = 2 * xi + yi
    c_idx = jnp.reshape(ci, (1,)).astype(jnp.int32)
    p_idx = jnp.reshape(p, (1,)).astype(jnp.int32)
    cp_idx = jnp.stack([ci, p]).astype(jnp.int32)
    x, target = d["x"][0], d["loss_target"][0]
    tiles = _tiles(x.shape[0])

    groups = [(l, grp) for l in range(DEPTH) for grp in _GROUPS]
    place = lambda l, grp, deps: [_place_shard(d[n], l, p_idx, BF16 if n in _BIG else F32, deps) for n in _GROUPS[grp]]
    first = place(*groups[0], ())
    half_plans = [(_plan_gather_half, len(first), 3 * len(first))]
    first_sems = _exchange_start("gather_first_start", first, half_plans)
    tokens = [first_sems[3]]
    flying = {}
    for l, grp in groups[1:]:
        placed = place(l, grp, tokens[:1])
        plans = [(_plan_gather, len(placed), 3 * len(placed))]
        send_sems, recv_sems, bufs, token = _exchange_start("gather_l%d_%s_start" % (l, grp), placed, plans, tokens[-1:])
        flying[l, grp] = (send_sems, recv_sems, bufs, plans)
        tokens.append(token)
    first = _exchange_wait("gather_first_wait", first_sems[0], first_sems[1], first_sems[2], half_plans, tokens[-1])
    ready = {groups[0]: _exchange("gather_first_forward", first, [(_plan_forward_half, len(first), 3 * len(first))])}

    def weights_of(l):
        def weights(grp, after):
            if (l, grp) not in ready:
                send_sems, recv_sems, bufs, plans = flying[l, grp]
                ready[l, grp] = _exchange_wait("gather_l%d_%s_wait" % (l, grp), send_sems, recv_sems, bufs, plans, after)
            return _full_weights(grp, ready[l, grp])
        return weights

    small = {n: d[n] for n in _SMALL_REPL}
    x1, sv0 = _forward_layer(x, weights_of(0), _layer_params(small, 0), tiles)
    x2, sv1 = _forward_layer(x1, weights_of(1), _layer_params(small, 1), tiles)
    dx, lcols = _loss_grad(x2, target, tiles[0])

    pipe = _ReducePipeline(c_idx, cp_idx)
    sgrads = [None] * DEPTH
    for l, sv in ((1, sv1), (0, sv0)):
        bufs = _grad_buffers()

        def stage(done, dx, l=l):
            pipe.add(l, done)
            return pipe.hook(dx)

        dx, sgrads[l] = _backward_layer(dx, sv, bufs, tiles, stage)
    grad_x = dx

    stacked = {n: jnp.stack([sgrads[l][n].reshape(d[n].shape[1:]) for l in range(DEPTH)]) for n in _SMALL_REPL}
    for n in _SMALL_SHARDED:
        stacked[n] = jnp.stack([sgrads[l][n] for l in range(DEPTH)])
    loss_part = jnp.pad((0.5 / D) * jnp.sum(lcols).reshape(1), (0, 127))
    order = _SMALL_REPL + _SMALL_SHARDED
    small_gather = _SmallGather(_pack([loss_part] + [stacked[n] for n in order]))
    pipe.reducers.append(small_gather)

    results = {n: tuple(lax.empty(d[n].shape, F32) for _ in range(4)) for n in _BIG}
    applied = set()

    def apply_ready(deps, last):
        for (l, n), g in pipe.available().items():
            if (l, n) not in applied:
                results[n] = _adamw_layer(d[n], g, d["m_" + n], d["v_" + n], l, results[n], deps)
                applied.add((l, n))
                last = results[n][1]
                deps = [last]
        return last

    last = apply_ready(pipe.hook(grad_x), grad_x)
    token = pipe.hook(last)
    summed = _unpack(_sum_small(small_gather.buf, small_gather.gathered), [(128,)] + [stacked[n].shape for n in order])
    loss = summed[0][0]
    grads = {}
    for n, g in zip(order, summed[1:]):
        if n in _SMALL_SHARDED:
            g = lax.dynamic_slice_in_dim(g, p * (g.shape[2] // NSHARD), g.shape[2] // NSHARD, axis=2)
        grads[n] = g
    delta, new_m, new_v = {}, {}, {}
    small_out = _adamw_small([d[n] for n in order], [grads[n] for n in order], [d["m_" + n] for n in order],
                             [d["v_" + n] for n in order], token)
    for out, res in zip((delta, new_m, new_v), small_out):
        out.update(zip(order, res))
    last = apply_ready([small_out[0][0]], small_out[0][0])
    pipe.finish(last)
    apply_ready((), last)
    for n in _BIG:
        grads[n], delta[n], new_m[n], new_v[n] = results[n]

    return (loss, grad_x[None], *[grads[n] for n in _WEIGHTS], *[delta[n] for n in _WEIGHTS],
            *[new_m[n] for n in _WEIGHTS], *[new_v[n] for n in _WEIGHTS])
```

```python
import math

import jax
import jax.numpy as jnp
import numpy as np
from jax import lax
from jax.experimental import pallas as pl
from jax.experimental.pallas import tpu as pltpu

F32 = jnp.float32
BF16 = jnp.bfloat16
SDS = jax.ShapeDtypeStruct

D = 1024
DFF = 2816
FH = DFF // 2
DEPTH = 2
W_A = 256
W_B = 512
W_C = 256
NQ = 8
HD = 64
BLK = 128
ATT_NB_FWD = 1
ATT_NB_BWD = 8
P_IN = 1792
LRU_K = 4
CONV_K = 31
LRU_C = 8.0
NORM_EPS = 1e-6
LN_EPS = 1e-5
NEG_BIG = -1e30
SCALE = 1.0 / math.sqrt(HD)

ADAM_LR = 0.001
ADAM_B1 = 0.9
ADAM_B2 = 0.999
ADAM_EPS = 1e-08
ADAM_WD = 0.01
ADAM_STEP = 10

VMEM_LIMIT = 60 * 1024 * 1024
NSHARD = 4
NDEV = 8

TN = (((0,), (0,)), ((), ()))
NT = (((1,), (1,)), ((), ()))

MESH = pl.DeviceIdType.MESH
ANY = pl.BlockSpec(memory_space=pl.ANY)


def _cp(*sem):
    return pltpu.CompilerParams(dimension_semantics=sem if sem else None, vmem_limit_bytes=VMEM_LIMIT)


def _rsq(x, eps):
    return lax.rsqrt(jnp.mean(x * x, axis=-1, keepdims=True) + eps)


def _rms_bwd_rows(x, g, dy):
    r = _rsq(x, NORM_EPS)
    xh = x * r
    dyg = dy * g
    dx = r * (dyg - xh * jnp.mean(dyg * xh, axis=-1, keepdims=True))
    return dx, dy * xh


def _sig(x):
    return jax.nn.sigmoid(x)


def _post_norm_tail(dx, c, z_ref, g_ref, dz_ref, dg_ref, first):
    dz, dgr = _rms_bwd_rows(z_ref[...], g_ref[...], c * dx)
    dz_ref[...] = dz.astype(BF16)
    _acc_rows(dg_ref, first, dgr)


def _ffn_up(x, pre_g, wgu, l, tm, deps=()):
    s = x.shape[0]
    deps = list(deps)

    def body(x_ref, g_ref, wg_ref, wu_ref, *rest):
        h_ref, go_ref, uo_ref, a_ref = rest[len(deps):]

        @pl.when(pl.program_id(1) == 0)
        def _():
            xf = x_ref[...]
            h_ref[...] = (xf * _rsq(xf, NORM_EPS) * g_ref[...]).astype(BF16)

        h = h_ref[...]
        gg = jnp.dot(h, wg_ref[...], preferred_element_type=F32)
        uu = jnp.dot(h, wu_ref[...], preferred_element_type=F32)
        sg = _sig(gg)
        silu = gg * sg
        go_ref[...] = (uu * (sg * (1.0 + gg * (1.0 - sg)))).astype(BF16)
        uo_ref[...] = silu.astype(BF16)
        a_ref[...] = (silu * uu).astype(BF16)

    wide = pl.BlockSpec((tm, FH), lambda i, j: (i, j))
    return pl.pallas_call(
        body, name="ffn_up", grid=(s // tm, 2),
        in_specs=[pl.BlockSpec((tm, D), lambda i, j: (i, 0)), pl.BlockSpec((1, D), lambda i, j: (0, 0)),
                  pl.BlockSpec((None, None, D, FH), lambda i, j: (l, j, 0, 0)),
                  pl.BlockSpec((None, None, D, FH), lambda i, j: (l, j + 2, 0, 0))] + [ANY] * len(deps),
        out_specs=[pl.BlockSpec((tm, D), lambda i, j: (i, 0)), wide, wide, wide],
        out_shape=[SDS((s, D), BF16), SDS((s, DFF), BF16), SDS((s, DFF), BF16), SDS((s, DFF), BF16)],
        compiler_params=_cp("parallel", "arbitrary"),
    )(x, pre_g, wgu, wgu, *deps)


def _mm_rms_res(a, w, l, x, g, c, tm, tk, name):
    s, k_dim = a.shape
    nk = k_dim // tk

    def body(a_ref, w_ref, x_ref, g_ref, z_ref, x1_ref):
        k = pl.program_id(1)
        p = jnp.dot(a_ref[...], w_ref[...], preferred_element_type=F32)

        @pl.when(k == 0)
        def _():
            z_ref[...] = p

        @pl.when(k > 0)
        def _():
            z_ref[...] += p

        @pl.when(k == nk - 1)
        def _():
            z = z_ref[...]
            x1_ref[...] = x_ref[...] + c * (z * _rsq(z, NORM_EPS) * g_ref[...])

    row = pl.BlockSpec((tm, D), lambda i, k: (i, 0))
    return pl.pallas_call(
        body, name=name, grid=(s // tm, nk),
        in_specs=[pl.BlockSpec((tm, tk), lambda i, k: (i, k)), pl.BlockSpec((None, tk, D), lambda i, k: (l, k, 0)),
                  row, pl.BlockSpec((1, D), lambda i, k: (0, 0))],
        out_specs=[row, row],
        out_shape=[SDS((s, D), F32), SDS((s, D), F32)],
        compiler_params=_cp("parallel", "arbitrary"),
    )(a, w, x, g)


def _ffn_bwd_mid(dz, wd, l, dadg, dadu, tm, deps=()):
    s = dz.shape[0]
    deps = list(deps)

    def body(dz_ref, wd_ref, g_ref, u_ref, *rest):
        dgu_ref = rest[len(deps)]
        da = lax.dot_general(dz_ref[...], wd_ref[...], NT, preferred_element_type=F32)
        dgu_ref[:, 0:FH] = (da * g_ref[...].astype(F32)).astype(BF16)
        dgu_ref[:, FH:2 * FH] = (da * u_ref[...].astype(F32)).astype(BF16)

    wide = pl.BlockSpec((tm, FH), lambda i, j: (i, j))
    return pl.pallas_call(
        body, name="ffn_bwd_mid", grid=(s // tm, 2),
        in_specs=[pl.BlockSpec((tm, D), lambda i, j: (i, 0)), pl.BlockSpec((None, FH, D), lambda i, j: (l, j, 0)), wide, wide]
        + [ANY] * len(deps),
        out_specs=pl.BlockSpec((tm, 2 * FH), lambda i, j: (i, j)),
        out_shape=SDS((s, 2 * DFF), BF16),
        compiler_params=_cp("parallel", "arbitrary"),
    )(dz, wd, dadg, dadu, *deps)


def _ffn_bwd_dh(dgu, wgu, l, x, pre_g, dx1, tm, deps=(), below=None):
    s = x.shape[0]
    deps = list(deps)
    tail = [] if below is None else list(below[:2])

    def body(dgu_ref, w_hbm, x_ref, g_ref, dx1_ref, *rest):
        rest = rest[len(deps):]
        tail_in, (dx_ref, dgp_ref), rest = rest[:len(tail)], rest[len(tail):len(tail) + 2], rest[len(tail) + 2:]
        tail_out, (wcat_ref, sems) = rest[:len(tail)], rest[len(tail):]
        i = pl.program_id(0)

        @pl.when(i == 0)
        def _():
            cps = [pltpu.make_async_copy(w_hbm.at[l, q], wcat_ref.at[:, pl.ds((2 * (q % 2) + q // 2) * FH, FH)], sems.at[q])
                   for q in range(NSHARD)]
            for cp in cps:
                cp.start()
            for cp in cps:
                cp.wait()

        dh = lax.dot_general(dgu_ref[...], wcat_ref[...], NT, preferred_element_type=F32)
        dx, dgr = _rms_bwd_rows(x_ref[...], g_ref[...], dh)
        dx = dx1_ref[...] + dx
        dx_ref[...] = dx
        _acc_rows(dgp_ref, i == 0, dgr)
        if tail:
            _post_norm_tail(dx, below[2], *tail_in, *tail_out, i == 0)

    row = pl.BlockSpec((tm, D), lambda i: (i, 0))
    vec = pl.BlockSpec((1, D), lambda i: (0, 0))
    return pl.pallas_call(
        body, name="ffn_bwd_dh", grid=(s // tm,),
        in_specs=[pl.BlockSpec((tm, 2 * DFF), lambda i: (i, 0)), ANY, row, vec, row] + [ANY] * len(deps) + [row, vec][:len(tail)],
        out_specs=[row, vec] + [row, vec][:len(tail)],
        out_shape=[SDS((s, D), F32), SDS((1, D), F32)] + [SDS((s, D), BF16), SDS((1, D), F32)][:len(tail)],
        scratch_shapes=[pltpu.VMEM((D, 2 * DFF), BF16), pltpu.SemaphoreType.DMA((NSHARD,))],
        compiler_params=_cp("arbitrary"),
    )(dgu, wgu, x, pre_g, dx1, *deps, *tail)


def _mm_tn_into(buf, a, b, l, joff, tka, tn, ts, name, bstride=1, boff=0):
    s, ka = a.shape
    n = b.shape[1] // bstride

    def body(buf_ref, a_ref, b_ref, o_ref):
        p = lax.dot_general(a_ref[...], b_ref[...], TN, preferred_element_type=F32)

        @pl.when(pl.program_id(2) == 0)
        def _():
            o_ref[...] = p

        @pl.when(pl.program_id(2) > 0)
        def _():
            o_ref[...] += p

    return pl.pallas_call(
        body, name=name, grid=(ka // tka, n // tn, s // ts),
        in_specs=[pl.BlockSpec(memory_space=pl.ANY),
                  pl.BlockSpec((ts, tka), lambda ia, j, t: (t, ia)),
                  pl.BlockSpec((ts, tn), lambda ia, j, t: (t, bstride * j + boff))],
        out_specs=pl.BlockSpec((None, None, tka, tn), lambda ia, j, t: (l, joff + j, ia, 0)),
        out_shape=SDS(buf.shape, F32), input_output_aliases={0: 0},
        compiler_params=_cp("parallel", "parallel", "arbitrary"),
    )(buf, a, b)


def _proj(x, g, w_in, l, tm):
    s = x.shape[0]

    def body(x_ref, g_ref, w_ref, h_ref, p_ref):
        xf = x_ref[...]
        h = (xf * _rsq(xf, NORM_EPS) * g_ref[...]).astype(BF16)
        h_ref[...] = h
        p_ref[...] = jnp.dot(h, w_ref[...], preferred_element_type=F32)

    return pl.pallas_call(
        body, name="proj", grid=(s // tm,),
        in_specs=[pl.BlockSpec((tm, D), lambda i: (i, 0)), pl.BlockSpec((1, D), lambda i: (0, 0)),
                  pl.BlockSpec((None, D, P_IN), lambda i: (l, 0, 0))],
        out_specs=[pl.BlockSpec((tm, D), lambda i: (i, 0)), pl.BlockSpec((tm, P_IN), lambda i: (i, 0))],
        out_shape=[SDS((s, D), BF16), SDS((s, P_IN), F32)],
        compiler_params=_cp("parallel"),
    )(x, g, w_in)


def _mm_nt(a, w, l, tm, name):
    s, k_dim = a.shape
    n = w.shape[1]

    def body(a_ref, w_ref, o_ref):
        o_ref[...] = lax.dot_general(a_ref[...], w_ref[...], NT, preferred_element_type=F32)

    return pl.pallas_call(
        body, name=name, grid=(s // tm,),
        in_specs=[pl.BlockSpec((tm, k_dim), lambda i: (i, 0)), pl.BlockSpec((None, n, k_dim), lambda i: (l, 0, 0))],
        out_specs=pl.BlockSpec((tm, n), lambda i: (i, 0)),
        out_shape=SDS((s, n), F32), compiler_params=_cp("parallel"),
    )(a, w)


def _mm_nt_rmsbwd(dp, w_in, l, x, g, dx1, tm, below):
    s = x.shape[0]

    def body(dp_ref, w_ref, x_ref, g_ref, dx1_ref, zb_ref, gb_ref, dx_ref, dg_ref, dzb_ref, dgb_ref):
        first = pl.program_id(0) == 0
        dh = lax.dot_general(dp_ref[...], w_ref[...], NT, preferred_element_type=F32)
        dx, dgr = _rms_bwd_rows(x_ref[...], g_ref[...], dh)
        dx = dx1_ref[...] + dx
        dx_ref[...] = dx
        _acc_rows(dg_ref, first, dgr)
        _post_norm_tail(dx, below[2], zb_ref, gb_ref, dzb_ref, dgb_ref, first)

    row = pl.BlockSpec((tm, D), lambda i: (i, 0))
    vec = pl.BlockSpec((1, D), lambda i: (0, 0))
    return pl.pallas_call(
        body, name="mix_bwd_dx", grid=(s // tm,),
        in_specs=[pl.BlockSpec((tm, P_IN), lambda i: (i, 0)), pl.BlockSpec((None, D, P_IN), lambda i: (l, 0, 0)), row, vec, row,
                  row, vec],
        out_specs=[row, vec, row, vec],
        out_shape=[SDS((s, D), F32), SDS((1, D), F32), SDS((s, D), BF16), SDS((1, D), F32)],
        compiler_params=_cp("arbitrary"),
    )(dp, w_in, x, g, dx1, below[0], below[1])


def _row_iota(shape):
    return lax.broadcasted_iota(jnp.int32, shape, 0)


def _lru_gates(xc, wa_ref, ba_ref, wx_ref, bx_ref, lam_ref):
    xb = xc.astype(BF16)
    r = _sig(jnp.dot(xb, wa_ref[...], preferred_element_type=F32) + ba_ref[...])
    ig = _sig(jnp.dot(xb, wx_ref[...], preferred_element_type=F32) + bx_ref[...])
    nl = -lam_ref[...]
    sp = jnp.maximum(nl, 0.0) + jnp.log(1.0 + jnp.exp(-jnp.abs(nl)))
    log_a = -LRU_C * r * sp
    a = jnp.exp(log_a)
    mlt = jnp.sqrt((1.0 + a * a) * jnp.tanh(-log_a))
    return r, ig, a, mlt, sp


def _conv_taps(src_ref, w_ref, k_taps, pad, tc):
    acc = None
    for j in range(k_taps):
        term = w_ref[j:j + 1, :] * src_ref[pl.ds(pad - (k_taps - 1) + j, tc), :]
        acc = term if acc is None else acc + term
    return acc


def _fill_shifted(src_ref, sh_ref):
    n = src_ref.shape[0] - 8
    for s in range(1, 8):
        sh_ref[s, 0:n, :] = src_ref[pl.ds(s, n), :]


def _shifted_rows(src_ref, sh_ref, offset, tc):
    if offset % 8 == 0:
        return src_ref[pl.ds(offset, tc), :]
    return sh_ref[offset % 8, pl.ds(offset - offset % 8, tc), :]


def _gelu_parts(x):
    c0 = math.sqrt(2.0 / math.pi)
    inner = c0 * (x + 0.044715 * x * x * x)
    t = jnp.tanh(inner)
    gl = 0.5 * x * (1.0 + t)
    dgl = 0.5 * (1.0 + t) + 0.5 * x * (1.0 - t * t) * c0 * (1.0 + 3.0 * 0.044715 * x * x)
    return gl, dgl


def _lru_fwd(proj, cw, cb, wa, ba, wx, bx, lam, gg, tc):
    s = proj.shape[0]
    pad = 8

    def body(xcur_ref, xprev_ref, gate_ref, cw_ref, cb_ref, wa_ref, ba_ref, wx_ref, bx_ref, lam_ref, gg_ref,
             yn_ref, h_ref, xs_ref, hc_ref):
        i = pl.program_id(0)

        @pl.when(i == 0)
        def _():
            hc_ref[...] = jnp.zeros_like(hc_ref)

        xs_ref[0:pad, :] = jnp.where(i > 0, xprev_ref[tc - pad:tc, :], 0.0)
        xs_ref[pad:pad + tc, :] = xcur_ref[...]
        xc = _conv_taps(xs_ref, cw_ref, LRU_K, pad, tc) + cb_ref[...]
        _, ig, a, mlt, _ = _lru_gates(xc, wa_ref, ba_ref, wx_ref, bx_ref, lam_ref)
        u = mlt * (ig * xc)
        row = _row_iota((tc, W_A))
        d = 1
        while d < tc:
            ok = row >= d
            a_sh = jnp.where(ok, pltpu.roll(a, d, axis=0), 1.0)
            u_sh = jnp.where(ok, pltpu.roll(u, d, axis=0), 0.0)
            u = a * u_sh + u
            a = a * a_sh
            d *= 2
        h = u + a * hc_ref[...]
        hc_ref[...] = jnp.sum(jnp.where(row == tc - 1, h, 0.0), axis=0, keepdims=True)
        h_ref[...] = h
        gl, _ = _gelu_parts(gate_ref[...])
        ya = gl * h
        yn_ref[...] = (ya * _rsq(ya, NORM_EPS) * gg_ref[...]).astype(BF16)

    blk = lambda c: pl.BlockSpec((tc, W_A), lambda i, c=c: (i, c))
    full = lambda a: pl.BlockSpec(a.shape, lambda i: (0,) * a.ndim)
    params = [cw, cb, wa, ba, wx, bx, lam, gg]
    return pl.pallas_call(
        body, name="lru_fwd", grid=(s // tc,),
        in_specs=[blk(0), pl.BlockSpec((tc, W_A), lambda i: (jnp.maximum(i - 1, 0), 0)), blk(1)] + [full(a) for a in params],
        out_specs=[pl.BlockSpec((tc, W_A), lambda i: (i, 0))] * 2,
        out_shape=[SDS((s, W_A), BF16), SDS((s, W_A), F32)],
        scratch_shapes=[pltpu.VMEM((tc + pad, W_A), F32), pltpu.VMEM((1, W_A), F32)],
        compiler_params=_cp("arbitrary"),
    )(proj, proj, proj, *params)


def _acc_rows(ref, first, rows):
    _acc(ref, first, jnp.sum(rows, axis=0, keepdims=True))


def _acc(ref, first, val):
    @pl.when(first)
    def _():
        ref[...] = val

    @pl.when(jnp.logical_not(first))
    def _():
        ref[...] += val


def _lru_bwd(dy, proj, h, cw, cb, wa, ba, wx, bx, lam, gg, tc):
    s = proj.shape[0]
    nc = s // tc
    pad = 8

    def body(dy_ref, xcur_ref, xprev_ref, gate_ref, h_ref, hprev_ref, cw_ref, cb_ref, wa_ref, ba_ref, wx_ref, bx_ref,
             lam_ref, gg_ref,
             dp_ref, dcw_ref, dcb_ref, dwa_ref, dba_ref, dwx_ref, dbx_ref, dlam_ref, dgg_ref,
             xs_ref, ds_ref, mu_ref, nx_ref):
        step = pl.program_id(0)
        i = nc - 1 - step
        first = step == 0

        @pl.when(first)
        def _():
            mu_ref[...] = jnp.zeros_like(mu_ref)
            nx_ref[...] = jnp.zeros_like(nx_ref)

        xs_ref[0:pad, :] = jnp.where(i > 0, xprev_ref[tc - pad:tc, :], 0.0)
        xs_ref[pad:pad + tc, :] = xcur_ref[...]
        xc = _conv_taps(xs_ref, cw_ref, LRU_K, pad, tc) + cb_ref[...]
        r, ig, a, mlt, sp = _lru_gates(xc, wa_ref, ba_ref, wx_ref, bx_ref, lam_ref)
        hh = h_ref[...]
        gate = gate_ref[...]
        gl, dgl = _gelu_parts(gate)
        ya = gl * hh
        dya, dggr = _rms_bwd_rows(ya, gg_ref[...], dy_ref[...])
        _acc(dgg_ref, first, jnp.sum(dggr, axis=0, keepdims=True))
        dp_ref[:, W_A:2 * W_A] = dya * hh * dgl
        dh = dya * gl

        row = _row_iota((tc, W_A))
        aa = a
        uu = a * dh
        d = 1
        while d < tc:
            ok = row < tc - d
            a_sh = jnp.where(ok, pltpu.roll(aa, tc - d, axis=0), 1.0)
            u_sh = jnp.where(ok, pltpu.roll(uu, tc - d, axis=0), 0.0)
            uu = uu + aa * u_sh
            aa = aa * a_sh
            d *= 2
        cin = mu_ref[...]
        mu = uu + aa * cin
        lam_t = dh + jnp.where(row == tc - 1, cin, pltpu.roll(mu, tc - 1, axis=0))
        mu_ref[...] = jnp.sum(jnp.where(row == 0, mu, 0.0), axis=0, keepdims=True)
        hm1 = jnp.where(row == 0, jnp.where(i > 0, pltpu.roll(hprev_ref[...], 1, axis=0), 0.0),
                        pltpu.roll(hh, 1, axis=0))
        da = lam_t * hm1
        du = lam_t
        dmlt = du * ig * xc
        dig = du * mlt * xc
        dxc = du * mlt * ig
        dlog_a = da * a - dmlt * (a * a / mlt)
        dr = dlog_a * (-LRU_C * sp)
        dsp = jnp.sum(dlog_a * (-LRU_C * r), axis=0, keepdims=True)
        _acc(dlam_ref, first, dsp * (-_sig(-lam_ref[...])))
        dga = dr * r * (1.0 - r)
        dgx = dig * ig * (1.0 - ig)
        _acc(dba_ref, first, jnp.sum(dga, axis=0, keepdims=True))
        _acc(dbx_ref, first, jnp.sum(dgx, axis=0, keepdims=True))
        xb = xc.astype(BF16)
        dgab = dga.astype(BF16)
        dgxb = dgx.astype(BF16)
        _acc(dwa_ref, first, lax.dot_general(xb, dgab, TN, preferred_element_type=F32))
        _acc(dwx_ref, first, lax.dot_general(xb, dgxb, TN, preferred_element_type=F32))
        dxc = (dxc + lax.dot_general(dgab, wa_ref[...], NT, preferred_element_type=F32)
               + lax.dot_general(dgxb, wx_ref[...], NT, preferred_element_type=F32))

        _acc(dcb_ref, first, jnp.sum(dxc, axis=0, keepdims=True))
        r8 = _row_iota((8, W_A))
        dcw = jnp.zeros((8, W_A), F32)
        for j in range(LRU_K):
            tap = jnp.sum(dxc * xs_ref[pl.ds(pad - (LRU_K - 1) + j, tc), :], axis=0, keepdims=True)
            dcw = dcw + jnp.where(r8 == j, tap, 0.0)
        _acc(dcw_ref, first, dcw)
        ds_ref[0:tc, :] = dxc
        ds_ref[tc:tc + pad, :] = nx_ref[...]
        dlx = None
        for j in range(LRU_K):
            term = cw_ref[j:j + 1, :] * ds_ref[pl.ds(LRU_K - 1 - j, tc), :]
            dlx = term if dlx is None else dlx + term
        dp_ref[:, 0:W_A] = dlx
        nx_ref[...] = dxc[0:pad, :]

    rev = lambda c: pl.BlockSpec((tc, W_A), lambda t, c=c: (nc - 1 - t, c))
    prev = lambda c: pl.BlockSpec((tc, W_A), lambda t, c=c: (jnp.maximum(nc - 2 - t, 0), c))
    full = lambda a: pl.BlockSpec(a.shape, lambda t: (0,) * a.ndim)
    params = [cw, cb, wa, ba, wx, bx, lam, gg]
    vec = SDS((1, W_A), F32)
    sq = SDS((W_A, W_A), F32)
    outs = [SDS((s, 2 * W_A), F32), SDS((8, W_A), F32), vec, sq, vec, sq, vec, vec, vec]
    return pl.pallas_call(
        body, name="lru_bwd", grid=(nc,),
        in_specs=[rev(0), rev(0), prev(0), rev(1), rev(0), prev(0)] + [full(a) for a in params],
        out_specs=[pl.BlockSpec((tc, 2 * W_A), lambda t: (nc - 1 - t, 0))]
        + [pl.BlockSpec(o.shape, lambda t: (0, 0)) for o in outs[1:]],
        out_shape=outs,
        scratch_shapes=[pltpu.VMEM((tc + pad, W_A), F32), pltpu.VMEM((tc + pad, W_A), F32),
                        pltpu.VMEM((1, W_A), F32), pltpu.VMEM((pad, W_A), F32)],
        compiler_params=_cp("arbitrary"),
    )(dy, proj, proj, proj, h, h, *params)


def _attn_stack(qa, qb, kvh):
    lane = lax.broadcasted_iota(jnp.int32, qa.shape, 1)
    keep = (lane >= HD) if kvh == 1 else (lane < HD)
    parts = []
    for tile in (qa, qb):
        for half in (0, 1):
            y = tile if half == kvh else pltpu.roll(tile, HD, axis=1)
            parts.append(jnp.where(keep, y, 0.0))
    return jnp.concatenate(parts, axis=0)


def _attn_unstack(o, kvh):
    lane = lax.broadcasted_iota(jnp.int32, (BLK, 2 * HD), 1)
    tiles = []
    for t in range(2):
        halves = []
        for half in (0, 1):
            blk = o[(2 * t + half) * BLK:(2 * t + half + 1) * BLK, :]
            halves.append(blk if half == kvh else pltpu.roll(blk, HD, axis=1))
        tiles.append(jnp.where(lane < HD, halves[0], halves[1]))
    return tiles


def _attn_stack_all(x_ref_or_val):
    return jnp.concatenate([_attn_stack(x_ref_or_val[:, 256 * kvh:256 * kvh + 128],
                                        x_ref_or_val[:, 256 * kvh + 128:256 * kvh + 256], kvh) for kvh in range(2)], axis=0)


def _attn_unstack_all(o, dst_ref):
    for kvh in range(2):
        ta, tb = _attn_unstack(o[4 * BLK * kvh:4 * BLK * (kvh + 1), :], kvh)
        dst_ref[:, 256 * kvh:256 * kvh + 128] = ta
        dst_ref[:, 256 * kvh + 128:256 * kvh + 256] = tb


def _attn_windows(cur_ref, prev_ref, nb):
    blocks = [prev_ref[...]] + [cur_ref[b * BLK:(b + 1) * BLK, :] for b in range(nb)]
    return [jnp.concatenate(blocks[b:b + 2], axis=0).astype(BF16) for b in range(nb)]


def _attn_bias():
    qi = np.arange(NQ * BLK)[:, None] % BLK
    kj = np.arange(2 * BLK)[None, :]
    rel = BLK + qi - kj
    ok = (rel >= 0) & (rel < BLK)
    return jnp.asarray(np.stack([np.where(ok & (kj >= BLK), 0.0, NEG_BIG), np.where(ok, 0.0, NEG_BIG)]), F32)


def _attn_probs(qs, kw, first, sink_ref, bias_ref):
    rows = NQ * BLK
    bias = bias_ref[1] if first is False else jnp.where(first, bias_ref[0], bias_ref[1])
    sh = lax.dot_general(qs.astype(BF16), kw, NT, preferred_element_type=F32) * SCALE + bias
    head = lax.broadcasted_iota(jnp.int32, (rows, 1), 0) // BLK
    sk = jnp.zeros((rows, 1), F32)
    for h in range(NQ):
        sk = jnp.where(head == h, sink_ref[h:h + 1, 0:1], sk)
    m = jnp.maximum(jnp.max(sh, axis=-1, keepdims=True), sk)
    e = jnp.exp(sh - m)
    es = jnp.exp(sk - m)
    rz = 1.0 / (jnp.sum(e, axis=-1, keepdims=True) + es)
    return e * rz, es * rz


def _attn_fwd(proj, sinks8, gg):
    s = proj.shape[0]
    nb = ATT_NB_FWD

    def body(q_ref, kc_ref, kp_ref, vc_ref, vp_ref, sink_ref, gg_ref, bias_ref, yn_ref, ob_ref):
        kws, vws = _attn_windows(kc_ref, kp_ref, nb), _attn_windows(vc_ref, vp_ref, nb)
        for b in range(nb):
            rows = pl.ds(b * BLK, BLK)
            first = (pl.program_id(0) == 0) if b == 0 else False
            p, _ = _attn_probs(_attn_stack_all(q_ref.at[rows, :]), kws[b], first, sink_ref, bias_ref)
            _attn_unstack_all(jnp.dot(p.astype(BF16), vws[b], preferred_element_type=F32), ob_ref.at[rows, :])
        ob = ob_ref[...]
        yn_ref[...] = (ob * _rsq(ob, NORM_EPS) * gg_ref[...]).astype(BF16)

    tb = nb * BLK
    cur = lambda c: pl.BlockSpec((tb, 128), lambda m, c=c: (m, c))
    prev = lambda c: pl.BlockSpec((BLK, 128), lambda m, c=c: (jnp.maximum(nb * m - 1, 0), c))
    out = pl.BlockSpec((tb, W_B), lambda m: (m, 0))
    return pl.pallas_call(
        body, name="attn_fwd", grid=(s // tb,),
        in_specs=[pl.BlockSpec((tb, W_B), lambda m: (m, 1)), cur(8), prev(8), cur(9), prev(9),
                  pl.BlockSpec((8, 128), lambda n: (0, 0)), pl.BlockSpec((1, W_B), lambda n: (0, 0)),
                  pl.BlockSpec((2, NQ * BLK, 2 * BLK), lambda n: (0, 0, 0))],
        out_specs=[out, out], out_shape=[SDS((s, W_B), BF16), SDS((s, W_B), F32)],
        compiler_params=_cp("parallel"),
    )(proj, proj, proj, proj, proj, sinks8, gg, _attn_bias())


def _attn_bwd(dy, proj, ob, sinks8, gg):
    s = proj.shape[0]
    nb = ATT_NB_BWD

    def body(dya_ref, dyb_ref, q_ref, kc_ref, kp_ref, vc_ref, vp_ref, ob_ref, sink_ref, gg_ref, bias_ref,
             dq_ref, dcur_ref, dprev_ref, dsink_ref, dgg_ref):
        first = pl.program_id(0) == 0
        kws, vws = _attn_windows(kc_ref, kp_ref, nb), _attn_windows(vc_ref, vp_ref, nb)
        dyn = jnp.concatenate([dya_ref[...], dyb_ref[...]], axis=1)
        dob, dggr = _rms_bwd_rows(ob_ref[...], gg_ref[...], dyn)
        _acc(dgg_ref, first, jnp.sum(dggr, axis=0, keepdims=True))
        r8 = _row_iota((8, 128))
        dsk = jnp.zeros((8, 128), F32)
        for b in range(nb):
            rows = pl.ds(b * BLK, BLK)
            qs = _attn_stack_all(q_ref.at[rows, :])
            p, psink = _attn_probs(qs, kws[b], first if b == 0 else False, sink_ref, bias_ref)
            dosb = _attn_stack_all(dob[b * BLK:(b + 1) * BLK, :]).astype(BF16)
            dp = lax.dot_general(dosb, vws[b], NT, preferred_element_type=F32)
            dd = jnp.sum(p * dp, axis=-1, keepdims=True)
            dsb = (p * (dp - dd) * SCALE).astype(BF16)
            dsink_rows = -psink * dd
            for h in range(NQ):
                dsk = dsk + jnp.where(r8 == h, jnp.sum(dsink_rows[h * BLK:(h + 1) * BLK, :], axis=0, keepdims=True), 0.0)
            _attn_unstack_all(jnp.dot(dsb, kws[b], preferred_element_type=F32), dq_ref.at[rows, :])
            dkw = lax.dot_general(dsb, qs.astype(BF16), TN, preferred_element_type=F32)
            dvw = lax.dot_general(p.astype(BF16), dosb, TN, preferred_element_type=F32)
            dprev_ref[rows, 0:128] = dkw[0:BLK, :]
            dprev_ref[rows, 128:256] = dvw[0:BLK, :]
            dcur_ref[rows, 0:128] = dkw[BLK:2 * BLK, :]
            dcur_ref[rows, 128:256] = dvw[BLK:2 * BLK, :]
        _acc(dsink_ref, first, dsk)

    tb = nb * BLK
    cur = lambda c: pl.BlockSpec((tb, 128), lambda m, c=c: (m, c))
    prev = lambda c: pl.BlockSpec((BLK, 128), lambda m, c=c: (jnp.maximum(nb * m - 1, 0), c))
    wide = pl.BlockSpec((tb, W_B), lambda m: (m, 0))
    half = pl.BlockSpec((tb, 256), lambda m: (m, 0))
    return pl.pallas_call(
        body, name="attn_bwd", grid=(s // tb,),
        in_specs=[pl.BlockSpec((tb, 256), lambda m: (m, 1)), pl.BlockSpec((tb, 256), lambda m: (m, 2)),
                  pl.BlockSpec((tb, W_B), lambda m: (m, 1)), cur(8), prev(8), cur(9), prev(9), wide,
                  pl.BlockSpec((8, 128), lambda n: (0, 0)), pl.BlockSpec((1, W_B), lambda n: (0, 0)),
                  pl.BlockSpec((2, NQ * BLK, 2 * BLK), lambda n: (0, 0, 0))],
        out_specs=[wide, half, half, pl.BlockSpec((8, 128), lambda n: (0, 0)), pl.BlockSpec((1, W_B), lambda n: (0, 0))],
        out_shape=[SDS((s, W_B), F32), SDS((s, 256), F32), SDS((s, 256), F32), SDS((8, 128), F32), SDS((1, W_B), F32)],
        compiler_params=_cp("arbitrary"),
    )(dy, dy, proj, proj, proj, proj, proj, ob, sinks8, gg, _attn_bias())


def _ln_parts(y1, eps=LN_EPS):
    mu = jnp.mean(y1, axis=-1, keepdims=True)
    xc = y1 - mu
    rstd = lax.rsqrt(jnp.mean(xc * xc, axis=-1, keepdims=True) + eps)
    return xc * rstd, rstd


def _conf_fwd(proj, cw, cb, lg, lb, gg, tc):
    s = proj.shape[0]
    pad = 32

    def body(ac_ref, gc_ref, ap_ref, gp_ref, cw_ref, cb_ref, lg_ref, lb_ref, gg_ref, yn_ref, y1_ref, ys_ref, sh_ref):
        i = pl.program_id(0)
        tail = ap_ref[tc - pad:tc, :] * _sig(gp_ref[tc - pad:tc, :])
        ys_ref[0:pad, :] = jnp.where(i > 0, tail, 0.0)
        ys_ref[pad:pad + tc, :] = ac_ref[...] * _sig(gc_ref[...])
        _fill_shifted(ys_ref, sh_ref)
        y1 = cb_ref[...]
        for j in range(CONV_K):
            y1 = y1 + cw_ref[j:j + 1, :] * _shifted_rows(ys_ref, sh_ref, pad - (CONV_K - 1) + j, tc)
        y1_ref[...] = y1
        xh, _ = _ln_parts(y1)
        yl = xh * lg_ref[...] + lb_ref[...]
        yc = yl * _sig(yl)
        yn_ref[...] = (yc * _rsq(yc, NORM_EPS) * gg_ref[...]).astype(BF16)

    cur = lambda c: pl.BlockSpec((tc, W_C), lambda i, c=c: (i, c))
    prev = lambda c: pl.BlockSpec((tc, W_C), lambda i, c=c: (jnp.maximum(i - 1, 0), c))
    full = lambda a: pl.BlockSpec(a.shape, lambda i: (0,) * a.ndim)
    params = [cw, cb, lg, lb, gg]
    out = pl.BlockSpec((tc, W_C), lambda i: (i, 0))
    return pl.pallas_call(
        body, name="conf_fwd", grid=(s // tc,),
        in_specs=[cur(5), cur(6), prev(5), prev(6)] + [full(a) for a in params],
        out_specs=[out, out], out_shape=[SDS((s, W_C), BF16), SDS((s, W_C), F32)],
        scratch_shapes=[pltpu.VMEM((tc + pad, W_C), F32), pltpu.VMEM((8, tc + pad, W_C), F32)],
        compiler_params=_cp("parallel"),
    )(proj, proj, proj, proj, *params)


def _conf_bwd(dy, proj, y1, cw, cb, lg, lb, gg, tc):
    s = proj.shape[0]
    nc = s // tc
    pad = 32

    def body(dy_ref, ac_ref, gc_ref, ap_ref, gp_ref, y1_ref, cw_ref, cb_ref, lg_ref, lb_ref, gg_ref,
             dp_ref, dcw_ref, dcb_ref, dlg_ref, dlb_ref, dgg_ref, ys_ref, ds_ref, nx_ref, ysh_ref, dsh_ref):
        step = pl.program_id(0)
        i = nc - 1 - step
        first = step == 0

        @pl.when(first)
        def _():
            nx_ref[...] = jnp.zeros_like(nx_ref)

        a = ac_ref[...]
        sg = _sig(gc_ref[...])
        tail = ap_ref[tc - pad:tc, :] * _sig(gp_ref[tc - pad:tc, :])
        ys_ref[0:pad, :] = jnp.where(i > 0, tail, 0.0)
        ys_ref[pad:pad + tc, :] = a * sg
        xh, rstd = _ln_parts(y1_ref[...])
        yl = xh * lg_ref[...] + lb_ref[...]
        sl = _sig(yl)
        yc = yl * sl
        dyc, dggr = _rms_bwd_rows(yc, gg_ref[...], dy_ref[...])
        _acc(dgg_ref, first, jnp.sum(dggr, axis=0, keepdims=True))
        dyl = dyc * sl * (1.0 + yl * (1.0 - sl))
        _acc(dlg_ref, first, jnp.sum(dyl * xh, axis=0, keepdims=True))
        _acc(dlb_ref, first, jnp.sum(dyl, axis=0, keepdims=True))
        dxh = dyl * lg_ref[...]
        dy1 = rstd * (dxh - jnp.mean(dxh, axis=-1, keepdims=True) - xh * jnp.mean(dxh * xh, axis=-1, keepdims=True))
        _acc(dcb_ref, first, jnp.sum(dy1, axis=0, keepdims=True))
        r32 = _row_iota((32, W_C))
        dcw = jnp.zeros((32, W_C), F32)
        _fill_shifted(ys_ref, ysh_ref)
        for j in range(CONV_K):
            tap = jnp.sum(dy1 * _shifted_rows(ys_ref, ysh_ref, pad - (CONV_K - 1) + j, tc), axis=0, keepdims=True)
            dcw = dcw + jnp.where(r32 == j, tap, 0.0)
        _acc(dcw_ref, first, dcw)
        ds_ref[0:tc, :] = dy1
        ds_ref[tc:tc + pad, :] = nx_ref[...]
        _fill_shifted(ds_ref, dsh_ref)
        dy0 = None
        for j in range(CONV_K):
            term = cw_ref[j:j + 1, :] * _shifted_rows(ds_ref, dsh_ref, CONV_K - 1 - j, tc)
            dy0 = term if dy0 is None else dy0 + term
        dp_ref[:, 0:W_C] = dy0 * sg
        dp_ref[:, W_C:2 * W_C] = dy0 * a * sg * (1.0 - sg)
        nx_ref[...] = dy1[0:pad, :]

    rev = lambda c: pl.BlockSpec((tc, W_C), lambda t, c=c: (nc - 1 - t, c))
    prev = lambda c: pl.BlockSpec((tc, W_C), lambda t, c=c: (jnp.maximum(nc - 2 - t, 0), c))
    full = lambda a: pl.BlockSpec(a.shape, lambda t: (0,) * a.ndim)
    params = [cw, cb, lg, lb, gg]
    vec = SDS((1, W_C), F32)
    outs = [SDS((s, 2 * W_C), F32), SDS((32, W_C), F32), vec, vec, vec, vec]
    return pl.pallas_call(
        body, name="conf_bwd", grid=(nc,),
        in_specs=[rev(3), rev(5), rev(6), prev(5), prev(6), rev(0)] + [full(a) for a in params],
        out_specs=[pl.BlockSpec((tc, 2 * W_C), lambda t: (nc - 1 - t, 0))]
        + [pl.BlockSpec(o.shape, lambda t: (0, 0)) for o in outs[1:]],
        out_shape=outs,
        scratch_shapes=[pltpu.VMEM((tc + pad, W_C), F32), pltpu.VMEM((tc + pad, W_C), F32), pltpu.VMEM((pad, W_C), F32),
                        pltpu.VMEM((8, tc + pad, W_C), F32), pltpu.VMEM((8, tc + pad, W_C), F32)],
        compiler_params=_cp("arbitrary"),
    )(dy, proj, proj, proj, proj, y1, *params)


def _assemble_dproj(dlru, dq, dcur, dprev, dconf):
    s = dq.shape[0]
    nb = s // BLK

    def body(dl_ref, dq_ref, dc_ref, dn_ref, df_ref, o_ref):
        n = pl.program_id(0)
        o_ref[:, 0:512] = dl_ref[...].astype(BF16)
        o_ref[:, 512:1024] = dq_ref[...].astype(BF16)
        o_ref[:, 1024:1280] = (dc_ref[...] + jnp.where(n < nb - 1, dn_ref[...], 0.0)).astype(BF16)
        o_ref[:, 1280:1792] = df_ref[...].astype(BF16)

    wide = pl.BlockSpec((BLK, 512), lambda n: (n, 0))
    return pl.pallas_call(
        body, name="assemble_dproj", grid=(nb,),
        in_specs=[wide, wide, pl.BlockSpec((BLK, 256), lambda n: (n, 0)),
                  pl.BlockSpec((BLK, 256), lambda n: (jnp.minimum(n + 1, nb - 1), 0)), wide],
        out_specs=pl.BlockSpec((BLK, P_IN), lambda n: (n, 0)), out_shape=SDS((s, P_IN), BF16),
        compiler_params=_cp("parallel"),
    )(dlru, dq, dcur, dprev, dconf)


def _loss_grad(y, t, tm, below):
    s = y.shape[0]

    def body(y_ref, t_ref, zb_ref, gb_ref, dy_ref, l_ref, dzb_ref, dgb_ref):
        first = pl.program_id(0) == 0
        err = y_ref[...] - t_ref[...]
        dy = err * (1.0 / D)
        dy_ref[...] = dy
        _acc_rows(l_ref, first, err * err)
        _post_norm_tail(dy, below[2], zb_ref, gb_ref, dzb_ref, dgb_ref, first)

    row = pl.BlockSpec((tm, D), lambda i: (i, 0))
    vec = pl.BlockSpec((1, D), lambda i: (0, 0))
    return pl.pallas_call(
        body, name="loss_grad", grid=(s // tm,), in_specs=[row, row, row, vec],
        out_specs=[row, vec, row, vec],
        out_shape=[SDS((s, D), F32), SDS((1, D), F32), SDS((s, D), BF16), SDS((1, D), F32)], compiler_params=_cp("arbitrary"),
    )(y, t, below[0], below[1])


def _block_diag(w):
    rows = [jnp.concatenate([w[h] if k == h else jnp.zeros((64, 64), w.dtype) for k in range(4)], axis=1) for h in range(4)]
    return jnp.concatenate(rows, axis=0)


def _diag_blocks(m):
    return jnp.stack([m[64 * h:64 * (h + 1), 64 * h:64 * (h + 1)] for h in range(4)])


def _layer_params(small, l):
    v = lambda name: small[name][l].reshape(1, -1)
    gg = small["group_g"][l]
    return dict(
        ffn1_pre=v("ffn1_pre_g"), ffn1_post=v("ffn1_post_g"), mix_pre=v("mix_pre_g"), mix_post=v("mix_post_g"),
        ffn2_pre=v("ffn2_pre_g"), ffn2_post=v("ffn2_post_g"), lru_cb=v("lru_conv_b"),
        wa=_block_diag(small["lru_w_a"][l]).astype(BF16), ba=v("lru_b_a"),
        wx=_block_diag(small["lru_w_x"][l]).astype(BF16), bx=v("lru_b_x"), lam=v("lru_lambda"),
        sinks8=jnp.broadcast_to(small["attn_sinks"][l][:, None], (NQ, 128)),
        conv_b=v("conv_b"), ln_g=v("conv_ln_g"), ln_b=v("conv_ln_b"),
        gg_a=gg[0:W_A].reshape(1, -1), gg_b=gg[W_A:W_A + W_B].reshape(1, -1), gg_c=gg[W_A + W_B:].reshape(1, -1),
    )


def _forward_layer(x, weights, p, tiles, deps=()):
    _, mm, _, tc, _ = tiles
    big = dict(weights("ffn1_gu", x))
    p = dict(p)
    sv = dict(x0=x)
    h1, g1, u1, a1 = _ffn_up(x, p["ffn1_pre"], big["ffn1_w_gu"], 0, mm, deps)
    big.update(weights("ffn1_down", a1))
    z1, x = _mm_rms_res(a1, big["ffn1_w_down"], 0, x, p["ffn1_post"], 0.5, mm, DFF, "ffn_down")
    sv.update(h1=h1, g1=g1, u1=u1, a1=a1, z1=z1, x1=x)
    big.update(weights("mix", x))
    p.update(lru_cw=big.pop("lru_conv_w"), conv_w=big.pop("conv_w"))
    hn, proj = _proj(x, p["mix_pre"], big["w_in"], 0, mm)
    yn_a, hl = _lru_fwd(proj, p["lru_cw"], p["lru_cb"], p["wa"], p["ba"], p["wx"], p["bx"], p["lam"], p["gg_a"], tc)
    yn_b, ob = _attn_fwd(proj, p["sinks8"], p["gg_b"])
    yn_c, y1 = _conf_fwd(proj, p["conv_w"], p["conv_b"], p["ln_g"], p["ln_b"], p["gg_c"], tc)
    ycat = jnp.concatenate([yn_a, yn_b, yn_c], axis=1)
    zo, x = _mm_rms_res(ycat, big["w_out"], 0, x, p["mix_post"], 1.0, mm, D, "mix_out")
    sv.update(hn=hn, proj=proj, hl=hl, ob=ob, y1=y1, ycat=ycat, zo=zo, x2=x)
    big.update(weights("ffn2", x))
    h2, g2, u2, a2 = _ffn_up(x, p["ffn2_pre"], big["ffn2_w_gu"], 0, mm)
    z2, x = _mm_rms_res(a2, big["ffn2_w_down"], 0, x, p["ffn2_post"], 0.5, mm, DFF, "ffn_down")
    sv.update(h2=h2, g2=g2, u2=u2, a2=a2, z2=z2, p=p, big=big)
    return x, sv


def _grad_buffers():
    empty = lambda *shape: lax.empty(shape, F32)
    return dict(ffn1_w_gu=empty(1, NSHARD, D, FH), ffn2_w_gu=empty(1, NSHARD, D, FH), ffn1_w_down=empty(1, 1, DFF, D),
                ffn2_w_down=empty(1, 1, DFF, D), w_in=empty(1, 1, D, P_IN), w_out=empty(1, 1, D, D))


def _backward_layer(dx, dzp, sv, bufs, tiles, stage, below):
    p, big = sv["p"], sv["big"]
    tm, mm, dw, tc, dh_rows = tiles
    gr = {}

    def ffn_bwd(dx, dzp, which, xin, h, g, u, a, pre, deps, below):
        dz, gr[which + "_post_g"] = dzp
        dgu = _ffn_bwd_mid(dz, big[which + "_w_down"], 0, g, u, mm, deps)
        bufs[which + "_w_down"] = _mm_tn_into(bufs[which + "_w_down"], a, dz, 0, 0, FH, D, dw, "dw_down")
        bufs[which + "_w_gu"] = _mm_tn_into(bufs[which + "_w_gu"], h, dgu, 0, 0, D, FH, dw, "dw_gate", 2, 0)
        bufs[which + "_w_gu"] = _mm_tn_into(bufs[which + "_w_gu"], h, dgu, 0, 2, D, FH, dw, "dw_up", 2, 1)
        deps = stage({n: bufs[n] for n in (which + "_w_gu", which + "_w_down")}, bufs[which + "_w_gu"])
        out = _ffn_bwd_dh(dgu, big[which + "_w_gu"], 0, xin, pre, dx, dh_rows, deps, below)
        gr[which + "_pre_g"] = out[1]
        return out[0], (tuple(out[2:]) if below is not None else None)

    dx, (do, gr["mix_post_g"]) = ffn_bwd(dx, dzp, "ffn2", sv["x2"], sv["h2"], sv["g2"], sv["u2"], sv["a2"],
                                         p["ffn2_pre"], (), (sv["zo"], p["mix_post"], 1.0))
    bufs["w_out"] = _mm_tn_into(bufs["w_out"], sv["ycat"], do, 0, 0, D, D, dw, "dw_out")
    dy = _mm_nt(do, big["w_out"], 0, mm, "mix_dy")
    proj = sv["proj"]
    (dlru, dcw, gr["lru_conv_b"], dwa, gr["lru_b_a"], dwx, gr["lru_b_x"], gr["lru_lambda"], dgg_a) = _lru_bwd(
        dy, proj, sv["hl"], p["lru_cw"], p["lru_cb"], p["wa"], p["ba"], p["wx"], p["bx"], p["lam"], p["gg_a"], tc)
    dq, dcur, dprev, dsk, dgg_b = _attn_bwd(dy, proj, sv["ob"], p["sinks8"], p["gg_b"])
    dconf, dconvw, gr["conv_b"], gr["conv_ln_g"], gr["conv_ln_b"], dgg_c = _conf_bwd(
        dy, proj, sv["y1"], p["conv_w"], p["conv_b"], p["ln_g"], p["ln_b"], p["gg_c"], tc)
    dproj = _assemble_dproj(dlru, dq, dcur, dprev, dconf)
    bufs["w_in"] = _mm_tn_into(bufs["w_in"], sv["hn"], dproj, 0, 0, D, P_IN, dw, "dw_in")
    dx, gr["mix_pre_g"], dz1, dpost1 = _mm_nt_rmsbwd(dproj, big["w_in"], 0, sv["x1"], p["mix_pre"], dx, dh_rows,
                                                     (sv["z1"], p["ffn1_post"], 0.5))
    gr["lru_conv_w"] = dcw[0:LRU_K]
    gr["lru_w_a"] = _diag_blocks(dwa)
    gr["lru_w_x"] = _diag_blocks(dwx)
    gr["attn_sinks"] = dsk[:, 0]
    gr["conv_w"] = dconvw[0:CONV_K]
    gr["group_g"] = jnp.concatenate([dgg_a, dgg_b, dgg_c], axis=1)
    dx, dz_below = ffn_bwd(dx, (dz1, dpost1), "ffn1", sv["x0"], sv["h1"], sv["g1"], sv["u1"], sv["a1"], p["ffn1_pre"],
                           stage({n: bufs[n] for n in ("w_in", "w_out")}, dx), below)
    return dx, dz_below, gr


def _tiles(s):
    return min(1024, s), min(1024, s), min(2048, s), min(512, s // 2), min(512, s)


HBM_SPEC = pl.BlockSpec(memory_space=pltpu.HBM)
SEM_SPEC = pl.BlockSpec(memory_space=pltpu.SEMAPHORE)
EFFECT = pltpu.SideEffectType.DATAFLOW_SIDE_EFFECTING


def _place():
    x, y, c = lax.axis_index("x"), lax.axis_index("y"), lax.axis_index("c")
    return x, y, c, [(1 - x, y), (x, 1 - y), (1 - x, 1 - y)]


def _rcopy(src, dst, send_sems, recv_sems, k, to):
    return pltpu.make_async_remote_copy(src_ref=src, dst_ref=dst, send_sem=send_sems.at[k], recv_sem=recv_sems.at[k],
                                        device_id=to, device_id_type=MESH)


def _half(rows, which):
    return pl.ds(which * (rows // 2), rows // 2)


def _place_shard(w, l, p_idx, dtype, deps=()):
    _, rows, cols = w.shape
    tr = _rows_per_block(rows, cols, 16, SUM_BLOCK_ELEMS) if rows % 16 == 0 else rows
    deps = list(deps)

    def body(p_ref, buf_ref, w_ref, *rest):
        rest[len(deps)][...] = w_ref[...].astype(dtype)

    spec = pltpu.PrefetchScalarGridSpec(
        num_scalar_prefetch=1, grid=(rows // tr,),
        in_specs=[ANY, pl.BlockSpec((None, tr, cols), lambda i, pr: (l, i, 0))] + [ANY] * len(deps),
        out_specs=pl.BlockSpec((None, None, tr, cols), lambda i, pr: (0, pr[0], i, 0)))
    shape = (1, NSHARD, rows, cols)
    return pl.pallas_call(body, name="place_shard", grid_spec=spec, out_shape=SDS(shape, dtype),
                          input_output_aliases={1: 0}, compiler_params=_cp("parallel"),
                          )(p_idx, lax.empty(shape, dtype), w, *deps)


def _run_plans(plans, refs, send_sems, recv_sems):
    cps, b0, s0 = [], 0, 0
    for plan, nb, ns in plans:
        cps += plan(refs[b0:b0 + nb], send_sems, recv_sems, s0)
        b0, s0 = b0 + nb, s0 + ns
    return cps


def _exchange(name, bufs, plans):
    n = len(bufs)
    nsem = sum(ns for _, _, ns in plans)

    def body(*refs):
        cps = _run_plans(plans, refs[n:2 * n], refs[2 * n], refs[2 * n + 1])
        for cp in cps:
            cp.start()
        for cp in cps:
            cp.wait()

    return pl.pallas_call(
        body, name=name, in_specs=[ANY] * n, out_specs=[ANY] * n, out_shape=[SDS(b.shape, b.dtype) for b in bufs],
        input_output_aliases={a: a for a in range(n)},
        scratch_shapes=[pltpu.SemaphoreType.DMA((nsem,)), pltpu.SemaphoreType.DMA((nsem,))],
    )(*bufs)


def _exchange_start(name, bufs, plans, deps=()):
    n = len(bufs)
    nsem = sum(ns for _, _, ns in plans)
    deps = list(deps)
    first_out = n + len(deps)

    def body(*refs):
        for cp in _run_plans(plans, refs[:n], refs[first_out], refs[first_out + 1]):
            cp.start()
        token = refs[first_out + 2 + n]
        token[...] = jnp.zeros_like(token)

    outs = pl.pallas_call(
        body, name=name,
        out_shape=(pltpu.SemaphoreType.DMA((nsem,)), pltpu.SemaphoreType.DMA((nsem,)),
                   *[pltpu.HBM(b.shape, b.dtype) for b in bufs], SDS((8, 128), F32)),
        in_specs=[HBM_SPEC] * n + [ANY] * len(deps),
        out_specs=(SEM_SPEC, SEM_SPEC, *[HBM_SPEC] * n, pl.BlockSpec(memory_space=pltpu.VMEM)),
        input_output_aliases={a: 2 + a for a in range(n)},
        compiler_params=pltpu.CompilerParams(has_side_effects=EFFECT),
    )(*[pltpu.with_memory_space_constraint(b, pltpu.HBM) for b in bufs], *deps)
    return outs[0], outs[1], list(outs[2:2 + n]), outs[2 + n]


def _exchange_wait(name, send_sems, recv_sems, bufs, plans, after):
    n = len(bufs)

    def body(*refs):
        for cp in _run_plans(plans, refs[:n], refs[n], refs[n + 1]):
            cp.wait_send()
            cp.wait_recv()

    return pl.pallas_call(
        body, name=name, out_shape=[pltpu.HBM(b.shape, b.dtype) for b in bufs],
        in_specs=[HBM_SPEC] * n + [SEM_SPEC, SEM_SPEC, ANY], out_specs=[HBM_SPEC] * n,
        input_output_aliases={a: a for a in range(n)},
        compiler_params=pltpu.CompilerParams(has_side_effects=EFFECT),
    )(*bufs, send_sems, recv_sems, after)


def _plan_gather(refs, send_sems, recv_sems, base):
    x, y, c, chips = _place()
    p = 2 * x + y
    return [_rcopy(r.at[0, p], r.at[0, p], send_sems, recv_sems, base + 3 * a + j, (*chip, c))
            for a, r in enumerate(refs) for j, chip in enumerate(chips)]


def _plan_gather_half(refs, send_sems, recv_sems, base):
    x, y, c, chips = _place()
    p = 2 * x + y
    return [_rcopy(r.at[0, p, _half(r.shape[2], c)], r.at[0, p, _half(r.shape[2], c)], send_sems, recv_sems,
                   base + 3 * a + j, (*chip, c)) for a, r in enumerate(refs) for j, chip in enumerate(chips)]


def _plan_forward_half(refs, send_sems, recv_sems, base):
    x, y, c, chips = _place()
    cps = []
    for a, r in enumerate(refs):
        for j, chip in enumerate(chips):
            blk = r.at[0, 2 * chip[0] + chip[1], _half(r.shape[2], c)]
            cps.append(_rcopy(blk, blk, send_sems, recv_sems, base + 3 * a + j, (x, y, 1 - c)))
    return cps


def _plan_pair_exchange(refs, send_sems, recv_sems, base):
    x, y, c, _ = _place()
    n = len(refs) // 2
    return [_rcopy(refs[a].at[:, _half(refs[a].shape[1], 1 - c)], refs[n + a], send_sems, recv_sems, base + a,
                   (x, y, 1 - c)) for a in range(n)]


def _plan_chip_exchange(refs, send_sems, recv_sems, base):
    x, y, c, chips = _place()
    n = len(refs) // 2
    return [_rcopy(refs[a].at[2 * chip[0] + chip[1]], refs[n + a].at[j], send_sems, recv_sems, base + 3 * a + j,
                   (*chip, c)) for a in range(n) for j, chip in enumerate(chips)]


def _plan_pair_share(refs, send_sems, recv_sems, base):
    x, y, c, _ = _place()
    return [_rcopy(r.at[_half(r.shape[0], c)], r.at[_half(r.shape[0], c)], send_sems, recv_sems, base + a,
                   (x, y, 1 - c)) for a, r in enumerate(refs)]


def _plan_small_gather(refs, send_sems, recv_sems, base):
    x, y, c, _ = _place()
    me = 4 * x + 2 * y + c
    cps = []
    for m in range(1, NDEV):
        peer = (1 - x if m & 4 else x, 1 - y if m & 2 else y, 1 - c if m & 1 else c)
        cps.append(_rcopy(refs[0], refs[1].at[me], send_sems, recv_sems, base + m - 1, peer))
    return cps


def _sum_small(buf, gathered):
    def body(buf_ref, g_ref, o_ref):
        x, y, c, _ = _place()
        me = 4 * x + 2 * y + c
        total = jnp.where(me == 0, buf_ref[...], g_ref[0])
        for dev in range(1, NDEV):
            total = total + jnp.where(me == dev, buf_ref[...], g_ref[dev])
        o_ref[...] = total

    vm = pl.BlockSpec(memory_space=pltpu.VMEM)
    return pl.pallas_call(body, name="sum_small", in_specs=[vm, vm], out_specs=vm, out_shape=SDS(buf.shape, F32),
                          compiler_params=pltpu.CompilerParams(vmem_limit_bytes=VMEM_LIMIT))(buf, gathered)


BLOCK_ELEMS = 512 * 1024
SUM_BLOCK_ELEMS = 1024 * 1024


def _rows_per_block(rows, cols, mult, limit=BLOCK_ELEMS):
    best = None
    for tr in range(mult, rows + 1, mult):
        if rows % tr == 0 and tr * cols <= limit:
            best = tr
    assert best is not None, (rows, cols)
    return best


def _pair_sum(g, r, c_idx):
    nq, rows, cols = g.shape
    half = rows // 2
    tr = _rows_per_block(half, cols, 16, SUM_BLOCK_ELEMS)
    nb = half // tr

    def body(c_ref, g_ref, r_ref, t_ref):
        t_ref[...] = (g_ref[...] + r_ref[...]).astype(BF16)

    blk = pl.BlockSpec((None, tr, cols), lambda q, i, cr: (q, i, 0))
    spec = pltpu.PrefetchScalarGridSpec(
        num_scalar_prefetch=1, grid=(nq, nb),
        in_specs=[pl.BlockSpec((None, tr, cols), lambda q, i, cr: (q, cr[0] * nb + i, 0)), blk], out_specs=blk)
    return pl.pallas_call(body, name="grad_pair_sum", grid_spec=spec, out_shape=SDS((nq, half, cols), BF16),
                          compiler_params=_cp("parallel", "parallel"))(c_idx, g, r)


def _chip_sum(g, r, rr, cp_idx):
    _, rows, cols = g.shape
    half = rows // 2
    tr = _rows_per_block(half, cols, 16, SUM_BLOCK_ELEMS)
    nb = half // tr

    def body(cp_ref, buf_ref, g_ref, r_ref, rr_ref, o_ref):
        o_ref[...] = ((g_ref[...] + r_ref[...]) + rr_ref[0].astype(F32) + rr_ref[1].astype(F32) + rr_ref[2].astype(F32))

    spec = pltpu.PrefetchScalarGridSpec(
        num_scalar_prefetch=1, grid=(nb,),
        in_specs=[ANY, pl.BlockSpec((None, tr, cols), lambda i, cp: (cp[1], cp[0] * nb + i, 0)),
                  pl.BlockSpec((None, tr, cols), lambda i, cp: (cp[1], i, 0)),
                  pl.BlockSpec((3, tr, cols), lambda i, cp: (0, i, 0))],
        out_specs=pl.BlockSpec((tr, cols), lambda i, cp: (cp[0] * nb + i, 0)))
    return pl.pallas_call(body, name="grad_chip_sum", grid_spec=spec, out_shape=SDS((rows, cols), F32),
                          input_output_aliases={1: 0}, compiler_params=_cp("parallel"),
                          )(cp_idx, lax.empty((rows, cols), F32), g, r, rr)


def _adamw_math(w, g, m, v):
    mn = ADAM_B1 * m + (1.0 - ADAM_B1) * g
    vn = ADAM_B2 * v + (1.0 - ADAM_B2) * (g * g)
    m_hat = mn / (1.0 - ADAM_B1 ** ADAM_STEP)
    v_hat = vn / (1.0 - ADAM_B2 ** ADAM_STEP)
    return -ADAM_LR * (m_hat / (jnp.sqrt(v_hat) + ADAM_EPS) + ADAM_WD * w), mn, vn


def _adamw_layer(w, g, m, v, l, outs, deps=()):
    _, rows, cols = w.shape
    tr = _rows_per_block(rows, cols, 8)
    deps = list(deps)

    def body(*refs):
        w_ref, g_ref, m_ref, v_ref = refs[4:8]
        go_ref, d_ref, mo_ref, vo_ref = refs[8 + len(deps):]
        gg = g_ref[...]
        go_ref[...] = gg
        d_ref[...], mo_ref[...], vo_ref[...] = _adamw_math(w_ref[...], gg, m_ref[...], v_ref[...])

    blk = pl.BlockSpec((None, tr, cols), lambda i: (l, i, 0))
    return pl.pallas_call(
        body, name="adamw_layer", grid=(rows // tr,),
        in_specs=[ANY] * 4 + [blk, pl.BlockSpec((tr, cols), lambda i: (i, 0)), blk, blk] + [ANY] * len(deps),
        out_specs=[blk] * 4, out_shape=[SDS(w.shape, F32)] * 4, input_output_aliases={k: k for k in range(4)},
        compiler_params=_cp("parallel"))(*outs, w, g, m, v, *deps)


def _adamw_small(ws, gs, ms, vs, deps=()):
    n = len(ws)
    deps = list(deps)

    def body(*refs):
        refs = refs[:4 * n] + refs[4 * n + len(deps):]
        w, g, m, v, d_out, m_out, v_out = (refs[k * n:(k + 1) * n] for k in range(7))
        for k in range(n):
            d_out[k][...], m_out[k][...], v_out[k][...] = _adamw_math(w[k][...], g[k][...], m[k][...], v[k][...])

    vm = pl.BlockSpec(memory_space=pltpu.VMEM)
    outs = pl.pallas_call(body, name="adamw_small", in_specs=[vm] * (4 * n) + [ANY] * len(deps), out_specs=[vm] * (3 * n),
                          out_shape=[SDS(w.shape, F32) for w in ws] * 3,
                          compiler_params=pltpu.CompilerParams(vmem_limit_bytes=VMEM_LIMIT))(*ws, *gs, *ms, *vs, *deps)
    return outs[:n], outs[n:2 * n], outs[2 * n:]


_WEIGHTS = ["ffn1_pre_g", "ffn1_w_gu", "ffn1_w_down", "ffn1_post_g", "mix_pre_g", "w_in", "lru_conv_w", "lru_conv_b",
            "lru_w_a", "lru_b_a", "lru_w_x", "lru_b_x", "lru_lambda", "attn_sinks", "conv_w", "conv_b", "conv_ln_g",
            "conv_ln_b", "group_g", "w_out", "mix_post_g", "ffn2_pre_g", "ffn2_w_gu", "ffn2_w_down", "ffn2_post_g"]
_INPUTS = ["x"] + _WEIGHTS + ["loss_target"] + ["m_" + n for n in _WEIGHTS] + ["v_" + n for n in _WEIGHTS]
_BIG = ["ffn1_w_gu", "ffn1_w_down", "w_in", "w_out", "ffn2_w_gu", "ffn2_w_down"]
_SMALL_SHARDED = ["lru_conv_w", "conv_w"]
_SMALL_REPL = [n for n in _WEIGHTS if n not in _BIG and n not in _SMALL_SHARDED]

PACK_TILE = 8 * 128


def _pack(arrs):
    parts = []
    for a in arrs:
        flat = a.reshape(-1)
        parts.append(jnp.pad(flat, (0, -flat.shape[0] % PACK_TILE)).reshape(-1, 128))
    return jnp.concatenate(parts, axis=0)


def _unpack(buf, shapes):
    out, row = [], 0
    for shp in shapes:
        size = math.prod(shp)
        nrow = -(-size // PACK_TILE) * 8
        out.append(buf[row:row + nrow].reshape(-1)[:size].reshape(shp))
        row += nrow
    return out


def _unshard_cols(a):
    return a.transpose(0, 2, 1, 3).reshape(1, a.shape[2], NSHARD * a.shape[3])


_GROUPS = dict(ffn1_gu=["ffn1_w_gu"], ffn1_down=["ffn1_w_down"], mix=["w_in", "w_out", "lru_conv_w", "conv_w"],
               ffn2=["ffn2_w_gu", "ffn2_w_down"])


def _full_weights(group, gathered):
    g = dict(zip(_GROUPS[group], gathered))
    if group == "mix":
        return dict(w_in=_unshard_cols(g["w_in"]), w_out=g["w_out"].reshape(1, D, D),
                    lru_conv_w=_unshard_cols(g["lru_conv_w"])[0], conv_w=_unshard_cols(g["conv_w"])[0])
    return {n: (a.reshape(1, DFF, D) if n.endswith("w_down") else a) for n, a in g.items()}


def _by_shard(name, buf):
    if name.endswith("w_gu"):
        return buf[0]
    if name == "w_in":
        return buf.reshape(D, NSHARD, P_IN // NSHARD).transpose(1, 0, 2)
    return buf.reshape(NSHARD, buf.shape[2] // NSHARD, buf.shape[3])


class _Reducer:
    PLANS = (_plan_pair_exchange, _plan_chip_exchange, _plan_pair_share)

    def __init__(self, keys, gs, c_idx, cp_idx):
        self.keys, self.gs, self.c_idx, self.cp_idx = keys, gs, c_idx, cp_idx
        self.n = len(gs)
        self.step = 0
        self.result = None

    def inputs(self):
        n = self.n
        if self.step == 0:
            bufs = self.gs + [lax.empty((NSHARD, g.shape[1] // 2, g.shape[2]), F32) for g in self.gs]
        elif self.step == 1:
            ts = [_pair_sum(g, r, self.c_idx) for g, r in zip(self.gs, self.rs)]
            bufs = ts + [lax.empty((3,) + t.shape[1:], BF16) for t in ts]
        else:
            bufs = [_chip_sum(g, r, rr, self.cp_idx) for g, r, rr in zip(self.gs, self.rs, self.rrs)]
        return bufs, (self.PLANS[self.step], len(bufs), (n, 3 * n, n)[self.step])

    def absorb(self, done):
        n = self.n
        if self.step == 0:
            self.gs, self.rs = done[:n], done[n:]
        elif self.step == 1:
            self.rrs = done[n:]
        else:
            self.result = dict(zip(self.keys, done))
        self.step += 1


class _SmallGather:
    def __init__(self, buf):
        self.buf, self.step, self.result, self.gathered = buf, 0, {}, None

    def inputs(self):
        return [self.buf, jnp.zeros((NDEV,) + self.buf.shape, F32)], (_plan_small_gather, 2, NDEV - 1)

    def absorb(self, done):
        self.buf, self.gathered = done
        self.step = 3


class _ReducePipeline:
    def __init__(self, c_idx, cp_idx):
        self.c_idx, self.cp_idx = c_idx, cp_idx
        self.reducers, self.flying, self.calls = [], None, 0

    def add(self, layer, done):
        if done:
            keys = [(layer, n) for n in done]
            self.reducers.append(_Reducer(keys, [_by_shard(n, b) for n, b in done.items()], self.c_idx, self.cp_idx))

    def _next(self):
        active = [r for r in self.reducers if r.step < 3]
        bufs, plans = [], []
        for r in active:
            b, triple = r.inputs()
            bufs += b
            plans.append(triple)
        self.calls += 1
        return active, bufs, plans, "grad_exchange%d" % self.calls

    def _absorb(self, active, plans, done):
        at = 0
        for r, (_, nb, _) in zip(active, plans):
            r.absorb(done[at:at + nb])
            at += nb

    def _land(self, after):
        if self.flying is not None:
            active, plans, name, send_sems, recv_sems, bufs = self.flying
            self._absorb(active, plans, _exchange_wait(name + "_wait", send_sems, recv_sems, bufs, plans, after))
            self.flying = None

    def hook(self, after):
        self._land(after)
        active, bufs, plans, name = self._next()
        if not active:
            return []
        send_sems, recv_sems, bufs, token = _exchange_start(name + "_start", bufs, plans)
        self.flying = (active, plans, name, send_sems, recv_sems, bufs)
        return [token]

    def available(self):
        out = {}
        for r in self.reducers:
            if r.step == 3:
                out.update(r.result)
        return out

    def finish(self, after):
        self._land(after)
        while True:
            active, bufs, plans, name = self._next()
            if not active:
                break
            self._absorb(active, plans, _exchange(name, bufs, plans))
        out = {}
        for r in self.reducers:
            out.update(r.result)
        return out


def kernel(*args):
    d = dict(zip(_INPUTS, args, strict=True))
    xi, yi, ci = lax.axis_index("x"), lax.axis_index("y"), lax.axis_index("c")
    p = 2 * xi + yi
    c_idx = jnp.reshape(ci, (1,)).astype(jnp.int32)
    p_idx = jnp.reshape(p, (1,)).astype(jnp.int32)
    cp_idx = jnp.stack([ci, p]).astype(jnp.int32)
    x, target = d["x"][0], d["loss_target"][0]
    tiles = _tiles(x.shape[0])

    groups = [(l, grp) for l in range(DEPTH) for grp in _GROUPS]
    place = lambda l, grp, deps: [_place_shard(d[n], l, p_idx, BF16 if n in _BIG else F32, deps) for n in _GROUPS[grp]]
    first = place(*groups[0], ())
    half_plans = [(_plan_gather_half, len(first), 3 * len(first))]
    first_sems = _exchange_start("gather_first_start", first, half_plans)
    tokens = [first_sems[3]]
    flying = {}
    for l, grp in groups[1:]:
        placed = place(l, grp, tokens[:1])
        plans = [(_plan_gather, len(placed), 3 * len(placed))]
        send_sems, recv_sems, bufs, token = _exchange_start("gather_l%d_%s_start" % (l, grp), placed, plans, tokens[-1:])
        flying[l, grp] = (send_sems, recv_sems, bufs, plans)
        tokens.append(token)
    first = _exchange_wait("gather_first_wait", first_sems[0], first_sems[1], first_sems[2], half_plans, tokens[-1])
    ready = {groups[0]: _exchange("gather_first_forward", first, [(_plan_forward_half, len(first), 3 * len(first))])}

    def weights_of(l):
        def weights(grp, after):
            if (l, grp) not in ready:
                send_sems, recv_sems, bufs, plans = flying[l, grp]
                ready[l, grp] = _exchange_wait("gather_l%d_%s_wait" % (l, grp), send_sems, recv_sems, bufs, plans, after)
            return _full_weights(grp, ready[l, grp])
        return weights

    small = {n: d[n] for n in _SMALL_REPL}
    x1, sv0 = _forward_layer(x, weights_of(0), _layer_params(small, 0), tiles)
    x2, sv1 = _forward_layer(x1, weights_of(1), _layer_params(small, 1), tiles)
    dx, lcols, *dzp = _loss_grad(x2, target, tiles[0], (sv1["z2"], sv1["p"]["ffn2_post"], 0.5))

    pipe = _ReducePipeline(c_idx, cp_idx)
    sgrads = [None] * DEPTH
    for l, sv, below in ((1, sv1, (sv0["z2"], sv0["p"]["ffn2_post"], 0.5)), (0, sv0, None)):
        bufs = _grad_buffers()

        def stage(done, dx, l=l):
            pipe.add(l, done)
            return pipe.hook(dx)

        dx, dzp, sgrads[l] = _backward_layer(dx, tuple(dzp), sv, bufs, tiles, stage, below)
    grad_x = dx

    stacked = {n: jnp.stack([sgrads[l][n].reshape(d[n].shape[1:]) for l in range(DEPTH)]) for n in _SMALL_REPL}
    for n in _SMALL_SHARDED:
        stacked[n] = jnp.stack([sgrads[l][n] for l in range(DEPTH)])
    loss_part = jnp.pad((0.5 / D) * jnp.sum(lcols).reshape(1), (0, 127))
    order = _SMALL_REPL + _SMALL_SHARDED
    small_gather = _SmallGather(_pack([loss_part] + [stacked[n] for n in order]))
    pipe.reducers.append(small_gather)

    results = {n: tuple(lax.empty(d[n].shape, F32) for _ in range(4)) for n in _BIG}
    applied = set()

    def apply_ready(deps, last):
        for (l, n), g in pipe.available().items():
            if (l, n) not in applied:
                results[n] = _adamw_layer(d[n], g, d["m_" + n], d["v_" + n], l, results[n], deps)
                applied.add((l, n))
                last = results[n][1]
                deps = [last]
        return last

    last = apply_ready(pipe.hook(grad_x), grad_x)
    token = pipe.hook(last)
    summed = _unpack(_sum_small(small_gather.buf, small_gather.gathered), [(128,)] + [stacked[n].shape for n in order])
    loss = summed[0][0]
    grads = {}
    for n, g in zip(order, summed[1:]):
        if n in _SMALL_SHARDED:
            g = lax.dynamic_slice_in_dim(g, p * (g.shape[2] // NSHARD), g.shape[2] // NSHARD, axis=2)
        grads[n] = g
    delta, new_m, new_v = {}, {}, {}
    small_out = _adamw_small([d[n] for n in order], [grads[n] for n in order], [d["m_" + n] for n in order],
                             [d["v_" + n] for n in order], token)
    for out, res in zip((delta, new_m, new_v), small_out):
        out.update(zip(order, res))
    last = apply_ready([small_out[0][0]], small_out[0][0])
    pipe.finish(last)
    apply_ready((), last)
    for n in _BIG:
        grads[n], delta[n], new_m[n], new_v[n] = results[n]

    return (loss, grad_x[None], *[grads[n] for n in _WEIGHTS], *[delta[n] for n in _WEIGHTS],
            *[new_m[n] for n in _WEIGHTS], *[new_v[n] for n in _WEIGHTS])
```

```python
import math

import jax
import jax.numpy as jnp
import numpy as np
from jax import lax
from jax.experimental import pallas as pl
from jax.experimental.pallas import tpu as pltpu

F32 = jnp.float32
BF16 = jnp.bfloat16
SDS = jax.ShapeDtypeStruct

D = 1024
DFF = 2816
FH = DFF // 2
DEPTH = 2
W_A = 256
W_B = 512
W_C = 256
NQ = 8
HD = 64
BLK = 128
ATT_NB_FWD = 1
ATT_NB_BWD = 8
P_IN = 1792
LRU_K = 4
CONV_K = 31
LRU_C = 8.0
NORM_EPS = 1e-6
LN_EPS = 1e-5
NEG_BIG = -1e30
SCALE = 1.0 / math.sqrt(HD)

ADAM_LR = 0.001
ADAM_B1 = 0.9
ADAM_B2 = 0.999
ADAM_EPS = 1e-08
ADAM_WD = 0.01
ADAM_STEP = 10

VMEM_LIMIT = 60 * 1024 * 1024
NSHARD = 4
NDEV = 8

TN = (((0,), (0,)), ((), ()))
NT = (((1,), (1,)), ((), ()))

MESH = pl.DeviceIdType.MESH
ANY = pl.BlockSpec(memory_space=pl.ANY)


def _cp(*sem):
    return pltpu.CompilerParams(dimension_semantics=sem if sem else None, vmem_limit_bytes=VMEM_LIMIT)


def _rsq(x, eps):
    return lax.rsqrt(jnp.mean(x * x, axis=-1, keepdims=True) + eps)


def _rms_bwd_rows(x, g, dy):
    r = _rsq(x, NORM_EPS)
    xh = x * r
    dyg = dy * g
    dx = r * (dyg - xh * jnp.mean(dyg * xh, axis=-1, keepdims=True))
    return dx, dy * xh


def _sig(x):
    return jax.nn.sigmoid(x)


def _post_norm_tail(dx, c, z_ref, g_ref, dz_ref, dg_ref, first):
    dz, dgr = _rms_bwd_rows(z_ref[...], g_ref[...], c * dx)
    dz_ref[...] = dz.astype(BF16)
    _acc_rows(dg_ref, first, dgr)


def _ffn_up(x, pre_g, wgu, l, tm, deps=()):
    s = x.shape[0]
    deps = list(deps)

    def body(x_ref, g_ref, wg_ref, wu_ref, *rest):
        h_ref, go_ref, uo_ref, a_ref = rest[len(deps):]

        @pl.when(pl.program_id(1) == 0)
        def _():
            xf = x_ref[...]
            h_ref[...] = (xf * _rsq(xf, NORM_EPS) * g_ref[...]).astype(BF16)

        h = h_ref[...]
        gg = jnp.dot(h, wg_ref[...], preferred_element_type=F32)
        uu = jnp.dot(h, wu_ref[...], preferred_element_type=F32)
        sg = _sig(gg)
        silu = gg * sg
        go_ref[...] = (uu * (sg * (1.0 + gg * (1.0 - sg)))).astype(BF16)
        uo_ref[...] = silu.astype(BF16)
        a_ref[...] = (silu * uu).astype(BF16)

    wide = pl.BlockSpec((tm, FH), lambda i, j: (i, j))
    return pl.pallas_call(
        body, name="ffn_up", grid=(s // tm, 2),
        in_specs=[pl.BlockSpec((tm, D), lambda i, j: (i, 0)), pl.BlockSpec((1, D), lambda i, j: (0, 0)),
                  pl.BlockSpec((None, None, D, FH), lambda i, j: (l, j, 0, 0)),
                  pl.BlockSpec((None, None, D, FH), lambda i, j: (l, j + 2, 0, 0))] + [ANY] * len(deps),
        out_specs=[pl.BlockSpec((tm, D), lambda i, j: (i, 0)), wide, wide, wide],
        out_shape=[SDS((s, D), BF16), SDS((s, DFF), BF16), SDS((s, DFF), BF16), SDS((s, DFF), BF16)],
        compiler_params=_cp("parallel", "arbitrary"),
    )(x, pre_g, wgu, wgu, *deps)


def _mm_rms_res(a, w, l, x, g, c, tm, tk, name):
    s, k_dim = a.shape
    nk = k_dim // tk

    def body(a_ref, w_ref, x_ref, g_ref, z_ref, x1_ref):
        k = pl.program_id(1)
        p = jnp.dot(a_ref[...], w_ref[...], preferred_element_type=F32)

        @pl.when(k == 0)
        def _():
            z_ref[...] = p

        @pl.when(k > 0)
        def _():
            z_ref[...] += p

        @pl.when(k == nk - 1)
        def _():
            z = z_ref[...]
            x1_ref[...] = x_ref[...] + c * (z * _rsq(z, NORM_EPS) * g_ref[...])

    row = pl.BlockSpec((tm, D), lambda i, k: (i, 0))
    return pl.pallas_call(
        body, name=name, grid=(s // tm, nk),
        in_specs=[pl.BlockSpec((tm, tk), lambda i, k: (i, k)), pl.BlockSpec((None, tk, D), lambda i, k: (l, k, 0)),
                  row, pl.BlockSpec((1, D), lambda i, k: (0, 0))],
        out_specs=[row, row],
        out_shape=[SDS((s, D), F32), SDS((s, D), F32)],
        compiler_params=_cp("parallel", "arbitrary"),
    )(a, w, x, g)


def _ffn_bwd_mid(dz, wd, l, dadg, dadu, tm, deps=()):
    s = dz.shape[0]
    deps = list(deps)

    def body(dz_ref, wd_ref, g_ref, u_ref, *rest):
        dgu_ref = rest[len(deps)]
        da = lax.dot_general(dz_ref[...], wd_ref[...], NT, preferred_element_type=F32)
        dgu_ref[:, 0:FH] = (da * g_ref[...].astype(F32)).astype(BF16)
        dgu_ref[:, FH:2 * FH] = (da * u_ref[...].astype(F32)).astype(BF16)

    wide = pl.BlockSpec((tm, FH), lambda i, j: (i, j))
    return pl.pallas_call(
        body, name="ffn_bwd_mid", grid=(s // tm, 2),
        in_specs=[pl.BlockSpec((tm, D), lambda i, j: (i, 0)), pl.BlockSpec((None, FH, D), lambda i, j: (l, j, 0)), wide, wide]
        + [ANY] * len(deps),
        out_specs=pl.BlockSpec((tm, 2 * FH), lambda i, j: (i, j)),
        out_shape=SDS((s, 2 * DFF), BF16),
        compiler_params=_cp("parallel", "arbitrary"),
    )(dz, wd, dadg, dadu, *deps)


def _ffn_bwd_dh(dgu, wgu, l, x, pre_g, dx1, tm, deps=(), below=None):
    s = x.shape[0]
    deps = list(deps)
    tail = [] if below is None else list(below[:2])

    def body(dgu_ref, w_hbm, x_ref, g_ref, dx1_ref, *rest):
        rest = rest[len(deps):]
        tail_in, (dx_ref, dgp_ref), rest = rest[:len(tail)], rest[len(tail):len(tail) + 2], rest[len(tail) + 2:]
        tail_out, (wcat_ref, sems) = rest[:len(tail)], rest[len(tail):]
        i = pl.program_id(0)

        @pl.when(i == 0)
        def _():
            cps = [pltpu.make_async_copy(w_hbm.at[l, q], wcat_ref.at[:, pl.ds((2 * (q % 2) + q // 2) * FH, FH)], sems.at[q])
                   for q in range(NSHARD)]
            for cp in cps:
                cp.start()
            for cp in cps:
                cp.wait()

        dh = lax.dot_general(dgu_ref[...], wcat_ref[...], NT, preferred_element_type=F32)
        dx, dgr = _rms_bwd_rows(x_ref[...], g_ref[...], dh)
        dx = dx1_ref[...] + dx
        dx_ref[...] = dx
        _acc_rows(dgp_ref, i == 0, dgr)
        if tail:
            _post_norm_tail(dx, below[2], *tail_in, *tail_out, i == 0)

    row = pl.BlockSpec((tm, D), lambda i: (i, 0))
    vec = pl.BlockSpec((1, D), lambda i: (0, 0))
    return pl.pallas_call(
        body, name="ffn_bwd_dh", grid=(s // tm,),
        in_specs=[pl.BlockSpec((tm, 2 * DFF), lambda i: (i, 0)), ANY, row, vec, row] + [ANY] * len(deps) + [row, vec][:len(tail)],
        out_specs=[row, vec] + [row, vec][:len(tail)],
        out_shape=[SDS((s, D), F32), SDS((1, D), F32)] + [SDS((s, D), BF16), SDS((1, D), F32)][:len(tail)],
        scratch_shapes=[pltpu.VMEM((D, 2 * DFF), BF16), pltpu.SemaphoreType.DMA((NSHARD,))],
        compiler_params=_cp("arbitrary"),
    )(dgu, wgu, x, pre_g, dx1, *deps, *tail)


def _mm_tn_into(buf, a, b, l, joff, tka, tn, ts, name, bstride=1, boff=0):
    s, ka = a.shape
    n = b.shape[1] // bstride

    def body(buf_ref, a_ref, b_ref, o_ref):
        p = lax.dot_general(a_ref[...], b_ref[...], TN, preferred_element_type=F32)

        @pl.when(pl.program_id(2) == 0)
        def _():
            o_ref[...] = p

        @pl.when(pl.program_id(2) > 0)
        def _():
            o_ref[...] += p

    return pl.pallas_call(
        body, name=name, grid=(ka // tka, n // tn, s // ts),
        in_specs=[pl.BlockSpec(memory_space=pl.ANY),
                  pl.BlockSpec((ts, tka), lambda ia, j, t: (t, ia)),
                  pl.BlockSpec((ts, tn), lambda ia, j, t: (t, bstride * j + boff))],
        out_specs=pl.BlockSpec((None, None, tka, tn), lambda ia, j, t: (l, joff + j, ia, 0)),
        out_shape=SDS(buf.shape, F32), input_output_aliases={0: 0},
        compiler_params=_cp("parallel", "parallel", "arbitrary"),
    )(buf, a, b)


def _proj(x, g, w_in, l, tm):
    s = x.shape[0]

    def body(x_ref, g_ref, w_ref, h_ref, p_ref):
        xf = x_ref[...]
        h = (xf * _rsq(xf, NORM_EPS) * g_ref[...]).astype(BF16)
        h_ref[...] = h
        p_ref[...] = jnp.dot(h, w_ref[...], preferred_element_type=F32)

    return pl.pallas_call(
        body, name="proj", grid=(s // tm,),
        in_specs=[pl.BlockSpec((tm, D), lambda i: (i, 0)), pl.BlockSpec((1, D), lambda i: (0, 0)),
                  pl.BlockSpec((None, D, P_IN), lambda i: (l, 0, 0))],
        out_specs=[pl.BlockSpec((tm, D), lambda i: (i, 0)), pl.BlockSpec((tm, P_IN), lambda i: (i, 0))],
        out_shape=[SDS((s, D), BF16), SDS((s, P_IN), F32)],
        compiler_params=_cp("parallel"),
    )(x, g, w_in)


def _mm_nt(a, w, l, tm, name):
    s, k_dim = a.shape
    n = w.shape[1]

    def body(a_ref, w_ref, o_ref):
        o_ref[...] = lax.dot_general(a_ref[...], w_ref[...], NT, preferred_element_type=F32)

    return pl.pallas_call(
        body, name=name, grid=(s // tm,),
        in_specs=[pl.BlockSpec((tm, k_dim), lambda i: (i, 0)), pl.BlockSpec((None, n, k_dim), lambda i: (l, 0, 0))],
        out_specs=pl.BlockSpec((tm, n), lambda i: (i, 0)),
        out_shape=SDS((s, n), F32), compiler_params=_cp("parallel"),
    )(a, w)


def _mm_nt_rmsbwd(dp, w_in, l, x, g, dx1, tm, below):
    s = x.shape[0]

    def body(dp_ref, w_ref, x_ref, g_ref, dx1_ref, zb_ref, gb_ref, dx_ref, dg_ref, dzb_ref, dgb_ref):
        first = pl.program_id(0) == 0
        dh = lax.dot_general(dp_ref[...], w_ref[...], NT, preferred_element_type=F32)
        dx, dgr = _rms_bwd_rows(x_ref[...], g_ref[...], dh)
        dx = dx1_ref[...] + dx
        dx_ref[...] = dx
        _acc_rows(dg_ref, first, dgr)
        _post_norm_tail(dx, below[2], zb_ref, gb_ref, dzb_ref, dgb_ref, first)

    row = pl.BlockSpec((tm, D), lambda i: (i, 0))
    vec = pl.BlockSpec((1, D), lambda i: (0, 0))
    return pl.pallas_call(
        body, name="mix_bwd_dx", grid=(s // tm,),
        in_specs=[pl.BlockSpec((tm, P_IN), lambda i: (i, 0)), pl.BlockSpec((None, D, P_IN), lambda i: (l, 0, 0)), row, vec, row,
                  row, vec],
        out_specs=[row, vec, row, vec],
        out_shape=[SDS((s, D), F32), SDS((1, D), F32), SDS((s, D), BF16), SDS((1, D), F32)],
        compiler_params=_cp("arbitrary"),
    )(dp, w_in, x, g, dx1, below[0], below[1])


def _row_iota(shape):
    return lax.broadcasted_iota(jnp.int32, shape, 0)


def _lru_gates(xc, wa_ref, ba_ref, wx_ref, bx_ref, lam_ref):
    xb = xc.astype(BF16)
    r = _sig(jnp.dot(xb, wa_ref[...], preferred_element_type=F32) + ba_ref[...])
    ig = _sig(jnp.dot(xb, wx_ref[...], preferred_element_type=F32) + bx_ref[...])
    nl = -lam_ref[...]
    sp = jnp.maximum(nl, 0.0) + jnp.log(1.0 + jnp.exp(-jnp.abs(nl)))
    log_a = -LRU_C * r * sp
    a = jnp.exp(log_a)
    mlt = jnp.sqrt((1.0 + a * a) * jnp.tanh(-log_a))
    return r, ig, a, mlt, sp


def _conv_taps(src_ref, w_ref, k_taps, pad, tc):
    acc = None
    for j in range(k_taps):
        term = w_ref[j:j + 1, :] * src_ref[pl.ds(pad - (k_taps - 1) + j, tc), :]
        acc = term if acc is None else acc + term
    return acc


def _fill_shifted(src_ref, sh_ref):
    n = src_ref.shape[0] - 8
    for s in range(1, 8):
        sh_ref[s, 0:n, :] = src_ref[pl.ds(s, n), :]


def _shifted_rows(src_ref, sh_ref, offset, tc):
    if offset % 8 == 0:
        return src_ref[pl.ds(offset, tc), :]
    return sh_ref[offset % 8, pl.ds(offset - offset % 8, tc), :]


def _gelu_parts(x):
    c0 = math.sqrt(2.0 / math.pi)
    inner = c0 * (x + 0.044715 * x * x * x)
    t = jnp.tanh(inner)
    gl = 0.5 * x * (1.0 + t)
    dgl = 0.5 * (1.0 + t) + 0.5 * x * (1.0 - t * t) * c0 * (1.0 + 3.0 * 0.044715 * x * x)
    return gl, dgl


def _lru_fwd(proj, cw, cb, wa, ba, wx, bx, lam, gg, tc):
    s = proj.shape[0]
    pad = 8

    def body(xcur_ref, xprev_ref, gate_ref, cw_ref, cb_ref, wa_ref, ba_ref, wx_ref, bx_ref, lam_ref, gg_ref,
             yn_ref, h_ref, xs_ref, hc_ref):
        i = pl.program_id(0)

        @pl.when(i == 0)
        def _():
            hc_ref[...] = jnp.zeros_like(hc_ref)

        xs_ref[0:pad, :] = jnp.where(i > 0, xprev_ref[tc - pad:tc, :], 0.0)
        xs_ref[pad:pad + tc, :] = xcur_ref[...]
        xc = _conv_taps(xs_ref, cw_ref, LRU_K, pad, tc) + cb_ref[...]
        _, ig, a, mlt, _ = _lru_gates(xc, wa_ref, ba_ref, wx_ref, bx_ref, lam_ref)
        u = mlt * (ig * xc)
        row = _row_iota((tc, W_A))
        d = 1
        while d < tc:
            ok = row >= d
            a_sh = jnp.where(ok, pltpu.roll(a, d, axis=0), 1.0)
            u_sh = jnp.where(ok, pltpu.roll(u, d, axis=0), 0.0)
            u = a * u_sh + u
            a = a * a_sh
            d *= 2
        h = u + a * hc_ref[...]
        hc_ref[...] = jnp.sum(jnp.where(row == tc - 1, h, 0.0), axis=0, keepdims=True)
        h_ref[...] = h
        gl, _ = _gelu_parts(gate_ref[...])
        ya = gl * h
        yn_ref[...] = (ya * _rsq(ya, NORM_EPS) * gg_ref[...]).astype(BF16)

    blk = lambda c: pl.BlockSpec((tc, W_A), lambda i, c=c: (i, c))
    full = lambda a: pl.BlockSpec(a.shape, lambda i: (0,) * a.ndim)
    params = [cw, cb, wa, ba, wx, bx, lam, gg]
    return pl.pallas_call(
        body, name="lru_fwd", grid=(s // tc,),
        in_specs=[blk(0), pl.BlockSpec((tc, W_A), lambda i: (jnp.maximum(i - 1, 0), 0)), blk(1)] + [full(a) for a in params],
        out_specs=[pl.BlockSpec((tc, W_A), lambda i: (i, 0))] * 2,
        out_shape=[SDS((s, W_A), BF16), SDS((s, W_A), F32)],
        scratch_shapes=[pltpu.VMEM((tc + pad, W_A), F32), pltpu.VMEM((1, W_A), F32)],
        compiler_params=_cp("arbitrary"),
    )(proj, proj, proj, *params)


def _acc_rows(ref, first, rows):
    _acc(ref, first, jnp.sum(rows, axis=0, keepdims=True))


def _acc(ref, first, val):
    @pl.when(first)
    def _():
        ref[...] = val

    @pl.when(jnp.logical_not(first))
    def _():
        ref[...] += val


def _lru_bwd(dy, proj, h, cw, cb, wa, ba, wx, bx, lam, gg, tc):
    s = proj.shape[0]
    nc = s // tc
    pad = 8

    def body(dy_ref, xcur_ref, xprev_ref, gate_ref, h_ref, hprev_ref, cw_ref, cb_ref, wa_ref, ba_ref, wx_ref, bx_ref,
             lam_ref, gg_ref,
             dp_ref, dcw_ref, dcb_ref, dwa_ref, dba_ref, dwx_ref, dbx_ref, dlam_ref, dgg_ref,
             xs_ref, ds_ref, mu_ref, nx_ref):
        step = pl.program_id(0)
        i = nc - 1 - step
        first = step == 0

        @pl.when(first)
        def _():
            mu_ref[...] = jnp.zeros_like(mu_ref)
            nx_ref[...] = jnp.zeros_like(nx_ref)

        xs_ref[0:pad, :] = jnp.where(i > 0, xprev_ref[tc - pad:tc, :], 0.0)
        xs_ref[pad:pad + tc, :] = xcur_ref[...]
        xc = _conv_taps(xs_ref, cw_ref, LRU_K, pad, tc) + cb_ref[...]
        r, ig, a, mlt, sp = _lru_gates(xc, wa_ref, ba_ref, wx_ref, bx_ref, lam_ref)
        hh = h_ref[...]
        gate = gate_ref[...]
        gl, dgl = _gelu_parts(gate)
        ya = gl * hh
        dya, dggr = _rms_bwd_rows(ya, gg_ref[...], dy_ref[...])
        _acc(dgg_ref, first, jnp.sum(dggr, axis=0, keepdims=True))
        dp_ref[:, W_A:2 * W_A] = dya * hh * dgl
        dh = dya * gl

        row = _row_iota((tc, W_A))
        aa = a
        uu = a * dh
        d = 1
        while d < tc:
            ok = row < tc - d
            a_sh = jnp.where(ok, pltpu.roll(aa, tc - d, axis=0), 1.0)
            u_sh = jnp.where(ok, pltpu.roll(uu, tc - d, axis=0), 0.0)
            uu = uu + aa * u_sh
            aa = aa * a_sh
            d *= 2
        cin = mu_ref[...]
        mu = uu + aa * cin
        lam_t = dh + jnp.where(row == tc - 1, cin, pltpu.roll(mu, tc - 1, axis=0))
        mu_ref[...] = jnp.sum(jnp.where(row == 0, mu, 0.0), axis=0, keepdims=True)
        hm1 = jnp.where(row == 0, jnp.where(i > 0, pltpu.roll(hprev_ref[...], 1, axis=0), 0.0),
                        pltpu.roll(hh, 1, axis=0))
        da = lam_t * hm1
        du = lam_t
        dmlt = du * ig * xc
        dig = du * mlt * xc
        dxc = du * mlt * ig
        dlog_a = da * a - dmlt * (a * a / mlt)
        dr = dlog_a * (-LRU_C * sp)
        dsp = jnp.sum(dlog_a * (-LRU_C * r), axis=0, keepdims=True)
        _acc(dlam_ref, first, dsp * (-_sig(-lam_ref[...])))
        dga = dr * r * (1.0 - r)
        dgx = dig * ig * (1.0 - ig)
        _acc(dba_ref, first, jnp.sum(dga, axis=0, keepdims=True))
        _acc(dbx_ref, first, jnp.sum(dgx, axis=0, keepdims=True))
        xb = xc.astype(BF16)
        dgab = dga.astype(BF16)
        dgxb = dgx.astype(BF16)
        _acc(dwa_ref, first, lax.dot_general(xb, dgab, TN, preferred_element_type=F32))
        _acc(dwx_ref, first, lax.dot_general(xb, dgxb, TN, preferred_element_type=F32))
        dxc = (dxc + lax.dot_general(dgab, wa_ref[...], NT, preferred_element_type=F32)
               + lax.dot_general(dgxb, wx_ref[...], NT, preferred_element_type=F32))

        _acc(dcb_ref, first, jnp.sum(dxc, axis=0, keepdims=True))
        r8 = _row_iota((8, W_A))
        dcw = jnp.zeros((8, W_A), F32)
        for j in range(LRU_K):
            tap = jnp.sum(dxc * xs_ref[pl.ds(pad - (LRU_K - 1) + j, tc), :], axis=0, keepdims=True)
            dcw = dcw + jnp.where(r8 == j, tap, 0.0)
        _acc(dcw_ref, first, dcw)
        ds_ref[0:tc, :] = dxc
        ds_ref[tc:tc + pad, :] = nx_ref[...]
        dlx = None
        for j in range(LRU_K):
            term = cw_ref[j:j + 1, :] * ds_ref[pl.ds(LRU_K - 1 - j, tc), :]
            dlx = term if dlx is None else dlx + term
        dp_ref[:, 0:W_A] = dlx
        nx_ref[...] = dxc[0:pad, :]

    rev = lambda c: pl.BlockSpec((tc, W_A), lambda t, c=c: (nc - 1 - t, c))
    prev = lambda c: pl.BlockSpec((tc, W_A), lambda t, c=c: (jnp.maximum(nc - 2 - t, 0), c))
    full = lambda a: pl.BlockSpec(a.shape, lambda t: (0,) * a.ndim)
    params = [cw, cb, wa, ba, wx, bx, lam, gg]
    vec = SDS((1, W_A), F32)
    sq = SDS((W_A, W_A), F32)
    outs = [SDS((s, 2 * W_A), F32), SDS((8, W_A), F32), vec, sq, vec, sq, vec, vec, vec]
    return pl.pallas_call(
        body, name="lru_bwd", grid=(nc,),
        in_specs=[rev(0), rev(0), prev(0), rev(1), rev(0), prev(0)] + [full(a) for a in params],
        out_specs=[pl.BlockSpec((tc, 2 * W_A), lambda t: (nc - 1 - t, 0))]
        + [pl.BlockSpec(o.shape, lambda t: (0, 0)) for o in outs[1:]],
        out_shape=outs,
        scratch_shapes=[pltpu.VMEM((tc + pad, W_A), F32), pltpu.VMEM((tc + pad, W_A), F32),
                        pltpu.VMEM((1, W_A), F32), pltpu.VMEM((pad, W_A), F32)],
        compiler_params=_cp("arbitrary"),
    )(dy, proj, proj, proj, h, h, *params)


def _attn_stack(qa, qb, kvh):
    lane = lax.broadcasted_iota(jnp.int32, qa.shape, 1)
    keep = (lane >= HD) if kvh == 1 else (lane < HD)
    parts = []
    for tile in (qa, qb):
        for half in (0, 1):
            y = tile if half == kvh else pltpu.roll(tile, HD, axis=1)
            parts.append(jnp.where(keep, y, 0.0))
    return jnp.concatenate(parts, axis=0)


def _attn_unstack(o, kvh):
    lane = lax.broadcasted_iota(jnp.int32, (BLK, 2 * HD), 1)
    tiles = []
    for t in range(2):
        halves = []
        for half in (0, 1):
            blk = o[(2 * t + half) * BLK:(2 * t + half + 1) * BLK, :]
            halves.append(blk if half == kvh else pltpu.roll(blk, HD, axis=1))
        tiles.append(jnp.where(lane < HD, halves[0], halves[1]))
    return tiles


def _attn_stack_all(x_ref_or_val):
    return jnp.concatenate([_attn_stack(x_ref_or_val[:, 256 * kvh:256 * kvh + 128],
                                        x_ref_or_val[:, 256 * kvh + 128:256 * kvh + 256], kvh) for kvh in range(2)], axis=0)


def _attn_unstack_all(o, dst_ref):
    for kvh in range(2):
        ta, tb = _attn_unstack(o[4 * BLK * kvh:4 * BLK * (kvh + 1), :], kvh)
        dst_ref[:, 256 * kvh:256 * kvh + 128] = ta
        dst_ref[:, 256 * kvh + 128:256 * kvh + 256] = tb


def _attn_windows(cur_ref, prev_ref, nb):
    blocks = [prev_ref[...]] + [cur_ref[b * BLK:(b + 1) * BLK, :] for b in range(nb)]
    return [jnp.concatenate(blocks[b:b + 2], axis=0).astype(BF16) for b in range(nb)]


def _attn_bias():
    qi = np.arange(NQ * BLK)[:, None] % BLK
    kj = np.arange(2 * BLK)[None, :]
    rel = BLK + qi - kj
    ok = (rel >= 0) & (rel < BLK)
    return jnp.asarray(np.stack([np.where(ok & (kj >= BLK), 0.0, NEG_BIG), np.where(ok, 0.0, NEG_BIG)]), F32)


def _attn_probs(qs, kw, first, sink_ref, bias_ref):
    rows = NQ * BLK
    bias = bias_ref[1] if first is False else jnp.where(first, bias_ref[0], bias_ref[1])
    sh = lax.dot_general(qs.astype(BF16), kw, NT, preferred_element_type=F32) * SCALE + bias
    head = lax.broadcasted_iota(jnp.int32, (rows, 1), 0) // BLK
    sk = jnp.zeros((rows, 1), F32)
    for h in range(NQ):
        sk = jnp.where(head == h, sink_ref[h:h + 1, 0:1], sk)
    m = jnp.maximum(jnp.max(sh, axis=-1, keepdims=True), sk)
    e = jnp.exp(sh - m)
    es = jnp.exp(sk - m)
    rz = 1.0 / (jnp.sum(e, axis=-1, keepdims=True) + es)
    return e * rz, es * rz


def _attn_fwd(proj, sinks8, gg):
    s = proj.shape[0]
    nb = ATT_NB_FWD

    def body(q_ref, kc_ref, kp_ref, vc_ref, vp_ref, sink_ref, gg_ref, bias_ref, yn_ref, ob_ref):
        kws, vws = _attn_windows(kc_ref, kp_ref, nb), _attn_windows(vc_ref, vp_ref, nb)
        for b in range(nb):
            rows = pl.ds(b * BLK, BLK)
            first = (pl.program_id(0) == 0) if b == 0 else False
            p, _ = _attn_probs(_attn_stack_all(q_ref.at[rows, :]), kws[b], first, sink_ref, bias_ref)
            _attn_unstack_all(jnp.dot(p.astype(BF16), vws[b], preferred_element_type=F32), ob_ref.at[rows, :])
        ob = ob_ref[...]
        yn_ref[...] = (ob * _rsq(ob, NORM_EPS) * gg_ref[...]).astype(BF16)

    tb = nb * BLK
    cur = lambda c: pl.BlockSpec((tb, 128), lambda m, c=c: (m, c))
    prev = lambda c: pl.BlockSpec((BLK, 128), lambda m, c=c: (jnp.maximum(nb * m - 1, 0), c))
    out = pl.BlockSpec((tb, W_B), lambda m: (m, 0))
    return pl.pallas_call(
        body, name="attn_fwd", grid=(s // tb,),
        in_specs=[pl.BlockSpec((tb, W_B), lambda m: (m, 1)), cur(8), prev(8), cur(9), prev(9),
                  pl.BlockSpec((8, 128), lambda n: (0, 0)), pl.BlockSpec((1, W_B), lambda n: (0, 0)),
                  pl.BlockSpec((2, NQ * BLK, 2 * BLK), lambda n: (0, 0, 0))],
        out_specs=[out, out], out_shape=[SDS((s, W_B), BF16), SDS((s, W_B), F32)],
        compiler_params=_cp("parallel"),
    )(proj, proj, proj, proj, proj, sinks8, gg, _attn_bias())


def _attn_bwd(dy, proj, ob, sinks8, gg):
    s = proj.shape[0]
    nb = ATT_NB_BWD

    def body(dya_ref, dyb_ref, q_ref, kc_ref, kp_ref, vc_ref, vp_ref, ob_ref, sink_ref, gg_ref, bias_ref,
             dq_ref, dcur_ref, dprev_ref, dsink_ref, dgg_ref):
        first = pl.program_id(0) == 0
        kws, vws = _attn_windows(kc_ref, kp_ref, nb), _attn_windows(vc_ref, vp_ref, nb)
        dyn = jnp.concatenate([dya_ref[...], dyb_ref[...]], axis=1)
        dob, dggr = _rms_bwd_rows(ob_ref[...], gg_ref[...], dyn)
        _acc(dgg_ref, first, jnp.sum(dggr, axis=0, keepdims=True))
        r8 = _row_iota((8, 128))
        dsk = jnp.zeros((8, 128), F32)
        for b in range(nb):
            rows = pl.ds(b * BLK, BLK)
            qs = _attn_stack_all(q_ref.at[rows, :])
            p, psink = _attn_probs(qs, kws[b], first if b == 0 else False, sink_ref, bias_ref)
            dosb = _attn_stack_all(dob[b * BLK:(b + 1) * BLK, :]).astype(BF16)
            dp = lax.dot_general(dosb, vws[b], NT, preferred_element_type=F32)
            dd = jnp.sum(p * dp, axis=-1, keepdims=True)
            dsb = (p * (dp - dd) * SCALE).astype(BF16)
            dsink_rows = -psink * dd
            for h in range(NQ):
                dsk = dsk + jnp.where(r8 == h, jnp.sum(dsink_rows[h * BLK:(h + 1) * BLK, :], axis=0, keepdims=True), 0.0)
            _attn_unstack_all(jnp.dot(dsb, kws[b], preferred_element_type=F32), dq_ref.at[rows, :])
            dkw = lax.dot_general(dsb, qs.astype(BF16), TN, preferred_element_type=F32)
            dvw = lax.dot_general(p.astype(BF16), dosb, TN, preferred_element_type=F32)
            dprev_ref[rows, 0:128] = dkw[0:BLK, :]
            dprev_ref[rows, 128:256] = dvw[0:BLK, :]
            dcur_ref[rows, 0:128] = dkw[BLK:2 * BLK, :]
            dcur_ref[rows, 128:256] = dvw[BLK:2 * BLK, :]
        _acc(dsink_ref, first, dsk)

    tb = nb * BLK
    cur = lambda c: pl.BlockSpec((tb, 128), lambda m, c=c: (m, c))
    prev = lambda c: pl.BlockSpec((BLK, 128), lambda m, c=c: (jnp.maximum(nb * m - 1, 0), c))
    wide = pl.BlockSpec((tb, W_B), lambda m: (m, 0))
    half = pl.BlockSpec((tb, 256), lambda m: (m, 0))
    return pl.pallas_call(
        body, name="attn_bwd", grid=(s // tb,),
        in_specs=[pl.BlockSpec((tb, 256), lambda m: (m, 1)), pl.BlockSpec((tb, 256), lambda m: (m, 2)),
                  pl.BlockSpec((tb, W_B), lambda m: (m, 1)), cur(8), prev(8), cur(9), prev(9), wide,
                  pl.BlockSpec((8, 128), lambda n: (0, 0)), pl.BlockSpec((1, W_B), lambda n: (0, 0)),
                  pl.BlockSpec((2, NQ * BLK, 2 * BLK), lambda n: (0, 0, 0))],
        out_specs=[wide, half, half, pl.BlockSpec((8, 128), lambda n: (0, 0)), pl.BlockSpec((1, W_B), lambda n: (0, 0))],
        out_shape=[SDS((s, W_B), F32), SDS((s, 256), F32), SDS((s, 256), F32), SDS((8, 128), F32), SDS((1, W_B), F32)],
        compiler_params=_cp("arbitrary"),
    )(dy, dy, proj, proj, proj, proj, proj, ob, sinks8, gg, _attn_bias())


def _ln_parts(y1, eps=LN_EPS):
    mu = jnp.mean(y1, axis=-1, keepdims=True)
    xc = y1 - mu
    rstd = lax.rsqrt(jnp.mean(xc * xc, axis=-1, keepdims=True) + eps)
    return xc * rstd, rstd


def _conf_fwd(proj, cw, cb, lg, lb, gg, tc):
    s = proj.shape[0]
    pad = 32

    def body(ac_ref, gc_ref, ap_ref, gp_ref, cw_ref, cb_ref, lg_ref, lb_ref, gg_ref, yn_ref, y1_ref, ys_ref, sh_ref):
        i = pl.program_id(0)
        tail = ap_ref[tc - pad:tc, :] * _sig(gp_ref[tc - pad:tc, :])
        ys_ref[0:pad, :] = jnp.where(i > 0, tail, 0.0)
        ys_ref[pad:pad + tc, :] = ac_ref[...] * _sig(gc_ref[...])
        _fill_shifted(ys_ref, sh_ref)
        y1 = cb_ref[...]
        for j in range(CONV_K):
            y1 = y1 + cw_ref[j:j + 1, :] * _shifted_rows(ys_ref, sh_ref, pad - (CONV_K - 1) + j, tc)
        y1_ref[...] = y1
        xh, _ = _ln_parts(y1)
        yl = xh * lg_ref[...] + lb_ref[...]
        yc = yl * _sig(yl)
        yn_ref[...] = (yc * _rsq(yc, NORM_EPS) * gg_ref[...]).astype(BF16)

    cur = lambda c: pl.BlockSpec((tc, W_C), lambda i, c=c: (i, c))
    prev = lambda c: pl.BlockSpec((tc, W_C), lambda i, c=c: (jnp.maximum(i - 1, 0), c))
    full = lambda a: pl.BlockSpec(a.shape, lambda i: (0,) * a.ndim)
    params = [cw, cb, lg, lb, gg]
    out = pl.BlockSpec((tc, W_C), lambda i: (i, 0))
    return pl.pallas_call(
        body, name="conf_fwd", grid=(s // tc,),
        in_specs=[cur(5), cur(6), prev(5), prev(6)] + [full(a) for a in params],
        out_specs=[out, out], out_shape=[SDS((s, W_C), BF16), SDS((s, W_C), F32)],
        scratch_shapes=[pltpu.VMEM((tc + pad, W_C), F32), pltpu.VMEM((8, tc + pad, W_C), F32)],
        compiler_params=_cp("parallel"),
    )(proj, proj, proj, proj, *params)


def _conf_bwd(dy, proj, y1, cw, cb, lg, lb, gg, tc):
    s = proj.shape[0]
    nc = s // tc
    pad = 32

    def body(dy_ref, ac_ref, gc_ref, ap_ref, gp_ref, y1_ref, cw_ref, cb_ref, lg_ref, lb_ref, gg_ref,
             dp_ref, dcw_ref, dcb_ref, dlg_ref, dlb_ref, dgg_ref, ys_ref, ds_ref, nx_ref, ysh_ref, dsh_ref):
        step = pl.program_id(0)
        i = nc - 1 - step
        first = step == 0

        @pl.when(first)
        def _():
            nx_ref[...] = jnp.zeros_like(nx_ref)

        a = ac_ref[...]
        sg = _sig(gc_ref[...])
        tail = ap_ref[tc - pad:tc, :] * _sig(gp_ref[tc - pad:tc, :])
        ys_ref[0:pad, :] = jnp.where(i > 0, tail, 0.0)
        ys_ref[pad:pad + tc, :] = a * sg
        xh, rstd = _ln_parts(y1_ref[...])
        yl = xh * lg_ref[...] + lb_ref[...]
        sl = _sig(yl)
        yc = yl * sl
        dyc, dggr = _rms_bwd_rows(yc, gg_ref[...], dy_ref[...])
        _acc(dgg_ref, first, jnp.sum(dggr, axis=0, keepdims=True))
        dyl = dyc * sl * (1.0 + yl * (1.0 - sl))
        _acc(dlg_ref, first, jnp.sum(dyl * xh, axis=0, keepdims=True))
        _acc(dlb_ref, first, jnp.sum(dyl, axis=0, keepdims=True))
        dxh = dyl * lg_ref[...]
        dy1 = rstd * (dxh - jnp.mean(dxh, axis=-1, keepdims=True) - xh * jnp.mean(dxh * xh, axis=-1, keepdims=True))
        _acc(dcb_ref, first, jnp.sum(dy1, axis=0, keepdims=True))
        r32 = _row_iota((32, W_C))
        dcw = jnp.zeros((32, W_C), F32)
        _fill_shifted(ys_ref, ysh_ref)
        for j in range(CONV_K):
            tap = jnp.sum(dy1 * _shifted_rows(ys_ref, ysh_ref, pad - (CONV_K - 1) + j, tc), axis=0, keepdims=True)
            dcw = dcw + jnp.where(r32 == j, tap, 0.0)
        _acc(dcw_ref, first, dcw)
        ds_ref[0:tc, :] = dy1
        ds_ref[tc:tc + pad, :] = nx_ref[...]
        _fill_shifted(ds_ref, dsh_ref)
        dy0 = None
        for j in range(CONV_K):
            term = cw_ref[j:j + 1, :] * _shifted_rows(ds_ref, dsh_ref, CONV_K - 1 - j, tc)
            dy0 = term if dy0 is None else dy0 + term
        dp_ref[:, 0:W_C] = dy0 * sg
        dp_ref[:, W_C:2 * W_C] = dy0 * a * sg * (1.0 - sg)
        nx_ref[...] = dy1[0:pad, :]

    rev = lambda c: pl.BlockSpec((tc, W_C), lambda t, c=c: (nc - 1 - t, c))
    prev = lambda c: pl.BlockSpec((tc, W_C), lambda t, c=c: (jnp.maximum(nc - 2 - t, 0), c))
    full = lambda a: pl.BlockSpec(a.shape, lambda t: (0,) * a.ndim)
    params = [cw, cb, lg, lb, gg]
    vec = SDS((1, W_C), F32)
    outs = [SDS((s, 2 * W_C), F32), SDS((32, W_C), F32), vec, vec, vec, vec]
    return pl.pallas_call(
        body, name="conf_bwd", grid=(nc,),
        in_specs=[rev(3), rev(5), rev(6), prev(5), prev(6), rev(0)] + [full(a) for a in params],
        out_specs=[pl.BlockSpec((tc, 2 * W_C), lambda t: (nc - 1 - t, 0))]
        + [pl.BlockSpec(o.shape, lambda t: (0, 0)) for o in outs[1:]],
        out_shape=outs,
        scratch_shapes=[pltpu.VMEM((tc + pad, W_C), F32), pltpu.VMEM((tc + pad, W_C), F32), pltpu.VMEM((pad, W_C), F32),
                        pltpu.VMEM((8, tc + pad, W_C), F32), pltpu.VMEM((8, tc + pad, W_C), F32)],
        compiler_params=_cp("arbitrary"),
    )(dy, proj, proj, proj, proj, y1, *params)


def _assemble_dproj(dlru, dq, dcur, dprev, dconf):
    s = dq.shape[0]
    nb = s // BLK

    def body(dl_ref, dq_ref, dc_ref, dn_ref, df_ref, o_ref):
        n = pl.program_id(0)
        o_ref[:, 0:512] = dl_ref[...].astype(BF16)
        o_ref[:, 512:1024] = dq_ref[...].astype(BF16)
        o_ref[:, 1024:1280] = (dc_ref[...] + jnp.where(n < nb - 1, dn_ref[...], 0.0)).astype(BF16)
        o_ref[:, 1280:1792] = df_ref[...].astype(BF16)

    wide = pl.BlockSpec((BLK, 512), lambda n: (n, 0))
    return pl.pallas_call(
        body, name="assemble_dproj", grid=(nb,),
        in_specs=[wide, wide, pl.BlockSpec((BLK, 256), lambda n: (n, 0)),
                  pl.BlockSpec((BLK, 256), lambda n: (jnp.minimum(n + 1, nb - 1), 0)), wide],
        out_specs=pl.BlockSpec((BLK, P_IN), lambda n: (n, 0)), out_shape=SDS((s, P_IN), BF16),
        compiler_params=_cp("parallel"),
    )(dlru, dq, dcur, dprev, dconf)


def _loss_grad(y, t, tm, below):
    s = y.shape[0]

    def body(y_ref, t_ref, zb_ref, gb_ref, dy_ref, l_ref, dzb_ref, dgb_ref):
        first = pl.program_id(0) == 0
        err = y_ref[...] - t_ref[...]
        dy = err * (1.0 / D)
        dy_ref[...] = dy
        _acc_rows(l_ref, first, err * err)
        _post_norm_tail(dy, below[2], zb_ref, gb_ref, dzb_ref, dgb_ref, first)

    row = pl.BlockSpec((tm, D), lambda i: (i, 0))
    vec = pl.BlockSpec((1, D), lambda i: (0, 0))
    return pl.pallas_call(
        body, name="loss_grad", grid=(s // tm,), in_specs=[row, row, row, vec],
        out_specs=[row, vec, row, vec],
        out_shape=[SDS((s, D), F32), SDS((1, D), F32), SDS((s, D), BF16), SDS((1, D), F32)], compiler_params=_cp("arbitrary"),
    )(y, t, below[0], below[1])


def _block_diag(w):
    rows = [jnp.concatenate([w[h] if k == h else jnp.zeros((64, 64), w.dtype) for k in range(4)], axis=1) for h in range(4)]
    return jnp.concatenate(rows, axis=0)


def _diag_blocks(m):
    return jnp.stack([m[64 * h:64 * (h + 1), 64 * h:64 * (h + 1)] for h in range(4)])


def _layer_params(small, l):
    v = lambda name: small[name][l].reshape(1, -1)
    gg = small["group_g"][l]
    return dict(
        ffn1_pre=v("ffn1_pre_g"), ffn1_post=v("ffn1_post_g"), mix_pre=v("mix_pre_g"), mix_post=v("mix_post_g"),
        ffn2_pre=v("ffn2_pre_g"), ffn2_post=v("ffn2_post_g"), lru_cb=v("lru_conv_b"),
        wa=_block_diag(small["lru_w_a"][l]).astype(BF16), ba=v("lru_b_a"),
        wx=_block_diag(small["lru_w_x"][l]).astype(BF16), bx=v("lru_b_x"), lam=v("lru_lambda"),
        sinks8=jnp.broadcast_to(small["attn_sinks"][l][:, None], (NQ, 128)),
        conv_b=v("conv_b"), ln_g=v("conv_ln_g"), ln_b=v("conv_ln_b"),
        gg_a=gg[0:W_A].reshape(1, -1), gg_b=gg[W_A:W_A + W_B].reshape(1, -1), gg_c=gg[W_A + W_B:].reshape(1, -1),
    )


def _forward_layer(x, weights, p, tiles, deps=()):
    _, mm, _, tc, _ = tiles
    big = dict(weights("ffn1_gu", x))
    p = dict(p)
    sv = dict(x0=x)
    h1, g1, u1, a1 = _ffn_up(x, p["ffn1_pre"], big["ffn1_w_gu"], 0, mm, deps)
    big.update(weights("ffn1_down", a1))
    z1, x = _mm_rms_res(a1, big["ffn1_w_down"], 0, x, p["ffn1_post"], 0.5, mm, DFF, "ffn_down")
    sv.update(h1=h1, g1=g1, u1=u1, a1=a1, z1=z1, x1=x)
    big.update(weights("mix", x))
    p.update(lru_cw=big.pop("lru_conv_w"), conv_w=big.pop("conv_w"))
    hn, proj = _proj(x, p["mix_pre"], big["w_in"], 0, mm)
    yn_a, hl = _lru_fwd(proj, p["lru_cw"], p["lru_cb"], p["wa"], p["ba"], p["wx"], p["bx"], p["lam"], p["gg_a"], tc)
    yn_b, ob = _attn_fwd(proj, p["sinks8"], p["gg_b"])
    yn_c, y1 = _conf_fwd(proj, p["conv_w"], p["conv_b"], p["ln_g"], p["ln_b"], p["gg_c"], tc)
    ycat = jnp.concatenate([yn_a, yn_b, yn_c], axis=1)
    zo, x = _mm_rms_res(ycat, big["w_out"], 0, x, p["mix_post"], 1.0, mm, D, "mix_out")
    sv.update(hn=hn, proj=proj, hl=hl, ob=ob, y1=y1, ycat=ycat, zo=zo, x2=x)
    big.update(weights("ffn2", x))
    h2, g2, u2, a2 = _ffn_up(x, p["ffn2_pre"], big["ffn2_w_gu"], 0, mm)
    z2, x = _mm_rms_res(a2, big["ffn2_w_down"], 0, x, p["ffn2_post"], 0.5, mm, DFF, "ffn_down")
    sv.update(h2=h2, g2=g2, u2=u2, a2=a2, z2=z2, p=p, big=big)
    return x, sv


def _grad_buffers():
    empty = lambda *shape: lax.empty(shape, F32)
    return dict(ffn1_w_gu=empty(1, NSHARD, D, FH), ffn2_w_gu=empty(1, NSHARD, D, FH), ffn1_w_down=empty(1, 1, DFF, D),
                ffn2_w_down=empty(1, 1, DFF, D), w_in=empty(1, 1, D, P_IN), w_out=empty(1, 1, D, D))


def _backward_layer(dx, dzp, sv, bufs, tiles, stage, below):
    p, big = sv["p"], sv["big"]
    tm, mm, dw, tc, dh_rows = tiles
    gr = {}

    def ffn_bwd(dx, dzp, which, xin, h, g, u, a, pre, deps, below):
        dz, gr[which + "_post_g"] = dzp
        dgu = _ffn_bwd_mid(dz, big[which + "_w_down"], 0, g, u, mm, deps)
        bufs[which + "_w_down"] = _mm_tn_into(bufs[which + "_w_down"], a, dz, 0, 0, FH, D, dw, "dw_down")
        bufs[which + "_w_gu"] = _mm_tn_into(bufs[which + "_w_gu"], h, dgu, 0, 0, D, FH, dw, "dw_gate", 2, 0)
        bufs[which + "_w_gu"] = _mm_tn_into(bufs[which + "_w_gu"], h, dgu, 0, 2, D, FH, dw, "dw_up", 2, 1)
        deps = stage({n: bufs[n] for n in (which + "_w_gu", which + "_w_down")}, bufs[which + "_w_gu"], gr)
        out = _ffn_bwd_dh(dgu, big[which + "_w_gu"], 0, xin, pre, dx, dh_rows, deps, below)
        gr[which + "_pre_g"] = out[1]
        return out[0], (tuple(out[2:]) if below is not None else None)

    dx, (do, gr["mix_post_g"]) = ffn_bwd(dx, dzp, "ffn2", sv["x2"], sv["h2"], sv["g2"], sv["u2"], sv["a2"],
                                         p["ffn2_pre"], (), (sv["zo"], p["mix_post"], 1.0))
    bufs["w_out"] = _mm_tn_into(bufs["w_out"], sv["ycat"], do, 0, 0, D, D, dw, "dw_out")
    dy = _mm_nt(do, big["w_out"], 0, mm, "mix_dy")
    proj = sv["proj"]
    (dlru, dcw, gr["lru_conv_b"], dwa, gr["lru_b_a"], dwx, gr["lru_b_x"], gr["lru_lambda"], dgg_a) = _lru_bwd(
        dy, proj, sv["hl"], p["lru_cw"], p["lru_cb"], p["wa"], p["ba"], p["wx"], p["bx"], p["lam"], p["gg_a"], tc)
    dq, dcur, dprev, dsk, dgg_b = _attn_bwd(dy, proj, sv["ob"], p["sinks8"], p["gg_b"])
    dconf, dconvw, gr["conv_b"], gr["conv_ln_g"], gr["conv_ln_b"], dgg_c = _conf_bwd(
        dy, proj, sv["y1"], p["conv_w"], p["conv_b"], p["ln_g"], p["ln_b"], p["gg_c"], tc)
    dproj = _assemble_dproj(dlru, dq, dcur, dprev, dconf)
    bufs["w_in"] = _mm_tn_into(bufs["w_in"], sv["hn"], dproj, 0, 0, D, P_IN, dw, "dw_in")
    dx, gr["mix_pre_g"], dz1, dpost1 = _mm_nt_rmsbwd(dproj, big["w_in"], 0, sv["x1"], p["mix_pre"], dx, dh_rows,
                                                     (sv["z1"], p["ffn1_post"], 0.5))
    gr["lru_conv_w"] = dcw[0:LRU_K]
    gr["lru_w_a"] = _diag_blocks(dwa)
    gr["lru_w_x"] = _diag_blocks(dwx)
    gr["attn_sinks"] = dsk[:, 0]
    gr["conv_w"] = dconvw[0:CONV_K]
    gr["group_g"] = jnp.concatenate([dgg_a, dgg_b, dgg_c], axis=1)
    dx, dz_below = ffn_bwd(dx, (dz1, dpost1), "ffn1", sv["x0"], sv["h1"], sv["g1"], sv["u1"], sv["a1"], p["ffn1_pre"],
                           stage({n: bufs[n] for n in ("w_in", "w_out")}, dx, gr), below)
    return dx, dz_below, gr


def _tiles(s):
    return min(1024, s), min(1024, s), min(2048, s), min(512, s // 2), min(512, s)


HBM_SPEC = pl.BlockSpec(memory_space=pltpu.HBM)
SEM_SPEC = pl.BlockSpec(memory_space=pltpu.SEMAPHORE)
EFFECT = pltpu.SideEffectType.DATAFLOW_SIDE_EFFECTING


def _place():
    x, y, c = lax.axis_index("x"), lax.axis_index("y"), lax.axis_index("c")
    return x, y, c, [(1 - x, y), (x, 1 - y), (1 - x, 1 - y)]


def _rcopy(src, dst, send_sems, recv_sems, k, to):
    return pltpu.make_async_remote_copy(src_ref=src, dst_ref=dst, send_sem=send_sems.at[k], recv_sem=recv_sems.at[k],
                                        device_id=to, device_id_type=MESH)


def _half(rows, which):
    return pl.ds(which * (rows // 2), rows // 2)


def _place_shard(w, l, p_idx, dtype, deps=()):
    _, rows, cols = w.shape
    tr = _rows_per_block(rows, cols, 16, SUM_BLOCK_ELEMS) if rows % 16 == 0 else rows
    deps = list(deps)

    def body(p_ref, buf_ref, w_ref, *rest):
        rest[len(deps)][...] = w_ref[...].astype(dtype)

    spec = pltpu.PrefetchScalarGridSpec(
        num_scalar_prefetch=1, grid=(rows // tr,),
        in_specs=[ANY, pl.BlockSpec((None, tr, cols), lambda i, pr: (l, i, 0))] + [ANY] * len(deps),
        out_specs=pl.BlockSpec((None, None, tr, cols), lambda i, pr: (0, pr[0], i, 0)))
    shape = (1, NSHARD, rows, cols)
    return pl.pallas_call(body, name="place_shard", grid_spec=spec, out_shape=SDS(shape, dtype),
                          input_output_aliases={1: 0}, compiler_params=_cp("parallel"),
                          )(p_idx, lax.empty(shape, dtype), w, *deps)


def _run_plans(plans, refs, send_sems, recv_sems):
    cps, b0, s0 = [], 0, 0
    for plan, nb, ns in plans:
        cps += plan(refs[b0:b0 + nb], send_sems, recv_sems, s0)
        b0, s0 = b0 + nb, s0 + ns
    return cps


def _exchange(name, bufs, plans):
    n = len(bufs)
    nsem = sum(ns for _, _, ns in plans)

    def body(*refs):
        cps = _run_plans(plans, refs[n:2 * n], refs[2 * n], refs[2 * n + 1])
        for cp in cps:
            cp.start()
        for cp in cps:
            cp.wait()

    return pl.pallas_call(
        body, name=name, in_specs=[ANY] * n, out_specs=[ANY] * n, out_shape=[SDS(b.shape, b.dtype) for b in bufs],
        input_output_aliases={a: a for a in range(n)},
        scratch_shapes=[pltpu.SemaphoreType.DMA((nsem,)), pltpu.SemaphoreType.DMA((nsem,))],
    )(*bufs)


def _exchange_start(name, bufs, plans, deps=()):
    n = len(bufs)
    nsem = sum(ns for _, _, ns in plans)
    deps = list(deps)
    first_out = n + len(deps)

    def body(*refs):
        for cp in _run_plans(plans, refs[:n], refs[first_out], refs[first_out + 1]):
            cp.start()
        token = refs[first_out + 2 + n]
        token[...] = jnp.zeros_like(token)

    outs = pl.pallas_call(
        body, name=name,
        out_shape=(pltpu.SemaphoreType.DMA((nsem,)), pltpu.SemaphoreType.DMA((nsem,)),
                   *[pltpu.HBM(b.shape, b.dtype) for b in bufs], SDS((8, 128), F32)),
        in_specs=[HBM_SPEC] * n + [ANY] * len(deps),
        out_specs=(SEM_SPEC, SEM_SPEC, *[HBM_SPEC] * n, pl.BlockSpec(memory_space=pltpu.VMEM)),
        input_output_aliases={a: 2 + a for a in range(n)},
        compiler_params=pltpu.CompilerParams(has_side_effects=EFFECT),
    )(*[pltpu.with_memory_space_constraint(b, pltpu.HBM) for b in bufs], *deps)
    return outs[0], outs[1], list(outs[2:2 + n]), outs[2 + n]


def _exchange_wait(name, send_sems, recv_sems, bufs, plans, after):
    n = len(bufs)

    def body(*refs):
        for cp in _run_plans(plans, refs[:n], refs[n], refs[n + 1]):
            cp.wait_send()
            cp.wait_recv()

    return pl.pallas_call(
        body, name=name, out_shape=[pltpu.HBM(b.shape, b.dtype) for b in bufs],
        in_specs=[HBM_SPEC] * n + [SEM_SPEC, SEM_SPEC, ANY], out_specs=[HBM_SPEC] * n,
        input_output_aliases={a: a for a in range(n)},
        compiler_params=pltpu.CompilerParams(has_side_effects=EFFECT),
    )(*bufs, send_sems, recv_sems, after)


def _plan_gather(refs, send_sems, recv_sems, base):
    x, y, c, chips = _place()
    p = 2 * x + y
    return [_rcopy(r.at[0, p], r.at[0, p], send_sems, recv_sems, base + 3 * a + j, (*chip, c))
            for a, r in enumerate(refs) for j, chip in enumerate(chips)]


def _plan_gather_half(refs, send_sems, recv_sems, base):
    x, y, c, chips = _place()
    p = 2 * x + y
    return [_rcopy(r.at[0, p, _half(r.shape[2], c)], r.at[0, p, _half(r.shape[2], c)], send_sems, recv_sems,
                   base + 3 * a + j, (*chip, c)) for a, r in enumerate(refs) for j, chip in enumerate(chips)]


def _plan_forward_half(refs, send_sems, recv_sems, base):
    x, y, c, chips = _place()
    cps = []
    for a, r in enumerate(refs):
        for j, chip in enumerate(chips):
            blk = r.at[0, 2 * chip[0] + chip[1], _half(r.shape[2], c)]
            cps.append(_rcopy(blk, blk, send_sems, recv_sems, base + 3 * a + j, (x, y, 1 - c)))
    return cps


def _plan_pair_exchange(refs, send_sems, recv_sems, base):
    x, y, c, _ = _place()
    n = len(refs) // 2
    return [_rcopy(refs[a].at[:, _half(refs[a].shape[1], 1 - c)], refs[n + a], send_sems, recv_sems, base + a,
                   (x, y, 1 - c)) for a in range(n)]


def _plan_chip_exchange(refs, send_sems, recv_sems, base):
    x, y, c, chips = _place()
    n = len(refs) // 2
    return [_rcopy(refs[a].at[2 * chip[0] + chip[1]], refs[n + a].at[j], send_sems, recv_sems, base + 3 * a + j,
                   (*chip, c)) for a in range(n) for j, chip in enumerate(chips)]


def _plan_pair_share(refs, send_sems, recv_sems, base):
    x, y, c, _ = _place()
    return [_rcopy(r.at[_half(r.shape[0], c)], r.at[_half(r.shape[0], c)], send_sems, recv_sems, base + a,
                   (x, y, 1 - c)) for a, r in enumerate(refs)]


def _plan_small_gather(refs, send_sems, recv_sems, base):
    x, y, c, _ = _place()
    me = 4 * x + 2 * y + c
    cps = []
    for m in range(1, NDEV):
        peer = (1 - x if m & 4 else x, 1 - y if m & 2 else y, 1 - c if m & 1 else c)
        cps.append(_rcopy(refs[0], refs[1].at[me], send_sems, recv_sems, base + m - 1, peer))
    return cps


def _sum_small(buf, gathered):
    def body(buf_ref, g_ref, o_ref):
        x, y, c, _ = _place()
        me = 4 * x + 2 * y + c
        total = jnp.where(me == 0, buf_ref[...], g_ref[0])
        for dev in range(1, NDEV):
            total = total + jnp.where(me == dev, buf_ref[...], g_ref[dev])
        o_ref[...] = total

    vm = pl.BlockSpec(memory_space=pltpu.VMEM)
    return pl.pallas_call(body, name="sum_small", in_specs=[vm, vm], out_specs=vm, out_shape=SDS(buf.shape, F32),
                          compiler_params=pltpu.CompilerParams(vmem_limit_bytes=VMEM_LIMIT))(buf, gathered)


BLOCK_ELEMS = 512 * 1024
SUM_BLOCK_ELEMS = 1024 * 1024


def _rows_per_block(rows, cols, mult, limit=BLOCK_ELEMS):
    best = None
    for tr in range(mult, rows + 1, mult):
        if rows % tr == 0 and tr * cols <= limit:
            best = tr
    assert best is not None, (rows, cols)
    return best


def _pair_sum(g, r, c_idx):
    nq, rows, cols = g.shape
    half = rows // 2
    tr = _rows_per_block(half, cols, 16, SUM_BLOCK_ELEMS)
    nb = half // tr

    def body(c_ref, g_ref, r_ref, t_ref):
        t_ref[...] = (g_ref[...] + r_ref[...]).astype(BF16)

    blk = pl.BlockSpec((None, tr, cols), lambda q, i, cr: (q, i, 0))
    spec = pltpu.PrefetchScalarGridSpec(
        num_scalar_prefetch=1, grid=(nq, nb),
        in_specs=[pl.BlockSpec((None, tr, cols), lambda q, i, cr: (q, cr[0] * nb + i, 0)), blk], out_specs=blk)
    return pl.pallas_call(body, name="grad_pair_sum", grid_spec=spec, out_shape=SDS((nq, half, cols), BF16),
                          compiler_params=_cp("parallel", "parallel"))(c_idx, g, r)


def _chip_sum(g, r, rr, cp_idx):
    _, rows, cols = g.shape
    half = rows // 2
    tr = _rows_per_block(half, cols, 16, SUM_BLOCK_ELEMS)
    nb = half // tr

    def body(cp_ref, buf_ref, g_ref, r_ref, rr_ref, o_ref):
        o_ref[...] = ((g_ref[...] + r_ref[...]) + rr_ref[0].astype(F32) + rr_ref[1].astype(F32) + rr_ref[2].astype(F32))

    spec = pltpu.PrefetchScalarGridSpec(
        num_scalar_prefetch=1, grid=(nb,),
        in_specs=[ANY, pl.BlockSpec((None, tr, cols), lambda i, cp: (cp[1], cp[0] * nb + i, 0)),
                  pl.BlockSpec((None, tr, cols), lambda i, cp: (cp[1], i, 0)),
                  pl.BlockSpec((3, tr, cols), lambda i, cp: (0, i, 0))],
        out_specs=pl.BlockSpec((tr, cols), lambda i, cp: (cp[0] * nb + i, 0)))
    return pl.pallas_call(body, name="grad_chip_sum", grid_spec=spec, out_shape=SDS((rows, cols), F32),
                          input_output_aliases={1: 0}, compiler_params=_cp("parallel"),
                          )(cp_idx, lax.empty((rows, cols), F32), g, r, rr)


def _adamw_math(w, g, m, v):
    mn = ADAM_B1 * m + (1.0 - ADAM_B1) * g
    vn = ADAM_B2 * v + (1.0 - ADAM_B2) * (g * g)
    m_hat = mn / (1.0 - ADAM_B1 ** ADAM_STEP)
    v_hat = vn / (1.0 - ADAM_B2 ** ADAM_STEP)
    return -ADAM_LR * (m_hat / (jnp.sqrt(v_hat) + ADAM_EPS) + ADAM_WD * w), mn, vn


def _adamw_layer(w, g, m, v, l, outs, deps=()):
    _, rows, cols = w.shape
    tr = _rows_per_block(rows, cols, 8)
    deps = list(deps)

    def body(*refs):
        w_ref, g_ref, m_ref, v_ref = refs[4:8]
        go_ref, d_ref, mo_ref, vo_ref = refs[8 + len(deps):]
        gg = g_ref[...]
        go_ref[...] = gg
        d_ref[...], mo_ref[...], vo_ref[...] = _adamw_math(w_ref[...], gg, m_ref[...], v_ref[...])

    blk = pl.BlockSpec((None, tr, cols), lambda i: (l, i, 0))
    return pl.pallas_call(
        body, name="adamw_layer", grid=(rows // tr,),
        in_specs=[ANY] * 4 + [blk, pl.BlockSpec((tr, cols), lambda i: (i, 0)), blk, blk] + [ANY] * len(deps),
        out_specs=[blk] * 4, out_shape=[SDS(w.shape, F32)] * 4, input_output_aliases={k: k for k in range(4)},
        compiler_params=_cp("parallel"))(*outs, w, g, m, v, *deps)


def _adamw_small(ws, gs, ms, vs, deps=()):
    n = len(ws)
    deps = list(deps)

    def body(*refs):
        refs = refs[:4 * n] + refs[4 * n + len(deps):]
        w, g, m, v, d_out, m_out, v_out = (refs[k * n:(k + 1) * n] for k in range(7))
        for k in range(n):
            d_out[k][...], m_out[k][...], v_out[k][...] = _adamw_math(w[k][...], g[k][...], m[k][...], v[k][...])

    vm = pl.BlockSpec(memory_space=pltpu.VMEM)
    outs = pl.pallas_call(body, name="adamw_small", in_specs=[vm] * (4 * n) + [ANY] * len(deps), out_specs=[vm] * (3 * n),
                          out_shape=[SDS(w.shape, F32) for w in ws] * 3,
                          compiler_params=pltpu.CompilerParams(vmem_limit_bytes=VMEM_LIMIT))(*ws, *gs, *ms, *vs, *deps)
    return outs[:n], outs[n:2 * n], outs[2 * n:]


_WEIGHTS = ["ffn1_pre_g", "ffn1_w_gu", "ffn1_w_down", "ffn1_post_g", "mix_pre_g", "w_in", "lru_conv_w", "lru_conv_b",
            "lru_w_a", "lru_b_a", "lru_w_x", "lru_b_x", "lru_lambda", "attn_sinks", "conv_w", "conv_b", "conv_ln_g",
            "conv_ln_b", "group_g", "w_out", "mix_post_g", "ffn2_pre_g", "ffn2_w_gu", "ffn2_w_down", "ffn2_post_g"]
_INPUTS = ["x"] + _WEIGHTS + ["loss_target"] + ["m_" + n for n in _WEIGHTS] + ["v_" + n for n in _WEIGHTS]
_BIG = ["ffn1_w_gu", "ffn1_w_down", "w_in", "w_out", "ffn2_w_gu", "ffn2_w_down"]
_SMALL_SHARDED = ["lru_conv_w", "conv_w"]
_SMALL_REPL = [n for n in _WEIGHTS if n not in _BIG and n not in _SMALL_SHARDED]

PACK_TILE = 8 * 128


def _pack(arrs):
    parts = []
    for a in arrs:
        flat = a.reshape(-1)
        parts.append(jnp.pad(flat, (0, -flat.shape[0] % PACK_TILE)).reshape(-1, 128))
    return jnp.concatenate(parts, axis=0)


def _unpack(buf, shapes):
    out, row = [], 0
    for shp in shapes:
        size = math.prod(shp)
        nrow = -(-size // PACK_TILE) * 8
        out.append(buf[row:row + nrow].reshape(-1)[:size].reshape(shp))
        row += nrow
    return out


def _unshard_cols(a):
    return a.transpose(0, 2, 1, 3).reshape(1, a.shape[2], NSHARD * a.shape[3])


_GROUPS = dict(ffn1_gu=["ffn1_w_gu"], ffn1_down=["ffn1_w_down"], mix=["w_in", "w_out", "lru_conv_w", "conv_w"],
               ffn2=["ffn2_w_gu", "ffn2_w_down"])


def _full_weights(group, gathered):
    g = dict(zip(_GROUPS[group], gathered))
    if group == "mix":
        return dict(w_in=_unshard_cols(g["w_in"]), w_out=g["w_out"].reshape(1, D, D),
                    lru_conv_w=_unshard_cols(g["lru_conv_w"])[0], conv_w=_unshard_cols(g["conv_w"])[0])
    return {n: (a.reshape(1, DFF, D) if n.endswith("w_down") else a) for n, a in g.items()}


def _by_shard(name, buf):
    if name.endswith("w_gu"):
        return buf[0]
    if name == "w_in":
        return buf.reshape(D, NSHARD, P_IN // NSHARD).transpose(1, 0, 2)
    return buf.reshape(NSHARD, buf.shape[2] // NSHARD, buf.shape[3])


class _Reducer:
    PLANS = (_plan_pair_exchange, _plan_chip_exchange, _plan_pair_share)

    def __init__(self, keys, gs, c_idx, cp_idx):
        self.keys, self.gs, self.c_idx, self.cp_idx = keys, gs, c_idx, cp_idx
        self.n = len(gs)
        self.step = 0
        self.result = None

    def inputs(self):
        n = self.n
        if self.step == 0:
            bufs = self.gs + [lax.empty((NSHARD, g.shape[1] // 2, g.shape[2]), F32) for g in self.gs]
        elif self.step == 1:
            ts = [_pair_sum(g, r, self.c_idx) for g, r in zip(self.gs, self.rs)]
            bufs = ts + [lax.empty((3,) + t.shape[1:], BF16) for t in ts]
        else:
            bufs = [_chip_sum(g, r, rr, self.cp_idx) for g, r, rr in zip(self.gs, self.rs, self.rrs)]
        return bufs, (self.PLANS[self.step], len(bufs), (n, 3 * n, n)[self.step])

    def absorb(self, done):
        n = self.n
        if self.step == 0:
            self.gs, self.rs = done[:n], done[n:]
        elif self.step == 1:
            self.rrs = done[n:]
        else:
            self.result = dict(zip(self.keys, done))
        self.step += 1


class _SmallGather:
    def __init__(self, buf):
        self.buf, self.step, self.result, self.gathered = buf, 0, {}, None

    def inputs(self):
        return [self.buf, jnp.zeros((NDEV,) + self.buf.shape, F32)], (_plan_small_gather, 2, NDEV - 1)

    def absorb(self, done):
        self.buf, self.gathered = done
        self.step = 3


class _ReducePipeline:
    def __init__(self, c_idx, cp_idx):
        self.c_idx, self.cp_idx = c_idx, cp_idx
        self.reducers, self.flying, self.calls = [], None, 0

    def add(self, layer, done):
        if done:
            keys = [(layer, n) for n in done]
            self.reducers.append(_Reducer(keys, [_by_shard(n, b) for n, b in done.items()], self.c_idx, self.cp_idx))

    def _next(self):
        active = [r for r in self.reducers if r.step < 3]
        bufs, plans = [], []
        for r in active:
            b, triple = r.inputs()
            bufs += b
            plans.append(triple)
        self.calls += 1
        return active, bufs, plans, "grad_exchange%d" % self.calls

    def _absorb(self, active, plans, done):
        at = 0
        for r, (_, nb, _) in zip(active, plans):
            r.absorb(done[at:at + nb])
            at += nb

    def _land(self, after):
        if self.flying is not None:
            active, plans, name, send_sems, recv_sems, bufs = self.flying
            self._absorb(active, plans, _exchange_wait(name + "_wait", send_sems, recv_sems, bufs, plans, after))
            self.flying = None

    def hook(self, after):
        self._land(after)
        active, bufs, plans, name = self._next()
        if not active:
            return []
        send_sems, recv_sems, bufs, token = _exchange_start(name + "_start", bufs, plans)
        self.flying = (active, plans, name, send_sems, recv_sems, bufs)
        return [token]

    def available(self):
        out = {}
        for r in self.reducers:
            if r.step == 3:
                out.update(r.result)
        return out

    def finish(self, after):
        self._land(after)
        while True:
            active, bufs, plans, name = self._next()
            if not active:
                break
            self._absorb(active, plans, _exchange(name, bufs, plans))
        out = {}
        for r in self.reducers:
            out.update(r.result)
        return out


def kernel(*args):
    d = dict(zip(_INPUTS, args, strict=True))
    xi, yi, ci = lax.axis_index("x"), lax.axis_index("y"), lax.axis_index("c")
    p = 2 * xi + yi
    c_idx = jnp.reshape(ci, (1,)).astype(jnp.int32)
    p_idx = jnp.reshape(p, (1,)).astype(jnp.int32)
    cp_idx = jnp.stack([ci, p]).astype(jnp.int32)
    x, target = d["x"][0], d["loss_target"][0]
    tiles = _tiles(x.shape[0])

    groups = [(l, grp) for l in range(DEPTH) for grp in _GROUPS]
    place = lambda l, grp, deps: [_place_shard(d[n], l, p_idx, BF16 if n in _BIG else F32, deps) for n in _GROUPS[grp]]
    first = place(*groups[0], ())
    half_plans = [(_plan_gather_half, len(first), 3 * len(first))]
    first_sems = _exchange_start("gather_first_start", first, half_plans)
    tokens = [first_sems[3]]
    flying = {}
    for l, grp in groups[1:]:
        placed = place(l, grp, tokens[:1])
        plans = [(_plan_gather, len(placed), 3 * len(placed))]
        send_sems, recv_sems, bufs, token = _exchange_start("gather_l%d_%s_start" % (l, grp), placed, plans, tokens[-1:])
        flying[l, grp] = (send_sems, recv_sems, bufs, plans)
        tokens.append(token)
    first = _exchange_wait("gather_first_wait", first_sems[0], first_sems[1], first_sems[2], half_plans, tokens[-1])
    ready = {groups[0]: _exchange("gather_first_forward", first, [(_plan_forward_half, len(first), 3 * len(first))])}

    def weights_of(l):
        def weights(grp, after):
            if (l, grp) not in ready:
                send_sems, recv_sems, bufs, plans = flying[l, grp]
                ready[l, grp] = _exchange_wait("gather_l%d_%s_wait" % (l, grp), send_sems, recv_sems, bufs, plans, after)
            return _full_weights(grp, ready[l, grp])
        return weights

    small = {n: d[n] for n in _SMALL_REPL}
    x1, sv0 = _forward_layer(x, weights_of(0), _layer_params(small, 0), tiles)
    x2, sv1 = _forward_layer(x1, weights_of(1), _layer_params(small, 1), tiles)
    dx, lcols, *dzp = _loss_grad(x2, target, tiles[0], (sv1["z2"], sv1["p"]["ffn2_post"], 0.5))

    pipe = _ReducePipeline(c_idx, cp_idx)
    sgrads = [None] * DEPTH
    order = _SMALL_REPL + _SMALL_SHARDED
    early_names = [n for n in order if n != "ffn1_pre_g"]
    natural = lambda n, g: g.reshape(d[n].shape[1:]) if n in _SMALL_REPL else g
    loss_part = jnp.pad((0.5 / D) * jnp.sum(lcols).reshape(1), (0, 127))
    early = {}
    for l, sv, below in ((1, sv1, (sv0["z2"], sv0["p"]["ffn2_post"], 0.5)), (0, sv0, None)):
        bufs = _grad_buffers()

        def stage(done, dx, gr, l=l):
            pipe.add(l, done)
            if l == 0 and "ffn1_w_gu" in done:
                stacked = [jnp.stack([natural(n, gr[n]), natural(n, sgrads[1][n])]) for n in early_names]
                early["shapes"] = [(128,)] + [a.shape for a in stacked] + [(D,)]
                early["gather"] = _SmallGather(_pack([loss_part] + stacked + [sgrads[1]["ffn1_pre_g"].reshape(-1)]))
                pipe.reducers.append(early["gather"])
            return pipe.hook(dx)

        dx, dzp, sgrads[l] = _backward_layer(dx, tuple(dzp), sv, bufs, tiles, stage, below)
    grad_x = dx
    late_gather = _SmallGather(_pack([sgrads[0]["ffn1_pre_g"].reshape(-1)]))
    pipe.reducers.append(late_gather)

    results = {n: tuple(lax.empty(d[n].shape, F32) for _ in range(4)) for n in _BIG}
    applied = set()

    def apply_ready(deps, last):
        for (l, n), g in pipe.available().items():
            if (l, n) not in applied:
                results[n] = _adamw_layer(d[n], g, d["m_" + n], d["v_" + n], l, results[n], deps)
                applied.add((l, n))
                last = results[n][1]
                deps = [last]
        return last

    last = apply_ready(pipe.hook(grad_x), grad_x)
    token = pipe.hook(last)
    summed = _unpack(_sum_small(early["gather"].buf, early["gather"].gathered), early["shapes"])
    late = _unpack(_sum_small(late_gather.buf, late_gather.gathered), [(D,)])[0]
    loss = summed[0][0]
    grads = {"ffn1_pre_g": jnp.stack([late, summed[-1]])}
    for n, g in zip(early_names, summed[1:-1]):
        if n in _SMALL_SHARDED:
            g = lax.dynamic_slice_in_dim(g, p * (g.shape[2] // NSHARD), g.shape[2] // NSHARD, axis=2)
        grads[n] = g
    delta, new_m, new_v = {}, {}, {}
    small_out = _adamw_small([d[n] for n in order], [grads[n] for n in order], [d["m_" + n] for n in order],
                             [d["v_" + n] for n in order], token)
    for out, res in zip((delta, new_m, new_v), small_out):
        out.update(zip(order, res))
    last = apply_ready([small_out[0][0]], small_out[0][0])
    pipe.finish(last)
    apply_ready((), last)
    for n in _BIG:
        grads[n], delta[n], new_m[n], new_v[n] = results[n]

    return (loss, grad_x[None], *[grads[n] for n in _WEIGHTS], *[delta[n] for n in _WEIGHTS],
            *[new_m[n] for n in _WEIGHTS], *[new_v[n] for n in _WEIGHTS])
```

```python
import math

import jax
import jax.numpy as jnp
import numpy as np
from jax import lax
from jax.experimental import pallas as pl
from jax.experimental.pallas import tpu as pltpu

F32 = jnp.float32
BF16 = jnp.bfloat16
SDS = jax.ShapeDtypeStruct

D = 1024
DFF = 2816
FH = DFF // 2
DEPTH = 2
W_A = 256
W_B = 512
W_C = 256
NQ = 8
HD = 64
BLK = 128
ATT_NB_FWD = 1
ATT_NB_BWD = 8
P_IN = 1792
LRU_K = 4
CONV_K = 31
LRU_C = 8.0
NORM_EPS = 1e-6
LN_EPS = 1e-5
NEG_BIG = -1e30
SCALE = 1.0 / math.sqrt(HD)

ADAM_LR = 0.001
ADAM_B1 = 0.9
ADAM_B2 = 0.999
ADAM_EPS = 1e-08
ADAM_WD = 0.01
ADAM_STEP = 10

VMEM_LIMIT = 60 * 1024 * 1024
NSHARD = 4
NDEV = 8

TN = (((0,), (0,)), ((), ()))
NT = (((1,), (1,)), ((), ()))

MESH = pl.DeviceIdType.MESH
ANY = pl.BlockSpec(memory_space=pl.ANY)


def _cp(*sem):
    return pltpu.CompilerParams(dimension_semantics=sem if sem else None, vmem_limit_bytes=VMEM_LIMIT)


def _rsq(x, eps):
    return lax.rsqrt(jnp.mean(x * x, axis=-1, keepdims=True) + eps)


def _rms_bwd_rows(x, g, dy):
    r = _rsq(x, NORM_EPS)
    xh = x * r
    dyg = dy * g
    dx = r * (dyg - xh * jnp.mean(dyg * xh, axis=-1, keepdims=True))
    return dx, dy * xh


def _sig(x):
    return jax.nn.sigmoid(x)


def _post_norm_tail(dx, c, z_ref, g_ref, dz_ref, dg_ref, first):
    dz, dgr = _rms_bwd_rows(z_ref[...], g_ref[...], c * dx)
    dz_ref[...] = dz.astype(BF16)
    _acc_rows(dg_ref, first, dgr)


def _ffn_up(x, pre_g, wgu, l, tm, deps=()):
    s = x.shape[0]
    deps = list(deps)

    def body(x_ref, g_ref, wg_ref, wu_ref, *rest):
        h_ref, go_ref, uo_ref, a_ref = rest[len(deps):]

        @pl.when(pl.program_id(1) == 0)
        def _():
            xf = x_ref[...]
            h_ref[...] = (xf * _rsq(xf, NORM_EPS) * g_ref[...]).astype(BF16)

        h = h_ref[...]
        gg = jnp.dot(h, wg_ref[...], preferred_element_type=F32)
        uu = jnp.dot(h, wu_ref[...], preferred_element_type=F32)
        sg = _sig(gg)
        silu = gg * sg
        go_ref[...] = (uu * (sg * (1.0 + gg * (1.0 - sg)))).astype(BF16)
        uo_ref[...] = silu.astype(BF16)
        a_ref[...] = (silu * uu).astype(BF16)

    wide = pl.BlockSpec((tm, FH), lambda i, j: (i, j))
    return pl.pallas_call(
        body, name="ffn_up", grid=(s // tm, 2),
        in_specs=[pl.BlockSpec((tm, D), lambda i, j: (i, 0)), pl.BlockSpec((1, D), lambda i, j: (0, 0)),
                  pl.BlockSpec((None, None, D, FH), lambda i, j: (l, j, 0, 0)),
                  pl.BlockSpec((None, None, D, FH), lambda i, j: (l, j + 2, 0, 0))] + [ANY] * len(deps),
        out_specs=[pl.BlockSpec((tm, D), lambda i, j: (i, 0)), wide, wide, wide],
        out_shape=[SDS((s, D), BF16), SDS((s, DFF), BF16), SDS((s, DFF), BF16), SDS((s, DFF), BF16)],
        compiler_params=_cp("parallel", "arbitrary"),
    )(x, pre_g, wgu, wgu, *deps)


def _mm_rms_res(a, w, l, x, g, c, tm, tk, name):
    s, k_dim = a.shape
    nk = k_dim // tk

    def body(a_ref, w_ref, x_ref, g_ref, z_ref, x1_ref):
        k = pl.program_id(1)
        p = jnp.dot(a_ref[...], w_ref[...], preferred_element_type=F32)

        @pl.when(k == 0)
        def _():
            z_ref[...] = p

        @pl.when(k > 0)
        def _():
            z_ref[...] += p

        @pl.when(k == nk - 1)
        def _():
            z = z_ref[...]
            x1_ref[...] = x_ref[...] + c * (z * _rsq(z, NORM_EPS) * g_ref[...])

    row = pl.BlockSpec((tm, D), lambda i, k: (i, 0))
    return pl.pallas_call(
        body, name=name, grid=(s // tm, nk),
        in_specs=[pl.BlockSpec((tm, tk), lambda i, k: (i, k)), pl.BlockSpec((None, tk, D), lambda i, k: (l, k, 0)),
                  row, pl.BlockSpec((1, D), lambda i, k: (0, 0))],
        out_specs=[row, row],
        out_shape=[SDS((s, D), F32), SDS((s, D), F32)],
        compiler_params=_cp("parallel", "arbitrary"),
    )(a, w, x, g)


def _ffn_bwd_mid(dz, wd, l, dadg, dadu, tm, deps=()):
    s = dz.shape[0]
    deps = list(deps)

    def body(dz_ref, wd_ref, g_ref, u_ref, *rest):
        dgu_ref = rest[len(deps)]
        da = lax.dot_general(dz_ref[...], wd_ref[...], NT, preferred_element_type=F32)
        dgu_ref[:, 0:FH] = (da * g_ref[...].astype(F32)).astype(BF16)
        dgu_ref[:, FH:2 * FH] = (da * u_ref[...].astype(F32)).astype(BF16)

    wide = pl.BlockSpec((tm, FH), lambda i, j: (i, j))
    return pl.pallas_call(
        body, name="ffn_bwd_mid", grid=(s // tm, 2),
        in_specs=[pl.BlockSpec((tm, D), lambda i, j: (i, 0)), pl.BlockSpec((None, FH, D), lambda i, j: (l, j, 0)), wide, wide]
        + [ANY] * len(deps),
        out_specs=pl.BlockSpec((tm, 2 * FH), lambda i, j: (i, j)),
        out_shape=SDS((s, 2 * DFF), BF16),
        compiler_params=_cp("parallel", "arbitrary"),
    )(dz, wd, dadg, dadu, *deps)


def _ffn_bwd_dh(dgu, wgu, l, x, pre_g, dx1, tm, deps=(), below=None):
    s = x.shape[0]
    deps = list(deps)
    tail = [] if below is None else list(below[:2])

    def body(dgu_ref, w_hbm, x_ref, g_ref, dx1_ref, *rest):
        rest = rest[len(deps):]
        tail_in, (dx_ref, dgp_ref), rest = rest[:len(tail)], rest[len(tail):len(tail) + 2], rest[len(tail) + 2:]
        tail_out, (wcat_ref, sems) = rest[:len(tail)], rest[len(tail):]
        i = pl.program_id(0)

        @pl.when(i == 0)
        def _():
            cps = [pltpu.make_async_copy(w_hbm.at[l, q], wcat_ref.at[:, pl.ds((2 * (q % 2) + q // 2) * FH, FH)], sems.at[q])
                   for q in range(NSHARD)]
            for cp in cps:
                cp.start()
            for cp in cps:
                cp.wait()

        dh = lax.dot_general(dgu_ref[...], wcat_ref[...], NT, preferred_element_type=F32)
        dx, dgr = _rms_bwd_rows(x_ref[...], g_ref[...], dh)
        dx = dx1_ref[...] + dx
        dx_ref[...] = dx
        _acc_rows(dgp_ref, i == 0, dgr)
        if tail:
            _post_norm_tail(dx, below[2], *tail_in, *tail_out, i == 0)

    row = pl.BlockSpec((tm, D), lambda i: (i, 0))
    vec = pl.BlockSpec((1, D), lambda i: (0, 0))
    return pl.pallas_call(
        body, name="ffn_bwd_dh", grid=(s // tm,),
        in_specs=[pl.BlockSpec((tm, 2 * DFF), lambda i: (i, 0)), ANY, row, vec, row] + [ANY] * len(deps) + [row, vec][:len(tail)],
        out_specs=[row, vec] + [row, vec][:len(tail)],
        out_shape=[SDS((s, D), F32), SDS((1, D), F32)] + [SDS((s, D), BF16), SDS((1, D), F32)][:len(tail)],
        scratch_shapes=[pltpu.VMEM((D, 2 * DFF), BF16), pltpu.SemaphoreType.DMA((NSHARD,))],
        compiler_params=_cp("arbitrary"),
    )(dgu, wgu, x, pre_g, dx1, *deps, *tail)


def _mm_tn_into(buf, a, b, l, joff, tka, tn, ts, name, bstride=1, boff=0):
    s, ka = a.shape
    n = b.shape[1] // bstride

    def body(buf_ref, a_ref, b_ref, o_ref):
        p = lax.dot_general(a_ref[...], b_ref[...], TN, preferred_element_type=F32)

        @pl.when(pl.program_id(2) == 0)
        def _():
            o_ref[...] = p

        @pl.when(pl.program_id(2) > 0)
        def _():
            o_ref[...] += p

    return pl.pallas_call(
        body, name=name, grid=(ka // tka, n // tn, s // ts),
        in_specs=[pl.BlockSpec(memory_space=pl.ANY),
                  pl.BlockSpec((ts, tka), lambda ia, j, t: (t, ia)),
                  pl.BlockSpec((ts, tn), lambda ia, j, t: (t, bstride * j + boff))],
        out_specs=pl.BlockSpec((None, None, tka, tn), lambda ia, j, t: (l, joff + j, ia, 0)),
        out_shape=SDS(buf.shape, F32), input_output_aliases={0: 0},
        compiler_params=_cp("parallel", "parallel", "arbitrary"),
    )(buf, a, b)


def _proj(x, g, w_in, l, tm):
    s = x.shape[0]

    def body(x_ref, g_ref, w_ref, h_ref, p_ref):
        xf = x_ref[...]
        h = (xf * _rsq(xf, NORM_EPS) * g_ref[...]).astype(BF16)
        h_ref[...] = h
        p_ref[...] = jnp.dot(h, w_ref[...], preferred_element_type=F32)

    return pl.pallas_call(
        body, name="proj", grid=(s // tm,),
        in_specs=[pl.BlockSpec((tm, D), lambda i: (i, 0)), pl.BlockSpec((1, D), lambda i: (0, 0)),
                  pl.BlockSpec((None, D, P_IN), lambda i: (l, 0, 0))],
        out_specs=[pl.BlockSpec((tm, D), lambda i: (i, 0)), pl.BlockSpec((tm, P_IN), lambda i: (i, 0))],
        out_shape=[SDS((s, D), BF16), SDS((s, P_IN), F32)],
        compiler_params=_cp("parallel"),
    )(x, g, w_in)


def _mm_nt(a, w, l, tm, name):
    s, k_dim = a.shape
    n = w.shape[1]

    def body(a_ref, w_ref, o_ref):
        o_ref[...] = lax.dot_general(a_ref[...], w_ref[...], NT, preferred_element_type=F32)

    return pl.pallas_call(
        body, name=name, grid=(s // tm,),
        in_specs=[pl.BlockSpec((tm, k_dim), lambda i: (i, 0)), pl.BlockSpec((None, n, k_dim), lambda i: (l, 0, 0))],
        out_specs=pl.BlockSpec((tm, n), lambda i: (i, 0)),
        out_shape=SDS((s, n), F32), compiler_params=_cp("parallel"),
    )(a, w)


def _mm_nt_rmsbwd(dp, w_in, l, x, g, dx1, tm, below):
    s = x.shape[0]

    def body(dp_ref, w_ref, x_ref, g_ref, dx1_ref, zb_ref, gb_ref, dx_ref, dg_ref, dzb_ref, dgb_ref):
        first = pl.program_id(0) == 0
        dh = lax.dot_general(dp_ref[...], w_ref[...], NT, preferred_element_type=F32)
        dx, dgr = _rms_bwd_rows(x_ref[...], g_ref[...], dh)
        dx = dx1_ref[...] + dx
        dx_ref[...] = dx
        _acc_rows(dg_ref, first, dgr)
        _post_norm_tail(dx, below[2], zb_ref, gb_ref, dzb_ref, dgb_ref, first)

    row = pl.BlockSpec((tm, D), lambda i: (i, 0))
    vec = pl.BlockSpec((1, D), lambda i: (0, 0))
    return pl.pallas_call(
        body, name="mix_bwd_dx", grid=(s // tm,),
        in_specs=[pl.BlockSpec((tm, P_IN), lambda i: (i, 0)), pl.BlockSpec((None, D, P_IN), lambda i: (l, 0, 0)), row, vec, row,
                  row, vec],
        out_specs=[row, vec, row, vec],
        out_shape=[SDS((s, D), F32), SDS((1, D), F32), SDS((s, D), BF16), SDS((1, D), F32)],
        compiler_params=_cp("arbitrary"),
    )(dp, w_in, x, g, dx1, below[0], below[1])


def _row_iota(shape):
    return lax.broadcasted_iota(jnp.int32, shape, 0)


def _lru_gates(xc, wa_ref, ba_ref, wx_ref, bx_ref, lam_ref):
    xb = xc.astype(BF16)
    r = _sig(jnp.dot(xb, wa_ref[...], preferred_element_type=F32) + ba_ref[...])
    ig = _sig(jnp.dot(xb, wx_ref[...], preferred_element_type=F32) + bx_ref[...])
    nl = -lam_ref[...]
    sp = jnp.maximum(nl, 0.0) + jnp.log(1.0 + jnp.exp(-jnp.abs(nl)))
    log_a = -LRU_C * r * sp
    a = jnp.exp(log_a)
    mlt = jnp.sqrt((1.0 + a * a) * jnp.tanh(-log_a))
    return r, ig, a, mlt, sp


def _conv_taps(src_ref, w_ref, k_taps, pad, tc):
    acc = None
    for j in range(k_taps):
        term = w_ref[j:j + 1, :] * src_ref[pl.ds(pad - (k_taps - 1) + j, tc), :]
        acc = term if acc is None else acc + term
    return acc


def _fill_shifted(src_ref, sh_ref):
    n = src_ref.shape[0] - 8
    for s in range(1, 8):
        sh_ref[s, 0:n, :] = src_ref[pl.ds(s, n), :]


def _shifted_rows(src_ref, sh_ref, offset, tc):
    if offset % 8 == 0:
        return src_ref[pl.ds(offset, tc), :]
    return sh_ref[offset % 8, pl.ds(offset - offset % 8, tc), :]


def _gelu_parts(x):
    c0 = math.sqrt(2.0 / math.pi)
    inner = c0 * (x + 0.044715 * x * x * x)
    t = jnp.tanh(inner)
    gl = 0.5 * x * (1.0 + t)
    dgl = 0.5 * (1.0 + t) + 0.5 * x * (1.0 - t * t) * c0 * (1.0 + 3.0 * 0.044715 * x * x)
    return gl, dgl


def _lru_fwd(proj, cw, cb, wa, ba, wx, bx, lam, gg, tc):
    s = proj.shape[0]
    pad = 8

    def body(xcur_ref, xprev_ref, gate_ref, cw_ref, cb_ref, wa_ref, ba_ref, wx_ref, bx_ref, lam_ref, gg_ref,
             yn_ref, h_ref, xs_ref, hc_ref):
        i = pl.program_id(0)

        @pl.when(i == 0)
        def _():
            hc_ref[...] = jnp.zeros_like(hc_ref)

        xs_ref[0:pad, :] = jnp.where(i > 0, xprev_ref[tc - pad:tc, :], 0.0)
        xs_ref[pad:pad + tc, :] = xcur_ref[...]
        xc = _conv_taps(xs_ref, cw_ref, LRU_K, pad, tc) + cb_ref[...]
        _, ig, a, mlt, _ = _lru_gates(xc, wa_ref, ba_ref, wx_ref, bx_ref, lam_ref)
        u = mlt * (ig * xc)
        row = _row_iota((tc, W_A))
        d = 1
        while d < tc:
            ok = row >= d
            a_sh = jnp.where(ok, pltpu.roll(a, d, axis=0), 1.0)
            u_sh = jnp.where(ok, pltpu.roll(u, d, axis=0), 0.0)
            u = a * u_sh + u
            a = a * a_sh
            d *= 2
        h = u + a * hc_ref[...]
        hc_ref[...] = jnp.sum(jnp.where(row == tc - 1, h, 0.0), axis=0, keepdims=True)
        h_ref[...] = h
        gl, _ = _gelu_parts(gate_ref[...])
        ya = gl * h
        yn_ref[...] = (ya * _rsq(ya, NORM_EPS) * gg_ref[...]).astype(BF16)

    blk = lambda c: pl.BlockSpec((tc, W_A), lambda i, c=c: (i, c))
    full = lambda a: pl.BlockSpec(a.shape, lambda i: (0,) * a.ndim)
    params = [cw, cb, wa, ba, wx, bx, lam, gg]
    return pl.pallas_call(
        body, name="lru_fwd", grid=(s // tc,),
        in_specs=[blk(0), pl.BlockSpec((tc, W_A), lambda i: (jnp.maximum(i - 1, 0), 0)), blk(1)] + [full(a) for a in params],
        out_specs=[pl.BlockSpec((tc, W_A), lambda i: (i, 0))] * 2,
        out_shape=[SDS((s, W_A), BF16), SDS((s, W_A), F32)],
        scratch_shapes=[pltpu.VMEM((tc + pad, W_A), F32), pltpu.VMEM((1, W_A), F32)],
        compiler_params=_cp("arbitrary"),
    )(proj, proj, proj, *params)


def _acc_rows(ref, first, rows):
    _acc(ref, first, jnp.sum(rows, axis=0, keepdims=True))


def _acc(ref, first, val):
    @pl.when(first)
    def _():
        ref[...] = val

    @pl.when(jnp.logical_not(first))
    def _():
        ref[...] += val


def _lru_bwd(dy, proj, h, cw, cb, wa, ba, wx, bx, lam, gg, tc):
    s = proj.shape[0]
    nc = s // tc
    pad = 8

    def body(dy_ref, xcur_ref, xprev_ref, gate_ref, h_ref, hprev_ref, cw_ref, cb_ref, wa_ref, ba_ref, wx_ref, bx_ref,
             lam_ref, gg_ref,
             dp_ref, dcw_ref, dcb_ref, dwa_ref, dba_ref, dwx_ref, dbx_ref, dlam_ref, dgg_ref,
             xs_ref, ds_ref, mu_ref, nx_ref):
        step = pl.program_id(0)
        i = nc - 1 - step
        first = step == 0

        @pl.when(first)
        def _():
            mu_ref[...] = jnp.zeros_like(mu_ref)
            nx_ref[...] = jnp.zeros_like(nx_ref)

        xs_ref[0:pad, :] = jnp.where(i > 0, xprev_ref[tc - pad:tc, :], 0.0)
        xs_ref[pad:pad + tc, :] = xcur_ref[...]
        xc = _conv_taps(xs_ref, cw_ref, LRU_K, pad, tc) + cb_ref[...]
        r, ig, a, mlt, sp = _lru_gates(xc, wa_ref, ba_ref, wx_ref, bx_ref, lam_ref)
        hh = h_ref[...]
        gate = gate_ref[...]
        gl, dgl = _gelu_parts(gate)
        ya = gl * hh
        dya, dggr = _rms_bwd_rows(ya, gg_ref[...], dy_ref[...])
        _acc(dgg_ref, first, jnp.sum(dggr, axis=0, keepdims=True))
        dp_ref[:, W_A:2 * W_A] = dya * hh * dgl
        dh = dya * gl

        row = _row_iota((tc, W_A))
        aa = a
        uu = a * dh
        d = 1
        while d < tc:
            ok = row < tc - d
            a_sh = jnp.where(ok, pltpu.roll(aa, tc - d, axis=0), 1.0)
            u_sh = jnp.where(ok, pltpu.roll(uu, tc - d, axis=0), 0.0)
            uu = uu + aa * u_sh
            aa = aa * a_sh
            d *= 2
        cin = mu_ref[...]
        mu = uu + aa * cin
        lam_t = dh + jnp.where(row == tc - 1, cin, pltpu.roll(mu, tc - 1, axis=0))
        mu_ref[...] = jnp.sum(jnp.where(row == 0, mu, 0.0), axis=0, keepdims=True)
        hm1 = jnp.where(row == 0, jnp.where(i > 0, pltpu.roll(hprev_ref[...], 1, axis=0), 0.0),
                        pltpu.roll(hh, 1, axis=0))
        da = lam_t * hm1
        du = lam_t
        dmlt = du * ig * xc
        dig = du * mlt * xc
        dxc = du * mlt * ig
        dlog_a = da * a - dmlt * (a * a / mlt)
        dr = dlog_a * (-LRU_C * sp)
        dsp = jnp.sum(dlog_a * (-LRU_C * r), axis=0, keepdims=True)
        _acc(dlam_ref, first, dsp * (-_sig(-lam_ref[...])))
        dga = dr * r * (1.0 - r)
        dgx = dig * ig * (1.0 - ig)
        _acc(dba_ref, first, jnp.sum(dga, axis=0, keepdims=True))
        _acc(dbx_ref, first, jnp.sum(dgx, axis=0, keepdims=True))
        xb = xc.astype(BF16)
        dgab = dga.astype(BF16)
        dgxb = dgx.astype(BF16)
        _acc(dwa_ref, first, lax.dot_general(xb, dgab, TN, preferred_element_type=F32))
        _acc(dwx_ref, first, lax.dot_general(xb, dgxb, TN, preferred_element_type=F32))
        dxc = (dxc + lax.dot_general(dgab, wa_ref[...], NT, preferred_element_type=F32)
               + lax.dot_general(dgxb, wx_ref[...], NT, preferred_element_type=F32))

        _acc(dcb_ref, first, jnp.sum(dxc, axis=0, keepdims=True))
        r8 = _row_iota((8, W_A))
        dcw = jnp.zeros((8, W_A), F32)
        for j in range(LRU_K):
            tap = jnp.sum(dxc * xs_ref[pl.ds(pad - (LRU_K - 1) + j, tc), :], axis=0, keepdims=True)
            dcw = dcw + jnp.where(r8 == j, tap, 0.0)
        _acc(dcw_ref, first, dcw)
        ds_ref[0:tc, :] = dxc
        ds_ref[tc:tc + pad, :] = nx_ref[...]
        dlx = None
        for j in range(LRU_K):
            term = cw_ref[j:j + 1, :] * ds_ref[pl.ds(LRU_K - 1 - j, tc), :]
            dlx = term if dlx is None else dlx + term
        dp_ref[:, 0:W_A] = dlx
        nx_ref[...] = dxc[0:pad, :]

    rev = lambda c: pl.BlockSpec((tc, W_A), lambda t, c=c: (nc - 1 - t, c))
    prev = lambda c: pl.BlockSpec((tc, W_A), lambda t, c=c: (jnp.maximum(nc - 2 - t, 0), c))
    full = lambda a: pl.BlockSpec(a.shape, lambda t: (0,) * a.ndim)
    params = [cw, cb, wa, ba, wx, bx, lam, gg]
    vec = SDS((1, W_A), F32)
    sq = SDS((W_A, W_A), F32)
    outs = [SDS((s, 2 * W_A), F32), SDS((8, W_A), F32), vec, sq, vec, sq, vec, vec, vec]
    return pl.pallas_call(
        body, name="lru_bwd", grid=(nc,),
        in_specs=[rev(0), rev(0), prev(0), rev(1), rev(0), prev(0)] + [full(a) for a in params],
        out_specs=[pl.BlockSpec((tc, 2 * W_A), lambda t: (nc - 1 - t, 0))]
        + [pl.BlockSpec(o.shape, lambda t: (0, 0)) for o in outs[1:]],
        out_shape=outs,
        scratch_shapes=[pltpu.VMEM((tc + pad, W_A), F32), pltpu.VMEM((tc + pad, W_A), F32),
                        pltpu.VMEM((1, W_A), F32), pltpu.VMEM((pad, W_A), F32)],
        compiler_params=_cp("arbitrary"),
    )(dy, proj, proj, proj, h, h, *params)


def _attn_stack(qa, qb, kvh):
    lane = lax.broadcasted_iota(jnp.int32, qa.shape, 1)
    keep = (lane >= HD) if kvh == 1 else (lane < HD)
    parts = []
    for tile in (qa, qb):
        for half in (0, 1):
            y = tile if half == kvh else pltpu.roll(tile, HD, axis=1)
            parts.append(jnp.where(keep, y, 0.0))
    return jnp.concatenate(parts, axis=0)


def _attn_unstack(o, kvh):
    lane = lax.broadcasted_iota(jnp.int32, (BLK, 2 * HD), 1)
    tiles = []
    for t in range(2):
        halves = []
        for half in (0, 1):
            blk = o[(2 * t + half) * BLK:(2 * t + half + 1) * BLK, :]
            halves.append(blk if half == kvh else pltpu.roll(blk, HD, axis=1))
        tiles.append(jnp.where(lane < HD, halves[0], halves[1]))
    return tiles


def _attn_stack_all(x_ref_or_val):
    return jnp.concatenate([_attn_stack(x_ref_or_val[:, 256 * kvh:256 * kvh + 128],
                                        x_ref_or_val[:, 256 * kvh + 128:256 * kvh + 256], kvh) for kvh in range(2)], axis=0)


def _attn_unstack_all(o, dst_ref):
    for kvh in range(2):
        ta, tb = _attn_unstack(o[4 * BLK * kvh:4 * BLK * (kvh + 1), :], kvh)
        dst_ref[:, 256 * kvh:256 * kvh + 128] = ta
        dst_ref[:, 256 * kvh + 128:256 * kvh + 256] = tb


def _attn_windows(cur_ref, prev_ref, nb):
    blocks = [prev_ref[...]] + [cur_ref[b * BLK:(b + 1) * BLK, :] for b in range(nb)]
    return [jnp.concatenate(blocks[b:b + 2], axis=0).astype(BF16) for b in range(nb)]


def _attn_bias():
    qi = np.arange(NQ * BLK)[:, None] % BLK
    kj = np.arange(2 * BLK)[None, :]
    rel = BLK + qi - kj
    ok = (rel >= 0) & (rel < BLK)
    return jnp.asarray(np.stack([np.where(ok & (kj >= BLK), 0.0, NEG_BIG), np.where(ok, 0.0, NEG_BIG)]), F32)


def _attn_probs(qs, kw, first, sink_ref, bias_ref):
    rows = NQ * BLK
    bias = bias_ref[1] if first is False else jnp.where(first, bias_ref[0], bias_ref[1])
    sh = lax.dot_general(qs.astype(BF16), kw, NT, preferred_element_type=F32) * SCALE + bias
    head = lax.broadcasted_iota(jnp.int32, (rows, 1), 0) // BLK
    sk = jnp.zeros((rows, 1), F32)
    for h in range(NQ):
        sk = jnp.where(head == h, sink_ref[h:h + 1, 0:1], sk)
    m = jnp.maximum(jnp.max(sh, axis=-1, keepdims=True), sk)
    e = jnp.exp(sh - m)
    es = jnp.exp(sk - m)
    rz = 1.0 / (jnp.sum(e, axis=-1, keepdims=True) + es)
    return e * rz, es * rz


def _attn_fwd(proj, sinks8, gg):
    s = proj.shape[0]
    nb = ATT_NB_FWD

    def body(q_ref, kc_ref, kp_ref, vc_ref, vp_ref, sink_ref, gg_ref, bias_ref, yn_ref, ob_ref):
        kws, vws = _attn_windows(kc_ref, kp_ref, nb), _attn_windows(vc_ref, vp_ref, nb)
        for b in range(nb):
            rows = pl.ds(b * BLK, BLK)
            first = (pl.program_id(0) == 0) if b == 0 else False
            p, _ = _attn_probs(_attn_stack_all(q_ref.at[rows, :]), kws[b], first, sink_ref, bias_ref)
            _attn_unstack_all(jnp.dot(p.astype(BF16), vws[b], preferred_element_type=F32), ob_ref.at[rows, :])
        ob = ob_ref[...]
        yn_ref[...] = (ob * _rsq(ob, NORM_EPS) * gg_ref[...]).astype(BF16)

    tb = nb * BLK
    cur = lambda c: pl.BlockSpec((tb, 128), lambda m, c=c: (m, c))
    prev = lambda c: pl.BlockSpec((BLK, 128), lambda m, c=c: (jnp.maximum(nb * m - 1, 0), c))
    out = pl.BlockSpec((tb, W_B), lambda m: (m, 0))
    return pl.pallas_call(
        body, name="attn_fwd", grid=(s // tb,),
        in_specs=[pl.BlockSpec((tb, W_B), lambda m: (m, 1)), cur(8), prev(8), cur(9), prev(9),
                  pl.BlockSpec((8, 128), lambda n: (0, 0)), pl.BlockSpec((1, W_B), lambda n: (0, 0)),
                  pl.BlockSpec((2, NQ * BLK, 2 * BLK), lambda n: (0, 0, 0))],
        out_specs=[out, out], out_shape=[SDS((s, W_B), BF16), SDS((s, W_B), F32)],
        compiler_params=_cp("parallel"),
    )(proj, proj, proj, proj, proj, sinks8, gg, _attn_bias())


def _attn_bwd(dy, proj, ob, sinks8, gg):
    s = proj.shape[0]
    nb = ATT_NB_BWD

    def body(dya_ref, dyb_ref, q_ref, kc_ref, kp_ref, vc_ref, vp_ref, ob_ref, sink_ref, gg_ref, bias_ref,
             dq_ref, dcur_ref, dprev_ref, dsink_ref, dgg_ref):
        first = pl.program_id(0) == 0
        kws, vws = _attn_windows(kc_ref, kp_ref, nb), _attn_windows(vc_ref, vp_ref, nb)
        dyn = jnp.concatenate([dya_ref[...], dyb_ref[...]], axis=1)
        dob, dggr = _rms_bwd_rows(ob_ref[...], gg_ref[...], dyn)
        _acc(dgg_ref, first, jnp.sum(dggr, axis=0, keepdims=True))
        r8 = _row_iota((8, 128))
        dsk = jnp.zeros((8, 128), F32)
        for b in range(nb):
            rows = pl.ds(b * BLK, BLK)
            qs = _attn_stack_all(q_ref.at[rows, :])
            p, psink = _attn_probs(qs, kws[b], first if b == 0 else False, sink_ref, bias_ref)
            dosb = _attn_stack_all(dob[b * BLK:(b + 1) * BLK, :]).astype(BF16)
            dp = lax.dot_general(dosb, vws[b], NT, preferred_element_type=F32)
            dd = jnp.sum(p * dp, axis=-1, keepdims=True)
            dsb = (p * (dp - dd) * SCALE).astype(BF16)
            dsink_rows = -psink * dd
            for h in range(NQ):
                dsk = dsk + jnp.where(r8 == h, jnp.sum(dsink_rows[h * BLK:(h + 1) * BLK, :], axis=0, keepdims=True), 0.0)
            _attn_unstack_all(jnp.dot(dsb, kws[b], preferred_element_type=F32), dq_ref.at[rows, :])
            dkw = lax.dot_general(dsb, qs.astype(BF16), TN, preferred_element_type=F32)
            dvw = lax.dot_general(p.astype(BF16), dosb, TN, preferred_element_type=F32)
            dprev_ref[rows, 0:128] = dkw[0:BLK, :]
            dprev_ref[rows, 128:256] = dvw[0:BLK, :]
            dcur_ref[rows, 0:128] = dkw[BLK:2 * BLK, :]
            dcur_ref[rows, 128:256] = dvw[BLK:2 * BLK, :]
        _acc(dsink_ref, first, dsk)

    tb = nb * BLK
    cur = lambda c: pl.BlockSpec((tb, 128), lambda m, c=c: (m, c))
    prev = lambda c: pl.BlockSpec((BLK, 128), lambda m, c=c: (jnp.maximum(nb * m - 1, 0), c))
    wide = pl.BlockSpec((tb, W_B), lambda m: (m, 0))
    half = pl.BlockSpec((tb, 256), lambda m: (m, 0))
    return pl.pallas_call(
        body, name="attn_bwd", grid=(s // tb,),
        in_specs=[pl.BlockSpec((tb, 256), lambda m: (m, 1)), pl.BlockSpec((tb, 256), lambda m: (m, 2)),
                  pl.BlockSpec((tb, W_B), lambda m: (m, 1)), cur(8), prev(8), cur(9), prev(9), wide,
                  pl.BlockSpec((8, 128), lambda n: (0, 0)), pl.BlockSpec((1, W_B), lambda n: (0, 0)),
                  pl.BlockSpec((2, NQ * BLK, 2 * BLK), lambda n: (0, 0, 0))],
        out_specs=[wide, half, half, pl.BlockSpec((8, 128), lambda n: (0, 0)), pl.BlockSpec((1, W_B), lambda n: (0, 0))],
        out_shape=[SDS((s, W_B), F32), SDS((s, 256), F32), SDS((s, 256), F32), SDS((8, 128), F32), SDS((1, W_B), F32)],
        compiler_params=_cp("arbitrary"),
    )(dy, dy, proj, proj, proj, proj, proj, ob, sinks8, gg, _attn_bias())


def _ln_parts(y1, eps=LN_EPS):
    mu = jnp.mean(y1, axis=-1, keepdims=True)
    xc = y1 - mu
    rstd = lax.rsqrt(jnp.mean(xc * xc, axis=-1, keepdims=True) + eps)
    return xc * rstd, rstd


def _conf_fwd(proj, cw, cb, lg, lb, gg, tc):
    s = proj.shape[0]
    pad = 32

    def body(ac_ref, gc_ref, ap_ref, gp_ref, cw_ref, cb_ref, lg_ref, lb_ref, gg_ref, yn_ref, y1_ref, ys_ref, sh_ref):
        i = pl.program_id(0)
        tail = ap_ref[tc - pad:tc, :] * _sig(gp_ref[tc - pad:tc, :])
        ys_ref[0:pad, :] = jnp.where(i > 0, tail, 0.0)
        ys_ref[pad:pad + tc, :] = ac_ref[...] * _sig(gc_ref[...])
        _fill_shifted(ys_ref, sh_ref)
        y1 = cb_ref[...]
        for j in range(CONV_K):
            y1 = y1 + cw_ref[j:j + 1, :] * _shifted_rows(ys_ref, sh_ref, pad - (CONV_K - 1) + j, tc)
        y1_ref[...] = y1
        xh, _ = _ln_parts(y1)
        yl = xh * lg_ref[...] + lb_ref[...]
        yc = yl * _sig(yl)
        yn_ref[...] = (yc * _rsq(yc, NORM_EPS) * gg_ref[...]).astype(BF16)

    cur = lambda c: pl.BlockSpec((tc, W_C), lambda i, c=c: (i, c))
    prev = lambda c: pl.BlockSpec((tc, W_C), lambda i, c=c: (jnp.maximum(i - 1, 0), c))
    full = lambda a: pl.BlockSpec(a.shape, lambda i: (0,) * a.ndim)
    params = [cw, cb, lg, lb, gg]
    out = pl.BlockSpec((tc, W_C), lambda i: (i, 0))
    return pl.pallas_call(
        body, name="conf_fwd", grid=(s // tc,),
        in_specs=[cur(5), cur(6), prev(5), prev(6)] + [full(a) for a in params],
        out_specs=[out, out], out_shape=[SDS((s, W_C), BF16), SDS((s, W_C), F32)],
        scratch_shapes=[pltpu.VMEM((tc + pad, W_C), F32), pltpu.VMEM((8, tc + pad, W_C), F32)],
        compiler_params=_cp("parallel"),
    )(proj, proj, proj, proj, *params)


def _conf_bwd(dy, proj, y1, cw, cb, lg, lb, gg, tc):
    s = proj.shape[0]
    nc = s // tc
    pad = 32

    def body(dy_ref, ac_ref, gc_ref, ap_ref, gp_ref, y1_ref, cw_ref, cb_ref, lg_ref, lb_ref, gg_ref,
             dp_ref, dcw_ref, dcb_ref, dlg_ref, dlb_ref, dgg_ref, ys_ref, ds_ref, nx_ref, ysh_ref, dsh_ref):
        step = pl.program_id(0)
        i = nc - 1 - step
        first = step == 0

        @pl.when(first)
        def _():
            nx_ref[...] = jnp.zeros_like(nx_ref)

        a = ac_ref[...]
        sg = _sig(gc_ref[...])
        tail = ap_ref[tc - pad:tc, :] * _sig(gp_ref[tc - pad:tc, :])
        ys_ref[0:pad, :] = jnp.where(i > 0, tail, 0.0)
        ys_ref[pad:pad + tc, :] = a * sg
        xh, rstd = _ln_parts(y1_ref[...])
        yl = xh * lg_ref[...] + lb_ref[...]
        sl = _sig(yl)
        yc = yl * sl
        dyc, dggr = _rms_bwd_rows(yc, gg_ref[...], dy_ref[...])
        _acc(dgg_ref, first, jnp.sum(dggr, axis=0, keepdims=True))
        dyl = dyc * sl * (1.0 + yl * (1.0 - sl))
        _acc(dlg_ref, first, jnp.sum(dyl * xh, axis=0, keepdims=True))
        _acc(dlb_ref, first, jnp.sum(dyl, axis=0, keepdims=True))
        dxh = dyl * lg_ref[...]
        dy1 = rstd * (dxh - jnp.mean(dxh, axis=-1, keepdims=True) - xh * jnp.mean(dxh * xh, axis=-1, keepdims=True))
        _acc(dcb_ref, first, jnp.sum(dy1, axis=0, keepdims=True))
        r32 = _row_iota((32, W_C))
        dcw = jnp.zeros((32, W_C), F32)
        _fill_shifted(ys_ref, ysh_ref)
        for j in range(CONV_K):
            tap = jnp.sum(dy1 * _shifted_rows(ys_ref, ysh_ref, pad - (CONV_K - 1) + j, tc), axis=0, keepdims=True)
            dcw = dcw + jnp.where(r32 == j, tap, 0.0)
        _acc(dcw_ref, first, dcw)
        ds_ref[0:tc, :] = dy1
        ds_ref[tc:tc + pad, :] = nx_ref[...]
        _fill_shifted(ds_ref, dsh_ref)
        dy0 = None
        for j in range(CONV_K):
            term = cw_ref[j:j + 1, :] * _shifted_rows(ds_ref, dsh_ref, CONV_K - 1 - j, tc)
            dy0 = term if dy0 is None else dy0 + term
        dp_ref[:, 0:W_C] = dy0 * sg
        dp_ref[:, W_C:2 * W_C] = dy0 * a * sg * (1.0 - sg)
        nx_ref[...] = dy1[0:pad, :]

    rev = lambda c: pl.BlockSpec((tc, W_C), lambda t, c=c: (nc - 1 - t, c))
    prev = lambda c: pl.BlockSpec((tc, W_C), lambda t, c=c: (jnp.maximum(nc - 2 - t, 0), c))
    full = lambda a: pl.BlockSpec(a.shape, lambda t: (0,) * a.ndim)
    params = [cw, cb, lg, lb, gg]
    vec = SDS((1, W_C), F32)
    outs = [SDS((s, 2 * W_C), F32), SDS((32, W_C), F32), vec, vec, vec, vec]
    return pl.pallas_call(
        body, name="conf_bwd", grid=(nc,),
        in_specs=[rev(3), rev(5), rev(6), prev(5), prev(6), rev(0)] + [full(a) for a in params],
        out_specs=[pl.BlockSpec((tc, 2 * W_C), lambda t: (nc - 1 - t, 0))]
        + [pl.BlockSpec(o.shape, lambda t: (0, 0)) for o in outs[1:]],
        out_shape=outs,
        scratch_shapes=[pltpu.VMEM((tc + pad, W_C), F32), pltpu.VMEM((tc + pad, W_C), F32), pltpu.VMEM((pad, W_C), F32),
                        pltpu.VMEM((8, tc + pad, W_C), F32), pltpu.VMEM((8, tc + pad, W_C), F32)],
        compiler_params=_cp("arbitrary"),
    )(dy, proj, proj, proj, proj, y1, *params)


def _assemble_dproj(dlru, dq, dcur, dprev, dconf):
    s = dq.shape[0]
    nb = s // BLK

    def body(dl_ref, dq_ref, dc_ref, dn_ref, df_ref, o_ref):
        n = pl.program_id(0)
        o_ref[:, 0:512] = dl_ref[...].astype(BF16)
        o_ref[:, 512:1024] = dq_ref[...].astype(BF16)
        o_ref[:, 1024:1280] = (dc_ref[...] + jnp.where(n < nb - 1, dn_ref[...], 0.0)).astype(BF16)
        o_ref[:, 1280:1792] = df_ref[...].astype(BF16)

    wide = pl.BlockSpec((BLK, 512), lambda n: (n, 0))
    return pl.pallas_call(
        body, name="assemble_dproj", grid=(nb,),
        in_specs=[wide, wide, pl.BlockSpec((BLK, 256), lambda n: (n, 0)),
                  pl.BlockSpec((BLK, 256), lambda n: (jnp.minimum(n + 1, nb - 1), 0)), wide],
        out_specs=pl.BlockSpec((BLK, P_IN), lambda n: (n, 0)), out_shape=SDS((s, P_IN), BF16),
        compiler_params=_cp("parallel"),
    )(dlru, dq, dcur, dprev, dconf)


def _loss_grad(y, t, tm, below):
    s = y.shape[0]

    def body(y_ref, t_ref, zb_ref, gb_ref, dy_ref, l_ref, dzb_ref, dgb_ref):
        first = pl.program_id(0) == 0
        err = y_ref[...] - t_ref[...]
        dy = err * (1.0 / D)
        dy_ref[...] = dy
        _acc_rows(l_ref, first, err * err)
        _post_norm_tail(dy, below[2], zb_ref, gb_ref, dzb_ref, dgb_ref, first)

    row = pl.BlockSpec((tm, D), lambda i: (i, 0))
    vec = pl.BlockSpec((1, D), lambda i: (0, 0))
    return pl.pallas_call(
        body, name="loss_grad", grid=(s // tm,), in_specs=[row, row, row, vec],
        out_specs=[row, vec, row, vec],
        out_shape=[SDS((s, D), F32), SDS((1, D), F32), SDS((s, D), BF16), SDS((1, D), F32)], compiler_params=_cp("arbitrary"),
    )(y, t, below[0], below[1])


def _block_diag(w):
    rows = [jnp.concatenate([w[h] if k == h else jnp.zeros((64, 64), w.dtype) for k in range(4)], axis=1) for h in range(4)]
    return jnp.concatenate(rows, axis=0)


def _diag_blocks(m):
    return jnp.stack([m[64 * h:64 * (h + 1), 64 * h:64 * (h + 1)] for h in range(4)])


def _layer_params(small, l):
    v = lambda name: small[name][l].reshape(1, -1)
    gg = small["group_g"][l]
    return dict(
        ffn1_pre=v("ffn1_pre_g"), ffn1_post=v("ffn1_post_g"), mix_pre=v("mix_pre_g"), mix_post=v("mix_post_g"),
        ffn2_pre=v("ffn2_pre_g"), ffn2_post=v("ffn2_post_g"), lru_cb=v("lru_conv_b"),
        wa=_block_diag(small["lru_w_a"][l]).astype(BF16), ba=v("lru_b_a"),
        wx=_block_diag(small["lru_w_x"][l]).astype(BF16), bx=v("lru_b_x"), lam=v("lru_lambda"),
        sinks8=jnp.broadcast_to(small["attn_sinks"][l][:, None], (NQ, 128)),
        conv_b=v("conv_b"), ln_g=v("conv_ln_g"), ln_b=v("conv_ln_b"),
        gg_a=gg[0:W_A].reshape(1, -1), gg_b=gg[W_A:W_A + W_B].reshape(1, -1), gg_c=gg[W_A + W_B:].reshape(1, -1),
    )


def _forward_layer(x, weights, p, tiles, deps=()):
    _, mm, _, tc, _ = tiles
    big = dict(weights("ffn1_gu", x))
    p = dict(p)
    sv = dict(x0=x)
    h1, g1, u1, a1 = _ffn_up(x, p["ffn1_pre"], big["ffn1_w_gu"], 0, mm, deps)
    big.update(weights("ffn1_down", a1))
    z1, x = _mm_rms_res(a1, big["ffn1_w_down"], 0, x, p["ffn1_post"], 0.5, mm, DFF, "ffn_down")
    sv.update(h1=h1, g1=g1, u1=u1, a1=a1, z1=z1, x1=x)
    big.update(weights("mix", x))
    p.update(lru_cw=big.pop("lru_conv_w"), conv_w=big.pop("conv_w"))
    hn, proj = _proj(x, p["mix_pre"], big["w_in"], 0, mm)
    yn_a, hl = _lru_fwd(proj, p["lru_cw"], p["lru_cb"], p["wa"], p["ba"], p["wx"], p["bx"], p["lam"], p["gg_a"], tc)
    yn_b, ob = _attn_fwd(proj, p["sinks8"], p["gg_b"])
    yn_c, y1 = _conf_fwd(proj, p["conv_w"], p["conv_b"], p["ln_g"], p["ln_b"], p["gg_c"], tc)
    ycat = jnp.concatenate([yn_a, yn_b, yn_c], axis=1)
    zo, x = _mm_rms_res(ycat, big["w_out"], 0, x, p["mix_post"], 1.0, mm, D, "mix_out")
    sv.update(hn=hn, proj=proj, hl=hl, ob=ob, y1=y1, ycat=ycat, zo=zo, x2=x)
    big.update(weights("ffn2", x))
    h2, g2, u2, a2 = _ffn_up(x, p["ffn2_pre"], big["ffn2_w_gu"], 0, mm)
    z2, x = _mm_rms_res(a2, big["ffn2_w_down"], 0, x, p["ffn2_post"], 0.5, mm, DFF, "ffn_down")
    sv.update(h2=h2, g2=g2, u2=u2, a2=a2, z2=z2, p=p, big=big)
    return x, sv


def _grad_buffers():
    empty = lambda *shape: lax.empty(shape, F32)
    return dict(ffn1_w_gu=empty(1, NSHARD, D, FH), ffn2_w_gu=empty(1, NSHARD, D, FH), ffn1_w_down=empty(1, 1, DFF, D),
                ffn2_w_down=empty(1, 1, DFF, D), w_in=empty(1, 1, D, P_IN), w_out=empty(1, 1, D, D))


def _backward_layer(dx, dzp, sv, bufs, tiles, stage, below):
    p, big = sv["p"], sv["big"]
    tm, mm, dw, tc, dh_rows = tiles
    gr = {}

    def ffn_bwd(dx, dzp, which, xin, h, g, u, a, pre, deps, below):
        dz, gr[which + "_post_g"] = dzp
        dgu = _ffn_bwd_mid(dz, big[which + "_w_down"], 0, g, u, mm, deps)
        bufs[which + "_w_down"] = _mm_tn_into(bufs[which + "_w_down"], a, dz, 0, 0, FH, D, dw, "dw_down")
        bufs[which + "_w_gu"] = _mm_tn_into(bufs[which + "_w_gu"], h, dgu, 0, 0, D, FH, dw, "dw_gate", 2, 0)
        bufs[which + "_w_gu"] = _mm_tn_into(bufs[which + "_w_gu"], h, dgu, 0, 2, D, FH, dw, "dw_up", 2, 1)
        deps = stage({n: bufs[n] for n in (which + "_w_gu", which + "_w_down")}, bufs[which + "_w_gu"], gr)
        out = _ffn_bwd_dh(dgu, big[which + "_w_gu"], 0, xin, pre, dx, dh_rows, deps, below)
        gr[which + "_pre_g"] = out[1]
        return out[0], (tuple(out[2:]) if below is not None else None)

    dx, (do, gr["mix_post_g"]) = ffn_bwd(dx, dzp, "ffn2", sv["x2"], sv["h2"], sv["g2"], sv["u2"], sv["a2"],
                                         p["ffn2_pre"], (), (sv["zo"], p["mix_post"], 1.0))
    bufs["w_out"] = _mm_tn_into(bufs["w_out"], sv["ycat"], do, 0, 0, D, D, dw, "dw_out")
    dy = _mm_nt(do, big["w_out"], 0, mm, "mix_dy")
    proj = sv["proj"]
    (dlru, dcw, gr["lru_conv_b"], dwa, gr["lru_b_a"], dwx, gr["lru_b_x"], gr["lru_lambda"], dgg_a) = _lru_bwd(
        dy, proj, sv["hl"], p["lru_cw"], p["lru_cb"], p["wa"], p["ba"], p["wx"], p["bx"], p["lam"], p["gg_a"], tc)
    dq, dcur, dprev, dsk, dgg_b = _attn_bwd(dy, proj, sv["ob"], p["sinks8"], p["gg_b"])
    dconf, dconvw, gr["conv_b"], gr["conv_ln_g"], gr["conv_ln_b"], dgg_c = _conf_bwd(
        dy, proj, sv["y1"], p["conv_w"], p["conv_b"], p["ln_g"], p["ln_b"], p["gg_c"], tc)
    dproj = _assemble_dproj(dlru, dq, dcur, dprev, dconf)
    bufs["w_in"] = _mm_tn_into(bufs["w_in"], sv["hn"], dproj, 0, 0, D, P_IN, dw, "dw_in")
    dx, gr["mix_pre_g"], dz1, dpost1 = _mm_nt_rmsbwd(dproj, big["w_in"], 0, sv["x1"], p["mix_pre"], dx, mm,
                                                     (sv["z1"], p["ffn1_post"], 0.5))
    gr["lru_conv_w"] = dcw[0:LRU_K]
    gr["lru_w_a"] = _diag_blocks(dwa)
    gr["lru_w_x"] = _diag_blocks(dwx)
    gr["attn_sinks"] = dsk[:, 0]
    gr["conv_w"] = dconvw[0:CONV_K]
    gr["group_g"] = jnp.concatenate([dgg_a, dgg_b, dgg_c], axis=1)
    dx, dz_below = ffn_bwd(dx, (dz1, dpost1), "ffn1", sv["x0"], sv["h1"], sv["g1"], sv["u1"], sv["a1"], p["ffn1_pre"],
                           stage({n: bufs[n] for n in ("w_in", "w_out")}, dx, gr), below)
    return dx, dz_below, gr


def _tiles(s):
    return min(1024, s), min(1024, s), min(2048, s), min(512, s // 2), min(512, s)


HBM_SPEC = pl.BlockSpec(memory_space=pltpu.HBM)
SEM_SPEC = pl.BlockSpec(memory_space=pltpu.SEMAPHORE)
EFFECT = pltpu.SideEffectType.DATAFLOW_SIDE_EFFECTING


def _place():
    x, y, c = lax.axis_index("x"), lax.axis_index("y"), lax.axis_index("c")
    return x, y, c, [(1 - x, y), (x, 1 - y), (1 - x, 1 - y)]


def _rcopy(src, dst, send_sems, recv_sems, k, to):
    return pltpu.make_async_remote_copy(src_ref=src, dst_ref=dst, send_sem=send_sems.at[k], recv_sem=recv_sems.at[k],
                                        device_id=to, device_id_type=MESH)


def _half(rows, which):
    return pl.ds(which * (rows // 2), rows // 2)


def _place_shard(w, l, p_idx, dtype, deps=()):
    _, rows, cols = w.shape
    tr = _rows_per_block(rows, cols, 16, SUM_BLOCK_ELEMS) if rows % 16 == 0 else rows
    deps = list(deps)

    def body(p_ref, buf_ref, w_ref, *rest):
        rest[len(deps)][...] = w_ref[...].astype(dtype)

    spec = pltpu.PrefetchScalarGridSpec(
        num_scalar_prefetch=1, grid=(rows // tr,),
        in_specs=[ANY, pl.BlockSpec((None, tr, cols), lambda i, pr: (l, i, 0))] + [ANY] * len(deps),
        out_specs=pl.BlockSpec((None, None, tr, cols), lambda i, pr: (0, pr[0], i, 0)))
    shape = (1, NSHARD, rows, cols)
    return pl.pallas_call(body, name="place_shard", grid_spec=spec, out_shape=SDS(shape, dtype),
                          input_output_aliases={1: 0}, compiler_params=_cp("parallel"),
                          )(p_idx, lax.empty(shape, dtype), w, *deps)


def _run_plans(plans, refs, send_sems, recv_sems):
    cps, b0, s0 = [], 0, 0
    for plan, nb, ns in plans:
        cps += plan(refs[b0:b0 + nb], send_sems, recv_sems, s0)
        b0, s0 = b0 + nb, s0 + ns
    return cps


def _exchange(name, bufs, plans):
    n = len(bufs)
    nsem = sum(ns for _, _, ns in plans)

    def body(*refs):
        cps = _run_plans(plans, refs[n:2 * n], refs[2 * n], refs[2 * n + 1])
        for cp in cps:
            cp.start()
        for cp in cps:
            cp.wait()

    return pl.pallas_call(
        body, name=name, in_specs=[ANY] * n, out_specs=[ANY] * n, out_shape=[SDS(b.shape, b.dtype) for b in bufs],
        input_output_aliases={a: a for a in range(n)},
        scratch_shapes=[pltpu.SemaphoreType.DMA((nsem,)), pltpu.SemaphoreType.DMA((nsem,))],
    )(*bufs)


def _exchange_start(name, bufs, plans, deps=()):
    n = len(bufs)
    nsem = sum(ns for _, _, ns in plans)
    deps = list(deps)
    first_out = n + len(deps)

    def body(*refs):
        for cp in _run_plans(plans, refs[:n], refs[first_out], refs[first_out + 1]):
            cp.start()
        token = refs[first_out + 2 + n]
        token[...] = jnp.zeros_like(token)

    outs = pl.pallas_call(
        body, name=name,
        out_shape=(pltpu.SemaphoreType.DMA((nsem,)), pltpu.SemaphoreType.DMA((nsem,)),
                   *[pltpu.HBM(b.shape, b.dtype) for b in bufs], SDS((8, 128), F32)),
        in_specs=[HBM_SPEC] * n + [ANY] * len(deps),
        out_specs=(SEM_SPEC, SEM_SPEC, *[HBM_SPEC] * n, pl.BlockSpec(memory_space=pltpu.VMEM)),
        input_output_aliases={a: 2 + a for a in range(n)},
        compiler_params=pltpu.CompilerParams(has_side_effects=EFFECT),
    )(*[pltpu.with_memory_space_constraint(b, pltpu.HBM) for b in bufs], *deps)
    return outs[0], outs[1], list(outs[2:2 + n]), outs[2 + n]


def _exchange_wait(name, send_sems, recv_sems, bufs, plans, after):
    n = len(bufs)

    def body(*refs):
        for cp in _run_plans(plans, refs[:n], refs[n], refs[n + 1]):
            cp.wait_send()
            cp.wait_recv()

    return pl.pallas_call(
        body, name=name, out_shape=[pltpu.HBM(b.shape, b.dtype) for b in bufs],
        in_specs=[HBM_SPEC] * n + [SEM_SPEC, SEM_SPEC, ANY], out_specs=[HBM_SPEC] * n,
        input_output_aliases={a: a for a in range(n)},
        compiler_params=pltpu.CompilerParams(has_side_effects=EFFECT),
    )(*bufs, send_sems, recv_sems, after)


def _plan_gather(refs, send_sems, recv_sems, base):
    x, y, c, chips = _place()
    p = 2 * x + y
    return [_rcopy(r.at[0, p], r.at[0, p], send_sems, recv_sems, base + 3 * a + j, (*chip, c))
            for a, r in enumerate(refs) for j, chip in enumerate(chips)]


def _plan_gather_half(refs, send_sems, recv_sems, base):
    x, y, c, chips = _place()
    p = 2 * x + y
    return [_rcopy(r.at[0, p, _half(r.shape[2], c)], r.at[0, p, _half(r.shape[2], c)], send_sems, recv_sems,
                   base + 3 * a + j, (*chip, c)) for a, r in enumerate(refs) for j, chip in enumerate(chips)]


def _plan_forward_half(refs, send_sems, recv_sems, base):
    x, y, c, chips = _place()
    cps = []
    for a, r in enumerate(refs):
        for j, chip in enumerate(chips):
            blk = r.at[0, 2 * chip[0] + chip[1], _half(r.shape[2], c)]
            cps.append(_rcopy(blk, blk, send_sems, recv_sems, base + 3 * a + j, (x, y, 1 - c)))
    return cps


def _plan_pair_exchange(refs, send_sems, recv_sems, base):
    x, y, c, _ = _place()
    n = len(refs) // 2
    return [_rcopy(refs[a].at[:, _half(refs[a].shape[1], 1 - c)], refs[n + a], send_sems, recv_sems, base + a,
                   (x, y, 1 - c)) for a in range(n)]


def _plan_chip_exchange(refs, send_sems, recv_sems, base):
    x, y, c, chips = _place()
    n = len(refs) // 2
    return [_rcopy(refs[a].at[2 * chip[0] + chip[1]], refs[n + a].at[j], send_sems, recv_sems, base + 3 * a + j,
                   (*chip, c)) for a in range(n) for j, chip in enumerate(chips)]


def _plan_pair_share(refs, send_sems, recv_sems, base):
    x, y, c, _ = _place()
    return [_rcopy(r.at[_half(r.shape[0], c)], r.at[_half(r.shape[0], c)], send_sems, recv_sems, base + a,
                   (x, y, 1 - c)) for a, r in enumerate(refs)]


def _plan_small_gather(refs, send_sems, recv_sems, base):
    x, y, c, _ = _place()
    me = 4 * x + 2 * y + c
    cps = []
    for m in range(1, NDEV):
        peer = (1 - x if m & 4 else x, 1 - y if m & 2 else y, 1 - c if m & 1 else c)
        cps.append(_rcopy(refs[0], refs[1].at[me], send_sems, recv_sems, base + m - 1, peer))
    return cps


def _sum_small(buf, gathered):
    def body(buf_ref, g_ref, o_ref):
        x, y, c, _ = _place()
        me = 4 * x + 2 * y + c
        total = jnp.where(me == 0, buf_ref[...], g_ref[0])
        for dev in range(1, NDEV):
            total = total + jnp.where(me == dev, buf_ref[...], g_ref[dev])
        o_ref[...] = total

    vm = pl.BlockSpec(memory_space=pltpu.VMEM)
    return pl.pallas_call(body, name="sum_small", in_specs=[vm, vm], out_specs=vm, out_shape=SDS(buf.shape, F32),
                          compiler_params=pltpu.CompilerParams(vmem_limit_bytes=VMEM_LIMIT))(buf, gathered)


BLOCK_ELEMS = 512 * 1024
SUM_BLOCK_ELEMS = 1024 * 1024


def _rows_per_block(rows, cols, mult, limit=BLOCK_ELEMS):
    best = None
    for tr in range(mult, rows + 1, mult):
        if rows % tr == 0 and tr * cols <= limit:
            best = tr
    assert best is not None, (rows, cols)
    return best


def _pair_sum(g, r, c_idx):
    nq, rows, cols = g.shape
    half = rows // 2
    tr = _rows_per_block(half, cols, 16, SUM_BLOCK_ELEMS)
    nb = half // tr

    def body(c_ref, g_ref, r_ref, t_ref):
        t_ref[...] = (g_ref[...] + r_ref[...]).astype(BF16)

    blk = pl.BlockSpec((None, tr, cols), lambda q, i, cr: (q, i, 0))
    spec = pltpu.PrefetchScalarGridSpec(
        num_scalar_prefetch=1, grid=(nq, nb),
        in_specs=[pl.BlockSpec((None, tr, cols), lambda q, i, cr: (q, cr[0] * nb + i, 0)), blk], out_specs=blk)
    return pl.pallas_call(body, name="grad_pair_sum", grid_spec=spec, out_shape=SDS((nq, half, cols), BF16),
                          compiler_params=_cp("parallel", "parallel"))(c_idx, g, r)


def _chip_sum(g, r, rr, cp_idx):
    _, rows, cols = g.shape
    half = rows // 2
    tr = _rows_per_block(half, cols, 16, SUM_BLOCK_ELEMS)
    nb = half // tr

    def body(cp_ref, buf_ref, g_ref, r_ref, rr_ref, o_ref):
        o_ref[...] = ((g_ref[...] + r_ref[...]) + rr_ref[0].astype(F32) + rr_ref[1].astype(F32) + rr_ref[2].astype(F32))

    spec = pltpu.PrefetchScalarGridSpec(
        num_scalar_prefetch=1, grid=(nb,),
        in_specs=[ANY, pl.BlockSpec((None, tr, cols), lambda i, cp: (cp[1], cp[0] * nb + i, 0)),
                  pl.BlockSpec((None, tr, cols), lambda i, cp: (cp[1], i, 0)),
                  pl.BlockSpec((3, tr, cols), lambda i, cp: (0, i, 0))],
        out_specs=pl.BlockSpec((tr, cols), lambda i, cp: (cp[0] * nb + i, 0)))
    return pl.pallas_call(body, name="grad_chip_sum", grid_spec=spec, out_shape=SDS((rows, cols), F32),
                          input_output_aliases={1: 0}, compiler_params=_cp("parallel"),
                          )(cp_idx, lax.empty((rows, cols), F32), g, r, rr)


def _adamw_math(w, g, m, v):
    mn = ADAM_B1 * m + (1.0 - ADAM_B1) * g
    vn = ADAM_B2 * v + (1.0 - ADAM_B2) * (g * g)
    m_hat = mn / (1.0 - ADAM_B1 ** ADAM_STEP)
    v_hat = vn / (1.0 - ADAM_B2 ** ADAM_STEP)
    return -ADAM_LR * (m_hat / (jnp.sqrt(v_hat) + ADAM_EPS) + ADAM_WD * w), mn, vn


def _adamw_layer(w, g, m, v, l, outs, deps=()):
    _, rows, cols = w.shape
    tr = _rows_per_block(rows, cols, 8)
    deps = list(deps)

    def body(*refs):
        w_ref, g_ref, m_ref, v_ref = refs[4:8]
        go_ref, d_ref, mo_ref, vo_ref = refs[8 + len(deps):]
        gg = g_ref[...]
        go_ref[...] = gg
        d_ref[...], mo_ref[...], vo_ref[...] = _adamw_math(w_ref[...], gg, m_ref[...], v_ref[...])

    blk = pl.BlockSpec((None, tr, cols), lambda i: (l, i, 0))
    return pl.pallas_call(
        body, name="adamw_layer", grid=(rows // tr,),
        in_specs=[ANY] * 4 + [blk, pl.BlockSpec((tr, cols), lambda i: (i, 0)), blk, blk] + [ANY] * len(deps),
        out_specs=[blk] * 4, out_shape=[SDS(w.shape, F32)] * 4, input_output_aliases={k: k for k in range(4)},
        compiler_params=_cp("parallel"))(*outs, w, g, m, v, *deps)


def _adamw_small(ws, gs, ms, vs, deps=()):
    n = len(ws)
    deps = list(deps)

    def body(*refs):
        refs = refs[:4 * n] + refs[4 * n + len(deps):]
        w, g, m, v, d_out, m_out, v_out = (refs[k * n:(k + 1) * n] for k in range(7))
        for k in range(n):
            d_out[k][...], m_out[k][...], v_out[k][...] = _adamw_math(w[k][...], g[k][...], m[k][...], v[k][...])

    vm = pl.BlockSpec(memory_space=pltpu.VMEM)
    outs = pl.pallas_call(body, name="adamw_small", in_specs=[vm] * (4 * n) + [ANY] * len(deps), out_specs=[vm] * (3 * n),
                          out_shape=[SDS(w.shape, F32) for w in ws] * 3,
                          compiler_params=pltpu.CompilerParams(vmem_limit_bytes=VMEM_LIMIT))(*ws, *gs, *ms, *vs, *deps)
    return outs[:n], outs[n:2 * n], outs[2 * n:]


_WEIGHTS = ["ffn1_pre_g", "ffn1_w_gu", "ffn1_w_down", "ffn1_post_g", "mix_pre_g", "w_in", "lru_conv_w", "lru_conv_b",
            "lru_w_a", "lru_b_a", "lru_w_x", "lru_b_x", "lru_lambda", "attn_sinks", "conv_w", "conv_b", "conv_ln_g",
            "conv_ln_b", "group_g", "w_out", "mix_post_g", "ffn2_pre_g", "ffn2_w_gu", "ffn2_w_down", "ffn2_post_g"]
_INPUTS = ["x"] + _WEIGHTS + ["loss_target"] + ["m_" + n for n in _WEIGHTS] + ["v_" + n for n in _WEIGHTS]
_BIG = ["ffn1_w_gu", "ffn1_w_down", "w_in", "w_out", "ffn2_w_gu", "ffn2_w_down"]
_SMALL_SHARDED = ["lru_conv_w", "conv_w"]
_SMALL_REPL = [n for n in _WEIGHTS if n not in _BIG and n not in _SMALL_SHARDED]

PACK_TILE = 8 * 128


def _pack(arrs):
    parts = []
    for a in arrs:
        flat = a.reshape(-1)
        parts.append(jnp.pad(flat, (0, -flat.shape[0] % PACK_TILE)).reshape(-1, 128))
    return jnp.concatenate(parts, axis=0)


def _unpack(buf, shapes):
    out, row = [], 0
    for shp in shapes:
        size = math.prod(shp)
        nrow = -(-size // PACK_TILE) * 8
        out.append(buf[row:row + nrow].reshape(-1)[:size].reshape(shp))
        row += nrow
    return out


def _unshard_cols(a):
    return a.transpose(0, 2, 1, 3).reshape(1, a.shape[2], NSHARD * a.shape[3])


_GROUPS = dict(ffn1_gu=["ffn1_w_gu"], ffn1_down=["ffn1_w_down"], mix=["w_in", "w_out", "lru_conv_w", "conv_w"],
               ffn2=["ffn2_w_gu", "ffn2_w_down"])


def _full_weights(group, gathered):
    g = dict(zip(_GROUPS[group], gathered))
    if group == "mix":
        return dict(w_in=_unshard_cols(g["w_in"]), w_out=g["w_out"].reshape(1, D, D),
                    lru_conv_w=_unshard_cols(g["lru_conv_w"])[0], conv_w=_unshard_cols(g["conv_w"])[0])
    return {n: (a.reshape(1, DFF, D) if n.endswith("w_down") else a) for n, a in g.items()}


def _by_shard(name, buf):
    if name.endswith("w_gu"):
        return buf[0]
    if name == "w_in":
        return buf.reshape(D, NSHARD, P_IN // NSHARD).transpose(1, 0, 2)
    return buf.reshape(NSHARD, buf.shape[2] // NSHARD, buf.shape[3])


class _Reducer:
    PLANS = (_plan_pair_exchange, _plan_chip_exchange, _plan_pair_share)

    def __init__(self, keys, gs, c_idx, cp_idx):
        self.keys, self.gs, self.c_idx, self.cp_idx = keys, gs, c_idx, cp_idx
        self.n = len(gs)
        self.step = 0
        self.result = None

    def inputs(self):
        n = self.n
        if self.step == 0:
            bufs = self.gs + [lax.empty((NSHARD, g.shape[1] // 2, g.shape[2]), F32) for g in self.gs]
        elif self.step == 1:
            ts = [_pair_sum(g, r, self.c_idx) for g, r in zip(self.gs, self.rs)]
            bufs = ts + [lax.empty((3,) + t.shape[1:], BF16) for t in ts]
        else:
            bufs = [_chip_sum(g, r, rr, self.cp_idx) for g, r, rr in zip(self.gs, self.rs, self.rrs)]
        return bufs, (self.PLANS[self.step], len(bufs), (n, 3 * n, n)[self.step])

    def absorb(self, done):
        n = self.n
        if self.step == 0:
            self.gs, self.rs = done[:n], done[n:]
        elif self.step == 1:
            self.rrs = done[n:]
        else:
            self.result = dict(zip(self.keys, done))
        self.step += 1


class _SmallGather:
    def __init__(self, buf):
        self.buf, self.step, self.result, self.gathered = buf, 0, {}, None

    def inputs(self):
        return [self.buf, jnp.zeros((NDEV,) + self.buf.shape, F32)], (_plan_small_gather, 2, NDEV - 1)

    def absorb(self, done):
        self.buf, self.gathered = done
        self.step = 3


class _ReducePipeline:
    def __init__(self, c_idx, cp_idx):
        self.c_idx, self.cp_idx = c_idx, cp_idx
        self.reducers, self.flying, self.calls = [], None, 0

    def add(self, layer, done):
        if done:
            keys = [(layer, n) for n in done]
            self.reducers.append(_Reducer(keys, [_by_shard(n, b) for n, b in done.items()], self.c_idx, self.cp_idx))

    def _next(self):
        active = [r for r in self.reducers if r.step < 3]
        bufs, plans = [], []
        for r in active:
            b, triple = r.inputs()
            bufs += b
            plans.append(triple)
        self.calls += 1
        return active, bufs, plans, "grad_exchange%d" % self.calls

    def _absorb(self, active, plans, done):
        at = 0
        for r, (_, nb, _) in zip(active, plans):
            r.absorb(done[at:at + nb])
            at += nb

    def _land(self, after):
        if self.flying is not None:
            active, plans, name, send_sems, recv_sems, bufs = self.flying
            self._absorb(active, plans, _exchange_wait(name + "_wait", send_sems, recv_sems, bufs, plans, after))
            self.flying = None

    def hook(self, after):
        self._land(after)
        active, bufs, plans, name = self._next()
        if not active:
            return []
        send_sems, recv_sems, bufs, token = _exchange_start(name + "_start", bufs, plans)
        self.flying = (active, plans, name, send_sems, recv_sems, bufs)
        return [token]

    def available(self):
        out = {}
        for r in self.reducers:
            if r.step == 3:
                out.update(r.result)
        return out

    def finish(self, after):
        self._land(after)
        while True:
            active, bufs, plans, name = self._next()
            if not active:
                break
            self._absorb(active, plans, _exchange(name, bufs, plans))
        out = {}
        for r in self.reducers:
            out.update(r.result)
        return out


def kernel(*args):
    d = dict(zip(_INPUTS, args, strict=True))
    xi, yi, ci = lax.axis_index("x"), lax.axis_index("y"), lax.axis_index("c")
    p = 2 * xi + yi
    c_idx = jnp.reshape(ci, (1,)).astype(jnp.int32)
    p_idx = jnp.reshape(p, (1,)).astype(jnp.int32)
    cp_idx = jnp.stack([ci, p]).astype(jnp.int32)
    x, target = d["x"][0], d["loss_target"][0]
    tiles = _tiles(x.shape[0])

    groups = [(l, grp) for l in range(DEPTH) for grp in _GROUPS]
    place = lambda l, grp, deps: [_place_shard(d[n], l, p_idx, BF16 if n in _BIG else F32, deps) for n in _GROUPS[grp]]
    first = place(*groups[0], ())
    half_plans = [(_plan_gather_half, len(first), 3 * len(first))]
    first_sems = _exchange_start("gather_first_start", first, half_plans)
    tokens = [first_sems[3]]
    flying = {}
    for l, grp in groups[1:]:
        placed = place(l, grp, tokens[:1])
        plans = [(_plan_gather, len(placed), 3 * len(placed))]
        send_sems, recv_sems, bufs, token = _exchange_start("gather_l%d_%s_start" % (l, grp), placed, plans, tokens[-1:])
        flying[l, grp] = (send_sems, recv_sems, bufs, plans)
        tokens.append(token)
    first = _exchange_wait("gather_first_wait", first_sems[0], first_sems[1], first_sems[2], half_plans, tokens[-1])
    ready = {groups[0]: _exchange("gather_first_forward", first, [(_plan_forward_half, len(first), 3 * len(first))])}

    def weights_of(l):
        def weights(grp, after):
            if (l, grp) not in ready:
                send_sems, recv_sems, bufs, plans = flying[l, grp]
                ready[l, grp] = _exchange_wait("gather_l%d_%s_wait" % (l, grp), send_sems, recv_sems, bufs, plans, after)
            return _full_weights(grp, ready[l, grp])
        return weights

    small = {n: d[n] for n in _SMALL_REPL}
    x1, sv0 = _forward_layer(x, weights_of(0), _layer_params(small, 0), tiles)
    x2, sv1 = _forward_layer(x1, weights_of(1), _layer_params(small, 1), tiles)
    dx, lcols, *dzp = _loss_grad(x2, target, tiles[0], (sv1["z2"], sv1["p"]["ffn2_post"], 0.5))

    pipe = _ReducePipeline(c_idx, cp_idx)
    sgrads = [None] * DEPTH
    order = _SMALL_REPL + _SMALL_SHARDED
    early_names = [n for n in order if n != "ffn1_pre_g"]
    natural = lambda n, g: g.reshape(d[n].shape[1:]) if n in _SMALL_REPL else g
    loss_part = jnp.pad((0.5 / D) * jnp.sum(lcols).reshape(1), (0, 127))
    early = {}
    for l, sv, below in ((1, sv1, (sv0["z2"], sv0["p"]["ffn2_post"], 0.5)), (0, sv0, None)):
        bufs = _grad_buffers()

        def stage(done, dx, gr, l=l):
            pipe.add(l, done)
            if l == 0 and "ffn1_w_gu" in done:
                stacked = [jnp.stack([natural(n, gr[n]), natural(n, sgrads[1][n])]) for n in early_names]
                early["shapes"] = [(128,)] + [a.shape for a in stacked] + [(D,)]
                early["gather"] = _SmallGather(_pack([loss_part] + stacked + [sgrads[1]["ffn1_pre_g"].reshape(-1)]))
                pipe.reducers.append(early["gather"])
            return pipe.hook(dx)

        dx, dzp, sgrads[l] = _backward_layer(dx, tuple(dzp), sv, bufs, tiles, stage, below)
    grad_x = dx
    late_gather = _SmallGather(_pack([sgrads[0]["ffn1_pre_g"].reshape(-1)]))
    pipe.reducers.append(late_gather)

    results = {n: tuple(lax.empty(d[n].shape, F32) for _ in range(4)) for n in _BIG}
    applied = set()

    def apply_ready(deps, last):
        for (l, n), g in pipe.available().items():
            if (l, n) not in applied:
                results[n] = _adamw_layer(d[n], g, d["m_" + n], d["v_" + n], l, results[n], deps)
                applied.add((l, n))
                last = results[n][1]
                deps = [last]
        return last

    last = apply_ready(pipe.hook(grad_x), grad_x)
    token = pipe.hook(last)
    summed = _unpack(_sum_small(early["gather"].buf, early["gather"].gathered), early["shapes"])
    late = _unpack(_sum_small(late_gather.buf, late_gather.gathered), [(D,)])[0]
    loss = summed[0][0]
    grads = {"ffn1_pre_g": jnp.stack([late, summed[-1]])}
    for n, g in zip(early_names, summed[1:-1]):
        if n in _SMALL_SHARDED:
            g = lax.dynamic_slice_in_dim(g, p * (g.shape[2] // NSHARD), g.shape[2] // NSHARD, axis=2)
        grads[n] = g
    delta, new_m, new_v = {}, {}, {}
    small_out = _adamw_small([d[n] for n in order], [grads[n] for n in order], [d["m_" + n] for n in order],
                             [d["v_" + n] for n in order], token)
    for out, res in zip((delta, new_m, new_v), small_out):
        out.update(zip(order, res))
    last = apply_ready([small_out[0][0]], small_out[0][0])
    pipe.finish(last)
    apply_ready((), last)
    for n in _BIG:
        grads[n], delta[n], new_m[n], new_v[n] = results[n]

    return (loss, grad_x[None], *[grads[n] for n in _WEIGHTS], *[delta[n] for n in _WEIGHTS],
            *[new_m[n] for n in _WEIGHTS], *[new_v[n] for n in _WEIGHTS])
```

```python
import math

import jax
import jax.numpy as jnp
import numpy as np
from jax import lax
from jax.experimental import pallas as pl
from jax.experimental.pallas import tpu as pltpu

F32 = jnp.float32
BF16 = jnp.bfloat16
SDS = jax.ShapeDtypeStruct

D = 1024
DFF = 2816
FH = DFF // 2
DEPTH = 2
W_A = 256
W_B = 512
W_C = 256
NQ = 8
HD = 64
BLK = 128
ATT_NB_FWD = 1
ATT_NB_BWD = 8
P_IN = 1792
LRU_K = 4
CONV_K = 31
LRU_C = 8.0
NORM_EPS = 1e-6
LN_EPS = 1e-5
NEG_BIG = -1e30
SCALE = 1.0 / math.sqrt(HD)

ADAM_LR = 0.001
ADAM_B1 = 0.9
ADAM_B2 = 0.999
ADAM_EPS = 1e-08
ADAM_WD = 0.01
ADAM_STEP = 10

VMEM_LIMIT = 60 * 1024 * 1024
NSHARD = 4
NDEV = 8

TN = (((0,), (0,)), ((), ()))
NT = (((1,), (1,)), ((), ()))

MESH = pl.DeviceIdType.MESH
ANY = pl.BlockSpec(memory_space=pl.ANY)


def _cp(*sem):
    return pltpu.CompilerParams(dimension_semantics=sem if sem else None, vmem_limit_bytes=VMEM_LIMIT)


def _rsq(x, eps):
    return lax.rsqrt(jnp.mean(x * x, axis=-1, keepdims=True) + eps)


def _rms_bwd_rows(x, g, dy):
    r = _rsq(x, NORM_EPS)
    xh = x * r
    dyg = dy * g
    dx = r * (dyg - xh * jnp.mean(dyg * xh, axis=-1, keepdims=True))
    return dx, dy * xh


def _sig(x):
    return jax.nn.sigmoid(x)


def _post_norm_tail(dx, c, z_ref, g_ref, dz_ref, dg_ref, first):
    dz, dgr = _rms_bwd_rows(z_ref[...], g_ref[...], c * dx)
    dz_ref[...] = dz.astype(BF16)
    _acc_rows(dg_ref, first, dgr)


def _ffn_up(x, pre_g, wgu, l, tm, deps=()):
    s = x.shape[0]
    deps = list(deps)

    def body(x_ref, g_ref, wg_ref, wu_ref, *rest):
        h_ref, go_ref, uo_ref, a_ref = rest[len(deps):]

        @pl.when(pl.program_id(1) == 0)
        def _():
            xf = x_ref[...]
            h_ref[...] = (xf * _rsq(xf, NORM_EPS) * g_ref[...]).astype(BF16)

        h = h_ref[...]
        gg = jnp.dot(h, wg_ref[...], preferred_element_type=F32)
        uu = jnp.dot(h, wu_ref[...], preferred_element_type=F32)
        sg = _sig(gg)
        silu = gg * sg
        go_ref[...] = (uu * (sg * (1.0 + gg * (1.0 - sg)))).astype(BF16)
        uo_ref[...] = silu.astype(BF16)
        a_ref[...] = (silu * uu).astype(BF16)

    wide = pl.BlockSpec((tm, FH), lambda i, j: (i, j))
    return pl.pallas_call(
        body, name="ffn_up", grid=(s // tm, 2),
        in_specs=[pl.BlockSpec((tm, D), lambda i, j: (i, 0)), pl.BlockSpec((1, D), lambda i, j: (0, 0)),
                  pl.BlockSpec((None, None, D, FH), lambda i, j: (l, j, 0, 0)),
                  pl.BlockSpec((None, None, D, FH), lambda i, j: (l, j + 2, 0, 0))] + [ANY] * len(deps),
        out_specs=[pl.BlockSpec((tm, D), lambda i, j: (i, 0)), wide, wide, wide],
        out_shape=[SDS((s, D), BF16), SDS((s, DFF), BF16), SDS((s, DFF), BF16), SDS((s, DFF), BF16)],
        compiler_params=_cp("parallel", "arbitrary"),
    )(x, pre_g, wgu, wgu, *deps)


def _mm_rms_res(a, w, l, x, g, c, tm, tk, name):
    s, k_dim = a.shape
    nk = k_dim // tk

    def body(a_ref, w_ref, x_ref, g_ref, z_ref, x1_ref):
        k = pl.program_id(1)
        p = jnp.dot(a_ref[...], w_ref[...], preferred_element_type=F32)

        @pl.when(k == 0)
        def _():
            z_ref[...] = p

        @pl.when(k > 0)
        def _():
            z_ref[...] += p

        @pl.when(k == nk - 1)
        def _():
            z = z_ref[...]
            x1_ref[...] = x_ref[...] + c * (z * _rsq(z, NORM_EPS) * g_ref[...])

    row = pl.BlockSpec((tm, D), lambda i, k: (i, 0))
    return pl.pallas_call(
        body, name=name, grid=(s // tm, nk),
        in_specs=[pl.BlockSpec((tm, tk), lambda i, k: (i, k)), pl.BlockSpec((None, tk, D), lambda i, k: (l, k, 0)),
                  row, pl.BlockSpec((1, D), lambda i, k: (0, 0))],
        out_specs=[row, row],
        out_shape=[SDS((s, D), F32), SDS((s, D), F32)],
        compiler_params=_cp("parallel", "arbitrary"),
    )(a, w, x, g)


def _ffn_bwd_mid(dz, wd, l, dadg, dadu, tm, deps=()):
    s = dz.shape[0]
    deps = list(deps)

    def body(dz_ref, wd_ref, g_ref, u_ref, *rest):
        dgu_ref = rest[len(deps)]
        da = lax.dot_general(dz_ref[...], wd_ref[...], NT, preferred_element_type=F32)
        dgu_ref[:, 0:FH] = (da * g_ref[...].astype(F32)).astype(BF16)
        dgu_ref[:, FH:2 * FH] = (da * u_ref[...].astype(F32)).astype(BF16)

    wide = pl.BlockSpec((tm, FH), lambda i, j: (i, j))
    return pl.pallas_call(
        body, name="ffn_bwd_mid", grid=(s // tm, 2),
        in_specs=[pl.BlockSpec((tm, D), lambda i, j: (i, 0)), pl.BlockSpec((None, FH, D), lambda i, j: (l, j, 0)), wide, wide]
        + [ANY] * len(deps),
        out_specs=pl.BlockSpec((tm, 2 * FH), lambda i, j: (i, j)),
        out_shape=SDS((s, 2 * DFF), BF16),
        compiler_params=_cp("parallel", "arbitrary"),
    )(dz, wd, dadg, dadu, *deps)


def _ffn_bwd_dh(dgu, wgu, l, x, pre_g, dx1, tm, deps=(), below=None):
    s = x.shape[0]
    deps = list(deps)
    tail = [] if below is None else list(below[:2])

    def body(dgu_ref, w_hbm, x_ref, g_ref, dx1_ref, *rest):
        rest = rest[len(deps):]
        tail_in, (dx_ref, dgp_ref), rest = rest[:len(tail)], rest[len(tail):len(tail) + 2], rest[len(tail) + 2:]
        tail_out, (wcat_ref, sems) = rest[:len(tail)], rest[len(tail):]
        i = pl.program_id(0)

        @pl.when(i == 0)
        def _():
            cps = [pltpu.make_async_copy(w_hbm.at[l, q], wcat_ref.at[:, pl.ds((2 * (q % 2) + q // 2) * FH, FH)], sems.at[q])
                   for q in range(NSHARD)]
            for cp in cps:
                cp.start()
            for cp in cps:
                cp.wait()

        dh = lax.dot_general(dgu_ref[...], wcat_ref[...], NT, preferred_element_type=F32)
        dx, dgr = _rms_bwd_rows(x_ref[...], g_ref[...], dh)
        dx = dx1_ref[...] + dx
        dx_ref[...] = dx
        _acc_rows(dgp_ref, i == 0, dgr)
        if tail:
            _post_norm_tail(dx, below[2], *tail_in, *tail_out, i == 0)

    row = pl.BlockSpec((tm, D), lambda i: (i, 0))
    vec = pl.BlockSpec((1, D), lambda i: (0, 0))
    return pl.pallas_call(
        body, name="ffn_bwd_dh", grid=(s // tm,),
        in_specs=[pl.BlockSpec((tm, 2 * DFF), lambda i: (i, 0)), ANY, row, vec, row] + [ANY] * len(deps) + [row, vec][:len(tail)],
        out_specs=[row, vec] + [row, vec][:len(tail)],
        out_shape=[SDS((s, D), F32), SDS((1, D), F32)] + [SDS((s, D), BF16), SDS((1, D), F32)][:len(tail)],
        scratch_shapes=[pltpu.VMEM((D, 2 * DFF), BF16), pltpu.SemaphoreType.DMA((NSHARD,))],
        compiler_params=_cp("arbitrary"),
    )(dgu, wgu, x, pre_g, dx1, *deps, *tail)


def _mm_tn_into(buf, a, b, l, joff, tka, tn, ts, name, bstride=1, boff=0):
    s, ka = a.shape
    n = b.shape[1] // bstride

    def body(buf_ref, a_ref, b_ref, o_ref):
        p = lax.dot_general(a_ref[...], b_ref[...], TN, preferred_element_type=F32)

        @pl.when(pl.program_id(2) == 0)
        def _():
            o_ref[...] = p

        @pl.when(pl.program_id(2) > 0)
        def _():
            o_ref[...] += p

    return pl.pallas_call(
        body, name=name, grid=(ka // tka, n // tn, s // ts),
        in_specs=[pl.BlockSpec(memory_space=pl.ANY),
                  pl.BlockSpec((ts, tka), lambda ia, j, t: (t, ia)),
                  pl.BlockSpec((ts, tn), lambda ia, j, t: (t, bstride * j + boff))],
        out_specs=pl.BlockSpec((None, None, tka, tn), lambda ia, j, t: (l, joff + j, ia, 0)),
        out_shape=SDS(buf.shape, F32), input_output_aliases={0: 0},
        compiler_params=_cp("parallel", "parallel", "arbitrary"),
    )(buf, a, b)


def _proj(x, g, w_in, l, tm):
    s = x.shape[0]

    def body(x_ref, g_ref, w_ref, h_ref, p_ref):
        xf = x_ref[...]
        h = (xf * _rsq(xf, NORM_EPS) * g_ref[...]).astype(BF16)
        h_ref[...] = h
        p_ref[...] = jnp.dot(h, w_ref[...], preferred_element_type=F32)

    return pl.pallas_call(
        body, name="proj", grid=(s // tm,),
        in_specs=[pl.BlockSpec((tm, D), lambda i: (i, 0)), pl.BlockSpec((1, D), lambda i: (0, 0)),
                  pl.BlockSpec((None, D, P_IN), lambda i: (l, 0, 0))],
        out_specs=[pl.BlockSpec((tm, D), lambda i: (i, 0)), pl.BlockSpec((tm, P_IN), lambda i: (i, 0))],
        out_shape=[SDS((s, D), BF16), SDS((s, P_IN), F32)],
        compiler_params=_cp("parallel"),
    )(x, g, w_in)


def _mm_nt(a, w, l, tm, name):
    s, k_dim = a.shape
    n = w.shape[1]

    def body(a_ref, w_ref, o_ref):
        o_ref[...] = lax.dot_general(a_ref[...], w_ref[...], NT, preferred_element_type=F32)

    return pl.pallas_call(
        body, name=name, grid=(s // tm,),
        in_specs=[pl.BlockSpec((tm, k_dim), lambda i: (i, 0)), pl.BlockSpec((None, n, k_dim), lambda i: (l, 0, 0))],
        out_specs=pl.BlockSpec((tm, n), lambda i: (i, 0)),
        out_shape=SDS((s, n), F32), compiler_params=_cp("parallel"),
    )(a, w)


def _mm_nt_rmsbwd(dp, w_in, l, x, g, dx1, tm, below):
    s = x.shape[0]

    def body(dp_ref, w_ref, x_ref, g_ref, dx1_ref, zb_ref, gb_ref, dx_ref, dg_ref, dzb_ref, dgb_ref):
        first = pl.program_id(0) == 0
        dh = lax.dot_general(dp_ref[...], w_ref[...], NT, preferred_element_type=F32)
        dx, dgr = _rms_bwd_rows(x_ref[...], g_ref[...], dh)
        dx = dx1_ref[...] + dx
        dx_ref[...] = dx
        _acc_rows(dg_ref, first, dgr)
        _post_norm_tail(dx, below[2], zb_ref, gb_ref, dzb_ref, dgb_ref, first)

    row = pl.BlockSpec((tm, D), lambda i: (i, 0))
    vec = pl.BlockSpec((1, D), lambda i: (0, 0))
    return pl.pallas_call(
        body, name="mix_bwd_dx", grid=(s // tm,),
        in_specs=[pl.BlockSpec((tm, P_IN), lambda i: (i, 0)), pl.BlockSpec((None, D, P_IN), lambda i: (l, 0, 0)), row, vec, row,
                  row, vec],
        out_specs=[row, vec, row, vec],
        out_shape=[SDS((s, D), F32), SDS((1, D), F32), SDS((s, D), BF16), SDS((1, D), F32)],
        compiler_params=_cp("arbitrary"),
    )(dp, w_in, x, g, dx1, below[0], below[1])


def _row_iota(shape):
    return lax.broadcasted_iota(jnp.int32, shape, 0)


def _lru_gates(xc, wa_ref, ba_ref, wx_ref, bx_ref, lam_ref):
    xb = xc.astype(BF16)
    r = _sig(jnp.dot(xb, wa_ref[...], preferred_element_type=F32) + ba_ref[...])
    ig = _sig(jnp.dot(xb, wx_ref[...], preferred_element_type=F32) + bx_ref[...])
    nl = -lam_ref[...]
    sp = jnp.maximum(nl, 0.0) + jnp.log(1.0 + jnp.exp(-jnp.abs(nl)))
    log_a = -LRU_C * r * sp
    a = jnp.exp(log_a)
    mlt = jnp.sqrt((1.0 + a * a) * jnp.tanh(-log_a))
    return r, ig, a, mlt, sp


def _conv_taps(src_ref, w_ref, k_taps, pad, tc):
    acc = None
    for j in range(k_taps):
        term = w_ref[j:j + 1, :] * src_ref[pl.ds(pad - (k_taps - 1) + j, tc), :]
        acc = term if acc is None else acc + term
    return acc


def _fill_shifted(src_ref, sh_ref):
    n = src_ref.shape[0] - 8
    for s in range(1, 8):
        sh_ref[s, 0:n, :] = src_ref[pl.ds(s, n), :]


def _shifted_rows(src_ref, sh_ref, offset, tc):
    if offset % 8 == 0:
        return src_ref[pl.ds(offset, tc), :]
    return sh_ref[offset % 8, pl.ds(offset - offset % 8, tc), :]


def _gelu_parts(x):
    c0 = math.sqrt(2.0 / math.pi)
    inner = c0 * (x + 0.044715 * x * x * x)
    t = jnp.tanh(inner)
    gl = 0.5 * x * (1.0 + t)
    dgl = 0.5 * (1.0 + t) + 0.5 * x * (1.0 - t * t) * c0 * (1.0 + 3.0 * 0.044715 * x * x)
    return gl, dgl


def _lru_fwd(proj, cw, cb, wa, ba, wx, bx, lam, gg, tc):
    s = proj.shape[0]
    pad = 8

    def body(xcur_ref, xprev_ref, gate_ref, cw_ref, cb_ref, wa_ref, ba_ref, wx_ref, bx_ref, lam_ref, gg_ref,
             yn_ref, h_ref, xs_ref, hc_ref):
        i = pl.program_id(0)

        @pl.when(i == 0)
        def _():
            hc_ref[...] = jnp.zeros_like(hc_ref)

        xs_ref[0:pad, :] = jnp.where(i > 0, xprev_ref[tc - pad:tc, :], 0.0)
        xs_ref[pad:pad + tc, :] = xcur_ref[...]
        xc = _conv_taps(xs_ref, cw_ref, LRU_K, pad, tc) + cb_ref[...]
        _, ig, a, mlt, _ = _lru_gates(xc, wa_ref, ba_ref, wx_ref, bx_ref, lam_ref)
        u = mlt * (ig * xc)
        row = _row_iota((tc, W_A))
        d = 1
        while d < tc:
            ok = row >= d
            a_sh = jnp.where(ok, pltpu.roll(a, d, axis=0), 1.0)
            u_sh = jnp.where(ok, pltpu.roll(u, d, axis=0), 0.0)
            u = a * u_sh + u
            a = a * a_sh
            d *= 2
        h = u + a * hc_ref[...]
        hc_ref[...] = jnp.sum(jnp.where(row == tc - 1, h, 0.0), axis=0, keepdims=True)
        h_ref[...] = h
        gl, _ = _gelu_parts(gate_ref[...])
        ya = gl * h
        yn_ref[...] = (ya * _rsq(ya, NORM_EPS) * gg_ref[...]).astype(BF16)

    blk = lambda c: pl.BlockSpec((tc, W_A), lambda i, c=c: (i, c))
    full = lambda a: pl.BlockSpec(a.shape, lambda i: (0,) * a.ndim)
    params = [cw, cb, wa, ba, wx, bx, lam, gg]
    return pl.pallas_call(
        body, name="lru_fwd", grid=(s // tc,),
        in_specs=[blk(0), pl.BlockSpec((tc, W_A), lambda i: (jnp.maximum(i - 1, 0), 0)), blk(1)] + [full(a) for a in params],
        out_specs=[pl.BlockSpec((tc, W_A), lambda i: (i, 0))] * 2,
        out_shape=[SDS((s, W_A), BF16), SDS((s, W_A), F32)],
        scratch_shapes=[pltpu.VMEM((tc + pad, W_A), F32), pltpu.VMEM((1, W_A), F32)],
        compiler_params=_cp("arbitrary"),
    )(proj, proj, proj, *params)


def _acc_rows(ref, first, rows):
    _acc(ref, first, jnp.sum(rows, axis=0, keepdims=True))


def _acc(ref, first, val):
    @pl.when(first)
    def _():
        ref[...] = val

    @pl.when(jnp.logical_not(first))
    def _():
        ref[...] += val


def _lru_bwd(dy, proj, h, cw, cb, wa, ba, wx, bx, lam, gg, tc):
    s = proj.shape[0]
    nc = s // tc
    pad = 8

    def body(dy_ref, xcur_ref, xprev_ref, gate_ref, h_ref, hprev_ref, cw_ref, cb_ref, wa_ref, ba_ref, wx_ref, bx_ref,
             lam_ref, gg_ref,
             dp_ref, dcw_ref, dcb_ref, dwa_ref, dba_ref, dwx_ref, dbx_ref, dlam_ref, dgg_ref,
             xs_ref, ds_ref, mu_ref, nx_ref):
        step = pl.program_id(0)
        i = nc - 1 - step
        first = step == 0

        @pl.when(first)
        def _():
            mu_ref[...] = jnp.zeros_like(mu_ref)
            nx_ref[...] = jnp.zeros_like(nx_ref)

        xs_ref[0:pad, :] = jnp.where(i > 0, xprev_ref[tc - pad:tc, :], 0.0)
        xs_ref[pad:pad + tc, :] = xcur_ref[...]
        xc = _conv_taps(xs_ref, cw_ref, LRU_K, pad, tc) + cb_ref[...]
        r, ig, a, mlt, sp = _lru_gates(xc, wa_ref, ba_ref, wx_ref, bx_ref, lam_ref)
        hh = h_ref[...]
        gate = gate_ref[...]
        gl, dgl = _gelu_parts(gate)
        ya = gl * hh
        dya, dggr = _rms_bwd_rows(ya, gg_ref[...], dy_ref[...])
        _acc(dgg_ref, first, jnp.sum(dggr, axis=0, keepdims=True))
        dp_ref[:, W_A:2 * W_A] = dya * hh * dgl
        dh = dya * gl

        row = _row_iota((tc, W_A))
        aa = a
        uu = a * dh
        d = 1
        while d < tc:
            ok = row < tc - d
            a_sh = jnp.where(ok, pltpu.roll(aa, tc - d, axis=0), 1.0)
            u_sh = jnp.where(ok, pltpu.roll(uu, tc - d, axis=0), 0.0)
            uu = uu + aa * u_sh
            aa = aa * a_sh
            d *= 2
        cin = mu_ref[...]
        mu = uu + aa * cin
        lam_t = dh + jnp.where(row == tc - 1, cin, pltpu.roll(mu, tc - 1, axis=0))
        mu_ref[...] = jnp.sum(jnp.where(row == 0, mu, 0.0), axis=0, keepdims=True)
        hm1 = jnp.where(row == 0, jnp.where(i > 0, pltpu.roll(hprev_ref[...], 1, axis=0), 0.0),
                        pltpu.roll(hh, 1, axis=0))
        da = lam_t * hm1
        du = lam_t
        dmlt = du * ig * xc
        dig = du * mlt * xc
        dxc = du * mlt * ig
        dlog_a = da * a - dmlt * (a * a / mlt)
        dr = dlog_a * (-LRU_C * sp)
        dsp = jnp.sum(dlog_a * (-LRU_C * r), axis=0, keepdims=True)
        _acc(dlam_ref, first, dsp * (-_sig(-lam_ref[...])))
        dga = dr * r * (1.0 - r)
        dgx = dig * ig * (1.0 - ig)
        _acc(dba_ref, first, jnp.sum(dga, axis=0, keepdims=True))
        _acc(dbx_ref, first, jnp.sum(dgx, axis=0, keepdims=True))
        xb = xc.astype(BF16)
        dgab = dga.astype(BF16)
        dgxb = dgx.astype(BF16)
        _acc(dwa_ref, first, lax.dot_general(xb, dgab, TN, preferred_element_type=F32))
        _acc(dwx_ref, first, lax.dot_general(xb, dgxb, TN, preferred_element_type=F32))
        dxc = (dxc + lax.dot_general(dgab, wa_ref[...], NT, preferred_element_type=F32)
               + lax.dot_general(dgxb, wx_ref[...], NT, preferred_element_type=F32))

        _acc(dcb_ref, first, jnp.sum(dxc, axis=0, keepdims=True))
        r8 = _row_iota((8, W_A))
        dcw = jnp.zeros((8, W_A), F32)
        for j in range(LRU_K):
            tap = jnp.sum(dxc * xs_ref[pl.ds(pad - (LRU_K - 1) + j, tc), :], axis=0, keepdims=True)
            dcw = dcw + jnp.where(r8 == j, tap, 0.0)
        _acc(dcw_ref, first, dcw)
        ds_ref[0:tc, :] = dxc
        ds_ref[tc:tc + pad, :] = nx_ref[...]
        dlx = None
        for j in range(LRU_K):
            term = cw_ref[j:j + 1, :] * ds_ref[pl.ds(LRU_K - 1 - j, tc), :]
            dlx = term if dlx is None else dlx + term
        dp_ref[:, 0:W_A] = dlx
        nx_ref[...] = dxc[0:pad, :]

    rev = lambda c: pl.BlockSpec((tc, W_A), lambda t, c=c: (nc - 1 - t, c))
    prev = lambda c: pl.BlockSpec((tc, W_A), lambda t, c=c: (jnp.maximum(nc - 2 - t, 0), c))
    full = lambda a: pl.BlockSpec(a.shape, lambda t: (0,) * a.ndim)
    params = [cw, cb, wa, ba, wx, bx, lam, gg]
    vec = SDS((1, W_A), F32)
    sq = SDS((W_A, W_A), F32)
    outs = [SDS((s, 2 * W_A), F32), SDS((8, W_A), F32), vec, sq, vec, sq, vec, vec, vec]
    return pl.pallas_call(
        body, name="lru_bwd", grid=(nc,),
        in_specs=[rev(0), rev(0), prev(0), rev(1), rev(0), prev(0)] + [full(a) for a in params],
        out_specs=[pl.BlockSpec((tc, 2 * W_A), lambda t: (nc - 1 - t, 0))]
        + [pl.BlockSpec(o.shape, lambda t: (0, 0)) for o in outs[1:]],
        out_shape=outs,
        scratch_shapes=[pltpu.VMEM((tc + pad, W_A), F32), pltpu.VMEM((tc + pad, W_A), F32),
                        pltpu.VMEM((1, W_A), F32), pltpu.VMEM((pad, W_A), F32)],
        compiler_params=_cp("arbitrary"),
    )(dy, proj, proj, proj, h, h, *params)


def _attn_stack(qa, qb, kvh):
    lane = lax.broadcasted_iota(jnp.int32, qa.shape, 1)
    keep = (lane >= HD) if kvh == 1 else (lane < HD)
    parts = []
    for tile in (qa, qb):
        for half in (0, 1):
            y = tile if half == kvh else pltpu.roll(tile, HD, axis=1)
            parts.append(jnp.where(keep, y, 0.0))
    return jnp.concatenate(parts, axis=0)


def _attn_unstack(o, kvh):
    lane = lax.broadcasted_iota(jnp.int32, (BLK, 2 * HD), 1)
    tiles = []
    for t in range(2):
        halves = []
        for half in (0, 1):
            blk = o[(2 * t + half) * BLK:(2 * t + half + 1) * BLK, :]
            halves.append(blk if half == kvh else pltpu.roll(blk, HD, axis=1))
        tiles.append(jnp.where(lane < HD, halves[0], halves[1]))
    return tiles


def _attn_stack_all(x_ref_or_val):
    return jnp.concatenate([_attn_stack(x_ref_or_val[:, 256 * kvh:256 * kvh + 128],
                                        x_ref_or_val[:, 256 * kvh + 128:256 * kvh + 256], kvh) for kvh in range(2)], axis=0)


def _attn_unstack_all(o, dst_ref):
    for kvh in range(2):
        ta, tb = _attn_unstack(o[4 * BLK * kvh:4 * BLK * (kvh + 1), :], kvh)
        dst_ref[:, 256 * kvh:256 * kvh + 128] = ta
        dst_ref[:, 256 * kvh + 128:256 * kvh + 256] = tb


def _attn_windows(cur_ref, prev_ref, nb):
    blocks = [prev_ref[...]] + [cur_ref[b * BLK:(b + 1) * BLK, :] for b in range(nb)]
    return [jnp.concatenate(blocks[b:b + 2], axis=0).astype(BF16) for b in range(nb)]


def _attn_bias():
    qi = np.arange(NQ * BLK)[:, None] % BLK
    kj = np.arange(2 * BLK)[None, :]
    rel = BLK + qi - kj
    ok = (rel >= 0) & (rel < BLK)
    return jnp.asarray(np.stack([np.where(ok & (kj >= BLK), 0.0, NEG_BIG), np.where(ok, 0.0, NEG_BIG)]), F32)


def _attn_probs(qs, kw, first, sink_ref, bias_ref):
    rows = NQ * BLK
    bias = bias_ref[1] if first is False else jnp.where(first, bias_ref[0], bias_ref[1])
    sh = lax.dot_general(qs.astype(BF16), kw, NT, preferred_element_type=F32) * SCALE + bias
    head = lax.broadcasted_iota(jnp.int32, (rows, 1), 0) // BLK
    sk = jnp.zeros((rows, 1), F32)
    for h in range(NQ):
        sk = jnp.where(head == h, sink_ref[h:h + 1, 0:1], sk)
    m = jnp.maximum(jnp.max(sh, axis=-1, keepdims=True), sk)
    e = jnp.exp(sh - m)
    es = jnp.exp(sk - m)
    rz = 1.0 / (jnp.sum(e, axis=-1, keepdims=True) + es)
    return e * rz, es * rz


def _attn_fwd(proj, sinks8, gg):
    s = proj.shape[0]
    nb = ATT_NB_FWD

    def body(q_ref, kc_ref, kp_ref, vc_ref, vp_ref, sink_ref, gg_ref, bias_ref, yn_ref, ob_ref):
        kws, vws = _attn_windows(kc_ref, kp_ref, nb), _attn_windows(vc_ref, vp_ref, nb)
        for b in range(nb):
            rows = pl.ds(b * BLK, BLK)
            first = (pl.program_id(0) == 0) if b == 0 else False
            p, _ = _attn_probs(_attn_stack_all(q_ref.at[rows, :]), kws[b], first, sink_ref, bias_ref)
            _attn_unstack_all(jnp.dot(p.astype(BF16), vws[b], preferred_element_type=F32), ob_ref.at[rows, :])
        ob = ob_ref[...]
        yn_ref[...] = (ob * _rsq(ob, NORM_EPS) * gg_ref[...]).astype(BF16)

    tb = nb * BLK
    cur = lambda c: pl.BlockSpec((tb, 128), lambda m, c=c: (m, c))
    prev = lambda c: pl.BlockSpec((BLK, 128), lambda m, c=c: (jnp.maximum(nb * m - 1, 0), c))
    out = pl.BlockSpec((tb, W_B), lambda m: (m, 0))
    return pl.pallas_call(
        body, name="attn_fwd", grid=(s // tb,),
        in_specs=[pl.BlockSpec((tb, W_B), lambda m: (m, 1)), cur(8), prev(8), cur(9), prev(9),
                  pl.BlockSpec((8, 128), lambda n: (0, 0)), pl.BlockSpec((1, W_B), lambda n: (0, 0)),
                  pl.BlockSpec((2, NQ * BLK, 2 * BLK), lambda n: (0, 0, 0))],
        out_specs=[out, out], out_shape=[SDS((s, W_B), BF16), SDS((s, W_B), F32)],
        compiler_params=_cp("parallel"),
    )(proj, proj, proj, proj, proj, sinks8, gg, _attn_bias())


def _attn_bwd(dy, proj, ob, sinks8, gg):
    s = proj.shape[0]
    nb = ATT_NB_BWD

    def body(dya_ref, dyb_ref, q_ref, kc_ref, kp_ref, vc_ref, vp_ref, ob_ref, sink_ref, gg_ref, bias_ref,
             dq_ref, dcur_ref, dprev_ref, dsink_ref, dgg_ref):
        first = pl.program_id(0) == 0
        kws, vws = _attn_windows(kc_ref, kp_ref, nb), _attn_windows(vc_ref, vp_ref, nb)
        dyn = jnp.concatenate([dya_ref[...], dyb_ref[...]], axis=1)
        dob, dggr = _rms_bwd_rows(ob_ref[...], gg_ref[...], dyn)
        _acc(dgg_ref, first, jnp.sum(dggr, axis=0, keepdims=True))
        r8 = _row_iota((8, 128))
        dsk = jnp.zeros((8, 128), F32)
        for b in range(nb):
            rows = pl.ds(b * BLK, BLK)
            qs = _attn_stack_all(q_ref.at[rows, :])
            p, psink = _attn_probs(qs, kws[b], first if b == 0 else False, sink_ref, bias_ref)
            dosb = _attn_stack_all(dob[b * BLK:(b + 1) * BLK, :]).astype(BF16)
            dp = lax.dot_general(dosb, vws[b], NT, preferred_element_type=F32)
            dd = jnp.sum(p * dp, axis=-1, keepdims=True)
            dsb = (p * (dp - dd) * SCALE).astype(BF16)
            dsink_rows = -psink * dd
            for h in range(NQ):
                dsk = dsk + jnp.where(r8 == h, jnp.sum(dsink_rows[h * BLK:(h + 1) * BLK, :], axis=0, keepdims=True), 0.0)
            _attn_unstack_all(jnp.dot(dsb, kws[b], preferred_element_type=F32), dq_ref.at[rows, :])
            dkw = lax.dot_general(dsb, qs.astype(BF16), TN, preferred_element_type=F32)
            dvw = lax.dot_general(p.astype(BF16), dosb, TN, preferred_element_type=F32)
            dprev_ref[rows, 0:128] = dkw[0:BLK, :]
            dprev_ref[rows, 128:256] = dvw[0:BLK, :]
            dcur_ref[rows, 0:128] = dkw[BLK:2 * BLK, :]
            dcur_ref[rows, 128:256] = dvw[BLK:2 * BLK, :]
        _acc(dsink_ref, first, dsk)

    tb = nb * BLK
    cur = lambda c: pl.BlockSpec((tb, 128), lambda m, c=c: (m, c))
    prev = lambda c: pl.BlockSpec((BLK, 128), lambda m, c=c: (jnp.maximum(nb * m - 1, 0), c))
    wide = pl.BlockSpec((tb, W_B), lambda m: (m, 0))
    half = pl.BlockSpec((tb, 256), lambda m: (m, 0))
    return pl.pallas_call(
        body, name="attn_bwd", grid=(s // tb,),
        in_specs=[pl.BlockSpec((tb, 256), lambda m: (m, 1)), pl.BlockSpec((tb, 256), lambda m: (m, 2)),
                  pl.BlockSpec((tb, W_B), lambda m: (m, 1)), cur(8), prev(8), cur(9), prev(9), wide,
                  pl.BlockSpec((8, 128), lambda n: (0, 0)), pl.BlockSpec((1, W_B), lambda n: (0, 0)),
                  pl.BlockSpec((2, NQ * BLK, 2 * BLK), lambda n: (0, 0, 0))],
        out_specs=[wide, half, half, pl.BlockSpec((8, 128), lambda n: (0, 0)), pl.BlockSpec((1, W_B), lambda n: (0, 0))],
        out_shape=[SDS((s, W_B), F32), SDS((s, 256), F32), SDS((s, 256), F32), SDS((8, 128), F32), SDS((1, W_B), F32)],
        compiler_params=_cp("arbitrary"),
    )(dy, dy, proj, proj, proj, proj, proj, ob, sinks8, gg, _attn_bias())


def _ln_parts(y1, eps=LN_EPS):
    mu = jnp.mean(y1, axis=-1, keepdims=True)
    xc = y1 - mu
    rstd = lax.rsqrt(jnp.mean(xc * xc, axis=-1, keepdims=True) + eps)
    return xc * rstd, rstd


def _conf_fwd(proj, cw, cb, lg, lb, gg, tc):
    s = proj.shape[0]
    pad = 32

    def body(ac_ref, gc_ref, ap_ref, gp_ref, cw_ref, cb_ref, lg_ref, lb_ref, gg_ref, yn_ref, y1_ref, ys_ref, sh_ref):
        i = pl.program_id(0)
        tail = ap_ref[tc - pad:tc, :] * _sig(gp_ref[tc - pad:tc, :])
        ys_ref[0:pad, :] = jnp.where(i > 0, tail, 0.0)
        ys_ref[pad:pad + tc, :] = ac_ref[...] * _sig(gc_ref[...])
        _fill_shifted(ys_ref, sh_ref)
        y1 = cb_ref[...]
        for j in range(CONV_K):
            y1 = y1 + cw_ref[j:j + 1, :] * _shifted_rows(ys_ref, sh_ref, pad - (CONV_K - 1) + j, tc)
        y1_ref[...] = y1
        xh, _ = _ln_parts(y1)
        yl = xh * lg_ref[...] + lb_ref[...]
        yc = yl * _sig(yl)
        yn_ref[...] = (yc * _rsq(yc, NORM_EPS) * gg_ref[...]).astype(BF16)

    cur = lambda c: pl.BlockSpec((tc, W_C), lambda i, c=c: (i, c))
    prev = lambda c: pl.BlockSpec((tc, W_C), lambda i, c=c: (jnp.maximum(i - 1, 0), c))
    full = lambda a: pl.BlockSpec(a.shape, lambda i: (0,) * a.ndim)
    params = [cw, cb, lg, lb, gg]
    out = pl.BlockSpec((tc, W_C), lambda i: (i, 0))
    return pl.pallas_call(
        body, name="conf_fwd", grid=(s // tc,),
        in_specs=[cur(5), cur(6), prev(5), prev(6)] + [full(a) for a in params],
        out_specs=[out, out], out_shape=[SDS((s, W_C), BF16), SDS((s, W_C), F32)],
        scratch_shapes=[pltpu.VMEM((tc + pad, W_C), F32), pltpu.VMEM((8, tc + pad, W_C), F32)],
        compiler_params=_cp("parallel"),
    )(proj, proj, proj, proj, *params)


def _conf_bwd(dy, proj, y1, cw, cb, lg, lb, gg, tc):
    s = proj.shape[0]
    nc = s // tc
    pad = 32

    def body(dy_ref, ac_ref, gc_ref, ap_ref, gp_ref, y1_ref, cw_ref, cb_ref, lg_ref, lb_ref, gg_ref,
             dp_ref, dcw_ref, dcb_ref, dlg_ref, dlb_ref, dgg_ref, ys_ref, ds_ref, nx_ref, ysh_ref, dsh_ref):
        step = pl.program_id(0)
        i = nc - 1 - step
        first = step == 0

        @pl.when(first)
        def _():
            nx_ref[...] = jnp.zeros_like(nx_ref)

        a = ac_ref[...]
        sg = _sig(gc_ref[...])
        tail = ap_ref[tc - pad:tc, :] * _sig(gp_ref[tc - pad:tc, :])
        ys_ref[0:pad, :] = jnp.where(i > 0, tail, 0.0)
        ys_ref[pad:pad + tc, :] = a * sg
        xh, rstd = _ln_parts(y1_ref[...])
        yl = xh * lg_ref[...] + lb_ref[...]
        sl = _sig(yl)
        yc = yl * sl
        dyc, dggr = _rms_bwd_rows(yc, gg_ref[...], dy_ref[...])
        _acc(dgg_ref, first, jnp.sum(dggr, axis=0, keepdims=True))
        dyl = dyc * sl * (1.0 + yl * (1.0 - sl))
        _acc(dlg_ref, first, jnp.sum(dyl * xh, axis=0, keepdims=True))
        _acc(dlb_ref, first, jnp.sum(dyl, axis=0, keepdims=True))
        dxh = dyl * lg_ref[...]
        dy1 = rstd * (dxh - jnp.mean(dxh, axis=-1, keepdims=True) - xh * jnp.mean(dxh * xh, axis=-1, keepdims=True))
        _acc(dcb_ref, first, jnp.sum(dy1, axis=0, keepdims=True))
        r32 = _row_iota((32, W_C))
        dcw = jnp.zeros((32, W_C), F32)
        _fill_shifted(ys_ref, ysh_ref)
        for j in range(CONV_K):
            tap = jnp.sum(dy1 * _shifted_rows(ys_ref, ysh_ref, pad - (CONV_K - 1) + j, tc), axis=0, keepdims=True)
            dcw = dcw + jnp.where(r32 == j, tap, 0.0)
        _acc(dcw_ref, first, dcw)
        ds_ref[0:tc, :] = dy1
        ds_ref[tc:tc + pad, :] = nx_ref[...]
        _fill_shifted(ds_ref, dsh_ref)
        dy0 = None
        for j in range(CONV_K):
            term = cw_ref[j:j + 1, :] * _shifted_rows(ds_ref, dsh_ref, CONV_K - 1 - j, tc)
            dy0 = term if dy0 is None else dy0 + term
        dp_ref[:, 0:W_C] = dy0 * sg
        dp_ref[:, W_C:2 * W_C] = dy0 * a * sg * (1.0 - sg)
        nx_ref[...] = dy1[0:pad, :]

    rev = lambda c: pl.BlockSpec((tc, W_C), lambda t, c=c: (nc - 1 - t, c))
    prev = lambda c: pl.BlockSpec((tc, W_C), lambda t, c=c: (jnp.maximum(nc - 2 - t, 0), c))
    full = lambda a: pl.BlockSpec(a.shape, lambda t: (0,) * a.ndim)
    params = [cw, cb, lg, lb, gg]
    vec = SDS((1, W_C), F32)
    outs = [SDS((s, 2 * W_C), F32), SDS((32, W_C), F32), vec, vec, vec, vec]
    return pl.pallas_call(
        body, name="conf_bwd", grid=(nc,),
        in_specs=[rev(3), rev(5), rev(6), prev(5), prev(6), rev(0)] + [full(a) for a in params],
        out_specs=[pl.BlockSpec((tc, 2 * W_C), lambda t: (nc - 1 - t, 0))]
        + [pl.BlockSpec(o.shape, lambda t: (0, 0)) for o in outs[1:]],
        out_shape=outs,
        scratch_shapes=[pltpu.VMEM((tc + pad, W_C), F32), pltpu.VMEM((tc + pad, W_C), F32), pltpu.VMEM((pad, W_C), F32),
                        pltpu.VMEM((8, tc + pad, W_C), F32), pltpu.VMEM((8, tc + pad, W_C), F32)],
        compiler_params=_cp("arbitrary"),
    )(dy, proj, proj, proj, proj, y1, *params)


def _assemble_dproj(dlru, dq, dcur, dprev, dconf):
    s = dq.shape[0]
    nb = s // BLK

    def body(dl_ref, dq_ref, dc_ref, dn_ref, df_ref, o_ref):
        n = pl.program_id(0)
        o_ref[:, 0:512] = dl_ref[...].astype(BF16)
        o_ref[:, 512:1024] = dq_ref[...].astype(BF16)
        o_ref[:, 1024:1280] = (dc_ref[...] + jnp.where(n < nb - 1, dn_ref[...], 0.0)).astype(BF16)
        o_ref[:, 1280:1792] = df_ref[...].astype(BF16)

    wide = pl.BlockSpec((BLK, 512), lambda n: (n, 0))
    return pl.pallas_call(
        body, name="assemble_dproj", grid=(nb,),
        in_specs=[wide, wide, pl.BlockSpec((BLK, 256), lambda n: (n, 0)),
                  pl.BlockSpec((BLK, 256), lambda n: (jnp.minimum(n + 1, nb - 1), 0)), wide],
        out_specs=pl.BlockSpec((BLK, P_IN), lambda n: (n, 0)), out_shape=SDS((s, P_IN), BF16),
        compiler_params=_cp("parallel"),
    )(dlru, dq, dcur, dprev, dconf)


def _loss_grad(y, t, tm, below):
    s = y.shape[0]

    def body(y_ref, t_ref, zb_ref, gb_ref, dy_ref, l_ref, dzb_ref, dgb_ref):
        first = pl.program_id(0) == 0
        err = y_ref[...] - t_ref[...]
        dy = err * (1.0 / D)
        dy_ref[...] = dy
        _acc_rows(l_ref, first, err * err)
        _post_norm_tail(dy, below[2], zb_ref, gb_ref, dzb_ref, dgb_ref, first)

    row = pl.BlockSpec((tm, D), lambda i: (i, 0))
    vec = pl.BlockSpec((1, D), lambda i: (0, 0))
    return pl.pallas_call(
        body, name="loss_grad", grid=(s // tm,), in_specs=[row, row, row, vec],
        out_specs=[row, vec, row, vec],
        out_shape=[SDS((s, D), F32), SDS((1, D), F32), SDS((s, D), BF16), SDS((1, D), F32)], compiler_params=_cp("arbitrary"),
    )(y, t, below[0], below[1])


def _block_diag(w):
    rows = [jnp.concatenate([w[h] if k == h else jnp.zeros((64, 64), w.dtype) for k in range(4)], axis=1) for h in range(4)]
    return jnp.concatenate(rows, axis=0)


def _diag_blocks(m):
    return jnp.stack([m[64 * h:64 * (h + 1), 64 * h:64 * (h + 1)] for h in range(4)])


def _layer_params(small, l):
    v = lambda name: small[name][l].reshape(1, -1)
    gg = small["group_g"][l]
    return dict(
        ffn1_pre=v("ffn1_pre_g"), ffn1_post=v("ffn1_post_g"), mix_pre=v("mix_pre_g"), mix_post=v("mix_post_g"),
        ffn2_pre=v("ffn2_pre_g"), ffn2_post=v("ffn2_post_g"), lru_cb=v("lru_conv_b"),
        wa=_block_diag(small["lru_w_a"][l]).astype(BF16), ba=v("lru_b_a"),
        wx=_block_diag(small["lru_w_x"][l]).astype(BF16), bx=v("lru_b_x"), lam=v("lru_lambda"),
        sinks8=jnp.broadcast_to(small["attn_sinks"][l][:, None], (NQ, 128)),
        conv_b=v("conv_b"), ln_g=v("conv_ln_g"), ln_b=v("conv_ln_b"),
        gg_a=gg[0:W_A].reshape(1, -1), gg_b=gg[W_A:W_A + W_B].reshape(1, -1), gg_c=gg[W_A + W_B:].reshape(1, -1),
    )


def _forward_layer(x, weights, p, tiles, deps=()):
    _, mm, _, tc, _ = tiles
    big = dict(weights("ffn1_gu", x))
    p = dict(p)
    sv = dict(x0=x)
    h1, g1, u1, a1 = _ffn_up(x, p["ffn1_pre"], big["ffn1_w_gu"], 0, mm, deps)
    big.update(weights("ffn1_down", a1))
    z1, x = _mm_rms_res(a1, big["ffn1_w_down"], 0, x, p["ffn1_post"], 0.5, mm, DFF, "ffn_down")
    sv.update(h1=h1, g1=g1, u1=u1, a1=a1, z1=z1, x1=x)
    big.update(weights("mix", x))
    p.update(lru_cw=big.pop("lru_conv_w"), conv_w=big.pop("conv_w"))
    hn, proj = _proj(x, p["mix_pre"], big["w_in"], 0, mm)
    yn_a, hl = _lru_fwd(proj, p["lru_cw"], p["lru_cb"], p["wa"], p["ba"], p["wx"], p["bx"], p["lam"], p["gg_a"], tc)
    yn_b, ob = _attn_fwd(proj, p["sinks8"], p["gg_b"])
    yn_c, y1 = _conf_fwd(proj, p["conv_w"], p["conv_b"], p["ln_g"], p["ln_b"], p["gg_c"], tc)
    ycat = jnp.concatenate([yn_a, yn_b, yn_c], axis=1)
    zo, x = _mm_rms_res(ycat, big["w_out"], 0, x, p["mix_post"], 1.0, mm, D, "mix_out")
    sv.update(hn=hn, proj=proj, hl=hl, ob=ob, y1=y1, ycat=ycat, zo=zo, x2=x)
    big.update(weights("ffn2", x))
    h2, g2, u2, a2 = _ffn_up(x, p["ffn2_pre"], big["ffn2_w_gu"], 0, mm)
    z2, x = _mm_rms_res(a2, big["ffn2_w_down"], 0, x, p["ffn2_post"], 0.5, mm, DFF, "ffn_down")
    sv.update(h2=h2, g2=g2, u2=u2, a2=a2, z2=z2, p=p, big=big)
    return x, sv


def _grad_buffers():
    empty = lambda *shape: lax.empty(shape, F32)
    return dict(ffn1_w_gu=empty(1, NSHARD, D, FH), ffn2_w_gu=empty(1, NSHARD, D, FH), ffn1_w_down=empty(1, 1, DFF, D),
                ffn2_w_down=empty(1, 1, DFF, D), w_in=empty(1, 1, D, P_IN), w_out=empty(1, 1, D, D))


def _backward_layer(dx, dzp, sv, bufs, tiles, stage, below):
    p, big = sv["p"], sv["big"]
    tm, mm, dw, tc, dh_rows = tiles
    gr = {}

    def ffn_bwd(dx, dzp, which, xin, h, g, u, a, pre, deps, below):
        dz, gr[which + "_post_g"] = dzp
        dgu = _ffn_bwd_mid(dz, big[which + "_w_down"], 0, g, u, mm, deps)
        bufs[which + "_w_down"] = _mm_tn_into(bufs[which + "_w_down"], a, dz, 0, 0, FH, D, dw, "dw_down")
        bufs[which + "_w_gu"] = _mm_tn_into(bufs[which + "_w_gu"], h, dgu, 0, 0, D, FH, dw, "dw_gate", 2, 0)
        bufs[which + "_w_gu"] = _mm_tn_into(bufs[which + "_w_gu"], h, dgu, 0, 2, D, FH, dw, "dw_up", 2, 1)
        deps = stage({n: bufs[n] for n in (which + "_w_gu", which + "_w_down")}, bufs[which + "_w_gu"], gr)
        out = _ffn_bwd_dh(dgu, big[which + "_w_gu"], 0, xin, pre, dx, dh_rows, deps, below)
        gr[which + "_pre_g"] = out[1]
        return out[0], (tuple(out[2:]) if below is not None else None)

    dx, (do, gr["mix_post_g"]) = ffn_bwd(dx, dzp, "ffn2", sv["x2"], sv["h2"], sv["g2"], sv["u2"], sv["a2"],
                                         p["ffn2_pre"], (), (sv["zo"], p["mix_post"], 1.0))
    bufs["w_out"] = _mm_tn_into(bufs["w_out"], sv["ycat"], do, 0, 0, D, D, dw, "dw_out")
    dy = _mm_nt(do, big["w_out"], 0, mm, "mix_dy")
    proj = sv["proj"]
    (dlru, dcw, gr["lru_conv_b"], dwa, gr["lru_b_a"], dwx, gr["lru_b_x"], gr["lru_lambda"], dgg_a) = _lru_bwd(
        dy, proj, sv["hl"], p["lru_cw"], p["lru_cb"], p["wa"], p["ba"], p["wx"], p["bx"], p["lam"], p["gg_a"], tc)
    dq, dcur, dprev, dsk, dgg_b = _attn_bwd(dy, proj, sv["ob"], p["sinks8"], p["gg_b"])
    dconf, dconvw, gr["conv_b"], gr["conv_ln_g"], gr["conv_ln_b"], dgg_c = _conf_bwd(
        dy, proj, sv["y1"], p["conv_w"], p["conv_b"], p["ln_g"], p["ln_b"], p["gg_c"], tc)
    dproj = _assemble_dproj(dlru, dq, dcur, dprev, dconf)
    bufs["w_in"] = _mm_tn_into(bufs["w_in"], sv["hn"], dproj, 0, 0, D, P_IN, dw, "dw_in")
    dx, gr["mix_pre_g"], dz1, dpost1 = _mm_nt_rmsbwd(dproj, big["w_in"], 0, sv["x1"], p["mix_pre"], dx, dh_rows,
                                                     (sv["z1"], p["ffn1_post"], 0.5))
    gr["lru_conv_w"] = dcw[0:LRU_K]
    gr["lru_w_a"] = _diag_blocks(dwa)
    gr["lru_w_x"] = _diag_blocks(dwx)
    gr["attn_sinks"] = dsk[:, 0]
    gr["conv_w"] = dconvw[0:CONV_K]
    gr["group_g"] = jnp.concatenate([dgg_a, dgg_b, dgg_c], axis=1)
    dx, dz_below = ffn_bwd(dx, (dz1, dpost1), "ffn1", sv["x0"], sv["h1"], sv["g1"], sv["u1"], sv["a1"], p["ffn1_pre"],
                           stage({n: bufs[n] for n in ("w_in", "w_out")}, dx, gr), below)
    return dx, dz_below, gr


def _tiles(s):
    return min(1024, s), min(1024, s), min(2048, s), min(512, s // 2), min(512, s)


HBM_SPEC = pl.BlockSpec(memory_space=pltpu.HBM)
SEM_SPEC = pl.BlockSpec(memory_space=pltpu.SEMAPHORE)
EFFECT = pltpu.SideEffectType.DATAFLOW_SIDE_EFFECTING


def _place():
    x, y, c = lax.axis_index("x"), lax.axis_index("y"), lax.axis_index("c")
    return x, y, c, [(1 - x, y), (x, 1 - y), (1 - x, 1 - y)]


def _rcopy(src, dst, send_sems, recv_sems, k, to):
    return pltpu.make_async_remote_copy(src_ref=src, dst_ref=dst, send_sem=send_sems.at[k], recv_sem=recv_sems.at[k],
                                        device_id=to, device_id_type=MESH)


def _half(rows, which):
    return pl.ds(which * (rows // 2), rows // 2)


def _place_shard(w, l, p_idx, dtype, deps=()):
    _, rows, cols = w.shape
    tr = _rows_per_block(rows, cols, 16, SUM_BLOCK_ELEMS) if rows % 16 == 0 else rows
    deps = list(deps)

    def body(p_ref, buf_ref, w_ref, *rest):
        rest[len(deps)][...] = w_ref[...].astype(dtype)

    spec = pltpu.PrefetchScalarGridSpec(
        num_scalar_prefetch=1, grid=(rows // tr,),
        in_specs=[ANY, pl.BlockSpec((None, tr, cols), lambda i, pr: (l, i, 0))] + [ANY] * len(deps),
        out_specs=pl.BlockSpec((None, None, tr, cols), lambda i, pr: (0, pr[0], i, 0)))
    shape = (1, NSHARD, rows, cols)
    return pl.pallas_call(body, name="place_shard", grid_spec=spec, out_shape=SDS(shape, dtype),
                          input_output_aliases={1: 0}, compiler_params=_cp("parallel"),
                          )(p_idx, lax.empty(shape, dtype), w, *deps)


def _run_plans(plans, refs, send_sems, recv_sems):
    cps, b0, s0 = [], 0, 0
    for plan, nb, ns in plans:
        cps += plan(refs[b0:b0 + nb], send_sems, recv_sems, s0)
        b0, s0 = b0 + nb, s0 + ns
    return cps


def _exchange(name, bufs, plans):
    n = len(bufs)
    nsem = sum(ns for _, _, ns in plans)

    def body(*refs):
        cps = _run_plans(plans, refs[n:2 * n], refs[2 * n], refs[2 * n + 1])
        for cp in cps:
            cp.start()
        for cp in cps:
            cp.wait()

    return pl.pallas_call(
        body, name=name, in_specs=[ANY] * n, out_specs=[ANY] * n, out_shape=[SDS(b.shape, b.dtype) for b in bufs],
        input_output_aliases={a: a for a in range(n)},
        scratch_shapes=[pltpu.SemaphoreType.DMA((nsem,)), pltpu.SemaphoreType.DMA((nsem,))],
    )(*bufs)


def _exchange_start(name, bufs, plans, deps=()):
    n = len(bufs)
    nsem = sum(ns for _, _, ns in plans)
    deps = list(deps)
    first_out = n + len(deps)

    def body(*refs):
        for cp in _run_plans(plans, refs[:n], refs[first_out], refs[first_out + 1]):
            cp.start()
        token = refs[first_out + 2 + n]
        token[...] = jnp.zeros_like(token)

    outs = pl.pallas_call(
        body, name=name,
        out_shape=(pltpu.SemaphoreType.DMA((nsem,)), pltpu.SemaphoreType.DMA((nsem,)),
                   *[pltpu.HBM(b.shape, b.dtype) for b in bufs], SDS((8, 128), F32)),
        in_specs=[HBM_SPEC] * n + [ANY] * len(deps),
        out_specs=(SEM_SPEC, SEM_SPEC, *[HBM_SPEC] * n, pl.BlockSpec(memory_space=pltpu.VMEM)),
        input_output_aliases={a: 2 + a for a in range(n)},
        compiler_params=pltpu.CompilerParams(has_side_effects=EFFECT),
    )(*[pltpu.with_memory_space_constraint(b, pltpu.HBM) for b in bufs], *deps)
    return outs[0], outs[1], list(outs[2:2 + n]), outs[2 + n]


def _exchange_wait(name, send_sems, recv_sems, bufs, plans, after):
    n = len(bufs)

    def body(*refs):
        for cp in _run_plans(plans, refs[:n], refs[n], refs[n + 1]):
            cp.wait_send()
            cp.wait_recv()

    return pl.pallas_call(
        body, name=name, out_shape=[pltpu.HBM(b.shape, b.dtype) for b in bufs],
        in_specs=[HBM_SPEC] * n + [SEM_SPEC, SEM_SPEC, ANY], out_specs=[HBM_SPEC] * n,
        input_output_aliases={a: a for a in range(n)},
        compiler_params=pltpu.CompilerParams(has_side_effects=EFFECT),
    )(*bufs, send_sems, recv_sems, after)


def _plan_gather(refs, send_sems, recv_sems, base):
    x, y, c, chips = _place()
    p = 2 * x + y
    return [_rcopy(r.at[0, p], r.at[0, p], send_sems, recv_sems, base + 3 * a + j, (*chip, c))
            for a, r in enumerate(refs) for j, chip in enumerate(chips)]


def _plan_gather_half(refs, send_sems, recv_sems, base):
    x, y, c, chips = _place()
    p = 2 * x + y
    return [_rcopy(r.at[0, p, _half(r.shape[2], c)], r.at[0, p, _half(r.shape[2], c)], send_sems, recv_sems,
                   base + 3 * a + j, (*chip, c)) for a, r in enumerate(refs) for j, chip in enumerate(chips)]


def _plan_forward_half(refs, send_sems, recv_sems, base):
    x, y, c, chips = _place()
    cps = []
    for a, r in enumerate(refs):
        for j, chip in enumerate(chips):
            blk = r.at[0, 2 * chip[0] + chip[1], _half(r.shape[2], c)]
            cps.append(_rcopy(blk, blk, send_sems, recv_sems, base + 3 * a + j, (x, y, 1 - c)))
    return cps


def _plan_pair_exchange(refs, send_sems, recv_sems, base):
    x, y, c, _ = _place()
    n = len(refs) // 2
    return [_rcopy(refs[a].at[:, _half(refs[a].shape[1], 1 - c)], refs[n + a], send_sems, recv_sems, base + a,
                   (x, y, 1 - c)) for a in range(n)]


def _plan_chip_exchange(refs, send_sems, recv_sems, base):
    x, y, c, chips = _place()
    n = len(refs) // 2
    return [_rcopy(refs[a].at[2 * chip[0] + chip[1]], refs[n + a].at[j], send_sems, recv_sems, base + 3 * a + j,
                   (*chip, c)) for a in range(n) for j, chip in enumerate(chips)]


def _plan_pair_share(refs, send_sems, recv_sems, base):
    x, y, c, _ = _place()
    return [_rcopy(r.at[_half(r.shape[0], c)], r.at[_half(r.shape[0], c)], send_sems, recv_sems, base + a,
                   (x, y, 1 - c)) for a, r in enumerate(refs)]


def _plan_small_gather(refs, send_sems, recv_sems, base):
    x, y, c, _ = _place()
    me = 4 * x + 2 * y + c
    cps = []
    for m in range(1, NDEV):
        peer = (1 - x if m & 4 else x, 1 - y if m & 2 else y, 1 - c if m & 1 else c)
        cps.append(_rcopy(refs[0], refs[1].at[me], send_sems, recv_sems, base + m - 1, peer))
    return cps


def _sum_small(buf, gathered):
    def body(buf_ref, g_ref, o_ref):
        x, y, c, _ = _place()
        me = 4 * x + 2 * y + c
        total = jnp.where(me == 0, buf_ref[...], g_ref[0])
        for dev in range(1, NDEV):
            total = total + jnp.where(me == dev, buf_ref[...], g_ref[dev])
        o_ref[...] = total

    vm = pl.BlockSpec(memory_space=pltpu.VMEM)
    return pl.pallas_call(body, name="sum_small", in_specs=[vm, vm], out_specs=vm, out_shape=SDS(buf.shape, F32),
                          compiler_params=pltpu.CompilerParams(vmem_limit_bytes=VMEM_LIMIT))(buf, gathered)


BLOCK_ELEMS = 512 * 1024
SUM_BLOCK_ELEMS = 1024 * 1024


def _rows_per_block(rows, cols, mult, limit=BLOCK_ELEMS):
    best = None
    for tr in range(mult, rows + 1, mult):
        if rows % tr == 0 and tr * cols <= limit:
            best = tr
    assert best is not None, (rows, cols)
    return best


def _pair_sum(g, r, c_idx):
    nq, rows, cols = g.shape
    half = rows // 2
    tr = _rows_per_block(half, cols, 16, SUM_BLOCK_ELEMS)
    nb = half // tr

    def body(c_ref, g_ref, r_ref, t_ref):
        t_ref[...] = (g_ref[...] + r_ref[...]).astype(BF16)

    blk = pl.BlockSpec((None, tr, cols), lambda q, i, cr: (q, i, 0))
    spec = pltpu.PrefetchScalarGridSpec(
        num_scalar_prefetch=1, grid=(nq, nb),
        in_specs=[pl.BlockSpec((None, tr, cols), lambda q, i, cr: (q, cr[0] * nb + i, 0)), blk], out_specs=blk)
    return pl.pallas_call(body, name="grad_pair_sum", grid_spec=spec, out_shape=SDS((nq, half, cols), BF16),
                          compiler_params=_cp("parallel", "parallel"))(c_idx, g, r)


def _chip_sum(g, r, rr, cp_idx):
    _, rows, cols = g.shape
    half = rows // 2
    tr = _rows_per_block(half, cols, 16, SUM_BLOCK_ELEMS)
    nb = half // tr

    def body(cp_ref, buf_ref, g_ref, r_ref, rr_ref, o_ref):
        o_ref[...] = ((g_ref[...] + r_ref[...]) + rr_ref[0].astype(F32) + rr_ref[1].astype(F32) + rr_ref[2].astype(F32))

    spec = pltpu.PrefetchScalarGridSpec(
        num_scalar_prefetch=1, grid=(nb,),
        in_specs=[ANY, pl.BlockSpec((None, tr, cols), lambda i, cp: (cp[1], cp[0] * nb + i, 0)),
                  pl.BlockSpec((None, tr, cols), lambda i, cp: (cp[1], i, 0)),
                  pl.BlockSpec((3, tr, cols), lambda i, cp: (0, i, 0))],
        out_specs=pl.BlockSpec((tr, cols), lambda i, cp: (cp[0] * nb + i, 0)))
    return pl.pallas_call(body, name="grad_chip_sum", grid_spec=spec, out_shape=SDS((rows, cols), F32),
                          input_output_aliases={1: 0}, compiler_params=_cp("parallel"),
                          )(cp_idx, lax.empty((rows, cols), F32), g, r, rr)


def _adamw_math(w, g, m, v):
    mn = ADAM_B1 * m + (1.0 - ADAM_B1) * g
    vn = ADAM_B2 * v + (1.0 - ADAM_B2) * (g * g)
    m_hat = mn / (1.0 - ADAM_B1 ** ADAM_STEP)
    v_hat = vn / (1.0 - ADAM_B2 ** ADAM_STEP)
    return -ADAM_LR * (m_hat / (jnp.sqrt(v_hat) + ADAM_EPS) + ADAM_WD * w), mn, vn


def _adamw_layer(w, g, m, v, l, outs, deps=()):
    _, rows, cols = w.shape
    tr = _rows_per_block(rows, cols, 8)
    deps = list(deps)

    def body(*refs):
        w_ref, g_ref, m_ref, v_ref = refs[4:8]
        go_ref, d_ref, mo_ref, vo_ref = refs[8 + len(deps):]
        gg = g_ref[...]
        go_ref[...] = gg
        d_ref[...], mo_ref[...], vo_ref[...] = _adamw_math(w_ref[...], gg, m_ref[...], v_ref[...])

    blk = pl.BlockSpec((None, tr, cols), lambda i: (l, i, 0))
    return pl.pallas_call(
        body, name="adamw_layer", grid=(rows // tr,),
        in_specs=[ANY] * 4 + [blk, pl.BlockSpec((tr, cols), lambda i: (i, 0)), blk, blk] + [ANY] * len(deps),
        out_specs=[blk] * 4, out_shape=[SDS(w.shape, F32)] * 4, input_output_aliases={k: k for k in range(4)},
        compiler_params=_cp("parallel"))(*outs, w, g, m, v, *deps)


def _adamw_small(ws, gs, ms, vs, deps=()):
    n = len(ws)
    deps = list(deps)

    def body(*refs):
        refs = refs[:4 * n] + refs[4 * n + len(deps):]
        w, g, m, v, d_out, m_out, v_out = (refs[k * n:(k + 1) * n] for k in range(7))
        for k in range(n):
            d_out[k][...], m_out[k][...], v_out[k][...] = _adamw_math(w[k][...], g[k][...], m[k][...], v[k][...])

    vm = pl.BlockSpec(memory_space=pltpu.VMEM)
    outs = pl.pallas_call(body, name="adamw_small", in_specs=[vm] * (4 * n) + [ANY] * len(deps), out_specs=[vm] * (3 * n),
                          out_shape=[SDS(w.shape, F32) for w in ws] * 3,
                          compiler_params=pltpu.CompilerParams(vmem_limit_bytes=VMEM_LIMIT))(*ws, *gs, *ms, *vs, *deps)
    return outs[:n], outs[n:2 * n], outs[2 * n:]


_WEIGHTS = ["ffn1_pre_g", "ffn1_w_gu", "ffn1_w_down", "ffn1_post_g", "mix_pre_g", "w_in", "lru_conv_w", "lru_conv_b",
            "lru_w_a", "lru_b_a", "lru_w_x", "lru_b_x", "lru_lambda", "attn_sinks", "conv_w", "conv_b", "conv_ln_g",
            "conv_ln_b", "group_g", "w_out", "mix_post_g", "ffn2_pre_g", "ffn2_w_gu", "ffn2_w_down", "ffn2_post_g"]
_INPUTS = ["x"] + _WEIGHTS + ["loss_target"] + ["m_" + n for n in _WEIGHTS] + ["v_" + n for n in _WEIGHTS]
_BIG = ["ffn1_w_gu", "ffn1_w_down", "w_in", "w_out", "ffn2_w_gu", "ffn2_w_down"]
_SMALL_SHARDED = ["lru_conv_w", "conv_w"]
_SMALL_REPL = [n for n in _WEIGHTS if n not in _BIG and n not in _SMALL_SHARDED]

PACK_TILE = 8 * 128


def _pack(arrs):
    parts = []
    for a in arrs:
        flat = a.reshape(-1)
        parts.append(jnp.pad(flat, (0, -flat.shape[0] % PACK_TILE)).reshape(-1, 128))
    return jnp.concatenate(parts, axis=0)


def _unpack(buf, shapes):
    out, row = [], 0
    for shp in shapes:
        size = math.prod(shp)
        nrow = -(-size // PACK_TILE) * 8
        out.append(buf[row:row + nrow].reshape(-1)[:size].reshape(shp))
        row += nrow
    return out


def _unshard_cols(a):
    return a.transpose(0, 2, 1, 3).reshape(1, a.shape[2], NSHARD * a.shape[3])


_GROUPS = dict(ffn1_gu=["ffn1_w_gu"], ffn1_down=["ffn1_w_down"], mix=["w_in", "w_out", "lru_conv_w", "conv_w"],
               ffn2=["ffn2_w_gu", "ffn2_w_down"])


def _full_weights(group, gathered):
    g = dict(zip(_GROUPS[group], gathered))
    if group == "mix":
        return dict(w_in=_unshard_cols(g["w_in"]), w_out=g["w_out"].reshape(1, D, D),
                    lru_conv_w=_unshard_cols(g["lru_conv_w"])[0], conv_w=_unshard_cols(g["conv_w"])[0])
    return {n: (a.reshape(1, DFF, D) if n.endswith("w_down") else a) for n, a in g.items()}


def _by_shard(name, buf):
    if name.endswith("w_gu"):
        return buf[0]
    if name == "w_in":
        return buf.reshape(D, NSHARD, P_IN // NSHARD).transpose(1, 0, 2)
    return buf.reshape(NSHARD, buf.shape[2] // NSHARD, buf.shape[3])


class _Reducer:
    PLANS = (_plan_pair_exchange, _plan_chip_exchange, _plan_pair_share)

    def __init__(self, keys, gs, c_idx, cp_idx):
        self.keys, self.gs, self.c_idx, self.cp_idx = keys, gs, c_idx, cp_idx
        self.n = len(gs)
        self.step = 0
        self.result = None

    def inputs(self):
        n = self.n
        if self.step == 0:
            bufs = self.gs + [lax.empty((NSHARD, g.shape[1] // 2, g.shape[2]), F32) for g in self.gs]
        elif self.step == 1:
            ts = [_pair_sum(g, r, self.c_idx) for g, r in zip(self.gs, self.rs)]
            bufs = ts + [lax.empty((3,) + t.shape[1:], BF16) for t in ts]
        else:
            bufs = [_chip_sum(g, r, rr, self.cp_idx) for g, r, rr in zip(self.gs, self.rs, self.rrs)]
        return bufs, (self.PLANS[self.step], len(bufs), (n, 3 * n, n)[self.step])

    def absorb(self, done):
        n = self.n
        if self.step == 0:
            self.gs, self.rs = done[:n], done[n:]
        elif self.step == 1:
            self.rrs = done[n:]
        else:
            self.result = dict(zip(self.keys, done))
        self.step += 1


class _SmallGather:
    def __init__(self, buf):
        self.buf, self.step, self.result, self.gathered = buf, 0, {}, None

    def inputs(self):
        return [self.buf, jnp.zeros((NDEV,) + self.buf.shape, F32)], (_plan_small_gather, 2, NDEV - 1)

    def absorb(self, done):
        self.buf, self.gathered = done
        self.step = 3


class _ReducePipeline:
    def __init__(self, c_idx, cp_idx):
        self.c_idx, self.cp_idx = c_idx, cp_idx
        self.reducers, self.flying, self.calls = [], None, 0

    def add(self, layer, done):
        if done:
            keys = [(layer, n) for n in done]
            self.reducers.append(_Reducer(keys, [_by_shard(n, b) for n, b in done.items()], self.c_idx, self.cp_idx))

    def _next(self):
        active = [r for r in self.reducers if r.step < 3]
        bufs, plans = [], []
        for r in active:
            b, triple = r.inputs()
            bufs += b
            plans.append(triple)
        self.calls += 1
        return active, bufs, plans, "grad_exchange%d" % self.calls

    def _absorb(self, active, plans, done):
        at = 0
        for r, (_, nb, _) in zip(active, plans):
            r.absorb(done[at:at + nb])
            at += nb

    def _land(self, after):
        if self.flying is not None:
            active, plans, name, send_sems, recv_sems, bufs = self.flying
            self._absorb(active, plans, _exchange_wait(name + "_wait", send_sems, recv_sems, bufs, plans, after))
            self.flying = None

    def hook(self, after):
        self._land(after)
        active, bufs, plans, name = self._next()
        if not active:
            return []
        send_sems, recv_sems, bufs, token = _exchange_start(name + "_start", bufs, plans)
        self.flying = (active, plans, name, send_sems, recv_sems, bufs)
        return [token]

    def available(self):
        out = {}
        for r in self.reducers:
            if r.step == 3:
                out.update(r.result)
        return out

    def finish(self, after):
        self._land(after)
        while True:
            active, bufs, plans, name = self._next()
            if not active:
                break
            self._absorb(active, plans, _exchange(name, bufs, plans))
        out = {}
        for r in self.reducers:
            out.update(r.result)
        return out


def kernel(*args):
    d = dict(zip(_INPUTS, args, strict=True))
    xi, yi, ci = lax.axis_index("x"), lax.axis_index("y"), lax.axis_index("c")
    p = 2 * xi + yi
    c_idx = jnp.reshape(ci, (1,)).astype(jnp.int32)
    p_idx = jnp.reshape(p, (1,)).astype(jnp.int32)
    cp_idx = jnp.stack([ci, p]).astype(jnp.int32)
    x, target = d["x"][0], d["loss_target"][0]
    tiles = _tiles(x.shape[0])

    groups = [(l, grp) for l in range(DEPTH) for grp in _GROUPS]
    place = lambda l, grp, deps: [_place_shard(d[n], l, p_idx, BF16 if n in _BIG else F32, deps) for n in _GROUPS[grp]]
    first = place(*groups[0], ())
    half_plans = [(_plan_gather_half, len(first), 3 * len(first))]
    first_sems = _exchange_start("gather_first_start", first, half_plans)
    tokens = [first_sems[3]]
    flying = {}
    swapped = {(0, "ffn2")}
    for l, grp in groups[1:]:
        placed = place(l, grp, tokens[:1])
        plans = [(_plan_gather_half if (l, grp) in swapped else _plan_gather, len(placed), 3 * len(placed))]
        send_sems, recv_sems, bufs, token = _exchange_start("gather_l%d_%s_start" % (l, grp), placed, plans, tokens[-1:])
        flying[l, grp] = (send_sems, recv_sems, bufs, plans)
        tokens.append(token)
    first = _exchange_wait("gather_first_wait", first_sems[0], first_sems[1], first_sems[2], half_plans, tokens[-1])
    ready = {groups[0]: _exchange("gather_first_forward", first, [(_plan_forward_half, len(first), 3 * len(first))])}

    def weights_of(l):
        def weights(grp, after):
            if (l, grp) not in ready:
                send_sems, recv_sems, bufs, plans = flying[l, grp]
                got = _exchange_wait("gather_l%d_%s_wait" % (l, grp), send_sems, recv_sems, bufs, plans, after)
                if (l, grp) in swapped:
                    got = _exchange("gather_l%d_%s_forward" % (l, grp), got, [(_plan_forward_half, len(got), 3 * len(got))])
                ready[l, grp] = got
            return _full_weights(grp, ready[l, grp])
        return weights

    small = {n: d[n] for n in _SMALL_REPL}
    x1, sv0 = _forward_layer(x, weights_of(0), _layer_params(small, 0), tiles)
    x2, sv1 = _forward_layer(x1, weights_of(1), _layer_params(small, 1), tiles)
    dx, lcols, *dzp = _loss_grad(x2, target, tiles[0], (sv1["z2"], sv1["p"]["ffn2_post"], 0.5))

    pipe = _ReducePipeline(c_idx, cp_idx)
    sgrads = [None] * DEPTH
    order = _SMALL_REPL + _SMALL_SHARDED
    early_names = [n for n in order if n != "ffn1_pre_g"]
    natural = lambda n, g: g.reshape(d[n].shape[1:]) if n in _SMALL_REPL else g
    loss_part = jnp.pad((0.5 / D) * jnp.sum(lcols).reshape(1), (0, 127))
    early = {}
    for l, sv, below in ((1, sv1, (sv0["z2"], sv0["p"]["ffn2_post"], 0.5)), (0, sv0, None)):
        bufs = _grad_buffers()

        def stage(done, dx, gr, l=l):
            pipe.add(l, done)
            if l == 0 and "ffn1_w_gu" in done:
                stacked = [jnp.stack([natural(n, gr[n]), natural(n, sgrads[1][n])]) for n in early_names]
                early["shapes"] = [(128,)] + [a.shape for a in stacked] + [(D,)]
                early["gather"] = _SmallGather(_pack([loss_part] + stacked + [sgrads[1]["ffn1_pre_g"].reshape(-1)]))
                pipe.reducers.append(early["gather"])
            return pipe.hook(dx)

        dx, dzp, sgrads[l] = _backward_layer(dx, tuple(dzp), sv, bufs, tiles, stage, below)
    grad_x = dx
    late_gather = _SmallGather(_pack([sgrads[0]["ffn1_pre_g"].reshape(-1)]))
    pipe.reducers.append(late_gather)

    results = {n: tuple(lax.empty(d[n].shape, F32) for _ in range(4)) for n in _BIG}
    applied = set()

    def apply_ready(deps, last):
        for (l, n), g in pipe.available().items():
            if (l, n) not in applied:
                results[n] = _adamw_layer(d[n], g, d["m_" + n], d["v_" + n], l, results[n], deps)
                applied.add((l, n))
                last = results[n][1]
                deps = [last]
        return last

    last = apply_ready(pipe.hook(grad_x), grad_x)
    token = pipe.hook(last)
    summed = _unpack(_sum_small(early["gather"].buf, early["gather"].gathered), early["shapes"])
    late = _unpack(_sum_small(late_gather.buf, late_gather.gathered), [(D,)])[0]
    loss = summed[0][0]
    grads = {"ffn1_pre_g": jnp.stack([late, summed[-1]])}
    for n, g in zip(early_names, summed[1:-1]):
        if n in _SMALL_SHARDED:
            g = lax.dynamic_slice_in_dim(g, p * (g.shape[2] // NSHARD), g.shape[2] // NSHARD, axis=2)
        grads[n] = g
    delta, new_m, new_v = {}, {}, {}
    small_out = _adamw_small([d[n] for n in order], [grads[n] for n in order], [d["m_" + n] for n in order],
                             [d["v_" + n] for n in order], token)
    for out, res in zip((delta, new_m, new_v), small_out):
        out.update(zip(order, res))
    last = apply_ready([small_out[0][0]], small_out[0][0])
    pipe.finish(last)
    apply_ready((), last)
    for n in _BIG:
        grads[n], delta[n], new_m[n], new_v[n] = results[n]

    return (loss, grad_x[None], *[grads[n] for n in _WEIGHTS], *[delta[n] for n in _WEIGHTS],
            *[new_m[n] for n in _WEIGHTS], *[new_v[n] for n in _WEIGHTS])
```
